```python
import jax, jax.numpy as jnp
from jax import lax
import numpy as np

D_MODEL = 1024
BATCH = 16
SEQ = 2048
DEPTH = 2

N_META = 16
BLOCK = 128
N_PAD = BLOCK - N_META
NORM_EPS = 1e-6
NEG = -1e30

SB_HEADS = 8
SB_DIM = 64
SB_WIDTH = SB_HEADS * SB_DIM

MLA_HEADS = 8
MLA_Q_LORA = 256
MLA_KV_LORA = 128
MLA_NOPE = 64
MLA_ROPE = 32
MLA_V = 64
MLA_WIDTH = MLA_HEADS * MLA_V
ROPE_BASE = 10000.0

SWA_HEADS = 16
SWA_KV_HEADS = 2
SWA_DIM = 64
SWA_WINDOW = 128
SWA_WIDTH = SWA_HEADS * SWA_DIM

EVEN_SPLITS = [SB_WIDTH, SB_WIDTH, SB_WIDTH, SB_WIDTH,
               MLA_Q_LORA, MLA_KV_LORA, MLA_ROPE, MLA_WIDTH]
EVEN_IN = sum(EVEN_SPLITS)
EVEN_OUT = SB_WIDTH + MLA_WIDTH
ODD_SPLITS = [SWA_WIDTH, SWA_KV_HEADS * SWA_DIM, SWA_KV_HEADS * SWA_DIM, SWA_WIDTH]
ODD_IN = sum(ODD_SPLITS)
ODD_OUT = SWA_WIDTH

kernel_name = "hybrid_stickbreak_mla_swa_meta"


def _offsets(sizes):
    return [int(o) for o in np.cumsum(sizes)[:-1]]


def rmsnorm(x, g):
    xf = x.astype(jnp.float32)
    y = xf * lax.rsqrt(jnp.mean(xf * xf, axis=-1, keepdims=True) + NORM_EPS)
    return (y * g.astype(jnp.float32)).astype(x.dtype)


def apply_rope(x, pos):
    half = x.shape[-1] // 2
    inv = ROPE_BASE ** (-jnp.arange(half, dtype=jnp.float32) / half)
    ang = pos.astype(jnp.float32)[:, None] * inv[None, :]
    cos = jnp.cos(ang)[:, None, :]
    sin = jnp.sin(ang)[:, None, :]
    x1 = x[..., :half].astype(jnp.float32)
    x2 = x[..., half:].astype(jnp.float32)
    return jnp.concatenate([x1 * cos - x2 * sin, x1 * sin + x2 * cos], axis=-1).astype(x.dtype)


def alibi_slopes(n_heads):
    return 2.0 ** (-8.0 * (jnp.arange(n_heads, dtype=jnp.float32) + 1.0) / n_heads)


def stick_breaking_attention(q, k, v):
    Lp = q.shape[1]
    pos = jnp.arange(Lp)
    scale = SB_DIM ** -0.5
    outs = []
    for i in range(Lp // BLOCK):
        q0, q1 = i * BLOCK, (i + 1) * BLOCK
        z = jnp.einsum('bthd,bshd->bhts', q[:, q0:q1], k[:, :q1]).astype(jnp.float32) * scale
        t_pos = pos[q0:q1][:, None]
        s_pos = pos[:q1][None, :]
        mask = (s_pos < t_pos) & (s_pos >= N_PAD)
        log_beta = jax.nn.log_sigmoid(z)
        log_1m = jnp.where(mask, log_beta - z, 0.0)
        suffix = lax.cumsum(log_1m, axis=3, reverse=True) - log_1m
        a = jnp.where(mask, jnp.exp(log_beta + suffix), 0.0)
        outs.append(jnp.einsum('bhts,bshd->bthd', a.astype(v.dtype), v[:, :q1]))
    return jnp.concatenate(outs, axis=1)


def causal_block_softmax_attention(q, k, v, scale):
    Lp = q.shape[1]
    pos = jnp.arange(Lp)
    outs = []
    for i in range(Lp // BLOCK):
        q0, q1 = i * BLOCK, (i + 1) * BLOCK
        s = jnp.einsum('bthd,bshd->bhts', q[:, q0:q1], k[:, :q1]).astype(jnp.float32) * scale
        mask = (pos[None, :q1] <= pos[q0:q1, None]) & (pos[None, :q1] >= N_PAD)
        p = jax.nn.softmax(jnp.where(mask, s, NEG), axis=-1)
        outs.append(jnp.einsum('bhts,bshd->bthd', p.astype(v.dtype), v[:, :q1]))
    return jnp.concatenate(outs, axis=1)


def sliding_window_sink_attention(q, k, v, sinks):
    B, Lp = q.shape[0], q.shape[1]
    nb = Lp // BLOCK
    G = SWA_HEADS // SWA_KV_HEADS
    K = SWA_KV_HEADS
    qb = q.reshape(B, nb, BLOCK, K, G, SWA_DIM)
    kb = k.reshape(B, nb, BLOCK, K, SWA_DIM)
    vb = v.reshape(B, nb, BLOCK, K, SWA_DIM)
    shift = ((0, 0), (1, 0), (0, 0), (0, 0), (0, 0))
    k_band = jnp.concatenate([jnp.pad(kb[:, :-1], shift), kb], axis=2)
    v_band = jnp.concatenate([jnp.pad(vb[:, :-1], shift), vb], axis=2)
    k_meta = k[:, N_PAD:BLOCK]
    v_meta = v[:, N_PAD:BLOCK]
    blk = jnp.arange(nb)[:, None] * BLOCK
    t_pos = blk + jnp.arange(BLOCK)[None, :]
    s_pos = blk - BLOCK + jnp.arange(2 * BLOCK)[None, :]
    m_pos = N_PAD + jnp.arange(N_META)
    d_band = t_pos[:, :, None] - s_pos[:, None, :]
    d_meta = t_pos[:, :, None] - m_pos[None, None, :]
    band_ok = (d_band >= 0) & (d_band < SWA_WINDOW) & (s_pos[:, None, :] >= BLOCK)
    meta_ok = d_meta >= 0
    slopes = alibi_slopes(SWA_HEADS).reshape(K, G)[:, :, None, None]
    scale = SWA_DIM ** -0.5
    s_band = (jnp.einsum('bnqkgd,bnskd->bnkgqs', qb, k_band).astype(jnp.float32) * scale
              - slopes * d_band.astype(jnp.float32)[:, None, None])
    s_band = jnp.where(band_ok[:, None, None], s_band, NEG)
    s_meta = (jnp.einsum('bnqkgd,bmkd->bnkgqm', qb, k_meta).astype(jnp.float32) * scale
              - slopes * d_meta.astype(jnp.float32)[:, None, None])
    s_meta = jnp.where(meta_ok[:, None, None], s_meta, NEG)
    sink = jnp.broadcast_to(sinks.astype(jnp.float32).reshape(K, G, 1, 1),
                            s_band.shape[:-1] + (1,))
    p = jax.nn.softmax(jnp.concatenate([s_band, s_meta, sink], axis=-1), axis=-1)
    S = 2 * BLOCK
    p_band = p[..., :S].astype(v.dtype)
    p_meta = p[..., S:S + N_META].astype(v.dtype)
    o = (jnp.einsum('bnkgqs,bnskd->bnqkgd', p_band, v_band)
         + jnp.einsum('bnkgqm,bmkd->bnqkgd', p_meta, v_meta))
    return o.reshape(B, Lp, SWA_WIDTH)


def even_layer(h, pos, w_in, q_norm_g, kv_norm_g, w_uq, w_ukv, w_out):
    B, Lp = h.shape[0], h.shape[1]
    proj = h @ w_in
    q_sb, k_sb, v_sb, g_sb, c_q, c_kv, k_r, g_mla = jnp.split(proj, _offsets(EVEN_SPLITS), axis=-1)
    shp = (B, Lp, SB_HEADS, SB_DIM)
    o_sb = stick_breaking_attention(q_sb.reshape(shp), k_sb.reshape(shp), v_sb.reshape(shp))
    o_sb = o_sb.reshape(B, Lp, SB_WIDTH) * jax.nn.silu(g_sb)
    qh = (rmsnorm(c_q, q_norm_g) @ w_uq).reshape(B, Lp, MLA_HEADS, MLA_NOPE + MLA_ROPE)
    q_nope, q_rope = qh[..., :MLA_NOPE], qh[..., MLA_NOPE:]
    kvh = (rmsnorm(c_kv, kv_norm_g) @ w_ukv).reshape(B, Lp, MLA_HEADS, MLA_NOPE + MLA_V)
    k_nope, v_mla = kvh[..., :MLA_NOPE], kvh[..., MLA_NOPE:]
    k_rope = apply_rope(k_r[:, :, None, :], pos)
    q_full = jnp.concatenate([q_nope, apply_rope(q_rope, pos)], axis=-1)
    k_full = jnp.concatenate(
        [k_nope, jnp.broadcast_to(k_rope, (B, Lp, MLA_HEADS, MLA_ROPE))], axis=-1)
    o_mla = causal_block_softmax_attention(q_full, k_full, v_mla, (MLA_NOPE + MLA_ROPE) ** -0.5)
    o_mla = o_mla.reshape(B, Lp, MLA_WIDTH) * jax.nn.silu(g_mla)
    return jnp.concatenate([o_sb, o_mla], axis=-1) @ w_out


def odd_layer(h, w_in, sinks, w_out):
    B, Lp = h.shape[0], h.shape[1]
    proj = h @ w_in
    q, k, v, g = jnp.split(proj, _offsets(ODD_SPLITS), axis=-1)
    o = sliding_window_sink_attention(
        q.reshape(B, Lp, SWA_HEADS, SWA_DIM),
        k.reshape(B, Lp, SWA_KV_HEADS, SWA_DIM),
        v.reshape(B, Lp, SWA_KV_HEADS, SWA_DIM), sinks)
    return (o * jax.nn.silu(g)) @ w_out


def _fwd_setup_inputs(seed: int = 0) -> dict:
    key = jax.random.key(seed)
    ks = jax.random.split(key, 13)
    ne = (DEPTH + 1) // 2
    no = DEPTH // 2
    f32 = jnp.float32

    def w(k, shape, fan_in):
        return jax.random.normal(k, shape, f32) * (fan_in ** -0.5)

    def gain(k, shape):
        return 1.0 + 0.05 * jax.random.normal(k, shape, f32)

    return {
        "x": jax.random.normal(ks[0], (BATCH, SEQ, D_MODEL), f32),
        "meta": jax.random.normal(ks[1], (N_META, D_MODEL), f32),
        "norm_g": gain(ks[2], (DEPTH, D_MODEL)),
        "final_g": gain(ks[3], (D_MODEL,)),
        "ev_w_in": w(ks[4], (ne, D_MODEL, EVEN_IN), D_MODEL),
        "ev_q_norm_g": gain(ks[5], (ne, MLA_Q_LORA)),
        "ev_kv_norm_g": gain(ks[6], (ne, MLA_KV_LORA)),
        "ev_w_uq": w(ks[7], (ne, MLA_Q_LORA, MLA_HEADS * (MLA_NOPE + MLA_ROPE)), MLA_Q_LORA),
        "ev_w_ukv": w(ks[8], (ne, MLA_KV_LORA, MLA_HEADS * (MLA_NOPE + MLA_V)), MLA_KV_LORA),
        "ev_w_out": w(ks[9], (ne, EVEN_OUT, D_MODEL), EVEN_OUT),
        "od_w_in": w(ks[10], (no, D_MODEL, ODD_IN), D_MODEL),
        "od_sinks": 0.5 * jax.random.normal(ks[11], (no, SWA_HEADS), f32),
        "od_w_out": w(ks[12], (no, ODD_OUT, D_MODEL), ODD_OUT),
    }


def _fwd_reference(x, meta, norm_g, final_g, ev_w_in, ev_q_norm_g, ev_kv_norm_g, ev_w_uq,
              ev_w_ukv, ev_w_out, od_w_in, od_sinks, od_w_out):
    B = x.shape[0]
    meta_b = jnp.broadcast_to(meta.astype(x.dtype)[None], (B, N_META, D_MODEL))
    pad = jnp.zeros((B, N_PAD, D_MODEL), x.dtype)
    h = jnp.concatenate([pad, meta_b, x], axis=1)
    pos = jnp.arange(h.shape[1]) - N_PAD
    for layer in range(DEPTH):
        hn = rmsnorm(h, norm_g[layer])
        if layer % 2 == 0:
            i = layer // 2
            h = h + even_layer(hn, pos, ev_w_in[i], ev_q_norm_g[i], ev_kv_norm_g[i],
                               ev_w_uq[i], ev_w_ukv[i], ev_w_out[i])
        else:
            i = layer // 2
            h = h + odd_layer(hn, od_w_in[i], od_sinks[i], od_w_out[i])
    return rmsnorm(h, final_g)[:, BLOCK:]


import jax as _jax
import jax.numpy as _jnp

TWIN_FORMAT = 'train_step'
FWD_PARAMS = ['x', 'meta', 'norm_g', 'final_g', 'ev_w_in', 'ev_q_norm_g', 'ev_kv_norm_g', 'ev_w_uq', 'ev_w_ukv', 'ev_w_out', 'od_w_in', 'od_sinks', 'od_w_out']
TWIN_WEIGHTS = ['meta', 'norm_g', 'final_g', 'ev_w_in', 'ev_q_norm_g', 'ev_kv_norm_g', 'ev_w_uq', 'ev_w_ukv', 'ev_w_out', 'od_w_in', 'od_sinks', 'od_w_out']
TWIN_DIFF_INPUT = 'x'
TWIN_INPUTS = ['x', 'meta', 'norm_g', 'final_g', 'ev_w_in', 'ev_q_norm_g', 'ev_kv_norm_g', 'ev_w_uq', 'ev_w_ukv', 'ev_w_out', 'od_w_in', 'od_sinks', 'od_w_out', 'loss_target', 'm_meta', 'm_norm_g', 'm_final_g', 'm_ev_w_in', 'm_ev_q_norm_g', 'm_ev_kv_norm_g', 'm_ev_w_uq', 'm_ev_w_ukv', 'm_ev_w_out', 'm_od_w_in', 'm_od_sinks', 'm_od_w_out', 'v_meta', 'v_norm_g', 'v_final_g', 'v_ev_w_in', 'v_ev_q_norm_g', 'v_ev_kv_norm_g', 'v_ev_w_uq', 'v_ev_w_ukv', 'v_ev_w_out', 'v_od_w_in', 'v_od_sinks', 'v_od_w_out']
TWIN_OUTPUTS = ['loss', 'grad_x', 'grad_meta', 'grad_norm_g', 'grad_final_g', 'grad_ev_w_in', 'grad_ev_q_norm_g', 'grad_ev_kv_norm_g', 'grad_ev_w_uq', 'grad_ev_w_ukv', 'grad_ev_w_out', 'grad_od_w_in', 'grad_od_sinks', 'grad_od_w_out', 'delta_meta', 'delta_norm_g', 'delta_final_g', 'delta_ev_w_in', 'delta_ev_q_norm_g', 'delta_ev_kv_norm_g', 'delta_ev_w_uq', 'delta_ev_w_ukv', 'delta_ev_w_out', 'delta_od_w_in', 'delta_od_sinks', 'delta_od_w_out', 'new_m_meta', 'new_m_norm_g', 'new_m_final_g', 'new_m_ev_w_in', 'new_m_ev_q_norm_g', 'new_m_ev_kv_norm_g', 'new_m_ev_w_uq', 'new_m_ev_w_ukv', 'new_m_ev_w_out', 'new_m_od_w_in', 'new_m_od_sinks', 'new_m_od_w_out', 'new_v_meta', 'new_v_norm_g', 'new_v_final_g', 'new_v_ev_w_in', 'new_v_ev_q_norm_g', 'new_v_ev_kv_norm_g', 'new_v_ev_w_uq', 'new_v_ev_w_ukv', 'new_v_ev_w_out', 'new_v_od_w_in', 'new_v_od_sinks', 'new_v_od_w_out']
TWIN_LEAF_KINDS = {'loss': 'loss', 'grad_x': 'grad_x', 'grad_meta': 'grad_w', 'grad_norm_g': 'grad_w', 'grad_final_g': 'grad_w', 'grad_ev_w_in': 'grad_w', 'grad_ev_q_norm_g': 'grad_w', 'grad_ev_kv_norm_g': 'grad_w', 'grad_ev_w_uq': 'grad_w', 'grad_ev_w_ukv': 'grad_w', 'grad_ev_w_out': 'grad_w', 'grad_od_w_in': 'grad_w', 'grad_od_sinks': 'grad_w', 'grad_od_w_out': 'grad_w', 'delta_meta': 'delta_w', 'delta_norm_g': 'delta_w', 'delta_final_g': 'delta_w', 'delta_ev_w_in': 'delta_w', 'delta_ev_q_norm_g': 'delta_w', 'delta_ev_kv_norm_g': 'delta_w', 'delta_ev_w_uq': 'delta_w', 'delta_ev_w_ukv': 'delta_w', 'delta_ev_w_out': 'delta_w', 'delta_od_w_in': 'delta_w', 'delta_od_sinks': 'delta_w', 'delta_od_w_out': 'delta_w', 'new_m_meta': 'new_m', 'new_m_norm_g': 'new_m', 'new_m_final_g': 'new_m', 'new_m_ev_w_in': 'new_m', 'new_m_ev_q_norm_g': 'new_m', 'new_m_ev_kv_norm_g': 'new_m', 'new_m_ev_w_uq': 'new_m', 'new_m_ev_w_ukv': 'new_m', 'new_m_ev_w_out': 'new_m', 'new_m_od_w_in': 'new_m', 'new_m_od_sinks': 'new_m', 'new_m_od_w_out': 'new_m', 'new_v_meta': 'new_v', 'new_v_norm_g': 'new_v', 'new_v_final_g': 'new_v', 'new_v_ev_w_in': 'new_v', 'new_v_ev_q_norm_g': 'new_v', 'new_v_ev_kv_norm_g': 'new_v', 'new_v_ev_w_uq': 'new_v', 'new_v_ev_w_ukv': 'new_v', 'new_v_ev_w_out': 'new_v', 'new_v_od_w_in': 'new_v', 'new_v_od_sinks': 'new_v', 'new_v_od_w_out': 'new_v'}


def _forward(args):
    return _fwd_reference(*[args[k] for k in FWD_PARAMS])


def _output_shape():
    out = _jax.eval_shape(lambda: _forward(_fwd_setup_inputs(0)))
    return out.shape, out.dtype

N_MICROBATCH = 1
ADAM_LR = 0.001
ADAM_B1 = 0.9
ADAM_B2 = 0.999
ADAM_EPS = 1e-08
ADAM_WD = 0.01
ADAM_STEP = 10
PER_EXAMPLE_BATCH_AXIS = {'x': 0, 'loss_target': 0}
SHARED_INPUTS = []
_WEIGHT_DTYPES = {'meta': _jnp.float32, 'norm_g': _jnp.float32, 'final_g': _jnp.float32, 'ev_w_in': _jnp.float32, 'ev_q_norm_g': _jnp.float32, 'ev_kv_norm_g': _jnp.float32, 'ev_w_uq': _jnp.float32, 'ev_w_ukv': _jnp.float32, 'ev_w_out': _jnp.float32, 'od_w_in': _jnp.float32, 'od_sinks': _jnp.float32, 'od_w_out': _jnp.float32}
MOMENT_SCALE = {'meta': 3.310630e-03, 'norm_g': 7.024422e-02, 'final_g': 3.199082e+01, 'ev_w_in': 4.719000e-02, 'ev_q_norm_g': 2.707387e-02, 'ev_kv_norm_g': 5.106251e-02, 'ev_w_uq': 1.487058e-02, 'ev_w_ukv': 1.770329e-02, 'ev_w_out': 5.011543e-02, 'od_w_in': 3.598246e-02, 'od_sinks': 7.605811e-02, 'od_w_out': 2.956514e-02}


def _to_microbatches(a, axis):
    t = _jnp.moveaxis(a, axis, 0)
    t = t.reshape((N_MICROBATCH, t.shape[0] // N_MICROBATCH) + t.shape[1:])
    return _jnp.moveaxis(t, 1, axis + 1)


def setup_inputs(seed: int = 0) -> dict:
    inp = _fwd_setup_inputs(seed)
    key = _jax.random.fold_in(_jax.random.key(seed), 7919)
    shape, _ = _output_shape()
    out = dict(inp)
    out["loss_target"] = _jax.random.normal(_jax.random.fold_in(key, 0), shape, _jnp.float32)
    for i, name in enumerate(TWIN_WEIGHTS):
        w = inp[name].astype(_jnp.float32)
        if MOMENT_SCALE is None:
            s = _jnp.sqrt(_jnp.mean(_jnp.square(w)) + 1e-30)
        else:
            s = MOMENT_SCALE[name]
        km, kv = _jax.random.split(_jax.random.fold_in(key, i + 1))
        out[name] = w
        out["m_" + name] = s * _jax.random.normal(km, w.shape, _jnp.float32)
        out["v_" + name] = (s * s) * _jax.random.uniform(kv, w.shape, _jnp.float32, 0.5, 1.5)
    if N_MICROBATCH > 1:
        for name, axis in PER_EXAMPLE_BATCH_AXIS.items():
            out[name] = _to_microbatches(out[name], axis)
    return {'x': out['x'], 'meta': out['meta'], 'norm_g': out['norm_g'], 'final_g': out['final_g'], 'ev_w_in': out['ev_w_in'], 'ev_q_norm_g': out['ev_q_norm_g'], 'ev_kv_norm_g': out['ev_kv_norm_g'], 'ev_w_uq': out['ev_w_uq'], 'ev_w_ukv': out['ev_w_ukv'], 'ev_w_out': out['ev_w_out'], 'od_w_in': out['od_w_in'], 'od_sinks': out['od_sinks'], 'od_w_out': out['od_w_out'], 'loss_target': out['loss_target'], 'm_meta': out['m_meta'], 'm_norm_g': out['m_norm_g'], 'm_final_g': out['m_final_g'], 'm_ev_w_in': out['m_ev_w_in'], 'm_ev_q_norm_g': out['m_ev_q_norm_g'], 'm_ev_kv_norm_g': out['m_ev_kv_norm_g'], 'm_ev_w_uq': out['m_ev_w_uq'], 'm_ev_w_ukv': out['m_ev_w_ukv'], 'm_ev_w_out': out['m_ev_w_out'], 'm_od_w_in': out['m_od_w_in'], 'm_od_sinks': out['m_od_sinks'], 'm_od_w_out': out['m_od_w_out'], 'v_meta': out['v_meta'], 'v_norm_g': out['v_norm_g'], 'v_final_g': out['v_final_g'], 'v_ev_w_in': out['v_ev_w_in'], 'v_ev_q_norm_g': out['v_ev_q_norm_g'], 'v_ev_kv_norm_g': out['v_ev_kv_norm_g'], 'v_ev_w_uq': out['v_ev_w_uq'], 'v_ev_w_ukv': out['v_ev_w_ukv'], 'v_ev_w_out': out['v_ev_w_out'], 'v_od_w_in': out['v_od_w_in'], 'v_od_sinks': out['v_od_sinks'], 'v_od_w_out': out['v_od_w_out']}


def _loss(weights, diff, rest, loss_target):
    with _jax.named_scope("forward"):
        args = {**rest, TWIN_DIFF_INPUT: diff, **{k: w.astype(_WEIGHT_DTYPES[k]) for k, w in weights.items()}}
        y = _forward(args)
    with _jax.named_scope("loss_head"):
        err = _jnp.square(y.astype(_jnp.float32) - loss_target)
        return 0.5 * _jnp.sum(_jnp.mean(err, axis=-1)) if err.ndim else 0.5 * err


def _adamw(w, g, m, v):
    m = ADAM_B1 * m + (1.0 - ADAM_B1) * g
    v = ADAM_B2 * v + (1.0 - ADAM_B2) * _jnp.square(g)
    m_hat = m / (1.0 - ADAM_B1 ** ADAM_STEP)
    v_hat = v / (1.0 - ADAM_B2 ** ADAM_STEP)
    delta = -ADAM_LR * (m_hat / (_jnp.sqrt(v_hat) + ADAM_EPS) + ADAM_WD * w)
    return delta, m, v


def reference(x, meta, norm_g, final_g, ev_w_in, ev_q_norm_g, ev_kv_norm_g, ev_w_uq, ev_w_ukv, ev_w_out, od_w_in, od_sinks, od_w_out, loss_target, m_meta, m_norm_g, m_final_g, m_ev_w_in, m_ev_q_norm_g, m_ev_kv_norm_g, m_ev_w_uq, m_ev_w_ukv, m_ev_w_out, m_od_w_in, m_od_sinks, m_od_w_out, v_meta, v_norm_g, v_final_g, v_ev_w_in, v_ev_q_norm_g, v_ev_kv_norm_g, v_ev_w_uq, v_ev_w_ukv, v_ev_w_out, v_od_w_in, v_od_sinks, v_od_w_out):
    given = dict(x=x, meta=meta, norm_g=norm_g, final_g=final_g, ev_w_in=ev_w_in, ev_q_norm_g=ev_q_norm_g, ev_kv_norm_g=ev_kv_norm_g, ev_w_uq=ev_w_uq, ev_w_ukv=ev_w_ukv, ev_w_out=ev_w_out, od_w_in=od_w_in, od_sinks=od_sinks, od_w_out=od_w_out, loss_target=loss_target, m_meta=m_meta, m_norm_g=m_norm_g, m_final_g=m_final_g, m_ev_w_in=m_ev_w_in, m_ev_q_norm_g=m_ev_q_norm_g, m_ev_kv_norm_g=m_ev_kv_norm_g, m_ev_w_uq=m_ev_w_uq, m_ev_w_ukv=m_ev_w_ukv, m_ev_w_out=m_ev_w_out, m_od_w_in=m_od_w_in, m_od_sinks=m_od_sinks, m_od_w_out=m_od_w_out, v_meta=v_meta, v_norm_g=v_norm_g, v_final_g=v_final_g, v_ev_w_in=v_ev_w_in, v_ev_q_norm_g=v_ev_q_norm_g, v_ev_kv_norm_g=v_ev_kv_norm_g, v_ev_w_uq=v_ev_w_uq, v_ev_w_ukv=v_ev_w_ukv, v_ev_w_out=v_ev_w_out, v_od_w_in=v_od_w_in, v_od_sinks=v_od_sinks, v_od_w_out=v_od_w_out)
    weights = {n: given[n] for n in TWIN_WEIGHTS}
    shared = {n: given[n] for n in SHARED_INPUTS}
    per_example = {n: given[n] for n in ['x']}
    grad_fn = _jax.value_and_grad(_loss, argnums=(0, 1))

    def one_microbatch(ex, loss_target):
        ex = dict(ex)
        diff = ex.pop(TWIN_DIFF_INPUT)
        return grad_fn(weights, diff, {**shared, **ex}, loss_target)

    if N_MICROBATCH == 1:
        loss, (grad_w, grad_x) = one_microbatch(per_example, given["loss_target"])
    else:
        def body(carry, xs):
            loss_sum, grad_sum = carry
            l_k, (gw_k, gx_k) = one_microbatch(xs[0], xs[1])
            with _jax.named_scope("update"):
                return (loss_sum + l_k, _jax.tree.map(_jnp.add, grad_sum, gw_k)), gx_k

        init = (_jnp.zeros((), _jnp.float32), _jax.tree.map(_jnp.zeros_like, weights))
        (loss, grad_w), grad_x = _jax.lax.scan(body, init, (per_example, given["loss_target"]))
    with _jax.named_scope("update"):
        delta_w, new_m, new_v = {}, {}, {}
        for n in TWIN_WEIGHTS:
            delta_w[n], new_m[n], new_v[n] = _adamw(weights[n], grad_w[n], given["m_" + n], given["v_" + n])
    return (loss, grad_x, *[grad_w[n] for n in TWIN_WEIGHTS], *[delta_w[n] for n in TWIN_WEIGHTS],
            *[new_m[n] for n in TWIN_WEIGHTS], *[new_v[n] for n in TWIN_WEIGHTS])
```

```python
import functools
import math

import numpy as np
import jax
import jax.numpy as jnp
from jax import lax
from jax.experimental import pallas as pl
from jax.experimental.pallas import tpu as pltpu

F32 = jnp.float32
BF16 = jnp.bfloat16

D_MODEL = 1024
BLOCK = 128
N_META = 16
N_PAD = BLOCK - N_META
NORM_EPS = 1e-6
NEG = -1e30
HEAD = 64
SB_HEADS = 8
MLA_HEADS = 8
MLA_Q_LORA = 256
MLA_KV_LORA = 128
MLA_NOPE = 64
MLA_ROPE = 32
ROPE_BASE = 10000.0
SWA_HEADS = 16
SWA_KV_HEADS = 2
SWA_WINDOW = 128
N_CHIPS = 4
N_DEV = 8

ADAM_LR = 0.001
ADAM_B1 = 0.9
ADAM_B2 = 0.999
ADAM_EPS = 1e-08
ADAM_WD = 0.01
ADAM_STEP = 10

VMEM_LIMIT = 48 * 1024 * 1024

EV_G, EV_Q, EV_K, EV_V, EV_CQ, EV_CKV, EV_KR, EV_N = 0, 1024, 1536, 2048, 2560, 2816, 2944, 3072
OD_G, OD_Q, OD_K, OD_V, OD_N = 0, 1024, 2048, 2176, 2304


def _params(sem=None):
    return pltpu.CompilerParams(dimension_semantics=sem, vmem_limit_bytes=VMEM_LIMIT)


def _row_tile(m):
    return 256 if m % 256 == 0 else 128


def _dot(a, b):
    return jnp.dot(a.astype(BF16), b.astype(BF16), preferred_element_type=F32)


def _dot_nt(a, b):
    return lax.dot_general(a.astype(BF16), b.astype(BF16), (((1,), (1,)), ((), ())), preferred_element_type=F32)


def _dot_tn(a, b):
    return lax.dot_general(a.astype(BF16), b.astype(BF16), (((0,), (0,)), ((), ())), preferred_element_type=F32)


def _rms_fwd(h, g, name):
    t, d = h.shape
    tm = _row_tile(t)

    def body(h_ref, g_ref, o_ref):
        x = h_ref[...]
        r = lax.rsqrt(jnp.mean(x * x, axis=-1, keepdims=True) + NORM_EPS)
        o_ref[...] = ((x * r) * g_ref[...]).astype(o_ref.dtype)

    return pl.pallas_call(
        body, grid=(t // tm,),
        in_specs=[pl.BlockSpec((tm, d), lambda i: (i, 0)), pl.BlockSpec((1, d), lambda i: (0, 0))],
        out_specs=pl.BlockSpec((tm, d), lambda i: (i, 0)),
        out_shape=jax.ShapeDtypeStruct((t, d), BF16), compiler_params=_params(("parallel",)), name=name,
    )(h, g)


def _rms_bwd(h, g, dy, dres, name):
    t, d = h.shape
    tm = _row_tile(t)

    def body(h_ref, g_ref, dy_ref, dres_ref, dh_ref, dg_ref):
        @pl.when(pl.program_id(0) == 0)
        def _():
            dg_ref[...] = jnp.zeros_like(dg_ref)

        x = h_ref[...]
        r = lax.rsqrt(jnp.mean(x * x, axis=-1, keepdims=True) + NORM_EPS)
        xr = x * r
        dy_ = dy_ref[...]
        u = dy_ * g_ref[...]
        dh_ref[...] = dres_ref[...] + r * (u - xr * jnp.mean(u * xr, axis=-1, keepdims=True))
        dg_ref[...] += jnp.sum(dy_ * xr, axis=0, keepdims=True)

    row = pl.BlockSpec((tm, d), lambda i: (i, 0))
    vec = pl.BlockSpec((1, d), lambda i: (0, 0))
    return pl.pallas_call(
        body, grid=(t // tm,), in_specs=[row, vec, row, row], out_specs=[row, vec],
        out_shape=[jax.ShapeDtypeStruct((t, d), F32), jax.ShapeDtypeStruct((1, d), F32)],
        compiler_params=_params(("arbitrary",)), name=name,
    )(h, g, dy, dres)


def _col_tile(n):
    for c in (1024, 768, 640, 512, 384, 256, 128):
        if n % c == 0:
            return c
    return n


def _mm(a, w, name, res=None, out_dtype=F32, a_cols=None):
    m = a.shape[0]
    k, n = w.shape
    a_blk = 0 if a_cols is None else a_cols[0] // k
    assert a_cols is None or (a_cols[1] == k and a_cols[0] % k == 0)
    tm, tn = _row_tile(m), _col_tile(n)

    def body(*refs):
        if res is None:
            a_ref, w_ref, o_ref = refs
            acc = _dot(a_ref[...], w_ref[...])
        else:
            a_ref, w_ref, r_ref, o_ref = refs
            acc = r_ref[...] + _dot(a_ref[...], w_ref[...])
        o_ref[...] = acc.astype(o_ref.dtype)

    in_specs = [pl.BlockSpec((tm, k), lambda j, i: (i, a_blk)), pl.BlockSpec((k, tn), lambda j, i: (0, j))]
    args = [a, w]
    if res is not None:
        in_specs.append(pl.BlockSpec((tm, tn), lambda j, i: (i, j)))
        args.append(res)
    return pl.pallas_call(
        body, grid=(n // tn, m // tm), in_specs=in_specs, out_specs=pl.BlockSpec((tm, tn), lambda j, i: (i, j)),
        out_shape=jax.ShapeDtypeStruct((m, n), out_dtype), compiler_params=_params(("parallel", "parallel")), name=name,
    )(*args)


def _mm_nt(a, w, name):
    m, n = a.shape
    k = w.shape[0]
    tm, tk = _row_tile(m), _col_tile(k)

    def body(a_ref, w_ref, o_ref):
        o_ref[...] = _dot_nt(a_ref[...], w_ref[...])

    return pl.pallas_call(
        body, grid=(k // tk, m // tm),
        in_specs=[pl.BlockSpec((tm, n), lambda j, i: (i, 0)), pl.BlockSpec((tk, n), lambda j, i: (j, 0))],
        out_specs=pl.BlockSpec((tm, tk), lambda j, i: (i, j)),
        out_shape=jax.ShapeDtypeStruct((m, k), F32), compiler_params=_params(("parallel", "parallel")), name=name,
    )(a, w)


def _mm_tn(x, dy, name):
    m, k = x.shape
    n = dy.shape[1]
    tm, tn = _row_tile(m), _col_tile(n)

    def body(x_ref, dy_ref, o_ref):
        @pl.when(pl.program_id(1) == 0)
        def _():
            o_ref[...] = jnp.zeros_like(o_ref)

        o_ref[...] += _dot_tn(x_ref[...], dy_ref[...])

    return pl.pallas_call(
        body, grid=(n // tn, m // tm),
        in_specs=[pl.BlockSpec((tm, k), lambda j, i: (i, 0)), pl.BlockSpec((tm, tn), lambda j, i: (i, j))],
        out_specs=pl.BlockSpec((k, tn), lambda j, i: (0, j)),
        out_shape=jax.ShapeDtypeStruct((k, n), F32), compiler_params=_params(("parallel", "arbitrary")), name=name,
    )(x, dy)


def _silu_parts(g):
    s = 1.0 / (1.0 + jnp.exp(-g))
    return g * s, s * (1.0 + g * (1.0 - s))


def _gate_fwd(o_parts, proj, name):
    t = proj.shape[0]
    tm = _row_tile(t)
    w = D_MODEL // len(o_parts)

    def body(*refs):
        g_ref, o_ref = refs[-2], refs[-1]
        for p, r in enumerate(refs[:-2]):
            sil, _ = _silu_parts(g_ref[:, p * w:(p + 1) * w])
            o_ref[:, p * w:(p + 1) * w] = (r[...].astype(F32) * sil).astype(o_ref.dtype)

    return pl.pallas_call(
        body, grid=(t // tm,),
        in_specs=[pl.BlockSpec((tm, w), lambda i: (i, 0)) for _ in o_parts] + [pl.BlockSpec((tm, D_MODEL), lambda i: (i, 0))],
        out_specs=pl.BlockSpec((tm, D_MODEL), lambda i: (i, 0)),
        out_shape=jax.ShapeDtypeStruct((t, D_MODEL), BF16), compiler_params=_params(("parallel",)), name=name,
    )(*o_parts, proj)


def _gate_bwd(dao, o_parts, proj, name):
    t = proj.shape[0]
    tm = _row_tile(t)
    np_ = len(o_parts)
    w = D_MODEL // np_

    def body(*refs):
        dao_ref, g_ref = refs[0], refs[1 + np_]
        do_refs, dg_ref = refs[2 + np_:2 + 2 * np_], refs[-1]
        for p in range(np_):
            sl = slice(p * w, (p + 1) * w)
            sil, dsil = _silu_parts(g_ref[:, sl])
            da = dao_ref[:, sl]
            do_refs[p][...] = da * sil
            dg_ref[:, sl] = (da * refs[1 + p][...].astype(F32) * dsil).astype(dg_ref.dtype)

    full = pl.BlockSpec((tm, D_MODEL), lambda i: (i, 0))
    part = pl.BlockSpec((tm, w), lambda i: (i, 0))
    outs = pl.pallas_call(
        body, grid=(t // tm,), in_specs=[full] + [part] * np_ + [full], out_specs=[part] * np_ + [full],
        out_shape=[jax.ShapeDtypeStruct((t, w), F32)] * np_ + [jax.ShapeDtypeStruct((t, D_MODEL), BF16)],
        compiler_params=_params(("parallel",)), name=name,
    )(dao, *o_parts, proj)
    return outs[:np_], outs[np_]


def _loss_head(h2, gf, target, b, lp):
    d = h2.shape[1]
    nb = lp // BLOCK
    h3 = h2.reshape(b, lp, d)

    def body(h_ref, g_ref, t_ref, dh_ref, dg_ref, loss_ref):
        first = (pl.program_id(0) == 0) & (pl.program_id(1) == 0)

        @pl.when(first)
        def _():
            dg_ref[...] = jnp.zeros_like(dg_ref)
            loss_ref[...] = jnp.zeros_like(loss_ref)

        @pl.when(pl.program_id(1) == 0)
        def _():
            dh_ref[...] = jnp.zeros_like(dh_ref)

        @pl.when(pl.program_id(1) > 0)
        def _():
            x = h_ref[0]
            r = lax.rsqrt(jnp.mean(x * x, axis=-1, keepdims=True) + NORM_EPS)
            xr = x * r
            g = g_ref[...]
            diff = xr * g - t_ref[0]
            loss_ref[...] += 0.5 * jnp.sum(jnp.mean(diff * diff, axis=-1, keepdims=True))
            dy = diff * (1.0 / d)
            u = dy * g
            dh_ref[0] = r * (u - xr * jnp.mean(u * xr, axis=-1, keepdims=True))
            dg_ref[...] += jnp.sum(dy * xr, axis=0, keepdims=True)

    blk = pl.BlockSpec((1, BLOCK, d), lambda bi, n: (bi, n, 0))
    dh, dg, loss = pl.pallas_call(
        body, grid=(b, nb),
        in_specs=[blk, pl.BlockSpec((1, d), lambda bi, n: (0, 0)),
                  pl.BlockSpec((1, BLOCK, d), lambda bi, n: (bi, jnp.maximum(n - 1, 0), 0))],
        out_specs=[blk, pl.BlockSpec((1, d), lambda bi, n: (0, 0)), pl.BlockSpec((8, 128), lambda bi, n: (0, 0))],
        out_shape=[jax.ShapeDtypeStruct((b, lp, d), F32), jax.ShapeDtypeStruct((1, d), F32), jax.ShapeDtypeStruct((8, 128), F32)],
        compiler_params=_params(("arbitrary", "arbitrary")), name="loss_head",
    )(h3, gf, target)
    return dh.reshape(b * lp, d), dg, loss


def _iota2(shape, dim):
    return lax.broadcasted_iota(jnp.int32, shape, dim)


def _split_dot(x, tri):
    hi = x.astype(BF16)
    lo = (x - hi.astype(F32)).astype(BF16)
    return jnp.dot(hi, tri, preferred_element_type=F32) + jnp.dot(lo, tri, preferred_element_type=F32)


def _sb_tile(q, k, t_idx, s_idx, after):
    z = _dot_nt(q, k) * (HEAD ** -0.5)
    valid = (s_idx < t_idx) & (s_idx >= N_PAD)
    sp = jnp.log(1.0 + jnp.exp(-jnp.abs(z)))
    lb = jnp.minimum(z, 0.0) - sp
    l1m_all = -jnp.maximum(z, 0.0) - sp
    l1m = jnp.where(valid, l1m_all, 0.0)
    return lb, l1m_all, l1m, _split_dot(l1m, after), valid


def _sb_fwd(proj3, name):
    b, lp, _ = proj3.shape
    nb = lp // BLOCK
    npair = SB_HEADS // 2

    def body(q_ref, k_ref, v_ref, o_ref, tot_ref):
        after = (_iota2((BLOCK, BLOCK), 0) > _iota2((BLOCK, BLOCK), 1)).astype(BF16)
        for hh in range(2):
            ls = slice(hh * HEAD, (hh + 1) * HEAD)

            def qblock(i, _):
                r0 = pl.multiple_of(i * BLOCK, BLOCK)
                q = q_ref[0, pl.ds(r0, BLOCK), ls]
                t_idx = r0 + _iota2((BLOCK, BLOCK), 0)

                def kblock(jj, carry):
                    c, acc = carry
                    s0 = pl.multiple_of((i - jj) * BLOCK, BLOCK)
                    k = k_ref[0, pl.ds(s0, BLOCK), ls]
                    v = v_ref[0, pl.ds(s0, BLOCK), ls]
                    lb, _, l1m, suf, valid = _sb_tile(q, k, t_idx, s0 + _iota2((BLOCK, BLOCK), 1), after)
                    a = jnp.where(valid, jnp.exp(lb + suf + c), 0.0)
                    return c + jnp.sum(l1m, axis=1, keepdims=True), acc + _dot(a, v)

                c, acc = lax.fori_loop(0, i + 1, kblock, (jnp.zeros((BLOCK, 1), F32), jnp.zeros((BLOCK, HEAD), F32)))
                o_ref[0, pl.ds(r0, BLOCK), ls] = acc
                tot_ref[0, pl.ds(r0, BLOCK), ls] = jnp.broadcast_to(c, (BLOCK, HEAD))
                return 0

            lax.fori_loop(0, nb, qblock, 0)

    def col(first):
        return pl.BlockSpec((1, lp, 2 * HEAD), lambda bi, hp: (bi, 0, first // (2 * HEAD) + hp))

    shp = jax.ShapeDtypeStruct((b, lp, SB_HEADS * HEAD), F32)
    return pl.pallas_call(
        body, grid=(b, npair), in_specs=[col(EV_Q), col(EV_K), col(EV_V)], out_specs=[col(0), col(0)], out_shape=[shp, shp],
        compiler_params=_params(("parallel", "parallel")), name=name,
    )(proj3, proj3, proj3)


def _sb_bwd(proj3, tot, do, name):
    b, lp, _ = proj3.shape
    nb = lp // BLOCK
    npair = SB_HEADS // 2

    def body(q_ref, k_ref, v_ref, tot_ref, do_ref, dq_ref, dk_ref, dv_ref, dk_acc, dv_acc):
        after = (_iota2((BLOCK, BLOCK), 0) > _iota2((BLOCK, BLOCK), 1)).astype(BF16)
        before = (_iota2((BLOCK, BLOCK), 0) < _iota2((BLOCK, BLOCK), 1)).astype(BF16)
        dk_acc[...] = jnp.zeros_like(dk_acc)
        dv_acc[...] = jnp.zeros_like(dv_acc)
        for hh in range(2):
            ls = slice(hh * HEAD, (hh + 1) * HEAD)

            def qblock(i, _):
                r0 = pl.multiple_of(i * BLOCK, BLOCK)
                q = q_ref[0, pl.ds(r0, BLOCK), ls]
                do_i = do_ref[0, pl.ds(r0, BLOCK), ls]
                tot_i = tot_ref[0, pl.ds(r0, BLOCK), ls][:, 0:1]
                t_idx = r0 + _iota2((BLOCK, BLOCK), 0)

                def kblock(j, carry):
                    left, pre, dq = carry
                    s0 = pl.multiple_of(j * BLOCK, BLOCK)
                    k = k_ref[0, pl.ds(s0, BLOCK), ls]
                    v = v_ref[0, pl.ds(s0, BLOCK), ls]
                    lb, l1m_all, l1m, suf, valid = _sb_tile(q, k, t_idx, s0 + _iota2((BLOCK, BLOCK), 1), after)
                    here = jnp.sum(l1m, axis=1, keepdims=True)
                    a = jnp.where(valid, jnp.exp(lb + suf + (tot_i - left - here)), 0.0)
                    w = a * _dot_nt(do_i, v)
                    dz = jnp.where(valid, w * jnp.exp(l1m_all) - (pre + _split_dot(w, before)) * jnp.exp(lb), 0.0) * (HEAD ** -0.5)
                    dk_acc[hh, pl.ds(s0, BLOCK), :] += _dot_tn(dz, q)
                    dv_acc[hh, pl.ds(s0, BLOCK), :] += _dot_tn(a, do_i)
                    return left + here, pre + jnp.sum(w, axis=1, keepdims=True), dq + _dot(dz, k)

                zero = jnp.zeros((BLOCK, 1), F32)
                _, _, dq = lax.fori_loop(0, i + 1, kblock, (zero, zero, jnp.zeros((BLOCK, HEAD), F32)))
                dq_ref[0, pl.ds(r0, BLOCK), ls] = dq
                return 0

            lax.fori_loop(0, nb, qblock, 0)
            dk_ref[0, :, ls] = dk_acc[hh]
            dv_ref[0, :, ls] = dv_acc[hh]

    def col(first):
        return pl.BlockSpec((1, lp, 2 * HEAD), lambda bi, hp: (bi, 0, first // (2 * HEAD) + hp))

    shp = jax.ShapeDtypeStruct((b, lp, SB_HEADS * HEAD), F32)
    return pl.pallas_call(
        body, grid=(b, npair), in_specs=[col(EV_Q), col(EV_K), col(EV_V), col(0), col(0)], out_specs=[col(0)] * 3,
        out_shape=[shp] * 3, scratch_shapes=[pltpu.VMEM((2, lp, HEAD), F32), pltpu.VMEM((2, lp, HEAD), F32)],
        compiler_params=_params(("parallel", "parallel")), name=name,
    )(proj3, proj3, proj3, tot, do)


def _rope_tables(lp):
    half = MLA_ROPE // 2
    pos = (np.arange(lp) - N_PAD).astype(np.float32)
    inv = jnp.asarray(ROPE_BASE, F32) ** (-jnp.arange(half, dtype=F32) / half)
    ang = jnp.asarray(pos)[:, None] * inv[None, :]
    cos, sin = jnp.cos(ang), jnp.sin(ang)
    zeros = lambda n: jnp.zeros((lp, n), F32)
    c = jnp.concatenate([jnp.ones((lp, MLA_NOPE), F32), cos, cos, zeros(32)], axis=1)
    s1 = jnp.concatenate([zeros(MLA_NOPE), -sin, zeros(half), zeros(32)], axis=1)
    s2 = jnp.concatenate([zeros(MLA_NOPE), zeros(half), sin, zeros(32)], axis=1)
    return c, s1, s2


def _rope(x, c, s1, s2):
    half = MLA_ROPE // 2
    return x * c + pltpu.roll(x, BLOCK - half, 1) * s1 + pltpu.roll(x, half, 1) * s2


def _rope_t(dy, c, s1, s2):
    half = MLA_ROPE // 2
    return dy * c + pltpu.roll(dy * s1, half, 1) + pltpu.roll(dy * s2, BLOCK - half, 1)


def _rms_rows(x, g):
    r = lax.rsqrt(jnp.mean(x * x, axis=-1, keepdims=True) + NORM_EPS)
    return x * r, r


def _mla_prep_fwd(proj3, gq, gkv, wq, wk, wv, tabs, name):
    b, lp, _ = proj3.shape
    nb = lp // BLOCK
    hw = MLA_HEADS * BLOCK

    def body(cq_ref, ckv_ref, kr_ref, gq_ref, gkv_ref, wq_ref, wk_ref, wv_ref, c_ref, s1_ref, s2_ref, qf_ref, kf_ref, v_ref):
        c, s1, s2 = c_ref[...], s1_ref[...], s2_ref[...]
        xq, _ = _rms_rows(cq_ref[0], None)
        qh = _dot(xq * gq_ref[...], wq_ref[...])
        xk, _ = _rms_rows(ckv_ref[0], None)
        ckv_n = xk * gkv_ref[...]
        kv = _dot(ckv_n, wk_ref[...])
        v_ref[0] = _dot(ckv_n, wv_ref[...]).astype(v_ref.dtype)
        kr = _rope(kr_ref[0], c, s1, s2)
        for h in range(MLA_HEADS):
            ls = slice(h * BLOCK, (h + 1) * BLOCK)
            qf_ref[0, :, ls] = _rope(qh[:, ls], c, s1, s2).astype(qf_ref.dtype)
            kf_ref[0, :, ls] = (kv[:, ls] + kr).astype(kf_ref.dtype)

    def col(first, width):
        return pl.BlockSpec((1, BLOCK, width), lambda bi, n: (bi, n, first // width))

    def whole(a):
        return pl.BlockSpec(a.shape, lambda bi, n: (0,) * a.ndim)

    tab = pl.BlockSpec((BLOCK, BLOCK), lambda bi, n: (n, 0))
    return pl.pallas_call(
        body, grid=(b, nb),
        in_specs=[col(EV_CQ, MLA_Q_LORA), col(EV_CKV, MLA_KV_LORA), col(EV_KR, BLOCK), whole(gq), whole(gkv), whole(wq), whole(wk),
                  whole(wv), tab, tab, tab],
        out_specs=[col(0, hw), col(0, hw), col(0, MLA_HEADS * HEAD)],
        out_shape=[jax.ShapeDtypeStruct((b, lp, hw), BF16), jax.ShapeDtypeStruct((b, lp, hw), BF16),
                   jax.ShapeDtypeStruct((b, lp, MLA_HEADS * HEAD), BF16)],
        compiler_params=_params(("parallel", "parallel")), name=name,
    )(proj3, proj3, proj3, gq, gkv, wq, wk, wv, *tabs)


def _mla_prep_bwd(proj3, gq, gkv, wq, wk, wv, tabs, dqf, dkf, dv, name):
    b, lp, _ = proj3.shape
    nb = lp // BLOCK
    hw = MLA_HEADS * BLOCK

    def body(cq_ref, ckv_ref, gq_ref, gkv_ref, wq_ref, wk_ref, wv_ref, c_ref, s1_ref, s2_ref, dqf_ref, dkf_ref, dv_ref,
             dcq_ref, dckv_ref, dkr_ref, dwq_ref, dwk_ref, dwv_ref, dgq_ref, dgkv_ref, dqh):
        @pl.when((pl.program_id(0) == 0) & (pl.program_id(1) == 0))
        def _():
            for r in (dwq_ref, dwk_ref, dwv_ref, dgq_ref, dgkv_ref):
                r[...] = jnp.zeros_like(r)

        c, s1, s2 = c_ref[...], s1_ref[...], s2_ref[...]
        dkr = jnp.zeros((BLOCK, BLOCK), F32)
        for h in range(MLA_HEADS):
            ls = slice(h * BLOCK, (h + 1) * BLOCK)
            dqh[:, ls] = _rope_t(dqf_ref[0, :, ls].astype(F32), c, s1, s2).astype(dqh.dtype)
            dkr = dkr + dkf_ref[0, :, ls].astype(F32)
        dkr_ref[0] = _rope_t(dkr, c, s1, s2).astype(dkr_ref.dtype)

        def norm_bwd(x, g, dy, dg_ref):
            xr, r = _rms_rows(x, None)
            u = dy * g
            dg_ref[...] += jnp.sum(dy * xr, axis=0, keepdims=True)
            return r * (u - xr * jnp.mean(u * xr, axis=-1, keepdims=True))

        xq, _ = _rms_rows(cq_ref[0], None)
        cq_n = xq * gq_ref[...]
        dwq_ref[...] += _dot_tn(cq_n, dqh[...])
        dcq_ref[0] = norm_bwd(cq_ref[0], gq_ref[...], _dot_nt(dqh[...], wq_ref[...]), dgq_ref).astype(dcq_ref.dtype)
        xk, _ = _rms_rows(ckv_ref[0], None)
        ckv_n = xk * gkv_ref[...]
        dkf_, dv_ = dkf_ref[0], dv_ref[0]
        dwk_ref[...] += _dot_tn(ckv_n, dkf_)
        dwv_ref[...] += _dot_tn(ckv_n, dv_)
        dckv_n = _dot_nt(dkf_, wk_ref[...]) + _dot_nt(dv_, wv_ref[...])
        dckv_ref[0] = norm_bwd(ckv_ref[0], gkv_ref[...], dckv_n, dgkv_ref).astype(dckv_ref.dtype)

    def col(first, width):
        return pl.BlockSpec((1, BLOCK, width), lambda bi, n: (bi, n, first // width))

    def whole(a):
        return pl.BlockSpec(a.shape, lambda bi, n: (0,) * len(a.shape))

    tab = pl.BlockSpec((BLOCK, BLOCK), lambda bi, n: (n, 0))
    acc_shapes = [jax.ShapeDtypeStruct(a.shape, F32) for a in (wq, wk, wv, gq, gkv)]
    return pl.pallas_call(
        body, grid=(b, nb),
        in_specs=[col(EV_CQ, MLA_Q_LORA), col(EV_CKV, MLA_KV_LORA), whole(gq), whole(gkv), whole(wq), whole(wk), whole(wv), tab, tab, tab,
                  col(0, hw), col(0, hw), col(0, MLA_HEADS * HEAD)],
        out_specs=[col(0, MLA_Q_LORA), col(0, MLA_KV_LORA), col(0, BLOCK)] + [whole(a) for a in acc_shapes],
        out_shape=[jax.ShapeDtypeStruct((b, lp, MLA_Q_LORA), BF16), jax.ShapeDtypeStruct((b, lp, MLA_KV_LORA), BF16),
                   jax.ShapeDtypeStruct((b, lp, BLOCK), BF16)] + acc_shapes,
        scratch_shapes=[pltpu.VMEM((BLOCK, hw), BF16)],
        compiler_params=_params(("arbitrary", "arbitrary")), name=name,
    )(proj3, proj3, gq, gkv, wq, wk, wv, *tabs, dqf, dkf, dv)


def _mla_scores(q, k, t_idx, s_idx):
    s = _dot_nt(q, k) * ((MLA_NOPE + MLA_ROPE) ** -0.5)
    valid = (s_idx <= t_idx) & (s_idx >= N_PAD)
    return jnp.where(valid, s, NEG), valid


def _mla_fwd(qf, kf, v, name):
    b, lp, _ = qf.shape
    nb = lp // BLOCK
    npair = MLA_HEADS // 2

    def body(q_ref, k_ref, v_ref, o_ref, lse_ref):
        for hh in range(2):
            ls = slice(hh * HEAD, (hh + 1) * HEAD)
            lq = slice(hh * BLOCK, (hh + 1) * BLOCK)

            def qblock(i, _):
                r0 = pl.multiple_of(i * BLOCK, BLOCK)
                q = q_ref[0, pl.ds(r0, BLOCK), lq]
                t_idx = r0 + _iota2((BLOCK, BLOCK), 0)

                def kblock(j, carry):
                    m, l, acc = carry
                    s0 = pl.multiple_of(j * BLOCK, BLOCK)
                    s, valid = _mla_scores(q, k_ref[0, pl.ds(s0, BLOCK), lq], t_idx, s0 + _iota2((BLOCK, BLOCK), 1))
                    m_new = jnp.maximum(m, jnp.max(s, axis=1, keepdims=True))
                    p = jnp.where(valid, jnp.exp(s - m_new), 0.0)
                    alpha = jnp.exp(m - m_new)
                    return m_new, alpha * l + jnp.sum(p, axis=1, keepdims=True), alpha * acc + _dot(p, v_ref[0, pl.ds(s0, BLOCK), ls])

                m, l, acc = lax.fori_loop(0, i + 1, kblock, (jnp.full((BLOCK, 1), NEG, F32), jnp.zeros((BLOCK, 1), F32),
                                                             jnp.zeros((BLOCK, HEAD), F32)))
                seen = l > 0.0
                o_ref[0, pl.ds(r0, BLOCK), ls] = acc / jnp.where(seen, l, 1.0)
                lse_ref[0, pl.ds(r0, BLOCK), ls] = jnp.broadcast_to(jnp.where(seen, m + jnp.log(jnp.where(seen, l, 1.0)), 0.0), (BLOCK, HEAD))
                return 0

            lax.fori_loop(0, nb, qblock, 0)

    wide = pl.BlockSpec((1, lp, 2 * BLOCK), lambda bi, hp: (bi, 0, hp))
    thin = pl.BlockSpec((1, lp, 2 * HEAD), lambda bi, hp: (bi, 0, hp))
    shp = jax.ShapeDtypeStruct((b, lp, MLA_HEADS * HEAD), F32)
    return pl.pallas_call(
        body, grid=(b, npair), in_specs=[wide, wide, thin], out_specs=[thin, thin], out_shape=[shp, shp],
        compiler_params=_params(("parallel", "parallel")), name=name,
    )(qf, kf, v)


def _mla_bwd(qf, kf, v, o, lse, do, name):
    b, lp, _ = qf.shape
    nb = lp // BLOCK
    npair = MLA_HEADS // 2

    def body(q_ref, k_ref, v_ref, o_ref, lse_ref, do_ref, dq_ref, dk_ref, dv_ref, dk_acc, dv_acc):
        dk_acc[...] = jnp.zeros_like(dk_acc)
        dv_acc[...] = jnp.zeros_like(dv_acc)
        for hh in range(2):
            ls = slice(hh * HEAD, (hh + 1) * HEAD)
            lq = slice(hh * BLOCK, (hh + 1) * BLOCK)

            def qblock(i, _):
                r0 = pl.multiple_of(i * BLOCK, BLOCK)
                q = q_ref[0, pl.ds(r0, BLOCK), lq]
                do_i = do_ref[0, pl.ds(r0, BLOCK), ls]
                dsum = jnp.sum(do_i * o_ref[0, pl.ds(r0, BLOCK), ls], axis=1, keepdims=True)
                lse_i = lse_ref[0, pl.ds(r0, BLOCK), ls][:, 0:1]
                t_idx = r0 + _iota2((BLOCK, BLOCK), 0)

                def kblock(j, dq):
                    s0 = pl.multiple_of(j * BLOCK, BLOCK)
                    k = k_ref[0, pl.ds(s0, BLOCK), lq]
                    v_j = v_ref[0, pl.ds(s0, BLOCK), ls]
                    s, valid = _mla_scores(q, k, t_idx, s0 + _iota2((BLOCK, BLOCK), 1))
                    p = jnp.where(valid, jnp.exp(s - lse_i), 0.0)
                    ds = p * (_dot_nt(do_i, v_j) - dsum) * ((MLA_NOPE + MLA_ROPE) ** -0.5)
                    dk_acc[hh, pl.ds(s0, BLOCK), :] += _dot_tn(ds, q)
                    dv_acc[hh, pl.ds(s0, BLOCK), :] += _dot_tn(p, do_i)
                    return dq + _dot(ds, k)

                dq = lax.fori_loop(0, i + 1, kblock, jnp.zeros((BLOCK, BLOCK), F32))
                dq_ref[0, pl.ds(r0, BLOCK), lq] = dq.astype(dq_ref.dtype)
                return 0

            lax.fori_loop(0, nb, qblock, 0)
            dk_ref[0, :, lq] = dk_acc[hh].astype(dk_ref.dtype)
            dv_ref[0, :, ls] = dv_acc[hh].astype(dv_ref.dtype)

    wide = pl.BlockSpec((1, lp, 2 * BLOCK), lambda bi, hp: (bi, 0, hp))
    thin = pl.BlockSpec((1, lp, 2 * HEAD), lambda bi, hp: (bi, 0, hp))
    return pl.pallas_call(
        body, grid=(b, npair), in_specs=[wide, wide, thin, thin, thin, thin], out_specs=[wide, wide, thin],
        out_shape=[jax.ShapeDtypeStruct(qf.shape, BF16), jax.ShapeDtypeStruct(qf.shape, BF16), jax.ShapeDtypeStruct(v.shape, BF16)],
        scratch_shapes=[pltpu.VMEM((2, lp, BLOCK), F32), pltpu.VMEM((2, lp, HEAD), F32)],
        compiler_params=_params(("parallel", "parallel")), name=name,
    )(qf, kf, v, o, lse, do)


def _swa_tile(q_ref, k_ref, v_ref, sink_ref, n, pair, hh):
    prev = jnp.maximum(n - 1, 0)
    rows = lambda blk: pl.ds(pl.multiple_of(blk * BLOCK, BLOCK), BLOCK)
    kc = jnp.concatenate([k_ref[0, 0, rows(prev), :], k_ref[0, 0, rows(n), :], k_ref[0, 0, 0:BLOCK, :]], axis=0)
    vc = jnp.concatenate([v_ref[0, 0, rows(prev), :], v_ref[0, 0, rows(n), :], v_ref[0, 0, 0:BLOCK, :]], axis=0)
    col = _iota2((BLOCK, 3 * BLOCK), 1)
    loc = col % BLOCK
    s_idx = jnp.where(col < BLOCK, (n - 1) * BLOCK + loc, jnp.where(col < 2 * BLOCK, n * BLOCK + loc, loc))
    dist = n * BLOCK + _iota2((BLOCK, 3 * BLOCK), 0) - s_idx
    band = (col < 2 * BLOCK) & (dist >= 0) & (dist < SWA_WINDOW) & (s_idx >= BLOCK)
    meta = (col >= 2 * BLOCK) & (s_idx >= N_PAD) & (dist >= 0)
    valid = band | meta
    head = pair * 2 + hh
    slope = jnp.exp(jnp.full((1, 1), -8.0 * math.log(2.0) / SWA_HEADS, F32) * (head + 1).astype(F32))
    q = q_ref[0, :, hh * HEAD:(hh + 1) * HEAD]
    s = _dot_nt(q, kc) * (HEAD ** -0.5) - slope * dist.astype(F32)
    s = jnp.where(valid, s, NEG)
    sink = sink_ref[pl.ds(head, 1), 0:1]
    m = jnp.maximum(jnp.max(s, axis=1, keepdims=True), sink)
    e = jnp.where(valid, jnp.exp(s - m), 0.0)
    es = jnp.exp(sink - m)
    inv = 1.0 / (jnp.sum(e, axis=1, keepdims=True) + es)
    return e * inv, es * inv, q, kc, vc, prev


def _swa_specs(b, lp):
    nb = lp // BLOCK
    group = SWA_HEADS // SWA_KV_HEADS // 2
    grid = (b, SWA_KV_HEADS, nb, group)
    qcol = lambda first: pl.BlockSpec((1, BLOCK, 2 * HEAD), lambda bi, kv, n, p: (bi, n, first // (2 * HEAD) + kv * group + p))
    kvspec = pl.BlockSpec((1, 1, lp, HEAD), lambda bi, kv, n, p: (bi, kv, 0, 0))
    sink = pl.BlockSpec((SWA_HEADS, BLOCK), lambda bi, kv, n, p: (0, 0))
    return grid, qcol, kvspec, sink, group


def _swa_fwd(proj3, k4, v4, sinks, name):
    b, lp, _ = proj3.shape
    grid, qcol, kvspec, sink, group = _swa_specs(b, lp)

    def body(q_ref, k_ref, v_ref, sink_ref, o_ref):
        n, pair = pl.program_id(2), pl.program_id(1) * group + pl.program_id(3)
        for hh in range(2):
            p, _, _, _, vc, _ = _swa_tile(q_ref, k_ref, v_ref, sink_ref, n, pair, hh)
            o_ref[0, :, hh * HEAD:(hh + 1) * HEAD] = _dot(p, vc)

    return pl.pallas_call(
        body, grid=grid, in_specs=[qcol(OD_Q), kvspec, kvspec, sink], out_specs=qcol(0),
        out_shape=jax.ShapeDtypeStruct((b, lp, SWA_HEADS * HEAD), F32),
        compiler_params=_params(("parallel", "parallel", "parallel", "parallel")), name=name,
    )(proj3, k4, v4, sinks)


def _swa_bwd(proj3, k4, v4, sinks, do, name):
    b, lp, _ = proj3.shape
    grid, qcol, kvspec, sink, group = _swa_specs(b, lp)

    def body(q_ref, k_ref, v_ref, sink_ref, do_ref, dq_ref, dk_ref, dv_ref, dsink_ref):
        n, pair = pl.program_id(2), pl.program_id(1) * group + pl.program_id(3)
        first_of_kv = (n == 0) & (pl.program_id(3) == 0)

        @pl.when(first_of_kv & (pl.program_id(0) == 0) & (pl.program_id(1) == 0))
        def _():
            dsink_ref[...] = jnp.zeros_like(dsink_ref)

        @pl.when(first_of_kv)
        def _():
            dk_ref[...] = jnp.zeros_like(dk_ref)
            dv_ref[...] = jnp.zeros_like(dv_ref)

        rows = lambda blk: pl.ds(pl.multiple_of(blk * BLOCK, BLOCK), BLOCK)
        for hh in range(2):
            p, ps, q, kc, vc, prev = _swa_tile(q_ref, k_ref, v_ref, sink_ref, n, pair, hh)
            do_h = do_ref[0, :, hh * HEAD:(hh + 1) * HEAD]
            dp = _dot_nt(do_h, vc)
            dsum = jnp.sum(p * dp, axis=1, keepdims=True)
            ds = p * (dp - dsum)
            dsink_ref[pl.ds(pair * 2 + hh, 1), :] += jnp.broadcast_to(-jnp.sum(ps * dsum, axis=0, keepdims=True), (1, BLOCK))
            dsc = ds * (HEAD ** -0.5)
            dq_ref[0, :, hh * HEAD:(hh + 1) * HEAD] = _dot(dsc, kc)
            dkc = _dot_tn(dsc, q)
            dvc = _dot_tn(p, do_h)
            for part, blk in enumerate((prev, n, 0)):
                r = rows(blk) if part < 2 else slice(0, BLOCK)
                dk_ref[0, 0, r, :] += dkc[part * BLOCK:(part + 1) * BLOCK]
                dv_ref[0, 0, r, :] += dvc[part * BLOCK:(part + 1) * BLOCK]

    return pl.pallas_call(
        body, grid=grid, in_specs=[qcol(OD_Q), kvspec, kvspec, sink, qcol(0)], out_specs=[qcol(0), kvspec, kvspec, sink],
        out_shape=[jax.ShapeDtypeStruct((b, lp, SWA_HEADS * HEAD), F32), jax.ShapeDtypeStruct(k4.shape, F32),
                   jax.ShapeDtypeStruct(k4.shape, F32), jax.ShapeDtypeStruct((SWA_HEADS, BLOCK), F32)],
        compiler_params=_params(("arbitrary", "arbitrary", "arbitrary", "arbitrary")), name=name,
    )(proj3, k4, v4, sinks, do)


def _kernel_weights(ev_w_in, ev_w_uq, ev_w_ukv, od_w_in):
    zeros = lambda r, c: jnp.zeros((r, c), ev_w_in.dtype)
    q_sb, k_sb, v_sb, g_sb, c_q, c_kv, k_r, g_mla = jnp.split(ev_w_in, [512, 1024, 1536, 2048, 2304, 2432, 2464], axis=1)
    w0 = jnp.concatenate([g_sb, g_mla, q_sb, k_sb, v_sb, c_q, c_kv, zeros(D_MODEL, MLA_NOPE), k_r, zeros(D_MODEL, 32)], axis=1)
    uq = ev_w_uq.reshape(MLA_Q_LORA, MLA_HEADS, MLA_NOPE + MLA_ROPE)
    wq = jnp.pad(uq, ((0, 0), (0, 0), (0, BLOCK - MLA_NOPE - MLA_ROPE))).reshape(MLA_Q_LORA, MLA_HEADS * BLOCK)
    ukv = ev_w_ukv.reshape(MLA_KV_LORA, MLA_HEADS, BLOCK)
    wk = jnp.pad(ukv[:, :, :MLA_NOPE], ((0, 0), (0, 0), (0, BLOCK - MLA_NOPE))).reshape(MLA_KV_LORA, MLA_HEADS * BLOCK)
    wv = ukv[:, :, MLA_NOPE:].reshape(MLA_KV_LORA, MLA_HEADS * HEAD)
    q, k, v, g = jnp.split(od_w_in, [1024, 1152, 1280], axis=1)
    w1 = jnp.concatenate([g, q, k, v], axis=1)
    return w0, wq, wk, wv, w1


def _original_grads(dw0, dwq, dwk, dwv, dw1):
    sl = lambda a, first, n: a[:, first:first + n]
    d_ev_w_in = jnp.concatenate([sl(dw0, EV_Q, 512), sl(dw0, EV_K, 512), sl(dw0, EV_V, 512), sl(dw0, EV_G, 512), sl(dw0, EV_CQ, 256),
                                 sl(dw0, EV_CKV, 128), sl(dw0, EV_KR + MLA_NOPE, MLA_ROPE), sl(dw0, EV_G + 512, 512)], axis=1)
    d_uq = dwq.reshape(MLA_Q_LORA, MLA_HEADS, BLOCK)[:, :, :MLA_NOPE + MLA_ROPE].reshape(MLA_Q_LORA, -1)
    d_ukv = jnp.concatenate([dwk.reshape(MLA_KV_LORA, MLA_HEADS, BLOCK)[:, :, :MLA_NOPE], dwv.reshape(MLA_KV_LORA, MLA_HEADS, HEAD)],
                            axis=2).reshape(MLA_KV_LORA, -1)
    d_od_w_in = jnp.concatenate([sl(dw1, OD_Q, 1024), sl(dw1, OD_K, 128), sl(dw1, OD_V, 128), sl(dw1, OD_G, 1024)], axis=1)
    return d_ev_w_in, d_uq, d_ukv, d_od_w_in


def _meta_rows_sum(dh0_3):
    b, _, d = dh0_3.shape

    def body(x_ref, o_ref):
        acc = x_ref[0, N_PAD:BLOCK, :]
        for i in range(1, b):
            acc = acc + x_ref[i, N_PAD:BLOCK, :]
        o_ref[...] = acc

    return pl.pallas_call(
        body, grid=(1,), in_specs=[pl.BlockSpec((b, BLOCK, d), lambda i: (0, 0, 0))], out_specs=pl.BlockSpec((N_META, d), lambda i: (0, 0)),
        out_shape=jax.ShapeDtypeStruct((N_META, d), F32), compiler_params=_params(("arbitrary",)), name="meta_rows_sum",
    )(dh0_3)


def _local_step(x, meta, norm_g, final_g, gq, gkv, sinks, target, ev_w_in, ev_w_uq, ev_w_ukv, wo0, od_w_in, wo1):
    b, seq, d = x.shape
    lp = seq + BLOCK
    t = b * lp
    w0, wq, wk, wv, w1 = _kernel_weights(ev_w_in, ev_w_uq, ev_w_ukv, od_w_in)
    h0 = jnp.concatenate([jnp.zeros((b, N_PAD, d), F32), jnp.broadcast_to(meta[None], (b, N_META, d)), x], axis=1).reshape(t, d)
    tabs = _rope_tables(lp)
    g0, g1 = norm_g[0:1], norm_g[1:2]

    hn0 = _rms_fwd(h0, g0, "norm0")
    proj0 = _mm(hn0, w0, "inproj0")
    p0 = proj0.reshape(b, lp, EV_N)
    o_sb, sb_tot = _sb_fwd(p0, "sb_fwd")
    qf, kf, v = _mla_prep_fwd(p0, gq, gkv, wq, wk, wv, tabs, "mla_prep_fwd")
    o_mla, lse = _mla_fwd(qf, kf, v, "mla_fwd")
    o0 = [o_sb.reshape(t, -1), o_mla.reshape(t, -1)]
    ao0 = _gate_fwd(o0, proj0, "gate0")
    h1 = _mm(ao0, wo0, "outproj0", res=h0)

    hn1 = _rms_fwd(h1, g1, "norm1")
    proj1 = _mm(hn1, w1, "inproj1")
    p1 = proj1.reshape(b, lp, OD_N)
    heads = lambda a: a.reshape(b, lp, SWA_KV_HEADS, HEAD).transpose(0, 2, 1, 3)
    k4, v4 = heads(proj1[:, OD_K:OD_K + 128]), heads(proj1[:, OD_V:OD_V + 128])
    sinks_b = jnp.broadcast_to(sinks.reshape(SWA_HEADS, 1), (SWA_HEADS, BLOCK))
    o1 = _swa_fwd(p1, k4, v4, sinks_b, "swa_fwd").reshape(t, -1)
    ao1 = _gate_fwd([o1], proj1, "gate1")
    h2 = _mm(ao1, wo1, "outproj1", res=h1)

    dh2, d_final_g, loss = _loss_head(h2, final_g.reshape(1, d), target, b, lp)

    d_wo1 = _mm_tn(ao1, dh2, "d_wo1")
    dao1 = _mm_nt(dh2, wo1, "d_ao1")
    (do1,), dg1 = _gate_bwd(dao1, [o1], proj1, "gate1_bwd")
    dq1, dk4, dv4, d_sinks = _swa_bwd(p1, k4, v4, sinks_b, do1.reshape(b, lp, -1), "swa_bwd")
    unheads = lambda a: a.transpose(0, 2, 1, 3).reshape(t, SWA_KV_HEADS * HEAD).astype(BF16)
    dproj1 = jnp.concatenate([dg1, dq1.reshape(t, -1).astype(BF16), unheads(dk4), unheads(dv4)], axis=1)
    d_w1 = _mm_tn(hn1, dproj1, "d_w1")
    dhn1 = _mm_nt(dproj1, w1, "d_hn1")
    dh1, d_g1 = _rms_bwd(h1, g1, dhn1, dh2, "norm1_bwd")

    d_wo0 = _mm_tn(ao0, dh1, "d_wo0")
    dao0 = _mm_nt(dh1, wo0, "d_ao0")
    (do_sb, do_mla), dg0 = _gate_bwd(dao0, o0, proj0, "gate0_bwd")
    dq_sb, dk_sb, dv_sb = _sb_bwd(p0, sb_tot, do_sb.reshape(b, lp, -1), "sb_bwd")
    dqf, dkf, dv = _mla_bwd(qf, kf, v, o_mla, lse, do_mla.reshape(b, lp, -1), "mla_bwd")
    dcq, dckv, dkr, d_wq, d_wk, d_wv, d_gq, d_gkv = _mla_prep_bwd(p0, gq, gkv, wq, wk, wv, tabs, dqf, dkf, dv, "mla_prep_bwd")
    flat = lambda a: a.reshape(t, -1).astype(BF16)
    dproj0 = jnp.concatenate([dg0, flat(dq_sb), flat(dk_sb), flat(dv_sb), flat(dcq), flat(dckv), flat(dkr)], axis=1)
    d_w0 = _mm_tn(hn0, dproj0, "d_w0")
    dhn0 = _mm_nt(dproj0, w0, "d_hn0")
    dh0, d_g0 = _rms_bwd(h0, g0, dhn0, dh1, "norm0_bwd")
    dh0 = dh0.reshape(b, lp, d)

    d_ev_w_in, d_uq, d_ukv, d_od_w_in = _original_grads(d_w0, d_wq, d_wk, d_wv, d_w1)
    grads = dict(meta=_meta_rows_sum(dh0), norm_g=jnp.concatenate([d_g0, d_g1], axis=0), final_g=d_final_g.reshape(d),
                 ev_w_in=d_ev_w_in, ev_q_norm_g=d_gq, ev_kv_norm_g=d_gkv, ev_w_uq=d_uq, ev_w_ukv=d_ukv, ev_w_out=d_wo0,
                 od_w_in=d_od_w_in, od_sinks=d_sinks[:, 0].reshape(1, SWA_HEADS), od_w_out=d_wo1)
    return loss, dh0[:, BLOCK:], grads


MESH = pl.DeviceIdType.MESH
ANY = pl.BlockSpec(memory_space=pl.ANY)


def _place():
    return lax.axis_index("x"), lax.axis_index("y"), lax.axis_index("c")


def _other_chips(x, y):
    return [(1 - x, y), (x, 1 - y), (1 - x, 1 - y)]


def _all_gather_chips(arrs, name):
    n = len(arrs)

    def body(*refs):
        ins, outs = refs[:n], refs[n:2 * n]
        send_sems, recv_sems, loc_sems = refs[2 * n:]
        x, y, c = _place()
        me = 2 * x + y
        started = []
        for a in range(n):
            loc = pltpu.make_async_copy(ins[a], outs[a].at[me], loc_sems.at[a])
            loc.start()
            started.append(loc)
            for k, (px, py) in enumerate(_other_chips(x, y)):
                cp = pltpu.make_async_remote_copy(src_ref=ins[a], dst_ref=outs[a].at[me], send_sem=send_sems.at[3 * a + k],
                                                  recv_sem=recv_sems.at[3 * a + k], device_id=(px, py, c), device_id_type=MESH)
                cp.start()
        for a in range(n):
            for k, (px, py) in enumerate(_other_chips(x, y)):
                cp = pltpu.make_async_remote_copy(src_ref=ins[a], dst_ref=outs[a].at[2 * px + py], send_sem=send_sems.at[3 * a + k],
                                                  recv_sem=recv_sems.at[3 * a + k], device_id=(px, py, c), device_id_type=MESH)
                cp.wait_recv()
                cp.wait_send()
        for loc in started:
            loc.wait()

    return pl.pallas_call(
        body, in_specs=[ANY] * n, out_specs=[ANY] * n,
        out_shape=[jax.ShapeDtypeStruct((N_CHIPS,) + a.shape, a.dtype) for a in arrs],
        scratch_shapes=[pltpu.SemaphoreType.DMA((3 * n,)), pltpu.SemaphoreType.DMA((3 * n,)), pltpu.SemaphoreType.DMA((n,))],
        name=name,
    )(*arrs)


def _sibling_exchange(g, name):
    def body(g_ref, o_ref, send_sem, recv_sem):
        x, y, c = _place()
        cp = pltpu.make_async_remote_copy(src_ref=g_ref, dst_ref=o_ref, send_sem=send_sem, recv_sem=recv_sem,
                                          device_id=(x, y, 1 - c), device_id_type=MESH)
        cp.start()
        cp.wait()

    return pl.pallas_call(
        body, in_specs=[ANY], out_specs=ANY, out_shape=jax.ShapeDtypeStruct(g.shape, g.dtype),
        scratch_shapes=[pltpu.SemaphoreType.DMA(()), pltpu.SemaphoreType.DMA(())],
        name=name,
    )(g)


def _chip_scatter(s, name):
    def body(s_ref, r_ref, send_sems, recv_sems, loc_sem):
        x, y, c = _place()
        me = 2 * x + y
        loc = pltpu.make_async_copy(s_ref.at[me], r_ref.at[me], loc_sem)
        loc.start()
        for k, (px, py) in enumerate(_other_chips(x, y)):
            pltpu.make_async_remote_copy(src_ref=s_ref.at[2 * px + py], dst_ref=r_ref.at[me], send_sem=send_sems.at[k],
                                         recv_sem=recv_sems.at[k], device_id=(px, py, c), device_id_type=MESH).start()
        for k, (px, py) in enumerate(_other_chips(x, y)):
            cp = pltpu.make_async_remote_copy(src_ref=s_ref.at[2 * px + py], dst_ref=r_ref.at[2 * px + py], send_sem=send_sems.at[k],
                                              recv_sem=recv_sems.at[k], device_id=(px, py, c), device_id_type=MESH)
            cp.wait_recv()
            cp.wait_send()
        loc.wait()

    return pl.pallas_call(
        body, in_specs=[ANY], out_specs=ANY, out_shape=jax.ShapeDtypeStruct(s.shape, s.dtype),
        scratch_shapes=[pltpu.SemaphoreType.DMA((3,)), pltpu.SemaphoreType.DMA((3,)), pltpu.SemaphoreType.DMA(())],
        name=name,
    )(s)


def _all_reduce_small(v, name):
    shape = v.shape

    def body(v_ref, o_ref, slots, send_sems, recv_sems):
        x, y, c = _place()
        me = 4 * x + 2 * y + c
        slots[me] = v_ref[...]
        for r in range(1, N_DEV):
            peer = (x ^ (r >> 2), y ^ ((r >> 1) & 1), c ^ (r & 1))
            pltpu.make_async_remote_copy(src_ref=v_ref, dst_ref=slots.at[me], send_sem=send_sems.at[r - 1], recv_sem=recv_sems.at[r - 1],
                                         device_id=peer, device_id_type=MESH).start()
        for r in range(1, N_DEV):
            peer = (x ^ (r >> 2), y ^ ((r >> 1) & 1), c ^ (r & 1))
            cp = pltpu.make_async_remote_copy(src_ref=v_ref, dst_ref=slots.at[4 * peer[0] + 2 * peer[1] + peer[2]], send_sem=send_sems.at[r - 1],
                                              recv_sem=recv_sems.at[r - 1], device_id=peer, device_id_type=MESH)
            cp.wait_recv()
            cp.wait_send()
        acc = slots[0]
        for d in range(1, N_DEV):
            acc = acc + slots[d]
        o_ref[...] = acc

    vm = pl.BlockSpec(memory_space=pltpu.VMEM)
    return pl.pallas_call(
        body, in_specs=[vm], out_specs=vm, out_shape=jax.ShapeDtypeStruct(shape, F32),
        scratch_shapes=[pltpu.VMEM((N_DEV,) + shape, F32), pltpu.SemaphoreType.DMA((N_DEV - 1,)), pltpu.SemaphoreType.DMA((N_DEV - 1,))],
        name=name,
    )(v)


def _add2(a, b, name):
    n, r, cdim = a.shape
    tr = 384

    def body(a_ref, b_ref, o_ref):
        o_ref[...] = a_ref[...] + b_ref[...]

    blk = pl.BlockSpec((1, tr, cdim), lambda j, i: (j, i, 0))
    return pl.pallas_call(body, grid=(n, r // tr), in_specs=[blk, blk], out_specs=blk, out_shape=jax.ShapeDtypeStruct(a.shape, a.dtype),
                          compiler_params=_params(("parallel", "parallel")), name=name)(a, b)


def _adamw(parts, w, m, v, name):
    npart, r, cdim = parts.shape
    tr = 384 if r % 384 == 0 else r

    def body(p_ref, w_ref, m_ref, v_ref, g_ref, d_ref, nm_ref, nv_ref):
        g = p_ref[0]
        for j in range(1, npart):
            g = g + p_ref[j]
        m_new = ADAM_B1 * m_ref[...] + (1.0 - ADAM_B1) * g
        v_new = ADAM_B2 * v_ref[...] + (1.0 - ADAM_B2) * (g * g)
        m_hat = m_new / (1.0 - ADAM_B1 ** ADAM_STEP)
        v_hat = v_new / (1.0 - ADAM_B2 ** ADAM_STEP)
        g_ref[...] = g
        d_ref[...] = -ADAM_LR * (m_hat / (jnp.sqrt(v_hat) + ADAM_EPS) + ADAM_WD * w_ref[...])
        nm_ref[...] = m_new
        nv_ref[...] = v_new

    blk = pl.BlockSpec((tr, cdim), lambda i: (i, 0))
    shp = jax.ShapeDtypeStruct((r, cdim), F32)
    return pl.pallas_call(
        body, grid=(r // tr,), in_specs=[pl.BlockSpec((npart, tr, cdim), lambda i: (0, i, 0)), blk, blk, blk], out_specs=[blk] * 4,
        out_shape=[shp] * 4, compiler_params=_params(("parallel",)), name=name,
    )(parts, w, m, v)


BIG = ("ev_w_in", "ev_w_uq", "ev_w_ukv", "ev_w_out", "od_w_in", "od_w_out", "meta")
SMALL = ("norm_g", "final_g", "ev_q_norm_g", "ev_kv_norm_g", "od_sinks")
PACK_ROWS = 1920
SMALL_SHAPE = (8, 512)


def _pack_big(arrs):
    rows = [a.reshape(-1, D_MODEL) for a in arrs]
    used = sum(r.shape[0] for r in rows)
    return jnp.pad(jnp.concatenate(rows, axis=0), ((0, PACK_ROWS - used), (0, 0)))


def _unpack_big(p, shapes):
    out, at = [], 0
    for s in shapes:
        nrow = int(np.prod(s)) // D_MODEL
        out.append(p[at:at + nrow].reshape(s))
        at += nrow
    return out


def _pack_small(arrs, extra=None):
    flat = [a.reshape(-1) for a in arrs] + ([] if extra is None else [extra.reshape(-1)])
    used = sum(f.shape[0] for f in flat)
    return jnp.pad(jnp.concatenate(flat), (0, SMALL_SHAPE[0] * SMALL_SHAPE[1] - used)).reshape(SMALL_SHAPE)


def _unpack_small(p, shapes):
    flat, out, at = p.reshape(-1), [], 0
    for s in shapes:
        n = int(np.prod(s))
        out.append(flat[at:at + n].reshape(s))
        at += n
    return out, flat[at]


def _shard_of(full, name, j):
    if name in ("ev_w_out", "od_w_out"):
        n = full.shape[0] // N_CHIPS
        return full[j * n:(j + 1) * n]
    n = full.shape[1] // N_CHIPS
    return full[:, j * n:(j + 1) * n]


def kernel(x, meta, norm_g, final_g, ev_w_in, ev_q_norm_g, ev_kv_norm_g, ev_w_uq, ev_w_ukv, ev_w_out, od_w_in, od_sinks, od_w_out, loss_target, m_meta, m_norm_g, m_final_g, m_ev_w_in, m_ev_q_norm_g, m_ev_kv_norm_g, m_ev_w_uq, m_ev_w_ukv, m_ev_w_out, m_od_w_in, m_od_sinks, m_od_w_out, v_meta, v_norm_g, v_final_g, v_ev_w_in, v_ev_q_norm_g, v_ev_kv_norm_g, v_ev_w_uq, v_ev_w_ukv, v_ev_w_out, v_od_w_in, v_od_sinks, v_od_w_out):
    given = dict(locals())
    big_w = [given[n][0] if given[n].ndim == 3 else given[n] for n in BIG]
    big_shapes = [given[n].shape for n in BIG]

    wpack = _pack_big(big_w[:-1]).astype(BF16)
    wall, meta_all = _all_gather_chips([wpack, meta], "gather_weights")
    per_chip = [_unpack_big(wall[j], [a.shape for a in big_w[:-1]]) for j in range(N_CHIPS)]
    cat = lambda i, axis: jnp.concatenate([per_chip[j][i] for j in range(N_CHIPS)], axis=axis)
    full = dict(ev_w_in=cat(0, 1), ev_w_uq=cat(1, 1), ev_w_ukv=cat(2, 1), ev_w_out=cat(3, 0), od_w_in=cat(4, 1), od_w_out=cat(5, 0))
    meta_full = jnp.concatenate([meta_all[j] for j in range(N_CHIPS)], axis=1)

    loss, grad_x, grads = _local_step(x, meta_full, norm_g, final_g, ev_q_norm_g, ev_kv_norm_g, od_sinks, loss_target,
                                      full["ev_w_in"], full["ev_w_uq"], full["ev_w_ukv"], full["ev_w_out"], full["od_w_in"], full["od_w_out"])

    gpack = jnp.stack([_pack_big([_shard_of(grads[n], n, j) for n in BIG]) for j in range(N_CHIPS)])
    gsum = _add2(gpack, _sibling_exchange(gpack, "grads_to_sibling"), "add_sibling")
    parts = _chip_scatter(gsum, "grads_to_chips")
    big_out = _adamw(parts, _pack_big(big_w), _pack_big([given["m_" + n] for n in BIG]), _pack_big([given["v_" + n] for n in BIG]),
                     "adamw_matrices")
    big_out = [_unpack_big(o, big_shapes) for o in big_out]

    small_shapes = [given[n].shape for n in SMALL]
    ssum = _all_reduce_small(_pack_small([grads[n] for n in SMALL], loss[0, 0]), "reduce_vectors")
    small_out = _adamw(ssum[None], _pack_small([given[n] for n in SMALL]), _pack_small([given["m_" + n] for n in SMALL]),
                       _pack_small([given["v_" + n] for n in SMALL]), "adamw_vectors")
    total_loss = ssum.reshape(-1)[sum(int(np.prod(s)) for s in small_shapes)]
    small_out = [_unpack_small(o, small_shapes)[0] for o in small_out]

    names = ("meta", "norm_g", "final_g", "ev_w_in", "ev_q_norm_g", "ev_kv_norm_g", "ev_w_uq", "ev_w_ukv", "ev_w_out", "od_w_in", "od_sinks",
             "od_w_out")
    outs = [total_loss, grad_x]
    for kind in range(4):
        for n in names:
            outs.append(big_out[kind][BIG.index(n)] if n in BIG else small_out[kind][SMALL.index(n)])
    return tuple(outs)
```

```python
import functools
import math

import numpy as np
import jax
import jax.numpy as jnp
from jax import lax
from jax.experimental import pallas as pl
from jax.experimental.pallas import tpu as pltpu

F32 = jnp.float32
BF16 = jnp.bfloat16

D_MODEL = 1024
BLOCK = 128
N_META = 16
N_PAD = BLOCK - N_META
NORM_EPS = 1e-6
NEG = -1e30
HEAD = 64
SB_HEADS = 8
MLA_HEADS = 8
MLA_Q_LORA = 256
MLA_KV_LORA = 128
MLA_NOPE = 64
MLA_ROPE = 32
ROPE_BASE = 10000.0
SWA_HEADS = 16
SWA_KV_HEADS = 2
SWA_WINDOW = 128
N_CHIPS = 4
N_DEV = 8

ADAM_LR = 0.001
ADAM_B1 = 0.9
ADAM_B2 = 0.999
ADAM_EPS = 1e-08
ADAM_WD = 0.01
ADAM_STEP = 10

VMEM_LIMIT = 48 * 1024 * 1024

EV_G, EV_Q, EV_K, EV_V, EV_CQ, EV_CKV, EV_KR, EV_N = 0, 1024, 1536, 2048, 2560, 2816, 2944, 3072
OD_G, OD_Q, OD_K, OD_V, OD_N = 0, 1024, 2048, 2176, 2304


def _params(sem=None):
    return pltpu.CompilerParams(dimension_semantics=sem, vmem_limit_bytes=VMEM_LIMIT)


def _row_tile(m):
    return 256 if m % 256 == 0 else 128


def _dot(a, b):
    return jnp.dot(a.astype(BF16), b.astype(BF16), preferred_element_type=F32)


def _dot_nt(a, b):
    return lax.dot_general(a.astype(BF16), b.astype(BF16), (((1,), (1,)), ((), ())), preferred_element_type=F32)


def _dot_tn(a, b):
    return lax.dot_general(a.astype(BF16), b.astype(BF16), (((0,), (0,)), ((), ())), preferred_element_type=F32)


def _rms_fwd(h, g, name):
    t, d = h.shape
    tm = _row_tile(t)

    def body(h_ref, g_ref, o_ref):
        x = h_ref[...]
        r = lax.rsqrt(jnp.mean(x * x, axis=-1, keepdims=True) + NORM_EPS)
        o_ref[...] = ((x * r) * g_ref[...]).astype(o_ref.dtype)

    return pl.pallas_call(
        body, grid=(t // tm,),
        in_specs=[pl.BlockSpec((tm, d), lambda i: (i, 0)), pl.BlockSpec((1, d), lambda i: (0, 0))],
        out_specs=pl.BlockSpec((tm, d), lambda i: (i, 0)),
        out_shape=jax.ShapeDtypeStruct((t, d), BF16), compiler_params=_params(("parallel",)), name=name,
    )(h, g)


def _rms_bwd(h, g, dy, dres, name):
    t, d = h.shape
    tm = _row_tile(t)

    def body(h_ref, g_ref, dy_ref, dres_ref, dh_ref, dg_ref):
        @pl.when(pl.program_id(0) == 0)
        def _():
            dg_ref[...] = jnp.zeros_like(dg_ref)

        x = h_ref[...]
        r = lax.rsqrt(jnp.mean(x * x, axis=-1, keepdims=True) + NORM_EPS)
        xr = x * r
        dy_ = dy_ref[...]
        u = dy_ * g_ref[...]
        dh_ref[...] = dres_ref[...] + r * (u - xr * jnp.mean(u * xr, axis=-1, keepdims=True))
        dg_ref[...] += jnp.sum(dy_ * xr, axis=0, keepdims=True)

    row = pl.BlockSpec((tm, d), lambda i: (i, 0))
    vec = pl.BlockSpec((1, d), lambda i: (0, 0))
    return pl.pallas_call(
        body, grid=(t // tm,), in_specs=[row, vec, row, row], out_specs=[row, vec],
        out_shape=[jax.ShapeDtypeStruct((t, d), F32), jax.ShapeDtypeStruct((1, d), F32)],
        compiler_params=_params(("arbitrary",)), name=name,
    )(h, g, dy, dres)


def _col_tile(n):
    for c in (1024, 768, 640, 512, 384, 256, 128):
        if n % c == 0:
            return c
    return n


def _mm(a, w, name, res=None, out_dtype=F32, a_cols=None):
    m = a.shape[0]
    k, n = w.shape
    a_blk = 0 if a_cols is None else a_cols[0] // k
    assert a_cols is None or (a_cols[1] == k and a_cols[0] % k == 0)
    tm, tn = _row_tile(m), _col_tile(n)

    def body(*refs):
        if res is None:
            a_ref, w_ref, o_ref = refs
            acc = _dot(a_ref[...], w_ref[...])
        else:
            a_ref, w_ref, r_ref, o_ref = refs
            acc = r_ref[...] + _dot(a_ref[...], w_ref[...])
        o_ref[...] = acc.astype(o_ref.dtype)

    in_specs = [pl.BlockSpec((tm, k), lambda j, i: (i, a_blk)), pl.BlockSpec((k, tn), lambda j, i: (0, j))]
    args = [a, w]
    if res is not None:
        in_specs.append(pl.BlockSpec((tm, tn), lambda j, i: (i, j)))
        args.append(res)
    return pl.pallas_call(
        body, grid=(n // tn, m // tm), in_specs=in_specs, out_specs=pl.BlockSpec((tm, tn), lambda j, i: (i, j)),
        out_shape=jax.ShapeDtypeStruct((m, n), out_dtype), compiler_params=_params(("parallel", "parallel")), name=name,
    )(*args)


def _mm_nt(a, w, name):
    m, n = a.shape
    k = w.shape[0]
    tm, tk = _row_tile(m), _col_tile(k)

    def body(a_ref, w_ref, o_ref):
        o_ref[...] = _dot_nt(a_ref[...], w_ref[...])

    return pl.pallas_call(
        body, grid=(k // tk, m // tm),
        in_specs=[pl.BlockSpec((tm, n), lambda j, i: (i, 0)), pl.BlockSpec((tk, n), lambda j, i: (j, 0))],
        out_specs=pl.BlockSpec((tm, tk), lambda j, i: (i, j)),
        out_shape=jax.ShapeDtypeStruct((m, k), F32), compiler_params=_params(("parallel", "parallel")), name=name,
    )(a, w)


def _mm_tn(x, dy, name):
    m, k = x.shape
    n = dy.shape[1]
    tm, tn = _row_tile(m), _col_tile(n)

    def body(x_ref, dy_ref, o_ref):
        @pl.when(pl.program_id(1) == 0)
        def _():
            o_ref[...] = jnp.zeros_like(o_ref)

        o_ref[...] += _dot_tn(x_ref[...], dy_ref[...])

    return pl.pallas_call(
        body, grid=(n // tn, m // tm),
        in_specs=[pl.BlockSpec((tm, k), lambda j, i: (i, 0)), pl.BlockSpec((tm, tn), lambda j, i: (i, j))],
        out_specs=pl.BlockSpec((k, tn), lambda j, i: (0, j)),
        out_shape=jax.ShapeDtypeStruct((k, n), F32), compiler_params=_params(("parallel", "arbitrary")), name=name,
    )(x, dy)


def _silu_parts(g):
    s = 1.0 / (1.0 + jnp.exp(-g))
    return g * s, s * (1.0 + g * (1.0 - s))


def _gate_fwd(o_parts, proj, name):
    t = proj.shape[0]
    tm = _row_tile(t)
    w = D_MODEL // len(o_parts)

    def body(*refs):
        g_ref, o_ref = refs[-2], refs[-1]
        for p, r in enumerate(refs[:-2]):
            sil, _ = _silu_parts(g_ref[:, p * w:(p + 1) * w])
            o_ref[:, p * w:(p + 1) * w] = (r[...].astype(F32) * sil).astype(o_ref.dtype)

    return pl.pallas_call(
        body, grid=(t // tm,),
        in_specs=[pl.BlockSpec((tm, w), lambda i: (i, 0)) for _ in o_parts] + [pl.BlockSpec((tm, D_MODEL), lambda i: (i, 0))],
        out_specs=pl.BlockSpec((tm, D_MODEL), lambda i: (i, 0)),
        out_shape=jax.ShapeDtypeStruct((t, D_MODEL), BF16), compiler_params=_params(("parallel",)), name=name,
    )(*o_parts, proj)


def _gate_bwd(dao, o_parts, proj, name):
    t = proj.shape[0]
    tm = _row_tile(t)
    np_ = len(o_parts)
    w = D_MODEL // np_

    def body(*refs):
        dao_ref, g_ref = refs[0], refs[1 + np_]
        do_refs, dg_ref = refs[2 + np_:2 + 2 * np_], refs[-1]
        for p in range(np_):
            sl = slice(p * w, (p + 1) * w)
            sil, dsil = _silu_parts(g_ref[:, sl])
            da = dao_ref[:, sl]
            do_refs[p][...] = da * sil
            dg_ref[:, sl] = (da * refs[1 + p][...].astype(F32) * dsil).astype(dg_ref.dtype)

    full = pl.BlockSpec((tm, D_MODEL), lambda i: (i, 0))
    part = pl.BlockSpec((tm, w), lambda i: (i, 0))
    outs = pl.pallas_call(
        body, grid=(t // tm,), in_specs=[full] + [part] * np_ + [full], out_specs=[part] * np_ + [full],
        out_shape=[jax.ShapeDtypeStruct((t, w), F32)] * np_ + [jax.ShapeDtypeStruct((t, D_MODEL), BF16)],
        compiler_params=_params(("parallel",)), name=name,
    )(dao, *o_parts, proj)
    return outs[:np_], outs[np_]


def _loss_head(h2, gf, target, b, lp):
    d = h2.shape[1]
    nb = lp // BLOCK
    h3 = h2.reshape(b, lp, d)

    def body(h_ref, g_ref, t_ref, dh_ref, dg_ref, loss_ref):
        first = (pl.program_id(0) == 0) & (pl.program_id(1) == 0)

        @pl.when(first)
        def _():
            dg_ref[...] = jnp.zeros_like(dg_ref)
            loss_ref[...] = jnp.zeros_like(loss_ref)

        @pl.when(pl.program_id(1) == 0)
        def _():
            dh_ref[...] = jnp.zeros_like(dh_ref)

        @pl.when(pl.program_id(1) > 0)
        def _():
            x = h_ref[0]
            r = lax.rsqrt(jnp.mean(x * x, axis=-1, keepdims=True) + NORM_EPS)
            xr = x * r
            g = g_ref[...]
            diff = xr * g - t_ref[0]
            loss_ref[...] += 0.5 * jnp.sum(jnp.mean(diff * diff, axis=-1, keepdims=True))
            dy = diff * (1.0 / d)
            u = dy * g
            dh_ref[0] = r * (u - xr * jnp.mean(u * xr, axis=-1, keepdims=True))
            dg_ref[...] += jnp.sum(dy * xr, axis=0, keepdims=True)

    blk = pl.BlockSpec((1, BLOCK, d), lambda bi, n: (bi, n, 0))
    dh, dg, loss = pl.pallas_call(
        body, grid=(b, nb),
        in_specs=[blk, pl.BlockSpec((1, d), lambda bi, n: (0, 0)),
                  pl.BlockSpec((1, BLOCK, d), lambda bi, n: (bi, jnp.maximum(n - 1, 0), 0))],
        out_specs=[blk, pl.BlockSpec((1, d), lambda bi, n: (0, 0)), pl.BlockSpec((8, 128), lambda bi, n: (0, 0))],
        out_shape=[jax.ShapeDtypeStruct((b, lp, d), F32), jax.ShapeDtypeStruct((1, d), F32), jax.ShapeDtypeStruct((8, 128), F32)],
        compiler_params=_params(("arbitrary", "arbitrary")), name="loss_head",
    )(h3, gf, target)
    return dh.reshape(b * lp, d), dg, loss


def _iota2(shape, dim):
    return lax.broadcasted_iota(jnp.int32, shape, dim)


def _split_dot(x, tri):
    hi = x.astype(BF16)
    lo = (x - hi.astype(F32)).astype(BF16)
    return jnp.dot(hi, tri, preferred_element_type=F32) + jnp.dot(lo, tri, preferred_element_type=F32)


KEYS = 256


def _lo_lanes():
    return _iota2((1, BLOCK), 1) < HEAD


def _halves(x, lo):
    zero = jnp.zeros_like(x)
    return jnp.where(lo, x, zero), jnp.where(lo, zero, x)


def _stack_halves(x, lo):
    a, b = _halves(x, lo)
    return jnp.concatenate([a, b], axis=0)


def _pair(a, b, lo):
    return jnp.where(lo, a, b)


def _key_chunk(c, lp, t_idx, strict):
    first = c * KEYS
    s0 = pl.multiple_of(jnp.minimum(first, lp - KEYS), BLOCK)
    s_idx = s0 + _iota2((BLOCK, KEYS), 1)
    seen = (s_idx < t_idx) if strict else (s_idx <= t_idx)
    return s0, seen & (s_idx >= jnp.maximum(first, N_PAD))


def _n_chunks(i):
    return (i + 2) // 2


def _sb_scores(q_h, k, valid, after):
    z = _dot_nt(q_h, k) * (HEAD ** -0.5)
    sp = jnp.log(1.0 + jnp.exp(-jnp.abs(z)))
    lb = jnp.minimum(z, 0.0) - sp
    l1m_all = -jnp.maximum(z, 0.0) - sp
    l1m = jnp.where(valid, l1m_all, 0.0)
    return lb, l1m_all, l1m, _split_dot(l1m, after)


def _sb_fwd(proj3, name):
    b, lp, _ = proj3.shape
    nb = lp // BLOCK
    npair = SB_HEADS // 2

    def body(q_ref, k_ref, v_ref, o_ref, tot_ref):
        lo = _lo_lanes()
        after = (_iota2((KEYS, KEYS), 0) > _iota2((KEYS, KEYS), 1)).astype(BF16)

        def qblock(i, _):
            r0 = pl.multiple_of(i * BLOCK, BLOCK)
            qs = _halves(q_ref[0, pl.ds(r0, BLOCK), :].astype(BF16), lo)
            t_idx = r0 + _iota2((BLOCK, KEYS), 0)
            n = _n_chunks(i)

            def kchunk(cc, carry):
                cs, acc = carry[:2], carry[2]
                s0, valid = _key_chunk(n - 1 - cc, lp, t_idx, True)
                k = k_ref[0, pl.ds(s0, KEYS), :].astype(BF16)
                a_s, new = [], []
                for h in range(2):
                    lb, _, l1m, suf = _sb_scores(qs[h], k, valid, after)
                    a_s.append(jnp.where(valid, jnp.exp(lb + suf + cs[h]), 0.0).astype(BF16))
                    new.append(cs[h] + jnp.sum(l1m, axis=1, keepdims=True))
                v_bd = _stack_halves(v_ref[0, pl.ds(s0, KEYS), :].astype(BF16), lo)
                return (*new, acc + jnp.dot(jnp.concatenate(a_s, axis=1), v_bd, preferred_element_type=F32))

            zero = jnp.zeros((BLOCK, 1), F32)
            c_a, c_b, acc = lax.fori_loop(0, n, kchunk, (zero, zero, jnp.zeros((BLOCK, BLOCK), F32)))
            o_ref[0, pl.ds(r0, BLOCK), :] = acc
            tot_ref[0, pl.ds(r0, BLOCK), :] = jnp.broadcast_to(_pair(c_a, c_b, lo), (BLOCK, BLOCK))
            return 0

        lax.fori_loop(0, nb, qblock, 0)

    def col(first):
        return pl.BlockSpec((1, lp, 2 * HEAD), lambda bi, hp: (bi, 0, first // (2 * HEAD) + hp))

    shp = jax.ShapeDtypeStruct((b, lp, SB_HEADS * HEAD), F32)
    return pl.pallas_call(
        body, grid=(b, npair), in_specs=[col(EV_Q), col(EV_K), col(EV_V)], out_specs=[col(0), col(0)], out_shape=[shp, shp],
        compiler_params=_params(("parallel", "parallel")), name=name,
    )(proj3, proj3, proj3)


def _sb_bwd(proj3, tot, do, name):
    b, lp, _ = proj3.shape
    nb = lp // BLOCK
    npair = SB_HEADS // 2

    def body(q_ref, k_ref, v_ref, tot_ref, do_ref, dq_ref, dk_ref, dv_ref):
        lo = _lo_lanes()
        after = (_iota2((KEYS, KEYS), 0) > _iota2((KEYS, KEYS), 1)).astype(BF16)
        before = (_iota2((KEYS, KEYS), 0) < _iota2((KEYS, KEYS), 1)).astype(BF16)
        dk_ref[...] = jnp.zeros_like(dk_ref)
        dv_ref[...] = jnp.zeros_like(dv_ref)

        def qblock(i, _):
            r0 = pl.multiple_of(i * BLOCK, BLOCK)
            rows = pl.ds(r0, BLOCK)
            qs = _halves(q_ref[0, rows, :].astype(BF16), lo)
            dos = _halves(do_ref[0, rows, :].astype(BF16), lo)
            tot_i = tot_ref[0, rows, :]
            tots = (tot_i[:, 0:1], tot_i[:, HEAD:HEAD + 1])
            q_st, do_st = jnp.concatenate(qs, axis=0), jnp.concatenate(dos, axis=0)
            t_idx = r0 + _iota2((BLOCK, KEYS), 0)

            def kchunk(c, carry):
                s0, valid = _key_chunk(c, lp, t_idx, True)
                keys = pl.ds(s0, KEYS)
                k = k_ref[0, keys, :].astype(BF16)
                v = v_ref[0, keys, :].astype(BF16)
                a_s, dzs, new = [], [], []
                for h in range(2):
                    left, pre = carry[2 * h], carry[2 * h + 1]
                    lb, l1m_all, l1m, suf = _sb_scores(qs[h], k, valid, after)
                    here = jnp.sum(l1m, axis=1, keepdims=True)
                    a = jnp.where(valid, jnp.exp(lb + suf + (tots[h] - left - here)), 0.0)
                    w = a * _dot_nt(dos[h], v)
                    dz = jnp.where(valid, w * jnp.exp(l1m_all) - (pre + _split_dot(w, before)) * jnp.exp(lb), 0.0) * (HEAD ** -0.5)
                    new += [left + here, pre + jnp.sum(w, axis=1, keepdims=True)]
                    a_s.append(a.astype(BF16))
                    dzs.append(dz.astype(BF16))
                dk_ref[0, keys, :] += _dot_tn(jnp.concatenate(dzs, axis=0), q_st)
                dv_ref[0, keys, :] += _dot_tn(jnp.concatenate(a_s, axis=0), do_st)
                dq = carry[4] + jnp.dot(jnp.concatenate(dzs, axis=1), _stack_halves(k, lo), preferred_element_type=F32)
                return (*new, dq)

            zero = jnp.zeros((BLOCK, 1), F32)
            out = lax.fori_loop(0, _n_chunks(i), kchunk, (zero, zero, zero, zero, jnp.zeros((BLOCK, BLOCK), F32)))
            dq_ref[0, rows, :] = out[4]
            return 0

        lax.fori_loop(0, nb, qblock, 0)

    def col(first):
        return pl.BlockSpec((1, lp, 2 * HEAD), lambda bi, hp: (bi, 0, first // (2 * HEAD) + hp))

    shp = jax.ShapeDtypeStruct((b, lp, SB_HEADS * HEAD), F32)
    return pl.pallas_call(
        body, grid=(b, npair), in_specs=[col(EV_Q), col(EV_K), col(EV_V), col(0), col(0)], out_specs=[col(0)] * 3, out_shape=[shp] * 3,
        compiler_params=_params(("parallel", "parallel")), name=name,
    )(proj3, proj3, proj3, tot, do)


def _rope_tables(lp):
    half = MLA_ROPE // 2
    pos = (np.arange(lp) - N_PAD).astype(np.float32)
    inv = jnp.asarray(ROPE_BASE, F32) ** (-jnp.arange(half, dtype=F32) / half)
    ang = jnp.asarray(pos)[:, None] * inv[None, :]
    cos, sin = jnp.cos(ang), jnp.sin(ang)
    zeros = lambda n: jnp.zeros((lp, n), F32)
    c = jnp.concatenate([jnp.ones((lp, MLA_NOPE), F32), cos, cos, zeros(32)], axis=1)
    s1 = jnp.concatenate([zeros(MLA_NOPE), -sin, zeros(half), zeros(32)], axis=1)
    s2 = jnp.concatenate([zeros(MLA_NOPE), zeros(half), sin, zeros(32)], axis=1)
    return c, s1, s2


def _rope(x, c, s1, s2):
    half = MLA_ROPE // 2
    return x * c + pltpu.roll(x, BLOCK - half, 1) * s1 + pltpu.roll(x, half, 1) * s2


def _rope_t(dy, c, s1, s2):
    half = MLA_ROPE // 2
    return dy * c + pltpu.roll(dy * s1, half, 1) + pltpu.roll(dy * s2, BLOCK - half, 1)


def _rms_rows(x, g):
    r = lax.rsqrt(jnp.mean(x * x, axis=-1, keepdims=True) + NORM_EPS)
    return x * r, r


def _mla_prep_fwd(proj3, gq, gkv, wq, wk, wv, tabs, name):
    b, lp, _ = proj3.shape
    nb = lp // BLOCK
    hw = MLA_HEADS * BLOCK

    def body(cq_ref, ckv_ref, kr_ref, gq_ref, gkv_ref, wq_ref, wk_ref, wv_ref, c_ref, s1_ref, s2_ref, qf_ref, kf_ref, v_ref):
        c, s1, s2 = c_ref[...], s1_ref[...], s2_ref[...]
        xq, _ = _rms_rows(cq_ref[0], None)
        qh = _dot(xq * gq_ref[...], wq_ref[...])
        xk, _ = _rms_rows(ckv_ref[0], None)
        ckv_n = xk * gkv_ref[...]
        kv = _dot(ckv_n, wk_ref[...])
        v_ref[0] = _dot(ckv_n, wv_ref[...]).astype(v_ref.dtype)
        kr = _rope(kr_ref[0], c, s1, s2)
        for h in range(MLA_HEADS):
            ls = slice(h * BLOCK, (h + 1) * BLOCK)
            qf_ref[0, :, ls] = _rope(qh[:, ls], c, s1, s2).astype(qf_ref.dtype)
            kf_ref[0, :, ls] = (kv[:, ls] + kr).astype(kf_ref.dtype)

    def col(first, width):
        return pl.BlockSpec((1, BLOCK, width), lambda bi, n: (bi, n, first // width))

    def whole(a):
        return pl.BlockSpec(a.shape, lambda bi, n: (0,) * a.ndim)

    tab = pl.BlockSpec((BLOCK, BLOCK), lambda bi, n: (n, 0))
    return pl.pallas_call(
        body, grid=(b, nb),
        in_specs=[col(EV_CQ, MLA_Q_LORA), col(EV_CKV, MLA_KV_LORA), col(EV_KR, BLOCK), whole(gq), whole(gkv), whole(wq), whole(wk),
                  whole(wv), tab, tab, tab],
        out_specs=[col(0, hw), col(0, hw), col(0, MLA_HEADS * HEAD)],
        out_shape=[jax.ShapeDtypeStruct((b, lp, hw), BF16), jax.ShapeDtypeStruct((b, lp, hw), BF16),
                   jax.ShapeDtypeStruct((b, lp, MLA_HEADS * HEAD), BF16)],
        compiler_params=_params(("parallel", "parallel")), name=name,
    )(proj3, proj3, proj3, gq, gkv, wq, wk, wv, *tabs)


def _mla_prep_bwd(proj3, gq, gkv, wq, wk, wv, tabs, dqf, dkf, dv, name):
    b, lp, _ = proj3.shape
    nb = lp // BLOCK
    hw = MLA_HEADS * BLOCK

    def body(cq_ref, ckv_ref, gq_ref, gkv_ref, wq_ref, wk_ref, wv_ref, c_ref, s1_ref, s2_ref, dqf_ref, dkf_ref, dv_ref,
             dcq_ref, dckv_ref, dkr_ref, dwq_ref, dwk_ref, dwv_ref, dgq_ref, dgkv_ref, dqh):
        @pl.when((pl.program_id(0) == 0) & (pl.program_id(1) == 0))
        def _():
            for r in (dwq_ref, dwk_ref, dwv_ref, dgq_ref, dgkv_ref):
                r[...] = jnp.zeros_like(r)

        c, s1, s2 = c_ref[...], s1_ref[...], s2_ref[...]
        dkr = jnp.zeros((BLOCK, BLOCK), F32)
        for h in range(MLA_HEADS):
            ls = slice(h * BLOCK, (h + 1) * BLOCK)
            dqh[:, ls] = _rope_t(dqf_ref[0, :, ls].astype(F32), c, s1, s2).astype(dqh.dtype)
            dkr = dkr + dkf_ref[0, :, ls].astype(F32)
        dkr_ref[0] = _rope_t(dkr, c, s1, s2).astype(dkr_ref.dtype)

        def norm_bwd(x, g, dy, dg_ref):
            xr, r = _rms_rows(x, None)
            u = dy * g
            dg_ref[...] += jnp.sum(dy * xr, axis=0, keepdims=True)
            return r * (u - xr * jnp.mean(u * xr, axis=-1, keepdims=True))

        xq, _ = _rms_rows(cq_ref[0], None)
        cq_n = xq * gq_ref[...]
        dwq_ref[...] += _dot_tn(cq_n, dqh[...])
        dcq_ref[0] = norm_bwd(cq_ref[0], gq_ref[...], _dot_nt(dqh[...], wq_ref[...]), dgq_ref).astype(dcq_ref.dtype)
        xk, _ = _rms_rows(ckv_ref[0], None)
        ckv_n = xk * gkv_ref[...]
        dkf_, dv_ = dkf_ref[0], dv_ref[0]
        dwk_ref[...] += _dot_tn(ckv_n, dkf_)
        dwv_ref[...] += _dot_tn(ckv_n, dv_)
        dckv_n = _dot_nt(dkf_, wk_ref[...]) + _dot_nt(dv_, wv_ref[...])
        dckv_ref[0] = norm_bwd(ckv_ref[0], gkv_ref[...], dckv_n, dgkv_ref).astype(dckv_ref.dtype)

    def col(first, width):
        return pl.BlockSpec((1, BLOCK, width), lambda bi, n: (bi, n, first // width))

    def whole(a):
        return pl.BlockSpec(a.shape, lambda bi, n: (0,) * len(a.shape))

    tab = pl.BlockSpec((BLOCK, BLOCK), lambda bi, n: (n, 0))
    acc_shapes = [jax.ShapeDtypeStruct(a.shape, F32) for a in (wq, wk, wv, gq, gkv)]
    return pl.pallas_call(
        body, grid=(b, nb),
        in_specs=[col(EV_CQ, MLA_Q_LORA), col(EV_CKV, MLA_KV_LORA), whole(gq), whole(gkv), whole(wq), whole(wk), whole(wv), tab, tab, tab,
                  col(0, hw), col(0, hw), col(0, MLA_HEADS * HEAD)],
        out_specs=[col(0, MLA_Q_LORA), col(0, MLA_KV_LORA), col(0, BLOCK)] + [whole(a) for a in acc_shapes],
        out_shape=[jax.ShapeDtypeStruct((b, lp, MLA_Q_LORA), BF16), jax.ShapeDtypeStruct((b, lp, MLA_KV_LORA), BF16),
                   jax.ShapeDtypeStruct((b, lp, BLOCK), BF16)] + acc_shapes,
        scratch_shapes=[pltpu.VMEM((BLOCK, hw), BF16)],
        compiler_params=_params(("arbitrary", "arbitrary")), name=name,
    )(proj3, proj3, gq, gkv, wq, wk, wv, *tabs, dqf, dkf, dv)


def _mla_fwd(qf, kf, v, name):
    b, lp, _ = qf.shape
    nb = lp // BLOCK
    npair = MLA_HEADS // 2
    scale = (MLA_NOPE + MLA_ROPE) ** -0.5

    def body(q_ref, k_ref, v_ref, o_ref, lse_ref):
        lo = _lo_lanes()

        def qblock(i, _):
            r0 = pl.multiple_of(i * BLOCK, BLOCK)
            qs = [q_ref[0, pl.ds(r0, BLOCK), h * BLOCK:(h + 1) * BLOCK] for h in range(2)]
            t_idx = r0 + _iota2((BLOCK, KEYS), 0)

            def kchunk(c, carry):
                stats, acc = carry[:4], carry[4]
                s0, valid = _key_chunk(c, lp, t_idx, False)
                ps, new, alphas = [], [], []
                for h in range(2):
                    m, l = stats[2 * h], stats[2 * h + 1]
                    s = jnp.where(valid, _dot_nt(qs[h], k_ref[0, pl.ds(s0, KEYS), h * BLOCK:(h + 1) * BLOCK]) * scale, NEG)
                    m_new = jnp.maximum(m, jnp.max(s, axis=1, keepdims=True))
                    p = jnp.where(valid, jnp.exp(s - m_new), 0.0)
                    alpha = jnp.exp(m - m_new)
                    new += [m_new, alpha * l + jnp.sum(p, axis=1, keepdims=True)]
                    alphas.append(alpha)
                    ps.append(p.astype(BF16))
                pv = jnp.dot(jnp.concatenate(ps, axis=1), _stack_halves(v_ref[0, pl.ds(s0, KEYS), :], lo), preferred_element_type=F32)
                return (*new, _pair(alphas[0], alphas[1], lo) * acc + pv)

            neg, zero = jnp.full((BLOCK, 1), NEG, F32), jnp.zeros((BLOCK, 1), F32)
            m_a, l_a, m_b, l_b, acc = lax.fori_loop(0, _n_chunks(i), kchunk, (neg, zero, neg, zero, jnp.zeros((BLOCK, BLOCK), F32)))
            l = _pair(l_a, l_b, lo)
            seen = l > 0.0
            safe = jnp.where(seen, l, 1.0)
            o_ref[0, pl.ds(r0, BLOCK), :] = acc / safe
            lse_ref[0, pl.ds(r0, BLOCK), :] = jnp.where(seen, _pair(m_a, m_b, lo) + jnp.log(safe), 0.0)
            return 0

        lax.fori_loop(0, nb, qblock, 0)

    wide = pl.BlockSpec((1, lp, 2 * BLOCK), lambda bi, hp: (bi, 0, hp))
    thin = pl.BlockSpec((1, lp, 2 * HEAD), lambda bi, hp: (bi, 0, hp))
    shp = jax.ShapeDtypeStruct((b, lp, MLA_HEADS * HEAD), F32)
    return pl.pallas_call(
        body, grid=(b, npair), in_specs=[wide, wide, thin], out_specs=[thin, thin], out_shape=[shp, shp],
        compiler_params=_params(("parallel", "parallel")), name=name,
    )(qf, kf, v)


def _mla_bwd(qf, kf, v, o, lse, do, name):
    b, lp, _ = qf.shape
    nb = lp // BLOCK
    npair = MLA_HEADS // 2
    scale = (MLA_NOPE + MLA_ROPE) ** -0.5

    def body(q_ref, k_ref, v_ref, o_ref, lse_ref, do_ref, dq_ref, dk_ref, dv_ref):
        lo = _lo_lanes()
        dk_ref[...] = jnp.zeros_like(dk_ref)
        dv_ref[...] = jnp.zeros_like(dv_ref)

        def qblock(i, _):
            r0 = pl.multiple_of(i * BLOCK, BLOCK)
            rows = pl.ds(r0, BLOCK)
            qs = [q_ref[0, rows, h * BLOCK:(h + 1) * BLOCK] for h in range(2)]
            do_i = do_ref[0, rows, :]
            dos = _halves(do_i.astype(BF16), lo)
            both = do_i * o_ref[0, rows, :]
            d_a = jnp.sum(jnp.where(lo, both, 0.0), axis=1, keepdims=True)
            dsum = (d_a, jnp.sum(both, axis=1, keepdims=True) - d_a)
            lse_i = lse_ref[0, rows, :]
            lses = (lse_i[:, 0:1], lse_i[:, HEAD:HEAD + 1])
            t_idx = r0 + _iota2((BLOCK, KEYS), 0)

            def kchunk(c, dqs):
                s0, valid = _key_chunk(c, lp, t_idx, False)
                keys = pl.ds(s0, KEYS)
                v_c = v_ref[0, keys, :]
                ps, out = [], []
                for h in range(2):
                    lanes = slice(h * BLOCK, (h + 1) * BLOCK)
                    k = k_ref[0, keys, lanes]
                    s = jnp.where(valid, _dot_nt(qs[h], k) * scale, NEG)
                    p = jnp.where(valid, jnp.exp(s - lses[h]), 0.0)
                    ds = p * (_dot_nt(dos[h], v_c) - dsum[h]) * scale
                    dk_ref[0, keys, lanes] += _dot_tn(ds, qs[h])
                    out.append(dqs[h] + _dot(ds, k))
                    ps.append(p.astype(BF16))
                dv_ref[0, keys, :] += _dot_tn(jnp.concatenate(ps, axis=0), jnp.concatenate(dos, axis=0))
                return tuple(out)

            zero = jnp.zeros((BLOCK, BLOCK), F32)
            dq_a, dq_b = lax.fori_loop(0, _n_chunks(i), kchunk, (zero, zero))
            dq_ref[0, rows, 0:BLOCK] = dq_a
            dq_ref[0, rows, BLOCK:2 * BLOCK] = dq_b
            return 0

        lax.fori_loop(0, nb, qblock, 0)

    wide = pl.BlockSpec((1, lp, 2 * BLOCK), lambda bi, hp: (bi, 0, hp))
    thin = pl.BlockSpec((1, lp, 2 * HEAD), lambda bi, hp: (bi, 0, hp))
    return pl.pallas_call(
        body, grid=(b, npair), in_specs=[wide, wide, thin, thin, thin, thin], out_specs=[wide, wide, thin],
        out_shape=[jax.ShapeDtypeStruct(qf.shape, F32), jax.ShapeDtypeStruct(qf.shape, F32), jax.ShapeDtypeStruct(v.shape, F32)],
        compiler_params=_params(("parallel", "parallel")), name=name,
    )(qf, kf, v, o, lse, do)


def _swa_keys(k_ref, v_ref, n, kv, lo):
    prev = jnp.maximum(n - 1, 0)
    rows = lambda blk: pl.ds(pl.multiple_of(blk * BLOCK, BLOCK), BLOCK)
    mine = (_iota2((1, BLOCK), 1) >= HEAD).astype(jnp.int32) == kv

    def both_halves(ref):
        x = jnp.concatenate([ref[0, rows(prev), :], ref[0, rows(n), :], ref[0, 0:BLOCK, :]], axis=0)
        return jnp.where(mine, x, pltpu.roll(x, HEAD, 1)).astype(BF16)

    col = _iota2((BLOCK, 3 * BLOCK), 1)
    loc = col % BLOCK
    s_idx = jnp.where(col < BLOCK, (n - 1) * BLOCK + loc, jnp.where(col < 2 * BLOCK, n * BLOCK + loc, loc))
    dist = n * BLOCK + _iota2((BLOCK, 3 * BLOCK), 0) - s_idx
    band = (col < 2 * BLOCK) & (dist >= 0) & (dist < SWA_WINDOW) & (s_idx >= BLOCK)
    meta = (col >= 2 * BLOCK) & (s_idx >= N_PAD) & (dist >= 0)
    return both_halves(k_ref), both_halves(v_ref), band | meta, dist.astype(F32), prev


def _swa_probs(q_h, kdup, valid, dist, head, sink_ref):
    slope = jnp.exp(jnp.full((1, 1), -8.0 * math.log(2.0) / SWA_HEADS, F32) * (head + 1).astype(F32))
    s = jnp.where(valid, _dot_nt(q_h, kdup) * (HEAD ** -0.5) - slope * dist, NEG)
    sink = sink_ref[pl.ds(head, 1), 0:1]
    m = jnp.maximum(jnp.max(s, axis=1, keepdims=True), sink)
    e = jnp.where(valid, jnp.exp(s - m), 0.0)
    es = jnp.exp(sink - m)
    inv = 1.0 / (jnp.sum(e, axis=1, keepdims=True) + es)
    return e * inv, es * inv


SWA_PAIRS = SWA_HEADS // SWA_KV_HEADS // 2
SWA_GROUP = SWA_PAIRS * 2 * HEAD


def _swa_specs(b, lp):
    nb = lp // BLOCK
    qcol = lambda first: pl.BlockSpec((1, BLOCK, SWA_GROUP), lambda bi, kv, n: (bi, n, first // SWA_GROUP + kv))
    kcol = lambda first: pl.BlockSpec((1, lp, BLOCK), lambda bi, kv, n: (bi, 0, first // BLOCK))
    sink = pl.BlockSpec((SWA_HEADS, BLOCK), lambda bi, kv, n: (0, 0))
    return (b, SWA_KV_HEADS, nb), qcol, kcol, sink


def _swa_fwd(proj3, sinks, name):
    b, lp, _ = proj3.shape
    grid, qcol, kcol, sink = _swa_specs(b, lp)

    def body(q_ref, k_ref, v_ref, sink_ref, o_ref):
        kv, n = pl.program_id(1), pl.program_id(2)
        lo = _lo_lanes()
        kdup, vdup, valid, dist, _ = _swa_keys(k_ref, v_ref, n, kv, lo)
        v_bd = _stack_halves(vdup, lo)
        for p in range(SWA_PAIRS):
            lanes = slice(p * BLOCK, (p + 1) * BLOCK)
            qs = _halves(q_ref[0, :, lanes].astype(BF16), lo)
            probs = [_swa_probs(qs[hh], kdup, valid, dist, (kv * SWA_PAIRS + p) * 2 + hh, sink_ref)[0].astype(BF16) for hh in range(2)]
            o_ref[0, :, lanes] = jnp.dot(jnp.concatenate(probs, axis=1), v_bd, preferred_element_type=F32)

    return pl.pallas_call(
        body, grid=grid, in_specs=[qcol(OD_Q), kcol(OD_K), kcol(OD_V), sink], out_specs=qcol(0),
        out_shape=jax.ShapeDtypeStruct((b, lp, SWA_HEADS * HEAD), F32),
        compiler_params=_params(("parallel", "parallel", "parallel")), name=name,
    )(proj3, proj3, proj3, sinks)


def _swa_bwd(proj3, sinks, do, name):
    b, lp, _ = proj3.shape
    nb = lp // BLOCK
    grid, qcol, kcol, sink = _swa_specs(b, lp)

    def body(q_ref, k_ref, v_ref, sink_ref, do_ref, dq_ref, dk_ref, dv_ref, dsink_ref):
        kv, n = pl.program_id(1), pl.program_id(2)

        @pl.when((n == 0) & (pl.program_id(0) == 0) & (kv == 0))
        def _():
            dsink_ref[...] = jnp.zeros_like(dsink_ref)

        @pl.when(n == 0)
        def _():
            dk_ref[...] = jnp.zeros_like(dk_ref)
            dv_ref[...] = jnp.zeros_like(dv_ref)

        lo = _lo_lanes()
        kdup, vdup, valid, dist, prev = _swa_keys(k_ref, v_ref, n, kv, lo)
        k_bd = _stack_halves(kdup, lo)
        dkc = jnp.zeros((3 * BLOCK, BLOCK), F32)
        dvc = jnp.zeros((3 * BLOCK, BLOCK), F32)
        for p in range(SWA_PAIRS):
            lanes = slice(p * BLOCK, (p + 1) * BLOCK)
            qs = _halves(q_ref[0, :, lanes].astype(BF16), lo)
            dos = _halves(do_ref[0, :, lanes].astype(BF16), lo)
            dss, prs = [], []
            for hh in range(2):
                head = (kv * SWA_PAIRS + p) * 2 + hh
                pr, ps = _swa_probs(qs[hh], kdup, valid, dist, head, sink_ref)
                dp = _dot_nt(dos[hh], vdup)
                dsum = jnp.sum(pr * dp, axis=1, keepdims=True)
                dsink_ref[pl.ds(head, 1), :] += jnp.broadcast_to(-jnp.sum(ps * dsum, axis=0, keepdims=True), (1, BLOCK))
                dss.append((pr * (dp - dsum) * (HEAD ** -0.5)).astype(BF16))
                prs.append(pr.astype(BF16))
            dq_ref[0, :, lanes] = jnp.dot(jnp.concatenate(dss, axis=1), k_bd, preferred_element_type=F32)
            dkc = dkc + _dot_tn(jnp.concatenate(dss, axis=0), jnp.concatenate(qs, axis=0))
            dvc = dvc + _dot_tn(jnp.concatenate(prs, axis=0), jnp.concatenate(dos, axis=0))
        rows = lambda blk: pl.ds(pl.multiple_of(blk * BLOCK, BLOCK), BLOCK)
        for part, r in enumerate((rows(prev), rows(n), slice(0, BLOCK))):
            dk_ref[0, 0, r, :] += dkc[part * BLOCK:(part + 1) * BLOCK]
            dv_ref[0, 0, r, :] += dvc[part * BLOCK:(part + 1) * BLOCK]

        @pl.when(n == nb - 1)
        def _():
            for ref in (dk_ref, dv_ref):
                x = ref[0, 0]
                ref[0, 0] = x + pltpu.roll(x, HEAD, 1)

    kvout = pl.BlockSpec((1, 1, lp, BLOCK), lambda bi, kv, n: (bi, kv, 0, 0))
    kvshape = jax.ShapeDtypeStruct((b, SWA_KV_HEADS, lp, BLOCK), F32)
    return pl.pallas_call(
        body, grid=grid, in_specs=[qcol(OD_Q), kcol(OD_K), kcol(OD_V), sink, qcol(0)], out_specs=[qcol(0), kvout, kvout, sink],
        out_shape=[jax.ShapeDtypeStruct((b, lp, SWA_HEADS * HEAD), F32), kvshape, kvshape, jax.ShapeDtypeStruct((SWA_HEADS, BLOCK), F32)],
        compiler_params=_params(("arbitrary", "arbitrary", "arbitrary")), name=name,
    )(proj3, proj3, proj3, sinks, do)


def _kernel_weights(ev_w_in, ev_w_uq, ev_w_ukv, od_w_in):
    zeros = lambda r, c: jnp.zeros((r, c), ev_w_in.dtype)
    q_sb, k_sb, v_sb, g_sb, c_q, c_kv, k_r, g_mla = jnp.split(ev_w_in, [512, 1024, 1536, 2048, 2304, 2432, 2464], axis=1)
    w0 = jnp.concatenate([g_sb, g_mla, q_sb, k_sb, v_sb, c_q, c_kv, zeros(D_MODEL, MLA_NOPE), k_r, zeros(D_MODEL, 32)], axis=1)
    uq = ev_w_uq.reshape(MLA_Q_LORA, MLA_HEADS, MLA_NOPE + MLA_ROPE)
    wq = jnp.pad(uq, ((0, 0), (0, 0), (0, BLOCK - MLA_NOPE - MLA_ROPE))).reshape(MLA_Q_LORA, MLA_HEADS * BLOCK)
    ukv = ev_w_ukv.reshape(MLA_KV_LORA, MLA_HEADS, BLOCK)
    wk = jnp.pad(ukv[:, :, :MLA_NOPE], ((0, 0), (0, 0), (0, BLOCK - MLA_NOPE))).reshape(MLA_KV_LORA, MLA_HEADS * BLOCK)
    wv = ukv[:, :, MLA_NOPE:].reshape(MLA_KV_LORA, MLA_HEADS * HEAD)
    q, k, v, g = jnp.split(od_w_in, [1024, 1152, 1280], axis=1)
    w1 = jnp.concatenate([g, q, k, v], axis=1)
    return w0, wq, wk, wv, w1


def _original_grads(dw0, dwq, dwk, dwv, dw1):
    sl = lambda a, first, n: a[:, first:first + n]
    d_ev_w_in = jnp.concatenate([sl(dw0, EV_Q, 512), sl(dw0, EV_K, 512), sl(dw0, EV_V, 512), sl(dw0, EV_G, 512), sl(dw0, EV_CQ, 256),
                                 sl(dw0, EV_CKV, 128), sl(dw0, EV_KR + MLA_NOPE, MLA_ROPE), sl(dw0, EV_G + 512, 512)], axis=1)
    d_uq = dwq.reshape(MLA_Q_LORA, MLA_HEADS, BLOCK)[:, :, :MLA_NOPE + MLA_ROPE].reshape(MLA_Q_LORA, -1)
    d_ukv = jnp.concatenate([dwk.reshape(MLA_KV_LORA, MLA_HEADS, BLOCK)[:, :, :MLA_NOPE], dwv.reshape(MLA_KV_LORA, MLA_HEADS, HEAD)],
                            axis=2).reshape(MLA_KV_LORA, -1)
    d_od_w_in = jnp.concatenate([sl(dw1, OD_Q, 1024), sl(dw1, OD_K, 128), sl(dw1, OD_V, 128), sl(dw1, OD_G, 1024)], axis=1)
    return d_ev_w_in, d_uq, d_ukv, d_od_w_in


def _meta_rows_sum(dh0_3):
    b, _, d = dh0_3.shape

    def body(x_ref, o_ref):
        acc = x_ref[0, N_PAD:BLOCK, :]
        for i in range(1, b):
            acc = acc + x_ref[i, N_PAD:BLOCK, :]
        o_ref[...] = acc

    return pl.pallas_call(
        body, grid=(1,), in_specs=[pl.BlockSpec((b, BLOCK, d), lambda i: (0, 0, 0))], out_specs=pl.BlockSpec((N_META, d), lambda i: (0, 0)),
        out_shape=jax.ShapeDtypeStruct((N_META, d), F32), compiler_params=_params(("arbitrary",)), name="meta_rows_sum",
    )(dh0_3)


def _local_step(x, meta, norm_g, final_g, gq, gkv, sinks, target, ev_w_in, ev_w_uq, ev_w_ukv, wo0, od_w_in, wo1):
    b, seq, d = x.shape
    lp = seq + BLOCK
    t = b * lp
    w0, wq, wk, wv, w1 = _kernel_weights(ev_w_in, ev_w_uq, ev_w_ukv, od_w_in)
    h0 = jnp.concatenate([jnp.zeros((b, N_PAD, d), F32), jnp.broadcast_to(meta[None], (b, N_META, d)), x], axis=1).reshape(t, d)
    tabs = _rope_tables(lp)
    g0, g1 = norm_g[0:1], norm_g[1:2]

    hn0 = _rms_fwd(h0, g0, "norm0")
    proj0 = _mm(hn0, w0, "inproj0")
    p0 = proj0.reshape(b, lp, EV_N)
    o_sb, sb_tot = _sb_fwd(p0, "sb_fwd")
    qf, kf, v = _mla_prep_fwd(p0, gq, gkv, wq, wk, wv, tabs, "mla_prep_fwd")
    o_mla, lse = _mla_fwd(qf, kf, v, "mla_fwd")
    o0 = [o_sb.reshape(t, -1), o_mla.reshape(t, -1)]
    ao0 = _gate_fwd(o0, proj0, "gate0")
    h1 = _mm(ao0, wo0, "outproj0", res=h0)

    hn1 = _rms_fwd(h1, g1, "norm1")
    proj1 = _mm(hn1, w1, "inproj1")
    p1 = proj1.reshape(b, lp, OD_N)
    sinks_b = jnp.broadcast_to(sinks.reshape(SWA_HEADS, 1), (SWA_HEADS, BLOCK))
    o1 = _swa_fwd(p1, sinks_b, "swa_fwd").reshape(t, -1)
    ao1 = _gate_fwd([o1], proj1, "gate1")
    h2 = _mm(ao1, wo1, "outproj1", res=h1)

    dh2, d_final_g, loss = _loss_head(h2, final_g.reshape(1, d), target, b, lp)

    d_wo1 = _mm_tn(ao1, dh2, "d_wo1")
    dao1 = _mm_nt(dh2, wo1, "d_ao1")
    (do1,), dg1 = _gate_bwd(dao1, [o1], proj1, "gate1_bwd")
    dq1, dk4, dv4, d_sinks = _swa_bwd(p1, sinks_b, do1.reshape(b, lp, -1), "swa_bwd")
    unheads = lambda a: a[..., :HEAD].transpose(0, 2, 1, 3).reshape(t, SWA_KV_HEADS * HEAD).astype(BF16)
    dproj1 = jnp.concatenate([dg1, dq1.reshape(t, -1).astype(BF16), unheads(dk4), unheads(dv4)], axis=1)
    d_w1 = _mm_tn(hn1, dproj1, "d_w1")
    dhn1 = _mm_nt(dproj1, w1, "d_hn1")
    dh1, d_g1 = _rms_bwd(h1, g1, dhn1, dh2, "norm1_bwd")

    d_wo0 = _mm_tn(ao0, dh1, "d_wo0")
    dao0 = _mm_nt(dh1, wo0, "d_ao0")
    (do_sb, do_mla), dg0 = _gate_bwd(dao0, o0, proj0, "gate0_bwd")
    dq_sb, dk_sb, dv_sb = _sb_bwd(p0, sb_tot, do_sb.reshape(b, lp, -1), "sb_bwd")
    dqf, dkf, dv = _mla_bwd(qf, kf, v, o_mla, lse, do_mla.reshape(b, lp, -1), "mla_bwd")
    dcq, dckv, dkr, d_wq, d_wk, d_wv, d_gq, d_gkv = _mla_prep_bwd(p0, gq, gkv, wq, wk, wv, tabs, dqf, dkf, dv, "mla_prep_bwd")
    flat = lambda a: a.reshape(t, -1).astype(BF16)
    dproj0 = jnp.concatenate([dg0, flat(dq_sb), flat(dk_sb), flat(dv_sb), flat(dcq), flat(dckv), flat(dkr)], axis=1)
    d_w0 = _mm_tn(hn0, dproj0, "d_w0")
    dhn0 = _mm_nt(dproj0, w0, "d_hn0")
    dh0, d_g0 = _rms_bwd(h0, g0, dhn0, dh1, "norm0_bwd")
    dh0 = dh0.reshape(b, lp, d)

    d_ev_w_in, d_uq, d_ukv, d_od_w_in = _original_grads(d_w0, d_wq, d_wk, d_wv, d_w1)
    grads = dict(meta=_meta_rows_sum(dh0), norm_g=jnp.concatenate([d_g0, d_g1], axis=0), final_g=d_final_g.reshape(d),
                 ev_w_in=d_ev_w_in, ev_q_norm_g=d_gq, ev_kv_norm_g=d_gkv, ev_w_uq=d_uq, ev_w_ukv=d_ukv, ev_w_out=d_wo0,
                 od_w_in=d_od_w_in, od_sinks=d_sinks[:, 0].reshape(1, SWA_HEADS), od_w_out=d_wo1)
    return loss, dh0[:, BLOCK:], grads


MESH = pl.DeviceIdType.MESH
ANY = pl.BlockSpec(memory_space=pl.ANY)


def _place():
    return lax.axis_index("x"), lax.axis_index("y"), lax.axis_index("c")


def _other_chips(x, y):
    return [(1 - x, y), (x, 1 - y), (1 - x, 1 - y)]


def _all_gather_chips(arrs, name):
    n = len(arrs)

    def body(*refs):
        ins, outs = refs[:n], refs[n:2 * n]
        send_sems, recv_sems, loc_sems = refs[2 * n:]
        x, y, c = _place()
        me = 2 * x + y
        started = []
        for a in range(n):
            loc = pltpu.make_async_copy(ins[a], outs[a].at[me], loc_sems.at[a])
            loc.start()
            started.append(loc)
            for k, (px, py) in enumerate(_other_chips(x, y)):
                cp = pltpu.make_async_remote_copy(src_ref=ins[a], dst_ref=outs[a].at[me], send_sem=send_sems.at[3 * a + k],
                                                  recv_sem=recv_sems.at[3 * a + k], device_id=(px, py, c), device_id_type=MESH)
                cp.start()
        for a in range(n):
            for k, (px, py) in enumerate(_other_chips(x, y)):
                cp = pltpu.make_async_remote_copy(src_ref=ins[a], dst_ref=outs[a].at[2 * px + py], send_sem=send_sems.at[3 * a + k],
                                                  recv_sem=recv_sems.at[3 * a + k], device_id=(px, py, c), device_id_type=MESH)
                cp.wait_recv()
                cp.wait_send()
        for loc in started:
            loc.wait()

    return pl.pallas_call(
        body, in_specs=[ANY] * n, out_specs=[ANY] * n,
        out_shape=[jax.ShapeDtypeStruct((N_CHIPS,) + a.shape, a.dtype) for a in arrs],
        scratch_shapes=[pltpu.SemaphoreType.DMA((3 * n,)), pltpu.SemaphoreType.DMA((3 * n,)), pltpu.SemaphoreType.DMA((n,))],
        name=name,
    )(*arrs)


def _sibling_exchange(g, name):
    def body(g_ref, o_ref, send_sem, recv_sem):
        x, y, c = _place()
        cp = pltpu.make_async_remote_copy(src_ref=g_ref, dst_ref=o_ref, send_sem=send_sem, recv_sem=recv_sem,
                                          device_id=(x, y, 1 - c), device_id_type=MESH)
        cp.start()
        cp.wait()

    return pl.pallas_call(
        body, in_specs=[ANY], out_specs=ANY, out_shape=jax.ShapeDtypeStruct(g.shape, g.dtype),
        scratch_shapes=[pltpu.SemaphoreType.DMA(()), pltpu.SemaphoreType.DMA(())],
        name=name,
    )(g)


def _chip_scatter(s, name):
    def body(s_ref, r_ref, send_sems, recv_sems, loc_sem):
        x, y, c = _place()
        me = 2 * x + y
        loc = pltpu.make_async_copy(s_ref.at[me], r_ref.at[me], loc_sem)
        loc.start()
        for k, (px, py) in enumerate(_other_chips(x, y)):
            pltpu.make_async_remote_copy(src_ref=s_ref.at[2 * px + py], dst_ref=r_ref.at[me], send_sem=send_sems.at[k],
                                         recv_sem=recv_sems.at[k], device_id=(px, py, c), device_id_type=MESH).start()
        for k, (px, py) in enumerate(_other_chips(x, y)):
            cp = pltpu.make_async_remote_copy(src_ref=s_ref.at[2 * px + py], dst_ref=r_ref.at[2 * px + py], send_sem=send_sems.at[k],
                                              recv_sem=recv_sems.at[k], device_id=(px, py, c), device_id_type=MESH)
            cp.wait_recv()
            cp.wait_send()
        loc.wait()

    return pl.pallas_call(
        body, in_specs=[ANY], out_specs=ANY, out_shape=jax.ShapeDtypeStruct(s.shape, s.dtype),
        scratch_shapes=[pltpu.SemaphoreType.DMA((3,)), pltpu.SemaphoreType.DMA((3,)), pltpu.SemaphoreType.DMA(())],
        name=name,
    )(s)


def _all_reduce_small(v, name):
    shape = v.shape

    def body(v_ref, o_ref, slots, send_sems, recv_sems):
        x, y, c = _place()
        me = 4 * x + 2 * y + c
        slots[me] = v_ref[...]
        for r in range(1, N_DEV):
            peer = (x ^ (r >> 2), y ^ ((r >> 1) & 1), c ^ (r & 1))
            pltpu.make_async_remote_copy(src_ref=v_ref, dst_ref=slots.at[me], send_sem=send_sems.at[r - 1], recv_sem=recv_sems.at[r - 1],
                                         device_id=peer, device_id_type=MESH).start()
        for r in range(1, N_DEV):
            peer = (x ^ (r >> 2), y ^ ((r >> 1) & 1), c ^ (r & 1))
            cp = pltpu.make_async_remote_copy(src_ref=v_ref, dst_ref=slots.at[4 * peer[0] + 2 * peer[1] + peer[2]], send_sem=send_sems.at[r - 1],
                                              recv_sem=recv_sems.at[r - 1], device_id=peer, device_id_type=MESH)
            cp.wait_recv()
            cp.wait_send()
        acc = slots[0]
        for d in range(1, N_DEV):
            acc = acc + slots[d]
        o_ref[...] = acc

    vm = pl.BlockSpec(memory_space=pltpu.VMEM)
    return pl.pallas_call(
        body, in_specs=[vm], out_specs=vm, out_shape=jax.ShapeDtypeStruct(shape, F32),
        scratch_shapes=[pltpu.VMEM((N_DEV,) + shape, F32), pltpu.SemaphoreType.DMA((N_DEV - 1,)), pltpu.SemaphoreType.DMA((N_DEV - 1,))],
        name=name,
    )(v)


def _add2(a, b, name):
    n, r, cdim = a.shape
    tr = 384

    def body(a_ref, b_ref, o_ref):
        o_ref[...] = a_ref[...] + b_ref[...]

    blk = pl.BlockSpec((1, tr, cdim), lambda j, i: (j, i, 0))
    return pl.pallas_call(body, grid=(n, r // tr), in_specs=[blk, blk], out_specs=blk, out_shape=jax.ShapeDtypeStruct(a.shape, a.dtype),
                          compiler_params=_params(("parallel", "parallel")), name=name)(a, b)


def _adamw(parts, w, m, v, name):
    npart, r, cdim = parts.shape
    tr = 384 if r % 384 == 0 else r

    def body(p_ref, w_ref, m_ref, v_ref, g_ref, d_ref, nm_ref, nv_ref):
        g = p_ref[0]
        for j in range(1, npart):
            g = g + p_ref[j]
        m_new = ADAM_B1 * m_ref[...] + (1.0 - ADAM_B1) * g
        v_new = ADAM_B2 * v_ref[...] + (1.0 - ADAM_B2) * (g * g)
        m_hat = m_new / (1.0 - ADAM_B1 ** ADAM_STEP)
        v_hat = v_new / (1.0 - ADAM_B2 ** ADAM_STEP)
        g_ref[...] = g
        d_ref[...] = -ADAM_LR * (m_hat / (jnp.sqrt(v_hat) + ADAM_EPS) + ADAM_WD * w_ref[...])
        nm_ref[...] = m_new
        nv_ref[...] = v_new

    blk = pl.BlockSpec((tr, cdim), lambda i: (i, 0))
    shp = jax.ShapeDtypeStruct((r, cdim), F32)
    return pl.pallas_call(
        body, grid=(r // tr,), in_specs=[pl.BlockSpec((npart, tr, cdim), lambda i: (0, i, 0)), blk, blk, blk], out_specs=[blk] * 4,
        out_shape=[shp] * 4, compiler_params=_params(("parallel",)), name=name,
    )(parts, w, m, v)


BIG = ("ev_w_in", "ev_w_uq", "ev_w_ukv", "ev_w_out", "od_w_in", "od_w_out", "meta")
SMALL = ("norm_g", "final_g", "ev_q_norm_g", "ev_kv_norm_g", "od_sinks")
PACK_ROWS = 1920
SMALL_SHAPE = (8, 512)


def _pack_big(arrs):
    rows = [a.reshape(-1, D_MODEL) for a in arrs]
    used = sum(r.shape[0] for r in rows)
    return jnp.pad(jnp.concatenate(rows, axis=0), ((0, PACK_ROWS - used), (0, 0)))


def _unpack_big(p, shapes):
    out, at = [], 0
    for s in shapes:
        nrow = int(np.prod(s)) // D_MODEL
        out.append(p[at:at + nrow].reshape(s))
        at += nrow
    return out


def _pack_small(arrs, extra=None):
    flat = [a.reshape(-1) for a in arrs] + ([] if extra is None else [extra.reshape(-1)])
    used = sum(f.shape[0] for f in flat)
    return jnp.pad(jnp.concatenate(flat), (0, SMALL_SHAPE[0] * SMALL_SHAPE[1] - used)).reshape(SMALL_SHAPE)


def _unpack_small(p, shapes):
    flat, out, at = p.reshape(-1), [], 0
    for s in shapes:
        n = int(np.prod(s))
        out.append(flat[at:at + n].reshape(s))
        at += n
    return out, flat[at]


def _shard_of(full, name, j):
    if name in ("ev_w_out", "od_w_out"):
        n = full.shape[0] // N_CHIPS
        return full[j * n:(j + 1) * n]
    n = full.shape[1] // N_CHIPS
    return full[:, j * n:(j + 1) * n]


def kernel(x, meta, norm_g, final_g, ev_w_in, ev_q_norm_g, ev_kv_norm_g, ev_w_uq, ev_w_ukv, ev_w_out, od_w_in, od_sinks, od_w_out, loss_target, m_meta, m_norm_g, m_final_g, m_ev_w_in, m_ev_q_norm_g, m_ev_kv_norm_g, m_ev_w_uq, m_ev_w_ukv, m_ev_w_out, m_od_w_in, m_od_sinks, m_od_w_out, v_meta, v_norm_g, v_final_g, v_ev_w_in, v_ev_q_norm_g, v_ev_kv_norm_g, v_ev_w_uq, v_ev_w_ukv, v_ev_w_out, v_od_w_in, v_od_sinks, v_od_w_out):
    given = dict(locals())
    big_w = [given[n][0] if given[n].ndim == 3 else given[n] for n in BIG]
    big_shapes = [given[n].shape for n in BIG]

    wpack = _pack_big(big_w[:-1]).astype(BF16)
    wall, meta_all = _all_gather_chips([wpack, meta], "gather_weights")
    per_chip = [_unpack_big(wall[j], [a.shape for a in big_w[:-1]]) for j in range(N_CHIPS)]
    cat = lambda i, axis: jnp.concatenate([per_chip[j][i] for j in range(N_CHIPS)], axis=axis)
    full = dict(ev_w_in=cat(0, 1), ev_w_uq=cat(1, 1), ev_w_ukv=cat(2, 1), ev_w_out=cat(3, 0), od_w_in=cat(4, 1), od_w_out=cat(5, 0))
    meta_full = jnp.concatenate([meta_all[j] for j in range(N_CHIPS)], axis=1)

    loss, grad_x, grads = _local_step(x, meta_full, norm_g, final_g, ev_q_norm_g, ev_kv_norm_g, od_sinks, loss_target,
                                      full["ev_w_in"], full["ev_w_uq"], full["ev_w_ukv"], full["ev_w_out"], full["od_w_in"], full["od_w_out"])

    gpack = jnp.stack([_pack_big([_shard_of(grads[n], n, j) for n in BIG]) for j in range(N_CHIPS)])
    gsum = _add2(gpack, _sibling_exchange(gpack, "grads_to_sibling"), "add_sibling")
    parts = _chip_scatter(gsum, "grads_to_chips")
    big_out = _adamw(parts, _pack_big(big_w), _pack_big([given["m_" + n] for n in BIG]), _pack_big([given["v_" + n] for n in BIG]),
                     "adamw_matrices")
    big_out = [_unpack_big(o, big_shapes) for o in big_out]

    small_shapes = [given[n].shape for n in SMALL]
    ssum = _all_reduce_small(_pack_small([grads[n] for n in SMALL], loss[0, 0]), "reduce_vectors")
    small_out = _adamw(ssum[None], _pack_small([given[n] for n in SMALL]), _pack_small([given["m_" + n] for n in SMALL]),
                       _pack_small([given["v_" + n] for n in SMALL]), "adamw_vectors")
    total_loss = ssum.reshape(-1)[sum(int(np.prod(s)) for s in small_shapes)]
    small_out = [_unpack_small(o, small_shapes)[0] for o in small_out]

    names = ("meta", "norm_g", "final_g", "ev_w_in", "ev_q_norm_g", "ev_kv_norm_g", "ev_w_uq", "ev_w_ukv", "ev_w_out", "od_w_in", "od_sinks",
             "od_w_out")
    outs = [total_loss, grad_x]
    for kind in range(4):
        for n in names:
            outs.append(big_out[kind][BIG.index(n)] if n in BIG else small_out[kind][SMALL.index(n)])
    return tuple(outs)
```

```python
import functools
import math

import numpy as np
import jax
import jax.numpy as jnp
from jax import lax
from jax.experimental import pallas as pl
from jax.experimental.pallas import tpu as pltpu

F32 = jnp.float32
BF16 = jnp.bfloat16

D_MODEL = 1024
BLOCK = 128
N_META = 16
N_PAD = BLOCK - N_META
NORM_EPS = 1e-6
NEG = -1e30
HEAD = 64
SB_HEADS = 8
MLA_HEADS = 8
MLA_Q_LORA = 256
MLA_KV_LORA = 128
MLA_NOPE = 64
MLA_ROPE = 32
ROPE_BASE = 10000.0
SWA_HEADS = 16
SWA_KV_HEADS = 2
SWA_WINDOW = 128
N_CHIPS = 4
N_DEV = 8

ADAM_LR = 0.001
ADAM_B1 = 0.9
ADAM_B2 = 0.999
ADAM_EPS = 1e-08
ADAM_WD = 0.01
ADAM_STEP = 10

VMEM_LIMIT = 48 * 1024 * 1024

EV_G, EV_Q, EV_K, EV_V, EV_CQ, EV_CKV, EV_KR, EV_N = 0, 1024, 1536, 2048, 2560, 2816, 2944, 3072
OD_G, OD_Q, OD_K, OD_V, OD_N = 0, 1024, 2048, 2176, 2304


def _params(sem=None):
    return pltpu.CompilerParams(dimension_semantics=sem, vmem_limit_bytes=VMEM_LIMIT)


def _row_tile(m):
    return 256 if m % 256 == 0 else 128


def _dot(a, b):
    return jnp.dot(a.astype(BF16), b.astype(BF16), preferred_element_type=F32)


def _dot_nt(a, b):
    return lax.dot_general(a.astype(BF16), b.astype(BF16), (((1,), (1,)), ((), ())), preferred_element_type=F32)


def _dot_tn(a, b):
    return lax.dot_general(a.astype(BF16), b.astype(BF16), (((0,), (0,)), ((), ())), preferred_element_type=F32)


def _rms_fwd(h, g, name):
    t, d = h.shape
    tm = _row_tile(t)

    def body(h_ref, g_ref, o_ref):
        x = h_ref[...]
        r = lax.rsqrt(jnp.mean(x * x, axis=-1, keepdims=True) + NORM_EPS)
        o_ref[...] = ((x * r) * g_ref[...]).astype(o_ref.dtype)

    return pl.pallas_call(
        body, grid=(t // tm,),
        in_specs=[pl.BlockSpec((tm, d), lambda i: (i, 0)), pl.BlockSpec((1, d), lambda i: (0, 0))],
        out_specs=pl.BlockSpec((tm, d), lambda i: (i, 0)),
        out_shape=jax.ShapeDtypeStruct((t, d), BF16), compiler_params=_params(("parallel",)), name=name,
    )(h, g)


def _rms_bwd(h, g, dy, dres, name):
    t, d = h.shape
    tm = _row_tile(t)

    def body(h_ref, g_ref, dy_ref, dres_ref, dh_ref, dg_ref):
        @pl.when(pl.program_id(0) == 0)
        def _():
            dg_ref[...] = jnp.zeros_like(dg_ref)

        x = h_ref[...]
        r = lax.rsqrt(jnp.mean(x * x, axis=-1, keepdims=True) + NORM_EPS)
        xr = x * r
        dy_ = dy_ref[...]
        u = dy_ * g_ref[...]
        dh_ref[...] = dres_ref[...] + r * (u - xr * jnp.mean(u * xr, axis=-1, keepdims=True))
        dg_ref[...] += jnp.sum(dy_ * xr, axis=0, keepdims=True)

    row = pl.BlockSpec((tm, d), lambda i: (i, 0))
    vec = pl.BlockSpec((1, d), lambda i: (0, 0))
    return pl.pallas_call(
        body, grid=(t // tm,), in_specs=[row, vec, row, row], out_specs=[row, vec],
        out_shape=[jax.ShapeDtypeStruct((t, d), F32), jax.ShapeDtypeStruct((1, d), F32)],
        compiler_params=_params(("arbitrary",)), name=name,
    )(h, g, dy, dres)


def _col_tile(n):
    for c in (1024, 768, 640, 512, 384, 256, 128):
        if n % c == 0:
            return c
    return n


def _mm(a, w, name, res=None, out_dtype=F32, a_cols=None):
    m = a.shape[0]
    k, n = w.shape
    a_blk = 0 if a_cols is None else a_cols[0] // k
    assert a_cols is None or (a_cols[1] == k and a_cols[0] % k == 0)
    tm, tn = _row_tile(m), _col_tile(n)

    def body(*refs):
        if res is None:
            a_ref, w_ref, o_ref = refs
            acc = _dot(a_ref[...], w_ref[...])
        else:
            a_ref, w_ref, r_ref, o_ref = refs
            acc = r_ref[...] + _dot(a_ref[...], w_ref[...])
        o_ref[...] = acc.astype(o_ref.dtype)

    in_specs = [pl.BlockSpec((tm, k), lambda j, i: (i, a_blk)), pl.BlockSpec((k, tn), lambda j, i: (0, j))]
    args = [a, w]
    if res is not None:
        in_specs.append(pl.BlockSpec((tm, tn), lambda j, i: (i, j)))
        args.append(res)
    return pl.pallas_call(
        body, grid=(n // tn, m // tm), in_specs=in_specs, out_specs=pl.BlockSpec((tm, tn), lambda j, i: (i, j)),
        out_shape=jax.ShapeDtypeStruct((m, n), out_dtype), compiler_params=_params(("parallel", "parallel")), name=name,
    )(*args)


def _mm_nt(a, w, name):
    m, n = a.shape
    k = w.shape[0]
    tm, tk = _row_tile(m), _col_tile(k)

    def body(a_ref, w_ref, o_ref):
        o_ref[...] = _dot_nt(a_ref[...], w_ref[...])

    return pl.pallas_call(
        body, grid=(k // tk, m // tm),
        in_specs=[pl.BlockSpec((tm, n), lambda j, i: (i, 0)), pl.BlockSpec((tk, n), lambda j, i: (j, 0))],
        out_specs=pl.BlockSpec((tm, tk), lambda j, i: (i, j)),
        out_shape=jax.ShapeDtypeStruct((m, k), F32), compiler_params=_params(("parallel", "parallel")), name=name,
    )(a, w)


def _mm_tn(x, dy, name):
    m, k = x.shape
    n = dy.shape[1]
    tm, tn = _row_tile(m), _col_tile(n)

    def body(x_ref, dy_ref, o_ref):
        @pl.when(pl.program_id(1) == 0)
        def _():
            o_ref[...] = jnp.zeros_like(o_ref)

        o_ref[...] += _dot_tn(x_ref[...], dy_ref[...])

    return pl.pallas_call(
        body, grid=(n // tn, m // tm),
        in_specs=[pl.BlockSpec((tm, k), lambda j, i: (i, 0)), pl.BlockSpec((tm, tn), lambda j, i: (i, j))],
        out_specs=pl.BlockSpec((k, tn), lambda j, i: (0, j)),
        out_shape=jax.ShapeDtypeStruct((k, n), F32), compiler_params=_params(("parallel", "arbitrary")), name=name,
    )(x, dy)


def _silu_parts(g):
    s = 1.0 / (1.0 + jnp.exp(-g))
    return g * s, s * (1.0 + g * (1.0 - s))


def _gate_fwd(o_parts, proj, name):
    t = proj.shape[0]
    tm = _row_tile(t)
    w = D_MODEL // len(o_parts)

    def body(*refs):
        g_ref, o_ref = refs[-2], refs[-1]
        for p, r in enumerate(refs[:-2]):
            sil, _ = _silu_parts(g_ref[:, p * w:(p + 1) * w])
            o_ref[:, p * w:(p + 1) * w] = (r[...].astype(F32) * sil).astype(o_ref.dtype)

    return pl.pallas_call(
        body, grid=(t // tm,),
        in_specs=[pl.BlockSpec((tm, w), lambda i: (i, 0)) for _ in o_parts] + [pl.BlockSpec((tm, D_MODEL), lambda i: (i, 0))],
        out_specs=pl.BlockSpec((tm, D_MODEL), lambda i: (i, 0)),
        out_shape=jax.ShapeDtypeStruct((t, D_MODEL), BF16), compiler_params=_params(("parallel",)), name=name,
    )(*o_parts, proj)


def _gate_bwd(dao, o_parts, proj, name):
    t = proj.shape[0]
    tm = _row_tile(t)
    np_ = len(o_parts)
    w = D_MODEL // np_

    def body(*refs):
        dao_ref, g_ref = refs[0], refs[1 + np_]
        do_refs, dg_ref = refs[2 + np_:2 + 2 * np_], refs[-1]
        for p in range(np_):
            sl = slice(p * w, (p + 1) * w)
            sil, dsil = _silu_parts(g_ref[:, sl])
            da = dao_ref[:, sl]
            do_refs[p][...] = da * sil
            dg_ref[:, sl] = (da * refs[1 + p][...].astype(F32) * dsil).astype(dg_ref.dtype)

    full = pl.BlockSpec((tm, D_MODEL), lambda i: (i, 0))
    part = pl.BlockSpec((tm, w), lambda i: (i, 0))
    outs = pl.pallas_call(
        body, grid=(t // tm,), in_specs=[full] + [part] * np_ + [full], out_specs=[part] * np_ + [full],
        out_shape=[jax.ShapeDtypeStruct((t, w), F32)] * np_ + [jax.ShapeDtypeStruct((t, D_MODEL), BF16)],
        compiler_params=_params(("parallel",)), name=name,
    )(dao, *o_parts, proj)
    return outs[:np_], outs[np_]


def _loss_head(h2, gf, target, b, lp):
    d = h2.shape[1]
    nb = lp // BLOCK
    h3 = h2.reshape(b, lp, d)

    def body(h_ref, g_ref, t_ref, dh_ref, dg_ref, loss_ref):
        first = (pl.program_id(0) == 0) & (pl.program_id(1) == 0)

        @pl.when(first)
        def _():
            dg_ref[...] = jnp.zeros_like(dg_ref)
            loss_ref[...] = jnp.zeros_like(loss_ref)

        @pl.when(pl.program_id(1) == 0)
        def _():
            dh_ref[...] = jnp.zeros_like(dh_ref)

        @pl.when(pl.program_id(1) > 0)
        def _():
            x = h_ref[0]
            r = lax.rsqrt(jnp.mean(x * x, axis=-1, keepdims=True) + NORM_EPS)
            xr = x * r
            g = g_ref[...]
            diff = xr * g - t_ref[0]
            loss_ref[...] += 0.5 * jnp.sum(jnp.mean(diff * diff, axis=-1, keepdims=True))
            dy = diff * (1.0 / d)
            u = dy * g
            dh_ref[0] = r * (u - xr * jnp.mean(u * xr, axis=-1, keepdims=True))
            dg_ref[...] += jnp.sum(dy * xr, axis=0, keepdims=True)

    blk = pl.BlockSpec((1, BLOCK, d), lambda bi, n: (bi, n, 0))
    dh, dg, loss = pl.pallas_call(
        body, grid=(b, nb),
        in_specs=[blk, pl.BlockSpec((1, d), lambda bi, n: (0, 0)),
                  pl.BlockSpec((1, BLOCK, d), lambda bi, n: (bi, jnp.maximum(n - 1, 0), 0))],
        out_specs=[blk, pl.BlockSpec((1, d), lambda bi, n: (0, 0)), pl.BlockSpec((8, 128), lambda bi, n: (0, 0))],
        out_shape=[jax.ShapeDtypeStruct((b, lp, d), F32), jax.ShapeDtypeStruct((1, d), F32), jax.ShapeDtypeStruct((8, 128), F32)],
        compiler_params=_params(("arbitrary", "arbitrary")), name="loss_head",
    )(h3, gf, target)
    return dh.reshape(b * lp, d), dg, loss


def _iota2(shape, dim):
    return lax.broadcasted_iota(jnp.int32, shape, dim)


def _split_dot(x, tri):
    hi = x.astype(BF16)
    lo = (x - hi.astype(F32)).astype(BF16)
    return jnp.dot(hi, tri, preferred_element_type=F32) + jnp.dot(lo, tri, preferred_element_type=F32)


KEYS = 256


def _lo_lanes():
    return _iota2((1, BLOCK), 1) < HEAD


def _halves(x, lo):
    zero = jnp.zeros_like(x)
    return jnp.where(lo, x, zero), jnp.where(lo, zero, x)


def _stack_halves(x, lo):
    a, b = _halves(x, lo)
    return jnp.concatenate([a, b], axis=0)


def _pair(a, b, lo):
    return jnp.where(lo, a, b)


def _key_chunk(c, lp, t_idx, strict):
    first = c * KEYS
    s0 = pl.multiple_of(jnp.minimum(first, lp - KEYS), BLOCK)
    s_idx = s0 + _iota2((BLOCK, KEYS), 1)
    seen = (s_idx < t_idx) if strict else (s_idx <= t_idx)
    return s0, seen & (s_idx >= jnp.maximum(first, N_PAD))


def _n_chunks(i):
    return (i + 2) // 2


def _sb_scores(q_h, k, valid, after):
    z = _dot_nt(q_h, k) * (HEAD ** -0.5)
    sp = jnp.log(1.0 + jnp.exp(-jnp.abs(z)))
    lb = jnp.minimum(z, 0.0) - sp
    l1m_all = -jnp.maximum(z, 0.0) - sp
    l1m = jnp.where(valid, l1m_all, 0.0)
    return lb, l1m_all, l1m, _split_dot(l1m, after)


def _sb_fwd(proj3, name):
    b, lp, _ = proj3.shape
    nb = lp // BLOCK
    npair = SB_HEADS // 2

    def body(q_ref, k_ref, v_ref, o_ref, tot_ref):
        lo = _lo_lanes()
        after = (_iota2((KEYS, KEYS), 0) > _iota2((KEYS, KEYS), 1)).astype(BF16)

        def qblock(i, _):
            r0 = pl.multiple_of(i * BLOCK, BLOCK)
            qs = _halves(q_ref[0, pl.ds(r0, BLOCK), :].astype(BF16), lo)
            t_idx = r0 + _iota2((BLOCK, KEYS), 0)
            n = _n_chunks(i)

            def kchunk(cc, carry):
                cs, acc = carry[:2], carry[2]
                s0, valid = _key_chunk(n - 1 - cc, lp, t_idx, True)
                k = k_ref[0, pl.ds(s0, KEYS), :].astype(BF16)
                a_s, new = [], []
                for h in range(2):
                    lb, _, l1m, suf = _sb_scores(qs[h], k, valid, after)
                    a_s.append(jnp.where(valid, jnp.exp(lb + suf + cs[h]), 0.0).astype(BF16))
                    new.append(cs[h] + jnp.sum(l1m, axis=1, keepdims=True))
                v_bd = _stack_halves(v_ref[0, pl.ds(s0, KEYS), :].astype(BF16), lo)
                return (*new, acc + jnp.dot(jnp.concatenate(a_s, axis=1), v_bd, preferred_element_type=F32))

            zero = jnp.zeros((BLOCK, 1), F32)
            c_a, c_b, acc = lax.fori_loop(0, n, kchunk, (zero, zero, jnp.zeros((BLOCK, BLOCK), F32)))
            o_ref[0, pl.ds(r0, BLOCK), :] = acc
            tot_ref[0, pl.ds(r0, BLOCK), :] = jnp.broadcast_to(_pair(c_a, c_b, lo), (BLOCK, BLOCK))
            return 0

        lax.fori_loop(0, nb, qblock, 0)

    def col(first):
        return pl.BlockSpec((1, lp, 2 * HEAD), lambda bi, hp: (bi, 0, first // (2 * HEAD) + hp))

    shp = jax.ShapeDtypeStruct((b, lp, SB_HEADS * HEAD), F32)
    return pl.pallas_call(
        body, grid=(b, npair), in_specs=[col(EV_Q), col(EV_K), col(EV_V)], out_specs=[col(0), col(0)], out_shape=[shp, shp],
        compiler_params=_params(("parallel", "parallel")), name=name,
    )(proj3, proj3, proj3)


def _sb_bwd(proj3, tot, do, name):
    b, lp, _ = proj3.shape
    nb = lp // BLOCK
    npair = SB_HEADS // 2

    def body(q_ref, k_ref, v_ref, tot_ref, do_ref, dq_ref, dk_ref, dv_ref):
        lo = _lo_lanes()
        after = (_iota2((KEYS, KEYS), 0) > _iota2((KEYS, KEYS), 1)).astype(BF16)
        before = (_iota2((KEYS, KEYS), 0) < _iota2((KEYS, KEYS), 1)).astype(BF16)
        dk_ref[...] = jnp.zeros_like(dk_ref)
        dv_ref[...] = jnp.zeros_like(dv_ref)

        def qblock(i, _):
            r0 = pl.multiple_of(i * BLOCK, BLOCK)
            rows = pl.ds(r0, BLOCK)
            qs = _halves(q_ref[0, rows, :].astype(BF16), lo)
            dos = _halves(do_ref[0, rows, :].astype(BF16), lo)
            tot_i = tot_ref[0, rows, :]
            tots = (tot_i[:, 0:1], tot_i[:, HEAD:HEAD + 1])
            q_st, do_st = jnp.concatenate(qs, axis=0), jnp.concatenate(dos, axis=0)
            t_idx = r0 + _iota2((BLOCK, KEYS), 0)

            def kchunk(c, carry):
                s0, valid = _key_chunk(c, lp, t_idx, True)
                keys = pl.ds(s0, KEYS)
                k = k_ref[0, keys, :].astype(BF16)
                v = v_ref[0, keys, :].astype(BF16)
                a_s, dzs, new = [], [], []
                for h in range(2):
                    left, pre = carry[2 * h], carry[2 * h + 1]
                    lb, l1m_all, l1m, suf = _sb_scores(qs[h], k, valid, after)
                    here = jnp.sum(l1m, axis=1, keepdims=True)
                    a = jnp.where(valid, jnp.exp(lb + suf + (tots[h] - left - here)), 0.0)
                    w = a * _dot_nt(dos[h], v)
                    dz = jnp.where(valid, w * jnp.exp(l1m_all) - (pre + _split_dot(w, before)) * jnp.exp(lb), 0.0) * (HEAD ** -0.5)
                    new += [left + here, pre + jnp.sum(w, axis=1, keepdims=True)]
                    a_s.append(a.astype(BF16))
                    dzs.append(dz.astype(BF16))
                dk_ref[0, keys, :] += _dot_tn(jnp.concatenate(dzs, axis=0), q_st)
                dv_ref[0, keys, :] += _dot_tn(jnp.concatenate(a_s, axis=0), do_st)
                dq = carry[4] + jnp.dot(jnp.concatenate(dzs, axis=1), _stack_halves(k, lo), preferred_element_type=F32)
                return (*new, dq)

            zero = jnp.zeros((BLOCK, 1), F32)
            out = lax.fori_loop(0, _n_chunks(i), kchunk, (zero, zero, zero, zero, jnp.zeros((BLOCK, BLOCK), F32)))
            dq_ref[0, rows, :] = out[4]
            return 0

        lax.fori_loop(0, nb, qblock, 0)

    def col(first):
        return pl.BlockSpec((1, lp, 2 * HEAD), lambda bi, hp: (bi, 0, first // (2 * HEAD) + hp))

    shp = jax.ShapeDtypeStruct((b, lp, SB_HEADS * HEAD), F32)
    return pl.pallas_call(
        body, grid=(b, npair), in_specs=[col(EV_Q), col(EV_K), col(EV_V), col(0), col(0)], out_specs=[col(0)] * 3, out_shape=[shp] * 3,
        compiler_params=_params(("parallel", "parallel")), name=name,
    )(proj3, proj3, proj3, tot, do)


def _rope_tables(lp):
    half = MLA_ROPE // 2
    pos = (np.arange(lp) - N_PAD).astype(np.float32)
    inv = jnp.asarray(ROPE_BASE, F32) ** (-jnp.arange(half, dtype=F32) / half)
    ang = jnp.asarray(pos)[:, None] * inv[None, :]
    cos, sin = jnp.cos(ang), jnp.sin(ang)
    zeros = lambda n: jnp.zeros((lp, n), F32)
    c = jnp.concatenate([jnp.ones((lp, MLA_NOPE), F32), cos, cos, zeros(32)], axis=1)
    s1 = jnp.concatenate([zeros(MLA_NOPE), -sin, zeros(half), zeros(32)], axis=1)
    s2 = jnp.concatenate([zeros(MLA_NOPE), zeros(half), sin, zeros(32)], axis=1)
    return c, s1, s2


def _rope(x, c, s1, s2):
    half = MLA_ROPE // 2
    return x * c + pltpu.roll(x, BLOCK - half, 1) * s1 + pltpu.roll(x, half, 1) * s2


def _rope_t(dy, c, s1, s2):
    half = MLA_ROPE // 2
    return dy * c + pltpu.roll(dy * s1, half, 1) + pltpu.roll(dy * s2, BLOCK - half, 1)


def _rms_rows(x, g):
    r = lax.rsqrt(jnp.mean(x * x, axis=-1, keepdims=True) + NORM_EPS)
    return x * r, r


def _mla_prep_fwd(proj3, gq, gkv, wq, wk, wv, tabs, name):
    b, lp, _ = proj3.shape
    nb = lp // BLOCK
    hw = MLA_HEADS * BLOCK

    def body(cq_ref, ckv_ref, kr_ref, gq_ref, gkv_ref, wq_ref, wk_ref, wv_ref, c_ref, s1_ref, s2_ref, qf_ref, kf_ref, v_ref):
        c, s1, s2 = c_ref[...], s1_ref[...], s2_ref[...]
        xq, _ = _rms_rows(cq_ref[0], None)
        qh = _dot(xq * gq_ref[...], wq_ref[...])
        xk, _ = _rms_rows(ckv_ref[0], None)
        ckv_n = xk * gkv_ref[...]
        kv = _dot(ckv_n, wk_ref[...])
        v_ref[0] = _dot(ckv_n, wv_ref[...]).astype(v_ref.dtype)
        kr = _rope(kr_ref[0], c, s1, s2)
        for h in range(MLA_HEADS):
            ls = slice(h * BLOCK, (h + 1) * BLOCK)
            qf_ref[0, :, ls] = _rope(qh[:, ls], c, s1, s2).astype(qf_ref.dtype)
            kf_ref[0, :, ls] = (kv[:, ls] + kr).astype(kf_ref.dtype)

    def col(first, width):
        return pl.BlockSpec((1, BLOCK, width), lambda bi, n: (bi, n, first // width))

    def whole(a):
        return pl.BlockSpec(a.shape, lambda bi, n: (0,) * a.ndim)

    tab = pl.BlockSpec((BLOCK, BLOCK), lambda bi, n: (n, 0))
    return pl.pallas_call(
        body, grid=(b, nb),
        in_specs=[col(EV_CQ, MLA_Q_LORA), col(EV_CKV, MLA_KV_LORA), col(EV_KR, BLOCK), whole(gq), whole(gkv), whole(wq), whole(wk),
                  whole(wv), tab, tab, tab],
        out_specs=[col(0, hw), col(0, hw), col(0, MLA_HEADS * HEAD)],
        out_shape=[jax.ShapeDtypeStruct((b, lp, hw), BF16), jax.ShapeDtypeStruct((b, lp, hw), BF16),
                   jax.ShapeDtypeStruct((b, lp, MLA_HEADS * HEAD), BF16)],
        compiler_params=_params(("parallel", "parallel")), name=name,
    )(proj3, proj3, proj3, gq, gkv, wq, wk, wv, *tabs)


def _mla_prep_bwd(proj3, gq, gkv, wq, wk, wv, tabs, dqf, dkf, dv, name):
    b, lp, _ = proj3.shape
    nb = lp // BLOCK
    hw = MLA_HEADS * BLOCK

    def body(cq_ref, ckv_ref, gq_ref, gkv_ref, wq_ref, wk_ref, wv_ref, c_ref, s1_ref, s2_ref, dqf_ref, dkf_ref, dv_ref,
             dcq_ref, dckv_ref, dkr_ref, dwq_ref, dwk_ref, dwv_ref, dgq_ref, dgkv_ref, dqh):
        @pl.when((pl.program_id(0) == 0) & (pl.program_id(1) == 0))
        def _():
            for r in (dwq_ref, dwk_ref, dwv_ref, dgq_ref, dgkv_ref):
                r[...] = jnp.zeros_like(r)

        c, s1, s2 = c_ref[...], s1_ref[...], s2_ref[...]
        dkr = jnp.zeros((BLOCK, BLOCK), F32)
        for h in range(MLA_HEADS):
            ls = slice(h * BLOCK, (h + 1) * BLOCK)
            dqh[:, ls] = _rope_t(dqf_ref[0, :, ls].astype(F32), c, s1, s2).astype(dqh.dtype)
            dkr = dkr + dkf_ref[0, :, ls].astype(F32)
        dkr_ref[0] = _rope_t(dkr, c, s1, s2).astype(dkr_ref.dtype)

        def norm_bwd(x, g, dy, dg_ref):
            xr, r = _rms_rows(x, None)
            u = dy * g
            dg_ref[...] += jnp.sum(dy * xr, axis=0, keepdims=True)
            return r * (u - xr * jnp.mean(u * xr, axis=-1, keepdims=True))

        xq, _ = _rms_rows(cq_ref[0], None)
        cq_n = xq * gq_ref[...]
        dwq_ref[...] += _dot_tn(cq_n, dqh[...])
        dcq_ref[0] = norm_bwd(cq_ref[0], gq_ref[...], _dot_nt(dqh[...], wq_ref[...]), dgq_ref).astype(dcq_ref.dtype)
        xk, _ = _rms_rows(ckv_ref[0], None)
        ckv_n = xk * gkv_ref[...]
        dkf_, dv_ = dkf_ref[0], dv_ref[0]
        dwk_ref[...] += _dot_tn(ckv_n, dkf_)
        dwv_ref[...] += _dot_tn(ckv_n, dv_)
        dckv_n = _dot_nt(dkf_, wk_ref[...]) + _dot_nt(dv_, wv_ref[...])
        dckv_ref[0] = norm_bwd(ckv_ref[0], gkv_ref[...], dckv_n, dgkv_ref).astype(dckv_ref.dtype)

    def col(first, width):
        return pl.BlockSpec((1, BLOCK, width), lambda bi, n: (bi, n, first // width))

    def whole(a):
        return pl.BlockSpec(a.shape, lambda bi, n: (0,) * len(a.shape))

    tab = pl.BlockSpec((BLOCK, BLOCK), lambda bi, n: (n, 0))
    acc_shapes = [jax.ShapeDtypeStruct(a.shape, F32) for a in (wq, wk, wv, gq, gkv)]
    return pl.pallas_call(
        body, grid=(b, nb),
        in_specs=[col(EV_CQ, MLA_Q_LORA), col(EV_CKV, MLA_KV_LORA), whole(gq), whole(gkv), whole(wq), whole(wk), whole(wv), tab, tab, tab,
                  col(0, hw), col(0, hw), col(0, MLA_HEADS * HEAD)],
        out_specs=[col(0, MLA_Q_LORA), col(0, MLA_KV_LORA), col(0, BLOCK)] + [whole(a) for a in acc_shapes],
        out_shape=[jax.ShapeDtypeStruct((b, lp, MLA_Q_LORA), BF16), jax.ShapeDtypeStruct((b, lp, MLA_KV_LORA), BF16),
                   jax.ShapeDtypeStruct((b, lp, BLOCK), BF16)] + acc_shapes,
        scratch_shapes=[pltpu.VMEM((BLOCK, hw), BF16)],
        compiler_params=_params(("arbitrary", "arbitrary")), name=name,
    )(proj3, proj3, gq, gkv, wq, wk, wv, *tabs, dqf, dkf, dv)


def _mla_fwd(qf, kf, v, name):
    b, lp, _ = qf.shape
    nb = lp // BLOCK
    npair = MLA_HEADS // 2
    scale = (MLA_NOPE + MLA_ROPE) ** -0.5

    def body(q_ref, k_ref, v_ref, o_ref, lse_ref):
        lo = _lo_lanes()

        def qblock(i, _):
            r0 = pl.multiple_of(i * BLOCK, BLOCK)
            qs = [q_ref[0, pl.ds(r0, BLOCK), h * BLOCK:(h + 1) * BLOCK] for h in range(2)]
            t_idx = r0 + _iota2((BLOCK, KEYS), 0)

            def kchunk(c, carry):
                stats, acc = carry[:4], carry[4]
                s0, valid = _key_chunk(c, lp, t_idx, False)
                ps, new, alphas = [], [], []
                for h in range(2):
                    m, l = stats[2 * h], stats[2 * h + 1]
                    s = jnp.where(valid, _dot_nt(qs[h], k_ref[0, pl.ds(s0, KEYS), h * BLOCK:(h + 1) * BLOCK]) * scale, NEG)
                    m_new = jnp.maximum(m, jnp.max(s, axis=1, keepdims=True))
                    p = jnp.where(valid, jnp.exp(s - m_new), 0.0)
                    alpha = jnp.exp(m - m_new)
                    new += [m_new, alpha * l + jnp.sum(p, axis=1, keepdims=True)]
                    alphas.append(alpha)
                    ps.append(p.astype(BF16))
                pv = jnp.dot(jnp.concatenate(ps, axis=1), _stack_halves(v_ref[0, pl.ds(s0, KEYS), :], lo), preferred_element_type=F32)
                return (*new, _pair(alphas[0], alphas[1], lo) * acc + pv)

            neg, zero = jnp.full((BLOCK, 1), NEG, F32), jnp.zeros((BLOCK, 1), F32)
            m_a, l_a, m_b, l_b, acc = lax.fori_loop(0, _n_chunks(i), kchunk, (neg, zero, neg, zero, jnp.zeros((BLOCK, BLOCK), F32)))
            l = _pair(l_a, l_b, lo)
            seen = l > 0.0
            safe = jnp.where(seen, l, 1.0)
            o_ref[0, pl.ds(r0, BLOCK), :] = acc / safe
            lse_ref[0, pl.ds(r0, BLOCK), :] = jnp.where(seen, _pair(m_a, m_b, lo) + jnp.log(safe), 0.0)
            return 0

        lax.fori_loop(0, nb, qblock, 0)

    wide = pl.BlockSpec((1, lp, 2 * BLOCK), lambda bi, hp: (bi, 0, hp))
    thin = pl.BlockSpec((1, lp, 2 * HEAD), lambda bi, hp: (bi, 0, hp))
    shp = jax.ShapeDtypeStruct((b, lp, MLA_HEADS * HEAD), F32)
    return pl.pallas_call(
        body, grid=(b, npair), in_specs=[wide, wide, thin], out_specs=[thin, thin], out_shape=[shp, shp],
        compiler_params=_params(("parallel", "parallel")), name=name,
    )(qf, kf, v)


def _mla_bwd(qf, kf, v, o, lse, do, name):
    b, lp, _ = qf.shape
    nb = lp // BLOCK
    npair = MLA_HEADS // 2
    scale = (MLA_NOPE + MLA_ROPE) ** -0.5

    def body(q_ref, k_ref, v_ref, o_ref, lse_ref, do_ref, dq_ref, dk_ref, dv_ref):
        lo = _lo_lanes()
        dk_ref[...] = jnp.zeros_like(dk_ref)
        dv_ref[...] = jnp.zeros_like(dv_ref)

        def qblock(i, _):
            r0 = pl.multiple_of(i * BLOCK, BLOCK)
            rows = pl.ds(r0, BLOCK)
            qs = [q_ref[0, rows, h * BLOCK:(h + 1) * BLOCK] for h in range(2)]
            do_i = do_ref[0, rows, :]
            dos = _halves(do_i.astype(BF16), lo)
            both = do_i * o_ref[0, rows, :]
            d_a = jnp.sum(jnp.where(lo, both, 0.0), axis=1, keepdims=True)
            dsum = (d_a, jnp.sum(both, axis=1, keepdims=True) - d_a)
            lse_i = lse_ref[0, rows, :]
            lses = (lse_i[:, 0:1], lse_i[:, HEAD:HEAD + 1])
            t_idx = r0 + _iota2((BLOCK, KEYS), 0)

            def kchunk(c, dqs):
                s0, valid = _key_chunk(c, lp, t_idx, False)
                keys = pl.ds(s0, KEYS)
                v_c = v_ref[0, keys, :]
                ps, out = [], []
                for h in range(2):
                    lanes = slice(h * BLOCK, (h + 1) * BLOCK)
                    k = k_ref[0, keys, lanes]
                    s = jnp.where(valid, _dot_nt(qs[h], k) * scale, NEG)
                    p = jnp.where(valid, jnp.exp(s - lses[h]), 0.0)
                    ds = p * (_dot_nt(dos[h], v_c) - dsum[h]) * scale
                    dk_ref[0, keys, lanes] += _dot_tn(ds, qs[h])
                    out.append(dqs[h] + _dot(ds, k))
                    ps.append(p.astype(BF16))
                dv_ref[0, keys, :] += _dot_tn(jnp.concatenate(ps, axis=0), jnp.concatenate(dos, axis=0))
                return tuple(out)

            zero = jnp.zeros((BLOCK, BLOCK), F32)
            dq_a, dq_b = lax.fori_loop(0, _n_chunks(i), kchunk, (zero, zero))
            dq_ref[0, rows, 0:BLOCK] = dq_a
            dq_ref[0, rows, BLOCK:2 * BLOCK] = dq_b
            return 0

        lax.fori_loop(0, nb, qblock, 0)

    wide = pl.BlockSpec((1, lp, 2 * BLOCK), lambda bi, hp: (bi, 0, hp))
    thin = pl.BlockSpec((1, lp, 2 * HEAD), lambda bi, hp: (bi, 0, hp))
    return pl.pallas_call(
        body, grid=(b, npair), in_specs=[wide, wide, thin, thin, thin, thin], out_specs=[wide, wide, thin],
        out_shape=[jax.ShapeDtypeStruct(qf.shape, F32), jax.ShapeDtypeStruct(qf.shape, F32), jax.ShapeDtypeStruct(v.shape, F32)],
        compiler_params=_params(("parallel", "parallel")), name=name,
    )(qf, kf, v, o, lse, do)


def _swa_keys(k_ref, v_ref, n, kv, lo):
    prev = jnp.maximum(n - 1, 0)
    rows = lambda blk: pl.ds(pl.multiple_of(blk * BLOCK, BLOCK), BLOCK)
    mine = (_iota2((1, BLOCK), 1) >= HEAD).astype(jnp.int32) == kv

    def both_halves(ref):
        x = jnp.concatenate([ref[0, rows(prev), :], ref[0, rows(n), :], ref[0, 0:BLOCK, :]], axis=0)
        return jnp.where(mine, x, pltpu.roll(x, HEAD, 1)).astype(BF16)

    col = _iota2((BLOCK, 3 * BLOCK), 1)
    loc = col % BLOCK
    s_idx = jnp.where(col < BLOCK, (n - 1) * BLOCK + loc, jnp.where(col < 2 * BLOCK, n * BLOCK + loc, loc))
    dist = n * BLOCK + _iota2((BLOCK, 3 * BLOCK), 0) - s_idx
    band = (col < 2 * BLOCK) & (dist >= 0) & (dist < SWA_WINDOW) & (s_idx >= BLOCK)
    meta = (col >= 2 * BLOCK) & (s_idx >= N_PAD) & (dist >= 0)
    return both_halves(k_ref), both_halves(v_ref), band | meta, dist.astype(F32), prev


def _swa_probs(q_h, kdup, valid, dist, head, sink_ref):
    slope = jnp.exp(jnp.full((1, 1), -8.0 * math.log(2.0) / SWA_HEADS, F32) * (head + 1).astype(F32))
    s = jnp.where(valid, _dot_nt(q_h, kdup) * (HEAD ** -0.5) - slope * dist, NEG)
    sink = sink_ref[pl.ds(head, 1), 0:1]
    m = jnp.maximum(jnp.max(s, axis=1, keepdims=True), sink)
    e = jnp.where(valid, jnp.exp(s - m), 0.0)
    es = jnp.exp(sink - m)
    inv = 1.0 / (jnp.sum(e, axis=1, keepdims=True) + es)
    return e * inv, es * inv


SWA_PAIRS = SWA_HEADS // SWA_KV_HEADS // 2
SWA_GROUP = SWA_PAIRS * 2 * HEAD


def _swa_specs(b, lp):
    nb = lp // BLOCK
    qcol = lambda first: pl.BlockSpec((1, BLOCK, SWA_GROUP), lambda bi, kv, n: (bi, n, first // SWA_GROUP + kv))
    kcol = lambda first: pl.BlockSpec((1, lp, BLOCK), lambda bi, kv, n: (bi, 0, first // BLOCK))
    sink = pl.BlockSpec((SWA_HEADS, BLOCK), lambda bi, kv, n: (0, 0))
    return (b, SWA_KV_HEADS, nb), qcol, kcol, sink


def _swa_fwd(proj3, sinks, name):
    b, lp, _ = proj3.shape
    grid, qcol, kcol, sink = _swa_specs(b, lp)

    def body(q_ref, k_ref, v_ref, sink_ref, o_ref):
        kv, n = pl.program_id(1), pl.program_id(2)
        lo = _lo_lanes()
        kdup, vdup, valid, dist, _ = _swa_keys(k_ref, v_ref, n, kv, lo)
        v_bd = _stack_halves(vdup, lo)
        for p in range(SWA_PAIRS):
            lanes = slice(p * BLOCK, (p + 1) * BLOCK)
            qs = _halves(q_ref[0, :, lanes].astype(BF16), lo)
            probs = [_swa_probs(qs[hh], kdup, valid, dist, (kv * SWA_PAIRS + p) * 2 + hh, sink_ref)[0].astype(BF16) for hh in range(2)]
            o_ref[0, :, lanes] = jnp.dot(jnp.concatenate(probs, axis=1), v_bd, preferred_element_type=F32)

    return pl.pallas_call(
        body, grid=grid, in_specs=[qcol(OD_Q), kcol(OD_K), kcol(OD_V), sink], out_specs=qcol(0),
        out_shape=jax.ShapeDtypeStruct((b, lp, SWA_HEADS * HEAD), F32),
        compiler_params=_params(("parallel", "parallel", "parallel")), name=name,
    )(proj3, proj3, proj3, sinks)


def _swa_bwd(proj3, sinks, do, name):
    b, lp, _ = proj3.shape
    nb = lp // BLOCK
    grid, qcol, kcol, sink = _swa_specs(b, lp)

    def body(q_ref, k_ref, v_ref, sink_ref, do_ref, dq_ref, dk_ref, dv_ref, dsink_ref):
        kv, n = pl.program_id(1), pl.program_id(2)

        @pl.when((n == 0) & (pl.program_id(0) == 0) & (kv == 0))
        def _():
            dsink_ref[...] = jnp.zeros_like(dsink_ref)

        @pl.when(n == 0)
        def _():
            dk_ref[...] = jnp.zeros_like(dk_ref)
            dv_ref[...] = jnp.zeros_like(dv_ref)

        lo = _lo_lanes()
        kdup, vdup, valid, dist, prev = _swa_keys(k_ref, v_ref, n, kv, lo)
        k_bd = _stack_halves(kdup, lo)
        dkc = jnp.zeros((3 * BLOCK, BLOCK), F32)
        dvc = jnp.zeros((3 * BLOCK, BLOCK), F32)
        for p in range(SWA_PAIRS):
            lanes = slice(p * BLOCK, (p + 1) * BLOCK)
            qs = _halves(q_ref[0, :, lanes].astype(BF16), lo)
            dos = _halves(do_ref[0, :, lanes].astype(BF16), lo)
            dss, prs = [], []
            for hh in range(2):
                head = (kv * SWA_PAIRS + p) * 2 + hh
                pr, ps = _swa_probs(qs[hh], kdup, valid, dist, head, sink_ref)
                dp = _dot_nt(dos[hh], vdup)
                dsum = jnp.sum(pr * dp, axis=1, keepdims=True)
                dsink_ref[pl.ds(head, 1), :] += jnp.broadcast_to(-jnp.sum(ps * dsum, axis=0, keepdims=True), (1, BLOCK))
                dss.append((pr * (dp - dsum) * (HEAD ** -0.5)).astype(BF16))
                prs.append(pr.astype(BF16))
            dq_ref[0, :, lanes] = jnp.dot(jnp.concatenate(dss, axis=1), k_bd, preferred_element_type=F32)
            dkc = dkc + _dot_tn(jnp.concatenate(dss, axis=0), jnp.concatenate(qs, axis=0))
            dvc = dvc + _dot_tn(jnp.concatenate(prs, axis=0), jnp.concatenate(dos, axis=0))
        rows = lambda blk: pl.ds(pl.multiple_of(blk * BLOCK, BLOCK), BLOCK)
        for part, r in enumerate((rows(prev), rows(n), slice(0, BLOCK))):
            dk_ref[0, 0, r, :] += dkc[part * BLOCK:(part + 1) * BLOCK]
            dv_ref[0, 0, r, :] += dvc[part * BLOCK:(part + 1) * BLOCK]

        @pl.when(n == nb - 1)
        def _():
            for ref in (dk_ref, dv_ref):
                x = ref[0, 0]
                ref[0, 0] = x + pltpu.roll(x, HEAD, 1)

    kvout = pl.BlockSpec((1, 1, lp, BLOCK), lambda bi, kv, n: (bi, kv, 0, 0))
    kvshape = jax.ShapeDtypeStruct((b, SWA_KV_HEADS, lp, BLOCK), F32)
    return pl.pallas_call(
        body, grid=grid, in_specs=[qcol(OD_Q), kcol(OD_K), kcol(OD_V), sink, qcol(0)], out_specs=[qcol(0), kvout, kvout, sink],
        out_shape=[jax.ShapeDtypeStruct((b, lp, SWA_HEADS * HEAD), F32), kvshape, kvshape, jax.ShapeDtypeStruct((SWA_HEADS, BLOCK), F32)],
        compiler_params=_params(("arbitrary", "arbitrary", "arbitrary")), name=name,
    )(proj3, proj3, proj3, sinks, do)


def _kernel_weights(ev_w_in, ev_w_uq, ev_w_ukv, od_w_in):
    zeros = lambda r, c: jnp.zeros((r, c), ev_w_in.dtype)
    q_sb, k_sb, v_sb, g_sb, c_q, c_kv, k_r, g_mla = jnp.split(ev_w_in, [512, 1024, 1536, 2048, 2304, 2432, 2464], axis=1)
    w0 = jnp.concatenate([g_sb, g_mla, q_sb, k_sb, v_sb, c_q, c_kv, zeros(D_MODEL, MLA_NOPE), k_r, zeros(D_MODEL, 32)], axis=1)
    uq = ev_w_uq.reshape(MLA_Q_LORA, MLA_HEADS, MLA_NOPE + MLA_ROPE)
    wq = jnp.pad(uq, ((0, 0), (0, 0), (0, BLOCK - MLA_NOPE - MLA_ROPE))).reshape(MLA_Q_LORA, MLA_HEADS * BLOCK)
    ukv = ev_w_ukv.reshape(MLA_KV_LORA, MLA_HEADS, BLOCK)
    wk = jnp.pad(ukv[:, :, :MLA_NOPE], ((0, 0), (0, 0), (0, BLOCK - MLA_NOPE))).reshape(MLA_KV_LORA, MLA_HEADS * BLOCK)
    wv = ukv[:, :, MLA_NOPE:].reshape(MLA_KV_LORA, MLA_HEADS * HEAD)
    q, k, v, g = jnp.split(od_w_in, [1024, 1152, 1280], axis=1)
    w1 = jnp.concatenate([g, q, k, v], axis=1)
    return w0, wq, wk, wv, w1


def _original_grads(dw0, dwq, dwk, dwv, dw1):
    sl = lambda a, first, n: a[:, first:first + n]
    d_ev_w_in = jnp.concatenate([sl(dw0, EV_Q, 512), sl(dw0, EV_K, 512), sl(dw0, EV_V, 512), sl(dw0, EV_G, 512), sl(dw0, EV_CQ, 256),
                                 sl(dw0, EV_CKV, 128), sl(dw0, EV_KR + MLA_NOPE, MLA_ROPE), sl(dw0, EV_G + 512, 512)], axis=1)
    d_uq = dwq.reshape(MLA_Q_LORA, MLA_HEADS, BLOCK)[:, :, :MLA_NOPE + MLA_ROPE].reshape(MLA_Q_LORA, -1)
    d_ukv = jnp.concatenate([dwk.reshape(MLA_KV_LORA, MLA_HEADS, BLOCK)[:, :, :MLA_NOPE], dwv.reshape(MLA_KV_LORA, MLA_HEADS, HEAD)],
                            axis=2).reshape(MLA_KV_LORA, -1)
    d_od_w_in = jnp.concatenate([sl(dw1, OD_Q, 1024), sl(dw1, OD_K, 128), sl(dw1, OD_V, 128), sl(dw1, OD_G, 1024)], axis=1)
    return d_ev_w_in, d_uq, d_ukv, d_od_w_in


def _meta_rows_sum(dh0_3):
    b, _, d = dh0_3.shape

    def body(x_ref, o_ref):
        acc = x_ref[0, N_PAD:BLOCK, :]
        for i in range(1, b):
            acc = acc + x_ref[i, N_PAD:BLOCK, :]
        o_ref[...] = acc

    return pl.pallas_call(
        body, grid=(1,), in_specs=[pl.BlockSpec((b, BLOCK, d), lambda i: (0, 0, 0))], out_specs=pl.BlockSpec((N_META, d), lambda i: (0, 0)),
        out_shape=jax.ShapeDtypeStruct((N_META, d), F32), compiler_params=_params(("arbitrary",)), name="meta_rows_sum",
    )(dh0_3)


def _local_step(x, meta, norm_g, final_g, gq, gkv, sinks, target, ev_w_in, ev_w_uq, ev_w_ukv, wo0, od_w_in, wo1):
    b, seq, d = x.shape
    lp = seq + BLOCK
    t = b * lp
    w0, wq, wk, wv, w1 = _kernel_weights(ev_w_in, ev_w_uq, ev_w_ukv, od_w_in)
    h0 = jnp.concatenate([jnp.zeros((b, N_PAD, d), F32), jnp.broadcast_to(meta[None], (b, N_META, d)), x], axis=1).reshape(t, d)
    tabs = _rope_tables(lp)
    g0, g1 = norm_g[0:1], norm_g[1:2]

    hn0 = _rms_fwd(h0, g0, "norm0")
    proj0 = _mm(hn0, w0, "inproj0")
    p0 = proj0.reshape(b, lp, EV_N)
    o_sb, sb_tot = _sb_fwd(p0, "sb_fwd")
    qf, kf, v = _mla_prep_fwd(p0, gq, gkv, wq, wk, wv, tabs, "mla_prep_fwd")
    o_mla, lse = _mla_fwd(qf, kf, v, "mla_fwd")
    o0 = [o_sb.reshape(t, -1), o_mla.reshape(t, -1)]
    ao0 = _gate_fwd(o0, proj0, "gate0")
    h1 = _mm(ao0, wo0, "outproj0", res=h0)

    hn1 = _rms_fwd(h1, g1, "norm1")
    proj1 = _mm(hn1, w1, "inproj1")
    p1 = proj1.reshape(b, lp, OD_N)
    sinks_b = jnp.broadcast_to(sinks.reshape(SWA_HEADS, 1), (SWA_HEADS, BLOCK))
    o1 = _swa_fwd(p1, sinks_b, "swa_fwd").reshape(t, -1)
    ao1 = _gate_fwd([o1], proj1, "gate1")
    h2 = _mm(ao1, wo1, "outproj1", res=h1)

    dh2, d_final_g, loss = _loss_head(h2, final_g.reshape(1, d), target, b, lp)

    d_wo1 = _mm_tn(ao1, dh2, "d_wo1")
    dao1 = _mm_nt(dh2, wo1, "d_ao1")
    (do1,), dg1 = _gate_bwd(dao1, [o1], proj1, "gate1_bwd")
    dq1, dk4, dv4, d_sinks = _swa_bwd(p1, sinks_b, do1.reshape(b, lp, -1), "swa_bwd")
    unheads = lambda a: a[..., :HEAD].transpose(0, 2, 1, 3).reshape(t, SWA_KV_HEADS * HEAD).astype(BF16)
    dproj1 = jnp.concatenate([dg1, dq1.reshape(t, -1).astype(BF16), unheads(dk4), unheads(dv4)], axis=1)
    d_w1 = _mm_tn(hn1, dproj1, "d_w1")
    dhn1 = _mm_nt(dproj1, w1, "d_hn1")
    dh1, d_g1 = _rms_bwd(h1, g1, dhn1, dh2, "norm1_bwd")

    d_wo0 = _mm_tn(ao0, dh1, "d_wo0")
    dao0 = _mm_nt(dh1, wo0, "d_ao0")
    (do_sb, do_mla), dg0 = _gate_bwd(dao0, o0, proj0, "gate0_bwd")
    dq_sb, dk_sb, dv_sb = _sb_bwd(p0, sb_tot, do_sb.reshape(b, lp, -1), "sb_bwd")
    dqf, dkf, dv = _mla_bwd(qf, kf, v, o_mla, lse, do_mla.reshape(b, lp, -1), "mla_bwd")
    dcq, dckv, dkr, d_wq, d_wk, d_wv, d_gq, d_gkv = _mla_prep_bwd(p0, gq, gkv, wq, wk, wv, tabs, dqf, dkf, dv, "mla_prep_bwd")
    flat = lambda a: a.reshape(t, -1).astype(BF16)
    dproj0 = jnp.concatenate([dg0, flat(dq_sb), flat(dk_sb), flat(dv_sb), flat(dcq), flat(dckv), flat(dkr)], axis=1)
    d_w0 = _mm_tn(hn0, dproj0, "d_w0")
    dhn0 = _mm_nt(dproj0, w0, "d_hn0")
    dh0, d_g0 = _rms_bwd(h0, g0, dhn0, dh1, "norm0_bwd")
    dh0 = dh0.reshape(b, lp, d)

    d_ev_w_in, d_uq, d_ukv, d_od_w_in = _original_grads(d_w0, d_wq, d_wk, d_wv, d_w1)
    grads = dict(meta=_meta_rows_sum(dh0), norm_g=jnp.concatenate([d_g0, d_g1], axis=0), final_g=d_final_g.reshape(d),
                 ev_w_in=d_ev_w_in, ev_q_norm_g=d_gq, ev_kv_norm_g=d_gkv, ev_w_uq=d_uq, ev_w_ukv=d_ukv, ev_w_out=d_wo0,
                 od_w_in=d_od_w_in, od_sinks=d_sinks[:, 0].reshape(1, SWA_HEADS), od_w_out=d_wo1)
    return loss, dh0[:, BLOCK:], grads


MESH = pl.DeviceIdType.MESH
ANY = pl.BlockSpec(memory_space=pl.ANY)


def _place():
    return lax.axis_index("x"), lax.axis_index("y"), lax.axis_index("c")


def _other_chips(x, y):
    return [(1 - x, y), (x, 1 - y), (1 - x, 1 - y)]


PACK_ROWS = 1920
HALF_ROWS = PACK_ROWS // 2


def _gather_weights(wpack, meta, name):
    def body(w_ref, m_ref, wo_ref, mo_ref, send_sems, recv_sems, loc_sems):
        x, y, c = _place()
        me, sib = 2 * x + y, (x, y, 1 - c)
        chips = _other_chips(x, y)
        mine = w_ref.at[pl.ds(c * HALF_ROWS, HALF_ROWS), :]
        half = lambda chip, h: wo_ref.at[chip, pl.ds(h * HALF_ROWS, HALF_ROWS), :]

        def copy(k, src, dst, to):
            return pltpu.make_async_remote_copy(src_ref=src, dst_ref=dst, send_sem=send_sems.at[k], recv_sem=recv_sems.at[k], device_id=to,
                                                device_id_type=MESH)

        local = [pltpu.make_async_copy(w_ref, wo_ref.at[me], loc_sems.at[0]), pltpu.make_async_copy(m_ref, mo_ref.at[me], loc_sems.at[1])]
        sent = [copy(k, mine, half(me, c), (px, py, c)) for k, (px, py) in enumerate(chips)]
        sent += [copy(6 + k, m_ref, mo_ref.at[me], (px, py, c)) for k, (px, py) in enumerate(chips)]
        for cp in local + sent:
            cp.start()
        for k, (px, py) in enumerate(chips):
            landed = half(2 * px + py, c)
            copy(k, mine, landed, (px, py, c)).wait_recv()
            fwd = copy(3 + k, landed, landed, sib)
            fwd.start()
            sent.append(fwd)
        for k, (px, py) in enumerate(chips):
            other = half(2 * px + py, 1 - c)
            copy(3 + k, other, other, sib).wait_recv()
            copy(6 + k, m_ref, mo_ref.at[2 * px + py], (px, py, c)).wait_recv()
        for cp in sent:
            cp.wait_send()
        for cp in local:
            cp.wait()

    return pl.pallas_call(
        body, in_specs=[ANY, ANY], out_specs=[ANY, ANY],
        out_shape=[jax.ShapeDtypeStruct((N_CHIPS,) + wpack.shape, wpack.dtype), jax.ShapeDtypeStruct((N_CHIPS,) + meta.shape, meta.dtype)],
        scratch_shapes=[pltpu.SemaphoreType.DMA((9,)), pltpu.SemaphoreType.DMA((9,)), pltpu.SemaphoreType.DMA((2,))],
        name=name,
    )(wpack, meta)


def _grads_to_sibling(g, name):
    def body(g_ref, o_ref, send_sem, recv_sem):
        x, y, c = _place()
        cp = pltpu.make_async_remote_copy(src_ref=g_ref.at[:, pl.ds((1 - c) * HALF_ROWS, HALF_ROWS), :], dst_ref=o_ref, send_sem=send_sem,
                                          recv_sem=recv_sem, device_id=(x, y, 1 - c), device_id_type=MESH)
        cp.start()
        cp.wait()

    return pl.pallas_call(
        body, in_specs=[ANY], out_specs=ANY, out_shape=jax.ShapeDtypeStruct((g.shape[0], HALF_ROWS, g.shape[2]), g.dtype),
        scratch_shapes=[pltpu.SemaphoreType.DMA(()), pltpu.SemaphoreType.DMA(())],
        name=name,
    )(g)


def _share_halves(r, name):
    def body(r_ref, o_ref, send_sem, recv_sem, loc_sem):
        x, y, c = _place()
        mine = o_ref.at[pl.ds(c * HALF_ROWS, HALF_ROWS), :]
        loc = pltpu.make_async_copy(r_ref, mine, loc_sem)
        loc.start()
        out = pltpu.make_async_remote_copy(src_ref=r_ref, dst_ref=mine, send_sem=send_sem, recv_sem=recv_sem, device_id=(x, y, 1 - c),
                                           device_id_type=MESH)
        out.start()
        theirs = o_ref.at[pl.ds((1 - c) * HALF_ROWS, HALF_ROWS), :]
        pltpu.make_async_remote_copy(src_ref=r_ref, dst_ref=theirs, send_sem=send_sem, recv_sem=recv_sem, device_id=(x, y, 1 - c),
                                     device_id_type=MESH).wait_recv()
        out.wait_send()
        loc.wait()

    return pl.pallas_call(
        body, in_specs=[ANY], out_specs=ANY, out_shape=jax.ShapeDtypeStruct((PACK_ROWS, r.shape[1]), r.dtype),
        scratch_shapes=[pltpu.SemaphoreType.DMA(()), pltpu.SemaphoreType.DMA(()), pltpu.SemaphoreType.DMA(())],
        name=name,
    )(r)


def _chip_scatter(s, name):
    def body(s_ref, r_ref, send_sems, recv_sems, loc_sem):
        x, y, c = _place()
        me = 2 * x + y
        loc = pltpu.make_async_copy(s_ref.at[me], r_ref.at[me], loc_sem)
        loc.start()
        for k, (px, py) in enumerate(_other_chips(x, y)):
            pltpu.make_async_remote_copy(src_ref=s_ref.at[2 * px + py], dst_ref=r_ref.at[me], send_sem=send_sems.at[k],
                                         recv_sem=recv_sems.at[k], device_id=(px, py, c), device_id_type=MESH).start()
        for k, (px, py) in enumerate(_other_chips(x, y)):
            cp = pltpu.make_async_remote_copy(src_ref=s_ref.at[2 * px + py], dst_ref=r_ref.at[2 * px + py], send_sem=send_sems.at[k],
                                              recv_sem=recv_sems.at[k], device_id=(px, py, c), device_id_type=MESH)
            cp.wait_recv()
            cp.wait_send()
        loc.wait()

    return pl.pallas_call(
        body, in_specs=[ANY], out_specs=ANY, out_shape=jax.ShapeDtypeStruct(s.shape, s.dtype),
        scratch_shapes=[pltpu.SemaphoreType.DMA((3,)), pltpu.SemaphoreType.DMA((3,)), pltpu.SemaphoreType.DMA(())],
        name=name,
    )(s)


def _all_reduce_small(v, name):
    shape = v.shape

    def body(v_ref, o_ref, slots, send_sems, recv_sems):
        x, y, c = _place()
        me = 4 * x + 2 * y + c
        slots[me] = v_ref[...]
        for r in range(1, N_DEV):
            peer = (x ^ (r >> 2), y ^ ((r >> 1) & 1), c ^ (r & 1))
            pltpu.make_async_remote_copy(src_ref=v_ref, dst_ref=slots.at[me], send_sem=send_sems.at[r - 1], recv_sem=recv_sems.at[r - 1],
                                         device_id=peer, device_id_type=MESH).start()
        for r in range(1, N_DEV):
            peer = (x ^ (r >> 2), y ^ ((r >> 1) & 1), c ^ (r & 1))
            cp = pltpu.make_async_remote_copy(src_ref=v_ref, dst_ref=slots.at[4 * peer[0] + 2 * peer[1] + peer[2]], send_sem=send_sems.at[r - 1],
                                              recv_sem=recv_sems.at[r - 1], device_id=peer, device_id_type=MESH)
            cp.wait_recv()
            cp.wait_send()
        acc = slots[0]
        for d in range(1, N_DEV):
            acc = acc + slots[d]
        o_ref[...] = acc

    vm = pl.BlockSpec(memory_space=pltpu.VMEM)
    return pl.pallas_call(
        body, in_specs=[vm], out_specs=vm, out_shape=jax.ShapeDtypeStruct(shape, F32),
        scratch_shapes=[pltpu.VMEM((N_DEV,) + shape, F32), pltpu.SemaphoreType.DMA((N_DEV - 1,)), pltpu.SemaphoreType.DMA((N_DEV - 1,))],
        name=name,
    )(v)


SUM_ROWS = 480


def _add_sibling(g, gsib, core, name):
    n, _, cdim = g.shape
    per_half = HALF_ROWS // SUM_ROWS

    def body(core_ref, a_ref, b_ref, o_ref):
        o_ref[...] = (a_ref[...] + b_ref[...]).astype(o_ref.dtype)

    blk = pl.BlockSpec((1, SUM_ROWS, cdim), lambda j, i, core_ref: (j, i, 0))
    return pl.pallas_call(
        body,
        grid_spec=pltpu.PrefetchScalarGridSpec(
            num_scalar_prefetch=1, grid=(n, per_half),
            in_specs=[pl.BlockSpec((1, SUM_ROWS, cdim), lambda j, i, core_ref: (j, core_ref[0] * per_half + i, 0)), blk], out_specs=blk),
        out_shape=jax.ShapeDtypeStruct(gsib.shape, BF16), compiler_params=_params(("parallel", "parallel")), name=name,
    )(core, g, gsib)


def _sum_parts(parts, name):
    n, r, cdim = parts.shape

    def body(p_ref, o_ref):
        acc = p_ref[0].astype(F32)
        for j in range(1, n):
            acc = acc + p_ref[j].astype(F32)
        o_ref[...] = acc

    return pl.pallas_call(
        body, grid=(r // SUM_ROWS,), in_specs=[pl.BlockSpec((n, SUM_ROWS, cdim), lambda i: (0, i, 0))],
        out_specs=pl.BlockSpec((SUM_ROWS, cdim), lambda i: (i, 0)), out_shape=jax.ShapeDtypeStruct((r, cdim), F32),
        compiler_params=_params(("parallel",)), name=name,
    )(parts)


def _adamw(parts, w, m, v, name):
    npart, r, cdim = parts.shape
    tr = 384 if r % 384 == 0 else r

    def body(p_ref, w_ref, m_ref, v_ref, g_ref, d_ref, nm_ref, nv_ref):
        g = p_ref[0]
        for j in range(1, npart):
            g = g + p_ref[j]
        m_new = ADAM_B1 * m_ref[...] + (1.0 - ADAM_B1) * g
        v_new = ADAM_B2 * v_ref[...] + (1.0 - ADAM_B2) * (g * g)
        m_hat = m_new / (1.0 - ADAM_B1 ** ADAM_STEP)
        v_hat = v_new / (1.0 - ADAM_B2 ** ADAM_STEP)
        g_ref[...] = g
        d_ref[...] = -ADAM_LR * (m_hat / (jnp.sqrt(v_hat) + ADAM_EPS) + ADAM_WD * w_ref[...])
        nm_ref[...] = m_new
        nv_ref[...] = v_new

    blk = pl.BlockSpec((tr, cdim), lambda i: (i, 0))
    shp = jax.ShapeDtypeStruct((r, cdim), F32)
    return pl.pallas_call(
        body, grid=(r // tr,), in_specs=[pl.BlockSpec((npart, tr, cdim), lambda i: (0, i, 0)), blk, blk, blk], out_specs=[blk] * 4,
        out_shape=[shp] * 4, compiler_params=_params(("parallel",)), name=name,
    )(parts, w, m, v)


BIG = ("ev_w_in", "ev_w_uq", "ev_w_ukv", "ev_w_out", "od_w_in", "od_w_out", "meta")
SMALL = ("norm_g", "final_g", "ev_q_norm_g", "ev_kv_norm_g", "od_sinks")
SMALL_SHAPE = (8, 512)


def _pack_big(arrs):
    rows = [a.reshape(-1, D_MODEL) for a in arrs]
    used = sum(r.shape[0] for r in rows)
    return jnp.pad(jnp.concatenate(rows, axis=0), ((0, PACK_ROWS - used), (0, 0)))


def _unpack_big(p, shapes):
    out, at = [], 0
    for s in shapes:
        nrow = int(np.prod(s)) // D_MODEL
        out.append(p[at:at + nrow].reshape(s))
        at += nrow
    return out


def _pack_small(arrs, extra=None):
    flat = [a.reshape(-1) for a in arrs] + ([] if extra is None else [extra.reshape(-1)])
    used = sum(f.shape[0] for f in flat)
    return jnp.pad(jnp.concatenate(flat), (0, SMALL_SHAPE[0] * SMALL_SHAPE[1] - used)).reshape(SMALL_SHAPE)


def _unpack_small(p, shapes):
    flat, out, at = p.reshape(-1), [], 0
    for s in shapes:
        n = int(np.prod(s))
        out.append(flat[at:at + n].reshape(s))
        at += n
    return out, flat[at]


def _shard_of(full, name, j):
    if name in ("ev_w_out", "od_w_out"):
        n = full.shape[0] // N_CHIPS
        return full[j * n:(j + 1) * n]
    n = full.shape[1] // N_CHIPS
    return full[:, j * n:(j + 1) * n]


def kernel(x, meta, norm_g, final_g, ev_w_in, ev_q_norm_g, ev_kv_norm_g, ev_w_uq, ev_w_ukv, ev_w_out, od_w_in, od_sinks, od_w_out, loss_target, m_meta, m_norm_g, m_final_g, m_ev_w_in, m_ev_q_norm_g, m_ev_kv_norm_g, m_ev_w_uq, m_ev_w_ukv, m_ev_w_out, m_od_w_in, m_od_sinks, m_od_w_out, v_meta, v_norm_g, v_final_g, v_ev_w_in, v_ev_q_norm_g, v_ev_kv_norm_g, v_ev_w_uq, v_ev_w_ukv, v_ev_w_out, v_od_w_in, v_od_sinks, v_od_w_out):
    given = dict(locals())
    big_w = [given[n][0] if given[n].ndim == 3 else given[n] for n in BIG]
    big_shapes = [given[n].shape for n in BIG]

    wpack = _pack_big(big_w[:-1]).astype(BF16)
    wall, meta_all = _gather_weights(wpack, meta, "gather_weights")
    per_chip = [_unpack_big(wall[j], [a.shape for a in big_w[:-1]]) for j in range(N_CHIPS)]
    cat = lambda i, axis: jnp.concatenate([per_chip[j][i] for j in range(N_CHIPS)], axis=axis)
    full = dict(ev_w_in=cat(0, 1), ev_w_uq=cat(1, 1), ev_w_ukv=cat(2, 1), ev_w_out=cat(3, 0), od_w_in=cat(4, 1), od_w_out=cat(5, 0))
    meta_full = jnp.concatenate([meta_all[j] for j in range(N_CHIPS)], axis=1)

    loss, grad_x, grads = _local_step(x, meta_full, norm_g, final_g, ev_q_norm_g, ev_kv_norm_g, od_sinks, loss_target,
                                      full["ev_w_in"], full["ev_w_uq"], full["ev_w_ukv"], full["ev_w_out"], full["od_w_in"], full["od_w_out"])

    gpack = jnp.stack([_pack_big([_shard_of(grads[n], n, j) for n in BIG]) for j in range(N_CHIPS)])
    core = lax.axis_index("c").astype(jnp.int32).reshape(1)
    gsum = _add_sibling(gpack, _grads_to_sibling(gpack, "grads_to_sibling"), core, "add_sibling")
    reduced = _share_halves(_sum_parts(_chip_scatter(gsum, "grads_to_chips"), "add_chips"), "reduced_to_sibling")
    big_out = _adamw(reduced[None], _pack_big(big_w), _pack_big([given["m_" + n] for n in BIG]), _pack_big([given["v_" + n] for n in BIG]),
                     "adamw_matrices")
    big_out = [_unpack_big(o, big_shapes) for o in big_out]

    small_shapes = [given[n].shape for n in SMALL]
    ssum = _all_reduce_small(_pack_small([grads[n] for n in SMALL], loss[0, 0]), "reduce_vectors")
    small_out = _adamw(ssum[None], _pack_small([given[n] for n in SMALL]), _pack_small([given["m_" + n] for n in SMALL]),
                       _pack_small([given["v_" + n] for n in SMALL]), "adamw_vectors")
    total_loss = ssum.reshape(-1)[sum(int(np.prod(s)) for s in small_shapes)]
    small_out = [_unpack_small(o, small_shapes)[0] for o in small_out]

    names = ("meta", "norm_g", "final_g", "ev_w_in", "ev_q_norm_g", "ev_kv_norm_g", "ev_w_uq", "ev_w_ukv", "ev_w_out", "od_w_in", "od_sinks",
             "od_w_out")
    outs = [total_loss, grad_x]
    for kind in range(4):
        for n in names:
            outs.append(big_out[kind][BIG.index(n)] if n in BIG else small_out[kind][SMALL.index(n)])
    return tuple(outs)
```

```python
import functools
import math

import numpy as np
import jax
import jax.numpy as jnp
from jax import lax
from jax.experimental import pallas as pl
from jax.experimental.pallas import tpu as pltpu

F32 = jnp.float32
BF16 = jnp.bfloat16

D_MODEL = 1024
BLOCK = 128
N_META = 16
N_PAD = BLOCK - N_META
NORM_EPS = 1e-6
NEG = -1e30
HEAD = 64
SB_HEADS = 8
MLA_HEADS = 8
MLA_Q_LORA = 256
MLA_KV_LORA = 128
MLA_NOPE = 64
MLA_ROPE = 32
ROPE_BASE = 10000.0
SWA_HEADS = 16
SWA_KV_HEADS = 2
SWA_WINDOW = 128
N_CHIPS = 4
N_DEV = 8

ADAM_LR = 0.001
ADAM_B1 = 0.9
ADAM_B2 = 0.999
ADAM_EPS = 1e-08
ADAM_WD = 0.01
ADAM_STEP = 10

VMEM_LIMIT = 48 * 1024 * 1024

EV_G, EV_Q, EV_K, EV_V, EV_CQ, EV_CKV, EV_KR, EV_N = 0, 1024, 1536, 2048, 2560, 2816, 2944, 3072
OD_G, OD_Q, OD_K, OD_V, OD_N = 0, 1024, 2048, 2176, 2304


def _params(sem=None):
    return pltpu.CompilerParams(dimension_semantics=sem, vmem_limit_bytes=VMEM_LIMIT)


def _row_tile(m):
    return 256 if m % 256 == 0 else 128


def _dot(a, b):
    return jnp.dot(a.astype(BF16), b.astype(BF16), preferred_element_type=F32)


def _dot_nt(a, b):
    return lax.dot_general(a.astype(BF16), b.astype(BF16), (((1,), (1,)), ((), ())), preferred_element_type=F32)


def _dot_tn(a, b):
    return lax.dot_general(a.astype(BF16), b.astype(BF16), (((0,), (0,)), ((), ())), preferred_element_type=F32)


def _rms_fwd(h, g, name):
    t, d = h.shape
    tm = _row_tile(t)

    def body(h_ref, g_ref, o_ref):
        x = h_ref[...]
        r = lax.rsqrt(jnp.mean(x * x, axis=-1, keepdims=True) + NORM_EPS)
        o_ref[...] = ((x * r) * g_ref[...]).astype(o_ref.dtype)

    return pl.pallas_call(
        body, grid=(t // tm,),
        in_specs=[pl.BlockSpec((tm, d), lambda i: (i, 0)), pl.BlockSpec((1, d), lambda i: (0, 0))],
        out_specs=pl.BlockSpec((tm, d), lambda i: (i, 0)),
        out_shape=jax.ShapeDtypeStruct((t, d), BF16), compiler_params=_params(("parallel",)), name=name,
    )(h, g)


def _rms_bwd(h, g, dy, dres, name):
    t, d = h.shape
    tm = _row_tile(t)

    def body(h_ref, g_ref, dy_ref, dres_ref, dh_ref, dg_ref):
        @pl.when(pl.program_id(0) == 0)
        def _():
            dg_ref[...] = jnp.zeros_like(dg_ref)

        x = h_ref[...]
        r = lax.rsqrt(jnp.mean(x * x, axis=-1, keepdims=True) + NORM_EPS)
        xr = x * r
        dy_ = dy_ref[...]
        u = dy_ * g_ref[...]
        dh_ref[...] = dres_ref[...] + r * (u - xr * jnp.mean(u * xr, axis=-1, keepdims=True))
        dg_ref[...] += jnp.sum(dy_ * xr, axis=0, keepdims=True)

    row = pl.BlockSpec((tm, d), lambda i: (i, 0))
    vec = pl.BlockSpec((1, d), lambda i: (0, 0))
    return pl.pallas_call(
        body, grid=(t // tm,), in_specs=[row, vec, row, row], out_specs=[row, vec],
        out_shape=[jax.ShapeDtypeStruct((t, d), F32), jax.ShapeDtypeStruct((1, d), F32)],
        compiler_params=_params(("arbitrary",)), name=name,
    )(h, g, dy, dres)


def _col_tile(n):
    for c in (1024, 768, 640, 512, 384, 256, 128):
        if n % c == 0:
            return c
    return n


def _mm(a, w, name, res=None, out_dtype=F32, a_cols=None):
    m = a.shape[0]
    k, n = w.shape
    a_blk = 0 if a_cols is None else a_cols[0] // k
    assert a_cols is None or (a_cols[1] == k and a_cols[0] % k == 0)
    tm, tn = _row_tile(m), _col_tile(n)

    def body(*refs):
        if res is None:
            a_ref, w_ref, o_ref = refs
            acc = _dot(a_ref[...], w_ref[...])
        else:
            a_ref, w_ref, r_ref, o_ref = refs
            acc = r_ref[...] + _dot(a_ref[...], w_ref[...])
        o_ref[...] = acc.astype(o_ref.dtype)

    in_specs = [pl.BlockSpec((tm, k), lambda j, i: (i, a_blk)), pl.BlockSpec((k, tn), lambda j, i: (0, j))]
    args = [a, w]
    if res is not None:
        in_specs.append(pl.BlockSpec((tm, tn), lambda j, i: (i, j)))
        args.append(res)
    return pl.pallas_call(
        body, grid=(n // tn, m // tm), in_specs=in_specs, out_specs=pl.BlockSpec((tm, tn), lambda j, i: (i, j)),
        out_shape=jax.ShapeDtypeStruct((m, n), out_dtype), compiler_params=_params(("parallel", "parallel")), name=name,
    )(*args)


def _mm_nt(a, w, name):
    m, n = a.shape
    k = w.shape[0]
    tm, tk = _row_tile(m), _col_tile(k)

    def body(a_ref, w_ref, o_ref):
        o_ref[...] = _dot_nt(a_ref[...], w_ref[...])

    return pl.pallas_call(
        body, grid=(k // tk, m // tm),
        in_specs=[pl.BlockSpec((tm, n), lambda j, i: (i, 0)), pl.BlockSpec((tk, n), lambda j, i: (j, 0))],
        out_specs=pl.BlockSpec((tm, tk), lambda j, i: (i, j)),
        out_shape=jax.ShapeDtypeStruct((m, k), F32), compiler_params=_params(("parallel", "parallel")), name=name,
    )(a, w)


def _mm_tn(x, dy, name):
    m, k = x.shape
    n = dy.shape[1]
    tm, tn = _row_tile(m), _col_tile(n)

    def body(x_ref, dy_ref, o_ref):
        @pl.when(pl.program_id(1) == 0)
        def _():
            o_ref[...] = jnp.zeros_like(o_ref)

        o_ref[...] += _dot_tn(x_ref[...], dy_ref[...])

    return pl.pallas_call(
        body, grid=(n // tn, m // tm),
        in_specs=[pl.BlockSpec((tm, k), lambda j, i: (i, 0)), pl.BlockSpec((tm, tn), lambda j, i: (i, j))],
        out_specs=pl.BlockSpec((k, tn), lambda j, i: (0, j)),
        out_shape=jax.ShapeDtypeStruct((k, n), F32), compiler_params=_params(("parallel", "arbitrary")), name=name,
    )(x, dy)


def _silu_parts(g):
    s = 1.0 / (1.0 + jnp.exp(-g))
    return g * s, s * (1.0 + g * (1.0 - s))


def _gate_fwd(o_parts, proj, name):
    t = proj.shape[0]
    tm = _row_tile(t)
    w = D_MODEL // len(o_parts)

    def body(*refs):
        g_ref, o_ref = refs[-2], refs[-1]
        for p, r in enumerate(refs[:-2]):
            sil, _ = _silu_parts(g_ref[:, p * w:(p + 1) * w])
            o_ref[:, p * w:(p + 1) * w] = (r[...].astype(F32) * sil).astype(o_ref.dtype)

    return pl.pallas_call(
        body, grid=(t // tm,),
        in_specs=[pl.BlockSpec((tm, w), lambda i: (i, 0)) for _ in o_parts] + [pl.BlockSpec((tm, D_MODEL), lambda i: (i, 0))],
        out_specs=pl.BlockSpec((tm, D_MODEL), lambda i: (i, 0)),
        out_shape=jax.ShapeDtypeStruct((t, D_MODEL), BF16), compiler_params=_params(("parallel",)), name=name,
    )(*o_parts, proj)


def _gate_bwd(dao, o_parts, proj, name):
    t = proj.shape[0]
    tm = _row_tile(t)
    np_ = len(o_parts)
    w = D_MODEL // np_

    def body(*refs):
        dao_ref, g_ref = refs[0], refs[1 + np_]
        do_refs, dg_ref = refs[2 + np_:2 + 2 * np_], refs[-1]
        for p in range(np_):
            sl = slice(p * w, (p + 1) * w)
            sil, dsil = _silu_parts(g_ref[:, sl])
            da = dao_ref[:, sl]
            do_refs[p][...] = da * sil
            dg_ref[:, sl] = (da * refs[1 + p][...].astype(F32) * dsil).astype(dg_ref.dtype)

    full = pl.BlockSpec((tm, D_MODEL), lambda i: (i, 0))
    part = pl.BlockSpec((tm, w), lambda i: (i, 0))
    outs = pl.pallas_call(
        body, grid=(t // tm,), in_specs=[full] + [part] * np_ + [full], out_specs=[part] * np_ + [full],
        out_shape=[jax.ShapeDtypeStruct((t, w), F32)] * np_ + [jax.ShapeDtypeStruct((t, D_MODEL), BF16)],
        compiler_params=_params(("parallel",)), name=name,
    )(dao, *o_parts, proj)
    return outs[:np_], outs[np_]


def _loss_head(h2, gf, target, b, lp):
    d = h2.shape[1]
    nb = lp // BLOCK
    h3 = h2.reshape(b, lp, d)

    def body(h_ref, g_ref, t_ref, dh_ref, dg_ref, loss_ref):
        first = (pl.program_id(0) == 0) & (pl.program_id(1) == 0)

        @pl.when(first)
        def _():
            dg_ref[...] = jnp.zeros_like(dg_ref)
            loss_ref[...] = jnp.zeros_like(loss_ref)

        @pl.when(pl.program_id(1) == 0)
        def _():
            dh_ref[...] = jnp.zeros_like(dh_ref)

        @pl.when(pl.program_id(1) > 0)
        def _():
            x = h_ref[0]
            r = lax.rsqrt(jnp.mean(x * x, axis=-1, keepdims=True) + NORM_EPS)
            xr = x * r
            g = g_ref[...]
            diff = xr * g - t_ref[0]
            loss_ref[...] += 0.5 * jnp.sum(jnp.mean(diff * diff, axis=-1, keepdims=True))
            dy = diff * (1.0 / d)
            u = dy * g
            dh_ref[0] = r * (u - xr * jnp.mean(u * xr, axis=-1, keepdims=True))
            dg_ref[...] += jnp.sum(dy * xr, axis=0, keepdims=True)

    blk = pl.BlockSpec((1, BLOCK, d), lambda bi, n: (bi, n, 0))
    dh, dg, loss = pl.pallas_call(
        body, grid=(b, nb),
        in_specs=[blk, pl.BlockSpec((1, d), lambda bi, n: (0, 0)),
                  pl.BlockSpec((1, BLOCK, d), lambda bi, n: (bi, jnp.maximum(n - 1, 0), 0))],
        out_specs=[blk, pl.BlockSpec((1, d), lambda bi, n: (0, 0)), pl.BlockSpec((8, 128), lambda bi, n: (0, 0))],
        out_shape=[jax.ShapeDtypeStruct((b, lp, d), F32), jax.ShapeDtypeStruct((1, d), F32), jax.ShapeDtypeStruct((8, 128), F32)],
        compiler_params=_params(("arbitrary", "arbitrary")), name="loss_head",
    )(h3, gf, target)
    return dh.reshape(b * lp, d), dg, loss


def _iota2(shape, dim):
    return lax.broadcasted_iota(jnp.int32, shape, dim)


def _split_dot(x, tri):
    hi = x.astype(BF16)
    lo = (x - hi.astype(F32)).astype(BF16)
    return jnp.dot(hi, tri, preferred_element_type=F32) + jnp.dot(lo, tri, preferred_element_type=F32)


KEYS = 256


def _lo_lanes():
    return _iota2((1, BLOCK), 1) < HEAD


def _halves(x, lo):
    zero = jnp.zeros_like(x)
    return jnp.where(lo, x, zero), jnp.where(lo, zero, x)


def _stack_halves(x, lo):
    a, b = _halves(x, lo)
    return jnp.concatenate([a, b], axis=0)


def _pair(a, b, lo):
    return jnp.where(lo, a, b)


def _key_chunk(c, lp, t_idx, strict):
    first = c * KEYS
    s0 = pl.multiple_of(jnp.minimum(first, lp - KEYS), BLOCK)
    s_idx = s0 + _iota2((BLOCK, KEYS), 1)
    seen = (s_idx < t_idx) if strict else (s_idx <= t_idx)
    return s0, seen & (s_idx >= jnp.maximum(first, N_PAD))


def _n_chunks(i):
    return (i + 2) // 2


def _sb_scores(q_h, k, valid, after):
    z = _dot_nt(q_h, k) * (HEAD ** -0.5)
    sp = jnp.log(1.0 + jnp.exp(-jnp.abs(z)))
    lb = jnp.minimum(z, 0.0) - sp
    l1m_all = -jnp.maximum(z, 0.0) - sp
    l1m = jnp.where(valid, l1m_all, 0.0)
    return lb, l1m_all, l1m, _split_dot(l1m, after)


def _sb_fwd(proj3, name):
    b, lp, _ = proj3.shape
    nb = lp // BLOCK
    npair = SB_HEADS // 2

    def body(q_ref, k_ref, v_ref, o_ref, tot_ref):
        lo = _lo_lanes()
        after = (_iota2((KEYS, KEYS), 0) > _iota2((KEYS, KEYS), 1)).astype(BF16)

        def qblock(i, _):
            r0 = pl.multiple_of(i * BLOCK, BLOCK)
            qs = _halves(q_ref[0, pl.ds(r0, BLOCK), :].astype(BF16), lo)
            t_idx = r0 + _iota2((BLOCK, KEYS), 0)
            n = _n_chunks(i)

            def kchunk(cc, carry):
                cs, acc = carry[:2], carry[2]
                s0, valid = _key_chunk(n - 1 - cc, lp, t_idx, True)
                k = k_ref[0, pl.ds(s0, KEYS), :].astype(BF16)
                a_s, new = [], []
                for h in range(2):
                    lb, _, l1m, suf = _sb_scores(qs[h], k, valid, after)
                    a_s.append(jnp.where(valid, jnp.exp(lb + suf + cs[h]), 0.0).astype(BF16))
                    new.append(cs[h] + jnp.sum(l1m, axis=1, keepdims=True))
                v_bd = _stack_halves(v_ref[0, pl.ds(s0, KEYS), :].astype(BF16), lo)
                return (*new, acc + jnp.dot(jnp.concatenate(a_s, axis=1), v_bd, preferred_element_type=F32))

            zero = jnp.zeros((BLOCK, 1), F32)
            c_a, c_b, acc = lax.fori_loop(0, n, kchunk, (zero, zero, jnp.zeros((BLOCK, BLOCK), F32)))
            o_ref[0, pl.ds(r0, BLOCK), :] = acc
            tot_ref[0, pl.ds(r0, BLOCK), :] = jnp.broadcast_to(_pair(c_a, c_b, lo), (BLOCK, BLOCK))
            return 0

        lax.fori_loop(0, nb, qblock, 0)

    def col(first):
        return pl.BlockSpec((1, lp, 2 * HEAD), lambda bi, hp: (bi, 0, first // (2 * HEAD) + hp))

    shp = jax.ShapeDtypeStruct((b, lp, SB_HEADS * HEAD), F32)
    return pl.pallas_call(
        body, grid=(b, npair), in_specs=[col(EV_Q), col(EV_K), col(EV_V)], out_specs=[col(0), col(0)], out_shape=[shp, shp],
        compiler_params=_params(("parallel", "parallel")), name=name,
    )(proj3, proj3, proj3)


def _sb_bwd(proj3, tot, do, name):
    b, lp, _ = proj3.shape
    nb = lp // BLOCK
    npair = SB_HEADS // 2

    def body(q_ref, k_ref, v_ref, tot_ref, do_ref, dq_ref, dk_ref, dv_ref):
        lo = _lo_lanes()
        after = (_iota2((KEYS, KEYS), 0) > _iota2((KEYS, KEYS), 1)).astype(BF16)
        before = (_iota2((KEYS, KEYS), 0) < _iota2((KEYS, KEYS), 1)).astype(BF16)
        dk_ref[...] = jnp.zeros_like(dk_ref)
        dv_ref[...] = jnp.zeros_like(dv_ref)

        def qblock(i, _):
            r0 = pl.multiple_of(i * BLOCK, BLOCK)
            rows = pl.ds(r0, BLOCK)
            qs = _halves(q_ref[0, rows, :].astype(BF16), lo)
            dos = _halves(do_ref[0, rows, :].astype(BF16), lo)
            tot_i = tot_ref[0, rows, :]
            tots = (tot_i[:, 0:1], tot_i[:, HEAD:HEAD + 1])
            q_st, do_st = jnp.concatenate(qs, axis=0), jnp.concatenate(dos, axis=0)
            t_idx = r0 + _iota2((BLOCK, KEYS), 0)

            def kchunk(c, carry):
                s0, valid = _key_chunk(c, lp, t_idx, True)
                keys = pl.ds(s0, KEYS)
                k = k_ref[0, keys, :].astype(BF16)
                v = v_ref[0, keys, :].astype(BF16)
                a_s, dzs, new = [], [], []
                for h in range(2):
                    left, pre = carry[2 * h], carry[2 * h + 1]
                    lb, l1m_all, l1m, suf = _sb_scores(qs[h], k, valid, after)
                    here = jnp.sum(l1m, axis=1, keepdims=True)
                    a = jnp.where(valid, jnp.exp(lb + suf + (tots[h] - left - here)), 0.0)
                    w = a * _dot_nt(dos[h], v)
                    dz = jnp.where(valid, w * jnp.exp(l1m_all) - (pre + _split_dot(w, before)) * jnp.exp(lb), 0.0) * (HEAD ** -0.5)
                    new += [left + here, pre + jnp.sum(w, axis=1, keepdims=True)]
                    a_s.append(a.astype(BF16))
                    dzs.append(dz.astype(BF16))
                dk_ref[0, keys, :] += _dot_tn(jnp.concatenate(dzs, axis=0), q_st)
                dv_ref[0, keys, :] += _dot_tn(jnp.concatenate(a_s, axis=0), do_st)
                dq = carry[4] + jnp.dot(jnp.concatenate(dzs, axis=1), _stack_halves(k, lo), preferred_element_type=F32)
                return (*new, dq)

            zero = jnp.zeros((BLOCK, 1), F32)
            out = lax.fori_loop(0, _n_chunks(i), kchunk, (zero, zero, zero, zero, jnp.zeros((BLOCK, BLOCK), F32)))
            dq_ref[0, rows, :] = out[4]
            return 0

        lax.fori_loop(0, nb, qblock, 0)

    def col(first):
        return pl.BlockSpec((1, lp, 2 * HEAD), lambda bi, hp: (bi, 0, first // (2 * HEAD) + hp))

    shp = jax.ShapeDtypeStruct((b, lp, SB_HEADS * HEAD), F32)
    return pl.pallas_call(
        body, grid=(b, npair), in_specs=[col(EV_Q), col(EV_K), col(EV_V), col(0), col(0)], out_specs=[col(0)] * 3, out_shape=[shp] * 3,
        compiler_params=_params(("parallel", "parallel")), name=name,
    )(proj3, proj3, proj3, tot, do)


def _rope_tables(lp):
    half = MLA_ROPE // 2
    pos = (np.arange(lp) - N_PAD).astype(np.float32)
    inv = jnp.asarray(ROPE_BASE, F32) ** (-jnp.arange(half, dtype=F32) / half)
    ang = jnp.asarray(pos)[:, None] * inv[None, :]
    cos, sin = jnp.cos(ang), jnp.sin(ang)
    zeros = lambda n: jnp.zeros((lp, n), F32)
    c = jnp.concatenate([jnp.ones((lp, MLA_NOPE), F32), cos, cos, zeros(32)], axis=1)
    s1 = jnp.concatenate([zeros(MLA_NOPE), -sin, zeros(half), zeros(32)], axis=1)
    s2 = jnp.concatenate([zeros(MLA_NOPE), zeros(half), sin, zeros(32)], axis=1)
    return c, s1, s2


def _rope(x, c, s1, s2):
    half = MLA_ROPE // 2
    return x * c + pltpu.roll(x, BLOCK - half, 1) * s1 + pltpu.roll(x, half, 1) * s2


def _rope_t(dy, c, s1, s2):
    half = MLA_ROPE // 2
    return dy * c + pltpu.roll(dy * s1, half, 1) + pltpu.roll(dy * s2, BLOCK - half, 1)


def _rms_rows(x, g):
    r = lax.rsqrt(jnp.mean(x * x, axis=-1, keepdims=True) + NORM_EPS)
    return x * r, r


def _mla_prep_fwd(proj3, gq, gkv, wq, wk, wv, tabs, name):
    b, lp, _ = proj3.shape
    nb = lp // BLOCK
    hw = MLA_HEADS * BLOCK

    def body(cq_ref, ckv_ref, kr_ref, gq_ref, gkv_ref, wq_ref, wk_ref, wv_ref, c_ref, s1_ref, s2_ref, qf_ref, kf_ref, v_ref):
        c, s1, s2 = c_ref[...], s1_ref[...], s2_ref[...]
        xq, _ = _rms_rows(cq_ref[0], None)
        qh = _dot(xq * gq_ref[...], wq_ref[...])
        xk, _ = _rms_rows(ckv_ref[0], None)
        ckv_n = xk * gkv_ref[...]
        kv = _dot(ckv_n, wk_ref[...])
        v_ref[0] = _dot(ckv_n, wv_ref[...]).astype(v_ref.dtype)
        kr = _rope(kr_ref[0], c, s1, s2)
        for h in range(MLA_HEADS):
            ls = slice(h * BLOCK, (h + 1) * BLOCK)
            qf_ref[0, :, ls] = _rope(qh[:, ls], c, s1, s2).astype(qf_ref.dtype)
            kf_ref[0, :, ls] = (kv[:, ls] + kr).astype(kf_ref.dtype)

    def col(first, width):
        return pl.BlockSpec((1, BLOCK, width), lambda bi, n: (bi, n, first // width))

    def whole(a):
        return pl.BlockSpec(a.shape, lambda bi, n: (0,) * a.ndim)

    tab = pl.BlockSpec((BLOCK, BLOCK), lambda bi, n: (n, 0))
    return pl.pallas_call(
        body, grid=(b, nb),
        in_specs=[col(EV_CQ, MLA_Q_LORA), col(EV_CKV, MLA_KV_LORA), col(EV_KR, BLOCK), whole(gq), whole(gkv), whole(wq), whole(wk),
                  whole(wv), tab, tab, tab],
        out_specs=[col(0, hw), col(0, hw), col(0, MLA_HEADS * HEAD)],
        out_shape=[jax.ShapeDtypeStruct((b, lp, hw), BF16), jax.ShapeDtypeStruct((b, lp, hw), BF16),
                   jax.ShapeDtypeStruct((b, lp, MLA_HEADS * HEAD), BF16)],
        compiler_params=_params(("parallel", "parallel")), name=name,
    )(proj3, proj3, proj3, gq, gkv, wq, wk, wv, *tabs)


def _mla_prep_bwd(proj3, gq, gkv, wq, wk, wv, tabs, dqf, dkf, dv, name):
    b, lp, _ = proj3.shape
    nb = lp // BLOCK
    hw = MLA_HEADS * BLOCK

    def body(cq_ref, ckv_ref, gq_ref, gkv_ref, wq_ref, wk_ref, wv_ref, c_ref, s1_ref, s2_ref, dqf_ref, dkf_ref, dv_ref,
             dcq_ref, dckv_ref, dkr_ref, dwq_ref, dwk_ref, dwv_ref, dgq_ref, dgkv_ref, dqh):
        @pl.when((pl.program_id(0) == 0) & (pl.program_id(1) == 0))
        def _():
            for r in (dwq_ref, dwk_ref, dwv_ref, dgq_ref, dgkv_ref):
                r[...] = jnp.zeros_like(r)

        c, s1, s2 = c_ref[...], s1_ref[...], s2_ref[...]
        dkr = jnp.zeros((BLOCK, BLOCK), F32)
        for h in range(MLA_HEADS):
            ls = slice(h * BLOCK, (h + 1) * BLOCK)
            dqh[:, ls] = _rope_t(dqf_ref[0, :, ls].astype(F32), c, s1, s2).astype(dqh.dtype)
            dkr = dkr + dkf_ref[0, :, ls].astype(F32)
        dkr_ref[0] = _rope_t(dkr, c, s1, s2).astype(dkr_ref.dtype)

        def norm_bwd(x, g, dy, dg_ref):
            xr, r = _rms_rows(x, None)
            u = dy * g
            dg_ref[...] += jnp.sum(dy * xr, axis=0, keepdims=True)
            return r * (u - xr * jnp.mean(u * xr, axis=-1, keepdims=True))

        xq, _ = _rms_rows(cq_ref[0], None)
        cq_n = xq * gq_ref[...]
        dwq_ref[...] += _dot_tn(cq_n, dqh[...])
        dcq_ref[0] = norm_bwd(cq_ref[0], gq_ref[...], _dot_nt(dqh[...], wq_ref[...]), dgq_ref).astype(dcq_ref.dtype)
        xk, _ = _rms_rows(ckv_ref[0], None)
        ckv_n = xk * gkv_ref[...]
        dkf_, dv_ = dkf_ref[0], dv_ref[0]
        dwk_ref[...] += _dot_tn(ckv_n, dkf_)
        dwv_ref[...] += _dot_tn(ckv_n, dv_)
        dckv_n = _dot_nt(dkf_, wk_ref[...]) + _dot_nt(dv_, wv_ref[...])
        dckv_ref[0] = norm_bwd(ckv_ref[0], gkv_ref[...], dckv_n, dgkv_ref).astype(dckv_ref.dtype)

    def col(first, width):
        return pl.BlockSpec((1, BLOCK, width), lambda bi, n: (bi, n, first // width))

    def whole(a):
        return pl.BlockSpec(a.shape, lambda bi, n: (0,) * len(a.shape))

    tab = pl.BlockSpec((BLOCK, BLOCK), lambda bi, n: (n, 0))
    acc_shapes = [jax.ShapeDtypeStruct(a.shape, F32) for a in (wq, wk, wv, gq, gkv)]
    return pl.pallas_call(
        body, grid=(b, nb),
        in_specs=[col(EV_CQ, MLA_Q_LORA), col(EV_CKV, MLA_KV_LORA), whole(gq), whole(gkv), whole(wq), whole(wk), whole(wv), tab, tab, tab,
                  col(0, hw), col(0, hw), col(0, MLA_HEADS * HEAD)],
        out_specs=[col(0, MLA_Q_LORA), col(0, MLA_KV_LORA), col(0, BLOCK)] + [whole(a) for a in acc_shapes],
        out_shape=[jax.ShapeDtypeStruct((b, lp, MLA_Q_LORA), BF16), jax.ShapeDtypeStruct((b, lp, MLA_KV_LORA), BF16),
                   jax.ShapeDtypeStruct((b, lp, BLOCK), BF16)] + acc_shapes,
        scratch_shapes=[pltpu.VMEM((BLOCK, hw), BF16)],
        compiler_params=_params(("arbitrary", "arbitrary")), name=name,
    )(proj3, proj3, gq, gkv, wq, wk, wv, *tabs, dqf, dkf, dv)


def _mla_fwd(qf, kf, v, name):
    b, lp, _ = qf.shape
    nb = lp // BLOCK
    npair = MLA_HEADS // 2
    scale = (MLA_NOPE + MLA_ROPE) ** -0.5

    def body(q_ref, k_ref, v_ref, o_ref, lse_ref):
        lo = _lo_lanes()

        def qblock(i, _):
            r0 = pl.multiple_of(i * BLOCK, BLOCK)
            qs = [q_ref[0, pl.ds(r0, BLOCK), h * BLOCK:(h + 1) * BLOCK] for h in range(2)]
            t_idx = r0 + _iota2((BLOCK, KEYS), 0)

            def kchunk(c, carry):
                stats, acc = carry[:4], carry[4]
                s0, valid = _key_chunk(c, lp, t_idx, False)
                ps, new, alphas = [], [], []
                for h in range(2):
                    m, l = stats[2 * h], stats[2 * h + 1]
                    s = jnp.where(valid, _dot_nt(qs[h], k_ref[0, pl.ds(s0, KEYS), h * BLOCK:(h + 1) * BLOCK]) * scale, NEG)
                    m_new = jnp.maximum(m, jnp.max(s, axis=1, keepdims=True))
                    p = jnp.where(valid, jnp.exp(s - m_new), 0.0)
                    alpha = jnp.exp(m - m_new)
                    new += [m_new, alpha * l + jnp.sum(p, axis=1, keepdims=True)]
                    alphas.append(alpha)
                    ps.append(p.astype(BF16))
                pv = jnp.dot(jnp.concatenate(ps, axis=1), _stack_halves(v_ref[0, pl.ds(s0, KEYS), :], lo), preferred_element_type=F32)
                return (*new, _pair(alphas[0], alphas[1], lo) * acc + pv)

            neg, zero = jnp.full((BLOCK, 1), NEG, F32), jnp.zeros((BLOCK, 1), F32)
            m_a, l_a, m_b, l_b, acc = lax.fori_loop(0, _n_chunks(i), kchunk, (neg, zero, neg, zero, jnp.zeros((BLOCK, BLOCK), F32)))
            l = _pair(l_a, l_b, lo)
            seen = l > 0.0
            safe = jnp.where(seen, l, 1.0)
            o_ref[0, pl.ds(r0, BLOCK), :] = acc / safe
            lse_ref[0, pl.ds(r0, BLOCK), :] = jnp.where(seen, _pair(m_a, m_b, lo) + jnp.log(safe), 0.0)
            return 0

        lax.fori_loop(0, nb, qblock, 0)

    wide = pl.BlockSpec((1, lp, 2 * BLOCK), lambda bi, hp: (bi, 0, hp))
    thin = pl.BlockSpec((1, lp, 2 * HEAD), lambda bi, hp: (bi, 0, hp))
    shp = jax.ShapeDtypeStruct((b, lp, MLA_HEADS * HEAD), F32)
    return pl.pallas_call(
        body, grid=(b, npair), in_specs=[wide, wide, thin], out_specs=[thin, thin], out_shape=[shp, shp],
        compiler_params=_params(("parallel", "parallel")), name=name,
    )(qf, kf, v)


def _mla_bwd(qf, kf, v, o, lse, do, name):
    b, lp, _ = qf.shape
    nb = lp // BLOCK
    npair = MLA_HEADS // 2
    scale = (MLA_NOPE + MLA_ROPE) ** -0.5

    def body(q_ref, k_ref, v_ref, o_ref, lse_ref, do_ref, dq_ref, dk_ref, dv_ref):
        lo = _lo_lanes()
        dk_ref[...] = jnp.zeros_like(dk_ref)
        dv_ref[...] = jnp.zeros_like(dv_ref)

        def qblock(i, _):
            r0 = pl.multiple_of(i * BLOCK, BLOCK)
            rows = pl.ds(r0, BLOCK)
            qs = [q_ref[0, rows, h * BLOCK:(h + 1) * BLOCK] for h in range(2)]
            do_i = do_ref[0, rows, :]
            dos = _halves(do_i.astype(BF16), lo)
            both = do_i * o_ref[0, rows, :]
            d_a = jnp.sum(jnp.where(lo, both, 0.0), axis=1, keepdims=True)
            dsum = (d_a, jnp.sum(both, axis=1, keepdims=True) - d_a)
            lse_i = lse_ref[0, rows, :]
            lses = (lse_i[:, 0:1], lse_i[:, HEAD:HEAD + 1])
            t_idx = r0 + _iota2((BLOCK, KEYS), 0)

            def kchunk(c, dqs):
                s0, valid = _key_chunk(c, lp, t_idx, False)
                keys = pl.ds(s0, KEYS)
                v_c = v_ref[0, keys, :]
                ps, out = [], []
                for h in range(2):
                    lanes = slice(h * BLOCK, (h + 1) * BLOCK)
                    k = k_ref[0, keys, lanes]
                    s = jnp.where(valid, _dot_nt(qs[h], k) * scale, NEG)
                    p = jnp.where(valid, jnp.exp(s - lses[h]), 0.0)
                    ds = p * (_dot_nt(dos[h], v_c) - dsum[h]) * scale
                    dk_ref[0, keys, lanes] += _dot_tn(ds, qs[h])
                    out.append(dqs[h] + _dot(ds, k))
                    ps.append(p.astype(BF16))
                dv_ref[0, keys, :] += _dot_tn(jnp.concatenate(ps, axis=0), jnp.concatenate(dos, axis=0))
                return tuple(out)

            zero = jnp.zeros((BLOCK, BLOCK), F32)
            dq_a, dq_b = lax.fori_loop(0, _n_chunks(i), kchunk, (zero, zero))
            dq_ref[0, rows, 0:BLOCK] = dq_a
            dq_ref[0, rows, BLOCK:2 * BLOCK] = dq_b
            return 0

        lax.fori_loop(0, nb, qblock, 0)

    wide = pl.BlockSpec((1, lp, 2 * BLOCK), lambda bi, hp: (bi, 0, hp))
    thin = pl.BlockSpec((1, lp, 2 * HEAD), lambda bi, hp: (bi, 0, hp))
    return pl.pallas_call(
        body, grid=(b, npair), in_specs=[wide, wide, thin, thin, thin, thin], out_specs=[wide, wide, thin],
        out_shape=[jax.ShapeDtypeStruct(qf.shape, F32), jax.ShapeDtypeStruct(qf.shape, F32), jax.ShapeDtypeStruct(v.shape, F32)],
        compiler_params=_params(("parallel", "parallel")), name=name,
    )(qf, kf, v, o, lse, do)


def _swa_keys(k_ref, v_ref, n, kv, lo):
    prev = jnp.maximum(n - 1, 0)
    rows = lambda blk: pl.ds(pl.multiple_of(blk * BLOCK, BLOCK), BLOCK)
    mine = (_iota2((1, BLOCK), 1) >= HEAD).astype(jnp.int32) == kv

    def both_halves(ref):
        x = jnp.concatenate([ref[0, rows(prev), :], ref[0, rows(n), :], ref[0, 0:BLOCK, :]], axis=0)
        return jnp.where(mine, x, pltpu.roll(x, HEAD, 1)).astype(BF16)

    col = _iota2((BLOCK, 3 * BLOCK), 1)
    loc = col % BLOCK
    s_idx = jnp.where(col < BLOCK, (n - 1) * BLOCK + loc, jnp.where(col < 2 * BLOCK, n * BLOCK + loc, loc))
    dist = n * BLOCK + _iota2((BLOCK, 3 * BLOCK), 0) - s_idx
    band = (col < 2 * BLOCK) & (dist >= 0) & (dist < SWA_WINDOW) & (s_idx >= BLOCK)
    meta = (col >= 2 * BLOCK) & (s_idx >= N_PAD) & (dist >= 0)
    return both_halves(k_ref), both_halves(v_ref), band | meta, dist.astype(F32), prev


def _swa_probs(q_h, kdup, valid, dist, head, sink_ref):
    slope = jnp.exp(jnp.full((1, 1), -8.0 * math.log(2.0) / SWA_HEADS, F32) * (head + 1).astype(F32))
    s = jnp.where(valid, _dot_nt(q_h, kdup) * (HEAD ** -0.5) - slope * dist, NEG)
    sink = sink_ref[pl.ds(head, 1), 0:1]
    m = jnp.maximum(jnp.max(s, axis=1, keepdims=True), sink)
    e = jnp.where(valid, jnp.exp(s - m), 0.0)
    es = jnp.exp(sink - m)
    inv = 1.0 / (jnp.sum(e, axis=1, keepdims=True) + es)
    return e * inv, es * inv


SWA_PAIRS = SWA_HEADS // SWA_KV_HEADS // 2
SWA_GROUP = SWA_PAIRS * 2 * HEAD


def _swa_specs(b, lp):
    nb = lp // BLOCK
    qcol = lambda first: pl.BlockSpec((1, BLOCK, SWA_GROUP), lambda bi, kv, n: (bi, n, first // SWA_GROUP + kv))
    kcol = lambda first: pl.BlockSpec((1, lp, BLOCK), lambda bi, kv, n: (bi, 0, first // BLOCK))
    sink = pl.BlockSpec((SWA_HEADS, BLOCK), lambda bi, kv, n: (0, 0))
    return (b, SWA_KV_HEADS, nb), qcol, kcol, sink


def _swa_fwd(proj3, sinks, name):
    b, lp, _ = proj3.shape
    grid, qcol, kcol, sink = _swa_specs(b, lp)

    def body(q_ref, k_ref, v_ref, sink_ref, o_ref):
        kv, n = pl.program_id(1), pl.program_id(2)
        lo = _lo_lanes()
        kdup, vdup, valid, dist, _ = _swa_keys(k_ref, v_ref, n, kv, lo)
        v_bd = _stack_halves(vdup, lo)
        for p in range(SWA_PAIRS):
            lanes = slice(p * BLOCK, (p + 1) * BLOCK)
            qs = _halves(q_ref[0, :, lanes].astype(BF16), lo)
            probs = [_swa_probs(qs[hh], kdup, valid, dist, (kv * SWA_PAIRS + p) * 2 + hh, sink_ref)[0].astype(BF16) for hh in range(2)]
            o_ref[0, :, lanes] = jnp.dot(jnp.concatenate(probs, axis=1), v_bd, preferred_element_type=F32)

    return pl.pallas_call(
        body, grid=grid, in_specs=[qcol(OD_Q), kcol(OD_K), kcol(OD_V), sink], out_specs=qcol(0),
        out_shape=jax.ShapeDtypeStruct((b, lp, SWA_HEADS * HEAD), F32),
        compiler_params=_params(("parallel", "parallel", "parallel")), name=name,
    )(proj3, proj3, proj3, sinks)


def _swa_bwd(proj3, sinks, do, name):
    b, lp, _ = proj3.shape
    nb = lp // BLOCK
    grid, qcol, kcol, sink = _swa_specs(b, lp)

    def body(q_ref, k_ref, v_ref, sink_ref, do_ref, dq_ref, dk_ref, dv_ref, dsink_ref):
        kv, n = pl.program_id(1), pl.program_id(2)

        @pl.when((n == 0) & (pl.program_id(0) == 0) & (kv == 0))
        def _():
            dsink_ref[...] = jnp.zeros_like(dsink_ref)

        @pl.when(n == 0)
        def _():
            dk_ref[...] = jnp.zeros_like(dk_ref)
            dv_ref[...] = jnp.zeros_like(dv_ref)

        lo = _lo_lanes()
        kdup, vdup, valid, dist, prev = _swa_keys(k_ref, v_ref, n, kv, lo)
        k_bd = _stack_halves(kdup, lo)
        dkc = jnp.zeros((3 * BLOCK, BLOCK), F32)
        dvc = jnp.zeros((3 * BLOCK, BLOCK), F32)
        for p in range(SWA_PAIRS):
            lanes = slice(p * BLOCK, (p + 1) * BLOCK)
            qs = _halves(q_ref[0, :, lanes].astype(BF16), lo)
            dos = _halves(do_ref[0, :, lanes].astype(BF16), lo)
            dss, prs = [], []
            for hh in range(2):
                head = (kv * SWA_PAIRS + p) * 2 + hh
                pr, ps = _swa_probs(qs[hh], kdup, valid, dist, head, sink_ref)
                dp = _dot_nt(dos[hh], vdup)
                dsum = jnp.sum(pr * dp, axis=1, keepdims=True)
                dsink_ref[pl.ds(head, 1), :] += jnp.broadcast_to(-jnp.sum(ps * dsum, axis=0, keepdims=True), (1, BLOCK))
                dss.append((pr * (dp - dsum) * (HEAD ** -0.5)).astype(BF16))
                prs.append(pr.astype(BF16))
            dq_ref[0, :, lanes] = jnp.dot(jnp.concatenate(dss, axis=1), k_bd, preferred_element_type=F32)
            dkc = dkc + _dot_tn(jnp.concatenate(dss, axis=0), jnp.concatenate(qs, axis=0))
            dvc = dvc + _dot_tn(jnp.concatenate(prs, axis=0), jnp.concatenate(dos, axis=0))
        rows = lambda blk: pl.ds(pl.multiple_of(blk * BLOCK, BLOCK), BLOCK)
        for part, r in enumerate((rows(prev), rows(n), slice(0, BLOCK))):
            dk_ref[0, 0, r, :] += dkc[part * BLOCK:(part + 1) * BLOCK]
            dv_ref[0, 0, r, :] += dvc[part * BLOCK:(part + 1) * BLOCK]

        @pl.when(n == nb - 1)
        def _():
            for ref in (dk_ref, dv_ref):
                x = ref[0, 0]
                ref[0, 0] = x + pltpu.roll(x, HEAD, 1)

    kvout = pl.BlockSpec((1, 1, lp, BLOCK), lambda bi, kv, n: (bi, kv, 0, 0))
    kvshape = jax.ShapeDtypeStruct((b, SWA_KV_HEADS, lp, BLOCK), F32)
    return pl.pallas_call(
        body, grid=grid, in_specs=[qcol(OD_Q), kcol(OD_K), kcol(OD_V), sink, qcol(0)], out_specs=[qcol(0), kvout, kvout, sink],
        out_shape=[jax.ShapeDtypeStruct((b, lp, SWA_HEADS * HEAD), F32), kvshape, kvshape, jax.ShapeDtypeStruct((SWA_HEADS, BLOCK), F32)],
        compiler_params=_params(("arbitrary", "arbitrary", "arbitrary")), name=name,
    )(proj3, proj3, proj3, sinks, do)


def _kernel_weights(ev_w_in, ev_w_uq, ev_w_ukv, od_w_in):
    zeros = lambda r, c: jnp.zeros((r, c), ev_w_in.dtype)
    q_sb, k_sb, v_sb, g_sb, c_q, c_kv, k_r, g_mla = jnp.split(ev_w_in, [512, 1024, 1536, 2048, 2304, 2432, 2464], axis=1)
    w0 = jnp.concatenate([g_sb, g_mla, q_sb, k_sb, v_sb, c_q, c_kv, zeros(D_MODEL, MLA_NOPE), k_r, zeros(D_MODEL, 32)], axis=1)
    uq = ev_w_uq.reshape(MLA_Q_LORA, MLA_HEADS, MLA_NOPE + MLA_ROPE)
    wq = jnp.pad(uq, ((0, 0), (0, 0), (0, BLOCK - MLA_NOPE - MLA_ROPE))).reshape(MLA_Q_LORA, MLA_HEADS * BLOCK)
    ukv = ev_w_ukv.reshape(MLA_KV_LORA, MLA_HEADS, BLOCK)
    wk = jnp.pad(ukv[:, :, :MLA_NOPE], ((0, 0), (0, 0), (0, BLOCK - MLA_NOPE))).reshape(MLA_KV_LORA, MLA_HEADS * BLOCK)
    wv = ukv[:, :, MLA_NOPE:].reshape(MLA_KV_LORA, MLA_HEADS * HEAD)
    q, k, v, g = jnp.split(od_w_in, [1024, 1152, 1280], axis=1)
    w1 = jnp.concatenate([g, q, k, v], axis=1)
    return w0, wq, wk, wv, w1


def _original_grads(dw0, dwq, dwk, dwv, dw1):
    sl = lambda a, first, n: a[:, first:first + n]
    d_ev_w_in = jnp.concatenate([sl(dw0, EV_Q, 512), sl(dw0, EV_K, 512), sl(dw0, EV_V, 512), sl(dw0, EV_G, 512), sl(dw0, EV_CQ, 256),
                                 sl(dw0, EV_CKV, 128), sl(dw0, EV_KR + MLA_NOPE, MLA_ROPE), sl(dw0, EV_G + 512, 512)], axis=1)
    d_uq = dwq.reshape(MLA_Q_LORA, MLA_HEADS, BLOCK)[:, :, :MLA_NOPE + MLA_ROPE].reshape(MLA_Q_LORA, -1)
    d_ukv = jnp.concatenate([dwk.reshape(MLA_KV_LORA, MLA_HEADS, BLOCK)[:, :, :MLA_NOPE], dwv.reshape(MLA_KV_LORA, MLA_HEADS, HEAD)],
                            axis=2).reshape(MLA_KV_LORA, -1)
    d_od_w_in = jnp.concatenate([sl(dw1, OD_Q, 1024), sl(dw1, OD_K, 128), sl(dw1, OD_V, 128), sl(dw1, OD_G, 1024)], axis=1)
    return d_ev_w_in, d_uq, d_ukv, d_od_w_in


def _meta_rows_sum(dh0_3):
    b, _, d = dh0_3.shape

    def body(x_ref, o_ref):
        acc = x_ref[0, N_PAD:BLOCK, :]
        for i in range(1, b):
            acc = acc + x_ref[i, N_PAD:BLOCK, :]
        o_ref[...] = acc

    return pl.pallas_call(
        body, grid=(1,), in_specs=[pl.BlockSpec((b, BLOCK, d), lambda i: (0, 0, 0))], out_specs=pl.BlockSpec((N_META, d), lambda i: (0, 0)),
        out_shape=jax.ShapeDtypeStruct((N_META, d), F32), compiler_params=_params(("arbitrary",)), name="meta_rows_sum",
    )(dh0_3)


def _local_step(x, meta, norm_g, final_g, gq, gkv, sinks, target, ev_w_in, ev_w_uq, ev_w_ukv, wo0, od_w_in, wo1):
    b, seq, d = x.shape
    lp = seq + BLOCK
    t = b * lp
    w0, wq, wk, wv, w1 = _kernel_weights(ev_w_in, ev_w_uq, ev_w_ukv, od_w_in)
    h0 = jnp.concatenate([jnp.zeros((b, N_PAD, d), F32), jnp.broadcast_to(meta[None], (b, N_META, d)), x], axis=1).reshape(t, d)
    tabs = _rope_tables(lp)
    g0, g1 = norm_g[0:1], norm_g[1:2]

    hn0 = _rms_fwd(h0, g0, "norm0")
    proj0 = _mm(hn0, w0, "inproj0")
    p0 = proj0.reshape(b, lp, EV_N)
    o_sb, sb_tot = _sb_fwd(p0, "sb_fwd")
    qf, kf, v = _mla_prep_fwd(p0, gq, gkv, wq, wk, wv, tabs, "mla_prep_fwd")
    o_mla, lse = _mla_fwd(qf, kf, v, "mla_fwd")
    o0 = [o_sb.reshape(t, -1), o_mla.reshape(t, -1)]
    ao0 = _gate_fwd(o0, proj0, "gate0")
    h1 = _mm(ao0, wo0, "outproj0", res=h0)

    hn1 = _rms_fwd(h1, g1, "norm1")
    proj1 = _mm(hn1, w1, "inproj1")
    p1 = proj1.reshape(b, lp, OD_N)
    sinks_b = jnp.broadcast_to(sinks.reshape(SWA_HEADS, 1), (SWA_HEADS, BLOCK))
    o1 = _swa_fwd(p1, sinks_b, "swa_fwd").reshape(t, -1)
    ao1 = _gate_fwd([o1], proj1, "gate1")
    h2 = _mm(ao1, wo1, "outproj1", res=h1)

    dh2, d_final_g, loss = _loss_head(h2, final_g.reshape(1, d), target, b, lp)

    d_wo1 = _mm_tn(ao1, dh2, "d_wo1")
    dao1 = _mm_nt(dh2, wo1, "d_ao1")
    (do1,), dg1 = _gate_bwd(dao1, [o1], proj1, "gate1_bwd")
    dq1, dk4, dv4, d_sinks = _swa_bwd(p1, sinks_b, do1.reshape(b, lp, -1), "swa_bwd")
    unheads = lambda a: a[..., :HEAD].transpose(0, 2, 1, 3).reshape(t, SWA_KV_HEADS * HEAD).astype(BF16)
    dproj1 = jnp.concatenate([dg1, dq1.reshape(t, -1).astype(BF16), unheads(dk4), unheads(dv4)], axis=1)
    d_w1 = _mm_tn(hn1, dproj1, "d_w1")
    dhn1 = _mm_nt(dproj1, w1, "d_hn1")
    dh1, d_g1 = _rms_bwd(h1, g1, dhn1, dh2, "norm1_bwd")

    d_wo0 = _mm_tn(ao0, dh1, "d_wo0")
    dao0 = _mm_nt(dh1, wo0, "d_ao0")
    (do_sb, do_mla), dg0 = _gate_bwd(dao0, o0, proj0, "gate0_bwd")
    dq_sb, dk_sb, dv_sb = _sb_bwd(p0, sb_tot, do_sb.reshape(b, lp, -1), "sb_bwd")
    dqf, dkf, dv = _mla_bwd(qf, kf, v, o_mla, lse, do_mla.reshape(b, lp, -1), "mla_bwd")
    dcq, dckv, dkr, d_wq, d_wk, d_wv, d_gq, d_gkv = _mla_prep_bwd(p0, gq, gkv, wq, wk, wv, tabs, dqf, dkf, dv, "mla_prep_bwd")
    flat = lambda a: a.reshape(t, -1).astype(BF16)
    dproj0 = jnp.concatenate([dg0, flat(dq_sb), flat(dk_sb), flat(dv_sb), flat(dcq), flat(dckv), flat(dkr)], axis=1)
    d_w0 = _mm_tn(hn0, dproj0, "d_w0")
    dhn0 = _mm_nt(dproj0, w0, "d_hn0")
    dh0, d_g0 = _rms_bwd(h0, g0, dhn0, dh1, "norm0_bwd")
    dh0 = dh0.reshape(b, lp, d)

    d_ev_w_in, d_uq, d_ukv, d_od_w_in = _original_grads(d_w0, d_wq, d_wk, d_wv, d_w1)
    grads = dict(meta=_meta_rows_sum(dh0), norm_g=jnp.concatenate([d_g0, d_g1], axis=0), final_g=d_final_g.reshape(d),
                 ev_w_in=d_ev_w_in, ev_q_norm_g=d_gq, ev_kv_norm_g=d_gkv, ev_w_uq=d_uq, ev_w_ukv=d_ukv, ev_w_out=d_wo0,
                 od_w_in=d_od_w_in, od_sinks=d_sinks[:, 0].reshape(1, SWA_HEADS), od_w_out=d_wo1)
    return loss, dh0[:, BLOCK:], grads


MESH = pl.DeviceIdType.MESH
ANY = pl.BlockSpec(memory_space=pl.ANY)


def _place():
    return lax.axis_index("x"), lax.axis_index("y"), lax.axis_index("c")


def _other_chips(x, y):
    return [(1 - x, y), (x, 1 - y), (1 - x, 1 - y)]


PACK_ROWS = 1920
HALF_ROWS = PACK_ROWS // 2


def _with_own_slot(slots, own):
    me = 2 * lax.axis_index("x") + lax.axis_index("y")
    return lax.dynamic_update_slice(slots, own[None], (me,) + (0,) * own.ndim)


def _gather_weights(wpack, meta, name):
    def body(w_ref, m_ref, wo_ref, mo_ref, send_sems, recv_sems):
        x, y, c = _place()
        me, sib = 2 * x + y, (x, y, 1 - c)
        chips = _other_chips(x, y)
        mine = w_ref.at[pl.ds(c * HALF_ROWS, HALF_ROWS), :]
        half = lambda chip, h: wo_ref.at[chip, pl.ds(h * HALF_ROWS, HALF_ROWS), :]

        def copy(k, src, dst, to):
            return pltpu.make_async_remote_copy(src_ref=src, dst_ref=dst, send_sem=send_sems.at[k], recv_sem=recv_sems.at[k], device_id=to,
                                                device_id_type=MESH)

        sent = [copy(k, mine, half(me, c), (px, py, c)) for k, (px, py) in enumerate(chips)]
        sent += [copy(6 + k, m_ref, mo_ref.at[me], (px, py, c)) for k, (px, py) in enumerate(chips)]
        for cp in sent:
            cp.start()
        for k, (px, py) in enumerate(chips):
            landed = half(2 * px + py, c)
            copy(k, mine, landed, (px, py, c)).wait_recv()
            fwd = copy(3 + k, landed, landed, sib)
            fwd.start()
            sent.append(fwd)
        for k, (px, py) in enumerate(chips):
            other = half(2 * px + py, 1 - c)
            copy(3 + k, other, other, sib).wait_recv()
            copy(6 + k, m_ref, mo_ref.at[2 * px + py], (px, py, c)).wait_recv()
        for cp in sent:
            cp.wait_send()

    wall, meta_all = pl.pallas_call(
        body, in_specs=[ANY, ANY], out_specs=[ANY, ANY],
        out_shape=[jax.ShapeDtypeStruct((N_CHIPS,) + wpack.shape, wpack.dtype), jax.ShapeDtypeStruct((N_CHIPS,) + meta.shape, meta.dtype)],
        scratch_shapes=[pltpu.SemaphoreType.DMA((9,)), pltpu.SemaphoreType.DMA((9,))],
        name=name,
    )(wpack, meta)
    return _with_own_slot(wall, wpack), _with_own_slot(meta_all, meta)


def _grads_to_sibling(g, name):
    def body(g_ref, o_ref, send_sem, recv_sem):
        x, y, c = _place()
        cp = pltpu.make_async_remote_copy(src_ref=g_ref.at[:, pl.ds((1 - c) * HALF_ROWS, HALF_ROWS), :], dst_ref=o_ref, send_sem=send_sem,
                                          recv_sem=recv_sem, device_id=(x, y, 1 - c), device_id_type=MESH)
        cp.start()
        cp.wait()

    return pl.pallas_call(
        body, in_specs=[ANY], out_specs=ANY, out_shape=jax.ShapeDtypeStruct((g.shape[0], HALF_ROWS, g.shape[2]), g.dtype),
        scratch_shapes=[pltpu.SemaphoreType.DMA(()), pltpu.SemaphoreType.DMA(())],
        name=name,
    )(g)


def _share_halves(r, name):
    def body(r_ref, o_ref, send_sem, recv_sem):
        x, y, c = _place()
        cp = pltpu.make_async_remote_copy(src_ref=r_ref, dst_ref=o_ref, send_sem=send_sem, recv_sem=recv_sem, device_id=(x, y, 1 - c),
                                          device_id_type=MESH)
        cp.start()
        cp.wait()

    theirs = pl.pallas_call(
        body, in_specs=[ANY], out_specs=ANY, out_shape=jax.ShapeDtypeStruct(r.shape, r.dtype),
        scratch_shapes=[pltpu.SemaphoreType.DMA(()), pltpu.SemaphoreType.DMA(())],
        name=name,
    )(r)
    return jnp.where(lax.axis_index("c") == 0, jnp.concatenate([r, theirs], axis=0), jnp.concatenate([theirs, r], axis=0))


def _chip_scatter(s, name):
    def body(s_ref, r_ref, send_sems, recv_sems):
        x, y, c = _place()
        me = 2 * x + y
        for k, (px, py) in enumerate(_other_chips(x, y)):
            pltpu.make_async_remote_copy(src_ref=s_ref.at[2 * px + py], dst_ref=r_ref.at[me], send_sem=send_sems.at[k],
                                         recv_sem=recv_sems.at[k], device_id=(px, py, c), device_id_type=MESH).start()
        for k, (px, py) in enumerate(_other_chips(x, y)):
            cp = pltpu.make_async_remote_copy(src_ref=s_ref.at[2 * px + py], dst_ref=r_ref.at[2 * px + py], send_sem=send_sems.at[k],
                                              recv_sem=recv_sems.at[k], device_id=(px, py, c), device_id_type=MESH)
            cp.wait_recv()
            cp.wait_send()

    parts = pl.pallas_call(
        body, in_specs=[ANY], out_specs=ANY, out_shape=jax.ShapeDtypeStruct(s.shape, s.dtype),
        scratch_shapes=[pltpu.SemaphoreType.DMA((3,)), pltpu.SemaphoreType.DMA((3,))],
        name=name,
    )(s)
    me = 2 * lax.axis_index("x") + lax.axis_index("y")
    return _with_own_slot(parts, lax.dynamic_index_in_dim(s, me, axis=0, keepdims=False))


def _all_reduce_small(v, name):
    shape = v.shape

    def body(v_ref, o_ref, slots, send_sems, recv_sems):
        x, y, c = _place()
        me = 4 * x + 2 * y + c
        slots[me] = v_ref[...]
        for r in range(1, N_DEV):
            peer = (x ^ (r >> 2), y ^ ((r >> 1) & 1), c ^ (r & 1))
            pltpu.make_async_remote_copy(src_ref=v_ref, dst_ref=slots.at[me], send_sem=send_sems.at[r - 1], recv_sem=recv_sems.at[r - 1],
                                         device_id=peer, device_id_type=MESH).start()
        for r in range(1, N_DEV):
            peer = (x ^ (r >> 2), y ^ ((r >> 1) & 1), c ^ (r & 1))
            cp = pltpu.make_async_remote_copy(src_ref=v_ref, dst_ref=slots.at[4 * peer[0] + 2 * peer[1] + peer[2]], send_sem=send_sems.at[r - 1],
                                              recv_sem=recv_sems.at[r - 1], device_id=peer, device_id_type=MESH)
            cp.wait_recv()
            cp.wait_send()
        acc = slots[0]
        for d in range(1, N_DEV):
            acc = acc + slots[d]
        o_ref[...] = acc

    vm = pl.BlockSpec(memory_space=pltpu.VMEM)
    return pl.pallas_call(
        body, in_specs=[vm], out_specs=vm, out_shape=jax.ShapeDtypeStruct(shape, F32),
        scratch_shapes=[pltpu.VMEM((N_DEV,) + shape, F32), pltpu.SemaphoreType.DMA((N_DEV - 1,)), pltpu.SemaphoreType.DMA((N_DEV - 1,))],
        name=name,
    )(v)


SUM_ROWS = 480


def _add_sibling(g, gsib, core, name):
    n, _, cdim = g.shape
    per_half = HALF_ROWS // SUM_ROWS

    def body(core_ref, a_ref, b_ref, o_ref):
        o_ref[...] = (a_ref[...] + b_ref[...]).astype(o_ref.dtype)

    blk = pl.BlockSpec((1, SUM_ROWS, cdim), lambda j, i, core_ref: (j, i, 0))
    return pl.pallas_call(
        body,
        grid_spec=pltpu.PrefetchScalarGridSpec(
            num_scalar_prefetch=1, grid=(n, per_half),
            in_specs=[pl.BlockSpec((1, SUM_ROWS, cdim), lambda j, i, core_ref: (j, core_ref[0] * per_half + i, 0)), blk], out_specs=blk),
        out_shape=jax.ShapeDtypeStruct(gsib.shape, BF16), compiler_params=_params(("parallel", "parallel")), name=name,
    )(core, g, gsib)


def _sum_parts(parts, name):
    n, r, cdim = parts.shape

    def body(p_ref, o_ref):
        acc = p_ref[0].astype(F32)
        for j in range(1, n):
            acc = acc + p_ref[j].astype(F32)
        o_ref[...] = acc

    return pl.pallas_call(
        body, grid=(r // SUM_ROWS,), in_specs=[pl.BlockSpec((n, SUM_ROWS, cdim), lambda i: (0, i, 0))],
        out_specs=pl.BlockSpec((SUM_ROWS, cdim), lambda i: (i, 0)), out_shape=jax.ShapeDtypeStruct((r, cdim), F32),
        compiler_params=_params(("parallel",)), name=name,
    )(parts)


def _adamw(parts, w, m, v, name):
    npart, r, cdim = parts.shape
    tr = 384 if r % 384 == 0 else r

    def body(p_ref, w_ref, m_ref, v_ref, g_ref, d_ref, nm_ref, nv_ref):
        g = p_ref[0]
        for j in range(1, npart):
            g = g + p_ref[j]
        m_new = ADAM_B1 * m_ref[...] + (1.0 - ADAM_B1) * g
        v_new = ADAM_B2 * v_ref[...] + (1.0 - ADAM_B2) * (g * g)
        m_hat = m_new / (1.0 - ADAM_B1 ** ADAM_STEP)
        v_hat = v_new / (1.0 - ADAM_B2 ** ADAM_STEP)
        g_ref[...] = g
        d_ref[...] = -ADAM_LR * (m_hat / (jnp.sqrt(v_hat) + ADAM_EPS) + ADAM_WD * w_ref[...])
        nm_ref[...] = m_new
        nv_ref[...] = v_new

    blk = pl.BlockSpec((tr, cdim), lambda i: (i, 0))
    shp = jax.ShapeDtypeStruct((r, cdim), F32)
    return pl.pallas_call(
        body, grid=(r // tr,), in_specs=[pl.BlockSpec((npart, tr, cdim), lambda i: (0, i, 0)), blk, blk, blk], out_specs=[blk] * 4,
        out_shape=[shp] * 4, compiler_params=_params(("parallel",)), name=name,
    )(parts, w, m, v)


BIG = ("ev_w_in", "ev_w_uq", "ev_w_ukv", "ev_w_out", "od_w_in", "od_w_out", "meta")
SMALL = ("norm_g", "final_g", "ev_q_norm_g", "ev_kv_norm_g", "od_sinks")
SMALL_SHAPE = (8, 512)


def _pack_big(arrs):
    rows = [a.reshape(-1, D_MODEL) for a in arrs]
    used = sum(r.shape[0] for r in rows)
    return jnp.pad(jnp.concatenate(rows, axis=0), ((0, PACK_ROWS - used), (0, 0)))


def _unpack_big(p, shapes):
    out, at = [], 0
    for s in shapes:
        nrow = int(np.prod(s)) // D_MODEL
        out.append(p[at:at + nrow].reshape(s))
        at += nrow
    return out


def _pack_small(arrs, extra=None):
    flat = [a.reshape(-1) for a in arrs] + ([] if extra is None else [extra.reshape(-1)])
    used = sum(f.shape[0] for f in flat)
    return jnp.pad(jnp.concatenate(flat), (0, SMALL_SHAPE[0] * SMALL_SHAPE[1] - used)).reshape(SMALL_SHAPE)


def _unpack_small(p, shapes):
    flat, out, at = p.reshape(-1), [], 0
    for s in shapes:
        n = int(np.prod(s))
        out.append(flat[at:at + n].reshape(s))
        at += n
    return out, flat[at]


def _shard_of(full, name, j):
    if name in ("ev_w_out", "od_w_out"):
        n = full.shape[0] // N_CHIPS
        return full[j * n:(j + 1) * n]
    n = full.shape[1] // N_CHIPS
    return full[:, j * n:(j + 1) * n]


def kernel(x, meta, norm_g, final_g, ev_w_in, ev_q_norm_g, ev_kv_norm_g, ev_w_uq, ev_w_ukv, ev_w_out, od_w_in, od_sinks, od_w_out, loss_target, m_meta, m_norm_g, m_final_g, m_ev_w_in, m_ev_q_norm_g, m_ev_kv_norm_g, m_ev_w_uq, m_ev_w_ukv, m_ev_w_out, m_od_w_in, m_od_sinks, m_od_w_out, v_meta, v_norm_g, v_final_g, v_ev_w_in, v_ev_q_norm_g, v_ev_kv_norm_g, v_ev_w_uq, v_ev_w_ukv, v_ev_w_out, v_od_w_in, v_od_sinks, v_od_w_out):
    given = dict(locals())
    big_w = [given[n][0] if given[n].ndim == 3 else given[n] for n in BIG]
    big_shapes = [given[n].shape for n in BIG]

    wpack = _pack_big(big_w[:-1]).astype(BF16)
    wall, meta_all = _gather_weights(wpack, meta, "gather_weights")
    per_chip = [_unpack_big(wall[j], [a.shape for a in big_w[:-1]]) for j in range(N_CHIPS)]
    cat = lambda i, axis: jnp.concatenate([per_chip[j][i] for j in range(N_CHIPS)], axis=axis)
    full = dict(ev_w_in=cat(0, 1), ev_w_uq=cat(1, 1), ev_w_ukv=cat(2, 1), ev_w_out=cat(3, 0), od_w_in=cat(4, 1), od_w_out=cat(5, 0))
    meta_full = jnp.concatenate([meta_all[j] for j in range(N_CHIPS)], axis=1)

    loss, grad_x, grads = _local_step(x, meta_full, norm_g, final_g, ev_q_norm_g, ev_kv_norm_g, od_sinks, loss_target,
                                      full["ev_w_in"], full["ev_w_uq"], full["ev_w_ukv"], full["ev_w_out"], full["od_w_in"], full["od_w_out"])

    gpack = jnp.stack([_pack_big([_shard_of(grads[n], n, j) for n in BIG]) for j in range(N_CHIPS)])
    core = lax.axis_index("c").astype(jnp.int32).reshape(1)
    gsum = _add_sibling(gpack, _grads_to_sibling(gpack, "grads_to_sibling"), core, "add_sibling")
    reduced = _share_halves(_sum_parts(_chip_scatter(gsum, "grads_to_chips"), "add_chips"), "reduced_to_sibling")
    big_out = _adamw(reduced[None], _pack_big(big_w), _pack_big([given["m_" + n] for n in BIG]), _pack_big([given["v_" + n] for n in BIG]),
                     "adamw_matrices")
    big_out = [_unpack_big(o, big_shapes) for o in big_out]

    small_shapes = [given[n].shape for n in SMALL]
    ssum = _all_reduce_small(_pack_small([grads[n] for n in SMALL], loss[0, 0]), "reduce_vectors")
    small_out = _adamw(ssum[None], _pack_small([given[n] for n in SMALL]), _pack_small([given["m_" + n] for n in SMALL]),
                       _pack_small([given["v_" + n] for n in SMALL]), "adamw_vectors")
    total_loss = ssum.reshape(-1)[sum(int(np.prod(s)) for s in small_shapes)]
    small_out = [_unpack_small(o, small_shapes)[0] for o in small_out]

    names = ("meta", "norm_g", "final_g", "ev_w_in", "ev_q_norm_g", "ev_kv_norm_g", "ev_w_uq", "ev_w_ukv", "ev_w_out", "od_w_in", "od_sinks",
             "od_w_out")
    outs = [total_loss, grad_x]
    for kind in range(4):
        for n in names:
            outs.append(big_out[kind][BIG.index(n)] if n in BIG else small_out[kind][SMALL.index(n)])
    return tuple(outs)
```

```python
import functools
import math

import numpy as np
import jax
import jax.numpy as jnp
from jax import lax
from jax.experimental import pallas as pl
from jax.experimental.pallas import tpu as pltpu

F32 = jnp.float32
BF16 = jnp.bfloat16

D_MODEL = 1024
BLOCK = 128
N_META = 16
N_PAD = BLOCK - N_META
NORM_EPS = 1e-6
NEG = -1e30
HEAD = 64
SB_HEADS = 8
MLA_HEADS = 8
MLA_Q_LORA = 256
MLA_KV_LORA = 128
MLA_NOPE = 64
MLA_ROPE = 32
ROPE_BASE = 10000.0
SWA_HEADS = 16
SWA_KV_HEADS = 2
SWA_WINDOW = 128
N_CHIPS = 4
N_DEV = 8

ADAM_LR = 0.001
ADAM_B1 = 0.9
ADAM_B2 = 0.999
ADAM_EPS = 1e-08
ADAM_WD = 0.01
ADAM_STEP = 10

VMEM_LIMIT = 48 * 1024 * 1024

EV_G, EV_Q, EV_K, EV_V, EV_CQ, EV_CKV, EV_KR, EV_N = 0, 1024, 1536, 2048, 2560, 2816, 2944, 3072
OD_G, OD_Q, OD_K, OD_V, OD_N = 0, 1024, 2048, 2176, 2304


def _params(sem=None):
    return pltpu.CompilerParams(dimension_semantics=sem, vmem_limit_bytes=VMEM_LIMIT)


def _row_tile(m):
    return 256 if m % 256 == 0 else 128


def _dot(a, b):
    return jnp.dot(a.astype(BF16), b.astype(BF16), preferred_element_type=F32)


def _dot_nt(a, b):
    return lax.dot_general(a.astype(BF16), b.astype(BF16), (((1,), (1,)), ((), ())), preferred_element_type=F32)


def _dot_tn(a, b):
    return lax.dot_general(a.astype(BF16), b.astype(BF16), (((0,), (0,)), ((), ())), preferred_element_type=F32)


def _rms_fwd(h, g, name):
    t, d = h.shape
    tm = _row_tile(t)

    def body(h_ref, g_ref, o_ref):
        x = h_ref[...]
        r = lax.rsqrt(jnp.mean(x * x, axis=-1, keepdims=True) + NORM_EPS)
        o_ref[...] = ((x * r) * g_ref[...]).astype(o_ref.dtype)

    return pl.pallas_call(
        body, grid=(t // tm,),
        in_specs=[pl.BlockSpec((tm, d), lambda i: (i, 0)), pl.BlockSpec((1, d), lambda i: (0, 0))],
        out_specs=pl.BlockSpec((tm, d), lambda i: (i, 0)),
        out_shape=jax.ShapeDtypeStruct((t, d), BF16), compiler_params=_params(("parallel",)), name=name,
    )(h, g)


def _rms_bwd(h, g, dy, dres, name):
    t, d = h.shape
    tm = _row_tile(t)

    def body(h_ref, g_ref, dy_ref, dres_ref, dh_ref, dg_ref):
        @pl.when(pl.program_id(0) == 0)
        def _():
            dg_ref[...] = jnp.zeros_like(dg_ref)

        x = h_ref[...]
        r = lax.rsqrt(jnp.mean(x * x, axis=-1, keepdims=True) + NORM_EPS)
        xr = x * r
        dy_ = dy_ref[...]
        u = dy_ * g_ref[...]
        dh_ref[...] = dres_ref[...] + r * (u - xr * jnp.mean(u * xr, axis=-1, keepdims=True))
        dg_ref[...] += jnp.sum(dy_ * xr, axis=0, keepdims=True)

    row = pl.BlockSpec((tm, d), lambda i: (i, 0))
    vec = pl.BlockSpec((1, d), lambda i: (0, 0))
    return pl.pallas_call(
        body, grid=(t // tm,), in_specs=[row, vec, row, row], out_specs=[row, vec],
        out_shape=[jax.ShapeDtypeStruct((t, d), F32), jax.ShapeDtypeStruct((1, d), F32)],
        compiler_params=_params(("arbitrary",)), name=name,
    )(h, g, dy, dres)


def _col_tile(n):
    for c in (1024, 768, 640, 512, 384, 256, 128):
        if n % c == 0:
            return c
    return n


def _mm(a, w, name, res=None, out_dtype=F32, a_cols=None):
    m = a.shape[0]
    k, n = w.shape
    a_blk = 0 if a_cols is None else a_cols[0] // k
    assert a_cols is None or (a_cols[1] == k and a_cols[0] % k == 0)
    tm, tn = _row_tile(m), _col_tile(n)

    def body(*refs):
        if res is None:
            a_ref, w_ref, o_ref = refs
            acc = _dot(a_ref[...], w_ref[...])
        else:
            a_ref, w_ref, r_ref, o_ref = refs
            acc = r_ref[...] + _dot(a_ref[...], w_ref[...])
        o_ref[...] = acc.astype(o_ref.dtype)

    in_specs = [pl.BlockSpec((tm, k), lambda j, i: (i, a_blk)), pl.BlockSpec((k, tn), lambda j, i: (0, j))]
    args = [a, w]
    if res is not None:
        in_specs.append(pl.BlockSpec((tm, tn), lambda j, i: (i, j)))
        args.append(res)
    return pl.pallas_call(
        body, grid=(n // tn, m // tm), in_specs=in_specs, out_specs=pl.BlockSpec((tm, tn), lambda j, i: (i, j)),
        out_shape=jax.ShapeDtypeStruct((m, n), out_dtype), compiler_params=_params(("parallel", "parallel")), name=name,
    )(*args)


def _mm_nt(a, w, name):
    m, n = a.shape
    k = w.shape[0]
    tm, tk = _row_tile(m), _col_tile(k)

    def body(a_ref, w_ref, o_ref):
        o_ref[...] = _dot_nt(a_ref[...], w_ref[...])

    return pl.pallas_call(
        body, grid=(k // tk, m // tm),
        in_specs=[pl.BlockSpec((tm, n), lambda j, i: (i, 0)), pl.BlockSpec((tk, n), lambda j, i: (j, 0))],
        out_specs=pl.BlockSpec((tm, tk), lambda j, i: (i, j)),
        out_shape=jax.ShapeDtypeStruct((m, k), F32), compiler_params=_params(("parallel", "parallel")), name=name,
    )(a, w)


def _mm_tn(x, dy, name):
    m, k = x.shape
    n = dy.shape[1]
    tm, tn = _row_tile(m), _col_tile(n)

    def body(x_ref, dy_ref, o_ref):
        @pl.when(pl.program_id(1) == 0)
        def _():
            o_ref[...] = jnp.zeros_like(o_ref)

        o_ref[...] += _dot_tn(x_ref[...], dy_ref[...])

    return pl.pallas_call(
        body, grid=(n // tn, m // tm),
        in_specs=[pl.BlockSpec((tm, k), lambda j, i: (i, 0)), pl.BlockSpec((tm, tn), lambda j, i: (i, j))],
        out_specs=pl.BlockSpec((k, tn), lambda j, i: (0, j)),
        out_shape=jax.ShapeDtypeStruct((k, n), F32), compiler_params=_params(("parallel", "arbitrary")), name=name,
    )(x, dy)


def _silu_parts(g):
    s = 1.0 / (1.0 + jnp.exp(-g))
    return g * s, s * (1.0 + g * (1.0 - s))


def _gate_fwd(o_parts, proj, name):
    t = proj.shape[0]
    tm = _row_tile(t)
    w = D_MODEL // len(o_parts)

    def body(*refs):
        g_ref, o_ref = refs[-2], refs[-1]
        for p, r in enumerate(refs[:-2]):
            sil, _ = _silu_parts(g_ref[:, p * w:(p + 1) * w])
            o_ref[:, p * w:(p + 1) * w] = (r[...].astype(F32) * sil).astype(o_ref.dtype)

    return pl.pallas_call(
        body, grid=(t // tm,),
        in_specs=[pl.BlockSpec((tm, w), lambda i: (i, 0)) for _ in o_parts] + [pl.BlockSpec((tm, D_MODEL), lambda i: (i, 0))],
        out_specs=pl.BlockSpec((tm, D_MODEL), lambda i: (i, 0)),
        out_shape=jax.ShapeDtypeStruct((t, D_MODEL), BF16), compiler_params=_params(("parallel",)), name=name,
    )(*o_parts, proj)


def _gate_bwd(dao, o_parts, proj, name):
    t = proj.shape[0]
    tm = _row_tile(t)
    np_ = len(o_parts)
    w = D_MODEL // np_

    def body(*refs):
        dao_ref, g_ref = refs[0], refs[1 + np_]
        do_refs, dg_ref = refs[2 + np_:2 + 2 * np_], refs[-1]
        for p in range(np_):
            sl = slice(p * w, (p + 1) * w)
            sil, dsil = _silu_parts(g_ref[:, sl])
            da = dao_ref[:, sl]
            do_refs[p][...] = da * sil
            dg_ref[:, sl] = (da * refs[1 + p][...].astype(F32) * dsil).astype(dg_ref.dtype)

    full = pl.BlockSpec((tm, D_MODEL), lambda i: (i, 0))
    part = pl.BlockSpec((tm, w), lambda i: (i, 0))
    outs = pl.pallas_call(
        body, grid=(t // tm,), in_specs=[full] + [part] * np_ + [full], out_specs=[part] * np_ + [full],
        out_shape=[jax.ShapeDtypeStruct((t, w), F32)] * np_ + [jax.ShapeDtypeStruct((t, D_MODEL), BF16)],
        compiler_params=_params(("parallel",)), name=name,
    )(dao, *o_parts, proj)
    return outs[:np_], outs[np_]


def _loss_head(h2, gf, target, b, lp):
    d = h2.shape[1]
    nb = lp // BLOCK
    h3 = h2.reshape(b, lp, d)

    def body(h_ref, g_ref, t_ref, dh_ref, dg_ref, loss_ref):
        first = (pl.program_id(0) == 0) & (pl.program_id(1) == 0)

        @pl.when(first)
        def _():
            dg_ref[...] = jnp.zeros_like(dg_ref)
            loss_ref[...] = jnp.zeros_like(loss_ref)

        @pl.when(pl.program_id(1) == 0)
        def _():
            dh_ref[...] = jnp.zeros_like(dh_ref)

        @pl.when(pl.program_id(1) > 0)
        def _():
            x = h_ref[0]
            r = lax.rsqrt(jnp.mean(x * x, axis=-1, keepdims=True) + NORM_EPS)
            xr = x * r
            g = g_ref[...]
            diff = xr * g - t_ref[0]
            loss_ref[...] += 0.5 * jnp.sum(jnp.mean(diff * diff, axis=-1, keepdims=True))
            dy = diff * (1.0 / d)
            u = dy * g
            dh_ref[0] = r * (u - xr * jnp.mean(u * xr, axis=-1, keepdims=True))
            dg_ref[...] += jnp.sum(dy * xr, axis=0, keepdims=True)

    blk = pl.BlockSpec((1, BLOCK, d), lambda bi, n: (bi, n, 0))
    dh, dg, loss = pl.pallas_call(
        body, grid=(b, nb),
        in_specs=[blk, pl.BlockSpec((1, d), lambda bi, n: (0, 0)),
                  pl.BlockSpec((1, BLOCK, d), lambda bi, n: (bi, jnp.maximum(n - 1, 0), 0))],
        out_specs=[blk, pl.BlockSpec((1, d), lambda bi, n: (0, 0)), pl.BlockSpec((8, 128), lambda bi, n: (0, 0))],
        out_shape=[jax.ShapeDtypeStruct((b, lp, d), F32), jax.ShapeDtypeStruct((1, d), F32), jax.ShapeDtypeStruct((8, 128), F32)],
        compiler_params=_params(("arbitrary", "arbitrary")), name="loss_head",
    )(h3, gf, target)
    return dh.reshape(b * lp, d), dg, loss


def _iota2(shape, dim):
    return lax.broadcasted_iota(jnp.int32, shape, dim)


def _split_dot(x, tri):
    hi = x.astype(BF16)
    lo = (x - hi.astype(F32)).astype(BF16)
    return jnp.dot(hi, tri, preferred_element_type=F32) + jnp.dot(lo, tri, preferred_element_type=F32)


KEYS = 256


def _lo_lanes():
    return _iota2((1, BLOCK), 1) < HEAD


def _halves(x, lo):
    zero = jnp.zeros_like(x)
    return jnp.where(lo, x, zero), jnp.where(lo, zero, x)


def _stack_halves(x, lo):
    a, b = _halves(x, lo)
    return jnp.concatenate([a, b], axis=0)


def _pair(a, b, lo):
    return jnp.where(lo, a, b)


def _key_chunk(c, lp, t_idx, strict):
    first = c * KEYS
    s0 = pl.multiple_of(jnp.minimum(first, lp - KEYS), BLOCK)
    s_idx = s0 + _iota2(t_idx.shape, 1)
    seen = (s_idx < t_idx) if strict else (s_idx <= t_idx)
    return s0, seen & (s_idx >= jnp.maximum(first, N_PAD))


def _n_chunks(i):
    return (i + 2) // 2


QROWS = 256


def _for_query_tiles(nb, tile):
    per = QROWS // BLOCK

    def step(j, _):
        tile(pl.multiple_of(j * QROWS, QROWS), QROWS, j + 1)
        return 0

    lax.fori_loop(0, nb // per, step, 0)
    if nb % per:
        tile((nb - 1) * BLOCK, BLOCK, _n_chunks(nb - 1))


def _sb_scores(q_h, k, valid, after):
    z = _dot_nt(q_h, k) * (HEAD ** -0.5)
    sp = jnp.log(1.0 + jnp.exp(-jnp.abs(z)))
    lb = jnp.minimum(z, 0.0) - sp
    l1m_all = -jnp.maximum(z, 0.0) - sp
    l1m = jnp.where(valid, l1m_all, 0.0)
    return lb, l1m_all, l1m, _split_dot(l1m, after)


def _sb_fwd(proj3, name):
    b, lp, _ = proj3.shape
    nb = lp // BLOCK
    npair = SB_HEADS // 2

    def body(q_ref, k_ref, v_ref, o_ref, tot_ref):
        lo = _lo_lanes()
        after = (_iota2((KEYS, KEYS), 0) > _iota2((KEYS, KEYS), 1)).astype(BF16)

        def qtile(r0, bq, n):
            qs = _halves(q_ref[0, pl.ds(r0, bq), :].astype(BF16), lo)
            t_idx = r0 + _iota2((bq, KEYS), 0)

            def kchunk(cc, carry):
                cs, acc = carry[:2], carry[2]
                s0, valid = _key_chunk(n - 1 - cc, lp, t_idx, True)
                k = k_ref[0, pl.ds(s0, KEYS), :].astype(BF16)
                a_s, new = [], []
                for h in range(2):
                    lb, _, l1m, suf = _sb_scores(qs[h], k, valid, after)
                    a_s.append(jnp.where(valid, jnp.exp(lb + suf + cs[h]), 0.0).astype(BF16))
                    new.append(cs[h] + jnp.sum(l1m, axis=1, keepdims=True))
                v_bd = _stack_halves(v_ref[0, pl.ds(s0, KEYS), :].astype(BF16), lo)
                return (*new, acc + jnp.dot(jnp.concatenate(a_s, axis=1), v_bd, preferred_element_type=F32))

            zero = jnp.zeros((bq, 1), F32)
            c_a, c_b, acc = lax.fori_loop(0, n, kchunk, (zero, zero, jnp.zeros((bq, BLOCK), F32)))
            o_ref[0, pl.ds(r0, bq), :] = acc
            tot_ref[0, pl.ds(r0, bq), :] = jnp.broadcast_to(_pair(c_a, c_b, lo), (bq, BLOCK))

        _for_query_tiles(nb, qtile)

    def col(first):
        return pl.BlockSpec((1, lp, 2 * HEAD), lambda bi, hp: (bi, 0, first // (2 * HEAD) + hp))

    shp = jax.ShapeDtypeStruct((b, lp, SB_HEADS * HEAD), F32)
    return pl.pallas_call(
        body, grid=(b, npair), in_specs=[col(EV_Q), col(EV_K), col(EV_V)], out_specs=[col(0), col(0)], out_shape=[shp, shp],
        compiler_params=_params(("parallel", "parallel")), name=name,
    )(proj3, proj3, proj3)


def _sb_bwd(proj3, tot, do, name):
    b, lp, _ = proj3.shape
    nb = lp // BLOCK
    npair = SB_HEADS // 2

    def body(q_ref, k_ref, v_ref, tot_ref, do_ref, dq_ref, dk_ref, dv_ref):
        lo = _lo_lanes()
        after = (_iota2((KEYS, KEYS), 0) > _iota2((KEYS, KEYS), 1)).astype(BF16)
        before = (_iota2((KEYS, KEYS), 0) < _iota2((KEYS, KEYS), 1)).astype(BF16)
        dk_ref[...] = jnp.zeros_like(dk_ref)
        dv_ref[...] = jnp.zeros_like(dv_ref)

        def qtile(r0, bq, n):
            rows = pl.ds(r0, bq)
            qs = _halves(q_ref[0, rows, :].astype(BF16), lo)
            dos = _halves(do_ref[0, rows, :].astype(BF16), lo)
            tot_i = tot_ref[0, rows, :]
            tots = (tot_i[:, 0:1], tot_i[:, HEAD:HEAD + 1])
            q_st, do_st = jnp.concatenate(qs, axis=0), jnp.concatenate(dos, axis=0)
            t_idx = r0 + _iota2((bq, KEYS), 0)

            def kchunk(c, carry):
                s0, valid = _key_chunk(c, lp, t_idx, True)
                keys = pl.ds(s0, KEYS)
                k = k_ref[0, keys, :].astype(BF16)
                v = v_ref[0, keys, :].astype(BF16)
                a_s, dzs, new = [], [], []
                for h in range(2):
                    left, pre = carry[2 * h], carry[2 * h + 1]
                    lb, l1m_all, l1m, suf = _sb_scores(qs[h], k, valid, after)
                    here = jnp.sum(l1m, axis=1, keepdims=True)
                    a = jnp.where(valid, jnp.exp(lb + suf + (tots[h] - left - here)), 0.0)
                    w = a * _dot_nt(dos[h], v)
                    dz = jnp.where(valid, w * jnp.exp(l1m_all) - (pre + _split_dot(w, before)) * jnp.exp(lb), 0.0) * (HEAD ** -0.5)
                    new += [left + here, pre + jnp.sum(w, axis=1, keepdims=True)]
                    a_s.append(a.astype(BF16))
                    dzs.append(dz.astype(BF16))
                dk_ref[0, keys, :] += _dot_tn(jnp.concatenate(dzs, axis=0), q_st)
                dv_ref[0, keys, :] += _dot_tn(jnp.concatenate(a_s, axis=0), do_st)
                dq = carry[4] + jnp.dot(jnp.concatenate(dzs, axis=1), _stack_halves(k, lo), preferred_element_type=F32)
                return (*new, dq)

            zero = jnp.zeros((bq, 1), F32)
            out = lax.fori_loop(0, n, kchunk, (zero, zero, zero, zero, jnp.zeros((bq, BLOCK), F32)))
            dq_ref[0, rows, :] = out[4]

        _for_query_tiles(nb, qtile)

    def col(first):
        return pl.BlockSpec((1, lp, 2 * HEAD), lambda bi, hp: (bi, 0, first // (2 * HEAD) + hp))

    shp = jax.ShapeDtypeStruct((b, lp, SB_HEADS * HEAD), F32)
    return pl.pallas_call(
        body, grid=(b, npair), in_specs=[col(EV_Q), col(EV_K), col(EV_V), col(0), col(0)], out_specs=[col(0)] * 3, out_shape=[shp] * 3,
        compiler_params=_params(("parallel", "parallel")), name=name,
    )(proj3, proj3, proj3, tot, do)


def _rope_tables(lp):
    half = MLA_ROPE // 2
    pos = (np.arange(lp) - N_PAD).astype(np.float32)
    inv = jnp.asarray(ROPE_BASE, F32) ** (-jnp.arange(half, dtype=F32) / half)
    ang = jnp.asarray(pos)[:, None] * inv[None, :]
    cos, sin = jnp.cos(ang), jnp.sin(ang)
    zeros = lambda n: jnp.zeros((lp, n), F32)
    c = jnp.concatenate([jnp.ones((lp, MLA_NOPE), F32), cos, cos, zeros(32)], axis=1)
    s1 = jnp.concatenate([zeros(MLA_NOPE), -sin, zeros(half), zeros(32)], axis=1)
    s2 = jnp.concatenate([zeros(MLA_NOPE), zeros(half), sin, zeros(32)], axis=1)
    return c, s1, s2


def _rope(x, c, s1, s2):
    half = MLA_ROPE // 2
    return x * c + pltpu.roll(x, BLOCK - half, 1) * s1 + pltpu.roll(x, half, 1) * s2


def _rope_t(dy, c, s1, s2):
    half = MLA_ROPE // 2
    return dy * c + pltpu.roll(dy * s1, half, 1) + pltpu.roll(dy * s2, BLOCK - half, 1)


def _rms_rows(x, g):
    r = lax.rsqrt(jnp.mean(x * x, axis=-1, keepdims=True) + NORM_EPS)
    return x * r, r


def _mla_prep_fwd(proj3, gq, gkv, wq, wk, wv, tabs, name):
    b, lp, _ = proj3.shape
    nb = lp // BLOCK
    hw = MLA_HEADS * BLOCK

    def body(cq_ref, ckv_ref, kr_ref, gq_ref, gkv_ref, wq_ref, wk_ref, wv_ref, c_ref, s1_ref, s2_ref, qf_ref, kf_ref, v_ref):
        c, s1, s2 = c_ref[...], s1_ref[...], s2_ref[...]
        xq, _ = _rms_rows(cq_ref[0], None)
        qh = _dot(xq * gq_ref[...], wq_ref[...])
        xk, _ = _rms_rows(ckv_ref[0], None)
        ckv_n = xk * gkv_ref[...]
        kv = _dot(ckv_n, wk_ref[...])
        v_ref[0] = _dot(ckv_n, wv_ref[...]).astype(v_ref.dtype)
        kr = _rope(kr_ref[0], c, s1, s2)
        for h in range(MLA_HEADS):
            ls = slice(h * BLOCK, (h + 1) * BLOCK)
            qf_ref[0, :, ls] = _rope(qh[:, ls], c, s1, s2).astype(qf_ref.dtype)
            kf_ref[0, :, ls] = (kv[:, ls] + kr).astype(kf_ref.dtype)

    def col(first, width):
        return pl.BlockSpec((1, BLOCK, width), lambda bi, n: (bi, n, first // width))

    def whole(a):
        return pl.BlockSpec(a.shape, lambda bi, n: (0,) * a.ndim)

    tab = pl.BlockSpec((BLOCK, BLOCK), lambda bi, n: (n, 0))
    return pl.pallas_call(
        body, grid=(b, nb),
        in_specs=[col(EV_CQ, MLA_Q_LORA), col(EV_CKV, MLA_KV_LORA), col(EV_KR, BLOCK), whole(gq), whole(gkv), whole(wq), whole(wk),
                  whole(wv), tab, tab, tab],
        out_specs=[col(0, hw), col(0, hw), col(0, MLA_HEADS * HEAD)],
        out_shape=[jax.ShapeDtypeStruct((b, lp, hw), BF16), jax.ShapeDtypeStruct((b, lp, hw), BF16),
                   jax.ShapeDtypeStruct((b, lp, MLA_HEADS * HEAD), BF16)],
        compiler_params=_params(("parallel", "parallel")), name=name,
    )(proj3, proj3, proj3, gq, gkv, wq, wk, wv, *tabs)


def _mla_prep_bwd(proj3, gq, gkv, wq, wk, wv, tabs, dqf, dkf, dv, name):
    b, lp, _ = proj3.shape
    nb = lp // BLOCK
    hw = MLA_HEADS * BLOCK

    def body(cq_ref, ckv_ref, gq_ref, gkv_ref, wq_ref, wk_ref, wv_ref, c_ref, s1_ref, s2_ref, dqf_ref, dkf_ref, dv_ref,
             dcq_ref, dckv_ref, dkr_ref, dwq_ref, dwk_ref, dwv_ref, dgq_ref, dgkv_ref, dqh):
        @pl.when((pl.program_id(0) == 0) & (pl.program_id(1) == 0))
        def _():
            for r in (dwq_ref, dwk_ref, dwv_ref, dgq_ref, dgkv_ref):
                r[...] = jnp.zeros_like(r)

        c, s1, s2 = c_ref[...], s1_ref[...], s2_ref[...]
        dkr = jnp.zeros((BLOCK, BLOCK), F32)
        for h in range(MLA_HEADS):
            ls = slice(h * BLOCK, (h + 1) * BLOCK)
            dqh[:, ls] = _rope_t(dqf_ref[0, :, ls].astype(F32), c, s1, s2).astype(dqh.dtype)
            dkr = dkr + dkf_ref[0, :, ls].astype(F32)
        dkr_ref[0] = _rope_t(dkr, c, s1, s2).astype(dkr_ref.dtype)

        def norm_bwd(x, g, dy, dg_ref):
            xr, r = _rms_rows(x, None)
            u = dy * g
            dg_ref[...] += jnp.sum(dy * xr, axis=0, keepdims=True)
            return r * (u - xr * jnp.mean(u * xr, axis=-1, keepdims=True))

        xq, _ = _rms_rows(cq_ref[0], None)
        cq_n = xq * gq_ref[...]
        dwq_ref[...] += _dot_tn(cq_n, dqh[...])
        dcq_ref[0] = norm_bwd(cq_ref[0], gq_ref[...], _dot_nt(dqh[...], wq_ref[...]), dgq_ref).astype(dcq_ref.dtype)
        xk, _ = _rms_rows(ckv_ref[0], None)
        ckv_n = xk * gkv_ref[...]
        dkf_, dv_ = dkf_ref[0], dv_ref[0]
        dwk_ref[...] += _dot_tn(ckv_n, dkf_)
        dwv_ref[...] += _dot_tn(ckv_n, dv_)
        dckv_n = _dot_nt(dkf_, wk_ref[...]) + _dot_nt(dv_, wv_ref[...])
        dckv_ref[0] = norm_bwd(ckv_ref[0], gkv_ref[...], dckv_n, dgkv_ref).astype(dckv_ref.dtype)

    def col(first, width):
        return pl.BlockSpec((1, BLOCK, width), lambda bi, n: (bi, n, first // width))

    def whole(a):
        return pl.BlockSpec(a.shape, lambda bi, n: (0,) * len(a.shape))

    tab = pl.BlockSpec((BLOCK, BLOCK), lambda bi, n: (n, 0))
    acc_shapes = [jax.ShapeDtypeStruct(a.shape, F32) for a in (wq, wk, wv, gq, gkv)]
    return pl.pallas_call(
        body, grid=(b, nb),
        in_specs=[col(EV_CQ, MLA_Q_LORA), col(EV_CKV, MLA_KV_LORA), whole(gq), whole(gkv), whole(wq), whole(wk), whole(wv), tab, tab, tab,
                  col(0, hw), col(0, hw), col(0, MLA_HEADS * HEAD)],
        out_specs=[col(0, MLA_Q_LORA), col(0, MLA_KV_LORA), col(0, BLOCK)] + [whole(a) for a in acc_shapes],
        out_shape=[jax.ShapeDtypeStruct((b, lp, MLA_Q_LORA), BF16), jax.ShapeDtypeStruct((b, lp, MLA_KV_LORA), BF16),
                   jax.ShapeDtypeStruct((b, lp, BLOCK), BF16)] + acc_shapes,
        scratch_shapes=[pltpu.VMEM((BLOCK, hw), BF16)],
        compiler_params=_params(("arbitrary", "arbitrary")), name=name,
    )(proj3, proj3, gq, gkv, wq, wk, wv, *tabs, dqf, dkf, dv)


def _mla_fwd(qf, kf, v, name):
    b, lp, _ = qf.shape
    nb = lp // BLOCK
    npair = MLA_HEADS // 2
    scale = (MLA_NOPE + MLA_ROPE) ** -0.5

    def body(q_ref, k_ref, v_ref, o_ref, lse_ref):
        lo = _lo_lanes()

        def qtile(r0, bq, n):
            qs = [q_ref[0, pl.ds(r0, bq), h * BLOCK:(h + 1) * BLOCK] for h in range(2)]
            t_idx = r0 + _iota2((bq, KEYS), 0)

            def kchunk(c, carry):
                stats, acc = carry[:4], carry[4]
                s0, valid = _key_chunk(c, lp, t_idx, False)
                ps, new, alphas = [], [], []
                for h in range(2):
                    m, l = stats[2 * h], stats[2 * h + 1]
                    s = jnp.where(valid, _dot_nt(qs[h], k_ref[0, pl.ds(s0, KEYS), h * BLOCK:(h + 1) * BLOCK]) * scale, NEG)
                    m_new = jnp.maximum(m, jnp.max(s, axis=1, keepdims=True))
                    p = jnp.where(valid, jnp.exp(s - m_new), 0.0)
                    alpha = jnp.exp(m - m_new)
                    new += [m_new, alpha * l + jnp.sum(p, axis=1, keepdims=True)]
                    alphas.append(alpha)
                    ps.append(p.astype(BF16))
                pv = jnp.dot(jnp.concatenate(ps, axis=1), _stack_halves(v_ref[0, pl.ds(s0, KEYS), :], lo), preferred_element_type=F32)
                return (*new, _pair(alphas[0], alphas[1], lo) * acc + pv)

            neg, zero = jnp.full((bq, 1), NEG, F32), jnp.zeros((bq, 1), F32)
            m_a, l_a, m_b, l_b, acc = lax.fori_loop(0, n, kchunk, (neg, zero, neg, zero, jnp.zeros((bq, BLOCK), F32)))
            l = _pair(l_a, l_b, lo)
            seen = l > 0.0
            safe = jnp.where(seen, l, 1.0)
            o_ref[0, pl.ds(r0, bq), :] = acc / safe
            lse_ref[0, pl.ds(r0, bq), :] = jnp.where(seen, _pair(m_a, m_b, lo) + jnp.log(safe), 0.0)

        _for_query_tiles(nb, qtile)

    wide = pl.BlockSpec((1, lp, 2 * BLOCK), lambda bi, hp: (bi, 0, hp))
    thin = pl.BlockSpec((1, lp, 2 * HEAD), lambda bi, hp: (bi, 0, hp))
    shp = jax.ShapeDtypeStruct((b, lp, MLA_HEADS * HEAD), F32)
    return pl.pallas_call(
        body, grid=(b, npair), in_specs=[wide, wide, thin], out_specs=[thin, thin], out_shape=[shp, shp],
        compiler_params=_params(("parallel", "parallel")), name=name,
    )(qf, kf, v)


def _mla_bwd(qf, kf, v, o, lse, do, name):
    b, lp, _ = qf.shape
    nb = lp // BLOCK
    npair = MLA_HEADS // 2
    scale = (MLA_NOPE + MLA_ROPE) ** -0.5

    def body(q_ref, k_ref, v_ref, o_ref, lse_ref, do_ref, dq_ref, dk_ref, dv_ref):
        lo = _lo_lanes()
        dk_ref[...] = jnp.zeros_like(dk_ref)
        dv_ref[...] = jnp.zeros_like(dv_ref)

        def qtile(r0, bq, n):
            rows = pl.ds(r0, bq)
            qs = [q_ref[0, rows, h * BLOCK:(h + 1) * BLOCK] for h in range(2)]
            do_i = do_ref[0, rows, :]
            dos = _halves(do_i.astype(BF16), lo)
            both = do_i * o_ref[0, rows, :]
            d_a = jnp.sum(jnp.where(lo, both, 0.0), axis=1, keepdims=True)
            dsum = (d_a, jnp.sum(both, axis=1, keepdims=True) - d_a)
            lse_i = lse_ref[0, rows, :]
            lses = (lse_i[:, 0:1], lse_i[:, HEAD:HEAD + 1])
            t_idx = r0 + _iota2((bq, KEYS), 0)

            def kchunk(c, dqs):
                s0, valid = _key_chunk(c, lp, t_idx, False)
                keys = pl.ds(s0, KEYS)
                v_c = v_ref[0, keys, :]
                ps, out = [], []
                for h in range(2):
                    lanes = slice(h * BLOCK, (h + 1) * BLOCK)
                    k = k_ref[0, keys, lanes]
                    s = jnp.where(valid, _dot_nt(qs[h], k) * scale, NEG)
                    p = jnp.where(valid, jnp.exp(s - lses[h]), 0.0)
                    ds = p * (_dot_nt(dos[h], v_c) - dsum[h]) * scale
                    dk_ref[0, keys, lanes] += _dot_tn(ds, qs[h])
                    out.append(dqs[h] + _dot(ds, k))
                    ps.append(p.astype(BF16))
                dv_ref[0, keys, :] += _dot_tn(jnp.concatenate(ps, axis=0), jnp.concatenate(dos, axis=0))
                return tuple(out)

            zero = jnp.zeros((bq, BLOCK), F32)
            dq_a, dq_b = lax.fori_loop(0, n, kchunk, (zero, zero))
            dq_ref[0, rows, 0:BLOCK] = dq_a
            dq_ref[0, rows, BLOCK:2 * BLOCK] = dq_b

        _for_query_tiles(nb, qtile)

    wide = pl.BlockSpec((1, lp, 2 * BLOCK), lambda bi, hp: (bi, 0, hp))
    thin = pl.BlockSpec((1, lp, 2 * HEAD), lambda bi, hp: (bi, 0, hp))
    return pl.pallas_call(
        body, grid=(b, npair), in_specs=[wide, wide, thin, thin, thin, thin], out_specs=[wide, wide, thin],
        out_shape=[jax.ShapeDtypeStruct(qf.shape, F32), jax.ShapeDtypeStruct(qf.shape, F32), jax.ShapeDtypeStruct(v.shape, F32)],
        compiler_params=_params(("parallel", "parallel")), name=name,
    )(qf, kf, v, o, lse, do)


def _swa_keys(k_ref, v_ref, n, kv, lo):
    prev = jnp.maximum(n - 1, 0)
    rows = lambda blk: pl.ds(pl.multiple_of(blk * BLOCK, BLOCK), BLOCK)
    mine = (_iota2((1, BLOCK), 1) >= HEAD).astype(jnp.int32) == kv

    def both_halves(ref):
        x = jnp.concatenate([ref[0, rows(prev), :], ref[0, rows(n), :], ref[0, 0:BLOCK, :]], axis=0)
        return jnp.where(mine, x, pltpu.roll(x, HEAD, 1)).astype(BF16)

    col = _iota2((BLOCK, 3 * BLOCK), 1)
    loc = col % BLOCK
    s_idx = jnp.where(col < BLOCK, (n - 1) * BLOCK + loc, jnp.where(col < 2 * BLOCK, n * BLOCK + loc, loc))
    dist = n * BLOCK + _iota2((BLOCK, 3 * BLOCK), 0) - s_idx
    band = (col < 2 * BLOCK) & (dist >= 0) & (dist < SWA_WINDOW) & (s_idx >= BLOCK)
    meta = (col >= 2 * BLOCK) & (s_idx >= N_PAD) & (dist >= 0)
    return both_halves(k_ref), both_halves(v_ref), band | meta, dist.astype(F32), prev


def _swa_probs(q_h, kdup, valid, dist, head, sink_ref):
    slope = jnp.exp(jnp.full((1, 1), -8.0 * math.log(2.0) / SWA_HEADS, F32) * (head + 1).astype(F32))
    s = jnp.where(valid, _dot_nt(q_h, kdup) * (HEAD ** -0.5) - slope * dist, NEG)
    sink = sink_ref[pl.ds(head, 1), 0:1]
    m = jnp.maximum(jnp.max(s, axis=1, keepdims=True), sink)
    e = jnp.where(valid, jnp.exp(s - m), 0.0)
    es = jnp.exp(sink - m)
    inv = 1.0 / (jnp.sum(e, axis=1, keepdims=True) + es)
    return e * inv, es * inv


SWA_PAIRS = SWA_HEADS // SWA_KV_HEADS // 2
SWA_GROUP = SWA_PAIRS * 2 * HEAD


def _swa_specs(b, lp):
    nb = lp // BLOCK
    qcol = lambda first: pl.BlockSpec((1, BLOCK, SWA_GROUP), lambda bi, kv, n: (bi, n, first // SWA_GROUP + kv))
    kcol = lambda first: pl.BlockSpec((1, lp, BLOCK), lambda bi, kv, n: (bi, 0, first // BLOCK))
    sink = pl.BlockSpec((SWA_HEADS, BLOCK), lambda bi, kv, n: (0, 0))
    return (b, SWA_KV_HEADS, nb), qcol, kcol, sink


def _swa_fwd(proj3, sinks, name):
    b, lp, _ = proj3.shape
    grid, qcol, kcol, sink = _swa_specs(b, lp)

    def body(q_ref, k_ref, v_ref, sink_ref, o_ref):
        kv, n = pl.program_id(1), pl.program_id(2)
        lo = _lo_lanes()
        kdup, vdup, valid, dist, _ = _swa_keys(k_ref, v_ref, n, kv, lo)
        v_bd = _stack_halves(vdup, lo)
        for p in range(SWA_PAIRS):
            lanes = slice(p * BLOCK, (p + 1) * BLOCK)
            qs = _halves(q_ref[0, :, lanes].astype(BF16), lo)
            probs = [_swa_probs(qs[hh], kdup, valid, dist, (kv * SWA_PAIRS + p) * 2 + hh, sink_ref)[0].astype(BF16) for hh in range(2)]
            o_ref[0, :, lanes] = jnp.dot(jnp.concatenate(probs, axis=1), v_bd, preferred_element_type=F32)

    return pl.pallas_call(
        body, grid=grid, in_specs=[qcol(OD_Q), kcol(OD_K), kcol(OD_V), sink], out_specs=qcol(0),
        out_shape=jax.ShapeDtypeStruct((b, lp, SWA_HEADS * HEAD), F32),
        compiler_params=_params(("parallel", "parallel", "parallel")), name=name,
    )(proj3, proj3, proj3, sinks)


def _swa_bwd(proj3, sinks, do, name):
    b, lp, _ = proj3.shape
    nb = lp // BLOCK
    grid, qcol, kcol, sink = _swa_specs(b, lp)

    def body(q_ref, k_ref, v_ref, sink_ref, do_ref, dq_ref, dk_ref, dv_ref, dsink_ref):
        kv, n = pl.program_id(1), pl.program_id(2)

        @pl.when((n == 0) & (pl.program_id(0) == 0) & (kv == 0))
        def _():
            dsink_ref[...] = jnp.zeros_like(dsink_ref)

        @pl.when(n == 0)
        def _():
            dk_ref[...] = jnp.zeros_like(dk_ref)
            dv_ref[...] = jnp.zeros_like(dv_ref)

        lo = _lo_lanes()
        kdup, vdup, valid, dist, prev = _swa_keys(k_ref, v_ref, n, kv, lo)
        k_bd = _stack_halves(kdup, lo)
        dkc = jnp.zeros((3 * BLOCK, BLOCK), F32)
        dvc = jnp.zeros((3 * BLOCK, BLOCK), F32)
        for p in range(SWA_PAIRS):
            lanes = slice(p * BLOCK, (p + 1) * BLOCK)
            qs = _halves(q_ref[0, :, lanes].astype(BF16), lo)
            dos = _halves(do_ref[0, :, lanes].astype(BF16), lo)
            dss, prs = [], []
            for hh in range(2):
                head = (kv * SWA_PAIRS + p) * 2 + hh
                pr, ps = _swa_probs(qs[hh], kdup, valid, dist, head, sink_ref)
                dp = _dot_nt(dos[hh], vdup)
                dsum = jnp.sum(pr * dp, axis=1, keepdims=True)
                dsink_ref[pl.ds(head, 1), :] += jnp.broadcast_to(-jnp.sum(ps * dsum, axis=0, keepdims=True), (1, BLOCK))
                dss.append((pr * (dp - dsum) * (HEAD ** -0.5)).astype(BF16))
                prs.append(pr.astype(BF16))
            dq_ref[0, :, lanes] = jnp.dot(jnp.concatenate(dss, axis=1), k_bd, preferred_element_type=F32)
            dkc = dkc + _dot_tn(jnp.concatenate(dss, axis=0), jnp.concatenate(qs, axis=0))
            dvc = dvc + _dot_tn(jnp.concatenate(prs, axis=0), jnp.concatenate(dos, axis=0))
        rows = lambda blk: pl.ds(pl.multiple_of(blk * BLOCK, BLOCK), BLOCK)
        for part, r in enumerate((rows(prev), rows(n), slice(0, BLOCK))):
            dk_ref[0, 0, r, :] += dkc[part * BLOCK:(part + 1) * BLOCK]
            dv_ref[0, 0, r, :] += dvc[part * BLOCK:(part + 1) * BLOCK]

        @pl.when(n == nb - 1)
        def _():
            for ref in (dk_ref, dv_ref):
                x = ref[0, 0]
                ref[0, 0] = x + pltpu.roll(x, HEAD, 1)

    kvout = pl.BlockSpec((1, 1, lp, BLOCK), lambda bi, kv, n: (bi, kv, 0, 0))
    kvshape = jax.ShapeDtypeStruct((b, SWA_KV_HEADS, lp, BLOCK), F32)
    return pl.pallas_call(
        body, grid=grid, in_specs=[qcol(OD_Q), kcol(OD_K), kcol(OD_V), sink, qcol(0)], out_specs=[qcol(0), kvout, kvout, sink],
        out_shape=[jax.ShapeDtypeStruct((b, lp, SWA_HEADS * HEAD), F32), kvshape, kvshape, jax.ShapeDtypeStruct((SWA_HEADS, BLOCK), F32)],
        compiler_params=_params(("arbitrary", "arbitrary", "arbitrary")), name=name,
    )(proj3, proj3, proj3, sinks, do)


def _kernel_weights(ev_w_in, ev_w_uq, ev_w_ukv, od_w_in):
    zeros = lambda r, c: jnp.zeros((r, c), ev_w_in.dtype)
    q_sb, k_sb, v_sb, g_sb, c_q, c_kv, k_r, g_mla = jnp.split(ev_w_in, [512, 1024, 1536, 2048, 2304, 2432, 2464], axis=1)
    w0 = jnp.concatenate([g_sb, g_mla, q_sb, k_sb, v_sb, c_q, c_kv, zeros(D_MODEL, MLA_NOPE), k_r, zeros(D_MODEL, 32)], axis=1)
    uq = ev_w_uq.reshape(MLA_Q_LORA, MLA_HEADS, MLA_NOPE + MLA_ROPE)
    wq = jnp.pad(uq, ((0, 0), (0, 0), (0, BLOCK - MLA_NOPE - MLA_ROPE))).reshape(MLA_Q_LORA, MLA_HEADS * BLOCK)
    ukv = ev_w_ukv.reshape(MLA_KV_LORA, MLA_HEADS, BLOCK)
    wk = jnp.pad(ukv[:, :, :MLA_NOPE], ((0, 0), (0, 0), (0, BLOCK - MLA_NOPE))).reshape(MLA_KV_LORA, MLA_HEADS * BLOCK)
    wv = ukv[:, :, MLA_NOPE:].reshape(MLA_KV_LORA, MLA_HEADS * HEAD)
    q, k, v, g = jnp.split(od_w_in, [1024, 1152, 1280], axis=1)
    w1 = jnp.concatenate([g, q, k, v], axis=1)
    return w0, wq, wk, wv, w1


def _original_grads(dw0, dwq, dwk, dwv, dw1):
    sl = lambda a, first, n: a[:, first:first + n]
    d_ev_w_in = jnp.concatenate([sl(dw0, EV_Q, 512), sl(dw0, EV_K, 512), sl(dw0, EV_V, 512), sl(dw0, EV_G, 512), sl(dw0, EV_CQ, 256),
                                 sl(dw0, EV_CKV, 128), sl(dw0, EV_KR + MLA_NOPE, MLA_ROPE), sl(dw0, EV_G + 512, 512)], axis=1)
    d_uq = dwq.reshape(MLA_Q_LORA, MLA_HEADS, BLOCK)[:, :, :MLA_NOPE + MLA_ROPE].reshape(MLA_Q_LORA, -1)
    d_ukv = jnp.concatenate([dwk.reshape(MLA_KV_LORA, MLA_HEADS, BLOCK)[:, :, :MLA_NOPE], dwv.reshape(MLA_KV_LORA, MLA_HEADS, HEAD)],
                            axis=2).reshape(MLA_KV_LORA, -1)
    d_od_w_in = jnp.concatenate([sl(dw1, OD_Q, 1024), sl(dw1, OD_K, 128), sl(dw1, OD_V, 128), sl(dw1, OD_G, 1024)], axis=1)
    return d_ev_w_in, d_uq, d_ukv, d_od_w_in


def _meta_rows_sum(dh0_3):
    b, _, d = dh0_3.shape

    def body(x_ref, o_ref):
        acc = x_ref[0, N_PAD:BLOCK, :]
        for i in range(1, b):
            acc = acc + x_ref[i, N_PAD:BLOCK, :]
        o_ref[...] = acc

    return pl.pallas_call(
        body, grid=(1,), in_specs=[pl.BlockSpec((b, BLOCK, d), lambda i: (0, 0, 0))], out_specs=pl.BlockSpec((N_META, d), lambda i: (0, 0)),
        out_shape=jax.ShapeDtypeStruct((N_META, d), F32), compiler_params=_params(("arbitrary",)), name="meta_rows_sum",
    )(dh0_3)


def _local_step(x, meta, norm_g, final_g, gq, gkv, sinks, target, ev_w_in, ev_w_uq, ev_w_ukv, wo0, od_w_in, wo1):
    b, seq, d = x.shape
    lp = seq + BLOCK
    t = b * lp
    w0, wq, wk, wv, w1 = _kernel_weights(ev_w_in, ev_w_uq, ev_w_ukv, od_w_in)
    h0 = jnp.concatenate([jnp.zeros((b, N_PAD, d), F32), jnp.broadcast_to(meta[None], (b, N_META, d)), x], axis=1).reshape(t, d)
    tabs = _rope_tables(lp)
    g0, g1 = norm_g[0:1], norm_g[1:2]

    hn0 = _rms_fwd(h0, g0, "norm0")
    proj0 = _mm(hn0, w0, "inproj0")
    p0 = proj0.reshape(b, lp, EV_N)
    o_sb, sb_tot = _sb_fwd(p0, "sb_fwd")
    qf, kf, v = _mla_prep_fwd(p0, gq, gkv, wq, wk, wv, tabs, "mla_prep_fwd")
    o_mla, lse = _mla_fwd(qf, kf, v, "mla_fwd")
    o0 = [o_sb.reshape(t, -1), o_mla.reshape(t, -1)]
    ao0 = _gate_fwd(o0, proj0, "gate0")
    h1 = _mm(ao0, wo0, "outproj0", res=h0)

    hn1 = _rms_fwd(h1, g1, "norm1")
    proj1 = _mm(hn1, w1, "inproj1")
    p1 = proj1.reshape(b, lp, OD_N)
    sinks_b = jnp.broadcast_to(sinks.reshape(SWA_HEADS, 1), (SWA_HEADS, BLOCK))
    o1 = _swa_fwd(p1, sinks_b, "swa_fwd").reshape(t, -1)
    ao1 = _gate_fwd([o1], proj1, "gate1")
    h2 = _mm(ao1, wo1, "outproj1", res=h1)

    dh2, d_final_g, loss = _loss_head(h2, final_g.reshape(1, d), target, b, lp)

    d_wo1 = _mm_tn(ao1, dh2, "d_wo1")
    dao1 = _mm_nt(dh2, wo1, "d_ao1")
    (do1,), dg1 = _gate_bwd(dao1, [o1], proj1, "gate1_bwd")
    dq1, dk4, dv4, d_sinks = _swa_bwd(p1, sinks_b, do1.reshape(b, lp, -1), "swa_bwd")
    unheads = lambda a: a[..., :HEAD].transpose(0, 2, 1, 3).reshape(t, SWA_KV_HEADS * HEAD).astype(BF16)
    dproj1 = jnp.concatenate([dg1, dq1.reshape(t, -1).astype(BF16), unheads(dk4), unheads(dv4)], axis=1)
    d_w1 = _mm_tn(hn1, dproj1, "d_w1")
    dhn1 = _mm_nt(dproj1, w1, "d_hn1")
    dh1, d_g1 = _rms_bwd(h1, g1, dhn1, dh2, "norm1_bwd")

    d_wo0 = _mm_tn(ao0, dh1, "d_wo0")
    dao0 = _mm_nt(dh1, wo0, "d_ao0")
    (do_sb, do_mla), dg0 = _gate_bwd(dao0, o0, proj0, "gate0_bwd")
    dq_sb, dk_sb, dv_sb = _sb_bwd(p0, sb_tot, do_sb.reshape(b, lp, -1), "sb_bwd")
    dqf, dkf, dv = _mla_bwd(qf, kf, v, o_mla, lse, do_mla.reshape(b, lp, -1), "mla_bwd")
    dcq, dckv, dkr, d_wq, d_wk, d_wv, d_gq, d_gkv = _mla_prep_bwd(p0, gq, gkv, wq, wk, wv, tabs, dqf, dkf, dv, "mla_prep_bwd")
    flat = lambda a: a.reshape(t, -1).astype(BF16)
    dproj0 = jnp.concatenate([dg0, flat(dq_sb), flat(dk_sb), flat(dv_sb), flat(dcq), flat(dckv), flat(dkr)], axis=1)
    d_w0 = _mm_tn(hn0, dproj0, "d_w0")
    dhn0 = _mm_nt(dproj0, w0, "d_hn0")
    dh0, d_g0 = _rms_bwd(h0, g0, dhn0, dh1, "norm0_bwd")
    dh0 = dh0.reshape(b, lp, d)

    d_ev_w_in, d_uq, d_ukv, d_od_w_in = _original_grads(d_w0, d_wq, d_wk, d_wv, d_w1)
    grads = dict(meta=_meta_rows_sum(dh0), norm_g=jnp.concatenate([d_g0, d_g1], axis=0), final_g=d_final_g.reshape(d),
                 ev_w_in=d_ev_w_in, ev_q_norm_g=d_gq, ev_kv_norm_g=d_gkv, ev_w_uq=d_uq, ev_w_ukv=d_ukv, ev_w_out=d_wo0,
                 od_w_in=d_od_w_in, od_sinks=d_sinks[:, 0].reshape(1, SWA_HEADS), od_w_out=d_wo1)
    return loss, dh0[:, BLOCK:], grads


MESH = pl.DeviceIdType.MESH
ANY = pl.BlockSpec(memory_space=pl.ANY)


def _place():
    return lax.axis_index("x"), lax.axis_index("y"), lax.axis_index("c")


def _other_chips(x, y):
    return [(1 - x, y), (x, 1 - y), (1 - x, 1 - y)]


PACK_ROWS = 1920
HALF_ROWS = PACK_ROWS // 2


def _with_own_slot(slots, own):
    me = 2 * lax.axis_index("x") + lax.axis_index("y")
    return lax.dynamic_update_slice(slots, own[None], (me,) + (0,) * own.ndim)


def _gather_weights(wpack, meta, name):
    def body(w_ref, m_ref, wo_ref, mo_ref, send_sems, recv_sems):
        x, y, c = _place()
        me, sib = 2 * x + y, (x, y, 1 - c)
        chips = _other_chips(x, y)
        mine = w_ref.at[pl.ds(c * HALF_ROWS, HALF_ROWS), :]
        half = lambda chip, h: wo_ref.at[chip, pl.ds(h * HALF_ROWS, HALF_ROWS), :]

        def copy(k, src, dst, to):
            return pltpu.make_async_remote_copy(src_ref=src, dst_ref=dst, send_sem=send_sems.at[k], recv_sem=recv_sems.at[k], device_id=to,
                                                device_id_type=MESH)

        sent = [copy(k, mine, half(me, c), (px, py, c)) for k, (px, py) in enumerate(chips)]
        sent += [copy(6 + k, m_ref, mo_ref.at[me], (px, py, c)) for k, (px, py) in enumerate(chips)]
        for cp in sent:
            cp.start()
        for k, (px, py) in enumerate(chips):
            landed = half(2 * px + py, c)
            copy(k, mine, landed, (px, py, c)).wait_recv()
            fwd = copy(3 + k, landed, landed, sib)
            fwd.start()
            sent.append(fwd)
        for k, (px, py) in enumerate(chips):
            other = half(2 * px + py, 1 - c)
            copy(3 + k, other, other, sib).wait_recv()
            copy(6 + k, m_ref, mo_ref.at[2 * px + py], (px, py, c)).wait_recv()
        for cp in sent:
            cp.wait_send()

    wall, meta_all = pl.pallas_call(
        body, in_specs=[ANY, ANY], out_specs=[ANY, ANY],
        out_shape=[jax.ShapeDtypeStruct((N_CHIPS,) + wpack.shape, wpack.dtype), jax.ShapeDtypeStruct((N_CHIPS,) + meta.shape, meta.dtype)],
        scratch_shapes=[pltpu.SemaphoreType.DMA((9,)), pltpu.SemaphoreType.DMA((9,))],
        name=name,
    )(wpack, meta)
    return _with_own_slot(wall, wpack), _with_own_slot(meta_all, meta)


def _grads_to_sibling(g, name):
    def body(g_ref, o_ref, send_sem, recv_sem):
        x, y, c = _place()
        cp = pltpu.make_async_remote_copy(src_ref=g_ref.at[:, pl.ds((1 - c) * HALF_ROWS, HALF_ROWS), :], dst_ref=o_ref, send_sem=send_sem,
                                          recv_sem=recv_sem, device_id=(x, y, 1 - c), device_id_type=MESH)
        cp.start()
        cp.wait()

    return pl.pallas_call(
        body, in_specs=[ANY], out_specs=ANY, out_shape=jax.ShapeDtypeStruct((g.shape[0], HALF_ROWS, g.shape[2]), g.dtype),
        scratch_shapes=[pltpu.SemaphoreType.DMA(()), pltpu.SemaphoreType.DMA(())],
        name=name,
    )(g)


def _share_halves(r, name):
    def body(r_ref, o_ref, send_sem, recv_sem):
        x, y, c = _place()
        cp = pltpu.make_async_remote_copy(src_ref=r_ref, dst_ref=o_ref, send_sem=send_sem, recv_sem=recv_sem, device_id=(x, y, 1 - c),
                                          device_id_type=MESH)
        cp.start()
        cp.wait()

    theirs = pl.pallas_call(
        body, in_specs=[ANY], out_specs=ANY, out_shape=jax.ShapeDtypeStruct(r.shape, r.dtype),
        scratch_shapes=[pltpu.SemaphoreType.DMA(()), pltpu.SemaphoreType.DMA(())],
        name=name,
    )(r)
    return jnp.where(lax.axis_index("c") == 0, jnp.concatenate([r, theirs], axis=0), jnp.concatenate([theirs, r], axis=0))


def _chip_scatter(s, name):
    def body(s_ref, r_ref, send_sems, recv_sems):
        x, y, c = _place()
        me = 2 * x + y
        for k, (px, py) in enumerate(_other_chips(x, y)):
            pltpu.make_async_remote_copy(src_ref=s_ref.at[2 * px + py], dst_ref=r_ref.at[me], send_sem=send_sems.at[k],
                                         recv_sem=recv_sems.at[k], device_id=(px, py, c), device_id_type=MESH).start()
        for k, (px, py) in enumerate(_other_chips(x, y)):
            cp = pltpu.make_async_remote_copy(src_ref=s_ref.at[2 * px + py], dst_ref=r_ref.at[2 * px + py], send_sem=send_sems.at[k],
                                              recv_sem=recv_sems.at[k], device_id=(px, py, c), device_id_type=MESH)
            cp.wait_recv()
            cp.wait_send()

    parts = pl.pallas_call(
        body, in_specs=[ANY], out_specs=ANY, out_shape=jax.ShapeDtypeStruct(s.shape, s.dtype),
        scratch_shapes=[pltpu.SemaphoreType.DMA((3,)), pltpu.SemaphoreType.DMA((3,))],
        name=name,
    )(s)
    me = 2 * lax.axis_index("x") + lax.axis_index("y")
    return _with_own_slot(parts, lax.dynamic_index_in_dim(s, me, axis=0, keepdims=False))


def _all_reduce_small(v, name):
    shape = v.shape

    def body(v_ref, o_ref, slots, send_sems, recv_sems):
        x, y, c = _place()
        me = 4 * x + 2 * y + c
        slots[me] = v_ref[...]
        for r in range(1, N_DEV):
            peer = (x ^ (r >> 2), y ^ ((r >> 1) & 1), c ^ (r & 1))
            pltpu.make_async_remote_copy(src_ref=v_ref, dst_ref=slots.at[me], send_sem=send_sems.at[r - 1], recv_sem=recv_sems.at[r - 1],
                                         device_id=peer, device_id_type=MESH).start()
        for r in range(1, N_DEV):
            peer = (x ^ (r >> 2), y ^ ((r >> 1) & 1), c ^ (r & 1))
            cp = pltpu.make_async_remote_copy(src_ref=v_ref, dst_ref=slots.at[4 * peer[0] + 2 * peer[1] + peer[2]], send_sem=send_sems.at[r - 1],
                                              recv_sem=recv_sems.at[r - 1], device_id=peer, device_id_type=MESH)
            cp.wait_recv()
            cp.wait_send()
        acc = slots[0]
        for d in range(1, N_DEV):
            acc = acc + slots[d]
        o_ref[...] = acc

    vm = pl.BlockSpec(memory_space=pltpu.VMEM)
    return pl.pallas_call(
        body, in_specs=[vm], out_specs=vm, out_shape=jax.ShapeDtypeStruct(shape, F32),
        scratch_shapes=[pltpu.VMEM((N_DEV,) + shape, F32), pltpu.SemaphoreType.DMA((N_DEV - 1,)), pltpu.SemaphoreType.DMA((N_DEV - 1,))],
        name=name,
    )(v)


SUM_ROWS = 480


def _add_sibling(g, gsib, core, name):
    n, _, cdim = g.shape
    per_half = HALF_ROWS // SUM_ROWS

    def body(core_ref, a_ref, b_ref, o_ref):
        o_ref[...] = (a_ref[...] + b_ref[...]).astype(o_ref.dtype)

    blk = pl.BlockSpec((1, SUM_ROWS, cdim), lambda j, i, core_ref: (j, i, 0))
    return pl.pallas_call(
        body,
        grid_spec=pltpu.PrefetchScalarGridSpec(
            num_scalar_prefetch=1, grid=(n, per_half),
            in_specs=[pl.BlockSpec((1, SUM_ROWS, cdim), lambda j, i, core_ref: (j, core_ref[0] * per_half + i, 0)), blk], out_specs=blk),
        out_shape=jax.ShapeDtypeStruct(gsib.shape, BF16), compiler_params=_params(("parallel", "parallel")), name=name,
    )(core, g, gsib)


def _sum_parts(parts, name):
    n, r, cdim = parts.shape

    def body(p_ref, o_ref):
        acc = p_ref[0].astype(F32)
        for j in range(1, n):
            acc = acc + p_ref[j].astype(F32)
        o_ref[...] = acc

    return pl.pallas_call(
        body, grid=(r // SUM_ROWS,), in_specs=[pl.BlockSpec((n, SUM_ROWS, cdim), lambda i: (0, i, 0))],
        out_specs=pl.BlockSpec((SUM_ROWS, cdim), lambda i: (i, 0)), out_shape=jax.ShapeDtypeStruct((r, cdim), F32),
        compiler_params=_params(("parallel",)), name=name,
    )(parts)


def _adamw(parts, w, m, v, name):
    npart, r, cdim = parts.shape
    tr = 384 if r % 384 == 0 else r

    def body(p_ref, w_ref, m_ref, v_ref, g_ref, d_ref, nm_ref, nv_ref):
        g = p_ref[0]
        for j in range(1, npart):
            g = g + p_ref[j]
        m_new = ADAM_B1 * m_ref[...] + (1.0 - ADAM_B1) * g
        v_new = ADAM_B2 * v_ref[...] + (1.0 - ADAM_B2) * (g * g)
        m_hat = m_new / (1.0 - ADAM_B1 ** ADAM_STEP)
        v_hat = v_new / (1.0 - ADAM_B2 ** ADAM_STEP)
        g_ref[...] = g
        d_ref[...] = -ADAM_LR * (m_hat / (jnp.sqrt(v_hat) + ADAM_EPS) + ADAM_WD * w_ref[...])
        nm_ref[...] = m_new
        nv_ref[...] = v_new

    blk = pl.BlockSpec((tr, cdim), lambda i: (i, 0))
    shp = jax.ShapeDtypeStruct((r, cdim), F32)
    return pl.pallas_call(
        body, grid=(r // tr,), in_specs=[pl.BlockSpec((npart, tr, cdim), lambda i: (0, i, 0)), blk, blk, blk], out_specs=[blk] * 4,
        out_shape=[shp] * 4, compiler_params=_params(("parallel",)), name=name,
    )(parts, w, m, v)


BIG = ("ev_w_in", "ev_w_uq", "ev_w_ukv", "ev_w_out", "od_w_in", "od_w_out", "meta")
SMALL = ("norm_g", "final_g", "ev_q_norm_g", "ev_kv_norm_g", "od_sinks")
SMALL_SHAPE = (8, 512)


def _pack_big(arrs):
    rows = [a.reshape(-1, D_MODEL) for a in arrs]
    used = sum(r.shape[0] for r in rows)
    return jnp.pad(jnp.concatenate(rows, axis=0), ((0, PACK_ROWS - used), (0, 0)))


def _unpack_big(p, shapes):
    out, at = [], 0
    for s in shapes:
        nrow = int(np.prod(s)) // D_MODEL
        out.append(p[at:at + nrow].reshape(s))
        at += nrow
    return out


def _pack_small(arrs, extra=None):
    flat = [a.reshape(-1) for a in arrs] + ([] if extra is None else [extra.reshape(-1)])
    used = sum(f.shape[0] for f in flat)
    return jnp.pad(jnp.concatenate(flat), (0, SMALL_SHAPE[0] * SMALL_SHAPE[1] - used)).reshape(SMALL_SHAPE)


def _unpack_small(p, shapes):
    flat, out, at = p.reshape(-1), [], 0
    for s in shapes:
        n = int(np.prod(s))
        out.append(flat[at:at + n].reshape(s))
        at += n
    return out, flat[at]


def _shard_of(full, name, j):
    if name in ("ev_w_out", "od_w_out"):
        n = full.shape[0] // N_CHIPS
        return full[j * n:(j + 1) * n]
    n = full.shape[1] // N_CHIPS
    return full[:, j * n:(j + 1) * n]


def kernel(x, meta, norm_g, final_g, ev_w_in, ev_q_norm_g, ev_kv_norm_g, ev_w_uq, ev_w_ukv, ev_w_out, od_w_in, od_sinks, od_w_out, loss_target, m_meta, m_norm_g, m_final_g, m_ev_w_in, m_ev_q_norm_g, m_ev_kv_norm_g, m_ev_w_uq, m_ev_w_ukv, m_ev_w_out, m_od_w_in, m_od_sinks, m_od_w_out, v_meta, v_norm_g, v_final_g, v_ev_w_in, v_ev_q_norm_g, v_ev_kv_norm_g, v_ev_w_uq, v_ev_w_ukv, v_ev_w_out, v_od_w_in, v_od_sinks, v_od_w_out):
    given = dict(locals())
    big_w = [given[n][0] if given[n].ndim == 3 else given[n] for n in BIG]
    big_shapes = [given[n].shape for n in BIG]

    wpack = _pack_big(big_w[:-1]).astype(BF16)
    wall, meta_all = _gather_weights(wpack, meta, "gather_weights")
    per_chip = [_unpack_big(wall[j], [a.shape for a in big_w[:-1]]) for j in range(N_CHIPS)]
    cat = lambda i, axis: jnp.concatenate([per_chip[j][i] for j in range(N_CHIPS)], axis=axis)
    full = dict(ev_w_in=cat(0, 1), ev_w_uq=cat(1, 1), ev_w_ukv=cat(2, 1), ev_w_out=cat(3, 0), od_w_in=cat(4, 1), od_w_out=cat(5, 0))
    meta_full = jnp.concatenate([meta_all[j] for j in range(N_CHIPS)], axis=1)

    loss, grad_x, grads = _local_step(x, meta_full, norm_g, final_g, ev_q_norm_g, ev_kv_norm_g, od_sinks, loss_target,
                                      full["ev_w_in"], full["ev_w_uq"], full["ev_w_ukv"], full["ev_w_out"], full["od_w_in"], full["od_w_out"])

    gpack = jnp.stack([_pack_big([_shard_of(grads[n], n, j) for n in BIG]) for j in range(N_CHIPS)])
    core = lax.axis_index("c").astype(jnp.int32).reshape(1)
    gsum = _add_sibling(gpack, _grads_to_sibling(gpack, "grads_to_sibling"), core, "add_sibling")
    reduced = _share_halves(_sum_parts(_chip_scatter(gsum, "grads_to_chips"), "add_chips"), "reduced_to_sibling")
    big_out = _adamw(reduced[None], _pack_big(big_w), _pack_big([given["m_" + n] for n in BIG]), _pack_big([given["v_" + n] for n in BIG]),
                     "adamw_matrices")
    big_out = [_unpack_big(o, big_shapes) for o in big_out]

    small_shapes = [given[n].shape for n in SMALL]
    ssum = _all_reduce_small(_pack_small([grads[n] for n in SMALL], loss[0, 0]), "reduce_vectors")
    small_out = _adamw(ssum[None], _pack_small([given[n] for n in SMALL]), _pack_small([given["m_" + n] for n in SMALL]),
                       _pack_small([given["v_" + n] for n in SMALL]), "adamw_vectors")
    total_loss = ssum.reshape(-1)[sum(int(np.prod(s)) for s in small_shapes)]
    small_out = [_unpack_small(o, small_shapes)[0] for o in small_out]

    names = ("meta", "norm_g", "final_g", "ev_w_in", "ev_q_norm_g", "ev_kv_norm_g", "ev_w_uq", "ev_w_ukv", "ev_w_out", "od_w_in", "od_sinks",
             "od_w_out")
    outs = [total_loss, grad_x]
    for kind in range(4):
        for n in names:
            outs.append(big_out[kind][BIG.index(n)] if n in BIG else small_out[kind][SMALL.index(n)])
    return tuple(outs)
```

```python
import functools
import math

import numpy as np
import jax
import jax.numpy as jnp
from jax import lax
from jax.experimental import pallas as pl
from jax.experimental.pallas import tpu as pltpu

F32 = jnp.float32
BF16 = jnp.bfloat16

D_MODEL = 1024
BLOCK = 128
N_META = 16
N_PAD = BLOCK - N_META
NORM_EPS = 1e-6
NEG = -1e30
HEAD = 64
SB_HEADS = 8
MLA_HEADS = 8
MLA_Q_LORA = 256
MLA_KV_LORA = 128
MLA_NOPE = 64
MLA_ROPE = 32
ROPE_BASE = 10000.0
SWA_HEADS = 16
SWA_KV_HEADS = 2
SWA_WINDOW = 128
N_CHIPS = 4
N_DEV = 8

ADAM_LR = 0.001
ADAM_B1 = 0.9
ADAM_B2 = 0.999
ADAM_EPS = 1e-08
ADAM_WD = 0.01
ADAM_STEP = 10

VMEM_LIMIT = 48 * 1024 * 1024

EV_G, EV_Q, EV_K, EV_V, EV_CQ, EV_CKV, EV_KR, EV_N = 0, 1024, 1536, 2048, 2560, 2816, 2944, 3072
OD_G, OD_Q, OD_K, OD_V, OD_N = 0, 1024, 2048, 2176, 2304


def _params(sem=None):
    return pltpu.CompilerParams(dimension_semantics=sem, vmem_limit_bytes=VMEM_LIMIT)


def _row_tile(m):
    return 256 if m % 256 == 0 else 128


def _matmul_rows(m):
    for c in (1088, 1024, 768, 640, 512, 384, 256):
        if m % c == 0:
            return c
    return 128


def _dot(a, b):
    return jnp.dot(a.astype(BF16), b.astype(BF16), preferred_element_type=F32)


def _dot_nt(a, b):
    return lax.dot_general(a.astype(BF16), b.astype(BF16), (((1,), (1,)), ((), ())), preferred_element_type=F32)


def _dot_tn(a, b):
    return lax.dot_general(a.astype(BF16), b.astype(BF16), (((0,), (0,)), ((), ())), preferred_element_type=F32)


def _rms_fwd(h, g, name):
    t, d = h.shape
    tm = _row_tile(t)

    def body(h_ref, g_ref, o_ref):
        x = h_ref[...]
        r = lax.rsqrt(jnp.mean(x * x, axis=-1, keepdims=True) + NORM_EPS)
        o_ref[...] = ((x * r) * g_ref[...]).astype(o_ref.dtype)

    return pl.pallas_call(
        body, grid=(t // tm,),
        in_specs=[pl.BlockSpec((tm, d), lambda i: (i, 0)), pl.BlockSpec((1, d), lambda i: (0, 0))],
        out_specs=pl.BlockSpec((tm, d), lambda i: (i, 0)),
        out_shape=jax.ShapeDtypeStruct((t, d), BF16), compiler_params=_params(("parallel",)), name=name,
    )(h, g)


def _rms_bwd(h, g, dy, dres, name):
    t, d = h.shape
    tm = _row_tile(t)

    def body(h_ref, g_ref, dy_ref, dres_ref, dh_ref, dg_ref):
        @pl.when(pl.program_id(0) == 0)
        def _():
            dg_ref[...] = jnp.zeros_like(dg_ref)

        x = h_ref[...]
        r = lax.rsqrt(jnp.mean(x * x, axis=-1, keepdims=True) + NORM_EPS)
        xr = x * r
        dy_ = dy_ref[...]
        u = dy_ * g_ref[...]
        dh_ref[...] = dres_ref[...] + r * (u - xr * jnp.mean(u * xr, axis=-1, keepdims=True))
        dg_ref[...] += jnp.sum(dy_ * xr, axis=0, keepdims=True)

    row = pl.BlockSpec((tm, d), lambda i: (i, 0))
    vec = pl.BlockSpec((1, d), lambda i: (0, 0))
    return pl.pallas_call(
        body, grid=(t // tm,), in_specs=[row, vec, row, row], out_specs=[row, vec],
        out_shape=[jax.ShapeDtypeStruct((t, d), F32), jax.ShapeDtypeStruct((1, d), F32)],
        compiler_params=_params(("arbitrary",)), name=name,
    )(h, g, dy, dres)


def _col_tile(n):
    for c in (1024, 768, 640, 512, 384, 256, 128):
        if n % c == 0:
            return c
    return n


def _mm(a, w, name, res=None, out_dtype=F32, a_cols=None):
    m = a.shape[0]
    k, n = w.shape
    a_blk = 0 if a_cols is None else a_cols[0] // k
    assert a_cols is None or (a_cols[1] == k and a_cols[0] % k == 0)
    tm, tn = _matmul_rows(m), _col_tile(n)

    def body(*refs):
        if res is None:
            a_ref, w_ref, o_ref = refs
            acc = _dot(a_ref[...], w_ref[...])
        else:
            a_ref, w_ref, r_ref, o_ref = refs
            acc = r_ref[...] + _dot(a_ref[...], w_ref[...])
        o_ref[...] = acc.astype(o_ref.dtype)

    in_specs = [pl.BlockSpec((tm, k), lambda j, i: (i, a_blk)), pl.BlockSpec((k, tn), lambda j, i: (0, j))]
    args = [a, w]
    if res is not None:
        in_specs.append(pl.BlockSpec((tm, tn), lambda j, i: (i, j)))
        args.append(res)
    return pl.pallas_call(
        body, grid=(n // tn, m // tm), in_specs=in_specs, out_specs=pl.BlockSpec((tm, tn), lambda j, i: (i, j)),
        out_shape=jax.ShapeDtypeStruct((m, n), out_dtype), compiler_params=_params(("parallel", "parallel")), name=name,
    )(*args)


def _mm_nt(a, w, name):
    m, n = a.shape
    k = w.shape[0]
    tm, tk = _matmul_rows(m), _col_tile(k)

    def body(a_ref, w_ref, o_ref):
        o_ref[...] = _dot_nt(a_ref[...], w_ref[...])

    return pl.pallas_call(
        body, grid=(k // tk, m // tm),
        in_specs=[pl.BlockSpec((tm, n), lambda j, i: (i, 0)), pl.BlockSpec((tk, n), lambda j, i: (j, 0))],
        out_specs=pl.BlockSpec((tm, tk), lambda j, i: (i, j)),
        out_shape=jax.ShapeDtypeStruct((m, k), F32), compiler_params=_params(("parallel", "parallel")), name=name,
    )(a, w)


def _mm_tn(x, dy, name):
    m, k = x.shape
    n = dy.shape[1]
    tm, tn = _matmul_rows(m), _col_tile(n)

    def body(x_ref, dy_ref, o_ref):
        @pl.when(pl.program_id(1) == 0)
        def _():
            o_ref[...] = jnp.zeros_like(o_ref)

        o_ref[...] += _dot_tn(x_ref[...], dy_ref[...])

    return pl.pallas_call(
        body, grid=(n // tn, m // tm),
        in_specs=[pl.BlockSpec((tm, k), lambda j, i: (i, 0)), pl.BlockSpec((tm, tn), lambda j, i: (i, j))],
        out_specs=pl.BlockSpec((k, tn), lambda j, i: (0, j)),
        out_shape=jax.ShapeDtypeStruct((k, n), F32), compiler_params=_params(("parallel", "arbitrary")), name=name,
    )(x, dy)


def _silu_parts(g):
    s = 1.0 / (1.0 + jnp.exp(-g))
    return g * s, s * (1.0 + g * (1.0 - s))


def _gate_fwd(o_parts, proj, name):
    t = proj.shape[0]
    tm = _row_tile(t)
    w = D_MODEL // len(o_parts)

    def body(*refs):
        g_ref, o_ref = refs[-2], refs[-1]
        for p, r in enumerate(refs[:-2]):
            sil, _ = _silu_parts(g_ref[:, p * w:(p + 1) * w])
            o_ref[:, p * w:(p + 1) * w] = (r[...].astype(F32) * sil).astype(o_ref.dtype)

    return pl.pallas_call(
        body, grid=(t // tm,),
        in_specs=[pl.BlockSpec((tm, w), lambda i: (i, 0)) for _ in o_parts] + [pl.BlockSpec((tm, D_MODEL), lambda i: (i, 0))],
        out_specs=pl.BlockSpec((tm, D_MODEL), lambda i: (i, 0)),
        out_shape=jax.ShapeDtypeStruct((t, D_MODEL), BF16), compiler_params=_params(("parallel",)), name=name,
    )(*o_parts, proj)


def _gate_bwd(dao, o_parts, proj, name):
    t = proj.shape[0]
    tm = _row_tile(t)
    np_ = len(o_parts)
    w = D_MODEL // np_

    def body(*refs):
        dao_ref, g_ref = refs[0], refs[1 + np_]
        do_refs, dg_ref = refs[2 + np_:2 + 2 * np_], refs[-1]
        for p in range(np_):
            sl = slice(p * w, (p + 1) * w)
            sil, dsil = _silu_parts(g_ref[:, sl])
            da = dao_ref[:, sl]
            do_refs[p][...] = da * sil
            dg_ref[:, sl] = (da * refs[1 + p][...].astype(F32) * dsil).astype(dg_ref.dtype)

    full = pl.BlockSpec((tm, D_MODEL), lambda i: (i, 0))
    part = pl.BlockSpec((tm, w), lambda i: (i, 0))
    outs = pl.pallas_call(
        body, grid=(t // tm,), in_specs=[full] + [part] * np_ + [full], out_specs=[part] * np_ + [full],
        out_shape=[jax.ShapeDtypeStruct((t, w), F32)] * np_ + [jax.ShapeDtypeStruct((t, D_MODEL), BF16)],
        compiler_params=_params(("parallel",)), name=name,
    )(dao, *o_parts, proj)
    return outs[:np_], outs[np_]


def _loss_head(h2, gf, target, b, lp):
    d = h2.shape[1]
    nb = lp // BLOCK
    h3 = h2.reshape(b, lp, d)

    def body(h_ref, g_ref, t_ref, dh_ref, dg_ref, loss_ref):
        first = (pl.program_id(0) == 0) & (pl.program_id(1) == 0)

        @pl.when(first)
        def _():
            dg_ref[...] = jnp.zeros_like(dg_ref)
            loss_ref[...] = jnp.zeros_like(loss_ref)

        @pl.when(pl.program_id(1) == 0)
        def _():
            dh_ref[...] = jnp.zeros_like(dh_ref)

        @pl.when(pl.program_id(1) > 0)
        def _():
            x = h_ref[0]
            r = lax.rsqrt(jnp.mean(x * x, axis=-1, keepdims=True) + NORM_EPS)
            xr = x * r
            g = g_ref[...]
            diff = xr * g - t_ref[0]
            loss_ref[...] += 0.5 * jnp.sum(jnp.mean(diff * diff, axis=-1, keepdims=True))
            dy = diff * (1.0 / d)
            u = dy * g
            dh_ref[0] = r * (u - xr * jnp.mean(u * xr, axis=-1, keepdims=True))
            dg_ref[...] += jnp.sum(dy * xr, axis=0, keepdims=True)

    blk = pl.BlockSpec((1, BLOCK, d), lambda bi, n: (bi, n, 0))
    dh, dg, loss = pl.pallas_call(
        body, grid=(b, nb),
        in_specs=[blk, pl.BlockSpec((1, d), lambda bi, n: (0, 0)),
                  pl.BlockSpec((1, BLOCK, d), lambda bi, n: (bi, jnp.maximum(n - 1, 0), 0))],
        out_specs=[blk, pl.BlockSpec((1, d), lambda bi, n: (0, 0)), pl.BlockSpec((8, 128), lambda bi, n: (0, 0))],
        out_shape=[jax.ShapeDtypeStruct((b, lp, d), F32), jax.ShapeDtypeStruct((1, d), F32), jax.ShapeDtypeStruct((8, 128), F32)],
        compiler_params=_params(("arbitrary", "arbitrary")), name="loss_head",
    )(h3, gf, target)
    return dh.reshape(b * lp, d), dg, loss


def _iota2(shape, dim):
    return lax.broadcasted_iota(jnp.int32, shape, dim)


def _split_dot(x, tri):
    hi = x.astype(BF16)
    lo = (x - hi.astype(F32)).astype(BF16)
    return jnp.dot(hi, tri, preferred_element_type=F32) + jnp.dot(lo, tri, preferred_element_type=F32)


KEYS = 256


def _lo_lanes():
    return _iota2((1, BLOCK), 1) < HEAD


def _halves(x, lo):
    zero = jnp.zeros_like(x)
    return jnp.where(lo, x, zero), jnp.where(lo, zero, x)


def _stack_halves(x, lo):
    a, b = _halves(x, lo)
    return jnp.concatenate([a, b], axis=0)


def _pair(a, b, lo):
    return jnp.where(lo, a, b)


def _key_chunk(c, lp, t_idx, strict):
    first = c * KEYS
    s0 = pl.multiple_of(jnp.minimum(first, lp - KEYS), BLOCK)
    s_idx = s0 + _iota2(t_idx.shape, 1)
    seen = (s_idx < t_idx) if strict else (s_idx <= t_idx)
    return s0, seen & (s_idx >= jnp.maximum(first, N_PAD))


def _n_chunks(i):
    return (i + 2) // 2


QROWS = 512


def _for_query_tiles(nb, tile):
    per = QROWS // BLOCK

    def step(j, _):
        tile(pl.multiple_of(j * QROWS, QROWS), QROWS, (j + 1) * (QROWS // KEYS))
        return 0

    lax.fori_loop(0, nb // per, step, 0)
    for i in range(nb - nb % per, nb):
        tile(i * BLOCK, BLOCK, _n_chunks(i))


def _sb_scores(q_h, k, valid, after):
    z = _dot_nt(q_h, k) * (HEAD ** -0.5)
    sp = jnp.log(1.0 + jnp.exp(-jnp.abs(z)))
    lb = jnp.minimum(z, 0.0) - sp
    l1m_all = -jnp.maximum(z, 0.0) - sp
    l1m = jnp.where(valid, l1m_all, 0.0)
    return lb, l1m_all, l1m, _split_dot(l1m, after)


def _sb_fwd(proj3, name):
    b, lp, _ = proj3.shape
    nb = lp // BLOCK
    npair = SB_HEADS // 2

    def body(q_ref, k_ref, v_ref, o_ref, tot_ref):
        lo = _lo_lanes()
        after = (_iota2((KEYS, KEYS), 0) > _iota2((KEYS, KEYS), 1)).astype(BF16)

        def qtile(r0, bq, n):
            qs = _halves(q_ref[0, pl.ds(r0, bq), :].astype(BF16), lo)
            t_idx = r0 + _iota2((bq, KEYS), 0)

            def kchunk(cc, carry):
                cs, acc = carry[:2], carry[2]
                s0, valid = _key_chunk(n - 1 - cc, lp, t_idx, True)
                k = k_ref[0, pl.ds(s0, KEYS), :].astype(BF16)
                a_s, new = [], []
                for h in range(2):
                    lb, _, l1m, suf = _sb_scores(qs[h], k, valid, after)
                    a_s.append(jnp.where(valid, jnp.exp(lb + suf + cs[h]), 0.0).astype(BF16))
                    new.append(cs[h] + jnp.sum(l1m, axis=1, keepdims=True))
                v_bd = _stack_halves(v_ref[0, pl.ds(s0, KEYS), :].astype(BF16), lo)
                return (*new, acc + jnp.dot(jnp.concatenate(a_s, axis=1), v_bd, preferred_element_type=F32))

            zero = jnp.zeros((bq, 1), F32)
            c_a, c_b, acc = lax.fori_loop(0, n, kchunk, (zero, zero, jnp.zeros((bq, BLOCK), F32)))
            o_ref[0, pl.ds(r0, bq), :] = acc
            tot_ref[0, pl.ds(r0, bq), :] = jnp.broadcast_to(_pair(c_a, c_b, lo), (bq, BLOCK))

        _for_query_tiles(nb, qtile)

    def col(first):
        return pl.BlockSpec((1, lp, 2 * HEAD), lambda bi, hp: (bi, 0, first // (2 * HEAD) + hp))

    shp = jax.ShapeDtypeStruct((b, lp, SB_HEADS * HEAD), F32)
    return pl.pallas_call(
        body, grid=(b, npair), in_specs=[col(EV_Q), col(EV_K), col(EV_V)], out_specs=[col(0), col(0)], out_shape=[shp, shp],
        compiler_params=_params(("parallel", "parallel")), name=name,
    )(proj3, proj3, proj3)


def _sb_bwd(proj3, tot, do, name):
    b, lp, _ = proj3.shape
    nb = lp // BLOCK
    npair = SB_HEADS // 2

    def body(q_ref, k_ref, v_ref, tot_ref, do_ref, dq_ref, dk_ref, dv_ref):
        lo = _lo_lanes()
        after = (_iota2((KEYS, KEYS), 0) > _iota2((KEYS, KEYS), 1)).astype(BF16)
        before = (_iota2((KEYS, KEYS), 0) < _iota2((KEYS, KEYS), 1)).astype(BF16)
        dk_ref[...] = jnp.zeros_like(dk_ref)
        dv_ref[...] = jnp.zeros_like(dv_ref)

        def qtile(r0, bq, n):
            rows = pl.ds(r0, bq)
            qs = _halves(q_ref[0, rows, :].astype(BF16), lo)
            dos = _halves(do_ref[0, rows, :].astype(BF16), lo)
            tot_i = tot_ref[0, rows, :]
            tots = (tot_i[:, 0:1], tot_i[:, HEAD:HEAD + 1])
            q_st, do_st = jnp.concatenate(qs, axis=0), jnp.concatenate(dos, axis=0)
            t_idx = r0 + _iota2((bq, KEYS), 0)

            def kchunk(c, carry):
                s0, valid = _key_chunk(c, lp, t_idx, True)
                keys = pl.ds(s0, KEYS)
                k = k_ref[0, keys, :].astype(BF16)
                v = v_ref[0, keys, :].astype(BF16)
                a_s, dzs, new = [], [], []
                for h in range(2):
                    left, pre = carry[2 * h], carry[2 * h + 1]
                    lb, l1m_all, l1m, suf = _sb_scores(qs[h], k, valid, after)
                    here = jnp.sum(l1m, axis=1, keepdims=True)
                    a = jnp.where(valid, jnp.exp(lb + suf + (tots[h] - left - here)), 0.0)
                    w = a * _dot_nt(dos[h], v)
                    dz = jnp.where(valid, w * jnp.exp(l1m_all) - (pre + _split_dot(w, before)) * jnp.exp(lb), 0.0) * (HEAD ** -0.5)
                    new += [left + here, pre + jnp.sum(w, axis=1, keepdims=True)]
                    a_s.append(a.astype(BF16))
                    dzs.append(dz.astype(BF16))
                dk_ref[0, keys, :] += _dot_tn(jnp.concatenate(dzs, axis=0), q_st)
                dv_ref[0, keys, :] += _dot_tn(jnp.concatenate(a_s, axis=0), do_st)
                dq = carry[4] + jnp.dot(jnp.concatenate(dzs, axis=1), _stack_halves(k, lo), preferred_element_type=F32)
                return (*new, dq)

            zero = jnp.zeros((bq, 1), F32)
            out = lax.fori_loop(0, n, kchunk, (zero, zero, zero, zero, jnp.zeros((bq, BLOCK), F32)))
            dq_ref[0, rows, :] = out[4]

        _for_query_tiles(nb, qtile)

    def col(first):
        return pl.BlockSpec((1, lp, 2 * HEAD), lambda bi, hp: (bi, 0, first // (2 * HEAD) + hp))

    shp = jax.ShapeDtypeStruct((b, lp, SB_HEADS * HEAD), F32)
    return pl.pallas_call(
        body, grid=(b, npair), in_specs=[col(EV_Q), col(EV_K), col(EV_V), col(0), col(0)], out_specs=[col(0)] * 3, out_shape=[shp] * 3,
        compiler_params=_params(("parallel", "parallel")), name=name,
    )(proj3, proj3, proj3, tot, do)


def _rope_tables(lp):
    half = MLA_ROPE // 2
    pos = (np.arange(lp) - N_PAD).astype(np.float32)
    inv = jnp.asarray(ROPE_BASE, F32) ** (-jnp.arange(half, dtype=F32) / half)
    ang = jnp.asarray(pos)[:, None] * inv[None, :]
    cos, sin = jnp.cos(ang), jnp.sin(ang)
    zeros = lambda n: jnp.zeros((lp, n), F32)
    c = jnp.concatenate([jnp.ones((lp, MLA_NOPE), F32), cos, cos, zeros(32)], axis=1)
    s1 = jnp.concatenate([zeros(MLA_NOPE), -sin, zeros(half), zeros(32)], axis=1)
    s2 = jnp.concatenate([zeros(MLA_NOPE), zeros(half), sin, zeros(32)], axis=1)
    return c, s1, s2


def _rope(x, c, s1, s2):
    half = MLA_ROPE // 2
    return x * c + pltpu.roll(x, BLOCK - half, 1) * s1 + pltpu.roll(x, half, 1) * s2


def _rope_t(dy, c, s1, s2):
    half = MLA_ROPE // 2
    return dy * c + pltpu.roll(dy * s1, half, 1) + pltpu.roll(dy * s2, BLOCK - half, 1)


def _rms_rows(x, g):
    r = lax.rsqrt(jnp.mean(x * x, axis=-1, keepdims=True) + NORM_EPS)
    return x * r, r


def _mla_prep_fwd(proj3, gq, gkv, wq, wk, wv, tabs, name):
    b, lp, _ = proj3.shape
    nb = lp // BLOCK
    hw = MLA_HEADS * BLOCK

    def body(cq_ref, ckv_ref, kr_ref, gq_ref, gkv_ref, wq_ref, wk_ref, wv_ref, c_ref, s1_ref, s2_ref, qf_ref, kf_ref, v_ref):
        c, s1, s2 = c_ref[...], s1_ref[...], s2_ref[...]
        xq, _ = _rms_rows(cq_ref[0], None)
        qh = _dot(xq * gq_ref[...], wq_ref[...])
        xk, _ = _rms_rows(ckv_ref[0], None)
        ckv_n = xk * gkv_ref[...]
        kv = _dot(ckv_n, wk_ref[...])
        v_ref[0] = _dot(ckv_n, wv_ref[...]).astype(v_ref.dtype)
        kr = _rope(kr_ref[0], c, s1, s2)
        for h in range(MLA_HEADS):
            ls = slice(h * BLOCK, (h + 1) * BLOCK)
            qf_ref[0, :, ls] = _rope(qh[:, ls], c, s1, s2).astype(qf_ref.dtype)
            kf_ref[0, :, ls] = (kv[:, ls] + kr).astype(kf_ref.dtype)

    def col(first, width):
        return pl.BlockSpec((1, BLOCK, width), lambda bi, n: (bi, n, first // width))

    def whole(a):
        return pl.BlockSpec(a.shape, lambda bi, n: (0,) * a.ndim)

    tab = pl.BlockSpec((BLOCK, BLOCK), lambda bi, n: (n, 0))
    return pl.pallas_call(
        body, grid=(b, nb),
        in_specs=[col(EV_CQ, MLA_Q_LORA), col(EV_CKV, MLA_KV_LORA), col(EV_KR, BLOCK), whole(gq), whole(gkv), whole(wq), whole(wk),
                  whole(wv), tab, tab, tab],
        out_specs=[col(0, hw), col(0, hw), col(0, MLA_HEADS * HEAD)],
        out_shape=[jax.ShapeDtypeStruct((b, lp, hw), BF16), jax.ShapeDtypeStruct((b, lp, hw), BF16),
                   jax.ShapeDtypeStruct((b, lp, MLA_HEADS * HEAD), BF16)],
        compiler_params=_params(("parallel", "parallel")), name=name,
    )(proj3, proj3, proj3, gq, gkv, wq, wk, wv, *tabs)


def _mla_prep_bwd(proj3, gq, gkv, wq, wk, wv, tabs, dqf, dkf, dv, name):
    b, lp, _ = proj3.shape
    nb = lp // BLOCK
    hw = MLA_HEADS * BLOCK

    def body(cq_ref, ckv_ref, gq_ref, gkv_ref, wq_ref, wk_ref, wv_ref, c_ref, s1_ref, s2_ref, dqf_ref, dkf_ref, dv_ref,
             dcq_ref, dckv_ref, dkr_ref, dwq_ref, dwk_ref, dwv_ref, dgq_ref, dgkv_ref, dqh):
        @pl.when((pl.program_id(0) == 0) & (pl.program_id(1) == 0))
        def _():
            for r in (dwq_ref, dwk_ref, dwv_ref, dgq_ref, dgkv_ref):
                r[...] = jnp.zeros_like(r)

        c, s1, s2 = c_ref[...], s1_ref[...], s2_ref[...]
        dkr = jnp.zeros((BLOCK, BLOCK), F32)
        for h in range(MLA_HEADS):
            ls = slice(h * BLOCK, (h + 1) * BLOCK)
            dqh[:, ls] = _rope_t(dqf_ref[0, :, ls].astype(F32), c, s1, s2).astype(dqh.dtype)
            dkr = dkr + dkf_ref[0, :, ls].astype(F32)
        dkr_ref[0] = _rope_t(dkr, c, s1, s2).astype(dkr_ref.dtype)

        def norm_bwd(x, g, dy, dg_ref):
            xr, r = _rms_rows(x, None)
            u = dy * g
            dg_ref[...] += jnp.sum(dy * xr, axis=0, keepdims=True)
            return r * (u - xr * jnp.mean(u * xr, axis=-1, keepdims=True))

        xq, _ = _rms_rows(cq_ref[0], None)
        cq_n = xq * gq_ref[...]
        dwq_ref[...] += _dot_tn(cq_n, dqh[...])
        dcq_ref[0] = norm_bwd(cq_ref[0], gq_ref[...], _dot_nt(dqh[...], wq_ref[...]), dgq_ref).astype(dcq_ref.dtype)
        xk, _ = _rms_rows(ckv_ref[0], None)
        ckv_n = xk * gkv_ref[...]
        dkf_, dv_ = dkf_ref[0], dv_ref[0]
        dwk_ref[...] += _dot_tn(ckv_n, dkf_)
        dwv_ref[...] += _dot_tn(ckv_n, dv_)
        dckv_n = _dot_nt(dkf_, wk_ref[...]) + _dot_nt(dv_, wv_ref[...])
        dckv_ref[0] = norm_bwd(ckv_ref[0], gkv_ref[...], dckv_n, dgkv_ref).astype(dckv_ref.dtype)

    def col(first, width):
        return pl.BlockSpec((1, BLOCK, width), lambda bi, n: (bi, n, first // width))

    def whole(a):
        return pl.BlockSpec(a.shape, lambda bi, n: (0,) * len(a.shape))

    tab = pl.BlockSpec((BLOCK, BLOCK), lambda bi, n: (n, 0))
    acc_shapes = [jax.ShapeDtypeStruct(a.shape, F32) for a in (wq, wk, wv, gq, gkv)]
    return pl.pallas_call(
        body, grid=(b, nb),
        in_specs=[col(EV_CQ, MLA_Q_LORA), col(EV_CKV, MLA_KV_LORA), whole(gq), whole(gkv), whole(wq), whole(wk), whole(wv), tab, tab, tab,
                  col(0, hw), col(0, hw), col(0, MLA_HEADS * HEAD)],
        out_specs=[col(0, MLA_Q_LORA), col(0, MLA_KV_LORA), col(0, BLOCK)] + [whole(a) for a in acc_shapes],
        out_shape=[jax.ShapeDtypeStruct((b, lp, MLA_Q_LORA), BF16), jax.ShapeDtypeStruct((b, lp, MLA_KV_LORA), BF16),
                   jax.ShapeDtypeStruct((b, lp, BLOCK), BF16)] + acc_shapes,
        scratch_shapes=[pltpu.VMEM((BLOCK, hw), BF16)],
        compiler_params=_params(("arbitrary", "arbitrary")), name=name,
    )(proj3, proj3, gq, gkv, wq, wk, wv, *tabs, dqf, dkf, dv)


def _mla_fwd(qf, kf, v, name):
    b, lp, _ = qf.shape
    nb = lp // BLOCK
    npair = MLA_HEADS // 2
    scale = (MLA_NOPE + MLA_ROPE) ** -0.5

    def body(q_ref, k_ref, v_ref, o_ref, lse_ref):
        lo = _lo_lanes()

        def qtile(r0, bq, n):
            qs = [q_ref[0, pl.ds(r0, bq), h * BLOCK:(h + 1) * BLOCK] for h in range(2)]
            t_idx = r0 + _iota2((bq, KEYS), 0)

            def kchunk(c, carry):
                stats, acc = carry[:4], carry[4]
                s0, valid = _key_chunk(c, lp, t_idx, False)
                ps, new, alphas = [], [], []
                for h in range(2):
                    m, l = stats[2 * h], stats[2 * h + 1]
                    s = jnp.where(valid, _dot_nt(qs[h], k_ref[0, pl.ds(s0, KEYS), h * BLOCK:(h + 1) * BLOCK]) * scale, NEG)
                    m_new = jnp.maximum(m, jnp.max(s, axis=1, keepdims=True))
                    p = jnp.where(valid, jnp.exp(s - m_new), 0.0)
                    alpha = jnp.exp(m - m_new)
                    new += [m_new, alpha * l + jnp.sum(p, axis=1, keepdims=True)]
                    alphas.append(alpha)
                    ps.append(p.astype(BF16))
                pv = jnp.dot(jnp.concatenate(ps, axis=1), _stack_halves(v_ref[0, pl.ds(s0, KEYS), :], lo), preferred_element_type=F32)
                return (*new, _pair(alphas[0], alphas[1], lo) * acc + pv)

            neg, zero = jnp.full((bq, 1), NEG, F32), jnp.zeros((bq, 1), F32)
            m_a, l_a, m_b, l_b, acc = lax.fori_loop(0, n, kchunk, (neg, zero, neg, zero, jnp.zeros((bq, BLOCK), F32)))
            l = _pair(l_a, l_b, lo)
            seen = l > 0.0
            safe = jnp.where(seen, l, 1.0)
            o_ref[0, pl.ds(r0, bq), :] = acc / safe
            lse_ref[0, pl.ds(r0, bq), :] = jnp.where(seen, _pair(m_a, m_b, lo) + jnp.log(safe), 0.0)

        _for_query_tiles(nb, qtile)

    wide = pl.BlockSpec((1, lp, 2 * BLOCK), lambda bi, hp: (bi, 0, hp))
    thin = pl.BlockSpec((1, lp, 2 * HEAD), lambda bi, hp: (bi, 0, hp))
    shp = jax.ShapeDtypeStruct((b, lp, MLA_HEADS * HEAD), F32)
    return pl.pallas_call(
        body, grid=(b, npair), in_specs=[wide, wide, thin], out_specs=[thin, thin], out_shape=[shp, shp],
        compiler_params=_params(("parallel", "parallel")), name=name,
    )(qf, kf, v)


def _mla_bwd(qf, kf, v, o, lse, do, name):
    b, lp, _ = qf.shape
    nb = lp // BLOCK
    npair = MLA_HEADS // 2
    scale = (MLA_NOPE + MLA_ROPE) ** -0.5

    def body(q_ref, k_ref, v_ref, o_ref, lse_ref, do_ref, dq_ref, dk_ref, dv_ref):
        lo = _lo_lanes()
        dk_ref[...] = jnp.zeros_like(dk_ref)
        dv_ref[...] = jnp.zeros_like(dv_ref)

        def qtile(r0, bq, n):
            rows = pl.ds(r0, bq)
            qs = [q_ref[0, rows, h * BLOCK:(h + 1) * BLOCK] for h in range(2)]
            do_i = do_ref[0, rows, :]
            dos = _halves(do_i.astype(BF16), lo)
            both = do_i * o_ref[0, rows, :]
            d_a = jnp.sum(jnp.where(lo, both, 0.0), axis=1, keepdims=True)
            dsum = (d_a, jnp.sum(both, axis=1, keepdims=True) - d_a)
            lse_i = lse_ref[0, rows, :]
            lses = (lse_i[:, 0:1], lse_i[:, HEAD:HEAD + 1])
            t_idx = r0 + _iota2((bq, KEYS), 0)

            def kchunk(c, dqs):
                s0, valid = _key_chunk(c, lp, t_idx, False)
                keys = pl.ds(s0, KEYS)
                v_c = v_ref[0, keys, :]
                ps, out = [], []
                for h in range(2):
                    lanes = slice(h * BLOCK, (h + 1) * BLOCK)
                    k = k_ref[0, keys, lanes]
                    s = jnp.where(valid, _dot_nt(qs[h], k) * scale, NEG)
                    p = jnp.where(valid, jnp.exp(s - lses[h]), 0.0)
                    ds = p * (_dot_nt(dos[h], v_c) - dsum[h]) * scale
                    dk_ref[0, keys, lanes] += _dot_tn(ds, qs[h])
                    out.append(dqs[h] + _dot(ds, k))
                    ps.append(p.astype(BF16))
                dv_ref[0, keys, :] += _dot_tn(jnp.concatenate(ps, axis=0), jnp.concatenate(dos, axis=0))
                return tuple(out)

            zero = jnp.zeros((bq, BLOCK), F32)
            dq_a, dq_b = lax.fori_loop(0, n, kchunk, (zero, zero))
            dq_ref[0, rows, 0:BLOCK] = dq_a
            dq_ref[0, rows, BLOCK:2 * BLOCK] = dq_b

        _for_query_tiles(nb, qtile)

    wide = pl.BlockSpec((1, lp, 2 * BLOCK), lambda bi, hp: (bi, 0, hp))
    thin = pl.BlockSpec((1, lp, 2 * HEAD), lambda bi, hp: (bi, 0, hp))
    return pl.pallas_call(
        body, grid=(b, npair), in_specs=[wide, wide, thin, thin, thin, thin], out_specs=[wide, wide, thin],
        out_shape=[jax.ShapeDtypeStruct(qf.shape, F32), jax.ShapeDtypeStruct(qf.shape, F32), jax.ShapeDtypeStruct(v.shape, F32)],
        compiler_params=_params(("parallel", "parallel")), name=name,
    )(qf, kf, v, o, lse, do)


def _swa_keys(k_ref, v_ref, n, kv, lo):
    prev = jnp.maximum(n - 1, 0)
    rows = lambda blk: pl.ds(pl.multiple_of(blk * BLOCK, BLOCK), BLOCK)
    mine = (_iota2((1, BLOCK), 1) >= HEAD).astype(jnp.int32) == kv

    def both_halves(ref):
        x = jnp.concatenate([ref[0, rows(prev), :], ref[0, rows(n), :], ref[0, 0:BLOCK, :]], axis=0)
        return jnp.where(mine, x, pltpu.roll(x, HEAD, 1)).astype(BF16)

    col = _iota2((BLOCK, 3 * BLOCK), 1)
    loc = col % BLOCK
    s_idx = jnp.where(col < BLOCK, (n - 1) * BLOCK + loc, jnp.where(col < 2 * BLOCK, n * BLOCK + loc, loc))
    dist = n * BLOCK + _iota2((BLOCK, 3 * BLOCK), 0) - s_idx
    band = (col < 2 * BLOCK) & (dist >= 0) & (dist < SWA_WINDOW) & (s_idx >= BLOCK)
    meta = (col >= 2 * BLOCK) & (s_idx >= N_PAD) & (dist >= 0)
    return both_halves(k_ref), both_halves(v_ref), band | meta, dist.astype(F32), prev


def _swa_probs(q_h, kdup, valid, dist, head, sink_ref):
    slope = jnp.exp(jnp.full((1, 1), -8.0 * math.log(2.0) / SWA_HEADS, F32) * (head + 1).astype(F32))
    s = jnp.where(valid, _dot_nt(q_h, kdup) * (HEAD ** -0.5) - slope * dist, NEG)
    sink = sink_ref[pl.ds(head, 1), 0:1]
    m = jnp.maximum(jnp.max(s, axis=1, keepdims=True), sink)
    e = jnp.where(valid, jnp.exp(s - m), 0.0)
    es = jnp.exp(sink - m)
    inv = 1.0 / (jnp.sum(e, axis=1, keepdims=True) + es)
    return e * inv, es * inv


SWA_PAIRS = SWA_HEADS // SWA_KV_HEADS // 2
SWA_GROUP = SWA_PAIRS * 2 * HEAD


def _swa_specs(b, lp):
    nb = lp // BLOCK
    qcol = lambda first: pl.BlockSpec((1, BLOCK, SWA_GROUP), lambda bi, kv, n: (bi, n, first // SWA_GROUP + kv))
    kcol = lambda first: pl.BlockSpec((1, lp, BLOCK), lambda bi, kv, n: (bi, 0, first // BLOCK))
    sink = pl.BlockSpec((SWA_HEADS, BLOCK), lambda bi, kv, n: (0, 0))
    return (b, SWA_KV_HEADS, nb), qcol, kcol, sink


def _swa_fwd(proj3, sinks, name):
    b, lp, _ = proj3.shape
    grid, qcol, kcol, sink = _swa_specs(b, lp)

    def body(q_ref, k_ref, v_ref, sink_ref, o_ref):
        kv, n = pl.program_id(1), pl.program_id(2)
        lo = _lo_lanes()
        kdup, vdup, valid, dist, _ = _swa_keys(k_ref, v_ref, n, kv, lo)
        v_bd = _stack_halves(vdup, lo)
        for p in range(SWA_PAIRS):
            lanes = slice(p * BLOCK, (p + 1) * BLOCK)
            qs = _halves(q_ref[0, :, lanes].astype(BF16), lo)
            probs = [_swa_probs(qs[hh], kdup, valid, dist, (kv * SWA_PAIRS + p) * 2 + hh, sink_ref)[0].astype(BF16) for hh in range(2)]
            o_ref[0, :, lanes] = jnp.dot(jnp.concatenate(probs, axis=1), v_bd, preferred_element_type=F32)

    return pl.pallas_call(
        body, grid=grid, in_specs=[qcol(OD_Q), kcol(OD_K), kcol(OD_V), sink], out_specs=qcol(0),
        out_shape=jax.ShapeDtypeStruct((b, lp, SWA_HEADS * HEAD), F32),
        compiler_params=_params(("parallel", "parallel", "parallel")), name=name,
    )(proj3, proj3, proj3, sinks)


def _swa_bwd(proj3, sinks, do, name):
    b, lp, _ = proj3.shape
    nb = lp // BLOCK
    grid, qcol, kcol, sink = _swa_specs(b, lp)

    def body(q_ref, k_ref, v_ref, sink_ref, do_ref, dq_ref, dk_ref, dv_ref, dsink_ref):
        kv, n = pl.program_id(1), pl.program_id(2)

        @pl.when((n == 0) & (pl.program_id(0) == 0) & (kv == 0))
        def _():
            dsink_ref[...] = jnp.zeros_like(dsink_ref)

        @pl.when(n == 0)
        def _():
            dk_ref[...] = jnp.zeros_like(dk_ref)
            dv_ref[...] = jnp.zeros_like(dv_ref)

        lo = _lo_lanes()
        kdup, vdup, valid, dist, prev = _swa_keys(k_ref, v_ref, n, kv, lo)
        k_bd = _stack_halves(kdup, lo)
        dkc = jnp.zeros((3 * BLOCK, BLOCK), F32)
        dvc = jnp.zeros((3 * BLOCK, BLOCK), F32)
        for p in range(SWA_PAIRS):
            lanes = slice(p * BLOCK, (p + 1) * BLOCK)
            qs = _halves(q_ref[0, :, lanes].astype(BF16), lo)
            dos = _halves(do_ref[0, :, lanes].astype(BF16), lo)
            dss, prs = [], []
            for hh in range(2):
                head = (kv * SWA_PAIRS + p) * 2 + hh
                pr, ps = _swa_probs(qs[hh], kdup, valid, dist, head, sink_ref)
                dp = _dot_nt(dos[hh], vdup)
                dsum = jnp.sum(pr * dp, axis=1, keepdims=True)
                dsink_ref[pl.ds(head, 1), :] += jnp.broadcast_to(-jnp.sum(ps * dsum, axis=0, keepdims=True), (1, BLOCK))
                dss.append((pr * (dp - dsum) * (HEAD ** -0.5)).astype(BF16))
                prs.append(pr.astype(BF16))
            dq_ref[0, :, lanes] = jnp.dot(jnp.concatenate(dss, axis=1), k_bd, preferred_element_type=F32)
            dkc = dkc + _dot_tn(jnp.concatenate(dss, axis=0), jnp.concatenate(qs, axis=0))
            dvc = dvc + _dot_tn(jnp.concatenate(prs, axis=0), jnp.concatenate(dos, axis=0))
        rows = lambda blk: pl.ds(pl.multiple_of(blk * BLOCK, BLOCK), BLOCK)
        for part, r in enumerate((rows(prev), rows(n), slice(0, BLOCK))):
            dk_ref[0, 0, r, :] += dkc[part * BLOCK:(part + 1) * BLOCK]
            dv_ref[0, 0, r, :] += dvc[part * BLOCK:(part + 1) * BLOCK]

        @pl.when(n == nb - 1)
        def _():
            for ref in (dk_ref, dv_ref):
                x = ref[0, 0]
                ref[0, 0] = x + pltpu.roll(x, HEAD, 1)

    kvout = pl.BlockSpec((1, 1, lp, BLOCK), lambda bi, kv, n: (bi, kv, 0, 0))
    kvshape = jax.ShapeDtypeStruct((b, SWA_KV_HEADS, lp, BLOCK), F32)
    return pl.pallas_call(
        body, grid=grid, in_specs=[qcol(OD_Q), kcol(OD_K), kcol(OD_V), sink, qcol(0)], out_specs=[qcol(0), kvout, kvout, sink],
        out_shape=[jax.ShapeDtypeStruct((b, lp, SWA_HEADS * HEAD), F32), kvshape, kvshape, jax.ShapeDtypeStruct((SWA_HEADS, BLOCK), F32)],
        compiler_params=_params(("arbitrary", "arbitrary", "arbitrary")), name=name,
    )(proj3, proj3, proj3, sinks, do)


def _kernel_weights(ev_w_in, ev_w_uq, ev_w_ukv, od_w_in):
    zeros = lambda r, c: jnp.zeros((r, c), ev_w_in.dtype)
    q_sb, k_sb, v_sb, g_sb, c_q, c_kv, k_r, g_mla = jnp.split(ev_w_in, [512, 1024, 1536, 2048, 2304, 2432, 2464], axis=1)
    w0 = jnp.concatenate([g_sb, g_mla, q_sb, k_sb, v_sb, c_q, c_kv, zeros(D_MODEL, MLA_NOPE), k_r, zeros(D_MODEL, 32)], axis=1)
    uq = ev_w_uq.reshape(MLA_Q_LORA, MLA_HEADS, MLA_NOPE + MLA_ROPE)
    wq = jnp.pad(uq, ((0, 0), (0, 0), (0, BLOCK - MLA_NOPE - MLA_ROPE))).reshape(MLA_Q_LORA, MLA_HEADS * BLOCK)
    ukv = ev_w_ukv.reshape(MLA_KV_LORA, MLA_HEADS, BLOCK)
    wk = jnp.pad(ukv[:, :, :MLA_NOPE], ((0, 0), (0, 0), (0, BLOCK - MLA_NOPE))).reshape(MLA_KV_LORA, MLA_HEADS * BLOCK)
    wv = ukv[:, :, MLA_NOPE:].reshape(MLA_KV_LORA, MLA_HEADS * HEAD)
    q, k, v, g = jnp.split(od_w_in, [1024, 1152, 1280], axis=1)
    w1 = jnp.concatenate([g, q, k, v], axis=1)
    return w0, wq, wk, wv, w1


def _original_grads(dw0, dwq, dwk, dwv, dw1):
    sl = lambda a, first, n: a[:, first:first + n]
    d_ev_w_in = jnp.concatenate([sl(dw0, EV_Q, 512), sl(dw0, EV_K, 512), sl(dw0, EV_V, 512), sl(dw0, EV_G, 512), sl(dw0, EV_CQ, 256),
                                 sl(dw0, EV_CKV, 128), sl(dw0, EV_KR + MLA_NOPE, MLA_ROPE), sl(dw0, EV_G + 512, 512)], axis=1)
    d_uq = dwq.reshape(MLA_Q_LORA, MLA_HEADS, BLOCK)[:, :, :MLA_NOPE + MLA_ROPE].reshape(MLA_Q_LORA, -1)
    d_ukv = jnp.concatenate([dwk.reshape(MLA_KV_LORA, MLA_HEADS, BLOCK)[:, :, :MLA_NOPE], dwv.reshape(MLA_KV_LORA, MLA_HEADS, HEAD)],
                            axis=2).reshape(MLA_KV_LORA, -1)
    d_od_w_in = jnp.concatenate([sl(dw1, OD_Q, 1024), sl(dw1, OD_K, 128), sl(dw1, OD_V, 128), sl(dw1, OD_G, 1024)], axis=1)
    return d_ev_w_in, d_uq, d_ukv, d_od_w_in


def _meta_rows_sum(dh0_3):
    b, _, d = dh0_3.shape

    def body(x_ref, o_ref):
        acc = x_ref[0, N_PAD:BLOCK, :]
        for i in range(1, b):
            acc = acc + x_ref[i, N_PAD:BLOCK, :]
        o_ref[...] = acc

    return pl.pallas_call(
        body, grid=(1,), in_specs=[pl.BlockSpec((b, BLOCK, d), lambda i: (0, 0, 0))], out_specs=pl.BlockSpec((N_META, d), lambda i: (0, 0)),
        out_shape=jax.ShapeDtypeStruct((N_META, d), F32), compiler_params=_params(("arbitrary",)), name="meta_rows_sum",
    )(dh0_3)


def _local_step(x, meta, norm_g, final_g, gq, gkv, sinks, target, ev_w_in, ev_w_uq, ev_w_ukv, wo0, od_w_in, wo1):
    b, seq, d = x.shape
    lp = seq + BLOCK
    t = b * lp
    w0, wq, wk, wv, w1 = _kernel_weights(ev_w_in, ev_w_uq, ev_w_ukv, od_w_in)
    h0 = jnp.concatenate([jnp.zeros((b, N_PAD, d), F32), jnp.broadcast_to(meta[None], (b, N_META, d)), x], axis=1).reshape(t, d)
    tabs = _rope_tables(lp)
    g0, g1 = norm_g[0:1], norm_g[1:2]

    hn0 = _rms_fwd(h0, g0, "norm0")
    proj0 = _mm(hn0, w0, "inproj0")
    p0 = proj0.reshape(b, lp, EV_N)
    o_sb, sb_tot = _sb_fwd(p0, "sb_fwd")
    qf, kf, v = _mla_prep_fwd(p0, gq, gkv, wq, wk, wv, tabs, "mla_prep_fwd")
    o_mla, lse = _mla_fwd(qf, kf, v, "mla_fwd")
    o0 = [o_sb.reshape(t, -1), o_mla.reshape(t, -1)]
    ao0 = _gate_fwd(o0, proj0, "gate0")
    h1 = _mm(ao0, wo0, "outproj0", res=h0)

    hn1 = _rms_fwd(h1, g1, "norm1")
    proj1 = _mm(hn1, w1, "inproj1")
    p1 = proj1.reshape(b, lp, OD_N)
    sinks_b = jnp.broadcast_to(sinks.reshape(SWA_HEADS, 1), (SWA_HEADS, BLOCK))
    o1 = _swa_fwd(p1, sinks_b, "swa_fwd").reshape(t, -1)
    ao1 = _gate_fwd([o1], proj1, "gate1")
    h2 = _mm(ao1, wo1, "outproj1", res=h1)

    dh2, d_final_g, loss = _loss_head(h2, final_g.reshape(1, d), target, b, lp)

    d_wo1 = _mm_tn(ao1, dh2, "d_wo1")
    dao1 = _mm_nt(dh2, wo1, "d_ao1")
    (do1,), dg1 = _gate_bwd(dao1, [o1], proj1, "gate1_bwd")
    dq1, dk4, dv4, d_sinks = _swa_bwd(p1, sinks_b, do1.reshape(b, lp, -1), "swa_bwd")
    unheads = lambda a: a[..., :HEAD].transpose(0, 2, 1, 3).reshape(t, SWA_KV_HEADS * HEAD).astype(BF16)
    dproj1 = jnp.concatenate([dg1, dq1.reshape(t, -1).astype(BF16), unheads(dk4), unheads(dv4)], axis=1)
    d_w1 = _mm_tn(hn1, dproj1, "d_w1")
    dhn1 = _mm_nt(dproj1, w1, "d_hn1")
    dh1, d_g1 = _rms_bwd(h1, g1, dhn1, dh2, "norm1_bwd")

    d_wo0 = _mm_tn(ao0, dh1, "d_wo0")
    dao0 = _mm_nt(dh1, wo0, "d_ao0")
    (do_sb, do_mla), dg0 = _gate_bwd(dao0, o0, proj0, "gate0_bwd")
    dq_sb, dk_sb, dv_sb = _sb_bwd(p0, sb_tot, do_sb.reshape(b, lp, -1), "sb_bwd")
    dqf, dkf, dv = _mla_bwd(qf, kf, v, o_mla, lse, do_mla.reshape(b, lp, -1), "mla_bwd")
    dcq, dckv, dkr, d_wq, d_wk, d_wv, d_gq, d_gkv = _mla_prep_bwd(p0, gq, gkv, wq, wk, wv, tabs, dqf, dkf, dv, "mla_prep_bwd")
    flat = lambda a: a.reshape(t, -1).astype(BF16)
    dproj0 = jnp.concatenate([dg0, flat(dq_sb), flat(dk_sb), flat(dv_sb), flat(dcq), flat(dckv), flat(dkr)], axis=1)
    d_w0 = _mm_tn(hn0, dproj0, "d_w0")
    dhn0 = _mm_nt(dproj0, w0, "d_hn0")
    dh0, d_g0 = _rms_bwd(h0, g0, dhn0, dh1, "norm0_bwd")
    dh0 = dh0.reshape(b, lp, d)

    d_ev_w_in, d_uq, d_ukv, d_od_w_in = _original_grads(d_w0, d_wq, d_wk, d_wv, d_w1)
    grads = dict(meta=_meta_rows_sum(dh0), norm_g=jnp.concatenate([d_g0, d_g1], axis=0), final_g=d_final_g.reshape(d),
                 ev_w_in=d_ev_w_in, ev_q_norm_g=d_gq, ev_kv_norm_g=d_gkv, ev_w_uq=d_uq, ev_w_ukv=d_ukv, ev_w_out=d_wo0,
                 od_w_in=d_od_w_in, od_sinks=d_sinks[:, 0].reshape(1, SWA_HEADS), od_w_out=d_wo1)
    return loss, dh0[:, BLOCK:], grads


MESH = pl.DeviceIdType.MESH
ANY = pl.BlockSpec(memory_space=pl.ANY)


def _place():
    return lax.axis_index("x"), lax.axis_index("y"), lax.axis_index("c")


def _other_chips(x, y):
    return [(1 - x, y), (x, 1 - y), (1 - x, 1 - y)]


PACK_ROWS = 1920
HALF_ROWS = PACK_ROWS // 2


def _with_own_slot(slots, own):
    me = 2 * lax.axis_index("x") + lax.axis_index("y")
    return lax.dynamic_update_slice(slots, own[None], (me,) + (0,) * own.ndim)


def _gather_weights(wpack, meta, name):
    def body(w_ref, m_ref, wo_ref, mo_ref, send_sems, recv_sems):
        x, y, c = _place()
        me, sib = 2 * x + y, (x, y, 1 - c)
        chips = _other_chips(x, y)
        mine = w_ref.at[pl.ds(c * HALF_ROWS, HALF_ROWS), :]
        half = lambda chip, h: wo_ref.at[chip, pl.ds(h * HALF_ROWS, HALF_ROWS), :]

        def copy(k, src, dst, to):
            return pltpu.make_async_remote_copy(src_ref=src, dst_ref=dst, send_sem=send_sems.at[k], recv_sem=recv_sems.at[k], device_id=to,
                                                device_id_type=MESH)

        sent = [copy(k, mine, half(me, c), (px, py, c)) for k, (px, py) in enumerate(chips)]
        sent += [copy(6 + k, m_ref, mo_ref.at[me], (px, py, c)) for k, (px, py) in enumerate(chips)]
        for cp in sent:
            cp.start()
        for k, (px, py) in enumerate(chips):
            landed = half(2 * px + py, c)
            copy(k, mine, landed, (px, py, c)).wait_recv()
            fwd = copy(3 + k, landed, landed, sib)
            fwd.start()
            sent.append(fwd)
        for k, (px, py) in enumerate(chips):
            other = half(2 * px + py, 1 - c)
            copy(3 + k, other, other, sib).wait_recv()
            copy(6 + k, m_ref, mo_ref.at[2 * px + py], (px, py, c)).wait_recv()
        for cp in sent:
            cp.wait_send()

    wall, meta_all = pl.pallas_call(
        body, in_specs=[ANY, ANY], out_specs=[ANY, ANY],
        out_shape=[jax.ShapeDtypeStruct((N_CHIPS,) + wpack.shape, wpack.dtype), jax.ShapeDtypeStruct((N_CHIPS,) + meta.shape, meta.dtype)],
        scratch_shapes=[pltpu.SemaphoreType.DMA((9,)), pltpu.SemaphoreType.DMA((9,))],
        name=name,
    )(wpack, meta)
    return _with_own_slot(wall, wpack), _with_own_slot(meta_all, meta)


def _grads_to_sibling(g, name):
    def body(g_ref, o_ref, send_sem, recv_sem):
        x, y, c = _place()
        cp = pltpu.make_async_remote_copy(src_ref=g_ref.at[:, pl.ds((1 - c) * HALF_ROWS, HALF_ROWS), :], dst_ref=o_ref, send_sem=send_sem,
                                          recv_sem=recv_sem, device_id=(x, y, 1 - c), device_id_type=MESH)
        cp.start()
        cp.wait()

    return pl.pallas_call(
        body, in_specs=[ANY], out_specs=ANY, out_shape=jax.ShapeDtypeStruct((g.shape[0], HALF_ROWS, g.shape[2]), g.dtype),
        scratch_shapes=[pltpu.SemaphoreType.DMA(()), pltpu.SemaphoreType.DMA(())],
        name=name,
    )(g)


def _share_halves(r, name):
    def body(r_ref, o_ref, send_sem, recv_sem):
        x, y, c = _place()
        cp = pltpu.make_async_remote_copy(src_ref=r_ref, dst_ref=o_ref, send_sem=send_sem, recv_sem=recv_sem, device_id=(x, y, 1 - c),
                                          device_id_type=MESH)
        cp.start()
        cp.wait()

    theirs = pl.pallas_call(
        body, in_specs=[ANY], out_specs=ANY, out_shape=jax.ShapeDtypeStruct(r.shape, r.dtype),
        scratch_shapes=[pltpu.SemaphoreType.DMA(()), pltpu.SemaphoreType.DMA(())],
        name=name,
    )(r)
    return jnp.where(lax.axis_index("c") == 0, jnp.concatenate([r, theirs], axis=0), jnp.concatenate([theirs, r], axis=0))


def _chip_scatter(s, name):
    def body(s_ref, r_ref, send_sems, recv_sems):
        x, y, c = _place()
        me = 2 * x + y
        for k, (px, py) in enumerate(_other_chips(x, y)):
            pltpu.make_async_remote_copy(src_ref=s_ref.at[2 * px + py], dst_ref=r_ref.at[me], send_sem=send_sems.at[k],
                                         recv_sem=recv_sems.at[k], device_id=(px, py, c), device_id_type=MESH).start()
        for k, (px, py) in enumerate(_other_chips(x, y)):
            cp = pltpu.make_async_remote_copy(src_ref=s_ref.at[2 * px + py], dst_ref=r_ref.at[2 * px + py], send_sem=send_sems.at[k],
                                              recv_sem=recv_sems.at[k], device_id=(px, py, c), device_id_type=MESH)
            cp.wait_recv()
            cp.wait_send()

    parts = pl.pallas_call(
        body, in_specs=[ANY], out_specs=ANY, out_shape=jax.ShapeDtypeStruct(s.shape, s.dtype),
        scratch_shapes=[pltpu.SemaphoreType.DMA((3,)), pltpu.SemaphoreType.DMA((3,))],
        name=name,
    )(s)
    me = 2 * lax.axis_index("x") + lax.axis_index("y")
    return _with_own_slot(parts, lax.dynamic_index_in_dim(s, me, axis=0, keepdims=False))


def _all_reduce_small(v, name):
    shape = v.shape

    def body(v_ref, o_ref, slots, send_sems, recv_sems):
        x, y, c = _place()
        me = 4 * x + 2 * y + c
        slots[me] = v_ref[...]
        for r in range(1, N_DEV):
            peer = (x ^ (r >> 2), y ^ ((r >> 1) & 1), c ^ (r & 1))
            pltpu.make_async_remote_copy(src_ref=v_ref, dst_ref=slots.at[me], send_sem=send_sems.at[r - 1], recv_sem=recv_sems.at[r - 1],
                                         device_id=peer, device_id_type=MESH).start()
        for r in range(1, N_DEV):
            peer = (x ^ (r >> 2), y ^ ((r >> 1) & 1), c ^ (r & 1))
            cp = pltpu.make_async_remote_copy(src_ref=v_ref, dst_ref=slots.at[4 * peer[0] + 2 * peer[1] + peer[2]], send_sem=send_sems.at[r - 1],
                                              recv_sem=recv_sems.at[r - 1], device_id=peer, device_id_type=MESH)
            cp.wait_recv()
            cp.wait_send()
        acc = slots[0]
        for d in range(1, N_DEV):
            acc = acc + slots[d]
        o_ref[...] = acc

    vm = pl.BlockSpec(memory_space=pltpu.VMEM)
    return pl.pallas_call(
        body, in_specs=[vm], out_specs=vm, out_shape=jax.ShapeDtypeStruct(shape, F32),
        scratch_shapes=[pltpu.VMEM((N_DEV,) + shape, F32), pltpu.SemaphoreType.DMA((N_DEV - 1,)), pltpu.SemaphoreType.DMA((N_DEV - 1,))],
        name=name,
    )(v)


SUM_ROWS = 480


def _add_sibling(g, gsib, core, name):
    n, _, cdim = g.shape
    per_half = HALF_ROWS // SUM_ROWS

    def body(core_ref, a_ref, b_ref, o_ref):
        o_ref[...] = (a_ref[...] + b_ref[...]).astype(o_ref.dtype)

    blk = pl.BlockSpec((1, SUM_ROWS, cdim), lambda j, i, core_ref: (j, i, 0))
    return pl.pallas_call(
        body,
        grid_spec=pltpu.PrefetchScalarGridSpec(
            num_scalar_prefetch=1, grid=(n, per_half),
            in_specs=[pl.BlockSpec((1, SUM_ROWS, cdim), lambda j, i, core_ref: (j, core_ref[0] * per_half + i, 0)), blk], out_specs=blk),
        out_shape=jax.ShapeDtypeStruct(gsib.shape, BF16), compiler_params=_params(("parallel", "parallel")), name=name,
    )(core, g, gsib)


def _sum_parts(parts, name):
    n, r, cdim = parts.shape

    def body(p_ref, o_ref):
        acc = p_ref[0].astype(F32)
        for j in range(1, n):
            acc = acc + p_ref[j].astype(F32)
        o_ref[...] = acc

    return pl.pallas_call(
        body, grid=(r // SUM_ROWS,), in_specs=[pl.BlockSpec((n, SUM_ROWS, cdim), lambda i: (0, i, 0))],
        out_specs=pl.BlockSpec((SUM_ROWS, cdim), lambda i: (i, 0)), out_shape=jax.ShapeDtypeStruct((r, cdim), F32),
        compiler_params=_params(("parallel",)), name=name,
    )(parts)


def _adamw(parts, w, m, v, name):
    npart, r, cdim = parts.shape
    tr = 384 if r % 384 == 0 else r

    def body(p_ref, w_ref, m_ref, v_ref, g_ref, d_ref, nm_ref, nv_ref):
        g = p_ref[0]
        for j in range(1, npart):
            g = g + p_ref[j]
        m_new = ADAM_B1 * m_ref[...] + (1.0 - ADAM_B1) * g
        v_new = ADAM_B2 * v_ref[...] + (1.0 - ADAM_B2) * (g * g)
        m_hat = m_new / (1.0 - ADAM_B1 ** ADAM_STEP)
        v_hat = v_new / (1.0 - ADAM_B2 ** ADAM_STEP)
        g_ref[...] = g
        d_ref[...] = -ADAM_LR * (m_hat / (jnp.sqrt(v_hat) + ADAM_EPS) + ADAM_WD * w_ref[...])
        nm_ref[...] = m_new
        nv_ref[...] = v_new

    blk = pl.BlockSpec((tr, cdim), lambda i: (i, 0))
    shp = jax.ShapeDtypeStruct((r, cdim), F32)
    return pl.pallas_call(
        body, grid=(r // tr,), in_specs=[pl.BlockSpec((npart, tr, cdim), lambda i: (0, i, 0)), blk, blk, blk], out_specs=[blk] * 4,
        out_shape=[shp] * 4, compiler_params=_params(("parallel",)), name=name,
    )(parts, w, m, v)


BIG = ("ev_w_in", "ev_w_uq", "ev_w_ukv", "ev_w_out", "od_w_in", "od_w_out", "meta")
SMALL = ("norm_g", "final_g", "ev_q_norm_g", "ev_kv_norm_g", "od_sinks")
SMALL_SHAPE = (8, 512)


def _pack_big(arrs):
    rows = [a.reshape(-1, D_MODEL) for a in arrs]
    used = sum(r.shape[0] for r in rows)
    return jnp.pad(jnp.concatenate(rows, axis=0), ((0, PACK_ROWS - used), (0, 0)))


def _unpack_big(p, shapes):
    out, at = [], 0
    for s in shapes:
        nrow = int(np.prod(s)) // D_MODEL
        out.append(p[at:at + nrow].reshape(s))
        at += nrow
    return out


def _pack_small(arrs, extra=None):
    flat = [a.reshape(-1) for a in arrs] + ([] if extra is None else [extra.reshape(-1)])
    used = sum(f.shape[0] for f in flat)
    return jnp.pad(jnp.concatenate(flat), (0, SMALL_SHAPE[0] * SMALL_SHAPE[1] - used)).reshape(SMALL_SHAPE)


def _unpack_small(p, shapes):
    flat, out, at = p.reshape(-1), [], 0
    for s in shapes:
        n = int(np.prod(s))
        out.append(flat[at:at + n].reshape(s))
        at += n
    return out, flat[at]


def _shard_of(full, name, j):
    if name in ("ev_w_out", "od_w_out"):
        n = full.shape[0] // N_CHIPS
        return full[j * n:(j + 1) * n]
    n = full.shape[1] // N_CHIPS
    return full[:, j * n:(j + 1) * n]


def kernel(x, meta, norm_g, final_g, ev_w_in, ev_q_norm_g, ev_kv_norm_g, ev_w_uq, ev_w_ukv, ev_w_out, od_w_in, od_sinks, od_w_out, loss_target, m_meta, m_norm_g, m_final_g, m_ev_w_in, m_ev_q_norm_g, m_ev_kv_norm_g, m_ev_w_uq, m_ev_w_ukv, m_ev_w_out, m_od_w_in, m_od_sinks, m_od_w_out, v_meta, v_norm_g, v_final_g, v_ev_w_in, v_ev_q_norm_g, v_ev_kv_norm_g, v_ev_w_uq, v_ev_w_ukv, v_ev_w_out, v_od_w_in, v_od_sinks, v_od_w_out):
    given = dict(locals())
    big_w = [given[n][0] if given[n].ndim == 3 else given[n] for n in BIG]
    big_shapes = [given[n].shape for n in BIG]

    wpack = _pack_big(big_w[:-1]).astype(BF16)
    wall, meta_all = _gather_weights(wpack, meta, "gather_weights")
    per_chip = [_unpack_big(wall[j], [a.shape for a in big_w[:-1]]) for j in range(N_CHIPS)]
    cat = lambda i, axis: jnp.concatenate([per_chip[j][i] for j in range(N_CHIPS)], axis=axis)
    full = dict(ev_w_in=cat(0, 1), ev_w_uq=cat(1, 1), ev_w_ukv=cat(2, 1), ev_w_out=cat(3, 0), od_w_in=cat(4, 1), od_w_out=cat(5, 0))
    meta_full = jnp.concatenate([meta_all[j] for j in range(N_CHIPS)], axis=1)

    loss, grad_x, grads = _local_step(x, meta_full, norm_g, final_g, ev_q_norm_g, ev_kv_norm_g, od_sinks, loss_target,
                                      full["ev_w_in"], full["ev_w_uq"], full["ev_w_ukv"], full["ev_w_out"], full["od_w_in"], full["od_w_out"])

    gpack = jnp.stack([_pack_big([_shard_of(grads[n], n, j) for n in BIG]) for j in range(N_CHIPS)])
    core = lax.axis_index("c").astype(jnp.int32).reshape(1)
    gsum = _add_sibling(gpack, _grads_to_sibling(gpack, "grads_to_sibling"), core, "add_sibling")
    reduced = _share_halves(_sum_parts(_chip_scatter(gsum, "grads_to_chips"), "add_chips"), "reduced_to_sibling")
    big_out = _adamw(reduced[None], _pack_big(big_w), _pack_big([given["m_" + n] for n in BIG]), _pack_big([given["v_" + n] for n in BIG]),
                     "adamw_matrices")
    big_out = [_unpack_big(o, big_shapes) for o in big_out]

    small_shapes = [given[n].shape for n in SMALL]
    ssum = _all_reduce_small(_pack_small([grads[n] for n in SMALL], loss[0, 0]), "reduce_vectors")
    small_out = _adamw(ssum[None], _pack_small([given[n] for n in SMALL]), _pack_small([given["m_" + n] for n in SMALL]),
                       _pack_small([given["v_" + n] for n in SMALL]), "adamw_vectors")
    total_loss = ssum.reshape(-1)[sum(int(np.prod(s)) for s in small_shapes)]
    small_out = [_unpack_small(o, small_shapes)[0] for o in small_out]

    names = ("meta", "norm_g", "final_g", "ev_w_in", "ev_q_norm_g", "ev_kv_norm_g", "ev_w_uq", "ev_w_ukv", "ev_w_out", "od_w_in", "od_sinks",
             "od_w_out")
    outs = [total_loss, grad_x]
    for kind in range(4):
        for n in names:
            outs.append(big_out[kind][BIG.index(n)] if n in BIG else small_out[kind][SMALL.index(n)])
    return tuple(outs)
```

```python
import functools
import math

import numpy as np
import jax
import jax.numpy as jnp
from jax import lax
from jax.experimental import pallas as pl
from jax.experimental.pallas import tpu as pltpu

F32 = jnp.float32
BF16 = jnp.bfloat16

D_MODEL = 1024
BLOCK = 128
N_META = 16
N_PAD = BLOCK - N_META
NORM_EPS = 1e-6
NEG = -1e30
HEAD = 64
SB_HEADS = 8
MLA_HEADS = 8
MLA_Q_LORA = 256
MLA_KV_LORA = 128
MLA_NOPE = 64
MLA_ROPE = 32
ROPE_BASE = 10000.0
SWA_HEADS = 16
SWA_KV_HEADS = 2
SWA_WINDOW = 128
N_CHIPS = 4
N_DEV = 8

ADAM_LR = 0.001
ADAM_B1 = 0.9
ADAM_B2 = 0.999
ADAM_EPS = 1e-08
ADAM_WD = 0.01
ADAM_STEP = 10

VMEM_LIMIT = 48 * 1024 * 1024

EV_G, EV_Q, EV_K, EV_V, EV_CQ, EV_CKV, EV_KR, EV_N = 0, 1024, 1536, 2048, 2560, 2816, 2944, 3072
OD_G, OD_Q, OD_K, OD_V, OD_N = 0, 1024, 2048, 2176, 2304


def _params(sem=None):
    return pltpu.CompilerParams(dimension_semantics=sem, vmem_limit_bytes=VMEM_LIMIT)


def _row_tile(m):
    return 256 if m % 256 == 0 else 128


def _matmul_rows(m):
    for c in (1088, 1024, 768, 640, 512, 384, 256):
        if m % c == 0:
            return c
    return 128


def _dot(a, b):
    return jnp.dot(a.astype(BF16), b.astype(BF16), preferred_element_type=F32)


def _dot_nt(a, b):
    return lax.dot_general(a.astype(BF16), b.astype(BF16), (((1,), (1,)), ((), ())), preferred_element_type=F32)


def _dot_tn(a, b):
    return lax.dot_general(a.astype(BF16), b.astype(BF16), (((0,), (0,)), ((), ())), preferred_element_type=F32)


def _rms_fwd(h, g, name):
    t, d = h.shape
    tm = _row_tile(t)

    def body(h_ref, g_ref, o_ref):
        x = h_ref[...]
        r = lax.rsqrt(jnp.mean(x * x, axis=-1, keepdims=True) + NORM_EPS)
        o_ref[...] = ((x * r) * g_ref[...]).astype(o_ref.dtype)

    return pl.pallas_call(
        body, grid=(t // tm,),
        in_specs=[pl.BlockSpec((tm, d), lambda i: (i, 0)), pl.BlockSpec((1, d), lambda i: (0, 0))],
        out_specs=pl.BlockSpec((tm, d), lambda i: (i, 0)),
        out_shape=jax.ShapeDtypeStruct((t, d), BF16), compiler_params=_params(("parallel",)), name=name,
    )(h, g)


def _rms_bwd(h, g, dy, dres, name):
    t, d = h.shape
    tm = _row_tile(t)

    def body(h_ref, g_ref, dy_ref, dres_ref, dh_ref, dg_ref):
        @pl.when(pl.program_id(0) == 0)
        def _():
            dg_ref[...] = jnp.zeros_like(dg_ref)

        x = h_ref[...]
        r = lax.rsqrt(jnp.mean(x * x, axis=-1, keepdims=True) + NORM_EPS)
        xr = x * r
        dy_ = dy_ref[...]
        u = dy_ * g_ref[...]
        dh_ref[...] = dres_ref[...] + r * (u - xr * jnp.mean(u * xr, axis=-1, keepdims=True))
        dg_ref[...] += jnp.sum(dy_ * xr, axis=0, keepdims=True)

    row = pl.BlockSpec((tm, d), lambda i: (i, 0))
    vec = pl.BlockSpec((1, d), lambda i: (0, 0))
    return pl.pallas_call(
        body, grid=(t // tm,), in_specs=[row, vec, row, row], out_specs=[row, vec],
        out_shape=[jax.ShapeDtypeStruct((t, d), F32), jax.ShapeDtypeStruct((1, d), F32)],
        compiler_params=_params(("arbitrary",)), name=name,
    )(h, g, dy, dres)


def _col_tile(n):
    for c in (1024, 768, 640, 512, 384, 256, 128):
        if n % c == 0:
            return c
    return n


def _mm(a, w, name, res=None, out_dtype=F32, a_cols=None):
    m = a.shape[0]
    k, n = w.shape
    a_blk = 0 if a_cols is None else a_cols[0] // k
    assert a_cols is None or (a_cols[1] == k and a_cols[0] % k == 0)
    tm, tn = _matmul_rows(m), _col_tile(n)

    def body(*refs):
        if res is None:
            a_ref, w_ref, o_ref = refs
            acc = _dot(a_ref[...], w_ref[...])
        else:
            a_ref, w_ref, r_ref, o_ref = refs
            acc = r_ref[...] + _dot(a_ref[...], w_ref[...])
        o_ref[...] = acc.astype(o_ref.dtype)

    in_specs = [pl.BlockSpec((tm, k), lambda j, i: (i, a_blk)), pl.BlockSpec((k, tn), lambda j, i: (0, j))]
    args = [a, w]
    if res is not None:
        in_specs.append(pl.BlockSpec((tm, tn), lambda j, i: (i, j)))
        args.append(res)
    return pl.pallas_call(
        body, grid=(n // tn, m // tm), in_specs=in_specs, out_specs=pl.BlockSpec((tm, tn), lambda j, i: (i, j)),
        out_shape=jax.ShapeDtypeStruct((m, n), out_dtype), compiler_params=_params(("parallel", "parallel")), name=name,
    )(*args)


def _mm_nt(a, w, name):
    m, n = a.shape
    k = w.shape[0]
    tm, tk = _matmul_rows(m), _col_tile(k)

    def body(a_ref, w_ref, o_ref):
        o_ref[...] = _dot_nt(a_ref[...], w_ref[...])

    return pl.pallas_call(
        body, grid=(k // tk, m // tm),
        in_specs=[pl.BlockSpec((tm, n), lambda j, i: (i, 0)), pl.BlockSpec((tk, n), lambda j, i: (j, 0))],
        out_specs=pl.BlockSpec((tm, tk), lambda j, i: (i, j)),
        out_shape=jax.ShapeDtypeStruct((m, k), F32), compiler_params=_params(("parallel", "parallel")), name=name,
    )(a, w)


def _mm_tn(x, dy, name):
    m, k = x.shape
    n = dy.shape[1]
    tm, tn = _matmul_rows(m), _col_tile(n)

    def body(x_ref, dy_ref, o_ref):
        @pl.when(pl.program_id(1) == 0)
        def _():
            o_ref[...] = jnp.zeros_like(o_ref)

        o_ref[...] += _dot_tn(x_ref[...], dy_ref[...])

    return pl.pallas_call(
        body, grid=(n // tn, m // tm),
        in_specs=[pl.BlockSpec((tm, k), lambda j, i: (i, 0)), pl.BlockSpec((tm, tn), lambda j, i: (i, j))],
        out_specs=pl.BlockSpec((k, tn), lambda j, i: (0, j)),
        out_shape=jax.ShapeDtypeStruct((k, n), F32), compiler_params=_params(("parallel", "arbitrary")), name=name,
    )(x, dy)


def _silu_parts(g):
    s = 1.0 / (1.0 + jnp.exp(-g))
    return g * s, s * (1.0 + g * (1.0 - s))


def _gate_fwd(o_parts, proj, name):
    t = proj.shape[0]
    tm = _row_tile(t)
    w = D_MODEL // len(o_parts)

    def body(*refs):
        g_ref, o_ref = refs[-2], refs[-1]
        for p, r in enumerate(refs[:-2]):
            sil, _ = _silu_parts(g_ref[:, p * w:(p + 1) * w])
            o_ref[:, p * w:(p + 1) * w] = (r[...].astype(F32) * sil).astype(o_ref.dtype)

    return pl.pallas_call(
        body, grid=(t // tm,),
        in_specs=[pl.BlockSpec((tm, w), lambda i: (i, 0)) for _ in o_parts] + [pl.BlockSpec((tm, D_MODEL), lambda i: (i, 0))],
        out_specs=pl.BlockSpec((tm, D_MODEL), lambda i: (i, 0)),
        out_shape=jax.ShapeDtypeStruct((t, D_MODEL), BF16), compiler_params=_params(("parallel",)), name=name,
    )(*o_parts, proj)


def _gate_bwd(dao, o_parts, proj, name):
    t = proj.shape[0]
    tm = _row_tile(t)
    np_ = len(o_parts)
    w = D_MODEL // np_

    def body(*refs):
        dao_ref, g_ref = refs[0], refs[1 + np_]
        do_refs, dg_ref = refs[2 + np_:2 + 2 * np_], refs[-1]
        for p in range(np_):
            sl = slice(p * w, (p + 1) * w)
            sil, dsil = _silu_parts(g_ref[:, sl])
            da = dao_ref[:, sl]
            do_refs[p][...] = da * sil
            dg_ref[:, sl] = (da * refs[1 + p][...].astype(F32) * dsil).astype(dg_ref.dtype)

    full = pl.BlockSpec((tm, D_MODEL), lambda i: (i, 0))
    part = pl.BlockSpec((tm, w), lambda i: (i, 0))
    outs = pl.pallas_call(
        body, grid=(t // tm,), in_specs=[full] + [part] * np_ + [full], out_specs=[part] * np_ + [full],
        out_shape=[jax.ShapeDtypeStruct((t, w), F32)] * np_ + [jax.ShapeDtypeStruct((t, D_MODEL), BF16)],
        compiler_params=_params(("parallel",)), name=name,
    )(dao, *o_parts, proj)
    return outs[:np_], outs[np_]


def _loss_head(h2, gf, target, b, lp):
    d = h2.shape[1]
    nb = lp // BLOCK
    h3 = h2.reshape(b, lp, d)

    def body(h_ref, g_ref, t_ref, dh_ref, dg_ref, loss_ref):
        first = (pl.program_id(0) == 0) & (pl.program_id(1) == 0)

        @pl.when(first)
        def _():
            dg_ref[...] = jnp.zeros_like(dg_ref)
            loss_ref[...] = jnp.zeros_like(loss_ref)

        @pl.when(pl.program_id(1) == 0)
        def _():
            dh_ref[...] = jnp.zeros_like(dh_ref)

        @pl.when(pl.program_id(1) > 0)
        def _():
            x = h_ref[0]
            r = lax.rsqrt(jnp.mean(x * x, axis=-1, keepdims=True) + NORM_EPS)
            xr = x * r
            g = g_ref[...]
            diff = xr * g - t_ref[0]
            loss_ref[...] += 0.5 * jnp.sum(jnp.mean(diff * diff, axis=-1, keepdims=True))
            dy = diff * (1.0 / d)
            u = dy * g
            dh_ref[0] = r * (u - xr * jnp.mean(u * xr, axis=-1, keepdims=True))
            dg_ref[...] += jnp.sum(dy * xr, axis=0, keepdims=True)

    blk = pl.BlockSpec((1, BLOCK, d), lambda bi, n: (bi, n, 0))
    dh, dg, loss = pl.pallas_call(
        body, grid=(b, nb),
        in_specs=[blk, pl.BlockSpec((1, d), lambda bi, n: (0, 0)),
                  pl.BlockSpec((1, BLOCK, d), lambda bi, n: (bi, jnp.maximum(n - 1, 0), 0))],
        out_specs=[blk, pl.BlockSpec((1, d), lambda bi, n: (0, 0)), pl.BlockSpec((8, 128), lambda bi, n: (0, 0))],
        out_shape=[jax.ShapeDtypeStruct((b, lp, d), F32), jax.ShapeDtypeStruct((1, d), F32), jax.ShapeDtypeStruct((8, 128), F32)],
        compiler_params=_params(("arbitrary", "arbitrary")), name="loss_head",
    )(h3, gf, target)
    return dh.reshape(b * lp, d), dg, loss


def _iota2(shape, dim):
    return lax.broadcasted_iota(jnp.int32, shape, dim)


def _split_dot(x, tri):
    hi = x.astype(BF16)
    lo = (x - hi.astype(F32)).astype(BF16)
    return jnp.dot(hi, tri, preferred_element_type=F32) + jnp.dot(lo, tri, preferred_element_type=F32)


KEYS = 256


def _lo_lanes():
    return _iota2((1, BLOCK), 1) < HEAD


def _halves(x, lo):
    zero = jnp.zeros_like(x)
    return jnp.where(lo, x, zero), jnp.where(lo, zero, x)


def _stack_halves(x, lo):
    a, b = _halves(x, lo)
    return jnp.concatenate([a, b], axis=0)


def _pair(a, b, lo):
    return jnp.where(lo, a, b)


def _key_chunk(c, lp, t_idx, strict):
    first = c * KEYS
    s0 = pl.multiple_of(jnp.minimum(first, lp - KEYS), BLOCK)
    s_idx = s0 + _iota2(t_idx.shape, 1)
    seen = (s_idx < t_idx) if strict else (s_idx <= t_idx)
    return s0, seen & (s_idx >= jnp.maximum(first, N_PAD))


def _n_chunks(i):
    return (i + 2) // 2


QROWS = 512


def _for_query_tiles(nb, tile):
    per = QROWS // BLOCK

    def step(j, _):
        tile(pl.multiple_of(j * QROWS, QROWS), QROWS, (j + 1) * (QROWS // KEYS))
        return 0

    lax.fori_loop(0, nb // per, step, 0)
    for i in range(nb - nb % per, nb):
        tile(i * BLOCK, BLOCK, _n_chunks(i))


def _sb_scores(q_h, k, valid, after):
    z = _dot_nt(q_h, k) * (HEAD ** -0.5)
    sp = jnp.log(1.0 + jnp.exp(-jnp.abs(z)))
    lb = jnp.minimum(z, 0.0) - sp
    l1m_all = -jnp.maximum(z, 0.0) - sp
    l1m = jnp.where(valid, l1m_all, 0.0)
    return lb, l1m_all, l1m, _split_dot(l1m, after)


def _sb_fwd(proj3, name):
    b, lp, _ = proj3.shape
    nb = lp // BLOCK
    npair = SB_HEADS // 2

    def body(q_ref, k_ref, v_ref, o_ref, tot_ref):
        lo = _lo_lanes()
        after = (_iota2((KEYS, KEYS), 0) > _iota2((KEYS, KEYS), 1)).astype(BF16)

        def qtile(r0, bq, n):
            qs = _halves(q_ref[0, pl.ds(r0, bq), :].astype(BF16), lo)
            t_idx = r0 + _iota2((bq, KEYS), 0)

            def kchunk(cc, carry):
                cs, acc = carry[:2], carry[2]
                s0, valid = _key_chunk(n - 1 - cc, lp, t_idx, True)
                k = k_ref[0, pl.ds(s0, KEYS), :].astype(BF16)
                a_s, new = [], []
                for h in range(2):
                    lb, _, l1m, suf = _sb_scores(qs[h], k, valid, after)
                    a_s.append(jnp.where(valid, jnp.exp(lb + suf + cs[h]), 0.0).astype(BF16))
                    new.append(cs[h] + jnp.sum(l1m, axis=1, keepdims=True))
                v_bd = _stack_halves(v_ref[0, pl.ds(s0, KEYS), :].astype(BF16), lo)
                return (*new, acc + jnp.dot(jnp.concatenate(a_s, axis=1), v_bd, preferred_element_type=F32))

            zero = jnp.zeros((bq, 1), F32)
            c_a, c_b, acc = lax.fori_loop(0, n, kchunk, (zero, zero, jnp.zeros((bq, BLOCK), F32)))
            o_ref[0, pl.ds(r0, bq), :] = acc
            tot_ref[0, pl.ds(r0, bq), :] = jnp.broadcast_to(_pair(c_a, c_b, lo), (bq, BLOCK))

        _for_query_tiles(nb, qtile)

    def col(first):
        return pl.BlockSpec((1, lp, 2 * HEAD), lambda bi, hp: (bi, 0, first // (2 * HEAD) + hp))

    shp = jax.ShapeDtypeStruct((b, lp, SB_HEADS * HEAD), F32)
    return pl.pallas_call(
        body, grid=(b, npair), in_specs=[col(EV_Q), col(EV_K), col(EV_V)], out_specs=[col(0), col(0)], out_shape=[shp, shp],
        compiler_params=_params(("parallel", "parallel")), name=name,
    )(proj3, proj3, proj3)


def _sb_bwd(proj3, tot, do, name):
    b, lp, _ = proj3.shape
    nb = lp // BLOCK
    npair = SB_HEADS // 2

    def body(q_ref, k_ref, v_ref, tot_ref, do_ref, dq_ref, dk_ref, dv_ref):
        lo = _lo_lanes()
        after = (_iota2((KEYS, KEYS), 0) > _iota2((KEYS, KEYS), 1)).astype(BF16)
        before = (_iota2((KEYS, KEYS), 0) < _iota2((KEYS, KEYS), 1)).astype(BF16)
        dk_ref[...] = jnp.zeros_like(dk_ref)
        dv_ref[...] = jnp.zeros_like(dv_ref)

        def qtile(r0, bq, n):
            rows = pl.ds(r0, bq)
            qs = _halves(q_ref[0, rows, :].astype(BF16), lo)
            dos = _halves(do_ref[0, rows, :].astype(BF16), lo)
            tot_i = tot_ref[0, rows, :]
            tots = (tot_i[:, 0:1], tot_i[:, HEAD:HEAD + 1])
            q_st, do_st = jnp.concatenate(qs, axis=0), jnp.concatenate(dos, axis=0)
            t_idx = r0 + _iota2((bq, KEYS), 0)

            def kchunk(c, carry):
                s0, valid = _key_chunk(c, lp, t_idx, True)
                keys = pl.ds(s0, KEYS)
                k = k_ref[0, keys, :].astype(BF16)
                v = v_ref[0, keys, :].astype(BF16)
                a_s, dzs, new = [], [], []
                for h in range(2):
                    left, pre = carry[2 * h], carry[2 * h + 1]
                    lb, l1m_all, l1m, suf = _sb_scores(qs[h], k, valid, after)
                    here = jnp.sum(l1m, axis=1, keepdims=True)
                    a = jnp.where(valid, jnp.exp(lb + suf + (tots[h] - left - here)), 0.0)
                    w = a * _dot_nt(dos[h], v)
                    dz = jnp.where(valid, w * jnp.exp(l1m_all) - (pre + _split_dot(w, before)) * jnp.exp(lb), 0.0) * (HEAD ** -0.5)
                    new += [left + here, pre + jnp.sum(w, axis=1, keepdims=True)]
                    a_s.append(a.astype(BF16))
                    dzs.append(dz.astype(BF16))
                dk_ref[0, keys, :] += _dot_tn(jnp.concatenate(dzs, axis=0), q_st)
                dv_ref[0, keys, :] += _dot_tn(jnp.concatenate(a_s, axis=0), do_st)
                dq = carry[4] + jnp.dot(jnp.concatenate(dzs, axis=1), _stack_halves(k, lo), preferred_element_type=F32)
                return (*new, dq)

            zero = jnp.zeros((bq, 1), F32)
            out = lax.fori_loop(0, n, kchunk, (zero, zero, zero, zero, jnp.zeros((bq, BLOCK), F32)))
            dq_ref[0, rows, :] = out[4]

        _for_query_tiles(nb, qtile)

    def col(first):
        return pl.BlockSpec((1, lp, 2 * HEAD), lambda bi, hp: (bi, 0, first // (2 * HEAD) + hp))

    shp = jax.ShapeDtypeStruct((b, lp, SB_HEADS * HEAD), F32)
    return pl.pallas_call(
        body, grid=(b, npair), in_specs=[col(EV_Q), col(EV_K), col(EV_V), col(0), col(0)], out_specs=[col(0)] * 3, out_shape=[shp] * 3,
        compiler_params=_params(("parallel", "parallel")), name=name,
    )(proj3, proj3, proj3, tot, do)


def _rope_tables(lp):
    half = MLA_ROPE // 2
    pos = (np.arange(lp) - N_PAD).astype(np.float32)
    inv = jnp.asarray(ROPE_BASE, F32) ** (-jnp.arange(half, dtype=F32) / half)
    ang = jnp.asarray(pos)[:, None] * inv[None, :]
    cos, sin = jnp.cos(ang), jnp.sin(ang)
    zeros = lambda n: jnp.zeros((lp, n), F32)
    c = jnp.concatenate([jnp.ones((lp, MLA_NOPE), F32), cos, cos, zeros(32)], axis=1)
    s1 = jnp.concatenate([zeros(MLA_NOPE), -sin, zeros(half), zeros(32)], axis=1)
    s2 = jnp.concatenate([zeros(MLA_NOPE), zeros(half), sin, zeros(32)], axis=1)
    return c, s1, s2


def _rope(x, c, s1, s2):
    half = MLA_ROPE // 2
    return x * c + pltpu.roll(x, BLOCK - half, 1) * s1 + pltpu.roll(x, half, 1) * s2


def _rope_t(dy, c, s1, s2):
    half = MLA_ROPE // 2
    return dy * c + pltpu.roll(dy * s1, half, 1) + pltpu.roll(dy * s2, BLOCK - half, 1)


def _rms_rows(x, g):
    r = lax.rsqrt(jnp.mean(x * x, axis=-1, keepdims=True) + NORM_EPS)
    return x * r, r


def _mla_prep_fwd(proj3, gq, gkv, wq, wk, wv, tabs, name):
    b, lp, _ = proj3.shape
    nb = lp // BLOCK
    hw = MLA_HEADS * BLOCK

    def body(cq_ref, ckv_ref, kr_ref, gq_ref, gkv_ref, wq_ref, wk_ref, wv_ref, c_ref, s1_ref, s2_ref, qf_ref, kf_ref, v_ref):
        c, s1, s2 = c_ref[...], s1_ref[...], s2_ref[...]
        xq, _ = _rms_rows(cq_ref[0], None)
        qh = _dot(xq * gq_ref[...], wq_ref[...])
        xk, _ = _rms_rows(ckv_ref[0], None)
        ckv_n = xk * gkv_ref[...]
        kv = _dot(ckv_n, wk_ref[...])
        v_ref[0] = _dot(ckv_n, wv_ref[...]).astype(v_ref.dtype)
        kr = _rope(kr_ref[0], c, s1, s2)
        for h in range(MLA_HEADS):
            ls = slice(h * BLOCK, (h + 1) * BLOCK)
            qf_ref[0, :, ls] = _rope(qh[:, ls], c, s1, s2).astype(qf_ref.dtype)
            kf_ref[0, :, ls] = (kv[:, ls] + kr).astype(kf_ref.dtype)

    def col(first, width):
        return pl.BlockSpec((1, BLOCK, width), lambda bi, n: (bi, n, first // width))

    def whole(a):
        return pl.BlockSpec(a.shape, lambda bi, n: (0,) * a.ndim)

    tab = pl.BlockSpec((BLOCK, BLOCK), lambda bi, n: (n, 0))
    return pl.pallas_call(
        body, grid=(b, nb),
        in_specs=[col(EV_CQ, MLA_Q_LORA), col(EV_CKV, MLA_KV_LORA), col(EV_KR, BLOCK), whole(gq), whole(gkv), whole(wq), whole(wk),
                  whole(wv), tab, tab, tab],
        out_specs=[col(0, hw), col(0, hw), col(0, MLA_HEADS * HEAD)],
        out_shape=[jax.ShapeDtypeStruct((b, lp, hw), BF16), jax.ShapeDtypeStruct((b, lp, hw), BF16),
                   jax.ShapeDtypeStruct((b, lp, MLA_HEADS * HEAD), BF16)],
        compiler_params=_params(("parallel", "parallel")), name=name,
    )(proj3, proj3, proj3, gq, gkv, wq, wk, wv, *tabs)


def _mla_prep_bwd(proj3, gq, gkv, wq, wk, wv, tabs, dqf, dkf, dv, name):
    b, lp, _ = proj3.shape
    nb = lp // BLOCK
    hw = MLA_HEADS * BLOCK

    def body(cq_ref, ckv_ref, gq_ref, gkv_ref, wq_ref, wk_ref, wv_ref, c_ref, s1_ref, s2_ref, dqf_ref, dkf_ref, dv_ref,
             dcq_ref, dckv_ref, dkr_ref, dwq_ref, dwk_ref, dwv_ref, dgq_ref, dgkv_ref, dqh):
        @pl.when((pl.program_id(0) == 0) & (pl.program_id(1) == 0))
        def _():
            for r in (dwq_ref, dwk_ref, dwv_ref, dgq_ref, dgkv_ref):
                r[...] = jnp.zeros_like(r)

        c, s1, s2 = c_ref[...], s1_ref[...], s2_ref[...]
        dkr = jnp.zeros((BLOCK, BLOCK), F32)
        for h in range(MLA_HEADS):
            ls = slice(h * BLOCK, (h + 1) * BLOCK)
            dqh[:, ls] = _rope_t(dqf_ref[0, :, ls].astype(F32), c, s1, s2).astype(dqh.dtype)
            dkr = dkr + dkf_ref[0, :, ls].astype(F32)
        dkr_ref[0] = _rope_t(dkr, c, s1, s2).astype(dkr_ref.dtype)

        def norm_bwd(x, g, dy, dg_ref):
            xr, r = _rms_rows(x, None)
            u = dy * g
            dg_ref[...] += jnp.sum(dy * xr, axis=0, keepdims=True)
            return r * (u - xr * jnp.mean(u * xr, axis=-1, keepdims=True))

        xq, _ = _rms_rows(cq_ref[0], None)
        cq_n = xq * gq_ref[...]
        dwq_ref[...] += _dot_tn(cq_n, dqh[...])
        dcq_ref[0] = norm_bwd(cq_ref[0], gq_ref[...], _dot_nt(dqh[...], wq_ref[...]), dgq_ref).astype(dcq_ref.dtype)
        xk, _ = _rms_rows(ckv_ref[0], None)
        ckv_n = xk * gkv_ref[...]
        dkf_, dv_ = dkf_ref[0], dv_ref[0]
        dwk_ref[...] += _dot_tn(ckv_n, dkf_)
        dwv_ref[...] += _dot_tn(ckv_n, dv_)
        dckv_n = _dot_nt(dkf_, wk_ref[...]) + _dot_nt(dv_, wv_ref[...])
        dckv_ref[0] = norm_bwd(ckv_ref[0], gkv_ref[...], dckv_n, dgkv_ref).astype(dckv_ref.dtype)

    def col(first, width):
        return pl.BlockSpec((1, BLOCK, width), lambda bi, n: (bi, n, first // width))

    def whole(a):
        return pl.BlockSpec(a.shape, lambda bi, n: (0,) * len(a.shape))

    tab = pl.BlockSpec((BLOCK, BLOCK), lambda bi, n: (n, 0))
    acc_shapes = [jax.ShapeDtypeStruct(a.shape, F32) for a in (wq, wk, wv, gq, gkv)]
    return pl.pallas_call(
        body, grid=(b, nb),
        in_specs=[col(EV_CQ, MLA_Q_LORA), col(EV_CKV, MLA_KV_LORA), whole(gq), whole(gkv), whole(wq), whole(wk), whole(wv), tab, tab, tab,
                  col(0, hw), col(0, hw), col(0, MLA_HEADS * HEAD)],
        out_specs=[col(0, MLA_Q_LORA), col(0, MLA_KV_LORA), col(0, BLOCK)] + [whole(a) for a in acc_shapes],
        out_shape=[jax.ShapeDtypeStruct((b, lp, MLA_Q_LORA), BF16), jax.ShapeDtypeStruct((b, lp, MLA_KV_LORA), BF16),
                   jax.ShapeDtypeStruct((b, lp, BLOCK), BF16)] + acc_shapes,
        scratch_shapes=[pltpu.VMEM((BLOCK, hw), BF16)],
        compiler_params=_params(("arbitrary", "arbitrary")), name=name,
    )(proj3, proj3, gq, gkv, wq, wk, wv, *tabs, dqf, dkf, dv)


def _mla_fwd(qf, kf, v, name):
    b, lp, _ = qf.shape
    nb = lp // BLOCK
    npair = MLA_HEADS // 2
    scale = (MLA_NOPE + MLA_ROPE) ** -0.5

    def body(q_ref, k_ref, v_ref, o_ref, lse_ref):
        lo = _lo_lanes()

        def qtile(r0, bq, n):
            qs = [q_ref[0, pl.ds(r0, bq), h * BLOCK:(h + 1) * BLOCK] for h in range(2)]
            t_idx = r0 + _iota2((bq, KEYS), 0)

            def kchunk(c, carry):
                stats, acc = carry[:4], carry[4]
                s0, valid = _key_chunk(c, lp, t_idx, False)
                ps, new, alphas = [], [], []
                for h in range(2):
                    m, l = stats[2 * h], stats[2 * h + 1]
                    s = jnp.where(valid, _dot_nt(qs[h], k_ref[0, pl.ds(s0, KEYS), h * BLOCK:(h + 1) * BLOCK]) * scale, NEG)
                    m_new = jnp.maximum(m, jnp.max(s, axis=1, keepdims=True))
                    p = jnp.where(valid, jnp.exp(s - m_new), 0.0)
                    alpha = jnp.exp(m - m_new)
                    new += [m_new, alpha * l + jnp.sum(p, axis=1, keepdims=True)]
                    alphas.append(alpha)
                    ps.append(p.astype(BF16))
                pv = jnp.dot(jnp.concatenate(ps, axis=1), _stack_halves(v_ref[0, pl.ds(s0, KEYS), :], lo), preferred_element_type=F32)
                return (*new, _pair(alphas[0], alphas[1], lo) * acc + pv)

            neg, zero = jnp.full((bq, 1), NEG, F32), jnp.zeros((bq, 1), F32)
            m_a, l_a, m_b, l_b, acc = lax.fori_loop(0, n, kchunk, (neg, zero, neg, zero, jnp.zeros((bq, BLOCK), F32)))
            l = _pair(l_a, l_b, lo)
            seen = l > 0.0
            safe = jnp.where(seen, l, 1.0)
            o_ref[0, pl.ds(r0, bq), :] = acc / safe
            lse_ref[0, pl.ds(r0, bq), :] = jnp.where(seen, _pair(m_a, m_b, lo) + jnp.log(safe), 0.0)

        _for_query_tiles(nb, qtile)

    wide = pl.BlockSpec((1, lp, 2 * BLOCK), lambda bi, hp: (bi, 0, hp))
    thin = pl.BlockSpec((1, lp, 2 * HEAD), lambda bi, hp: (bi, 0, hp))
    shp = jax.ShapeDtypeStruct((b, lp, MLA_HEADS * HEAD), F32)
    return pl.pallas_call(
        body, grid=(b, npair), in_specs=[wide, wide, thin], out_specs=[thin, thin], out_shape=[shp, shp],
        compiler_params=_params(("parallel", "parallel")), name=name,
    )(qf, kf, v)


def _mla_bwd(qf, kf, v, o, lse, do, name):
    b, lp, _ = qf.shape
    nb = lp // BLOCK
    npair = MLA_HEADS // 2
    scale = (MLA_NOPE + MLA_ROPE) ** -0.5

    def body(q_ref, k_ref, v_ref, o_ref, lse_ref, do_ref, dq_ref, dk_ref, dv_ref):
        lo = _lo_lanes()
        dk_ref[...] = jnp.zeros_like(dk_ref)
        dv_ref[...] = jnp.zeros_like(dv_ref)

        def qtile(r0, bq, n):
            rows = pl.ds(r0, bq)
            qs = [q_ref[0, rows, h * BLOCK:(h + 1) * BLOCK] for h in range(2)]
            do_i = do_ref[0, rows, :]
            dos = _halves(do_i.astype(BF16), lo)
            both = do_i * o_ref[0, rows, :]
            d_a = jnp.sum(jnp.where(lo, both, 0.0), axis=1, keepdims=True)
            dsum = (d_a, jnp.sum(both, axis=1, keepdims=True) - d_a)
            lse_i = lse_ref[0, rows, :]
            lses = (lse_i[:, 0:1], lse_i[:, HEAD:HEAD + 1])
            t_idx = r0 + _iota2((bq, KEYS), 0)

            def kchunk(c, dqs):
                s0, valid = _key_chunk(c, lp, t_idx, False)
                keys = pl.ds(s0, KEYS)
                v_c = v_ref[0, keys, :]
                ps, out = [], []
                for h in range(2):
                    lanes = slice(h * BLOCK, (h + 1) * BLOCK)
                    k = k_ref[0, keys, lanes]
                    s = jnp.where(valid, _dot_nt(qs[h], k) * scale, NEG)
                    p = jnp.where(valid, jnp.exp(s - lses[h]), 0.0)
                    ds = p * (_dot_nt(dos[h], v_c) - dsum[h]) * scale
                    dk_ref[0, keys, lanes] += _dot_tn(ds, qs[h])
                    out.append(dqs[h] + _dot(ds, k))
                    ps.append(p.astype(BF16))
                dv_ref[0, keys, :] += _dot_tn(jnp.concatenate(ps, axis=0), jnp.concatenate(dos, axis=0))
                return tuple(out)

            zero = jnp.zeros((bq, BLOCK), F32)
            dq_a, dq_b = lax.fori_loop(0, n, kchunk, (zero, zero))
            dq_ref[0, rows, 0:BLOCK] = dq_a
            dq_ref[0, rows, BLOCK:2 * BLOCK] = dq_b

        _for_query_tiles(nb, qtile)

    wide = pl.BlockSpec((1, lp, 2 * BLOCK), lambda bi, hp: (bi, 0, hp))
    thin = pl.BlockSpec((1, lp, 2 * HEAD), lambda bi, hp: (bi, 0, hp))
    return pl.pallas_call(
        body, grid=(b, npair), in_specs=[wide, wide, thin, thin, thin, thin], out_specs=[wide, wide, thin],
        out_shape=[jax.ShapeDtypeStruct(qf.shape, F32), jax.ShapeDtypeStruct(qf.shape, F32), jax.ShapeDtypeStruct(v.shape, F32)],
        compiler_params=_params(("parallel", "parallel")), name=name,
    )(qf, kf, v, o, lse, do)


def _swa_keys(k_ref, v_ref, n, kv):
    prev = jnp.maximum(n - 1, 0)
    rows = lambda blk: pl.ds(pl.multiple_of(blk * BLOCK, BLOCK), BLOCK)
    mine = (_iota2((1, BLOCK), 1) >= HEAD).astype(jnp.int32) == kv

    def both_halves(ref):
        x = jnp.concatenate([ref[0, rows(prev), :], ref[0, rows(n), :], ref[0, 0:BLOCK, :]], axis=0)
        return jnp.where(mine, x, pltpu.roll(x, HEAD, 1))

    slot = _iota2((3 * BLOCK, BLOCK), 0)
    loc = slot % BLOCK
    s_idx = jnp.where(slot < BLOCK, (n - 1) * BLOCK + loc, jnp.where(slot < 2 * BLOCK, n * BLOCK + loc, loc))
    dist = n * BLOCK + _iota2((3 * BLOCK, BLOCK), 1) - s_idx
    band = (slot < 2 * BLOCK) & (dist >= 0) & (dist < SWA_WINDOW) & (s_idx >= BLOCK)
    meta = (slot >= 2 * BLOCK) & (s_idx >= N_PAD) & (dist >= 0)
    return both_halves(k_ref), both_halves(v_ref), band | meta, dist.astype(F32), prev


def _swa_probs(q_h, kdup, valid, dist, head, sink_ref):
    slope = jnp.exp(jnp.full((1, 1), -8.0 * math.log(2.0) / SWA_HEADS, F32) * (head + 1).astype(F32))
    s = jnp.where(valid, _dot_nt(kdup, q_h) * (HEAD ** -0.5) - slope * dist, NEG)
    sink = sink_ref[pl.ds(head, 1), 0:1]
    m = jnp.maximum(jnp.max(s, axis=0, keepdims=True), sink)
    e = jnp.where(valid, jnp.exp(s - m), 0.0)
    es = jnp.exp(sink - m)
    inv = 1.0 / (jnp.sum(e, axis=0, keepdims=True) + es)
    return e * inv, es * inv


def _split_rows_t(x):
    xt = x.T
    first = _iota2(xt.shape, 0) < HEAD
    zero = jnp.zeros_like(xt)
    return jnp.concatenate([jnp.where(first, xt, zero), jnp.where(first, zero, xt)], axis=1).astype(BF16)


SWA_PAIRS = SWA_HEADS // SWA_KV_HEADS // 2
SWA_GROUP = SWA_PAIRS * 2 * HEAD


def _swa_specs(b, lp):
    nb = lp // BLOCK
    qcol = lambda first: pl.BlockSpec((1, BLOCK, SWA_GROUP), lambda bi, kv, n: (bi, n, first // SWA_GROUP + kv))
    kcol = lambda first: pl.BlockSpec((1, lp, BLOCK), lambda bi, kv, n: (bi, 0, first // BLOCK))
    sink = pl.BlockSpec((SWA_HEADS, BLOCK), lambda bi, kv, n: (0, 0))
    return (b, SWA_KV_HEADS, nb), qcol, kcol, sink


def _swa_fwd(proj3, sinks, name):
    b, lp, _ = proj3.shape
    grid, qcol, kcol, sink = _swa_specs(b, lp)

    def body(q_ref, k_ref, v_ref, sink_ref, o_ref):
        kv, n = pl.program_id(1), pl.program_id(2)
        lo = _lo_lanes()
        kdup, vdup, valid, dist, _ = _swa_keys(k_ref, v_ref, n, kv)
        kdup = kdup.astype(BF16)
        vt = _split_rows_t(vdup)
        for p in range(SWA_PAIRS):
            lanes = slice(p * BLOCK, (p + 1) * BLOCK)
            qs = _halves(q_ref[0, :, lanes].astype(BF16), lo)
            probs = [_swa_probs(qs[hh], kdup, valid, dist, (kv * SWA_PAIRS + p) * 2 + hh, sink_ref)[0].astype(BF16) for hh in range(2)]
            o_ref[0, :, lanes] = jnp.dot(vt, jnp.concatenate(probs, axis=0), preferred_element_type=F32).T

    return pl.pallas_call(
        body, grid=grid, in_specs=[qcol(OD_Q), kcol(OD_K), kcol(OD_V), sink], out_specs=qcol(0),
        out_shape=jax.ShapeDtypeStruct((b, lp, SWA_HEADS * HEAD), F32),
        compiler_params=_params(("parallel", "parallel", "parallel")), name=name,
    )(proj3, proj3, proj3, sinks)


def _swa_bwd(proj3, sinks, do, name):
    b, lp, _ = proj3.shape
    nb = lp // BLOCK
    grid, qcol, kcol, sink = _swa_specs(b, lp)

    def body(q_ref, k_ref, v_ref, sink_ref, do_ref, dq_ref, dk_ref, dv_ref, dsink_ref):
        kv, n = pl.program_id(1), pl.program_id(2)

        @pl.when((n == 0) & (pl.program_id(0) == 0) & (kv == 0))
        def _():
            dsink_ref[...] = jnp.zeros_like(dsink_ref)

        @pl.when(n == 0)
        def _():
            dk_ref[...] = jnp.zeros_like(dk_ref)
            dv_ref[...] = jnp.zeros_like(dv_ref)

        lo = _lo_lanes()
        kdup, vdup, valid, dist, prev = _swa_keys(k_ref, v_ref, n, kv)
        kt = _split_rows_t(kdup)
        kdup, vdup = kdup.astype(BF16), vdup.astype(BF16)
        dkc = jnp.zeros((3 * BLOCK, BLOCK), F32)
        dvc = jnp.zeros((3 * BLOCK, BLOCK), F32)
        for p in range(SWA_PAIRS):
            lanes = slice(p * BLOCK, (p + 1) * BLOCK)
            qs = _halves(q_ref[0, :, lanes].astype(BF16), lo)
            dos = _halves(do_ref[0, :, lanes].astype(BF16), lo)
            dss, prs = [], []
            for hh in range(2):
                head = (kv * SWA_PAIRS + p) * 2 + hh
                pr, ps = _swa_probs(qs[hh], kdup, valid, dist, head, sink_ref)
                dp = _dot_nt(vdup, dos[hh])
                dsum = jnp.sum(pr * dp, axis=0, keepdims=True)
                dsink_ref[pl.ds(head, 1), :] += jnp.broadcast_to(-jnp.sum(ps * dsum, axis=1, keepdims=True), (1, BLOCK))
                dss.append((pr * (dp - dsum) * (HEAD ** -0.5)).astype(BF16))
                prs.append(pr.astype(BF16))
            dq_ref[0, :, lanes] = jnp.dot(kt, jnp.concatenate(dss, axis=0), preferred_element_type=F32).T
            dkc = dkc + jnp.dot(jnp.concatenate(dss, axis=1), jnp.concatenate(qs, axis=0), preferred_element_type=F32)
            dvc = dvc + jnp.dot(jnp.concatenate(prs, axis=1), jnp.concatenate(dos, axis=0), preferred_element_type=F32)
        rows = lambda blk: pl.ds(pl.multiple_of(blk * BLOCK, BLOCK), BLOCK)
        for part, r in enumerate((rows(prev), rows(n), slice(0, BLOCK))):
            dk_ref[0, 0, r, :] += dkc[part * BLOCK:(part + 1) * BLOCK]
            dv_ref[0, 0, r, :] += dvc[part * BLOCK:(part + 1) * BLOCK]

        @pl.when(n == nb - 1)
        def _():
            for ref in (dk_ref, dv_ref):
                x = ref[0, 0]
                ref[0, 0] = x + pltpu.roll(x, HEAD, 1)

    kvout = pl.BlockSpec((1, 1, lp, BLOCK), lambda bi, kv, n: (bi, kv, 0, 0))
    kvshape = jax.ShapeDtypeStruct((b, SWA_KV_HEADS, lp, BLOCK), F32)
    return pl.pallas_call(
        body, grid=grid, in_specs=[qcol(OD_Q), kcol(OD_K), kcol(OD_V), sink, qcol(0)], out_specs=[qcol(0), kvout, kvout, sink],
        out_shape=[jax.ShapeDtypeStruct((b, lp, SWA_HEADS * HEAD), F32), kvshape, kvshape, jax.ShapeDtypeStruct((SWA_HEADS, BLOCK), F32)],
        compiler_params=_params(("arbitrary", "arbitrary", "arbitrary")), name=name,
    )(proj3, proj3, proj3, sinks, do)


def _kernel_weights(ev_w_in, ev_w_uq, ev_w_ukv, od_w_in):
    zeros = lambda r, c: jnp.zeros((r, c), ev_w_in.dtype)
    q_sb, k_sb, v_sb, g_sb, c_q, c_kv, k_r, g_mla = jnp.split(ev_w_in, [512, 1024, 1536, 2048, 2304, 2432, 2464], axis=1)
    w0 = jnp.concatenate([g_sb, g_mla, q_sb, k_sb, v_sb, c_q, c_kv, zeros(D_MODEL, MLA_NOPE), k_r, zeros(D_MODEL, 32)], axis=1)
    uq = ev_w_uq.reshape(MLA_Q_LORA, MLA_HEADS, MLA_NOPE + MLA_ROPE)
    wq = jnp.pad(uq, ((0, 0), (0, 0), (0, BLOCK - MLA_NOPE - MLA_ROPE))).reshape(MLA_Q_LORA, MLA_HEADS * BLOCK)
    ukv = ev_w_ukv.reshape(MLA_KV_LORA, MLA_HEADS, BLOCK)
    wk = jnp.pad(ukv[:, :, :MLA_NOPE], ((0, 0), (0, 0), (0, BLOCK - MLA_NOPE))).reshape(MLA_KV_LORA, MLA_HEADS * BLOCK)
    wv = ukv[:, :, MLA_NOPE:].reshape(MLA_KV_LORA, MLA_HEADS * HEAD)
    q, k, v, g = jnp.split(od_w_in, [1024, 1152, 1280], axis=1)
    w1 = jnp.concatenate([g, q, k, v], axis=1)
    return w0, wq, wk, wv, w1


def _original_grads(dw0, dwq, dwk, dwv, dw1):
    sl = lambda a, first, n: a[:, first:first + n]
    d_ev_w_in = jnp.concatenate([sl(dw0, EV_Q, 512), sl(dw0, EV_K, 512), sl(dw0, EV_V, 512), sl(dw0, EV_G, 512), sl(dw0, EV_CQ, 256),
                                 sl(dw0, EV_CKV, 128), sl(dw0, EV_KR + MLA_NOPE, MLA_ROPE), sl(dw0, EV_G + 512, 512)], axis=1)
    d_uq = dwq.reshape(MLA_Q_LORA, MLA_HEADS, BLOCK)[:, :, :MLA_NOPE + MLA_ROPE].reshape(MLA_Q_LORA, -1)
    d_ukv = jnp.concatenate([dwk.reshape(MLA_KV_LORA, MLA_HEADS, BLOCK)[:, :, :MLA_NOPE], dwv.reshape(MLA_KV_LORA, MLA_HEADS, HEAD)],
                            axis=2).reshape(MLA_KV_LORA, -1)
    d_od_w_in = jnp.concatenate([sl(dw1, OD_Q, 1024), sl(dw1, OD_K, 128), sl(dw1, OD_V, 128), sl(dw1, OD_G, 1024)], axis=1)
    return d_ev_w_in, d_uq, d_ukv, d_od_w_in


def _meta_rows_sum(dh0_3):
    b, _, d = dh0_3.shape

    def body(x_ref, o_ref):
        acc = x_ref[0, N_PAD:BLOCK, :]
        for i in range(1, b):
            acc = acc + x_ref[i, N_PAD:BLOCK, :]
        o_ref[...] = acc

    return pl.pallas_call(
        body, grid=(1,), in_specs=[pl.BlockSpec((b, BLOCK, d), lambda i: (0, 0, 0))], out_specs=pl.BlockSpec((N_META, d), lambda i: (0, 0)),
        out_shape=jax.ShapeDtypeStruct((N_META, d), F32), compiler_params=_params(("arbitrary",)), name="meta_rows_sum",
    )(dh0_3)


def _local_step(x, meta, norm_g, final_g, gq, gkv, sinks, target, ev_w_in, ev_w_uq, ev_w_ukv, wo0, od_w_in, wo1):
    b, seq, d = x.shape
    lp = seq + BLOCK
    t = b * lp
    w0, wq, wk, wv, w1 = _kernel_weights(ev_w_in, ev_w_uq, ev_w_ukv, od_w_in)
    h0 = jnp.concatenate([jnp.zeros((b, N_PAD, d), F32), jnp.broadcast_to(meta[None], (b, N_META, d)), x], axis=1).reshape(t, d)
    tabs = _rope_tables(lp)
    g0, g1 = norm_g[0:1], norm_g[1:2]

    hn0 = _rms_fwd(h0, g0, "norm0")
    proj0 = _mm(hn0, w0, "inproj0")
    p0 = proj0.reshape(b, lp, EV_N)
    o_sb, sb_tot = _sb_fwd(p0, "sb_fwd")
    qf, kf, v = _mla_prep_fwd(p0, gq, gkv, wq, wk, wv, tabs, "mla_prep_fwd")
    o_mla, lse = _mla_fwd(qf, kf, v, "mla_fwd")
    o0 = [o_sb.reshape(t, -1), o_mla.reshape(t, -1)]
    ao0 = _gate_fwd(o0, proj0, "gate0")
    h1 = _mm(ao0, wo0, "outproj0", res=h0)

    hn1 = _rms_fwd(h1, g1, "norm1")
    proj1 = _mm(hn1, w1, "inproj1")
    p1 = proj1.reshape(b, lp, OD_N)
    sinks_b = jnp.broadcast_to(sinks.reshape(SWA_HEADS, 1), (SWA_HEADS, BLOCK))
    o1 = _swa_fwd(p1, sinks_b, "swa_fwd").reshape(t, -1)
    ao1 = _gate_fwd([o1], proj1, "gate1")
    h2 = _mm(ao1, wo1, "outproj1", res=h1)

    dh2, d_final_g, loss = _loss_head(h2, final_g.reshape(1, d), target, b, lp)

    d_wo1 = _mm_tn(ao1, dh2, "d_wo1")
    dao1 = _mm_nt(dh2, wo1, "d_ao1")
    (do1,), dg1 = _gate_bwd(dao1, [o1], proj1, "gate1_bwd")
    dq1, dk4, dv4, d_sinks = _swa_bwd(p1, sinks_b, do1.reshape(b, lp, -1), "swa_bwd")
    unheads = lambda a: a[..., :HEAD].transpose(0, 2, 1, 3).reshape(t, SWA_KV_HEADS * HEAD).astype(BF16)
    dproj1 = jnp.concatenate([dg1, dq1.reshape(t, -1).astype(BF16), unheads(dk4), unheads(dv4)], axis=1)
    d_w1 = _mm_tn(hn1, dproj1, "d_w1")
    dhn1 = _mm_nt(dproj1, w1, "d_hn1")
    dh1, d_g1 = _rms_bwd(h1, g1, dhn1, dh2, "norm1_bwd")

    d_wo0 = _mm_tn(ao0, dh1, "d_wo0")
    dao0 = _mm_nt(dh1, wo0, "d_ao0")
    (do_sb, do_mla), dg0 = _gate_bwd(dao0, o0, proj0, "gate0_bwd")
    dq_sb, dk_sb, dv_sb = _sb_bwd(p0, sb_tot, do_sb.reshape(b, lp, -1), "sb_bwd")
    dqf, dkf, dv = _mla_bwd(qf, kf, v, o_mla, lse, do_mla.reshape(b, lp, -1), "mla_bwd")
    dcq, dckv, dkr, d_wq, d_wk, d_wv, d_gq, d_gkv = _mla_prep_bwd(p0, gq, gkv, wq, wk, wv, tabs, dqf, dkf, dv, "mla_prep_bwd")
    flat = lambda a: a.reshape(t, -1).astype(BF16)
    dproj0 = jnp.concatenate([dg0, flat(dq_sb), flat(dk_sb), flat(dv_sb), flat(dcq), flat(dckv), flat(dkr)], axis=1)
    d_w0 = _mm_tn(hn0, dproj0, "d_w0")
    dhn0 = _mm_nt(dproj0, w0, "d_hn0")
    dh0, d_g0 = _rms_bwd(h0, g0, dhn0, dh1, "norm0_bwd")
    dh0 = dh0.reshape(b, lp, d)

    d_ev_w_in, d_uq, d_ukv, d_od_w_in = _original_grads(d_w0, d_wq, d_wk, d_wv, d_w1)
    grads = dict(meta=_meta_rows_sum(dh0), norm_g=jnp.concatenate([d_g0, d_g1], axis=0), final_g=d_final_g.reshape(d),
                 ev_w_in=d_ev_w_in, ev_q_norm_g=d_gq, ev_kv_norm_g=d_gkv, ev_w_uq=d_uq, ev_w_ukv=d_ukv, ev_w_out=d_wo0,
                 od_w_in=d_od_w_in, od_sinks=d_sinks[:, 0].reshape(1, SWA_HEADS), od_w_out=d_wo1)
    return loss, dh0[:, BLOCK:], grads


MESH = pl.DeviceIdType.MESH
ANY = pl.BlockSpec(memory_space=pl.ANY)


def _place():
    return lax.axis_index("x"), lax.axis_index("y"), lax.axis_index("c")


def _other_chips(x, y):
    return [(1 - x, y), (x, 1 - y), (1 - x, 1 - y)]


PACK_ROWS = 1920
HALF_ROWS = PACK_ROWS // 2


def _with_own_slot(slots, own):
    me = 2 * lax.axis_index("x") + lax.axis_index("y")
    return lax.dynamic_update_slice(slots, own[None], (me,) + (0,) * own.ndim)


def _gather_weights(wpack, meta, name):
    def body(w_ref, m_ref, wo_ref, mo_ref, send_sems, recv_sems):
        x, y, c = _place()
        me, sib = 2 * x + y, (x, y, 1 - c)
        chips = _other_chips(x, y)
        mine = w_ref.at[pl.ds(c * HALF_ROWS, HALF_ROWS), :]
        half = lambda chip, h: wo_ref.at[chip, pl.ds(h * HALF_ROWS, HALF_ROWS), :]

        def copy(k, src, dst, to):
            return pltpu.make_async_remote_copy(src_ref=src, dst_ref=dst, send_sem=send_sems.at[k], recv_sem=recv_sems.at[k], device_id=to,
                                                device_id_type=MESH)

        sent = [copy(k, mine, half(me, c), (px, py, c)) for k, (px, py) in enumerate(chips)]
        sent += [copy(6 + k, m_ref, mo_ref.at[me], (px, py, c)) for k, (px, py) in enumerate(chips)]
        for cp in sent:
            cp.start()
        for k, (px, py) in enumerate(chips):
            landed = half(2 * px + py, c)
            copy(k, mine, landed, (px, py, c)).wait_recv()
            fwd = copy(3 + k, landed, landed, sib)
            fwd.start()
            sent.append(fwd)
        for k, (px, py) in enumerate(chips):
            other = half(2 * px + py, 1 - c)
            copy(3 + k, other, other, sib).wait_recv()
            copy(6 + k, m_ref, mo_ref.at[2 * px + py], (px, py, c)).wait_recv()
        for cp in sent:
            cp.wait_send()

    wall, meta_all = pl.pallas_call(
        body, in_specs=[ANY, ANY], out_specs=[ANY, ANY],
        out_shape=[jax.ShapeDtypeStruct((N_CHIPS,) + wpack.shape, wpack.dtype), jax.ShapeDtypeStruct((N_CHIPS,) + meta.shape, meta.dtype)],
        scratch_shapes=[pltpu.SemaphoreType.DMA((9,)), pltpu.SemaphoreType.DMA((9,))],
        name=name,
    )(wpack, meta)
    return _with_own_slot(wall, wpack), _with_own_slot(meta_all, meta)


def _grads_to_sibling(g, name):
    def body(g_ref, o_ref, send_sem, recv_sem):
        x, y, c = _place()
        cp = pltpu.make_async_remote_copy(src_ref=g_ref.at[:, pl.ds((1 - c) * HALF_ROWS, HALF_ROWS), :], dst_ref=o_ref, send_sem=send_sem,
                                          recv_sem=recv_sem, device_id=(x, y, 1 - c), device_id_type=MESH)
        cp.start()
        cp.wait()

    return pl.pallas_call(
        body, in_specs=[ANY], out_specs=ANY, out_shape=jax.ShapeDtypeStruct((g.shape[0], HALF_ROWS, g.shape[2]), g.dtype),
        scratch_shapes=[pltpu.SemaphoreType.DMA(()), pltpu.SemaphoreType.DMA(())],
        name=name,
    )(g)


def _share_halves(r, name):
    def body(r_ref, o_ref, send_sem, recv_sem):
        x, y, c = _place()
        cp = pltpu.make_async_remote_copy(src_ref=r_ref, dst_ref=o_ref, send_sem=send_sem, recv_sem=recv_sem, device_id=(x, y, 1 - c),
                                          device_id_type=MESH)
        cp.start()
        cp.wait()

    theirs = pl.pallas_call(
        body, in_specs=[ANY], out_specs=ANY, out_shape=jax.ShapeDtypeStruct(r.shape, r.dtype),
        scratch_shapes=[pltpu.SemaphoreType.DMA(()), pltpu.SemaphoreType.DMA(())],
        name=name,
    )(r)
    return jnp.where(lax.axis_index("c") == 0, jnp.concatenate([r, theirs], axis=0), jnp.concatenate([theirs, r], axis=0))


def _chip_scatter(s, name):
    def body(s_ref, r_ref, send_sems, recv_sems):
        x, y, c = _place()
        me = 2 * x + y
        for k, (px, py) in enumerate(_other_chips(x, y)):
            pltpu.make_async_remote_copy(src_ref=s_ref.at[2 * px + py], dst_ref=r_ref.at[me], send_sem=send_sems.at[k],
                                         recv_sem=recv_sems.at[k], device_id=(px, py, c), device_id_type=MESH).start()
        for k, (px, py) in enumerate(_other_chips(x, y)):
            cp = pltpu.make_async_remote_copy(src_ref=s_ref.at[2 * px + py], dst_ref=r_ref.at[2 * px + py], send_sem=send_sems.at[k],
                                              recv_sem=recv_sems.at[k], device_id=(px, py, c), device_id_type=MESH)
            cp.wait_recv()
            cp.wait_send()

    parts = pl.pallas_call(
        body, in_specs=[ANY], out_specs=ANY, out_shape=jax.ShapeDtypeStruct(s.shape, s.dtype),
        scratch_shapes=[pltpu.SemaphoreType.DMA((3,)), pltpu.SemaphoreType.DMA((3,))],
        name=name,
    )(s)
    me = 2 * lax.axis_index("x") + lax.axis_index("y")
    return _with_own_slot(parts, lax.dynamic_index_in_dim(s, me, axis=0, keepdims=False))


def _all_reduce_small(v, name):
    shape = v.shape

    def body(v_ref, o_ref, slots, send_sems, recv_sems):
        x, y, c = _place()
        me = 4 * x + 2 * y + c
        slots[me] = v_ref[...]
        for r in range(1, N_DEV):
            peer = (x ^ (r >> 2), y ^ ((r >> 1) & 1), c ^ (r & 1))
            pltpu.make_async_remote_copy(src_ref=v_ref, dst_ref=slots.at[me], send_sem=send_sems.at[r - 1], recv_sem=recv_sems.at[r - 1],
                                         device_id=peer, device_id_type=MESH).start()
        for r in range(1, N_DEV):
            peer = (x ^ (r >> 2), y ^ ((r >> 1) & 1), c ^ (r & 1))
            cp = pltpu.make_async_remote_copy(src_ref=v_ref, dst_ref=slots.at[4 * peer[0] + 2 * peer[1] + peer[2]], send_sem=send_sems.at[r - 1],
                                              recv_sem=recv_sems.at[r - 1], device_id=peer, device_id_type=MESH)
            cp.wait_recv()
            cp.wait_send()
        acc = slots[0]
        for d in range(1, N_DEV):
            acc = acc + slots[d]
        o_ref[...] = acc

    vm = pl.BlockSpec(memory_space=pltpu.VMEM)
    return pl.pallas_call(
        body, in_specs=[vm], out_specs=vm, out_shape=jax.ShapeDtypeStruct(shape, F32),
        scratch_shapes=[pltpu.VMEM((N_DEV,) + shape, F32), pltpu.SemaphoreType.DMA((N_DEV - 1,)), pltpu.SemaphoreType.DMA((N_DEV - 1,))],
        name=name,
    )(v)


SUM_ROWS = 480


def _add_sibling(g, gsib, core, name):
    n, _, cdim = g.shape
    per_half = HALF_ROWS // SUM_ROWS

    def body(core_ref, a_ref, b_ref, o_ref):
        o_ref[...] = (a_ref[...] + b_ref[...]).astype(o_ref.dtype)

    blk = pl.BlockSpec((1, SUM_ROWS, cdim), lambda j, i, core_ref: (j, i, 0))
    return pl.pallas_call(
        body,
        grid_spec=pltpu.PrefetchScalarGridSpec(
            num_scalar_prefetch=1, grid=(n, per_half),
            in_specs=[pl.BlockSpec((1, SUM_ROWS, cdim), lambda j, i, core_ref: (j, core_ref[0] * per_half + i, 0)), blk], out_specs=blk),
        out_shape=jax.ShapeDtypeStruct(gsib.shape, BF16), compiler_params=_params(("parallel", "parallel")), name=name,
    )(core, g, gsib)


def _sum_parts(parts, name):
    n, r, cdim = parts.shape

    def body(p_ref, o_ref):
        acc = p_ref[0].astype(F32)
        for j in range(1, n):
            acc = acc + p_ref[j].astype(F32)
        o_ref[...] = acc

    return pl.pallas_call(
        body, grid=(r // SUM_ROWS,), in_specs=[pl.BlockSpec((n, SUM_ROWS, cdim), lambda i: (0, i, 0))],
        out_specs=pl.BlockSpec((SUM_ROWS, cdim), lambda i: (i, 0)), out_shape=jax.ShapeDtypeStruct((r, cdim), F32),
        compiler_params=_params(("parallel",)), name=name,
    )(parts)


def _adamw(parts, w, m, v, name):
    npart, r, cdim = parts.shape
    tr = 384 if r % 384 == 0 else r

    def body(p_ref, w_ref, m_ref, v_ref, g_ref, d_ref, nm_ref, nv_ref):
        g = p_ref[0]
        for j in range(1, npart):
            g = g + p_ref[j]
        m_new = ADAM_B1 * m_ref[...] + (1.0 - ADAM_B1) * g
        v_new = ADAM_B2 * v_ref[...] + (1.0 - ADAM_B2) * (g * g)
        m_hat = m_new / (1.0 - ADAM_B1 ** ADAM_STEP)
        v_hat = v_new / (1.0 - ADAM_B2 ** ADAM_STEP)
        g_ref[...] = g
        d_ref[...] = -ADAM_LR * (m_hat / (jnp.sqrt(v_hat) + ADAM_EPS) + ADAM_WD * w_ref[...])
        nm_ref[...] = m_new
        nv_ref[...] = v_new

    blk = pl.BlockSpec((tr, cdim), lambda i: (i, 0))
    shp = jax.ShapeDtypeStruct((r, cdim), F32)
    return pl.pallas_call(
        body, grid=(r // tr,), in_specs=[pl.BlockSpec((npart, tr, cdim), lambda i: (0, i, 0)), blk, blk, blk], out_specs=[blk] * 4,
        out_shape=[shp] * 4, compiler_params=_params(("parallel",)), name=name,
    )(parts, w, m, v)


BIG = ("ev_w_in", "ev_w_uq", "ev_w_ukv", "ev_w_out", "od_w_in", "od_w_out", "meta")
SMALL = ("norm_g", "final_g", "ev_q_norm_g", "ev_kv_norm_g", "od_sinks")
SMALL_SHAPE = (8, 512)


def _pack_big(arrs):
    rows = [a.reshape(-1, D_MODEL) for a in arrs]
    used = sum(r.shape[0] for r in rows)
    return jnp.pad(jnp.concatenate(rows, axis=0), ((0, PACK_ROWS - used), (0, 0)))


def _unpack_big(p, shapes):
    out, at = [], 0
    for s in shapes:
        nrow = int(np.prod(s)) // D_MODEL
        out.append(p[at:at + nrow].reshape(s))
        at += nrow
    return out


def _pack_small(arrs, extra=None):
    flat = [a.reshape(-1) for a in arrs] + ([] if extra is None else [extra.reshape(-1)])
    used = sum(f.shape[0] for f in flat)
    return jnp.pad(jnp.concatenate(flat), (0, SMALL_SHAPE[0] * SMALL_SHAPE[1] - used)).reshape(SMALL_SHAPE)


def _unpack_small(p, shapes):
    flat, out, at = p.reshape(-1), [], 0
    for s in shapes:
        n = int(np.prod(s))
        out.append(flat[at:at + n].reshape(s))
        at += n
    return out, flat[at]


def _shard_of(full, name, j):
    if name in ("ev_w_out", "od_w_out"):
        n = full.shape[0] // N_CHIPS
        return full[j * n:(j + 1) * n]
    n = full.shape[1] // N_CHIPS
    return full[:, j * n:(j + 1) * n]


def kernel(x, meta, norm_g, final_g, ev_w_in, ev_q_norm_g, ev_kv_norm_g, ev_w_uq, ev_w_ukv, ev_w_out, od_w_in, od_sinks, od_w_out, loss_target, m_meta, m_norm_g, m_final_g, m_ev_w_in, m_ev_q_norm_g, m_ev_kv_norm_g, m_ev_w_uq, m_ev_w_ukv, m_ev_w_out, m_od_w_in, m_od_sinks, m_od_w_out, v_meta, v_norm_g, v_final_g, v_ev_w_in, v_ev_q_norm_g, v_ev_kv_norm_g, v_ev_w_uq, v_ev_w_ukv, v_ev_w_out, v_od_w_in, v_od_sinks, v_od_w_out):
    given = dict(locals())
    big_w = [given[n][0] if given[n].ndim == 3 else given[n] for n in BIG]
    big_shapes = [given[n].shape for n in BIG]

    wpack = _pack_big(big_w[:-1]).astype(BF16)
    wall, meta_all = _gather_weights(wpack, meta, "gather_weights")
    per_chip = [_unpack_big(wall[j], [a.shape for a in big_w[:-1]]) for j in range(N_CHIPS)]
    cat = lambda i, axis: jnp.concatenate([per_chip[j][i] for j in range(N_CHIPS)], axis=axis)
    full = dict(ev_w_in=cat(0, 1), ev_w_uq=cat(1, 1), ev_w_ukv=cat(2, 1), ev_w_out=cat(3, 0), od_w_in=cat(4, 1), od_w_out=cat(5, 0))
    meta_full = jnp.concatenate([meta_all[j] for j in range(N_CHIPS)], axis=1)

    loss, grad_x, grads = _local_step(x, meta_full, norm_g, final_g, ev_q_norm_g, ev_kv_norm_g, od_sinks, loss_target,
                                      full["ev_w_in"], full["ev_w_uq"], full["ev_w_ukv"], full["ev_w_out"], full["od_w_in"], full["od_w_out"])

    gpack = jnp.stack([_pack_big([_shard_of(grads[n], n, j) for n in BIG]) for j in range(N_CHIPS)])
    core = lax.axis_index("c").astype(jnp.int32).reshape(1)
    gsum = _add_sibling(gpack, _grads_to_sibling(gpack, "grads_to_sibling"), core, "add_sibling")
    reduced = _share_halves(_sum_parts(_chip_scatter(gsum, "grads_to_chips"), "add_chips"), "reduced_to_sibling")
    big_out = _adamw(reduced[None], _pack_big(big_w), _pack_big([given["m_" + n] for n in BIG]), _pack_big([given["v_" + n] for n in BIG]),
                     "adamw_matrices")
    big_out = [_unpack_big(o, big_shapes) for o in big_out]

    small_shapes = [given[n].shape for n in SMALL]
    ssum = _all_reduce_small(_pack_small([grads[n] for n in SMALL], loss[0, 0]), "reduce_vectors")
    small_out = _adamw(ssum[None], _pack_small([given[n] for n in SMALL]), _pack_small([given["m_" + n] for n in SMALL]),
                       _pack_small([given["v_" + n] for n in SMALL]), "adamw_vectors")
    total_loss = ssum.reshape(-1)[sum(int(np.prod(s)) for s in small_shapes)]
    small_out = [_unpack_small(o, small_shapes)[0] for o in small_out]

    names = ("meta", "norm_g", "final_g", "ev_w_in", "ev_q_norm_g", "ev_kv_norm_g", "ev_w_uq", "ev_w_ukv", "ev_w_out", "od_w_in", "od_sinks",
             "od_w_out")
    outs = [total_loss, grad_x]
    for kind in range(4):
        for n in names:
            outs.append(big_out[kind][BIG.index(n)] if n in BIG else small_out[kind][SMALL.index(n)])
    return tuple(outs)
```

```python
import functools
import math

import numpy as np
import jax
import jax.numpy as jnp
from jax import lax
from jax.experimental import pallas as pl
from jax.experimental.pallas import tpu as pltpu

F32 = jnp.float32
BF16 = jnp.bfloat16

D_MODEL = 1024
BLOCK = 128
N_META = 16
N_PAD = BLOCK - N_META
NORM_EPS = 1e-6
NEG = -1e30
HEAD = 64
SB_HEADS = 8
MLA_HEADS = 8
MLA_Q_LORA = 256
MLA_KV_LORA = 128
MLA_NOPE = 64
MLA_ROPE = 32
ROPE_BASE = 10000.0
SWA_HEADS = 16
SWA_KV_HEADS = 2
SWA_WINDOW = 128
N_CHIPS = 4
N_DEV = 8

ADAM_LR = 0.001
ADAM_B1 = 0.9
ADAM_B2 = 0.999
ADAM_EPS = 1e-08
ADAM_WD = 0.01
ADAM_STEP = 10

VMEM_LIMIT = 48 * 1024 * 1024

EV_G, EV_Q, EV_K, EV_V, EV_CQ, EV_CKV, EV_KR, EV_N = 0, 1024, 1536, 2048, 2560, 2816, 2944, 3072
OD_G, OD_Q, OD_K, OD_V, OD_N = 0, 1024, 2048, 2176, 2304


def _params(sem=None):
    return pltpu.CompilerParams(dimension_semantics=sem, vmem_limit_bytes=VMEM_LIMIT)


def _row_tile(m):
    return 256 if m % 256 == 0 else 128


def _matmul_rows(m):
    for c in (1088, 1024, 768, 640, 512, 384, 256):
        if m % c == 0:
            return c
    return 128


def _dot(a, b):
    return jnp.dot(a.astype(BF16), b.astype(BF16), preferred_element_type=F32)


def _dot_nt(a, b):
    return lax.dot_general(a.astype(BF16), b.astype(BF16), (((1,), (1,)), ((), ())), preferred_element_type=F32)


def _dot_tn(a, b):
    return lax.dot_general(a.astype(BF16), b.astype(BF16), (((0,), (0,)), ((), ())), preferred_element_type=F32)


def _rms_fwd(h, g, name):
    t, d = h.shape
    tm = _row_tile(t)

    def body(h_ref, g_ref, o_ref):
        x = h_ref[...]
        r = lax.rsqrt(jnp.mean(x * x, axis=-1, keepdims=True) + NORM_EPS)
        o_ref[...] = ((x * r) * g_ref[...]).astype(o_ref.dtype)

    return pl.pallas_call(
        body, grid=(t // tm,),
        in_specs=[pl.BlockSpec((tm, d), lambda i: (i, 0)), pl.BlockSpec((1, d), lambda i: (0, 0))],
        out_specs=pl.BlockSpec((tm, d), lambda i: (i, 0)),
        out_shape=jax.ShapeDtypeStruct((t, d), BF16), compiler_params=_params(("parallel",)), name=name,
    )(h, g)


def _rms_bwd(h, g, dy, dres, name):
    t, d = h.shape
    tm = _row_tile(t)

    def body(h_ref, g_ref, dy_ref, dres_ref, dh_ref, dg_ref):
        @pl.when(pl.program_id(0) == 0)
        def _():
            dg_ref[...] = jnp.zeros_like(dg_ref)

        x = h_ref[...]
        r = lax.rsqrt(jnp.mean(x * x, axis=-1, keepdims=True) + NORM_EPS)
        xr = x * r
        dy_ = dy_ref[...]
        u = dy_ * g_ref[...]
        dh_ref[...] = dres_ref[...] + r * (u - xr * jnp.mean(u * xr, axis=-1, keepdims=True))
        dg_ref[...] += jnp.sum(dy_ * xr, axis=0, keepdims=True)

    row = pl.BlockSpec((tm, d), lambda i: (i, 0))
    vec = pl.BlockSpec((1, d), lambda i: (0, 0))
    return pl.pallas_call(
        body, grid=(t // tm,), in_specs=[row, vec, row, row], out_specs=[row, vec],
        out_shape=[jax.ShapeDtypeStruct((t, d), F32), jax.ShapeDtypeStruct((1, d), F32)],
        compiler_params=_params(("arbitrary",)), name=name,
    )(h, g, dy, dres)


def _col_tile(n):
    for c in (1024, 768, 640, 512, 384, 256, 128):
        if n % c == 0:
            return c
    return n


def _mm(a, w, name, res=None, out_dtype=F32, a_cols=None):
    m = a.shape[0]
    k, n = w.shape
    a_blk = 0 if a_cols is None else a_cols[0] // k
    assert a_cols is None or (a_cols[1] == k and a_cols[0] % k == 0)
    tm, tn = _matmul_rows(m), _col_tile(n)

    def body(*refs):
        if res is None:
            a_ref, w_ref, o_ref = refs
            acc = _dot(a_ref[...], w_ref[...])
        else:
            a_ref, w_ref, r_ref, o_ref = refs
            acc = r_ref[...] + _dot(a_ref[...], w_ref[...])
        o_ref[...] = acc.astype(o_ref.dtype)

    in_specs = [pl.BlockSpec((tm, k), lambda j, i: (i, a_blk)), pl.BlockSpec((k, tn), lambda j, i: (0, j))]
    args = [a, w]
    if res is not None:
        in_specs.append(pl.BlockSpec((tm, tn), lambda j, i: (i, j)))
        args.append(res)
    return pl.pallas_call(
        body, grid=(n // tn, m // tm), in_specs=in_specs, out_specs=pl.BlockSpec((tm, tn), lambda j, i: (i, j)),
        out_shape=jax.ShapeDtypeStruct((m, n), out_dtype), compiler_params=_params(("parallel", "parallel")), name=name,
    )(*args)


def _mm_nt(a, w, name):
    m, n = a.shape
    k = w.shape[0]
    tm, tk = _matmul_rows(m), _col_tile(k)

    def body(a_ref, w_ref, o_ref):
        o_ref[...] = _dot_nt(a_ref[...], w_ref[...])

    return pl.pallas_call(
        body, grid=(k // tk, m // tm),
        in_specs=[pl.BlockSpec((tm, n), lambda j, i: (i, 0)), pl.BlockSpec((tk, n), lambda j, i: (j, 0))],
        out_specs=pl.BlockSpec((tm, tk), lambda j, i: (i, j)),
        out_shape=jax.ShapeDtypeStruct((m, k), F32), compiler_params=_params(("parallel", "parallel")), name=name,
    )(a, w)


def _mm_tn(x, dy, name):
    m, k = x.shape
    n = dy.shape[1]
    tm, tn = _matmul_rows(m), _col_tile(n)

    def body(x_ref, dy_ref, o_ref):
        @pl.when(pl.program_id(1) == 0)
        def _():
            o_ref[...] = jnp.zeros_like(o_ref)

        o_ref[...] += _dot_tn(x_ref[...], dy_ref[...])

    return pl.pallas_call(
        body, grid=(n // tn, m // tm),
        in_specs=[pl.BlockSpec((tm, k), lambda j, i: (i, 0)), pl.BlockSpec((tm, tn), lambda j, i: (i, j))],
        out_specs=pl.BlockSpec((k, tn), lambda j, i: (0, j)),
        out_shape=jax.ShapeDtypeStruct((k, n), F32), compiler_params=_params(("parallel", "arbitrary")), name=name,
    )(x, dy)


def _silu_parts(g):
    s = 1.0 / (1.0 + jnp.exp(-g))
    return g * s, s * (1.0 + g * (1.0 - s))


def _gate_fwd(o_parts, proj, name):
    t = proj.shape[0]
    tm = _row_tile(t)
    w = D_MODEL // len(o_parts)

    def body(*refs):
        g_ref, o_ref = refs[-2], refs[-1]
        for p, r in enumerate(refs[:-2]):
            sil, _ = _silu_parts(g_ref[:, p * w:(p + 1) * w])
            o_ref[:, p * w:(p + 1) * w] = (r[...].astype(F32) * sil).astype(o_ref.dtype)

    return pl.pallas_call(
        body, grid=(t // tm,),
        in_specs=[pl.BlockSpec((tm, w), lambda i: (i, 0)) for _ in o_parts] + [pl.BlockSpec((tm, D_MODEL), lambda i: (i, 0))],
        out_specs=pl.BlockSpec((tm, D_MODEL), lambda i: (i, 0)),
        out_shape=jax.ShapeDtypeStruct((t, D_MODEL), BF16), compiler_params=_params(("parallel",)), name=name,
    )(*o_parts, proj)


def _gate_bwd(dao, o_parts, proj, name):
    t = proj.shape[0]
    tm = _row_tile(t)
    np_ = len(o_parts)
    w = D_MODEL // np_

    def body(*refs):
        dao_ref, g_ref = refs[0], refs[1 + np_]
        do_refs, dg_ref = refs[2 + np_:2 + 2 * np_], refs[-1]
        for p in range(np_):
            sl = slice(p * w, (p + 1) * w)
            sil, dsil = _silu_parts(g_ref[:, sl])
            da = dao_ref[:, sl]
            do_refs[p][...] = da * sil
            dg_ref[:, sl] = (da * refs[1 + p][...].astype(F32) * dsil).astype(dg_ref.dtype)

    full = pl.BlockSpec((tm, D_MODEL), lambda i: (i, 0))
    part = pl.BlockSpec((tm, w), lambda i: (i, 0))
    outs = pl.pallas_call(
        body, grid=(t // tm,), in_specs=[full] + [part] * np_ + [full], out_specs=[part] * np_ + [full],
        out_shape=[jax.ShapeDtypeStruct((t, w), F32)] * np_ + [jax.ShapeDtypeStruct((t, D_MODEL), BF16)],
        compiler_params=_params(("parallel",)), name=name,
    )(dao, *o_parts, proj)
    return outs[:np_], outs[np_]


def _loss_head(h2, gf, target, b, lp):
    d = h2.shape[1]
    nb = lp // BLOCK
    h3 = h2.reshape(b, lp, d)

    def body(h_ref, g_ref, t_ref, dh_ref, dg_ref, loss_ref):
        first = (pl.program_id(0) == 0) & (pl.program_id(1) == 0)

        @pl.when(first)
        def _():
            dg_ref[...] = jnp.zeros_like(dg_ref)
            loss_ref[...] = jnp.zeros_like(loss_ref)

        @pl.when(pl.program_id(1) == 0)
        def _():
            dh_ref[...] = jnp.zeros_like(dh_ref)

        @pl.when(pl.program_id(1) > 0)
        def _():
            x = h_ref[0]
            r = lax.rsqrt(jnp.mean(x * x, axis=-1, keepdims=True) + NORM_EPS)
            xr = x * r
            g = g_ref[...]
            diff = xr * g - t_ref[0]
            loss_ref[...] += 0.5 * jnp.sum(jnp.mean(diff * diff, axis=-1, keepdims=True))
            dy = diff * (1.0 / d)
            u = dy * g
            dh_ref[0] = r * (u - xr * jnp.mean(u * xr, axis=-1, keepdims=True))
            dg_ref[...] += jnp.sum(dy * xr, axis=0, keepdims=True)

    blk = pl.BlockSpec((1, BLOCK, d), lambda bi, n: (bi, n, 0))
    dh, dg, loss = pl.pallas_call(
        body, grid=(b, nb),
        in_specs=[blk, pl.BlockSpec((1, d), lambda bi, n: (0, 0)),
                  pl.BlockSpec((1, BLOCK, d), lambda bi, n: (bi, jnp.maximum(n - 1, 0), 0))],
        out_specs=[blk, pl.BlockSpec((1, d), lambda bi, n: (0, 0)), pl.BlockSpec((8, 128), lambda bi, n: (0, 0))],
        out_shape=[jax.ShapeDtypeStruct((b, lp, d), F32), jax.ShapeDtypeStruct((1, d), F32), jax.ShapeDtypeStruct((8, 128), F32)],
        compiler_params=_params(("arbitrary", "arbitrary")), name="loss_head",
    )(h3, gf, target)
    return dh.reshape(b * lp, d), dg, loss


def _iota2(shape, dim):
    return lax.broadcasted_iota(jnp.int32, shape, dim)


KEYS = 256


def _lo_lanes():
    return _iota2((1, BLOCK), 1) < HEAD


def _halves(x, lo):
    zero = jnp.zeros_like(x)
    return jnp.where(lo, x, zero), jnp.where(lo, zero, x)


def _rows_of_pair(a, b):
    return jnp.where(_iota2((BLOCK, 1), 0) < HEAD, a, b)


def _split_rows_t(x):
    xt = x.T
    first = _iota2(xt.shape, 0) < HEAD
    zero = jnp.zeros_like(xt)
    return jnp.concatenate([jnp.where(first, xt, zero), jnp.where(first, zero, xt)], axis=1).astype(BF16)


def _key_chunk(c, lp, t_idx, strict, key_axis):
    first = c * KEYS
    s0 = pl.multiple_of(jnp.minimum(first, lp - KEYS), BLOCK)
    s_idx = s0 + _iota2(t_idx.shape, key_axis)
    seen = (s_idx < t_idx) if strict else (s_idx <= t_idx)
    return s0, seen & (s_idx >= jnp.maximum(first, N_PAD))


def _split_dot(x, tri):
    hi = x.astype(BF16)
    lo = (x - hi.astype(F32)).astype(BF16)
    return jnp.dot(hi, tri, preferred_element_type=F32) + jnp.dot(lo, tri, preferred_element_type=F32)


def _stack_halves(x, lo):
    a, b = _halves(x, lo)
    return jnp.concatenate([a, b], axis=0)


def _pair(a, b, lo):
    return jnp.where(lo, a, b)


def _chunk_starts(lp):
    return [min(c * KEYS, lp - KEYS) for c in range(-(-lp // KEYS))]


def _put_rows(ref, r0, bq, a, b):
    for t in range(bq // BLOCK):
        part = slice(t * BLOCK, (t + 1) * BLOCK)
        ref[0, 0, r0 // BLOCK + t] = jnp.concatenate([a[:, part], b[:, part], jnp.zeros((6, BLOCK), F32)], axis=0)


def _get_rows(ref, r0, bq):
    return [jnp.concatenate([ref[0, 0, r0 // BLOCK + t, h:h + 1, :] for t in range(bq // BLOCK)], axis=1) for h in range(2)]


def _n_chunks(i):
    return (i + 2) // 2


QROWS = 512


def _for_query_tiles(nb, tile):
    per = QROWS // BLOCK

    def step(j, _):
        tile(pl.multiple_of(j * QROWS, QROWS), QROWS, (j + 1) * (QROWS // KEYS))
        return 0

    lax.fori_loop(0, nb // per, step, 0)
    for i in range(nb - nb % per, nb):
        tile(i * BLOCK, BLOCK, _n_chunks(i))


def _sb_scores(q_h, k, valid, after):
    z = _dot_nt(q_h, k) * (HEAD ** -0.5)
    sp = jnp.log(1.0 + jnp.exp(-jnp.abs(z)))
    lb = jnp.minimum(z, 0.0) - sp
    l1m_all = -jnp.maximum(z, 0.0) - sp
    l1m = jnp.where(valid, l1m_all, 0.0)
    return lb, l1m_all, l1m, _split_dot(l1m, after)


def _pair_stat_spec(nb):
    return pl.BlockSpec((1, 1, nb, 8, BLOCK), lambda bi, hp: (bi, hp, 0, 0, 0))


def _sb_fwd(proj3, name):
    b, lp, _ = proj3.shape
    nb = lp // BLOCK
    npair = SB_HEADS // 2

    def body(q_ref, k_ref, v_ref, o_ref, tot_ref):
        lo = _lo_lanes()
        after = (_iota2((KEYS, KEYS), 0) > _iota2((KEYS, KEYS), 1)).astype(BF16)

        def qtile(r0, bq, n):
            qs = _halves(q_ref[0, pl.ds(r0, bq), :].astype(BF16), lo)
            t_idx = r0 + _iota2((bq, KEYS), 0)

            def kchunk(cc, carry):
                cs, acc = carry[:2], carry[2]
                s0, valid = _key_chunk(n - 1 - cc, lp, t_idx, True, 1)
                k = k_ref[0, pl.ds(s0, KEYS), :].astype(BF16)
                a_s, new = [], []
                for h in range(2):
                    lb, _, l1m, suf = _sb_scores(qs[h], k, valid, after)
                    a_s.append(jnp.where(valid, jnp.exp(lb + suf + cs[h]), 0.0).astype(BF16))
                    new.append(cs[h] + jnp.sum(l1m, axis=1, keepdims=True))
                v_bd = _stack_halves(v_ref[0, pl.ds(s0, KEYS), :].astype(BF16), lo)
                return (*new, acc + jnp.dot(jnp.concatenate(a_s, axis=1), v_bd, preferred_element_type=F32))

            zero = jnp.zeros((bq, 1), F32)
            c_a, c_b, acc = lax.fori_loop(0, n, kchunk, (zero, zero, jnp.zeros((bq, BLOCK), F32)))
            o_ref[0, pl.ds(r0, bq), :] = acc
            tot_ref[0, pl.ds(r0, bq), :] = jnp.broadcast_to(_pair(c_a, c_b, lo), (bq, BLOCK))

        _for_query_tiles(nb, qtile)

    def col(first):
        return pl.BlockSpec((1, lp, 2 * HEAD), lambda bi, hp: (bi, 0, first // (2 * HEAD) + hp))

    shp = jax.ShapeDtypeStruct((b, lp, SB_HEADS * HEAD), F32)
    return pl.pallas_call(
        body, grid=(b, npair), in_specs=[col(EV_Q), col(EV_K), col(EV_V)], out_specs=[col(0), col(0)], out_shape=[shp, shp],
        compiler_params=_params(("parallel", "parallel")), name=name,
    )(proj3, proj3, proj3)


def _sb_bwd(proj3, tot, do, name):
    b, lp, _ = proj3.shape
    nb = lp // BLOCK
    npair = SB_HEADS // 2

    def body(q_ref, k_ref, v_ref, tot_ref, do_ref, dq_ref, dk_ref, dv_ref):
        lo = _lo_lanes()
        after = (_iota2((KEYS, KEYS), 0) > _iota2((KEYS, KEYS), 1)).astype(BF16)
        before = (_iota2((KEYS, KEYS), 0) < _iota2((KEYS, KEYS), 1)).astype(BF16)
        dk_ref[...] = jnp.zeros_like(dk_ref)
        dv_ref[...] = jnp.zeros_like(dv_ref)

        def qtile(r0, bq, n):
            rows = pl.ds(r0, bq)
            qs = _halves(q_ref[0, rows, :].astype(BF16), lo)
            dos = _halves(do_ref[0, rows, :].astype(BF16), lo)
            tot_i = tot_ref[0, rows, :]
            tots = (tot_i[:, 0:1], tot_i[:, HEAD:HEAD + 1])
            q_st, do_st = jnp.concatenate(qs, axis=0), jnp.concatenate(dos, axis=0)
            t_idx = r0 + _iota2((bq, KEYS), 0)

            def kchunk(c, carry):
                s0, valid = _key_chunk(c, lp, t_idx, True, 1)
                keys = pl.ds(s0, KEYS)
                k = k_ref[0, keys, :].astype(BF16)
                v = v_ref[0, keys, :].astype(BF16)
                a_s, dzs, new = [], [], []
                for h in range(2):
                    left, pre = carry[2 * h], carry[2 * h + 1]
                    lb, l1m_all, l1m, suf = _sb_scores(qs[h], k, valid, after)
                    here = jnp.sum(l1m, axis=1, keepdims=True)
                    a = jnp.where(valid, jnp.exp(lb + suf + (tots[h] - left - here)), 0.0)
                    w = a * _dot_nt(dos[h], v)
                    dz = jnp.where(valid, w * jnp.exp(l1m_all) - (pre + _split_dot(w, before)) * jnp.exp(lb), 0.0) * (HEAD ** -0.5)
                    new += [left + here, pre + jnp.sum(w, axis=1, keepdims=True)]
                    a_s.append(a.astype(BF16))
                    dzs.append(dz.astype(BF16))
                dk_ref[0, keys, :] += _dot_tn(jnp.concatenate(dzs, axis=0), q_st)
                dv_ref[0, keys, :] += _dot_tn(jnp.concatenate(a_s, axis=0), do_st)
                dq = carry[4] + jnp.dot(jnp.concatenate(dzs, axis=1), _stack_halves(k, lo), preferred_element_type=F32)
                return (*new, dq)

            zero = jnp.zeros((bq, 1), F32)
            out = lax.fori_loop(0, n, kchunk, (zero, zero, zero, zero, jnp.zeros((bq, BLOCK), F32)))
            dq_ref[0, rows, :] = out[4]

        _for_query_tiles(nb, qtile)

    def col(first):
        return pl.BlockSpec((1, lp, 2 * HEAD), lambda bi, hp: (bi, 0, first // (2 * HEAD) + hp))

    shp = jax.ShapeDtypeStruct((b, lp, SB_HEADS * HEAD), F32)
    return pl.pallas_call(
        body, grid=(b, npair), in_specs=[col(EV_Q), col(EV_K), col(EV_V), col(0), col(0)], out_specs=[col(0)] * 3, out_shape=[shp] * 3,
        compiler_params=_params(("parallel", "parallel")), name=name,
    )(proj3, proj3, proj3, tot, do)


def _rope_tables(lp):
    half = MLA_ROPE // 2
    pos = (np.arange(lp) - N_PAD).astype(np.float32)
    inv = jnp.asarray(ROPE_BASE, F32) ** (-jnp.arange(half, dtype=F32) / half)
    ang = jnp.asarray(pos)[:, None] * inv[None, :]
    cos, sin = jnp.cos(ang), jnp.sin(ang)
    zeros = lambda n: jnp.zeros((lp, n), F32)
    c = jnp.concatenate([jnp.ones((lp, MLA_NOPE), F32), cos, cos, zeros(32)], axis=1)
    s1 = jnp.concatenate([zeros(MLA_NOPE), -sin, zeros(half), zeros(32)], axis=1)
    s2 = jnp.concatenate([zeros(MLA_NOPE), zeros(half), sin, zeros(32)], axis=1)
    return c, s1, s2


def _rope(x, c, s1, s2):
    half = MLA_ROPE // 2
    return x * c + pltpu.roll(x, BLOCK - half, 1) * s1 + pltpu.roll(x, half, 1) * s2


def _rope_t(dy, c, s1, s2):
    half = MLA_ROPE // 2
    return dy * c + pltpu.roll(dy * s1, half, 1) + pltpu.roll(dy * s2, BLOCK - half, 1)


def _rms_rows(x, g):
    r = lax.rsqrt(jnp.mean(x * x, axis=-1, keepdims=True) + NORM_EPS)
    return x * r, r


def _mla_prep_fwd(proj3, gq, gkv, wq, wk, wv, tabs, name):
    b, lp, _ = proj3.shape
    nb = lp // BLOCK
    hw = MLA_HEADS * BLOCK

    def body(cq_ref, ckv_ref, kr_ref, gq_ref, gkv_ref, wq_ref, wk_ref, wv_ref, c_ref, s1_ref, s2_ref, qf_ref, kf_ref, v_ref):
        c, s1, s2 = c_ref[...], s1_ref[...], s2_ref[...]
        xq, _ = _rms_rows(cq_ref[0], None)
        qh = _dot(xq * gq_ref[...], wq_ref[...])
        xk, _ = _rms_rows(ckv_ref[0], None)
        ckv_n = xk * gkv_ref[...]
        kv = _dot(ckv_n, wk_ref[...])
        v_ref[0] = _dot(ckv_n, wv_ref[...]).astype(v_ref.dtype)
        kr = _rope(kr_ref[0], c, s1, s2)
        for h in range(MLA_HEADS):
            ls = slice(h * BLOCK, (h + 1) * BLOCK)
            qf_ref[0, :, ls] = _rope(qh[:, ls], c, s1, s2).astype(qf_ref.dtype)
            kf_ref[0, :, ls] = (kv[:, ls] + kr).astype(kf_ref.dtype)

    def col(first, width):
        return pl.BlockSpec((1, BLOCK, width), lambda bi, n: (bi, n, first // width))

    def whole(a):
        return pl.BlockSpec(a.shape, lambda bi, n: (0,) * a.ndim)

    tab = pl.BlockSpec((BLOCK, BLOCK), lambda bi, n: (n, 0))
    return pl.pallas_call(
        body, grid=(b, nb),
        in_specs=[col(EV_CQ, MLA_Q_LORA), col(EV_CKV, MLA_KV_LORA), col(EV_KR, BLOCK), whole(gq), whole(gkv), whole(wq), whole(wk),
                  whole(wv), tab, tab, tab],
        out_specs=[col(0, hw), col(0, hw), col(0, MLA_HEADS * HEAD)],
        out_shape=[jax.ShapeDtypeStruct((b, lp, hw), BF16), jax.ShapeDtypeStruct((b, lp, hw), BF16),
                   jax.ShapeDtypeStruct((b, lp, MLA_HEADS * HEAD), BF16)],
        compiler_params=_params(("parallel", "parallel")), name=name,
    )(proj3, proj3, proj3, gq, gkv, wq, wk, wv, *tabs)


def _mla_prep_bwd(proj3, gq, gkv, wq, wk, wv, tabs, dqf, dkf, dv, name):
    b, lp, _ = proj3.shape
    nb = lp // BLOCK
    hw = MLA_HEADS * BLOCK

    def body(cq_ref, ckv_ref, gq_ref, gkv_ref, wq_ref, wk_ref, wv_ref, c_ref, s1_ref, s2_ref, dqf_ref, dkf_ref, dv_ref,
             dcq_ref, dckv_ref, dkr_ref, dwq_ref, dwk_ref, dwv_ref, dgq_ref, dgkv_ref, dqh):
        @pl.when((pl.program_id(0) == 0) & (pl.program_id(1) == 0))
        def _():
            for r in (dwq_ref, dwk_ref, dwv_ref, dgq_ref, dgkv_ref):
                r[...] = jnp.zeros_like(r)

        c, s1, s2 = c_ref[...], s1_ref[...], s2_ref[...]
        dkr = jnp.zeros((BLOCK, BLOCK), F32)
        for h in range(MLA_HEADS):
            ls = slice(h * BLOCK, (h + 1) * BLOCK)
            dqh[:, ls] = _rope_t(dqf_ref[0, :, ls].astype(F32), c, s1, s2).astype(dqh.dtype)
            dkr = dkr + dkf_ref[0, :, ls].astype(F32)
        dkr_ref[0] = _rope_t(dkr, c, s1, s2).astype(dkr_ref.dtype)

        def norm_bwd(x, g, dy, dg_ref):
            xr, r = _rms_rows(x, None)
            u = dy * g
            dg_ref[...] += jnp.sum(dy * xr, axis=0, keepdims=True)
            return r * (u - xr * jnp.mean(u * xr, axis=-1, keepdims=True))

        xq, _ = _rms_rows(cq_ref[0], None)
        cq_n = xq * gq_ref[...]
        dwq_ref[...] += _dot_tn(cq_n, dqh[...])
        dcq_ref[0] = norm_bwd(cq_ref[0], gq_ref[...], _dot_nt(dqh[...], wq_ref[...]), dgq_ref).astype(dcq_ref.dtype)
        xk, _ = _rms_rows(ckv_ref[0], None)
        ckv_n = xk * gkv_ref[...]
        dkf_, dv_ = dkf_ref[0], dv_ref[0]
        dwk_ref[...] += _dot_tn(ckv_n, dkf_)
        dwv_ref[...] += _dot_tn(ckv_n, dv_)
        dckv_n = _dot_nt(dkf_, wk_ref[...]) + _dot_nt(dv_, wv_ref[...])
        dckv_ref[0] = norm_bwd(ckv_ref[0], gkv_ref[...], dckv_n, dgkv_ref).astype(dckv_ref.dtype)

    def col(first, width):
        return pl.BlockSpec((1, BLOCK, width), lambda bi, n: (bi, n, first // width))

    def whole(a):
        return pl.BlockSpec(a.shape, lambda bi, n: (0,) * len(a.shape))

    tab = pl.BlockSpec((BLOCK, BLOCK), lambda bi, n: (n, 0))
    acc_shapes = [jax.ShapeDtypeStruct(a.shape, F32) for a in (wq, wk, wv, gq, gkv)]
    return pl.pallas_call(
        body, grid=(b, nb),
        in_specs=[col(EV_CQ, MLA_Q_LORA), col(EV_CKV, MLA_KV_LORA), whole(gq), whole(gkv), whole(wq), whole(wk), whole(wv), tab, tab, tab,
                  col(0, hw), col(0, hw), col(0, MLA_HEADS * HEAD)],
        out_specs=[col(0, MLA_Q_LORA), col(0, MLA_KV_LORA), col(0, BLOCK)] + [whole(a) for a in acc_shapes],
        out_shape=[jax.ShapeDtypeStruct((b, lp, MLA_Q_LORA), BF16), jax.ShapeDtypeStruct((b, lp, MLA_KV_LORA), BF16),
                   jax.ShapeDtypeStruct((b, lp, BLOCK), BF16)] + acc_shapes,
        scratch_shapes=[pltpu.VMEM((BLOCK, hw), BF16)],
        compiler_params=_params(("arbitrary", "arbitrary")), name=name,
    )(proj3, proj3, gq, gkv, wq, wk, wv, *tabs, dqf, dkf, dv)


def _mla_fwd(qf, kf, v, name):
    b, lp, _ = qf.shape
    nb = lp // BLOCK
    npair = MLA_HEADS // 2
    scale = (MLA_NOPE + MLA_ROPE) ** -0.5
    starts = _chunk_starts(lp)

    def body(q_ref, k_ref, v_ref, o_ref, lse_ref, vt_ref):
        for c, s0 in enumerate(starts):
            vt_ref[c] = _split_rows_t(v_ref[0, s0:s0 + KEYS, :].astype(F32))

        def qtile(r0, bq, n):
            qs = [q_ref[0, pl.ds(r0, bq), h * BLOCK:(h + 1) * BLOCK] for h in range(2)]
            t_idx = r0 + _iota2((KEYS, bq), 1)

            def kchunk(c, carry):
                stats, acc = carry[:4], carry[4]
                s0, valid = _key_chunk(c, lp, t_idx, False, 0)
                ps, new, alphas = [], [], []
                for h in range(2):
                    m, l = stats[2 * h], stats[2 * h + 1]
                    s = jnp.where(valid, _dot_nt(k_ref[0, pl.ds(s0, KEYS), h * BLOCK:(h + 1) * BLOCK], qs[h]) * scale, NEG)
                    m_new = jnp.maximum(m, jnp.max(s, axis=0, keepdims=True))
                    p = jnp.where(valid, jnp.exp(s - m_new), 0.0)
                    alpha = jnp.exp(m - m_new)
                    new += [m_new, alpha * l + jnp.sum(p, axis=0, keepdims=True)]
                    alphas.append(alpha)
                    ps.append(p.astype(BF16))
                pv = jnp.dot(vt_ref[c], jnp.concatenate(ps, axis=0), preferred_element_type=F32)
                return (*new, _rows_of_pair(alphas[0], alphas[1]) * acc + pv)

            neg, zero = jnp.full((1, bq), NEG, F32), jnp.zeros((1, bq), F32)
            m_a, l_a, m_b, l_b, acc = lax.fori_loop(0, n, kchunk, (neg, zero, neg, zero, jnp.zeros((BLOCK, bq), F32)))
            safe = [jnp.where(l > 0.0, l, 1.0) for l in (l_a, l_b)]
            o_ref[0, pl.ds(r0, bq), :] = (acc / _rows_of_pair(safe[0], safe[1])).T
            lse = [jnp.where(l > 0.0, m + jnp.log(sf), 0.0) for m, l, sf in ((m_a, l_a, safe[0]), (m_b, l_b, safe[1]))]
            _put_rows(lse_ref, r0, bq, lse[0], lse[1])

        _for_query_tiles(nb, qtile)

    wide = pl.BlockSpec((1, lp, 2 * BLOCK), lambda bi, hp: (bi, 0, hp))
    thin = pl.BlockSpec((1, lp, 2 * HEAD), lambda bi, hp: (bi, 0, hp))
    return pl.pallas_call(
        body, grid=(b, npair), in_specs=[wide, wide, thin], out_specs=[thin, _pair_stat_spec(nb)],
        out_shape=[jax.ShapeDtypeStruct((b, lp, MLA_HEADS * HEAD), F32), jax.ShapeDtypeStruct((b, npair, nb, 8, BLOCK), F32)],
        scratch_shapes=[pltpu.VMEM((len(starts), BLOCK, 2 * KEYS), BF16)],
        compiler_params=_params(("parallel", "parallel")), name=name,
    )(qf, kf, v)


def _mla_bwd(qf, kf, v, o, lse, do, name):
    b, lp, _ = qf.shape
    nb = lp // BLOCK
    npair = MLA_HEADS // 2
    scale = (MLA_NOPE + MLA_ROPE) ** -0.5

    starts = _chunk_starts(lp)

    def body(q_ref, k_ref, v_ref, o_ref, lse_ref, do_ref, dq_ref, dk_ref, dv_ref, kt_ref):
        lo = _lo_lanes()
        dk_ref[...] = jnp.zeros_like(dk_ref)
        dv_ref[...] = jnp.zeros_like(dv_ref)
        for c, s0 in enumerate(starts):
            for h in range(2):
                kt_ref[c, h] = k_ref[0, s0:s0 + KEYS, h * BLOCK:(h + 1) * BLOCK].astype(F32).T.astype(BF16)

        def qtile(r0, bq, n):
            rows = pl.ds(r0, bq)
            qs = [q_ref[0, rows, h * BLOCK:(h + 1) * BLOCK] for h in range(2)]
            do_i = do_ref[0, rows, :]
            dos = _halves(do_i.astype(BF16), lo)
            do_st = jnp.concatenate(dos, axis=0)
            both = (do_i * o_ref[0, rows, :]).T
            dsum = (jnp.sum(both[:HEAD], axis=0, keepdims=True), jnp.sum(both[HEAD:], axis=0, keepdims=True))
            lses = _get_rows(lse_ref, r0, bq)
            t_idx = r0 + _iota2((KEYS, bq), 1)

            def kchunk(c, dqts):
                s0, valid = _key_chunk(c, lp, t_idx, False, 0)
                keys = pl.ds(s0, KEYS)
                v_c = v_ref[0, keys, :]
                ps, out = [], []
                for h in range(2):
                    lanes = slice(h * BLOCK, (h + 1) * BLOCK)
                    s = jnp.where(valid, _dot_nt(k_ref[0, keys, lanes], qs[h]) * scale, NEG)
                    p = jnp.where(valid, jnp.exp(s - lses[h]), 0.0)
                    ds = (p * (_dot_nt(v_c, dos[h]) - dsum[h]) * scale).astype(BF16)
                    dk_ref[0, keys, lanes] += jnp.dot(ds, qs[h], preferred_element_type=F32)
                    out.append(dqts[h] + jnp.dot(kt_ref[c, h], ds, preferred_element_type=F32))
                    ps.append(p.astype(BF16))
                dv_ref[0, keys, :] += jnp.dot(jnp.concatenate(ps, axis=1), do_st, preferred_element_type=F32)
                return tuple(out)

            zero = jnp.zeros((BLOCK, bq), F32)
            dq_a, dq_b = lax.fori_loop(0, n, kchunk, (zero, zero))
            dq_ref[0, rows, 0:BLOCK] = dq_a.T
            dq_ref[0, rows, BLOCK:2 * BLOCK] = dq_b.T

        _for_query_tiles(nb, qtile)

    wide = pl.BlockSpec((1, lp, 2 * BLOCK), lambda bi, hp: (bi, 0, hp))
    thin = pl.BlockSpec((1, lp, 2 * HEAD), lambda bi, hp: (bi, 0, hp))
    return pl.pallas_call(
        body, grid=(b, npair), in_specs=[wide, wide, thin, thin, _pair_stat_spec(nb), thin], out_specs=[wide, wide, thin],
        out_shape=[jax.ShapeDtypeStruct(qf.shape, F32), jax.ShapeDtypeStruct(qf.shape, F32), jax.ShapeDtypeStruct(v.shape, F32)],
        scratch_shapes=[pltpu.VMEM((len(starts), 2, BLOCK, KEYS), BF16)],
        compiler_params=_params(("parallel", "parallel")), name=name,
    )(qf, kf, v, o, lse, do)


def _swa_keys(k_ref, v_ref, n, kv):
    prev = jnp.maximum(n - 1, 0)
    rows = lambda blk: pl.ds(pl.multiple_of(blk * BLOCK, BLOCK), BLOCK)
    mine = (_iota2((1, BLOCK), 1) >= HEAD).astype(jnp.int32) == kv

    def both_halves(ref):
        x = jnp.concatenate([ref[0, rows(prev), :], ref[0, rows(n), :], ref[0, 0:BLOCK, :]], axis=0)
        return jnp.where(mine, x, pltpu.roll(x, HEAD, 1))

    slot = _iota2((3 * BLOCK, BLOCK), 0)
    loc = slot % BLOCK
    s_idx = jnp.where(slot < BLOCK, (n - 1) * BLOCK + loc, jnp.where(slot < 2 * BLOCK, n * BLOCK + loc, loc))
    dist = n * BLOCK + _iota2((3 * BLOCK, BLOCK), 1) - s_idx
    band = (slot < 2 * BLOCK) & (dist >= 0) & (dist < SWA_WINDOW) & (s_idx >= BLOCK)
    meta = (slot >= 2 * BLOCK) & (s_idx >= N_PAD) & (dist >= 0)
    return both_halves(k_ref), both_halves(v_ref), band | meta, dist.astype(F32), prev


def _swa_probs(q_h, kdup, valid, dist, head, sink_ref):
    slope = jnp.exp(jnp.full((1, 1), -8.0 * math.log(2.0) / SWA_HEADS, F32) * (head + 1).astype(F32))
    s = jnp.where(valid, _dot_nt(kdup, q_h) * (HEAD ** -0.5) - slope * dist, NEG)
    sink = sink_ref[pl.ds(head, 1), 0:1]
    m = jnp.maximum(jnp.max(s, axis=0, keepdims=True), sink)
    e = jnp.where(valid, jnp.exp(s - m), 0.0)
    es = jnp.exp(sink - m)
    inv = 1.0 / (jnp.sum(e, axis=0, keepdims=True) + es)
    return e * inv, es * inv


SWA_PAIRS = SWA_HEADS // SWA_KV_HEADS // 2
SWA_GROUP = SWA_PAIRS * 2 * HEAD


def _swa_specs(b, lp):
    nb = lp // BLOCK
    qcol = lambda first: pl.BlockSpec((1, BLOCK, SWA_GROUP), lambda bi, kv, n: (bi, n, first // SWA_GROUP + kv))
    kcol = lambda first: pl.BlockSpec((1, lp, BLOCK), lambda bi, kv, n: (bi, 0, first // BLOCK))
    sink = pl.BlockSpec((SWA_HEADS, BLOCK), lambda bi, kv, n: (0, 0))
    return (b, SWA_KV_HEADS, nb), qcol, kcol, sink


def _swa_fwd(proj3, sinks, name):
    b, lp, _ = proj3.shape
    grid, qcol, kcol, sink = _swa_specs(b, lp)

    def body(q_ref, k_ref, v_ref, sink_ref, o_ref):
        kv, n = pl.program_id(1), pl.program_id(2)
        lo = _lo_lanes()
        kdup, vdup, valid, dist, _ = _swa_keys(k_ref, v_ref, n, kv)
        kdup = kdup.astype(BF16)
        vt = _split_rows_t(vdup)
        for p in range(SWA_PAIRS):
            lanes = slice(p * BLOCK, (p + 1) * BLOCK)
            qs = _halves(q_ref[0, :, lanes].astype(BF16), lo)
            probs = [_swa_probs(qs[hh], kdup, valid, dist, (kv * SWA_PAIRS + p) * 2 + hh, sink_ref)[0].astype(BF16) for hh in range(2)]
            o_ref[0, :, lanes] = jnp.dot(vt, jnp.concatenate(probs, axis=0), preferred_element_type=F32).T

    return pl.pallas_call(
        body, grid=grid, in_specs=[qcol(OD_Q), kcol(OD_K), kcol(OD_V), sink], out_specs=qcol(0),
        out_shape=jax.ShapeDtypeStruct((b, lp, SWA_HEADS * HEAD), F32),
        compiler_params=_params(("parallel", "parallel", "parallel")), name=name,
    )(proj3, proj3, proj3, sinks)


def _swa_bwd(proj3, sinks, do, name):
    b, lp, _ = proj3.shape
    nb = lp // BLOCK
    grid, qcol, kcol, sink = _swa_specs(b, lp)

    def body(q_ref, k_ref, v_ref, sink_ref, do_ref, dq_ref, dk_ref, dv_ref, dsink_ref, dk_acc, dv_acc):
        kv, n = pl.program_id(1), pl.program_id(2)

        @pl.when((n == 0) & (pl.program_id(0) == 0) & (kv == 0))
        def _():
            dsink_ref[...] = jnp.zeros_like(dsink_ref)

        @pl.when(n == 0)
        def _():
            dk_acc[...] = jnp.zeros_like(dk_acc)
            dv_acc[...] = jnp.zeros_like(dv_acc)

        lo = _lo_lanes()
        kdup, vdup, valid, dist, prev = _swa_keys(k_ref, v_ref, n, kv)
        kt = _split_rows_t(kdup)
        kdup, vdup = kdup.astype(BF16), vdup.astype(BF16)
        dkc = jnp.zeros((3 * BLOCK, BLOCK), F32)
        dvc = jnp.zeros((3 * BLOCK, BLOCK), F32)
        for p in range(SWA_PAIRS):
            lanes = slice(p * BLOCK, (p + 1) * BLOCK)
            qs = _halves(q_ref[0, :, lanes].astype(BF16), lo)
            dos = _halves(do_ref[0, :, lanes].astype(BF16), lo)
            dss, prs = [], []
            for hh in range(2):
                head = (kv * SWA_PAIRS + p) * 2 + hh
                pr, ps = _swa_probs(qs[hh], kdup, valid, dist, head, sink_ref)
                dp = _dot_nt(vdup, dos[hh])
                dsum = jnp.sum(pr * dp, axis=0, keepdims=True)
                dsink_ref[pl.ds(head, 1), :] += jnp.broadcast_to(-jnp.sum(ps * dsum, axis=1, keepdims=True), (1, BLOCK))
                dss.append((pr * (dp - dsum) * (HEAD ** -0.5)).astype(BF16))
                prs.append(pr.astype(BF16))
            dq_ref[0, :, lanes] = jnp.dot(kt, jnp.concatenate(dss, axis=0), preferred_element_type=F32).T
            dkc = dkc + jnp.dot(jnp.concatenate(dss, axis=1), jnp.concatenate(qs, axis=0), preferred_element_type=F32)
            dvc = dvc + jnp.dot(jnp.concatenate(prs, axis=1), jnp.concatenate(dos, axis=0), preferred_element_type=F32)
        rows = lambda blk: pl.ds(pl.multiple_of(blk * BLOCK, BLOCK), BLOCK)
        for part, r in enumerate((rows(prev), rows(n), slice(0, BLOCK))):
            dk_acc[r, :] += dkc[part * BLOCK:(part + 1) * BLOCK]
            dv_acc[r, :] += dvc[part * BLOCK:(part + 1) * BLOCK]

        for acc, ref in ((dk_acc, dk_ref), (dv_acc, dv_ref)):
            @pl.when((n == nb - 1) & (kv == 0))
            def _():
                x = acc[...]
                ref[0] = x + pltpu.roll(x, HEAD, 1)

            @pl.when((n == nb - 1) & (kv == 1))
            def _():
                x = acc[...]
                ref[0] = jnp.where(lo, ref[0], x + pltpu.roll(x, HEAD, 1))

    kvout = pl.BlockSpec((1, lp, BLOCK), lambda bi, kv, n: (bi, 0, 0))
    kvshape = jax.ShapeDtypeStruct((b, lp, BLOCK), F32)
    return pl.pallas_call(
        body, grid=grid, in_specs=[qcol(OD_Q), kcol(OD_K), kcol(OD_V), sink, qcol(0)], out_specs=[qcol(0), kvout, kvout, sink],
        out_shape=[jax.ShapeDtypeStruct((b, lp, SWA_HEADS * HEAD), F32), kvshape, kvshape, jax.ShapeDtypeStruct((SWA_HEADS, BLOCK), F32)],
        scratch_shapes=[pltpu.VMEM((lp, BLOCK), F32), pltpu.VMEM((lp, BLOCK), F32)],
        compiler_params=_params(("arbitrary", "arbitrary", "arbitrary")), name=name,
    )(proj3, proj3, proj3, sinks, do)


def _kernel_weights(ev_w_in, ev_w_uq, ev_w_ukv, od_w_in):
    zeros = lambda r, c: jnp.zeros((r, c), ev_w_in.dtype)
    q_sb, k_sb, v_sb, g_sb, c_q, c_kv, k_r, g_mla = jnp.split(ev_w_in, [512, 1024, 1536, 2048, 2304, 2432, 2464], axis=1)
    w0 = jnp.concatenate([g_sb, g_mla, q_sb, k_sb, v_sb, c_q, c_kv, zeros(D_MODEL, MLA_NOPE), k_r, zeros(D_MODEL, 32)], axis=1)
    uq = ev_w_uq.reshape(MLA_Q_LORA, MLA_HEADS, MLA_NOPE + MLA_ROPE)
    wq = jnp.pad(uq, ((0, 0), (0, 0), (0, BLOCK - MLA_NOPE - MLA_ROPE))).reshape(MLA_Q_LORA, MLA_HEADS * BLOCK)
    ukv = ev_w_ukv.reshape(MLA_KV_LORA, MLA_HEADS, BLOCK)
    wk = jnp.pad(ukv[:, :, :MLA_NOPE], ((0, 0), (0, 0), (0, BLOCK - MLA_NOPE))).reshape(MLA_KV_LORA, MLA_HEADS * BLOCK)
    wv = ukv[:, :, MLA_NOPE:].reshape(MLA_KV_LORA, MLA_HEADS * HEAD)
    q, k, v, g = jnp.split(od_w_in, [1024, 1152, 1280], axis=1)
    w1 = jnp.concatenate([g, q, k, v], axis=1)
    return w0, wq, wk, wv, w1


def _original_grads(dw0, dwq, dwk, dwv, dw1):
    sl = lambda a, first, n: a[:, first:first + n]
    d_ev_w_in = jnp.concatenate([sl(dw0, EV_Q, 512), sl(dw0, EV_K, 512), sl(dw0, EV_V, 512), sl(dw0, EV_G, 512), sl(dw0, EV_CQ, 256),
                                 sl(dw0, EV_CKV, 128), sl(dw0, EV_KR + MLA_NOPE, MLA_ROPE), sl(dw0, EV_G + 512, 512)], axis=1)
    d_uq = dwq.reshape(MLA_Q_LORA, MLA_HEADS, BLOCK)[:, :, :MLA_NOPE + MLA_ROPE].reshape(MLA_Q_LORA, -1)
    d_ukv = jnp.concatenate([dwk.reshape(MLA_KV_LORA, MLA_HEADS, BLOCK)[:, :, :MLA_NOPE], dwv.reshape(MLA_KV_LORA, MLA_HEADS, HEAD)],
                            axis=2).reshape(MLA_KV_LORA, -1)
    d_od_w_in = jnp.concatenate([sl(dw1, OD_Q, 1024), sl(dw1, OD_K, 128), sl(dw1, OD_V, 128), sl(dw1, OD_G, 1024)], axis=1)
    return d_ev_w_in, d_uq, d_ukv, d_od_w_in


def _meta_rows_sum(dh0_3):
    b, _, d = dh0_3.shape

    def body(x_ref, o_ref):
        acc = x_ref[0, N_PAD:BLOCK, :]
        for i in range(1, b):
            acc = acc + x_ref[i, N_PAD:BLOCK, :]
        o_ref[...] = acc

    return pl.pallas_call(
        body, grid=(1,), in_specs=[pl.BlockSpec((b, BLOCK, d), lambda i: (0, 0, 0))], out_specs=pl.BlockSpec((N_META, d), lambda i: (0, 0)),
        out_shape=jax.ShapeDtypeStruct((N_META, d), F32), compiler_params=_params(("arbitrary",)), name="meta_rows_sum",
    )(dh0_3)


def _local_step(x, meta, norm_g, final_g, gq, gkv, sinks, target, ev_w_in, ev_w_uq, ev_w_ukv, wo0, od_w_in, wo1):
    b, seq, d = x.shape
    lp = seq + BLOCK
    t = b * lp
    w0, wq, wk, wv, w1 = _kernel_weights(ev_w_in, ev_w_uq, ev_w_ukv, od_w_in)
    h0 = jnp.concatenate([jnp.zeros((b, N_PAD, d), F32), jnp.broadcast_to(meta[None], (b, N_META, d)), x], axis=1).reshape(t, d)
    tabs = _rope_tables(lp)
    g0, g1 = norm_g[0:1], norm_g[1:2]

    hn0 = _rms_fwd(h0, g0, "norm0")
    proj0 = _mm(hn0, w0, "inproj0")
    p0 = proj0.reshape(b, lp, EV_N)
    o_sb, sb_tot = _sb_fwd(p0, "sb_fwd")
    qf, kf, v = _mla_prep_fwd(p0, gq, gkv, wq, wk, wv, tabs, "mla_prep_fwd")
    o_mla, lse = _mla_fwd(qf, kf, v, "mla_fwd")
    o0 = [o_sb.reshape(t, -1), o_mla.reshape(t, -1)]
    ao0 = _gate_fwd(o0, proj0, "gate0")
    h1 = _mm(ao0, wo0, "outproj0", res=h0)

    hn1 = _rms_fwd(h1, g1, "norm1")
    proj1 = _mm(hn1, w1, "inproj1")
    p1 = proj1.reshape(b, lp, OD_N)
    sinks_b = jnp.broadcast_to(sinks.reshape(SWA_HEADS, 1), (SWA_HEADS, BLOCK))
    o1 = _swa_fwd(p1, sinks_b, "swa_fwd").reshape(t, -1)
    ao1 = _gate_fwd([o1], proj1, "gate1")
    h2 = _mm(ao1, wo1, "outproj1", res=h1)

    dh2, d_final_g, loss = _loss_head(h2, final_g.reshape(1, d), target, b, lp)

    d_wo1 = _mm_tn(ao1, dh2, "d_wo1")
    dao1 = _mm_nt(dh2, wo1, "d_ao1")
    (do1,), dg1 = _gate_bwd(dao1, [o1], proj1, "gate1_bwd")
    dq1, dk4, dv4, d_sinks = _swa_bwd(p1, sinks_b, do1.reshape(b, lp, -1), "swa_bwd")
    unheads = lambda a: a.reshape(t, SWA_KV_HEADS * HEAD).astype(BF16)
    dproj1 = jnp.concatenate([dg1, dq1.reshape(t, -1).astype(BF16), unheads(dk4), unheads(dv4)], axis=1)
    d_w1 = _mm_tn(hn1, dproj1, "d_w1")
    dhn1 = _mm_nt(dproj1, w1, "d_hn1")
    dh1, d_g1 = _rms_bwd(h1, g1, dhn1, dh2, "norm1_bwd")

    d_wo0 = _mm_tn(ao0, dh1, "d_wo0")
    dao0 = _mm_nt(dh1, wo0, "d_ao0")
    (do_sb, do_mla), dg0 = _gate_bwd(dao0, o0, proj0, "gate0_bwd")
    dq_sb, dk_sb, dv_sb = _sb_bwd(p0, sb_tot, do_sb.reshape(b, lp, -1), "sb_bwd")
    dqf, dkf, dv = _mla_bwd(qf, kf, v, o_mla, lse, do_mla.reshape(b, lp, -1), "mla_bwd")
    dcq, dckv, dkr, d_wq, d_wk, d_wv, d_gq, d_gkv = _mla_prep_bwd(p0, gq, gkv, wq, wk, wv, tabs, dqf, dkf, dv, "mla_prep_bwd")
    flat = lambda a: a.reshape(t, -1).astype(BF16)
    dproj0 = jnp.concatenate([dg0, flat(dq_sb), flat(dk_sb), flat(dv_sb), flat(dcq), flat(dckv), flat(dkr)], axis=1)
    d_w0 = _mm_tn(hn0, dproj0, "d_w0")
    dhn0 = _mm_nt(dproj0, w0, "d_hn0")
    dh0, d_g0 = _rms_bwd(h0, g0, dhn0, dh1, "norm0_bwd")
    dh0 = dh0.reshape(b, lp, d)

    d_ev_w_in, d_uq, d_ukv, d_od_w_in = _original_grads(d_w0, d_wq, d_wk, d_wv, d_w1)
    grads = dict(meta=_meta_rows_sum(dh0), norm_g=jnp.concatenate([d_g0, d_g1], axis=0), final_g=d_final_g.reshape(d),
                 ev_w_in=d_ev_w_in, ev_q_norm_g=d_gq, ev_kv_norm_g=d_gkv, ev_w_uq=d_uq, ev_w_ukv=d_ukv, ev_w_out=d_wo0,
                 od_w_in=d_od_w_in, od_sinks=d_sinks[:, 0].reshape(1, SWA_HEADS), od_w_out=d_wo1)
    return loss, dh0[:, BLOCK:], grads


MESH = pl.DeviceIdType.MESH
ANY = pl.BlockSpec(memory_space=pl.ANY)


def _place():
    return lax.axis_index("x"), lax.axis_index("y"), lax.axis_index("c")


def _other_chips(x, y):
    return [(1 - x, y), (x, 1 - y), (1 - x, 1 - y)]


PACK_ROWS = 1920
HALF_ROWS = PACK_ROWS // 2


def _with_own_slot(slots, own):
    me = 2 * lax.axis_index("x") + lax.axis_index("y")
    return lax.dynamic_update_slice(slots, own[None], (me,) + (0,) * own.ndim)


def _gather_weights(wpack, meta, name):
    def body(w_ref, m_ref, wo_ref, mo_ref, send_sems, recv_sems):
        x, y, c = _place()
        me, sib = 2 * x + y, (x, y, 1 - c)
        chips = _other_chips(x, y)
        mine = w_ref.at[pl.ds(c * HALF_ROWS, HALF_ROWS), :]
        half = lambda chip, h: wo_ref.at[chip, pl.ds(h * HALF_ROWS, HALF_ROWS), :]

        def copy(k, src, dst, to):
            return pltpu.make_async_remote_copy(src_ref=src, dst_ref=dst, send_sem=send_sems.at[k], recv_sem=recv_sems.at[k], device_id=to,
                                                device_id_type=MESH)

        sent = [copy(k, mine, half(me, c), (px, py, c)) for k, (px, py) in enumerate(chips)]
        sent += [copy(6 + k, m_ref, mo_ref.at[me], (px, py, c)) for k, (px, py) in enumerate(chips)]
        for cp in sent:
            cp.start()
        for k, (px, py) in enumerate(chips):
            landed = half(2 * px + py, c)
            copy(k, mine, landed, (px, py, c)).wait_recv()
            fwd = copy(3 + k, landed, landed, sib)
            fwd.start()
            sent.append(fwd)
        for k, (px, py) in enumerate(chips):
            other = half(2 * px + py, 1 - c)
            copy(3 + k, other, other, sib).wait_recv()
            copy(6 + k, m_ref, mo_ref.at[2 * px + py], (px, py, c)).wait_recv()
        for cp in sent:
            cp.wait_send()

    wall, meta_all = pl.pallas_call(
        body, in_specs=[ANY, ANY], out_specs=[ANY, ANY],
        out_shape=[jax.ShapeDtypeStruct((N_CHIPS,) + wpack.shape, wpack.dtype), jax.ShapeDtypeStruct((N_CHIPS,) + meta.shape, meta.dtype)],
        scratch_shapes=[pltpu.SemaphoreType.DMA((9,)), pltpu.SemaphoreType.DMA((9,))],
        name=name,
    )(wpack, meta)
    return _with_own_slot(wall, wpack), _with_own_slot(meta_all, meta)


def _grads_to_sibling(g, name):
    def body(g_ref, o_ref, send_sem, recv_sem):
        x, y, c = _place()
        cp = pltpu.make_async_remote_copy(src_ref=g_ref.at[:, pl.ds((1 - c) * HALF_ROWS, HALF_ROWS), :], dst_ref=o_ref, send_sem=send_sem,
                                          recv_sem=recv_sem, device_id=(x, y, 1 - c), device_id_type=MESH)
        cp.start()
        cp.wait()

    return pl.pallas_call(
        body, in_specs=[ANY], out_specs=ANY, out_shape=jax.ShapeDtypeStruct((g.shape[0], HALF_ROWS, g.shape[2]), g.dtype),
        scratch_shapes=[pltpu.SemaphoreType.DMA(()), pltpu.SemaphoreType.DMA(())],
        name=name,
    )(g)


def _share_halves(r, name):
    def body(r_ref, o_ref, send_sem, recv_sem):
        x, y, c = _place()
        cp = pltpu.make_async_remote_copy(src_ref=r_ref, dst_ref=o_ref, send_sem=send_sem, recv_sem=recv_sem, device_id=(x, y, 1 - c),
                                          device_id_type=MESH)
        cp.start()
        cp.wait()

    theirs = pl.pallas_call(
        body, in_specs=[ANY], out_specs=ANY, out_shape=jax.ShapeDtypeStruct(r.shape, r.dtype),
        scratch_shapes=[pltpu.SemaphoreType.DMA(()), pltpu.SemaphoreType.DMA(())],
        name=name,
    )(r)
    return jnp.where(lax.axis_index("c") == 0, jnp.concatenate([r, theirs], axis=0), jnp.concatenate([theirs, r], axis=0))


def _chip_scatter(s, name):
    def body(s_ref, r_ref, send_sems, recv_sems):
        x, y, c = _place()
        me = 2 * x + y
        for k, (px, py) in enumerate(_other_chips(x, y)):
            pltpu.make_async_remote_copy(src_ref=s_ref.at[2 * px + py], dst_ref=r_ref.at[me], send_sem=send_sems.at[k],
                                         recv_sem=recv_sems.at[k], device_id=(px, py, c), device_id_type=MESH).start()
        for k, (px, py) in enumerate(_other_chips(x, y)):
            cp = pltpu.make_async_remote_copy(src_ref=s_ref.at[2 * px + py], dst_ref=r_ref.at[2 * px + py], send_sem=send_sems.at[k],
                                              recv_sem=recv_sems.at[k], device_id=(px, py, c), device_id_type=MESH)
            cp.wait_recv()
            cp.wait_send()

    parts = pl.pallas_call(
        body, in_specs=[ANY], out_specs=ANY, out_shape=jax.ShapeDtypeStruct(s.shape, s.dtype),
        scratch_shapes=[pltpu.SemaphoreType.DMA((3,)), pltpu.SemaphoreType.DMA((3,))],
        name=name,
    )(s)
    me = 2 * lax.axis_index("x") + lax.axis_index("y")
    return _with_own_slot(parts, lax.dynamic_index_in_dim(s, me, axis=0, keepdims=False))


def _all_reduce_small(v, name):
    shape = v.shape

    def body(v_ref, o_ref, slots, send_sems, recv_sems):
        x, y, c = _place()
        me = 4 * x + 2 * y + c
        slots[me] = v_ref[...]
        for r in range(1, N_DEV):
            peer = (x ^ (r >> 2), y ^ ((r >> 1) & 1), c ^ (r & 1))
            pltpu.make_async_remote_copy(src_ref=v_ref, dst_ref=slots.at[me], send_sem=send_sems.at[r - 1], recv_sem=recv_sems.at[r - 1],
                                         device_id=peer, device_id_type=MESH).start()
        for r in range(1, N_DEV):
            peer = (x ^ (r >> 2), y ^ ((r >> 1) & 1), c ^ (r & 1))
            cp = pltpu.make_async_remote_copy(src_ref=v_ref, dst_ref=slots.at[4 * peer[0] + 2 * peer[1] + peer[2]], send_sem=send_sems.at[r - 1],
                                              recv_sem=recv_sems.at[r - 1], device_id=peer, device_id_type=MESH)
            cp.wait_recv()
            cp.wait_send()
        acc = slots[0]
        for d in range(1, N_DEV):
            acc = acc + slots[d]
        o_ref[...] = acc

    vm = pl.BlockSpec(memory_space=pltpu.VMEM)
    return pl.pallas_call(
        body, in_specs=[vm], out_specs=vm, out_shape=jax.ShapeDtypeStruct(shape, F32),
        scratch_shapes=[pltpu.VMEM((N_DEV,) + shape, F32), pltpu.SemaphoreType.DMA((N_DEV - 1,)), pltpu.SemaphoreType.DMA((N_DEV - 1,))],
        name=name,
    )(v)


SUM_ROWS = 480


def _add_sibling(g, gsib, core, name):
    n, _, cdim = g.shape
    per_half = HALF_ROWS // SUM_ROWS

    def body(core_ref, a_ref, b_ref, o_ref):
        o_ref[...] = (a_ref[...] + b_ref[...]).astype(o_ref.dtype)

    blk = pl.BlockSpec((1, SUM_ROWS, cdim), lambda j, i, core_ref: (j, i, 0))
    return pl.pallas_call(
        body,
        grid_spec=pltpu.PrefetchScalarGridSpec(
            num_scalar_prefetch=1, grid=(n, per_half),
            in_specs=[pl.BlockSpec((1, SUM_ROWS, cdim), lambda j, i, core_ref: (j, core_ref[0] * per_half + i, 0)), blk], out_specs=blk),
        out_shape=jax.ShapeDtypeStruct(gsib.shape, BF16), compiler_params=_params(("parallel", "parallel")), name=name,
    )(core, g, gsib)


def _sum_parts(parts, name):
    n, r, cdim = parts.shape

    def body(p_ref, o_ref):
        acc = p_ref[0].astype(F32)
        for j in range(1, n):
            acc = acc + p_ref[j].astype(F32)
        o_ref[...] = acc

    return pl.pallas_call(
        body, grid=(r // SUM_ROWS,), in_specs=[pl.BlockSpec((n, SUM_ROWS, cdim), lambda i: (0, i, 0))],
        out_specs=pl.BlockSpec((SUM_ROWS, cdim), lambda i: (i, 0)), out_shape=jax.ShapeDtypeStruct((r, cdim), F32),
        compiler_params=_params(("parallel",)), name=name,
    )(parts)


def _adamw(parts, w, m, v, name):
    npart, r, cdim = parts.shape
    tr = 384 if r % 384 == 0 else r

    def body(p_ref, w_ref, m_ref, v_ref, g_ref, d_ref, nm_ref, nv_ref):
        g = p_ref[0]
        for j in range(1, npart):
            g = g + p_ref[j]
        m_new = ADAM_B1 * m_ref[...] + (1.0 - ADAM_B1) * g
        v_new = ADAM_B2 * v_ref[...] + (1.0 - ADAM_B2) * (g * g)
        m_hat = m_new / (1.0 - ADAM_B1 ** ADAM_STEP)
        v_hat = v_new / (1.0 - ADAM_B2 ** ADAM_STEP)
        g_ref[...] = g
        d_ref[...] = -ADAM_LR * (m_hat / (jnp.sqrt(v_hat) + ADAM_EPS) + ADAM_WD * w_ref[...])
        nm_ref[...] = m_new
        nv_ref[...] = v_new

    blk = pl.BlockSpec((tr, cdim), lambda i: (i, 0))
    shp = jax.ShapeDtypeStruct((r, cdim), F32)
    return pl.pallas_call(
        body, grid=(r // tr,), in_specs=[pl.BlockSpec((npart, tr, cdim), lambda i: (0, i, 0)), blk, blk, blk], out_specs=[blk] * 4,
        out_shape=[shp] * 4, compiler_params=_params(("parallel",)), name=name,
    )(parts, w, m, v)


BIG = ("ev_w_in", "ev_w_uq", "ev_w_ukv", "ev_w_out", "od_w_in", "od_w_out", "meta")
SMALL = ("norm_g", "final_g", "ev_q_norm_g", "ev_kv_norm_g", "od_sinks")
SMALL_SHAPE = (8, 512)


def _pack_big(arrs):
    rows = [a.reshape(-1, D_MODEL) for a in arrs]
    used = sum(r.shape[0] for r in rows)
    return jnp.pad(jnp.concatenate(rows, axis=0), ((0, PACK_ROWS - used), (0, 0)))


def _unpack_big(p, shapes):
    out, at = [], 0
    for s in shapes:
        nrow = int(np.prod(s)) // D_MODEL
        out.append(p[at:at + nrow].reshape(s))
        at += nrow
    return out


def _pack_small(arrs, extra=None):
    flat = [a.reshape(-1) for a in arrs] + ([] if extra is None else [extra.reshape(-1)])
    used = sum(f.shape[0] for f in flat)
    return jnp.pad(jnp.concatenate(flat), (0, SMALL_SHAPE[0] * SMALL_SHAPE[1] - used)).reshape(SMALL_SHAPE)


def _unpack_small(p, shapes):
    flat, out, at = p.reshape(-1), [], 0
    for s in shapes:
        n = int(np.prod(s))
        out.append(flat[at:at + n].reshape(s))
        at += n
    return out, flat[at]


def _shard_of(full, name, j):
    if name in ("ev_w_out", "od_w_out"):
        n = full.shape[0] // N_CHIPS
        return full[j * n:(j + 1) * n]
    n = full.shape[1] // N_CHIPS
    return full[:, j * n:(j + 1) * n]


def kernel(x, meta, norm_g, final_g, ev_w_in, ev_q_norm_g, ev_kv_norm_g, ev_w_uq, ev_w_ukv, ev_w_out, od_w_in, od_sinks, od_w_out, loss_target, m_meta, m_norm_g, m_final_g, m_ev_w_in, m_ev_q_norm_g, m_ev_kv_norm_g, m_ev_w_uq, m_ev_w_ukv, m_ev_w_out, m_od_w_in, m_od_sinks, m_od_w_out, v_meta, v_norm_g, v_final_g, v_ev_w_in, v_ev_q_norm_g, v_ev_kv_norm_g, v_ev_w_uq, v_ev_w_ukv, v_ev_w_out, v_od_w_in, v_od_sinks, v_od_w_out):
    given = dict(locals())
    big_w = [given[n][0] if given[n].ndim == 3 else given[n] for n in BIG]
    big_shapes = [given[n].shape for n in BIG]

    wpack = _pack_big(big_w[:-1]).astype(BF16)
    wall, meta_all = _gather_weights(wpack, meta, "gather_weights")
    per_chip = [_unpack_big(wall[j], [a.shape for a in big_w[:-1]]) for j in range(N_CHIPS)]
    cat = lambda i, axis: jnp.concatenate([per_chip[j][i] for j in range(N_CHIPS)], axis=axis)
    full = dict(ev_w_in=cat(0, 1), ev_w_uq=cat(1, 1), ev_w_ukv=cat(2, 1), ev_w_out=cat(3, 0), od_w_in=cat(4, 1), od_w_out=cat(5, 0))
    meta_full = jnp.concatenate([meta_all[j] for j in range(N_CHIPS)], axis=1)

    loss, grad_x, grads = _local_step(x, meta_full, norm_g, final_g, ev_q_norm_g, ev_kv_norm_g, od_sinks, loss_target,
                                      full["ev_w_in"], full["ev_w_uq"], full["ev_w_ukv"], full["ev_w_out"], full["od_w_in"], full["od_w_out"])

    gpack = jnp.stack([_pack_big([_shard_of(grads[n], n, j) for n in BIG]) for j in range(N_CHIPS)])
    core = lax.axis_index("c").astype(jnp.int32).reshape(1)
    gsum = _add_sibling(gpack, _grads_to_sibling(gpack, "grads_to_sibling"), core, "add_sibling")
    reduced = _share_halves(_sum_parts(_chip_scatter(gsum, "grads_to_chips"), "add_chips"), "reduced_to_sibling")
    big_out = _adamw(reduced[None], _pack_big(big_w), _pack_big([given["m_" + n] for n in BIG]), _pack_big([given["v_" + n] for n in BIG]),
                     "adamw_matrices")
    big_out = [_unpack_big(o, big_shapes) for o in big_out]

    small_shapes = [given[n].shape for n in SMALL]
    ssum = _all_reduce_small(_pack_small([grads[n] for n in SMALL], loss[0, 0]), "reduce_vectors")
    small_out = _adamw(ssum[None], _pack_small([given[n] for n in SMALL]), _pack_small([given["m_" + n] for n in SMALL]),
                       _pack_small([given["v_" + n] for n in SMALL]), "adamw_vectors")
    total_loss = ssum.reshape(-1)[sum(int(np.prod(s)) for s in small_shapes)]
    small_out = [_unpack_small(o, small_shapes)[0] for o in small_out]

    names = ("meta", "norm_g", "final_g", "ev_w_in", "ev_q_norm_g", "ev_kv_norm_g", "ev_w_uq", "ev_w_ukv", "ev_w_out", "od_w_in", "od_sinks",
             "od_w_out")
    outs = [total_loss, grad_x]
    for kind in range(4):
        for n in names:
            outs.append(big_out[kind][BIG.index(n)] if n in BIG else small_out[kind][SMALL.index(n)])
    return tuple(outs)
```

```python
import functools
import math

import numpy as np
import jax
import jax.numpy as jnp
from jax import lax
from jax.experimental import pallas as pl
from jax.experimental.pallas import tpu as pltpu

F32 = jnp.float32
BF16 = jnp.bfloat16

D_MODEL = 1024
BLOCK = 128
N_META = 16
N_PAD = BLOCK - N_META
NORM_EPS = 1e-6
NEG = -1e30
HEAD = 64
SB_HEADS = 8
MLA_HEADS = 8
MLA_Q_LORA = 256
MLA_KV_LORA = 128
MLA_NOPE = 64
MLA_ROPE = 32
ROPE_BASE = 10000.0
SWA_HEADS = 16
SWA_KV_HEADS = 2
SWA_WINDOW = 128
N_CHIPS = 4
N_DEV = 8

ADAM_LR = 0.001
ADAM_B1 = 0.9
ADAM_B2 = 0.999
ADAM_EPS = 1e-08
ADAM_WD = 0.01
ADAM_STEP = 10

VMEM_LIMIT = 48 * 1024 * 1024

EV_G, EV_Q, EV_K, EV_V, EV_CQ, EV_CKV, EV_KR, EV_N = 0, 1024, 1536, 2048, 2560, 2816, 2944, 3072
OD_G, OD_Q, OD_K, OD_V, OD_N = 0, 1024, 2048, 2176, 2304


def _params(sem=None):
    return pltpu.CompilerParams(dimension_semantics=sem, vmem_limit_bytes=VMEM_LIMIT)


def _row_tile(m):
    return 256 if m % 256 == 0 else 128


def _matmul_rows(m):
    for c in (1088, 1024, 768, 640, 512, 384, 256):
        if m % c == 0:
            return c
    return 128


def _dot(a, b):
    return jnp.dot(a.astype(BF16), b.astype(BF16), preferred_element_type=F32)


def _dot_nt(a, b):
    return lax.dot_general(a.astype(BF16), b.astype(BF16), (((1,), (1,)), ((), ())), preferred_element_type=F32)


def _dot_tn(a, b):
    return lax.dot_general(a.astype(BF16), b.astype(BF16), (((0,), (0,)), ((), ())), preferred_element_type=F32)


def _rms_fwd(h, g, name):
    t, d = h.shape
    tm = _row_tile(t)

    def body(h_ref, g_ref, o_ref):
        x = h_ref[...]
        r = lax.rsqrt(jnp.mean(x * x, axis=-1, keepdims=True) + NORM_EPS)
        o_ref[...] = ((x * r) * g_ref[...]).astype(o_ref.dtype)

    return pl.pallas_call(
        body, grid=(t // tm,),
        in_specs=[pl.BlockSpec((tm, d), lambda i: (i, 0)), pl.BlockSpec((1, d), lambda i: (0, 0))],
        out_specs=pl.BlockSpec((tm, d), lambda i: (i, 0)),
        out_shape=jax.ShapeDtypeStruct((t, d), BF16), compiler_params=_params(("parallel",)), name=name,
    )(h, g)


def _rms_bwd(h, g, dy, dres, name):
    t, d = h.shape
    tm = _row_tile(t)

    def body(h_ref, g_ref, dy_ref, dres_ref, dh_ref, dg_ref):
        @pl.when(pl.program_id(0) == 0)
        def _():
            dg_ref[...] = jnp.zeros_like(dg_ref)

        x = h_ref[...]
        r = lax.rsqrt(jnp.mean(x * x, axis=-1, keepdims=True) + NORM_EPS)
        xr = x * r
        dy_ = dy_ref[...]
        u = dy_ * g_ref[...]
        dh_ref[...] = dres_ref[...] + r * (u - xr * jnp.mean(u * xr, axis=-1, keepdims=True))
        dg_ref[...] += jnp.sum(dy_ * xr, axis=0, keepdims=True)

    row = pl.BlockSpec((tm, d), lambda i: (i, 0))
    vec = pl.BlockSpec((1, d), lambda i: (0, 0))
    return pl.pallas_call(
        body, grid=(t // tm,), in_specs=[row, vec, row, row], out_specs=[row, vec],
        out_shape=[jax.ShapeDtypeStruct((t, d), F32), jax.ShapeDtypeStruct((1, d), F32)],
        compiler_params=_params(("arbitrary",)), name=name,
    )(h, g, dy, dres)


def _col_tile(n):
    for c in (1024, 768, 640, 512, 384, 256, 128):
        if n % c == 0:
            return c
    return n


def _mm(a, w, name, res=None, out_dtype=F32, a_cols=None):
    m = a.shape[0]
    k, n = w.shape
    a_blk = 0 if a_cols is None else a_cols[0] // k
    assert a_cols is None or (a_cols[1] == k and a_cols[0] % k == 0)
    tm, tn = _matmul_rows(m), _col_tile(n)

    def body(*refs):
        if res is None:
            a_ref, w_ref, o_ref = refs
            acc = _dot(a_ref[...], w_ref[...])
        else:
            a_ref, w_ref, r_ref, o_ref = refs
            acc = r_ref[...] + _dot(a_ref[...], w_ref[...])
        o_ref[...] = acc.astype(o_ref.dtype)

    in_specs = [pl.BlockSpec((tm, k), lambda j, i: (i, a_blk)), pl.BlockSpec((k, tn), lambda j, i: (0, j))]
    args = [a, w]
    if res is not None:
        in_specs.append(pl.BlockSpec((tm, tn), lambda j, i: (i, j)))
        args.append(res)
    return pl.pallas_call(
        body, grid=(n // tn, m // tm), in_specs=in_specs, out_specs=pl.BlockSpec((tm, tn), lambda j, i: (i, j)),
        out_shape=jax.ShapeDtypeStruct((m, n), out_dtype), compiler_params=_params(("parallel", "parallel")), name=name,
    )(*args)


def _mm_nt(a, w, name):
    m, n = a.shape
    k = w.shape[0]
    tm, tk = _matmul_rows(m), _col_tile(k)

    def body(a_ref, w_ref, o_ref):
        o_ref[...] = _dot_nt(a_ref[...], w_ref[...])

    return pl.pallas_call(
        body, grid=(k // tk, m // tm),
        in_specs=[pl.BlockSpec((tm, n), lambda j, i: (i, 0)), pl.BlockSpec((tk, n), lambda j, i: (j, 0))],
        out_specs=pl.BlockSpec((tm, tk), lambda j, i: (i, j)),
        out_shape=jax.ShapeDtypeStruct((m, k), F32), compiler_params=_params(("parallel", "parallel")), name=name,
    )(a, w)


def _mm_tn(x, dy, name):
    m, k = x.shape
    n = dy.shape[1]
    tm, tn = _matmul_rows(m), _col_tile(n)

    def body(x_ref, dy_ref, o_ref):
        @pl.when(pl.program_id(1) == 0)
        def _():
            o_ref[...] = jnp.zeros_like(o_ref)

        o_ref[...] += _dot_tn(x_ref[...], dy_ref[...])

    return pl.pallas_call(
        body, grid=(n // tn, m // tm),
        in_specs=[pl.BlockSpec((tm, k), lambda j, i: (i, 0)), pl.BlockSpec((tm, tn), lambda j, i: (i, j))],
        out_specs=pl.BlockSpec((k, tn), lambda j, i: (0, j)),
        out_shape=jax.ShapeDtypeStruct((k, n), F32), compiler_params=_params(("parallel", "arbitrary")), name=name,
    )(x, dy)


def _silu_parts(g):
    s = 1.0 / (1.0 + jnp.exp(-g))
    return g * s, s * (1.0 + g * (1.0 - s))


def _gate_fwd(o_parts, proj, name):
    t = proj.shape[0]
    tm = _row_tile(t)
    w = D_MODEL // len(o_parts)

    def body(*refs):
        g_ref, o_ref = refs[-2], refs[-1]
        for p, r in enumerate(refs[:-2]):
            sil, _ = _silu_parts(g_ref[:, p * w:(p + 1) * w])
            o_ref[:, p * w:(p + 1) * w] = (r[...].astype(F32) * sil).astype(o_ref.dtype)

    return pl.pallas_call(
        body, grid=(t // tm,),
        in_specs=[pl.BlockSpec((tm, w), lambda i: (i, 0)) for _ in o_parts] + [pl.BlockSpec((tm, D_MODEL), lambda i: (i, 0))],
        out_specs=pl.BlockSpec((tm, D_MODEL), lambda i: (i, 0)),
        out_shape=jax.ShapeDtypeStruct((t, D_MODEL), BF16), compiler_params=_params(("parallel",)), name=name,
    )(*o_parts, proj)


def _gate_bwd(dao, o_parts, proj, name):
    t = proj.shape[0]
    tm = _row_tile(t)
    np_ = len(o_parts)
    w = D_MODEL // np_

    def body(*refs):
        dao_ref, g_ref = refs[0], refs[1 + np_]
        do_refs, dg_ref = refs[2 + np_:2 + 2 * np_], refs[-1]
        for p in range(np_):
            sl = slice(p * w, (p + 1) * w)
            sil, dsil = _silu_parts(g_ref[:, sl])
            da = dao_ref[:, sl]
            do_refs[p][...] = da * sil
            dg_ref[:, sl] = (da * refs[1 + p][...].astype(F32) * dsil).astype(dg_ref.dtype)

    full = pl.BlockSpec((tm, D_MODEL), lambda i: (i, 0))
    part = pl.BlockSpec((tm, w), lambda i: (i, 0))
    outs = pl.pallas_call(
        body, grid=(t // tm,), in_specs=[full] + [part] * np_ + [full], out_specs=[part] * np_ + [full],
        out_shape=[jax.ShapeDtypeStruct((t, w), F32)] * np_ + [jax.ShapeDtypeStruct((t, D_MODEL), BF16)],
        compiler_params=_params(("parallel",)), name=name,
    )(dao, *o_parts, proj)
    return outs[:np_], outs[np_]


def _loss_head(h2, gf, target, b, lp):
    d = h2.shape[1]
    nb = lp // BLOCK
    h3 = h2.reshape(b, lp, d)

    def body(h_ref, g_ref, t_ref, dh_ref, dg_ref, loss_ref):
        first = (pl.program_id(0) == 0) & (pl.program_id(1) == 0)

        @pl.when(first)
        def _():
            dg_ref[...] = jnp.zeros_like(dg_ref)
            loss_ref[...] = jnp.zeros_like(loss_ref)

        @pl.when(pl.program_id(1) == 0)
        def _():
            dh_ref[...] = jnp.zeros_like(dh_ref)

        @pl.when(pl.program_id(1) > 0)
        def _():
            x = h_ref[0]
            r = lax.rsqrt(jnp.mean(x * x, axis=-1, keepdims=True) + NORM_EPS)
            xr = x * r
            g = g_ref[...]
            diff = xr * g - t_ref[0]
            loss_ref[...] += 0.5 * jnp.sum(jnp.mean(diff * diff, axis=-1, keepdims=True))
            dy = diff * (1.0 / d)
            u = dy * g
            dh_ref[0] = r * (u - xr * jnp.mean(u * xr, axis=-1, keepdims=True))
            dg_ref[...] += jnp.sum(dy * xr, axis=0, keepdims=True)

    blk = pl.BlockSpec((1, BLOCK, d), lambda bi, n: (bi, n, 0))
    dh, dg, loss = pl.pallas_call(
        body, grid=(b, nb),
        in_specs=[blk, pl.BlockSpec((1, d), lambda bi, n: (0, 0)),
                  pl.BlockSpec((1, BLOCK, d), lambda bi, n: (bi, jnp.maximum(n - 1, 0), 0))],
        out_specs=[blk, pl.BlockSpec((1, d), lambda bi, n: (0, 0)), pl.BlockSpec((8, 128), lambda bi, n: (0, 0))],
        out_shape=[jax.ShapeDtypeStruct((b, lp, d), F32), jax.ShapeDtypeStruct((1, d), F32), jax.ShapeDtypeStruct((8, 128), F32)],
        compiler_params=_params(("arbitrary", "arbitrary")), name="loss_head",
    )(h3, gf, target)
    return dh.reshape(b * lp, d), dg, loss


def _iota2(shape, dim):
    return lax.broadcasted_iota(jnp.int32, shape, dim)


KEYS = 256


def _lo_lanes():
    return _iota2((1, BLOCK), 1) < HEAD


def _halves(x, lo):
    zero = jnp.zeros_like(x)
    return jnp.where(lo, x, zero), jnp.where(lo, zero, x)


def _rows_of_pair(a, b):
    return jnp.where(_iota2((BLOCK, 1), 0) < HEAD, a, b)


def _split_rows_t(x):
    xt = x.T
    first = _iota2(xt.shape, 0) < HEAD
    zero = jnp.zeros_like(xt)
    return jnp.concatenate([jnp.where(first, xt, zero), jnp.where(first, zero, xt)], axis=1).astype(BF16)


def _key_chunk(c, lp, t_idx, strict, key_axis):
    first = c * KEYS
    s0 = pl.multiple_of(jnp.minimum(first, lp - KEYS), BLOCK)
    s_idx = s0 + _iota2(t_idx.shape, key_axis)
    seen = (s_idx < t_idx) if strict else (s_idx <= t_idx)
    return s0, seen & (s_idx >= jnp.maximum(first, N_PAD))


def _split_dot(x, tri):
    hi = x.astype(BF16)
    lo = (x - hi.astype(F32)).astype(BF16)
    return jnp.dot(hi, tri, preferred_element_type=F32) + jnp.dot(lo, tri, preferred_element_type=F32)


def _stack_halves(x, lo):
    a, b = _halves(x, lo)
    return jnp.concatenate([a, b], axis=0)


def _pair(a, b, lo):
    return jnp.where(lo, a, b)


def _chunk_starts(lp):
    return [min(c * KEYS, lp - KEYS) for c in range(-(-lp // KEYS))]


def _put_rows(ref, r0, bq, a, b):
    for t in range(bq // BLOCK):
        part = slice(t * BLOCK, (t + 1) * BLOCK)
        ref[0, 0, r0 // BLOCK + t] = jnp.concatenate([a[:, part], b[:, part], jnp.zeros((6, BLOCK), F32)], axis=0)


def _get_rows(ref, r0, bq):
    return [jnp.concatenate([ref[0, 0, r0 // BLOCK + t, h:h + 1, :] for t in range(bq // BLOCK)], axis=1) for h in range(2)]


def _n_chunks(i):
    return (i + 2) // 2


QROWS = 512


def _for_query_tiles(nb, tile):
    per = QROWS // BLOCK

    def step(j, _):
        tile(pl.multiple_of(j * QROWS, QROWS), QROWS, (j + 1) * (QROWS // KEYS))
        return 0

    lax.fori_loop(0, nb // per, step, 0)
    for i in range(nb - nb % per, nb):
        tile(i * BLOCK, BLOCK, _n_chunks(i))


def _sb_scores(q_h, k, valid, after):
    z = _dot_nt(q_h, k) * (HEAD ** -0.5)
    sp = jnp.log(1.0 + jnp.exp(-jnp.abs(z)))
    lb = jnp.minimum(z, 0.0) - sp
    l1m_all = -jnp.maximum(z, 0.0) - sp
    l1m = jnp.where(valid, l1m_all, 0.0)
    return lb, l1m_all, l1m, _split_dot(l1m, after)


def _pair_stat_spec(nb):
    return pl.BlockSpec((1, 1, nb, 8, BLOCK), lambda bi, hp: (bi, hp, 0, 0, 0))


def _sb_fwd(proj3, name):
    b, lp, _ = proj3.shape
    nb = lp // BLOCK
    npair = SB_HEADS // 2

    def body(q_ref, k_ref, v_ref, o_ref, tot_ref):
        lo = _lo_lanes()
        after = (_iota2((KEYS, KEYS), 0) > _iota2((KEYS, KEYS), 1)).astype(BF16)

        def qtile(r0, bq, n):
            qs = _halves(q_ref[0, pl.ds(r0, bq), :].astype(BF16), lo)
            t_idx = r0 + _iota2((bq, KEYS), 0)

            def kchunk(cc, carry):
                cs, acc = carry[:2], carry[2]
                s0, valid = _key_chunk(n - 1 - cc, lp, t_idx, True, 1)
                k = k_ref[0, pl.ds(s0, KEYS), :].astype(BF16)
                a_s, new = [], []
                for h in range(2):
                    lb, _, l1m, suf = _sb_scores(qs[h], k, valid, after)
                    a_s.append(jnp.where(valid, jnp.exp(lb + suf + cs[h]), 0.0).astype(BF16))
                    new.append(cs[h] + jnp.sum(l1m, axis=1, keepdims=True))
                v_bd = _stack_halves(v_ref[0, pl.ds(s0, KEYS), :].astype(BF16), lo)
                return (*new, acc + jnp.dot(jnp.concatenate(a_s, axis=1), v_bd, preferred_element_type=F32))

            zero = jnp.zeros((bq, 1), F32)
            c_a, c_b, acc = lax.fori_loop(0, n, kchunk, (zero, zero, jnp.zeros((bq, BLOCK), F32)))
            o_ref[0, pl.ds(r0, bq), :] = acc
            tot_ref[0, pl.ds(r0, bq), :] = jnp.broadcast_to(_pair(c_a, c_b, lo), (bq, BLOCK))

        _for_query_tiles(nb, qtile)

    def col(first):
        return pl.BlockSpec((1, lp, 2 * HEAD), lambda bi, hp: (bi, 0, first // (2 * HEAD) + hp))

    shp = jax.ShapeDtypeStruct((b, lp, SB_HEADS * HEAD), F32)
    return pl.pallas_call(
        body, grid=(b, npair), in_specs=[col(EV_Q), col(EV_K), col(EV_V)], out_specs=[col(0), col(0)], out_shape=[shp, shp],
        compiler_params=_params(("parallel", "parallel")), name=name,
    )(proj3, proj3, proj3)


def _sb_bwd(proj3, tot, do, name):
    b, lp, _ = proj3.shape
    nb = lp // BLOCK
    npair = SB_HEADS // 2

    def body(q_ref, k_ref, v_ref, tot_ref, do_ref, dq_ref, dk_ref, dv_ref):
        lo = _lo_lanes()
        after = (_iota2((KEYS, KEYS), 0) > _iota2((KEYS, KEYS), 1)).astype(BF16)
        before = (_iota2((KEYS, KEYS), 0) < _iota2((KEYS, KEYS), 1)).astype(BF16)
        dk_ref[...] = jnp.zeros_like(dk_ref)
        dv_ref[...] = jnp.zeros_like(dv_ref)

        def qtile(r0, bq, n):
            rows = pl.ds(r0, bq)
            qs = _halves(q_ref[0, rows, :].astype(BF16), lo)
            dos = _halves(do_ref[0, rows, :].astype(BF16), lo)
            tot_i = tot_ref[0, rows, :]
            tots = (tot_i[:, 0:1], tot_i[:, HEAD:HEAD + 1])
            q_st, do_st = jnp.concatenate(qs, axis=0), jnp.concatenate(dos, axis=0)
            t_idx = r0 + _iota2((bq, KEYS), 0)

            def kchunk(c, carry):
                s0, valid = _key_chunk(c, lp, t_idx, True, 1)
                keys = pl.ds(s0, KEYS)
                k = k_ref[0, keys, :].astype(BF16)
                v = v_ref[0, keys, :].astype(BF16)
                a_s, dzs, new = [], [], []
                for h in range(2):
                    left, pre = carry[2 * h], carry[2 * h + 1]
                    lb, l1m_all, l1m, suf = _sb_scores(qs[h], k, valid, after)
                    here = jnp.sum(l1m, axis=1, keepdims=True)
                    a = jnp.where(valid, jnp.exp(lb + suf + (tots[h] - left - here)), 0.0)
                    w = a * _dot_nt(dos[h], v)
                    dz = jnp.where(valid, w * jnp.exp(l1m_all) - (pre + _split_dot(w, before)) * jnp.exp(lb), 0.0) * (HEAD ** -0.5)
                    new += [left + here, pre + jnp.sum(w, axis=1, keepdims=True)]
                    a_s.append(a.astype(BF16))
                    dzs.append(dz.astype(BF16))
                dk_ref[0, keys, :] += _dot_tn(jnp.concatenate(dzs, axis=0), q_st)
                dv_ref[0, keys, :] += _dot_tn(jnp.concatenate(a_s, axis=0), do_st)
                dq = carry[4] + jnp.dot(jnp.concatenate(dzs, axis=1), _stack_halves(k, lo), preferred_element_type=F32)
                return (*new, dq)

            zero = jnp.zeros((bq, 1), F32)
            out = lax.fori_loop(0, n, kchunk, (zero, zero, zero, zero, jnp.zeros((bq, BLOCK), F32)))
            dq_ref[0, rows, :] = out[4]

        _for_query_tiles(nb, qtile)

    def col(first):
        return pl.BlockSpec((1, lp, 2 * HEAD), lambda bi, hp: (bi, 0, first // (2 * HEAD) + hp))

    shp = jax.ShapeDtypeStruct((b, lp, SB_HEADS * HEAD), F32)
    return pl.pallas_call(
        body, grid=(b, npair), in_specs=[col(EV_Q), col(EV_K), col(EV_V), col(0), col(0)], out_specs=[col(0)] * 3, out_shape=[shp] * 3,
        compiler_params=_params(("parallel", "parallel")), name=name,
    )(proj3, proj3, proj3, tot, do)


def _rope_tables(lp):
    half = MLA_ROPE // 2
    pos = (np.arange(lp) - N_PAD).astype(np.float32)
    inv = jnp.asarray(ROPE_BASE, F32) ** (-jnp.arange(half, dtype=F32) / half)
    ang = jnp.asarray(pos)[:, None] * inv[None, :]
    cos, sin = jnp.cos(ang), jnp.sin(ang)
    zeros = lambda n: jnp.zeros((lp, n), F32)
    c = jnp.concatenate([jnp.ones((lp, MLA_NOPE), F32), cos, cos, zeros(32)], axis=1)
    s1 = jnp.concatenate([zeros(MLA_NOPE), -sin, zeros(half), zeros(32)], axis=1)
    s2 = jnp.concatenate([zeros(MLA_NOPE), zeros(half), sin, zeros(32)], axis=1)
    return c, s1, s2


def _rope(x, c, s1, s2):
    half = MLA_ROPE // 2
    return x * c + pltpu.roll(x, BLOCK - half, 1) * s1 + pltpu.roll(x, half, 1) * s2


def _rope_t(dy, c, s1, s2):
    half = MLA_ROPE // 2
    return dy * c + pltpu.roll(dy * s1, half, 1) + pltpu.roll(dy * s2, BLOCK - half, 1)


def _rms_rows(x, g):
    r = lax.rsqrt(jnp.mean(x * x, axis=-1, keepdims=True) + NORM_EPS)
    return x * r, r


def _mla_prep_fwd(proj3, gq, gkv, wq, wk, wv, tabs, name):
    b, lp, _ = proj3.shape
    nb = lp // BLOCK
    hw = MLA_HEADS * BLOCK

    def body(cq_ref, ckv_ref, kr_ref, gq_ref, gkv_ref, wq_ref, wk_ref, wv_ref, c_ref, s1_ref, s2_ref, qf_ref, kf_ref, v_ref):
        c, s1, s2 = c_ref[...], s1_ref[...], s2_ref[...]
        xq, _ = _rms_rows(cq_ref[0], None)
        qh = _dot(xq * gq_ref[...], wq_ref[...])
        xk, _ = _rms_rows(ckv_ref[0], None)
        ckv_n = xk * gkv_ref[...]
        kv = _dot(ckv_n, wk_ref[...])
        v_ref[0] = _dot(ckv_n, wv_ref[...]).astype(v_ref.dtype)
        kr = _rope(kr_ref[0], c, s1, s2)
        for h in range(MLA_HEADS):
            ls = slice(h * BLOCK, (h + 1) * BLOCK)
            qf_ref[0, :, ls] = _rope(qh[:, ls], c, s1, s2).astype(qf_ref.dtype)
            kf_ref[0, :, ls] = (kv[:, ls] + kr).astype(kf_ref.dtype)

    def col(first, width):
        return pl.BlockSpec((1, BLOCK, width), lambda bi, n: (bi, n, first // width))

    def whole(a):
        return pl.BlockSpec(a.shape, lambda bi, n: (0,) * a.ndim)

    tab = pl.BlockSpec((BLOCK, BLOCK), lambda bi, n: (n, 0))
    return pl.pallas_call(
        body, grid=(b, nb),
        in_specs=[col(EV_CQ, MLA_Q_LORA), col(EV_CKV, MLA_KV_LORA), col(EV_KR, BLOCK), whole(gq), whole(gkv), whole(wq), whole(wk),
                  whole(wv), tab, tab, tab],
        out_specs=[col(0, hw), col(0, hw), col(0, MLA_HEADS * HEAD)],
        out_shape=[jax.ShapeDtypeStruct((b, lp, hw), BF16), jax.ShapeDtypeStruct((b, lp, hw), BF16),
                   jax.ShapeDtypeStruct((b, lp, MLA_HEADS * HEAD), BF16)],
        compiler_params=_params(("parallel", "parallel")), name=name,
    )(proj3, proj3, proj3, gq, gkv, wq, wk, wv, *tabs)


def _mla_prep_bwd(proj3, gq, gkv, wq, wk, wv, tabs, dqf, dkf, dv, name):
    b, lp, _ = proj3.shape
    nb = lp // BLOCK
    hw = MLA_HEADS * BLOCK

    def body(cq_ref, ckv_ref, gq_ref, gkv_ref, wq_ref, wk_ref, wv_ref, c_ref, s1_ref, s2_ref, dqf_ref, dkf_ref, dv_ref,
             dcq_ref, dckv_ref, dkr_ref, dwq_ref, dwk_ref, dwv_ref, dgq_ref, dgkv_ref, dqh):
        @pl.when((pl.program_id(0) == 0) & (pl.program_id(1) == 0))
        def _():
            for r in (dwq_ref, dwk_ref, dwv_ref, dgq_ref, dgkv_ref):
                r[...] = jnp.zeros_like(r)

        c, s1, s2 = c_ref[...], s1_ref[...], s2_ref[...]
        dkr = jnp.zeros((BLOCK, BLOCK), F32)
        for h in range(MLA_HEADS):
            ls = slice(h * BLOCK, (h + 1) * BLOCK)
            dqh[:, ls] = _rope_t(dqf_ref[0, :, ls].astype(F32), c, s1, s2).astype(dqh.dtype)
            dkr = dkr + dkf_ref[0, :, ls].astype(F32)
        dkr_ref[0] = _rope_t(dkr, c, s1, s2).astype(dkr_ref.dtype)

        def norm_bwd(x, g, dy, dg_ref):
            xr, r = _rms_rows(x, None)
            u = dy * g
            dg_ref[...] += jnp.sum(dy * xr, axis=0, keepdims=True)
            return r * (u - xr * jnp.mean(u * xr, axis=-1, keepdims=True))

        xq, _ = _rms_rows(cq_ref[0], None)
        cq_n = xq * gq_ref[...]
        dwq_ref[...] += _dot_tn(cq_n, dqh[...])
        dcq_ref[0] = norm_bwd(cq_ref[0], gq_ref[...], _dot_nt(dqh[...], wq_ref[...]), dgq_ref).astype(dcq_ref.dtype)
        xk, _ = _rms_rows(ckv_ref[0], None)
        ckv_n = xk * gkv_ref[...]
        dkf_, dv_ = dkf_ref[0], dv_ref[0]
        dwk_ref[...] += _dot_tn(ckv_n, dkf_)
        dwv_ref[...] += _dot_tn(ckv_n, dv_)
        dckv_n = _dot_nt(dkf_, wk_ref[...]) + _dot_nt(dv_, wv_ref[...])
        dckv_ref[0] = norm_bwd(ckv_ref[0], gkv_ref[...], dckv_n, dgkv_ref).astype(dckv_ref.dtype)

    def col(first, width):
        return pl.BlockSpec((1, BLOCK, width), lambda bi, n: (bi, n, first // width))

    def whole(a):
        return pl.BlockSpec(a.shape, lambda bi, n: (0,) * len(a.shape))

    tab = pl.BlockSpec((BLOCK, BLOCK), lambda bi, n: (n, 0))
    acc_shapes = [jax.ShapeDtypeStruct(a.shape, F32) for a in (wq, wk, wv, gq, gkv)]
    return pl.pallas_call(
        body, grid=(b, nb),
        in_specs=[col(EV_CQ, MLA_Q_LORA), col(EV_CKV, MLA_KV_LORA), whole(gq), whole(gkv), whole(wq), whole(wk), whole(wv), tab, tab, tab,
                  col(0, hw), col(0, hw), col(0, MLA_HEADS * HEAD)],
        out_specs=[col(0, MLA_Q_LORA), col(0, MLA_KV_LORA), col(0, BLOCK)] + [whole(a) for a in acc_shapes],
        out_shape=[jax.ShapeDtypeStruct((b, lp, MLA_Q_LORA), BF16), jax.ShapeDtypeStruct((b, lp, MLA_KV_LORA), BF16),
                   jax.ShapeDtypeStruct((b, lp, BLOCK), BF16)] + acc_shapes,
        scratch_shapes=[pltpu.VMEM((BLOCK, hw), BF16)],
        compiler_params=_params(("arbitrary", "arbitrary")), name=name,
    )(proj3, proj3, gq, gkv, wq, wk, wv, *tabs, dqf, dkf, dv)


def _mla_fwd(qf, kf, v, name):
    b, lp, _ = qf.shape
    nb = lp // BLOCK
    npair = MLA_HEADS // 2
    scale = (MLA_NOPE + MLA_ROPE) ** -0.5
    starts = _chunk_starts(lp)

    def body(q_ref, k_ref, v_ref, o_ref, lse_ref, vt_ref):
        for c, s0 in enumerate(starts):
            vt_ref[c] = _split_rows_t(v_ref[0, s0:s0 + KEYS, :].astype(F32))

        def qtile(r0, bq, n):
            qs = [q_ref[0, pl.ds(r0, bq), h * BLOCK:(h + 1) * BLOCK] for h in range(2)]
            t_idx = r0 + _iota2((KEYS, bq), 1)

            def kchunk(c, carry):
                stats, acc = carry[:4], carry[4]
                s0, valid = _key_chunk(c, lp, t_idx, False, 0)
                ps, new, alphas = [], [], []
                for h in range(2):
                    m, l = stats[2 * h], stats[2 * h + 1]
                    s = jnp.where(valid, _dot_nt(k_ref[0, pl.ds(s0, KEYS), h * BLOCK:(h + 1) * BLOCK], qs[h]) * scale, NEG)
                    m_new = jnp.maximum(m, jnp.max(s, axis=0, keepdims=True))
                    p = jnp.where(valid, jnp.exp(s - m_new), 0.0)
                    alpha = jnp.exp(m - m_new)
                    new += [m_new, alpha * l + jnp.sum(p, axis=0, keepdims=True)]
                    alphas.append(alpha)
                    ps.append(p.astype(BF16))
                pv = jnp.dot(vt_ref[c], jnp.concatenate(ps, axis=0), preferred_element_type=F32)
                return (*new, _rows_of_pair(alphas[0], alphas[1]) * acc + pv)

            neg, zero = jnp.full((1, bq), NEG, F32), jnp.zeros((1, bq), F32)
            m_a, l_a, m_b, l_b, acc = lax.fori_loop(0, n, kchunk, (neg, zero, neg, zero, jnp.zeros((BLOCK, bq), F32)))
            safe = [jnp.where(l > 0.0, l, 1.0) for l in (l_a, l_b)]
            o_ref[0, pl.ds(r0, bq), :] = (acc / _rows_of_pair(safe[0], safe[1])).T
            lse = [jnp.where(l > 0.0, m + jnp.log(sf), 0.0) for m, l, sf in ((m_a, l_a, safe[0]), (m_b, l_b, safe[1]))]
            _put_rows(lse_ref, r0, bq, lse[0], lse[1])

        _for_query_tiles(nb, qtile)

    wide = pl.BlockSpec((1, lp, 2 * BLOCK), lambda bi, hp: (bi, 0, hp))
    thin = pl.BlockSpec((1, lp, 2 * HEAD), lambda bi, hp: (bi, 0, hp))
    return pl.pallas_call(
        body, grid=(b, npair), in_specs=[wide, wide, thin], out_specs=[thin, _pair_stat_spec(nb)],
        out_shape=[jax.ShapeDtypeStruct((b, lp, MLA_HEADS * HEAD), F32), jax.ShapeDtypeStruct((b, npair, nb, 8, BLOCK), F32)],
        scratch_shapes=[pltpu.VMEM((len(starts), BLOCK, 2 * KEYS), BF16)],
        compiler_params=_params(("parallel", "parallel")), name=name,
    )(qf, kf, v)


def _mla_bwd(qf, kf, v, o, lse, do, name):
    b, lp, _ = qf.shape
    nb = lp // BLOCK
    npair = MLA_HEADS // 2
    scale = (MLA_NOPE + MLA_ROPE) ** -0.5

    starts = _chunk_starts(lp)

    def body(q_ref, k_ref, v_ref, o_ref, lse_ref, do_ref, dq_ref, dk_ref, dv_ref, kt_ref):
        lo = _lo_lanes()
        dk_ref[...] = jnp.zeros_like(dk_ref)
        dv_ref[...] = jnp.zeros_like(dv_ref)
        for c, s0 in enumerate(starts):
            for h in range(2):
                kt_ref[c, h] = k_ref[0, s0:s0 + KEYS, h * BLOCK:(h + 1) * BLOCK].astype(F32).T.astype(BF16)

        def qtile(r0, bq, n):
            rows = pl.ds(r0, bq)
            qs = [q_ref[0, rows, h * BLOCK:(h + 1) * BLOCK] for h in range(2)]
            do_i = do_ref[0, rows, :]
            dos = _halves(do_i.astype(BF16), lo)
            do_st = jnp.concatenate(dos, axis=0)
            both = (do_i * o_ref[0, rows, :]).T
            dsum = (jnp.sum(both[:HEAD], axis=0, keepdims=True), jnp.sum(both[HEAD:], axis=0, keepdims=True))
            lses = _get_rows(lse_ref, r0, bq)
            t_idx = r0 + _iota2((KEYS, bq), 1)

            def kchunk(c, dqts):
                s0, valid = _key_chunk(c, lp, t_idx, False, 0)
                keys = pl.ds(s0, KEYS)
                v_c = v_ref[0, keys, :]
                ps, out = [], []
                for h in range(2):
                    lanes = slice(h * BLOCK, (h + 1) * BLOCK)
                    s = jnp.where(valid, _dot_nt(k_ref[0, keys, lanes], qs[h]) * scale, NEG)
                    p = jnp.where(valid, jnp.exp(s - lses[h]), 0.0)
                    ds = (p * (_dot_nt(v_c, dos[h]) - dsum[h]) * scale).astype(BF16)
                    dk_ref[0, keys, lanes] += jnp.dot(ds, qs[h], preferred_element_type=F32)
                    out.append(dqts[h] + jnp.dot(kt_ref[c, h], ds, preferred_element_type=F32))
                    ps.append(p.astype(BF16))
                dv_ref[0, keys, :] += jnp.dot(jnp.concatenate(ps, axis=1), do_st, preferred_element_type=F32)
                return tuple(out)

            zero = jnp.zeros((BLOCK, bq), F32)
            dq_a, dq_b = lax.fori_loop(0, n, kchunk, (zero, zero))
            dq_ref[0, rows, 0:BLOCK] = dq_a.T
            dq_ref[0, rows, BLOCK:2 * BLOCK] = dq_b.T

        _for_query_tiles(nb, qtile)

    wide = pl.BlockSpec((1, lp, 2 * BLOCK), lambda bi, hp: (bi, 0, hp))
    thin = pl.BlockSpec((1, lp, 2 * HEAD), lambda bi, hp: (bi, 0, hp))
    return pl.pallas_call(
        body, grid=(b, npair), in_specs=[wide, wide, thin, thin, _pair_stat_spec(nb), thin], out_specs=[wide, wide, thin],
        out_shape=[jax.ShapeDtypeStruct(qf.shape, F32), jax.ShapeDtypeStruct(qf.shape, F32), jax.ShapeDtypeStruct(v.shape, F32)],
        scratch_shapes=[pltpu.VMEM((len(starts), 2, BLOCK, KEYS), BF16)],
        compiler_params=_params(("parallel", "parallel")), name=name,
    )(qf, kf, v, o, lse, do)


def _swa_keys(k_ref, v_ref, n, kv):
    prev = jnp.maximum(n - 1, 0)
    rows = lambda blk: pl.ds(pl.multiple_of(blk * BLOCK, BLOCK), BLOCK)
    mine = (_iota2((1, BLOCK), 1) >= HEAD).astype(jnp.int32) == kv

    def both_halves(ref):
        x = jnp.concatenate([ref[0, rows(prev), :], ref[0, rows(n), :], ref[0, 0:BLOCK, :]], axis=0)
        return jnp.where(mine, x, pltpu.roll(x, HEAD, 1))

    slot = _iota2((3 * BLOCK, BLOCK), 0)
    loc = slot % BLOCK
    s_idx = jnp.where(slot < BLOCK, (n - 1) * BLOCK + loc, jnp.where(slot < 2 * BLOCK, n * BLOCK + loc, loc))
    dist = n * BLOCK + _iota2((3 * BLOCK, BLOCK), 1) - s_idx
    band = (slot < 2 * BLOCK) & (dist >= 0) & (dist < SWA_WINDOW) & (s_idx >= BLOCK)
    meta = (slot >= 2 * BLOCK) & (s_idx >= N_PAD) & (dist >= 0)
    return both_halves(k_ref), both_halves(v_ref), band | meta, dist.astype(F32), prev


def _swa_probs(q_h, kdup, valid, dist, head, sink_ref):
    slope = jnp.exp(jnp.full((1, 1), -8.0 * math.log(2.0) / SWA_HEADS, F32) * (head + 1).astype(F32))
    s = jnp.where(valid, _dot_nt(kdup, q_h) * (HEAD ** -0.5) - slope * dist, NEG)
    sink = sink_ref[pl.ds(head, 1), 0:1]
    m = jnp.maximum(jnp.max(s, axis=0, keepdims=True), sink)
    e = jnp.where(valid, jnp.exp(s - m), 0.0)
    es = jnp.exp(sink - m)
    inv = 1.0 / (jnp.sum(e, axis=0, keepdims=True) + es)
    return e * inv, es * inv


SWA_PAIRS = SWA_HEADS // SWA_KV_HEADS // 2
SWA_GROUP = SWA_PAIRS * 2 * HEAD


def _swa_specs(b, lp):
    nb = lp // BLOCK
    qcol = lambda first: pl.BlockSpec((1, BLOCK, SWA_GROUP), lambda bi, kv, n: (bi, n, first // SWA_GROUP + kv))
    kcol = lambda first: pl.BlockSpec((1, lp, BLOCK), lambda bi, kv, n: (bi, 0, first // BLOCK))
    sink = pl.BlockSpec((SWA_HEADS, BLOCK), lambda bi, kv, n: (0, 0))
    return (b, SWA_KV_HEADS, nb), qcol, kcol, sink


def _swa_fwd(proj3, sinks, name):
    b, lp, _ = proj3.shape
    grid, qcol, kcol, sink = _swa_specs(b, lp)

    def body(q_ref, k_ref, v_ref, sink_ref, o_ref):
        kv, n = pl.program_id(1), pl.program_id(2)
        lo = _lo_lanes()
        kdup, vdup, valid, dist, _ = _swa_keys(k_ref, v_ref, n, kv)
        kdup = kdup.astype(BF16)
        vt = _split_rows_t(vdup)
        for p in range(SWA_PAIRS):
            lanes = slice(p * BLOCK, (p + 1) * BLOCK)
            qs = _halves(q_ref[0, :, lanes].astype(BF16), lo)
            probs = [_swa_probs(qs[hh], kdup, valid, dist, (kv * SWA_PAIRS + p) * 2 + hh, sink_ref)[0].astype(BF16) for hh in range(2)]
            o_ref[0, :, lanes] = jnp.dot(vt, jnp.concatenate(probs, axis=0), preferred_element_type=F32).T

    return pl.pallas_call(
        body, grid=grid, in_specs=[qcol(OD_Q), kcol(OD_K), kcol(OD_V), sink], out_specs=qcol(0),
        out_shape=jax.ShapeDtypeStruct((b, lp, SWA_HEADS * HEAD), F32),
        compiler_params=_params(("parallel", "parallel", "parallel")), name=name,
    )(proj3, proj3, proj3, sinks)


def _swa_bwd(proj3, sinks, do, name):
    b, lp, _ = proj3.shape
    nb = lp // BLOCK
    grid, qcol, kcol, sink = _swa_specs(b, lp)

    def body(q_ref, k_ref, v_ref, sink_ref, do_ref, dq_ref, dk_ref, dv_ref, dsink_ref, dk_acc, dv_acc):
        kv, n = pl.program_id(1), pl.program_id(2)

        @pl.when((n == 0) & (pl.program_id(0) == 0) & (kv == 0))
        def _():
            dsink_ref[...] = jnp.zeros_like(dsink_ref)

        @pl.when(n == 0)
        def _():
            dk_acc[...] = jnp.zeros_like(dk_acc)
            dv_acc[...] = jnp.zeros_like(dv_acc)

        lo = _lo_lanes()
        kdup, vdup, valid, dist, prev = _swa_keys(k_ref, v_ref, n, kv)
        kt = _split_rows_t(kdup)
        kdup, vdup = kdup.astype(BF16), vdup.astype(BF16)
        dkc = jnp.zeros((3 * BLOCK, BLOCK), F32)
        dvc = jnp.zeros((3 * BLOCK, BLOCK), F32)
        for p in range(SWA_PAIRS):
            lanes = slice(p * BLOCK, (p + 1) * BLOCK)
            qs = _halves(q_ref[0, :, lanes].astype(BF16), lo)
            dos = _halves(do_ref[0, :, lanes].astype(BF16), lo)
            dss, prs = [], []
            for hh in range(2):
                head = (kv * SWA_PAIRS + p) * 2 + hh
                pr, ps = _swa_probs(qs[hh], kdup, valid, dist, head, sink_ref)
                dp = _dot_nt(vdup, dos[hh])
                dsum = jnp.sum(pr * dp, axis=0, keepdims=True)
                dsink_ref[pl.ds(head, 1), :] += jnp.broadcast_to(-jnp.sum(ps * dsum, axis=1, keepdims=True), (1, BLOCK))
                dss.append((pr * (dp - dsum) * (HEAD ** -0.5)).astype(BF16))
                prs.append(pr.astype(BF16))
            dq_ref[0, :, lanes] = jnp.dot(kt, jnp.concatenate(dss, axis=0), preferred_element_type=F32).T
            dkc = dkc + jnp.dot(jnp.concatenate(dss, axis=1), jnp.concatenate(qs, axis=0), preferred_element_type=F32)
            dvc = dvc + jnp.dot(jnp.concatenate(prs, axis=1), jnp.concatenate(dos, axis=0), preferred_element_type=F32)
        rows = lambda blk: pl.ds(pl.multiple_of(blk * BLOCK, BLOCK), BLOCK)
        for part, r in enumerate((rows(prev), rows(n), slice(0, BLOCK))):
            dk_acc[r, :] += dkc[part * BLOCK:(part + 1) * BLOCK]
            dv_acc[r, :] += dvc[part * BLOCK:(part + 1) * BLOCK]

        for acc, ref in ((dk_acc, dk_ref), (dv_acc, dv_ref)):
            @pl.when((n == nb - 1) & (kv == 0))
            def _():
                x = acc[...]
                ref[0] = x + pltpu.roll(x, HEAD, 1)

            @pl.when((n == nb - 1) & (kv == 1))
            def _():
                x = acc[...]
                ref[0] = jnp.where(lo, ref[0], x + pltpu.roll(x, HEAD, 1))

    kvout = pl.BlockSpec((1, lp, BLOCK), lambda bi, kv, n: (bi, 0, 0))
    kvshape = jax.ShapeDtypeStruct((b, lp, BLOCK), F32)
    return pl.pallas_call(
        body, grid=grid, in_specs=[qcol(OD_Q), kcol(OD_K), kcol(OD_V), sink, qcol(0)], out_specs=[qcol(0), kvout, kvout, sink],
        out_shape=[jax.ShapeDtypeStruct((b, lp, SWA_HEADS * HEAD), F32), kvshape, kvshape, jax.ShapeDtypeStruct((SWA_HEADS, BLOCK), F32)],
        scratch_shapes=[pltpu.VMEM((lp, BLOCK), F32), pltpu.VMEM((lp, BLOCK), F32)],
        compiler_params=_params(("arbitrary", "arbitrary", "arbitrary")), name=name,
    )(proj3, proj3, proj3, sinks, do)


def _kernel_weights(ev_w_in, ev_w_uq, ev_w_ukv, od_w_in):
    zeros = lambda r, c: jnp.zeros((r, c), ev_w_in.dtype)
    q_sb, k_sb, v_sb, g_sb, c_q, c_kv, k_r, g_mla = jnp.split(ev_w_in, [512, 1024, 1536, 2048, 2304, 2432, 2464], axis=1)
    w0 = jnp.concatenate([g_sb, g_mla, q_sb, k_sb, v_sb, c_q, c_kv, zeros(D_MODEL, MLA_NOPE), k_r, zeros(D_MODEL, 32)], axis=1)
    uq = ev_w_uq.reshape(MLA_Q_LORA, MLA_HEADS, MLA_NOPE + MLA_ROPE)
    wq = jnp.pad(uq, ((0, 0), (0, 0), (0, BLOCK - MLA_NOPE - MLA_ROPE))).reshape(MLA_Q_LORA, MLA_HEADS * BLOCK)
    ukv = ev_w_ukv.reshape(MLA_KV_LORA, MLA_HEADS, BLOCK)
    wk = jnp.pad(ukv[:, :, :MLA_NOPE], ((0, 0), (0, 0), (0, BLOCK - MLA_NOPE))).reshape(MLA_KV_LORA, MLA_HEADS * BLOCK)
    wv = ukv[:, :, MLA_NOPE:].reshape(MLA_KV_LORA, MLA_HEADS * HEAD)
    q, k, v, g = jnp.split(od_w_in, [1024, 1152, 1280], axis=1)
    w1 = jnp.concatenate([g, q, k, v], axis=1)
    return w0, wq, wk, wv, w1


def _original_grads(dw0, dwq, dwk, dwv, dw1):
    sl = lambda a, first, n: a[:, first:first + n]
    d_ev_w_in = jnp.concatenate([sl(dw0, EV_Q, 512), sl(dw0, EV_K, 512), sl(dw0, EV_V, 512), sl(dw0, EV_G, 512), sl(dw0, EV_CQ, 256),
                                 sl(dw0, EV_CKV, 128), sl(dw0, EV_KR + MLA_NOPE, MLA_ROPE), sl(dw0, EV_G + 512, 512)], axis=1)
    d_uq = dwq.reshape(MLA_Q_LORA, MLA_HEADS, BLOCK)[:, :, :MLA_NOPE + MLA_ROPE].reshape(MLA_Q_LORA, -1)
    d_ukv = jnp.concatenate([dwk.reshape(MLA_KV_LORA, MLA_HEADS, BLOCK)[:, :, :MLA_NOPE], dwv.reshape(MLA_KV_LORA, MLA_HEADS, HEAD)],
                            axis=2).reshape(MLA_KV_LORA, -1)
    d_od_w_in = jnp.concatenate([sl(dw1, OD_Q, 1024), sl(dw1, OD_K, 128), sl(dw1, OD_V, 128), sl(dw1, OD_G, 1024)], axis=1)
    return d_ev_w_in, d_uq, d_ukv, d_od_w_in


def _meta_rows_sum(dh0_3):
    b, _, d = dh0_3.shape

    def body(x_ref, o_ref):
        acc = x_ref[0, N_PAD:BLOCK, :]
        for i in range(1, b):
            acc = acc + x_ref[i, N_PAD:BLOCK, :]
        o_ref[...] = acc

    return pl.pallas_call(
        body, grid=(1,), in_specs=[pl.BlockSpec((b, BLOCK, d), lambda i: (0, 0, 0))], out_specs=pl.BlockSpec((N_META, d), lambda i: (0, 0)),
        out_shape=jax.ShapeDtypeStruct((N_META, d), F32), compiler_params=_params(("arbitrary",)), name="meta_rows_sum",
    )(dh0_3)


def _local_step(x, meta, norm_g, final_g, gq, gkv, sinks, target, ev_w_in, ev_w_uq, ev_w_ukv, wo0, od_w_in, wo1):
    b, seq, d = x.shape
    lp = seq + BLOCK
    t = b * lp
    w0, wq, wk, wv, w1 = _kernel_weights(ev_w_in, ev_w_uq, ev_w_ukv, od_w_in)
    h0 = jnp.concatenate([jnp.zeros((b, N_PAD, d), F32), jnp.broadcast_to(meta[None], (b, N_META, d)), x], axis=1).reshape(t, d)
    tabs = _rope_tables(lp)
    g0, g1 = norm_g[0:1], norm_g[1:2]

    hn0 = _rms_fwd(h0, g0, "norm0")
    proj0 = _mm(hn0, w0, "inproj0")
    p0 = proj0.reshape(b, lp, EV_N)
    o_sb, sb_tot = _sb_fwd(p0, "sb_fwd")
    qf, kf, v = _mla_prep_fwd(p0, gq, gkv, wq, wk, wv, tabs, "mla_prep_fwd")
    o_mla, lse = _mla_fwd(qf, kf, v, "mla_fwd")
    o0 = [o_sb.reshape(t, -1), o_mla.reshape(t, -1)]
    ao0 = _gate_fwd(o0, proj0, "gate0")
    h1 = _mm(ao0, wo0, "outproj0", res=h0)

    hn1 = _rms_fwd(h1, g1, "norm1")
    proj1 = _mm(hn1, w1, "inproj1")
    p1 = proj1.reshape(b, lp, OD_N)
    sinks_b = jnp.broadcast_to(sinks.reshape(SWA_HEADS, 1), (SWA_HEADS, BLOCK))
    o1 = _swa_fwd(p1, sinks_b, "swa_fwd").reshape(t, -1)
    ao1 = _gate_fwd([o1], proj1, "gate1")
    h2 = _mm(ao1, wo1, "outproj1", res=h1)

    dh2, d_final_g, loss = _loss_head(h2, final_g.reshape(1, d), target, b, lp)

    d_wo1 = _mm_tn(ao1, dh2, "d_wo1")
    dao1 = _mm_nt(dh2, wo1, "d_ao1")
    (do1,), dg1 = _gate_bwd(dao1, [o1], proj1, "gate1_bwd")
    dq1, dk4, dv4, d_sinks = _swa_bwd(p1, sinks_b, do1.reshape(b, lp, -1), "swa_bwd")
    unheads = lambda a: a.reshape(t, SWA_KV_HEADS * HEAD).astype(BF16)
    dproj1 = jnp.concatenate([dg1, dq1.reshape(t, -1).astype(BF16), unheads(dk4), unheads(dv4)], axis=1)
    d_w1 = _mm_tn(hn1, dproj1, "d_w1")
    dhn1 = _mm_nt(dproj1, w1, "d_hn1")
    dh1, d_g1 = _rms_bwd(h1, g1, dhn1, dh2, "norm1_bwd")

    d_wo0 = _mm_tn(ao0, dh1, "d_wo0")
    dao0 = _mm_nt(dh1, wo0, "d_ao0")
    (do_sb, do_mla), dg0 = _gate_bwd(dao0, o0, proj0, "gate0_bwd")
    dq_sb, dk_sb, dv_sb = _sb_bwd(p0, sb_tot, do_sb.reshape(b, lp, -1), "sb_bwd")
    dqf, dkf, dv = _mla_bwd(qf, kf, v, o_mla, lse, do_mla.reshape(b, lp, -1), "mla_bwd")
    dcq, dckv, dkr, d_wq, d_wk, d_wv, d_gq, d_gkv = _mla_prep_bwd(p0, gq, gkv, wq, wk, wv, tabs, dqf, dkf, dv, "mla_prep_bwd")
    flat = lambda a: a.reshape(t, -1).astype(BF16)
    dproj0 = jnp.concatenate([dg0, flat(dq_sb), flat(dk_sb), flat(dv_sb), flat(dcq), flat(dckv), flat(dkr)], axis=1)
    d_w0 = _mm_tn(hn0, dproj0, "d_w0")
    dhn0 = _mm_nt(dproj0, w0, "d_hn0")
    dh0, d_g0 = _rms_bwd(h0, g0, dhn0, dh1, "norm0_bwd")
    dh0 = dh0.reshape(b, lp, d)

    d_ev_w_in, d_uq, d_ukv, d_od_w_in = _original_grads(d_w0, d_wq, d_wk, d_wv, d_w1)
    grads = dict(meta=_meta_rows_sum(dh0), norm_g=jnp.concatenate([d_g0, d_g1], axis=0), final_g=d_final_g.reshape(d),
                 ev_w_in=d_ev_w_in, ev_q_norm_g=d_gq, ev_kv_norm_g=d_gkv, ev_w_uq=d_uq, ev_w_ukv=d_ukv, ev_w_out=d_wo0,
                 od_w_in=d_od_w_in, od_sinks=d_sinks[:, 0].reshape(1, SWA_HEADS), od_w_out=d_wo1)
    return loss, dh0[:, BLOCK:], grads


MESH = pl.DeviceIdType.MESH
ANY = pl.BlockSpec(memory_space=pl.ANY)


def _place():
    return lax.axis_index("x"), lax.axis_index("y"), lax.axis_index("c")


def _other_chips(x, y):
    return [(1 - x, y), (x, 1 - y), (1 - x, 1 - y)]


def _with_own_slot(slots, own):
    me = 2 * lax.axis_index("x") + lax.axis_index("y")
    return lax.dynamic_update_slice(slots, own[None], (me,) + (0,) * own.ndim)


def _gather_weights(packs, meta, name):
    n = len(packs)

    def body(*refs):
        ins, m_ref, outs, mo_ref = refs[:n], refs[n], refs[n + 1:2 * n + 1], refs[2 * n + 1]
        send_sems, recv_sems = refs[2 * n + 2:]
        x, y, c = _place()
        me, sib = 2 * x + y, (x, y, 1 - c)
        chips = _other_chips(x, y)

        def copy(k, src, dst, to):
            return pltpu.make_async_remote_copy(src_ref=src, dst_ref=dst, send_sem=send_sems.at[k], recv_sem=recv_sems.at[k], device_id=to,
                                                device_id_type=MESH)

        def half(i, chip, h):
            rows = packs[i].shape[0] // 2
            return outs[i].at[chip, pl.ds(h * rows, rows), :]

        def mine(i):
            rows = packs[i].shape[0] // 2
            return ins[i].at[pl.ds(c * rows, rows), :]

        sent = [copy(6 * i + k, mine(i), half(i, me, c), (px, py, c)) for i in range(n) for k, (px, py) in enumerate(chips)]
        sent += [copy(6 * n + k, m_ref, mo_ref.at[me], (px, py, c)) for k, (px, py) in enumerate(chips)]
        for cp in sent:
            cp.start()
        for i in range(n):
            for k, (px, py) in enumerate(chips):
                landed = half(i, 2 * px + py, c)
                copy(6 * i + k, mine(i), landed, (px, py, c)).wait_recv()
                fwd = copy(6 * i + 3 + k, landed, landed, sib)
                fwd.start()
                sent.append(fwd)
        for k, (px, py) in enumerate(chips):
            for i in range(n):
                other = half(i, 2 * px + py, 1 - c)
                copy(6 * i + 3 + k, other, other, sib).wait_recv()
            copy(6 * n + k, m_ref, mo_ref.at[2 * px + py], (px, py, c)).wait_recv()
        for cp in sent:
            cp.wait_send()

    nsem = 6 * n + 3
    res = pl.pallas_call(
        body, in_specs=[ANY] * (n + 1), out_specs=[ANY] * (n + 1),
        out_shape=[jax.ShapeDtypeStruct((N_CHIPS,) + a.shape, a.dtype) for a in list(packs) + [meta]],
        scratch_shapes=[pltpu.SemaphoreType.DMA((nsem,)), pltpu.SemaphoreType.DMA((nsem,))],
        name=name,
    )(*packs, meta)
    return [_with_own_slot(r, a) for r, a in zip(res[:n], packs)], _with_own_slot(res[n], meta)


def _grads_to_sibling(gs, name):
    n = len(gs)

    def body(*refs):
        ins, outs, send_sems, recv_sems = refs[:n], refs[n:2 * n], refs[2 * n], refs[2 * n + 1]
        x, y, c = _place()
        cps = []
        for i in range(n):
            rows = gs[i].shape[1] // 2
            cps.append(pltpu.make_async_remote_copy(src_ref=ins[i].at[:, pl.ds((1 - c) * rows, rows), :], dst_ref=outs[i],
                                                    send_sem=send_sems.at[i], recv_sem=recv_sems.at[i], device_id=(x, y, 1 - c),
                                                    device_id_type=MESH))
        for cp in cps:
            cp.start()
        for cp in cps:
            cp.wait()

    return pl.pallas_call(
        body, in_specs=[ANY] * n, out_specs=[ANY] * n,
        out_shape=[jax.ShapeDtypeStruct((g.shape[0], g.shape[1] // 2, g.shape[2]), g.dtype) for g in gs],
        scratch_shapes=[pltpu.SemaphoreType.DMA((n,)), pltpu.SemaphoreType.DMA((n,))],
        name=name,
    )(*gs)


def _share_halves(rs, name):
    n = len(rs)

    def body(*refs):
        ins, outs, send_sems, recv_sems = refs[:n], refs[n:2 * n], refs[2 * n], refs[2 * n + 1]
        x, y, c = _place()
        cps = [pltpu.make_async_remote_copy(src_ref=ins[i], dst_ref=outs[i], send_sem=send_sems.at[i], recv_sem=recv_sems.at[i],
                                            device_id=(x, y, 1 - c), device_id_type=MESH) for i in range(n)]
        for cp in cps:
            cp.start()
        for cp in cps:
            cp.wait()

    theirs = pl.pallas_call(
        body, in_specs=[ANY] * n, out_specs=[ANY] * n, out_shape=[jax.ShapeDtypeStruct(r.shape, r.dtype) for r in rs],
        scratch_shapes=[pltpu.SemaphoreType.DMA((n,)), pltpu.SemaphoreType.DMA((n,))],
        name=name,
    )(*rs)
    first = lax.axis_index("c") == 0
    return [jnp.where(first, jnp.concatenate([r, t], axis=0), jnp.concatenate([t, r], axis=0)) for r, t in zip(rs, theirs)]


def _chip_scatter(ss, name):
    n = len(ss)

    def body(*refs):
        ins, outs, send_sems, recv_sems = refs[:n], refs[n:2 * n], refs[2 * n], refs[2 * n + 1]
        x, y, c = _place()
        me = 2 * x + y
        chips = _other_chips(x, y)
        for i in range(n):
            for k, (px, py) in enumerate(chips):
                pltpu.make_async_remote_copy(src_ref=ins[i].at[2 * px + py], dst_ref=outs[i].at[me], send_sem=send_sems.at[3 * i + k],
                                             recv_sem=recv_sems.at[3 * i + k], device_id=(px, py, c), device_id_type=MESH).start()
        for i in range(n):
            for k, (px, py) in enumerate(chips):
                cp = pltpu.make_async_remote_copy(src_ref=ins[i].at[2 * px + py], dst_ref=outs[i].at[2 * px + py],
                                                  send_sem=send_sems.at[3 * i + k], recv_sem=recv_sems.at[3 * i + k],
                                                  device_id=(px, py, c), device_id_type=MESH)
                cp.wait_recv()
                cp.wait_send()

    parts = pl.pallas_call(
        body, in_specs=[ANY] * n, out_specs=[ANY] * n, out_shape=[jax.ShapeDtypeStruct(s.shape, s.dtype) for s in ss],
        scratch_shapes=[pltpu.SemaphoreType.DMA((3 * n,)), pltpu.SemaphoreType.DMA((3 * n,))],
        name=name,
    )(*ss)
    me = 2 * lax.axis_index("x") + lax.axis_index("y")
    return [_with_own_slot(p, lax.dynamic_index_in_dim(s, me, axis=0, keepdims=False)) for p, s in zip(parts, ss)]


def _all_reduce_small(v, name):
    shape = v.shape

    def body(v_ref, o_ref, slots, send_sems, recv_sems):
        x, y, c = _place()
        me = 4 * x + 2 * y + c
        slots[me] = v_ref[...]
        for r in range(1, N_DEV):
            peer = (x ^ (r >> 2), y ^ ((r >> 1) & 1), c ^ (r & 1))
            pltpu.make_async_remote_copy(src_ref=v_ref, dst_ref=slots.at[me], send_sem=send_sems.at[r - 1], recv_sem=recv_sems.at[r - 1],
                                         device_id=peer, device_id_type=MESH).start()
        for r in range(1, N_DEV):
            peer = (x ^ (r >> 2), y ^ ((r >> 1) & 1), c ^ (r & 1))
            cp = pltpu.make_async_remote_copy(src_ref=v_ref, dst_ref=slots.at[4 * peer[0] + 2 * peer[1] + peer[2]], send_sem=send_sems.at[r - 1],
                                              recv_sem=recv_sems.at[r - 1], device_id=peer, device_id_type=MESH)
            cp.wait_recv()
            cp.wait_send()
        acc = slots[0]
        for d in range(1, N_DEV):
            acc = acc + slots[d]
        o_ref[...] = acc

    vm = pl.BlockSpec(memory_space=pltpu.VMEM)
    return pl.pallas_call(
        body, in_specs=[vm], out_specs=vm, out_shape=jax.ShapeDtypeStruct(shape, F32),
        scratch_shapes=[pltpu.VMEM((N_DEV,) + shape, F32), pltpu.SemaphoreType.DMA((N_DEV - 1,)), pltpu.SemaphoreType.DMA((N_DEV - 1,))],
        name=name,
    )(v)


def _add_sibling(g, gsib, core, name):
    n, _, cdim = g.shape
    half = gsib.shape[1]
    tr = half // 2

    def body(core_ref, a_ref, b_ref, o_ref):
        o_ref[...] = (a_ref[...] + b_ref[...]).astype(o_ref.dtype)

    blk = pl.BlockSpec((1, tr, cdim), lambda j, i, core_ref: (j, i, 0))
    return pl.pallas_call(
        body,
        grid_spec=pltpu.PrefetchScalarGridSpec(
            num_scalar_prefetch=1, grid=(n, half // tr),
            in_specs=[pl.BlockSpec((1, tr, cdim), lambda j, i, core_ref: (j, core_ref[0] * (half // tr) + i, 0)), blk], out_specs=blk),
        out_shape=jax.ShapeDtypeStruct(gsib.shape, BF16), compiler_params=_params(("parallel", "parallel")), name=name,
    )(core, g, gsib)


def _sum_parts(parts, name):
    n, r, cdim = parts.shape
    tr = r // 2

    def body(p_ref, o_ref):
        acc = p_ref[0].astype(F32)
        for j in range(1, n):
            acc = acc + p_ref[j].astype(F32)
        o_ref[...] = acc

    return pl.pallas_call(
        body, grid=(r // tr,), in_specs=[pl.BlockSpec((n, tr, cdim), lambda i: (0, i, 0))],
        out_specs=pl.BlockSpec((tr, cdim), lambda i: (i, 0)), out_shape=jax.ShapeDtypeStruct((r, cdim), F32),
        compiler_params=_params(("parallel",)), name=name,
    )(parts)


def _adamw(parts, w, m, v, name):
    npart, r, cdim = parts.shape
    tr = r // 4 if r % 32 == 0 else r

    def body(p_ref, w_ref, m_ref, v_ref, g_ref, d_ref, nm_ref, nv_ref):
        g = p_ref[0]
        for j in range(1, npart):
            g = g + p_ref[j]
        m_new = ADAM_B1 * m_ref[...] + (1.0 - ADAM_B1) * g
        v_new = ADAM_B2 * v_ref[...] + (1.0 - ADAM_B2) * (g * g)
        m_hat = m_new / (1.0 - ADAM_B1 ** ADAM_STEP)
        v_hat = v_new / (1.0 - ADAM_B2 ** ADAM_STEP)
        g_ref[...] = g
        d_ref[...] = -ADAM_LR * (m_hat / (jnp.sqrt(v_hat) + ADAM_EPS) + ADAM_WD * w_ref[...])
        nm_ref[...] = m_new
        nv_ref[...] = v_new

    blk = pl.BlockSpec((tr, cdim), lambda i: (i, 0))
    shp = jax.ShapeDtypeStruct((r, cdim), F32)
    return pl.pallas_call(
        body, grid=(r // tr,), in_specs=[pl.BlockSpec((npart, tr, cdim), lambda i: (0, i, 0)), blk, blk, blk], out_specs=[blk] * 4,
        out_shape=[shp] * 4, compiler_params=_params(("parallel",)), name=name,
    )(parts, w, m, v)


BIG = ("ev_w_in", "ev_w_uq", "ev_w_ukv", "ev_w_out", "od_w_in", "od_w_out", "meta")
SMALL = ("norm_g", "final_g", "ev_q_norm_g", "ev_kv_norm_g", "od_sinks")
SMALL_SHAPE = (8, 512)
BY_ROWS = ("ev_w_out", "od_w_out")

EV_IN_SHARD, OD_IN_SHARD, UQ_SHARD = 2976 // N_CHIPS, 2304 // N_CHIPS, 768 // N_CHIPS
OD_IN_AT, UQ_AT, CORNER_AT = 768, 2048, 2304


def _pack_big(a, lead=()):
    dtype = a["ev_w_in"].dtype
    z = lambda r, c: jnp.zeros(lead + (r, c), dtype)
    ax = len(lead)
    pa = jnp.concatenate([a["ev_w_in"], z(D_MODEL, OD_IN_AT - EV_IN_SHARD), a["od_w_in"], z(D_MODEL, 1408 - OD_IN_AT - OD_IN_SHARD)],
                         axis=ax + 1)
    corner = jnp.concatenate([a["ev_w_ukv"], a["meta"], z(256 - MLA_KV_LORA - N_META, 256)], axis=ax)
    pb = jnp.concatenate([a["ev_w_out"], a["od_w_out"], a["ev_w_uq"], z(256, CORNER_AT - UQ_AT - UQ_SHARD), corner], axis=ax + 1)
    return pa, pb


def _unpack_big(pa, pb):
    return dict(ev_w_in=pa[..., :EV_IN_SHARD], od_w_in=pa[..., OD_IN_AT:OD_IN_AT + OD_IN_SHARD], ev_w_out=pb[..., :D_MODEL],
                od_w_out=pb[..., D_MODEL:2 * D_MODEL], ev_w_uq=pb[..., UQ_AT:UQ_AT + UQ_SHARD],
                ev_w_ukv=pb[..., :MLA_KV_LORA, CORNER_AT:], meta=pb[..., MLA_KV_LORA:MLA_KV_LORA + N_META, CORNER_AT:])


def _chip_shards(full, by_rows):
    if by_rows:
        return full.reshape(N_CHIPS, full.shape[0] // N_CHIPS, full.shape[1])
    return full.reshape(full.shape[0], N_CHIPS, -1).transpose(1, 0, 2)


def _from_chip_shards(slots, by_rows):
    if by_rows:
        return slots.reshape(-1, slots.shape[2])
    return slots.transpose(1, 0, 2).reshape(slots.shape[1], -1)


def _pack_small(arrs, extra=None):
    flat = [a.reshape(-1) for a in arrs] + ([] if extra is None else [extra.reshape(-1)])
    used = sum(f.shape[0] for f in flat)
    return jnp.pad(jnp.concatenate(flat), (0, SMALL_SHAPE[0] * SMALL_SHAPE[1] - used)).reshape(SMALL_SHAPE)


def _unpack_small(p, shapes):
    flat, out, at = p.reshape(-1), [], 0
    for s in shapes:
        n = int(np.prod(s))
        out.append(flat[at:at + n].reshape(s))
        at += n
    return out, flat[at]


def kernel(x, meta, norm_g, final_g, ev_w_in, ev_q_norm_g, ev_kv_norm_g, ev_w_uq, ev_w_ukv, ev_w_out, od_w_in, od_sinks, od_w_out, loss_target, m_meta, m_norm_g, m_final_g, m_ev_w_in, m_ev_q_norm_g, m_ev_kv_norm_g, m_ev_w_uq, m_ev_w_ukv, m_ev_w_out, m_od_w_in, m_od_sinks, m_od_w_out, v_meta, v_norm_g, v_final_g, v_ev_w_in, v_ev_q_norm_g, v_ev_kv_norm_g, v_ev_w_uq, v_ev_w_ukv, v_ev_w_out, v_od_w_in, v_od_sinks, v_od_w_out):
    given = dict(locals())
    two_d = lambda a: a[0] if a.ndim == 3 else a
    packs = {k: _pack_big({n: two_d(given[k + n]) for n in BIG}) for k in ("", "m_", "v_")}

    gathered, meta_all = _gather_weights([p.astype(BF16) for p in packs[""]], meta, "gather_weights")
    full = {n: _from_chip_shards(a, n in BY_ROWS) for n, a in _unpack_big(*gathered).items()}
    meta_full = _from_chip_shards(meta_all, False)

    loss, grad_x, grads = _local_step(x, meta_full, norm_g, final_g, ev_q_norm_g, ev_kv_norm_g, od_sinks, loss_target,
                                      full["ev_w_in"], full["ev_w_uq"], full["ev_w_ukv"], full["ev_w_out"], full["od_w_in"], full["od_w_out"])

    gpacks = _pack_big({n: _chip_shards(grads[n], n in BY_ROWS) for n in BIG}, lead=(N_CHIPS,))
    core = lax.axis_index("c").astype(jnp.int32).reshape(1)
    from_sibling = _grads_to_sibling(gpacks, "grads_to_sibling")
    sums = [_add_sibling(g, s, core, "add_sibling_" + t) for g, s, t in zip(gpacks, from_sibling, "ab")]
    parts = _chip_scatter(sums, "grads_to_chips")
    reduced = _share_halves([_sum_parts(p, "add_chips_" + t) for p, t in zip(parts, "ab")], "reduced_to_sibling")
    updated = [_adamw(reduced[i][None], packs[""][i], packs["m_"][i], packs["v_"][i], "adamw_matrices_" + t) for i, t in enumerate("ab")]
    big_out = [{n: a.reshape(given[n].shape) for n, a in _unpack_big(oa, ob).items()} for oa, ob in zip(*updated)]

    small_shapes = [given[n].shape for n in SMALL]
    ssum = _all_reduce_small(_pack_small([grads[n] for n in SMALL], loss[0, 0]), "reduce_vectors")
    small_out = _adamw(ssum[None], _pack_small([given[n] for n in SMALL]), _pack_small([given["m_" + n] for n in SMALL]),
                       _pack_small([given["v_" + n] for n in SMALL]), "adamw_vectors")
    total_loss = ssum.reshape(-1)[sum(int(np.prod(s)) for s in small_shapes)]
    small_out = [_unpack_small(o, small_shapes)[0] for o in small_out]

    names = ("meta", "norm_g", "final_g", "ev_w_in", "ev_q_norm_g", "ev_kv_norm_g", "ev_w_uq", "ev_w_ukv", "ev_w_out", "od_w_in", "od_sinks",
             "od_w_out")
    outs = [total_loss, grad_x]
    for kind in range(4):
        for n in names:
            outs.append(big_out[kind][n] if n in BIG else small_out[kind][SMALL.index(n)])
    return tuple(outs)
```

```python
import functools
import math

import numpy as np
import jax
import jax.numpy as jnp
from jax import lax
from jax.experimental import pallas as pl
from jax.experimental.pallas import tpu as pltpu

F32 = jnp.float32
BF16 = jnp.bfloat16

D_MODEL = 1024
BLOCK = 128
N_META = 16
N_PAD = BLOCK - N_META
NORM_EPS = 1e-6
NEG = -1e30
HEAD = 64
SB_HEADS = 8
MLA_HEADS = 8
MLA_Q_LORA = 256
MLA_KV_LORA = 128
MLA_NOPE = 64
MLA_ROPE = 32
ROPE_BASE = 10000.0
SWA_HEADS = 16
SWA_KV_HEADS = 2
SWA_WINDOW = 128
N_CHIPS = 4
N_DEV = 8

ADAM_LR = 0.001
ADAM_B1 = 0.9
ADAM_B2 = 0.999
ADAM_EPS = 1e-08
ADAM_WD = 0.01
ADAM_STEP = 10

VMEM_LIMIT = 48 * 1024 * 1024

EV_G, EV_Q, EV_K, EV_V, EV_CQ, EV_CKV, EV_KR, EV_N = 0, 1024, 1536, 2048, 2560, 2816, 2944, 3072
OD_G, OD_Q, OD_K, OD_V, OD_N = 0, 1024, 2048, 2176, 2304


def _params(sem=None):
    return pltpu.CompilerParams(dimension_semantics=sem, vmem_limit_bytes=VMEM_LIMIT)


def _row_tile(m):
    return 256 if m % 256 == 0 else 128


def _matmul_rows(m):
    for c in (1088, 1024, 768, 640, 512, 384, 256):
        if m % c == 0:
            return c
    return 128


def _dot(a, b):
    return jnp.dot(a.astype(BF16), b.astype(BF16), preferred_element_type=F32)


def _dot_nt(a, b):
    return lax.dot_general(a.astype(BF16), b.astype(BF16), (((1,), (1,)), ((), ())), preferred_element_type=F32)


def _dot_tn(a, b):
    return lax.dot_general(a.astype(BF16), b.astype(BF16), (((0,), (0,)), ((), ())), preferred_element_type=F32)


def _rms_fwd(h, g, name):
    t, d = h.shape
    tm = _row_tile(t)

    def body(h_ref, g_ref, o_ref):
        x = h_ref[...]
        r = lax.rsqrt(jnp.mean(x * x, axis=-1, keepdims=True) + NORM_EPS)
        o_ref[...] = ((x * r) * g_ref[...]).astype(o_ref.dtype)

    return pl.pallas_call(
        body, grid=(t // tm,),
        in_specs=[pl.BlockSpec((tm, d), lambda i: (i, 0)), pl.BlockSpec((1, d), lambda i: (0, 0))],
        out_specs=pl.BlockSpec((tm, d), lambda i: (i, 0)),
        out_shape=jax.ShapeDtypeStruct((t, d), BF16), compiler_params=_params(("parallel",)), name=name,
    )(h, g)


def _rms_bwd(h, g, dy, dres, name):
    t, d = h.shape
    tm = _row_tile(t)

    def body(h_ref, g_ref, dy_ref, dres_ref, dh_ref, dg_ref):
        @pl.when(pl.program_id(0) == 0)
        def _():
            dg_ref[...] = jnp.zeros_like(dg_ref)

        x = h_ref[...]
        r = lax.rsqrt(jnp.mean(x * x, axis=-1, keepdims=True) + NORM_EPS)
        xr = x * r
        dy_ = dy_ref[...]
        u = dy_ * g_ref[...]
        dh_ref[...] = dres_ref[...] + r * (u - xr * jnp.mean(u * xr, axis=-1, keepdims=True))
        dg_ref[...] += jnp.sum(dy_ * xr, axis=0, keepdims=True)

    row = pl.BlockSpec((tm, d), lambda i: (i, 0))
    vec = pl.BlockSpec((1, d), lambda i: (0, 0))
    return pl.pallas_call(
        body, grid=(t // tm,), in_specs=[row, vec, row, row], out_specs=[row, vec],
        out_shape=[jax.ShapeDtypeStruct((t, d), F32), jax.ShapeDtypeStruct((1, d), F32)],
        compiler_params=_params(("arbitrary",)), name=name,
    )(h, g, dy, dres)


def _col_tile(n):
    for c in (1024, 768, 640, 512, 384, 256, 128):
        if n % c == 0:
            return c
    return n


def _mm(a, w, name, res=None, out_dtype=F32, a_cols=None):
    m = a.shape[0]
    k, n = w.shape
    a_blk = 0 if a_cols is None else a_cols[0] // k
    assert a_cols is None or (a_cols[1] == k and a_cols[0] % k == 0)
    tm, tn = _matmul_rows(m), _col_tile(n)

    def body(*refs):
        if res is None:
            a_ref, w_ref, o_ref = refs
            acc = _dot(a_ref[...], w_ref[...])
        else:
            a_ref, w_ref, r_ref, o_ref = refs
            acc = r_ref[...] + _dot(a_ref[...], w_ref[...])
        o_ref[...] = acc.astype(o_ref.dtype)

    in_specs = [pl.BlockSpec((tm, k), lambda j, i: (i, a_blk)), pl.BlockSpec((k, tn), lambda j, i: (0, j))]
    args = [a, w]
    if res is not None:
        in_specs.append(pl.BlockSpec((tm, tn), lambda j, i: (i, j)))
        args.append(res)
    return pl.pallas_call(
        body, grid=(n // tn, m // tm), in_specs=in_specs, out_specs=pl.BlockSpec((tm, tn), lambda j, i: (i, j)),
        out_shape=jax.ShapeDtypeStruct((m, n), out_dtype), compiler_params=_params(("parallel", "parallel")), name=name,
    )(*args)


def _mm_nt(a, w, name):
    m, n = a.shape
    k = w.shape[0]
    tm, tk = _matmul_rows(m), _col_tile(k)

    def body(a_ref, w_ref, o_ref):
        o_ref[...] = _dot_nt(a_ref[...], w_ref[...])

    return pl.pallas_call(
        body, grid=(k // tk, m // tm),
        in_specs=[pl.BlockSpec((tm, n), lambda j, i: (i, 0)), pl.BlockSpec((tk, n), lambda j, i: (j, 0))],
        out_specs=pl.BlockSpec((tm, tk), lambda j, i: (i, j)),
        out_shape=jax.ShapeDtypeStruct((m, k), F32), compiler_params=_params(("parallel", "parallel")), name=name,
    )(a, w)


def _mm_tn(x, dy, name):
    m, k = x.shape
    n = dy.shape[1]
    tm, tn = _matmul_rows(m), _col_tile(n)

    def body(x_ref, dy_ref, o_ref):
        @pl.when(pl.program_id(1) == 0)
        def _():
            o_ref[...] = jnp.zeros_like(o_ref)

        o_ref[...] += _dot_tn(x_ref[...], dy_ref[...])

    return pl.pallas_call(
        body, grid=(n // tn, m // tm),
        in_specs=[pl.BlockSpec((tm, k), lambda j, i: (i, 0)), pl.BlockSpec((tm, tn), lambda j, i: (i, j))],
        out_specs=pl.BlockSpec((k, tn), lambda j, i: (0, j)),
        out_shape=jax.ShapeDtypeStruct((k, n), F32), compiler_params=_params(("parallel", "arbitrary")), name=name,
    )(x, dy)


def _silu_parts(g):
    s = 1.0 / (1.0 + jnp.exp(-g))
    return g * s, s * (1.0 + g * (1.0 - s))


def _gate_fwd(o_parts, proj, name):
    t = proj.shape[0]
    tm = _row_tile(t)
    w = D_MODEL // len(o_parts)

    def body(*refs):
        g_ref, o_ref = refs[-2], refs[-1]
        for p, r in enumerate(refs[:-2]):
            sil, _ = _silu_parts(g_ref[:, p * w:(p + 1) * w])
            o_ref[:, p * w:(p + 1) * w] = (r[...].astype(F32) * sil).astype(o_ref.dtype)

    return pl.pallas_call(
        body, grid=(t // tm,),
        in_specs=[pl.BlockSpec((tm, w), lambda i: (i, 0)) for _ in o_parts] + [pl.BlockSpec((tm, D_MODEL), lambda i: (i, 0))],
        out_specs=pl.BlockSpec((tm, D_MODEL), lambda i: (i, 0)),
        out_shape=jax.ShapeDtypeStruct((t, D_MODEL), BF16), compiler_params=_params(("parallel",)), name=name,
    )(*o_parts, proj)


def _gate_bwd(dao, o_parts, proj, name):
    t = proj.shape[0]
    tm = _row_tile(t)
    np_ = len(o_parts)
    w = D_MODEL // np_

    def body(*refs):
        dao_ref, g_ref = refs[0], refs[1 + np_]
        do_refs, dg_ref = refs[2 + np_:2 + 2 * np_], refs[-1]
        for p in range(np_):
            sl = slice(p * w, (p + 1) * w)
            sil, dsil = _silu_parts(g_ref[:, sl])
            da = dao_ref[:, sl]
            do_refs[p][...] = da * sil
            dg_ref[:, sl] = (da * refs[1 + p][...].astype(F32) * dsil).astype(dg_ref.dtype)

    full = pl.BlockSpec((tm, D_MODEL), lambda i: (i, 0))
    part = pl.BlockSpec((tm, w), lambda i: (i, 0))
    outs = pl.pallas_call(
        body, grid=(t // tm,), in_specs=[full] + [part] * np_ + [full], out_specs=[part] * np_ + [full],
        out_shape=[jax.ShapeDtypeStruct((t, w), F32)] * np_ + [jax.ShapeDtypeStruct((t, D_MODEL), BF16)],
        compiler_params=_params(("parallel",)), name=name,
    )(dao, *o_parts, proj)
    return outs[:np_], outs[np_]


def _loss_head(h2, gf, target, b, lp):
    d = h2.shape[1]
    nb = lp // BLOCK
    h3 = h2.reshape(b, lp, d)

    def body(h_ref, g_ref, t_ref, dh_ref, dg_ref, loss_ref):
        first = (pl.program_id(0) == 0) & (pl.program_id(1) == 0)

        @pl.when(first)
        def _():
            dg_ref[...] = jnp.zeros_like(dg_ref)
            loss_ref[...] = jnp.zeros_like(loss_ref)

        @pl.when(pl.program_id(1) == 0)
        def _():
            dh_ref[...] = jnp.zeros_like(dh_ref)

        @pl.when(pl.program_id(1) > 0)
        def _():
            x = h_ref[0]
            r = lax.rsqrt(jnp.mean(x * x, axis=-1, keepdims=True) + NORM_EPS)
            xr = x * r
            g = g_ref[...]
            diff = xr * g - t_ref[0]
            loss_ref[...] += 0.5 * jnp.sum(jnp.mean(diff * diff, axis=-1, keepdims=True))
            dy = diff * (1.0 / d)
            u = dy * g
            dh_ref[0] = r * (u - xr * jnp.mean(u * xr, axis=-1, keepdims=True))
            dg_ref[...] += jnp.sum(dy * xr, axis=0, keepdims=True)

    blk = pl.BlockSpec((1, BLOCK, d), lambda bi, n: (bi, n, 0))
    dh, dg, loss = pl.pallas_call(
        body, grid=(b, nb),
        in_specs=[blk, pl.BlockSpec((1, d), lambda bi, n: (0, 0)),
                  pl.BlockSpec((1, BLOCK, d), lambda bi, n: (bi, jnp.maximum(n - 1, 0), 0))],
        out_specs=[blk, pl.BlockSpec((1, d), lambda bi, n: (0, 0)), pl.BlockSpec((8, 128), lambda bi, n: (0, 0))],
        out_shape=[jax.ShapeDtypeStruct((b, lp, d), F32), jax.ShapeDtypeStruct((1, d), F32), jax.ShapeDtypeStruct((8, 128), F32)],
        compiler_params=_params(("arbitrary", "arbitrary")), name="loss_head",
    )(h3, gf, target)
    return dh.reshape(b * lp, d), dg, loss


def _iota2(shape, dim):
    return lax.broadcasted_iota(jnp.int32, shape, dim)


KEYS = 256


def _lo_lanes():
    return _iota2((1, BLOCK), 1) < HEAD


def _halves(x, lo):
    zero = jnp.zeros_like(x)
    return jnp.where(lo, x, zero), jnp.where(lo, zero, x)


def _rows_of_pair(a, b):
    return jnp.where(_iota2((BLOCK, 1), 0) < HEAD, a, b)


def _split_rows_t(x):
    xt = x.T
    first = _iota2(xt.shape, 0) < HEAD
    zero = jnp.zeros_like(xt)
    return jnp.concatenate([jnp.where(first, xt, zero), jnp.where(first, zero, xt)], axis=1).astype(BF16)


def _key_chunk(c, lp, t_idx, strict, key_axis):
    first = c * KEYS
    s0 = pl.multiple_of(jnp.minimum(first, lp - KEYS), BLOCK)
    s_idx = s0 + _iota2(t_idx.shape, key_axis)
    seen = (s_idx < t_idx) if strict else (s_idx <= t_idx)
    return s0, seen & (s_idx >= jnp.maximum(first, N_PAD))


def _split_dot(x, tri):
    hi = x.astype(BF16)
    lo = (x - hi.astype(F32)).astype(BF16)
    return jnp.dot(hi, tri, preferred_element_type=F32) + jnp.dot(lo, tri, preferred_element_type=F32)


def _stack_halves(x, lo):
    a, b = _halves(x, lo)
    return jnp.concatenate([a, b], axis=0)


def _pair(a, b, lo):
    return jnp.where(lo, a, b)


def _chunk_starts(lp):
    return [min(c * KEYS, lp - KEYS) for c in range(-(-lp // KEYS))]


def _put_rows(ref, r0, bq, a, b):
    for t in range(bq // BLOCK):
        part = slice(t * BLOCK, (t + 1) * BLOCK)
        ref[0, 0, r0 // BLOCK + t] = jnp.concatenate([a[:, part], b[:, part], jnp.zeros((6, BLOCK), F32)], axis=0)


def _get_rows(ref, r0, bq):
    return [jnp.concatenate([ref[0, 0, r0 // BLOCK + t, h:h + 1, :] for t in range(bq // BLOCK)], axis=1) for h in range(2)]


def _n_chunks(i):
    return (i + 2) // 2


QROWS = 512


def _for_query_tiles(nb, tile):
    per = QROWS // BLOCK

    def step(j, _):
        tile(pl.multiple_of(j * QROWS, QROWS), QROWS, (j + 1) * (QROWS // KEYS))
        return 0

    lax.fori_loop(0, nb // per, step, 0)
    for i in range(nb - nb % per, nb):
        tile(i * BLOCK, BLOCK, _n_chunks(i))


def _walk_chunks(r0, n, chunk, carry, leftwards=False):
    diag = jnp.maximum(r0 // KEYS, 1)

    def span(first, last, masked, carry):
        def step(t, cr):
            return chunk(last - 1 - t if leftwards else first + t, cr, masked)
        return lax.fori_loop(0, last - first, step, carry)

    spans = [(0, 1, True), (1, diag, False), (diag, n, True)]
    for first, last, masked in (reversed(spans) if leftwards else spans):
        carry = span(first, last, masked, carry)
    return carry


def _where(valid, x, other):
    return x if valid is None else jnp.where(valid, x, other)


def _sb_scores(q_h, k, valid, after):
    z = _dot_nt(q_h, k) * (HEAD ** -0.5)
    sp = jnp.log(1.0 + jnp.exp(-jnp.abs(z)))
    lb = jnp.minimum(z, 0.0) - sp
    l1m_all = -jnp.maximum(z, 0.0) - sp
    l1m = _where(valid, l1m_all, 0.0)
    return lb, l1m_all, l1m, _split_dot(l1m, after)


def _pair_stat_spec(nb):
    return pl.BlockSpec((1, 1, nb, 8, BLOCK), lambda bi, hp: (bi, hp, 0, 0, 0))


def _sb_fwd(proj3, name):
    b, lp, _ = proj3.shape
    nb = lp // BLOCK
    npair = SB_HEADS // 2

    def body(q_ref, k_ref, v_ref, o_ref, tot_ref):
        lo = _lo_lanes()
        after = (_iota2((KEYS, KEYS), 0) > _iota2((KEYS, KEYS), 1)).astype(BF16)

        def qtile(r0, bq, n):
            qs = _halves(q_ref[0, pl.ds(r0, bq), :].astype(BF16), lo)
            t_idx = r0 + _iota2((bq, KEYS), 0)

            def kchunk(c, carry, masked):
                cs, acc = carry[:2], carry[2]
                s0, valid = _key_chunk(c, lp, t_idx, True, 1)
                valid = valid if masked else None
                k = k_ref[0, pl.ds(s0, KEYS), :].astype(BF16)
                a_s, new = [], []
                for h in range(2):
                    lb, _, l1m, suf = _sb_scores(qs[h], k, valid, after)
                    a_s.append(_where(valid, jnp.exp(lb + suf + cs[h]), 0.0).astype(BF16))
                    new.append(cs[h] + jnp.sum(l1m, axis=1, keepdims=True))
                v_bd = _stack_halves(v_ref[0, pl.ds(s0, KEYS), :].astype(BF16), lo)
                return (*new, acc + jnp.dot(jnp.concatenate(a_s, axis=1), v_bd, preferred_element_type=F32))

            zero = jnp.zeros((bq, 1), F32)
            c_a, c_b, acc = _walk_chunks(r0, n, kchunk, (zero, zero, jnp.zeros((bq, BLOCK), F32)), leftwards=True)
            o_ref[0, pl.ds(r0, bq), :] = acc
            tot_ref[0, pl.ds(r0, bq), :] = jnp.broadcast_to(_pair(c_a, c_b, lo), (bq, BLOCK))

        _for_query_tiles(nb, qtile)

    def col(first):
        return pl.BlockSpec((1, lp, 2 * HEAD), lambda bi, hp: (bi, 0, first // (2 * HEAD) + hp))

    shp = jax.ShapeDtypeStruct((b, lp, SB_HEADS * HEAD), F32)
    return pl.pallas_call(
        body, grid=(b, npair), in_specs=[col(EV_Q), col(EV_K), col(EV_V)], out_specs=[col(0), col(0)], out_shape=[shp, shp],
        compiler_params=_params(("parallel", "parallel")), name=name,
    )(proj3, proj3, proj3)


def _sb_bwd(proj3, tot, do, name):
    b, lp, _ = proj3.shape
    nb = lp // BLOCK
    npair = SB_HEADS // 2

    def body(q_ref, k_ref, v_ref, tot_ref, do_ref, dq_ref, dk_ref, dv_ref):
        lo = _lo_lanes()
        after = (_iota2((KEYS, KEYS), 0) > _iota2((KEYS, KEYS), 1)).astype(BF16)
        before = (_iota2((KEYS, KEYS), 0) < _iota2((KEYS, KEYS), 1)).astype(BF16)
        dk_ref[...] = jnp.zeros_like(dk_ref)
        dv_ref[...] = jnp.zeros_like(dv_ref)

        def qtile(r0, bq, n):
            rows = pl.ds(r0, bq)
            qs = _halves(q_ref[0, rows, :].astype(BF16), lo)
            dos = _halves(do_ref[0, rows, :].astype(BF16), lo)
            tot_i = tot_ref[0, rows, :]
            tots = (tot_i[:, 0:1], tot_i[:, HEAD:HEAD + 1])
            q_st, do_st = jnp.concatenate(qs, axis=0), jnp.concatenate(dos, axis=0)
            t_idx = r0 + _iota2((bq, KEYS), 0)

            def kchunk(c, carry, masked):
                s0, valid = _key_chunk(c, lp, t_idx, True, 1)
                valid = valid if masked else None
                keys = pl.ds(s0, KEYS)
                k = k_ref[0, keys, :].astype(BF16)
                v = v_ref[0, keys, :].astype(BF16)
                a_s, dzs, new = [], [], []
                for h in range(2):
                    left, pre = carry[2 * h], carry[2 * h + 1]
                    lb, l1m_all, l1m, suf = _sb_scores(qs[h], k, valid, after)
                    here = jnp.sum(l1m, axis=1, keepdims=True)
                    a = _where(valid, jnp.exp(lb + suf + (tots[h] - left - here)), 0.0)
                    w = a * _dot_nt(dos[h], v)
                    dz = _where(valid, w * jnp.exp(l1m_all) - (pre + _split_dot(w, before)) * jnp.exp(lb), 0.0) * (HEAD ** -0.5)
                    new += [left + here, pre + jnp.sum(w, axis=1, keepdims=True)]
                    a_s.append(a.astype(BF16))
                    dzs.append(dz.astype(BF16))
                dk_ref[0, keys, :] += _dot_tn(jnp.concatenate(dzs, axis=0), q_st)
                dv_ref[0, keys, :] += _dot_tn(jnp.concatenate(a_s, axis=0), do_st)
                dq = carry[4] + jnp.dot(jnp.concatenate(dzs, axis=1), _stack_halves(k, lo), preferred_element_type=F32)
                return (*new, dq)

            zero = jnp.zeros((bq, 1), F32)
            out = _walk_chunks(r0, n, kchunk, (zero, zero, zero, zero, jnp.zeros((bq, BLOCK), F32)))
            dq_ref[0, rows, :] = out[4]

        _for_query_tiles(nb, qtile)

    def col(first):
        return pl.BlockSpec((1, lp, 2 * HEAD), lambda bi, hp: (bi, 0, first // (2 * HEAD) + hp))

    shp = jax.ShapeDtypeStruct((b, lp, SB_HEADS * HEAD), F32)
    return pl.pallas_call(
        body, grid=(b, npair), in_specs=[col(EV_Q), col(EV_K), col(EV_V), col(0), col(0)], out_specs=[col(0)] * 3, out_shape=[shp] * 3,
        compiler_params=_params(("parallel", "parallel")), name=name,
    )(proj3, proj3, proj3, tot, do)


def _rope_tables(lp):
    half = MLA_ROPE // 2
    pos = (np.arange(lp) - N_PAD).astype(np.float32)
    inv = jnp.asarray(ROPE_BASE, F32) ** (-jnp.arange(half, dtype=F32) / half)
    ang = jnp.asarray(pos)[:, None] * inv[None, :]
    cos, sin = jnp.cos(ang), jnp.sin(ang)
    zeros = lambda n: jnp.zeros((lp, n), F32)
    c = jnp.concatenate([jnp.ones((lp, MLA_NOPE), F32), cos, cos, zeros(32)], axis=1)
    s1 = jnp.concatenate([zeros(MLA_NOPE), -sin, zeros(half), zeros(32)], axis=1)
    s2 = jnp.concatenate([zeros(MLA_NOPE), zeros(half), sin, zeros(32)], axis=1)
    return c, s1, s2


def _rope(x, c, s1, s2):
    half = MLA_ROPE // 2
    return x * c + pltpu.roll(x, BLOCK - half, 1) * s1 + pltpu.roll(x, half, 1) * s2


def _rope_t(dy, c, s1, s2):
    half = MLA_ROPE // 2
    return dy * c + pltpu.roll(dy * s1, half, 1) + pltpu.roll(dy * s2, BLOCK - half, 1)


def _rms_rows(x, g):
    r = lax.rsqrt(jnp.mean(x * x, axis=-1, keepdims=True) + NORM_EPS)
    return x * r, r


def _mla_prep_fwd(proj3, gq, gkv, wq, wk, wv, tabs, name):
    b, lp, _ = proj3.shape
    nb = lp // BLOCK
    hw = MLA_HEADS * BLOCK

    def body(cq_ref, ckv_ref, kr_ref, gq_ref, gkv_ref, wq_ref, wk_ref, wv_ref, c_ref, s1_ref, s2_ref, qf_ref, kf_ref, v_ref):
        c, s1, s2 = c_ref[...], s1_ref[...], s2_ref[...]
        xq, _ = _rms_rows(cq_ref[0], None)
        qh = _dot(xq * gq_ref[...], wq_ref[...])
        xk, _ = _rms_rows(ckv_ref[0], None)
        ckv_n = xk * gkv_ref[...]
        kv = _dot(ckv_n, wk_ref[...])
        v_ref[0] = _dot(ckv_n, wv_ref[...]).astype(v_ref.dtype)
        kr = _rope(kr_ref[0], c, s1, s2)
        for h in range(MLA_HEADS):
            ls = slice(h * BLOCK, (h + 1) * BLOCK)
            qf_ref[0, :, ls] = _rope(qh[:, ls], c, s1, s2).astype(qf_ref.dtype)
            kf_ref[0, :, ls] = (kv[:, ls] + kr).astype(kf_ref.dtype)

    def col(first, width):
        return pl.BlockSpec((1, BLOCK, width), lambda bi, n: (bi, n, first // width))

    def whole(a):
        return pl.BlockSpec(a.shape, lambda bi, n: (0,) * a.ndim)

    tab = pl.BlockSpec((BLOCK, BLOCK), lambda bi, n: (n, 0))
    return pl.pallas_call(
        body, grid=(b, nb),
        in_specs=[col(EV_CQ, MLA_Q_LORA), col(EV_CKV, MLA_KV_LORA), col(EV_KR, BLOCK), whole(gq), whole(gkv), whole(wq), whole(wk),
                  whole(wv), tab, tab, tab],
        out_specs=[col(0, hw), col(0, hw), col(0, MLA_HEADS * HEAD)],
        out_shape=[jax.ShapeDtypeStruct((b, lp, hw), BF16), jax.ShapeDtypeStruct((b, lp, hw), BF16),
                   jax.ShapeDtypeStruct((b, lp, MLA_HEADS * HEAD), BF16)],
        compiler_params=_params(("parallel", "parallel")), name=name,
    )(proj3, proj3, proj3, gq, gkv, wq, wk, wv, *tabs)


def _mla_prep_bwd(proj3, gq, gkv, wq, wk, wv, tabs, dqf, dkf, dv, name):
    b, lp, _ = proj3.shape
    nb = lp // BLOCK
    hw = MLA_HEADS * BLOCK

    def body(cq_ref, ckv_ref, gq_ref, gkv_ref, wq_ref, wk_ref, wv_ref, c_ref, s1_ref, s2_ref, dqf_ref, dkf_ref, dv_ref,
             dcq_ref, dckv_ref, dkr_ref, dwq_ref, dwk_ref, dwv_ref, dgq_ref, dgkv_ref, dqh):
        @pl.when((pl.program_id(0) == 0) & (pl.program_id(1) == 0))
        def _():
            for r in (dwq_ref, dwk_ref, dwv_ref, dgq_ref, dgkv_ref):
                r[...] = jnp.zeros_like(r)

        c, s1, s2 = c_ref[...], s1_ref[...], s2_ref[...]
        dkr = jnp.zeros((BLOCK, BLOCK), F32)
        for h in range(MLA_HEADS):
            ls = slice(h * BLOCK, (h + 1) * BLOCK)
            dqh[:, ls] = _rope_t(dqf_ref[0, :, ls].astype(F32), c, s1, s2).astype(dqh.dtype)
            dkr = dkr + dkf_ref[0, :, ls].astype(F32)
        dkr_ref[0] = _rope_t(dkr, c, s1, s2).astype(dkr_ref.dtype)

        def norm_bwd(x, g, dy, dg_ref):
            xr, r = _rms_rows(x, None)
            u = dy * g
            dg_ref[...] += jnp.sum(dy * xr, axis=0, keepdims=True)
            return r * (u - xr * jnp.mean(u * xr, axis=-1, keepdims=True))

        xq, _ = _rms_rows(cq_ref[0], None)
        cq_n = xq * gq_ref[...]
        dwq_ref[...] += _dot_tn(cq_n, dqh[...])
        dcq_ref[0] = norm_bwd(cq_ref[0], gq_ref[...], _dot_nt(dqh[...], wq_ref[...]), dgq_ref).astype(dcq_ref.dtype)
        xk, _ = _rms_rows(ckv_ref[0], None)
        ckv_n = xk * gkv_ref[...]
        dkf_, dv_ = dkf_ref[0], dv_ref[0]
        dwk_ref[...] += _dot_tn(ckv_n, dkf_)
        dwv_ref[...] += _dot_tn(ckv_n, dv_)
        dckv_n = _dot_nt(dkf_, wk_ref[...]) + _dot_nt(dv_, wv_ref[...])
        dckv_ref[0] = norm_bwd(ckv_ref[0], gkv_ref[...], dckv_n, dgkv_ref).astype(dckv_ref.dtype)

    def col(first, width):
        return pl.BlockSpec((1, BLOCK, width), lambda bi, n: (bi, n, first // width))

    def whole(a):
        return pl.BlockSpec(a.shape, lambda bi, n: (0,) * len(a.shape))

    tab = pl.BlockSpec((BLOCK, BLOCK), lambda bi, n: (n, 0))
    acc_shapes = [jax.ShapeDtypeStruct(a.shape, F32) for a in (wq, wk, wv, gq, gkv)]
    return pl.pallas_call(
        body, grid=(b, nb),
        in_specs=[col(EV_CQ, MLA_Q_LORA), col(EV_CKV, MLA_KV_LORA), whole(gq), whole(gkv), whole(wq), whole(wk), whole(wv), tab, tab, tab,
                  col(0, hw), col(0, hw), col(0, MLA_HEADS * HEAD)],
        out_specs=[col(0, MLA_Q_LORA), col(0, MLA_KV_LORA), col(0, BLOCK)] + [whole(a) for a in acc_shapes],
        out_shape=[jax.ShapeDtypeStruct((b, lp, MLA_Q_LORA), BF16), jax.ShapeDtypeStruct((b, lp, MLA_KV_LORA), BF16),
                   jax.ShapeDtypeStruct((b, lp, BLOCK), BF16)] + acc_shapes,
        scratch_shapes=[pltpu.VMEM((BLOCK, hw), BF16)],
        compiler_params=_params(("arbitrary", "arbitrary")), name=name,
    )(proj3, proj3, gq, gkv, wq, wk, wv, *tabs, dqf, dkf, dv)


def _mla_fwd(qf, kf, v, name):
    b, lp, _ = qf.shape
    nb = lp // BLOCK
    npair = MLA_HEADS // 2
    scale = (MLA_NOPE + MLA_ROPE) ** -0.5
    starts = _chunk_starts(lp)

    def body(q_ref, k_ref, v_ref, o_ref, lse_ref, vt_ref):
        for c, s0 in enumerate(starts):
            vt_ref[c] = _split_rows_t(v_ref[0, s0:s0 + KEYS, :].astype(F32))

        def qtile(r0, bq, n):
            qs = [q_ref[0, pl.ds(r0, bq), h * BLOCK:(h + 1) * BLOCK] for h in range(2)]
            t_idx = r0 + _iota2((KEYS, bq), 1)

            def kchunk(c, carry, masked):
                stats, acc = carry[:4], carry[4]
                s0, valid = _key_chunk(c, lp, t_idx, False, 0)
                valid = valid if masked else None
                ps, new, alphas = [], [], []
                for h in range(2):
                    m, l = stats[2 * h], stats[2 * h + 1]
                    s = _where(valid, _dot_nt(k_ref[0, pl.ds(s0, KEYS), h * BLOCK:(h + 1) * BLOCK], qs[h]) * scale, NEG)
                    m_new = jnp.maximum(m, jnp.max(s, axis=0, keepdims=True))
                    p = _where(valid, jnp.exp(s - m_new), 0.0)
                    alpha = jnp.exp(m - m_new)
                    new += [m_new, alpha * l + jnp.sum(p, axis=0, keepdims=True)]
                    alphas.append(alpha)
                    ps.append(p.astype(BF16))
                pv = jnp.dot(vt_ref[c], jnp.concatenate(ps, axis=0), preferred_element_type=F32)
                return (*new, _rows_of_pair(alphas[0], alphas[1]) * acc + pv)

            neg, zero = jnp.full((1, bq), NEG, F32), jnp.zeros((1, bq), F32)
            m_a, l_a, m_b, l_b, acc = _walk_chunks(r0, n, kchunk, (neg, zero, neg, zero, jnp.zeros((BLOCK, bq), F32)))
            safe = [jnp.where(l > 0.0, l, 1.0) for l in (l_a, l_b)]
            o_ref[0, pl.ds(r0, bq), :] = (acc / _rows_of_pair(safe[0], safe[1])).T
            lse = [jnp.where(l > 0.0, m + jnp.log(sf), 0.0) for m, l, sf in ((m_a, l_a, safe[0]), (m_b, l_b, safe[1]))]
            _put_rows(lse_ref, r0, bq, lse[0], lse[1])

        _for_query_tiles(nb, qtile)

    wide = pl.BlockSpec((1, lp, 2 * BLOCK), lambda bi, hp: (bi, 0, hp))
    thin = pl.BlockSpec((1, lp, 2 * HEAD), lambda bi, hp: (bi, 0, hp))
    return pl.pallas_call(
        body, grid=(b, npair), in_specs=[wide, wide, thin], out_specs=[thin, _pair_stat_spec(nb)],
        out_shape=[jax.ShapeDtypeStruct((b, lp, MLA_HEADS * HEAD), F32), jax.ShapeDtypeStruct((b, npair, nb, 8, BLOCK), F32)],
        scratch_shapes=[pltpu.VMEM((len(starts), BLOCK, 2 * KEYS), BF16)],
        compiler_params=_params(("parallel", "parallel")), name=name,
    )(qf, kf, v)


def _mla_bwd(qf, kf, v, o, lse, do, name):
    b, lp, _ = qf.shape
    nb = lp // BLOCK
    npair = MLA_HEADS // 2
    scale = (MLA_NOPE + MLA_ROPE) ** -0.5

    starts = _chunk_starts(lp)

    def body(q_ref, k_ref, v_ref, o_ref, lse_ref, do_ref, dq_ref, dk_ref, dv_ref, kt_ref):
        lo = _lo_lanes()
        dk_ref[...] = jnp.zeros_like(dk_ref)
        dv_ref[...] = jnp.zeros_like(dv_ref)
        for c, s0 in enumerate(starts):
            for h in range(2):
                kt_ref[c, h] = k_ref[0, s0:s0 + KEYS, h * BLOCK:(h + 1) * BLOCK].astype(F32).T.astype(BF16)

        def qtile(r0, bq, n):
            rows = pl.ds(r0, bq)
            qs = [q_ref[0, rows, h * BLOCK:(h + 1) * BLOCK] for h in range(2)]
            do_i = do_ref[0, rows, :]
            dos = _halves(do_i.astype(BF16), lo)
            do_st = jnp.concatenate(dos, axis=0)
            both = (do_i * o_ref[0, rows, :]).T
            dsum = (jnp.sum(both[:HEAD], axis=0, keepdims=True), jnp.sum(both[HEAD:], axis=0, keepdims=True))
            lses = _get_rows(lse_ref, r0, bq)
            t_idx = r0 + _iota2((KEYS, bq), 1)

            def kchunk(c, dqts, masked):
                s0, valid = _key_chunk(c, lp, t_idx, False, 0)
                valid = valid if masked else None
                keys = pl.ds(s0, KEYS)
                v_c = v_ref[0, keys, :]
                ps, out = [], []
                for h in range(2):
                    lanes = slice(h * BLOCK, (h + 1) * BLOCK)
                    s = _dot_nt(k_ref[0, keys, lanes], qs[h]) * scale
                    p = _where(valid, jnp.exp(s - lses[h]), 0.0)
                    ds = (p * (_dot_nt(v_c, dos[h]) - dsum[h]) * scale).astype(BF16)
                    dk_ref[0, keys, lanes] += jnp.dot(ds, qs[h], preferred_element_type=F32)
                    out.append(dqts[h] + jnp.dot(kt_ref[c, h], ds, preferred_element_type=F32))
                    ps.append(p.astype(BF16))
                dv_ref[0, keys, :] += jnp.dot(jnp.concatenate(ps, axis=1), do_st, preferred_element_type=F32)
                return tuple(out)

            zero = jnp.zeros((BLOCK, bq), F32)
            dq_a, dq_b = _walk_chunks(r0, n, kchunk, (zero, zero))
            dq_ref[0, rows, 0:BLOCK] = dq_a.T
            dq_ref[0, rows, BLOCK:2 * BLOCK] = dq_b.T

        _for_query_tiles(nb, qtile)

    wide = pl.BlockSpec((1, lp, 2 * BLOCK), lambda bi, hp: (bi, 0, hp))
    thin = pl.BlockSpec((1, lp, 2 * HEAD), lambda bi, hp: (bi, 0, hp))
    return pl.pallas_call(
        body, grid=(b, npair), in_specs=[wide, wide, thin, thin, _pair_stat_spec(nb), thin], out_specs=[wide, wide, thin],
        out_shape=[jax.ShapeDtypeStruct(qf.shape, F32), jax.ShapeDtypeStruct(qf.shape, F32), jax.ShapeDtypeStruct(v.shape, F32)],
        scratch_shapes=[pltpu.VMEM((len(starts), 2, BLOCK, KEYS), BF16)],
        compiler_params=_params(("parallel", "parallel")), name=name,
    )(qf, kf, v, o, lse, do)


def _swa_keys(k_ref, v_ref, n, kv):
    prev = jnp.maximum(n - 1, 0)
    rows = lambda blk: pl.ds(pl.multiple_of(blk * BLOCK, BLOCK), BLOCK)
    mine = (_iota2((1, BLOCK), 1) >= HEAD).astype(jnp.int32) == kv

    def both_halves(ref):
        x = jnp.concatenate([ref[0, rows(prev), :], ref[0, rows(n), :], ref[0, 0:BLOCK, :]], axis=0)
        return jnp.where(mine, x, pltpu.roll(x, HEAD, 1))

    slot = _iota2((3 * BLOCK, BLOCK), 0)
    loc = slot % BLOCK
    s_idx = jnp.where(slot < BLOCK, (n - 1) * BLOCK + loc, jnp.where(slot < 2 * BLOCK, n * BLOCK + loc, loc))
    dist = n * BLOCK + _iota2((3 * BLOCK, BLOCK), 1) - s_idx
    band = (slot < 2 * BLOCK) & (dist >= 0) & (dist < SWA_WINDOW) & (s_idx >= BLOCK)
    meta = (slot >= 2 * BLOCK) & (s_idx >= N_PAD) & (dist >= 0)
    return both_halves(k_ref), both_halves(v_ref), band | meta, dist.astype(F32), prev


def _swa_probs(q_h, kdup, valid, dist, head, sink_ref):
    slope = jnp.exp(jnp.full((1, 1), -8.0 * math.log(2.0) / SWA_HEADS, F32) * (head + 1).astype(F32))
    s = jnp.where(valid, _dot_nt(kdup, q_h) * (HEAD ** -0.5) - slope * dist, NEG)
    sink = sink_ref[pl.ds(head, 1), 0:1]
    m = jnp.maximum(jnp.max(s, axis=0, keepdims=True), sink)
    e = jnp.where(valid, jnp.exp(s - m), 0.0)
    es = jnp.exp(sink - m)
    inv = 1.0 / (jnp.sum(e, axis=0, keepdims=True) + es)
    return e * inv, es * inv


SWA_PAIRS = SWA_HEADS // SWA_KV_HEADS // 2
SWA_GROUP = SWA_PAIRS * 2 * HEAD


def _swa_specs(b, lp):
    nb = lp // BLOCK
    qcol = lambda first: pl.BlockSpec((1, BLOCK, SWA_GROUP), lambda bi, kv, n: (bi, n, first // SWA_GROUP + kv))
    kcol = lambda first: pl.BlockSpec((1, lp, BLOCK), lambda bi, kv, n: (bi, 0, first // BLOCK))
    sink = pl.BlockSpec((SWA_HEADS, BLOCK), lambda bi, kv, n: (0, 0))
    return (b, SWA_KV_HEADS, nb), qcol, kcol, sink


def _swa_fwd(proj3, sinks, name):
    b, lp, _ = proj3.shape
    grid, qcol, kcol, sink = _swa_specs(b, lp)

    def body(q_ref, k_ref, v_ref, sink_ref, o_ref):
        kv, n = pl.program_id(1), pl.program_id(2)
        lo = _lo_lanes()
        kdup, vdup, valid, dist, _ = _swa_keys(k_ref, v_ref, n, kv)
        kdup = kdup.astype(BF16)
        vt = _split_rows_t(vdup)
        for p in range(SWA_PAIRS):
            lanes = slice(p * BLOCK, (p + 1) * BLOCK)
            qs = _halves(q_ref[0, :, lanes].astype(BF16), lo)
            probs = [_swa_probs(qs[hh], kdup, valid, dist, (kv * SWA_PAIRS + p) * 2 + hh, sink_ref)[0].astype(BF16) for hh in range(2)]
            o_ref[0, :, lanes] = jnp.dot(vt, jnp.concatenate(probs, axis=0), preferred_element_type=F32).T

    return pl.pallas_call(
        body, grid=grid, in_specs=[qcol(OD_Q), kcol(OD_K), kcol(OD_V), sink], out_specs=qcol(0),
        out_shape=jax.ShapeDtypeStruct((b, lp, SWA_HEADS * HEAD), F32),
        compiler_params=_params(("parallel", "parallel", "parallel")), name=name,
    )(proj3, proj3, proj3, sinks)


def _swa_bwd(proj3, sinks, do, name):
    b, lp, _ = proj3.shape
    nb = lp // BLOCK
    grid, qcol, kcol, sink = _swa_specs(b, lp)

    def body(q_ref, k_ref, v_ref, sink_ref, do_ref, dq_ref, dk_ref, dv_ref, dsink_ref, dk_acc, dv_acc):
        kv, n = pl.program_id(1), pl.program_id(2)

        @pl.when((n == 0) & (pl.program_id(0) == 0) & (kv == 0))
        def _():
            dsink_ref[...] = jnp.zeros_like(dsink_ref)

        @pl.when(n == 0)
        def _():
            dk_acc[...] = jnp.zeros_like(dk_acc)
            dv_acc[...] = jnp.zeros_like(dv_acc)

        lo = _lo_lanes()
        kdup, vdup, valid, dist, prev = _swa_keys(k_ref, v_ref, n, kv)
        kt = _split_rows_t(kdup)
        kdup, vdup = kdup.astype(BF16), vdup.astype(BF16)
        dkc = jnp.zeros((3 * BLOCK, BLOCK), F32)
        dvc = jnp.zeros((3 * BLOCK, BLOCK), F32)
        for p in range(SWA_PAIRS):
            lanes = slice(p * BLOCK, (p + 1) * BLOCK)
            qs = _halves(q_ref[0, :, lanes].astype(BF16), lo)
            dos = _halves(do_ref[0, :, lanes].astype(BF16), lo)
            dss, prs = [], []
            for hh in range(2):
                head = (kv * SWA_PAIRS + p) * 2 + hh
                pr, ps = _swa_probs(qs[hh], kdup, valid, dist, head, sink_ref)
                dp = _dot_nt(vdup, dos[hh])
                dsum = jnp.sum(pr * dp, axis=0, keepdims=True)
                dsink_ref[pl.ds(head, 1), :] += jnp.broadcast_to(-jnp.sum(ps * dsum, axis=1, keepdims=True), (1, BLOCK))
                dss.append((pr * (dp - dsum) * (HEAD ** -0.5)).astype(BF16))
                prs.append(pr.astype(BF16))
            dq_ref[0, :, lanes] = jnp.dot(kt, jnp.concatenate(dss, axis=0), preferred_element_type=F32).T
            dkc = dkc + jnp.dot(jnp.concatenate(dss, axis=1), jnp.concatenate(qs, axis=0), preferred_element_type=F32)
            dvc = dvc + jnp.dot(jnp.concatenate(prs, axis=1), jnp.concatenate(dos, axis=0), preferred_element_type=F32)
        rows = lambda blk: pl.ds(pl.multiple_of(blk * BLOCK, BLOCK), BLOCK)
        for part, r in enumerate((rows(prev), rows(n), slice(0, BLOCK))):
            dk_acc[r, :] += dkc[part * BLOCK:(part + 1) * BLOCK]
            dv_acc[r, :] += dvc[part * BLOCK:(part + 1) * BLOCK]

        for acc, ref in ((dk_acc, dk_ref), (dv_acc, dv_ref)):
            @pl.when((n == nb - 1) & (kv == 0))
            def _():
                x = acc[...]
                ref[0] = x + pltpu.roll(x, HEAD, 1)

            @pl.when((n == nb - 1) & (kv == 1))
            def _():
                x = acc[...]
                ref[0] = jnp.where(lo, ref[0], x + pltpu.roll(x, HEAD, 1))

    kvout = pl.BlockSpec((1, lp, BLOCK), lambda bi, kv, n: (bi, 0, 0))
    kvshape = jax.ShapeDtypeStruct((b, lp, BLOCK), F32)
    return pl.pallas_call(
        body, grid=grid, in_specs=[qcol(OD_Q), kcol(OD_K), kcol(OD_V), sink, qcol(0)], out_specs=[qcol(0), kvout, kvout, sink],
        out_shape=[jax.ShapeDtypeStruct((b, lp, SWA_HEADS * HEAD), F32), kvshape, kvshape, jax.ShapeDtypeStruct((SWA_HEADS, BLOCK), F32)],
        scratch_shapes=[pltpu.VMEM((lp, BLOCK), F32), pltpu.VMEM((lp, BLOCK), F32)],
        compiler_params=_params(("arbitrary", "arbitrary", "arbitrary")), name=name,
    )(proj3, proj3, proj3, sinks, do)


def _kernel_weights(ev_w_in, ev_w_uq, ev_w_ukv, od_w_in):
    zeros = lambda r, c: jnp.zeros((r, c), ev_w_in.dtype)
    q_sb, k_sb, v_sb, g_sb, c_q, c_kv, k_r, g_mla = jnp.split(ev_w_in, [512, 1024, 1536, 2048, 2304, 2432, 2464], axis=1)
    w0 = jnp.concatenate([g_sb, g_mla, q_sb, k_sb, v_sb, c_q, c_kv, zeros(D_MODEL, MLA_NOPE), k_r, zeros(D_MODEL, 32)], axis=1)
    uq = ev_w_uq.reshape(MLA_Q_LORA, MLA_HEADS, MLA_NOPE + MLA_ROPE)
    wq = jnp.pad(uq, ((0, 0), (0, 0), (0, BLOCK - MLA_NOPE - MLA_ROPE))).reshape(MLA_Q_LORA, MLA_HEADS * BLOCK)
    ukv = ev_w_ukv.reshape(MLA_KV_LORA, MLA_HEADS, BLOCK)
    wk = jnp.pad(ukv[:, :, :MLA_NOPE], ((0, 0), (0, 0), (0, BLOCK - MLA_NOPE))).reshape(MLA_KV_LORA, MLA_HEADS * BLOCK)
    wv = ukv[:, :, MLA_NOPE:].reshape(MLA_KV_LORA, MLA_HEADS * HEAD)
    q, k, v, g = jnp.split(od_w_in, [1024, 1152, 1280], axis=1)
    w1 = jnp.concatenate([g, q, k, v], axis=1)
    return w0, wq, wk, wv, w1


def _original_grads(dw0, dwq, dwk, dwv, dw1):
    sl = lambda a, first, n: a[:, first:first + n]
    d_ev_w_in = jnp.concatenate([sl(dw0, EV_Q, 512), sl(dw0, EV_K, 512), sl(dw0, EV_V, 512), sl(dw0, EV_G, 512), sl(dw0, EV_CQ, 256),
                                 sl(dw0, EV_CKV, 128), sl(dw0, EV_KR + MLA_NOPE, MLA_ROPE), sl(dw0, EV_G + 512, 512)], axis=1)
    d_uq = dwq.reshape(MLA_Q_LORA, MLA_HEADS, BLOCK)[:, :, :MLA_NOPE + MLA_ROPE].reshape(MLA_Q_LORA, -1)
    d_ukv = jnp.concatenate([dwk.reshape(MLA_KV_LORA, MLA_HEADS, BLOCK)[:, :, :MLA_NOPE], dwv.reshape(MLA_KV_LORA, MLA_HEADS, HEAD)],
                            axis=2).reshape(MLA_KV_LORA, -1)
    d_od_w_in = jnp.concatenate([sl(dw1, OD_Q, 1024), sl(dw1, OD_K, 128), sl(dw1, OD_V, 128), sl(dw1, OD_G, 1024)], axis=1)
    return d_ev_w_in, d_uq, d_ukv, d_od_w_in


def _meta_rows_sum(dh0_3):
    b, _, d = dh0_3.shape

    def body(x_ref, o_ref):
        acc = x_ref[0, N_PAD:BLOCK, :]
        for i in range(1, b):
            acc = acc + x_ref[i, N_PAD:BLOCK, :]
        o_ref[...] = acc

    return pl.pallas_call(
        body, grid=(1,), in_specs=[pl.BlockSpec((b, BLOCK, d), lambda i: (0, 0, 0))], out_specs=pl.BlockSpec((N_META, d), lambda i: (0, 0)),
        out_shape=jax.ShapeDtypeStruct((N_META, d), F32), compiler_params=_params(("arbitrary",)), name="meta_rows_sum",
    )(dh0_3)


def _local_step(x, meta, norm_g, final_g, gq, gkv, sinks, target, ev_w_in, ev_w_uq, ev_w_ukv, wo0, od_w_in, wo1):
    b, seq, d = x.shape
    lp = seq + BLOCK
    t = b * lp
    w0, wq, wk, wv, w1 = _kernel_weights(ev_w_in, ev_w_uq, ev_w_ukv, od_w_in)
    h0 = jnp.concatenate([jnp.zeros((b, N_PAD, d), F32), jnp.broadcast_to(meta[None], (b, N_META, d)), x], axis=1).reshape(t, d)
    tabs = _rope_tables(lp)
    g0, g1 = norm_g[0:1], norm_g[1:2]

    hn0 = _rms_fwd(h0, g0, "norm0")
    proj0 = _mm(hn0, w0, "inproj0")
    p0 = proj0.reshape(b, lp, EV_N)
    o_sb, sb_tot = _sb_fwd(p0, "sb_fwd")
    qf, kf, v = _mla_prep_fwd(p0, gq, gkv, wq, wk, wv, tabs, "mla_prep_fwd")
    o_mla, lse = _mla_fwd(qf, kf, v, "mla_fwd")
    o0 = [o_sb.reshape(t, -1), o_mla.reshape(t, -1)]
    ao0 = _gate_fwd(o0, proj0, "gate0")
    h1 = _mm(ao0, wo0, "outproj0", res=h0)

    hn1 = _rms_fwd(h1, g1, "norm1")
    proj1 = _mm(hn1, w1, "inproj1")
    p1 = proj1.reshape(b, lp, OD_N)
    sinks_b = jnp.broadcast_to(sinks.reshape(SWA_HEADS, 1), (SWA_HEADS, BLOCK))
    o1 = _swa_fwd(p1, sinks_b, "swa_fwd").reshape(t, -1)
    ao1 = _gate_fwd([o1], proj1, "gate1")
    h2 = _mm(ao1, wo1, "outproj1", res=h1)

    dh2, d_final_g, loss = _loss_head(h2, final_g.reshape(1, d), target, b, lp)

    d_wo1 = _mm_tn(ao1, dh2, "d_wo1")
    dao1 = _mm_nt(dh2, wo1, "d_ao1")
    (do1,), dg1 = _gate_bwd(dao1, [o1], proj1, "gate1_bwd")
    dq1, dk4, dv4, d_sinks = _swa_bwd(p1, sinks_b, do1.reshape(b, lp, -1), "swa_bwd")
    unheads = lambda a: a.reshape(t, SWA_KV_HEADS * HEAD).astype(BF16)
    dproj1 = jnp.concatenate([dg1, dq1.reshape(t, -1).astype(BF16), unheads(dk4), unheads(dv4)], axis=1)
    d_w1 = _mm_tn(hn1, dproj1, "d_w1")
    dhn1 = _mm_nt(dproj1, w1, "d_hn1")
    dh1, d_g1 = _rms_bwd(h1, g1, dhn1, dh2, "norm1_bwd")

    d_wo0 = _mm_tn(ao0, dh1, "d_wo0")
    dao0 = _mm_nt(dh1, wo0, "d_ao0")
    (do_sb, do_mla), dg0 = _gate_bwd(dao0, o0, proj0, "gate0_bwd")
    dq_sb, dk_sb, dv_sb = _sb_bwd(p0, sb_tot, do_sb.reshape(b, lp, -1), "sb_bwd")
    dqf, dkf, dv = _mla_bwd(qf, kf, v, o_mla, lse, do_mla.reshape(b, lp, -1), "mla_bwd")
    dcq, dckv, dkr, d_wq, d_wk, d_wv, d_gq, d_gkv = _mla_prep_bwd(p0, gq, gkv, wq, wk, wv, tabs, dqf, dkf, dv, "mla_prep_bwd")
    flat = lambda a: a.reshape(t, -1).astype(BF16)
    dproj0 = jnp.concatenate([dg0, flat(dq_sb), flat(dk_sb), flat(dv_sb), flat(dcq), flat(dckv), flat(dkr)], axis=1)
    d_w0 = _mm_tn(hn0, dproj0, "d_w0")
    dhn0 = _mm_nt(dproj0, w0, "d_hn0")
    dh0, d_g0 = _rms_bwd(h0, g0, dhn0, dh1, "norm0_bwd")
    dh0 = dh0.reshape(b, lp, d)

    d_ev_w_in, d_uq, d_ukv, d_od_w_in = _original_grads(d_w0, d_wq, d_wk, d_wv, d_w1)
    grads = dict(meta=_meta_rows_sum(dh0), norm_g=jnp.concatenate([d_g0, d_g1], axis=0), final_g=d_final_g.reshape(d),
                 ev_w_in=d_ev_w_in, ev_q_norm_g=d_gq, ev_kv_norm_g=d_gkv, ev_w_uq=d_uq, ev_w_ukv=d_ukv, ev_w_out=d_wo0,
                 od_w_in=d_od_w_in, od_sinks=d_sinks[:, 0].reshape(1, SWA_HEADS), od_w_out=d_wo1)
    return loss, dh0[:, BLOCK:], grads


MESH = pl.DeviceIdType.MESH
ANY = pl.BlockSpec(memory_space=pl.ANY)


def _place():
    return lax.axis_index("x"), lax.axis_index("y"), lax.axis_index("c")


def _other_chips(x, y):
    return [(1 - x, y), (x, 1 - y), (1 - x, 1 - y)]


def _with_own_slot(slots, own):
    me = 2 * lax.axis_index("x") + lax.axis_index("y")
    return lax.dynamic_update_slice(slots, own[None], (me,) + (0,) * own.ndim)


def _gather_weights(packs, meta, name):
    n = len(packs)

    def body(*refs):
        ins, m_ref, outs, mo_ref = refs[:n], refs[n], refs[n + 1:2 * n + 1], refs[2 * n + 1]
        send_sems, recv_sems = refs[2 * n + 2:]
        x, y, c = _place()
        me, sib = 2 * x + y, (x, y, 1 - c)
        chips = _other_chips(x, y)

        def copy(k, src, dst, to):
            return pltpu.make_async_remote_copy(src_ref=src, dst_ref=dst, send_sem=send_sems.at[k], recv_sem=recv_sems.at[k], device_id=to,
                                                device_id_type=MESH)

        def half(i, chip, h):
            rows = packs[i].shape[0] // 2
            return outs[i].at[chip, pl.ds(h * rows, rows), :]

        def mine(i):
            rows = packs[i].shape[0] // 2
            return ins[i].at[pl.ds(c * rows, rows), :]

        sent = [copy(6 * i + k, mine(i), half(i, me, c), (px, py, c)) for i in range(n) for k, (px, py) in enumerate(chips)]
        sent += [copy(6 * n + k, m_ref, mo_ref.at[me], (px, py, c)) for k, (px, py) in enumerate(chips)]
        for cp in sent:
            cp.start()
        for i in range(n):
            for k, (px, py) in enumerate(chips):
                landed = half(i, 2 * px + py, c)
                copy(6 * i + k, mine(i), landed, (px, py, c)).wait_recv()
                fwd = copy(6 * i + 3 + k, landed, landed, sib)
                fwd.start()
                sent.append(fwd)
        for k, (px, py) in enumerate(chips):
            for i in range(n):
                other = half(i, 2 * px + py, 1 - c)
                copy(6 * i + 3 + k, other, other, sib).wait_recv()
            copy(6 * n + k, m_ref, mo_ref.at[2 * px + py], (px, py, c)).wait_recv()
        for cp in sent:
            cp.wait_send()

    nsem = 6 * n + 3
    res = pl.pallas_call(
        body, in_specs=[ANY] * (n + 1), out_specs=[ANY] * (n + 1),
        out_shape=[jax.ShapeDtypeStruct((N_CHIPS,) + a.shape, a.dtype) for a in list(packs) + [meta]],
        scratch_shapes=[pltpu.SemaphoreType.DMA((nsem,)), pltpu.SemaphoreType.DMA((nsem,))],
        name=name,
    )(*packs, meta)
    return [_with_own_slot(r, a) for r, a in zip(res[:n], packs)], _with_own_slot(res[n], meta)


def _grads_to_sibling(gs, name):
    n = len(gs)

    def body(*refs):
        ins, outs, send_sems, recv_sems = refs[:n], refs[n:2 * n], refs[2 * n], refs[2 * n + 1]
        x, y, c = _place()
        cps = []
        for i in range(n):
            rows = gs[i].shape[1] // 2
            cps.append(pltpu.make_async_remote_copy(src_ref=ins[i].at[:, pl.ds((1 - c) * rows, rows), :], dst_ref=outs[i],
                                                    send_sem=send_sems.at[i], recv_sem=recv_sems.at[i], device_id=(x, y, 1 - c),
                                                    device_id_type=MESH))
        for cp in cps:
            cp.start()
        for cp in cps:
            cp.wait()

    return pl.pallas_call(
        body, in_specs=[ANY] * n, out_specs=[ANY] * n,
        out_shape=[jax.ShapeDtypeStruct((g.shape[0], g.shape[1] // 2, g.shape[2]), g.dtype) for g in gs],
        scratch_shapes=[pltpu.SemaphoreType.DMA((n,)), pltpu.SemaphoreType.DMA((n,))],
        name=name,
    )(*gs)


def _share_halves(rs, name):
    n = len(rs)

    def body(*refs):
        ins, outs, send_sems, recv_sems = refs[:n], refs[n:2 * n], refs[2 * n], refs[2 * n + 1]
        x, y, c = _place()
        cps = [pltpu.make_async_remote_copy(src_ref=ins[i], dst_ref=outs[i], send_sem=send_sems.at[i], recv_sem=recv_sems.at[i],
                                            device_id=(x, y, 1 - c), device_id_type=MESH) for i in range(n)]
        for cp in cps:
            cp.start()
        for cp in cps:
            cp.wait()

    theirs = pl.pallas_call(
        body, in_specs=[ANY] * n, out_specs=[ANY] * n, out_shape=[jax.ShapeDtypeStruct(r.shape, r.dtype) for r in rs],
        scratch_shapes=[pltpu.SemaphoreType.DMA((n,)), pltpu.SemaphoreType.DMA((n,))],
        name=name,
    )(*rs)
    first = lax.axis_index("c") == 0
    return [jnp.where(first, jnp.concatenate([r, t], axis=0), jnp.concatenate([t, r], axis=0)) for r, t in zip(rs, theirs)]


def _chip_scatter(ss, name):
    n = len(ss)

    def body(*refs):
        ins, outs, send_sems, recv_sems = refs[:n], refs[n:2 * n], refs[2 * n], refs[2 * n + 1]
        x, y, c = _place()
        me = 2 * x + y
        chips = _other_chips(x, y)
        for i in range(n):
            for k, (px, py) in enumerate(chips):
                pltpu.make_async_remote_copy(src_ref=ins[i].at[2 * px + py], dst_ref=outs[i].at[me], send_sem=send_sems.at[3 * i + k],
                                             recv_sem=recv_sems.at[3 * i + k], device_id=(px, py, c), device_id_type=MESH).start()
        for i in range(n):
            for k, (px, py) in enumerate(chips):
                cp = pltpu.make_async_remote_copy(src_ref=ins[i].at[2 * px + py], dst_ref=outs[i].at[2 * px + py],
                                                  send_sem=send_sems.at[3 * i + k], recv_sem=recv_sems.at[3 * i + k],
                                                  device_id=(px, py, c), device_id_type=MESH)
                cp.wait_recv()
                cp.wait_send()

    parts = pl.pallas_call(
        body, in_specs=[ANY] * n, out_specs=[ANY] * n, out_shape=[jax.ShapeDtypeStruct(s.shape, s.dtype) for s in ss],
        scratch_shapes=[pltpu.SemaphoreType.DMA((3 * n,)), pltpu.SemaphoreType.DMA((3 * n,))],
        name=name,
    )(*ss)
    me = 2 * lax.axis_index("x") + lax.axis_index("y")
    return [_with_own_slot(p, lax.dynamic_index_in_dim(s, me, axis=0, keepdims=False)) for p, s in zip(parts, ss)]


def _all_reduce_small(v, name):
    shape = v.shape

    def body(v_ref, o_ref, slots, send_sems, recv_sems):
        x, y, c = _place()
        me = 4 * x + 2 * y + c
        slots[me] = v_ref[...]
        for r in range(1, N_DEV):
            peer = (x ^ (r >> 2), y ^ ((r >> 1) & 1), c ^ (r & 1))
            pltpu.make_async_remote_copy(src_ref=v_ref, dst_ref=slots.at[me], send_sem=send_sems.at[r - 1], recv_sem=recv_sems.at[r - 1],
                                         device_id=peer, device_id_type=MESH).start()
        for r in range(1, N_DEV):
            peer = (x ^ (r >> 2), y ^ ((r >> 1) & 1), c ^ (r & 1))
            cp = pltpu.make_async_remote_copy(src_ref=v_ref, dst_ref=slots.at[4 * peer[0] + 2 * peer[1] + peer[2]], send_sem=send_sems.at[r - 1],
                                              recv_sem=recv_sems.at[r - 1], device_id=peer, device_id_type=MESH)
            cp.wait_recv()
            cp.wait_send()
        acc = slots[0]
        for d in range(1, N_DEV):
            acc = acc + slots[d]
        o_ref[...] = acc

    vm = pl.BlockSpec(memory_space=pltpu.VMEM)
    return pl.pallas_call(
        body, in_specs=[vm], out_specs=vm, out_shape=jax.ShapeDtypeStruct(shape, F32),
        scratch_shapes=[pltpu.VMEM((N_DEV,) + shape, F32), pltpu.SemaphoreType.DMA((N_DEV - 1,)), pltpu.SemaphoreType.DMA((N_DEV - 1,))],
        name=name,
    )(v)


def _add_sibling(g, gsib, core, name):
    n, _, cdim = g.shape
    half = gsib.shape[1]
    tr = half // 2

    def body(core_ref, a_ref, b_ref, o_ref):
        o_ref[...] = (a_ref[...] + b_ref[...]).astype(o_ref.dtype)

    blk = pl.BlockSpec((1, tr, cdim), lambda j, i, core_ref: (j, i, 0))
    return pl.pallas_call(
        body,
        grid_spec=pltpu.PrefetchScalarGridSpec(
            num_scalar_prefetch=1, grid=(n, half // tr),
            in_specs=[pl.BlockSpec((1, tr, cdim), lambda j, i, core_ref: (j, core_ref[0] * (half // tr) + i, 0)), blk], out_specs=blk),
        out_shape=jax.ShapeDtypeStruct(gsib.shape, BF16), compiler_params=_params(("parallel", "parallel")), name=name,
    )(core, g, gsib)


def _sum_parts(parts, name):
    n, r, cdim = parts.shape
    tr = r // 2

    def body(p_ref, o_ref):
        acc = p_ref[0].astype(F32)
        for j in range(1, n):
            acc = acc + p_ref[j].astype(F32)
        o_ref[...] = acc

    return pl.pallas_call(
        body, grid=(r // tr,), in_specs=[pl.BlockSpec((n, tr, cdim), lambda i: (0, i, 0))],
        out_specs=pl.BlockSpec((tr, cdim), lambda i: (i, 0)), out_shape=jax.ShapeDtypeStruct((r, cdim), F32),
        compiler_params=_params(("parallel",)), name=name,
    )(parts)


def _adamw(parts, w, m, v, name):
    npart, r, cdim = parts.shape
    tr = r // 4 if r % 32 == 0 else r

    def body(p_ref, w_ref, m_ref, v_ref, g_ref, d_ref, nm_ref, nv_ref):
        g = p_ref[0]
        for j in range(1, npart):
            g = g + p_ref[j]
        m_new = ADAM_B1 * m_ref[...] + (1.0 - ADAM_B1) * g
        v_new = ADAM_B2 * v_ref[...] + (1.0 - ADAM_B2) * (g * g)
        m_hat = m_new / (1.0 - ADAM_B1 ** ADAM_STEP)
        v_hat = v_new / (1.0 - ADAM_B2 ** ADAM_STEP)
        g_ref[...] = g
        d_ref[...] = -ADAM_LR * (m_hat / (jnp.sqrt(v_hat) + ADAM_EPS) + ADAM_WD * w_ref[...])
        nm_ref[...] = m_new
        nv_ref[...] = v_new

    blk = pl.BlockSpec((tr, cdim), lambda i: (i, 0))
    shp = jax.ShapeDtypeStruct((r, cdim), F32)
    return pl.pallas_call(
        body, grid=(r // tr,), in_specs=[pl.BlockSpec((npart, tr, cdim), lambda i: (0, i, 0)), blk, blk, blk], out_specs=[blk] * 4,
        out_shape=[shp] * 4, compiler_params=_params(("parallel",)), name=name,
    )(parts, w, m, v)


BIG = ("ev_w_in", "ev_w_uq", "ev_w_ukv", "ev_w_out", "od_w_in", "od_w_out", "meta")
SMALL = ("norm_g", "final_g", "ev_q_norm_g", "ev_kv_norm_g", "od_sinks")
SMALL_SHAPE = (8, 512)
BY_ROWS = ("ev_w_out", "od_w_out")

EV_IN_SHARD, OD_IN_SHARD, UQ_SHARD = 2976 // N_CHIPS, 2304 // N_CHIPS, 768 // N_CHIPS
OD_IN_AT, UQ_AT, CORNER_AT = 768, 2048, 2304


def _pack_big(a, lead=()):
    dtype = a["ev_w_in"].dtype
    z = lambda r, c: jnp.zeros(lead + (r, c), dtype)
    ax = len(lead)
    pa = jnp.concatenate([a["ev_w_in"], z(D_MODEL, OD_IN_AT - EV_IN_SHARD), a["od_w_in"], z(D_MODEL, 1408 - OD_IN_AT - OD_IN_SHARD)],
                         axis=ax + 1)
    corner = jnp.concatenate([a["ev_w_ukv"], a["meta"], z(256 - MLA_KV_LORA - N_META, 256)], axis=ax)
    pb = jnp.concatenate([a["ev_w_out"], a["od_w_out"], a["ev_w_uq"], z(256, CORNER_AT - UQ_AT - UQ_SHARD), corner], axis=ax + 1)
    return pa, pb


def _unpack_big(pa, pb):
    return dict(ev_w_in=pa[..., :EV_IN_SHARD], od_w_in=pa[..., OD_IN_AT:OD_IN_AT + OD_IN_SHARD], ev_w_out=pb[..., :D_MODEL],
                od_w_out=pb[..., D_MODEL:2 * D_MODEL], ev_w_uq=pb[..., UQ_AT:UQ_AT + UQ_SHARD],
                ev_w_ukv=pb[..., :MLA_KV_LORA, CORNER_AT:], meta=pb[..., MLA_KV_LORA:MLA_KV_LORA + N_META, CORNER_AT:])


def _chip_shards(full, by_rows):
    if by_rows:
        return full.reshape(N_CHIPS, full.shape[0] // N_CHIPS, full.shape[1])
    return full.reshape(full.shape[0], N_CHIPS, -1).transpose(1, 0, 2)


def _from_chip_shards(slots, by_rows):
    if by_rows:
        return slots.reshape(-1, slots.shape[2])
    return slots.transpose(1, 0, 2).reshape(slots.shape[1], -1)


def _pack_small(arrs, extra=None):
    flat = [a.reshape(-1) for a in arrs] + ([] if extra is None else [extra.reshape(-1)])
    used = sum(f.shape[0] for f in flat)
    return jnp.pad(jnp.concatenate(flat), (0, SMALL_SHAPE[0] * SMALL_SHAPE[1] - used)).reshape(SMALL_SHAPE)


def _unpack_small(p, shapes):
    flat, out, at = p.reshape(-1), [], 0
    for s in shapes:
        n = int(np.prod(s))
        out.append(flat[at:at + n].reshape(s))
        at += n
    return out, flat[at]


def kernel(x, meta, norm_g, final_g, ev_w_in, ev_q_norm_g, ev_kv_norm_g, ev_w_uq, ev_w_ukv, ev_w_out, od_w_in, od_sinks, od_w_out, loss_target, m_meta, m_norm_g, m_final_g, m_ev_w_in, m_ev_q_norm_g, m_ev_kv_norm_g, m_ev_w_uq, m_ev_w_ukv, m_ev_w_out, m_od_w_in, m_od_sinks, m_od_w_out, v_meta, v_norm_g, v_final_g, v_ev_w_in, v_ev_q_norm_g, v_ev_kv_norm_g, v_ev_w_uq, v_ev_w_ukv, v_ev_w_out, v_od_w_in, v_od_sinks, v_od_w_out):
    given = dict(locals())
    two_d = lambda a: a[0] if a.ndim == 3 else a
    packs = {k: _pack_big({n: two_d(given[k + n]) for n in BIG}) for k in ("", "m_", "v_")}

    gathered, meta_all = _gather_weights([p.astype(BF16) for p in packs[""]], meta, "gather_weights")
    full = {n: _from_chip_shards(a, n in BY_ROWS) for n, a in _unpack_big(*gathered).items()}
    meta_full = _from_chip_shards(meta_all, False)

    loss, grad_x, grads = _local_step(x, meta_full, norm_g, final_g, ev_q_norm_g, ev_kv_norm_g, od_sinks, loss_target,
                                      full["ev_w_in"], full["ev_w_uq"], full["ev_w_ukv"], full["ev_w_out"], full["od_w_in"], full["od_w_out"])

    gpacks = _pack_big({n: _chip_shards(grads[n], n in BY_ROWS) for n in BIG}, lead=(N_CHIPS,))
    core = lax.axis_index("c").astype(jnp.int32).reshape(1)
    from_sibling = _grads_to_sibling(gpacks, "grads_to_sibling")
    sums = [_add_sibling(g, s, core, "add_sibling_" + t) for g, s, t in zip(gpacks, from_sibling, "ab")]
    parts = _chip_scatter(sums, "grads_to_chips")
    reduced = _share_halves([_sum_parts(p, "add_chips_" + t) for p, t in zip(parts, "ab")], "reduced_to_sibling")
    updated = [_adamw(reduced[i][None], packs[""][i], packs["m_"][i], packs["v_"][i], "adamw_matrices_" + t) for i, t in enumerate("ab")]
    big_out = [{n: a.reshape(given[n].shape) for n, a in _unpack_big(oa, ob).items()} for oa, ob in zip(*updated)]

    small_shapes = [given[n].shape for n in SMALL]
    ssum = _all_reduce_small(_pack_small([grads[n] for n in SMALL], loss[0, 0]), "reduce_vectors")
    small_out = _adamw(ssum[None], _pack_small([given[n] for n in SMALL]), _pack_small([given["m_" + n] for n in SMALL]),
                       _pack_small([given["v_" + n] for n in SMALL]), "adamw_vectors")
    total_loss = ssum.reshape(-1)[sum(int(np.prod(s)) for s in small_shapes)]
    small_out = [_unpack_small(o, small_shapes)[0] for o in small_out]

    names = ("meta", "norm_g", "final_g", "ev_w_in", "ev_q_norm_g", "ev_kv_norm_g", "ev_w_uq", "ev_w_ukv", "ev_w_out", "od_w_in", "od_sinks",
             "od_w_out")
    outs = [total_loss, grad_x]
    for kind in range(4):
        for n in names:
            outs.append(big_out[kind][n] if n in BIG else small_out[kind][SMALL.index(n)])
    return tuple(outs)
```

```python
import functools
import math

import numpy as np
import jax
import jax.numpy as jnp
from jax import lax
from jax.experimental import pallas as pl
from jax.experimental.pallas import tpu as pltpu
from jax.experimental.pallas import tpu_sc as plsc

F32 = jnp.float32
BF16 = jnp.bfloat16

D_MODEL = 1024
BLOCK = 128
N_META = 16
N_PAD = BLOCK - N_META
NORM_EPS = 1e-6
NEG = -1e30
HEAD = 64
SB_HEADS = 8
MLA_HEADS = 8
MLA_Q_LORA = 256
MLA_KV_LORA = 128
MLA_NOPE = 64
MLA_ROPE = 32
ROPE_BASE = 10000.0
SWA_HEADS = 16
SWA_KV_HEADS = 2
SWA_WINDOW = 128
N_CHIPS = 4
N_DEV = 8

ADAM_LR = 0.001
ADAM_B1 = 0.9
ADAM_B2 = 0.999
ADAM_EPS = 1e-08
ADAM_WD = 0.01
ADAM_STEP = 10

VMEM_LIMIT = 48 * 1024 * 1024

EV_G, EV_Q, EV_K, EV_V, EV_CQ, EV_CKV, EV_KR, EV_N = 0, 1024, 1536, 2048, 2560, 2816, 2944, 3072
OD_G, OD_Q, OD_K, OD_V, OD_N = 0, 1024, 2048, 2176, 2304


def _params(sem=None):
    return pltpu.CompilerParams(dimension_semantics=sem, vmem_limit_bytes=VMEM_LIMIT)


def _row_tile(m):
    return 256 if m % 256 == 0 else 128


def _matmul_rows(m):
    for c in (1088, 1024, 768, 640, 512, 384, 256):
        if m % c == 0:
            return c
    return 128


def _dot(a, b):
    return jnp.dot(a.astype(BF16), b.astype(BF16), preferred_element_type=F32)


def _dot_nt(a, b):
    return lax.dot_general(a.astype(BF16), b.astype(BF16), (((1,), (1,)), ((), ())), preferred_element_type=F32)


def _dot_tn(a, b):
    return lax.dot_general(a.astype(BF16), b.astype(BF16), (((0,), (0,)), ((), ())), preferred_element_type=F32)


def _rms_fwd(h, g, name):
    t, d = h.shape
    tm = _row_tile(t)

    def body(h_ref, g_ref, o_ref):
        x = h_ref[...]
        r = lax.rsqrt(jnp.mean(x * x, axis=-1, keepdims=True) + NORM_EPS)
        o_ref[...] = ((x * r) * g_ref[...]).astype(o_ref.dtype)

    return pl.pallas_call(
        body, grid=(t // tm,),
        in_specs=[pl.BlockSpec((tm, d), lambda i: (i, 0)), pl.BlockSpec((1, d), lambda i: (0, 0))],
        out_specs=pl.BlockSpec((tm, d), lambda i: (i, 0)),
        out_shape=jax.ShapeDtypeStruct((t, d), BF16), compiler_params=_params(("parallel",)), name=name,
    )(h, g)


def _rms_bwd(h, g, dy, dres, name):
    t, d = h.shape
    tm = _row_tile(t)

    def body(h_ref, g_ref, dy_ref, dres_ref, dh_ref, dg_ref):
        @pl.when(pl.program_id(0) == 0)
        def _():
            dg_ref[...] = jnp.zeros_like(dg_ref)

        x = h_ref[...]
        r = lax.rsqrt(jnp.mean(x * x, axis=-1, keepdims=True) + NORM_EPS)
        xr = x * r
        dy_ = dy_ref[...]
        u = dy_ * g_ref[...]
        dh_ref[...] = dres_ref[...] + r * (u - xr * jnp.mean(u * xr, axis=-1, keepdims=True))
        dg_ref[...] += jnp.sum(dy_ * xr, axis=0, keepdims=True)

    row = pl.BlockSpec((tm, d), lambda i: (i, 0))
    vec = pl.BlockSpec((1, d), lambda i: (0, 0))
    return pl.pallas_call(
        body, grid=(t // tm,), in_specs=[row, vec, row, row], out_specs=[row, vec],
        out_shape=[jax.ShapeDtypeStruct((t, d), F32), jax.ShapeDtypeStruct((1, d), F32)],
        compiler_params=_params(("arbitrary",)), name=name,
    )(h, g, dy, dres)


def _col_tile(n):
    for c in (1024, 768, 640, 512, 384, 256, 128):
        if n % c == 0:
            return c
    return n


def _mm(a, w, name, res=None, out_dtype=F32, a_cols=None):
    m = a.shape[0]
    k, n = w.shape
    a_blk = 0 if a_cols is None else a_cols[0] // k
    assert a_cols is None or (a_cols[1] == k and a_cols[0] % k == 0)
    tm, tn = _matmul_rows(m), _col_tile(n)

    def body(*refs):
        if res is None:
            a_ref, w_ref, o_ref = refs
            acc = _dot(a_ref[...], w_ref[...])
        else:
            a_ref, w_ref, r_ref, o_ref = refs
            acc = r_ref[...] + _dot(a_ref[...], w_ref[...])
        o_ref[...] = acc.astype(o_ref.dtype)

    in_specs = [pl.BlockSpec((tm, k), lambda j, i: (i, a_blk)), pl.BlockSpec((k, tn), lambda j, i: (0, j))]
    args = [a, w]
    if res is not None:
        in_specs.append(pl.BlockSpec((tm, tn), lambda j, i: (i, j)))
        args.append(res)
    return pl.pallas_call(
        body, grid=(n // tn, m // tm), in_specs=in_specs, out_specs=pl.BlockSpec((tm, tn), lambda j, i: (i, j)),
        out_shape=jax.ShapeDtypeStruct((m, n), out_dtype), compiler_params=_params(("parallel", "parallel")), name=name,
    )(*args)


def _mm_nt(a, w, name):
    m, n = a.shape
    k = w.shape[0]
    tm, tk = _matmul_rows(m), _col_tile(k)

    def body(a_ref, w_ref, o_ref):
        o_ref[...] = _dot_nt(a_ref[...], w_ref[...])

    return pl.pallas_call(
        body, grid=(k // tk, m // tm),
        in_specs=[pl.BlockSpec((tm, n), lambda j, i: (i, 0)), pl.BlockSpec((tk, n), lambda j, i: (j, 0))],
        out_specs=pl.BlockSpec((tm, tk), lambda j, i: (i, j)),
        out_shape=jax.ShapeDtypeStruct((m, k), F32), compiler_params=_params(("parallel", "parallel")), name=name,
    )(a, w)


def _mm_tn(x, dy, name):
    m, k = x.shape
    n = dy.shape[1]
    tm, tn = _matmul_rows(m), _col_tile(n)

    def body(x_ref, dy_ref, o_ref):
        @pl.when(pl.program_id(1) == 0)
        def _():
            o_ref[...] = jnp.zeros_like(o_ref)

        o_ref[...] += _dot_tn(x_ref[...], dy_ref[...])

    return pl.pallas_call(
        body, grid=(n // tn, m // tm),
        in_specs=[pl.BlockSpec((tm, k), lambda j, i: (i, 0)), pl.BlockSpec((tm, tn), lambda j, i: (i, j))],
        out_specs=pl.BlockSpec((k, tn), lambda j, i: (0, j)),
        out_shape=jax.ShapeDtypeStruct((k, n), F32), compiler_params=_params(("parallel", "arbitrary")), name=name,
    )(x, dy)


def _silu_parts(g):
    s = 1.0 / (1.0 + jnp.exp(-g))
    return g * s, s * (1.0 + g * (1.0 - s))


def _gate_fwd(o_parts, proj, name):
    t = proj.shape[0]
    tm = _row_tile(t)
    w = D_MODEL // len(o_parts)

    def body(*refs):
        g_ref, o_ref = refs[-2], refs[-1]
        for p, r in enumerate(refs[:-2]):
            sil, _ = _silu_parts(g_ref[:, p * w:(p + 1) * w])
            o_ref[:, p * w:(p + 1) * w] = (r[...].astype(F32) * sil).astype(o_ref.dtype)

    return pl.pallas_call(
        body, grid=(t // tm,),
        in_specs=[pl.BlockSpec((tm, w), lambda i: (i, 0)) for _ in o_parts] + [pl.BlockSpec((tm, D_MODEL), lambda i: (i, 0))],
        out_specs=pl.BlockSpec((tm, D_MODEL), lambda i: (i, 0)),
        out_shape=jax.ShapeDtypeStruct((t, D_MODEL), BF16), compiler_params=_params(("parallel",)), name=name,
    )(*o_parts, proj)


def _gate_bwd(dao, o_parts, proj, name):
    t = proj.shape[0]
    tm = _row_tile(t)
    np_ = len(o_parts)
    w = D_MODEL // np_

    def body(*refs):
        dao_ref, g_ref = refs[0], refs[1 + np_]
        do_refs, dg_ref = refs[2 + np_:2 + 2 * np_], refs[-1]
        for p in range(np_):
            sl = slice(p * w, (p + 1) * w)
            sil, dsil = _silu_parts(g_ref[:, sl])
            da = dao_ref[:, sl]
            do_refs[p][...] = da * sil
            dg_ref[:, sl] = (da * refs[1 + p][...].astype(F32) * dsil).astype(dg_ref.dtype)

    full = pl.BlockSpec((tm, D_MODEL), lambda i: (i, 0))
    part = pl.BlockSpec((tm, w), lambda i: (i, 0))
    outs = pl.pallas_call(
        body, grid=(t // tm,), in_specs=[full] + [part] * np_ + [full], out_specs=[part] * np_ + [full],
        out_shape=[jax.ShapeDtypeStruct((t, w), F32)] * np_ + [jax.ShapeDtypeStruct((t, D_MODEL), BF16)],
        compiler_params=_params(("parallel",)), name=name,
    )(dao, *o_parts, proj)
    return outs[:np_], outs[np_]


def _loss_head(h2, gf, target, b, lp):
    d = h2.shape[1]
    nb = lp // BLOCK
    h3 = h2.reshape(b, lp, d)

    def body(h_ref, g_ref, t_ref, dh_ref, dg_ref, loss_ref):
        first = (pl.program_id(0) == 0) & (pl.program_id(1) == 0)

        @pl.when(first)
        def _():
            dg_ref[...] = jnp.zeros_like(dg_ref)
            loss_ref[...] = jnp.zeros_like(loss_ref)

        @pl.when(pl.program_id(1) == 0)
        def _():
            dh_ref[...] = jnp.zeros_like(dh_ref)

        @pl.when(pl.program_id(1) > 0)
        def _():
            x = h_ref[0]
            r = lax.rsqrt(jnp.mean(x * x, axis=-1, keepdims=True) + NORM_EPS)
            xr = x * r
            g = g_ref[...]
            diff = xr * g - t_ref[0]
            loss_ref[...] += 0.5 * jnp.sum(jnp.mean(diff * diff, axis=-1, keepdims=True))
            dy = diff * (1.0 / d)
            u = dy * g
            dh_ref[0] = r * (u - xr * jnp.mean(u * xr, axis=-1, keepdims=True))
            dg_ref[...] += jnp.sum(dy * xr, axis=0, keepdims=True)

    blk = pl.BlockSpec((1, BLOCK, d), lambda bi, n: (bi, n, 0))
    dh, dg, loss = pl.pallas_call(
        body, grid=(b, nb),
        in_specs=[blk, pl.BlockSpec((1, d), lambda bi, n: (0, 0)),
                  pl.BlockSpec((1, BLOCK, d), lambda bi, n: (bi, jnp.maximum(n - 1, 0), 0))],
        out_specs=[blk, pl.BlockSpec((1, d), lambda bi, n: (0, 0)), pl.BlockSpec((8, 128), lambda bi, n: (0, 0))],
        out_shape=[jax.ShapeDtypeStruct((b, lp, d), F32), jax.ShapeDtypeStruct((1, d), F32), jax.ShapeDtypeStruct((8, 128), F32)],
        compiler_params=_params(("arbitrary", "arbitrary")), name="loss_head",
    )(h3, gf, target)
    return dh.reshape(b * lp, d), dg, loss


def _iota2(shape, dim):
    return lax.broadcasted_iota(jnp.int32, shape, dim)


KEYS = 256


def _lo_lanes():
    return _iota2((1, BLOCK), 1) < HEAD


def _halves(x, lo):
    zero = jnp.zeros_like(x)
    return jnp.where(lo, x, zero), jnp.where(lo, zero, x)


def _rows_of_pair(a, b):
    return jnp.where(_iota2((BLOCK, 1), 0) < HEAD, a, b)


def _split_rows_t(x):
    xt = x.T
    first = _iota2(xt.shape, 0) < HEAD
    zero = jnp.zeros_like(xt)
    return jnp.concatenate([jnp.where(first, xt, zero), jnp.where(first, zero, xt)], axis=1).astype(BF16)


def _key_chunk(c, lp, t_idx, strict, key_axis):
    first = c * KEYS
    s0 = pl.multiple_of(jnp.minimum(first, lp - KEYS), BLOCK)
    s_idx = s0 + _iota2(t_idx.shape, key_axis)
    seen = (s_idx < t_idx) if strict else (s_idx <= t_idx)
    return s0, seen & (s_idx >= jnp.maximum(first, N_PAD))


def _split_dot(x, tri):
    hi = x.astype(BF16)
    lo = (x - hi.astype(F32)).astype(BF16)
    return jnp.dot(hi, tri, preferred_element_type=F32) + jnp.dot(lo, tri, preferred_element_type=F32)


def _stack_halves(x, lo):
    a, b = _halves(x, lo)
    return jnp.concatenate([a, b], axis=0)


def _pair(a, b, lo):
    return jnp.where(lo, a, b)


def _chunk_starts(lp):
    return [min(c * KEYS, lp - KEYS) for c in range(-(-lp // KEYS))]


def _put_rows(ref, r0, bq, a, b):
    for t in range(bq // BLOCK):
        part = slice(t * BLOCK, (t + 1) * BLOCK)
        ref[0, 0, r0 // BLOCK + t] = jnp.concatenate([a[:, part], b[:, part], jnp.zeros((6, BLOCK), F32)], axis=0)


def _get_rows(ref, r0, bq):
    return [jnp.concatenate([ref[0, 0, r0 // BLOCK + t, h:h + 1, :] for t in range(bq // BLOCK)], axis=1) for h in range(2)]


def _n_chunks(i):
    return (i + 2) // 2


QROWS = 512


def _for_query_tiles(nb, tile):
    per = QROWS // BLOCK

    def step(j, _):
        tile(pl.multiple_of(j * QROWS, QROWS), QROWS, (j + 1) * (QROWS // KEYS))
        return 0

    lax.fori_loop(0, nb // per, step, 0)
    for i in range(nb - nb % per, nb):
        tile(i * BLOCK, BLOCK, _n_chunks(i))


def _walk_chunks(r0, n, chunk, carry, leftwards=False):
    diag = jnp.maximum(r0 // KEYS, 1)

    def span(first, last, masked, carry):
        def step(t, cr):
            return chunk(last - 1 - t if leftwards else first + t, cr, masked)
        return lax.fori_loop(0, last - first, step, carry)

    spans = [(0, 1, True), (1, diag, False), (diag, n, True)]
    for first, last, masked in (reversed(spans) if leftwards else spans):
        carry = span(first, last, masked, carry)
    return carry


def _where(valid, x, other):
    return x if valid is None else jnp.where(valid, x, other)


def _sb_scores(q_h, k, valid, after):
    z = _dot_nt(q_h, k) * (HEAD ** -0.5)
    sp = jnp.log(1.0 + jnp.exp(-jnp.abs(z)))
    lb = jnp.minimum(z, 0.0) - sp
    l1m_all = -jnp.maximum(z, 0.0) - sp
    l1m = _where(valid, l1m_all, 0.0)
    return lb, l1m_all, l1m, _split_dot(l1m, after)


def _pair_stat_spec(nb):
    return pl.BlockSpec((1, 1, nb, 8, BLOCK), lambda bi, hp: (bi, hp, 0, 0, 0))


def _sb_fwd(proj3, name):
    b, lp, _ = proj3.shape
    nb = lp // BLOCK
    npair = SB_HEADS // 2

    def body(q_ref, k_ref, v_ref, o_ref, tot_ref):
        lo = _lo_lanes()
        after = (_iota2((KEYS, KEYS), 0) > _iota2((KEYS, KEYS), 1)).astype(BF16)

        def qtile(r0, bq, n):
            qs = _halves(q_ref[0, pl.ds(r0, bq), :].astype(BF16), lo)
            t_idx = r0 + _iota2((bq, KEYS), 0)

            def kchunk(c, carry, masked):
                cs, acc = carry[:2], carry[2]
                s0, valid = _key_chunk(c, lp, t_idx, True, 1)
                valid = valid if masked else None
                k = k_ref[0, pl.ds(s0, KEYS), :].astype(BF16)
                a_s, new = [], []
                for h in range(2):
                    lb, _, l1m, suf = _sb_scores(qs[h], k, valid, after)
                    a_s.append(_where(valid, jnp.exp(lb + suf + cs[h]), 0.0).astype(BF16))
                    new.append(cs[h] + jnp.sum(l1m, axis=1, keepdims=True))
                v_bd = _stack_halves(v_ref[0, pl.ds(s0, KEYS), :].astype(BF16), lo)
                return (*new, acc + jnp.dot(jnp.concatenate(a_s, axis=1), v_bd, preferred_element_type=F32))

            zero = jnp.zeros((bq, 1), F32)
            c_a, c_b, acc = _walk_chunks(r0, n, kchunk, (zero, zero, jnp.zeros((bq, BLOCK), F32)), leftwards=True)
            o_ref[0, pl.ds(r0, bq), :] = acc
            tot_ref[0, pl.ds(r0, bq), :] = jnp.broadcast_to(_pair(c_a, c_b, lo), (bq, BLOCK))

        _for_query_tiles(nb, qtile)

    def col(first):
        return pl.BlockSpec((1, lp, 2 * HEAD), lambda bi, hp: (bi, 0, first // (2 * HEAD) + hp))

    shp = jax.ShapeDtypeStruct((b, lp, SB_HEADS * HEAD), F32)
    return pl.pallas_call(
        body, grid=(b, npair), in_specs=[col(EV_Q), col(EV_K), col(EV_V)], out_specs=[col(0), col(0)], out_shape=[shp, shp],
        compiler_params=_params(("parallel", "parallel")), name=name,
    )(proj3, proj3, proj3)


def _sb_bwd(proj3, tot, do, name):
    b, lp, _ = proj3.shape
    nb = lp // BLOCK
    npair = SB_HEADS // 2

    def body(q_ref, k_ref, v_ref, tot_ref, do_ref, dq_ref, dk_ref, dv_ref):
        lo = _lo_lanes()
        after = (_iota2((KEYS, KEYS), 0) > _iota2((KEYS, KEYS), 1)).astype(BF16)
        before = (_iota2((KEYS, KEYS), 0) < _iota2((KEYS, KEYS), 1)).astype(BF16)
        dk_ref[...] = jnp.zeros_like(dk_ref)
        dv_ref[...] = jnp.zeros_like(dv_ref)

        def qtile(r0, bq, n):
            rows = pl.ds(r0, bq)
            qs = _halves(q_ref[0, rows, :].astype(BF16), lo)
            dos = _halves(do_ref[0, rows, :].astype(BF16), lo)
            tot_i = tot_ref[0, rows, :]
            tots = (tot_i[:, 0:1], tot_i[:, HEAD:HEAD + 1])
            q_st, do_st = jnp.concatenate(qs, axis=0), jnp.concatenate(dos, axis=0)
            t_idx = r0 + _iota2((bq, KEYS), 0)

            def kchunk(c, carry, masked):
                s0, valid = _key_chunk(c, lp, t_idx, True, 1)
                valid = valid if masked else None
                keys = pl.ds(s0, KEYS)
                k = k_ref[0, keys, :].astype(BF16)
                v = v_ref[0, keys, :].astype(BF16)
                a_s, dzs, new = [], [], []
                for h in range(2):
                    left, pre = carry[2 * h], carry[2 * h + 1]
                    lb, l1m_all, l1m, suf = _sb_scores(qs[h], k, valid, after)
                    here = jnp.sum(l1m, axis=1, keepdims=True)
                    a = _where(valid, jnp.exp(lb + suf + (tots[h] - left - here)), 0.0)
                    w = a * _dot_nt(dos[h], v)
                    dz = _where(valid, w * jnp.exp(l1m_all) - (pre + _split_dot(w, before)) * jnp.exp(lb), 0.0) * (HEAD ** -0.5)
                    new += [left + here, pre + jnp.sum(w, axis=1, keepdims=True)]
                    a_s.append(a.astype(BF16))
                    dzs.append(dz.astype(BF16))
                dk_ref[0, keys, :] += _dot_tn(jnp.concatenate(dzs, axis=0), q_st)
                dv_ref[0, keys, :] += _dot_tn(jnp.concatenate(a_s, axis=0), do_st)
                dq = carry[4] + jnp.dot(jnp.concatenate(dzs, axis=1), _stack_halves(k, lo), preferred_element_type=F32)
                return (*new, dq)

            zero = jnp.zeros((bq, 1), F32)
            out = _walk_chunks(r0, n, kchunk, (zero, zero, zero, zero, jnp.zeros((bq, BLOCK), F32)))
            dq_ref[0, rows, :] = out[4]

        _for_query_tiles(nb, qtile)

    def col(first):
        return pl.BlockSpec((1, lp, 2 * HEAD), lambda bi, hp: (bi, 0, first // (2 * HEAD) + hp))

    shp = jax.ShapeDtypeStruct((b, lp, SB_HEADS * HEAD), F32)
    return pl.pallas_call(
        body, grid=(b, npair), in_specs=[col(EV_Q), col(EV_K), col(EV_V), col(0), col(0)], out_specs=[col(0)] * 3, out_shape=[shp] * 3,
        compiler_params=_params(("parallel", "parallel")), name=name,
    )(proj3, proj3, proj3, tot, do)


def _rope_tables(lp):
    half = MLA_ROPE // 2
    pos = (np.arange(lp) - N_PAD).astype(np.float32)
    inv = jnp.asarray(ROPE_BASE, F32) ** (-jnp.arange(half, dtype=F32) / half)
    ang = jnp.asarray(pos)[:, None] * inv[None, :]
    cos, sin = jnp.cos(ang), jnp.sin(ang)
    zeros = lambda n: jnp.zeros((lp, n), F32)
    c = jnp.concatenate([jnp.ones((lp, MLA_NOPE), F32), cos, cos, zeros(32)], axis=1)
    s1 = jnp.concatenate([zeros(MLA_NOPE), -sin, zeros(half), zeros(32)], axis=1)
    s2 = jnp.concatenate([zeros(MLA_NOPE), zeros(half), sin, zeros(32)], axis=1)
    return c, s1, s2


def _rope(x, c, s1, s2):
    half = MLA_ROPE // 2
    return x * c + pltpu.roll(x, BLOCK - half, 1) * s1 + pltpu.roll(x, half, 1) * s2


def _rope_t(dy, c, s1, s2):
    half = MLA_ROPE // 2
    return dy * c + pltpu.roll(dy * s1, half, 1) + pltpu.roll(dy * s2, BLOCK - half, 1)


def _rms_rows(x, g):
    r = lax.rsqrt(jnp.mean(x * x, axis=-1, keepdims=True) + NORM_EPS)
    return x * r, r


def _mla_prep_fwd(proj3, gq, gkv, wq, wk, wv, tabs, name):
    b, lp, _ = proj3.shape
    nb = lp // BLOCK
    hw = MLA_HEADS * BLOCK

    def body(cq_ref, ckv_ref, kr_ref, gq_ref, gkv_ref, wq_ref, wk_ref, wv_ref, c_ref, s1_ref, s2_ref, qf_ref, kf_ref, v_ref):
        c, s1, s2 = c_ref[...], s1_ref[...], s2_ref[...]
        xq, _ = _rms_rows(cq_ref[0], None)
        qh = _dot(xq * gq_ref[...], wq_ref[...])
        xk, _ = _rms_rows(ckv_ref[0], None)
        ckv_n = xk * gkv_ref[...]
        kv = _dot(ckv_n, wk_ref[...])
        v_ref[0] = _dot(ckv_n, wv_ref[...]).astype(v_ref.dtype)
        kr = _rope(kr_ref[0], c, s1, s2)
        for h in range(MLA_HEADS):
            ls = slice(h * BLOCK, (h + 1) * BLOCK)
            qf_ref[0, :, ls] = _rope(qh[:, ls], c, s1, s2).astype(qf_ref.dtype)
            kf_ref[0, :, ls] = (kv[:, ls] + kr).astype(kf_ref.dtype)

    def col(first, width):
        return pl.BlockSpec((1, BLOCK, width), lambda bi, n: (bi, n, first // width))

    def whole(a):
        return pl.BlockSpec(a.shape, lambda bi, n: (0,) * a.ndim)

    tab = pl.BlockSpec((BLOCK, BLOCK), lambda bi, n: (n, 0))
    return pl.pallas_call(
        body, grid=(b, nb),
        in_specs=[col(EV_CQ, MLA_Q_LORA), col(EV_CKV, MLA_KV_LORA), col(EV_KR, BLOCK), whole(gq), whole(gkv), whole(wq), whole(wk),
                  whole(wv), tab, tab, tab],
        out_specs=[col(0, hw), col(0, hw), col(0, MLA_HEADS * HEAD)],
        out_shape=[jax.ShapeDtypeStruct((b, lp, hw), BF16), jax.ShapeDtypeStruct((b, lp, hw), BF16),
                   jax.ShapeDtypeStruct((b, lp, MLA_HEADS * HEAD), BF16)],
        compiler_params=_params(("parallel", "parallel")), name=name,
    )(proj3, proj3, proj3, gq, gkv, wq, wk, wv, *tabs)


def _mla_prep_bwd(proj3, gq, gkv, wq, wk, wv, tabs, dqf, dkf, dv, name):
    b, lp, _ = proj3.shape
    nb = lp // BLOCK
    hw = MLA_HEADS * BLOCK

    def body(cq_ref, ckv_ref, gq_ref, gkv_ref, wq_ref, wk_ref, wv_ref, c_ref, s1_ref, s2_ref, dqf_ref, dkf_ref, dv_ref,
             dcq_ref, dckv_ref, dkr_ref, dwq_ref, dwk_ref, dwv_ref, dgq_ref, dgkv_ref, dqh):
        @pl.when((pl.program_id(0) == 0) & (pl.program_id(1) == 0))
        def _():
            for r in (dwq_ref, dwk_ref, dwv_ref, dgq_ref, dgkv_ref):
                r[...] = jnp.zeros_like(r)

        c, s1, s2 = c_ref[...], s1_ref[...], s2_ref[...]
        dkr = jnp.zeros((BLOCK, BLOCK), F32)
        for h in range(MLA_HEADS):
            ls = slice(h * BLOCK, (h + 1) * BLOCK)
            dqh[:, ls] = _rope_t(dqf_ref[0, :, ls].astype(F32), c, s1, s2).astype(dqh.dtype)
            dkr = dkr + dkf_ref[0, :, ls].astype(F32)
        dkr_ref[0] = _rope_t(dkr, c, s1, s2).astype(dkr_ref.dtype)

        def norm_bwd(x, g, dy, dg_ref):
            xr, r = _rms_rows(x, None)
            u = dy * g
            dg_ref[...] += jnp.sum(dy * xr, axis=0, keepdims=True)
            return r * (u - xr * jnp.mean(u * xr, axis=-1, keepdims=True))

        xq, _ = _rms_rows(cq_ref[0], None)
        cq_n = xq * gq_ref[...]
        dwq_ref[...] += _dot_tn(cq_n, dqh[...])
        dcq_ref[0] = norm_bwd(cq_ref[0], gq_ref[...], _dot_nt(dqh[...], wq_ref[...]), dgq_ref).astype(dcq_ref.dtype)
        xk, _ = _rms_rows(ckv_ref[0], None)
        ckv_n = xk * gkv_ref[...]
        dkf_, dv_ = dkf_ref[0], dv_ref[0]
        dwk_ref[...] += _dot_tn(ckv_n, dkf_)
        dwv_ref[...] += _dot_tn(ckv_n, dv_)
        dckv_n = _dot_nt(dkf_, wk_ref[...]) + _dot_nt(dv_, wv_ref[...])
        dckv_ref[0] = norm_bwd(ckv_ref[0], gkv_ref[...], dckv_n, dgkv_ref).astype(dckv_ref.dtype)

    def col(first, width):
        return pl.BlockSpec((1, BLOCK, width), lambda bi, n: (bi, n, first // width))

    def whole(a):
        return pl.BlockSpec(a.shape, lambda bi, n: (0,) * len(a.shape))

    tab = pl.BlockSpec((BLOCK, BLOCK), lambda bi, n: (n, 0))
    acc_shapes = [jax.ShapeDtypeStruct(a.shape, F32) for a in (wq, wk, wv, gq, gkv)]
    return pl.pallas_call(
        body, grid=(b, nb),
        in_specs=[col(EV_CQ, MLA_Q_LORA), col(EV_CKV, MLA_KV_LORA), whole(gq), whole(gkv), whole(wq), whole(wk), whole(wv), tab, tab, tab,
                  col(0, hw), col(0, hw), col(0, MLA_HEADS * HEAD)],
        out_specs=[col(0, MLA_Q_LORA), col(0, MLA_KV_LORA), col(0, BLOCK)] + [whole(a) for a in acc_shapes],
        out_shape=[jax.ShapeDtypeStruct((b, lp, MLA_Q_LORA), BF16), jax.ShapeDtypeStruct((b, lp, MLA_KV_LORA), BF16),
                   jax.ShapeDtypeStruct((b, lp, BLOCK), BF16)] + acc_shapes,
        scratch_shapes=[pltpu.VMEM((BLOCK, hw), BF16)],
        compiler_params=_params(("arbitrary", "arbitrary")), name=name,
    )(proj3, proj3, gq, gkv, wq, wk, wv, *tabs, dqf, dkf, dv)


def _mla_fwd(qf, kf, v, name):
    b, lp, _ = qf.shape
    nb = lp // BLOCK
    npair = MLA_HEADS // 2
    scale = (MLA_NOPE + MLA_ROPE) ** -0.5
    starts = _chunk_starts(lp)

    def body(q_ref, k_ref, v_ref, o_ref, lse_ref, vt_ref):
        for c, s0 in enumerate(starts):
            vt_ref[c] = _split_rows_t(v_ref[0, s0:s0 + KEYS, :].astype(F32))

        def qtile(r0, bq, n):
            qs = [q_ref[0, pl.ds(r0, bq), h * BLOCK:(h + 1) * BLOCK] for h in range(2)]
            t_idx = r0 + _iota2((KEYS, bq), 1)

            def kchunk(c, carry, masked):
                stats, acc = carry[:4], carry[4]
                s0, valid = _key_chunk(c, lp, t_idx, False, 0)
                valid = valid if masked else None
                ps, new, alphas = [], [], []
                for h in range(2):
                    m, l = stats[2 * h], stats[2 * h + 1]
                    s = _where(valid, _dot_nt(k_ref[0, pl.ds(s0, KEYS), h * BLOCK:(h + 1) * BLOCK], qs[h]) * scale, NEG)
                    m_new = jnp.maximum(m, jnp.max(s, axis=0, keepdims=True))
                    p = _where(valid, jnp.exp(s - m_new), 0.0)
                    alpha = jnp.exp(m - m_new)
                    new += [m_new, alpha * l + jnp.sum(p, axis=0, keepdims=True)]
                    alphas.append(alpha)
                    ps.append(p.astype(BF16))
                pv = jnp.dot(vt_ref[c], jnp.concatenate(ps, axis=0), preferred_element_type=F32)
                return (*new, _rows_of_pair(alphas[0], alphas[1]) * acc + pv)

            neg, zero = jnp.full((1, bq), NEG, F32), jnp.zeros((1, bq), F32)
            m_a, l_a, m_b, l_b, acc = _walk_chunks(r0, n, kchunk, (neg, zero, neg, zero, jnp.zeros((BLOCK, bq), F32)))
            safe = [jnp.where(l > 0.0, l, 1.0) for l in (l_a, l_b)]
            o_ref[0, pl.ds(r0, bq), :] = (acc / _rows_of_pair(safe[0], safe[1])).T
            lse = [jnp.where(l > 0.0, m + jnp.log(sf), 0.0) for m, l, sf in ((m_a, l_a, safe[0]), (m_b, l_b, safe[1]))]
            _put_rows(lse_ref, r0, bq, lse[0], lse[1])

        _for_query_tiles(nb, qtile)

    wide = pl.BlockSpec((1, lp, 2 * BLOCK), lambda bi, hp: (bi, 0, hp))
    thin = pl.BlockSpec((1, lp, 2 * HEAD), lambda bi, hp: (bi, 0, hp))
    return pl.pallas_call(
        body, grid=(b, npair), in_specs=[wide, wide, thin], out_specs=[thin, _pair_stat_spec(nb)],
        out_shape=[jax.ShapeDtypeStruct((b, lp, MLA_HEADS * HEAD), F32), jax.ShapeDtypeStruct((b, npair, nb, 8, BLOCK), F32)],
        scratch_shapes=[pltpu.VMEM((len(starts), BLOCK, 2 * KEYS), BF16)],
        compiler_params=_params(("parallel", "parallel")), name=name,
    )(qf, kf, v)


def _mla_bwd(qf, kf, v, o, lse, do, name):
    b, lp, _ = qf.shape
    nb = lp // BLOCK
    npair = MLA_HEADS // 2
    scale = (MLA_NOPE + MLA_ROPE) ** -0.5

    starts = _chunk_starts(lp)

    def body(q_ref, k_ref, v_ref, o_ref, lse_ref, do_ref, dq_ref, dk_ref, dv_ref, kt_ref):
        lo = _lo_lanes()
        dk_ref[...] = jnp.zeros_like(dk_ref)
        dv_ref[...] = jnp.zeros_like(dv_ref)
        for c, s0 in enumerate(starts):
            for h in range(2):
                kt_ref[c, h] = k_ref[0, s0:s0 + KEYS, h * BLOCK:(h + 1) * BLOCK].astype(F32).T.astype(BF16)

        def qtile(r0, bq, n):
            rows = pl.ds(r0, bq)
            qs = [q_ref[0, rows, h * BLOCK:(h + 1) * BLOCK] for h in range(2)]
            do_i = do_ref[0, rows, :]
            dos = _halves(do_i.astype(BF16), lo)
            do_st = jnp.concatenate(dos, axis=0)
            both = (do_i * o_ref[0, rows, :]).T
            dsum = (jnp.sum(both[:HEAD], axis=0, keepdims=True), jnp.sum(both[HEAD:], axis=0, keepdims=True))
            lses = _get_rows(lse_ref, r0, bq)
            t_idx = r0 + _iota2((KEYS, bq), 1)

            def kchunk(c, dqts, masked):
                s0, valid = _key_chunk(c, lp, t_idx, False, 0)
                valid = valid if masked else None
                keys = pl.ds(s0, KEYS)
                v_c = v_ref[0, keys, :]
                ps, out = [], []
                for h in range(2):
                    lanes = slice(h * BLOCK, (h + 1) * BLOCK)
                    s = _dot_nt(k_ref[0, keys, lanes], qs[h]) * scale
                    p = _where(valid, jnp.exp(s - lses[h]), 0.0)
                    ds = (p * (_dot_nt(v_c, dos[h]) - dsum[h]) * scale).astype(BF16)
                    dk_ref[0, keys, lanes] += jnp.dot(ds, qs[h], preferred_element_type=F32)
                    out.append(dqts[h] + jnp.dot(kt_ref[c, h], ds, preferred_element_type=F32))
                    ps.append(p.astype(BF16))
                dv_ref[0, keys, :] += jnp.dot(jnp.concatenate(ps, axis=1), do_st, preferred_element_type=F32)
                return tuple(out)

            zero = jnp.zeros((BLOCK, bq), F32)
            dq_a, dq_b = _walk_chunks(r0, n, kchunk, (zero, zero))
            dq_ref[0, rows, 0:BLOCK] = dq_a.T
            dq_ref[0, rows, BLOCK:2 * BLOCK] = dq_b.T

        _for_query_tiles(nb, qtile)

    wide = pl.BlockSpec((1, lp, 2 * BLOCK), lambda bi, hp: (bi, 0, hp))
    thin = pl.BlockSpec((1, lp, 2 * HEAD), lambda bi, hp: (bi, 0, hp))
    return pl.pallas_call(
        body, grid=(b, npair), in_specs=[wide, wide, thin, thin, _pair_stat_spec(nb), thin], out_specs=[wide, wide, thin],
        out_shape=[jax.ShapeDtypeStruct(qf.shape, F32), jax.ShapeDtypeStruct(qf.shape, F32), jax.ShapeDtypeStruct(v.shape, F32)],
        scratch_shapes=[pltpu.VMEM((len(starts), 2, BLOCK, KEYS), BF16)],
        compiler_params=_params(("parallel", "parallel")), name=name,
    )(qf, kf, v, o, lse, do)


def _swa_keys(k_ref, v_ref, n, kv):
    prev = jnp.maximum(n - 1, 0)
    rows = lambda blk: pl.ds(pl.multiple_of(blk * BLOCK, BLOCK), BLOCK)
    mine = (_iota2((1, BLOCK), 1) >= HEAD).astype(jnp.int32) == kv

    def both_halves(ref):
        x = jnp.concatenate([ref[0, rows(prev), :], ref[0, rows(n), :], ref[0, 0:BLOCK, :]], axis=0)
        return jnp.where(mine, x, pltpu.roll(x, HEAD, 1))

    slot = _iota2((3 * BLOCK, BLOCK), 0)
    loc = slot % BLOCK
    s_idx = jnp.where(slot < BLOCK, (n - 1) * BLOCK + loc, jnp.where(slot < 2 * BLOCK, n * BLOCK + loc, loc))
    dist = n * BLOCK + _iota2((3 * BLOCK, BLOCK), 1) - s_idx
    band = (slot < 2 * BLOCK) & (dist >= 0) & (dist < SWA_WINDOW) & (s_idx >= BLOCK)
    meta = (slot >= 2 * BLOCK) & (s_idx >= N_PAD) & (dist >= 0)
    return both_halves(k_ref), both_halves(v_ref), band | meta, dist.astype(F32), prev


def _swa_probs(q_h, kdup, valid, dist, head, sink_ref):
    slope = jnp.exp(jnp.full((1, 1), -8.0 * math.log(2.0) / SWA_HEADS, F32) * (head + 1).astype(F32))
    s = jnp.where(valid, _dot_nt(kdup, q_h) * (HEAD ** -0.5) - slope * dist, NEG)
    sink = sink_ref[pl.ds(head, 1), 0:1]
    m = jnp.maximum(jnp.max(s, axis=0, keepdims=True), sink)
    e = jnp.where(valid, jnp.exp(s - m), 0.0)
    es = jnp.exp(sink - m)
    inv = 1.0 / (jnp.sum(e, axis=0, keepdims=True) + es)
    return e * inv, es * inv


SWA_PAIRS = SWA_HEADS // SWA_KV_HEADS // 2
SWA_GROUP = SWA_PAIRS * 2 * HEAD


def _swa_specs(b, lp):
    nb = lp // BLOCK
    qcol = lambda first: pl.BlockSpec((1, BLOCK, SWA_GROUP), lambda bi, kv, n: (bi, n, first // SWA_GROUP + kv))
    kcol = lambda first: pl.BlockSpec((1, lp, BLOCK), lambda bi, kv, n: (bi, 0, first // BLOCK))
    sink = pl.BlockSpec((SWA_HEADS, BLOCK), lambda bi, kv, n: (0, 0))
    return (b, SWA_KV_HEADS, nb), qcol, kcol, sink


def _swa_fwd(proj3, sinks, name):
    b, lp, _ = proj3.shape
    grid, qcol, kcol, sink = _swa_specs(b, lp)

    def body(q_ref, k_ref, v_ref, sink_ref, o_ref):
        kv, n = pl.program_id(1), pl.program_id(2)
        lo = _lo_lanes()
        kdup, vdup, valid, dist, _ = _swa_keys(k_ref, v_ref, n, kv)
        kdup = kdup.astype(BF16)
        vt = _split_rows_t(vdup)
        for p in range(SWA_PAIRS):
            lanes = slice(p * BLOCK, (p + 1) * BLOCK)
            qs = _halves(q_ref[0, :, lanes].astype(BF16), lo)
            probs = [_swa_probs(qs[hh], kdup, valid, dist, (kv * SWA_PAIRS + p) * 2 + hh, sink_ref)[0].astype(BF16) for hh in range(2)]
            o_ref[0, :, lanes] = jnp.dot(vt, jnp.concatenate(probs, axis=0), preferred_element_type=F32).T

    return pl.pallas_call(
        body, grid=grid, in_specs=[qcol(OD_Q), kcol(OD_K), kcol(OD_V), sink], out_specs=qcol(0),
        out_shape=jax.ShapeDtypeStruct((b, lp, SWA_HEADS * HEAD), F32),
        compiler_params=_params(("parallel", "parallel", "parallel")), name=name,
    )(proj3, proj3, proj3, sinks)


def _swa_bwd(proj3, sinks, do, name):
    b, lp, _ = proj3.shape
    nb = lp // BLOCK
    grid, qcol, kcol, sink = _swa_specs(b, lp)

    def body(q_ref, k_ref, v_ref, sink_ref, do_ref, dq_ref, dk_ref, dv_ref, dsink_ref, dk_acc, dv_acc):
        kv, n = pl.program_id(1), pl.program_id(2)

        @pl.when((n == 0) & (pl.program_id(0) == 0) & (kv == 0))
        def _():
            dsink_ref[...] = jnp.zeros_like(dsink_ref)

        @pl.when(n == 0)
        def _():
            dk_acc[...] = jnp.zeros_like(dk_acc)
            dv_acc[...] = jnp.zeros_like(dv_acc)

        lo = _lo_lanes()
        kdup, vdup, valid, dist, prev = _swa_keys(k_ref, v_ref, n, kv)
        kt = _split_rows_t(kdup)
        kdup, vdup = kdup.astype(BF16), vdup.astype(BF16)
        dkc = jnp.zeros((3 * BLOCK, BLOCK), F32)
        dvc = jnp.zeros((3 * BLOCK, BLOCK), F32)
        for p in range(SWA_PAIRS):
            lanes = slice(p * BLOCK, (p + 1) * BLOCK)
            qs = _halves(q_ref[0, :, lanes].astype(BF16), lo)
            dos = _halves(do_ref[0, :, lanes].astype(BF16), lo)
            dss, prs = [], []
            for hh in range(2):
                head = (kv * SWA_PAIRS + p) * 2 + hh
                pr, ps = _swa_probs(qs[hh], kdup, valid, dist, head, sink_ref)
                dp = _dot_nt(vdup, dos[hh])
                dsum = jnp.sum(pr * dp, axis=0, keepdims=True)
                dsink_ref[pl.ds(head, 1), :] += jnp.broadcast_to(-jnp.sum(ps * dsum, axis=1, keepdims=True), (1, BLOCK))
                dss.append((pr * (dp - dsum) * (HEAD ** -0.5)).astype(BF16))
                prs.append(pr.astype(BF16))
            dq_ref[0, :, lanes] = jnp.dot(kt, jnp.concatenate(dss, axis=0), preferred_element_type=F32).T
            dkc = dkc + jnp.dot(jnp.concatenate(dss, axis=1), jnp.concatenate(qs, axis=0), preferred_element_type=F32)
            dvc = dvc + jnp.dot(jnp.concatenate(prs, axis=1), jnp.concatenate(dos, axis=0), preferred_element_type=F32)
        rows = lambda blk: pl.ds(pl.multiple_of(blk * BLOCK, BLOCK), BLOCK)
        for part, r in enumerate((rows(prev), rows(n), slice(0, BLOCK))):
            dk_acc[r, :] += dkc[part * BLOCK:(part + 1) * BLOCK]
            dv_acc[r, :] += dvc[part * BLOCK:(part + 1) * BLOCK]

        for acc, ref in ((dk_acc, dk_ref), (dv_acc, dv_ref)):
            @pl.when((n == nb - 1) & (kv == 0))
            def _():
                x = acc[...]
                ref[0] = x + pltpu.roll(x, HEAD, 1)

            @pl.when((n == nb - 1) & (kv == 1))
            def _():
                x = acc[...]
                ref[0] = jnp.where(lo, ref[0], x + pltpu.roll(x, HEAD, 1))

    kvout = pl.BlockSpec((1, lp, BLOCK), lambda bi, kv, n: (bi, 0, 0))
    kvshape = jax.ShapeDtypeStruct((b, lp, BLOCK), F32)
    return pl.pallas_call(
        body, grid=grid, in_specs=[qcol(OD_Q), kcol(OD_K), kcol(OD_V), sink, qcol(0)], out_specs=[qcol(0), kvout, kvout, sink],
        out_shape=[jax.ShapeDtypeStruct((b, lp, SWA_HEADS * HEAD), F32), kvshape, kvshape, jax.ShapeDtypeStruct((SWA_HEADS, BLOCK), F32)],
        scratch_shapes=[pltpu.VMEM((lp, BLOCK), F32), pltpu.VMEM((lp, BLOCK), F32)],
        compiler_params=_params(("arbitrary", "arbitrary", "arbitrary")), name=name,
    )(proj3, proj3, proj3, sinks, do)


def _kernel_weights(ev_w_in, ev_w_uq, ev_w_ukv, od_w_in):
    zeros = lambda r, c: jnp.zeros((r, c), ev_w_in.dtype)
    q_sb, k_sb, v_sb, g_sb, c_q, c_kv, k_r, g_mla = jnp.split(ev_w_in, [512, 1024, 1536, 2048, 2304, 2432, 2464], axis=1)
    w0 = jnp.concatenate([g_sb, g_mla, q_sb, k_sb, v_sb, c_q, c_kv, zeros(D_MODEL, MLA_NOPE), k_r, zeros(D_MODEL, 32)], axis=1)
    uq = ev_w_uq.reshape(MLA_Q_LORA, MLA_HEADS, MLA_NOPE + MLA_ROPE)
    wq = jnp.pad(uq, ((0, 0), (0, 0), (0, BLOCK - MLA_NOPE - MLA_ROPE))).reshape(MLA_Q_LORA, MLA_HEADS * BLOCK)
    ukv = ev_w_ukv.reshape(MLA_KV_LORA, MLA_HEADS, BLOCK)
    wk = jnp.pad(ukv[:, :, :MLA_NOPE], ((0, 0), (0, 0), (0, BLOCK - MLA_NOPE))).reshape(MLA_KV_LORA, MLA_HEADS * BLOCK)
    wv = ukv[:, :, MLA_NOPE:].reshape(MLA_KV_LORA, MLA_HEADS * HEAD)
    q, k, v, g = jnp.split(od_w_in, [1024, 1152, 1280], axis=1)
    w1 = jnp.concatenate([g, q, k, v], axis=1)
    return w0, wq, wk, wv, w1


def _original_grads(dw0, dwq, dwk, dwv, dw1):
    sl = lambda a, first, n: a[:, first:first + n]
    d_ev_w_in = jnp.concatenate([sl(dw0, EV_Q, 512), sl(dw0, EV_K, 512), sl(dw0, EV_V, 512), sl(dw0, EV_G, 512), sl(dw0, EV_CQ, 256),
                                 sl(dw0, EV_CKV, 128), sl(dw0, EV_KR + MLA_NOPE, MLA_ROPE), sl(dw0, EV_G + 512, 512)], axis=1)
    d_uq = dwq.reshape(MLA_Q_LORA, MLA_HEADS, BLOCK)[:, :, :MLA_NOPE + MLA_ROPE].reshape(MLA_Q_LORA, -1)
    d_ukv = jnp.concatenate([dwk.reshape(MLA_KV_LORA, MLA_HEADS, BLOCK)[:, :, :MLA_NOPE], dwv.reshape(MLA_KV_LORA, MLA_HEADS, HEAD)],
                            axis=2).reshape(MLA_KV_LORA, -1)
    d_od_w_in = jnp.concatenate([sl(dw1, OD_Q, 1024), sl(dw1, OD_K, 128), sl(dw1, OD_V, 128), sl(dw1, OD_G, 1024)], axis=1)
    return d_ev_w_in, d_uq, d_ukv, d_od_w_in


def _meta_rows_sum(dh0_3):
    b, _, d = dh0_3.shape

    def body(x_ref, o_ref):
        acc = x_ref[0, N_PAD:BLOCK, :]
        for i in range(1, b):
            acc = acc + x_ref[i, N_PAD:BLOCK, :]
        o_ref[...] = acc

    return pl.pallas_call(
        body, grid=(1,), in_specs=[pl.BlockSpec((b, BLOCK, d), lambda i: (0, 0, 0))], out_specs=pl.BlockSpec((N_META, d), lambda i: (0, 0)),
        out_shape=jax.ShapeDtypeStruct((N_META, d), F32), compiler_params=_params(("arbitrary",)), name="meta_rows_sum",
    )(dh0_3)


def _local_step(x, meta, norm_g, final_g, gq, gkv, sinks, target, ev_w_in, ev_w_uq, ev_w_ukv, wo0, od_w_in, wo1):
    b, seq, d = x.shape
    lp = seq + BLOCK
    t = b * lp
    w0, wq, wk, wv, w1 = _kernel_weights(ev_w_in, ev_w_uq, ev_w_ukv, od_w_in)
    h0 = jnp.concatenate([jnp.zeros((b, N_PAD, d), F32), jnp.broadcast_to(meta[None], (b, N_META, d)), x], axis=1).reshape(t, d)
    tabs = _rope_tables(lp)
    g0, g1 = norm_g[0:1], norm_g[1:2]

    hn0 = _rms_fwd(h0, g0, "norm0")
    proj0 = _mm(hn0, w0, "inproj0")
    p0 = proj0.reshape(b, lp, EV_N)
    o_sb, sb_tot = _sb_fwd(p0, "sb_fwd")
    qf, kf, v = _mla_prep_fwd(p0, gq, gkv, wq, wk, wv, tabs, "mla_prep_fwd")
    o_mla, lse = _mla_fwd(qf, kf, v, "mla_fwd")
    o0 = [o_sb.reshape(t, -1), o_mla.reshape(t, -1)]
    ao0 = _gate_fwd(o0, proj0, "gate0")
    h1 = _mm(ao0, wo0, "outproj0", res=h0)

    hn1 = _rms_fwd(h1, g1, "norm1")
    proj1 = _mm(hn1, w1, "inproj1")
    p1 = proj1.reshape(b, lp, OD_N)
    sinks_b = jnp.broadcast_to(sinks.reshape(SWA_HEADS, 1), (SWA_HEADS, BLOCK))
    o1 = _swa_fwd(p1, sinks_b, "swa_fwd").reshape(t, -1)
    ao1 = _gate_fwd([o1], proj1, "gate1")
    h2 = _mm(ao1, wo1, "outproj1", res=h1)

    dh2, d_final_g, loss = _loss_head(h2, final_g.reshape(1, d), target, b, lp)

    d_wo1 = _mm_tn(ao1, dh2, "d_wo1")
    dao1 = _mm_nt(dh2, wo1, "d_ao1")
    (do1,), dg1 = _gate_bwd(dao1, [o1], proj1, "gate1_bwd")
    dq1, dk4, dv4, d_sinks = _swa_bwd(p1, sinks_b, do1.reshape(b, lp, -1), "swa_bwd")
    unheads = lambda a: a.reshape(t, SWA_KV_HEADS * HEAD).astype(BF16)
    dproj1 = jnp.concatenate([dg1, dq1.reshape(t, -1).astype(BF16), unheads(dk4), unheads(dv4)], axis=1)
    d_w1 = _mm_tn(hn1, dproj1, "d_w1")
    dhn1 = _mm_nt(dproj1, w1, "d_hn1")
    dh1, d_g1 = _rms_bwd(h1, g1, dhn1, dh2, "norm1_bwd")

    d_wo0 = _mm_tn(ao0, dh1, "d_wo0")
    dao0 = _mm_nt(dh1, wo0, "d_ao0")
    (do_sb, do_mla), dg0 = _gate_bwd(dao0, o0, proj0, "gate0_bwd")
    dq_sb, dk_sb, dv_sb = _sb_bwd(p0, sb_tot, do_sb.reshape(b, lp, -1), "sb_bwd")
    dqf, dkf, dv = _mla_bwd(qf, kf, v, o_mla, lse, do_mla.reshape(b, lp, -1), "mla_bwd")
    dcq, dckv, dkr, d_wq, d_wk, d_wv, d_gq, d_gkv = _mla_prep_bwd(p0, gq, gkv, wq, wk, wv, tabs, dqf, dkf, dv, "mla_prep_bwd")
    flat = lambda a: a.reshape(t, -1).astype(BF16)
    dproj0 = jnp.concatenate([dg0, flat(dq_sb), flat(dk_sb), flat(dv_sb), flat(dcq), flat(dckv), flat(dkr)], axis=1)
    d_w0 = _mm_tn(hn0, dproj0, "d_w0")
    dhn0 = _mm_nt(dproj0, w0, "d_hn0")
    dh0, d_g0 = _rms_bwd(h0, g0, dhn0, dh1, "norm0_bwd")
    dh0 = dh0.reshape(b, lp, d)

    d_ev_w_in, d_uq, d_ukv, d_od_w_in = _original_grads(d_w0, d_wq, d_wk, d_wv, d_w1)
    grads = dict(meta=_meta_rows_sum(dh0), norm_g=jnp.concatenate([d_g0, d_g1], axis=0), final_g=d_final_g.reshape(d),
                 ev_w_in=d_ev_w_in, ev_q_norm_g=d_gq, ev_kv_norm_g=d_gkv, ev_w_uq=d_uq, ev_w_ukv=d_ukv, ev_w_out=d_wo0,
                 od_w_in=d_od_w_in, od_sinks=d_sinks[:, 0].reshape(1, SWA_HEADS), od_w_out=d_wo1)
    return loss, dh0[:, BLOCK:], grads


MESH = pl.DeviceIdType.MESH
ANY = pl.BlockSpec(memory_space=pl.ANY)


def _place():
    return lax.axis_index("x"), lax.axis_index("y"), lax.axis_index("c")


def _other_chips(x, y):
    return [(1 - x, y), (x, 1 - y), (1 - x, 1 - y)]


def _with_own_slot(slots, own):
    me = 2 * lax.axis_index("x") + lax.axis_index("y")
    return lax.dynamic_update_slice(slots, own[None], (me,) + (0,) * own.ndim)


def _gather_weights(packs, meta, name):
    n = len(packs)

    def body(*refs):
        ins, m_ref, outs, mo_ref = refs[:n], refs[n], refs[n + 1:2 * n + 1], refs[2 * n + 1]
        send_sems, recv_sems = refs[2 * n + 2:]
        x, y, c = _place()
        me, sib = 2 * x + y, (x, y, 1 - c)
        chips = _other_chips(x, y)

        def copy(k, src, dst, to):
            return pltpu.make_async_remote_copy(src_ref=src, dst_ref=dst, send_sem=send_sems.at[k], recv_sem=recv_sems.at[k], device_id=to,
                                                device_id_type=MESH)

        def half(i, chip, h):
            rows = packs[i].shape[0] // 2
            return outs[i].at[chip, pl.ds(h * rows, rows), :]

        def mine(i):
            rows = packs[i].shape[0] // 2
            return ins[i].at[pl.ds(c * rows, rows), :]

        sent = [copy(6 * i + k, mine(i), half(i, me, c), (px, py, c)) for i in range(n) for k, (px, py) in enumerate(chips)]
        sent += [copy(6 * n + k, m_ref, mo_ref.at[me], (px, py, c)) for k, (px, py) in enumerate(chips)]
        for cp in sent:
            cp.start()
        for i in range(n):
            for k, (px, py) in enumerate(chips):
                landed = half(i, 2 * px + py, c)
                copy(6 * i + k, mine(i), landed, (px, py, c)).wait_recv()
                fwd = copy(6 * i + 3 + k, landed, landed, sib)
                fwd.start()
                sent.append(fwd)
        for k, (px, py) in enumerate(chips):
            for i in range(n):
                other = half(i, 2 * px + py, 1 - c)
                copy(6 * i + 3 + k, other, other, sib).wait_recv()
            copy(6 * n + k, m_ref, mo_ref.at[2 * px + py], (px, py, c)).wait_recv()
        for cp in sent:
            cp.wait_send()

    nsem = 6 * n + 3
    res = pl.pallas_call(
        body, in_specs=[ANY] * (n + 1), out_specs=[ANY] * (n + 1),
        out_shape=[jax.ShapeDtypeStruct((N_CHIPS,) + a.shape, a.dtype) for a in list(packs) + [meta]],
        scratch_shapes=[pltpu.SemaphoreType.DMA((nsem,)), pltpu.SemaphoreType.DMA((nsem,))],
        name=name,
    )(*packs, meta)
    return [_with_own_slot(r, a) for r, a in zip(res[:n], packs)], _with_own_slot(res[n], meta)


def _grads_to_sibling(gs, name):
    n = len(gs)

    def body(*refs):
        ins, outs, send_sems, recv_sems = refs[:n], refs[n:2 * n], refs[2 * n], refs[2 * n + 1]
        x, y, c = _place()
        cps = []
        for i in range(n):
            rows = gs[i].shape[1] // 2
            cps.append(pltpu.make_async_remote_copy(src_ref=ins[i].at[:, pl.ds((1 - c) * rows, rows), :], dst_ref=outs[i],
                                                    send_sem=send_sems.at[i], recv_sem=recv_sems.at[i], device_id=(x, y, 1 - c),
                                                    device_id_type=MESH))
        for cp in cps:
            cp.start()
        for cp in cps:
            cp.wait()

    return pl.pallas_call(
        body, in_specs=[ANY] * n, out_specs=[ANY] * n,
        out_shape=[jax.ShapeDtypeStruct((g.shape[0], g.shape[1] // 2, g.shape[2]), g.dtype) for g in gs],
        scratch_shapes=[pltpu.SemaphoreType.DMA((n,)), pltpu.SemaphoreType.DMA((n,))],
        name=name,
    )(*gs)


def _share_halves(rs, name):
    n = len(rs)

    def body(*refs):
        ins, outs, send_sems, recv_sems = refs[:n], refs[n:2 * n], refs[2 * n], refs[2 * n + 1]
        x, y, c = _place()
        cps = [pltpu.make_async_remote_copy(src_ref=ins[i], dst_ref=outs[i], send_sem=send_sems.at[i], recv_sem=recv_sems.at[i],
                                            device_id=(x, y, 1 - c), device_id_type=MESH) for i in range(n)]
        for cp in cps:
            cp.start()
        for cp in cps:
            cp.wait()

    theirs = pl.pallas_call(
        body, in_specs=[ANY] * n, out_specs=[ANY] * n, out_shape=[jax.ShapeDtypeStruct(r.shape, r.dtype) for r in rs],
        scratch_shapes=[pltpu.SemaphoreType.DMA((n,)), pltpu.SemaphoreType.DMA((n,))],
        name=name,
    )(*rs)
    first = lax.axis_index("c") == 0
    return [jnp.where(first, jnp.concatenate([r, t], axis=0), jnp.concatenate([t, r], axis=0)) for r, t in zip(rs, theirs)]


def _chip_scatter(ss, name):
    n = len(ss)

    def body(*refs):
        ins, outs, send_sems, recv_sems = refs[:n], refs[n:2 * n], refs[2 * n], refs[2 * n + 1]
        x, y, c = _place()
        me = 2 * x + y
        chips = _other_chips(x, y)
        for i in range(n):
            for k, (px, py) in enumerate(chips):
                pltpu.make_async_remote_copy(src_ref=ins[i].at[2 * px + py], dst_ref=outs[i].at[me], send_sem=send_sems.at[3 * i + k],
                                             recv_sem=recv_sems.at[3 * i + k], device_id=(px, py, c), device_id_type=MESH).start()
        for i in range(n):
            for k, (px, py) in enumerate(chips):
                cp = pltpu.make_async_remote_copy(src_ref=ins[i].at[2 * px + py], dst_ref=outs[i].at[2 * px + py],
                                                  send_sem=send_sems.at[3 * i + k], recv_sem=recv_sems.at[3 * i + k],
                                                  device_id=(px, py, c), device_id_type=MESH)
                cp.wait_recv()
                cp.wait_send()

    parts = pl.pallas_call(
        body, in_specs=[ANY] * n, out_specs=[ANY] * n, out_shape=[jax.ShapeDtypeStruct(s.shape, s.dtype) for s in ss],
        scratch_shapes=[pltpu.SemaphoreType.DMA((3 * n,)), pltpu.SemaphoreType.DMA((3 * n,))],
        name=name,
    )(*ss)
    me = 2 * lax.axis_index("x") + lax.axis_index("y")
    return [_with_own_slot(p, lax.dynamic_index_in_dim(s, me, axis=0, keepdims=False)) for p, s in zip(parts, ss)]


HBM_SPACE = pltpu.MemorySpace.HBM


def _on_sequencer(name, collective_id, n_sems, body):
    @pl.kernel(mesh=plsc.ScalarSubcoreMesh(axis_name="sequencer", num_cores=1), name=name,
               scratch_types=(pltpu.SemaphoreType.DMA((n_sems,)), pltpu.SemaphoreType.DMA((n_sems,))),
               compiler_params=pltpu.CompilerParams(collective_id=collective_id))
    def launch(send_sems, recv_sems):
        body(send_sems, recv_sems)

    launch()


def _handshake(peers):
    barrier = pltpu.get_barrier_semaphore()
    for peer in peers:
        pl.semaphore_signal(barrier, inc=1, device_id=peer, device_id_type=MESH)
    pl.semaphore_wait(barrier, len(peers))


def _gather_on_sequencer(packs, name):
    n = len(packs)
    ins = [jax.new_ref(p, memory_space=HBM_SPACE) for p in packs]
    outs = [jax.empty_ref(jax.ShapeDtypeStruct((N_CHIPS,) + p.shape, p.dtype), memory_space=HBM_SPACE) for p in packs]

    def body(send_sems, recv_sems):
        x, y, c = _place()
        me, sib = 2 * x + y, (x, y, 1 - c)
        chips = _other_chips(x, y)
        _handshake([(px, py, c) for px, py in chips] + [sib])

        def copy(k, src, dst, to):
            return pltpu.make_async_remote_copy(src_ref=src, dst_ref=dst, send_sem=send_sems.at[k], recv_sem=recv_sems.at[k], device_id=to,
                                                device_id_type=MESH)

        def half(i, chip, h):
            rows = packs[i].shape[0] // 2
            return outs[i].at[chip, pl.ds(h * rows, rows), :]

        def mine(i):
            rows = packs[i].shape[0] // 2
            return ins[i].at[pl.ds(c * rows, rows), :]

        sent = [copy(6 * i + k, mine(i), half(i, me, c), (px, py, c)) for i in range(n) for k, (px, py) in enumerate(chips)]
        for cp in sent:
            cp.start()
        for i in range(n):
            for k, (px, py) in enumerate(chips):
                landed = half(i, 2 * px + py, c)
                copy(6 * i + k, mine(i), landed, (px, py, c)).wait_recv()
                fwd = copy(6 * i + 3 + k, landed, landed, sib)
                fwd.start()
                sent.append(fwd)
        for i in range(n):
            for k, (px, py) in enumerate(chips):
                other = half(i, 2 * px + py, 1 - c)
                copy(6 * i + 3 + k, other, other, sib).wait_recv()
        for cp in sent:
            cp.wait_send()

    _on_sequencer(name, 1, 6 * n, body)
    return [_with_own_slot(o[...], p) for o, p in zip(outs, packs)]


def _chip_scatter_on_sequencer(ss, name):
    n = len(ss)
    ins = [jax.new_ref(s, memory_space=HBM_SPACE) for s in ss]
    outs = [jax.empty_ref(jax.ShapeDtypeStruct(s.shape, s.dtype), memory_space=HBM_SPACE) for s in ss]

    def body(send_sems, recv_sems):
        x, y, c = _place()
        me = 2 * x + y
        chips = _other_chips(x, y)
        _handshake([(px, py, c) for px, py in chips])
        for i in range(n):
            for k, (px, py) in enumerate(chips):
                pltpu.make_async_remote_copy(src_ref=ins[i].at[2 * px + py], dst_ref=outs[i].at[me], send_sem=send_sems.at[3 * i + k],
                                             recv_sem=recv_sems.at[3 * i + k], device_id=(px, py, c), device_id_type=MESH).start()
        for i in range(n):
            for k, (px, py) in enumerate(chips):
                cp = pltpu.make_async_remote_copy(src_ref=ins[i].at[2 * px + py], dst_ref=outs[i].at[2 * px + py],
                                                  send_sem=send_sems.at[3 * i + k], recv_sem=recv_sems.at[3 * i + k],
                                                  device_id=(px, py, c), device_id_type=MESH)
                cp.wait_recv()
                cp.wait_send()

    _on_sequencer(name, 2, 3 * n, body)
    me = 2 * lax.axis_index("x") + lax.axis_index("y")
    return [_with_own_slot(o[...], lax.dynamic_index_in_dim(s, me, axis=0, keepdims=False)) for o, s in zip(outs, ss)]


def _all_reduce_small(v, name):
    shape = v.shape

    def body(v_ref, o_ref, slots, send_sems, recv_sems):
        x, y, c = _place()
        me = 4 * x + 2 * y + c
        slots[me] = v_ref[...]
        for r in range(1, N_DEV):
            peer = (x ^ (r >> 2), y ^ ((r >> 1) & 1), c ^ (r & 1))
            pltpu.make_async_remote_copy(src_ref=v_ref, dst_ref=slots.at[me], send_sem=send_sems.at[r - 1], recv_sem=recv_sems.at[r - 1],
                                         device_id=peer, device_id_type=MESH).start()
        for r in range(1, N_DEV):
            peer = (x ^ (r >> 2), y ^ ((r >> 1) & 1), c ^ (r & 1))
            cp = pltpu.make_async_remote_copy(src_ref=v_ref, dst_ref=slots.at[4 * peer[0] + 2 * peer[1] + peer[2]], send_sem=send_sems.at[r - 1],
                                              recv_sem=recv_sems.at[r - 1], device_id=peer, device_id_type=MESH)
            cp.wait_recv()
            cp.wait_send()
        acc = slots[0]
        for d in range(1, N_DEV):
            acc = acc + slots[d]
        o_ref[...] = acc

    vm = pl.BlockSpec(memory_space=pltpu.VMEM)
    return pl.pallas_call(
        body, in_specs=[vm], out_specs=vm, out_shape=jax.ShapeDtypeStruct(shape, F32),
        scratch_shapes=[pltpu.VMEM((N_DEV,) + shape, F32), pltpu.SemaphoreType.DMA((N_DEV - 1,)), pltpu.SemaphoreType.DMA((N_DEV - 1,))],
        name=name,
    )(v)


def _add_sibling(g, gsib, core, name):
    n, _, cdim = g.shape
    half = gsib.shape[1]
    tr = half // 2

    def body(core_ref, a_ref, b_ref, o_ref):
        o_ref[...] = (a_ref[...] + b_ref[...]).astype(o_ref.dtype)

    blk = pl.BlockSpec((1, tr, cdim), lambda j, i, core_ref: (j, i, 0))
    return pl.pallas_call(
        body,
        grid_spec=pltpu.PrefetchScalarGridSpec(
            num_scalar_prefetch=1, grid=(n, half // tr),
            in_specs=[pl.BlockSpec((1, tr, cdim), lambda j, i, core_ref: (j, core_ref[0] * (half // tr) + i, 0)), blk], out_specs=blk),
        out_shape=jax.ShapeDtypeStruct(gsib.shape, BF16), compiler_params=_params(("parallel", "parallel")), name=name,
    )(core, g, gsib)


def _sum_parts(parts, name):
    n, r, cdim = parts.shape
    tr = r // 2

    def body(p_ref, o_ref):
        acc = p_ref[0].astype(F32)
        for j in range(1, n):
            acc = acc + p_ref[j].astype(F32)
        o_ref[...] = acc

    return pl.pallas_call(
        body, grid=(r // tr,), in_specs=[pl.BlockSpec((n, tr, cdim), lambda i: (0, i, 0))],
        out_specs=pl.BlockSpec((tr, cdim), lambda i: (i, 0)), out_shape=jax.ShapeDtypeStruct((r, cdim), F32),
        compiler_params=_params(("parallel",)), name=name,
    )(parts)


def _adamw(parts, w, m, v, name):
    npart, r, cdim = parts.shape
    tr = r // 4 if r % 32 == 0 else r

    def body(p_ref, w_ref, m_ref, v_ref, g_ref, d_ref, nm_ref, nv_ref):
        g = p_ref[0]
        for j in range(1, npart):
            g = g + p_ref[j]
        m_new = ADAM_B1 * m_ref[...] + (1.0 - ADAM_B1) * g
        v_new = ADAM_B2 * v_ref[...] + (1.0 - ADAM_B2) * (g * g)
        m_hat = m_new / (1.0 - ADAM_B1 ** ADAM_STEP)
        v_hat = v_new / (1.0 - ADAM_B2 ** ADAM_STEP)
        g_ref[...] = g
        d_ref[...] = -ADAM_LR * (m_hat / (jnp.sqrt(v_hat) + ADAM_EPS) + ADAM_WD * w_ref[...])
        nm_ref[...] = m_new
        nv_ref[...] = v_new

    blk = pl.BlockSpec((tr, cdim), lambda i: (i, 0))
    shp = jax.ShapeDtypeStruct((r, cdim), F32)
    return pl.pallas_call(
        body, grid=(r // tr,), in_specs=[pl.BlockSpec((npart, tr, cdim), lambda i: (0, i, 0)), blk, blk, blk], out_specs=[blk] * 4,
        out_shape=[shp] * 4, compiler_params=_params(("parallel",)), name=name,
    )(parts, w, m, v)


BIG = ("ev_w_in", "ev_w_uq", "ev_w_ukv", "ev_w_out", "od_w_in", "od_w_out", "meta")
SMALL = ("norm_g", "final_g", "ev_q_norm_g", "ev_kv_norm_g", "od_sinks")
SMALL_SHAPE = (8, 512)
BY_ROWS = ("ev_w_out", "od_w_out")

EV_IN_SHARD, OD_IN_SHARD, UQ_SHARD = 2976 // N_CHIPS, 2304 // N_CHIPS, 768 // N_CHIPS


def _pack_big(a, lead=()):
    dtype = a["ev_w_in"].dtype
    z = lambda r, c: jnp.zeros(lead + (r, c), dtype)
    ax = len(lead)
    pad_to = lambda x, width: jnp.concatenate([x, z(x.shape[ax], width - x.shape[ax + 1])], axis=ax + 1)
    corner = jnp.concatenate([a["ev_w_ukv"], a["meta"], z(256 - MLA_KV_LORA - N_META, 256)], axis=ax)
    return (pad_to(a["ev_w_in"], 768), jnp.concatenate([pad_to(a["ev_w_uq"], 256), corner], axis=ax + 1),
            pad_to(a["od_w_in"], 640), jnp.concatenate([a["ev_w_out"], a["od_w_out"]], axis=ax + 1))


N_FIRST = 2


def _unpack_big(p_in0, p_lat, p_in1, p_out):
    return dict(ev_w_in=p_in0[..., :EV_IN_SHARD], od_w_in=p_in1[..., :OD_IN_SHARD], ev_w_out=p_out[..., :D_MODEL],
                od_w_out=p_out[..., D_MODEL:], ev_w_uq=p_lat[..., :UQ_SHARD], ev_w_ukv=p_lat[..., :MLA_KV_LORA, 256:],
                meta=p_lat[..., MLA_KV_LORA:MLA_KV_LORA + N_META, 256:])


def _chip_shards(full, by_rows):
    if by_rows:
        return full.reshape(N_CHIPS, full.shape[0] // N_CHIPS, full.shape[1])
    return full.reshape(full.shape[0], N_CHIPS, -1).transpose(1, 0, 2)


def _from_chip_shards(slots, by_rows):
    if by_rows:
        return slots.reshape(-1, slots.shape[2])
    return slots.transpose(1, 0, 2).reshape(slots.shape[1], -1)


def _pack_small(arrs, extra=None):
    flat = [a.reshape(-1) for a in arrs] + ([] if extra is None else [extra.reshape(-1)])
    used = sum(f.shape[0] for f in flat)
    return jnp.pad(jnp.concatenate(flat), (0, SMALL_SHAPE[0] * SMALL_SHAPE[1] - used)).reshape(SMALL_SHAPE)


def _unpack_small(p, shapes):
    flat, out, at = p.reshape(-1), [], 0
    for s in shapes:
        n = int(np.prod(s))
        out.append(flat[at:at + n].reshape(s))
        at += n
    return out, flat[at]


def kernel(x, meta, norm_g, final_g, ev_w_in, ev_q_norm_g, ev_kv_norm_g, ev_w_uq, ev_w_ukv, ev_w_out, od_w_in, od_sinks, od_w_out, loss_target, m_meta, m_norm_g, m_final_g, m_ev_w_in, m_ev_q_norm_g, m_ev_kv_norm_g, m_ev_w_uq, m_ev_w_ukv, m_ev_w_out, m_od_w_in, m_od_sinks, m_od_w_out, v_meta, v_norm_g, v_final_g, v_ev_w_in, v_ev_q_norm_g, v_ev_kv_norm_g, v_ev_w_uq, v_ev_w_ukv, v_ev_w_out, v_od_w_in, v_od_sinks, v_od_w_out):
    given = dict(locals())
    two_d = lambda a: a[0] if a.ndim == 3 else a
    packs = {k: _pack_big({n: two_d(given[k + n]) for n in BIG}) for k in ("", "m_", "v_")}

    wbf = [p.astype(BF16) for p in packs[""]]
    later = _gather_on_sequencer(wbf[N_FIRST:], "gather_later_weights")
    first, meta_all = _gather_weights(wbf[:N_FIRST], meta, "gather_weights")
    full = {n: _from_chip_shards(a, n in BY_ROWS) for n, a in _unpack_big(*first, *later).items()}
    meta_full = _from_chip_shards(meta_all, False)

    loss, grad_x, grads = _local_step(x, meta_full, norm_g, final_g, ev_q_norm_g, ev_kv_norm_g, od_sinks, loss_target,
                                      full["ev_w_in"], full["ev_w_uq"], full["ev_w_ukv"], full["ev_w_out"], full["od_w_in"], full["od_w_out"])

    gpacks = _pack_big({n: _chip_shards(grads[n], n in BY_ROWS) for n in BIG}, lead=(N_CHIPS,))
    core = lax.axis_index("c").astype(jnp.int32).reshape(1)

    def reduce_group(group, tag, scatter):
        from_sibling = _grads_to_sibling(group, "grads_to_sibling_" + tag)
        sums = [_add_sibling(g, s, core, f"add_sibling_{tag}{i}") for i, (g, s) in enumerate(zip(group, from_sibling))]
        return [_sum_parts(p, f"add_chips_{tag}{i}") for i, p in enumerate(scatter(sums, "grads_to_chips_" + tag))]

    reduced_later = reduce_group(gpacks[N_FIRST:], "later", _chip_scatter_on_sequencer)
    reduced_first = reduce_group(gpacks[:N_FIRST], "first", _chip_scatter)
    reduced = _share_halves(reduced_first + reduced_later, "reduced_to_sibling")
    updated = [_adamw(r[None], packs[""][i], packs["m_"][i], packs["v_"][i], f"adamw_matrices_{i}") for i, r in enumerate(reduced)]
    big_out = [{n: a.reshape(given[n].shape) for n, a in _unpack_big(*outs).items()} for outs in zip(*updated)]

    small_shapes = [given[n].shape for n in SMALL]
    ssum = _all_reduce_small(_pack_small([grads[n] for n in SMALL], loss[0, 0]), "reduce_vectors")
    small_out = _adamw(ssum[None], _pack_small([given[n] for n in SMALL]), _pack_small([given["m_" + n] for n in SMALL]),
                       _pack_small([given["v_" + n] for n in SMALL]), "adamw_vectors")
    total_loss = ssum.reshape(-1)[sum(int(np.prod(s)) for s in small_shapes)]
    small_out = [_unpack_small(o, small_shapes)[0] for o in small_out]

    names = ("meta", "norm_g", "final_g", "ev_w_in", "ev_q_norm_g", "ev_kv_norm_g", "ev_w_uq", "ev_w_ukv", "ev_w_out", "od_w_in", "od_sinks",
             "od_w_out")
    outs = [total_loss, grad_x]
    for kind in range(4):
        for n in names:
            outs.append(big_out[kind][n] if n in BIG else small_out[kind][SMALL.index(n)])
    return tuple(outs)
```

```python
import functools
import math

import numpy as np
import jax
import jax.numpy as jnp
from jax import lax
from jax.experimental import pallas as pl
from jax.experimental.pallas import tpu as pltpu
from jax.experimental.pallas import tpu_sc as plsc

F32 = jnp.float32
BF16 = jnp.bfloat16

D_MODEL = 1024
BLOCK = 128
N_META = 16
N_PAD = BLOCK - N_META
NORM_EPS = 1e-6
NEG = -1e30
HEAD = 64
SB_HEADS = 8
MLA_HEADS = 8
MLA_Q_LORA = 256
MLA_KV_LORA = 128
MLA_NOPE = 64
MLA_ROPE = 32
ROPE_BASE = 10000.0
SWA_HEADS = 16
SWA_KV_HEADS = 2
SWA_WINDOW = 128
N_CHIPS = 4
N_DEV = 8

ADAM_LR = 0.001
ADAM_B1 = 0.9
ADAM_B2 = 0.999
ADAM_EPS = 1e-08
ADAM_WD = 0.01
ADAM_STEP = 10

VMEM_LIMIT = 48 * 1024 * 1024

EV_G, EV_Q, EV_K, EV_V, EV_CQ, EV_CKV, EV_KR, EV_N = 0, 1024, 1536, 2048, 2560, 2816, 2944, 3072
OD_G, OD_Q, OD_K, OD_V, OD_N = 0, 1024, 2048, 2176, 2304


def _params(sem=None):
    return pltpu.CompilerParams(dimension_semantics=sem, vmem_limit_bytes=VMEM_LIMIT)


def _row_tile(m):
    return 256 if m % 256 == 0 else 128


def _matmul_rows(m):
    for c in (1088, 1024, 768, 640, 512, 384, 256):
        if m % c == 0:
            return c
    return 128


def _dot(a, b):
    return jnp.dot(a.astype(BF16), b.astype(BF16), preferred_element_type=F32)


def _dot_nt(a, b):
    return lax.dot_general(a.astype(BF16), b.astype(BF16), (((1,), (1,)), ((), ())), preferred_element_type=F32)


def _dot_tn(a, b):
    return lax.dot_general(a.astype(BF16), b.astype(BF16), (((0,), (0,)), ((), ())), preferred_element_type=F32)


def _rms_fwd(h, g, name):
    t, d = h.shape
    tm = _row_tile(t)

    def body(h_ref, g_ref, o_ref):
        x = h_ref[...]
        r = lax.rsqrt(jnp.mean(x * x, axis=-1, keepdims=True) + NORM_EPS)
        o_ref[...] = ((x * r) * g_ref[...]).astype(o_ref.dtype)

    return pl.pallas_call(
        body, grid=(t // tm,),
        in_specs=[pl.BlockSpec((tm, d), lambda i: (i, 0)), pl.BlockSpec((1, d), lambda i: (0, 0))],
        out_specs=pl.BlockSpec((tm, d), lambda i: (i, 0)),
        out_shape=jax.ShapeDtypeStruct((t, d), BF16), compiler_params=_params(("parallel",)), name=name,
    )(h, g)


def _rms_bwd(h, g, dy, dres, name):
    t, d = h.shape
    tm = _row_tile(t)

    def body(h_ref, g_ref, dy_ref, dres_ref, dh_ref, dg_ref):
        @pl.when(pl.program_id(0) == 0)
        def _():
            dg_ref[...] = jnp.zeros_like(dg_ref)

        x = h_ref[...]
        r = lax.rsqrt(jnp.mean(x * x, axis=-1, keepdims=True) + NORM_EPS)
        xr = x * r
        dy_ = dy_ref[...]
        u = dy_ * g_ref[...]
        dh_ref[...] = dres_ref[...] + r * (u - xr * jnp.mean(u * xr, axis=-1, keepdims=True))
        dg_ref[...] += jnp.sum(dy_ * xr, axis=0, keepdims=True)

    row = pl.BlockSpec((tm, d), lambda i: (i, 0))
    vec = pl.BlockSpec((1, d), lambda i: (0, 0))
    return pl.pallas_call(
        body, grid=(t // tm,), in_specs=[row, vec, row, row], out_specs=[row, vec],
        out_shape=[jax.ShapeDtypeStruct((t, d), F32), jax.ShapeDtypeStruct((1, d), F32)],
        compiler_params=_params(("arbitrary",)), name=name,
    )(h, g, dy, dres)


def _col_tile(n):
    for c in (1024, 768, 640, 512, 384, 256, 128):
        if n % c == 0:
            return c
    return n


def _mm(a, w, name, res=None, out_dtype=F32, a_cols=None):
    m = a.shape[0]
    k, n = w.shape
    a_blk = 0 if a_cols is None else a_cols[0] // k
    assert a_cols is None or (a_cols[1] == k and a_cols[0] % k == 0)
    tm, tn = _matmul_rows(m), _col_tile(n)

    def body(*refs):
        if res is None:
            a_ref, w_ref, o_ref = refs
            acc = _dot(a_ref[...], w_ref[...])
        else:
            a_ref, w_ref, r_ref, o_ref = refs
            acc = r_ref[...] + _dot(a_ref[...], w_ref[...])
        o_ref[...] = acc.astype(o_ref.dtype)

    in_specs = [pl.BlockSpec((tm, k), lambda j, i: (i, a_blk)), pl.BlockSpec((k, tn), lambda j, i: (0, j))]
    args = [a, w]
    if res is not None:
        in_specs.append(pl.BlockSpec((tm, tn), lambda j, i: (i, j)))
        args.append(res)
    return pl.pallas_call(
        body, grid=(n // tn, m // tm), in_specs=in_specs, out_specs=pl.BlockSpec((tm, tn), lambda j, i: (i, j)),
        out_shape=jax.ShapeDtypeStruct((m, n), out_dtype), compiler_params=_params(("parallel", "parallel")), name=name,
    )(*args)


def _mm_nt(a, w, name):
    m, n = a.shape
    k = w.shape[0]
    tm, tk = _matmul_rows(m), _col_tile(k)

    def body(a_ref, w_ref, o_ref):
        o_ref[...] = _dot_nt(a_ref[...], w_ref[...])

    return pl.pallas_call(
        body, grid=(k // tk, m // tm),
        in_specs=[pl.BlockSpec((tm, n), lambda j, i: (i, 0)), pl.BlockSpec((tk, n), lambda j, i: (j, 0))],
        out_specs=pl.BlockSpec((tm, tk), lambda j, i: (i, j)),
        out_shape=jax.ShapeDtypeStruct((m, k), F32), compiler_params=_params(("parallel", "parallel")), name=name,
    )(a, w)


def _mm_tn(x, dy, name):
    m, k = x.shape
    n = dy.shape[1]
    tm, tn = _matmul_rows(m), _col_tile(n)

    def body(x_ref, dy_ref, o_ref):
        @pl.when(pl.program_id(1) == 0)
        def _():
            o_ref[...] = jnp.zeros_like(o_ref)

        o_ref[...] += _dot_tn(x_ref[...], dy_ref[...])

    return pl.pallas_call(
        body, grid=(n // tn, m // tm),
        in_specs=[pl.BlockSpec((tm, k), lambda j, i: (i, 0)), pl.BlockSpec((tm, tn), lambda j, i: (i, j))],
        out_specs=pl.BlockSpec((k, tn), lambda j, i: (0, j)),
        out_shape=jax.ShapeDtypeStruct((k, n), F32), compiler_params=_params(("parallel", "arbitrary")), name=name,
    )(x, dy)


def _silu_parts(g):
    s = 1.0 / (1.0 + jnp.exp(-g))
    return g * s, s * (1.0 + g * (1.0 - s))


def _gate_fwd(o_parts, proj, name):
    t = proj.shape[0]
    tm = _row_tile(t)
    w = D_MODEL // len(o_parts)

    def body(*refs):
        g_ref, o_ref = refs[-2], refs[-1]
        for p, r in enumerate(refs[:-2]):
            sil, _ = _silu_parts(g_ref[:, p * w:(p + 1) * w])
            o_ref[:, p * w:(p + 1) * w] = (r[...].astype(F32) * sil).astype(o_ref.dtype)

    return pl.pallas_call(
        body, grid=(t // tm,),
        in_specs=[pl.BlockSpec((tm, w), lambda i: (i, 0)) for _ in o_parts] + [pl.BlockSpec((tm, D_MODEL), lambda i: (i, 0))],
        out_specs=pl.BlockSpec((tm, D_MODEL), lambda i: (i, 0)),
        out_shape=jax.ShapeDtypeStruct((t, D_MODEL), BF16), compiler_params=_params(("parallel",)), name=name,
    )(*o_parts, proj)


def _gate_bwd(dao, o_parts, proj, name):
    t = proj.shape[0]
    tm = _row_tile(t)
    np_ = len(o_parts)
    w = D_MODEL // np_

    def body(*refs):
        dao_ref, g_ref = refs[0], refs[1 + np_]
        do_refs, dg_ref = refs[2 + np_:2 + 2 * np_], refs[-1]
        for p in range(np_):
            sl = slice(p * w, (p + 1) * w)
            sil, dsil = _silu_parts(g_ref[:, sl])
            da = dao_ref[:, sl]
            do_refs[p][...] = da * sil
            dg_ref[:, sl] = (da * refs[1 + p][...].astype(F32) * dsil).astype(dg_ref.dtype)

    full = pl.BlockSpec((tm, D_MODEL), lambda i: (i, 0))
    part = pl.BlockSpec((tm, w), lambda i: (i, 0))
    outs = pl.pallas_call(
        body, grid=(t // tm,), in_specs=[full] + [part] * np_ + [full], out_specs=[part] * np_ + [full],
        out_shape=[jax.ShapeDtypeStruct((t, w), F32)] * np_ + [jax.ShapeDtypeStruct((t, D_MODEL), BF16)],
        compiler_params=_params(("parallel",)), name=name,
    )(dao, *o_parts, proj)
    return outs[:np_], outs[np_]


def _loss_head(h2, gf, target, b, lp):
    d = h2.shape[1]
    nb = lp // BLOCK
    h3 = h2.reshape(b, lp, d)

    def body(h_ref, g_ref, t_ref, dh_ref, dg_ref, loss_ref):
        first = (pl.program_id(0) == 0) & (pl.program_id(1) == 0)

        @pl.when(first)
        def _():
            dg_ref[...] = jnp.zeros_like(dg_ref)
            loss_ref[...] = jnp.zeros_like(loss_ref)

        @pl.when(pl.program_id(1) == 0)
        def _():
            dh_ref[...] = jnp.zeros_like(dh_ref)

        @pl.when(pl.program_id(1) > 0)
        def _():
            x = h_ref[0]
            r = lax.rsqrt(jnp.mean(x * x, axis=-1, keepdims=True) + NORM_EPS)
            xr = x * r
            g = g_ref[...]
            diff = xr * g - t_ref[0]
            loss_ref[...] += 0.5 * jnp.sum(jnp.mean(diff * diff, axis=-1, keepdims=True))
            dy = diff * (1.0 / d)
            u = dy * g
            dh_ref[0] = r * (u - xr * jnp.mean(u * xr, axis=-1, keepdims=True))
            dg_ref[...] += jnp.sum(dy * xr, axis=0, keepdims=True)

    blk = pl.BlockSpec((1, BLOCK, d), lambda bi, n: (bi, n, 0))
    dh, dg, loss = pl.pallas_call(
        body, grid=(b, nb),
        in_specs=[blk, pl.BlockSpec((1, d), lambda bi, n: (0, 0)),
                  pl.BlockSpec((1, BLOCK, d), lambda bi, n: (bi, jnp.maximum(n - 1, 0), 0))],
        out_specs=[blk, pl.BlockSpec((1, d), lambda bi, n: (0, 0)), pl.BlockSpec((8, 128), lambda bi, n: (0, 0))],
        out_shape=[jax.ShapeDtypeStruct((b, lp, d), F32), jax.ShapeDtypeStruct((1, d), F32), jax.ShapeDtypeStruct((8, 128), F32)],
        compiler_params=_params(("arbitrary", "arbitrary")), name="loss_head",
    )(h3, gf, target)
    return dh.reshape(b * lp, d), dg, loss


def _iota2(shape, dim):
    return lax.broadcasted_iota(jnp.int32, shape, dim)


KEYS = 256


def _lo_lanes():
    return _iota2((1, BLOCK), 1) < HEAD


def _halves(x, lo):
    zero = jnp.zeros_like(x)
    return jnp.where(lo, x, zero), jnp.where(lo, zero, x)


def _rows_of_pair(a, b):
    return jnp.where(_iota2((BLOCK, 1), 0) < HEAD, a, b)


def _split_rows_t(x):
    xt = x.T
    first = _iota2(xt.shape, 0) < HEAD
    zero = jnp.zeros_like(xt)
    return jnp.concatenate([jnp.where(first, xt, zero), jnp.where(first, zero, xt)], axis=1).astype(BF16)


def _key_chunk(c, lp, t_idx, strict, key_axis):
    first = c * KEYS
    s0 = pl.multiple_of(jnp.minimum(first, lp - KEYS), BLOCK)
    s_idx = s0 + _iota2(t_idx.shape, key_axis)
    seen = (s_idx < t_idx) if strict else (s_idx <= t_idx)
    return s0, seen & (s_idx >= jnp.maximum(first, N_PAD))


def _split_dot(x, tri):
    hi = x.astype(BF16)
    lo = (x - hi.astype(F32)).astype(BF16)
    return jnp.dot(hi, tri, preferred_element_type=F32) + jnp.dot(lo, tri, preferred_element_type=F32)


def _stack_halves(x, lo):
    a, b = _halves(x, lo)
    return jnp.concatenate([a, b], axis=0)


def _pair(a, b, lo):
    return jnp.where(lo, a, b)


def _chunk_starts(lp):
    return [min(c * KEYS, lp - KEYS) for c in range(-(-lp // KEYS))]


def _put_rows(ref, r0, bq, a, b):
    for t in range(bq // BLOCK):
        part = slice(t * BLOCK, (t + 1) * BLOCK)
        ref[0, 0, r0 // BLOCK + t] = jnp.concatenate([a[:, part], b[:, part], jnp.zeros((6, BLOCK), F32)], axis=0)


def _get_rows(ref, r0, bq):
    return [jnp.concatenate([ref[0, 0, r0 // BLOCK + t, h:h + 1, :] for t in range(bq // BLOCK)], axis=1) for h in range(2)]


def _n_chunks(i):
    return (i + 2) // 2


QROWS = 512


def _for_query_tiles(nb, tile):
    per = QROWS // BLOCK

    def step(j, _):
        tile(pl.multiple_of(j * QROWS, QROWS), QROWS, (j + 1) * (QROWS // KEYS))
        return 0

    lax.fori_loop(0, nb // per, step, 0)
    for i in range(nb - nb % per, nb):
        tile(i * BLOCK, BLOCK, _n_chunks(i))


def _walk_chunks(r0, n, chunk, carry, leftwards=False):
    diag = jnp.maximum(r0 // KEYS, 1)

    def span(first, last, masked, carry):
        def step(t, cr):
            return chunk(last - 1 - t if leftwards else first + t, cr, masked)
        return lax.fori_loop(0, last - first, step, carry)

    spans = [(0, 1, True), (1, diag, False), (diag, n, True)]
    for first, last, masked in (reversed(spans) if leftwards else spans):
        carry = span(first, last, masked, carry)
    return carry


def _where(valid, x, other):
    return x if valid is None else jnp.where(valid, x, other)


def _sb_scores(q_h, k, valid, after):
    z = _dot_nt(q_h, k) * (HEAD ** -0.5)
    sp = jnp.log(1.0 + jnp.exp(-jnp.abs(z)))
    lb = jnp.minimum(z, 0.0) - sp
    l1m_all = -jnp.maximum(z, 0.0) - sp
    l1m = _where(valid, l1m_all, 0.0)
    return lb, l1m_all, l1m, _split_dot(l1m, after)


def _pair_stat_spec(nb):
    return pl.BlockSpec((1, 1, nb, 8, BLOCK), lambda bi, hp: (bi, hp, 0, 0, 0))


def _sb_fwd(proj3, name):
    b, lp, _ = proj3.shape
    nb = lp // BLOCK
    npair = SB_HEADS // 2

    def body(q_ref, k_ref, v_ref, o_ref, tot_ref):
        lo = _lo_lanes()
        after = (_iota2((KEYS, KEYS), 0) > _iota2((KEYS, KEYS), 1)).astype(BF16)

        def qtile(r0, bq, n):
            qs = _halves(q_ref[0, pl.ds(r0, bq), :].astype(BF16), lo)
            t_idx = r0 + _iota2((bq, KEYS), 0)

            def kchunk(c, carry, masked):
                cs, acc = carry[:2], carry[2]
                s0, valid = _key_chunk(c, lp, t_idx, True, 1)
                valid = valid if masked else None
                k = k_ref[0, pl.ds(s0, KEYS), :].astype(BF16)
                a_s, new = [], []
                for h in range(2):
                    lb, _, l1m, suf = _sb_scores(qs[h], k, valid, after)
                    a_s.append(_where(valid, jnp.exp(lb + suf + cs[h]), 0.0).astype(BF16))
                    new.append(cs[h] + jnp.sum(l1m, axis=1, keepdims=True))
                v_bd = _stack_halves(v_ref[0, pl.ds(s0, KEYS), :].astype(BF16), lo)
                return (*new, acc + jnp.dot(jnp.concatenate(a_s, axis=1), v_bd, preferred_element_type=F32))

            zero = jnp.zeros((bq, 1), F32)
            c_a, c_b, acc = _walk_chunks(r0, n, kchunk, (zero, zero, jnp.zeros((bq, BLOCK), F32)), leftwards=True)
            o_ref[0, pl.ds(r0, bq), :] = acc
            tot_ref[0, pl.ds(r0, bq), :] = jnp.broadcast_to(_pair(c_a, c_b, lo), (bq, BLOCK))

        _for_query_tiles(nb, qtile)

    def col(first):
        return pl.BlockSpec((1, lp, 2 * HEAD), lambda bi, hp: (bi, 0, first // (2 * HEAD) + hp))

    shp = jax.ShapeDtypeStruct((b, lp, SB_HEADS * HEAD), F32)
    return pl.pallas_call(
        body, grid=(b, npair), in_specs=[col(EV_Q), col(EV_K), col(EV_V)], out_specs=[col(0), col(0)], out_shape=[shp, shp],
        compiler_params=_params(("parallel", "parallel")), name=name,
    )(proj3, proj3, proj3)


def _sb_bwd(proj3, tot, do, name):
    b, lp, _ = proj3.shape
    nb = lp // BLOCK
    npair = SB_HEADS // 2

    def body(q_ref, k_ref, v_ref, tot_ref, do_ref, dq_ref, dk_ref, dv_ref):
        lo = _lo_lanes()
        after = (_iota2((KEYS, KEYS), 0) > _iota2((KEYS, KEYS), 1)).astype(BF16)
        before = (_iota2((KEYS, KEYS), 0) < _iota2((KEYS, KEYS), 1)).astype(BF16)
        dk_ref[...] = jnp.zeros_like(dk_ref)
        dv_ref[...] = jnp.zeros_like(dv_ref)

        def qtile(r0, bq, n):
            rows = pl.ds(r0, bq)
            qs = _halves(q_ref[0, rows, :].astype(BF16), lo)
            dos = _halves(do_ref[0, rows, :].astype(BF16), lo)
            tot_i = tot_ref[0, rows, :]
            tots = (tot_i[:, 0:1], tot_i[:, HEAD:HEAD + 1])
            q_st, do_st = jnp.concatenate(qs, axis=0), jnp.concatenate(dos, axis=0)
            t_idx = r0 + _iota2((bq, KEYS), 0)

            def kchunk(c, carry, masked):
                s0, valid = _key_chunk(c, lp, t_idx, True, 1)
                valid = valid if masked else None
                keys = pl.ds(s0, KEYS)
                k = k_ref[0, keys, :].astype(BF16)
                v = v_ref[0, keys, :].astype(BF16)
                a_s, dzs, new = [], [], []
                for h in range(2):
                    left, pre = carry[2 * h], carry[2 * h + 1]
                    lb, l1m_all, l1m, suf = _sb_scores(qs[h], k, valid, after)
                    here = jnp.sum(l1m, axis=1, keepdims=True)
                    a = _where(valid, jnp.exp(lb + suf + (tots[h] - left - here)), 0.0)
                    w = a * _dot_nt(dos[h], v)
                    dz = _where(valid, w * jnp.exp(l1m_all) - (pre + _split_dot(w, before)) * jnp.exp(lb), 0.0) * (HEAD ** -0.5)
                    new += [left + here, pre + jnp.sum(w, axis=1, keepdims=True)]
                    a_s.append(a.astype(BF16))
                    dzs.append(dz.astype(BF16))
                dk_ref[0, keys, :] += _dot_tn(jnp.concatenate(dzs, axis=0), q_st)
                dv_ref[0, keys, :] += _dot_tn(jnp.concatenate(a_s, axis=0), do_st)
                dq = carry[4] + jnp.dot(jnp.concatenate(dzs, axis=1), _stack_halves(k, lo), preferred_element_type=F32)
                return (*new, dq)

            zero = jnp.zeros((bq, 1), F32)
            out = _walk_chunks(r0, n, kchunk, (zero, zero, zero, zero, jnp.zeros((bq, BLOCK), F32)))
            dq_ref[0, rows, :] = out[4]

        _for_query_tiles(nb, qtile)

    def col(first):
        return pl.BlockSpec((1, lp, 2 * HEAD), lambda bi, hp: (bi, 0, first // (2 * HEAD) + hp))

    shp = jax.ShapeDtypeStruct((b, lp, SB_HEADS * HEAD), F32)
    return pl.pallas_call(
        body, grid=(b, npair), in_specs=[col(EV_Q), col(EV_K), col(EV_V), col(0), col(0)], out_specs=[col(0)] * 3, out_shape=[shp] * 3,
        compiler_params=_params(("parallel", "parallel")), name=name,
    )(proj3, proj3, proj3, tot, do)


def _rope_tables(lp):
    half = MLA_ROPE // 2
    pos = (np.arange(lp) - N_PAD).astype(np.float32)
    inv = jnp.asarray(ROPE_BASE, F32) ** (-jnp.arange(half, dtype=F32) / half)
    ang = jnp.asarray(pos)[:, None] * inv[None, :]
    cos, sin = jnp.cos(ang), jnp.sin(ang)
    zeros = lambda n: jnp.zeros((lp, n), F32)
    c = jnp.concatenate([jnp.ones((lp, MLA_NOPE), F32), cos, cos, zeros(32)], axis=1)
    s1 = jnp.concatenate([zeros(MLA_NOPE), -sin, zeros(half), zeros(32)], axis=1)
    s2 = jnp.concatenate([zeros(MLA_NOPE), zeros(half), sin, zeros(32)], axis=1)
    return c, s1, s2


def _rope(x, c, s1, s2):
    half = MLA_ROPE // 2
    return x * c + pltpu.roll(x, BLOCK - half, 1) * s1 + pltpu.roll(x, half, 1) * s2


def _rope_t(dy, c, s1, s2):
    half = MLA_ROPE // 2
    return dy * c + pltpu.roll(dy * s1, half, 1) + pltpu.roll(dy * s2, BLOCK - half, 1)


def _rms_rows(x, g):
    r = lax.rsqrt(jnp.mean(x * x, axis=-1, keepdims=True) + NORM_EPS)
    return x * r, r


def _mla_prep_fwd(proj3, gq, gkv, wq, wk, wv, tabs, name):
    b, lp, _ = proj3.shape
    nb = lp // BLOCK
    hw = MLA_HEADS * BLOCK

    def body(cq_ref, ckv_ref, kr_ref, gq_ref, gkv_ref, wq_ref, wk_ref, wv_ref, c_ref, s1_ref, s2_ref, qf_ref, kf_ref, v_ref):
        c, s1, s2 = c_ref[...], s1_ref[...], s2_ref[...]
        xq, _ = _rms_rows(cq_ref[0], None)
        qh = _dot(xq * gq_ref[...], wq_ref[...])
        xk, _ = _rms_rows(ckv_ref[0], None)
        ckv_n = xk * gkv_ref[...]
        kv = _dot(ckv_n, wk_ref[...])
        v_ref[0] = _dot(ckv_n, wv_ref[...]).astype(v_ref.dtype)
        kr = _rope(kr_ref[0], c, s1, s2)
        for h in range(MLA_HEADS):
            ls = slice(h * BLOCK, (h + 1) * BLOCK)
            qf_ref[0, :, ls] = _rope(qh[:, ls], c, s1, s2).astype(qf_ref.dtype)
            kf_ref[0, :, ls] = (kv[:, ls] + kr).astype(kf_ref.dtype)

    def col(first, width):
        return pl.BlockSpec((1, BLOCK, width), lambda bi, n: (bi, n, first // width))

    def whole(a):
        return pl.BlockSpec(a.shape, lambda bi, n: (0,) * a.ndim)

    tab = pl.BlockSpec((BLOCK, BLOCK), lambda bi, n: (n, 0))
    return pl.pallas_call(
        body, grid=(b, nb),
        in_specs=[col(EV_CQ, MLA_Q_LORA), col(EV_CKV, MLA_KV_LORA), col(EV_KR, BLOCK), whole(gq), whole(gkv), whole(wq), whole(wk),
                  whole(wv), tab, tab, tab],
        out_specs=[col(0, hw), col(0, hw), col(0, MLA_HEADS * HEAD)],
        out_shape=[jax.ShapeDtypeStruct((b, lp, hw), BF16), jax.ShapeDtypeStruct((b, lp, hw), BF16),
                   jax.ShapeDtypeStruct((b, lp, MLA_HEADS * HEAD), BF16)],
        compiler_params=_params(("parallel", "parallel")), name=name,
    )(proj3, proj3, proj3, gq, gkv, wq, wk, wv, *tabs)


def _mla_prep_bwd(proj3, gq, gkv, wq, wk, wv, tabs, dqf, dkf, dv, name):
    b, lp, _ = proj3.shape
    nb = lp // BLOCK
    hw = MLA_HEADS * BLOCK

    def body(cq_ref, ckv_ref, gq_ref, gkv_ref, wq_ref, wk_ref, wv_ref, c_ref, s1_ref, s2_ref, dqf_ref, dkf_ref, dv_ref,
             dcq_ref, dckv_ref, dkr_ref, dwq_ref, dwk_ref, dwv_ref, dgq_ref, dgkv_ref, dqh):
        @pl.when((pl.program_id(0) == 0) & (pl.program_id(1) == 0))
        def _():
            for r in (dwq_ref, dwk_ref, dwv_ref, dgq_ref, dgkv_ref):
                r[...] = jnp.zeros_like(r)

        c, s1, s2 = c_ref[...], s1_ref[...], s2_ref[...]
        dkr = jnp.zeros((BLOCK, BLOCK), F32)
        for h in range(MLA_HEADS):
            ls = slice(h * BLOCK, (h + 1) * BLOCK)
            dqh[:, ls] = _rope_t(dqf_ref[0, :, ls].astype(F32), c, s1, s2).astype(dqh.dtype)
            dkr = dkr + dkf_ref[0, :, ls].astype(F32)
        dkr_ref[0] = _rope_t(dkr, c, s1, s2).astype(dkr_ref.dtype)

        def norm_bwd(x, g, dy, dg_ref):
            xr, r = _rms_rows(x, None)
            u = dy * g
            dg_ref[...] += jnp.sum(dy * xr, axis=0, keepdims=True)
            return r * (u - xr * jnp.mean(u * xr, axis=-1, keepdims=True))

        xq, _ = _rms_rows(cq_ref[0], None)
        cq_n = xq * gq_ref[...]
        dwq_ref[...] += _dot_tn(cq_n, dqh[...])
        dcq_ref[0] = norm_bwd(cq_ref[0], gq_ref[...], _dot_nt(dqh[...], wq_ref[...]), dgq_ref).astype(dcq_ref.dtype)
        xk, _ = _rms_rows(ckv_ref[0], None)
        ckv_n = xk * gkv_ref[...]
        dkf_, dv_ = dkf_ref[0], dv_ref[0]
        dwk_ref[...] += _dot_tn(ckv_n, dkf_)
        dwv_ref[...] += _dot_tn(ckv_n, dv_)
        dckv_n = _dot_nt(dkf_, wk_ref[...]) + _dot_nt(dv_, wv_ref[...])
        dckv_ref[0] = norm_bwd(ckv_ref[0], gkv_ref[...], dckv_n, dgkv_ref).astype(dckv_ref.dtype)

    def col(first, width):
        return pl.BlockSpec((1, BLOCK, width), lambda bi, n: (bi, n, first // width))

    def whole(a):
        return pl.BlockSpec(a.shape, lambda bi, n: (0,) * len(a.shape))

    tab = pl.BlockSpec((BLOCK, BLOCK), lambda bi, n: (n, 0))
    acc_shapes = [jax.ShapeDtypeStruct(a.shape, F32) for a in (wq, wk, wv, gq, gkv)]
    return pl.pallas_call(
        body, grid=(b, nb),
        in_specs=[col(EV_CQ, MLA_Q_LORA), col(EV_CKV, MLA_KV_LORA), whole(gq), whole(gkv), whole(wq), whole(wk), whole(wv), tab, tab, tab,
                  col(0, hw), col(0, hw), col(0, MLA_HEADS * HEAD)],
        out_specs=[col(0, MLA_Q_LORA), col(0, MLA_KV_LORA), col(0, BLOCK)] + [whole(a) for a in acc_shapes],
        out_shape=[jax.ShapeDtypeStruct((b, lp, MLA_Q_LORA), BF16), jax.ShapeDtypeStruct((b, lp, MLA_KV_LORA), BF16),
                   jax.ShapeDtypeStruct((b, lp, BLOCK), BF16)] + acc_shapes,
        scratch_shapes=[pltpu.VMEM((BLOCK, hw), BF16)],
        compiler_params=_params(("arbitrary", "arbitrary")), name=name,
    )(proj3, proj3, gq, gkv, wq, wk, wv, *tabs, dqf, dkf, dv)


def _mla_fwd(qf, kf, v, name):
    b, lp, _ = qf.shape
    nb = lp // BLOCK
    npair = MLA_HEADS // 2
    scale = (MLA_NOPE + MLA_ROPE) ** -0.5
    starts = _chunk_starts(lp)

    def body(q_ref, k_ref, v_ref, o_ref, lse_ref, vt_ref):
        for c, s0 in enumerate(starts):
            vt_ref[c] = _split_rows_t(v_ref[0, s0:s0 + KEYS, :].astype(F32))

        def qtile(r0, bq, n):
            qs = [q_ref[0, pl.ds(r0, bq), h * BLOCK:(h + 1) * BLOCK] for h in range(2)]
            t_idx = r0 + _iota2((KEYS, bq), 1)

            def kchunk(c, carry, masked):
                stats, acc = carry[:4], carry[4]
                s0, valid = _key_chunk(c, lp, t_idx, False, 0)
                valid = valid if masked else None
                ps, new, alphas = [], [], []
                for h in range(2):
                    m, l = stats[2 * h], stats[2 * h + 1]
                    s = _where(valid, _dot_nt(k_ref[0, pl.ds(s0, KEYS), h * BLOCK:(h + 1) * BLOCK], qs[h]) * scale, NEG)
                    m_new = jnp.maximum(m, jnp.max(s, axis=0, keepdims=True))
                    p = _where(valid, jnp.exp(s - m_new), 0.0)
                    alpha = jnp.exp(m - m_new)
                    new += [m_new, alpha * l + jnp.sum(p, axis=0, keepdims=True)]
                    alphas.append(alpha)
                    ps.append(p.astype(BF16))
                pv = jnp.dot(vt_ref[c], jnp.concatenate(ps, axis=0), preferred_element_type=F32)
                return (*new, _rows_of_pair(alphas[0], alphas[1]) * acc + pv)

            neg, zero = jnp.full((1, bq), NEG, F32), jnp.zeros((1, bq), F32)
            m_a, l_a, m_b, l_b, acc = _walk_chunks(r0, n, kchunk, (neg, zero, neg, zero, jnp.zeros((BLOCK, bq), F32)))
            safe = [jnp.where(l > 0.0, l, 1.0) for l in (l_a, l_b)]
            o_ref[0, pl.ds(r0, bq), :] = (acc / _rows_of_pair(safe[0], safe[1])).T
            lse = [jnp.where(l > 0.0, m + jnp.log(sf), 0.0) for m, l, sf in ((m_a, l_a, safe[0]), (m_b, l_b, safe[1]))]
            _put_rows(lse_ref, r0, bq, lse[0], lse[1])

        _for_query_tiles(nb, qtile)

    wide = pl.BlockSpec((1, lp, 2 * BLOCK), lambda bi, hp: (bi, 0, hp))
    thin = pl.BlockSpec((1, lp, 2 * HEAD), lambda bi, hp: (bi, 0, hp))
    return pl.pallas_call(
        body, grid=(b, npair), in_specs=[wide, wide, thin], out_specs=[thin, _pair_stat_spec(nb)],
        out_shape=[jax.ShapeDtypeStruct((b, lp, MLA_HEADS * HEAD), F32), jax.ShapeDtypeStruct((b, npair, nb, 8, BLOCK), F32)],
        scratch_shapes=[pltpu.VMEM((len(starts), BLOCK, 2 * KEYS), BF16)],
        compiler_params=_params(("parallel", "parallel")), name=name,
    )(qf, kf, v)


def _mla_bwd(qf, kf, v, o, lse, do, name):
    b, lp, _ = qf.shape
    nb = lp // BLOCK
    npair = MLA_HEADS // 2
    scale = (MLA_NOPE + MLA_ROPE) ** -0.5

    starts = _chunk_starts(lp)

    def body(q_ref, k_ref, v_ref, o_ref, lse_ref, do_ref, dq_ref, dk_ref, dv_ref, kt_ref):
        lo = _lo_lanes()
        dk_ref[...] = jnp.zeros_like(dk_ref)
        dv_ref[...] = jnp.zeros_like(dv_ref)
        for c, s0 in enumerate(starts):
            for h in range(2):
                kt_ref[c, h] = k_ref[0, s0:s0 + KEYS, h * BLOCK:(h + 1) * BLOCK].astype(F32).T.astype(BF16)

        def qtile(r0, bq, n):
            rows = pl.ds(r0, bq)
            qs = [q_ref[0, rows, h * BLOCK:(h + 1) * BLOCK] for h in range(2)]
            do_i = do_ref[0, rows, :]
            dos = _halves(do_i.astype(BF16), lo)
            do_st = jnp.concatenate(dos, axis=0)
            both = (do_i * o_ref[0, rows, :]).T
            dsum = (jnp.sum(both[:HEAD], axis=0, keepdims=True), jnp.sum(both[HEAD:], axis=0, keepdims=True))
            lses = _get_rows(lse_ref, r0, bq)
            t_idx = r0 + _iota2((KEYS, bq), 1)

            def kchunk(c, dqts, masked):
                s0, valid = _key_chunk(c, lp, t_idx, False, 0)
                valid = valid if masked else None
                keys = pl.ds(s0, KEYS)
                v_c = v_ref[0, keys, :]
                ps, out = [], []
                for h in range(2):
                    lanes = slice(h * BLOCK, (h + 1) * BLOCK)
                    s = _dot_nt(k_ref[0, keys, lanes], qs[h]) * scale
                    p = _where(valid, jnp.exp(s - lses[h]), 0.0)
                    ds = (p * (_dot_nt(v_c, dos[h]) - dsum[h]) * scale).astype(BF16)
                    dk_ref[0, keys, lanes] += jnp.dot(ds, qs[h], preferred_element_type=F32)
                    out.append(dqts[h] + jnp.dot(kt_ref[c, h], ds, preferred_element_type=F32))
                    ps.append(p.astype(BF16))
                dv_ref[0, keys, :] += jnp.dot(jnp.concatenate(ps, axis=1), do_st, preferred_element_type=F32)
                return tuple(out)

            zero = jnp.zeros((BLOCK, bq), F32)
            dq_a, dq_b = _walk_chunks(r0, n, kchunk, (zero, zero))
            dq_ref[0, rows, 0:BLOCK] = dq_a.T
            dq_ref[0, rows, BLOCK:2 * BLOCK] = dq_b.T

        _for_query_tiles(nb, qtile)

    wide = pl.BlockSpec((1, lp, 2 * BLOCK), lambda bi, hp: (bi, 0, hp))
    thin = pl.BlockSpec((1, lp, 2 * HEAD), lambda bi, hp: (bi, 0, hp))
    return pl.pallas_call(
        body, grid=(b, npair), in_specs=[wide, wide, thin, thin, _pair_stat_spec(nb), thin], out_specs=[wide, wide, thin],
        out_shape=[jax.ShapeDtypeStruct(qf.shape, F32), jax.ShapeDtypeStruct(qf.shape, F32), jax.ShapeDtypeStruct(v.shape, F32)],
        scratch_shapes=[pltpu.VMEM((len(starts), 2, BLOCK, KEYS), BF16)],
        compiler_params=_params(("parallel", "parallel")), name=name,
    )(qf, kf, v, o, lse, do)


def _swa_keys(k_ref, v_ref, n, kv):
    prev = jnp.maximum(n - 1, 0)
    rows = lambda blk: pl.ds(pl.multiple_of(blk * BLOCK, BLOCK), BLOCK)
    mine = (_iota2((1, BLOCK), 1) >= HEAD).astype(jnp.int32) == kv

    def both_halves(ref):
        x = jnp.concatenate([ref[0, rows(prev), :], ref[0, rows(n), :], ref[0, 0:BLOCK, :]], axis=0)
        return jnp.where(mine, x, pltpu.roll(x, HEAD, 1))

    slot = _iota2((3 * BLOCK, BLOCK), 0)
    loc = slot % BLOCK
    s_idx = jnp.where(slot < BLOCK, (n - 1) * BLOCK + loc, jnp.where(slot < 2 * BLOCK, n * BLOCK + loc, loc))
    dist = n * BLOCK + _iota2((3 * BLOCK, BLOCK), 1) - s_idx
    band = (slot < 2 * BLOCK) & (dist >= 0) & (dist < SWA_WINDOW) & (s_idx >= BLOCK)
    meta = (slot >= 2 * BLOCK) & (s_idx >= N_PAD) & (dist >= 0)
    return both_halves(k_ref), both_halves(v_ref), band | meta, dist.astype(F32), prev


def _swa_probs(q_h, kdup, valid, dist, head, sink_ref):
    slope = jnp.exp(jnp.full((1, 1), -8.0 * math.log(2.0) / SWA_HEADS, F32) * (head + 1).astype(F32))
    s = jnp.where(valid, _dot_nt(kdup, q_h) * (HEAD ** -0.5) - slope * dist, NEG)
    sink = sink_ref[pl.ds(head, 1), 0:1]
    m = jnp.maximum(jnp.max(s, axis=0, keepdims=True), sink)
    e = jnp.where(valid, jnp.exp(s - m), 0.0)
    es = jnp.exp(sink - m)
    inv = 1.0 / (jnp.sum(e, axis=0, keepdims=True) + es)
    return e * inv, es * inv


SWA_PAIRS = SWA_HEADS // SWA_KV_HEADS // 2
SWA_GROUP = SWA_PAIRS * 2 * HEAD


def _swa_specs(b, lp):
    nb = lp // BLOCK
    qcol = lambda first: pl.BlockSpec((1, BLOCK, SWA_GROUP), lambda bi, kv, n: (bi, n, first // SWA_GROUP + kv))
    kcol = lambda first: pl.BlockSpec((1, lp, BLOCK), lambda bi, kv, n: (bi, 0, first // BLOCK))
    sink = pl.BlockSpec((SWA_HEADS, BLOCK), lambda bi, kv, n: (0, 0))
    return (b, SWA_KV_HEADS, nb), qcol, kcol, sink


def _swa_fwd(proj3, sinks, name):
    b, lp, _ = proj3.shape
    grid, qcol, kcol, sink = _swa_specs(b, lp)

    def body(q_ref, k_ref, v_ref, sink_ref, o_ref):
        kv, n = pl.program_id(1), pl.program_id(2)
        lo = _lo_lanes()
        kdup, vdup, valid, dist, _ = _swa_keys(k_ref, v_ref, n, kv)
        kdup = kdup.astype(BF16)
        vt = _split_rows_t(vdup)
        for p in range(SWA_PAIRS):
            lanes = slice(p * BLOCK, (p + 1) * BLOCK)
            qs = _halves(q_ref[0, :, lanes].astype(BF16), lo)
            probs = [_swa_probs(qs[hh], kdup, valid, dist, (kv * SWA_PAIRS + p) * 2 + hh, sink_ref)[0].astype(BF16) for hh in range(2)]
            o_ref[0, :, lanes] = jnp.dot(vt, jnp.concatenate(probs, axis=0), preferred_element_type=F32).T

    return pl.pallas_call(
        body, grid=grid, in_specs=[qcol(OD_Q), kcol(OD_K), kcol(OD_V), sink], out_specs=qcol(0),
        out_shape=jax.ShapeDtypeStruct((b, lp, SWA_HEADS * HEAD), F32),
        compiler_params=_params(("parallel", "parallel", "parallel")), name=name,
    )(proj3, proj3, proj3, sinks)


def _swa_bwd(proj3, sinks, do, name):
    b, lp, _ = proj3.shape
    nb = lp // BLOCK
    grid, qcol, kcol, sink = _swa_specs(b, lp)

    def body(q_ref, k_ref, v_ref, sink_ref, do_ref, dq_ref, dk_ref, dv_ref, dsink_ref, dk_acc, dv_acc):
        kv, n = pl.program_id(1), pl.program_id(2)

        @pl.when((n == 0) & (pl.program_id(0) == 0) & (kv == 0))
        def _():
            dsink_ref[...] = jnp.zeros_like(dsink_ref)

        @pl.when(n == 0)
        def _():
            dk_acc[...] = jnp.zeros_like(dk_acc)
            dv_acc[...] = jnp.zeros_like(dv_acc)

        lo = _lo_lanes()
        kdup, vdup, valid, dist, prev = _swa_keys(k_ref, v_ref, n, kv)
        kt = _split_rows_t(kdup)
        kdup, vdup = kdup.astype(BF16), vdup.astype(BF16)
        dkc = jnp.zeros((3 * BLOCK, BLOCK), F32)
        dvc = jnp.zeros((3 * BLOCK, BLOCK), F32)
        for p in range(SWA_PAIRS):
            lanes = slice(p * BLOCK, (p + 1) * BLOCK)
            qs = _halves(q_ref[0, :, lanes].astype(BF16), lo)
            dos = _halves(do_ref[0, :, lanes].astype(BF16), lo)
            dss, prs = [], []
            for hh in range(2):
                head = (kv * SWA_PAIRS + p) * 2 + hh
                pr, ps = _swa_probs(qs[hh], kdup, valid, dist, head, sink_ref)
                dp = _dot_nt(vdup, dos[hh])
                dsum = jnp.sum(pr * dp, axis=0, keepdims=True)
                dsink_ref[pl.ds(head, 1), :] += jnp.broadcast_to(-jnp.sum(ps * dsum, axis=1, keepdims=True), (1, BLOCK))
                dss.append((pr * (dp - dsum) * (HEAD ** -0.5)).astype(BF16))
                prs.append(pr.astype(BF16))
            dq_ref[0, :, lanes] = jnp.dot(kt, jnp.concatenate(dss, axis=0), preferred_element_type=F32).T
            dkc = dkc + jnp.dot(jnp.concatenate(dss, axis=1), jnp.concatenate(qs, axis=0), preferred_element_type=F32)
            dvc = dvc + jnp.dot(jnp.concatenate(prs, axis=1), jnp.concatenate(dos, axis=0), preferred_element_type=F32)
        rows = lambda blk: pl.ds(pl.multiple_of(blk * BLOCK, BLOCK), BLOCK)
        for part, r in enumerate((rows(prev), rows(n), slice(0, BLOCK))):
            dk_acc[r, :] += dkc[part * BLOCK:(part + 1) * BLOCK]
            dv_acc[r, :] += dvc[part * BLOCK:(part + 1) * BLOCK]

        for acc, ref in ((dk_acc, dk_ref), (dv_acc, dv_ref)):
            @pl.when((n == nb - 1) & (kv == 0))
            def _():
                x = acc[...]
                ref[0] = x + pltpu.roll(x, HEAD, 1)

            @pl.when((n == nb - 1) & (kv == 1))
            def _():
                x = acc[...]
                ref[0] = jnp.where(lo, ref[0], x + pltpu.roll(x, HEAD, 1))

    kvout = pl.BlockSpec((1, lp, BLOCK), lambda bi, kv, n: (bi, 0, 0))
    kvshape = jax.ShapeDtypeStruct((b, lp, BLOCK), F32)
    return pl.pallas_call(
        body, grid=grid, in_specs=[qcol(OD_Q), kcol(OD_K), kcol(OD_V), sink, qcol(0)], out_specs=[qcol(0), kvout, kvout, sink],
        out_shape=[jax.ShapeDtypeStruct((b, lp, SWA_HEADS * HEAD), F32), kvshape, kvshape, jax.ShapeDtypeStruct((SWA_HEADS, BLOCK), F32)],
        scratch_shapes=[pltpu.VMEM((lp, BLOCK), F32), pltpu.VMEM((lp, BLOCK), F32)],
        compiler_params=_params(("arbitrary", "arbitrary", "arbitrary")), name=name,
    )(proj3, proj3, proj3, sinks, do)


def _kernel_weights(ev_w_in, ev_w_uq, ev_w_ukv, od_w_in):
    zeros = lambda r, c: jnp.zeros((r, c), ev_w_in.dtype)
    q_sb, k_sb, v_sb, g_sb, c_q, c_kv, k_r, g_mla = jnp.split(ev_w_in, [512, 1024, 1536, 2048, 2304, 2432, 2464], axis=1)
    w0 = jnp.concatenate([g_sb, g_mla, q_sb, k_sb, v_sb, c_q, c_kv, zeros(D_MODEL, MLA_NOPE), k_r, zeros(D_MODEL, 32)], axis=1)
    uq = ev_w_uq.reshape(MLA_Q_LORA, MLA_HEADS, MLA_NOPE + MLA_ROPE)
    wq = jnp.pad(uq, ((0, 0), (0, 0), (0, BLOCK - MLA_NOPE - MLA_ROPE))).reshape(MLA_Q_LORA, MLA_HEADS * BLOCK)
    ukv = ev_w_ukv.reshape(MLA_KV_LORA, MLA_HEADS, BLOCK)
    wk = jnp.pad(ukv[:, :, :MLA_NOPE], ((0, 0), (0, 0), (0, BLOCK - MLA_NOPE))).reshape(MLA_KV_LORA, MLA_HEADS * BLOCK)
    wv = ukv[:, :, MLA_NOPE:].reshape(MLA_KV_LORA, MLA_HEADS * HEAD)
    q, k, v, g = jnp.split(od_w_in, [1024, 1152, 1280], axis=1)
    w1 = jnp.concatenate([g, q, k, v], axis=1)
    return w0, wq, wk, wv, w1


def _original_grads(dw0, dwq, dwk, dwv, dw1):
    sl = lambda a, first, n: a[:, first:first + n]
    d_ev_w_in = jnp.concatenate([sl(dw0, EV_Q, 512), sl(dw0, EV_K, 512), sl(dw0, EV_V, 512), sl(dw0, EV_G, 512), sl(dw0, EV_CQ, 256),
                                 sl(dw0, EV_CKV, 128), sl(dw0, EV_KR + MLA_NOPE, MLA_ROPE), sl(dw0, EV_G + 512, 512)], axis=1)
    d_uq = dwq.reshape(MLA_Q_LORA, MLA_HEADS, BLOCK)[:, :, :MLA_NOPE + MLA_ROPE].reshape(MLA_Q_LORA, -1)
    d_ukv = jnp.concatenate([dwk.reshape(MLA_KV_LORA, MLA_HEADS, BLOCK)[:, :, :MLA_NOPE], dwv.reshape(MLA_KV_LORA, MLA_HEADS, HEAD)],
                            axis=2).reshape(MLA_KV_LORA, -1)
    d_od_w_in = jnp.concatenate([sl(dw1, OD_Q, 1024), sl(dw1, OD_K, 128), sl(dw1, OD_V, 128), sl(dw1, OD_G, 1024)], axis=1)
    return d_ev_w_in, d_uq, d_ukv, d_od_w_in


def _meta_rows_sum(dh0_3):
    b, _, d = dh0_3.shape

    def body(x_ref, o_ref):
        acc = x_ref[0, N_PAD:BLOCK, :]
        for i in range(1, b):
            acc = acc + x_ref[i, N_PAD:BLOCK, :]
        o_ref[...] = acc

    return pl.pallas_call(
        body, grid=(1,), in_specs=[pl.BlockSpec((b, BLOCK, d), lambda i: (0, 0, 0))], out_specs=pl.BlockSpec((N_META, d), lambda i: (0, 0)),
        out_shape=jax.ShapeDtypeStruct((N_META, d), F32), compiler_params=_params(("arbitrary",)), name="meta_rows_sum",
    )(dh0_3)


def _local_step(x, meta, norm_g, final_g, gq, gkv, sinks, target, ev_w_in, ev_w_uq, ev_w_ukv, wo0, od_w_in, wo1):
    b, seq, d = x.shape
    lp = seq + BLOCK
    t = b * lp
    w0, wq, wk, wv, w1 = _kernel_weights(ev_w_in, ev_w_uq, ev_w_ukv, od_w_in)
    h0 = jnp.concatenate([jnp.zeros((b, N_PAD, d), F32), jnp.broadcast_to(meta[None], (b, N_META, d)), x], axis=1).reshape(t, d)
    tabs = _rope_tables(lp)
    g0, g1 = norm_g[0:1], norm_g[1:2]

    hn0 = _rms_fwd(h0, g0, "norm0")
    proj0 = _mm(hn0, w0, "inproj0")
    p0 = proj0.reshape(b, lp, EV_N)
    o_sb, sb_tot = _sb_fwd(p0, "sb_fwd")
    qf, kf, v = _mla_prep_fwd(p0, gq, gkv, wq, wk, wv, tabs, "mla_prep_fwd")
    o_mla, lse = _mla_fwd(qf, kf, v, "mla_fwd")
    o0 = [o_sb.reshape(t, -1), o_mla.reshape(t, -1)]
    ao0 = _gate_fwd(o0, proj0, "gate0")
    h1 = _mm(ao0, wo0, "outproj0", res=h0)

    hn1 = _rms_fwd(h1, g1, "norm1")
    proj1 = _mm(hn1, w1, "inproj1")
    p1 = proj1.reshape(b, lp, OD_N)
    sinks_b = jnp.broadcast_to(sinks.reshape(SWA_HEADS, 1), (SWA_HEADS, BLOCK))
    o1 = _swa_fwd(p1, sinks_b, "swa_fwd").reshape(t, -1)
    ao1 = _gate_fwd([o1], proj1, "gate1")
    h2 = _mm(ao1, wo1, "outproj1", res=h1)

    dh2, d_final_g, loss = _loss_head(h2, final_g.reshape(1, d), target, b, lp)

    d_wo1 = _mm_tn(ao1, dh2, "d_wo1")
    dao1 = _mm_nt(dh2, wo1, "d_ao1")
    (do1,), dg1 = _gate_bwd(dao1, [o1], proj1, "gate1_bwd")
    dq1, dk4, dv4, d_sinks = _swa_bwd(p1, sinks_b, do1.reshape(b, lp, -1), "swa_bwd")
    unheads = lambda a: a.reshape(t, SWA_KV_HEADS * HEAD).astype(BF16)
    dproj1 = jnp.concatenate([dg1, dq1.reshape(t, -1).astype(BF16), unheads(dk4), unheads(dv4)], axis=1)
    d_w1 = _mm_tn(hn1, dproj1, "d_w1")
    dhn1 = _mm_nt(dproj1, w1, "d_hn1")
    dh1, d_g1 = _rms_bwd(h1, g1, dhn1, dh2, "norm1_bwd")

    d_wo0 = _mm_tn(ao0, dh1, "d_wo0")
    dao0 = _mm_nt(dh1, wo0, "d_ao0")
    (do_sb, do_mla), dg0 = _gate_bwd(dao0, o0, proj0, "gate0_bwd")
    dq_sb, dk_sb, dv_sb = _sb_bwd(p0, sb_tot, do_sb.reshape(b, lp, -1), "sb_bwd")
    dqf, dkf, dv = _mla_bwd(qf, kf, v, o_mla, lse, do_mla.reshape(b, lp, -1), "mla_bwd")
    dcq, dckv, dkr, d_wq, d_wk, d_wv, d_gq, d_gkv = _mla_prep_bwd(p0, gq, gkv, wq, wk, wv, tabs, dqf, dkf, dv, "mla_prep_bwd")
    flat = lambda a: a.reshape(t, -1).astype(BF16)
    dproj0 = jnp.concatenate([dg0, flat(dq_sb), flat(dk_sb), flat(dv_sb), flat(dcq), flat(dckv), flat(dkr)], axis=1)
    d_w0 = _mm_tn(hn0, dproj0, "d_w0")
    dhn0 = _mm_nt(dproj0, w0, "d_hn0")
    dh0, d_g0 = _rms_bwd(h0, g0, dhn0, dh1, "norm0_bwd")
    dh0 = dh0.reshape(b, lp, d)

    d_ev_w_in, d_uq, d_ukv, d_od_w_in = _original_grads(d_w0, d_wq, d_wk, d_wv, d_w1)
    grads = dict(meta=_meta_rows_sum(dh0), norm_g=jnp.concatenate([d_g0, d_g1], axis=0), final_g=d_final_g.reshape(d),
                 ev_w_in=d_ev_w_in, ev_q_norm_g=d_gq, ev_kv_norm_g=d_gkv, ev_w_uq=d_uq, ev_w_ukv=d_ukv, ev_w_out=d_wo0,
                 od_w_in=d_od_w_in, od_sinks=d_sinks[:, 0].reshape(1, SWA_HEADS), od_w_out=d_wo1)
    return loss, dh0[:, BLOCK:], grads


MESH = pl.DeviceIdType.MESH
ANY = pl.BlockSpec(memory_space=pl.ANY)


def _place():
    return lax.axis_index("x"), lax.axis_index("y"), lax.axis_index("c")


def _other_chips(x, y):
    return [(1 - x, y), (x, 1 - y), (1 - x, 1 - y)]


def _with_own_slot(slots, own):
    me = 2 * lax.axis_index("x") + lax.axis_index("y")
    return lax.dynamic_update_slice(slots, own[None], (me,) + (0,) * own.ndim)


def _gather_weights(packs, meta, name):
    n = len(packs)

    def body(*refs):
        ins, m_ref, outs, mo_ref = refs[:n], refs[n], refs[n + 1:2 * n + 1], refs[2 * n + 1]
        send_sems, recv_sems = refs[2 * n + 2:]
        x, y, c = _place()
        me, sib = 2 * x + y, (x, y, 1 - c)
        chips = _other_chips(x, y)

        def copy(k, src, dst, to):
            return pltpu.make_async_remote_copy(src_ref=src, dst_ref=dst, send_sem=send_sems.at[k], recv_sem=recv_sems.at[k], device_id=to,
                                                device_id_type=MESH)

        def half(i, chip, h):
            rows = packs[i].shape[0] // 2
            return outs[i].at[chip, pl.ds(h * rows, rows), :]

        def mine(i):
            rows = packs[i].shape[0] // 2
            return ins[i].at[pl.ds(c * rows, rows), :]

        sent = [copy(6 * i + k, mine(i), half(i, me, c), (px, py, c)) for i in range(n) for k, (px, py) in enumerate(chips)]
        sent += [copy(6 * n + k, m_ref, mo_ref.at[me], (px, py, c)) for k, (px, py) in enumerate(chips)]
        for cp in sent:
            cp.start()
        for i in range(n):
            for k, (px, py) in enumerate(chips):
                landed = half(i, 2 * px + py, c)
                copy(6 * i + k, mine(i), landed, (px, py, c)).wait_recv()
                fwd = copy(6 * i + 3 + k, landed, landed, sib)
                fwd.start()
                sent.append(fwd)
        for k, (px, py) in enumerate(chips):
            for i in range(n):
                other = half(i, 2 * px + py, 1 - c)
                copy(6 * i + 3 + k, other, other, sib).wait_recv()
            copy(6 * n + k, m_ref, mo_ref.at[2 * px + py], (px, py, c)).wait_recv()
        for cp in sent:
            cp.wait_send()

    nsem = 6 * n + 3
    res = pl.pallas_call(
        body, in_specs=[ANY] * (n + 1), out_specs=[ANY] * (n + 1),
        out_shape=[jax.ShapeDtypeStruct((N_CHIPS,) + a.shape, a.dtype) for a in list(packs) + [meta]],
        scratch_shapes=[pltpu.SemaphoreType.DMA((nsem,)), pltpu.SemaphoreType.DMA((nsem,))],
        name=name,
    )(*packs, meta)
    return [_with_own_slot(r, a) for r, a in zip(res[:n], packs)], _with_own_slot(res[n], meta)


def _grads_to_sibling(gs, name):
    n = len(gs)

    def body(*refs):
        ins, outs, send_sems, recv_sems = refs[:n], refs[n:2 * n], refs[2 * n], refs[2 * n + 1]
        x, y, c = _place()
        cps = []
        for i in range(n):
            rows = gs[i].shape[1] // 2
            cps.append(pltpu.make_async_remote_copy(src_ref=ins[i].at[:, pl.ds((1 - c) * rows, rows), :], dst_ref=outs[i],
                                                    send_sem=send_sems.at[i], recv_sem=recv_sems.at[i], device_id=(x, y, 1 - c),
                                                    device_id_type=MESH))
        for cp in cps:
            cp.start()
        for cp in cps:
            cp.wait()

    return pl.pallas_call(
        body, in_specs=[ANY] * n, out_specs=[ANY] * n,
        out_shape=[jax.ShapeDtypeStruct((g.shape[0], g.shape[1] // 2, g.shape[2]), g.dtype) for g in gs],
        scratch_shapes=[pltpu.SemaphoreType.DMA((n,)), pltpu.SemaphoreType.DMA((n,))],
        name=name,
    )(*gs)


def _share_halves(rs, name):
    n = len(rs)

    def body(*refs):
        ins, outs, send_sems, recv_sems = refs[:n], refs[n:2 * n], refs[2 * n], refs[2 * n + 1]
        x, y, c = _place()
        cps = [pltpu.make_async_remote_copy(src_ref=ins[i], dst_ref=outs[i], send_sem=send_sems.at[i], recv_sem=recv_sems.at[i],
                                            device_id=(x, y, 1 - c), device_id_type=MESH) for i in range(n)]
        for cp in cps:
            cp.start()
        for cp in cps:
            cp.wait()

    theirs = pl.pallas_call(
        body, in_specs=[ANY] * n, out_specs=[ANY] * n, out_shape=[jax.ShapeDtypeStruct(r.shape, r.dtype) for r in rs],
        scratch_shapes=[pltpu.SemaphoreType.DMA((n,)), pltpu.SemaphoreType.DMA((n,))],
        name=name,
    )(*rs)
    first = lax.axis_index("c") == 0
    return [jnp.where(first, jnp.concatenate([r, t], axis=0), jnp.concatenate([t, r], axis=0)) for r, t in zip(rs, theirs)]


def _chip_scatter(ss, name):
    n = len(ss)

    def body(*refs):
        ins, outs, send_sems, recv_sems = refs[:n], refs[n:2 * n], refs[2 * n], refs[2 * n + 1]
        x, y, c = _place()
        me = 2 * x + y
        chips = _other_chips(x, y)
        for i in range(n):
            for k, (px, py) in enumerate(chips):
                pltpu.make_async_remote_copy(src_ref=ins[i].at[2 * px + py], dst_ref=outs[i].at[me], send_sem=send_sems.at[3 * i + k],
                                             recv_sem=recv_sems.at[3 * i + k], device_id=(px, py, c), device_id_type=MESH).start()
        for i in range(n):
            for k, (px, py) in enumerate(chips):
                cp = pltpu.make_async_remote_copy(src_ref=ins[i].at[2 * px + py], dst_ref=outs[i].at[2 * px + py],
                                                  send_sem=send_sems.at[3 * i + k], recv_sem=recv_sems.at[3 * i + k],
                                                  device_id=(px, py, c), device_id_type=MESH)
                cp.wait_recv()
                cp.wait_send()

    parts = pl.pallas_call(
        body, in_specs=[ANY] * n, out_specs=[ANY] * n, out_shape=[jax.ShapeDtypeStruct(s.shape, s.dtype) for s in ss],
        scratch_shapes=[pltpu.SemaphoreType.DMA((3 * n,)), pltpu.SemaphoreType.DMA((3 * n,))],
        name=name,
    )(*ss)
    me = 2 * lax.axis_index("x") + lax.axis_index("y")
    return [_with_own_slot(p, lax.dynamic_index_in_dim(s, me, axis=0, keepdims=False)) for p, s in zip(parts, ss)]


HBM_SPACE = pltpu.MemorySpace.HBM


def _on_sequencer(name, collective_id, n_sems, body):
    @pl.kernel(mesh=plsc.ScalarSubcoreMesh(axis_name="sequencer", num_cores=1), name=name,
               scratch_types=(pltpu.SemaphoreType.DMA((n_sems,)), pltpu.SemaphoreType.DMA((n_sems,))),
               compiler_params=pltpu.CompilerParams(collective_id=collective_id))
    def launch(send_sems, recv_sems):
        body(send_sems, recv_sems)

    launch()


def _handshake(peers):
    barrier = pltpu.get_barrier_semaphore()
    for peer in peers:
        pl.semaphore_signal(barrier, inc=1, device_id=peer, device_id_type=MESH)
    pl.semaphore_wait(barrier, len(peers))


def _gather_on_sequencer(packs, name):
    n = len(packs)
    ins = [jax.new_ref(p, memory_space=HBM_SPACE) for p in packs]
    outs = [jax.empty_ref(jax.ShapeDtypeStruct((N_CHIPS,) + p.shape, p.dtype), memory_space=HBM_SPACE) for p in packs]

    def body(send_sems, recv_sems):
        x, y, c = _place()
        me, sib = 2 * x + y, (x, y, 1 - c)
        chips = _other_chips(x, y)
        _handshake([(px, py, c) for px, py in chips] + [sib])

        def copy(k, src, dst, to):
            return pltpu.make_async_remote_copy(src_ref=src, dst_ref=dst, send_sem=send_sems.at[k], recv_sem=recv_sems.at[k], device_id=to,
                                                device_id_type=MESH)

        def half(i, chip, h):
            rows = packs[i].shape[0] // 2
            return outs[i].at[chip, pl.ds(h * rows, rows), :]

        def mine(i):
            rows = packs[i].shape[0] // 2
            return ins[i].at[pl.ds(c * rows, rows), :]

        sent = [copy(6 * i + k, mine(i), half(i, me, c), (px, py, c)) for i in range(n) for k, (px, py) in enumerate(chips)]
        for cp in sent:
            cp.start()
        for i in range(n):
            for k, (px, py) in enumerate(chips):
                landed = half(i, 2 * px + py, c)
                copy(6 * i + k, mine(i), landed, (px, py, c)).wait_recv()
                fwd = copy(6 * i + 3 + k, landed, landed, sib)
                fwd.start()
                sent.append(fwd)
        for i in range(n):
            for k, (px, py) in enumerate(chips):
                other = half(i, 2 * px + py, 1 - c)
                copy(6 * i + 3 + k, other, other, sib).wait_recv()
        for cp in sent:
            cp.wait_send()

    _on_sequencer(name, 1, 6 * n, body)
    return [_with_own_slot(o[...], p) for o, p in zip(outs, packs)]


def _grads_to_sibling_on_sequencer(gs, name, collective_id):
    n = len(gs)
    ins = [jax.new_ref(g, memory_space=HBM_SPACE) for g in gs]
    outs = [jax.empty_ref(jax.ShapeDtypeStruct((g.shape[0], g.shape[1] // 2, g.shape[2]), g.dtype), memory_space=HBM_SPACE) for g in gs]

    def body(send_sems, recv_sems):
        x, y, c = _place()
        _handshake([(x, y, 1 - c)])
        cps = []
        for i in range(n):
            rows = gs[i].shape[1] // 2
            cps.append(pltpu.make_async_remote_copy(src_ref=ins[i].at[:, pl.ds((1 - c) * rows, rows), :], dst_ref=outs[i],
                                                    send_sem=send_sems.at[i], recv_sem=recv_sems.at[i], device_id=(x, y, 1 - c),
                                                    device_id_type=MESH))
        for cp in cps:
            cp.start()
        for cp in cps:
            cp.wait()

    _on_sequencer(name, collective_id, n, body)
    return [o[...] for o in outs]


def _chip_scatter_on_sequencer(ss, name, collective_id):
    n = len(ss)
    ins = [jax.new_ref(s, memory_space=HBM_SPACE) for s in ss]
    outs = [jax.empty_ref(jax.ShapeDtypeStruct(s.shape, s.dtype), memory_space=HBM_SPACE) for s in ss]

    def body(send_sems, recv_sems):
        x, y, c = _place()
        me = 2 * x + y
        chips = _other_chips(x, y)
        _handshake([(px, py, c) for px, py in chips])
        for i in range(n):
            for k, (px, py) in enumerate(chips):
                pltpu.make_async_remote_copy(src_ref=ins[i].at[2 * px + py], dst_ref=outs[i].at[me], send_sem=send_sems.at[3 * i + k],
                                             recv_sem=recv_sems.at[3 * i + k], device_id=(px, py, c), device_id_type=MESH).start()
        for i in range(n):
            for k, (px, py) in enumerate(chips):
                cp = pltpu.make_async_remote_copy(src_ref=ins[i].at[2 * px + py], dst_ref=outs[i].at[2 * px + py],
                                                  send_sem=send_sems.at[3 * i + k], recv_sem=recv_sems.at[3 * i + k],
                                                  device_id=(px, py, c), device_id_type=MESH)
                cp.wait_recv()
                cp.wait_send()

    _on_sequencer(name, collective_id, 3 * n, body)
    me = 2 * lax.axis_index("x") + lax.axis_index("y")
    return [_with_own_slot(o[...], lax.dynamic_index_in_dim(s, me, axis=0, keepdims=False)) for o, s in zip(outs, ss)]


def _all_reduce_small(v, name):
    shape = v.shape

    def body(v_ref, o_ref, slots, send_sems, recv_sems):
        x, y, c = _place()
        me = 4 * x + 2 * y + c
        slots[me] = v_ref[...]
        for r in range(1, N_DEV):
            peer = (x ^ (r >> 2), y ^ ((r >> 1) & 1), c ^ (r & 1))
            pltpu.make_async_remote_copy(src_ref=v_ref, dst_ref=slots.at[me], send_sem=send_sems.at[r - 1], recv_sem=recv_sems.at[r - 1],
                                         device_id=peer, device_id_type=MESH).start()
        for r in range(1, N_DEV):
            peer = (x ^ (r >> 2), y ^ ((r >> 1) & 1), c ^ (r & 1))
            cp = pltpu.make_async_remote_copy(src_ref=v_ref, dst_ref=slots.at[4 * peer[0] + 2 * peer[1] + peer[2]], send_sem=send_sems.at[r - 1],
                                              recv_sem=recv_sems.at[r - 1], device_id=peer, device_id_type=MESH)
            cp.wait_recv()
            cp.wait_send()
        acc = slots[0]
        for d in range(1, N_DEV):
            acc = acc + slots[d]
        o_ref[...] = acc

    vm = pl.BlockSpec(memory_space=pltpu.VMEM)
    return pl.pallas_call(
        body, in_specs=[vm], out_specs=vm, out_shape=jax.ShapeDtypeStruct(shape, F32),
        scratch_shapes=[pltpu.VMEM((N_DEV,) + shape, F32), pltpu.SemaphoreType.DMA((N_DEV - 1,)), pltpu.SemaphoreType.DMA((N_DEV - 1,))],
        name=name,
    )(v)


def _add_sibling(g, gsib, core, name):
    n, _, cdim = g.shape
    half = gsib.shape[1]
    tr = half // 2

    def body(core_ref, a_ref, b_ref, o_ref):
        o_ref[...] = (a_ref[...] + b_ref[...]).astype(o_ref.dtype)

    blk = pl.BlockSpec((1, tr, cdim), lambda j, i, core_ref: (j, i, 0))
    return pl.pallas_call(
        body,
        grid_spec=pltpu.PrefetchScalarGridSpec(
            num_scalar_prefetch=1, grid=(n, half // tr),
            in_specs=[pl.BlockSpec((1, tr, cdim), lambda j, i, core_ref: (j, core_ref[0] * (half // tr) + i, 0)), blk], out_specs=blk),
        out_shape=jax.ShapeDtypeStruct(gsib.shape, BF16), compiler_params=_params(("parallel", "parallel")), name=name,
    )(core, g, gsib)


def _sum_parts(parts, name):
    n, r, cdim = parts.shape
    tr = r // 2

    def body(p_ref, o_ref):
        acc = p_ref[0].astype(F32)
        for j in range(1, n):
            acc = acc + p_ref[j].astype(F32)
        o_ref[...] = acc

    return pl.pallas_call(
        body, grid=(r // tr,), in_specs=[pl.BlockSpec((n, tr, cdim), lambda i: (0, i, 0))],
        out_specs=pl.BlockSpec((tr, cdim), lambda i: (i, 0)), out_shape=jax.ShapeDtypeStruct((r, cdim), F32),
        compiler_params=_params(("parallel",)), name=name,
    )(parts)


def _adamw(parts, w, m, v, name):
    npart, r, cdim = parts.shape
    tr = r // 4 if r % 32 == 0 else r

    def body(p_ref, w_ref, m_ref, v_ref, g_ref, d_ref, nm_ref, nv_ref):
        g = p_ref[0]
        for j in range(1, npart):
            g = g + p_ref[j]
        m_new = ADAM_B1 * m_ref[...] + (1.0 - ADAM_B1) * g
        v_new = ADAM_B2 * v_ref[...] + (1.0 - ADAM_B2) * (g * g)
        m_hat = m_new / (1.0 - ADAM_B1 ** ADAM_STEP)
        v_hat = v_new / (1.0 - ADAM_B2 ** ADAM_STEP)
        g_ref[...] = g
        d_ref[...] = -ADAM_LR * (m_hat / (jnp.sqrt(v_hat) + ADAM_EPS) + ADAM_WD * w_ref[...])
        nm_ref[...] = m_new
        nv_ref[...] = v_new

    blk = pl.BlockSpec((tr, cdim), lambda i: (i, 0))
    shp = jax.ShapeDtypeStruct((r, cdim), F32)
    return pl.pallas_call(
        body, grid=(r // tr,), in_specs=[pl.BlockSpec((npart, tr, cdim), lambda i: (0, i, 0)), blk, blk, blk], out_specs=[blk] * 4,
        out_shape=[shp] * 4, compiler_params=_params(("parallel",)), name=name,
    )(parts, w, m, v)


BIG = ("ev_w_in", "ev_w_uq", "ev_w_ukv", "ev_w_out", "od_w_in", "od_w_out", "meta")
SMALL = ("norm_g", "final_g", "ev_q_norm_g", "ev_kv_norm_g", "od_sinks")
SMALL_SHAPE = (8, 512)
BY_ROWS = ("ev_w_out", "od_w_out")

EV_IN_SHARD, OD_IN_SHARD, UQ_SHARD = 2976 // N_CHIPS, 2304 // N_CHIPS, 768 // N_CHIPS


def _pack_big(a, lead=()):
    dtype = a["ev_w_in"].dtype
    z = lambda r, c: jnp.zeros(lead + (r, c), dtype)
    ax = len(lead)
    pad_to = lambda x, width: jnp.concatenate([x, z(x.shape[ax], width - x.shape[ax + 1])], axis=ax + 1)
    corner = jnp.concatenate([a["ev_w_ukv"], a["meta"], z(256 - MLA_KV_LORA - N_META, 256)], axis=ax)
    return (pad_to(a["ev_w_in"], 768), jnp.concatenate([pad_to(a["ev_w_uq"], 256), corner], axis=ax + 1),
            pad_to(a["od_w_in"], 640), jnp.concatenate([a["ev_w_out"], a["od_w_out"]], axis=ax + 1))


N_FIRST = 2


def _unpack_big(p_in0, p_lat, p_in1, p_out):
    return dict(ev_w_in=p_in0[..., :EV_IN_SHARD], od_w_in=p_in1[..., :OD_IN_SHARD], ev_w_out=p_out[..., :D_MODEL],
                od_w_out=p_out[..., D_MODEL:], ev_w_uq=p_lat[..., :UQ_SHARD], ev_w_ukv=p_lat[..., :MLA_KV_LORA, 256:],
                meta=p_lat[..., MLA_KV_LORA:MLA_KV_LORA + N_META, 256:])


def _chip_shards(full, by_rows):
    if by_rows:
        return full.reshape(N_CHIPS, full.shape[0] // N_CHIPS, full.shape[1])
    return full.reshape(full.shape[0], N_CHIPS, -1).transpose(1, 0, 2)


def _from_chip_shards(slots, by_rows):
    if by_rows:
        return slots.reshape(-1, slots.shape[2])
    return slots.transpose(1, 0, 2).reshape(slots.shape[1], -1)


def _pack_small(arrs, extra=None):
    flat = [a.reshape(-1) for a in arrs] + ([] if extra is None else [extra.reshape(-1)])
    used = sum(f.shape[0] for f in flat)
    return jnp.pad(jnp.concatenate(flat), (0, SMALL_SHAPE[0] * SMALL_SHAPE[1] - used)).reshape(SMALL_SHAPE)


def _unpack_small(p, shapes):
    flat, out, at = p.reshape(-1), [], 0
    for s in shapes:
        n = int(np.prod(s))
        out.append(flat[at:at + n].reshape(s))
        at += n
    return out, flat[at]


def kernel(x, meta, norm_g, final_g, ev_w_in, ev_q_norm_g, ev_kv_norm_g, ev_w_uq, ev_w_ukv, ev_w_out, od_w_in, od_sinks, od_w_out, loss_target, m_meta, m_norm_g, m_final_g, m_ev_w_in, m_ev_q_norm_g, m_ev_kv_norm_g, m_ev_w_uq, m_ev_w_ukv, m_ev_w_out, m_od_w_in, m_od_sinks, m_od_w_out, v_meta, v_norm_g, v_final_g, v_ev_w_in, v_ev_q_norm_g, v_ev_kv_norm_g, v_ev_w_uq, v_ev_w_ukv, v_ev_w_out, v_od_w_in, v_od_sinks, v_od_w_out):
    given = dict(locals())
    two_d = lambda a: a[0] if a.ndim == 3 else a
    packs = {k: _pack_big({n: two_d(given[k + n]) for n in BIG}) for k in ("", "m_", "v_")}

    wbf = [p.astype(BF16) for p in packs[""]]
    later = _gather_on_sequencer(wbf[N_FIRST:], "gather_later_weights")
    first, meta_all = _gather_weights(wbf[:N_FIRST], meta, "gather_weights")
    full = {n: _from_chip_shards(a, n in BY_ROWS) for n, a in _unpack_big(*first, *later).items()}
    meta_full = _from_chip_shards(meta_all, False)

    loss, grad_x, grads = _local_step(x, meta_full, norm_g, final_g, ev_q_norm_g, ev_kv_norm_g, od_sinks, loss_target,
                                      full["ev_w_in"], full["ev_w_uq"], full["ev_w_ukv"], full["ev_w_out"], full["od_w_in"], full["od_w_out"])

    gpacks = _pack_big({n: _chip_shards(grads[n], n in BY_ROWS) for n in BIG}, lead=(N_CHIPS,))
    core = lax.axis_index("c").astype(jnp.int32).reshape(1)

    def reduce_group(group, tag, to_sibling, scatter):
        sums = [_add_sibling(g, s, core, f"add_sibling_{tag}{i}") for i, (g, s) in enumerate(zip(group, to_sibling(group)))]
        parts = [_sum_parts(p, f"add_chips_{tag}{i}") for i, p in enumerate(scatter(sums))]
        return _share_halves(parts, "reduced_to_sibling_" + tag)

    reduced_later = reduce_group(gpacks[N_FIRST:], "later", lambda g: _grads_to_sibling_on_sequencer(g, "grads_to_sibling_later", 3),
                                 lambda s: _chip_scatter_on_sequencer(s, "grads_to_chips_later", 2))
    reduced_first = reduce_group(gpacks[:N_FIRST], "first", lambda g: _grads_to_sibling(g, "grads_to_sibling_first"),
                                 lambda s: _chip_scatter_on_sequencer(s, "grads_to_chips_first", 4))
    reduced = reduced_first + reduced_later
    updated = [_adamw(r[None], packs[""][i], packs["m_"][i], packs["v_"][i], f"adamw_matrices_{i}") for i, r in enumerate(reduced)]
    big_out = [{n: a.reshape(given[n].shape) for n, a in _unpack_big(*outs).items()} for outs in zip(*updated)]

    small_shapes = [given[n].shape for n in SMALL]
    ssum = _all_reduce_small(_pack_small([grads[n] for n in SMALL], loss[0, 0]), "reduce_vectors")
    small_out = _adamw(ssum[None], _pack_small([given[n] for n in SMALL]), _pack_small([given["m_" + n] for n in SMALL]),
                       _pack_small([given["v_" + n] for n in SMALL]), "adamw_vectors")
    total_loss = ssum.reshape(-1)[sum(int(np.prod(s)) for s in small_shapes)]
    small_out = [_unpack_small(o, small_shapes)[0] for o in small_out]

    names = ("meta", "norm_g", "final_g", "ev_w_in", "ev_q_norm_g", "ev_kv_norm_g", "ev_w_uq", "ev_w_ukv", "ev_w_out", "od_w_in", "od_sinks",
             "od_w_out")
    outs = [total_loss, grad_x]
    for kind in range(4):
        for n in names:
            outs.append(big_out[kind][n] if n in BIG else small_out[kind][SMALL.index(n)])
    return tuple(outs)
```

```python
import functools
import math

import numpy as np
import jax
import jax.numpy as jnp
from jax import lax
from jax.experimental import pallas as pl
from jax.experimental.pallas import tpu as pltpu
from jax.experimental.pallas import tpu_sc as plsc

F32 = jnp.float32
BF16 = jnp.bfloat16

D_MODEL = 1024
BLOCK = 128
N_META = 16
N_PAD = BLOCK - N_META
NORM_EPS = 1e-6
NEG = -1e30
HEAD = 64
SB_HEADS = 8
MLA_HEADS = 8
MLA_Q_LORA = 256
MLA_KV_LORA = 128
MLA_NOPE = 64
MLA_ROPE = 32
ROPE_BASE = 10000.0
SWA_HEADS = 16
SWA_KV_HEADS = 2
SWA_WINDOW = 128
N_CHIPS = 4
N_DEV = 8

ADAM_LR = 0.001
ADAM_B1 = 0.9
ADAM_B2 = 0.999
ADAM_EPS = 1e-08
ADAM_WD = 0.01
ADAM_STEP = 10

VMEM_LIMIT = 48 * 1024 * 1024

EV_G, EV_Q, EV_K, EV_V, EV_CQ, EV_CKV, EV_KR, EV_N = 0, 1024, 1536, 2048, 2560, 2816, 2944, 3072
OD_G, OD_Q, OD_K, OD_V, OD_N = 0, 1024, 2048, 2176, 2304


def _params(sem=None):
    return pltpu.CompilerParams(dimension_semantics=sem, vmem_limit_bytes=VMEM_LIMIT)


def _row_tile(m):
    return 256 if m % 256 == 0 else 128


def _matmul_rows(m):
    for c in (1088, 1024, 768, 640, 512, 384, 256):
        if m % c == 0:
            return c
    return 128


def _dot(a, b):
    return jnp.dot(a.astype(BF16), b.astype(BF16), preferred_element_type=F32)


def _dot_nt(a, b):
    return lax.dot_general(a.astype(BF16), b.astype(BF16), (((1,), (1,)), ((), ())), preferred_element_type=F32)


def _dot_tn(a, b):
    return lax.dot_general(a.astype(BF16), b.astype(BF16), (((0,), (0,)), ((), ())), preferred_element_type=F32)


def _rms_fwd(h, g, name):
    t, d = h.shape
    tm = _row_tile(t)

    def body(h_ref, g_ref, o_ref):
        x = h_ref[...]
        r = lax.rsqrt(jnp.mean(x * x, axis=-1, keepdims=True) + NORM_EPS)
        o_ref[...] = ((x * r) * g_ref[...]).astype(o_ref.dtype)

    return pl.pallas_call(
        body, grid=(t // tm,),
        in_specs=[pl.BlockSpec((tm, d), lambda i: (i, 0)), pl.BlockSpec((1, d), lambda i: (0, 0))],
        out_specs=pl.BlockSpec((tm, d), lambda i: (i, 0)),
        out_shape=jax.ShapeDtypeStruct((t, d), BF16), compiler_params=_params(("parallel",)), name=name,
    )(h, g)


def _rms_bwd(h, g, dy, dres, name):
    t, d = h.shape
    tm = _row_tile(t)

    def body(h_ref, g_ref, dy_ref, dres_ref, dh_ref, dg_ref):
        @pl.when(pl.program_id(0) == 0)
        def _():
            dg_ref[...] = jnp.zeros_like(dg_ref)

        x = h_ref[...]
        r = lax.rsqrt(jnp.mean(x * x, axis=-1, keepdims=True) + NORM_EPS)
        xr = x * r
        dy_ = dy_ref[...]
        u = dy_ * g_ref[...]
        dh_ref[...] = dres_ref[...] + r * (u - xr * jnp.mean(u * xr, axis=-1, keepdims=True))
        dg_ref[...] += jnp.sum(dy_ * xr, axis=0, keepdims=True)

    row = pl.BlockSpec((tm, d), lambda i: (i, 0))
    vec = pl.BlockSpec((1, d), lambda i: (0, 0))
    return pl.pallas_call(
        body, grid=(t // tm,), in_specs=[row, vec, row, row], out_specs=[row, vec],
        out_shape=[jax.ShapeDtypeStruct((t, d), F32), jax.ShapeDtypeStruct((1, d), F32)],
        compiler_params=_params(("arbitrary",)), name=name,
    )(h, g, dy, dres)


def _col_tile(n):
    for c in (1024, 768, 640, 512, 384, 256, 128):
        if n % c == 0:
            return c
    return n


def _mm(a, w, name, res=None, out_dtype=F32, a_cols=None):
    m = a.shape[0]
    k, n = w.shape
    a_blk = 0 if a_cols is None else a_cols[0] // k
    assert a_cols is None or (a_cols[1] == k and a_cols[0] % k == 0)
    tm, tn = _matmul_rows(m), _col_tile(n)

    def body(*refs):
        if res is None:
            a_ref, w_ref, o_ref = refs
            acc = _dot(a_ref[...], w_ref[...])
        else:
            a_ref, w_ref, r_ref, o_ref = refs
            acc = r_ref[...] + _dot(a_ref[...], w_ref[...])
        o_ref[...] = acc.astype(o_ref.dtype)

    in_specs = [pl.BlockSpec((tm, k), lambda j, i: (i, a_blk)), pl.BlockSpec((k, tn), lambda j, i: (0, j))]
    args = [a, w]
    if res is not None:
        in_specs.append(pl.BlockSpec((tm, tn), lambda j, i: (i, j)))
        args.append(res)
    return pl.pallas_call(
        body, grid=(n // tn, m // tm), in_specs=in_specs, out_specs=pl.BlockSpec((tm, tn), lambda j, i: (i, j)),
        out_shape=jax.ShapeDtypeStruct((m, n), out_dtype), compiler_params=_params(("parallel", "parallel")), name=name,
    )(*args)


def _mm_nt(a, w, name):
    m, n = a.shape
    k = w.shape[0]
    tm, tk = _matmul_rows(m), _col_tile(k)

    def body(a_ref, w_ref, o_ref):
        o_ref[...] = _dot_nt(a_ref[...], w_ref[...])

    return pl.pallas_call(
        body, grid=(k // tk, m // tm),
        in_specs=[pl.BlockSpec((tm, n), lambda j, i: (i, 0)), pl.BlockSpec((tk, n), lambda j, i: (j, 0))],
        out_specs=pl.BlockSpec((tm, tk), lambda j, i: (i, j)),
        out_shape=jax.ShapeDtypeStruct((m, k), F32), compiler_params=_params(("parallel", "parallel")), name=name,
    )(a, w)


def _mm_tn(x, dy, name):
    m, k = x.shape
    n = dy.shape[1]
    tm, tn = _matmul_rows(m), _col_tile(n)

    def body(x_ref, dy_ref, o_ref):
        @pl.when(pl.program_id(1) == 0)
        def _():
            o_ref[...] = jnp.zeros_like(o_ref)

        o_ref[...] += _dot_tn(x_ref[...], dy_ref[...])

    return pl.pallas_call(
        body, grid=(n // tn, m // tm),
        in_specs=[pl.BlockSpec((tm, k), lambda j, i: (i, 0)), pl.BlockSpec((tm, tn), lambda j, i: (i, j))],
        out_specs=pl.BlockSpec((k, tn), lambda j, i: (0, j)),
        out_shape=jax.ShapeDtypeStruct((k, n), F32), compiler_params=_params(("parallel", "arbitrary")), name=name,
    )(x, dy)


def _silu_parts(g):
    s = 1.0 / (1.0 + jnp.exp(-g))
    return g * s, s * (1.0 + g * (1.0 - s))


def _gate_fwd(o_parts, proj, name):
    t = proj.shape[0]
    tm = _row_tile(t)
    w = D_MODEL // len(o_parts)

    def body(*refs):
        g_ref, o_ref = refs[-2], refs[-1]
        for p, r in enumerate(refs[:-2]):
            sil, _ = _silu_parts(g_ref[:, p * w:(p + 1) * w])
            o_ref[:, p * w:(p + 1) * w] = (r[...].astype(F32) * sil).astype(o_ref.dtype)

    return pl.pallas_call(
        body, grid=(t // tm,),
        in_specs=[pl.BlockSpec((tm, w), lambda i: (i, 0)) for _ in o_parts] + [pl.BlockSpec((tm, D_MODEL), lambda i: (i, 0))],
        out_specs=pl.BlockSpec((tm, D_MODEL), lambda i: (i, 0)),
        out_shape=jax.ShapeDtypeStruct((t, D_MODEL), BF16), compiler_params=_params(("parallel",)), name=name,
    )(*o_parts, proj)


def _gate_bwd(dao, o_parts, proj, name):
    t = proj.shape[0]
    tm = _row_tile(t)
    np_ = len(o_parts)
    w = D_MODEL // np_

    def body(*refs):
        dao_ref, g_ref = refs[0], refs[1 + np_]
        do_refs, dg_ref = refs[2 + np_:2 + 2 * np_], refs[-1]
        for p in range(np_):
            sl = slice(p * w, (p + 1) * w)
            sil, dsil = _silu_parts(g_ref[:, sl])
            da = dao_ref[:, sl]
            do_refs[p][...] = da * sil
            dg_ref[:, sl] = (da * refs[1 + p][...].astype(F32) * dsil).astype(dg_ref.dtype)

    full = pl.BlockSpec((tm, D_MODEL), lambda i: (i, 0))
    part = pl.BlockSpec((tm, w), lambda i: (i, 0))
    outs = pl.pallas_call(
        body, grid=(t // tm,), in_specs=[full] + [part] * np_ + [full], out_specs=[part] * np_ + [full],
        out_shape=[jax.ShapeDtypeStruct((t, w), F32)] * np_ + [jax.ShapeDtypeStruct((t, D_MODEL), BF16)],
        compiler_params=_params(("parallel",)), name=name,
    )(dao, *o_parts, proj)
    return outs[:np_], outs[np_]


def _loss_head(h2, gf, target, b, lp):
    d = h2.shape[1]
    nb = lp // BLOCK
    h3 = h2.reshape(b, lp, d)

    def body(h_ref, g_ref, t_ref, dh_ref, dg_ref, loss_ref):
        first = (pl.program_id(0) == 0) & (pl.program_id(1) == 0)

        @pl.when(first)
        def _():
            dg_ref[...] = jnp.zeros_like(dg_ref)
            loss_ref[...] = jnp.zeros_like(loss_ref)

        @pl.when(pl.program_id(1) == 0)
        def _():
            dh_ref[...] = jnp.zeros_like(dh_ref)

        @pl.when(pl.program_id(1) > 0)
        def _():
            x = h_ref[0]
            r = lax.rsqrt(jnp.mean(x * x, axis=-1, keepdims=True) + NORM_EPS)
            xr = x * r
            g = g_ref[...]
            diff = xr * g - t_ref[0]
            loss_ref[...] += 0.5 * jnp.sum(jnp.mean(diff * diff, axis=-1, keepdims=True))
            dy = diff * (1.0 / d)
            u = dy * g
            dh_ref[0] = r * (u - xr * jnp.mean(u * xr, axis=-1, keepdims=True))
            dg_ref[...] += jnp.sum(dy * xr, axis=0, keepdims=True)

    blk = pl.BlockSpec((1, BLOCK, d), lambda bi, n: (bi, n, 0))
    dh, dg, loss = pl.pallas_call(
        body, grid=(b, nb),
        in_specs=[blk, pl.BlockSpec((1, d), lambda bi, n: (0, 0)),
                  pl.BlockSpec((1, BLOCK, d), lambda bi, n: (bi, jnp.maximum(n - 1, 0), 0))],
        out_specs=[blk, pl.BlockSpec((1, d), lambda bi, n: (0, 0)), pl.BlockSpec((8, 128), lambda bi, n: (0, 0))],
        out_shape=[jax.ShapeDtypeStruct((b, lp, d), F32), jax.ShapeDtypeStruct((1, d), F32), jax.ShapeDtypeStruct((8, 128), F32)],
        compiler_params=_params(("arbitrary", "arbitrary")), name="loss_head",
    )(h3, gf, target)
    return dh.reshape(b * lp, d), dg, loss


def _iota2(shape, dim):
    return lax.broadcasted_iota(jnp.int32, shape, dim)


KEYS = 256


def _lo_lanes():
    return _iota2((1, BLOCK), 1) < HEAD


def _halves(x, lo):
    zero = jnp.zeros_like(x)
    return jnp.where(lo, x, zero), jnp.where(lo, zero, x)


def _rows_of_pair(a, b):
    return jnp.where(_iota2((BLOCK, 1), 0) < HEAD, a, b)


def _split_rows_t(x):
    xt = x.T
    first = _iota2(xt.shape, 0) < HEAD
    zero = jnp.zeros_like(xt)
    return jnp.concatenate([jnp.where(first, xt, zero), jnp.where(first, zero, xt)], axis=1).astype(BF16)


def _key_chunk(c, lp, t_idx, strict, key_axis):
    first = c * KEYS
    s0 = pl.multiple_of(jnp.minimum(first, lp - KEYS), BLOCK)
    s_idx = s0 + _iota2(t_idx.shape, key_axis)
    seen = (s_idx < t_idx) if strict else (s_idx <= t_idx)
    return s0, seen & (s_idx >= jnp.maximum(first, N_PAD))


def _split_dot(x, tri):
    hi = x.astype(BF16)
    lo = (x - hi.astype(F32)).astype(BF16)
    return jnp.dot(hi, tri, preferred_element_type=F32) + jnp.dot(lo, tri, preferred_element_type=F32)


def _stack_halves(x, lo):
    a, b = _halves(x, lo)
    return jnp.concatenate([a, b], axis=0)


def _pair(a, b, lo):
    return jnp.where(lo, a, b)


def _chunk_starts(lp):
    return [min(c * KEYS, lp - KEYS) for c in range(-(-lp // KEYS))]


def _put_rows(ref, r0, bq, a, b):
    for t in range(bq // BLOCK):
        part = slice(t * BLOCK, (t + 1) * BLOCK)
        ref[0, 0, r0 // BLOCK + t] = jnp.concatenate([a[:, part], b[:, part], jnp.zeros((6, BLOCK), F32)], axis=0)


def _get_rows(ref, r0, bq):
    return [jnp.concatenate([ref[0, 0, r0 // BLOCK + t, h:h + 1, :] for t in range(bq // BLOCK)], axis=1) for h in range(2)]


def _n_chunks(i):
    return (i + 2) // 2


QROWS = 512


def _for_query_tiles(nb, tile):
    per = QROWS // BLOCK

    def step(j, _):
        tile(pl.multiple_of(j * QROWS, QROWS), QROWS, (j + 1) * (QROWS // KEYS))
        return 0

    lax.fori_loop(0, nb // per, step, 0)
    for i in range(nb - nb % per, nb):
        tile(i * BLOCK, BLOCK, _n_chunks(i))


def _walk_chunks(r0, n, chunk, carry, leftwards=False):
    diag = jnp.maximum(r0 // KEYS, 1)

    def span(first, last, masked, carry):
        def step(t, cr):
            return chunk(last - 1 - t if leftwards else first + t, cr, masked)
        return lax.fori_loop(0, last - first, step, carry)

    spans = [(0, 1, True), (1, diag, False), (diag, n, True)]
    for first, last, masked in (reversed(spans) if leftwards else spans):
        carry = span(first, last, masked, carry)
    return carry


def _where(valid, x, other):
    return x if valid is None else jnp.where(valid, x, other)


HEAD_SCALE = HEAD ** -0.5
assert math.frexp(HEAD_SCALE)[0] == 0.5


def _sb_scores(q_h, k, valid, after):
    z = _dot_nt(q_h, k)
    lb = jnp.minimum(z, 0.0) - jnp.log(1.0 + jnp.exp(-jnp.abs(z)))
    l1m_all = lb - z
    l1m = _where(valid, l1m_all, 0.0)
    return lb, l1m_all, l1m, _split_dot(l1m, after)


def _pair_stat_spec(nb):
    return pl.BlockSpec((1, 1, nb, 8, BLOCK), lambda bi, hp: (bi, hp, 0, 0, 0))


def _sb_fwd(proj3, name):
    b, lp, _ = proj3.shape
    nb = lp // BLOCK
    npair = SB_HEADS // 2

    def body(q_ref, k_ref, v_ref, o_ref, tot_ref):
        lo = _lo_lanes()
        after = (_iota2((KEYS, KEYS), 0) > _iota2((KEYS, KEYS), 1)).astype(BF16)

        def qtile(r0, bq, n):
            qs = _halves((q_ref[0, pl.ds(r0, bq), :] * HEAD_SCALE).astype(BF16), lo)
            t_idx = r0 + _iota2((bq, KEYS), 0)

            def kchunk(c, carry, masked):
                cs, acc = carry[:2], carry[2]
                s0, valid = _key_chunk(c, lp, t_idx, True, 1)
                valid = valid if masked else None
                k = k_ref[0, pl.ds(s0, KEYS), :].astype(BF16)
                a_s, new = [], []
                for h in range(2):
                    lb, _, l1m, suf = _sb_scores(qs[h], k, valid, after)
                    a_s.append(_where(valid, jnp.exp(lb + suf + cs[h]), 0.0).astype(BF16))
                    new.append(cs[h] + jnp.sum(l1m, axis=1, keepdims=True))
                v_bd = _stack_halves(v_ref[0, pl.ds(s0, KEYS), :].astype(BF16), lo)
                return (*new, acc + jnp.dot(jnp.concatenate(a_s, axis=1), v_bd, preferred_element_type=F32))

            zero = jnp.zeros((bq, 1), F32)
            c_a, c_b, acc = _walk_chunks(r0, n, kchunk, (zero, zero, jnp.zeros((bq, BLOCK), F32)), leftwards=True)
            o_ref[0, pl.ds(r0, bq), :] = acc
            tot_ref[0, pl.ds(r0, bq), :] = jnp.broadcast_to(_pair(c_a, c_b, lo), (bq, BLOCK))

        _for_query_tiles(nb, qtile)

    def col(first):
        return pl.BlockSpec((1, lp, 2 * HEAD), lambda bi, hp: (bi, 0, first // (2 * HEAD) + hp))

    shp = jax.ShapeDtypeStruct((b, lp, SB_HEADS * HEAD), F32)
    return pl.pallas_call(
        body, grid=(b, npair), in_specs=[col(EV_Q), col(EV_K), col(EV_V)], out_specs=[col(0), col(0)], out_shape=[shp, shp],
        compiler_params=_params(("parallel", "parallel")), name=name,
    )(proj3, proj3, proj3)


def _sb_bwd(proj3, tot, do, name):
    b, lp, _ = proj3.shape
    nb = lp // BLOCK
    npair = SB_HEADS // 2

    def body(q_ref, k_ref, v_ref, tot_ref, do_ref, dq_ref, dk_ref, dv_ref):
        lo = _lo_lanes()
        after = (_iota2((KEYS, KEYS), 0) > _iota2((KEYS, KEYS), 1)).astype(BF16)
        before = (_iota2((KEYS, KEYS), 0) < _iota2((KEYS, KEYS), 1)).astype(BF16)
        dk_ref[...] = jnp.zeros_like(dk_ref)
        dv_ref[...] = jnp.zeros_like(dv_ref)

        def qtile(r0, bq, n):
            rows = pl.ds(r0, bq)
            qs = _halves((q_ref[0, rows, :] * HEAD_SCALE).astype(BF16), lo)
            dos = _halves(do_ref[0, rows, :].astype(BF16), lo)
            tot_i = tot_ref[0, rows, :]
            tots = (tot_i[:, 0:1], tot_i[:, HEAD:HEAD + 1])
            q_st, do_st = jnp.concatenate(qs, axis=0), jnp.concatenate(dos, axis=0)
            t_idx = r0 + _iota2((bq, KEYS), 0)

            def kchunk(c, carry, masked):
                s0, valid = _key_chunk(c, lp, t_idx, True, 1)
                valid = valid if masked else None
                keys = pl.ds(s0, KEYS)
                k = k_ref[0, keys, :].astype(BF16)
                v = v_ref[0, keys, :].astype(BF16)
                a_s, dzs, new = [], [], []
                for h in range(2):
                    left, pre = carry[2 * h], carry[2 * h + 1]
                    lb, l1m_all, l1m, suf = _sb_scores(qs[h], k, valid, after)
                    here = jnp.sum(l1m, axis=1, keepdims=True)
                    a = _where(valid, jnp.exp(lb + suf + (tots[h] - left - here)), 0.0)
                    w = a * _dot_nt(dos[h], v)
                    dz = _where(valid, w * jnp.exp(l1m_all) - (pre + _split_dot(w, before)) * jnp.exp(lb), 0.0)
                    new += [left + here, pre + jnp.sum(w, axis=1, keepdims=True)]
                    a_s.append(a.astype(BF16))
                    dzs.append(dz.astype(BF16))
                dk_ref[0, keys, :] += _dot_tn(jnp.concatenate(dzs, axis=0), q_st)
                dv_ref[0, keys, :] += _dot_tn(jnp.concatenate(a_s, axis=0), do_st)
                dq = carry[4] + jnp.dot(jnp.concatenate(dzs, axis=1), _stack_halves(k, lo), preferred_element_type=F32)
                return (*new, dq)

            zero = jnp.zeros((bq, 1), F32)
            out = _walk_chunks(r0, n, kchunk, (zero, zero, zero, zero, jnp.zeros((bq, BLOCK), F32)))
            dq_ref[0, rows, :] = out[4] * HEAD_SCALE

        _for_query_tiles(nb, qtile)

    def col(first):
        return pl.BlockSpec((1, lp, 2 * HEAD), lambda bi, hp: (bi, 0, first // (2 * HEAD) + hp))

    shp = jax.ShapeDtypeStruct((b, lp, SB_HEADS * HEAD), F32)
    return pl.pallas_call(
        body, grid=(b, npair), in_specs=[col(EV_Q), col(EV_K), col(EV_V), col(0), col(0)], out_specs=[col(0)] * 3, out_shape=[shp] * 3,
        compiler_params=_params(("parallel", "parallel")), name=name,
    )(proj3, proj3, proj3, tot, do)


def _rope_tables(lp):
    half = MLA_ROPE // 2
    pos = (np.arange(lp) - N_PAD).astype(np.float32)
    inv = jnp.asarray(ROPE_BASE, F32) ** (-jnp.arange(half, dtype=F32) / half)
    ang = jnp.asarray(pos)[:, None] * inv[None, :]
    cos, sin = jnp.cos(ang), jnp.sin(ang)
    zeros = lambda n: jnp.zeros((lp, n), F32)
    c = jnp.concatenate([jnp.ones((lp, MLA_NOPE), F32), cos, cos, zeros(32)], axis=1)
    s1 = jnp.concatenate([zeros(MLA_NOPE), -sin, zeros(half), zeros(32)], axis=1)
    s2 = jnp.concatenate([zeros(MLA_NOPE), zeros(half), sin, zeros(32)], axis=1)
    return c, s1, s2


def _rope(x, c, s1, s2):
    half = MLA_ROPE // 2
    return x * c + pltpu.roll(x, BLOCK - half, 1) * s1 + pltpu.roll(x, half, 1) * s2


def _rope_t(dy, c, s1, s2):
    half = MLA_ROPE // 2
    return dy * c + pltpu.roll(dy * s1, half, 1) + pltpu.roll(dy * s2, BLOCK - half, 1)


def _rms_rows(x, g):
    r = lax.rsqrt(jnp.mean(x * x, axis=-1, keepdims=True) + NORM_EPS)
    return x * r, r


def _mla_prep_fwd(proj3, gq, gkv, wq, wk, wv, tabs, name):
    b, lp, _ = proj3.shape
    nb = lp // BLOCK
    hw = MLA_HEADS * BLOCK

    def body(cq_ref, ckv_ref, kr_ref, gq_ref, gkv_ref, wq_ref, wk_ref, wv_ref, c_ref, s1_ref, s2_ref, qf_ref, kf_ref, v_ref):
        c, s1, s2 = c_ref[...], s1_ref[...], s2_ref[...]
        xq, _ = _rms_rows(cq_ref[0], None)
        qh = _dot(xq * gq_ref[...], wq_ref[...])
        xk, _ = _rms_rows(ckv_ref[0], None)
        ckv_n = xk * gkv_ref[...]
        kv = _dot(ckv_n, wk_ref[...])
        v_ref[0] = _dot(ckv_n, wv_ref[...]).astype(v_ref.dtype)
        kr = _rope(kr_ref[0], c, s1, s2)
        for h in range(MLA_HEADS):
            ls = slice(h * BLOCK, (h + 1) * BLOCK)
            qf_ref[0, :, ls] = _rope(qh[:, ls], c, s1, s2).astype(qf_ref.dtype)
            kf_ref[0, :, ls] = (kv[:, ls] + kr).astype(kf_ref.dtype)

    def col(first, width):
        return pl.BlockSpec((1, BLOCK, width), lambda bi, n: (bi, n, first // width))

    def whole(a):
        return pl.BlockSpec(a.shape, lambda bi, n: (0,) * a.ndim)

    tab = pl.BlockSpec((BLOCK, BLOCK), lambda bi, n: (n, 0))
    return pl.pallas_call(
        body, grid=(b, nb),
        in_specs=[col(EV_CQ, MLA_Q_LORA), col(EV_CKV, MLA_KV_LORA), col(EV_KR, BLOCK), whole(gq), whole(gkv), whole(wq), whole(wk),
                  whole(wv), tab, tab, tab],
        out_specs=[col(0, hw), col(0, hw), col(0, MLA_HEADS * HEAD)],
        out_shape=[jax.ShapeDtypeStruct((b, lp, hw), BF16), jax.ShapeDtypeStruct((b, lp, hw), BF16),
                   jax.ShapeDtypeStruct((b, lp, MLA_HEADS * HEAD), BF16)],
        compiler_params=_params(("parallel", "parallel")), name=name,
    )(proj3, proj3, proj3, gq, gkv, wq, wk, wv, *tabs)


def _mla_prep_bwd(proj3, gq, gkv, wq, wk, wv, tabs, dqf, dkf, dv, name):
    b, lp, _ = proj3.shape
    nb = lp // BLOCK
    hw = MLA_HEADS * BLOCK

    def body(cq_ref, ckv_ref, gq_ref, gkv_ref, wq_ref, wk_ref, wv_ref, c_ref, s1_ref, s2_ref, dqf_ref, dkf_ref, dv_ref,
             dcq_ref, dckv_ref, dkr_ref, dwq_ref, dwk_ref, dwv_ref, dgq_ref, dgkv_ref, dqh):
        @pl.when((pl.program_id(0) == 0) & (pl.program_id(1) == 0))
        def _():
            for r in (dwq_ref, dwk_ref, dwv_ref, dgq_ref, dgkv_ref):
                r[...] = jnp.zeros_like(r)

        c, s1, s2 = c_ref[...], s1_ref[...], s2_ref[...]
        dkr = jnp.zeros((BLOCK, BLOCK), F32)
        for h in range(MLA_HEADS):
            ls = slice(h * BLOCK, (h + 1) * BLOCK)
            dqh[:, ls] = _rope_t(dqf_ref[0, :, ls].astype(F32), c, s1, s2).astype(dqh.dtype)
            dkr = dkr + dkf_ref[0, :, ls].astype(F32)
        dkr_ref[0] = _rope_t(dkr, c, s1, s2).astype(dkr_ref.dtype)

        def norm_bwd(x, g, dy, dg_ref):
            xr, r = _rms_rows(x, None)
            u = dy * g
            dg_ref[...] += jnp.sum(dy * xr, axis=0, keepdims=True)
            return r * (u - xr * jnp.mean(u * xr, axis=-1, keepdims=True))

        xq, _ = _rms_rows(cq_ref[0], None)
        cq_n = xq * gq_ref[...]
        dwq_ref[...] += _dot_tn(cq_n, dqh[...])
        dcq_ref[0] = norm_bwd(cq_ref[0], gq_ref[...], _dot_nt(dqh[...], wq_ref[...]), dgq_ref).astype(dcq_ref.dtype)
        xk, _ = _rms_rows(ckv_ref[0], None)
        ckv_n = xk * gkv_ref[...]
        dkf_, dv_ = dkf_ref[0], dv_ref[0]
        dwk_ref[...] += _dot_tn(ckv_n, dkf_)
        dwv_ref[...] += _dot_tn(ckv_n, dv_)
        dckv_n = _dot_nt(dkf_, wk_ref[...]) + _dot_nt(dv_, wv_ref[...])
        dckv_ref[0] = norm_bwd(ckv_ref[0], gkv_ref[...], dckv_n, dgkv_ref).astype(dckv_ref.dtype)

    def col(first, width):
        return pl.BlockSpec((1, BLOCK, width), lambda bi, n: (bi, n, first // width))

    def whole(a):
        return pl.BlockSpec(a.shape, lambda bi, n: (0,) * len(a.shape))

    tab = pl.BlockSpec((BLOCK, BLOCK), lambda bi, n: (n, 0))
    acc_shapes = [jax.ShapeDtypeStruct(a.shape, F32) for a in (wq, wk, wv, gq, gkv)]
    return pl.pallas_call(
        body, grid=(b, nb),
        in_specs=[col(EV_CQ, MLA_Q_LORA), col(EV_CKV, MLA_KV_LORA), whole(gq), whole(gkv), whole(wq), whole(wk), whole(wv), tab, tab, tab,
                  col(0, hw), col(0, hw), col(0, MLA_HEADS * HEAD)],
        out_specs=[col(0, MLA_Q_LORA), col(0, MLA_KV_LORA), col(0, BLOCK)] + [whole(a) for a in acc_shapes],
        out_shape=[jax.ShapeDtypeStruct((b, lp, MLA_Q_LORA), BF16), jax.ShapeDtypeStruct((b, lp, MLA_KV_LORA), BF16),
                   jax.ShapeDtypeStruct((b, lp, BLOCK), BF16)] + acc_shapes,
        scratch_shapes=[pltpu.VMEM((BLOCK, hw), BF16)],
        compiler_params=_params(("arbitrary", "arbitrary")), name=name,
    )(proj3, proj3, gq, gkv, wq, wk, wv, *tabs, dqf, dkf, dv)


def _mla_fwd(qf, kf, v, name):
    b, lp, _ = qf.shape
    nb = lp // BLOCK
    npair = MLA_HEADS // 2
    scale = (MLA_NOPE + MLA_ROPE) ** -0.5
    starts = _chunk_starts(lp)

    def body(q_ref, k_ref, v_ref, o_ref, lse_ref, vt_ref):
        for c, s0 in enumerate(starts):
            vt_ref[c] = _split_rows_t(v_ref[0, s0:s0 + KEYS, :].astype(F32))

        def qtile(r0, bq, n):
            qs = [q_ref[0, pl.ds(r0, bq), h * BLOCK:(h + 1) * BLOCK] for h in range(2)]
            t_idx = r0 + _iota2((KEYS, bq), 1)

            def kchunk(c, carry, masked):
                stats, acc = carry[:4], carry[4]
                s0, valid = _key_chunk(c, lp, t_idx, False, 0)
                valid = valid if masked else None
                ps, new, alphas = [], [], []
                for h in range(2):
                    m, l = stats[2 * h], stats[2 * h + 1]
                    s = _where(valid, _dot_nt(k_ref[0, pl.ds(s0, KEYS), h * BLOCK:(h + 1) * BLOCK], qs[h]) * scale, NEG)
                    m_new = jnp.maximum(m, jnp.max(s, axis=0, keepdims=True))
                    p = _where(valid, jnp.exp(s - m_new), 0.0)
                    alpha = jnp.exp(m - m_new)
                    new += [m_new, alpha * l + jnp.sum(p, axis=0, keepdims=True)]
                    alphas.append(alpha)
                    ps.append(p.astype(BF16))
                pv = jnp.dot(vt_ref[c], jnp.concatenate(ps, axis=0), preferred_element_type=F32)
                return (*new, _rows_of_pair(alphas[0], alphas[1]) * acc + pv)

            neg, zero = jnp.full((1, bq), NEG, F32), jnp.zeros((1, bq), F32)
            m_a, l_a, m_b, l_b, acc = _walk_chunks(r0, n, kchunk, (neg, zero, neg, zero, jnp.zeros((BLOCK, bq), F32)))
            safe = [jnp.where(l > 0.0, l, 1.0) for l in (l_a, l_b)]
            o_ref[0, pl.ds(r0, bq), :] = (acc / _rows_of_pair(safe[0], safe[1])).T
            lse = [jnp.where(l > 0.0, m + jnp.log(sf), 0.0) for m, l, sf in ((m_a, l_a, safe[0]), (m_b, l_b, safe[1]))]
            _put_rows(lse_ref, r0, bq, lse[0], lse[1])

        _for_query_tiles(nb, qtile)

    wide = pl.BlockSpec((1, lp, 2 * BLOCK), lambda bi, hp: (bi, 0, hp))
    thin = pl.BlockSpec((1, lp, 2 * HEAD), lambda bi, hp: (bi, 0, hp))
    return pl.pallas_call(
        body, grid=(b, npair), in_specs=[wide, wide, thin], out_specs=[thin, _pair_stat_spec(nb)],
        out_shape=[jax.ShapeDtypeStruct((b, lp, MLA_HEADS * HEAD), F32), jax.ShapeDtypeStruct((b, npair, nb, 8, BLOCK), F32)],
        scratch_shapes=[pltpu.VMEM((len(starts), BLOCK, 2 * KEYS), BF16)],
        compiler_params=_params(("parallel", "parallel")), name=name,
    )(qf, kf, v)


def _mla_bwd(qf, kf, v, o, lse, do, name):
    b, lp, _ = qf.shape
    nb = lp // BLOCK
    npair = MLA_HEADS // 2
    scale = (MLA_NOPE + MLA_ROPE) ** -0.5

    starts = _chunk_starts(lp)

    def body(q_ref, k_ref, v_ref, o_ref, lse_ref, do_ref, dq_ref, dk_ref, dv_ref, kt_ref):
        lo = _lo_lanes()
        dk_ref[...] = jnp.zeros_like(dk_ref)
        dv_ref[...] = jnp.zeros_like(dv_ref)
        for c, s0 in enumerate(starts):
            for h in range(2):
                kt_ref[c, h] = k_ref[0, s0:s0 + KEYS, h * BLOCK:(h + 1) * BLOCK].astype(F32).T.astype(BF16)

        def qtile(r0, bq, n):
            rows = pl.ds(r0, bq)
            qs = [q_ref[0, rows, h * BLOCK:(h + 1) * BLOCK] for h in range(2)]
            do_i = do_ref[0, rows, :]
            dos = _halves(do_i.astype(BF16), lo)
            do_st = jnp.concatenate(dos, axis=0)
            both = (do_i * o_ref[0, rows, :]).T
            dsum = (jnp.sum(both[:HEAD], axis=0, keepdims=True), jnp.sum(both[HEAD:], axis=0, keepdims=True))
            lses = _get_rows(lse_ref, r0, bq)
            t_idx = r0 + _iota2((KEYS, bq), 1)

            def kchunk(c, dqts, masked):
                s0, valid = _key_chunk(c, lp, t_idx, False, 0)
                valid = valid if masked else None
                keys = pl.ds(s0, KEYS)
                v_c = v_ref[0, keys, :]
                ps, out = [], []
                for h in range(2):
                    lanes = slice(h * BLOCK, (h + 1) * BLOCK)
                    s = _dot_nt(k_ref[0, keys, lanes], qs[h]) * scale
                    p = _where(valid, jnp.exp(s - lses[h]), 0.0)
                    ds = (p * (_dot_nt(v_c, dos[h]) - dsum[h]) * scale).astype(BF16)
                    dk_ref[0, keys, lanes] += jnp.dot(ds, qs[h], preferred_element_type=F32)
                    out.append(dqts[h] + jnp.dot(kt_ref[c, h], ds, preferred_element_type=F32))
                    ps.append(p.astype(BF16))
                dv_ref[0, keys, :] += jnp.dot(jnp.concatenate(ps, axis=1), do_st, preferred_element_type=F32)
                return tuple(out)

            zero = jnp.zeros((BLOCK, bq), F32)
            dq_a, dq_b = _walk_chunks(r0, n, kchunk, (zero, zero))
            dq_ref[0, rows, 0:BLOCK] = dq_a.T
            dq_ref[0, rows, BLOCK:2 * BLOCK] = dq_b.T

        _for_query_tiles(nb, qtile)

    wide = pl.BlockSpec((1, lp, 2 * BLOCK), lambda bi, hp: (bi, 0, hp))
    thin = pl.BlockSpec((1, lp, 2 * HEAD), lambda bi, hp: (bi, 0, hp))
    return pl.pallas_call(
        body, grid=(b, npair), in_specs=[wide, wide, thin, thin, _pair_stat_spec(nb), thin], out_specs=[wide, wide, thin],
        out_shape=[jax.ShapeDtypeStruct(qf.shape, F32), jax.ShapeDtypeStruct(qf.shape, F32), jax.ShapeDtypeStruct(v.shape, F32)],
        scratch_shapes=[pltpu.VMEM((len(starts), 2, BLOCK, KEYS), BF16)],
        compiler_params=_params(("parallel", "parallel")), name=name,
    )(qf, kf, v, o, lse, do)


def _swa_keys(k_ref, v_ref, n, kv):
    prev = jnp.maximum(n - 1, 0)
    rows = lambda blk: pl.ds(pl.multiple_of(blk * BLOCK, BLOCK), BLOCK)
    mine = (_iota2((1, BLOCK), 1) >= HEAD).astype(jnp.int32) == kv

    def both_halves(ref):
        x = jnp.concatenate([ref[0, rows(prev), :], ref[0, rows(n), :], ref[0, 0:BLOCK, :]], axis=0)
        return jnp.where(mine, x, pltpu.roll(x, HEAD, 1))

    slot = _iota2((3 * BLOCK, BLOCK), 0)
    loc = slot % BLOCK
    s_idx = jnp.where(slot < BLOCK, (n - 1) * BLOCK + loc, jnp.where(slot < 2 * BLOCK, n * BLOCK + loc, loc))
    dist = n * BLOCK + _iota2((3 * BLOCK, BLOCK), 1) - s_idx
    band = (slot < 2 * BLOCK) & (dist >= 0) & (dist < SWA_WINDOW) & (s_idx >= BLOCK)
    meta = (slot >= 2 * BLOCK) & (s_idx >= N_PAD) & (dist >= 0)
    return both_halves(k_ref), both_halves(v_ref), band | meta, dist.astype(F32), prev


def _swa_probs(q_h, kdup, valid, dist, head, sink_ref):
    slope = jnp.exp(jnp.full((1, 1), -8.0 * math.log(2.0) / SWA_HEADS, F32) * (head + 1).astype(F32))
    s = jnp.where(valid, _dot_nt(kdup, q_h) - slope * dist, NEG)
    sink = sink_ref[pl.ds(head, 1), 0:1]
    m = jnp.maximum(jnp.max(s, axis=0, keepdims=True), sink)
    e = jnp.where(valid, jnp.exp(s - m), 0.0)
    es = jnp.exp(sink - m)
    inv = 1.0 / (jnp.sum(e, axis=0, keepdims=True) + es)
    return e * inv, es * inv


SWA_PAIRS = SWA_HEADS // SWA_KV_HEADS // 2
SWA_GROUP = SWA_PAIRS * 2 * HEAD


def _swa_specs(b, lp):
    nb = lp // BLOCK
    qcol = lambda first: pl.BlockSpec((1, BLOCK, SWA_GROUP), lambda bi, kv, n: (bi, n, first // SWA_GROUP + kv))
    kcol = lambda first: pl.BlockSpec((1, lp, BLOCK), lambda bi, kv, n: (bi, 0, first // BLOCK))
    sink = pl.BlockSpec((SWA_HEADS, BLOCK), lambda bi, kv, n: (0, 0))
    return (b, SWA_KV_HEADS, nb), qcol, kcol, sink


def _swa_fwd(proj3, sinks, name):
    b, lp, _ = proj3.shape
    grid, qcol, kcol, sink = _swa_specs(b, lp)

    def body(q_ref, k_ref, v_ref, sink_ref, o_ref):
        kv, n = pl.program_id(1), pl.program_id(2)
        lo = _lo_lanes()
        kdup, vdup, valid, dist, _ = _swa_keys(k_ref, v_ref, n, kv)
        kdup = kdup.astype(BF16)
        vt = _split_rows_t(vdup)
        for p in range(SWA_PAIRS):
            lanes = slice(p * BLOCK, (p + 1) * BLOCK)
            qs = _halves((q_ref[0, :, lanes] * HEAD_SCALE).astype(BF16), lo)
            probs = [_swa_probs(qs[hh], kdup, valid, dist, (kv * SWA_PAIRS + p) * 2 + hh, sink_ref)[0].astype(BF16) for hh in range(2)]
            o_ref[0, :, lanes] = jnp.dot(vt, jnp.concatenate(probs, axis=0), preferred_element_type=F32).T

    return pl.pallas_call(
        body, grid=grid, in_specs=[qcol(OD_Q), kcol(OD_K), kcol(OD_V), sink], out_specs=qcol(0),
        out_shape=jax.ShapeDtypeStruct((b, lp, SWA_HEADS * HEAD), F32),
        compiler_params=_params(("parallel", "parallel", "parallel")), name=name,
    )(proj3, proj3, proj3, sinks)


def _swa_bwd(proj3, sinks, do, name):
    b, lp, _ = proj3.shape
    nb = lp // BLOCK
    grid, qcol, kcol, sink = _swa_specs(b, lp)

    def body(q_ref, k_ref, v_ref, sink_ref, do_ref, dq_ref, dk_ref, dv_ref, dsink_ref, dk_acc, dv_acc):
        kv, n = pl.program_id(1), pl.program_id(2)

        @pl.when((n == 0) & (pl.program_id(0) == 0) & (kv == 0))
        def _():
            dsink_ref[...] = jnp.zeros_like(dsink_ref)

        @pl.when(n == 0)
        def _():
            dk_acc[...] = jnp.zeros_like(dk_acc)
            dv_acc[...] = jnp.zeros_like(dv_acc)

        lo = _lo_lanes()
        kdup, vdup, valid, dist, prev = _swa_keys(k_ref, v_ref, n, kv)
        kt = _split_rows_t(kdup)
        kdup, vdup = kdup.astype(BF16), vdup.astype(BF16)
        dkc = jnp.zeros((3 * BLOCK, BLOCK), F32)
        dvc = jnp.zeros((3 * BLOCK, BLOCK), F32)
        for p in range(SWA_PAIRS):
            lanes = slice(p * BLOCK, (p + 1) * BLOCK)
            qs = _halves((q_ref[0, :, lanes] * HEAD_SCALE).astype(BF16), lo)
            dos = _halves(do_ref[0, :, lanes].astype(BF16), lo)
            dss, prs = [], []
            for hh in range(2):
                head = (kv * SWA_PAIRS + p) * 2 + hh
                pr, ps = _swa_probs(qs[hh], kdup, valid, dist, head, sink_ref)
                dp = _dot_nt(vdup, dos[hh])
                dsum = jnp.sum(pr * dp, axis=0, keepdims=True)
                dsink_ref[pl.ds(head, 1), :] += jnp.broadcast_to(-jnp.sum(ps * dsum, axis=1, keepdims=True), (1, BLOCK))
                dss.append((pr * (dp - dsum)).astype(BF16))
                prs.append(pr.astype(BF16))
            dq_ref[0, :, lanes] = jnp.dot(kt, jnp.concatenate(dss, axis=0), preferred_element_type=F32).T * HEAD_SCALE
            dkc = dkc + jnp.dot(jnp.concatenate(dss, axis=1), jnp.concatenate(qs, axis=0), preferred_element_type=F32)
            dvc = dvc + jnp.dot(jnp.concatenate(prs, axis=1), jnp.concatenate(dos, axis=0), preferred_element_type=F32)
        rows = lambda blk: pl.ds(pl.multiple_of(blk * BLOCK, BLOCK), BLOCK)
        for part, r in enumerate((rows(prev), rows(n), slice(0, BLOCK))):
            dk_acc[r, :] += dkc[part * BLOCK:(part + 1) * BLOCK]
            dv_acc[r, :] += dvc[part * BLOCK:(part + 1) * BLOCK]

        for acc, ref in ((dk_acc, dk_ref), (dv_acc, dv_ref)):
            @pl.when((n == nb - 1) & (kv == 0))
            def _():
                x = acc[...]
                ref[0] = x + pltpu.roll(x, HEAD, 1)

            @pl.when((n == nb - 1) & (kv == 1))
            def _():
                x = acc[...]
                ref[0] = jnp.where(lo, ref[0], x + pltpu.roll(x, HEAD, 1))

    kvout = pl.BlockSpec((1, lp, BLOCK), lambda bi, kv, n: (bi, 0, 0))
    kvshape = jax.ShapeDtypeStruct((b, lp, BLOCK), F32)
    return pl.pallas_call(
        body, grid=grid, in_specs=[qcol(OD_Q), kcol(OD_K), kcol(OD_V), sink, qcol(0)], out_specs=[qcol(0), kvout, kvout, sink],
        out_shape=[jax.ShapeDtypeStruct((b, lp, SWA_HEADS * HEAD), F32), kvshape, kvshape, jax.ShapeDtypeStruct((SWA_HEADS, BLOCK), F32)],
        scratch_shapes=[pltpu.VMEM((lp, BLOCK), F32), pltpu.VMEM((lp, BLOCK), F32)],
        compiler_params=_params(("arbitrary", "arbitrary", "arbitrary")), name=name,
    )(proj3, proj3, proj3, sinks, do)


def _kernel_weights(ev_w_in, ev_w_uq, ev_w_ukv, od_w_in):
    zeros = lambda r, c: jnp.zeros((r, c), ev_w_in.dtype)
    q_sb, k_sb, v_sb, g_sb, c_q, c_kv, k_r, g_mla = jnp.split(ev_w_in, [512, 1024, 1536, 2048, 2304, 2432, 2464], axis=1)
    w0 = jnp.concatenate([g_sb, g_mla, q_sb, k_sb, v_sb, c_q, c_kv, zeros(D_MODEL, MLA_NOPE), k_r, zeros(D_MODEL, 32)], axis=1)
    uq = ev_w_uq.reshape(MLA_Q_LORA, MLA_HEADS, MLA_NOPE + MLA_ROPE)
    wq = jnp.pad(uq, ((0, 0), (0, 0), (0, BLOCK - MLA_NOPE - MLA_ROPE))).reshape(MLA_Q_LORA, MLA_HEADS * BLOCK)
    ukv = ev_w_ukv.reshape(MLA_KV_LORA, MLA_HEADS, BLOCK)
    wk = jnp.pad(ukv[:, :, :MLA_NOPE], ((0, 0), (0, 0), (0, BLOCK - MLA_NOPE))).reshape(MLA_KV_LORA, MLA_HEADS * BLOCK)
    wv = ukv[:, :, MLA_NOPE:].reshape(MLA_KV_LORA, MLA_HEADS * HEAD)
    q, k, v, g = jnp.split(od_w_in, [1024, 1152, 1280], axis=1)
    w1 = jnp.concatenate([g, q, k, v], axis=1)
    return w0, wq, wk, wv, w1


def _original_grads(dw0, dwq, dwk, dwv, dw1):
    sl = lambda a, first, n: a[:, first:first + n]
    d_ev_w_in = jnp.concatenate([sl(dw0, EV_Q, 512), sl(dw0, EV_K, 512), sl(dw0, EV_V, 512), sl(dw0, EV_G, 512), sl(dw0, EV_CQ, 256),
                                 sl(dw0, EV_CKV, 128), sl(dw0, EV_KR + MLA_NOPE, MLA_ROPE), sl(dw0, EV_G + 512, 512)], axis=1)
    d_uq = dwq.reshape(MLA_Q_LORA, MLA_HEADS, BLOCK)[:, :, :MLA_NOPE + MLA_ROPE].reshape(MLA_Q_LORA, -1)
    d_ukv = jnp.concatenate([dwk.reshape(MLA_KV_LORA, MLA_HEADS, BLOCK)[:, :, :MLA_NOPE], dwv.reshape(MLA_KV_LORA, MLA_HEADS, HEAD)],
                            axis=2).reshape(MLA_KV_LORA, -1)
    d_od_w_in = jnp.concatenate([sl(dw1, OD_Q, 1024), sl(dw1, OD_K, 128), sl(dw1, OD_V, 128), sl(dw1, OD_G, 1024)], axis=1)
    return d_ev_w_in, d_uq, d_ukv, d_od_w_in


def _meta_rows_sum(dh0_3):
    b, _, d = dh0_3.shape

    def body(x_ref, o_ref):
        acc = x_ref[0, N_PAD:BLOCK, :]
        for i in range(1, b):
            acc = acc + x_ref[i, N_PAD:BLOCK, :]
        o_ref[...] = acc

    return pl.pallas_call(
        body, grid=(1,), in_specs=[pl.BlockSpec((b, BLOCK, d), lambda i: (0, 0, 0))], out_specs=pl.BlockSpec((N_META, d), lambda i: (0, 0)),
        out_shape=jax.ShapeDtypeStruct((N_META, d), F32), compiler_params=_params(("arbitrary",)), name="meta_rows_sum",
    )(dh0_3)


def _local_step(x, meta, norm_g, final_g, gq, gkv, sinks, target, ev_w_in, ev_w_uq, ev_w_ukv, wo0, od_w_in, wo1):
    b, seq, d = x.shape
    lp = seq + BLOCK
    t = b * lp
    w0, wq, wk, wv, w1 = _kernel_weights(ev_w_in, ev_w_uq, ev_w_ukv, od_w_in)
    h0 = jnp.concatenate([jnp.zeros((b, N_PAD, d), F32), jnp.broadcast_to(meta[None], (b, N_META, d)), x], axis=1).reshape(t, d)
    tabs = _rope_tables(lp)
    g0, g1 = norm_g[0:1], norm_g[1:2]

    hn0 = _rms_fwd(h0, g0, "norm0")
    proj0 = _mm(hn0, w0, "inproj0")
    p0 = proj0.reshape(b, lp, EV_N)
    o_sb, sb_tot = _sb_fwd(p0, "sb_fwd")
    qf, kf, v = _mla_prep_fwd(p0, gq, gkv, wq, wk, wv, tabs, "mla_prep_fwd")
    o_mla, lse = _mla_fwd(qf, kf, v, "mla_fwd")
    o0 = [o_sb.reshape(t, -1), o_mla.reshape(t, -1)]
    ao0 = _gate_fwd(o0, proj0, "gate0")
    h1 = _mm(ao0, wo0, "outproj0", res=h0)

    hn1 = _rms_fwd(h1, g1, "norm1")
    proj1 = _mm(hn1, w1, "inproj1")
    p1 = proj1.reshape(b, lp, OD_N)
    sinks_b = jnp.broadcast_to(sinks.reshape(SWA_HEADS, 1), (SWA_HEADS, BLOCK))
    o1 = _swa_fwd(p1, sinks_b, "swa_fwd").reshape(t, -1)
    ao1 = _gate_fwd([o1], proj1, "gate1")
    h2 = _mm(ao1, wo1, "outproj1", res=h1)

    dh2, d_final_g, loss = _loss_head(h2, final_g.reshape(1, d), target, b, lp)

    d_wo1 = _mm_tn(ao1, dh2, "d_wo1")
    dao1 = _mm_nt(dh2, wo1, "d_ao1")
    (do1,), dg1 = _gate_bwd(dao1, [o1], proj1, "gate1_bwd")
    dq1, dk4, dv4, d_sinks = _swa_bwd(p1, sinks_b, do1.reshape(b, lp, -1), "swa_bwd")
    unheads = lambda a: a.reshape(t, SWA_KV_HEADS * HEAD).astype(BF16)
    dproj1 = jnp.concatenate([dg1, dq1.reshape(t, -1).astype(BF16), unheads(dk4), unheads(dv4)], axis=1)
    d_w1 = _mm_tn(hn1, dproj1, "d_w1")
    dhn1 = _mm_nt(dproj1, w1, "d_hn1")
    dh1, d_g1 = _rms_bwd(h1, g1, dhn1, dh2, "norm1_bwd")

    d_wo0 = _mm_tn(ao0, dh1, "d_wo0")
    dao0 = _mm_nt(dh1, wo0, "d_ao0")
    (do_sb, do_mla), dg0 = _gate_bwd(dao0, o0, proj0, "gate0_bwd")
    dq_sb, dk_sb, dv_sb = _sb_bwd(p0, sb_tot, do_sb.reshape(b, lp, -1), "sb_bwd")
    dqf, dkf, dv = _mla_bwd(qf, kf, v, o_mla, lse, do_mla.reshape(b, lp, -1), "mla_bwd")
    dcq, dckv, dkr, d_wq, d_wk, d_wv, d_gq, d_gkv = _mla_prep_bwd(p0, gq, gkv, wq, wk, wv, tabs, dqf, dkf, dv, "mla_prep_bwd")
    flat = lambda a: a.reshape(t, -1).astype(BF16)
    dproj0 = jnp.concatenate([dg0, flat(dq_sb), flat(dk_sb), flat(dv_sb), flat(dcq), flat(dckv), flat(dkr)], axis=1)
    d_w0 = _mm_tn(hn0, dproj0, "d_w0")
    dhn0 = _mm_nt(dproj0, w0, "d_hn0")
    dh0, d_g0 = _rms_bwd(h0, g0, dhn0, dh1, "norm0_bwd")
    dh0 = dh0.reshape(b, lp, d)

    d_ev_w_in, d_uq, d_ukv, d_od_w_in = _original_grads(d_w0, d_wq, d_wk, d_wv, d_w1)
    grads = dict(meta=_meta_rows_sum(dh0), norm_g=jnp.concatenate([d_g0, d_g1], axis=0), final_g=d_final_g.reshape(d),
                 ev_w_in=d_ev_w_in, ev_q_norm_g=d_gq, ev_kv_norm_g=d_gkv, ev_w_uq=d_uq, ev_w_ukv=d_ukv, ev_w_out=d_wo0,
                 od_w_in=d_od_w_in, od_sinks=d_sinks[:, 0].reshape(1, SWA_HEADS), od_w_out=d_wo1)
    return loss, dh0[:, BLOCK:], grads


MESH = pl.DeviceIdType.MESH
ANY = pl.BlockSpec(memory_space=pl.ANY)


def _place():
    return lax.axis_index("x"), lax.axis_index("y"), lax.axis_index("c")


def _other_chips(x, y):
    return [(1 - x, y), (x, 1 - y), (1 - x, 1 - y)]


def _with_own_slot(slots, own):
    me = 2 * lax.axis_index("x") + lax.axis_index("y")
    return lax.dynamic_update_slice(slots, own[None], (me,) + (0,) * own.ndim)


def _gather_weights(packs, meta, name):
    n = len(packs)

    def body(*refs):
        ins, m_ref, outs, mo_ref = refs[:n], refs[n], refs[n + 1:2 * n + 1], refs[2 * n + 1]
        send_sems, recv_sems = refs[2 * n + 2:]
        x, y, c = _place()
        me, sib = 2 * x + y, (x, y, 1 - c)
        chips = _other_chips(x, y)

        def copy(k, src, dst, to):
            return pltpu.make_async_remote_copy(src_ref=src, dst_ref=dst, send_sem=send_sems.at[k], recv_sem=recv_sems.at[k], device_id=to,
                                                device_id_type=MESH)

        def half(i, chip, h):
            rows = packs[i].shape[0] // 2
            return outs[i].at[chip, pl.ds(h * rows, rows), :]

        def mine(i):
            rows = packs[i].shape[0] // 2
            return ins[i].at[pl.ds(c * rows, rows), :]

        sent = [copy(6 * i + k, mine(i), half(i, me, c), (px, py, c)) for i in range(n) for k, (px, py) in enumerate(chips)]
        sent += [copy(6 * n + k, m_ref, mo_ref.at[me], (px, py, c)) for k, (px, py) in enumerate(chips)]
        for cp in sent:
            cp.start()
        for i in range(n):
            for k, (px, py) in enumerate(chips):
                landed = half(i, 2 * px + py, c)
                copy(6 * i + k, mine(i), landed, (px, py, c)).wait_recv()
                fwd = copy(6 * i + 3 + k, landed, landed, sib)
                fwd.start()
                sent.append(fwd)
        for k, (px, py) in enumerate(chips):
            for i in range(n):
                other = half(i, 2 * px + py, 1 - c)
                copy(6 * i + 3 + k, other, other, sib).wait_recv()
            copy(6 * n + k, m_ref, mo_ref.at[2 * px + py], (px, py, c)).wait_recv()
        for cp in sent:
            cp.wait_send()

    nsem = 6 * n + 3
    res = pl.pallas_call(
        body, in_specs=[ANY] * (n + 1), out_specs=[ANY] * (n + 1),
        out_shape=[jax.ShapeDtypeStruct((N_CHIPS,) + a.shape, a.dtype) for a in list(packs) + [meta]],
        scratch_shapes=[pltpu.SemaphoreType.DMA((nsem,)), pltpu.SemaphoreType.DMA((nsem,))],
        name=name,
    )(*packs, meta)
    return [_with_own_slot(r, a) for r, a in zip(res[:n], packs)], _with_own_slot(res[n], meta)


def _grads_to_sibling(gs, name):
    n = len(gs)

    def body(*refs):
        ins, outs, send_sems, recv_sems = refs[:n], refs[n:2 * n], refs[2 * n], refs[2 * n + 1]
        x, y, c = _place()
        cps = []
        for i in range(n):
            rows = gs[i].shape[1] // 2
            cps.append(pltpu.make_async_remote_copy(src_ref=ins[i].at[:, pl.ds((1 - c) * rows, rows), :], dst_ref=outs[i],
                                                    send_sem=send_sems.at[i], recv_sem=recv_sems.at[i], device_id=(x, y, 1 - c),
                                                    device_id_type=MESH))
        for cp in cps:
            cp.start()
        for cp in cps:
            cp.wait()

    return pl.pallas_call(
        body, in_specs=[ANY] * n, out_specs=[ANY] * n,
        out_shape=[jax.ShapeDtypeStruct((g.shape[0], g.shape[1] // 2, g.shape[2]), g.dtype) for g in gs],
        scratch_shapes=[pltpu.SemaphoreType.DMA((n,)), pltpu.SemaphoreType.DMA((n,))],
        name=name,
    )(*gs)


def _share_halves(rs, name):
    n = len(rs)

    def body(*refs):
        ins, outs, send_sems, recv_sems = refs[:n], refs[n:2 * n], refs[2 * n], refs[2 * n + 1]
        x, y, c = _place()
        cps = [pltpu.make_async_remote_copy(src_ref=ins[i], dst_ref=outs[i], send_sem=send_sems.at[i], recv_sem=recv_sems.at[i],
                                            device_id=(x, y, 1 - c), device_id_type=MESH) for i in range(n)]
        for cp in cps:
            cp.start()
        for cp in cps:
            cp.wait()

    theirs = pl.pallas_call(
        body, in_specs=[ANY] * n, out_specs=[ANY] * n, out_shape=[jax.ShapeDtypeStruct(r.shape, r.dtype) for r in rs],
        scratch_shapes=[pltpu.SemaphoreType.DMA((n,)), pltpu.SemaphoreType.DMA((n,))],
        name=name,
    )(*rs)
    first = lax.axis_index("c") == 0
    return [jnp.where(first, jnp.concatenate([r, t], axis=0), jnp.concatenate([t, r], axis=0)) for r, t in zip(rs, theirs)]


def _chip_scatter(ss, name):
    n = len(ss)

    def body(*refs):
        ins, outs, send_sems, recv_sems = refs[:n], refs[n:2 * n], refs[2 * n], refs[2 * n + 1]
        x, y, c = _place()
        me = 2 * x + y
        chips = _other_chips(x, y)
        for i in range(n):
            for k, (px, py) in enumerate(chips):
                pltpu.make_async_remote_copy(src_ref=ins[i].at[2 * px + py], dst_ref=outs[i].at[me], send_sem=send_sems.at[3 * i + k],
                                             recv_sem=recv_sems.at[3 * i + k], device_id=(px, py, c), device_id_type=MESH).start()
        for i in range(n):
            for k, (px, py) in enumerate(chips):
                cp = pltpu.make_async_remote_copy(src_ref=ins[i].at[2 * px + py], dst_ref=outs[i].at[2 * px + py],
                                                  send_sem=send_sems.at[3 * i + k], recv_sem=recv_sems.at[3 * i + k],
                                                  device_id=(px, py, c), device_id_type=MESH)
                cp.wait_recv()
                cp.wait_send()

    parts = pl.pallas_call(
        body, in_specs=[ANY] * n, out_specs=[ANY] * n, out_shape=[jax.ShapeDtypeStruct(s.shape, s.dtype) for s in ss],
        scratch_shapes=[pltpu.SemaphoreType.DMA((3 * n,)), pltpu.SemaphoreType.DMA((3 * n,))],
        name=name,
    )(*ss)
    me = 2 * lax.axis_index("x") + lax.axis_index("y")
    return [_with_own_slot(p, lax.dynamic_index_in_dim(s, me, axis=0, keepdims=False)) for p, s in zip(parts, ss)]


HBM_SPACE = pltpu.MemorySpace.HBM


def _on_sequencer(name, collective_id, n_sems, body):
    @pl.kernel(mesh=plsc.ScalarSubcoreMesh(axis_name="sequencer", num_cores=1), name=name,
               scratch_types=(pltpu.SemaphoreType.DMA((n_sems,)), pltpu.SemaphoreType.DMA((n_sems,))),
               compiler_params=pltpu.CompilerParams(collective_id=collective_id))
    def launch(send_sems, recv_sems):
        body(send_sems, recv_sems)

    launch()


def _handshake(peers):
    barrier = pltpu.get_barrier_semaphore()
    for peer in peers:
        pl.semaphore_signal(barrier, inc=1, device_id=peer, device_id_type=MESH)
    pl.semaphore_wait(barrier, len(peers))


def _gather_on_sequencer(packs, name):
    n = len(packs)
    ins = [jax.new_ref(p, memory_space=HBM_SPACE) for p in packs]
    outs = [jax.empty_ref(jax.ShapeDtypeStruct((N_CHIPS,) + p.shape, p.dtype), memory_space=HBM_SPACE) for p in packs]

    def body(send_sems, recv_sems):
        x, y, c = _place()
        me, sib = 2 * x + y, (x, y, 1 - c)
        chips = _other_chips(x, y)
        _handshake([(px, py, c) for px, py in chips] + [sib])

        def copy(k, src, dst, to):
            return pltpu.make_async_remote_copy(src_ref=src, dst_ref=dst, send_sem=send_sems.at[k], recv_sem=recv_sems.at[k], device_id=to,
                                                device_id_type=MESH)

        def half(i, chip, h):
            rows = packs[i].shape[0] // 2
            return outs[i].at[chip, pl.ds(h * rows, rows), :]

        def mine(i):
            rows = packs[i].shape[0] // 2
            return ins[i].at[pl.ds(c * rows, rows), :]

        sent = [copy(6 * i + k, mine(i), half(i, me, c), (px, py, c)) for i in range(n) for k, (px, py) in enumerate(chips)]
        for cp in sent:
            cp.start()
        for i in range(n):
            for k, (px, py) in enumerate(chips):
                landed = half(i, 2 * px + py, c)
                copy(6 * i + k, mine(i), landed, (px, py, c)).wait_recv()
                fwd = copy(6 * i + 3 + k, landed, landed, sib)
                fwd.start()
                sent.append(fwd)
        for i in range(n):
            for k, (px, py) in enumerate(chips):
                other = half(i, 2 * px + py, 1 - c)
                copy(6 * i + 3 + k, other, other, sib).wait_recv()
        for cp in sent:
            cp.wait_send()

    _on_sequencer(name, 1, 6 * n, body)
    return [_with_own_slot(o[...], p) for o, p in zip(outs, packs)]


def _grads_to_sibling_on_sequencer(gs, name, collective_id):
    n = len(gs)
    ins = [jax.new_ref(g, memory_space=HBM_SPACE) for g in gs]
    outs = [jax.empty_ref(jax.ShapeDtypeStruct((g.shape[0], g.shape[1] // 2, g.shape[2]), g.dtype), memory_space=HBM_SPACE) for g in gs]

    def body(send_sems, recv_sems):
        x, y, c = _place()
        _handshake([(x, y, 1 - c)])
        cps = []
        for i in range(n):
            rows = gs[i].shape[1] // 2
            cps.append(pltpu.make_async_remote_copy(src_ref=ins[i].at[:, pl.ds((1 - c) * rows, rows), :], dst_ref=outs[i],
                                                    send_sem=send_sems.at[i], recv_sem=recv_sems.at[i], device_id=(x, y, 1 - c),
                                                    device_id_type=MESH))
        for cp in cps:
            cp.start()
        for cp in cps:
            cp.wait()

    _on_sequencer(name, collective_id, n, body)
    return [o[...] for o in outs]


def _chip_scatter_on_sequencer(ss, name, collective_id):
    n = len(ss)
    ins = [jax.new_ref(s, memory_space=HBM_SPACE) for s in ss]
    outs = [jax.empty_ref(jax.ShapeDtypeStruct(s.shape, s.dtype), memory_space=HBM_SPACE) for s in ss]

    def body(send_sems, recv_sems):
        x, y, c = _place()
        me = 2 * x + y
        chips = _other_chips(x, y)
        _handshake([(px, py, c) for px, py in chips])
        for i in range(n):
            for k, (px, py) in enumerate(chips):
                pltpu.make_async_remote_copy(src_ref=ins[i].at[2 * px + py], dst_ref=outs[i].at[me], send_sem=send_sems.at[3 * i + k],
                                             recv_sem=recv_sems.at[3 * i + k], device_id=(px, py, c), device_id_type=MESH).start()
        for i in range(n):
            for k, (px, py) in enumerate(chips):
                cp = pltpu.make_async_remote_copy(src_ref=ins[i].at[2 * px + py], dst_ref=outs[i].at[2 * px + py],
                                                  send_sem=send_sems.at[3 * i + k], recv_sem=recv_sems.at[3 * i + k],
                                                  device_id=(px, py, c), device_id_type=MESH)
                cp.wait_recv()
                cp.wait_send()

    _on_sequencer(name, collective_id, 3 * n, body)
    me = 2 * lax.axis_index("x") + lax.axis_index("y")
    return [_with_own_slot(o[...], lax.dynamic_index_in_dim(s, me, axis=0, keepdims=False)) for o, s in zip(outs, ss)]


def _all_reduce_small(v, name):
    shape = v.shape

    def body(v_ref, o_ref, slots, send_sems, recv_sems):
        x, y, c = _place()
        me = 4 * x + 2 * y + c
        slots[me] = v_ref[...]
        for r in range(1, N_DEV):
            peer = (x ^ (r >> 2), y ^ ((r >> 1) & 1), c ^ (r & 1))
            pltpu.make_async_remote_copy(src_ref=v_ref, dst_ref=slots.at[me], send_sem=send_sems.at[r - 1], recv_sem=recv_sems.at[r - 1],
                                         device_id=peer, device_id_type=MESH).start()
        for r in range(1, N_DEV):
            peer = (x ^ (r >> 2), y ^ ((r >> 1) & 1), c ^ (r & 1))
            cp = pltpu.make_async_remote_copy(src_ref=v_ref, dst_ref=slots.at[4 * peer[0] + 2 * peer[1] + peer[2]], send_sem=send_sems.at[r - 1],
                                              recv_sem=recv_sems.at[r - 1], device_id=peer, device_id_type=MESH)
            cp.wait_recv()
            cp.wait_send()
        acc = slots[0]
        for d in range(1, N_DEV):
            acc = acc + slots[d]
        o_ref[...] = acc

    vm = pl.BlockSpec(memory_space=pltpu.VMEM)
    return pl.pallas_call(
        body, in_specs=[vm], out_specs=vm, out_shape=jax.ShapeDtypeStruct(shape, F32),
        scratch_shapes=[pltpu.VMEM((N_DEV,) + shape, F32), pltpu.SemaphoreType.DMA((N_DEV - 1,)), pltpu.SemaphoreType.DMA((N_DEV - 1,))],
        name=name,
    )(v)


def _add_sibling(g, gsib, core, name):
    n, _, cdim = g.shape
    half = gsib.shape[1]
    tr = half // 2

    def body(core_ref, a_ref, b_ref, o_ref):
        o_ref[...] = (a_ref[...] + b_ref[...]).astype(o_ref.dtype)

    blk = pl.BlockSpec((1, tr, cdim), lambda j, i, core_ref: (j, i, 0))
    return pl.pallas_call(
        body,
        grid_spec=pltpu.PrefetchScalarGridSpec(
            num_scalar_prefetch=1, grid=(n, half // tr),
            in_specs=[pl.BlockSpec((1, tr, cdim), lambda j, i, core_ref: (j, core_ref[0] * (half // tr) + i, 0)), blk], out_specs=blk),
        out_shape=jax.ShapeDtypeStruct(gsib.shape, BF16), compiler_params=_params(("parallel", "parallel")), name=name,
    )(core, g, gsib)


def _sum_parts(parts, name):
    n, r, cdim = parts.shape
    tr = r // 2

    def body(p_ref, o_ref):
        acc = p_ref[0].astype(F32)
        for j in range(1, n):
            acc = acc + p_ref[j].astype(F32)
        o_ref[...] = acc

    return pl.pallas_call(
        body, grid=(r // tr,), in_specs=[pl.BlockSpec((n, tr, cdim), lambda i: (0, i, 0))],
        out_specs=pl.BlockSpec((tr, cdim), lambda i: (i, 0)), out_shape=jax.ShapeDtypeStruct((r, cdim), F32),
        compiler_params=_params(("parallel",)), name=name,
    )(parts)


def _adamw(parts, w, m, v, name):
    npart, r, cdim = parts.shape
    tr = r // 4 if r % 32 == 0 else r

    def body(p_ref, w_ref, m_ref, v_ref, g_ref, d_ref, nm_ref, nv_ref):
        g = p_ref[0]
        for j in range(1, npart):
            g = g + p_ref[j]
        m_new = ADAM_B1 * m_ref[...] + (1.0 - ADAM_B1) * g
        v_new = ADAM_B2 * v_ref[...] + (1.0 - ADAM_B2) * (g * g)
        m_hat = m_new / (1.0 - ADAM_B1 ** ADAM_STEP)
        v_hat = v_new / (1.0 - ADAM_B2 ** ADAM_STEP)
        g_ref[...] = g
        d_ref[...] = -ADAM_LR * (m_hat / (jnp.sqrt(v_hat) + ADAM_EPS) + ADAM_WD * w_ref[...])
        nm_ref[...] = m_new
        nv_ref[...] = v_new

    blk = pl.BlockSpec((tr, cdim), lambda i: (i, 0))
    shp = jax.ShapeDtypeStruct((r, cdim), F32)
    return pl.pallas_call(
        body, grid=(r // tr,), in_specs=[pl.BlockSpec((npart, tr, cdim), lambda i: (0, i, 0)), blk, blk, blk], out_specs=[blk] * 4,
        out_shape=[shp] * 4, compiler_params=_params(("parallel",)), name=name,
    )(parts, w, m, v)


BIG = ("ev_w_in", "ev_w_uq", "ev_w_ukv", "ev_w_out", "od_w_in", "od_w_out", "meta")
SMALL = ("norm_g", "final_g", "ev_q_norm_g", "ev_kv_norm_g", "od_sinks")
SMALL_SHAPE = (8, 512)
BY_ROWS = ("ev_w_out", "od_w_out")

EV_IN_SHARD, OD_IN_SHARD, UQ_SHARD = 2976 // N_CHIPS, 2304 // N_CHIPS, 768 // N_CHIPS


def _pack_big(a, lead=()):
    dtype = a["ev_w_in"].dtype
    z = lambda r, c: jnp.zeros(lead + (r, c), dtype)
    ax = len(lead)
    pad_to = lambda x, width: jnp.concatenate([x, z(x.shape[ax], width - x.shape[ax + 1])], axis=ax + 1)
    corner = jnp.concatenate([a["ev_w_ukv"], a["meta"], z(256 - MLA_KV_LORA - N_META, 256)], axis=ax)
    return (pad_to(a["ev_w_in"], 768), jnp.concatenate([pad_to(a["ev_w_uq"], 256), corner], axis=ax + 1),
            pad_to(a["od_w_in"], 640), jnp.concatenate([a["ev_w_out"], a["od_w_out"]], axis=ax + 1))


N_FIRST = 2


def _unpack_big(p_in0, p_lat, p_in1, p_out):
    return dict(ev_w_in=p_in0[..., :EV_IN_SHARD], od_w_in=p_in1[..., :OD_IN_SHARD], ev_w_out=p_out[..., :D_MODEL],
                od_w_out=p_out[..., D_MODEL:], ev_w_uq=p_lat[..., :UQ_SHARD], ev_w_ukv=p_lat[..., :MLA_KV_LORA, 256:],
                meta=p_lat[..., MLA_KV_LORA:MLA_KV_LORA + N_META, 256:])


def _chip_shards(full, by_rows):
    if by_rows:
        return full.reshape(N_CHIPS, full.shape[0] // N_CHIPS, full.shape[1])
    return full.reshape(full.shape[0], N_CHIPS, -1).transpose(1, 0, 2)


def _from_chip_shards(slots, by_rows):
    if by_rows:
        return slots.reshape(-1, slots.shape[2])
    return slots.transpose(1, 0, 2).reshape(slots.shape[1], -1)


def _pack_small(arrs, extra=None):
    flat = [a.reshape(-1) for a in arrs] + ([] if extra is None else [extra.reshape(-1)])
    used = sum(f.shape[0] for f in flat)
    return jnp.pad(jnp.concatenate(flat), (0, SMALL_SHAPE[0] * SMALL_SHAPE[1] - used)).reshape(SMALL_SHAPE)


def _unpack_small(p, shapes):
    flat, out, at = p.reshape(-1), [], 0
    for s in shapes:
        n = int(np.prod(s))
        out.append(flat[at:at + n].reshape(s))
        at += n
    return out, flat[at]


def kernel(x, meta, norm_g, final_g, ev_w_in, ev_q_norm_g, ev_kv_norm_g, ev_w_uq, ev_w_ukv, ev_w_out, od_w_in, od_sinks, od_w_out, loss_target, m_meta, m_norm_g, m_final_g, m_ev_w_in, m_ev_q_norm_g, m_ev_kv_norm_g, m_ev_w_uq, m_ev_w_ukv, m_ev_w_out, m_od_w_in, m_od_sinks, m_od_w_out, v_meta, v_norm_g, v_final_g, v_ev_w_in, v_ev_q_norm_g, v_ev_kv_norm_g, v_ev_w_uq, v_ev_w_ukv, v_ev_w_out, v_od_w_in, v_od_sinks, v_od_w_out):
    given = dict(locals())
    two_d = lambda a: a[0] if a.ndim == 3 else a
    packs = {k: _pack_big({n: two_d(given[k + n]) for n in BIG}) for k in ("", "m_", "v_")}

    wbf = [p.astype(BF16) for p in packs[""]]
    later = _gather_on_sequencer(wbf[N_FIRST:], "gather_later_weights")
    first, meta_all = _gather_weights(wbf[:N_FIRST], meta, "gather_weights")
    full = {n: _from_chip_shards(a, n in BY_ROWS) for n, a in _unpack_big(*first, *later).items()}
    meta_full = _from_chip_shards(meta_all, False)

    loss, grad_x, grads = _local_step(x, meta_full, norm_g, final_g, ev_q_norm_g, ev_kv_norm_g, od_sinks, loss_target,
                                      full["ev_w_in"], full["ev_w_uq"], full["ev_w_ukv"], full["ev_w_out"], full["od_w_in"], full["od_w_out"])

    gpacks = _pack_big({n: _chip_shards(grads[n], n in BY_ROWS) for n in BIG}, lead=(N_CHIPS,))
    core = lax.axis_index("c").astype(jnp.int32).reshape(1)

    def reduce_group(group, tag, to_sibling, scatter):
        sums = [_add_sibling(g, s, core, f"add_sibling_{tag}{i}") for i, (g, s) in enumerate(zip(group, to_sibling(group)))]
        parts = [_sum_parts(p, f"add_chips_{tag}{i}") for i, p in enumerate(scatter(sums))]
        return _share_halves(parts, "reduced_to_sibling_" + tag)

    reduced_later = reduce_group(gpacks[N_FIRST:], "later", lambda g: _grads_to_sibling_on_sequencer(g, "grads_to_sibling_later", 3),
                                 lambda s: _chip_scatter_on_sequencer(s, "grads_to_chips_later", 2))
    reduced_first = reduce_group(gpacks[:N_FIRST], "first", lambda g: _grads_to_sibling(g, "grads_to_sibling_first"),
                                 lambda s: _chip_scatter_on_sequencer(s, "grads_to_chips_first", 4))
    reduced = reduced_first + reduced_later
    updated = [_adamw(r[None], packs[""][i], packs["m_"][i], packs["v_"][i], f"adamw_matrices_{i}") for i, r in enumerate(reduced)]
    big_out = [{n: a.reshape(given[n].shape) for n, a in _unpack_big(*outs).items()} for outs in zip(*updated)]

    small_shapes = [given[n].shape for n in SMALL]
    ssum = _all_reduce_small(_pack_small([grads[n] for n in SMALL], loss[0, 0]), "reduce_vectors")
    small_out = _adamw(ssum[None], _pack_small([given[n] for n in SMALL]), _pack_small([given["m_" + n] for n in SMALL]),
                       _pack_small([given["v_" + n] for n in SMALL]), "adamw_vectors")
    total_loss = ssum.reshape(-1)[sum(int(np.prod(s)) for s in small_shapes)]
    small_out = [_unpack_small(o, small_shapes)[0] for o in small_out]

    names = ("meta", "norm_g", "final_g", "ev_w_in", "ev_q_norm_g", "ev_kv_norm_g", "ev_w_uq", "ev_w_ukv", "ev_w_out", "od_w_in", "od_sinks",
             "od_w_out")
    outs = [total_loss, grad_x]
    for kind in range(4):
        for n in names:
            outs.append(big_out[kind][n] if n in BIG else small_out[kind][SMALL.index(n)])
    return tuple(outs)
```

```python
import functools
import math

import numpy as np
import jax
import jax.numpy as jnp
from jax import lax
from jax.experimental import pallas as pl
from jax.experimental.pallas import tpu as pltpu
from jax.experimental.pallas import tpu_sc as plsc

F32 = jnp.float32
BF16 = jnp.bfloat16

D_MODEL = 1024
BLOCK = 128
N_META = 16
N_PAD = BLOCK - N_META
NORM_EPS = 1e-6
NEG = -1e30
HEAD = 64
SB_HEADS = 8
MLA_HEADS = 8
MLA_Q_LORA = 256
MLA_KV_LORA = 128
MLA_NOPE = 64
MLA_ROPE = 32
ROPE_BASE = 10000.0
SWA_HEADS = 16
SWA_KV_HEADS = 2
SWA_WINDOW = 128
N_CHIPS = 4
N_DEV = 8

ADAM_LR = 0.001
ADAM_B1 = 0.9
ADAM_B2 = 0.999
ADAM_EPS = 1e-08
ADAM_WD = 0.01
ADAM_STEP = 10

VMEM_LIMIT = 48 * 1024 * 1024

EV_G, EV_Q, EV_K, EV_V, EV_CQ, EV_CKV, EV_KR, EV_N = 0, 1024, 1536, 2048, 2560, 2816, 2944, 3072
OD_G, OD_Q, OD_K, OD_V, OD_N = 0, 1024, 2048, 2176, 2304


def _params(sem=None):
    return pltpu.CompilerParams(dimension_semantics=sem, vmem_limit_bytes=VMEM_LIMIT)


def _row_tile(m):
    return 256 if m % 256 == 0 else 128


def _matmul_rows(m):
    for c in (1088, 1024, 768, 640, 512, 384, 256):
        if m % c == 0:
            return c
    return 128


def _dot(a, b):
    return jnp.dot(a.astype(BF16), b.astype(BF16), preferred_element_type=F32)


def _dot_nt(a, b):
    return lax.dot_general(a.astype(BF16), b.astype(BF16), (((1,), (1,)), ((), ())), preferred_element_type=F32)


def _dot_tn(a, b):
    return lax.dot_general(a.astype(BF16), b.astype(BF16), (((0,), (0,)), ((), ())), preferred_element_type=F32)


def _rms_fwd(h, g, name):
    t, d = h.shape
    tm = _row_tile(t)

    def body(h_ref, g_ref, o_ref):
        x = h_ref[...]
        r = lax.rsqrt(jnp.mean(x * x, axis=-1, keepdims=True) + NORM_EPS)
        o_ref[...] = ((x * r) * g_ref[...]).astype(o_ref.dtype)

    return pl.pallas_call(
        body, grid=(t // tm,),
        in_specs=[pl.BlockSpec((tm, d), lambda i: (i, 0)), pl.BlockSpec((1, d), lambda i: (0, 0))],
        out_specs=pl.BlockSpec((tm, d), lambda i: (i, 0)),
        out_shape=jax.ShapeDtypeStruct((t, d), BF16), compiler_params=_params(("parallel",)), name=name,
    )(h, g)


def _rms_bwd(h, g, dy, dres, name):
    t, d = h.shape
    tm = _row_tile(t)

    def body(h_ref, g_ref, dy_ref, dres_ref, dh_ref, dg_ref):
        @pl.when(pl.program_id(0) == 0)
        def _():
            dg_ref[...] = jnp.zeros_like(dg_ref)

        x = h_ref[...]
        r = lax.rsqrt(jnp.mean(x * x, axis=-1, keepdims=True) + NORM_EPS)
        xr = x * r
        dy_ = dy_ref[...]
        u = dy_ * g_ref[...]
        dh_ref[...] = dres_ref[...] + r * (u - xr * jnp.mean(u * xr, axis=-1, keepdims=True))
        dg_ref[...] += jnp.sum(dy_ * xr, axis=0, keepdims=True)

    row = pl.BlockSpec((tm, d), lambda i: (i, 0))
    vec = pl.BlockSpec((1, d), lambda i: (0, 0))
    return pl.pallas_call(
        body, grid=(t // tm,), in_specs=[row, vec, row, row], out_specs=[row, vec],
        out_shape=[jax.ShapeDtypeStruct((t, d), F32), jax.ShapeDtypeStruct((1, d), F32)],
        compiler_params=_params(("arbitrary",)), name=name,
    )(h, g, dy, dres)


def _col_tile(n):
    for c in (1024, 768, 640, 512, 384, 256, 128):
        if n % c == 0:
            return c
    return n


def _mm(a, w, name, res=None, out_dtype=F32, a_cols=None):
    m = a.shape[0]
    k, n = w.shape
    a_blk = 0 if a_cols is None else a_cols[0] // k
    assert a_cols is None or (a_cols[1] == k and a_cols[0] % k == 0)
    tm, tn = _matmul_rows(m), _col_tile(n)

    def body(*refs):
        if res is None:
            a_ref, w_ref, o_ref = refs
            acc = _dot(a_ref[...], w_ref[...])
        else:
            a_ref, w_ref, r_ref, o_ref = refs
            acc = r_ref[...] + _dot(a_ref[...], w_ref[...])
        o_ref[...] = acc.astype(o_ref.dtype)

    in_specs = [pl.BlockSpec((tm, k), lambda j, i: (i, a_blk)), pl.BlockSpec((k, tn), lambda j, i: (0, j))]
    args = [a, w]
    if res is not None:
        in_specs.append(pl.BlockSpec((tm, tn), lambda j, i: (i, j)))
        args.append(res)
    return pl.pallas_call(
        body, grid=(n // tn, m // tm), in_specs=in_specs, out_specs=pl.BlockSpec((tm, tn), lambda j, i: (i, j)),
        out_shape=jax.ShapeDtypeStruct((m, n), out_dtype), compiler_params=_params(("parallel", "parallel")), name=name,
    )(*args)


def _mm_nt(a, w, name):
    m, n = a.shape
    k = w.shape[0]
    tm, tk = _matmul_rows(m), _col_tile(k)

    def body(a_ref, w_ref, o_ref):
        o_ref[...] = _dot_nt(a_ref[...], w_ref[...])

    return pl.pallas_call(
        body, grid=(k // tk, m // tm),
        in_specs=[pl.BlockSpec((tm, n), lambda j, i: (i, 0)), pl.BlockSpec((tk, n), lambda j, i: (j, 0))],
        out_specs=pl.BlockSpec((tm, tk), lambda j, i: (i, j)),
        out_shape=jax.ShapeDtypeStruct((m, k), F32), compiler_params=_params(("parallel", "parallel")), name=name,
    )(a, w)


def _mm_tn(x, dy, name):
    m, k = x.shape
    n = dy.shape[1]
    tm, tn = _matmul_rows(m), _col_tile(n)

    def body(x_ref, dy_ref, o_ref):
        @pl.when(pl.program_id(1) == 0)
        def _():
            o_ref[...] = jnp.zeros_like(o_ref)

        o_ref[...] += _dot_tn(x_ref[...], dy_ref[...])

    return pl.pallas_call(
        body, grid=(n // tn, m // tm),
        in_specs=[pl.BlockSpec((tm, k), lambda j, i: (i, 0)), pl.BlockSpec((tm, tn), lambda j, i: (i, j))],
        out_specs=pl.BlockSpec((k, tn), lambda j, i: (0, j)),
        out_shape=jax.ShapeDtypeStruct((k, n), F32), compiler_params=_params(("parallel", "arbitrary")), name=name,
    )(x, dy)


def _silu_parts(g):
    s = 1.0 / (1.0 + jnp.exp(-g))
    return g * s, s * (1.0 + g * (1.0 - s))


def _gate_fwd(o_parts, proj, name):
    t = proj.shape[0]
    tm = _row_tile(t)
    w = D_MODEL // len(o_parts)

    def body(*refs):
        g_ref, o_ref = refs[-2], refs[-1]
        for p, r in enumerate(refs[:-2]):
            sil, _ = _silu_parts(g_ref[:, p * w:(p + 1) * w])
            o_ref[:, p * w:(p + 1) * w] = (r[...].astype(F32) * sil).astype(o_ref.dtype)

    return pl.pallas_call(
        body, grid=(t // tm,),
        in_specs=[pl.BlockSpec((tm, w), lambda i: (i, 0)) for _ in o_parts] + [pl.BlockSpec((tm, D_MODEL), lambda i: (i, 0))],
        out_specs=pl.BlockSpec((tm, D_MODEL), lambda i: (i, 0)),
        out_shape=jax.ShapeDtypeStruct((t, D_MODEL), BF16), compiler_params=_params(("parallel",)), name=name,
    )(*o_parts, proj)


def _gate_bwd(dao, o_parts, proj, name):
    t = proj.shape[0]
    tm = _row_tile(t)
    np_ = len(o_parts)
    w = D_MODEL // np_

    def body(*refs):
        dao_ref, g_ref = refs[0], refs[1 + np_]
        do_refs, dg_ref = refs[2 + np_:2 + 2 * np_], refs[-1]
        for p in range(np_):
            sl = slice(p * w, (p + 1) * w)
            sil, dsil = _silu_parts(g_ref[:, sl])
            da = dao_ref[:, sl]
            do_refs[p][...] = da * sil
            dg_ref[:, sl] = (da * refs[1 + p][...].astype(F32) * dsil).astype(dg_ref.dtype)

    full = pl.BlockSpec((tm, D_MODEL), lambda i: (i, 0))
    part = pl.BlockSpec((tm, w), lambda i: (i, 0))
    outs = pl.pallas_call(
        body, grid=(t // tm,), in_specs=[full] + [part] * np_ + [full], out_specs=[part] * np_ + [full],
        out_shape=[jax.ShapeDtypeStruct((t, w), F32)] * np_ + [jax.ShapeDtypeStruct((t, D_MODEL), BF16)],
        compiler_params=_params(("parallel",)), name=name,
    )(dao, *o_parts, proj)
    return outs[:np_], outs[np_]


def _loss_head(h2, gf, target, b, lp):
    d = h2.shape[1]
    nb = lp // BLOCK
    h3 = h2.reshape(b, lp, d)

    def body(h_ref, g_ref, t_ref, dh_ref, dg_ref, loss_ref):
        first = (pl.program_id(0) == 0) & (pl.program_id(1) == 0)

        @pl.when(first)
        def _():
            dg_ref[...] = jnp.zeros_like(dg_ref)
            loss_ref[...] = jnp.zeros_like(loss_ref)

        @pl.when(pl.program_id(1) == 0)
        def _():
            dh_ref[...] = jnp.zeros_like(dh_ref)

        @pl.when(pl.program_id(1) > 0)
        def _():
            x = h_ref[0]
            r = lax.rsqrt(jnp.mean(x * x, axis=-1, keepdims=True) + NORM_EPS)
            xr = x * r
            g = g_ref[...]
            diff = xr * g - t_ref[0]
            loss_ref[...] += 0.5 * jnp.sum(jnp.mean(diff * diff, axis=-1, keepdims=True))
            dy = diff * (1.0 / d)
            u = dy * g
            dh_ref[0] = r * (u - xr * jnp.mean(u * xr, axis=-1, keepdims=True))
            dg_ref[...] += jnp.sum(dy * xr, axis=0, keepdims=True)

    blk = pl.BlockSpec((1, BLOCK, d), lambda bi, n: (bi, n, 0))
    dh, dg, loss = pl.pallas_call(
        body, grid=(b, nb),
        in_specs=[blk, pl.BlockSpec((1, d), lambda bi, n: (0, 0)),
                  pl.BlockSpec((1, BLOCK, d), lambda bi, n: (bi, jnp.maximum(n - 1, 0), 0))],
        out_specs=[blk, pl.BlockSpec((1, d), lambda bi, n: (0, 0)), pl.BlockSpec((8, 128), lambda bi, n: (0, 0))],
        out_shape=[jax.ShapeDtypeStruct((b, lp, d), F32), jax.ShapeDtypeStruct((1, d), F32), jax.ShapeDtypeStruct((8, 128), F32)],
        compiler_params=_params(("arbitrary", "arbitrary")), name="loss_head",
    )(h3, gf, target)
    return dh.reshape(b * lp, d), dg, loss


def _iota2(shape, dim):
    return lax.broadcasted_iota(jnp.int32, shape, dim)


KEYS = 512


def _lo_lanes():
    return _iota2((1, BLOCK), 1) < HEAD


def _halves(x, lo):
    zero = jnp.zeros_like(x)
    return jnp.where(lo, x, zero), jnp.where(lo, zero, x)


def _rows_of_pair(a, b):
    return jnp.where(_iota2((BLOCK, 1), 0) < HEAD, a, b)


def _split_rows_t(x):
    xt = x.T
    first = _iota2(xt.shape, 0) < HEAD
    zero = jnp.zeros_like(xt)
    return jnp.concatenate([jnp.where(first, xt, zero), jnp.where(first, zero, xt)], axis=1).astype(BF16)


def _key_chunk(c, lp, t_idx, strict, key_axis):
    first = c * KEYS
    s0 = pl.multiple_of(jnp.minimum(first, lp - KEYS), BLOCK)
    s_idx = s0 + _iota2(t_idx.shape, key_axis)
    seen = (s_idx < t_idx) if strict else (s_idx <= t_idx)
    return s0, seen & (s_idx >= jnp.maximum(first, N_PAD))


def _split_dot(x, tri):
    hi = x.astype(BF16)
    lo = (x - hi.astype(F32)).astype(BF16)
    return jnp.dot(hi, tri, preferred_element_type=F32) + jnp.dot(lo, tri, preferred_element_type=F32)


def _stack_halves(x, lo):
    a, b = _halves(x, lo)
    return jnp.concatenate([a, b], axis=0)


def _pair(a, b, lo):
    return jnp.where(lo, a, b)


def _chunk_starts(lp):
    return [min(c * KEYS, lp - KEYS) for c in range(-(-lp // KEYS))]


def _put_rows(ref, r0, bq, a, b):
    for t in range(bq // BLOCK):
        part = slice(t * BLOCK, (t + 1) * BLOCK)
        ref[0, 0, r0 // BLOCK + t] = jnp.concatenate([a[:, part], b[:, part], jnp.zeros((6, BLOCK), F32)], axis=0)


def _get_rows(ref, r0, bq):
    return [jnp.concatenate([ref[0, 0, r0 // BLOCK + t, h:h + 1, :] for t in range(bq // BLOCK)], axis=1) for h in range(2)]


def _n_chunks(i):
    return ((i + 1) * BLOCK + KEYS - 1) // KEYS


QROWS = 512


def _for_query_tiles(nb, tile):
    per = QROWS // BLOCK

    def step(j, _):
        tile(pl.multiple_of(j * QROWS, QROWS), QROWS, (j + 1) * (QROWS // KEYS))
        return 0

    lax.fori_loop(0, nb // per, step, 0)
    for i in range(nb - nb % per, nb):
        tile(i * BLOCK, BLOCK, _n_chunks(i))


def _walk_chunks(r0, n, chunk, carry, leftwards=False):
    diag = jnp.maximum(r0 // KEYS, 1)

    def span(first, last, masked, carry):
        def step(t, cr):
            return chunk(last - 1 - t if leftwards else first + t, cr, masked)
        return lax.fori_loop(0, last - first, step, carry)

    spans = [(0, 1, True), (1, diag, False), (diag, n, True)]
    for first, last, masked in (reversed(spans) if leftwards else spans):
        carry = span(first, last, masked, carry)
    return carry


def _where(valid, x, other):
    return x if valid is None else jnp.where(valid, x, other)


HEAD_SCALE = HEAD ** -0.5
assert math.frexp(HEAD_SCALE)[0] == 0.5


def _sb_scores(q_h, k, valid, after):
    z = _dot_nt(q_h, k)
    lb = jnp.minimum(z, 0.0) - jnp.log(1.0 + jnp.exp(-jnp.abs(z)))
    l1m_all = lb - z
    l1m = _where(valid, l1m_all, 0.0)
    return lb, l1m_all, l1m, _split_dot(l1m, after)


def _pair_stat_spec(nb):
    return pl.BlockSpec((1, 1, nb, 8, BLOCK), lambda bi, hp: (bi, hp, 0, 0, 0))


def _sb_fwd(proj3, name):
    b, lp, _ = proj3.shape
    nb = lp // BLOCK
    npair = SB_HEADS // 2

    def body(q_ref, k_ref, v_ref, o_ref, tot_ref):
        lo = _lo_lanes()
        after = (_iota2((KEYS, KEYS), 0) > _iota2((KEYS, KEYS), 1)).astype(BF16)

        def qtile(r0, bq, n):
            qs = _halves((q_ref[0, pl.ds(r0, bq), :] * HEAD_SCALE).astype(BF16), lo)
            t_idx = r0 + _iota2((bq, KEYS), 0)

            def kchunk(c, carry, masked):
                cs, acc = carry[:2], carry[2]
                s0, valid = _key_chunk(c, lp, t_idx, True, 1)
                valid = valid if masked else None
                k = k_ref[0, pl.ds(s0, KEYS), :].astype(BF16)
                a_s, new = [], []
                for h in range(2):
                    lb, _, l1m, suf = _sb_scores(qs[h], k, valid, after)
                    a_s.append(_where(valid, jnp.exp(lb + suf + cs[h]), 0.0).astype(BF16))
                    new.append(cs[h] + jnp.sum(l1m, axis=1, keepdims=True))
                v_bd = _stack_halves(v_ref[0, pl.ds(s0, KEYS), :].astype(BF16), lo)
                return (*new, acc + jnp.dot(jnp.concatenate(a_s, axis=1), v_bd, preferred_element_type=F32))

            zero = jnp.zeros((bq, 1), F32)
            c_a, c_b, acc = _walk_chunks(r0, n, kchunk, (zero, zero, jnp.zeros((bq, BLOCK), F32)), leftwards=True)
            o_ref[0, pl.ds(r0, bq), :] = acc
            tot_ref[0, pl.ds(r0, bq), :] = jnp.broadcast_to(_pair(c_a, c_b, lo), (bq, BLOCK))

        _for_query_tiles(nb, qtile)

    def col(first):
        return pl.BlockSpec((1, lp, 2 * HEAD), lambda bi, hp: (bi, 0, first // (2 * HEAD) + hp))

    shp = jax.ShapeDtypeStruct((b, lp, SB_HEADS * HEAD), F32)
    return pl.pallas_call(
        body, grid=(b, npair), in_specs=[col(EV_Q), col(EV_K), col(EV_V)], out_specs=[col(0), col(0)], out_shape=[shp, shp],
        compiler_params=_params(("parallel", "parallel")), name=name,
    )(proj3, proj3, proj3)


def _sb_bwd(proj3, tot, do, name):
    b, lp, _ = proj3.shape
    nb = lp // BLOCK
    npair = SB_HEADS // 2

    def body(q_ref, k_ref, v_ref, tot_ref, do_ref, dq_ref, dk_ref, dv_ref):
        lo = _lo_lanes()
        after = (_iota2((KEYS, KEYS), 0) > _iota2((KEYS, KEYS), 1)).astype(BF16)
        before = (_iota2((KEYS, KEYS), 0) < _iota2((KEYS, KEYS), 1)).astype(BF16)
        dk_ref[...] = jnp.zeros_like(dk_ref)
        dv_ref[...] = jnp.zeros_like(dv_ref)

        def qtile(r0, bq, n):
            rows = pl.ds(r0, bq)
            qs = _halves((q_ref[0, rows, :] * HEAD_SCALE).astype(BF16), lo)
            dos = _halves(do_ref[0, rows, :].astype(BF16), lo)
            tot_i = tot_ref[0, rows, :]
            tots = (tot_i[:, 0:1], tot_i[:, HEAD:HEAD + 1])
            q_st, do_st = jnp.concatenate(qs, axis=0), jnp.concatenate(dos, axis=0)
            t_idx = r0 + _iota2((bq, KEYS), 0)

            def kchunk(c, carry, masked):
                s0, valid = _key_chunk(c, lp, t_idx, True, 1)
                valid = valid if masked else None
                keys = pl.ds(s0, KEYS)
                k = k_ref[0, keys, :].astype(BF16)
                v = v_ref[0, keys, :].astype(BF16)
                a_s, dzs, new = [], [], []
                for h in range(2):
                    left, pre = carry[2 * h], carry[2 * h + 1]
                    lb, l1m_all, l1m, suf = _sb_scores(qs[h], k, valid, after)
                    here = jnp.sum(l1m, axis=1, keepdims=True)
                    a = _where(valid, jnp.exp(lb + suf + (tots[h] - left - here)), 0.0)
                    w = a * _dot_nt(dos[h], v)
                    dz = _where(valid, w * jnp.exp(l1m_all) - (pre + _split_dot(w, before)) * jnp.exp(lb), 0.0)
                    new += [left + here, pre + jnp.sum(w, axis=1, keepdims=True)]
                    a_s.append(a.astype(BF16))
                    dzs.append(dz.astype(BF16))
                dk_ref[0, keys, :] += _dot_tn(jnp.concatenate(dzs, axis=0), q_st)
                dv_ref[0, keys, :] += _dot_tn(jnp.concatenate(a_s, axis=0), do_st)
                dq = carry[4] + jnp.dot(jnp.concatenate(dzs, axis=1), _stack_halves(k, lo), preferred_element_type=F32)
                return (*new, dq)

            zero = jnp.zeros((bq, 1), F32)
            out = _walk_chunks(r0, n, kchunk, (zero, zero, zero, zero, jnp.zeros((bq, BLOCK), F32)))
            dq_ref[0, rows, :] = out[4] * HEAD_SCALE

        _for_query_tiles(nb, qtile)

    def col(first):
        return pl.BlockSpec((1, lp, 2 * HEAD), lambda bi, hp: (bi, 0, first // (2 * HEAD) + hp))

    shp = jax.ShapeDtypeStruct((b, lp, SB_HEADS * HEAD), F32)
    return pl.pallas_call(
        body, grid=(b, npair), in_specs=[col(EV_Q), col(EV_K), col(EV_V), col(0), col(0)], out_specs=[col(0)] * 3, out_shape=[shp] * 3,
        compiler_params=_params(("parallel", "parallel")), name=name,
    )(proj3, proj3, proj3, tot, do)


def _rope_tables(lp):
    half = MLA_ROPE // 2
    pos = (np.arange(lp) - N_PAD).astype(np.float32)
    inv = jnp.asarray(ROPE_BASE, F32) ** (-jnp.arange(half, dtype=F32) / half)
    ang = jnp.asarray(pos)[:, None] * inv[None, :]
    cos, sin = jnp.cos(ang), jnp.sin(ang)
    zeros = lambda n: jnp.zeros((lp, n), F32)
    c = jnp.concatenate([jnp.ones((lp, MLA_NOPE), F32), cos, cos, zeros(32)], axis=1)
    s1 = jnp.concatenate([zeros(MLA_NOPE), -sin, zeros(half), zeros(32)], axis=1)
    s2 = jnp.concatenate([zeros(MLA_NOPE), zeros(half), sin, zeros(32)], axis=1)
    return c, s1, s2


def _rope(x, c, s1, s2):
    half = MLA_ROPE // 2
    return x * c + pltpu.roll(x, BLOCK - half, 1) * s1 + pltpu.roll(x, half, 1) * s2


def _rope_t(dy, c, s1, s2):
    half = MLA_ROPE // 2
    return dy * c + pltpu.roll(dy * s1, half, 1) + pltpu.roll(dy * s2, BLOCK - half, 1)


def _rms_rows(x, g):
    r = lax.rsqrt(jnp.mean(x * x, axis=-1, keepdims=True) + NORM_EPS)
    return x * r, r


def _mla_prep_fwd(proj3, gq, gkv, wq, wk, wv, tabs, name):
    b, lp, _ = proj3.shape
    nb = lp // BLOCK
    hw = MLA_HEADS * BLOCK

    def body(cq_ref, ckv_ref, kr_ref, gq_ref, gkv_ref, wq_ref, wk_ref, wv_ref, c_ref, s1_ref, s2_ref, qf_ref, kf_ref, v_ref):
        c, s1, s2 = c_ref[...], s1_ref[...], s2_ref[...]
        xq, _ = _rms_rows(cq_ref[0], None)
        qh = _dot(xq * gq_ref[...], wq_ref[...])
        xk, _ = _rms_rows(ckv_ref[0], None)
        ckv_n = xk * gkv_ref[...]
        kv = _dot(ckv_n, wk_ref[...])
        v_ref[0] = _dot(ckv_n, wv_ref[...]).astype(v_ref.dtype)
        kr = _rope(kr_ref[0], c, s1, s2)
        for h in range(MLA_HEADS):
            ls = slice(h * BLOCK, (h + 1) * BLOCK)
            qf_ref[0, :, ls] = _rope(qh[:, ls], c, s1, s2).astype(qf_ref.dtype)
            kf_ref[0, :, ls] = (kv[:, ls] + kr).astype(kf_ref.dtype)

    def col(first, width):
        return pl.BlockSpec((1, BLOCK, width), lambda bi, n: (bi, n, first // width))

    def whole(a):
        return pl.BlockSpec(a.shape, lambda bi, n: (0,) * a.ndim)

    tab = pl.BlockSpec((BLOCK, BLOCK), lambda bi, n: (n, 0))
    return pl.pallas_call(
        body, grid=(b, nb),
        in_specs=[col(EV_CQ, MLA_Q_LORA), col(EV_CKV, MLA_KV_LORA), col(EV_KR, BLOCK), whole(gq), whole(gkv), whole(wq), whole(wk),
                  whole(wv), tab, tab, tab],
        out_specs=[col(0, hw), col(0, hw), col(0, MLA_HEADS * HEAD)],
        out_shape=[jax.ShapeDtypeStruct((b, lp, hw), BF16), jax.ShapeDtypeStruct((b, lp, hw), BF16),
                   jax.ShapeDtypeStruct((b, lp, MLA_HEADS * HEAD), BF16)],
        compiler_params=_params(("parallel", "parallel")), name=name,
    )(proj3, proj3, proj3, gq, gkv, wq, wk, wv, *tabs)


def _mla_prep_bwd(proj3, gq, gkv, wq, wk, wv, tabs, dqf, dkf, dv, name):
    b, lp, _ = proj3.shape
    nb = lp // BLOCK
    hw = MLA_HEADS * BLOCK

    def body(cq_ref, ckv_ref, gq_ref, gkv_ref, wq_ref, wk_ref, wv_ref, c_ref, s1_ref, s2_ref, dqf_ref, dkf_ref, dv_ref,
             dcq_ref, dckv_ref, dkr_ref, dwq_ref, dwk_ref, dwv_ref, dgq_ref, dgkv_ref, dqh):
        @pl.when((pl.program_id(0) == 0) & (pl.program_id(1) == 0))
        def _():
            for r in (dwq_ref, dwk_ref, dwv_ref, dgq_ref, dgkv_ref):
                r[...] = jnp.zeros_like(r)

        c, s1, s2 = c_ref[...], s1_ref[...], s2_ref[...]
        dkr = jnp.zeros((BLOCK, BLOCK), F32)
        for h in range(MLA_HEADS):
            ls = slice(h * BLOCK, (h + 1) * BLOCK)
            dqh[:, ls] = _rope_t(dqf_ref[0, :, ls].astype(F32), c, s1, s2).astype(dqh.dtype)
            dkr = dkr + dkf_ref[0, :, ls].astype(F32)
        dkr_ref[0] = _rope_t(dkr, c, s1, s2).astype(dkr_ref.dtype)

        def norm_bwd(x, g, dy, dg_ref):
            xr, r = _rms_rows(x, None)
            u = dy * g
            dg_ref[...] += jnp.sum(dy * xr, axis=0, keepdims=True)
            return r * (u - xr * jnp.mean(u * xr, axis=-1, keepdims=True))

        xq, _ = _rms_rows(cq_ref[0], None)
        cq_n = xq * gq_ref[...]
        dwq_ref[...] += _dot_tn(cq_n, dqh[...])
        dcq_ref[0] = norm_bwd(cq_ref[0], gq_ref[...], _dot_nt(dqh[...], wq_ref[...]), dgq_ref).astype(dcq_ref.dtype)
        xk, _ = _rms_rows(ckv_ref[0], None)
        ckv_n = xk * gkv_ref[...]
        dkf_, dv_ = dkf_ref[0], dv_ref[0]
        dwk_ref[...] += _dot_tn(ckv_n, dkf_)
        dwv_ref[...] += _dot_tn(ckv_n, dv_)
        dckv_n = _dot_nt(dkf_, wk_ref[...]) + _dot_nt(dv_, wv_ref[...])
        dckv_ref[0] = norm_bwd(ckv_ref[0], gkv_ref[...], dckv_n, dgkv_ref).astype(dckv_ref.dtype)

    def col(first, width):
        return pl.BlockSpec((1, BLOCK, width), lambda bi, n: (bi, n, first // width))

    def whole(a):
        return pl.BlockSpec(a.shape, lambda bi, n: (0,) * len(a.shape))

    tab = pl.BlockSpec((BLOCK, BLOCK), lambda bi, n: (n, 0))
    acc_shapes = [jax.ShapeDtypeStruct(a.shape, F32) for a in (wq, wk, wv, gq, gkv)]
    return pl.pallas_call(
        body, grid=(b, nb),
        in_specs=[col(EV_CQ, MLA_Q_LORA), col(EV_CKV, MLA_KV_LORA), whole(gq), whole(gkv), whole(wq), whole(wk), whole(wv), tab, tab, tab,
                  col(0, hw), col(0, hw), col(0, MLA_HEADS * HEAD)],
        out_specs=[col(0, MLA_Q_LORA), col(0, MLA_KV_LORA), col(0, BLOCK)] + [whole(a) for a in acc_shapes],
        out_shape=[jax.ShapeDtypeStruct((b, lp, MLA_Q_LORA), BF16), jax.ShapeDtypeStruct((b, lp, MLA_KV_LORA), BF16),
                   jax.ShapeDtypeStruct((b, lp, BLOCK), BF16)] + acc_shapes,
        scratch_shapes=[pltpu.VMEM((BLOCK, hw), BF16)],
        compiler_params=_params(("arbitrary", "arbitrary")), name=name,
    )(proj3, proj3, gq, gkv, wq, wk, wv, *tabs, dqf, dkf, dv)


def _mla_fwd(qf, kf, v, name):
    b, lp, _ = qf.shape
    nb = lp // BLOCK
    npair = MLA_HEADS // 2
    scale = (MLA_NOPE + MLA_ROPE) ** -0.5
    starts = _chunk_starts(lp)

    def body(q_ref, k_ref, v_ref, o_ref, lse_ref, vt_ref):
        for c, s0 in enumerate(starts):
            vt_ref[c] = _split_rows_t(v_ref[0, s0:s0 + KEYS, :].astype(F32))

        def qtile(r0, bq, n):
            qs = [q_ref[0, pl.ds(r0, bq), h * BLOCK:(h + 1) * BLOCK] for h in range(2)]
            t_idx = r0 + _iota2((KEYS, bq), 1)

            def kchunk(c, carry, masked):
                stats, acc = carry[:4], carry[4]
                s0, valid = _key_chunk(c, lp, t_idx, False, 0)
                valid = valid if masked else None
                ps, new, alphas = [], [], []
                for h in range(2):
                    m, l = stats[2 * h], stats[2 * h + 1]
                    s = _where(valid, _dot_nt(k_ref[0, pl.ds(s0, KEYS), h * BLOCK:(h + 1) * BLOCK], qs[h]) * scale, NEG)
                    m_new = jnp.maximum(m, jnp.max(s, axis=0, keepdims=True))
                    p = _where(valid, jnp.exp(s - m_new), 0.0)
                    alpha = jnp.exp(m - m_new)
                    new += [m_new, alpha * l + jnp.sum(p, axis=0, keepdims=True)]
                    alphas.append(alpha)
                    ps.append(p.astype(BF16))
                pv = jnp.dot(vt_ref[c], jnp.concatenate(ps, axis=0), preferred_element_type=F32)
                return (*new, _rows_of_pair(alphas[0], alphas[1]) * acc + pv)

            neg, zero = jnp.full((1, bq), NEG, F32), jnp.zeros((1, bq), F32)
            m_a, l_a, m_b, l_b, acc = _walk_chunks(r0, n, kchunk, (neg, zero, neg, zero, jnp.zeros((BLOCK, bq), F32)))
            safe = [jnp.where(l > 0.0, l, 1.0) for l in (l_a, l_b)]
            o_ref[0, pl.ds(r0, bq), :] = (acc / _rows_of_pair(safe[0], safe[1])).T
            lse = [jnp.where(l > 0.0, m + jnp.log(sf), 0.0) for m, l, sf in ((m_a, l_a, safe[0]), (m_b, l_b, safe[1]))]
            _put_rows(lse_ref, r0, bq, lse[0], lse[1])

        _for_query_tiles(nb, qtile)

    wide = pl.BlockSpec((1, lp, 2 * BLOCK), lambda bi, hp: (bi, 0, hp))
    thin = pl.BlockSpec((1, lp, 2 * HEAD), lambda bi, hp: (bi, 0, hp))
    return pl.pallas_call(
        body, grid=(b, npair), in_specs=[wide, wide, thin], out_specs=[thin, _pair_stat_spec(nb)],
        out_shape=[jax.ShapeDtypeStruct((b, lp, MLA_HEADS * HEAD), F32), jax.ShapeDtypeStruct((b, npair, nb, 8, BLOCK), F32)],
        scratch_shapes=[pltpu.VMEM((len(starts), BLOCK, 2 * KEYS), BF16)],
        compiler_params=_params(("parallel", "parallel")), name=name,
    )(qf, kf, v)


def _mla_bwd(qf, kf, v, o, lse, do, name):
    b, lp, _ = qf.shape
    nb = lp // BLOCK
    npair = MLA_HEADS // 2
    scale = (MLA_NOPE + MLA_ROPE) ** -0.5

    starts = _chunk_starts(lp)

    def body(q_ref, k_ref, v_ref, o_ref, lse_ref, do_ref, dq_ref, dk_ref, dv_ref, kt_ref):
        lo = _lo_lanes()
        dk_ref[...] = jnp.zeros_like(dk_ref)
        dv_ref[...] = jnp.zeros_like(dv_ref)
        for c, s0 in enumerate(starts):
            for h in range(2):
                kt_ref[c, h] = k_ref[0, s0:s0 + KEYS, h * BLOCK:(h + 1) * BLOCK].astype(F32).T.astype(BF16)

        def qtile(r0, bq, n):
            rows = pl.ds(r0, bq)
            qs = [q_ref[0, rows, h * BLOCK:(h + 1) * BLOCK] for h in range(2)]
            do_i = do_ref[0, rows, :]
            dos = _halves(do_i.astype(BF16), lo)
            do_st = jnp.concatenate(dos, axis=0)
            both = (do_i * o_ref[0, rows, :]).T
            dsum = (jnp.sum(both[:HEAD], axis=0, keepdims=True), jnp.sum(both[HEAD:], axis=0, keepdims=True))
            lses = _get_rows(lse_ref, r0, bq)
            t_idx = r0 + _iota2((KEYS, bq), 1)

            def kchunk(c, dqts, masked):
                s0, valid = _key_chunk(c, lp, t_idx, False, 0)
                valid = valid if masked else None
                keys = pl.ds(s0, KEYS)
                v_c = v_ref[0, keys, :]
                ps, out = [], []
                for h in range(2):
                    lanes = slice(h * BLOCK, (h + 1) * BLOCK)
                    s = _dot_nt(k_ref[0, keys, lanes], qs[h]) * scale
                    p = _where(valid, jnp.exp(s - lses[h]), 0.0)
                    ds = (p * (_dot_nt(v_c, dos[h]) - dsum[h]) * scale).astype(BF16)
                    dk_ref[0, keys, lanes] += jnp.dot(ds, qs[h], preferred_element_type=F32)
                    out.append(dqts[h] + jnp.dot(kt_ref[c, h], ds, preferred_element_type=F32))
                    ps.append(p.astype(BF16))
                dv_ref[0, keys, :] += jnp.dot(jnp.concatenate(ps, axis=1), do_st, preferred_element_type=F32)
                return tuple(out)

            zero = jnp.zeros((BLOCK, bq), F32)
            dq_a, dq_b = _walk_chunks(r0, n, kchunk, (zero, zero))
            dq_ref[0, rows, 0:BLOCK] = dq_a.T
            dq_ref[0, rows, BLOCK:2 * BLOCK] = dq_b.T

        _for_query_tiles(nb, qtile)

    wide = pl.BlockSpec((1, lp, 2 * BLOCK), lambda bi, hp: (bi, 0, hp))
    thin = pl.BlockSpec((1, lp, 2 * HEAD), lambda bi, hp: (bi, 0, hp))
    return pl.pallas_call(
        body, grid=(b, npair), in_specs=[wide, wide, thin, thin, _pair_stat_spec(nb), thin], out_specs=[wide, wide, thin],
        out_shape=[jax.ShapeDtypeStruct(qf.shape, F32), jax.ShapeDtypeStruct(qf.shape, F32), jax.ShapeDtypeStruct(v.shape, F32)],
        scratch_shapes=[pltpu.VMEM((len(starts), 2, BLOCK, KEYS), BF16)],
        compiler_params=_params(("parallel", "parallel")), name=name,
    )(qf, kf, v, o, lse, do)


def _swa_keys(k_ref, v_ref, n, kv):
    prev = jnp.maximum(n - 1, 0)
    rows = lambda blk: pl.ds(pl.multiple_of(blk * BLOCK, BLOCK), BLOCK)
    mine = (_iota2((1, BLOCK), 1) >= HEAD).astype(jnp.int32) == kv

    def both_halves(ref):
        x = jnp.concatenate([ref[0, rows(prev), :], ref[0, rows(n), :], ref[0, 0:BLOCK, :]], axis=0)
        return jnp.where(mine, x, pltpu.roll(x, HEAD, 1))

    slot = _iota2((3 * BLOCK, BLOCK), 0)
    loc = slot % BLOCK
    s_idx = jnp.where(slot < BLOCK, (n - 1) * BLOCK + loc, jnp.where(slot < 2 * BLOCK, n * BLOCK + loc, loc))
    dist = n * BLOCK + _iota2((3 * BLOCK, BLOCK), 1) - s_idx
    band = (slot < 2 * BLOCK) & (dist >= 0) & (dist < SWA_WINDOW) & (s_idx >= BLOCK)
    meta = (slot >= 2 * BLOCK) & (s_idx >= N_PAD) & (dist >= 0)
    return both_halves(k_ref), both_halves(v_ref), band | meta, dist.astype(F32), prev


def _swa_probs(q_h, kdup, valid, dist, head, sink_ref):
    slope = jnp.exp(jnp.full((1, 1), -8.0 * math.log(2.0) / SWA_HEADS, F32) * (head + 1).astype(F32))
    s = jnp.where(valid, _dot_nt(kdup, q_h) - slope * dist, NEG)
    sink = sink_ref[pl.ds(head, 1), 0:1]
    m = jnp.maximum(jnp.max(s, axis=0, keepdims=True), sink)
    e = jnp.where(valid, jnp.exp(s - m), 0.0)
    es = jnp.exp(sink - m)
    inv = 1.0 / (jnp.sum(e, axis=0, keepdims=True) + es)
    return e * inv, es * inv


SWA_PAIRS = SWA_HEADS // SWA_KV_HEADS // 2
SWA_GROUP = SWA_PAIRS * 2 * HEAD


def _swa_specs(b, lp):
    nb = lp // BLOCK
    qcol = lambda first: pl.BlockSpec((1, BLOCK, SWA_GROUP), lambda bi, kv, n: (bi, n, first // SWA_GROUP + kv))
    kcol = lambda first: pl.BlockSpec((1, lp, BLOCK), lambda bi, kv, n: (bi, 0, first // BLOCK))
    sink = pl.BlockSpec((SWA_HEADS, BLOCK), lambda bi, kv, n: (0, 0))
    return (b, SWA_KV_HEADS, nb), qcol, kcol, sink


def _swa_fwd(proj3, sinks, name):
    b, lp, _ = proj3.shape
    grid, qcol, kcol, sink = _swa_specs(b, lp)

    def body(q_ref, k_ref, v_ref, sink_ref, o_ref):
        kv, n = pl.program_id(1), pl.program_id(2)
        lo = _lo_lanes()
        kdup, vdup, valid, dist, _ = _swa_keys(k_ref, v_ref, n, kv)
        kdup = kdup.astype(BF16)
        vt = _split_rows_t(vdup)
        for p in range(SWA_PAIRS):
            lanes = slice(p * BLOCK, (p + 1) * BLOCK)
            qs = _halves((q_ref[0, :, lanes] * HEAD_SCALE).astype(BF16), lo)
            probs = [_swa_probs(qs[hh], kdup, valid, dist, (kv * SWA_PAIRS + p) * 2 + hh, sink_ref)[0].astype(BF16) for hh in range(2)]
            o_ref[0, :, lanes] = jnp.dot(vt, jnp.concatenate(probs, axis=0), preferred_element_type=F32).T

    return pl.pallas_call(
        body, grid=grid, in_specs=[qcol(OD_Q), kcol(OD_K), kcol(OD_V), sink], out_specs=qcol(0),
        out_shape=jax.ShapeDtypeStruct((b, lp, SWA_HEADS * HEAD), F32),
        compiler_params=_params(("parallel", "parallel", "parallel")), name=name,
    )(proj3, proj3, proj3, sinks)


def _swa_bwd(proj3, sinks, do, name):
    b, lp, _ = proj3.shape
    nb = lp // BLOCK
    grid, qcol, kcol, sink = _swa_specs(b, lp)

    def body(q_ref, k_ref, v_ref, sink_ref, do_ref, dq_ref, dk_ref, dv_ref, dsink_ref, dk_acc, dv_acc):
        kv, n = pl.program_id(1), pl.program_id(2)

        @pl.when((n == 0) & (pl.program_id(0) == 0) & (kv == 0))
        def _():
            dsink_ref[...] = jnp.zeros_like(dsink_ref)

        @pl.when(n == 0)
        def _():
            dk_acc[...] = jnp.zeros_like(dk_acc)
            dv_acc[...] = jnp.zeros_like(dv_acc)

        lo = _lo_lanes()
        kdup, vdup, valid, dist, prev = _swa_keys(k_ref, v_ref, n, kv)
        kt = _split_rows_t(kdup)
        kdup, vdup = kdup.astype(BF16), vdup.astype(BF16)
        dkc = jnp.zeros((3 * BLOCK, BLOCK), F32)
        dvc = jnp.zeros((3 * BLOCK, BLOCK), F32)
        for p in range(SWA_PAIRS):
            lanes = slice(p * BLOCK, (p + 1) * BLOCK)
            qs = _halves((q_ref[0, :, lanes] * HEAD_SCALE).astype(BF16), lo)
            dos = _halves(do_ref[0, :, lanes].astype(BF16), lo)
            dss, prs = [], []
            for hh in range(2):
                head = (kv * SWA_PAIRS + p) * 2 + hh
                pr, ps = _swa_probs(qs[hh], kdup, valid, dist, head, sink_ref)
                dp = _dot_nt(vdup, dos[hh])
                dsum = jnp.sum(pr * dp, axis=0, keepdims=True)
                dsink_ref[pl.ds(head, 1), :] += jnp.broadcast_to(-jnp.sum(ps * dsum, axis=1, keepdims=True), (1, BLOCK))
                dss.append((pr * (dp - dsum)).astype(BF16))
                prs.append(pr.astype(BF16))
            dq_ref[0, :, lanes] = jnp.dot(kt, jnp.concatenate(dss, axis=0), preferred_element_type=F32).T * HEAD_SCALE
            dkc = dkc + jnp.dot(jnp.concatenate(dss, axis=1), jnp.concatenate(qs, axis=0), preferred_element_type=F32)
            dvc = dvc + jnp.dot(jnp.concatenate(prs, axis=1), jnp.concatenate(dos, axis=0), preferred_element_type=F32)
        rows = lambda blk: pl.ds(pl.multiple_of(blk * BLOCK, BLOCK), BLOCK)
        for part, r in enumerate((rows(prev), rows(n), slice(0, BLOCK))):
            dk_acc[r, :] += dkc[part * BLOCK:(part + 1) * BLOCK]
            dv_acc[r, :] += dvc[part * BLOCK:(part + 1) * BLOCK]

        for acc, ref in ((dk_acc, dk_ref), (dv_acc, dv_ref)):
            @pl.when((n == nb - 1) & (kv == 0))
            def _():
                x = acc[...]
                ref[0] = x + pltpu.roll(x, HEAD, 1)

            @pl.when((n == nb - 1) & (kv == 1))
            def _():
                x = acc[...]
                ref[0] = jnp.where(lo, ref[0], x + pltpu.roll(x, HEAD, 1))

    kvout = pl.BlockSpec((1, lp, BLOCK), lambda bi, kv, n: (bi, 0, 0))
    kvshape = jax.ShapeDtypeStruct((b, lp, BLOCK), F32)
    return pl.pallas_call(
        body, grid=grid, in_specs=[qcol(OD_Q), kcol(OD_K), kcol(OD_V), sink, qcol(0)], out_specs=[qcol(0), kvout, kvout, sink],
        out_shape=[jax.ShapeDtypeStruct((b, lp, SWA_HEADS * HEAD), F32), kvshape, kvshape, jax.ShapeDtypeStruct((SWA_HEADS, BLOCK), F32)],
        scratch_shapes=[pltpu.VMEM((lp, BLOCK), F32), pltpu.VMEM((lp, BLOCK), F32)],
        compiler_params=_params(("arbitrary", "arbitrary", "arbitrary")), name=name,
    )(proj3, proj3, proj3, sinks, do)


def _kernel_weights(ev_w_in, ev_w_uq, ev_w_ukv, od_w_in):
    zeros = lambda r, c: jnp.zeros((r, c), ev_w_in.dtype)
    q_sb, k_sb, v_sb, g_sb, c_q, c_kv, k_r, g_mla = jnp.split(ev_w_in, [512, 1024, 1536, 2048, 2304, 2432, 2464], axis=1)
    w0 = jnp.concatenate([g_sb, g_mla, q_sb, k_sb, v_sb, c_q, c_kv, zeros(D_MODEL, MLA_NOPE), k_r, zeros(D_MODEL, 32)], axis=1)
    uq = ev_w_uq.reshape(MLA_Q_LORA, MLA_HEADS, MLA_NOPE + MLA_ROPE)
    wq = jnp.pad(uq, ((0, 0), (0, 0), (0, BLOCK - MLA_NOPE - MLA_ROPE))).reshape(MLA_Q_LORA, MLA_HEADS * BLOCK)
    ukv = ev_w_ukv.reshape(MLA_KV_LORA, MLA_HEADS, BLOCK)
    wk = jnp.pad(ukv[:, :, :MLA_NOPE], ((0, 0), (0, 0), (0, BLOCK - MLA_NOPE))).reshape(MLA_KV_LORA, MLA_HEADS * BLOCK)
    wv = ukv[:, :, MLA_NOPE:].reshape(MLA_KV_LORA, MLA_HEADS * HEAD)
    q, k, v, g = jnp.split(od_w_in, [1024, 1152, 1280], axis=1)
    w1 = jnp.concatenate([g, q, k, v], axis=1)
    return w0, wq, wk, wv, w1


def _original_grads(dw0, dwq, dwk, dwv, dw1):
    sl = lambda a, first, n: a[:, first:first + n]
    d_ev_w_in = jnp.concatenate([sl(dw0, EV_Q, 512), sl(dw0, EV_K, 512), sl(dw0, EV_V, 512), sl(dw0, EV_G, 512), sl(dw0, EV_CQ, 256),
                                 sl(dw0, EV_CKV, 128), sl(dw0, EV_KR + MLA_NOPE, MLA_ROPE), sl(dw0, EV_G + 512, 512)], axis=1)
    d_uq = dwq.reshape(MLA_Q_LORA, MLA_HEADS, BLOCK)[:, :, :MLA_NOPE + MLA_ROPE].reshape(MLA_Q_LORA, -1)
    d_ukv = jnp.concatenate([dwk.reshape(MLA_KV_LORA, MLA_HEADS, BLOCK)[:, :, :MLA_NOPE], dwv.reshape(MLA_KV_LORA, MLA_HEADS, HEAD)],
                            axis=2).reshape(MLA_KV_LORA, -1)
    d_od_w_in = jnp.concatenate([sl(dw1, OD_Q, 1024), sl(dw1, OD_K, 128), sl(dw1, OD_V, 128), sl(dw1, OD_G, 1024)], axis=1)
    return d_ev_w_in, d_uq, d_ukv, d_od_w_in


def _meta_rows_sum(dh0_3):
    b, _, d = dh0_3.shape

    def body(x_ref, o_ref):
        acc = x_ref[0, N_PAD:BLOCK, :]
        for i in range(1, b):
            acc = acc + x_ref[i, N_PAD:BLOCK, :]
        o_ref[...] = acc

    return pl.pallas_call(
        body, grid=(1,), in_specs=[pl.BlockSpec((b, BLOCK, d), lambda i: (0, 0, 0))], out_specs=pl.BlockSpec((N_META, d), lambda i: (0, 0)),
        out_shape=jax.ShapeDtypeStruct((N_META, d), F32), compiler_params=_params(("arbitrary",)), name="meta_rows_sum",
    )(dh0_3)


def _local_step(x, meta, norm_g, final_g, gq, gkv, sinks, target, ev_w_in, ev_w_uq, ev_w_ukv, wo0, od_w_in, wo1):
    b, seq, d = x.shape
    lp = seq + BLOCK
    t = b * lp
    w0, wq, wk, wv, w1 = _kernel_weights(ev_w_in, ev_w_uq, ev_w_ukv, od_w_in)
    h0 = jnp.concatenate([jnp.zeros((b, N_PAD, d), F32), jnp.broadcast_to(meta[None], (b, N_META, d)), x], axis=1).reshape(t, d)
    tabs = _rope_tables(lp)
    g0, g1 = norm_g[0:1], norm_g[1:2]

    hn0 = _rms_fwd(h0, g0, "norm0")
    proj0 = _mm(hn0, w0, "inproj0")
    p0 = proj0.reshape(b, lp, EV_N)
    o_sb, sb_tot = _sb_fwd(p0, "sb_fwd")
    qf, kf, v = _mla_prep_fwd(p0, gq, gkv, wq, wk, wv, tabs, "mla_prep_fwd")
    o_mla, lse = _mla_fwd(qf, kf, v, "mla_fwd")
    o0 = [o_sb.reshape(t, -1), o_mla.reshape(t, -1)]
    ao0 = _gate_fwd(o0, proj0, "gate0")
    h1 = _mm(ao0, wo0, "outproj0", res=h0)

    hn1 = _rms_fwd(h1, g1, "norm1")
    proj1 = _mm(hn1, w1, "inproj1")
    p1 = proj1.reshape(b, lp, OD_N)
    sinks_b = jnp.broadcast_to(sinks.reshape(SWA_HEADS, 1), (SWA_HEADS, BLOCK))
    o1 = _swa_fwd(p1, sinks_b, "swa_fwd").reshape(t, -1)
    ao1 = _gate_fwd([o1], proj1, "gate1")
    h2 = _mm(ao1, wo1, "outproj1", res=h1)

    dh2, d_final_g, loss = _loss_head(h2, final_g.reshape(1, d), target, b, lp)

    d_wo1 = _mm_tn(ao1, dh2, "d_wo1")
    dao1 = _mm_nt(dh2, wo1, "d_ao1")
    (do1,), dg1 = _gate_bwd(dao1, [o1], proj1, "gate1_bwd")
    dq1, dk4, dv4, d_sinks = _swa_bwd(p1, sinks_b, do1.reshape(b, lp, -1), "swa_bwd")
    unheads = lambda a: a.reshape(t, SWA_KV_HEADS * HEAD).astype(BF16)
    dproj1 = jnp.concatenate([dg1, dq1.reshape(t, -1).astype(BF16), unheads(dk4), unheads(dv4)], axis=1)
    d_w1 = _mm_tn(hn1, dproj1, "d_w1")
    dhn1 = _mm_nt(dproj1, w1, "d_hn1")
    dh1, d_g1 = _rms_bwd(h1, g1, dhn1, dh2, "norm1_bwd")

    d_wo0 = _mm_tn(ao0, dh1, "d_wo0")
    dao0 = _mm_nt(dh1, wo0, "d_ao0")
    (do_sb, do_mla), dg0 = _gate_bwd(dao0, o0, proj0, "gate0_bwd")
    dq_sb, dk_sb, dv_sb = _sb_bwd(p0, sb_tot, do_sb.reshape(b, lp, -1), "sb_bwd")
    dqf, dkf, dv = _mla_bwd(qf, kf, v, o_mla, lse, do_mla.reshape(b, lp, -1), "mla_bwd")
    dcq, dckv, dkr, d_wq, d_wk, d_wv, d_gq, d_gkv = _mla_prep_bwd(p0, gq, gkv, wq, wk, wv, tabs, dqf, dkf, dv, "mla_prep_bwd")
    flat = lambda a: a.reshape(t, -1).astype(BF16)
    dproj0 = jnp.concatenate([dg0, flat(dq_sb), flat(dk_sb), flat(dv_sb), flat(dcq), flat(dckv), flat(dkr)], axis=1)
    d_w0 = _mm_tn(hn0, dproj0, "d_w0")
    dhn0 = _mm_nt(dproj0, w0, "d_hn0")
    dh0, d_g0 = _rms_bwd(h0, g0, dhn0, dh1, "norm0_bwd")
    dh0 = dh0.reshape(b, lp, d)

    d_ev_w_in, d_uq, d_ukv, d_od_w_in = _original_grads(d_w0, d_wq, d_wk, d_wv, d_w1)
    grads = dict(meta=_meta_rows_sum(dh0), norm_g=jnp.concatenate([d_g0, d_g1], axis=0), final_g=d_final_g.reshape(d),
                 ev_w_in=d_ev_w_in, ev_q_norm_g=d_gq, ev_kv_norm_g=d_gkv, ev_w_uq=d_uq, ev_w_ukv=d_ukv, ev_w_out=d_wo0,
                 od_w_in=d_od_w_in, od_sinks=d_sinks[:, 0].reshape(1, SWA_HEADS), od_w_out=d_wo1)
    return loss, dh0[:, BLOCK:], grads


MESH = pl.DeviceIdType.MESH
ANY = pl.BlockSpec(memory_space=pl.ANY)


def _place():
    return lax.axis_index("x"), lax.axis_index("y"), lax.axis_index("c")


def _other_chips(x, y):
    return [(1 - x, y), (x, 1 - y), (1 - x, 1 - y)]


def _with_own_slot(slots, own):
    me = 2 * lax.axis_index("x") + lax.axis_index("y")
    return lax.dynamic_update_slice(slots, own[None], (me,) + (0,) * own.ndim)


def _gather_weights(packs, meta, name):
    n = len(packs)

    def body(*refs):
        ins, m_ref, outs, mo_ref = refs[:n], refs[n], refs[n + 1:2 * n + 1], refs[2 * n + 1]
        send_sems, recv_sems = refs[2 * n + 2:]
        x, y, c = _place()
        me, sib = 2 * x + y, (x, y, 1 - c)
        chips = _other_chips(x, y)

        def copy(k, src, dst, to):
            return pltpu.make_async_remote_copy(src_ref=src, dst_ref=dst, send_sem=send_sems.at[k], recv_sem=recv_sems.at[k], device_id=to,
                                                device_id_type=MESH)

        def half(i, chip, h):
            rows = packs[i].shape[0] // 2
            return outs[i].at[chip, pl.ds(h * rows, rows), :]

        def mine(i):
            rows = packs[i].shape[0] // 2
            return ins[i].at[pl.ds(c * rows, rows), :]

        sent = [copy(6 * i + k, mine(i), half(i, me, c), (px, py, c)) for i in range(n) for k, (px, py) in enumerate(chips)]
        sent += [copy(6 * n + k, m_ref, mo_ref.at[me], (px, py, c)) for k, (px, py) in enumerate(chips)]
        for cp in sent:
            cp.start()
        for i in range(n):
            for k, (px, py) in enumerate(chips):
                landed = half(i, 2 * px + py, c)
                copy(6 * i + k, mine(i), landed, (px, py, c)).wait_recv()
                fwd = copy(6 * i + 3 + k, landed, landed, sib)
                fwd.start()
                sent.append(fwd)
        for k, (px, py) in enumerate(chips):
            for i in range(n):
                other = half(i, 2 * px + py, 1 - c)
                copy(6 * i + 3 + k, other, other, sib).wait_recv()
            copy(6 * n + k, m_ref, mo_ref.at[2 * px + py], (px, py, c)).wait_recv()
        for cp in sent:
            cp.wait_send()

    nsem = 6 * n + 3
    res = pl.pallas_call(
        body, in_specs=[ANY] * (n + 1), out_specs=[ANY] * (n + 1),
        out_shape=[jax.ShapeDtypeStruct((N_CHIPS,) + a.shape, a.dtype) for a in list(packs) + [meta]],
        scratch_shapes=[pltpu.SemaphoreType.DMA((nsem,)), pltpu.SemaphoreType.DMA((nsem,))],
        name=name,
    )(*packs, meta)
    return [_with_own_slot(r, a) for r, a in zip(res[:n], packs)], _with_own_slot(res[n], meta)


def _grads_to_sibling(gs, name):
    n = len(gs)

    def body(*refs):
        ins, outs, send_sems, recv_sems = refs[:n], refs[n:2 * n], refs[2 * n], refs[2 * n + 1]
        x, y, c = _place()
        cps = []
        for i in range(n):
            rows = gs[i].shape[1] // 2
            cps.append(pltpu.make_async_remote_copy(src_ref=ins[i].at[:, pl.ds((1 - c) * rows, rows), :], dst_ref=outs[i],
                                                    send_sem=send_sems.at[i], recv_sem=recv_sems.at[i], device_id=(x, y, 1 - c),
                                                    device_id_type=MESH))
        for cp in cps:
            cp.start()
        for cp in cps:
            cp.wait()

    return pl.pallas_call(
        body, in_specs=[ANY] * n, out_specs=[ANY] * n,
        out_shape=[jax.ShapeDtypeStruct((g.shape[0], g.shape[1] // 2, g.shape[2]), g.dtype) for g in gs],
        scratch_shapes=[pltpu.SemaphoreType.DMA((n,)), pltpu.SemaphoreType.DMA((n,))],
        name=name,
    )(*gs)


def _share_halves(rs, name):
    n = len(rs)

    def body(*refs):
        ins, outs, send_sems, recv_sems = refs[:n], refs[n:2 * n], refs[2 * n], refs[2 * n + 1]
        x, y, c = _place()
        cps = [pltpu.make_async_remote_copy(src_ref=ins[i], dst_ref=outs[i], send_sem=send_sems.at[i], recv_sem=recv_sems.at[i],
                                            device_id=(x, y, 1 - c), device_id_type=MESH) for i in range(n)]
        for cp in cps:
            cp.start()
        for cp in cps:
            cp.wait()

    theirs = pl.pallas_call(
        body, in_specs=[ANY] * n, out_specs=[ANY] * n, out_shape=[jax.ShapeDtypeStruct(r.shape, r.dtype) for r in rs],
        scratch_shapes=[pltpu.SemaphoreType.DMA((n,)), pltpu.SemaphoreType.DMA((n,))],
        name=name,
    )(*rs)
    first = lax.axis_index("c") == 0
    return [jnp.where(first, jnp.concatenate([r, t], axis=0), jnp.concatenate([t, r], axis=0)) for r, t in zip(rs, theirs)]


def _chip_scatter(ss, name):
    n = len(ss)

    def body(*refs):
        ins, outs, send_sems, recv_sems = refs[:n], refs[n:2 * n], refs[2 * n], refs[2 * n + 1]
        x, y, c = _place()
        me = 2 * x + y
        chips = _other_chips(x, y)
        for i in range(n):
            for k, (px, py) in enumerate(chips):
                pltpu.make_async_remote_copy(src_ref=ins[i].at[2 * px + py], dst_ref=outs[i].at[me], send_sem=send_sems.at[3 * i + k],
                                             recv_sem=recv_sems.at[3 * i + k], device_id=(px, py, c), device_id_type=MESH).start()
        for i in range(n):
            for k, (px, py) in enumerate(chips):
                cp = pltpu.make_async_remote_copy(src_ref=ins[i].at[2 * px + py], dst_ref=outs[i].at[2 * px + py],
                                                  send_sem=send_sems.at[3 * i + k], recv_sem=recv_sems.at[3 * i + k],
                                                  device_id=(px, py, c), device_id_type=MESH)
                cp.wait_recv()
                cp.wait_send()

    parts = pl.pallas_call(
        body, in_specs=[ANY] * n, out_specs=[ANY] * n, out_shape=[jax.ShapeDtypeStruct(s.shape, s.dtype) for s in ss],
        scratch_shapes=[pltpu.SemaphoreType.DMA((3 * n,)), pltpu.SemaphoreType.DMA((3 * n,))],
        name=name,
    )(*ss)
    me = 2 * lax.axis_index("x") + lax.axis_index("y")
    return [_with_own_slot(p, lax.dynamic_index_in_dim(s, me, axis=0, keepdims=False)) for p, s in zip(parts, ss)]


HBM_SPACE = pltpu.MemorySpace.HBM


def _on_sequencer(name, collective_id, n_sems, body):
    @pl.kernel(mesh=plsc.ScalarSubcoreMesh(axis_name="sequencer", num_cores=1), name=name,
               scratch_types=(pltpu.SemaphoreType.DMA((n_sems,)), pltpu.SemaphoreType.DMA((n_sems,))),
               compiler_params=pltpu.CompilerParams(collective_id=collective_id))
    def launch(send_sems, recv_sems):
        body(send_sems, recv_sems)

    launch()


def _handshake(peers):
    barrier = pltpu.get_barrier_semaphore()
    for peer in peers:
        pl.semaphore_signal(barrier, inc=1, device_id=peer, device_id_type=MESH)
    pl.semaphore_wait(barrier, len(peers))


def _gather_on_sequencer(packs, name):
    n = len(packs)
    ins = [jax.new_ref(p, memory_space=HBM_SPACE) for p in packs]
    outs = [jax.empty_ref(jax.ShapeDtypeStruct((N_CHIPS,) + p.shape, p.dtype), memory_space=HBM_SPACE) for p in packs]

    def body(send_sems, recv_sems):
        x, y, c = _place()
        me, sib = 2 * x + y, (x, y, 1 - c)
        chips = _other_chips(x, y)
        _handshake([(px, py, c) for px, py in chips] + [sib])

        def copy(k, src, dst, to):
            return pltpu.make_async_remote_copy(src_ref=src, dst_ref=dst, send_sem=send_sems.at[k], recv_sem=recv_sems.at[k], device_id=to,
                                                device_id_type=MESH)

        def half(i, chip, h):
            rows = packs[i].shape[0] // 2
            return outs[i].at[chip, pl.ds(h * rows, rows), :]

        def mine(i):
            rows = packs[i].shape[0] // 2
            return ins[i].at[pl.ds(c * rows, rows), :]

        sent = [copy(6 * i + k, mine(i), half(i, me, c), (px, py, c)) for i in range(n) for k, (px, py) in enumerate(chips)]
        for cp in sent:
            cp.start()
        for i in range(n):
            for k, (px, py) in enumerate(chips):
                landed = half(i, 2 * px + py, c)
                copy(6 * i + k, mine(i), landed, (px, py, c)).wait_recv()
                fwd = copy(6 * i + 3 + k, landed, landed, sib)
                fwd.start()
                sent.append(fwd)
        for i in range(n):
            for k, (px, py) in enumerate(chips):
                other = half(i, 2 * px + py, 1 - c)
                copy(6 * i + 3 + k, other, other, sib).wait_recv()
        for cp in sent:
            cp.wait_send()

    _on_sequencer(name, 1, 6 * n, body)
    return [_with_own_slot(o[...], p) for o, p in zip(outs, packs)]


def _grads_to_sibling_on_sequencer(gs, name, collective_id):
    n = len(gs)
    ins = [jax.new_ref(g, memory_space=HBM_SPACE) for g in gs]
    outs = [jax.empty_ref(jax.ShapeDtypeStruct((g.shape[0], g.shape[1] // 2, g.shape[2]), g.dtype), memory_space=HBM_SPACE) for g in gs]

    def body(send_sems, recv_sems):
        x, y, c = _place()
        _handshake([(x, y, 1 - c)])
        cps = []
        for i in range(n):
            rows = gs[i].shape[1] // 2
            cps.append(pltpu.make_async_remote_copy(src_ref=ins[i].at[:, pl.ds((1 - c) * rows, rows), :], dst_ref=outs[i],
                                                    send_sem=send_sems.at[i], recv_sem=recv_sems.at[i], device_id=(x, y, 1 - c),
                                                    device_id_type=MESH))
        for cp in cps:
            cp.start()
        for cp in cps:
            cp.wait()

    _on_sequencer(name, collective_id, n, body)
    return [o[...] for o in outs]


def _chip_scatter_on_sequencer(ss, name, collective_id):
    n = len(ss)
    ins = [jax.new_ref(s, memory_space=HBM_SPACE) for s in ss]
    outs = [jax.empty_ref(jax.ShapeDtypeStruct(s.shape, s.dtype), memory_space=HBM_SPACE) for s in ss]

    def body(send_sems, recv_sems):
        x, y, c = _place()
        me = 2 * x + y
        chips = _other_chips(x, y)
        _handshake([(px, py, c) for px, py in chips])
        for i in range(n):
            for k, (px, py) in enumerate(chips):
                pltpu.make_async_remote_copy(src_ref=ins[i].at[2 * px + py], dst_ref=outs[i].at[me], send_sem=send_sems.at[3 * i + k],
                                             recv_sem=recv_sems.at[3 * i + k], device_id=(px, py, c), device_id_type=MESH).start()
        for i in range(n):
            for k, (px, py) in enumerate(chips):
                cp = pltpu.make_async_remote_copy(src_ref=ins[i].at[2 * px + py], dst_ref=outs[i].at[2 * px + py],
                                                  send_sem=send_sems.at[3 * i + k], recv_sem=recv_sems.at[3 * i + k],
                                                  device_id=(px, py, c), device_id_type=MESH)
                cp.wait_recv()
                cp.wait_send()

    _on_sequencer(name, collective_id, 3 * n, body)
    me = 2 * lax.axis_index("x") + lax.axis_index("y")
    return [_with_own_slot(o[...], lax.dynamic_index_in_dim(s, me, axis=0, keepdims=False)) for o, s in zip(outs, ss)]


def _all_reduce_small(v, name):
    shape = v.shape

    def body(v_ref, o_ref, slots, send_sems, recv_sems):
        x, y, c = _place()
        me = 4 * x + 2 * y + c
        slots[me] = v_ref[...]
        for r in range(1, N_DEV):
            peer = (x ^ (r >> 2), y ^ ((r >> 1) & 1), c ^ (r & 1))
            pltpu.make_async_remote_copy(src_ref=v_ref, dst_ref=slots.at[me], send_sem=send_sems.at[r - 1], recv_sem=recv_sems.at[r - 1],
                                         device_id=peer, device_id_type=MESH).start()
        for r in range(1, N_DEV):
            peer = (x ^ (r >> 2), y ^ ((r >> 1) & 1), c ^ (r & 1))
            cp = pltpu.make_async_remote_copy(src_ref=v_ref, dst_ref=slots.at[4 * peer[0] + 2 * peer[1] + peer[2]], send_sem=send_sems.at[r - 1],
                                              recv_sem=recv_sems.at[r - 1], device_id=peer, device_id_type=MESH)
            cp.wait_recv()
            cp.wait_send()
        acc = slots[0]
        for d in range(1, N_DEV):
            acc = acc + slots[d]
        o_ref[...] = acc

    vm = pl.BlockSpec(memory_space=pltpu.VMEM)
    return pl.pallas_call(
        body, in_specs=[vm], out_specs=vm, out_shape=jax.ShapeDtypeStruct(shape, F32),
        scratch_shapes=[pltpu.VMEM((N_DEV,) + shape, F32), pltpu.SemaphoreType.DMA((N_DEV - 1,)), pltpu.SemaphoreType.DMA((N_DEV - 1,))],
        name=name,
    )(v)


def _add_sibling(g, gsib, core, name):
    n, _, cdim = g.shape
    half = gsib.shape[1]
    tr = half // 2

    def body(core_ref, a_ref, b_ref, o_ref):
        o_ref[...] = (a_ref[...] + b_ref[...]).astype(o_ref.dtype)

    blk = pl.BlockSpec((1, tr, cdim), lambda j, i, core_ref: (j, i, 0))
    return pl.pallas_call(
        body,
        grid_spec=pltpu.PrefetchScalarGridSpec(
            num_scalar_prefetch=1, grid=(n, half // tr),
            in_specs=[pl.BlockSpec((1, tr, cdim), lambda j, i, core_ref: (j, core_ref[0] * (half // tr) + i, 0)), blk], out_specs=blk),
        out_shape=jax.ShapeDtypeStruct(gsib.shape, BF16), compiler_params=_params(("parallel", "parallel")), name=name,
    )(core, g, gsib)


def _sum_parts(parts, name):
    n, r, cdim = parts.shape
    tr = r // 2

    def body(p_ref, o_ref):
        acc = p_ref[0].astype(F32)
        for j in range(1, n):
            acc = acc + p_ref[j].astype(F32)
        o_ref[...] = acc

    return pl.pallas_call(
        body, grid=(r // tr,), in_specs=[pl.BlockSpec((n, tr, cdim), lambda i: (0, i, 0))],
        out_specs=pl.BlockSpec((tr, cdim), lambda i: (i, 0)), out_shape=jax.ShapeDtypeStruct((r, cdim), F32),
        compiler_params=_params(("parallel",)), name=name,
    )(parts)


def _adamw(parts, w, m, v, name):
    npart, r, cdim = parts.shape
    tr = r // 4 if r % 32 == 0 else r

    def body(p_ref, w_ref, m_ref, v_ref, g_ref, d_ref, nm_ref, nv_ref):
        g = p_ref[0]
        for j in range(1, npart):
            g = g + p_ref[j]
        m_new = ADAM_B1 * m_ref[...] + (1.0 - ADAM_B1) * g
        v_new = ADAM_B2 * v_ref[...] + (1.0 - ADAM_B2) * (g * g)
        m_hat = m_new / (1.0 - ADAM_B1 ** ADAM_STEP)
        v_hat = v_new / (1.0 - ADAM_B2 ** ADAM_STEP)
        g_ref[...] = g
        d_ref[...] = -ADAM_LR * (m_hat / (jnp.sqrt(v_hat) + ADAM_EPS) + ADAM_WD * w_ref[...])
        nm_ref[...] = m_new
        nv_ref[...] = v_new

    blk = pl.BlockSpec((tr, cdim), lambda i: (i, 0))
    shp = jax.ShapeDtypeStruct((r, cdim), F32)
    return pl.pallas_call(
        body, grid=(r // tr,), in_specs=[pl.BlockSpec((npart, tr, cdim), lambda i: (0, i, 0)), blk, blk, blk], out_specs=[blk] * 4,
        out_shape=[shp] * 4, compiler_params=_params(("parallel",)), name=name,
    )(parts, w, m, v)


BIG = ("ev_w_in", "ev_w_uq", "ev_w_ukv", "ev_w_out", "od_w_in", "od_w_out", "meta")
SMALL = ("norm_g", "final_g", "ev_q_norm_g", "ev_kv_norm_g", "od_sinks")
SMALL_SHAPE = (8, 512)
BY_ROWS = ("ev_w_out", "od_w_out")

EV_IN_SHARD, OD_IN_SHARD, UQ_SHARD = 2976 // N_CHIPS, 2304 // N_CHIPS, 768 // N_CHIPS


def _pack_big(a, lead=()):
    dtype = a["ev_w_in"].dtype
    z = lambda r, c: jnp.zeros(lead + (r, c), dtype)
    ax = len(lead)
    pad_to = lambda x, width: jnp.concatenate([x, z(x.shape[ax], width - x.shape[ax + 1])], axis=ax + 1)
    corner = jnp.concatenate([a["ev_w_ukv"], a["meta"], z(256 - MLA_KV_LORA - N_META, 256)], axis=ax)
    return (pad_to(a["ev_w_in"], 768), jnp.concatenate([pad_to(a["ev_w_uq"], 256), corner], axis=ax + 1),
            pad_to(a["od_w_in"], 640), jnp.concatenate([a["ev_w_out"], a["od_w_out"]], axis=ax + 1))


N_FIRST = 2


def _unpack_big(p_in0, p_lat, p_in1, p_out):
    return dict(ev_w_in=p_in0[..., :EV_IN_SHARD], od_w_in=p_in1[..., :OD_IN_SHARD], ev_w_out=p_out[..., :D_MODEL],
                od_w_out=p_out[..., D_MODEL:], ev_w_uq=p_lat[..., :UQ_SHARD], ev_w_ukv=p_lat[..., :MLA_KV_LORA, 256:],
                meta=p_lat[..., MLA_KV_LORA:MLA_KV_LORA + N_META, 256:])


def _chip_shards(full, by_rows):
    if by_rows:
        return full.reshape(N_CHIPS, full.shape[0] // N_CHIPS, full.shape[1])
    return full.reshape(full.shape[0], N_CHIPS, -1).transpose(1, 0, 2)


def _from_chip_shards(slots, by_rows):
    if by_rows:
        return slots.reshape(-1, slots.shape[2])
    return slots.transpose(1, 0, 2).reshape(slots.shape[1], -1)


def _pack_small(arrs, extra=None):
    flat = [a.reshape(-1) for a in arrs] + ([] if extra is None else [extra.reshape(-1)])
    used = sum(f.shape[0] for f in flat)
    return jnp.pad(jnp.concatenate(flat), (0, SMALL_SHAPE[0] * SMALL_SHAPE[1] - used)).reshape(SMALL_SHAPE)


def _unpack_small(p, shapes):
    flat, out, at = p.reshape(-1), [], 0
    for s in shapes:
        n = int(np.prod(s))
        out.append(flat[at:at + n].reshape(s))
        at += n
    return out, flat[at]


def kernel(x, meta, norm_g, final_g, ev_w_in, ev_q_norm_g, ev_kv_norm_g, ev_w_uq, ev_w_ukv, ev_w_out, od_w_in, od_sinks, od_w_out, loss_target, m_meta, m_norm_g, m_final_g, m_ev_w_in, m_ev_q_norm_g, m_ev_kv_norm_g, m_ev_w_uq, m_ev_w_ukv, m_ev_w_out, m_od_w_in, m_od_sinks, m_od_w_out, v_meta, v_norm_g, v_final_g, v_ev_w_in, v_ev_q_norm_g, v_ev_kv_norm_g, v_ev_w_uq, v_ev_w_ukv, v_ev_w_out, v_od_w_in, v_od_sinks, v_od_w_out):
    given = dict(locals())
    two_d = lambda a: a[0] if a.ndim == 3 else a
    packs = {k: _pack_big({n: two_d(given[k + n]) for n in BIG}) for k in ("", "m_", "v_")}

    wbf = [p.astype(BF16) for p in packs[""]]
    later = _gather_on_sequencer(wbf[N_FIRST:], "gather_later_weights")
    first, meta_all = _gather_weights(wbf[:N_FIRST], meta, "gather_weights")
    full = {n: _from_chip_shards(a, n in BY_ROWS) for n, a in _unpack_big(*first, *later).items()}
    meta_full = _from_chip_shards(meta_all, False)

    loss, grad_x, grads = _local_step(x, meta_full, norm_g, final_g, ev_q_norm_g, ev_kv_norm_g, od_sinks, loss_target,
                                      full["ev_w_in"], full["ev_w_uq"], full["ev_w_ukv"], full["ev_w_out"], full["od_w_in"], full["od_w_out"])

    gpacks = _pack_big({n: _chip_shards(grads[n], n in BY_ROWS) for n in BIG}, lead=(N_CHIPS,))
    core = lax.axis_index("c").astype(jnp.int32).reshape(1)

    def reduce_group(group, tag, to_sibling, scatter):
        sums = [_add_sibling(g, s, core, f"add_sibling_{tag}{i}") for i, (g, s) in enumerate(zip(group, to_sibling(group)))]
        parts = [_sum_parts(p, f"add_chips_{tag}{i}") for i, p in enumerate(scatter(sums))]
        return _share_halves(parts, "reduced_to_sibling_" + tag)

    reduced_later = reduce_group(gpacks[N_FIRST:], "later", lambda g: _grads_to_sibling_on_sequencer(g, "grads_to_sibling_later", 3),
                                 lambda s: _chip_scatter_on_sequencer(s, "grads_to_chips_later", 2))
    reduced_first = reduce_group(gpacks[:N_FIRST], "first", lambda g: _grads_to_sibling(g, "grads_to_sibling_first"),
                                 lambda s: _chip_scatter_on_sequencer(s, "grads_to_chips_first", 4))
    reduced = reduced_first + reduced_later
    updated = [_adamw(r[None], packs[""][i], packs["m_"][i], packs["v_"][i], f"adamw_matrices_{i}") for i, r in enumerate(reduced)]
    big_out = [{n: a.reshape(given[n].shape) for n, a in _unpack_big(*outs).items()} for outs in zip(*updated)]

    small_shapes = [given[n].shape for n in SMALL]
    ssum = _all_reduce_small(_pack_small([grads[n] for n in SMALL], loss[0, 0]), "reduce_vectors")
    small_out = _adamw(ssum[None], _pack_small([given[n] for n in SMALL]), _pack_small([given["m_" + n] for n in SMALL]),
                       _pack_small([given["v_" + n] for n in SMALL]), "adamw_vectors")
    total_loss = ssum.reshape(-1)[sum(int(np.prod(s)) for s in small_shapes)]
    small_out = [_unpack_small(o, small_shapes)[0] for o in small_out]

    names = ("meta", "norm_g", "final_g", "ev_w_in", "ev_q_norm_g", "ev_kv_norm_g", "ev_w_uq", "ev_w_ukv", "ev_w_out", "od_w_in", "od_sinks",
             "od_w_out")
    outs = [total_loss, grad_x]
    for kind in range(4):
        for n in names:
            outs.append(big_out[kind][n] if n in BIG else small_out[kind][SMALL.index(n)])
    return tuple(outs)
```

```python
import functools
import math

import numpy as np
import jax
import jax.numpy as jnp
from jax import lax
from jax.experimental import pallas as pl
from jax.experimental.pallas import tpu as pltpu
from jax.experimental.pallas import tpu_sc as plsc

F32 = jnp.float32
BF16 = jnp.bfloat16

D_MODEL = 1024
BLOCK = 128
N_META = 16
N_PAD = BLOCK - N_META
NORM_EPS = 1e-6
NEG = -1e30
HEAD = 64
SB_HEADS = 8
MLA_HEADS = 8
MLA_Q_LORA = 256
MLA_KV_LORA = 128
MLA_NOPE = 64
MLA_ROPE = 32
ROPE_BASE = 10000.0
SWA_HEADS = 16
SWA_KV_HEADS = 2
SWA_WINDOW = 128
N_CHIPS = 4
N_DEV = 8

ADAM_LR = 0.001
ADAM_B1 = 0.9
ADAM_B2 = 0.999
ADAM_EPS = 1e-08
ADAM_WD = 0.01
ADAM_STEP = 10

VMEM_LIMIT = 48 * 1024 * 1024

EV_G, EV_Q, EV_K, EV_V, EV_CQ, EV_CKV, EV_KR, EV_N = 0, 1024, 1536, 2048, 2560, 2816, 2944, 3072
OD_G, OD_Q, OD_K, OD_V, OD_N = 0, 1024, 2048, 2176, 2304


def _params(sem=None):
    return pltpu.CompilerParams(dimension_semantics=sem, vmem_limit_bytes=VMEM_LIMIT)


def _row_tile(m):
    return 256 if m % 256 == 0 else 128


def _matmul_rows(m):
    for c in (1088, 1024, 768, 640, 512, 384, 256):
        if m % c == 0:
            return c
    return 128


def _dot(a, b):
    return jnp.dot(a.astype(BF16), b.astype(BF16), preferred_element_type=F32)


def _dot_nt(a, b):
    return lax.dot_general(a.astype(BF16), b.astype(BF16), (((1,), (1,)), ((), ())), preferred_element_type=F32)


def _dot_tn(a, b):
    return lax.dot_general(a.astype(BF16), b.astype(BF16), (((0,), (0,)), ((), ())), preferred_element_type=F32)


def _rms_fwd(h, g, name):
    t, d = h.shape
    tm = _row_tile(t)

    def body(h_ref, g_ref, o_ref):
        x = h_ref[...]
        r = lax.rsqrt(jnp.mean(x * x, axis=-1, keepdims=True) + NORM_EPS)
        o_ref[...] = ((x * r) * g_ref[...]).astype(o_ref.dtype)

    return pl.pallas_call(
        body, grid=(t // tm,),
        in_specs=[pl.BlockSpec((tm, d), lambda i: (i, 0)), pl.BlockSpec((1, d), lambda i: (0, 0))],
        out_specs=pl.BlockSpec((tm, d), lambda i: (i, 0)),
        out_shape=jax.ShapeDtypeStruct((t, d), BF16), compiler_params=_params(("parallel",)), name=name,
    )(h, g)


def _rms_bwd(h, g, dy, dres, name):
    t, d = h.shape
    tm = _row_tile(t)

    def body(h_ref, g_ref, dy_ref, dres_ref, dh_ref, dg_ref):
        @pl.when(pl.program_id(0) == 0)
        def _():
            dg_ref[...] = jnp.zeros_like(dg_ref)

        x = h_ref[...]
        r = lax.rsqrt(jnp.mean(x * x, axis=-1, keepdims=True) + NORM_EPS)
        xr = x * r
        dy_ = dy_ref[...]
        u = dy_ * g_ref[...]
        dh_ref[...] = dres_ref[...] + r * (u - xr * jnp.mean(u * xr, axis=-1, keepdims=True))
        dg_ref[...] += jnp.sum(dy_ * xr, axis=0, keepdims=True)

    row = pl.BlockSpec((tm, d), lambda i: (i, 0))
    vec = pl.BlockSpec((1, d), lambda i: (0, 0))
    return pl.pallas_call(
        body, grid=(t // tm,), in_specs=[row, vec, row, row], out_specs=[row, vec],
        out_shape=[jax.ShapeDtypeStruct((t, d), F32), jax.ShapeDtypeStruct((1, d), F32)],
        compiler_params=_params(("arbitrary",)), name=name,
    )(h, g, dy, dres)


def _col_tile(n):
    for c in (1024, 768, 640, 512, 384, 256, 128):
        if n % c == 0:
            return c
    return n


def _mm(a, w, name, res=None, out_dtype=F32, a_cols=None):
    m = a.shape[0]
    k, n = w.shape
    a_blk = 0 if a_cols is None else a_cols[0] // k
    assert a_cols is None or (a_cols[1] == k and a_cols[0] % k == 0)
    tm, tn = _matmul_rows(m), _col_tile(n)

    def body(*refs):
        if res is None:
            a_ref, w_ref, o_ref = refs
            acc = _dot(a_ref[...], w_ref[...])
        else:
            a_ref, w_ref, r_ref, o_ref = refs
            acc = r_ref[...] + _dot(a_ref[...], w_ref[...])
        o_ref[...] = acc.astype(o_ref.dtype)

    in_specs = [pl.BlockSpec((tm, k), lambda j, i: (i, a_blk)), pl.BlockSpec((k, tn), lambda j, i: (0, j))]
    args = [a, w]
    if res is not None:
        in_specs.append(pl.BlockSpec((tm, tn), lambda j, i: (i, j)))
        args.append(res)
    return pl.pallas_call(
        body, grid=(n // tn, m // tm), in_specs=in_specs, out_specs=pl.BlockSpec((tm, tn), lambda j, i: (i, j)),
        out_shape=jax.ShapeDtypeStruct((m, n), out_dtype), compiler_params=_params(("parallel", "parallel")), name=name,
    )(*args)


def _mm_nt(a, w, name):
    m, n = a.shape
    k = w.shape[0]
    tm, tk = _matmul_rows(m), _col_tile(k)

    def body(a_ref, w_ref, o_ref):
        o_ref[...] = _dot_nt(a_ref[...], w_ref[...])

    return pl.pallas_call(
        body, grid=(k // tk, m // tm),
        in_specs=[pl.BlockSpec((tm, n), lambda j, i: (i, 0)), pl.BlockSpec((tk, n), lambda j, i: (j, 0))],
        out_specs=pl.BlockSpec((tm, tk), lambda j, i: (i, j)),
        out_shape=jax.ShapeDtypeStruct((m, k), F32), compiler_params=_params(("parallel", "parallel")), name=name,
    )(a, w)


def _mm_tn(x, dy, name):
    m, k = x.shape
    n = dy.shape[1]
    tm, tn = _matmul_rows(m), _col_tile(n)

    def body(x_ref, dy_ref, o_ref):
        @pl.when(pl.program_id(1) == 0)
        def _():
            o_ref[...] = jnp.zeros_like(o_ref)

        o_ref[...] += _dot_tn(x_ref[...], dy_ref[...])

    return pl.pallas_call(
        body, grid=(n // tn, m // tm),
        in_specs=[pl.BlockSpec((tm, k), lambda j, i: (i, 0)), pl.BlockSpec((tm, tn), lambda j, i: (i, j))],
        out_specs=pl.BlockSpec((k, tn), lambda j, i: (0, j)),
        out_shape=jax.ShapeDtypeStruct((k, n), F32), compiler_params=_params(("parallel", "arbitrary")), name=name,
    )(x, dy)


def _silu_parts(g):
    s = 1.0 / (1.0 + jnp.exp(-g))
    return g * s, s * (1.0 + g * (1.0 - s))


def _gate_fwd(o_parts, proj, name):
    t = proj.shape[0]
    tm = _row_tile(t)
    w = D_MODEL // len(o_parts)

    def body(*refs):
        g_ref, o_ref = refs[-2], refs[-1]
        for p, r in enumerate(refs[:-2]):
            sil, _ = _silu_parts(g_ref[:, p * w:(p + 1) * w])
            o_ref[:, p * w:(p + 1) * w] = (r[...].astype(F32) * sil).astype(o_ref.dtype)

    return pl.pallas_call(
        body, grid=(t // tm,),
        in_specs=[pl.BlockSpec((tm, w), lambda i: (i, 0)) for _ in o_parts] + [pl.BlockSpec((tm, D_MODEL), lambda i: (i, 0))],
        out_specs=pl.BlockSpec((tm, D_MODEL), lambda i: (i, 0)),
        out_shape=jax.ShapeDtypeStruct((t, D_MODEL), BF16), compiler_params=_params(("parallel",)), name=name,
    )(*o_parts, proj)


def _gate_bwd(dao, o_parts, proj, name):
    t = proj.shape[0]
    tm = _row_tile(t)
    np_ = len(o_parts)
    w = D_MODEL // np_

    def body(*refs):
        dao_ref, g_ref = refs[0], refs[1 + np_]
        do_refs, dg_ref = refs[2 + np_:2 + 2 * np_], refs[-1]
        for p in range(np_):
            sl = slice(p * w, (p + 1) * w)
            sil, dsil = _silu_parts(g_ref[:, sl])
            da = dao_ref[:, sl]
            do_refs[p][...] = da * sil
            dg_ref[:, sl] = (da * refs[1 + p][...].astype(F32) * dsil).astype(dg_ref.dtype)

    full = pl.BlockSpec((tm, D_MODEL), lambda i: (i, 0))
    part = pl.BlockSpec((tm, w), lambda i: (i, 0))
    outs = pl.pallas_call(
        body, grid=(t // tm,), in_specs=[full] + [part] * np_ + [full], out_specs=[part] * np_ + [full],
        out_shape=[jax.ShapeDtypeStruct((t, w), F32)] * np_ + [jax.ShapeDtypeStruct((t, D_MODEL), BF16)],
        compiler_params=_params(("parallel",)), name=name,
    )(dao, *o_parts, proj)
    return outs[:np_], outs[np_]


def _loss_head(h2, gf, target, b, lp):
    d = h2.shape[1]
    nb = lp // BLOCK
    h3 = h2.reshape(b, lp, d)

    def body(h_ref, g_ref, t_ref, dh_ref, dg_ref, loss_ref):
        first = (pl.program_id(0) == 0) & (pl.program_id(1) == 0)

        @pl.when(first)
        def _():
            dg_ref[...] = jnp.zeros_like(dg_ref)
            loss_ref[...] = jnp.zeros_like(loss_ref)

        @pl.when(pl.program_id(1) == 0)
        def _():
            dh_ref[...] = jnp.zeros_like(dh_ref)

        @pl.when(pl.program_id(1) > 0)
        def _():
            x = h_ref[0]
            r = lax.rsqrt(jnp.mean(x * x, axis=-1, keepdims=True) + NORM_EPS)
            xr = x * r
            g = g_ref[...]
            diff = xr * g - t_ref[0]
            loss_ref[...] += 0.5 * jnp.sum(jnp.mean(diff * diff, axis=-1, keepdims=True))
            dy = diff * (1.0 / d)
            u = dy * g
            dh_ref[0] = r * (u - xr * jnp.mean(u * xr, axis=-1, keepdims=True))
            dg_ref[...] += jnp.sum(dy * xr, axis=0, keepdims=True)

    blk = pl.BlockSpec((1, BLOCK, d), lambda bi, n: (bi, n, 0))
    dh, dg, loss = pl.pallas_call(
        body, grid=(b, nb),
        in_specs=[blk, pl.BlockSpec((1, d), lambda bi, n: (0, 0)),
                  pl.BlockSpec((1, BLOCK, d), lambda bi, n: (bi, jnp.maximum(n - 1, 0), 0))],
        out_specs=[blk, pl.BlockSpec((1, d), lambda bi, n: (0, 0)), pl.BlockSpec((8, 128), lambda bi, n: (0, 0))],
        out_shape=[jax.ShapeDtypeStruct((b, lp, d), F32), jax.ShapeDtypeStruct((1, d), F32), jax.ShapeDtypeStruct((8, 128), F32)],
        compiler_params=_params(("arbitrary", "arbitrary")), name="loss_head",
    )(h3, gf, target)
    return dh.reshape(b * lp, d), dg, loss


def _iota2(shape, dim):
    return lax.broadcasted_iota(jnp.int32, shape, dim)


KEYS = 512
SB_FWD_KEYS = 256


def _lo_lanes():
    return _iota2((1, BLOCK), 1) < HEAD


def _halves(x, lo):
    zero = jnp.zeros_like(x)
    return jnp.where(lo, x, zero), jnp.where(lo, zero, x)


def _rows_of_pair(a, b):
    return jnp.where(_iota2((BLOCK, 1), 0) < HEAD, a, b)


def _split_rows_t(x):
    xt = x.T
    first = _iota2(xt.shape, 0) < HEAD
    zero = jnp.zeros_like(xt)
    return jnp.concatenate([jnp.where(first, xt, zero), jnp.where(first, zero, xt)], axis=1).astype(BF16)


def _key_chunk(c, lp, t_idx, strict, key_axis):
    keys = t_idx.shape[key_axis]
    first = c * keys
    s0 = pl.multiple_of(jnp.minimum(first, lp - keys), BLOCK)
    s_idx = s0 + _iota2(t_idx.shape, key_axis)
    seen = (s_idx < t_idx) if strict else (s_idx <= t_idx)
    return s0, seen & (s_idx >= jnp.maximum(first, N_PAD))


def _split_dot(x, tri):
    hi = x.astype(BF16)
    lo = (x - hi.astype(F32)).astype(BF16)
    return jnp.dot(hi, tri, preferred_element_type=F32) + jnp.dot(lo, tri, preferred_element_type=F32)


def _stack_halves(x, lo):
    a, b = _halves(x, lo)
    return jnp.concatenate([a, b], axis=0)


def _pair(a, b, lo):
    return jnp.where(lo, a, b)


def _chunk_starts(lp):
    return [min(c * KEYS, lp - KEYS) for c in range(-(-lp // KEYS))]


def _put_rows(ref, r0, bq, a, b):
    for t in range(bq // BLOCK):
        part = slice(t * BLOCK, (t + 1) * BLOCK)
        ref[0, 0, r0 // BLOCK + t] = jnp.concatenate([a[:, part], b[:, part], jnp.zeros((6, BLOCK), F32)], axis=0)


def _get_rows(ref, r0, bq):
    return [jnp.concatenate([ref[0, 0, r0 // BLOCK + t, h:h + 1, :] for t in range(bq // BLOCK)], axis=1) for h in range(2)]


def _n_chunks(i, keys):
    return ((i + 1) * BLOCK + keys - 1) // keys


QROWS = 512


def _for_query_tiles(nb, tile, keys=KEYS):
    per = QROWS // BLOCK

    def step(j, _):
        tile(pl.multiple_of(j * QROWS, QROWS), QROWS, (j + 1) * (QROWS // keys))
        return 0

    lax.fori_loop(0, nb // per, step, 0)
    for i in range(nb - nb % per, nb):
        tile(i * BLOCK, BLOCK, _n_chunks(i, keys))


def _walk_chunks(r0, n, chunk, carry, leftwards=False, keys=KEYS):
    diag = jnp.maximum(r0 // keys, 1)

    def span(first, last, masked, carry):
        def step(t, cr):
            return chunk(last - 1 - t if leftwards else first + t, cr, masked)
        return lax.fori_loop(0, last - first, step, carry)

    spans = [(0, 1, True), (1, diag, False), (diag, n, True)]
    for first, last, masked in (reversed(spans) if leftwards else spans):
        carry = span(first, last, masked, carry)
    return carry


def _where(valid, x, other):
    return x if valid is None else jnp.where(valid, x, other)


HEAD_SCALE = HEAD ** -0.5
assert math.frexp(HEAD_SCALE)[0] == 0.5


def _sb_scores(q_h, k, valid, after):
    z = _dot_nt(q_h, k)
    lb = jnp.minimum(z, 0.0) - jnp.log(1.0 + jnp.exp(-jnp.abs(z)))
    l1m_all = lb - z
    l1m = _where(valid, l1m_all, 0.0)
    return lb, l1m_all, l1m, _split_dot(l1m, after)


def _pair_stat_spec(nb):
    return pl.BlockSpec((1, 1, nb, 8, BLOCK), lambda bi, hp: (bi, hp, 0, 0, 0))


def _sb_fwd(proj3, name):
    b, lp, _ = proj3.shape
    nb = lp // BLOCK
    npair = SB_HEADS // 2

    keys = SB_FWD_KEYS

    def body(q_ref, k_ref, v_ref, o_ref, tot_ref):
        lo = _lo_lanes()
        after = (_iota2((keys, keys), 0) > _iota2((keys, keys), 1)).astype(BF16)

        def qtile(r0, bq, n):
            qs = _halves((q_ref[0, pl.ds(r0, bq), :] * HEAD_SCALE).astype(BF16), lo)
            t_idx = r0 + _iota2((bq, keys), 0)

            def kchunk(c, carry, masked):
                cs, acc = carry[:2], carry[2]
                s0, valid = _key_chunk(c, lp, t_idx, True, 1)
                valid = valid if masked else None
                k = k_ref[0, pl.ds(s0, keys), :].astype(BF16)
                a_s, new = [], []
                for h in range(2):
                    lb, _, l1m, suf = _sb_scores(qs[h], k, valid, after)
                    a_s.append(_where(valid, jnp.exp(lb + suf + cs[h]), 0.0).astype(BF16))
                    new.append(cs[h] + jnp.sum(l1m, axis=1, keepdims=True))
                v_bd = _stack_halves(v_ref[0, pl.ds(s0, keys), :].astype(BF16), lo)
                return (*new, acc + jnp.dot(jnp.concatenate(a_s, axis=1), v_bd, preferred_element_type=F32))

            zero = jnp.zeros((bq, 1), F32)
            c_a, c_b, acc = _walk_chunks(r0, n, kchunk, (zero, zero, jnp.zeros((bq, BLOCK), F32)), leftwards=True, keys=keys)
            o_ref[0, pl.ds(r0, bq), :] = acc
            tot_ref[0, pl.ds(r0, bq), :] = jnp.broadcast_to(_pair(c_a, c_b, lo), (bq, BLOCK))

        _for_query_tiles(nb, qtile, keys)

    def col(first):
        return pl.BlockSpec((1, lp, 2 * HEAD), lambda bi, hp: (bi, 0, first // (2 * HEAD) + hp))

    shp = jax.ShapeDtypeStruct((b, lp, SB_HEADS * HEAD), F32)
    return pl.pallas_call(
        body, grid=(b, npair), in_specs=[col(EV_Q), col(EV_K), col(EV_V)], out_specs=[col(0), col(0)], out_shape=[shp, shp],
        compiler_params=_params(("parallel", "parallel")), name=name,
    )(proj3, proj3, proj3)


def _sb_bwd(proj3, tot, do, name):
    b, lp, _ = proj3.shape
    nb = lp // BLOCK
    npair = SB_HEADS // 2

    def body(q_ref, k_ref, v_ref, tot_ref, do_ref, dq_ref, dk_ref, dv_ref):
        lo = _lo_lanes()
        after = (_iota2((KEYS, KEYS), 0) > _iota2((KEYS, KEYS), 1)).astype(BF16)
        before = (_iota2((KEYS, KEYS), 0) < _iota2((KEYS, KEYS), 1)).astype(BF16)
        dk_ref[...] = jnp.zeros_like(dk_ref)
        dv_ref[...] = jnp.zeros_like(dv_ref)

        def qtile(r0, bq, n):
            rows = pl.ds(r0, bq)
            qs = _halves((q_ref[0, rows, :] * HEAD_SCALE).astype(BF16), lo)
            dos = _halves(do_ref[0, rows, :].astype(BF16), lo)
            tot_i = tot_ref[0, rows, :]
            tots = (tot_i[:, 0:1], tot_i[:, HEAD:HEAD + 1])
            q_st, do_st = jnp.concatenate(qs, axis=0), jnp.concatenate(dos, axis=0)
            t_idx = r0 + _iota2((bq, KEYS), 0)

            def kchunk(c, carry, masked):
                s0, valid = _key_chunk(c, lp, t_idx, True, 1)
                valid = valid if masked else None
                keys = pl.ds(s0, KEYS)
                k = k_ref[0, keys, :].astype(BF16)
                v = v_ref[0, keys, :].astype(BF16)
                a_s, dzs, new = [], [], []
                for h in range(2):
                    left, pre = carry[2 * h], carry[2 * h + 1]
                    lb, l1m_all, l1m, suf = _sb_scores(qs[h], k, valid, after)
                    here = jnp.sum(l1m, axis=1, keepdims=True)
                    a = _where(valid, jnp.exp(lb + suf + (tots[h] - left - here)), 0.0)
                    w = a * _dot_nt(dos[h], v)
                    dz = _where(valid, w * jnp.exp(l1m_all) - (pre + _split_dot(w, before)) * jnp.exp(lb), 0.0)
                    new += [left + here, pre + jnp.sum(w, axis=1, keepdims=True)]
                    a_s.append(a.astype(BF16))
                    dzs.append(dz.astype(BF16))
                dk_ref[0, keys, :] += _dot_tn(jnp.concatenate(dzs, axis=0), q_st)
                dv_ref[0, keys, :] += _dot_tn(jnp.concatenate(a_s, axis=0), do_st)
                dq = carry[4] + jnp.dot(jnp.concatenate(dzs, axis=1), _stack_halves(k, lo), preferred_element_type=F32)
                return (*new, dq)

            zero = jnp.zeros((bq, 1), F32)
            out = _walk_chunks(r0, n, kchunk, (zero, zero, zero, zero, jnp.zeros((bq, BLOCK), F32)))
            dq_ref[0, rows, :] = out[4] * HEAD_SCALE

        _for_query_tiles(nb, qtile)

    def col(first):
        return pl.BlockSpec((1, lp, 2 * HEAD), lambda bi, hp: (bi, 0, first // (2 * HEAD) + hp))

    shp = jax.ShapeDtypeStruct((b, lp, SB_HEADS * HEAD), F32)
    return pl.pallas_call(
        body, grid=(b, npair), in_specs=[col(EV_Q), col(EV_K), col(EV_V), col(0), col(0)], out_specs=[col(0)] * 3, out_shape=[shp] * 3,
        compiler_params=_params(("parallel", "parallel")), name=name,
    )(proj3, proj3, proj3, tot, do)


def _rope_tables(lp):
    half = MLA_ROPE // 2
    pos = (np.arange(lp) - N_PAD).astype(np.float32)
    inv = jnp.asarray(ROPE_BASE, F32) ** (-jnp.arange(half, dtype=F32) / half)
    ang = jnp.asarray(pos)[:, None] * inv[None, :]
    cos, sin = jnp.cos(ang), jnp.sin(ang)
    zeros = lambda n: jnp.zeros((lp, n), F32)
    c = jnp.concatenate([jnp.ones((lp, MLA_NOPE), F32), cos, cos, zeros(32)], axis=1)
    s1 = jnp.concatenate([zeros(MLA_NOPE), -sin, zeros(half), zeros(32)], axis=1)
    s2 = jnp.concatenate([zeros(MLA_NOPE), zeros(half), sin, zeros(32)], axis=1)
    return c, s1, s2


def _rope(x, c, s1, s2):
    half = MLA_ROPE // 2
    return x * c + pltpu.roll(x, BLOCK - half, 1) * s1 + pltpu.roll(x, half, 1) * s2


def _rope_t(dy, c, s1, s2):
    half = MLA_ROPE // 2
    return dy * c + pltpu.roll(dy * s1, half, 1) + pltpu.roll(dy * s2, BLOCK - half, 1)


def _rms_rows(x, g):
    r = lax.rsqrt(jnp.mean(x * x, axis=-1, keepdims=True) + NORM_EPS)
    return x * r, r


def _mla_prep_fwd(proj3, gq, gkv, wq, wk, wv, tabs, name):
    b, lp, _ = proj3.shape
    nb = lp // BLOCK
    hw = MLA_HEADS * BLOCK

    def body(cq_ref, ckv_ref, kr_ref, gq_ref, gkv_ref, wq_ref, wk_ref, wv_ref, c_ref, s1_ref, s2_ref, qf_ref, kf_ref, v_ref):
        c, s1, s2 = c_ref[...], s1_ref[...], s2_ref[...]
        xq, _ = _rms_rows(cq_ref[0], None)
        qh = _dot(xq * gq_ref[...], wq_ref[...])
        xk, _ = _rms_rows(ckv_ref[0], None)
        ckv_n = xk * gkv_ref[...]
        kv = _dot(ckv_n, wk_ref[...])
        v_ref[0] = _dot(ckv_n, wv_ref[...]).astype(v_ref.dtype)
        kr = _rope(kr_ref[0], c, s1, s2)
        for h in range(MLA_HEADS):
            ls = slice(h * BLOCK, (h + 1) * BLOCK)
            qf_ref[0, :, ls] = _rope(qh[:, ls], c, s1, s2).astype(qf_ref.dtype)
            kf_ref[0, :, ls] = (kv[:, ls] + kr).astype(kf_ref.dtype)

    def col(first, width):
        return pl.BlockSpec((1, BLOCK, width), lambda bi, n: (bi, n, first // width))

    def whole(a):
        return pl.BlockSpec(a.shape, lambda bi, n: (0,) * a.ndim)

    tab = pl.BlockSpec((BLOCK, BLOCK), lambda bi, n: (n, 0))
    return pl.pallas_call(
        body, grid=(b, nb),
        in_specs=[col(EV_CQ, MLA_Q_LORA), col(EV_CKV, MLA_KV_LORA), col(EV_KR, BLOCK), whole(gq), whole(gkv), whole(wq), whole(wk),
                  whole(wv), tab, tab, tab],
        out_specs=[col(0, hw), col(0, hw), col(0, MLA_HEADS * HEAD)],
        out_shape=[jax.ShapeDtypeStruct((b, lp, hw), BF16), jax.ShapeDtypeStruct((b, lp, hw), BF16),
                   jax.ShapeDtypeStruct((b, lp, MLA_HEADS * HEAD), BF16)],
        compiler_params=_params(("parallel", "parallel")), name=name,
    )(proj3, proj3, proj3, gq, gkv, wq, wk, wv, *tabs)


def _mla_prep_bwd(proj3, gq, gkv, wq, wk, wv, tabs, dqf, dkf, dv, name):
    b, lp, _ = proj3.shape
    nb = lp // BLOCK
    hw = MLA_HEADS * BLOCK

    def body(cq_ref, ckv_ref, gq_ref, gkv_ref, wq_ref, wk_ref, wv_ref, c_ref, s1_ref, s2_ref, dqf_ref, dkf_ref, dv_ref,
             dcq_ref, dckv_ref, dkr_ref, dwq_ref, dwk_ref, dwv_ref, dgq_ref, dgkv_ref, dqh):
        @pl.when((pl.program_id(0) == 0) & (pl.program_id(1) == 0))
        def _():
            for r in (dwq_ref, dwk_ref, dwv_ref, dgq_ref, dgkv_ref):
                r[...] = jnp.zeros_like(r)

        c, s1, s2 = c_ref[...], s1_ref[...], s2_ref[...]
        dkr = jnp.zeros((BLOCK, BLOCK), F32)
        for h in range(MLA_HEADS):
            ls = slice(h * BLOCK, (h + 1) * BLOCK)
            dqh[:, ls] = _rope_t(dqf_ref[0, :, ls].astype(F32), c, s1, s2).astype(dqh.dtype)
            dkr = dkr + dkf_ref[0, :, ls].astype(F32)
        dkr_ref[0] = _rope_t(dkr, c, s1, s2).astype(dkr_ref.dtype)

        def norm_bwd(x, g, dy, dg_ref):
            xr, r = _rms_rows(x, None)
            u = dy * g
            dg_ref[...] += jnp.sum(dy * xr, axis=0, keepdims=True)
            return r * (u - xr * jnp.mean(u * xr, axis=-1, keepdims=True))

        xq, _ = _rms_rows(cq_ref[0], None)
        cq_n = xq * gq_ref[...]
        dwq_ref[...] += _dot_tn(cq_n, dqh[...])
        dcq_ref[0] = norm_bwd(cq_ref[0], gq_ref[...], _dot_nt(dqh[...], wq_ref[...]), dgq_ref).astype(dcq_ref.dtype)
        xk, _ = _rms_rows(ckv_ref[0], None)
        ckv_n = xk * gkv_ref[...]
        dkf_, dv_ = dkf_ref[0], dv_ref[0]
        dwk_ref[...] += _dot_tn(ckv_n, dkf_)
        dwv_ref[...] += _dot_tn(ckv_n, dv_)
        dckv_n = _dot_nt(dkf_, wk_ref[...]) + _dot_nt(dv_, wv_ref[...])
        dckv_ref[0] = norm_bwd(ckv_ref[0], gkv_ref[...], dckv_n, dgkv_ref).astype(dckv_ref.dtype)

    def col(first, width):
        return pl.BlockSpec((1, BLOCK, width), lambda bi, n: (bi, n, first // width))

    def whole(a):
        return pl.BlockSpec(a.shape, lambda bi, n: (0,) * len(a.shape))

    tab = pl.BlockSpec((BLOCK, BLOCK), lambda bi, n: (n, 0))
    acc_shapes = [jax.ShapeDtypeStruct(a.shape, F32) for a in (wq, wk, wv, gq, gkv)]
    return pl.pallas_call(
        body, grid=(b, nb),
        in_specs=[col(EV_CQ, MLA_Q_LORA), col(EV_CKV, MLA_KV_LORA), whole(gq), whole(gkv), whole(wq), whole(wk), whole(wv), tab, tab, tab,
                  col(0, hw), col(0, hw), col(0, MLA_HEADS * HEAD)],
        out_specs=[col(0, MLA_Q_LORA), col(0, MLA_KV_LORA), col(0, BLOCK)] + [whole(a) for a in acc_shapes],
        out_shape=[jax.ShapeDtypeStruct((b, lp, MLA_Q_LORA), BF16), jax.ShapeDtypeStruct((b, lp, MLA_KV_LORA), BF16),
                   jax.ShapeDtypeStruct((b, lp, BLOCK), BF16)] + acc_shapes,
        scratch_shapes=[pltpu.VMEM((BLOCK, hw), BF16)],
        compiler_params=_params(("arbitrary", "arbitrary")), name=name,
    )(proj3, proj3, gq, gkv, wq, wk, wv, *tabs, dqf, dkf, dv)


def _mla_fwd(qf, kf, v, name):
    b, lp, _ = qf.shape
    nb = lp // BLOCK
    npair = MLA_HEADS // 2
    scale = (MLA_NOPE + MLA_ROPE) ** -0.5
    starts = _chunk_starts(lp)

    def body(q_ref, k_ref, v_ref, o_ref, lse_ref, vt_ref):
        for c, s0 in enumerate(starts):
            vt_ref[c] = _split_rows_t(v_ref[0, s0:s0 + KEYS, :].astype(F32))

        def qtile(r0, bq, n):
            qs = [q_ref[0, pl.ds(r0, bq), h * BLOCK:(h + 1) * BLOCK] for h in range(2)]
            t_idx = r0 + _iota2((KEYS, bq), 1)

            def kchunk(c, carry, masked):
                stats, acc = carry[:4], carry[4]
                s0, valid = _key_chunk(c, lp, t_idx, False, 0)
                valid = valid if masked else None
                ps, new, alphas = [], [], []
                for h in range(2):
                    m, l = stats[2 * h], stats[2 * h + 1]
                    s = _where(valid, _dot_nt(k_ref[0, pl.ds(s0, KEYS), h * BLOCK:(h + 1) * BLOCK], qs[h]) * scale, NEG)
                    m_new = jnp.maximum(m, jnp.max(s, axis=0, keepdims=True))
                    p = _where(valid, jnp.exp(s - m_new), 0.0)
                    alpha = jnp.exp(m - m_new)
                    new += [m_new, alpha * l + jnp.sum(p, axis=0, keepdims=True)]
                    alphas.append(alpha)
                    ps.append(p.astype(BF16))
                pv = jnp.dot(vt_ref[c], jnp.concatenate(ps, axis=0), preferred_element_type=F32)
                return (*new, _rows_of_pair(alphas[0], alphas[1]) * acc + pv)

            neg, zero = jnp.full((1, bq), NEG, F32), jnp.zeros((1, bq), F32)
            m_a, l_a, m_b, l_b, acc = _walk_chunks(r0, n, kchunk, (neg, zero, neg, zero, jnp.zeros((BLOCK, bq), F32)))
            safe = [jnp.where(l > 0.0, l, 1.0) for l in (l_a, l_b)]
            o_ref[0, pl.ds(r0, bq), :] = (acc / _rows_of_pair(safe[0], safe[1])).T
            lse = [jnp.where(l > 0.0, m + jnp.log(sf), 0.0) for m, l, sf in ((m_a, l_a, safe[0]), (m_b, l_b, safe[1]))]
            _put_rows(lse_ref, r0, bq, lse[0], lse[1])

        _for_query_tiles(nb, qtile)

    wide = pl.BlockSpec((1, lp, 2 * BLOCK), lambda bi, hp: (bi, 0, hp))
    thin = pl.BlockSpec((1, lp, 2 * HEAD), lambda bi, hp: (bi, 0, hp))
    return pl.pallas_call(
        body, grid=(b, npair), in_specs=[wide, wide, thin], out_specs=[thin, _pair_stat_spec(nb)],
        out_shape=[jax.ShapeDtypeStruct((b, lp, MLA_HEADS * HEAD), F32), jax.ShapeDtypeStruct((b, npair, nb, 8, BLOCK), F32)],
        scratch_shapes=[pltpu.VMEM((len(starts), BLOCK, 2 * KEYS), BF16)],
        compiler_params=_params(("parallel", "parallel")), name=name,
    )(qf, kf, v)


def _mla_bwd(qf, kf, v, o, lse, do, name):
    b, lp, _ = qf.shape
    nb = lp // BLOCK
    npair = MLA_HEADS // 2
    scale = (MLA_NOPE + MLA_ROPE) ** -0.5

    starts = _chunk_starts(lp)

    def body(q_ref, k_ref, v_ref, o_ref, lse_ref, do_ref, dq_ref, dk_ref, dv_ref, kt_ref):
        lo = _lo_lanes()
        dk_ref[...] = jnp.zeros_like(dk_ref)
        dv_ref[...] = jnp.zeros_like(dv_ref)
        for c, s0 in enumerate(starts):
            for h in range(2):
                kt_ref[c, h] = k_ref[0, s0:s0 + KEYS, h * BLOCK:(h + 1) * BLOCK].astype(F32).T.astype(BF16)

        def qtile(r0, bq, n):
            rows = pl.ds(r0, bq)
            qs = [q_ref[0, rows, h * BLOCK:(h + 1) * BLOCK] for h in range(2)]
            do_i = do_ref[0, rows, :]
            dos = _halves(do_i.astype(BF16), lo)
            do_st = jnp.concatenate(dos, axis=0)
            both = (do_i * o_ref[0, rows, :]).T
            dsum = (jnp.sum(both[:HEAD], axis=0, keepdims=True), jnp.sum(both[HEAD:], axis=0, keepdims=True))
            lses = _get_rows(lse_ref, r0, bq)
            t_idx = r0 + _iota2((KEYS, bq), 1)

            def kchunk(c, dqts, masked):
                s0, valid = _key_chunk(c, lp, t_idx, False, 0)
                valid = valid if masked else None
                keys = pl.ds(s0, KEYS)
                v_c = v_ref[0, keys, :]
                ps, out = [], []
                for h in range(2):
                    lanes = slice(h * BLOCK, (h + 1) * BLOCK)
                    s = _dot_nt(k_ref[0, keys, lanes], qs[h]) * scale
                    p = _where(valid, jnp.exp(s - lses[h]), 0.0)
                    ds = (p * (_dot_nt(v_c, dos[h]) - dsum[h]) * scale).astype(BF16)
                    dk_ref[0, keys, lanes] += jnp.dot(ds, qs[h], preferred_element_type=F32)
                    out.append(dqts[h] + jnp.dot(kt_ref[c, h], ds, preferred_element_type=F32))
                    ps.append(p.astype(BF16))
                dv_ref[0, keys, :] += jnp.dot(jnp.concatenate(ps, axis=1), do_st, preferred_element_type=F32)
                return tuple(out)

            zero = jnp.zeros((BLOCK, bq), F32)
            dq_a, dq_b = _walk_chunks(r0, n, kchunk, (zero, zero))
            dq_ref[0, rows, 0:BLOCK] = dq_a.T
            dq_ref[0, rows, BLOCK:2 * BLOCK] = dq_b.T

        _for_query_tiles(nb, qtile)

    wide = pl.BlockSpec((1, lp, 2 * BLOCK), lambda bi, hp: (bi, 0, hp))
    thin = pl.BlockSpec((1, lp, 2 * HEAD), lambda bi, hp: (bi, 0, hp))
    return pl.pallas_call(
        body, grid=(b, npair), in_specs=[wide, wide, thin, thin, _pair_stat_spec(nb), thin], out_specs=[wide, wide, thin],
        out_shape=[jax.ShapeDtypeStruct(qf.shape, F32), jax.ShapeDtypeStruct(qf.shape, F32), jax.ShapeDtypeStruct(v.shape, F32)],
        scratch_shapes=[pltpu.VMEM((len(starts), 2, BLOCK, KEYS), BF16)],
        compiler_params=_params(("parallel", "parallel")), name=name,
    )(qf, kf, v, o, lse, do)


SWA_KEYS = 2 * BLOCK + N_META


def _swa_keys(k_ref, v_ref, n, kv):
    prev = jnp.maximum(n - 1, 0)
    rows = lambda blk: pl.ds(pl.multiple_of(blk * BLOCK, BLOCK), BLOCK)
    mine = (_iota2((1, BLOCK), 1) >= HEAD).astype(jnp.int32) == kv

    def both_halves(ref):
        x = jnp.concatenate([ref[0, rows(prev), :], ref[0, rows(n), :], ref[0, N_PAD:BLOCK, :]], axis=0)
        return jnp.where(mine, x, pltpu.roll(x, HEAD, 1))

    slot = _iota2((SWA_KEYS, BLOCK), 0)
    s_idx = jnp.where(slot < 2 * BLOCK, (n - 1) * BLOCK + slot, slot - 2 * BLOCK + N_PAD)
    dist = n * BLOCK + _iota2((SWA_KEYS, BLOCK), 1) - s_idx
    band = (slot < 2 * BLOCK) & (dist >= 0) & (dist < SWA_WINDOW) & (s_idx >= BLOCK)
    meta = (slot >= 2 * BLOCK) & (dist >= 0)
    return both_halves(k_ref), both_halves(v_ref), band | meta, dist.astype(F32), prev


def _pad_keys(x):
    return jnp.concatenate([x, jnp.zeros((3 * BLOCK - SWA_KEYS, x.shape[1]), x.dtype)], axis=0)


def _swa_probs(q_h, kdup, valid, dist, head, sink_ref):
    slope = jnp.exp(jnp.full((1, 1), -8.0 * math.log(2.0) / SWA_HEADS, F32) * (head + 1).astype(F32))
    s = jnp.where(valid, _dot_nt(kdup, q_h) - slope * dist, NEG)
    sink = sink_ref[pl.ds(head, 1), 0:1]
    m = jnp.maximum(jnp.max(s, axis=0, keepdims=True), sink)
    e = jnp.where(valid, jnp.exp(s - m), 0.0)
    es = jnp.exp(sink - m)
    inv = 1.0 / (jnp.sum(e, axis=0, keepdims=True) + es)
    return e * inv, es * inv


SWA_PAIRS = SWA_HEADS // SWA_KV_HEADS // 2
SWA_GROUP = SWA_PAIRS * 2 * HEAD


def _swa_specs(b, lp):
    nb = lp // BLOCK
    qcol = lambda first: pl.BlockSpec((1, BLOCK, SWA_GROUP), lambda bi, kv, n: (bi, n, first // SWA_GROUP + kv))
    kcol = lambda first: pl.BlockSpec((1, lp, BLOCK), lambda bi, kv, n: (bi, 0, first // BLOCK))
    sink = pl.BlockSpec((SWA_HEADS, BLOCK), lambda bi, kv, n: (0, 0))
    return (b, SWA_KV_HEADS, nb), qcol, kcol, sink


def _swa_fwd(proj3, sinks, name):
    b, lp, _ = proj3.shape
    grid, qcol, kcol, sink = _swa_specs(b, lp)

    def body(q_ref, k_ref, v_ref, sink_ref, o_ref):
        kv, n = pl.program_id(1), pl.program_id(2)
        lo = _lo_lanes()
        kdup, vdup, valid, dist, _ = _swa_keys(k_ref, v_ref, n, kv)
        kdup = kdup.astype(BF16)
        vt = _split_rows_t(_pad_keys(vdup))
        for p in range(SWA_PAIRS):
            lanes = slice(p * BLOCK, (p + 1) * BLOCK)
            qs = _halves((q_ref[0, :, lanes] * HEAD_SCALE).astype(BF16), lo)
            probs = [_swa_probs(qs[hh], kdup, valid, dist, (kv * SWA_PAIRS + p) * 2 + hh, sink_ref)[0].astype(BF16) for hh in range(2)]
            o_ref[0, :, lanes] = jnp.dot(vt, jnp.concatenate([_pad_keys(pr) for pr in probs], axis=0), preferred_element_type=F32).T

    return pl.pallas_call(
        body, grid=grid, in_specs=[qcol(OD_Q), kcol(OD_K), kcol(OD_V), sink], out_specs=qcol(0),
        out_shape=jax.ShapeDtypeStruct((b, lp, SWA_HEADS * HEAD), F32),
        compiler_params=_params(("parallel", "parallel", "parallel")), name=name,
    )(proj3, proj3, proj3, sinks)


def _swa_bwd(proj3, sinks, do, name):
    b, lp, _ = proj3.shape
    nb = lp // BLOCK
    grid, qcol, kcol, sink = _swa_specs(b, lp)

    def body(q_ref, k_ref, v_ref, sink_ref, do_ref, dq_ref, dk_ref, dv_ref, dsink_ref, dk_acc, dv_acc):
        kv, n = pl.program_id(1), pl.program_id(2)

        @pl.when((n == 0) & (pl.program_id(0) == 0) & (kv == 0))
        def _():
            dsink_ref[...] = jnp.zeros_like(dsink_ref)

        @pl.when(n == 0)
        def _():
            dk_acc[...] = jnp.zeros_like(dk_acc)
            dv_acc[...] = jnp.zeros_like(dv_acc)

        lo = _lo_lanes()
        kdup, vdup, valid, dist, prev = _swa_keys(k_ref, v_ref, n, kv)
        kt = _split_rows_t(_pad_keys(kdup))
        kdup, vdup = kdup.astype(BF16), vdup.astype(BF16)
        dkc = jnp.zeros((SWA_KEYS, BLOCK), F32)
        dvc = jnp.zeros((SWA_KEYS, BLOCK), F32)
        for p in range(SWA_PAIRS):
            lanes = slice(p * BLOCK, (p + 1) * BLOCK)
            qs = _halves((q_ref[0, :, lanes] * HEAD_SCALE).astype(BF16), lo)
            dos = _halves(do_ref[0, :, lanes].astype(BF16), lo)
            dss, prs = [], []
            for hh in range(2):
                head = (kv * SWA_PAIRS + p) * 2 + hh
                pr, ps = _swa_probs(qs[hh], kdup, valid, dist, head, sink_ref)
                dp = _dot_nt(vdup, dos[hh])
                dsum = jnp.sum(pr * dp, axis=0, keepdims=True)
                dsink_ref[pl.ds(head, 1), :] += jnp.broadcast_to(-jnp.sum(ps * dsum, axis=1, keepdims=True), (1, BLOCK))
                dss.append((pr * (dp - dsum)).astype(BF16))
                prs.append(pr.astype(BF16))
            dq_ref[0, :, lanes] = jnp.dot(kt, jnp.concatenate([_pad_keys(d) for d in dss], axis=0), preferred_element_type=F32).T * HEAD_SCALE
            dkc = dkc + jnp.dot(jnp.concatenate(dss, axis=1), jnp.concatenate(qs, axis=0), preferred_element_type=F32)
            dvc = dvc + jnp.dot(jnp.concatenate(prs, axis=1), jnp.concatenate(dos, axis=0), preferred_element_type=F32)
        rows = lambda blk: pl.ds(pl.multiple_of(blk * BLOCK, BLOCK), BLOCK)
        for r, part in ((rows(prev), slice(0, BLOCK)), (rows(n), slice(BLOCK, 2 * BLOCK)), (slice(N_PAD, BLOCK), slice(2 * BLOCK, SWA_KEYS))):
            dk_acc[r, :] += dkc[part]
            dv_acc[r, :] += dvc[part]

        for acc, ref in ((dk_acc, dk_ref), (dv_acc, dv_ref)):
            @pl.when((n == nb - 1) & (kv == 0))
            def _():
                x = acc[...]
                ref[0] = x + pltpu.roll(x, HEAD, 1)

            @pl.when((n == nb - 1) & (kv == 1))
            def _():
                x = acc[...]
                ref[0] = jnp.where(lo, ref[0], x + pltpu.roll(x, HEAD, 1))

    kvout = pl.BlockSpec((1, lp, BLOCK), lambda bi, kv, n: (bi, 0, 0))
    kvshape = jax.ShapeDtypeStruct((b, lp, BLOCK), F32)
    return pl.pallas_call(
        body, grid=grid, in_specs=[qcol(OD_Q), kcol(OD_K), kcol(OD_V), sink, qcol(0)], out_specs=[qcol(0), kvout, kvout, sink],
        out_shape=[jax.ShapeDtypeStruct((b, lp, SWA_HEADS * HEAD), F32), kvshape, kvshape, jax.ShapeDtypeStruct((SWA_HEADS, BLOCK), F32)],
        scratch_shapes=[pltpu.VMEM((lp, BLOCK), F32), pltpu.VMEM((lp, BLOCK), F32)],
        compiler_params=_params(("arbitrary", "arbitrary", "arbitrary")), name=name,
    )(proj3, proj3, proj3, sinks, do)


def _kernel_weights(ev_w_in, ev_w_uq, ev_w_ukv, od_w_in):
    zeros = lambda r, c: jnp.zeros((r, c), ev_w_in.dtype)
    q_sb, k_sb, v_sb, g_sb, c_q, c_kv, k_r, g_mla = jnp.split(ev_w_in, [512, 1024, 1536, 2048, 2304, 2432, 2464], axis=1)
    w0 = jnp.concatenate([g_sb, g_mla, q_sb, k_sb, v_sb, c_q, c_kv, zeros(D_MODEL, MLA_NOPE), k_r, zeros(D_MODEL, 32)], axis=1)
    uq = ev_w_uq.reshape(MLA_Q_LORA, MLA_HEADS, MLA_NOPE + MLA_ROPE)
    wq = jnp.pad(uq, ((0, 0), (0, 0), (0, BLOCK - MLA_NOPE - MLA_ROPE))).reshape(MLA_Q_LORA, MLA_HEADS * BLOCK)
    ukv = ev_w_ukv.reshape(MLA_KV_LORA, MLA_HEADS, BLOCK)
    wk = jnp.pad(ukv[:, :, :MLA_NOPE], ((0, 0), (0, 0), (0, BLOCK - MLA_NOPE))).reshape(MLA_KV_LORA, MLA_HEADS * BLOCK)
    wv = ukv[:, :, MLA_NOPE:].reshape(MLA_KV_LORA, MLA_HEADS * HEAD)
    q, k, v, g = jnp.split(od_w_in, [1024, 1152, 1280], axis=1)
    w1 = jnp.concatenate([g, q, k, v], axis=1)
    return w0, wq, wk, wv, w1


def _original_grads(dw0, dwq, dwk, dwv, dw1):
    sl = lambda a, first, n: a[:, first:first + n]
    d_ev_w_in = jnp.concatenate([sl(dw0, EV_Q, 512), sl(dw0, EV_K, 512), sl(dw0, EV_V, 512), sl(dw0, EV_G, 512), sl(dw0, EV_CQ, 256),
                                 sl(dw0, EV_CKV, 128), sl(dw0, EV_KR + MLA_NOPE, MLA_ROPE), sl(dw0, EV_G + 512, 512)], axis=1)
    d_uq = dwq.reshape(MLA_Q_LORA, MLA_HEADS, BLOCK)[:, :, :MLA_NOPE + MLA_ROPE].reshape(MLA_Q_LORA, -1)
    d_ukv = jnp.concatenate([dwk.reshape(MLA_KV_LORA, MLA_HEADS, BLOCK)[:, :, :MLA_NOPE], dwv.reshape(MLA_KV_LORA, MLA_HEADS, HEAD)],
                            axis=2).reshape(MLA_KV_LORA, -1)
    d_od_w_in = jnp.concatenate([sl(dw1, OD_Q, 1024), sl(dw1, OD_K, 128), sl(dw1, OD_V, 128), sl(dw1, OD_G, 1024)], axis=1)
    return d_ev_w_in, d_uq, d_ukv, d_od_w_in


def _meta_rows_sum(dh0_3):
    b, _, d = dh0_3.shape

    def body(x_ref, o_ref):
        acc = x_ref[0, N_PAD:BLOCK, :]
        for i in range(1, b):
            acc = acc + x_ref[i, N_PAD:BLOCK, :]
        o_ref[...] = acc

    return pl.pallas_call(
        body, grid=(1,), in_specs=[pl.BlockSpec((b, BLOCK, d), lambda i: (0, 0, 0))], out_specs=pl.BlockSpec((N_META, d), lambda i: (0, 0)),
        out_shape=jax.ShapeDtypeStruct((N_META, d), F32), compiler_params=_params(("arbitrary",)), name="meta_rows_sum",
    )(dh0_3)


def _local_step(x, meta, norm_g, final_g, gq, gkv, sinks, target, ev_w_in, ev_w_uq, ev_w_ukv, wo0, od_w_in, wo1):
    b, seq, d = x.shape
    lp = seq + BLOCK
    t = b * lp
    w0, wq, wk, wv, w1 = _kernel_weights(ev_w_in, ev_w_uq, ev_w_ukv, od_w_in)
    h0 = jnp.concatenate([jnp.zeros((b, N_PAD, d), F32), jnp.broadcast_to(meta[None], (b, N_META, d)), x], axis=1).reshape(t, d)
    tabs = _rope_tables(lp)
    g0, g1 = norm_g[0:1], norm_g[1:2]

    hn0 = _rms_fwd(h0, g0, "norm0")
    proj0 = _mm(hn0, w0, "inproj0")
    p0 = proj0.reshape(b, lp, EV_N)
    o_sb, sb_tot = _sb_fwd(p0, "sb_fwd")
    qf, kf, v = _mla_prep_fwd(p0, gq, gkv, wq, wk, wv, tabs, "mla_prep_fwd")
    o_mla, lse = _mla_fwd(qf, kf, v, "mla_fwd")
    o0 = [o_sb.reshape(t, -1), o_mla.reshape(t, -1)]
    ao0 = _gate_fwd(o0, proj0, "gate0")
    h1 = _mm(ao0, wo0, "outproj0", res=h0)

    hn1 = _rms_fwd(h1, g1, "norm1")
    proj1 = _mm(hn1, w1, "inproj1")
    p1 = proj1.reshape(b, lp, OD_N)
    sinks_b = jnp.broadcast_to(sinks.reshape(SWA_HEADS, 1), (SWA_HEADS, BLOCK))
    o1 = _swa_fwd(p1, sinks_b, "swa_fwd").reshape(t, -1)
    ao1 = _gate_fwd([o1], proj1, "gate1")
    h2 = _mm(ao1, wo1, "outproj1", res=h1)

    dh2, d_final_g, loss = _loss_head(h2, final_g.reshape(1, d), target, b, lp)

    d_wo1 = _mm_tn(ao1, dh2, "d_wo1")
    dao1 = _mm_nt(dh2, wo1, "d_ao1")
    (do1,), dg1 = _gate_bwd(dao1, [o1], proj1, "gate1_bwd")
    dq1, dk4, dv4, d_sinks = _swa_bwd(p1, sinks_b, do1.reshape(b, lp, -1), "swa_bwd")
    unheads = lambda a: a.reshape(t, SWA_KV_HEADS * HEAD).astype(BF16)
    dproj1 = jnp.concatenate([dg1, dq1.reshape(t, -1).astype(BF16), unheads(dk4), unheads(dv4)], axis=1)
    d_w1 = _mm_tn(hn1, dproj1, "d_w1")
    dhn1 = _mm_nt(dproj1, w1, "d_hn1")
    dh1, d_g1 = _rms_bwd(h1, g1, dhn1, dh2, "norm1_bwd")

    d_wo0 = _mm_tn(ao0, dh1, "d_wo0")
    dao0 = _mm_nt(dh1, wo0, "d_ao0")
    (do_sb, do_mla), dg0 = _gate_bwd(dao0, o0, proj0, "gate0_bwd")
    dq_sb, dk_sb, dv_sb = _sb_bwd(p0, sb_tot, do_sb.reshape(b, lp, -1), "sb_bwd")
    dqf, dkf, dv = _mla_bwd(qf, kf, v, o_mla, lse, do_mla.reshape(b, lp, -1), "mla_bwd")
    dcq, dckv, dkr, d_wq, d_wk, d_wv, d_gq, d_gkv = _mla_prep_bwd(p0, gq, gkv, wq, wk, wv, tabs, dqf, dkf, dv, "mla_prep_bwd")
    flat = lambda a: a.reshape(t, -1).astype(BF16)
    dproj0 = jnp.concatenate([dg0, flat(dq_sb), flat(dk_sb), flat(dv_sb), flat(dcq), flat(dckv), flat(dkr)], axis=1)
    d_w0 = _mm_tn(hn0, dproj0, "d_w0")
    dhn0 = _mm_nt(dproj0, w0, "d_hn0")
    dh0, d_g0 = _rms_bwd(h0, g0, dhn0, dh1, "norm0_bwd")
    dh0 = dh0.reshape(b, lp, d)

    d_ev_w_in, d_uq, d_ukv, d_od_w_in = _original_grads(d_w0, d_wq, d_wk, d_wv, d_w1)
    grads = dict(meta=_meta_rows_sum(dh0), norm_g=jnp.concatenate([d_g0, d_g1], axis=0), final_g=d_final_g.reshape(d),
                 ev_w_in=d_ev_w_in, ev_q_norm_g=d_gq, ev_kv_norm_g=d_gkv, ev_w_uq=d_uq, ev_w_ukv=d_ukv, ev_w_out=d_wo0,
                 od_w_in=d_od_w_in, od_sinks=d_sinks[:, 0].reshape(1, SWA_HEADS), od_w_out=d_wo1)
    return loss, dh0[:, BLOCK:], grads


MESH = pl.DeviceIdType.MESH
ANY = pl.BlockSpec(memory_space=pl.ANY)


def _place():
    return lax.axis_index("x"), lax.axis_index("y"), lax.axis_index("c")


def _other_chips(x, y):
    return [(1 - x, y), (x, 1 - y), (1 - x, 1 - y)]


def _with_own_slot(slots, own):
    me = 2 * lax.axis_index("x") + lax.axis_index("y")
    return lax.dynamic_update_slice(slots, own[None], (me,) + (0,) * own.ndim)


def _gather_weights(packs, meta, name):
    n = len(packs)

    def body(*refs):
        ins, m_ref, outs, mo_ref = refs[:n], refs[n], refs[n + 1:2 * n + 1], refs[2 * n + 1]
        send_sems, recv_sems = refs[2 * n + 2:]
        x, y, c = _place()
        me, sib = 2 * x + y, (x, y, 1 - c)
        chips = _other_chips(x, y)

        def copy(k, src, dst, to):
            return pltpu.make_async_remote_copy(src_ref=src, dst_ref=dst, send_sem=send_sems.at[k], recv_sem=recv_sems.at[k], device_id=to,
                                                device_id_type=MESH)

        def half(i, chip, h):
            rows = packs[i].shape[0] // 2
            return outs[i].at[chip, pl.ds(h * rows, rows), :]

        def mine(i):
            rows = packs[i].shape[0] // 2
            return ins[i].at[pl.ds(c * rows, rows), :]

        sent = [copy(6 * i + k, mine(i), half(i, me, c), (px, py, c)) for i in range(n) for k, (px, py) in enumerate(chips)]
        sent += [copy(6 * n + k, m_ref, mo_ref.at[me], (px, py, c)) for k, (px, py) in enumerate(chips)]
        for cp in sent:
            cp.start()
        for i in range(n):
            for k, (px, py) in enumerate(chips):
                landed = half(i, 2 * px + py, c)
                copy(6 * i + k, mine(i), landed, (px, py, c)).wait_recv()
                fwd = copy(6 * i + 3 + k, landed, landed, sib)
                fwd.start()
                sent.append(fwd)
        for k, (px, py) in enumerate(chips):
            for i in range(n):
                other = half(i, 2 * px + py, 1 - c)
                copy(6 * i + 3 + k, other, other, sib).wait_recv()
            copy(6 * n + k, m_ref, mo_ref.at[2 * px + py], (px, py, c)).wait_recv()
        for cp in sent:
            cp.wait_send()

    nsem = 6 * n + 3
    res = pl.pallas_call(
        body, in_specs=[ANY] * (n + 1), out_specs=[ANY] * (n + 1),
        out_shape=[jax.ShapeDtypeStruct((N_CHIPS,) + a.shape, a.dtype) for a in list(packs) + [meta]],
        scratch_shapes=[pltpu.SemaphoreType.DMA((nsem,)), pltpu.SemaphoreType.DMA((nsem,))],
        name=name,
    )(*packs, meta)
    return [_with_own_slot(r, a) for r, a in zip(res[:n], packs)], _with_own_slot(res[n], meta)


def _grads_to_sibling(gs, name):
    n = len(gs)

    def body(*refs):
        ins, outs, send_sems, recv_sems = refs[:n], refs[n:2 * n], refs[2 * n], refs[2 * n + 1]
        x, y, c = _place()
        cps = []
        for i in range(n):
            rows = gs[i].shape[1] // 2
            cps.append(pltpu.make_async_remote_copy(src_ref=ins[i].at[:, pl.ds((1 - c) * rows, rows), :], dst_ref=outs[i],
                                                    send_sem=send_sems.at[i], recv_sem=recv_sems.at[i], device_id=(x, y, 1 - c),
                                                    device_id_type=MESH))
        for cp in cps:
            cp.start()
        for cp in cps:
            cp.wait()

    return pl.pallas_call(
        body, in_specs=[ANY] * n, out_specs=[ANY] * n,
        out_shape=[jax.ShapeDtypeStruct((g.shape[0], g.shape[1] // 2, g.shape[2]), g.dtype) for g in gs],
        scratch_shapes=[pltpu.SemaphoreType.DMA((n,)), pltpu.SemaphoreType.DMA((n,))],
        name=name,
    )(*gs)


def _share_halves(rs, name):
    n = len(rs)

    def body(*refs):
        ins, outs, send_sems, recv_sems = refs[:n], refs[n:2 * n], refs[2 * n], refs[2 * n + 1]
        x, y, c = _place()
        cps = [pltpu.make_async_remote_copy(src_ref=ins[i], dst_ref=outs[i], send_sem=send_sems.at[i], recv_sem=recv_sems.at[i],
                                            device_id=(x, y, 1 - c), device_id_type=MESH) for i in range(n)]
        for cp in cps:
            cp.start()
        for cp in cps:
            cp.wait()

    theirs = pl.pallas_call(
        body, in_specs=[ANY] * n, out_specs=[ANY] * n, out_shape=[jax.ShapeDtypeStruct(r.shape, r.dtype) for r in rs],
        scratch_shapes=[pltpu.SemaphoreType.DMA((n,)), pltpu.SemaphoreType.DMA((n,))],
        name=name,
    )(*rs)
    first = lax.axis_index("c") == 0
    return [jnp.where(first, jnp.concatenate([r, t], axis=0), jnp.concatenate([t, r], axis=0)) for r, t in zip(rs, theirs)]


def _chip_scatter(ss, name):
    n = len(ss)

    def body(*refs):
        ins, outs, send_sems, recv_sems = refs[:n], refs[n:2 * n], refs[2 * n], refs[2 * n + 1]
        x, y, c = _place()
        me = 2 * x + y
        chips = _other_chips(x, y)
        for i in range(n):
            for k, (px, py) in enumerate(chips):
                pltpu.make_async_remote_copy(src_ref=ins[i].at[2 * px + py], dst_ref=outs[i].at[me], send_sem=send_sems.at[3 * i + k],
                                             recv_sem=recv_sems.at[3 * i + k], device_id=(px, py, c), device_id_type=MESH).start()
        for i in range(n):
            for k, (px, py) in enumerate(chips):
                cp = pltpu.make_async_remote_copy(src_ref=ins[i].at[2 * px + py], dst_ref=outs[i].at[2 * px + py],
                                                  send_sem=send_sems.at[3 * i + k], recv_sem=recv_sems.at[3 * i + k],
                                                  device_id=(px, py, c), device_id_type=MESH)
                cp.wait_recv()
                cp.wait_send()

    parts = pl.pallas_call(
        body, in_specs=[ANY] * n, out_specs=[ANY] * n, out_shape=[jax.ShapeDtypeStruct(s.shape, s.dtype) for s in ss],
        scratch_shapes=[pltpu.SemaphoreType.DMA((3 * n,)), pltpu.SemaphoreType.DMA((3 * n,))],
        name=name,
    )(*ss)
    me = 2 * lax.axis_index("x") + lax.axis_index("y")
    return [_with_own_slot(p, lax.dynamic_index_in_dim(s, me, axis=0, keepdims=False)) for p, s in zip(parts, ss)]


HBM_SPACE = pltpu.MemorySpace.HBM


def _on_sequencer(name, collective_id, n_sems, body):
    @pl.kernel(mesh=plsc.ScalarSubcoreMesh(axis_name="sequencer", num_cores=1), name=name,
               scratch_types=(pltpu.SemaphoreType.DMA((n_sems,)), pltpu.SemaphoreType.DMA((n_sems,))),
               compiler_params=pltpu.CompilerParams(collective_id=collective_id))
    def launch(send_sems, recv_sems):
        body(send_sems, recv_sems)

    launch()


def _handshake(peers):
    barrier = pltpu.get_barrier_semaphore()
    for peer in peers:
        pl.semaphore_signal(barrier, inc=1, device_id=peer, device_id_type=MESH)
    pl.semaphore_wait(barrier, len(peers))


def _gather_on_sequencer(packs, name):
    n = len(packs)
    ins = [jax.new_ref(p, memory_space=HBM_SPACE) for p in packs]
    outs = [jax.empty_ref(jax.ShapeDtypeStruct((N_CHIPS,) + p.shape, p.dtype), memory_space=HBM_SPACE) for p in packs]

    def body(send_sems, recv_sems):
        x, y, c = _place()
        me, sib = 2 * x + y, (x, y, 1 - c)
        chips = _other_chips(x, y)
        _handshake([(px, py, c) for px, py in chips] + [sib])

        def copy(k, src, dst, to):
            return pltpu.make_async_remote_copy(src_ref=src, dst_ref=dst, send_sem=send_sems.at[k], recv_sem=recv_sems.at[k], device_id=to,
                                                device_id_type=MESH)

        def half(i, chip, h):
            rows = packs[i].shape[0] // 2
            return outs[i].at[chip, pl.ds(h * rows, rows), :]

        def mine(i):
            rows = packs[i].shape[0] // 2
            return ins[i].at[pl.ds(c * rows, rows), :]

        sent = [copy(6 * i + k, mine(i), half(i, me, c), (px, py, c)) for i in range(n) for k, (px, py) in enumerate(chips)]
        for cp in sent:
            cp.start()
        for i in range(n):
            for k, (px, py) in enumerate(chips):
                landed = half(i, 2 * px + py, c)
                copy(6 * i + k, mine(i), landed, (px, py, c)).wait_recv()
                fwd = copy(6 * i + 3 + k, landed, landed, sib)
                fwd.start()
                sent.append(fwd)
        for i in range(n):
            for k, (px, py) in enumerate(chips):
                other = half(i, 2 * px + py, 1 - c)
                copy(6 * i + 3 + k, other, other, sib).wait_recv()
        for cp in sent:
            cp.wait_send()

    _on_sequencer(name, 1, 6 * n, body)
    return [_with_own_slot(o[...], p) for o, p in zip(outs, packs)]


def _grads_to_sibling_on_sequencer(gs, name, collective_id):
    n = len(gs)
    ins = [jax.new_ref(g, memory_space=HBM_SPACE) for g in gs]
    outs = [jax.empty_ref(jax.ShapeDtypeStruct((g.shape[0], g.shape[1] // 2, g.shape[2]), g.dtype), memory_space=HBM_SPACE) for g in gs]

    def body(send_sems, recv_sems):
        x, y, c = _place()
        _handshake([(x, y, 1 - c)])
        cps = []
        for i in range(n):
            rows = gs[i].shape[1] // 2
            cps.append(pltpu.make_async_remote_copy(src_ref=ins[i].at[:, pl.ds((1 - c) * rows, rows), :], dst_ref=outs[i],
                                                    send_sem=send_sems.at[i], recv_sem=recv_sems.at[i], device_id=(x, y, 1 - c),
                                                    device_id_type=MESH))
        for cp in cps:
            cp.start()
        for cp in cps:
            cp.wait()

    _on_sequencer(name, collective_id, n, body)
    return [o[...] for o in outs]


def _chip_scatter_on_sequencer(ss, name, collective_id):
    n = len(ss)
    ins = [jax.new_ref(s, memory_space=HBM_SPACE) for s in ss]
    outs = [jax.empty_ref(jax.ShapeDtypeStruct(s.shape, s.dtype), memory_space=HBM_SPACE) for s in ss]

    def body(send_sems, recv_sems):
        x, y, c = _place()
        me = 2 * x + y
        chips = _other_chips(x, y)
        _handshake([(px, py, c) for px, py in chips])
        for i in range(n):
            for k, (px, py) in enumerate(chips):
                pltpu.make_async_remote_copy(src_ref=ins[i].at[2 * px + py], dst_ref=outs[i].at[me], send_sem=send_sems.at[3 * i + k],
                                             recv_sem=recv_sems.at[3 * i + k], device_id=(px, py, c), device_id_type=MESH).start()
        for i in range(n):
            for k, (px, py) in enumerate(chips):
                cp = pltpu.make_async_remote_copy(src_ref=ins[i].at[2 * px + py], dst_ref=outs[i].at[2 * px + py],
                                                  send_sem=send_sems.at[3 * i + k], recv_sem=recv_sems.at[3 * i + k],
                                                  device_id=(px, py, c), device_id_type=MESH)
                cp.wait_recv()
                cp.wait_send()

    _on_sequencer(name, collective_id, 3 * n, body)
    me = 2 * lax.axis_index("x") + lax.axis_index("y")
    return [_with_own_slot(o[...], lax.dynamic_index_in_dim(s, me, axis=0, keepdims=False)) for o, s in zip(outs, ss)]


def _all_reduce_small(v, name):
    shape = v.shape

    def body(v_ref, o_ref, slots, send_sems, recv_sems):
        x, y, c = _place()
        me = 4 * x + 2 * y + c
        slots[me] = v_ref[...]
        for r in range(1, N_DEV):
            peer = (x ^ (r >> 2), y ^ ((r >> 1) & 1), c ^ (r & 1))
            pltpu.make_async_remote_copy(src_ref=v_ref, dst_ref=slots.at[me], send_sem=send_sems.at[r - 1], recv_sem=recv_sems.at[r - 1],
                                         device_id=peer, device_id_type=MESH).start()
        for r in range(1, N_DEV):
            peer = (x ^ (r >> 2), y ^ ((r >> 1) & 1), c ^ (r & 1))
            cp = pltpu.make_async_remote_copy(src_ref=v_ref, dst_ref=slots.at[4 * peer[0] + 2 * peer[1] + peer[2]], send_sem=send_sems.at[r - 1],
                                              recv_sem=recv_sems.at[r - 1], device_id=peer, device_id_type=MESH)
            cp.wait_recv()
            cp.wait_send()
        acc = slots[0]
        for d in range(1, N_DEV):
            acc = acc + slots[d]
        o_ref[...] = acc

    vm = pl.BlockSpec(memory_space=pltpu.VMEM)
    return pl.pallas_call(
        body, in_specs=[vm], out_specs=vm, out_shape=jax.ShapeDtypeStruct(shape, F32),
        scratch_shapes=[pltpu.VMEM((N_DEV,) + shape, F32), pltpu.SemaphoreType.DMA((N_DEV - 1,)), pltpu.SemaphoreType.DMA((N_DEV - 1,))],
        name=name,
    )(v)


def _add_sibling(g, gsib, core, name):
    n, _, cdim = g.shape
    half = gsib.shape[1]
    tr = half // 2

    def body(core_ref, a_ref, b_ref, o_ref):
        o_ref[...] = (a_ref[...] + b_ref[...]).astype(o_ref.dtype)

    blk = pl.BlockSpec((1, tr, cdim), lambda j, i, core_ref: (j, i, 0))
    return pl.pallas_call(
        body,
        grid_spec=pltpu.PrefetchScalarGridSpec(
            num_scalar_prefetch=1, grid=(n, half // tr),
            in_specs=[pl.BlockSpec((1, tr, cdim), lambda j, i, core_ref: (j, core_ref[0] * (half // tr) + i, 0)), blk], out_specs=blk),
        out_shape=jax.ShapeDtypeStruct(gsib.shape, BF16), compiler_params=_params(("parallel", "parallel")), name=name,
    )(core, g, gsib)


def _sum_parts(parts, name):
    n, r, cdim = parts.shape
    tr = r // 2

    def body(p_ref, o_ref):
        acc = p_ref[0].astype(F32)
        for j in range(1, n):
            acc = acc + p_ref[j].astype(F32)
        o_ref[...] = acc

    return pl.pallas_call(
        body, grid=(r // tr,), in_specs=[pl.BlockSpec((n, tr, cdim), lambda i: (0, i, 0))],
        out_specs=pl.BlockSpec((tr, cdim), lambda i: (i, 0)), out_shape=jax.ShapeDtypeStruct((r, cdim), F32),
        compiler_params=_params(("parallel",)), name=name,
    )(parts)


def _adamw(parts, w, m, v, name):
    npart, r, cdim = parts.shape
    tr = r // 4 if r % 32 == 0 else r

    def body(p_ref, w_ref, m_ref, v_ref, g_ref, d_ref, nm_ref, nv_ref):
        g = p_ref[0]
        for j in range(1, npart):
            g = g + p_ref[j]
        m_new = ADAM_B1 * m_ref[...] + (1.0 - ADAM_B1) * g
        v_new = ADAM_B2 * v_ref[...] + (1.0 - ADAM_B2) * (g * g)
        m_hat = m_new / (1.0 - ADAM_B1 ** ADAM_STEP)
        v_hat = v_new / (1.0 - ADAM_B2 ** ADAM_STEP)
        g_ref[...] = g
        d_ref[...] = -ADAM_LR * (m_hat / (jnp.sqrt(v_hat) + ADAM_EPS) + ADAM_WD * w_ref[...])
        nm_ref[...] = m_new
        nv_ref[...] = v_new

    blk = pl.BlockSpec((tr, cdim), lambda i: (i, 0))
    shp = jax.ShapeDtypeStruct((r, cdim), F32)
    return pl.pallas_call(
        body, grid=(r // tr,), in_specs=[pl.BlockSpec((npart, tr, cdim), lambda i: (0, i, 0)), blk, blk, blk], out_specs=[blk] * 4,
        out_shape=[shp] * 4, compiler_params=_params(("parallel",)), name=name,
    )(parts, w, m, v)


BIG = ("ev_w_in", "ev_w_uq", "ev_w_ukv", "ev_w_out", "od_w_in", "od_w_out", "meta")
SMALL = ("norm_g", "final_g", "ev_q_norm_g", "ev_kv_norm_g", "od_sinks")
SMALL_SHAPE = (8, 512)
BY_ROWS = ("ev_w_out", "od_w_out")

EV_IN_SHARD, OD_IN_SHARD, UQ_SHARD = 2976 // N_CHIPS, 2304 // N_CHIPS, 768 // N_CHIPS


def _pack_big(a, lead=()):
    dtype = a["ev_w_in"].dtype
    z = lambda r, c: jnp.zeros(lead + (r, c), dtype)
    ax = len(lead)
    pad_to = lambda x, width: jnp.concatenate([x, z(x.shape[ax], width - x.shape[ax + 1])], axis=ax + 1)
    corner = jnp.concatenate([a["ev_w_ukv"], a["meta"], z(256 - MLA_KV_LORA - N_META, 256)], axis=ax)
    return (pad_to(a["ev_w_in"], 768), jnp.concatenate([pad_to(a["ev_w_uq"], 256), corner], axis=ax + 1),
            pad_to(a["od_w_in"], 640), jnp.concatenate([a["ev_w_out"], a["od_w_out"]], axis=ax + 1))


N_FIRST = 2


def _unpack_big(p_in0, p_lat, p_in1, p_out):
    return dict(ev_w_in=p_in0[..., :EV_IN_SHARD], od_w_in=p_in1[..., :OD_IN_SHARD], ev_w_out=p_out[..., :D_MODEL],
                od_w_out=p_out[..., D_MODEL:], ev_w_uq=p_lat[..., :UQ_SHARD], ev_w_ukv=p_lat[..., :MLA_KV_LORA, 256:],
                meta=p_lat[..., MLA_KV_LORA:MLA_KV_LORA + N_META, 256:])


def _chip_shards(full, by_rows):
    if by_rows:
        return full.reshape(N_CHIPS, full.shape[0] // N_CHIPS, full.shape[1])
    return full.reshape(full.shape[0], N_CHIPS, -1).transpose(1, 0, 2)


def _from_chip_shards(slots, by_rows):
    if by_rows:
        return slots.reshape(-1, slots.shape[2])
    return slots.transpose(1, 0, 2).reshape(slots.shape[1], -1)


def _pack_small(arrs, extra=None):
    flat = [a.reshape(-1) for a in arrs] + ([] if extra is None else [extra.reshape(-1)])
    used = sum(f.shape[0] for f in flat)
    return jnp.pad(jnp.concatenate(flat), (0, SMALL_SHAPE[0] * SMALL_SHAPE[1] - used)).reshape(SMALL_SHAPE)


def _unpack_small(p, shapes):
    flat, out, at = p.reshape(-1), [], 0
    for s in shapes:
        n = int(np.prod(s))
        out.append(flat[at:at + n].reshape(s))
        at += n
    return out, flat[at]


def kernel(x, meta, norm_g, final_g, ev_w_in, ev_q_norm_g, ev_kv_norm_g, ev_w_uq, ev_w_ukv, ev_w_out, od_w_in, od_sinks, od_w_out, loss_target, m_meta, m_norm_g, m_final_g, m_ev_w_in, m_ev_q_norm_g, m_ev_kv_norm_g, m_ev_w_uq, m_ev_w_ukv, m_ev_w_out, m_od_w_in, m_od_sinks, m_od_w_out, v_meta, v_norm_g, v_final_g, v_ev_w_in, v_ev_q_norm_g, v_ev_kv_norm_g, v_ev_w_uq, v_ev_w_ukv, v_ev_w_out, v_od_w_in, v_od_sinks, v_od_w_out):
    given = dict(locals())
    two_d = lambda a: a[0] if a.ndim == 3 else a
    packs = {k: _pack_big({n: two_d(given[k + n]) for n in BIG}) for k in ("", "m_", "v_")}

    wbf = [p.astype(BF16) for p in packs[""]]
    later = _gather_on_sequencer(wbf[N_FIRST:], "gather_later_weights")
    first, meta_all = _gather_weights(wbf[:N_FIRST], meta, "gather_weights")
    full = {n: _from_chip_shards(a, n in BY_ROWS) for n, a in _unpack_big(*first, *later).items()}
    meta_full = _from_chip_shards(meta_all, False)

    loss, grad_x, grads = _local_step(x, meta_full, norm_g, final_g, ev_q_norm_g, ev_kv_norm_g, od_sinks, loss_target,
                                      full["ev_w_in"], full["ev_w_uq"], full["ev_w_ukv"], full["ev_w_out"], full["od_w_in"], full["od_w_out"])

    gpacks = _pack_big({n: _chip_shards(grads[n], n in BY_ROWS) for n in BIG}, lead=(N_CHIPS,))
    core = lax.axis_index("c").astype(jnp.int32).reshape(1)

    def reduce_group(group, tag, to_sibling, scatter):
        sums = [_add_sibling(g, s, core, f"add_sibling_{tag}{i}") for i, (g, s) in enumerate(zip(group, to_sibling(group)))]
        parts = [_sum_parts(p, f"add_chips_{tag}{i}") for i, p in enumerate(scatter(sums))]
        return _share_halves(parts, "reduced_to_sibling_" + tag)

    reduced_later = reduce_group(gpacks[N_FIRST:], "later", lambda g: _grads_to_sibling_on_sequencer(g, "grads_to_sibling_later", 3),
                                 lambda s: _chip_scatter_on_sequencer(s, "grads_to_chips_later", 2))
    reduced_first = reduce_group(gpacks[:N_FIRST], "first", lambda g: _grads_to_sibling(g, "grads_to_sibling_first"),
                                 lambda s: _chip_scatter_on_sequencer(s, "grads_to_chips_first", 4))
    reduced = reduced_first + reduced_later
    updated = [_adamw(r[None], packs[""][i], packs["m_"][i], packs["v_"][i], f"adamw_matrices_{i}") for i, r in enumerate(reduced)]
    big_out = [{n: a.reshape(given[n].shape) for n, a in _unpack_big(*outs).items()} for outs in zip(*updated)]

    small_shapes = [given[n].shape for n in SMALL]
    ssum = _all_reduce_small(_pack_small([grads[n] for n in SMALL], loss[0, 0]), "reduce_vectors")
    small_out = _adamw(ssum[None], _pack_small([given[n] for n in SMALL]), _pack_small([given["m_" + n] for n in SMALL]),
                       _pack_small([given["v_" + n] for n in SMALL]), "adamw_vectors")
    total_loss = ssum.reshape(-1)[sum(int(np.prod(s)) for s in small_shapes)]
    small_out = [_unpack_small(o, small_shapes)[0] for o in small_out]

    names = ("meta", "norm_g", "final_g", "ev_w_in", "ev_q_norm_g", "ev_kv_norm_g", "ev_w_uq", "ev_w_ukv", "ev_w_out", "od_w_in", "od_sinks",
             "od_w_out")
    outs = [total_loss, grad_x]
    for kind in range(4):
        for n in names:
            outs.append(big_out[kind][n] if n in BIG else small_out[kind][SMALL.index(n)])
    return tuple(outs)
```

```python
import functools
import math

import numpy as np
import jax
import jax.numpy as jnp
from jax import lax
from jax.experimental import pallas as pl
from jax.experimental.pallas import tpu as pltpu
from jax.experimental.pallas import tpu_sc as plsc

F32 = jnp.float32
BF16 = jnp.bfloat16

D_MODEL = 1024
BLOCK = 128
N_META = 16
N_PAD = BLOCK - N_META
NORM_EPS = 1e-6
NEG = -1e30
HEAD = 64
SB_HEADS = 8
MLA_HEADS = 8
MLA_Q_LORA = 256
MLA_KV_LORA = 128
MLA_NOPE = 64
MLA_ROPE = 32
ROPE_BASE = 10000.0
SWA_HEADS = 16
SWA_KV_HEADS = 2
SWA_WINDOW = 128
N_CHIPS = 4
N_DEV = 8

ADAM_LR = 0.001
ADAM_B1 = 0.9
ADAM_B2 = 0.999
ADAM_EPS = 1e-08
ADAM_WD = 0.01
ADAM_STEP = 10

VMEM_LIMIT = 48 * 1024 * 1024

EV_G, EV_Q, EV_K, EV_V, EV_CQ, EV_CKV, EV_KR, EV_N = 0, 1024, 1536, 2048, 2560, 2816, 2944, 3072
OD_G, OD_Q, OD_K, OD_V, OD_N = 0, 1024, 2048, 2176, 2304


def _params(sem=None):
    return pltpu.CompilerParams(dimension_semantics=sem, vmem_limit_bytes=VMEM_LIMIT)


def _row_tile(m):
    return 256 if m % 256 == 0 else 128


def _matmul_rows(m):
    for c in (1088, 1024, 768, 640, 512, 384, 256):
        if m % c == 0:
            return c
    return 128


def _dot(a, b):
    return jnp.dot(a.astype(BF16), b.astype(BF16), preferred_element_type=F32)


def _dot_nt(a, b):
    return lax.dot_general(a.astype(BF16), b.astype(BF16), (((1,), (1,)), ((), ())), preferred_element_type=F32)


def _dot_tn(a, b):
    return lax.dot_general(a.astype(BF16), b.astype(BF16), (((0,), (0,)), ((), ())), preferred_element_type=F32)


def _rms_fwd(h, g, name):
    t, d = h.shape
    tm = _row_tile(t)

    def body(h_ref, g_ref, o_ref):
        x = h_ref[...]
        r = lax.rsqrt(jnp.mean(x * x, axis=-1, keepdims=True) + NORM_EPS)
        o_ref[...] = ((x * r) * g_ref[...]).astype(o_ref.dtype)

    return pl.pallas_call(
        body, grid=(t // tm,),
        in_specs=[pl.BlockSpec((tm, d), lambda i: (i, 0)), pl.BlockSpec((1, d), lambda i: (0, 0))],
        out_specs=pl.BlockSpec((tm, d), lambda i: (i, 0)),
        out_shape=jax.ShapeDtypeStruct((t, d), BF16), compiler_params=_params(("parallel",)), name=name,
    )(h, g)


def _rms_bwd(h, g, dy, dres, name):
    t, d = h.shape
    tm = _row_tile(t)

    def body(h_ref, g_ref, dy_ref, dres_ref, dh_ref, dg_ref):
        @pl.when(pl.program_id(0) == 0)
        def _():
            dg_ref[...] = jnp.zeros_like(dg_ref)

        x = h_ref[...]
        r = lax.rsqrt(jnp.mean(x * x, axis=-1, keepdims=True) + NORM_EPS)
        xr = x * r
        dy_ = dy_ref[...]
        u = dy_ * g_ref[...]
        dh_ref[...] = dres_ref[...] + r * (u - xr * jnp.mean(u * xr, axis=-1, keepdims=True))
        dg_ref[...] += jnp.sum(dy_ * xr, axis=0, keepdims=True)

    row = pl.BlockSpec((tm, d), lambda i: (i, 0))
    vec = pl.BlockSpec((1, d), lambda i: (0, 0))
    return pl.pallas_call(
        body, grid=(t // tm,), in_specs=[row, vec, row, row], out_specs=[row, vec],
        out_shape=[jax.ShapeDtypeStruct((t, d), F32), jax.ShapeDtypeStruct((1, d), F32)],
        compiler_params=_params(("arbitrary",)), name=name,
    )(h, g, dy, dres)


def _col_tile(n):
    for c in (1024, 768, 640, 512, 384, 256, 128):
        if n % c == 0:
            return c
    return n


def _mm(a, w, name, res=None, out_dtype=F32, a_cols=None):
    m = a.shape[0]
    k, n = w.shape
    a_blk = 0 if a_cols is None else a_cols[0] // k
    assert a_cols is None or (a_cols[1] == k and a_cols[0] % k == 0)
    tm, tn = _matmul_rows(m), _col_tile(n)

    def body(*refs):
        if res is None:
            a_ref, w_ref, o_ref = refs
            acc = _dot(a_ref[...], w_ref[...])
        else:
            a_ref, w_ref, r_ref, o_ref = refs
            acc = r_ref[...] + _dot(a_ref[...], w_ref[...])
        o_ref[...] = acc.astype(o_ref.dtype)

    in_specs = [pl.BlockSpec((tm, k), lambda j, i: (i, a_blk)), pl.BlockSpec((k, tn), lambda j, i: (0, j))]
    args = [a, w]
    if res is not None:
        in_specs.append(pl.BlockSpec((tm, tn), lambda j, i: (i, j)))
        args.append(res)
    return pl.pallas_call(
        body, grid=(n // tn, m // tm), in_specs=in_specs, out_specs=pl.BlockSpec((tm, tn), lambda j, i: (i, j)),
        out_shape=jax.ShapeDtypeStruct((m, n), out_dtype), compiler_params=_params(("parallel", "parallel")), name=name,
    )(*args)


def _mm_nt(a, w, name):
    m, n = a.shape
    k = w.shape[0]
    tm, tk = _matmul_rows(m), _col_tile(k)

    def body(a_ref, w_ref, o_ref):
        o_ref[...] = _dot_nt(a_ref[...], w_ref[...])

    return pl.pallas_call(
        body, grid=(k // tk, m // tm),
        in_specs=[pl.BlockSpec((tm, n), lambda j, i: (i, 0)), pl.BlockSpec((tk, n), lambda j, i: (j, 0))],
        out_specs=pl.BlockSpec((tm, tk), lambda j, i: (i, j)),
        out_shape=jax.ShapeDtypeStruct((m, k), F32), compiler_params=_params(("parallel", "parallel")), name=name,
    )(a, w)


def _mm_tn(x, dy, name):
    m, k = x.shape
    n = dy.shape[1]
    tm, tn = _matmul_rows(m), _col_tile(n)

    def body(x_ref, dy_ref, o_ref):
        @pl.when(pl.program_id(1) == 0)
        def _():
            o_ref[...] = jnp.zeros_like(o_ref)

        o_ref[...] += _dot_tn(x_ref[...], dy_ref[...])

    return pl.pallas_call(
        body, grid=(n // tn, m // tm),
        in_specs=[pl.BlockSpec((tm, k), lambda j, i: (i, 0)), pl.BlockSpec((tm, tn), lambda j, i: (i, j))],
        out_specs=pl.BlockSpec((k, tn), lambda j, i: (0, j)),
        out_shape=jax.ShapeDtypeStruct((k, n), F32), compiler_params=_params(("parallel", "arbitrary")), name=name,
    )(x, dy)


def _silu_parts(g):
    s = 1.0 / (1.0 + jnp.exp(-g))
    return g * s, s * (1.0 + g * (1.0 - s))


def _gate_fwd(o_parts, proj, name):
    t = proj.shape[0]
    tm = _row_tile(t)
    w = D_MODEL // len(o_parts)

    def body(*refs):
        g_ref, o_ref = refs[-2], refs[-1]
        for p, r in enumerate(refs[:-2]):
            sil, _ = _silu_parts(g_ref[:, p * w:(p + 1) * w])
            o_ref[:, p * w:(p + 1) * w] = (r[...].astype(F32) * sil).astype(o_ref.dtype)

    return pl.pallas_call(
        body, grid=(t // tm,),
        in_specs=[pl.BlockSpec((tm, w), lambda i: (i, 0)) for _ in o_parts] + [pl.BlockSpec((tm, D_MODEL), lambda i: (i, 0))],
        out_specs=pl.BlockSpec((tm, D_MODEL), lambda i: (i, 0)),
        out_shape=jax.ShapeDtypeStruct((t, D_MODEL), BF16), compiler_params=_params(("parallel",)), name=name,
    )(*o_parts, proj)


def _gate_bwd(dao, o_parts, proj, name):
    t = proj.shape[0]
    tm = _row_tile(t)
    np_ = len(o_parts)
    w = D_MODEL // np_

    def body(*refs):
        dao_ref, g_ref = refs[0], refs[1 + np_]
        do_refs, dg_ref = refs[2 + np_:2 + 2 * np_], refs[-1]
        for p in range(np_):
            sl = slice(p * w, (p + 1) * w)
            sil, dsil = _silu_parts(g_ref[:, sl])
            da = dao_ref[:, sl]
            do_refs[p][...] = da * sil
            dg_ref[:, sl] = (da * refs[1 + p][...].astype(F32) * dsil).astype(dg_ref.dtype)

    full = pl.BlockSpec((tm, D_MODEL), lambda i: (i, 0))
    part = pl.BlockSpec((tm, w), lambda i: (i, 0))
    outs = pl.pallas_call(
        body, grid=(t // tm,), in_specs=[full] + [part] * np_ + [full], out_specs=[part] * np_ + [full],
        out_shape=[jax.ShapeDtypeStruct((t, w), F32)] * np_ + [jax.ShapeDtypeStruct((t, D_MODEL), BF16)],
        compiler_params=_params(("parallel",)), name=name,
    )(dao, *o_parts, proj)
    return outs[:np_], outs[np_]


def _loss_head(h2, gf, target, b, lp):
    d = h2.shape[1]
    nb = lp // BLOCK
    h3 = h2.reshape(b, lp, d)

    def body(h_ref, g_ref, t_ref, dh_ref, dg_ref, loss_ref):
        first = (pl.program_id(0) == 0) & (pl.program_id(1) == 0)

        @pl.when(first)
        def _():
            dg_ref[...] = jnp.zeros_like(dg_ref)
            loss_ref[...] = jnp.zeros_like(loss_ref)

        @pl.when(pl.program_id(1) == 0)
        def _():
            dh_ref[...] = jnp.zeros_like(dh_ref)

        @pl.when(pl.program_id(1) > 0)
        def _():
            x = h_ref[0]
            r = lax.rsqrt(jnp.mean(x * x, axis=-1, keepdims=True) + NORM_EPS)
            xr = x * r
            g = g_ref[...]
            diff = xr * g - t_ref[0]
            loss_ref[...] += 0.5 * jnp.sum(jnp.mean(diff * diff, axis=-1, keepdims=True))
            dy = diff * (1.0 / d)
            u = dy * g
            dh_ref[0] = r * (u - xr * jnp.mean(u * xr, axis=-1, keepdims=True))
            dg_ref[...] += jnp.sum(dy * xr, axis=0, keepdims=True)

    blk = pl.BlockSpec((1, BLOCK, d), lambda bi, n: (bi, n, 0))
    dh, dg, loss = pl.pallas_call(
        body, grid=(b, nb),
        in_specs=[blk, pl.BlockSpec((1, d), lambda bi, n: (0, 0)),
                  pl.BlockSpec((1, BLOCK, d), lambda bi, n: (bi, jnp.maximum(n - 1, 0), 0))],
        out_specs=[blk, pl.BlockSpec((1, d), lambda bi, n: (0, 0)), pl.BlockSpec((8, 128), lambda bi, n: (0, 0))],
        out_shape=[jax.ShapeDtypeStruct((b, lp, d), F32), jax.ShapeDtypeStruct((1, d), F32), jax.ShapeDtypeStruct((8, 128), F32)],
        compiler_params=_params(("arbitrary", "arbitrary")), name="loss_head",
    )(h3, gf, target)
    return dh.reshape(b * lp, d), dg, loss


def _iota2(shape, dim):
    return lax.broadcasted_iota(jnp.int32, shape, dim)


KEYS = 512
SB_FWD_KEYS = 512


def _lo_lanes():
    return _iota2((1, BLOCK), 1) < HEAD


def _halves(x, lo):
    zero = jnp.zeros_like(x)
    return jnp.where(lo, x, zero), jnp.where(lo, zero, x)


def _rows_of_pair(a, b):
    return jnp.where(_iota2((BLOCK, 1), 0) < HEAD, a, b)


def _split_rows_t(x):
    xt = x.T
    first = _iota2(xt.shape, 0) < HEAD
    zero = jnp.zeros_like(xt)
    return jnp.concatenate([jnp.where(first, xt, zero), jnp.where(first, zero, xt)], axis=1).astype(BF16)


def _key_chunk(c, lp, t_idx, strict, key_axis):
    keys = t_idx.shape[key_axis]
    first = c * keys
    s0 = pl.multiple_of(jnp.minimum(first, lp - keys), BLOCK)
    s_idx = s0 + _iota2(t_idx.shape, key_axis)
    seen = (s_idx < t_idx) if strict else (s_idx <= t_idx)
    return s0, seen & (s_idx >= jnp.maximum(first, N_PAD))


def _tri_dot(x, tri):
    return jnp.dot(x.astype(BF16), tri, preferred_element_type=F32)


def _stack_halves(x, lo):
    a, b = _halves(x, lo)
    return jnp.concatenate([a, b], axis=0)


def _pair(a, b, lo):
    return jnp.where(lo, a, b)


def _chunk_starts(lp):
    return [min(c * KEYS, lp - KEYS) for c in range(-(-lp // KEYS))]


def _put_rows(ref, r0, bq, a, b):
    for t in range(bq // BLOCK):
        part = slice(t * BLOCK, (t + 1) * BLOCK)
        ref[0, 0, r0 // BLOCK + t] = jnp.concatenate([a[:, part], b[:, part], jnp.zeros((6, BLOCK), F32)], axis=0)


def _get_rows(ref, r0, bq):
    return [jnp.concatenate([ref[0, 0, r0 // BLOCK + t, h:h + 1, :] for t in range(bq // BLOCK)], axis=1) for h in range(2)]


def _n_chunks(i, keys):
    return ((i + 1) * BLOCK + keys - 1) // keys


QROWS = 512


def _for_query_tiles(nb, tile, keys=KEYS):
    per = QROWS // BLOCK

    def step(j, _):
        tile(pl.multiple_of(j * QROWS, QROWS), QROWS, (j + 1) * (QROWS // keys))
        return 0

    lax.fori_loop(0, nb // per, step, 0)
    for i in range(nb - nb % per, nb):
        tile(i * BLOCK, BLOCK, _n_chunks(i, keys))


def _walk_chunks(r0, n, chunk, carry, leftwards=False, keys=KEYS):
    diag = jnp.maximum(r0 // keys, 1)

    def span(first, last, masked, carry):
        def step(t, cr):
            return chunk(last - 1 - t if leftwards else first + t, cr, masked)
        return lax.fori_loop(0, last - first, step, carry)

    spans = [(0, 1, True), (1, diag, False), (diag, n, True)]
    for first, last, masked in (reversed(spans) if leftwards else spans):
        carry = span(first, last, masked, carry)
    return carry


def _where(valid, x, other):
    return x if valid is None else jnp.where(valid, x, other)


HEAD_SCALE = HEAD ** -0.5
assert math.frexp(HEAD_SCALE)[0] == 0.5


def _sb_scores(q_h, k, valid, after):
    z = _dot_nt(q_h, k)
    lb = jnp.minimum(z, 0.0) - jnp.log(1.0 + jnp.exp(-jnp.abs(z)))
    l1m_all = lb - z
    l1m = _where(valid, l1m_all, 0.0)
    return lb, l1m_all, l1m, _tri_dot(l1m, after)


def _pair_stat_spec(nb):
    return pl.BlockSpec((1, 1, nb, 8, BLOCK), lambda bi, hp: (bi, hp, 0, 0, 0))


def _sb_fwd(proj3, name):
    b, lp, _ = proj3.shape
    nb = lp // BLOCK
    npair = SB_HEADS // 2

    keys = SB_FWD_KEYS

    def body(q_ref, k_ref, v_ref, o_ref, tot_ref):
        lo = _lo_lanes()
        after = (_iota2((keys, keys), 0) > _iota2((keys, keys), 1)).astype(BF16)

        def qtile(r0, bq, n):
            qs = _halves((q_ref[0, pl.ds(r0, bq), :] * HEAD_SCALE).astype(BF16), lo)
            t_idx = r0 + _iota2((bq, keys), 0)

            def kchunk(c, carry, masked):
                cs, acc = carry[:2], carry[2]
                s0, valid = _key_chunk(c, lp, t_idx, True, 1)
                valid = valid if masked else None
                k = k_ref[0, pl.ds(s0, keys), :].astype(BF16)
                a_s, new = [], []
                for h in range(2):
                    lb, _, l1m, suf = _sb_scores(qs[h], k, valid, after)
                    a_s.append(_where(valid, jnp.exp(lb + suf + cs[h]), 0.0).astype(BF16))
                    new.append(cs[h] + jnp.sum(l1m, axis=1, keepdims=True))
                v_bd = _stack_halves(v_ref[0, pl.ds(s0, keys), :].astype(BF16), lo)
                return (*new, acc + jnp.dot(jnp.concatenate(a_s, axis=1), v_bd, preferred_element_type=F32))

            zero = jnp.zeros((bq, 1), F32)
            c_a, c_b, acc = _walk_chunks(r0, n, kchunk, (zero, zero, jnp.zeros((bq, BLOCK), F32)), leftwards=True, keys=keys)
            o_ref[0, pl.ds(r0, bq), :] = acc
            tot_ref[0, pl.ds(r0, bq), :] = jnp.broadcast_to(_pair(c_a, c_b, lo), (bq, BLOCK))

        _for_query_tiles(nb, qtile, keys)

    def col(first):
        return pl.BlockSpec((1, lp, 2 * HEAD), lambda bi, hp: (bi, 0, first // (2 * HEAD) + hp))

    shp = jax.ShapeDtypeStruct((b, lp, SB_HEADS * HEAD), F32)
    return pl.pallas_call(
        body, grid=(b, npair), in_specs=[col(EV_Q), col(EV_K), col(EV_V)], out_specs=[col(0), col(0)], out_shape=[shp, shp],
        compiler_params=_params(("parallel", "parallel")), name=name,
    )(proj3, proj3, proj3)


def _sb_bwd(proj3, tot, do, name):
    b, lp, _ = proj3.shape
    nb = lp // BLOCK
    npair = SB_HEADS // 2

    def body(q_ref, k_ref, v_ref, tot_ref, do_ref, dq_ref, dk_ref, dv_ref):
        lo = _lo_lanes()
        after = (_iota2((KEYS, KEYS), 0) > _iota2((KEYS, KEYS), 1)).astype(BF16)
        before = (_iota2((KEYS, KEYS), 0) < _iota2((KEYS, KEYS), 1)).astype(BF16)
        dk_ref[...] = jnp.zeros_like(dk_ref)
        dv_ref[...] = jnp.zeros_like(dv_ref)

        def qtile(r0, bq, n):
            rows = pl.ds(r0, bq)
            qs = _halves((q_ref[0, rows, :] * HEAD_SCALE).astype(BF16), lo)
            dos = _halves(do_ref[0, rows, :].astype(BF16), lo)
            tot_i = tot_ref[0, rows, :]
            tots = (tot_i[:, 0:1], tot_i[:, HEAD:HEAD + 1])
            q_st, do_st = jnp.concatenate(qs, axis=0), jnp.concatenate(dos, axis=0)
            t_idx = r0 + _iota2((bq, KEYS), 0)

            def kchunk(c, carry, masked):
                s0, valid = _key_chunk(c, lp, t_idx, True, 1)
                valid = valid if masked else None
                keys = pl.ds(s0, KEYS)
                k = k_ref[0, keys, :].astype(BF16)
                v = v_ref[0, keys, :].astype(BF16)
                a_s, dzs, new = [], [], []
                for h in range(2):
                    left, pre = carry[2 * h], carry[2 * h + 1]
                    lb, l1m_all, l1m, suf = _sb_scores(qs[h], k, valid, after)
                    here = jnp.sum(l1m, axis=1, keepdims=True)
                    a = _where(valid, jnp.exp(lb + suf + (tots[h] - left - here)), 0.0)
                    w = a * _dot_nt(dos[h], v)
                    dz = _where(valid, w * jnp.exp(l1m_all) - (pre + _tri_dot(w, before)) * jnp.exp(lb), 0.0)
                    new += [left + here, pre + jnp.sum(w, axis=1, keepdims=True)]
                    a_s.append(a.astype(BF16))
                    dzs.append(dz.astype(BF16))
                dk_ref[0, keys, :] += _dot_tn(jnp.concatenate(dzs, axis=0), q_st)
                dv_ref[0, keys, :] += _dot_tn(jnp.concatenate(a_s, axis=0), do_st)
                dq = carry[4] + jnp.dot(jnp.concatenate(dzs, axis=1), _stack_halves(k, lo), preferred_element_type=F32)
                return (*new, dq)

            zero = jnp.zeros((bq, 1), F32)
            out = _walk_chunks(r0, n, kchunk, (zero, zero, zero, zero, jnp.zeros((bq, BLOCK), F32)))
            dq_ref[0, rows, :] = out[4] * HEAD_SCALE

        _for_query_tiles(nb, qtile)

    def col(first):
        return pl.BlockSpec((1, lp, 2 * HEAD), lambda bi, hp: (bi, 0, first // (2 * HEAD) + hp))

    shp = jax.ShapeDtypeStruct((b, lp, SB_HEADS * HEAD), F32)
    return pl.pallas_call(
        body, grid=(b, npair), in_specs=[col(EV_Q), col(EV_K), col(EV_V), col(0), col(0)], out_specs=[col(0)] * 3, out_shape=[shp] * 3,
        compiler_params=_params(("parallel", "parallel")), name=name,
    )(proj3, proj3, proj3, tot, do)


def _rope_tables(lp):
    half = MLA_ROPE // 2
    pos = (np.arange(lp) - N_PAD).astype(np.float32)
    inv = jnp.asarray(ROPE_BASE, F32) ** (-jnp.arange(half, dtype=F32) / half)
    ang = jnp.asarray(pos)[:, None] * inv[None, :]
    cos, sin = jnp.cos(ang), jnp.sin(ang)
    zeros = lambda n: jnp.zeros((lp, n), F32)
    c = jnp.concatenate([jnp.ones((lp, MLA_NOPE), F32), cos, cos, zeros(32)], axis=1)
    s1 = jnp.concatenate([zeros(MLA_NOPE), -sin, zeros(half), zeros(32)], axis=1)
    s2 = jnp.concatenate([zeros(MLA_NOPE), zeros(half), sin, zeros(32)], axis=1)
    return c, s1, s2


def _rope(x, c, s1, s2):
    half = MLA_ROPE // 2
    return x * c + pltpu.roll(x, BLOCK - half, 1) * s1 + pltpu.roll(x, half, 1) * s2


def _rope_t(dy, c, s1, s2):
    half = MLA_ROPE // 2
    return dy * c + pltpu.roll(dy * s1, half, 1) + pltpu.roll(dy * s2, BLOCK - half, 1)


def _rms_rows(x, g):
    r = lax.rsqrt(jnp.mean(x * x, axis=-1, keepdims=True) + NORM_EPS)
    return x * r, r


def _mla_prep_fwd(proj3, gq, gkv, wq, wk, wv, tabs, name):
    b, lp, _ = proj3.shape
    nb = lp // BLOCK
    hw = MLA_HEADS * BLOCK

    def body(cq_ref, ckv_ref, kr_ref, gq_ref, gkv_ref, wq_ref, wk_ref, wv_ref, c_ref, s1_ref, s2_ref, qf_ref, kf_ref, v_ref):
        c, s1, s2 = c_ref[...], s1_ref[...], s2_ref[...]
        xq, _ = _rms_rows(cq_ref[0], None)
        qh = _dot(xq * gq_ref[...], wq_ref[...])
        xk, _ = _rms_rows(ckv_ref[0], None)
        ckv_n = xk * gkv_ref[...]
        kv = _dot(ckv_n, wk_ref[...])
        v_ref[0] = _dot(ckv_n, wv_ref[...]).astype(v_ref.dtype)
        kr = _rope(kr_ref[0], c, s1, s2)
        for h in range(MLA_HEADS):
            ls = slice(h * BLOCK, (h + 1) * BLOCK)
            qf_ref[0, :, ls] = _rope(qh[:, ls], c, s1, s2).astype(qf_ref.dtype)
            kf_ref[0, :, ls] = (kv[:, ls] + kr).astype(kf_ref.dtype)

    def col(first, width):
        return pl.BlockSpec((1, BLOCK, width), lambda bi, n: (bi, n, first // width))

    def whole(a):
        return pl.BlockSpec(a.shape, lambda bi, n: (0,) * a.ndim)

    tab = pl.BlockSpec((BLOCK, BLOCK), lambda bi, n: (n, 0))
    return pl.pallas_call(
        body, grid=(b, nb),
        in_specs=[col(EV_CQ, MLA_Q_LORA), col(EV_CKV, MLA_KV_LORA), col(EV_KR, BLOCK), whole(gq), whole(gkv), whole(wq), whole(wk),
                  whole(wv), tab, tab, tab],
        out_specs=[col(0, hw), col(0, hw), col(0, MLA_HEADS * HEAD)],
        out_shape=[jax.ShapeDtypeStruct((b, lp, hw), BF16), jax.ShapeDtypeStruct((b, lp, hw), BF16),
                   jax.ShapeDtypeStruct((b, lp, MLA_HEADS * HEAD), BF16)],
        compiler_params=_params(("parallel", "parallel")), name=name,
    )(proj3, proj3, proj3, gq, gkv, wq, wk, wv, *tabs)


def _mla_prep_bwd(proj3, gq, gkv, wq, wk, wv, tabs, dqf, dkf, dv, name):
    b, lp, _ = proj3.shape
    nb = lp // BLOCK
    hw = MLA_HEADS * BLOCK

    def body(cq_ref, ckv_ref, gq_ref, gkv_ref, wq_ref, wk_ref, wv_ref, c_ref, s1_ref, s2_ref, dqf_ref, dkf_ref, dv_ref,
             dcq_ref, dckv_ref, dkr_ref, dwq_ref, dwk_ref, dwv_ref, dgq_ref, dgkv_ref, dqh):
        @pl.when((pl.program_id(0) == 0) & (pl.program_id(1) == 0))
        def _():
            for r in (dwq_ref, dwk_ref, dwv_ref, dgq_ref, dgkv_ref):
                r[...] = jnp.zeros_like(r)

        c, s1, s2 = c_ref[...], s1_ref[...], s2_ref[...]
        dkr = jnp.zeros((BLOCK, BLOCK), F32)
        for h in range(MLA_HEADS):
            ls = slice(h * BLOCK, (h + 1) * BLOCK)
            dqh[:, ls] = _rope_t(dqf_ref[0, :, ls].astype(F32), c, s1, s2).astype(dqh.dtype)
            dkr = dkr + dkf_ref[0, :, ls].astype(F32)
        dkr_ref[0] = _rope_t(dkr, c, s1, s2).astype(dkr_ref.dtype)

        def norm_bwd(x, g, dy, dg_ref):
            xr, r = _rms_rows(x, None)
            u = dy * g
            dg_ref[...] += jnp.sum(dy * xr, axis=0, keepdims=True)
            return r * (u - xr * jnp.mean(u * xr, axis=-1, keepdims=True))

        xq, _ = _rms_rows(cq_ref[0], None)
        cq_n = xq * gq_ref[...]
        dwq_ref[...] += _dot_tn(cq_n, dqh[...])
        dcq_ref[0] = norm_bwd(cq_ref[0], gq_ref[...], _dot_nt(dqh[...], wq_ref[...]), dgq_ref).astype(dcq_ref.dtype)
        xk, _ = _rms_rows(ckv_ref[0], None)
        ckv_n = xk * gkv_ref[...]
        dkf_, dv_ = dkf_ref[0], dv_ref[0]
        dwk_ref[...] += _dot_tn(ckv_n, dkf_)
        dwv_ref[...] += _dot_tn(ckv_n, dv_)
        dckv_n = _dot_nt(dkf_, wk_ref[...]) + _dot_nt(dv_, wv_ref[...])
        dckv_ref[0] = norm_bwd(ckv_ref[0], gkv_ref[...], dckv_n, dgkv_ref).astype(dckv_ref.dtype)

    def col(first, width):
        return pl.BlockSpec((1, BLOCK, width), lambda bi, n: (bi, n, first // width))

    def whole(a):
        return pl.BlockSpec(a.shape, lambda bi, n: (0,) * len(a.shape))

    tab = pl.BlockSpec((BLOCK, BLOCK), lambda bi, n: (n, 0))
    acc_shapes = [jax.ShapeDtypeStruct(a.shape, F32) for a in (wq, wk, wv, gq, gkv)]
    return pl.pallas_call(
        body, grid=(b, nb),
        in_specs=[col(EV_CQ, MLA_Q_LORA), col(EV_CKV, MLA_KV_LORA), whole(gq), whole(gkv), whole(wq), whole(wk), whole(wv), tab, tab, tab,
                  col(0, hw), col(0, hw), col(0, MLA_HEADS * HEAD)],
        out_specs=[col(0, MLA_Q_LORA), col(0, MLA_KV_LORA), col(0, BLOCK)] + [whole(a) for a in acc_shapes],
        out_shape=[jax.ShapeDtypeStruct((b, lp, MLA_Q_LORA), BF16), jax.ShapeDtypeStruct((b, lp, MLA_KV_LORA), BF16),
                   jax.ShapeDtypeStruct((b, lp, BLOCK), BF16)] + acc_shapes,
        scratch_shapes=[pltpu.VMEM((BLOCK, hw), BF16)],
        compiler_params=_params(("arbitrary", "arbitrary")), name=name,
    )(proj3, proj3, gq, gkv, wq, wk, wv, *tabs, dqf, dkf, dv)


def _mla_fwd(qf, kf, v, name):
    b, lp, _ = qf.shape
    nb = lp // BLOCK
    npair = MLA_HEADS // 2
    scale = (MLA_NOPE + MLA_ROPE) ** -0.5
    starts = _chunk_starts(lp)

    def body(q_ref, k_ref, v_ref, o_ref, lse_ref, vt_ref):
        for c, s0 in enumerate(starts):
            vt_ref[c] = _split_rows_t(v_ref[0, s0:s0 + KEYS, :].astype(F32))

        def qtile(r0, bq, n):
            qs = [q_ref[0, pl.ds(r0, bq), h * BLOCK:(h + 1) * BLOCK] for h in range(2)]
            t_idx = r0 + _iota2((KEYS, bq), 1)

            def kchunk(c, carry, masked):
                stats, acc = carry[:4], carry[4]
                s0, valid = _key_chunk(c, lp, t_idx, False, 0)
                valid = valid if masked else None
                ps, new, alphas = [], [], []
                for h in range(2):
                    m, l = stats[2 * h], stats[2 * h + 1]
                    s = _where(valid, _dot_nt(k_ref[0, pl.ds(s0, KEYS), h * BLOCK:(h + 1) * BLOCK], qs[h]) * scale, NEG)
                    m_new = jnp.maximum(m, jnp.max(s, axis=0, keepdims=True))
                    p = _where(valid, jnp.exp(s - m_new), 0.0)
                    alpha = jnp.exp(m - m_new)
                    new += [m_new, alpha * l + jnp.sum(p, axis=0, keepdims=True)]
                    alphas.append(alpha)
                    ps.append(p.astype(BF16))
                pv = jnp.dot(vt_ref[c], jnp.concatenate(ps, axis=0), preferred_element_type=F32)
                return (*new, _rows_of_pair(alphas[0], alphas[1]) * acc + pv)

            neg, zero = jnp.full((1, bq), NEG, F32), jnp.zeros((1, bq), F32)
            m_a, l_a, m_b, l_b, acc = _walk_chunks(r0, n, kchunk, (neg, zero, neg, zero, jnp.zeros((BLOCK, bq), F32)))
            safe = [jnp.where(l > 0.0, l, 1.0) for l in (l_a, l_b)]
            o_ref[0, pl.ds(r0, bq), :] = (acc / _rows_of_pair(safe[0], safe[1])).T
            lse = [jnp.where(l > 0.0, m + jnp.log(sf), 0.0) for m, l, sf in ((m_a, l_a, safe[0]), (m_b, l_b, safe[1]))]
            _put_rows(lse_ref, r0, bq, lse[0], lse[1])

        _for_query_tiles(nb, qtile)

    wide = pl.BlockSpec((1, lp, 2 * BLOCK), lambda bi, hp: (bi, 0, hp))
    thin = pl.BlockSpec((1, lp, 2 * HEAD), lambda bi, hp: (bi, 0, hp))
    return pl.pallas_call(
        body, grid=(b, npair), in_specs=[wide, wide, thin], out_specs=[thin, _pair_stat_spec(nb)],
        out_shape=[jax.ShapeDtypeStruct((b, lp, MLA_HEADS * HEAD), F32), jax.ShapeDtypeStruct((b, npair, nb, 8, BLOCK), F32)],
        scratch_shapes=[pltpu.VMEM((len(starts), BLOCK, 2 * KEYS), BF16)],
        compiler_params=_params(("parallel", "parallel")), name=name,
    )(qf, kf, v)


def _mla_bwd(qf, kf, v, o, lse, do, name):
    b, lp, _ = qf.shape
    nb = lp // BLOCK
    npair = MLA_HEADS // 2
    scale = (MLA_NOPE + MLA_ROPE) ** -0.5

    starts = _chunk_starts(lp)

    def body(q_ref, k_ref, v_ref, o_ref, lse_ref, do_ref, dq_ref, dk_ref, dv_ref, kt_ref):
        lo = _lo_lanes()
        dk_ref[...] = jnp.zeros_like(dk_ref)
        dv_ref[...] = jnp.zeros_like(dv_ref)
        for c, s0 in enumerate(starts):
            for h in range(2):
                kt_ref[c, h] = k_ref[0, s0:s0 + KEYS, h * BLOCK:(h + 1) * BLOCK].astype(F32).T.astype(BF16)

        def qtile(r0, bq, n):
            rows = pl.ds(r0, bq)
            qs = [q_ref[0, rows, h * BLOCK:(h + 1) * BLOCK] for h in range(2)]
            do_i = do_ref[0, rows, :]
            dos = _halves(do_i.astype(BF16), lo)
            do_st = jnp.concatenate(dos, axis=0)
            both = (do_i * o_ref[0, rows, :]).T
            dsum = (jnp.sum(both[:HEAD], axis=0, keepdims=True), jnp.sum(both[HEAD:], axis=0, keepdims=True))
            lses = _get_rows(lse_ref, r0, bq)
            t_idx = r0 + _iota2((KEYS, bq), 1)

            def kchunk(c, dqts, masked):
                s0, valid = _key_chunk(c, lp, t_idx, False, 0)
                valid = valid if masked else None
                keys = pl.ds(s0, KEYS)
                v_c = v_ref[0, keys, :]
                ps, out = [], []
                for h in range(2):
                    lanes = slice(h * BLOCK, (h + 1) * BLOCK)
                    s = _dot_nt(k_ref[0, keys, lanes], qs[h]) * scale
                    p = _where(valid, jnp.exp(s - lses[h]), 0.0)
                    ds = (p * (_dot_nt(v_c, dos[h]) - dsum[h]) * scale).astype(BF16)
                    dk_ref[0, keys, lanes] += jnp.dot(ds, qs[h], preferred_element_type=F32)
                    out.append(dqts[h] + jnp.dot(kt_ref[c, h], ds, preferred_element_type=F32))
                    ps.append(p.astype(BF16))
                dv_ref[0, keys, :] += jnp.dot(jnp.concatenate(ps, axis=1), do_st, preferred_element_type=F32)
                return tuple(out)

            zero = jnp.zeros((BLOCK, bq), F32)
            dq_a, dq_b = _walk_chunks(r0, n, kchunk, (zero, zero))
            dq_ref[0, rows, 0:BLOCK] = dq_a.T
            dq_ref[0, rows, BLOCK:2 * BLOCK] = dq_b.T

        _for_query_tiles(nb, qtile)

    wide = pl.BlockSpec((1, lp, 2 * BLOCK), lambda bi, hp: (bi, 0, hp))
    thin = pl.BlockSpec((1, lp, 2 * HEAD), lambda bi, hp: (bi, 0, hp))
    return pl.pallas_call(
        body, grid=(b, npair), in_specs=[wide, wide, thin, thin, _pair_stat_spec(nb), thin], out_specs=[wide, wide, thin],
        out_shape=[jax.ShapeDtypeStruct(qf.shape, F32), jax.ShapeDtypeStruct(qf.shape, F32), jax.ShapeDtypeStruct(v.shape, F32)],
        scratch_shapes=[pltpu.VMEM((len(starts), 2, BLOCK, KEYS), BF16)],
        compiler_params=_params(("parallel", "parallel")), name=name,
    )(qf, kf, v, o, lse, do)


SWA_KEYS = 2 * BLOCK + N_META


def _swa_keys(k_ref, v_ref, n, kv):
    prev = jnp.maximum(n - 1, 0)
    rows = lambda blk: pl.ds(pl.multiple_of(blk * BLOCK, BLOCK), BLOCK)
    mine = (_iota2((1, BLOCK), 1) >= HEAD).astype(jnp.int32) == kv

    def both_halves(ref):
        x = jnp.concatenate([ref[0, rows(prev), :], ref[0, rows(n), :], ref[0, N_PAD:BLOCK, :]], axis=0)
        return jnp.where(mine, x, pltpu.roll(x, HEAD, 1))

    slot = _iota2((SWA_KEYS, BLOCK), 0)
    s_idx = jnp.where(slot < 2 * BLOCK, (n - 1) * BLOCK + slot, slot - 2 * BLOCK + N_PAD)
    dist = n * BLOCK + _iota2((SWA_KEYS, BLOCK), 1) - s_idx
    band = (slot < 2 * BLOCK) & (dist >= 0) & (dist < SWA_WINDOW) & (s_idx >= BLOCK)
    meta = (slot >= 2 * BLOCK) & (dist >= 0)
    return both_halves(k_ref), both_halves(v_ref), band | meta, dist.astype(F32), prev


def _pad_keys(x):
    return jnp.concatenate([x, jnp.zeros((3 * BLOCK - SWA_KEYS, x.shape[1]), x.dtype)], axis=0)


def _swa_probs(q_h, kdup, valid, dist, head, sink_ref):
    slope = jnp.exp(jnp.full((1, 1), -8.0 * math.log(2.0) / SWA_HEADS, F32) * (head + 1).astype(F32))
    s = jnp.where(valid, _dot_nt(kdup, q_h) - slope * dist, NEG)
    sink = sink_ref[pl.ds(head, 1), 0:1]
    m = jnp.maximum(jnp.max(s, axis=0, keepdims=True), sink)
    e = jnp.where(valid, jnp.exp(s - m), 0.0)
    es = jnp.exp(sink - m)
    inv = 1.0 / (jnp.sum(e, axis=0, keepdims=True) + es)
    return e * inv, es * inv


SWA_PAIRS = SWA_HEADS // SWA_KV_HEADS // 2
SWA_GROUP = SWA_PAIRS * 2 * HEAD


def _swa_specs(b, lp):
    nb = lp // BLOCK
    qcol = lambda first: pl.BlockSpec((1, BLOCK, SWA_GROUP), lambda bi, kv, n: (bi, n, first // SWA_GROUP + kv))
    kcol = lambda first: pl.BlockSpec((1, lp, BLOCK), lambda bi, kv, n: (bi, 0, first // BLOCK))
    sink = pl.BlockSpec((SWA_HEADS, BLOCK), lambda bi, kv, n: (0, 0))
    return (b, SWA_KV_HEADS, nb), qcol, kcol, sink


def _swa_fwd(proj3, sinks, name):
    b, lp, _ = proj3.shape
    grid, qcol, kcol, sink = _swa_specs(b, lp)

    def body(q_ref, k_ref, v_ref, sink_ref, o_ref):
        kv, n = pl.program_id(1), pl.program_id(2)
        lo = _lo_lanes()
        kdup, vdup, valid, dist, _ = _swa_keys(k_ref, v_ref, n, kv)
        kdup = kdup.astype(BF16)
        vt = _split_rows_t(_pad_keys(vdup))
        for p in range(SWA_PAIRS):
            lanes = slice(p * BLOCK, (p + 1) * BLOCK)
            qs = _halves((q_ref[0, :, lanes] * HEAD_SCALE).astype(BF16), lo)
            probs = [_swa_probs(qs[hh], kdup, valid, dist, (kv * SWA_PAIRS + p) * 2 + hh, sink_ref)[0].astype(BF16) for hh in range(2)]
            o_ref[0, :, lanes] = jnp.dot(vt, jnp.concatenate([_pad_keys(pr) for pr in probs], axis=0), preferred_element_type=F32).T

    return pl.pallas_call(
        body, grid=grid, in_specs=[qcol(OD_Q), kcol(OD_K), kcol(OD_V), sink], out_specs=qcol(0),
        out_shape=jax.ShapeDtypeStruct((b, lp, SWA_HEADS * HEAD), F32),
        compiler_params=_params(("parallel", "parallel", "parallel")), name=name,
    )(proj3, proj3, proj3, sinks)


def _swa_bwd(proj3, sinks, do, name):
    b, lp, _ = proj3.shape
    nb = lp // BLOCK
    grid, qcol, kcol, sink = _swa_specs(b, lp)

    def body(q_ref, k_ref, v_ref, sink_ref, do_ref, dq_ref, dk_ref, dv_ref, dsink_ref, dk_acc, dv_acc):
        kv, n = pl.program_id(1), pl.program_id(2)

        @pl.when((n == 0) & (pl.program_id(0) == 0) & (kv == 0))
        def _():
            dsink_ref[...] = jnp.zeros_like(dsink_ref)

        @pl.when(n == 0)
        def _():
            dk_acc[...] = jnp.zeros_like(dk_acc)
            dv_acc[...] = jnp.zeros_like(dv_acc)

        lo = _lo_lanes()
        kdup, vdup, valid, dist, prev = _swa_keys(k_ref, v_ref, n, kv)
        kt = _split_rows_t(_pad_keys(kdup))
        kdup, vdup = kdup.astype(BF16), vdup.astype(BF16)
        dkc = jnp.zeros((SWA_KEYS, BLOCK), F32)
        dvc = jnp.zeros((SWA_KEYS, BLOCK), F32)
        for p in range(SWA_PAIRS):
            lanes = slice(p * BLOCK, (p + 1) * BLOCK)
            qs = _halves((q_ref[0, :, lanes] * HEAD_SCALE).astype(BF16), lo)
            dos = _halves(do_ref[0, :, lanes].astype(BF16), lo)
            dss, prs = [], []
            for hh in range(2):
                head = (kv * SWA_PAIRS + p) * 2 + hh
                pr, ps = _swa_probs(qs[hh], kdup, valid, dist, head, sink_ref)
                dp = _dot_nt(vdup, dos[hh])
                dsum = jnp.sum(pr * dp, axis=0, keepdims=True)
                dsink_ref[pl.ds(head, 1), :] += jnp.broadcast_to(-jnp.sum(ps * dsum, axis=1, keepdims=True), (1, BLOCK))
                dss.append((pr * (dp - dsum)).astype(BF16))
                prs.append(pr.astype(BF16))
            dq_ref[0, :, lanes] = jnp.dot(kt, jnp.concatenate([_pad_keys(d) for d in dss], axis=0), preferred_element_type=F32).T * HEAD_SCALE
            dkc = dkc + jnp.dot(jnp.concatenate(dss, axis=1), jnp.concatenate(qs, axis=0), preferred_element_type=F32)
            dvc = dvc + jnp.dot(jnp.concatenate(prs, axis=1), jnp.concatenate(dos, axis=0), preferred_element_type=F32)
        rows = lambda blk: pl.ds(pl.multiple_of(blk * BLOCK, BLOCK), BLOCK)
        for r, part in ((rows(prev), slice(0, BLOCK)), (rows(n), slice(BLOCK, 2 * BLOCK)), (slice(N_PAD, BLOCK), slice(2 * BLOCK, SWA_KEYS))):
            dk_acc[r, :] += dkc[part]
            dv_acc[r, :] += dvc[part]

        for acc, ref in ((dk_acc, dk_ref), (dv_acc, dv_ref)):
            @pl.when((n == nb - 1) & (kv == 0))
            def _():
                x = acc[...]
                ref[0] = x + pltpu.roll(x, HEAD, 1)

            @pl.when((n == nb - 1) & (kv == 1))
            def _():
                x = acc[...]
                ref[0] = jnp.where(lo, ref[0], x + pltpu.roll(x, HEAD, 1))

    kvout = pl.BlockSpec((1, lp, BLOCK), lambda bi, kv, n: (bi, 0, 0))
    kvshape = jax.ShapeDtypeStruct((b, lp, BLOCK), F32)
    return pl.pallas_call(
        body, grid=grid, in_specs=[qcol(OD_Q), kcol(OD_K), kcol(OD_V), sink, qcol(0)], out_specs=[qcol(0), kvout, kvout, sink],
        out_shape=[jax.ShapeDtypeStruct((b, lp, SWA_HEADS * HEAD), F32), kvshape, kvshape, jax.ShapeDtypeStruct((SWA_HEADS, BLOCK), F32)],
        scratch_shapes=[pltpu.VMEM((lp, BLOCK), F32), pltpu.VMEM((lp, BLOCK), F32)],
        compiler_params=_params(("arbitrary", "arbitrary", "arbitrary")), name=name,
    )(proj3, proj3, proj3, sinks, do)


def _kernel_weights(ev_w_in, ev_w_uq, ev_w_ukv, od_w_in):
    zeros = lambda r, c: jnp.zeros((r, c), ev_w_in.dtype)
    q_sb, k_sb, v_sb, g_sb, c_q, c_kv, k_r, g_mla = jnp.split(ev_w_in, [512, 1024, 1536, 2048, 2304, 2432, 2464], axis=1)
    w0 = jnp.concatenate([g_sb, g_mla, q_sb, k_sb, v_sb, c_q, c_kv, zeros(D_MODEL, MLA_NOPE), k_r, zeros(D_MODEL, 32)], axis=1)
    uq = ev_w_uq.reshape(MLA_Q_LORA, MLA_HEADS, MLA_NOPE + MLA_ROPE)
    wq = jnp.pad(uq, ((0, 0), (0, 0), (0, BLOCK - MLA_NOPE - MLA_ROPE))).reshape(MLA_Q_LORA, MLA_HEADS * BLOCK)
    ukv = ev_w_ukv.reshape(MLA_KV_LORA, MLA_HEADS, BLOCK)
    wk = jnp.pad(ukv[:, :, :MLA_NOPE], ((0, 0), (0, 0), (0, BLOCK - MLA_NOPE))).reshape(MLA_KV_LORA, MLA_HEADS * BLOCK)
    wv = ukv[:, :, MLA_NOPE:].reshape(MLA_KV_LORA, MLA_HEADS * HEAD)
    q, k, v, g = jnp.split(od_w_in, [1024, 1152, 1280], axis=1)
    w1 = jnp.concatenate([g, q, k, v], axis=1)
    return w0, wq, wk, wv, w1


def _original_grads(dw0, dwq, dwk, dwv, dw1):
    sl = lambda a, first, n: a[:, first:first + n]
    d_ev_w_in = jnp.concatenate([sl(dw0, EV_Q, 512), sl(dw0, EV_K, 512), sl(dw0, EV_V, 512), sl(dw0, EV_G, 512), sl(dw0, EV_CQ, 256),
                                 sl(dw0, EV_CKV, 128), sl(dw0, EV_KR + MLA_NOPE, MLA_ROPE), sl(dw0, EV_G + 512, 512)], axis=1)
    d_uq = dwq.reshape(MLA_Q_LORA, MLA_HEADS, BLOCK)[:, :, :MLA_NOPE + MLA_ROPE].reshape(MLA_Q_LORA, -1)
    d_ukv = jnp.concatenate([dwk.reshape(MLA_KV_LORA, MLA_HEADS, BLOCK)[:, :, :MLA_NOPE], dwv.reshape(MLA_KV_LORA, MLA_HEADS, HEAD)],
                            axis=2).reshape(MLA_KV_LORA, -1)
    d_od_w_in = jnp.concatenate([sl(dw1, OD_Q, 1024), sl(dw1, OD_K, 128), sl(dw1, OD_V, 128), sl(dw1, OD_G, 1024)], axis=1)
    return d_ev_w_in, d_uq, d_ukv, d_od_w_in


def _meta_rows_sum(dh0_3):
    b, _, d = dh0_3.shape

    def body(x_ref, o_ref):
        acc = x_ref[0, N_PAD:BLOCK, :]
        for i in range(1, b):
            acc = acc + x_ref[i, N_PAD:BLOCK, :]
        o_ref[...] = acc

    return pl.pallas_call(
        body, grid=(1,), in_specs=[pl.BlockSpec((b, BLOCK, d), lambda i: (0, 0, 0))], out_specs=pl.BlockSpec((N_META, d), lambda i: (0, 0)),
        out_shape=jax.ShapeDtypeStruct((N_META, d), F32), compiler_params=_params(("arbitrary",)), name="meta_rows_sum",
    )(dh0_3)


def _local_step(x, meta, norm_g, final_g, gq, gkv, sinks, target, ev_w_in, ev_w_uq, ev_w_ukv, wo0, od_w_in, wo1):
    b, seq, d = x.shape
    lp = seq + BLOCK
    t = b * lp
    w0, wq, wk, wv, w1 = _kernel_weights(ev_w_in, ev_w_uq, ev_w_ukv, od_w_in)
    h0 = jnp.concatenate([jnp.zeros((b, N_PAD, d), F32), jnp.broadcast_to(meta[None], (b, N_META, d)), x], axis=1).reshape(t, d)
    tabs = _rope_tables(lp)
    g0, g1 = norm_g[0:1], norm_g[1:2]

    hn0 = _rms_fwd(h0, g0, "norm0")
    proj0 = _mm(hn0, w0, "inproj0")
    p0 = proj0.reshape(b, lp, EV_N)
    o_sb, sb_tot = _sb_fwd(p0, "sb_fwd")
    qf, kf, v = _mla_prep_fwd(p0, gq, gkv, wq, wk, wv, tabs, "mla_prep_fwd")
    o_mla, lse = _mla_fwd(qf, kf, v, "mla_fwd")
    o0 = [o_sb.reshape(t, -1), o_mla.reshape(t, -1)]
    ao0 = _gate_fwd(o0, proj0, "gate0")
    h1 = _mm(ao0, wo0, "outproj0", res=h0)

    hn1 = _rms_fwd(h1, g1, "norm1")
    proj1 = _mm(hn1, w1, "inproj1")
    p1 = proj1.reshape(b, lp, OD_N)
    sinks_b = jnp.broadcast_to(sinks.reshape(SWA_HEADS, 1), (SWA_HEADS, BLOCK))
    o1 = _swa_fwd(p1, sinks_b, "swa_fwd").reshape(t, -1)
    ao1 = _gate_fwd([o1], proj1, "gate1")
    h2 = _mm(ao1, wo1, "outproj1", res=h1)

    dh2, d_final_g, loss = _loss_head(h2, final_g.reshape(1, d), target, b, lp)

    d_wo1 = _mm_tn(ao1, dh2, "d_wo1")
    dao1 = _mm_nt(dh2, wo1, "d_ao1")
    (do1,), dg1 = _gate_bwd(dao1, [o1], proj1, "gate1_bwd")
    dq1, dk4, dv4, d_sinks = _swa_bwd(p1, sinks_b, do1.reshape(b, lp, -1), "swa_bwd")
    unheads = lambda a: a.reshape(t, SWA_KV_HEADS * HEAD).astype(BF16)
    dproj1 = jnp.concatenate([dg1, dq1.reshape(t, -1).astype(BF16), unheads(dk4), unheads(dv4)], axis=1)
    d_w1 = _mm_tn(hn1, dproj1, "d_w1")
    dhn1 = _mm_nt(dproj1, w1, "d_hn1")
    dh1, d_g1 = _rms_bwd(h1, g1, dhn1, dh2, "norm1_bwd")

    d_wo0 = _mm_tn(ao0, dh1, "d_wo0")
    dao0 = _mm_nt(dh1, wo0, "d_ao0")
    (do_sb, do_mla), dg0 = _gate_bwd(dao0, o0, proj0, "gate0_bwd")
    dq_sb, dk_sb, dv_sb = _sb_bwd(p0, sb_tot, do_sb.reshape(b, lp, -1), "sb_bwd")
    dqf, dkf, dv = _mla_bwd(qf, kf, v, o_mla, lse, do_mla.reshape(b, lp, -1), "mla_bwd")
    dcq, dckv, dkr, d_wq, d_wk, d_wv, d_gq, d_gkv = _mla_prep_bwd(p0, gq, gkv, wq, wk, wv, tabs, dqf, dkf, dv, "mla_prep_bwd")
    flat = lambda a: a.reshape(t, -1).astype(BF16)
    dproj0 = jnp.concatenate([dg0, flat(dq_sb), flat(dk_sb), flat(dv_sb), flat(dcq), flat(dckv), flat(dkr)], axis=1)
    d_w0 = _mm_tn(hn0, dproj0, "d_w0")
    dhn0 = _mm_nt(dproj0, w0, "d_hn0")
    dh0, d_g0 = _rms_bwd(h0, g0, dhn0, dh1, "norm0_bwd")
    dh0 = dh0.reshape(b, lp, d)

    d_ev_w_in, d_uq, d_ukv, d_od_w_in = _original_grads(d_w0, d_wq, d_wk, d_wv, d_w1)
    grads = dict(meta=_meta_rows_sum(dh0), norm_g=jnp.concatenate([d_g0, d_g1], axis=0), final_g=d_final_g.reshape(d),
                 ev_w_in=d_ev_w_in, ev_q_norm_g=d_gq, ev_kv_norm_g=d_gkv, ev_w_uq=d_uq, ev_w_ukv=d_ukv, ev_w_out=d_wo0,
                 od_w_in=d_od_w_in, od_sinks=d_sinks[:, 0].reshape(1, SWA_HEADS), od_w_out=d_wo1)
    return loss, dh0[:, BLOCK:], grads


MESH = pl.DeviceIdType.MESH
ANY = pl.BlockSpec(memory_space=pl.ANY)


def _place():
    return lax.axis_index("x"), lax.axis_index("y"), lax.axis_index("c")


def _other_chips(x, y):
    return [(1 - x, y), (x, 1 - y), (1 - x, 1 - y)]


def _with_own_slot(slots, own):
    me = 2 * lax.axis_index("x") + lax.axis_index("y")
    return lax.dynamic_update_slice(slots, own[None], (me,) + (0,) * own.ndim)


def _gather_weights(packs, meta, name):
    n = len(packs)

    def body(*refs):
        ins, m_ref, outs, mo_ref = refs[:n], refs[n], refs[n + 1:2 * n + 1], refs[2 * n + 1]
        send_sems, recv_sems = refs[2 * n + 2:]
        x, y, c = _place()
        me, sib = 2 * x + y, (x, y, 1 - c)
        chips = _other_chips(x, y)

        def copy(k, src, dst, to):
            return pltpu.make_async_remote_copy(src_ref=src, dst_ref=dst, send_sem=send_sems.at[k], recv_sem=recv_sems.at[k], device_id=to,
                                                device_id_type=MESH)

        def half(i, chip, h):
            rows = packs[i].shape[0] // 2
            return outs[i].at[chip, pl.ds(h * rows, rows), :]

        def mine(i):
            rows = packs[i].shape[0] // 2
            return ins[i].at[pl.ds(c * rows, rows), :]

        sent = [copy(6 * i + k, mine(i), half(i, me, c), (px, py, c)) for i in range(n) for k, (px, py) in enumerate(chips)]
        sent += [copy(6 * n + k, m_ref, mo_ref.at[me], (px, py, c)) for k, (px, py) in enumerate(chips)]
        for cp in sent:
            cp.start()
        for i in range(n):
            for k, (px, py) in enumerate(chips):
                landed = half(i, 2 * px + py, c)
                copy(6 * i + k, mine(i), landed, (px, py, c)).wait_recv()
                fwd = copy(6 * i + 3 + k, landed, landed, sib)
                fwd.start()
                sent.append(fwd)
        for k, (px, py) in enumerate(chips):
            for i in range(n):
                other = half(i, 2 * px + py, 1 - c)
                copy(6 * i + 3 + k, other, other, sib).wait_recv()
            copy(6 * n + k, m_ref, mo_ref.at[2 * px + py], (px, py, c)).wait_recv()
        for cp in sent:
            cp.wait_send()

    nsem = 6 * n + 3
    res = pl.pallas_call(
        body, in_specs=[ANY] * (n + 1), out_specs=[ANY] * (n + 1),
        out_shape=[jax.ShapeDtypeStruct((N_CHIPS,) + a.shape, a.dtype) for a in list(packs) + [meta]],
        scratch_shapes=[pltpu.SemaphoreType.DMA((nsem,)), pltpu.SemaphoreType.DMA((nsem,))],
        name=name,
    )(*packs, meta)
    return [_with_own_slot(r, a) for r, a in zip(res[:n], packs)], _with_own_slot(res[n], meta)


def _grads_to_sibling(gs, name):
    n = len(gs)

    def body(*refs):
        ins, outs, send_sems, recv_sems = refs[:n], refs[n:2 * n], refs[2 * n], refs[2 * n + 1]
        x, y, c = _place()
        cps = []
        for i in range(n):
            rows = gs[i].shape[1] // 2
            cps.append(pltpu.make_async_remote_copy(src_ref=ins[i].at[:, pl.ds((1 - c) * rows, rows), :], dst_ref=outs[i],
                                                    send_sem=send_sems.at[i], recv_sem=recv_sems.at[i], device_id=(x, y, 1 - c),
                                                    device_id_type=MESH))
        for cp in cps:
            cp.start()
        for cp in cps:
            cp.wait()

    return pl.pallas_call(
        body, in_specs=[ANY] * n, out_specs=[ANY] * n,
        out_shape=[jax.ShapeDtypeStruct((g.shape[0], g.shape[1] // 2, g.shape[2]), g.dtype) for g in gs],
        scratch_shapes=[pltpu.SemaphoreType.DMA((n,)), pltpu.SemaphoreType.DMA((n,))],
        name=name,
    )(*gs)


def _share_halves(rs, name):
    n = len(rs)

    def body(*refs):
        ins, outs, send_sems, recv_sems = refs[:n], refs[n:2 * n], refs[2 * n], refs[2 * n + 1]
        x, y, c = _place()
        cps = [pltpu.make_async_remote_copy(src_ref=ins[i], dst_ref=outs[i], send_sem=send_sems.at[i], recv_sem=recv_sems.at[i],
                                            device_id=(x, y, 1 - c), device_id_type=MESH) for i in range(n)]
        for cp in cps:
            cp.start()
        for cp in cps:
            cp.wait()

    theirs = pl.pallas_call(
        body, in_specs=[ANY] * n, out_specs=[ANY] * n, out_shape=[jax.ShapeDtypeStruct(r.shape, r.dtype) for r in rs],
        scratch_shapes=[pltpu.SemaphoreType.DMA((n,)), pltpu.SemaphoreType.DMA((n,))],
        name=name,
    )(*rs)
    first = lax.axis_index("c") == 0
    return [jnp.where(first, jnp.concatenate([r, t], axis=0), jnp.concatenate([t, r], axis=0)) for r, t in zip(rs, theirs)]


def _chip_scatter(ss, name):
    n = len(ss)

    def body(*refs):
        ins, outs, send_sems, recv_sems = refs[:n], refs[n:2 * n], refs[2 * n], refs[2 * n + 1]
        x, y, c = _place()
        me = 2 * x + y
        chips = _other_chips(x, y)
        for i in range(n):
            for k, (px, py) in enumerate(chips):
                pltpu.make_async_remote_copy(src_ref=ins[i].at[2 * px + py], dst_ref=outs[i].at[me], send_sem=send_sems.at[3 * i + k],
                                             recv_sem=recv_sems.at[3 * i + k], device_id=(px, py, c), device_id_type=MESH).start()
        for i in range(n):
            for k, (px, py) in enumerate(chips):
                cp = pltpu.make_async_remote_copy(src_ref=ins[i].at[2 * px + py], dst_ref=outs[i].at[2 * px + py],
                                                  send_sem=send_sems.at[3 * i + k], recv_sem=recv_sems.at[3 * i + k],
                                                  device_id=(px, py, c), device_id_type=MESH)
                cp.wait_recv()
                cp.wait_send()

    parts = pl.pallas_call(
        body, in_specs=[ANY] * n, out_specs=[ANY] * n, out_shape=[jax.ShapeDtypeStruct(s.shape, s.dtype) for s in ss],
        scratch_shapes=[pltpu.SemaphoreType.DMA((3 * n,)), pltpu.SemaphoreType.DMA((3 * n,))],
        name=name,
    )(*ss)
    me = 2 * lax.axis_index("x") + lax.axis_index("y")
    return [_with_own_slot(p, lax.dynamic_index_in_dim(s, me, axis=0, keepdims=False)) for p, s in zip(parts, ss)]


HBM_SPACE = pltpu.MemorySpace.HBM


def _on_sequencer(name, collective_id, n_sems, body):
    @pl.kernel(mesh=plsc.ScalarSubcoreMesh(axis_name="sequencer", num_cores=1), name=name,
               scratch_types=(pltpu.SemaphoreType.DMA((n_sems,)), pltpu.SemaphoreType.DMA((n_sems,))),
               compiler_params=pltpu.CompilerParams(collective_id=collective_id))
    def launch(send_sems, recv_sems):
        body(send_sems, recv_sems)

    launch()


def _handshake(peers):
    barrier = pltpu.get_barrier_semaphore()
    for peer in peers:
        pl.semaphore_signal(barrier, inc=1, device_id=peer, device_id_type=MESH)
    pl.semaphore_wait(barrier, len(peers))


def _gather_on_sequencer(packs, name):
    n = len(packs)
    ins = [jax.new_ref(p, memory_space=HBM_SPACE) for p in packs]
    outs = [jax.empty_ref(jax.ShapeDtypeStruct((N_CHIPS,) + p.shape, p.dtype), memory_space=HBM_SPACE) for p in packs]

    def body(send_sems, recv_sems):
        x, y, c = _place()
        me, sib = 2 * x + y, (x, y, 1 - c)
        chips = _other_chips(x, y)
        _handshake([(px, py, c) for px, py in chips] + [sib])

        def copy(k, src, dst, to):
            return pltpu.make_async_remote_copy(src_ref=src, dst_ref=dst, send_sem=send_sems.at[k], recv_sem=recv_sems.at[k], device_id=to,
                                                device_id_type=MESH)

        def half(i, chip, h):
            rows = packs[i].shape[0] // 2
            return outs[i].at[chip, pl.ds(h * rows, rows), :]

        def mine(i):
            rows = packs[i].shape[0] // 2
            return ins[i].at[pl.ds(c * rows, rows), :]

        sent = [copy(6 * i + k, mine(i), half(i, me, c), (px, py, c)) for i in range(n) for k, (px, py) in enumerate(chips)]
        for cp in sent:
            cp.start()
        for i in range(n):
            for k, (px, py) in enumerate(chips):
                landed = half(i, 2 * px + py, c)
                copy(6 * i + k, mine(i), landed, (px, py, c)).wait_recv()
                fwd = copy(6 * i + 3 + k, landed, landed, sib)
                fwd.start()
                sent.append(fwd)
        for i in range(n):
            for k, (px, py) in enumerate(chips):
                other = half(i, 2 * px + py, 1 - c)
                copy(6 * i + 3 + k, other, other, sib).wait_recv()
        for cp in sent:
            cp.wait_send()

    _on_sequencer(name, 1, 6 * n, body)
    return [_with_own_slot(o[...], p) for o, p in zip(outs, packs)]


def _grads_to_sibling_on_sequencer(gs, name, collective_id):
    n = len(gs)
    ins = [jax.new_ref(g, memory_space=HBM_SPACE) for g in gs]
    outs = [jax.empty_ref(jax.ShapeDtypeStruct((g.shape[0], g.shape[1] // 2, g.shape[2]), g.dtype), memory_space=HBM_SPACE) for g in gs]

    def body(send_sems, recv_sems):
        x, y, c = _place()
        _handshake([(x, y, 1 - c)])
        cps = []
        for i in range(n):
            rows = gs[i].shape[1] // 2
            cps.append(pltpu.make_async_remote_copy(src_ref=ins[i].at[:, pl.ds((1 - c) * rows, rows), :], dst_ref=outs[i],
                                                    send_sem=send_sems.at[i], recv_sem=recv_sems.at[i], device_id=(x, y, 1 - c),
                                                    device_id_type=MESH))
        for cp in cps:
            cp.start()
        for cp in cps:
            cp.wait()

    _on_sequencer(name, collective_id, n, body)
    return [o[...] for o in outs]


def _chip_scatter_on_sequencer(ss, name, collective_id):
    n = len(ss)
    ins = [jax.new_ref(s, memory_space=HBM_SPACE) for s in ss]
    outs = [jax.empty_ref(jax.ShapeDtypeStruct(s.shape, s.dtype), memory_space=HBM_SPACE) for s in ss]

    def body(send_sems, recv_sems):
        x, y, c = _place()
        me = 2 * x + y
        chips = _other_chips(x, y)
        _handshake([(px, py, c) for px, py in chips])
        for i in range(n):
            for k, (px, py) in enumerate(chips):
                pltpu.make_async_remote_copy(src_ref=ins[i].at[2 * px + py], dst_ref=outs[i].at[me], send_sem=send_sems.at[3 * i + k],
                                             recv_sem=recv_sems.at[3 * i + k], device_id=(px, py, c), device_id_type=MESH).start()
        for i in range(n):
            for k, (px, py) in enumerate(chips):
                cp = pltpu.make_async_remote_copy(src_ref=ins[i].at[2 * px + py], dst_ref=outs[i].at[2 * px + py],
                                                  send_sem=send_sems.at[3 * i + k], recv_sem=recv_sems.at[3 * i + k],
                                                  device_id=(px, py, c), device_id_type=MESH)
                cp.wait_recv()
                cp.wait_send()

    _on_sequencer(name, collective_id, 3 * n, body)
    me = 2 * lax.axis_index("x") + lax.axis_index("y")
    return [_with_own_slot(o[...], lax.dynamic_index_in_dim(s, me, axis=0, keepdims=False)) for o, s in zip(outs, ss)]


def _all_reduce_small(v, name):
    shape = v.shape

    def body(v_ref, o_ref, slots, send_sems, recv_sems):
        x, y, c = _place()
        me = 4 * x + 2 * y + c
        slots[me] = v_ref[...]
        for r in range(1, N_DEV):
            peer = (x ^ (r >> 2), y ^ ((r >> 1) & 1), c ^ (r & 1))
            pltpu.make_async_remote_copy(src_ref=v_ref, dst_ref=slots.at[me], send_sem=send_sems.at[r - 1], recv_sem=recv_sems.at[r - 1],
                                         device_id=peer, device_id_type=MESH).start()
        for r in range(1, N_DEV):
            peer = (x ^ (r >> 2), y ^ ((r >> 1) & 1), c ^ (r & 1))
            cp = pltpu.make_async_remote_copy(src_ref=v_ref, dst_ref=slots.at[4 * peer[0] + 2 * peer[1] + peer[2]], send_sem=send_sems.at[r - 1],
                                              recv_sem=recv_sems.at[r - 1], device_id=peer, device_id_type=MESH)
            cp.wait_recv()
            cp.wait_send()
        acc = slots[0]
        for d in range(1, N_DEV):
            acc = acc + slots[d]
        o_ref[...] = acc

    vm = pl.BlockSpec(memory_space=pltpu.VMEM)
    return pl.pallas_call(
        body, in_specs=[vm], out_specs=vm, out_shape=jax.ShapeDtypeStruct(shape, F32),
        scratch_shapes=[pltpu.VMEM((N_DEV,) + shape, F32), pltpu.SemaphoreType.DMA((N_DEV - 1,)), pltpu.SemaphoreType.DMA((N_DEV - 1,))],
        name=name,
    )(v)


def _add_sibling(g, gsib, core, name):
    n, _, cdim = g.shape
    half = gsib.shape[1]
    tr = half // 2

    def body(core_ref, a_ref, b_ref, o_ref):
        o_ref[...] = (a_ref[...] + b_ref[...]).astype(o_ref.dtype)

    blk = pl.BlockSpec((1, tr, cdim), lambda j, i, core_ref: (j, i, 0))
    return pl.pallas_call(
        body,
        grid_spec=pltpu.PrefetchScalarGridSpec(
            num_scalar_prefetch=1, grid=(n, half // tr),
            in_specs=[pl.BlockSpec((1, tr, cdim), lambda j, i, core_ref: (j, core_ref[0] * (half // tr) + i, 0)), blk], out_specs=blk),
        out_shape=jax.ShapeDtypeStruct(gsib.shape, BF16), compiler_params=_params(("parallel", "parallel")), name=name,
    )(core, g, gsib)


def _sum_parts(parts, name):
    n, r, cdim = parts.shape
    tr = r // 2

    def body(p_ref, o_ref):
        acc = p_ref[0].astype(F32)
        for j in range(1, n):
            acc = acc + p_ref[j].astype(F32)
        o_ref[...] = acc

    return pl.pallas_call(
        body, grid=(r // tr,), in_specs=[pl.BlockSpec((n, tr, cdim), lambda i: (0, i, 0))],
        out_specs=pl.BlockSpec((tr, cdim), lambda i: (i, 0)), out_shape=jax.ShapeDtypeStruct((r, cdim), F32),
        compiler_params=_params(("parallel",)), name=name,
    )(parts)


def _adamw(parts, w, m, v, name):
    npart, r, cdim = parts.shape
    tr = r // 4 if r % 32 == 0 else r

    def body(p_ref, w_ref, m_ref, v_ref, g_ref, d_ref, nm_ref, nv_ref):
        g = p_ref[0]
        for j in range(1, npart):
            g = g + p_ref[j]
        m_new = ADAM_B1 * m_ref[...] + (1.0 - ADAM_B1) * g
        v_new = ADAM_B2 * v_ref[...] + (1.0 - ADAM_B2) * (g * g)
        m_hat = m_new / (1.0 - ADAM_B1 ** ADAM_STEP)
        v_hat = v_new / (1.0 - ADAM_B2 ** ADAM_STEP)
        g_ref[...] = g
        d_ref[...] = -ADAM_LR * (m_hat / (jnp.sqrt(v_hat) + ADAM_EPS) + ADAM_WD * w_ref[...])
        nm_ref[...] = m_new
        nv_ref[...] = v_new

    blk = pl.BlockSpec((tr, cdim), lambda i: (i, 0))
    shp = jax.ShapeDtypeStruct((r, cdim), F32)
    return pl.pallas_call(
        body, grid=(r // tr,), in_specs=[pl.BlockSpec((npart, tr, cdim), lambda i: (0, i, 0)), blk, blk, blk], out_specs=[blk] * 4,
        out_shape=[shp] * 4, compiler_params=_params(("parallel",)), name=name,
    )(parts, w, m, v)


BIG = ("ev_w_in", "ev_w_uq", "ev_w_ukv", "ev_w_out", "od_w_in", "od_w_out", "meta")
SMALL = ("norm_g", "final_g", "ev_q_norm_g", "ev_kv_norm_g", "od_sinks")
SMALL_SHAPE = (8, 512)
BY_ROWS = ("ev_w_out", "od_w_out")

EV_IN_SHARD, OD_IN_SHARD, UQ_SHARD = 2976 // N_CHIPS, 2304 // N_CHIPS, 768 // N_CHIPS


def _pack_big(a, lead=()):
    dtype = a["ev_w_in"].dtype
    z = lambda r, c: jnp.zeros(lead + (r, c), dtype)
    ax = len(lead)
    pad_to = lambda x, width: jnp.concatenate([x, z(x.shape[ax], width - x.shape[ax + 1])], axis=ax + 1)
    corner = jnp.concatenate([a["ev_w_ukv"], a["meta"], z(256 - MLA_KV_LORA - N_META, 256)], axis=ax)
    return (pad_to(a["ev_w_in"], 768), jnp.concatenate([pad_to(a["ev_w_uq"], 256), corner], axis=ax + 1),
            pad_to(a["od_w_in"], 640), jnp.concatenate([a["ev_w_out"], a["od_w_out"]], axis=ax + 1))


N_FIRST = 2


def _unpack_big(p_in0, p_lat, p_in1, p_out):
    return dict(ev_w_in=p_in0[..., :EV_IN_SHARD], od_w_in=p_in1[..., :OD_IN_SHARD], ev_w_out=p_out[..., :D_MODEL],
                od_w_out=p_out[..., D_MODEL:], ev_w_uq=p_lat[..., :UQ_SHARD], ev_w_ukv=p_lat[..., :MLA_KV_LORA, 256:],
                meta=p_lat[..., MLA_KV_LORA:MLA_KV_LORA + N_META, 256:])


def _chip_shards(full, by_rows):
    if by_rows:
        return full.reshape(N_CHIPS, full.shape[0] // N_CHIPS, full.shape[1])
    return full.reshape(full.shape[0], N_CHIPS, -1).transpose(1, 0, 2)


def _from_chip_shards(slots, by_rows):
    if by_rows:
        return slots.reshape(-1, slots.shape[2])
    return slots.transpose(1, 0, 2).reshape(slots.shape[1], -1)


def _pack_small(arrs, extra=None):
    flat = [a.reshape(-1) for a in arrs] + ([] if extra is None else [extra.reshape(-1)])
    used = sum(f.shape[0] for f in flat)
    return jnp.pad(jnp.concatenate(flat), (0, SMALL_SHAPE[0] * SMALL_SHAPE[1] - used)).reshape(SMALL_SHAPE)


def _unpack_small(p, shapes):
    flat, out, at = p.reshape(-1), [], 0
    for s in shapes:
        n = int(np.prod(s))
        out.append(flat[at:at + n].reshape(s))
        at += n
    return out, flat[at]


def kernel(x, meta, norm_g, final_g, ev_w_in, ev_q_norm_g, ev_kv_norm_g, ev_w_uq, ev_w_ukv, ev_w_out, od_w_in, od_sinks, od_w_out, loss_target, m_meta, m_norm_g, m_final_g, m_ev_w_in, m_ev_q_norm_g, m_ev_kv_norm_g, m_ev_w_uq, m_ev_w_ukv, m_ev_w_out, m_od_w_in, m_od_sinks, m_od_w_out, v_meta, v_norm_g, v_final_g, v_ev_w_in, v_ev_q_norm_g, v_ev_kv_norm_g, v_ev_w_uq, v_ev_w_ukv, v_ev_w_out, v_od_w_in, v_od_sinks, v_od_w_out):
    given = dict(locals())
    two_d = lambda a: a[0] if a.ndim == 3 else a
    packs = {k: _pack_big({n: two_d(given[k + n]) for n in BIG}) for k in ("", "m_", "v_")}

    wbf = [p.astype(BF16) for p in packs[""]]
    later = _gather_on_sequencer(wbf[N_FIRST:], "gather_later_weights")
    first, meta_all = _gather_weights(wbf[:N_FIRST], meta, "gather_weights")
    full = {n: _from_chip_shards(a, n in BY_ROWS) for n, a in _unpack_big(*first, *later).items()}
    meta_full = _from_chip_shards(meta_all, False)

    loss, grad_x, grads = _local_step(x, meta_full, norm_g, final_g, ev_q_norm_g, ev_kv_norm_g, od_sinks, loss_target,
                                      full["ev_w_in"], full["ev_w_uq"], full["ev_w_ukv"], full["ev_w_out"], full["od_w_in"], full["od_w_out"])

    gpacks = _pack_big({n: _chip_shards(grads[n], n in BY_ROWS) for n in BIG}, lead=(N_CHIPS,))
    core = lax.axis_index("c").astype(jnp.int32).reshape(1)

    def reduce_group(group, tag, to_sibling, scatter):
        sums = [_add_sibling(g, s, core, f"add_sibling_{tag}{i}") for i, (g, s) in enumerate(zip(group, to_sibling(group)))]
        parts = [_sum_parts(p, f"add_chips_{tag}{i}") for i, p in enumerate(scatter(sums))]
        return _share_halves(parts, "reduced_to_sibling_" + tag)

    reduced_later = reduce_group(gpacks[N_FIRST:], "later", lambda g: _grads_to_sibling_on_sequencer(g, "grads_to_sibling_later", 3),
                                 lambda s: _chip_scatter_on_sequencer(s, "grads_to_chips_later", 2))
    reduced_first = reduce_group(gpacks[:N_FIRST], "first", lambda g: _grads_to_sibling(g, "grads_to_sibling_first"),
                                 lambda s: _chip_scatter_on_sequencer(s, "grads_to_chips_first", 4))
    reduced = reduced_first + reduced_later
    updated = [_adamw(r[None], packs[""][i], packs["m_"][i], packs["v_"][i], f"adamw_matrices_{i}") for i, r in enumerate(reduced)]
    big_out = [{n: a.reshape(given[n].shape) for n, a in _unpack_big(*outs).items()} for outs in zip(*updated)]

    small_shapes = [given[n].shape for n in SMALL]
    ssum = _all_reduce_small(_pack_small([grads[n] for n in SMALL], loss[0, 0]), "reduce_vectors")
    small_out = _adamw(ssum[None], _pack_small([given[n] for n in SMALL]), _pack_small([given["m_" + n] for n in SMALL]),
                       _pack_small([given["v_" + n] for n in SMALL]), "adamw_vectors")
    total_loss = ssum.reshape(-1)[sum(int(np.prod(s)) for s in small_shapes)]
    small_out = [_unpack_small(o, small_shapes)[0] for o in small_out]

    names = ("meta", "norm_g", "final_g", "ev_w_in", "ev_q_norm_g", "ev_kv_norm_g", "ev_w_uq", "ev_w_ukv", "ev_w_out", "od_w_in", "od_sinks",
             "od_w_out")
    outs = [total_loss, grad_x]
    for kind in range(4):
        for n in names:
            outs.append(big_out[kind][n] if n in BIG else small_out[kind][SMALL.index(n)])
    return tuple(outs)
```

```python
import functools
import math

import numpy as np
import jax
import jax.numpy as jnp
from jax import lax
from jax.experimental import pallas as pl
from jax.experimental.pallas import tpu as pltpu
from jax.experimental.pallas import tpu_sc as plsc

F32 = jnp.float32
BF16 = jnp.bfloat16

D_MODEL = 1024
BLOCK = 128
N_META = 16
N_PAD = BLOCK - N_META
NORM_EPS = 1e-6
NEG = -1e30
HEAD = 64
SB_HEADS = 8
MLA_HEADS = 8
MLA_Q_LORA = 256
MLA_KV_LORA = 128
MLA_NOPE = 64
MLA_ROPE = 32
ROPE_BASE = 10000.0
SWA_HEADS = 16
SWA_KV_HEADS = 2
SWA_WINDOW = 128
N_CHIPS = 4
N_DEV = 8

ADAM_LR = 0.001
ADAM_B1 = 0.9
ADAM_B2 = 0.999
ADAM_EPS = 1e-08
ADAM_WD = 0.01
ADAM_STEP = 10

VMEM_LIMIT = 48 * 1024 * 1024

EV_G, EV_Q, EV_K, EV_V, EV_CQ, EV_CKV, EV_KR, EV_N = 0, 1024, 1536, 2048, 2560, 2816, 2944, 3072
OD_G, OD_Q, OD_K, OD_V, OD_N = 0, 1024, 2048, 2176, 2304


def _params(sem=None):
    return pltpu.CompilerParams(dimension_semantics=sem, vmem_limit_bytes=VMEM_LIMIT)


def _row_tile(m):
    return 256 if m % 256 == 0 else 128


def _matmul_rows(m):
    for c in (1088, 1024, 768, 640, 512, 384, 256):
        if m % c == 0:
            return c
    return 128


def _dot(a, b):
    return jnp.dot(a.astype(BF16), b.astype(BF16), preferred_element_type=F32)


def _dot_nt(a, b):
    return lax.dot_general(a.astype(BF16), b.astype(BF16), (((1,), (1,)), ((), ())), preferred_element_type=F32)


def _dot_tn(a, b):
    return lax.dot_general(a.astype(BF16), b.astype(BF16), (((0,), (0,)), ((), ())), preferred_element_type=F32)


def _rms_fwd(h, g, name):
    t, d = h.shape
    tm = _row_tile(t)

    def body(h_ref, g_ref, o_ref):
        x = h_ref[...]
        r = lax.rsqrt(jnp.mean(x * x, axis=-1, keepdims=True) + NORM_EPS)
        o_ref[...] = ((x * r) * g_ref[...]).astype(o_ref.dtype)

    return pl.pallas_call(
        body, grid=(t // tm,),
        in_specs=[pl.BlockSpec((tm, d), lambda i: (i, 0)), pl.BlockSpec((1, d), lambda i: (0, 0))],
        out_specs=pl.BlockSpec((tm, d), lambda i: (i, 0)),
        out_shape=jax.ShapeDtypeStruct((t, d), BF16), compiler_params=_params(("parallel",)), name=name,
    )(h, g)


def _rms_bwd(h, g, dy, dres, name):
    t, d = h.shape
    tm = _row_tile(t)

    def body(h_ref, g_ref, dy_ref, dres_ref, dh_ref, dg_ref):
        @pl.when(pl.program_id(0) == 0)
        def _():
            dg_ref[...] = jnp.zeros_like(dg_ref)

        x = h_ref[...]
        r = lax.rsqrt(jnp.mean(x * x, axis=-1, keepdims=True) + NORM_EPS)
        xr = x * r
        dy_ = dy_ref[...]
        u = dy_ * g_ref[...]
        dh_ref[...] = dres_ref[...] + r * (u - xr * jnp.mean(u * xr, axis=-1, keepdims=True))
        dg_ref[...] += jnp.sum(dy_ * xr, axis=0, keepdims=True)

    row = pl.BlockSpec((tm, d), lambda i: (i, 0))
    vec = pl.BlockSpec((1, d), lambda i: (0, 0))
    return pl.pallas_call(
        body, grid=(t // tm,), in_specs=[row, vec, row, row], out_specs=[row, vec],
        out_shape=[jax.ShapeDtypeStruct((t, d), F32), jax.ShapeDtypeStruct((1, d), F32)],
        compiler_params=_params(("arbitrary",)), name=name,
    )(h, g, dy, dres)


def _col_tile(n):
    for c in (1024, 768, 640, 512, 384, 256, 128):
        if n % c == 0:
            return c
    return n


def _mm(a, w, name, res=None, out_dtype=F32, a_cols=None):
    m = a.shape[0]
    k, n = w.shape
    a_blk = 0 if a_cols is None else a_cols[0] // k
    assert a_cols is None or (a_cols[1] == k and a_cols[0] % k == 0)
    tm, tn = _matmul_rows(m), _col_tile(n)

    def body(*refs):
        if res is None:
            a_ref, w_ref, o_ref = refs
            acc = _dot(a_ref[...], w_ref[...])
        else:
            a_ref, w_ref, r_ref, o_ref = refs
            acc = r_ref[...] + _dot(a_ref[...], w_ref[...])
        o_ref[...] = acc.astype(o_ref.dtype)

    in_specs = [pl.BlockSpec((tm, k), lambda j, i: (i, a_blk)), pl.BlockSpec((k, tn), lambda j, i: (0, j))]
    args = [a, w]
    if res is not None:
        in_specs.append(pl.BlockSpec((tm, tn), lambda j, i: (i, j)))
        args.append(res)
    return pl.pallas_call(
        body, grid=(n // tn, m // tm), in_specs=in_specs, out_specs=pl.BlockSpec((tm, tn), lambda j, i: (i, j)),
        out_shape=jax.ShapeDtypeStruct((m, n), out_dtype), compiler_params=_params(("parallel", "parallel")), name=name,
    )(*args)


def _mm_nt(a, w, name):
    m, n = a.shape
    k = w.shape[0]
    tm, tk = _matmul_rows(m), _col_tile(k)

    def body(a_ref, w_ref, o_ref):
        o_ref[...] = _dot_nt(a_ref[...], w_ref[...])

    return pl.pallas_call(
        body, grid=(k // tk, m // tm),
        in_specs=[pl.BlockSpec((tm, n), lambda j, i: (i, 0)), pl.BlockSpec((tk, n), lambda j, i: (j, 0))],
        out_specs=pl.BlockSpec((tm, tk), lambda j, i: (i, j)),
        out_shape=jax.ShapeDtypeStruct((m, k), F32), compiler_params=_params(("parallel", "parallel")), name=name,
    )(a, w)


def _mm_tn(x, dy, name):
    m, k = x.shape
    n = dy.shape[1]
    tm, tn = _matmul_rows(m), _col_tile(n)

    def body(x_ref, dy_ref, o_ref):
        @pl.when(pl.program_id(1) == 0)
        def _():
            o_ref[...] = jnp.zeros_like(o_ref)

        o_ref[...] += _dot_tn(x_ref[...], dy_ref[...])

    return pl.pallas_call(
        body, grid=(n // tn, m // tm),
        in_specs=[pl.BlockSpec((tm, k), lambda j, i: (i, 0)), pl.BlockSpec((tm, tn), lambda j, i: (i, j))],
        out_specs=pl.BlockSpec((k, tn), lambda j, i: (0, j)),
        out_shape=jax.ShapeDtypeStruct((k, n), F32), compiler_params=_params(("parallel", "arbitrary")), name=name,
    )(x, dy)


def _silu_parts(g):
    s = 1.0 / (1.0 + jnp.exp(-g))
    return g * s, s * (1.0 + g * (1.0 - s))


def _gate_fwd(o_parts, proj, name):
    t = proj.shape[0]
    tm = _row_tile(t)
    w = D_MODEL // len(o_parts)

    def body(*refs):
        g_ref, o_ref = refs[-2], refs[-1]
        for p, r in enumerate(refs[:-2]):
            sil, _ = _silu_parts(g_ref[:, p * w:(p + 1) * w])
            o_ref[:, p * w:(p + 1) * w] = (r[...].astype(F32) * sil).astype(o_ref.dtype)

    return pl.pallas_call(
        body, grid=(t // tm,),
        in_specs=[pl.BlockSpec((tm, w), lambda i: (i, 0)) for _ in o_parts] + [pl.BlockSpec((tm, D_MODEL), lambda i: (i, 0))],
        out_specs=pl.BlockSpec((tm, D_MODEL), lambda i: (i, 0)),
        out_shape=jax.ShapeDtypeStruct((t, D_MODEL), BF16), compiler_params=_params(("parallel",)), name=name,
    )(*o_parts, proj)


def _gate_bwd(dao, o_parts, proj, name):
    t = proj.shape[0]
    tm = _row_tile(t)
    np_ = len(o_parts)
    w = D_MODEL // np_

    def body(*refs):
        dao_ref, g_ref = refs[0], refs[1 + np_]
        do_refs, dg_ref = refs[2 + np_:2 + 2 * np_], refs[-1]
        for p in range(np_):
            sl = slice(p * w, (p + 1) * w)
            sil, dsil = _silu_parts(g_ref[:, sl])
            da = dao_ref[:, sl]
            do_refs[p][...] = da * sil
            dg_ref[:, sl] = (da * refs[1 + p][...].astype(F32) * dsil).astype(dg_ref.dtype)

    full = pl.BlockSpec((tm, D_MODEL), lambda i: (i, 0))
    part = pl.BlockSpec((tm, w), lambda i: (i, 0))
    outs = pl.pallas_call(
        body, grid=(t // tm,), in_specs=[full] + [part] * np_ + [full], out_specs=[part] * np_ + [full],
        out_shape=[jax.ShapeDtypeStruct((t, w), F32)] * np_ + [jax.ShapeDtypeStruct((t, D_MODEL), BF16)],
        compiler_params=_params(("parallel",)), name=name,
    )(dao, *o_parts, proj)
    return outs[:np_], outs[np_]


def _loss_head(h2, gf, target, b, lp):
    d = h2.shape[1]
    nb = lp // BLOCK
    h3 = h2.reshape(b, lp, d)

    def body(h_ref, g_ref, t_ref, dh_ref, dg_ref, loss_ref):
        first = (pl.program_id(0) == 0) & (pl.program_id(1) == 0)

        @pl.when(first)
        def _():
            dg_ref[...] = jnp.zeros_like(dg_ref)
            loss_ref[...] = jnp.zeros_like(loss_ref)

        @pl.when(pl.program_id(1) == 0)
        def _():
            dh_ref[...] = jnp.zeros_like(dh_ref)

        @pl.when(pl.program_id(1) > 0)
        def _():
            x = h_ref[0]
            r = lax.rsqrt(jnp.mean(x * x, axis=-1, keepdims=True) + NORM_EPS)
            xr = x * r
            g = g_ref[...]
            diff = xr * g - t_ref[0]
            loss_ref[...] += 0.5 * jnp.sum(jnp.mean(diff * diff, axis=-1, keepdims=True))
            dy = diff * (1.0 / d)
            u = dy * g
            dh_ref[0] = r * (u - xr * jnp.mean(u * xr, axis=-1, keepdims=True))
            dg_ref[...] += jnp.sum(dy * xr, axis=0, keepdims=True)

    blk = pl.BlockSpec((1, BLOCK, d), lambda bi, n: (bi, n, 0))
    dh, dg, loss = pl.pallas_call(
        body, grid=(b, nb),
        in_specs=[blk, pl.BlockSpec((1, d), lambda bi, n: (0, 0)),
                  pl.BlockSpec((1, BLOCK, d), lambda bi, n: (bi, jnp.maximum(n - 1, 0), 0))],
        out_specs=[blk, pl.BlockSpec((1, d), lambda bi, n: (0, 0)), pl.BlockSpec((8, 128), lambda bi, n: (0, 0))],
        out_shape=[jax.ShapeDtypeStruct((b, lp, d), F32), jax.ShapeDtypeStruct((1, d), F32), jax.ShapeDtypeStruct((8, 128), F32)],
        compiler_params=_params(("arbitrary", "arbitrary")), name="loss_head",
    )(h3, gf, target)
    return dh.reshape(b * lp, d), dg, loss


def _iota2(shape, dim):
    return lax.broadcasted_iota(jnp.int32, shape, dim)


KEYS = 512
SB_FWD_KEYS = 512


def _lo_lanes():
    return _iota2((1, BLOCK), 1) < HEAD


def _halves(x, lo):
    zero = jnp.zeros_like(x)
    return jnp.where(lo, x, zero), jnp.where(lo, zero, x)


def _rows_of_pair(a, b):
    return jnp.where(_iota2((BLOCK, 1), 0) < HEAD, a, b)


def _split_rows_t(x):
    xt = x.T
    first = _iota2(xt.shape, 0) < HEAD
    zero = jnp.zeros_like(xt)
    return jnp.concatenate([jnp.where(first, xt, zero), jnp.where(first, zero, xt)], axis=1).astype(BF16)


def _key_chunk(c, lp, t_idx, strict, key_axis):
    keys = t_idx.shape[key_axis]
    first = c * keys
    s0 = pl.multiple_of(jnp.minimum(first, lp - keys), BLOCK)
    s_idx = s0 + _iota2(t_idx.shape, key_axis)
    seen = (s_idx < t_idx) if strict else (s_idx <= t_idx)
    return s0, seen & (s_idx >= jnp.maximum(first, N_PAD))


def _tri_dot(x, tri):
    return jnp.dot(x.astype(BF16), tri, preferred_element_type=F32)


def _stack_halves(x, lo):
    a, b = _halves(x, lo)
    return jnp.concatenate([a, b], axis=0)


def _pair(a, b, lo):
    return jnp.where(lo, a, b)


def _chunk_starts(lp):
    return [min(c * KEYS, lp - KEYS) for c in range(-(-lp // KEYS))]


def _put_rows(ref, r0, bq, a, b):
    for t in range(bq // BLOCK):
        part = slice(t * BLOCK, (t + 1) * BLOCK)
        ref[0, 0, r0 // BLOCK + t] = jnp.concatenate([a[:, part], b[:, part], jnp.zeros((6, BLOCK), F32)], axis=0)


def _get_rows(ref, r0, bq):
    return [jnp.concatenate([ref[0, 0, r0 // BLOCK + t, h:h + 1, :] for t in range(bq // BLOCK)], axis=1) for h in range(2)]


def _n_chunks(i, keys):
    return ((i + 1) * BLOCK + keys - 1) // keys


QROWS = 512


def _for_query_tiles(nb, tile, keys=KEYS):
    per = QROWS // BLOCK

    def step(j, _):
        tile(pl.multiple_of(j * QROWS, QROWS), QROWS, (j + 1) * (QROWS // keys))
        return 0

    lax.fori_loop(0, nb // per, step, 0)
    for i in range(nb - nb % per, nb):
        tile(i * BLOCK, BLOCK, _n_chunks(i, keys))


def _walk_chunks(r0, n, chunk, carry, leftwards=False, keys=KEYS):
    diag = jnp.maximum(r0 // keys, 1)

    def span(first, last, masked, carry):
        def step(t, cr):
            return chunk(last - 1 - t if leftwards else first + t, cr, masked)
        return lax.fori_loop(0, last - first, step, carry)

    spans = [(0, 1, True), (1, diag, False), (diag, n, True)]
    for first, last, masked in (reversed(spans) if leftwards else spans):
        carry = span(first, last, masked, carry)
    return carry


def _where(valid, x, other):
    return x if valid is None else jnp.where(valid, x, other)


HEAD_SCALE = HEAD ** -0.5
assert math.frexp(HEAD_SCALE)[0] == 0.5


def _sb_scores(q_h, k, valid, after):
    z = _dot_nt(q_h, k)
    lb = jnp.minimum(z, 0.0) - jnp.log(1.0 + jnp.exp(-jnp.abs(z)))
    l1m_all = lb - z
    l1m = _where(valid, l1m_all, 0.0)
    return lb, l1m_all, l1m, _tri_dot(l1m, after)


def _pair_stat_spec(nb):
    return pl.BlockSpec((1, 1, nb, 8, BLOCK), lambda bi, hp: (bi, hp, 0, 0, 0))


def _sb_fwd(proj3, name):
    b, lp, _ = proj3.shape
    nb = lp // BLOCK
    npair = SB_HEADS // 2

    keys = SB_FWD_KEYS

    def body(q_ref, k_ref, v_ref, o_ref, tot_ref):
        lo = _lo_lanes()
        after = (_iota2((keys, keys), 0) > _iota2((keys, keys), 1)).astype(BF16)

        def qtile(r0, bq, n):
            qs = _halves((q_ref[0, pl.ds(r0, bq), :] * HEAD_SCALE).astype(BF16), lo)
            t_idx = r0 + _iota2((bq, keys), 0)

            def kchunk(c, carry, masked):
                cs, acc = carry[:2], carry[2]
                s0, valid = _key_chunk(c, lp, t_idx, True, 1)
                valid = valid if masked else None
                k = k_ref[0, pl.ds(s0, keys), :].astype(BF16)
                a_s, new = [], []
                for h in range(2):
                    lb, _, l1m, suf = _sb_scores(qs[h], k, valid, after)
                    a_s.append(_where(valid, jnp.exp(lb + suf + cs[h]), 0.0).astype(BF16))
                    new.append(cs[h] + jnp.sum(l1m, axis=1, keepdims=True))
                v_bd = _stack_halves(v_ref[0, pl.ds(s0, keys), :].astype(BF16), lo)
                return (*new, acc + jnp.dot(jnp.concatenate(a_s, axis=1), v_bd, preferred_element_type=F32))

            zero = jnp.zeros((bq, 1), F32)
            c_a, c_b, acc = _walk_chunks(r0, n, kchunk, (zero, zero, jnp.zeros((bq, BLOCK), F32)), leftwards=True, keys=keys)
            o_ref[0, pl.ds(r0, bq), :] = acc
            tot_ref[0, pl.ds(r0, bq), :] = jnp.broadcast_to(_pair(c_a, c_b, lo), (bq, BLOCK))

        _for_query_tiles(nb, qtile, keys)

    def col(first):
        return pl.BlockSpec((1, lp, 2 * HEAD), lambda bi, hp: (bi, 0, first // (2 * HEAD) + hp))

    shp = jax.ShapeDtypeStruct((b, lp, SB_HEADS * HEAD), F32)
    return pl.pallas_call(
        body, grid=(b, npair), in_specs=[col(EV_Q), col(EV_K), col(EV_V)], out_specs=[col(0), col(0)], out_shape=[shp, shp],
        compiler_params=_params(("parallel", "parallel")), name=name,
    )(proj3, proj3, proj3)


def _sb_bwd(proj3, tot, do, name):
    b, lp, _ = proj3.shape
    nb = lp // BLOCK
    npair = SB_HEADS // 2

    def body(q_ref, k_ref, v_ref, tot_ref, do_ref, dq_ref, dk_ref, dv_ref):
        lo = _lo_lanes()
        after = (_iota2((KEYS, KEYS), 0) > _iota2((KEYS, KEYS), 1)).astype(BF16)
        before = (_iota2((KEYS, KEYS), 0) < _iota2((KEYS, KEYS), 1)).astype(BF16)
        dk_ref[...] = jnp.zeros_like(dk_ref)
        dv_ref[...] = jnp.zeros_like(dv_ref)

        def qtile(r0, bq, n):
            rows = pl.ds(r0, bq)
            qs = _halves((q_ref[0, rows, :] * HEAD_SCALE).astype(BF16), lo)
            dos = _halves(do_ref[0, rows, :].astype(BF16), lo)
            tot_i = tot_ref[0, rows, :]
            tots = (tot_i[:, 0:1], tot_i[:, HEAD:HEAD + 1])
            q_st, do_st = jnp.concatenate(qs, axis=0), jnp.concatenate(dos, axis=0)
            t_idx = r0 + _iota2((bq, KEYS), 0)

            def kchunk(c, carry, masked):
                s0, valid = _key_chunk(c, lp, t_idx, True, 1)
                valid = valid if masked else None
                keys = pl.ds(s0, KEYS)
                k = k_ref[0, keys, :].astype(BF16)
                v = v_ref[0, keys, :].astype(BF16)
                a_s, dzs, new = [], [], []
                for h in range(2):
                    left, pre = carry[2 * h], carry[2 * h + 1]
                    lb, l1m_all, l1m, suf = _sb_scores(qs[h], k, valid, after)
                    here = jnp.sum(l1m, axis=1, keepdims=True)
                    a = _where(valid, jnp.exp(lb + suf + (tots[h] - left - here)), 0.0)
                    w = a * _dot_nt(dos[h], v)
                    dz = _where(valid, w * jnp.exp(l1m_all) - (pre + _tri_dot(w, before)) * jnp.exp(lb), 0.0)
                    new += [left + here, pre + jnp.sum(w, axis=1, keepdims=True)]
                    a_s.append(a.astype(BF16))
                    dzs.append(dz.astype(BF16))
                dk_ref[0, keys, :] += _dot_tn(jnp.concatenate(dzs, axis=0), q_st)
                dv_ref[0, keys, :] += _dot_tn(jnp.concatenate(a_s, axis=0), do_st)
                dq = carry[4] + jnp.dot(jnp.concatenate(dzs, axis=1), _stack_halves(k, lo), preferred_element_type=F32)
                return (*new, dq)

            zero = jnp.zeros((bq, 1), F32)
            out = _walk_chunks(r0, n, kchunk, (zero, zero, zero, zero, jnp.zeros((bq, BLOCK), F32)))
            dq_ref[0, rows, :] = out[4] * HEAD_SCALE

        _for_query_tiles(nb, qtile)

    def col(first):
        return pl.BlockSpec((1, lp, 2 * HEAD), lambda bi, hp: (bi, 0, first // (2 * HEAD) + hp))

    shp = jax.ShapeDtypeStruct((b, lp, SB_HEADS * HEAD), F32)
    return pl.pallas_call(
        body, grid=(b, npair), in_specs=[col(EV_Q), col(EV_K), col(EV_V), col(0), col(0)], out_specs=[col(0)] * 3, out_shape=[shp] * 3,
        compiler_params=_params(("parallel", "parallel")), name=name,
    )(proj3, proj3, proj3, tot, do)


def _rope_tables(lp):
    half = MLA_ROPE // 2
    pos = (np.arange(lp) - N_PAD).astype(np.float32)
    inv = jnp.asarray(ROPE_BASE, F32) ** (-jnp.arange(half, dtype=F32) / half)
    ang = jnp.asarray(pos)[:, None] * inv[None, :]
    cos, sin = jnp.cos(ang), jnp.sin(ang)
    zeros = lambda n: jnp.zeros((lp, n), F32)
    c = jnp.concatenate([jnp.ones((lp, MLA_NOPE), F32), cos, cos, zeros(32)], axis=1)
    s1 = jnp.concatenate([zeros(MLA_NOPE), -sin, zeros(half), zeros(32)], axis=1)
    s2 = jnp.concatenate([zeros(MLA_NOPE), zeros(half), sin, zeros(32)], axis=1)
    return c, s1, s2


def _rope(x, c, s1, s2):
    half = MLA_ROPE // 2
    return x * c + pltpu.roll(x, BLOCK - half, 1) * s1 + pltpu.roll(x, half, 1) * s2


def _rope_t(dy, c, s1, s2):
    half = MLA_ROPE // 2
    return dy * c + pltpu.roll(dy * s1, half, 1) + pltpu.roll(dy * s2, BLOCK - half, 1)


def _rms_rows(x, g):
    r = lax.rsqrt(jnp.mean(x * x, axis=-1, keepdims=True) + NORM_EPS)
    return x * r, r


def _prep_rows(lp):
    return lp // 4 if lp % 64 == 0 else BLOCK


def _mla_prep_fwd(proj3, gq, gkv, wq, wk, wv, tabs, name):
    b, lp, _ = proj3.shape
    rows = _prep_rows(lp)
    hw = MLA_HEADS * BLOCK

    def body(cq_ref, ckv_ref, kr_ref, gq_ref, gkv_ref, wq_ref, wk_ref, wv_ref, c_ref, s1_ref, s2_ref, qf_ref, kf_ref, v_ref):
        c, s1, s2 = c_ref[...], s1_ref[...], s2_ref[...]
        xq, _ = _rms_rows(cq_ref[0], None)
        qh = _dot(xq * gq_ref[...], wq_ref[...])
        xk, _ = _rms_rows(ckv_ref[0], None)
        ckv_n = xk * gkv_ref[...]
        kv = _dot(ckv_n, wk_ref[...])
        v_ref[0] = _dot(ckv_n, wv_ref[...]).astype(v_ref.dtype)
        kr = _rope(kr_ref[0], c, s1, s2)
        for h in range(MLA_HEADS):
            ls = slice(h * BLOCK, (h + 1) * BLOCK)
            qf_ref[0, :, ls] = _rope(qh[:, ls], c, s1, s2).astype(qf_ref.dtype)
            kf_ref[0, :, ls] = (kv[:, ls] + kr).astype(kf_ref.dtype)

    def col(first, width):
        return pl.BlockSpec((1, rows, width), lambda bi, n: (bi, n, first // width))

    def whole(a):
        return pl.BlockSpec(a.shape, lambda bi, n: (0,) * a.ndim)

    tab = pl.BlockSpec((rows, BLOCK), lambda bi, n: (n, 0))
    return pl.pallas_call(
        body, grid=(b, lp // rows),
        in_specs=[col(EV_CQ, MLA_Q_LORA), col(EV_CKV, MLA_KV_LORA), col(EV_KR, BLOCK), whole(gq), whole(gkv), whole(wq), whole(wk),
                  whole(wv), tab, tab, tab],
        out_specs=[col(0, hw), col(0, hw), col(0, MLA_HEADS * HEAD)],
        out_shape=[jax.ShapeDtypeStruct((b, lp, hw), BF16), jax.ShapeDtypeStruct((b, lp, hw), BF16),
                   jax.ShapeDtypeStruct((b, lp, MLA_HEADS * HEAD), BF16)],
        compiler_params=_params(("parallel", "parallel")), name=name,
    )(proj3, proj3, proj3, gq, gkv, wq, wk, wv, *tabs)


def _mla_prep_bwd(proj3, gq, gkv, wq, wk, wv, tabs, dqf, dkf, dv, name):
    b, lp, _ = proj3.shape
    rows = _prep_rows(lp)
    hw = MLA_HEADS * BLOCK

    def body(cq_ref, ckv_ref, gq_ref, gkv_ref, wq_ref, wk_ref, wv_ref, c_ref, s1_ref, s2_ref, dqf_ref, dkf_ref, dv_ref,
             dcq_ref, dckv_ref, dkr_ref, dwq_ref, dwk_ref, dwv_ref, dgq_ref, dgkv_ref, dqh):
        @pl.when((pl.program_id(0) == 0) & (pl.program_id(1) == 0))
        def _():
            for r in (dwq_ref, dwk_ref, dwv_ref, dgq_ref, dgkv_ref):
                r[...] = jnp.zeros_like(r)

        c, s1, s2 = c_ref[...], s1_ref[...], s2_ref[...]
        dkr = jnp.zeros((rows, BLOCK), F32)
        for h in range(MLA_HEADS):
            ls = slice(h * BLOCK, (h + 1) * BLOCK)
            dqh[:, ls] = _rope_t(dqf_ref[0, :, ls].astype(F32), c, s1, s2).astype(dqh.dtype)
            dkr = dkr + dkf_ref[0, :, ls].astype(F32)
        dkr_ref[0] = _rope_t(dkr, c, s1, s2).astype(dkr_ref.dtype)

        def norm_bwd(x, g, dy, dg_ref):
            xr, r = _rms_rows(x, None)
            u = dy * g
            dg_ref[...] += jnp.sum(dy * xr, axis=0, keepdims=True)
            return r * (u - xr * jnp.mean(u * xr, axis=-1, keepdims=True))

        xq, _ = _rms_rows(cq_ref[0], None)
        cq_n = xq * gq_ref[...]
        dwq_ref[...] += _dot_tn(cq_n, dqh[...])
        dcq_ref[0] = norm_bwd(cq_ref[0], gq_ref[...], _dot_nt(dqh[...], wq_ref[...]), dgq_ref).astype(dcq_ref.dtype)
        xk, _ = _rms_rows(ckv_ref[0], None)
        ckv_n = xk * gkv_ref[...]
        dkf_, dv_ = dkf_ref[0], dv_ref[0]
        dwk_ref[...] += _dot_tn(ckv_n, dkf_)
        dwv_ref[...] += _dot_tn(ckv_n, dv_)
        dckv_n = _dot_nt(dkf_, wk_ref[...]) + _dot_nt(dv_, wv_ref[...])
        dckv_ref[0] = norm_bwd(ckv_ref[0], gkv_ref[...], dckv_n, dgkv_ref).astype(dckv_ref.dtype)

    def col(first, width):
        return pl.BlockSpec((1, rows, width), lambda bi, n: (bi, n, first // width))

    def whole(a):
        return pl.BlockSpec(a.shape, lambda bi, n: (0,) * len(a.shape))

    tab = pl.BlockSpec((rows, BLOCK), lambda bi, n: (n, 0))
    acc_shapes = [jax.ShapeDtypeStruct(a.shape, F32) for a in (wq, wk, wv, gq, gkv)]
    return pl.pallas_call(
        body, grid=(b, lp // rows),
        in_specs=[col(EV_CQ, MLA_Q_LORA), col(EV_CKV, MLA_KV_LORA), whole(gq), whole(gkv), whole(wq), whole(wk), whole(wv), tab, tab, tab,
                  col(0, hw), col(0, hw), col(0, MLA_HEADS * HEAD)],
        out_specs=[col(0, MLA_Q_LORA), col(0, MLA_KV_LORA), col(0, BLOCK)] + [whole(a) for a in acc_shapes],
        out_shape=[jax.ShapeDtypeStruct((b, lp, MLA_Q_LORA), BF16), jax.ShapeDtypeStruct((b, lp, MLA_KV_LORA), BF16),
                   jax.ShapeDtypeStruct((b, lp, BLOCK), BF16)] + acc_shapes,
        scratch_shapes=[pltpu.VMEM((rows, hw), BF16)],
        compiler_params=_params(("arbitrary", "arbitrary")), name=name,
    )(proj3, proj3, gq, gkv, wq, wk, wv, *tabs, dqf, dkf, dv)


def _mla_fwd(qf, kf, v, name):
    b, lp, _ = qf.shape
    nb = lp // BLOCK
    npair = MLA_HEADS // 2
    scale = (MLA_NOPE + MLA_ROPE) ** -0.5
    starts = _chunk_starts(lp)

    def body(q_ref, k_ref, v_ref, o_ref, lse_ref, vt_ref):
        for c, s0 in enumerate(starts):
            vt_ref[c] = _split_rows_t(v_ref[0, s0:s0 + KEYS, :].astype(F32))

        def qtile(r0, bq, n):
            qs = [q_ref[0, pl.ds(r0, bq), h * BLOCK:(h + 1) * BLOCK] for h in range(2)]
            t_idx = r0 + _iota2((KEYS, bq), 1)

            def kchunk(c, carry, masked):
                stats, acc = carry[:4], carry[4]
                s0, valid = _key_chunk(c, lp, t_idx, False, 0)
                valid = valid if masked else None
                ps, new, alphas = [], [], []
                for h in range(2):
                    m, l = stats[2 * h], stats[2 * h + 1]
                    s = _where(valid, _dot_nt(k_ref[0, pl.ds(s0, KEYS), h * BLOCK:(h + 1) * BLOCK], qs[h]) * scale, NEG)
                    m_new = jnp.maximum(m, jnp.max(s, axis=0, keepdims=True))
                    p = _where(valid, jnp.exp(s - m_new), 0.0)
                    alpha = jnp.exp(m - m_new)
                    new += [m_new, alpha * l + jnp.sum(p, axis=0, keepdims=True)]
                    alphas.append(alpha)
                    ps.append(p.astype(BF16))
                pv = jnp.dot(vt_ref[c], jnp.concatenate(ps, axis=0), preferred_element_type=F32)
                return (*new, _rows_of_pair(alphas[0], alphas[1]) * acc + pv)

            neg, zero = jnp.full((1, bq), NEG, F32), jnp.zeros((1, bq), F32)
            m_a, l_a, m_b, l_b, acc = _walk_chunks(r0, n, kchunk, (neg, zero, neg, zero, jnp.zeros((BLOCK, bq), F32)))
            safe = [jnp.where(l > 0.0, l, 1.0) for l in (l_a, l_b)]
            o_ref[0, pl.ds(r0, bq), :] = (acc / _rows_of_pair(safe[0], safe[1])).T
            lse = [jnp.where(l > 0.0, m + jnp.log(sf), 0.0) for m, l, sf in ((m_a, l_a, safe[0]), (m_b, l_b, safe[1]))]
            _put_rows(lse_ref, r0, bq, lse[0], lse[1])

        _for_query_tiles(nb, qtile)

    wide = pl.BlockSpec((1, lp, 2 * BLOCK), lambda bi, hp: (bi, 0, hp))
    thin = pl.BlockSpec((1, lp, 2 * HEAD), lambda bi, hp: (bi, 0, hp))
    return pl.pallas_call(
        body, grid=(b, npair), in_specs=[wide, wide, thin], out_specs=[thin, _pair_stat_spec(nb)],
        out_shape=[jax.ShapeDtypeStruct((b, lp, MLA_HEADS * HEAD), F32), jax.ShapeDtypeStruct((b, npair, nb, 8, BLOCK), F32)],
        scratch_shapes=[pltpu.VMEM((len(starts), BLOCK, 2 * KEYS), BF16)],
        compiler_params=_params(("parallel", "parallel")), name=name,
    )(qf, kf, v)


def _mla_bwd(qf, kf, v, o, lse, do, name):
    b, lp, _ = qf.shape
    nb = lp // BLOCK
    npair = MLA_HEADS // 2
    scale = (MLA_NOPE + MLA_ROPE) ** -0.5

    starts = _chunk_starts(lp)

    def body(q_ref, k_ref, v_ref, o_ref, lse_ref, do_ref, dq_ref, dk_ref, dv_ref, kt_ref):
        lo = _lo_lanes()
        dk_ref[...] = jnp.zeros_like(dk_ref)
        dv_ref[...] = jnp.zeros_like(dv_ref)
        for c, s0 in enumerate(starts):
            for h in range(2):
                kt_ref[c, h] = k_ref[0, s0:s0 + KEYS, h * BLOCK:(h + 1) * BLOCK].astype(F32).T.astype(BF16)

        def qtile(r0, bq, n):
            rows = pl.ds(r0, bq)
            qs = [q_ref[0, rows, h * BLOCK:(h + 1) * BLOCK] for h in range(2)]
            do_i = do_ref[0, rows, :]
            dos = _halves(do_i.astype(BF16), lo)
            do_st = jnp.concatenate(dos, axis=0)
            both = (do_i * o_ref[0, rows, :]).T
            dsum = (jnp.sum(both[:HEAD], axis=0, keepdims=True), jnp.sum(both[HEAD:], axis=0, keepdims=True))
            lses = _get_rows(lse_ref, r0, bq)
            t_idx = r0 + _iota2((KEYS, bq), 1)

            def kchunk(c, dqts, masked):
                s0, valid = _key_chunk(c, lp, t_idx, False, 0)
                valid = valid if masked else None
                keys = pl.ds(s0, KEYS)
                v_c = v_ref[0, keys, :]
                ps, out = [], []
                for h in range(2):
                    lanes = slice(h * BLOCK, (h + 1) * BLOCK)
                    s = _dot_nt(k_ref[0, keys, lanes], qs[h]) * scale
                    p = _where(valid, jnp.exp(s - lses[h]), 0.0)
                    ds = (p * (_dot_nt(v_c, dos[h]) - dsum[h]) * scale).astype(BF16)
                    dk_ref[0, keys, lanes] += jnp.dot(ds, qs[h], preferred_element_type=F32)
                    out.append(dqts[h] + jnp.dot(kt_ref[c, h], ds, preferred_element_type=F32))
                    ps.append(p.astype(BF16))
                dv_ref[0, keys, :] += jnp.dot(jnp.concatenate(ps, axis=1), do_st, preferred_element_type=F32)
                return tuple(out)

            zero = jnp.zeros((BLOCK, bq), F32)
            dq_a, dq_b = _walk_chunks(r0, n, kchunk, (zero, zero))
            dq_ref[0, rows, 0:BLOCK] = dq_a.T
            dq_ref[0, rows, BLOCK:2 * BLOCK] = dq_b.T

        _for_query_tiles(nb, qtile)

    wide = pl.BlockSpec((1, lp, 2 * BLOCK), lambda bi, hp: (bi, 0, hp))
    thin = pl.BlockSpec((1, lp, 2 * HEAD), lambda bi, hp: (bi, 0, hp))
    return pl.pallas_call(
        body, grid=(b, npair), in_specs=[wide, wide, thin, thin, _pair_stat_spec(nb), thin], out_specs=[wide, wide, thin],
        out_shape=[jax.ShapeDtypeStruct(qf.shape, F32), jax.ShapeDtypeStruct(qf.shape, F32), jax.ShapeDtypeStruct(v.shape, F32)],
        scratch_shapes=[pltpu.VMEM((len(starts), 2, BLOCK, KEYS), BF16)],
        compiler_params=_params(("parallel", "parallel")), name=name,
    )(qf, kf, v, o, lse, do)


SWA_KEYS = 2 * BLOCK + N_META


def _swa_keys(k_ref, v_ref, n, kv):
    prev = jnp.maximum(n - 1, 0)
    rows = lambda blk: pl.ds(pl.multiple_of(blk * BLOCK, BLOCK), BLOCK)
    mine = (_iota2((1, BLOCK), 1) >= HEAD).astype(jnp.int32) == kv

    def both_halves(ref):
        x = jnp.concatenate([ref[0, rows(prev), :], ref[0, rows(n), :], ref[0, N_PAD:BLOCK, :]], axis=0)
        return jnp.where(mine, x, pltpu.roll(x, HEAD, 1))

    slot = _iota2((SWA_KEYS, BLOCK), 0)
    s_idx = jnp.where(slot < 2 * BLOCK, (n - 1) * BLOCK + slot, slot - 2 * BLOCK + N_PAD)
    dist = n * BLOCK + _iota2((SWA_KEYS, BLOCK), 1) - s_idx
    band = (slot < 2 * BLOCK) & (dist >= 0) & (dist < SWA_WINDOW) & (s_idx >= BLOCK)
    meta = (slot >= 2 * BLOCK) & (dist >= 0)
    return both_halves(k_ref), both_halves(v_ref), band | meta, dist.astype(F32), prev


def _pad_keys(x):
    return jnp.concatenate([x, jnp.zeros((3 * BLOCK - SWA_KEYS, x.shape[1]), x.dtype)], axis=0)


def _swa_probs(q_h, kdup, valid, dist, head, sink_ref):
    slope = jnp.exp(jnp.full((1, 1), -8.0 * math.log(2.0) / SWA_HEADS, F32) * (head + 1).astype(F32))
    s = jnp.where(valid, _dot_nt(kdup, q_h) - slope * dist, NEG)
    sink = sink_ref[pl.ds(head, 1), 0:1]
    m = jnp.maximum(jnp.max(s, axis=0, keepdims=True), sink)
    e = jnp.where(valid, jnp.exp(s - m), 0.0)
    es = jnp.exp(sink - m)
    inv = 1.0 / (jnp.sum(e, axis=0, keepdims=True) + es)
    return e * inv, es * inv


SWA_PAIRS = SWA_HEADS // SWA_KV_HEADS // 2
SWA_GROUP = SWA_PAIRS * 2 * HEAD


def _swa_specs(b, lp):
    nb = lp // BLOCK
    qcol = lambda first: pl.BlockSpec((1, BLOCK, SWA_GROUP), lambda bi, kv, n: (bi, n, first // SWA_GROUP + kv))
    kcol = lambda first: pl.BlockSpec((1, lp, BLOCK), lambda bi, kv, n: (bi, 0, first // BLOCK))
    sink = pl.BlockSpec((SWA_HEADS, BLOCK), lambda bi, kv, n: (0, 0))
    return (b, SWA_KV_HEADS, nb), qcol, kcol, sink


def _swa_fwd(proj3, sinks, name):
    b, lp, _ = proj3.shape
    grid, qcol, kcol, sink = _swa_specs(b, lp)

    def body(q_ref, k_ref, v_ref, sink_ref, o_ref):
        kv, n = pl.program_id(1), pl.program_id(2)
        lo = _lo_lanes()
        kdup, vdup, valid, dist, _ = _swa_keys(k_ref, v_ref, n, kv)
        kdup = kdup.astype(BF16)
        vt = _split_rows_t(_pad_keys(vdup))
        for p in range(SWA_PAIRS):
            lanes = slice(p * BLOCK, (p + 1) * BLOCK)
            qs = _halves((q_ref[0, :, lanes] * HEAD_SCALE).astype(BF16), lo)
            probs = [_swa_probs(qs[hh], kdup, valid, dist, (kv * SWA_PAIRS + p) * 2 + hh, sink_ref)[0].astype(BF16) for hh in range(2)]
            o_ref[0, :, lanes] = jnp.dot(vt, jnp.concatenate([_pad_keys(pr) for pr in probs], axis=0), preferred_element_type=F32).T

    return pl.pallas_call(
        body, grid=grid, in_specs=[qcol(OD_Q), kcol(OD_K), kcol(OD_V), sink], out_specs=qcol(0),
        out_shape=jax.ShapeDtypeStruct((b, lp, SWA_HEADS * HEAD), F32),
        compiler_params=_params(("parallel", "parallel", "parallel")), name=name,
    )(proj3, proj3, proj3, sinks)


def _swa_bwd(proj3, sinks, do, name):
    b, lp, _ = proj3.shape
    nb = lp // BLOCK
    grid, qcol, kcol, sink = _swa_specs(b, lp)

    def body(q_ref, k_ref, v_ref, sink_ref, do_ref, dq_ref, dk_ref, dv_ref, dsink_ref, dk_acc, dv_acc):
        kv, n = pl.program_id(1), pl.program_id(2)

        @pl.when((n == 0) & (pl.program_id(0) == 0) & (kv == 0))
        def _():
            dsink_ref[...] = jnp.zeros_like(dsink_ref)

        @pl.when(n == 0)
        def _():
            dk_acc[...] = jnp.zeros_like(dk_acc)
            dv_acc[...] = jnp.zeros_like(dv_acc)

        lo = _lo_lanes()
        kdup, vdup, valid, dist, prev = _swa_keys(k_ref, v_ref, n, kv)
        kt = _split_rows_t(_pad_keys(kdup))
        kdup, vdup = kdup.astype(BF16), vdup.astype(BF16)
        dkc = jnp.zeros((SWA_KEYS, BLOCK), F32)
        dvc = jnp.zeros((SWA_KEYS, BLOCK), F32)
        for p in range(SWA_PAIRS):
            lanes = slice(p * BLOCK, (p + 1) * BLOCK)
            qs = _halves((q_ref[0, :, lanes] * HEAD_SCALE).astype(BF16), lo)
            dos = _halves(do_ref[0, :, lanes].astype(BF16), lo)
            dss, prs = [], []
            for hh in range(2):
                head = (kv * SWA_PAIRS + p) * 2 + hh
                pr, ps = _swa_probs(qs[hh], kdup, valid, dist, head, sink_ref)
                dp = _dot_nt(vdup, dos[hh])
                dsum = jnp.sum(pr * dp, axis=0, keepdims=True)
                dsink_ref[pl.ds(head, 1), :] += jnp.broadcast_to(-jnp.sum(ps * dsum, axis=1, keepdims=True), (1, BLOCK))
                dss.append((pr * (dp - dsum)).astype(BF16))
                prs.append(pr.astype(BF16))
            dq_ref[0, :, lanes] = jnp.dot(kt, jnp.concatenate([_pad_keys(d) for d in dss], axis=0), preferred_element_type=F32).T * HEAD_SCALE
            dkc = dkc + jnp.dot(jnp.concatenate(dss, axis=1), jnp.concatenate(qs, axis=0), preferred_element_type=F32)
            dvc = dvc + jnp.dot(jnp.concatenate(prs, axis=1), jnp.concatenate(dos, axis=0), preferred_element_type=F32)
        rows = lambda blk: pl.ds(pl.multiple_of(blk * BLOCK, BLOCK), BLOCK)
        for r, part in ((rows(prev), slice(0, BLOCK)), (rows(n), slice(BLOCK, 2 * BLOCK)), (slice(N_PAD, BLOCK), slice(2 * BLOCK, SWA_KEYS))):
            dk_acc[r, :] += dkc[part]
            dv_acc[r, :] += dvc[part]

        for acc, ref in ((dk_acc, dk_ref), (dv_acc, dv_ref)):
            @pl.when((n == nb - 1) & (kv == 0))
            def _():
                x = acc[...]
                ref[0] = x + pltpu.roll(x, HEAD, 1)

            @pl.when((n == nb - 1) & (kv == 1))
            def _():
                x = acc[...]
                ref[0] = jnp.where(lo, ref[0], x + pltpu.roll(x, HEAD, 1))

    kvout = pl.BlockSpec((1, lp, BLOCK), lambda bi, kv, n: (bi, 0, 0))
    kvshape = jax.ShapeDtypeStruct((b, lp, BLOCK), F32)
    return pl.pallas_call(
        body, grid=grid, in_specs=[qcol(OD_Q), kcol(OD_K), kcol(OD_V), sink, qcol(0)], out_specs=[qcol(0), kvout, kvout, sink],
        out_shape=[jax.ShapeDtypeStruct((b, lp, SWA_HEADS * HEAD), F32), kvshape, kvshape, jax.ShapeDtypeStruct((SWA_HEADS, BLOCK), F32)],
        scratch_shapes=[pltpu.VMEM((lp, BLOCK), F32), pltpu.VMEM((lp, BLOCK), F32)],
        compiler_params=_params(("arbitrary", "arbitrary", "arbitrary")), name=name,
    )(proj3, proj3, proj3, sinks, do)


def _kernel_weights(ev_w_in, ev_w_uq, ev_w_ukv, od_w_in):
    zeros = lambda r, c: jnp.zeros((r, c), ev_w_in.dtype)
    q_sb, k_sb, v_sb, g_sb, c_q, c_kv, k_r, g_mla = jnp.split(ev_w_in, [512, 1024, 1536, 2048, 2304, 2432, 2464], axis=1)
    w0 = jnp.concatenate([g_sb, g_mla, q_sb, k_sb, v_sb, c_q, c_kv, zeros(D_MODEL, MLA_NOPE), k_r, zeros(D_MODEL, 32)], axis=1)
    uq = ev_w_uq.reshape(MLA_Q_LORA, MLA_HEADS, MLA_NOPE + MLA_ROPE)
    wq = jnp.pad(uq, ((0, 0), (0, 0), (0, BLOCK - MLA_NOPE - MLA_ROPE))).reshape(MLA_Q_LORA, MLA_HEADS * BLOCK)
    ukv = ev_w_ukv.reshape(MLA_KV_LORA, MLA_HEADS, BLOCK)
    wk = jnp.pad(ukv[:, :, :MLA_NOPE], ((0, 0), (0, 0), (0, BLOCK - MLA_NOPE))).reshape(MLA_KV_LORA, MLA_HEADS * BLOCK)
    wv = ukv[:, :, MLA_NOPE:].reshape(MLA_KV_LORA, MLA_HEADS * HEAD)
    q, k, v, g = jnp.split(od_w_in, [1024, 1152, 1280], axis=1)
    w1 = jnp.concatenate([g, q, k, v], axis=1)
    return w0, wq, wk, wv, w1


def _original_grads(dw0, dwq, dwk, dwv, dw1):
    sl = lambda a, first, n: a[:, first:first + n]
    d_ev_w_in = jnp.concatenate([sl(dw0, EV_Q, 512), sl(dw0, EV_K, 512), sl(dw0, EV_V, 512), sl(dw0, EV_G, 512), sl(dw0, EV_CQ, 256),
                                 sl(dw0, EV_CKV, 128), sl(dw0, EV_KR + MLA_NOPE, MLA_ROPE), sl(dw0, EV_G + 512, 512)], axis=1)
    d_uq = dwq.reshape(MLA_Q_LORA, MLA_HEADS, BLOCK)[:, :, :MLA_NOPE + MLA_ROPE].reshape(MLA_Q_LORA, -1)
    d_ukv = jnp.concatenate([dwk.reshape(MLA_KV_LORA, MLA_HEADS, BLOCK)[:, :, :MLA_NOPE], dwv.reshape(MLA_KV_LORA, MLA_HEADS, HEAD)],
                            axis=2).reshape(MLA_KV_LORA, -1)
    d_od_w_in = jnp.concatenate([sl(dw1, OD_Q, 1024), sl(dw1, OD_K, 128), sl(dw1, OD_V, 128), sl(dw1, OD_G, 1024)], axis=1)
    return d_ev_w_in, d_uq, d_ukv, d_od_w_in


def _meta_rows_sum(dh0_3):
    b, _, d = dh0_3.shape

    def body(x_ref, o_ref):
        acc = x_ref[0, N_PAD:BLOCK, :]
        for i in range(1, b):
            acc = acc + x_ref[i, N_PAD:BLOCK, :]
        o_ref[...] = acc

    return pl.pallas_call(
        body, grid=(1,), in_specs=[pl.BlockSpec((b, BLOCK, d), lambda i: (0, 0, 0))], out_specs=pl.BlockSpec((N_META, d), lambda i: (0, 0)),
        out_shape=jax.ShapeDtypeStruct((N_META, d), F32), compiler_params=_params(("arbitrary",)), name="meta_rows_sum",
    )(dh0_3)


def _local_step(x, meta, norm_g, final_g, gq, gkv, sinks, target, ev_w_in, ev_w_uq, ev_w_ukv, wo0, od_w_in, wo1):
    b, seq, d = x.shape
    lp = seq + BLOCK
    t = b * lp
    w0, wq, wk, wv, w1 = _kernel_weights(ev_w_in, ev_w_uq, ev_w_ukv, od_w_in)
    h0 = jnp.concatenate([jnp.zeros((b, N_PAD, d), F32), jnp.broadcast_to(meta[None], (b, N_META, d)), x], axis=1).reshape(t, d)
    tabs = _rope_tables(lp)
    g0, g1 = norm_g[0:1], norm_g[1:2]

    hn0 = _rms_fwd(h0, g0, "norm0")
    proj0 = _mm(hn0, w0, "inproj0")
    p0 = proj0.reshape(b, lp, EV_N)
    o_sb, sb_tot = _sb_fwd(p0, "sb_fwd")
    qf, kf, v = _mla_prep_fwd(p0, gq, gkv, wq, wk, wv, tabs, "mla_prep_fwd")
    o_mla, lse = _mla_fwd(qf, kf, v, "mla_fwd")
    o0 = [o_sb.reshape(t, -1), o_mla.reshape(t, -1)]
    ao0 = _gate_fwd(o0, proj0, "gate0")
    h1 = _mm(ao0, wo0, "outproj0", res=h0)

    hn1 = _rms_fwd(h1, g1, "norm1")
    proj1 = _mm(hn1, w1, "inproj1")
    p1 = proj1.reshape(b, lp, OD_N)
    sinks_b = jnp.broadcast_to(sinks.reshape(SWA_HEADS, 1), (SWA_HEADS, BLOCK))
    o1 = _swa_fwd(p1, sinks_b, "swa_fwd").reshape(t, -1)
    ao1 = _gate_fwd([o1], proj1, "gate1")
    h2 = _mm(ao1, wo1, "outproj1", res=h1)

    dh2, d_final_g, loss = _loss_head(h2, final_g.reshape(1, d), target, b, lp)

    d_wo1 = _mm_tn(ao1, dh2, "d_wo1")
    dao1 = _mm_nt(dh2, wo1, "d_ao1")
    (do1,), dg1 = _gate_bwd(dao1, [o1], proj1, "gate1_bwd")
    dq1, dk4, dv4, d_sinks = _swa_bwd(p1, sinks_b, do1.reshape(b, lp, -1), "swa_bwd")
    unheads = lambda a: a.reshape(t, SWA_KV_HEADS * HEAD).astype(BF16)
    dproj1 = jnp.concatenate([dg1, dq1.reshape(t, -1).astype(BF16), unheads(dk4), unheads(dv4)], axis=1)
    d_w1 = _mm_tn(hn1, dproj1, "d_w1")
    dhn1 = _mm_nt(dproj1, w1, "d_hn1")
    dh1, d_g1 = _rms_bwd(h1, g1, dhn1, dh2, "norm1_bwd")

    d_wo0 = _mm_tn(ao0, dh1, "d_wo0")
    dao0 = _mm_nt(dh1, wo0, "d_ao0")
    (do_sb, do_mla), dg0 = _gate_bwd(dao0, o0, proj0, "gate0_bwd")
    dq_sb, dk_sb, dv_sb = _sb_bwd(p0, sb_tot, do_sb.reshape(b, lp, -1), "sb_bwd")
    dqf, dkf, dv = _mla_bwd(qf, kf, v, o_mla, lse, do_mla.reshape(b, lp, -1), "mla_bwd")
    dcq, dckv, dkr, d_wq, d_wk, d_wv, d_gq, d_gkv = _mla_prep_bwd(p0, gq, gkv, wq, wk, wv, tabs, dqf, dkf, dv, "mla_prep_bwd")
    flat = lambda a: a.reshape(t, -1).astype(BF16)
    dproj0 = jnp.concatenate([dg0, flat(dq_sb), flat(dk_sb), flat(dv_sb), flat(dcq), flat(dckv), flat(dkr)], axis=1)
    d_w0 = _mm_tn(hn0, dproj0, "d_w0")
    dhn0 = _mm_nt(dproj0, w0, "d_hn0")
    dh0, d_g0 = _rms_bwd(h0, g0, dhn0, dh1, "norm0_bwd")
    dh0 = dh0.reshape(b, lp, d)

    d_ev_w_in, d_uq, d_ukv, d_od_w_in = _original_grads(d_w0, d_wq, d_wk, d_wv, d_w1)
    grads = dict(meta=_meta_rows_sum(dh0), norm_g=jnp.concatenate([d_g0, d_g1], axis=0), final_g=d_final_g.reshape(d),
                 ev_w_in=d_ev_w_in, ev_q_norm_g=d_gq, ev_kv_norm_g=d_gkv, ev_w_uq=d_uq, ev_w_ukv=d_ukv, ev_w_out=d_wo0,
                 od_w_in=d_od_w_in, od_sinks=d_sinks[:, 0].reshape(1, SWA_HEADS), od_w_out=d_wo1)
    return loss, dh0[:, BLOCK:], grads


MESH = pl.DeviceIdType.MESH
ANY = pl.BlockSpec(memory_space=pl.ANY)


def _place():
    return lax.axis_index("x"), lax.axis_index("y"), lax.axis_index("c")


def _other_chips(x, y):
    return [(1 - x, y), (x, 1 - y), (1 - x, 1 - y)]


def _with_own_slot(slots, own):
    me = 2 * lax.axis_index("x") + lax.axis_index("y")
    return lax.dynamic_update_slice(slots, own[None], (me,) + (0,) * own.ndim)


def _gather_weights(packs, meta, name):
    n = len(packs)

    def body(*refs):
        ins, m_ref, outs, mo_ref = refs[:n], refs[n], refs[n + 1:2 * n + 1], refs[2 * n + 1]
        send_sems, recv_sems = refs[2 * n + 2:]
        x, y, c = _place()
        me, sib = 2 * x + y, (x, y, 1 - c)
        chips = _other_chips(x, y)

        def copy(k, src, dst, to):
            return pltpu.make_async_remote_copy(src_ref=src, dst_ref=dst, send_sem=send_sems.at[k], recv_sem=recv_sems.at[k], device_id=to,
                                                device_id_type=MESH)

        def half(i, chip, h):
            rows = packs[i].shape[0] // 2
            return outs[i].at[chip, pl.ds(h * rows, rows), :]

        def mine(i):
            rows = packs[i].shape[0] // 2
            return ins[i].at[pl.ds(c * rows, rows), :]

        sent = [copy(6 * i + k, mine(i), half(i, me, c), (px, py, c)) for i in range(n) for k, (px, py) in enumerate(chips)]
        sent += [copy(6 * n + k, m_ref, mo_ref.at[me], (px, py, c)) for k, (px, py) in enumerate(chips)]
        for cp in sent:
            cp.start()
        for i in range(n):
            for k, (px, py) in enumerate(chips):
                landed = half(i, 2 * px + py, c)
                copy(6 * i + k, mine(i), landed, (px, py, c)).wait_recv()
                fwd = copy(6 * i + 3 + k, landed, landed, sib)
                fwd.start()
                sent.append(fwd)
        for k, (px, py) in enumerate(chips):
            for i in range(n):
                other = half(i, 2 * px + py, 1 - c)
                copy(6 * i + 3 + k, other, other, sib).wait_recv()
            copy(6 * n + k, m_ref, mo_ref.at[2 * px + py], (px, py, c)).wait_recv()
        for cp in sent:
            cp.wait_send()

    nsem = 6 * n + 3
    res = pl.pallas_call(
        body, in_specs=[ANY] * (n + 1), out_specs=[ANY] * (n + 1),
        out_shape=[jax.ShapeDtypeStruct((N_CHIPS,) + a.shape, a.dtype) for a in list(packs) + [meta]],
        scratch_shapes=[pltpu.SemaphoreType.DMA((nsem,)), pltpu.SemaphoreType.DMA((nsem,))],
        name=name,
    )(*packs, meta)
    return [_with_own_slot(r, a) for r, a in zip(res[:n], packs)], _with_own_slot(res[n], meta)


def _grads_to_sibling(gs, name):
    n = len(gs)

    def body(*refs):
        ins, outs, send_sems, recv_sems = refs[:n], refs[n:2 * n], refs[2 * n], refs[2 * n + 1]
        x, y, c = _place()
        cps = []
        for i in range(n):
            rows = gs[i].shape[1] // 2
            cps.append(pltpu.make_async_remote_copy(src_ref=ins[i].at[:, pl.ds((1 - c) * rows, rows), :], dst_ref=outs[i],
                                                    send_sem=send_sems.at[i], recv_sem=recv_sems.at[i], device_id=(x, y, 1 - c),
                                                    device_id_type=MESH))
        for cp in cps:
            cp.start()
        for cp in cps:
            cp.wait()

    return pl.pallas_call(
        body, in_specs=[ANY] * n, out_specs=[ANY] * n,
        out_shape=[jax.ShapeDtypeStruct((g.shape[0], g.shape[1] // 2, g.shape[2]), g.dtype) for g in gs],
        scratch_shapes=[pltpu.SemaphoreType.DMA((n,)), pltpu.SemaphoreType.DMA((n,))],
        name=name,
    )(*gs)


def _share_halves(rs, name):
    n = len(rs)

    def body(*refs):
        ins, outs, send_sems, recv_sems = refs[:n], refs[n:2 * n], refs[2 * n], refs[2 * n + 1]
        x, y, c = _place()
        cps = [pltpu.make_async_remote_copy(src_ref=ins[i], dst_ref=outs[i], send_sem=send_sems.at[i], recv_sem=recv_sems.at[i],
                                            device_id=(x, y, 1 - c), device_id_type=MESH) for i in range(n)]
        for cp in cps:
            cp.start()
        for cp in cps:
            cp.wait()

    theirs = pl.pallas_call(
        body, in_specs=[ANY] * n, out_specs=[ANY] * n, out_shape=[jax.ShapeDtypeStruct(r.shape, r.dtype) for r in rs],
        scratch_shapes=[pltpu.SemaphoreType.DMA((n,)), pltpu.SemaphoreType.DMA((n,))],
        name=name,
    )(*rs)
    first = lax.axis_index("c") == 0
    return [jnp.where(first, jnp.concatenate([r, t], axis=0), jnp.concatenate([t, r], axis=0)) for r, t in zip(rs, theirs)]


def _chip_scatter(ss, name):
    n = len(ss)

    def body(*refs):
        ins, outs, send_sems, recv_sems = refs[:n], refs[n:2 * n], refs[2 * n], refs[2 * n + 1]
        x, y, c = _place()
        me = 2 * x + y
        chips = _other_chips(x, y)
        for i in range(n):
            for k, (px, py) in enumerate(chips):
                pltpu.make_async_remote_copy(src_ref=ins[i].at[2 * px + py], dst_ref=outs[i].at[me], send_sem=send_sems.at[3 * i + k],
                                             recv_sem=recv_sems.at[3 * i + k], device_id=(px, py, c), device_id_type=MESH).start()
        for i in range(n):
            for k, (px, py) in enumerate(chips):
                cp = pltpu.make_async_remote_copy(src_ref=ins[i].at[2 * px + py], dst_ref=outs[i].at[2 * px + py],
                                                  send_sem=send_sems.at[3 * i + k], recv_sem=recv_sems.at[3 * i + k],
                                                  device_id=(px, py, c), device_id_type=MESH)
                cp.wait_recv()
                cp.wait_send()

    parts = pl.pallas_call(
        body, in_specs=[ANY] * n, out_specs=[ANY] * n, out_shape=[jax.ShapeDtypeStruct(s.shape, s.dtype) for s in ss],
        scratch_shapes=[pltpu.SemaphoreType.DMA((3 * n,)), pltpu.SemaphoreType.DMA((3 * n,))],
        name=name,
    )(*ss)
    me = 2 * lax.axis_index("x") + lax.axis_index("y")
    return [_with_own_slot(p, lax.dynamic_index_in_dim(s, me, axis=0, keepdims=False)) for p, s in zip(parts, ss)]


HBM_SPACE = pltpu.MemorySpace.HBM


def _on_sequencer(name, collective_id, n_sems, body):
    @pl.kernel(mesh=plsc.ScalarSubcoreMesh(axis_name="sequencer", num_cores=1), name=name,
               scratch_types=(pltpu.SemaphoreType.DMA((n_sems,)), pltpu.SemaphoreType.DMA((n_sems,))),
               compiler_params=pltpu.CompilerParams(collective_id=collective_id))
    def launch(send_sems, recv_sems):
        body(send_sems, recv_sems)

    launch()


def _handshake(peers):
    barrier = pltpu.get_barrier_semaphore()
    for peer in peers:
        pl.semaphore_signal(barrier, inc=1, device_id=peer, device_id_type=MESH)
    pl.semaphore_wait(barrier, len(peers))


def _gather_on_sequencer(packs, name):
    n = len(packs)
    ins = [jax.new_ref(p, memory_space=HBM_SPACE) for p in packs]
    outs = [jax.empty_ref(jax.ShapeDtypeStruct((N_CHIPS,) + p.shape, p.dtype), memory_space=HBM_SPACE) for p in packs]

    def body(send_sems, recv_sems):
        x, y, c = _place()
        me, sib = 2 * x + y, (x, y, 1 - c)
        chips = _other_chips(x, y)
        _handshake([(px, py, c) for px, py in chips] + [sib])

        def copy(k, src, dst, to):
            return pltpu.make_async_remote_copy(src_ref=src, dst_ref=dst, send_sem=send_sems.at[k], recv_sem=recv_sems.at[k], device_id=to,
                                                device_id_type=MESH)

        def half(i, chip, h):
            rows = packs[i].shape[0] // 2
            return outs[i].at[chip, pl.ds(h * rows, rows), :]

        def mine(i):
            rows = packs[i].shape[0] // 2
            return ins[i].at[pl.ds(c * rows, rows), :]

        sent = [copy(6 * i + k, mine(i), half(i, me, c), (px, py, c)) for i in range(n) for k, (px, py) in enumerate(chips)]
        for cp in sent:
            cp.start()
        for i in range(n):
            for k, (px, py) in enumerate(chips):
                landed = half(i, 2 * px + py, c)
                copy(6 * i + k, mine(i), landed, (px, py, c)).wait_recv()
                fwd = copy(6 * i + 3 + k, landed, landed, sib)
                fwd.start()
                sent.append(fwd)
        for i in range(n):
            for k, (px, py) in enumerate(chips):
                other = half(i, 2 * px + py, 1 - c)
                copy(6 * i + 3 + k, other, other, sib).wait_recv()
        for cp in sent:
            cp.wait_send()

    _on_sequencer(name, 1, 6 * n, body)
    return [_with_own_slot(o[...], p) for o, p in zip(outs, packs)]


def _grads_to_sibling_on_sequencer(gs, name, collective_id):
    n = len(gs)
    ins = [jax.new_ref(g, memory_space=HBM_SPACE) for g in gs]
    outs = [jax.empty_ref(jax.ShapeDtypeStruct((g.shape[0], g.shape[1] // 2, g.shape[2]), g.dtype), memory_space=HBM_SPACE) for g in gs]

    def body(send_sems, recv_sems):
        x, y, c = _place()
        _handshake([(x, y, 1 - c)])
        cps = []
        for i in range(n):
            rows = gs[i].shape[1] // 2
            cps.append(pltpu.make_async_remote_copy(src_ref=ins[i].at[:, pl.ds((1 - c) * rows, rows), :], dst_ref=outs[i],
                                                    send_sem=send_sems.at[i], recv_sem=recv_sems.at[i], device_id=(x, y, 1 - c),
                                                    device_id_type=MESH))
        for cp in cps:
            cp.start()
        for cp in cps:
            cp.wait()

    _on_sequencer(name, collective_id, n, body)
    return [o[...] for o in outs]


def _chip_scatter_on_sequencer(ss, name, collective_id):
    n = len(ss)
    ins = [jax.new_ref(s, memory_space=HBM_SPACE) for s in ss]
    outs = [jax.empty_ref(jax.ShapeDtypeStruct(s.shape, s.dtype), memory_space=HBM_SPACE) for s in ss]

    def body(send_sems, recv_sems):
        x, y, c = _place()
        me = 2 * x + y
        chips = _other_chips(x, y)
        _handshake([(px, py, c) for px, py in chips])
        for i in range(n):
            for k, (px, py) in enumerate(chips):
                pltpu.make_async_remote_copy(src_ref=ins[i].at[2 * px + py], dst_ref=outs[i].at[me], send_sem=send_sems.at[3 * i + k],
                                             recv_sem=recv_sems.at[3 * i + k], device_id=(px, py, c), device_id_type=MESH).start()
        for i in range(n):
            for k, (px, py) in enumerate(chips):
                cp = pltpu.make_async_remote_copy(src_ref=ins[i].at[2 * px + py], dst_ref=outs[i].at[2 * px + py],
                                                  send_sem=send_sems.at[3 * i + k], recv_sem=recv_sems.at[3 * i + k],
                                                  device_id=(px, py, c), device_id_type=MESH)
                cp.wait_recv()
                cp.wait_send()

    _on_sequencer(name, collective_id, 3 * n, body)
    me = 2 * lax.axis_index("x") + lax.axis_index("y")
    return [_with_own_slot(o[...], lax.dynamic_index_in_dim(s, me, axis=0, keepdims=False)) for o, s in zip(outs, ss)]


def _all_reduce_small(v, name):
    shape = v.shape

    def body(v_ref, o_ref, slots, send_sems, recv_sems):
        x, y, c = _place()
        me = 4 * x + 2 * y + c
        slots[me] = v_ref[...]
        for r in range(1, N_DEV):
            peer = (x ^ (r >> 2), y ^ ((r >> 1) & 1), c ^ (r & 1))
            pltpu.make_async_remote_copy(src_ref=v_ref, dst_ref=slots.at[me], send_sem=send_sems.at[r - 1], recv_sem=recv_sems.at[r - 1],
                                         device_id=peer, device_id_type=MESH).start()
        for r in range(1, N_DEV):
            peer = (x ^ (r >> 2), y ^ ((r >> 1) & 1), c ^ (r & 1))
            cp = pltpu.make_async_remote_copy(src_ref=v_ref, dst_ref=slots.at[4 * peer[0] + 2 * peer[1] + peer[2]], send_sem=send_sems.at[r - 1],
                                              recv_sem=recv_sems.at[r - 1], device_id=peer, device_id_type=MESH)
            cp.wait_recv()
            cp.wait_send()
        acc = slots[0]
        for d in range(1, N_DEV):
            acc = acc + slots[d]
        o_ref[...] = acc

    vm = pl.BlockSpec(memory_space=pltpu.VMEM)
    return pl.pallas_call(
        body, in_specs=[vm], out_specs=vm, out_shape=jax.ShapeDtypeStruct(shape, F32),
        scratch_shapes=[pltpu.VMEM((N_DEV,) + shape, F32), pltpu.SemaphoreType.DMA((N_DEV - 1,)), pltpu.SemaphoreType.DMA((N_DEV - 1,))],
        name=name,
    )(v)


def _add_sibling(g, gsib, core, name):
    n, _, cdim = g.shape
    half = gsib.shape[1]
    tr = half // 2

    def body(core_ref, a_ref, b_ref, o_ref):
        o_ref[...] = (a_ref[...] + b_ref[...]).astype(o_ref.dtype)

    blk = pl.BlockSpec((1, tr, cdim), lambda j, i, core_ref: (j, i, 0))
    return pl.pallas_call(
        body,
        grid_spec=pltpu.PrefetchScalarGridSpec(
            num_scalar_prefetch=1, grid=(n, half // tr),
            in_specs=[pl.BlockSpec((1, tr, cdim), lambda j, i, core_ref: (j, core_ref[0] * (half // tr) + i, 0)), blk], out_specs=blk),
        out_shape=jax.ShapeDtypeStruct(gsib.shape, BF16), compiler_params=_params(("parallel", "parallel")), name=name,
    )(core, g, gsib)


def _sum_parts(parts, name):
    n, r, cdim = parts.shape
    tr = r // 2

    def body(p_ref, o_ref):
        acc = p_ref[0].astype(F32)
        for j in range(1, n):
            acc = acc + p_ref[j].astype(F32)
        o_ref[...] = acc

    return pl.pallas_call(
        body, grid=(r // tr,), in_specs=[pl.BlockSpec((n, tr, cdim), lambda i: (0, i, 0))],
        out_specs=pl.BlockSpec((tr, cdim), lambda i: (i, 0)), out_shape=jax.ShapeDtypeStruct((r, cdim), F32),
        compiler_params=_params(("parallel",)), name=name,
    )(parts)


def _adamw(parts, w, m, v, name):
    npart, r, cdim = parts.shape
    tr = r // 4 if r % 32 == 0 else r

    def body(p_ref, w_ref, m_ref, v_ref, g_ref, d_ref, nm_ref, nv_ref):
        g = p_ref[0]
        for j in range(1, npart):
            g = g + p_ref[j]
        m_new = ADAM_B1 * m_ref[...] + (1.0 - ADAM_B1) * g
        v_new = ADAM_B2 * v_ref[...] + (1.0 - ADAM_B2) * (g * g)
        m_hat = m_new / (1.0 - ADAM_B1 ** ADAM_STEP)
        v_hat = v_new / (1.0 - ADAM_B2 ** ADAM_STEP)
        g_ref[...] = g
        d_ref[...] = -ADAM_LR * (m_hat / (jnp.sqrt(v_hat) + ADAM_EPS) + ADAM_WD * w_ref[...])
        nm_ref[...] = m_new
        nv_ref[...] = v_new

    blk = pl.BlockSpec((tr, cdim), lambda i: (i, 0))
    shp = jax.ShapeDtypeStruct((r, cdim), F32)
    return pl.pallas_call(
        body, grid=(r // tr,), in_specs=[pl.BlockSpec((npart, tr, cdim), lambda i: (0, i, 0)), blk, blk, blk], out_specs=[blk] * 4,
        out_shape=[shp] * 4, compiler_params=_params(("parallel",)), name=name,
    )(parts, w, m, v)


BIG = ("ev_w_in", "ev_w_uq", "ev_w_ukv", "ev_w_out", "od_w_in", "od_w_out", "meta")
SMALL = ("norm_g", "final_g", "ev_q_norm_g", "ev_kv_norm_g", "od_sinks")
SMALL_SHAPE = (8, 512)
BY_ROWS = ("ev_w_out", "od_w_out")

EV_IN_SHARD, OD_IN_SHARD, UQ_SHARD = 2976 // N_CHIPS, 2304 // N_CHIPS, 768 // N_CHIPS


def _pack_big(a, lead=()):
    dtype = a["ev_w_in"].dtype
    z = lambda r, c: jnp.zeros(lead + (r, c), dtype)
    ax = len(lead)
    pad_to = lambda x, width: jnp.concatenate([x, z(x.shape[ax], width - x.shape[ax + 1])], axis=ax + 1)
    corner = jnp.concatenate([a["ev_w_ukv"], a["meta"], z(256 - MLA_KV_LORA - N_META, 256)], axis=ax)
    return (pad_to(a["ev_w_in"], 768), jnp.concatenate([pad_to(a["ev_w_uq"], 256), corner], axis=ax + 1),
            pad_to(a["od_w_in"], 640), jnp.concatenate([a["ev_w_out"], a["od_w_out"]], axis=ax + 1))


N_FIRST = 2


def _unpack_big(p_in0, p_lat, p_in1, p_out):
    return dict(ev_w_in=p_in0[..., :EV_IN_SHARD], od_w_in=p_in1[..., :OD_IN_SHARD], ev_w_out=p_out[..., :D_MODEL],
                od_w_out=p_out[..., D_MODEL:], ev_w_uq=p_lat[..., :UQ_SHARD], ev_w_ukv=p_lat[..., :MLA_KV_LORA, 256:],
                meta=p_lat[..., MLA_KV_LORA:MLA_KV_LORA + N_META, 256:])


def _chip_shards(full, by_rows):
    if by_rows:
        return full.reshape(N_CHIPS, full.shape[0] // N_CHIPS, full.shape[1])
    return full.reshape(full.shape[0], N_CHIPS, -1).transpose(1, 0, 2)


def _from_chip_shards(slots, by_rows):
    if by_rows:
        return slots.reshape(-1, slots.shape[2])
    return slots.transpose(1, 0, 2).reshape(slots.shape[1], -1)


def _pack_small(arrs, extra=None):
    flat = [a.reshape(-1) for a in arrs] + ([] if extra is None else [extra.reshape(-1)])
    used = sum(f.shape[0] for f in flat)
    return jnp.pad(jnp.concatenate(flat), (0, SMALL_SHAPE[0] * SMALL_SHAPE[1] - used)).reshape(SMALL_SHAPE)


def _unpack_small(p, shapes):
    flat, out, at = p.reshape(-1), [], 0
    for s in shapes:
        n = int(np.prod(s))
        out.append(flat[at:at + n].reshape(s))
        at += n
    return out, flat[at]


def kernel(x, meta, norm_g, final_g, ev_w_in, ev_q_norm_g, ev_kv_norm_g, ev_w_uq, ev_w_ukv, ev_w_out, od_w_in, od_sinks, od_w_out, loss_target, m_meta, m_norm_g, m_final_g, m_ev_w_in, m_ev_q_norm_g, m_ev_kv_norm_g, m_ev_w_uq, m_ev_w_ukv, m_ev_w_out, m_od_w_in, m_od_sinks, m_od_w_out, v_meta, v_norm_g, v_final_g, v_ev_w_in, v_ev_q_norm_g, v_ev_kv_norm_g, v_ev_w_uq, v_ev_w_ukv, v_ev_w_out, v_od_w_in, v_od_sinks, v_od_w_out):
    given = dict(locals())
    two_d = lambda a: a[0] if a.ndim == 3 else a
    packs = {k: _pack_big({n: two_d(given[k + n]) for n in BIG}) for k in ("", "m_", "v_")}

    wbf = [p.astype(BF16) for p in packs[""]]
    later = _gather_on_sequencer(wbf[N_FIRST:], "gather_later_weights")
    first, meta_all = _gather_weights(wbf[:N_FIRST], meta, "gather_weights")
    full = {n: _from_chip_shards(a, n in BY_ROWS) for n, a in _unpack_big(*first, *later).items()}
    meta_full = _from_chip_shards(meta_all, False)

    loss, grad_x, grads = _local_step(x, meta_full, norm_g, final_g, ev_q_norm_g, ev_kv_norm_g, od_sinks, loss_target,
                                      full["ev_w_in"], full["ev_w_uq"], full["ev_w_ukv"], full["ev_w_out"], full["od_w_in"], full["od_w_out"])

    gpacks = _pack_big({n: _chip_shards(grads[n], n in BY_ROWS) for n in BIG}, lead=(N_CHIPS,))
    core = lax.axis_index("c").astype(jnp.int32).reshape(1)

    def reduce_group(group, tag, to_sibling, scatter):
        sums = [_add_sibling(g, s, core, f"add_sibling_{tag}{i}") for i, (g, s) in enumerate(zip(group, to_sibling(group)))]
        parts = [_sum_parts(p, f"add_chips_{tag}{i}") for i, p in enumerate(scatter(sums))]
        return _share_halves(parts, "reduced_to_sibling_" + tag)

    reduced_later = reduce_group(gpacks[N_FIRST:], "later", lambda g: _grads_to_sibling_on_sequencer(g, "grads_to_sibling_later", 3),
                                 lambda s: _chip_scatter_on_sequencer(s, "grads_to_chips_later", 2))
    reduced_first = reduce_group(gpacks[:N_FIRST], "first", lambda g: _grads_to_sibling(g, "grads_to_sibling_first"),
                                 lambda s: _chip_scatter_on_sequencer(s, "grads_to_chips_first", 4))
    reduced = reduced_first + reduced_later
    updated = [_adamw(r[None], packs[""][i], packs["m_"][i], packs["v_"][i], f"adamw_matrices_{i}") for i, r in enumerate(reduced)]
    big_out = [{n: a.reshape(given[n].shape) for n, a in _unpack_big(*outs).items()} for outs in zip(*updated)]

    small_shapes = [given[n].shape for n in SMALL]
    ssum = _all_reduce_small(_pack_small([grads[n] for n in SMALL], loss[0, 0]), "reduce_vectors")
    small_out = _adamw(ssum[None], _pack_small([given[n] for n in SMALL]), _pack_small([given["m_" + n] for n in SMALL]),
                       _pack_small([given["v_" + n] for n in SMALL]), "adamw_vectors")
    total_loss = ssum.reshape(-1)[sum(int(np.prod(s)) for s in small_shapes)]
    small_out = [_unpack_small(o, small_shapes)[0] for o in small_out]

    names = ("meta", "norm_g", "final_g", "ev_w_in", "ev_q_norm_g", "ev_kv_norm_g", "ev_w_uq", "ev_w_ukv", "ev_w_out", "od_w_in", "od_sinks",
             "od_w_out")
    outs = [total_loss, grad_x]
    for kind in range(4):
        for n in names:
            outs.append(big_out[kind][n] if n in BIG else small_out[kind][SMALL.index(n)])
    return tuple(outs)
```

```python
import functools
import math

import numpy as np
import jax
import jax.numpy as jnp
from jax import lax
from jax.experimental import pallas as pl
from jax.experimental.pallas import tpu as pltpu
from jax.experimental.pallas import tpu_sc as plsc

F32 = jnp.float32
BF16 = jnp.bfloat16

D_MODEL = 1024
BLOCK = 128
N_META = 16
N_PAD = BLOCK - N_META
NORM_EPS = 1e-6
NEG = -1e30
HEAD = 64
SB_HEADS = 8
MLA_HEADS = 8
MLA_Q_LORA = 256
MLA_KV_LORA = 128
MLA_NOPE = 64
MLA_ROPE = 32
ROPE_BASE = 10000.0
SWA_HEADS = 16
SWA_KV_HEADS = 2
SWA_WINDOW = 128
N_CHIPS = 4
N_DEV = 8

ADAM_LR = 0.001
ADAM_B1 = 0.9
ADAM_B2 = 0.999
ADAM_EPS = 1e-08
ADAM_WD = 0.01
ADAM_STEP = 10

VMEM_LIMIT = 48 * 1024 * 1024

EV_G, EV_Q, EV_K, EV_V, EV_CQ, EV_CKV, EV_KR, EV_N = 0, 1024, 1536, 2048, 2560, 2816, 2944, 3072
OD_G, OD_Q, OD_K, OD_V, OD_N = 0, 1024, 2048, 2176, 2304


def _params(sem=None):
    return pltpu.CompilerParams(dimension_semantics=sem, vmem_limit_bytes=VMEM_LIMIT)


def _row_tile(m):
    return 256 if m % 256 == 0 else 128


def _matmul_rows(m):
    for c in (1088, 1024, 768, 640, 512, 384, 256):
        if m % c == 0:
            return c
    return 128


def _dot(a, b):
    return jnp.dot(a.astype(BF16), b.astype(BF16), preferred_element_type=F32)


def _dot_nt(a, b):
    return lax.dot_general(a.astype(BF16), b.astype(BF16), (((1,), (1,)), ((), ())), preferred_element_type=F32)


def _dot_tn(a, b):
    return lax.dot_general(a.astype(BF16), b.astype(BF16), (((0,), (0,)), ((), ())), preferred_element_type=F32)


def _rms_fwd(h, g, name):
    t, d = h.shape
    tm = _row_tile(t)

    def body(h_ref, g_ref, o_ref):
        x = h_ref[...]
        r = lax.rsqrt(jnp.mean(x * x, axis=-1, keepdims=True) + NORM_EPS)
        o_ref[...] = ((x * r) * g_ref[...]).astype(o_ref.dtype)

    return pl.pallas_call(
        body, grid=(t // tm,),
        in_specs=[pl.BlockSpec((tm, d), lambda i: (i, 0)), pl.BlockSpec((1, d), lambda i: (0, 0))],
        out_specs=pl.BlockSpec((tm, d), lambda i: (i, 0)),
        out_shape=jax.ShapeDtypeStruct((t, d), BF16), compiler_params=_params(("parallel",)), name=name,
    )(h, g)


def _rms_bwd(h, g, dy, dres, name):
    t, d = h.shape
    tm = _row_tile(t)

    def body(h_ref, g_ref, dy_ref, dres_ref, dh_ref, dg_ref):
        @pl.when(pl.program_id(0) == 0)
        def _():
            dg_ref[...] = jnp.zeros_like(dg_ref)

        x = h_ref[...]
        r = lax.rsqrt(jnp.mean(x * x, axis=-1, keepdims=True) + NORM_EPS)
        xr = x * r
        dy_ = dy_ref[...]
        u = dy_ * g_ref[...]
        dh_ref[...] = dres_ref[...] + r * (u - xr * jnp.mean(u * xr, axis=-1, keepdims=True))
        dg_ref[...] += jnp.sum(dy_ * xr, axis=0, keepdims=True)

    row = pl.BlockSpec((tm, d), lambda i: (i, 0))
    vec = pl.BlockSpec((1, d), lambda i: (0, 0))
    return pl.pallas_call(
        body, grid=(t // tm,), in_specs=[row, vec, row, row], out_specs=[row, vec],
        out_shape=[jax.ShapeDtypeStruct((t, d), F32), jax.ShapeDtypeStruct((1, d), F32)],
        compiler_params=_params(("arbitrary",)), name=name,
    )(h, g, dy, dres)


def _col_tile(n):
    for c in (1024, 768, 640, 512, 384, 256, 128):
        if n % c == 0:
            return c
    return n


def _mm(a, w, name, res=None, out_dtype=F32, a_cols=None):
    m = a.shape[0]
    k, n = w.shape
    a_blk = 0 if a_cols is None else a_cols[0] // k
    assert a_cols is None or (a_cols[1] == k and a_cols[0] % k == 0)
    tm, tn = _matmul_rows(m), _col_tile(n)

    def body(*refs):
        if res is None:
            a_ref, w_ref, o_ref = refs
            acc = _dot(a_ref[...], w_ref[...])
        else:
            a_ref, w_ref, r_ref, o_ref = refs
            acc = r_ref[...] + _dot(a_ref[...], w_ref[...])
        o_ref[...] = acc.astype(o_ref.dtype)

    in_specs = [pl.BlockSpec((tm, k), lambda j, i: (i, a_blk)), pl.BlockSpec((k, tn), lambda j, i: (0, j))]
    args = [a, w]
    if res is not None:
        in_specs.append(pl.BlockSpec((tm, tn), lambda j, i: (i, j)))
        args.append(res)
    return pl.pallas_call(
        body, grid=(n // tn, m // tm), in_specs=in_specs, out_specs=pl.BlockSpec((tm, tn), lambda j, i: (i, j)),
        out_shape=jax.ShapeDtypeStruct((m, n), out_dtype), compiler_params=_params(("parallel", "parallel")), name=name,
    )(*args)


def _mm_nt(a, w, name):
    m, n = a.shape
    k = w.shape[0]
    tm, tk = _matmul_rows(m), _col_tile(k)

    def body(a_ref, w_ref, o_ref):
        o_ref[...] = _dot_nt(a_ref[...], w_ref[...])

    return pl.pallas_call(
        body, grid=(k // tk, m // tm),
        in_specs=[pl.BlockSpec((tm, n), lambda j, i: (i, 0)), pl.BlockSpec((tk, n), lambda j, i: (j, 0))],
        out_specs=pl.BlockSpec((tm, tk), lambda j, i: (i, j)),
        out_shape=jax.ShapeDtypeStruct((m, k), F32), compiler_params=_params(("parallel", "parallel")), name=name,
    )(a, w)


def _mm_tn(x, dy, name):
    m, k = x.shape
    n = dy.shape[1]
    tm, tn = _matmul_rows(m), _col_tile(n)

    def body(x_ref, dy_ref, o_ref):
        @pl.when(pl.program_id(1) == 0)
        def _():
            o_ref[...] = jnp.zeros_like(o_ref)

        o_ref[...] += _dot_tn(x_ref[...], dy_ref[...])

    return pl.pallas_call(
        body, grid=(n // tn, m // tm),
        in_specs=[pl.BlockSpec((tm, k), lambda j, i: (i, 0)), pl.BlockSpec((tm, tn), lambda j, i: (i, j))],
        out_specs=pl.BlockSpec((k, tn), lambda j, i: (0, j)),
        out_shape=jax.ShapeDtypeStruct((k, n), F32), compiler_params=_params(("parallel", "arbitrary")), name=name,
    )(x, dy)


def _silu_parts(g):
    s = 1.0 / (1.0 + jnp.exp(-g))
    return g * s, s * (1.0 + g * (1.0 - s))


def _gate_fwd(o_parts, proj, name):
    t = proj.shape[0]
    tm = _row_tile(t)
    w = D_MODEL // len(o_parts)

    def body(*refs):
        g_ref, o_ref = refs[-2], refs[-1]
        for p, r in enumerate(refs[:-2]):
            sil, _ = _silu_parts(g_ref[:, p * w:(p + 1) * w])
            o_ref[:, p * w:(p + 1) * w] = (r[...].astype(F32) * sil).astype(o_ref.dtype)

    return pl.pallas_call(
        body, grid=(t // tm,),
        in_specs=[pl.BlockSpec((tm, w), lambda i: (i, 0)) for _ in o_parts] + [pl.BlockSpec((tm, D_MODEL), lambda i: (i, 0))],
        out_specs=pl.BlockSpec((tm, D_MODEL), lambda i: (i, 0)),
        out_shape=jax.ShapeDtypeStruct((t, D_MODEL), BF16), compiler_params=_params(("parallel",)), name=name,
    )(*o_parts, proj)


def _gate_bwd(dao, o_parts, proj, name):
    t = proj.shape[0]
    tm = _row_tile(t)
    np_ = len(o_parts)
    w = D_MODEL // np_

    def body(*refs):
        dao_ref, g_ref = refs[0], refs[1 + np_]
        do_refs, dg_ref = refs[2 + np_:2 + 2 * np_], refs[-1]
        for p in range(np_):
            sl = slice(p * w, (p + 1) * w)
            sil, dsil = _silu_parts(g_ref[:, sl])
            da = dao_ref[:, sl]
            do_refs[p][...] = da * sil
            dg_ref[:, sl] = (da * refs[1 + p][...].astype(F32) * dsil).astype(dg_ref.dtype)

    full = pl.BlockSpec((tm, D_MODEL), lambda i: (i, 0))
    part = pl.BlockSpec((tm, w), lambda i: (i, 0))
    outs = pl.pallas_call(
        body, grid=(t // tm,), in_specs=[full] + [part] * np_ + [full], out_specs=[part] * np_ + [full],
        out_shape=[jax.ShapeDtypeStruct((t, w), F32)] * np_ + [jax.ShapeDtypeStruct((t, D_MODEL), BF16)],
        compiler_params=_params(("parallel",)), name=name,
    )(dao, *o_parts, proj)
    return outs[:np_], outs[np_]


def _loss_head(h2, gf, target, b, lp):
    d = h2.shape[1]
    nb = lp // BLOCK
    h3 = h2.reshape(b, lp, d)

    def body(h_ref, g_ref, t_ref, dh_ref, dg_ref, loss_ref):
        first = (pl.program_id(0) == 0) & (pl.program_id(1) == 0)

        @pl.when(first)
        def _():
            dg_ref[...] = jnp.zeros_like(dg_ref)
            loss_ref[...] = jnp.zeros_like(loss_ref)

        @pl.when(pl.program_id(1) == 0)
        def _():
            dh_ref[...] = jnp.zeros_like(dh_ref)

        @pl.when(pl.program_id(1) > 0)
        def _():
            x = h_ref[0]
            r = lax.rsqrt(jnp.mean(x * x, axis=-1, keepdims=True) + NORM_EPS)
            xr = x * r
            g = g_ref[...]
            diff = xr * g - t_ref[0]
            loss_ref[...] += 0.5 * jnp.sum(jnp.mean(diff * diff, axis=-1, keepdims=True))
            dy = diff * (1.0 / d)
            u = dy * g
            dh_ref[0] = r * (u - xr * jnp.mean(u * xr, axis=-1, keepdims=True))
            dg_ref[...] += jnp.sum(dy * xr, axis=0, keepdims=True)

    blk = pl.BlockSpec((1, BLOCK, d), lambda bi, n: (bi, n, 0))
    dh, dg, loss = pl.pallas_call(
        body, grid=(b, nb),
        in_specs=[blk, pl.BlockSpec((1, d), lambda bi, n: (0, 0)),
                  pl.BlockSpec((1, BLOCK, d), lambda bi, n: (bi, jnp.maximum(n - 1, 0), 0))],
        out_specs=[blk, pl.BlockSpec((1, d), lambda bi, n: (0, 0)), pl.BlockSpec((8, 128), lambda bi, n: (0, 0))],
        out_shape=[jax.ShapeDtypeStruct((b, lp, d), F32), jax.ShapeDtypeStruct((1, d), F32), jax.ShapeDtypeStruct((8, 128), F32)],
        compiler_params=_params(("arbitrary", "arbitrary")), name="loss_head",
    )(h3, gf, target)
    return dh.reshape(b * lp, d), dg, loss


def _iota2(shape, dim):
    return lax.broadcasted_iota(jnp.int32, shape, dim)


KEYS = 512
SB_FWD_KEYS = 512


def _lo_lanes():
    return _iota2((1, BLOCK), 1) < HEAD


def _halves(x, lo):
    zero = jnp.zeros_like(x)
    return jnp.where(lo, x, zero), jnp.where(lo, zero, x)


def _rows_of_pair(a, b):
    return jnp.where(_iota2((BLOCK, 1), 0) < HEAD, a, b)


def _split_rows_t(x):
    xt = x.T
    first = _iota2(xt.shape, 0) < HEAD
    zero = jnp.zeros_like(xt)
    return jnp.concatenate([jnp.where(first, xt, zero), jnp.where(first, zero, xt)], axis=1).astype(BF16)


def _key_chunk(c, lp, t_idx, strict, key_axis):
    keys = t_idx.shape[key_axis]
    first = c * keys
    s0 = pl.multiple_of(jnp.minimum(first, lp - keys), BLOCK)
    s_idx = s0 + _iota2(t_idx.shape, key_axis)
    seen = (s_idx < t_idx) if strict else (s_idx <= t_idx)
    return s0, seen & (s_idx >= jnp.maximum(first, N_PAD))


def _tri_dot(x, tri):
    return jnp.dot(x.astype(BF16), tri, preferred_element_type=F32)


def _stack_halves(x, lo):
    a, b = _halves(x, lo)
    return jnp.concatenate([a, b], axis=0)


def _pair(a, b, lo):
    return jnp.where(lo, a, b)


def _chunk_starts(lp):
    return [min(c * KEYS, lp - KEYS) for c in range(-(-lp // KEYS))]


def _put_rows(ref, r0, bq, a, b):
    for t in range(bq // BLOCK):
        part = slice(t * BLOCK, (t + 1) * BLOCK)
        ref[0, 0, r0 // BLOCK + t] = jnp.concatenate([a[:, part], b[:, part], jnp.zeros((6, BLOCK), F32)], axis=0)


def _get_rows(ref, r0, bq):
    return [jnp.concatenate([ref[0, 0, r0 // BLOCK + t, h:h + 1, :] for t in range(bq // BLOCK)], axis=1) for h in range(2)]


def _n_chunks(i, keys):
    return ((i + 1) * BLOCK + keys - 1) // keys


QROWS = 512


def _for_query_tiles(nb, tile, keys=KEYS):
    per = QROWS // BLOCK

    def step(j, _):
        tile(pl.multiple_of(j * QROWS, QROWS), QROWS, (j + 1) * (QROWS // keys))
        return 0

    lax.fori_loop(0, nb // per, step, 0)
    for i in range(nb - nb % per, nb):
        tile(i * BLOCK, BLOCK, _n_chunks(i, keys))


def _walk_chunks(r0, n, chunk, carry, leftwards=False, keys=KEYS):
    diag = jnp.maximum(r0 // keys, 1)

    def span(first, last, masked, carry):
        def step(t, cr):
            return chunk(last - 1 - t if leftwards else first + t, cr, masked)
        return lax.fori_loop(0, last - first, step, carry)

    spans = [(0, 1, True), (1, diag, False), (diag, n, True)]
    for first, last, masked in (reversed(spans) if leftwards else spans):
        carry = span(first, last, masked, carry)
    return carry


def _where(valid, x, other):
    return x if valid is None else jnp.where(valid, x, other)


HEAD_SCALE = HEAD ** -0.5
assert math.frexp(HEAD_SCALE)[0] == 0.5


def _sb_scores(q_h, k, valid, after):
    z = _dot_nt(q_h, k)
    lb = jnp.minimum(z, 0.0) - jnp.log(1.0 + jnp.exp(-jnp.abs(z)))
    l1m_all = lb - z
    l1m = _where(valid, l1m_all, 0.0)
    return lb, l1m_all, l1m, _tri_dot(l1m, after)


def _pair_stat_spec(nb):
    return pl.BlockSpec((1, 1, nb, 8, BLOCK), lambda bi, hp: (bi, hp, 0, 0, 0))


def _sb_fwd(proj3, name):
    b, lp, _ = proj3.shape
    nb = lp // BLOCK
    npair = SB_HEADS // 2

    keys = SB_FWD_KEYS

    def body(q_ref, k_ref, v_ref, o_ref, tot_ref):
        lo = _lo_lanes()
        after = (_iota2((keys, keys), 0) > _iota2((keys, keys), 1)).astype(BF16)

        def qtile(r0, bq, n):
            qs = _halves((q_ref[0, pl.ds(r0, bq), :] * HEAD_SCALE).astype(BF16), lo)
            t_idx = r0 + _iota2((bq, keys), 0)

            def kchunk(c, carry, masked):
                cs, acc = carry[:2], carry[2]
                s0, valid = _key_chunk(c, lp, t_idx, True, 1)
                valid = valid if masked else None
                k = k_ref[0, pl.ds(s0, keys), :].astype(BF16)
                a_s, new = [], []
                for h in range(2):
                    lb, _, l1m, suf = _sb_scores(qs[h], k, valid, after)
                    a_s.append(_where(valid, jnp.exp(lb + suf + cs[h]), 0.0).astype(BF16))
                    new.append(cs[h] + jnp.sum(l1m, axis=1, keepdims=True))
                v_bd = _stack_halves(v_ref[0, pl.ds(s0, keys), :].astype(BF16), lo)
                return (*new, acc + jnp.dot(jnp.concatenate(a_s, axis=1), v_bd, preferred_element_type=F32))

            zero = jnp.zeros((bq, 1), F32)
            c_a, c_b, acc = _walk_chunks(r0, n, kchunk, (zero, zero, jnp.zeros((bq, BLOCK), F32)), leftwards=True, keys=keys)
            o_ref[0, pl.ds(r0, bq), :] = acc
            tot_ref[0, pl.ds(r0, bq), :] = jnp.broadcast_to(_pair(c_a, c_b, lo), (bq, BLOCK))

        _for_query_tiles(nb, qtile, keys)

    def col(first):
        return pl.BlockSpec((1, lp, 2 * HEAD), lambda bi, hp: (bi, 0, first // (2 * HEAD) + hp))

    shp = jax.ShapeDtypeStruct((b, lp, SB_HEADS * HEAD), F32)
    return pl.pallas_call(
        body, grid=(b, npair), in_specs=[col(EV_Q), col(EV_K), col(EV_V)], out_specs=[col(0), col(0)], out_shape=[shp, shp],
        compiler_params=_params(("parallel", "parallel")), name=name,
    )(proj3, proj3, proj3)


def _sb_bwd(proj3, tot, do, name):
    b, lp, _ = proj3.shape
    nb = lp // BLOCK
    npair = SB_HEADS // 2

    def body(q_ref, k_ref, v_ref, tot_ref, do_ref, dq_ref, dk_ref, dv_ref):
        lo = _lo_lanes()
        after = (_iota2((KEYS, KEYS), 0) > _iota2((KEYS, KEYS), 1)).astype(BF16)
        before = (_iota2((KEYS, KEYS), 0) < _iota2((KEYS, KEYS), 1)).astype(BF16)
        dk_ref[...] = jnp.zeros_like(dk_ref)
        dv_ref[...] = jnp.zeros_like(dv_ref)

        def qtile(r0, bq, n):
            rows = pl.ds(r0, bq)
            qs = _halves((q_ref[0, rows, :] * HEAD_SCALE).astype(BF16), lo)
            dos = _halves(do_ref[0, rows, :].astype(BF16), lo)
            tot_i = tot_ref[0, rows, :]
            tots = (tot_i[:, 0:1], tot_i[:, HEAD:HEAD + 1])
            q_st, do_st = jnp.concatenate(qs, axis=0), jnp.concatenate(dos, axis=0)
            t_idx = r0 + _iota2((bq, KEYS), 0)

            def kchunk(c, carry, masked):
                s0, valid = _key_chunk(c, lp, t_idx, True, 1)
                valid = valid if masked else None
                keys = pl.ds(s0, KEYS)
                k = k_ref[0, keys, :].astype(BF16)
                v = v_ref[0, keys, :].astype(BF16)
                a_s, dzs, new = [], [], []
                for h in range(2):
                    left, pre = carry[2 * h], carry[2 * h + 1]
                    lb, l1m_all, l1m, suf = _sb_scores(qs[h], k, valid, after)
                    here = jnp.sum(l1m, axis=1, keepdims=True)
                    a = _where(valid, jnp.exp(lb + suf + (tots[h] - left - here)), 0.0)
                    w = a * _dot_nt(dos[h], v)
                    dz = _where(valid, w * jnp.exp(l1m_all) - (pre + _tri_dot(w, before)) * jnp.exp(lb), 0.0)
                    new += [left + here, pre + jnp.sum(w, axis=1, keepdims=True)]
                    a_s.append(a.astype(BF16))
                    dzs.append(dz.astype(BF16))
                dk_ref[0, keys, :] += _dot_tn(jnp.concatenate(dzs, axis=0), q_st)
                dv_ref[0, keys, :] += _dot_tn(jnp.concatenate(a_s, axis=0), do_st)
                dq = carry[4] + jnp.dot(jnp.concatenate(dzs, axis=1), _stack_halves(k, lo), preferred_element_type=F32)
                return (*new, dq)

            zero = jnp.zeros((bq, 1), F32)
            out = _walk_chunks(r0, n, kchunk, (zero, zero, zero, zero, jnp.zeros((bq, BLOCK), F32)))
            dq_ref[0, rows, :] = out[4] * HEAD_SCALE

        _for_query_tiles(nb, qtile)

    def col(first):
        return pl.BlockSpec((1, lp, 2 * HEAD), lambda bi, hp: (bi, 0, first // (2 * HEAD) + hp))

    shp = jax.ShapeDtypeStruct((b, lp, SB_HEADS * HEAD), F32)
    return pl.pallas_call(
        body, grid=(b, npair), in_specs=[col(EV_Q), col(EV_K), col(EV_V), col(0), col(0)], out_specs=[col(0)] * 3, out_shape=[shp] * 3,
        compiler_params=_params(("parallel", "parallel")), name=name,
    )(proj3, proj3, proj3, tot, do)


def _rope_tables(lp):
    half = MLA_ROPE // 2
    pos = (np.arange(lp) - N_PAD).astype(np.float32)
    inv = jnp.asarray(ROPE_BASE, F32) ** (-jnp.arange(half, dtype=F32) / half)
    ang = jnp.asarray(pos)[:, None] * inv[None, :]
    cos, sin = jnp.cos(ang), jnp.sin(ang)
    zeros = lambda n: jnp.zeros((lp, n), F32)
    c = jnp.concatenate([jnp.ones((lp, MLA_NOPE), F32), cos, cos, zeros(32)], axis=1)
    s1 = jnp.concatenate([zeros(MLA_NOPE), -sin, zeros(half), zeros(32)], axis=1)
    s2 = jnp.concatenate([zeros(MLA_NOPE), zeros(half), sin, zeros(32)], axis=1)
    return c, s1, s2


def _rope(x, c, s1, s2):
    half = MLA_ROPE // 2
    return x * c + pltpu.roll(x, BLOCK - half, 1) * s1 + pltpu.roll(x, half, 1) * s2


def _rope_t(dy, c, s1, s2):
    half = MLA_ROPE // 2
    return dy * c + pltpu.roll(dy * s1, half, 1) + pltpu.roll(dy * s2, BLOCK - half, 1)


def _rms_rows(x, g):
    r = lax.rsqrt(jnp.mean(x * x, axis=-1, keepdims=True) + NORM_EPS)
    return x * r, r


def _prep_rows(lp):
    return lp // 4 if lp % 64 == 0 else BLOCK


def _mla_prep_fwd(proj3, gq, gkv, wq, wk, wv, tabs, name):
    b, lp, _ = proj3.shape
    rows = _prep_rows(lp)
    hw = MLA_HEADS * BLOCK

    def body(cq_ref, ckv_ref, kr_ref, gq_ref, gkv_ref, wq_ref, wk_ref, wv_ref, c_ref, s1_ref, s2_ref, qf_ref, kf_ref, v_ref):
        c, s1, s2 = c_ref[...], s1_ref[...], s2_ref[...]
        xq, _ = _rms_rows(cq_ref[0], None)
        qh = _dot(xq * gq_ref[...], wq_ref[...])
        xk, _ = _rms_rows(ckv_ref[0], None)
        ckv_n = xk * gkv_ref[...]
        kv = _dot(ckv_n, wk_ref[...])
        v_ref[0] = _dot(ckv_n, wv_ref[...]).astype(v_ref.dtype)
        kr = _rope(kr_ref[0], c, s1, s2)
        for h in range(MLA_HEADS):
            ls = slice(h * BLOCK, (h + 1) * BLOCK)
            qf_ref[0, :, ls] = _rope(qh[:, ls], c, s1, s2).astype(qf_ref.dtype)
            kf_ref[0, :, ls] = (kv[:, ls] + kr).astype(kf_ref.dtype)

    def col(first, width):
        return pl.BlockSpec((1, rows, width), lambda bi, n: (bi, n, first // width))

    def whole(a):
        return pl.BlockSpec(a.shape, lambda bi, n: (0,) * a.ndim)

    tab = pl.BlockSpec((rows, BLOCK), lambda bi, n: (n, 0))
    return pl.pallas_call(
        body, grid=(b, lp // rows),
        in_specs=[col(EV_CQ, MLA_Q_LORA), col(EV_CKV, MLA_KV_LORA), col(EV_KR, BLOCK), whole(gq), whole(gkv), whole(wq), whole(wk),
                  whole(wv), tab, tab, tab],
        out_specs=[col(0, hw), col(0, hw), col(0, MLA_HEADS * HEAD)],
        out_shape=[jax.ShapeDtypeStruct((b, lp, hw), BF16), jax.ShapeDtypeStruct((b, lp, hw), BF16),
                   jax.ShapeDtypeStruct((b, lp, MLA_HEADS * HEAD), BF16)],
        compiler_params=_params(("parallel", "parallel")), name=name,
    )(proj3, proj3, proj3, gq, gkv, wq, wk, wv, *tabs)


def _mla_prep_bwd(proj3, gq, gkv, wq, wk, wv, tabs, dqf, dkf, dv, name):
    b, lp, _ = proj3.shape
    rows = _prep_rows(lp)
    hw = MLA_HEADS * BLOCK

    def body(cq_ref, ckv_ref, gq_ref, gkv_ref, wq_ref, wk_ref, wv_ref, c_ref, s1_ref, s2_ref, dqf_ref, dkf_ref, dv_ref,
             dcq_ref, dckv_ref, dkr_ref, dwq_ref, dwk_ref, dwv_ref, dgq_ref, dgkv_ref, dqh):
        @pl.when((pl.program_id(0) == 0) & (pl.program_id(1) == 0))
        def _():
            for r in (dwq_ref, dwk_ref, dwv_ref, dgq_ref, dgkv_ref):
                r[...] = jnp.zeros_like(r)

        c, s1, s2 = c_ref[...], s1_ref[...], s2_ref[...]
        dkr = jnp.zeros((rows, BLOCK), F32)
        for h in range(MLA_HEADS):
            ls = slice(h * BLOCK, (h + 1) * BLOCK)
            dqh[:, ls] = _rope_t(dqf_ref[0, :, ls].astype(F32), c, s1, s2).astype(dqh.dtype)
            dkr = dkr + dkf_ref[0, :, ls].astype(F32)
        dkr_ref[0] = _rope_t(dkr, c, s1, s2).astype(dkr_ref.dtype)

        def norm_bwd(x, g, dy, dg_ref):
            xr, r = _rms_rows(x, None)
            u = dy * g
            dg_ref[...] += jnp.sum(dy * xr, axis=0, keepdims=True)
            return r * (u - xr * jnp.mean(u * xr, axis=-1, keepdims=True))

        xq, _ = _rms_rows(cq_ref[0], None)
        cq_n = xq * gq_ref[...]
        dwq_ref[...] += _dot_tn(cq_n, dqh[...])
        dcq_ref[0] = norm_bwd(cq_ref[0], gq_ref[...], _dot_nt(dqh[...], wq_ref[...]), dgq_ref).astype(dcq_ref.dtype)
        xk, _ = _rms_rows(ckv_ref[0], None)
        ckv_n = xk * gkv_ref[...]
        dkf_, dv_ = dkf_ref[0], dv_ref[0]
        dwk_ref[...] += _dot_tn(ckv_n, dkf_)
        dwv_ref[...] += _dot_tn(ckv_n, dv_)
        dckv_n = _dot_nt(dkf_, wk_ref[...]) + _dot_nt(dv_, wv_ref[...])
        dckv_ref[0] = norm_bwd(ckv_ref[0], gkv_ref[...], dckv_n, dgkv_ref).astype(dckv_ref.dtype)

    def col(first, width):
        return pl.BlockSpec((1, rows, width), lambda bi, n: (bi, n, first // width))

    def whole(a):
        return pl.BlockSpec(a.shape, lambda bi, n: (0,) * len(a.shape))

    tab = pl.BlockSpec((rows, BLOCK), lambda bi, n: (n, 0))
    acc_shapes = [jax.ShapeDtypeStruct(a.shape, F32) for a in (wq, wk, wv, gq, gkv)]
    return pl.pallas_call(
        body, grid=(b, lp // rows),
        in_specs=[col(EV_CQ, MLA_Q_LORA), col(EV_CKV, MLA_KV_LORA), whole(gq), whole(gkv), whole(wq), whole(wk), whole(wv), tab, tab, tab,
                  col(0, hw), col(0, hw), col(0, MLA_HEADS * HEAD)],
        out_specs=[col(0, MLA_Q_LORA), col(0, MLA_KV_LORA), col(0, BLOCK)] + [whole(a) for a in acc_shapes],
        out_shape=[jax.ShapeDtypeStruct((b, lp, MLA_Q_LORA), BF16), jax.ShapeDtypeStruct((b, lp, MLA_KV_LORA), BF16),
                   jax.ShapeDtypeStruct((b, lp, BLOCK), BF16)] + acc_shapes,
        scratch_shapes=[pltpu.VMEM((rows, hw), BF16)],
        compiler_params=_params(("arbitrary", "arbitrary")), name=name,
    )(proj3, proj3, gq, gkv, wq, wk, wv, *tabs, dqf, dkf, dv)


def _mla_fwd(qf, kf, v, name):
    b, lp, _ = qf.shape
    nb = lp // BLOCK
    npair = MLA_HEADS // 2
    scale = (MLA_NOPE + MLA_ROPE) ** -0.5
    starts = _chunk_starts(lp)

    def body(q_ref, k_ref, v_ref, o_ref, lse_ref, vt_ref):
        for c, s0 in enumerate(starts):
            vt_ref[c] = _split_rows_t(v_ref[0, s0:s0 + KEYS, :].astype(F32))

        def qtile(r0, bq, n):
            qs = [q_ref[0, pl.ds(r0, bq), h * BLOCK:(h + 1) * BLOCK] for h in range(2)]
            t_idx = r0 + _iota2((KEYS, bq), 1)

            def kchunk(c, carry, masked):
                stats, acc = carry[:4], carry[4]
                s0, valid = _key_chunk(c, lp, t_idx, False, 0)
                valid = valid if masked else None
                ps, new, alphas = [], [], []
                for h in range(2):
                    m, l = stats[2 * h], stats[2 * h + 1]
                    s = _where(valid, _dot_nt(k_ref[0, pl.ds(s0, KEYS), h * BLOCK:(h + 1) * BLOCK], qs[h]) * scale, NEG)
                    m_new = jnp.maximum(m, jnp.max(s, axis=0, keepdims=True))
                    p = _where(valid, jnp.exp(s - m_new), 0.0)
                    alpha = jnp.exp(m - m_new)
                    new += [m_new, alpha * l + jnp.sum(p, axis=0, keepdims=True)]
                    alphas.append(alpha)
                    ps.append(p.astype(BF16))
                pv = jnp.dot(vt_ref[c], jnp.concatenate(ps, axis=0), preferred_element_type=F32)
                return (*new, _rows_of_pair(alphas[0], alphas[1]) * acc + pv)

            neg, zero = jnp.full((1, bq), NEG, F32), jnp.zeros((1, bq), F32)
            m_a, l_a, m_b, l_b, acc = _walk_chunks(r0, n, kchunk, (neg, zero, neg, zero, jnp.zeros((BLOCK, bq), F32)))
            safe = [jnp.where(l > 0.0, l, 1.0) for l in (l_a, l_b)]
            o_ref[0, pl.ds(r0, bq), :] = (acc / _rows_of_pair(safe[0], safe[1])).T
            lse = [jnp.where(l > 0.0, m + jnp.log(sf), 0.0) for m, l, sf in ((m_a, l_a, safe[0]), (m_b, l_b, safe[1]))]
            _put_rows(lse_ref, r0, bq, lse[0], lse[1])

        _for_query_tiles(nb, qtile)

    wide = pl.BlockSpec((1, lp, 2 * BLOCK), lambda bi, hp: (bi, 0, hp))
    thin = pl.BlockSpec((1, lp, 2 * HEAD), lambda bi, hp: (bi, 0, hp))
    return pl.pallas_call(
        body, grid=(b, npair), in_specs=[wide, wide, thin], out_specs=[thin, _pair_stat_spec(nb)],
        out_shape=[jax.ShapeDtypeStruct((b, lp, MLA_HEADS * HEAD), F32), jax.ShapeDtypeStruct((b, npair, nb, 8, BLOCK), F32)],
        scratch_shapes=[pltpu.VMEM((len(starts), BLOCK, 2 * KEYS), BF16)],
        compiler_params=_params(("parallel", "parallel")), name=name,
    )(qf, kf, v)


def _mla_bwd(qf, kf, v, o, lse, do, name):
    b, lp, _ = qf.shape
    nb = lp // BLOCK
    npair = MLA_HEADS // 2
    scale = (MLA_NOPE + MLA_ROPE) ** -0.5

    starts = _chunk_starts(lp)

    def body(q_ref, k_ref, v_ref, o_ref, lse_ref, do_ref, dq_ref, dk_ref, dv_ref, kt_ref):
        lo = _lo_lanes()
        dk_ref[...] = jnp.zeros_like(dk_ref)
        dv_ref[...] = jnp.zeros_like(dv_ref)
        for c, s0 in enumerate(starts):
            for h in range(2):
                kt_ref[c, h] = k_ref[0, s0:s0 + KEYS, h * BLOCK:(h + 1) * BLOCK].astype(F32).T.astype(BF16)

        def qtile(r0, bq, n):
            rows = pl.ds(r0, bq)
            qs = [q_ref[0, rows, h * BLOCK:(h + 1) * BLOCK] for h in range(2)]
            do_i = do_ref[0, rows, :]
            dos = _halves(do_i.astype(BF16), lo)
            do_st = jnp.concatenate(dos, axis=0)
            both = (do_i * o_ref[0, rows, :]).T
            dsum = (jnp.sum(both[:HEAD], axis=0, keepdims=True), jnp.sum(both[HEAD:], axis=0, keepdims=True))
            lses = _get_rows(lse_ref, r0, bq)
            t_idx = r0 + _iota2((KEYS, bq), 1)

            def kchunk(c, dqts, masked):
                s0, valid = _key_chunk(c, lp, t_idx, False, 0)
                valid = valid if masked else None
                keys = pl.ds(s0, KEYS)
                v_c = v_ref[0, keys, :]
                ps, out = [], []
                for h in range(2):
                    lanes = slice(h * BLOCK, (h + 1) * BLOCK)
                    s = _dot_nt(k_ref[0, keys, lanes], qs[h]) * scale
                    p = _where(valid, jnp.exp(s - lses[h]), 0.0)
                    ds = (p * (_dot_nt(v_c, dos[h]) - dsum[h]) * scale).astype(BF16)
                    dk_ref[0, keys, lanes] += jnp.dot(ds, qs[h], preferred_element_type=F32)
                    out.append(dqts[h] + jnp.dot(kt_ref[c, h], ds, preferred_element_type=F32))
                    ps.append(p.astype(BF16))
                dv_ref[0, keys, :] += jnp.dot(jnp.concatenate(ps, axis=1), do_st, preferred_element_type=F32)
                return tuple(out)

            zero = jnp.zeros((BLOCK, bq), F32)
            dq_a, dq_b = _walk_chunks(r0, n, kchunk, (zero, zero))
            dq_ref[0, rows, 0:BLOCK] = dq_a.T
            dq_ref[0, rows, BLOCK:2 * BLOCK] = dq_b.T

        _for_query_tiles(nb, qtile)

    wide = pl.BlockSpec((1, lp, 2 * BLOCK), lambda bi, hp: (bi, 0, hp))
    thin = pl.BlockSpec((1, lp, 2 * HEAD), lambda bi, hp: (bi, 0, hp))
    return pl.pallas_call(
        body, grid=(b, npair), in_specs=[wide, wide, thin, thin, _pair_stat_spec(nb), thin], out_specs=[wide, wide, thin],
        out_shape=[jax.ShapeDtypeStruct(qf.shape, F32), jax.ShapeDtypeStruct(qf.shape, F32), jax.ShapeDtypeStruct(v.shape, F32)],
        scratch_shapes=[pltpu.VMEM((len(starts), 2, BLOCK, KEYS), BF16)],
        compiler_params=_params(("parallel", "parallel")), name=name,
    )(qf, kf, v, o, lse, do)


SWA_KEYS = 2 * BLOCK + N_META


def _swa_keys(k_ref, v_ref, n, kv):
    prev = jnp.maximum(n - 1, 0)
    rows = lambda blk: pl.ds(pl.multiple_of(blk * BLOCK, BLOCK), BLOCK)
    mine = (_iota2((1, BLOCK), 1) >= HEAD).astype(jnp.int32) == kv

    def both_halves(ref):
        x = jnp.concatenate([ref[0, rows(prev), :], ref[0, rows(n), :], ref[0, N_PAD:BLOCK, :]], axis=0)
        return jnp.where(mine, x, pltpu.roll(x, HEAD, 1))

    slot = _iota2((SWA_KEYS, BLOCK), 0)
    s_idx = jnp.where(slot < 2 * BLOCK, (n - 1) * BLOCK + slot, slot - 2 * BLOCK + N_PAD)
    dist = n * BLOCK + _iota2((SWA_KEYS, BLOCK), 1) - s_idx
    band = (slot < 2 * BLOCK) & (dist >= 0) & (dist < SWA_WINDOW) & (s_idx >= BLOCK)
    meta = (slot >= 2 * BLOCK) & (dist >= 0)
    return both_halves(k_ref), both_halves(v_ref), band | meta, dist.astype(F32), prev


def _pad_keys(x):
    return jnp.concatenate([x, jnp.zeros((3 * BLOCK - SWA_KEYS, x.shape[1]), x.dtype)], axis=0)


def _swa_probs(q_h, kdup, valid, dist, head, sink_ref):
    slope = jnp.exp(jnp.full((1, 1), -8.0 * math.log(2.0) / SWA_HEADS, F32) * (head + 1).astype(F32))
    s = jnp.where(valid, _dot_nt(kdup, q_h) - slope * dist, NEG)
    sink = sink_ref[pl.ds(head, 1), 0:1]
    m = jnp.maximum(jnp.max(s, axis=0, keepdims=True), sink)
    e = jnp.where(valid, jnp.exp(s - m), 0.0)
    es = jnp.exp(sink - m)
    inv = 1.0 / (jnp.sum(e, axis=0, keepdims=True) + es)
    return e * inv, es * inv


SWA_PAIRS = SWA_HEADS // SWA_KV_HEADS // 2
SWA_GROUP = SWA_PAIRS * 2 * HEAD


def _swa_specs(b, lp):
    nb = lp // BLOCK
    qcol = lambda first: pl.BlockSpec((1, BLOCK, SWA_GROUP), lambda bi, kv, n: (bi, n, first // SWA_GROUP + kv))
    kcol = lambda first: pl.BlockSpec((1, lp, BLOCK), lambda bi, kv, n: (bi, 0, first // BLOCK))
    sink = pl.BlockSpec((SWA_HEADS, BLOCK), lambda bi, kv, n: (0, 0))
    return (b, SWA_KV_HEADS, nb), qcol, kcol, sink


def _swa_fwd(proj3, sinks, name):
    b, lp, _ = proj3.shape
    grid, qcol, kcol, sink = _swa_specs(b, lp)

    def body(q_ref, k_ref, v_ref, sink_ref, o_ref):
        kv, n = pl.program_id(1), pl.program_id(2)
        lo = _lo_lanes()
        kdup, vdup, valid, dist, _ = _swa_keys(k_ref, v_ref, n, kv)
        kdup = kdup.astype(BF16)
        vt = _split_rows_t(_pad_keys(vdup))
        for p in range(SWA_PAIRS):
            lanes = slice(p * BLOCK, (p + 1) * BLOCK)
            qs = _halves((q_ref[0, :, lanes] * HEAD_SCALE).astype(BF16), lo)
            probs = [_swa_probs(qs[hh], kdup, valid, dist, (kv * SWA_PAIRS + p) * 2 + hh, sink_ref)[0].astype(BF16) for hh in range(2)]
            o_ref[0, :, lanes] = jnp.dot(vt, jnp.concatenate([_pad_keys(pr) for pr in probs], axis=0), preferred_element_type=F32).T

    return pl.pallas_call(
        body, grid=grid, in_specs=[qcol(OD_Q), kcol(OD_K), kcol(OD_V), sink], out_specs=qcol(0),
        out_shape=jax.ShapeDtypeStruct((b, lp, SWA_HEADS * HEAD), F32),
        compiler_params=_params(("parallel", "parallel", "parallel")), name=name,
    )(proj3, proj3, proj3, sinks)


def _swa_bwd(proj3, sinks, do, name):
    b, lp, _ = proj3.shape
    nb = lp // BLOCK
    grid, qcol, kcol, sink = _swa_specs(b, lp)

    def body(q_ref, k_ref, v_ref, sink_ref, do_ref, dq_ref, dk_ref, dv_ref, dsink_ref, dk_acc, dv_acc):
        kv, n = pl.program_id(1), pl.program_id(2)

        @pl.when((n == 0) & (pl.program_id(0) == 0) & (kv == 0))
        def _():
            dsink_ref[...] = jnp.zeros_like(dsink_ref)

        @pl.when(n == 0)
        def _():
            dk_acc[...] = jnp.zeros_like(dk_acc)
            dv_acc[...] = jnp.zeros_like(dv_acc)

        lo = _lo_lanes()
        kdup, vdup, valid, dist, prev = _swa_keys(k_ref, v_ref, n, kv)
        kt = _split_rows_t(_pad_keys(kdup))
        kdup, vdup = kdup.astype(BF16), vdup.astype(BF16)
        dkc = jnp.zeros((SWA_KEYS, BLOCK), F32)
        dvc = jnp.zeros((SWA_KEYS, BLOCK), F32)
        for p in range(SWA_PAIRS):
            lanes = slice(p * BLOCK, (p + 1) * BLOCK)
            qs = _halves((q_ref[0, :, lanes] * HEAD_SCALE).astype(BF16), lo)
            dos = _halves(do_ref[0, :, lanes].astype(BF16), lo)
            dss, prs = [], []
            for hh in range(2):
                head = (kv * SWA_PAIRS + p) * 2 + hh
                pr, ps = _swa_probs(qs[hh], kdup, valid, dist, head, sink_ref)
                dp = _dot_nt(vdup, dos[hh])
                dsum = jnp.sum(pr * dp, axis=0, keepdims=True)
                dsink_ref[pl.ds(head, 1), :] += jnp.broadcast_to(-jnp.sum(ps * dsum, axis=1, keepdims=True), (1, BLOCK))
                dss.append((pr * (dp - dsum)).astype(BF16))
                prs.append(pr.astype(BF16))
            dq_ref[0, :, lanes] = jnp.dot(kt, jnp.concatenate([_pad_keys(d) for d in dss], axis=0), preferred_element_type=F32).T * HEAD_SCALE
            dkc = dkc + jnp.dot(jnp.concatenate(dss, axis=1), jnp.concatenate(qs, axis=0), preferred_element_type=F32)
            dvc = dvc + jnp.dot(jnp.concatenate(prs, axis=1), jnp.concatenate(dos, axis=0), preferred_element_type=F32)
        rows = lambda blk: pl.ds(pl.multiple_of(blk * BLOCK, BLOCK), BLOCK)
        for r, part in ((rows(prev), slice(0, BLOCK)), (rows(n), slice(BLOCK, 2 * BLOCK)), (slice(N_PAD, BLOCK), slice(2 * BLOCK, SWA_KEYS))):
            dk_acc[r, :] += dkc[part]
            dv_acc[r, :] += dvc[part]

        for acc, ref in ((dk_acc, dk_ref), (dv_acc, dv_ref)):
            @pl.when((n == nb - 1) & (kv == 0))
            def _():
                x = acc[...]
                ref[0] = x + pltpu.roll(x, HEAD, 1)

            @pl.when((n == nb - 1) & (kv == 1))
            def _():
                x = acc[...]
                ref[0] = jnp.where(lo, ref[0], x + pltpu.roll(x, HEAD, 1))

    kvout = pl.BlockSpec((1, lp, BLOCK), lambda bi, kv, n: (bi, 0, 0))
    kvshape = jax.ShapeDtypeStruct((b, lp, BLOCK), F32)
    return pl.pallas_call(
        body, grid=grid, in_specs=[qcol(OD_Q), kcol(OD_K), kcol(OD_V), sink, qcol(0)], out_specs=[qcol(0), kvout, kvout, sink],
        out_shape=[jax.ShapeDtypeStruct((b, lp, SWA_HEADS * HEAD), F32), kvshape, kvshape, jax.ShapeDtypeStruct((SWA_HEADS, BLOCK), F32)],
        scratch_shapes=[pltpu.VMEM((lp, BLOCK), F32), pltpu.VMEM((lp, BLOCK), F32)],
        compiler_params=_params(("arbitrary", "arbitrary", "arbitrary")), name=name,
    )(proj3, proj3, proj3, sinks, do)


def _kernel_weights(ev_w_in, ev_w_uq, ev_w_ukv, od_w_in):
    zeros = lambda r, c: jnp.zeros((r, c), ev_w_in.dtype)
    q_sb, k_sb, v_sb, g_sb, c_q, c_kv, k_r, g_mla = jnp.split(ev_w_in, [512, 1024, 1536, 2048, 2304, 2432, 2464], axis=1)
    w0 = jnp.concatenate([g_sb, g_mla, q_sb, k_sb, v_sb, c_q, c_kv, zeros(D_MODEL, MLA_NOPE), k_r, zeros(D_MODEL, 32)], axis=1)
    uq = ev_w_uq.reshape(MLA_Q_LORA, MLA_HEADS, MLA_NOPE + MLA_ROPE)
    wq = jnp.pad(uq, ((0, 0), (0, 0), (0, BLOCK - MLA_NOPE - MLA_ROPE))).reshape(MLA_Q_LORA, MLA_HEADS * BLOCK)
    ukv = ev_w_ukv.reshape(MLA_KV_LORA, MLA_HEADS, BLOCK)
    wk = jnp.pad(ukv[:, :, :MLA_NOPE], ((0, 0), (0, 0), (0, BLOCK - MLA_NOPE))).reshape(MLA_KV_LORA, MLA_HEADS * BLOCK)
    wv = ukv[:, :, MLA_NOPE:].reshape(MLA_KV_LORA, MLA_HEADS * HEAD)
    q, k, v, g = jnp.split(od_w_in, [1024, 1152, 1280], axis=1)
    w1 = jnp.concatenate([g, q, k, v], axis=1)
    return w0, wq, wk, wv, w1


def _od_w_in_grad(dw1):
    sl = lambda a, first, n: a[:, first:first + n]
    return jnp.concatenate([sl(dw1, OD_Q, 1024), sl(dw1, OD_K, 128), sl(dw1, OD_V, 128), sl(dw1, OD_G, 1024)], axis=1)


def _original_grads(dw0, dwq, dwk, dwv):
    sl = lambda a, first, n: a[:, first:first + n]
    d_ev_w_in = jnp.concatenate([sl(dw0, EV_Q, 512), sl(dw0, EV_K, 512), sl(dw0, EV_V, 512), sl(dw0, EV_G, 512), sl(dw0, EV_CQ, 256),
                                 sl(dw0, EV_CKV, 128), sl(dw0, EV_KR + MLA_NOPE, MLA_ROPE), sl(dw0, EV_G + 512, 512)], axis=1)
    d_uq = dwq.reshape(MLA_Q_LORA, MLA_HEADS, BLOCK)[:, :, :MLA_NOPE + MLA_ROPE].reshape(MLA_Q_LORA, -1)
    d_ukv = jnp.concatenate([dwk.reshape(MLA_KV_LORA, MLA_HEADS, BLOCK)[:, :, :MLA_NOPE], dwv.reshape(MLA_KV_LORA, MLA_HEADS, HEAD)],
                            axis=2).reshape(MLA_KV_LORA, -1)
    return d_ev_w_in, d_uq, d_ukv


def _meta_rows_sum(dh0_3):
    b, _, d = dh0_3.shape

    def body(x_ref, o_ref):
        acc = x_ref[0, N_PAD:BLOCK, :]
        for i in range(1, b):
            acc = acc + x_ref[i, N_PAD:BLOCK, :]
        o_ref[...] = acc

    return pl.pallas_call(
        body, grid=(1,), in_specs=[pl.BlockSpec((b, BLOCK, d), lambda i: (0, 0, 0))], out_specs=pl.BlockSpec((N_META, d), lambda i: (0, 0)),
        out_shape=jax.ShapeDtypeStruct((N_META, d), F32), compiler_params=_params(("arbitrary",)), name="meta_rows_sum",
    )(dh0_3)


def _local_step(x, meta, norm_g, final_g, gq, gkv, sinks, target, ev_w_in, ev_w_uq, ev_w_ukv, wo0, od_w_in, wo1, reduce_early=None):
    b, seq, d = x.shape
    lp = seq + BLOCK
    t = b * lp
    w0, wq, wk, wv, w1 = _kernel_weights(ev_w_in, ev_w_uq, ev_w_ukv, od_w_in)
    h0 = jnp.concatenate([jnp.zeros((b, N_PAD, d), F32), jnp.broadcast_to(meta[None], (b, N_META, d)), x], axis=1).reshape(t, d)
    tabs = _rope_tables(lp)
    g0, g1 = norm_g[0:1], norm_g[1:2]

    hn0 = _rms_fwd(h0, g0, "norm0")
    proj0 = _mm(hn0, w0, "inproj0")
    p0 = proj0.reshape(b, lp, EV_N)
    o_sb, sb_tot = _sb_fwd(p0, "sb_fwd")
    qf, kf, v = _mla_prep_fwd(p0, gq, gkv, wq, wk, wv, tabs, "mla_prep_fwd")
    o_mla, lse = _mla_fwd(qf, kf, v, "mla_fwd")
    o0 = [o_sb.reshape(t, -1), o_mla.reshape(t, -1)]
    ao0 = _gate_fwd(o0, proj0, "gate0")
    h1 = _mm(ao0, wo0, "outproj0", res=h0)

    hn1 = _rms_fwd(h1, g1, "norm1")
    proj1 = _mm(hn1, w1, "inproj1")
    p1 = proj1.reshape(b, lp, OD_N)
    sinks_b = jnp.broadcast_to(sinks.reshape(SWA_HEADS, 1), (SWA_HEADS, BLOCK))
    o1 = _swa_fwd(p1, sinks_b, "swa_fwd").reshape(t, -1)
    ao1 = _gate_fwd([o1], proj1, "gate1")
    h2 = _mm(ao1, wo1, "outproj1", res=h1)

    dh2, d_final_g, loss = _loss_head(h2, final_g.reshape(1, d), target, b, lp)

    d_wo1 = _mm_tn(ao1, dh2, "d_wo1")
    dao1 = _mm_nt(dh2, wo1, "d_ao1")
    (do1,), dg1 = _gate_bwd(dao1, [o1], proj1, "gate1_bwd")
    dq1, dk4, dv4, d_sinks = _swa_bwd(p1, sinks_b, do1.reshape(b, lp, -1), "swa_bwd")
    unheads = lambda a: a.reshape(t, SWA_KV_HEADS * HEAD).astype(BF16)
    dproj1 = jnp.concatenate([dg1, dq1.reshape(t, -1).astype(BF16), unheads(dk4), unheads(dv4)], axis=1)
    d_w1 = _mm_tn(hn1, dproj1, "d_w1")
    dhn1 = _mm_nt(dproj1, w1, "d_hn1")
    dh1, d_g1 = _rms_bwd(h1, g1, dhn1, dh2, "norm1_bwd")

    d_wo0 = _mm_tn(ao0, dh1, "d_wo0")
    early = dict(od_w_in=_od_w_in_grad(d_w1), ev_w_out=d_wo0, od_w_out=d_wo1)
    dao0 = _mm_nt(dh1, wo0, "d_ao0")
    if reduce_early is not None:
        pending, finish = reduce_early(early)
        dao0, pending = lax.optimization_barrier((dao0, pending))
        early_done = finish(pending)
    (do_sb, do_mla), dg0 = _gate_bwd(dao0, o0, proj0, "gate0_bwd")
    dq_sb, dk_sb, dv_sb = _sb_bwd(p0, sb_tot, do_sb.reshape(b, lp, -1), "sb_bwd")
    dqf, dkf, dv = _mla_bwd(qf, kf, v, o_mla, lse, do_mla.reshape(b, lp, -1), "mla_bwd")
    dcq, dckv, dkr, d_wq, d_wk, d_wv, d_gq, d_gkv = _mla_prep_bwd(p0, gq, gkv, wq, wk, wv, tabs, dqf, dkf, dv, "mla_prep_bwd")
    flat = lambda a: a.reshape(t, -1).astype(BF16)
    dproj0 = jnp.concatenate([dg0, flat(dq_sb), flat(dk_sb), flat(dv_sb), flat(dcq), flat(dckv), flat(dkr)], axis=1)
    if reduce_early is not None:
        dproj0, early_done = lax.optimization_barrier((dproj0, early_done))
    d_w0 = _mm_tn(hn0, dproj0, "d_w0")
    dhn0 = _mm_nt(dproj0, w0, "d_hn0")
    dh0, d_g0 = _rms_bwd(h0, g0, dhn0, dh1, "norm0_bwd")
    dh0 = dh0.reshape(b, lp, d)

    d_ev_w_in, d_uq, d_ukv = _original_grads(d_w0, d_wq, d_wk, d_wv)
    grads = dict(meta=_meta_rows_sum(dh0), norm_g=jnp.concatenate([d_g0, d_g1], axis=0), final_g=d_final_g.reshape(d),
                 ev_w_in=d_ev_w_in, ev_q_norm_g=d_gq, ev_kv_norm_g=d_gkv, ev_w_uq=d_uq, ev_w_ukv=d_ukv,
                 od_sinks=d_sinks[:, 0].reshape(1, SWA_HEADS))
    if reduce_early is None:
        return loss, dh0[:, BLOCK:], {**grads, **early}
    return loss, dh0[:, BLOCK:], grads, early_done


MESH = pl.DeviceIdType.MESH
ANY = pl.BlockSpec(memory_space=pl.ANY)


def _place():
    return lax.axis_index("x"), lax.axis_index("y"), lax.axis_index("c")


def _other_chips(x, y):
    return [(1 - x, y), (x, 1 - y), (1 - x, 1 - y)]


def _with_own_slot(slots, own):
    me = 2 * lax.axis_index("x") + lax.axis_index("y")
    return lax.dynamic_update_slice(slots, own[None], (me,) + (0,) * own.ndim)


def _gather_weights(packs, meta, name):
    n = len(packs)

    def body(*refs):
        ins, m_ref, outs, mo_ref = refs[:n], refs[n], refs[n + 1:2 * n + 1], refs[2 * n + 1]
        send_sems, recv_sems = refs[2 * n + 2:]
        x, y, c = _place()
        me, sib = 2 * x + y, (x, y, 1 - c)
        chips = _other_chips(x, y)

        def copy(k, src, dst, to):
            return pltpu.make_async_remote_copy(src_ref=src, dst_ref=dst, send_sem=send_sems.at[k], recv_sem=recv_sems.at[k], device_id=to,
                                                device_id_type=MESH)

        def half(i, chip, h):
            rows = packs[i].shape[0] // 2
            return outs[i].at[chip, pl.ds(h * rows, rows), :]

        def mine(i):
            rows = packs[i].shape[0] // 2
            return ins[i].at[pl.ds(c * rows, rows), :]

        sent = [copy(6 * i + k, mine(i), half(i, me, c), (px, py, c)) for i in range(n) for k, (px, py) in enumerate(chips)]
        sent += [copy(6 * n + k, m_ref, mo_ref.at[me], (px, py, c)) for k, (px, py) in enumerate(chips)]
        for cp in sent:
            cp.start()
        for i in range(n):
            for k, (px, py) in enumerate(chips):
                landed = half(i, 2 * px + py, c)
                copy(6 * i + k, mine(i), landed, (px, py, c)).wait_recv()
                fwd = copy(6 * i + 3 + k, landed, landed, sib)
                fwd.start()
                sent.append(fwd)
        for k, (px, py) in enumerate(chips):
            for i in range(n):
                other = half(i, 2 * px + py, 1 - c)
                copy(6 * i + 3 + k, other, other, sib).wait_recv()
            copy(6 * n + k, m_ref, mo_ref.at[2 * px + py], (px, py, c)).wait_recv()
        for cp in sent:
            cp.wait_send()

    nsem = 6 * n + 3
    res = pl.pallas_call(
        body, in_specs=[ANY] * (n + 1), out_specs=[ANY] * (n + 1),
        out_shape=[jax.ShapeDtypeStruct((N_CHIPS,) + a.shape, a.dtype) for a in list(packs) + [meta]],
        scratch_shapes=[pltpu.SemaphoreType.DMA((nsem,)), pltpu.SemaphoreType.DMA((nsem,))],
        name=name,
    )(*packs, meta)
    return [_with_own_slot(r, a) for r, a in zip(res[:n], packs)], _with_own_slot(res[n], meta)


def _grads_to_sibling(gs, name):
    n = len(gs)

    def body(*refs):
        ins, outs, send_sems, recv_sems = refs[:n], refs[n:2 * n], refs[2 * n], refs[2 * n + 1]
        x, y, c = _place()
        cps = []
        for i in range(n):
            rows = gs[i].shape[1] // 2
            cps.append(pltpu.make_async_remote_copy(src_ref=ins[i].at[:, pl.ds((1 - c) * rows, rows), :], dst_ref=outs[i],
                                                    send_sem=send_sems.at[i], recv_sem=recv_sems.at[i], device_id=(x, y, 1 - c),
                                                    device_id_type=MESH))
        for cp in cps:
            cp.start()
        for cp in cps:
            cp.wait()

    return pl.pallas_call(
        body, in_specs=[ANY] * n, out_specs=[ANY] * n,
        out_shape=[jax.ShapeDtypeStruct((g.shape[0], g.shape[1] // 2, g.shape[2]), g.dtype) for g in gs],
        scratch_shapes=[pltpu.SemaphoreType.DMA((n,)), pltpu.SemaphoreType.DMA((n,))],
        name=name,
    )(*gs)


def _share_halves(rs, name):
    n = len(rs)

    def body(*refs):
        ins, outs, send_sems, recv_sems = refs[:n], refs[n:2 * n], refs[2 * n], refs[2 * n + 1]
        x, y, c = _place()
        cps = [pltpu.make_async_remote_copy(src_ref=ins[i], dst_ref=outs[i], send_sem=send_sems.at[i], recv_sem=recv_sems.at[i],
                                            device_id=(x, y, 1 - c), device_id_type=MESH) for i in range(n)]
        for cp in cps:
            cp.start()
        for cp in cps:
            cp.wait()

    theirs = pl.pallas_call(
        body, in_specs=[ANY] * n, out_specs=[ANY] * n, out_shape=[jax.ShapeDtypeStruct(r.shape, r.dtype) for r in rs],
        scratch_shapes=[pltpu.SemaphoreType.DMA((n,)), pltpu.SemaphoreType.DMA((n,))],
        name=name,
    )(*rs)
    first = lax.axis_index("c") == 0
    return [jnp.where(first, jnp.concatenate([r, t], axis=0), jnp.concatenate([t, r], axis=0)) for r, t in zip(rs, theirs)]


def _chip_scatter(ss, name):
    n = len(ss)

    def body(*refs):
        ins, outs, send_sems, recv_sems = refs[:n], refs[n:2 * n], refs[2 * n], refs[2 * n + 1]
        x, y, c = _place()
        me = 2 * x + y
        chips = _other_chips(x, y)
        for i in range(n):
            for k, (px, py) in enumerate(chips):
                pltpu.make_async_remote_copy(src_ref=ins[i].at[2 * px + py], dst_ref=outs[i].at[me], send_sem=send_sems.at[3 * i + k],
                                             recv_sem=recv_sems.at[3 * i + k], device_id=(px, py, c), device_id_type=MESH).start()
        for i in range(n):
            for k, (px, py) in enumerate(chips):
                cp = pltpu.make_async_remote_copy(src_ref=ins[i].at[2 * px + py], dst_ref=outs[i].at[2 * px + py],
                                                  send_sem=send_sems.at[3 * i + k], recv_sem=recv_sems.at[3 * i + k],
                                                  device_id=(px, py, c), device_id_type=MESH)
                cp.wait_recv()
                cp.wait_send()

    parts = pl.pallas_call(
        body, in_specs=[ANY] * n, out_specs=[ANY] * n, out_shape=[jax.ShapeDtypeStruct(s.shape, s.dtype) for s in ss],
        scratch_shapes=[pltpu.SemaphoreType.DMA((3 * n,)), pltpu.SemaphoreType.DMA((3 * n,))],
        name=name,
    )(*ss)
    me = 2 * lax.axis_index("x") + lax.axis_index("y")
    return [_with_own_slot(p, lax.dynamic_index_in_dim(s, me, axis=0, keepdims=False)) for p, s in zip(parts, ss)]


HBM_SPACE = pltpu.MemorySpace.HBM


def _on_sequencer(name, collective_id, n_sems, body):
    @pl.kernel(mesh=plsc.ScalarSubcoreMesh(axis_name="sequencer", num_cores=1), name=name,
               scratch_types=(pltpu.SemaphoreType.DMA((n_sems,)), pltpu.SemaphoreType.DMA((n_sems,))),
               compiler_params=pltpu.CompilerParams(collective_id=collective_id))
    def launch(send_sems, recv_sems):
        body(send_sems, recv_sems)

    launch()


def _handshake(peers):
    barrier = pltpu.get_barrier_semaphore()
    for peer in peers:
        pl.semaphore_signal(barrier, inc=1, device_id=peer, device_id_type=MESH)
    pl.semaphore_wait(barrier, len(peers))


def _gather_on_sequencer(packs, name):
    n = len(packs)
    ins = [jax.new_ref(p, memory_space=HBM_SPACE) for p in packs]
    outs = [jax.empty_ref(jax.ShapeDtypeStruct((N_CHIPS,) + p.shape, p.dtype), memory_space=HBM_SPACE) for p in packs]

    def body(send_sems, recv_sems):
        x, y, c = _place()
        me, sib = 2 * x + y, (x, y, 1 - c)
        chips = _other_chips(x, y)
        _handshake([(px, py, c) for px, py in chips] + [sib])

        def copy(k, src, dst, to):
            return pltpu.make_async_remote_copy(src_ref=src, dst_ref=dst, send_sem=send_sems.at[k], recv_sem=recv_sems.at[k], device_id=to,
                                                device_id_type=MESH)

        def half(i, chip, h):
            rows = packs[i].shape[0] // 2
            return outs[i].at[chip, pl.ds(h * rows, rows), :]

        def mine(i):
            rows = packs[i].shape[0] // 2
            return ins[i].at[pl.ds(c * rows, rows), :]

        sent = [copy(6 * i + k, mine(i), half(i, me, c), (px, py, c)) for i in range(n) for k, (px, py) in enumerate(chips)]
        for cp in sent:
            cp.start()
        for i in range(n):
            for k, (px, py) in enumerate(chips):
                landed = half(i, 2 * px + py, c)
                copy(6 * i + k, mine(i), landed, (px, py, c)).wait_recv()
                fwd = copy(6 * i + 3 + k, landed, landed, sib)
                fwd.start()
                sent.append(fwd)
        for i in range(n):
            for k, (px, py) in enumerate(chips):
                other = half(i, 2 * px + py, 1 - c)
                copy(6 * i + 3 + k, other, other, sib).wait_recv()
        for cp in sent:
            cp.wait_send()

    _on_sequencer(name, 1, 6 * n, body)
    return [_with_own_slot(o[...], p) for o, p in zip(outs, packs)]


def _grads_to_sibling_on_sequencer(gs, name, collective_id):
    n = len(gs)
    ins = [jax.new_ref(g, memory_space=HBM_SPACE) for g in gs]
    outs = [jax.empty_ref(jax.ShapeDtypeStruct((g.shape[0], g.shape[1] // 2, g.shape[2]), g.dtype), memory_space=HBM_SPACE) for g in gs]

    def body(send_sems, recv_sems):
        x, y, c = _place()
        _handshake([(x, y, 1 - c)])
        cps = []
        for i in range(n):
            rows = gs[i].shape[1] // 2
            cps.append(pltpu.make_async_remote_copy(src_ref=ins[i].at[:, pl.ds((1 - c) * rows, rows), :], dst_ref=outs[i],
                                                    send_sem=send_sems.at[i], recv_sem=recv_sems.at[i], device_id=(x, y, 1 - c),
                                                    device_id_type=MESH))
        for cp in cps:
            cp.start()
        for cp in cps:
            cp.wait()

    _on_sequencer(name, collective_id, n, body)
    return [o[...] for o in outs]


def _chip_scatter_on_sequencer(ss, name, collective_id):
    n = len(ss)
    ins = [jax.new_ref(s, memory_space=HBM_SPACE) for s in ss]
    outs = [jax.empty_ref(jax.ShapeDtypeStruct(s.shape, s.dtype), memory_space=HBM_SPACE) for s in ss]

    def body(send_sems, recv_sems):
        x, y, c = _place()
        me = 2 * x + y
        chips = _other_chips(x, y)
        _handshake([(px, py, c) for px, py in chips])
        for i in range(n):
            for k, (px, py) in enumerate(chips):
                pltpu.make_async_remote_copy(src_ref=ins[i].at[2 * px + py], dst_ref=outs[i].at[me], send_sem=send_sems.at[3 * i + k],
                                             recv_sem=recv_sems.at[3 * i + k], device_id=(px, py, c), device_id_type=MESH).start()
        for i in range(n):
            for k, (px, py) in enumerate(chips):
                cp = pltpu.make_async_remote_copy(src_ref=ins[i].at[2 * px + py], dst_ref=outs[i].at[2 * px + py],
                                                  send_sem=send_sems.at[3 * i + k], recv_sem=recv_sems.at[3 * i + k],
                                                  device_id=(px, py, c), device_id_type=MESH)
                cp.wait_recv()
                cp.wait_send()

    _on_sequencer(name, collective_id, 3 * n, body)
    me = 2 * lax.axis_index("x") + lax.axis_index("y")
    return [_with_own_slot(o[...], lax.dynamic_index_in_dim(s, me, axis=0, keepdims=False)) for o, s in zip(outs, ss)]


def _all_reduce_small(v, name):
    shape = v.shape

    def body(v_ref, o_ref, slots, send_sems, recv_sems):
        x, y, c = _place()
        me = 4 * x + 2 * y + c
        slots[me] = v_ref[...]
        for r in range(1, N_DEV):
            peer = (x ^ (r >> 2), y ^ ((r >> 1) & 1), c ^ (r & 1))
            pltpu.make_async_remote_copy(src_ref=v_ref, dst_ref=slots.at[me], send_sem=send_sems.at[r - 1], recv_sem=recv_sems.at[r - 1],
                                         device_id=peer, device_id_type=MESH).start()
        for r in range(1, N_DEV):
            peer = (x ^ (r >> 2), y ^ ((r >> 1) & 1), c ^ (r & 1))
            cp = pltpu.make_async_remote_copy(src_ref=v_ref, dst_ref=slots.at[4 * peer[0] + 2 * peer[1] + peer[2]], send_sem=send_sems.at[r - 1],
                                              recv_sem=recv_sems.at[r - 1], device_id=peer, device_id_type=MESH)
            cp.wait_recv()
            cp.wait_send()
        acc = slots[0]
        for d in range(1, N_DEV):
            acc = acc + slots[d]
        o_ref[...] = acc

    vm = pl.BlockSpec(memory_space=pltpu.VMEM)
    return pl.pallas_call(
        body, in_specs=[vm], out_specs=vm, out_shape=jax.ShapeDtypeStruct(shape, F32),
        scratch_shapes=[pltpu.VMEM((N_DEV,) + shape, F32), pltpu.SemaphoreType.DMA((N_DEV - 1,)), pltpu.SemaphoreType.DMA((N_DEV - 1,))],
        name=name,
    )(v)


def _add_sibling(g, gsib, core, name):
    n, _, cdim = g.shape
    half = gsib.shape[1]
    tr = half // 2

    def body(core_ref, a_ref, b_ref, o_ref):
        o_ref[...] = (a_ref[...] + b_ref[...]).astype(o_ref.dtype)

    blk = pl.BlockSpec((1, tr, cdim), lambda j, i, core_ref: (j, i, 0))
    return pl.pallas_call(
        body,
        grid_spec=pltpu.PrefetchScalarGridSpec(
            num_scalar_prefetch=1, grid=(n, half // tr),
            in_specs=[pl.BlockSpec((1, tr, cdim), lambda j, i, core_ref: (j, core_ref[0] * (half // tr) + i, 0)), blk], out_specs=blk),
        out_shape=jax.ShapeDtypeStruct(gsib.shape, BF16), compiler_params=_params(("parallel", "parallel")), name=name,
    )(core, g, gsib)


def _sum_parts(parts, name):
    n, r, cdim = parts.shape
    tr = r // 2

    def body(p_ref, o_ref):
        acc = p_ref[0].astype(F32)
        for j in range(1, n):
            acc = acc + p_ref[j].astype(F32)
        o_ref[...] = acc

    return pl.pallas_call(
        body, grid=(r // tr,), in_specs=[pl.BlockSpec((n, tr, cdim), lambda i: (0, i, 0))],
        out_specs=pl.BlockSpec((tr, cdim), lambda i: (i, 0)), out_shape=jax.ShapeDtypeStruct((r, cdim), F32),
        compiler_params=_params(("parallel",)), name=name,
    )(parts)


def _adamw(parts, w, m, v, name):
    npart, r, cdim = parts.shape
    tr = r // 4 if r % 32 == 0 else r

    def body(p_ref, w_ref, m_ref, v_ref, g_ref, d_ref, nm_ref, nv_ref):
        g = p_ref[0]
        for j in range(1, npart):
            g = g + p_ref[j]
        m_new = ADAM_B1 * m_ref[...] + (1.0 - ADAM_B1) * g
        v_new = ADAM_B2 * v_ref[...] + (1.0 - ADAM_B2) * (g * g)
        m_hat = m_new / (1.0 - ADAM_B1 ** ADAM_STEP)
        v_hat = v_new / (1.0 - ADAM_B2 ** ADAM_STEP)
        g_ref[...] = g
        d_ref[...] = -ADAM_LR * (m_hat / (jnp.sqrt(v_hat) + ADAM_EPS) + ADAM_WD * w_ref[...])
        nm_ref[...] = m_new
        nv_ref[...] = v_new

    blk = pl.BlockSpec((tr, cdim), lambda i: (i, 0))
    shp = jax.ShapeDtypeStruct((r, cdim), F32)
    return pl.pallas_call(
        body, grid=(r // tr,), in_specs=[pl.BlockSpec((npart, tr, cdim), lambda i: (0, i, 0)), blk, blk, blk], out_specs=[blk] * 4,
        out_shape=[shp] * 4, compiler_params=_params(("parallel",)), name=name,
    )(parts, w, m, v)


BIG = ("ev_w_in", "ev_w_uq", "ev_w_ukv", "ev_w_out", "od_w_in", "od_w_out", "meta")
SMALL = ("norm_g", "final_g", "ev_q_norm_g", "ev_kv_norm_g", "od_sinks")
SMALL_SHAPE = (8, 512)
BY_ROWS = ("ev_w_out", "od_w_out")

EV_IN_SHARD, OD_IN_SHARD, UQ_SHARD = 2976 // N_CHIPS, 2304 // N_CHIPS, 768 // N_CHIPS


def _pack_big(a, lead=()):
    return _pack_first(a, lead) + _pack_later(a, lead)


def _pad_lanes(x, width):
    return jnp.concatenate([x, jnp.zeros(x.shape[:-1] + (width - x.shape[-1],), x.dtype)], axis=-1)


def _pack_first(a, lead=()):
    ax = len(lead)
    corner = jnp.concatenate([a["ev_w_ukv"], a["meta"], jnp.zeros(lead + (256 - MLA_KV_LORA - N_META, 256), a["meta"].dtype)], axis=ax)
    return _pad_lanes(a["ev_w_in"], 768), jnp.concatenate([_pad_lanes(a["ev_w_uq"], 256), corner], axis=ax + 1)


def _pack_later(a, lead=()):
    return _pad_lanes(a["od_w_in"], 640), jnp.concatenate([a["ev_w_out"], a["od_w_out"]], axis=len(lead) + 1)


N_FIRST = 2
LATER = ("od_w_in", "ev_w_out", "od_w_out")


def _unpack_big(p_in0, p_lat, p_in1, p_out):
    return dict(ev_w_in=p_in0[..., :EV_IN_SHARD], od_w_in=p_in1[..., :OD_IN_SHARD], ev_w_out=p_out[..., :D_MODEL],
                od_w_out=p_out[..., D_MODEL:], ev_w_uq=p_lat[..., :UQ_SHARD], ev_w_ukv=p_lat[..., :MLA_KV_LORA, 256:],
                meta=p_lat[..., MLA_KV_LORA:MLA_KV_LORA + N_META, 256:])


def _chip_shards(full, by_rows):
    if by_rows:
        return full.reshape(N_CHIPS, full.shape[0] // N_CHIPS, full.shape[1])
    return full.reshape(full.shape[0], N_CHIPS, -1).transpose(1, 0, 2)


def _from_chip_shards(slots, by_rows):
    if by_rows:
        return slots.reshape(-1, slots.shape[2])
    return slots.transpose(1, 0, 2).reshape(slots.shape[1], -1)


def _pack_small(arrs, extra=None):
    flat = [a.reshape(-1) for a in arrs] + ([] if extra is None else [extra.reshape(-1)])
    used = sum(f.shape[0] for f in flat)
    return jnp.pad(jnp.concatenate(flat), (0, SMALL_SHAPE[0] * SMALL_SHAPE[1] - used)).reshape(SMALL_SHAPE)


def _unpack_small(p, shapes):
    flat, out, at = p.reshape(-1), [], 0
    for s in shapes:
        n = int(np.prod(s))
        out.append(flat[at:at + n].reshape(s))
        at += n
    return out, flat[at]


def kernel(x, meta, norm_g, final_g, ev_w_in, ev_q_norm_g, ev_kv_norm_g, ev_w_uq, ev_w_ukv, ev_w_out, od_w_in, od_sinks, od_w_out, loss_target, m_meta, m_norm_g, m_final_g, m_ev_w_in, m_ev_q_norm_g, m_ev_kv_norm_g, m_ev_w_uq, m_ev_w_ukv, m_ev_w_out, m_od_w_in, m_od_sinks, m_od_w_out, v_meta, v_norm_g, v_final_g, v_ev_w_in, v_ev_q_norm_g, v_ev_kv_norm_g, v_ev_w_uq, v_ev_w_ukv, v_ev_w_out, v_od_w_in, v_od_sinks, v_od_w_out):
    given = dict(locals())
    two_d = lambda a: a[0] if a.ndim == 3 else a
    packs = {k: _pack_big({n: two_d(given[k + n]) for n in BIG}) for k in ("", "m_", "v_")}

    wbf = [p.astype(BF16) for p in packs[""]]
    later = _gather_on_sequencer(wbf[N_FIRST:], "gather_later_weights")
    first, meta_all = _gather_weights(wbf[:N_FIRST], meta, "gather_weights")
    full = {n: _from_chip_shards(a, n in BY_ROWS) for n, a in _unpack_big(*first, *later).items()}
    meta_full = _from_chip_shards(meta_all, False)

    core = lax.axis_index("c").astype(jnp.int32).reshape(1)
    shards = lambda g: {n: _chip_shards(a, n in BY_ROWS) for n, a in g.items()}

    def sums_of(group, tag, to_sibling):
        return [_add_sibling(g, s, core, f"add_sibling_{tag}{i}") for i, (g, s) in enumerate(zip(group, to_sibling(group)))]

    def update(sums, tag, first_pack, collective_id):
        parts = _chip_scatter_on_sequencer(sums, "grads_to_chips_" + tag, collective_id)
        reduced = _share_halves([_sum_parts(p, f"add_chips_{tag}{i}") for i, p in enumerate(parts)], "reduced_to_sibling_" + tag)
        return [_adamw(r[None], packs[""][first_pack + i], packs["m_"][first_pack + i], packs["v_"][first_pack + i],
                       f"adamw_matrices_{first_pack + i}") for i, r in enumerate(reduced)]

    def reduce_early(g):
        sums = sums_of(_pack_later(shards(g), (N_CHIPS,)), "later", lambda p: _grads_to_sibling_on_sequencer(p, "grads_to_sibling_later", 3))
        return sums, lambda s: update(s, "later", N_FIRST, 2)

    loss, grad_x, grads, updated_later = _local_step(
        x, meta_full, norm_g, final_g, ev_q_norm_g, ev_kv_norm_g, od_sinks, loss_target, full["ev_w_in"], full["ev_w_uq"],
        full["ev_w_ukv"], full["ev_w_out"], full["od_w_in"], full["od_w_out"], reduce_early)

    first = _pack_first(shards({n: grads[n] for n in BIG if n not in LATER}), (N_CHIPS,))
    updated = update(sums_of(first, "first", lambda p: _grads_to_sibling(p, "grads_to_sibling_first")), "first", 0, 4) + updated_later
    big_out = [{n: a.reshape(given[n].shape) for n, a in _unpack_big(*outs).items()} for outs in zip(*updated)]

    small_shapes = [given[n].shape for n in SMALL]
    ssum = _all_reduce_small(_pack_small([grads[n] for n in SMALL], loss[0, 0]), "reduce_vectors")
    small_out = _adamw(ssum[None], _pack_small([given[n] for n in SMALL]), _pack_small([given["m_" + n] for n in SMALL]),
                       _pack_small([given["v_" + n] for n in SMALL]), "adamw_vectors")
    total_loss = ssum.reshape(-1)[sum(int(np.prod(s)) for s in small_shapes)]
    small_out = [_unpack_small(o, small_shapes)[0] for o in small_out]

    names = ("meta", "norm_g", "final_g", "ev_w_in", "ev_q_norm_g", "ev_kv_norm_g", "ev_w_uq", "ev_w_ukv", "ev_w_out", "od_w_in", "od_sinks",
             "od_w_out")
    outs = [total_loss, grad_x]
    for kind in range(4):
        for n in names:
            outs.append(big_out[kind][n] if n in BIG else small_out[kind][SMALL.index(n)])
    return tuple(outs)
```

```python
import functools
import math

import numpy as np
import jax
import jax.numpy as jnp
from jax import lax
from jax.experimental import pallas as pl
from jax.experimental.pallas import tpu as pltpu
from jax.experimental.pallas import tpu_sc as plsc

F32 = jnp.float32
BF16 = jnp.bfloat16

D_MODEL = 1024
BLOCK = 128
N_META = 16
N_PAD = BLOCK - N_META
NORM_EPS = 1e-6
NEG = -1e30
HEAD = 64
SB_HEADS = 8
MLA_HEADS = 8
MLA_Q_LORA = 256
MLA_KV_LORA = 128
MLA_NOPE = 64
MLA_ROPE = 32
ROPE_BASE = 10000.0
SWA_HEADS = 16
SWA_KV_HEADS = 2
SWA_WINDOW = 128
N_CHIPS = 4
N_DEV = 8

ADAM_LR = 0.001
ADAM_B1 = 0.9
ADAM_B2 = 0.999
ADAM_EPS = 1e-08
ADAM_WD = 0.01
ADAM_STEP = 10

VMEM_LIMIT = 48 * 1024 * 1024

EV_G, EV_Q, EV_K, EV_V, EV_CQ, EV_CKV, EV_KR, EV_N = 0, 1024, 1536, 2048, 2560, 2816, 2944, 3072
OD_G, OD_Q, OD_K, OD_V, OD_N = 0, 1024, 2048, 2176, 2304


def _params(sem=None):
    return pltpu.CompilerParams(dimension_semantics=sem, vmem_limit_bytes=VMEM_LIMIT)


def _row_tile(m):
    return 256 if m % 256 == 0 else 128


def _matmul_rows(m):
    for c in (1088, 1024, 768, 640, 512, 384, 256):
        if m % c == 0:
            return c
    return 128


def _dot(a, b):
    return jnp.dot(a.astype(BF16), b.astype(BF16), preferred_element_type=F32)


def _dot_nt(a, b):
    return lax.dot_general(a.astype(BF16), b.astype(BF16), (((1,), (1,)), ((), ())), preferred_element_type=F32)


def _dot_tn(a, b):
    return lax.dot_general(a.astype(BF16), b.astype(BF16), (((0,), (0,)), ((), ())), preferred_element_type=F32)


def _rms_fwd(h, g, name):
    t, d = h.shape
    tm = _row_tile(t)

    def body(h_ref, g_ref, o_ref):
        x = h_ref[...]
        r = lax.rsqrt(jnp.mean(x * x, axis=-1, keepdims=True) + NORM_EPS)
        o_ref[...] = ((x * r) * g_ref[...]).astype(o_ref.dtype)

    return pl.pallas_call(
        body, grid=(t // tm,),
        in_specs=[pl.BlockSpec((tm, d), lambda i: (i, 0)), pl.BlockSpec((1, d), lambda i: (0, 0))],
        out_specs=pl.BlockSpec((tm, d), lambda i: (i, 0)),
        out_shape=jax.ShapeDtypeStruct((t, d), BF16), compiler_params=_params(("parallel",)), name=name,
    )(h, g)


def _rms_bwd(h, g, dy, dres, name):
    t, d = h.shape
    tm = _row_tile(t)

    def body(h_ref, g_ref, dy_ref, dres_ref, dh_ref, dg_ref):
        @pl.when(pl.program_id(0) == 0)
        def _():
            dg_ref[...] = jnp.zeros_like(dg_ref)

        x = h_ref[...]
        r = lax.rsqrt(jnp.mean(x * x, axis=-1, keepdims=True) + NORM_EPS)
        xr = x * r
        dy_ = dy_ref[...]
        u = dy_ * g_ref[...]
        dh_ref[...] = dres_ref[...] + r * (u - xr * jnp.mean(u * xr, axis=-1, keepdims=True))
        dg_ref[...] += jnp.sum(dy_ * xr, axis=0, keepdims=True)

    row = pl.BlockSpec((tm, d), lambda i: (i, 0))
    vec = pl.BlockSpec((1, d), lambda i: (0, 0))
    return pl.pallas_call(
        body, grid=(t // tm,), in_specs=[row, vec, row, row], out_specs=[row, vec],
        out_shape=[jax.ShapeDtypeStruct((t, d), F32), jax.ShapeDtypeStruct((1, d), F32)],
        compiler_params=_params(("arbitrary",)), name=name,
    )(h, g, dy, dres)


def _col_tile(n):
    for c in (1024, 768, 640, 512, 384, 256, 128):
        if n % c == 0:
            return c
    return n


def _mm(a, w, name, res=None, out_dtype=F32, a_cols=None):
    m = a.shape[0]
    k, n = w.shape
    a_blk = 0 if a_cols is None else a_cols[0] // k
    assert a_cols is None or (a_cols[1] == k and a_cols[0] % k == 0)
    tm, tn = _matmul_rows(m), _col_tile(n)

    def body(*refs):
        if res is None:
            a_ref, w_ref, o_ref = refs
            acc = _dot(a_ref[...], w_ref[...])
        else:
            a_ref, w_ref, r_ref, o_ref = refs
            acc = r_ref[...] + _dot(a_ref[...], w_ref[...])
        o_ref[...] = acc.astype(o_ref.dtype)

    in_specs = [pl.BlockSpec((tm, k), lambda j, i: (i, a_blk)), pl.BlockSpec((k, tn), lambda j, i: (0, j))]
    args = [a, w]
    if res is not None:
        in_specs.append(pl.BlockSpec((tm, tn), lambda j, i: (i, j)))
        args.append(res)
    return pl.pallas_call(
        body, grid=(n // tn, m // tm), in_specs=in_specs, out_specs=pl.BlockSpec((tm, tn), lambda j, i: (i, j)),
        out_shape=jax.ShapeDtypeStruct((m, n), out_dtype), compiler_params=_params(("parallel", "parallel")), name=name,
    )(*args)


def _mm_nt(a, w, name):
    m, n = a.shape
    k = w.shape[0]
    tm, tk = _matmul_rows(m), _col_tile(k)

    def body(a_ref, w_ref, o_ref):
        o_ref[...] = _dot_nt(a_ref[...], w_ref[...])

    return pl.pallas_call(
        body, grid=(k // tk, m // tm),
        in_specs=[pl.BlockSpec((tm, n), lambda j, i: (i, 0)), pl.BlockSpec((tk, n), lambda j, i: (j, 0))],
        out_specs=pl.BlockSpec((tm, tk), lambda j, i: (i, j)),
        out_shape=jax.ShapeDtypeStruct((m, k), F32), compiler_params=_params(("parallel", "parallel")), name=name,
    )(a, w)


def _mm_tn(x, dy, name):
    m, k = x.shape
    n = dy.shape[1]
    tm, tn = _matmul_rows(m), _col_tile(n)

    def body(x_ref, dy_ref, o_ref):
        @pl.when(pl.program_id(1) == 0)
        def _():
            o_ref[...] = jnp.zeros_like(o_ref)

        o_ref[...] += _dot_tn(x_ref[...], dy_ref[...])

    return pl.pallas_call(
        body, grid=(n // tn, m // tm),
        in_specs=[pl.BlockSpec((tm, k), lambda j, i: (i, 0)), pl.BlockSpec((tm, tn), lambda j, i: (i, j))],
        out_specs=pl.BlockSpec((k, tn), lambda j, i: (0, j)),
        out_shape=jax.ShapeDtypeStruct((k, n), F32), compiler_params=_params(("parallel", "arbitrary")), name=name,
    )(x, dy)


def _silu_parts(g):
    s = 1.0 / (1.0 + jnp.exp(-g))
    return g * s, s * (1.0 + g * (1.0 - s))


def _gate_fwd(o_parts, proj, name):
    t = proj.shape[0]
    tm = _row_tile(t)
    w = D_MODEL // len(o_parts)

    def body(*refs):
        g_ref, o_ref = refs[-2], refs[-1]
        for p, r in enumerate(refs[:-2]):
            sil, _ = _silu_parts(g_ref[:, p * w:(p + 1) * w])
            o_ref[:, p * w:(p + 1) * w] = (r[...].astype(F32) * sil).astype(o_ref.dtype)

    return pl.pallas_call(
        body, grid=(t // tm,),
        in_specs=[pl.BlockSpec((tm, w), lambda i: (i, 0)) for _ in o_parts] + [pl.BlockSpec((tm, D_MODEL), lambda i: (i, 0))],
        out_specs=pl.BlockSpec((tm, D_MODEL), lambda i: (i, 0)),
        out_shape=jax.ShapeDtypeStruct((t, D_MODEL), BF16), compiler_params=_params(("parallel",)), name=name,
    )(*o_parts, proj)


def _gate_bwd(dao, o_parts, proj, name):
    t = proj.shape[0]
    tm = _row_tile(t)
    np_ = len(o_parts)
    w = D_MODEL // np_

    def body(*refs):
        dao_ref, g_ref = refs[0], refs[1 + np_]
        do_refs, dg_ref = refs[2 + np_:2 + 2 * np_], refs[-1]
        for p in range(np_):
            sl = slice(p * w, (p + 1) * w)
            sil, dsil = _silu_parts(g_ref[:, sl])
            da = dao_ref[:, sl]
            do_refs[p][...] = da * sil
            dg_ref[:, sl] = (da * refs[1 + p][...].astype(F32) * dsil).astype(dg_ref.dtype)

    full = pl.BlockSpec((tm, D_MODEL), lambda i: (i, 0))
    part = pl.BlockSpec((tm, w), lambda i: (i, 0))
    outs = pl.pallas_call(
        body, grid=(t // tm,), in_specs=[full] + [part] * np_ + [full], out_specs=[part] * np_ + [full],
        out_shape=[jax.ShapeDtypeStruct((t, w), F32)] * np_ + [jax.ShapeDtypeStruct((t, D_MODEL), BF16)],
        compiler_params=_params(("parallel",)), name=name,
    )(dao, *o_parts, proj)
    return outs[:np_], outs[np_]


def _loss_head(h2, gf, target, b, lp):
    d = h2.shape[1]
    nb = lp // BLOCK
    h3 = h2.reshape(b, lp, d)

    def body(h_ref, g_ref, t_ref, dh_ref, dg_ref, loss_ref):
        first = (pl.program_id(0) == 0) & (pl.program_id(1) == 0)

        @pl.when(first)
        def _():
            dg_ref[...] = jnp.zeros_like(dg_ref)
            loss_ref[...] = jnp.zeros_like(loss_ref)

        @pl.when(pl.program_id(1) == 0)
        def _():
            dh_ref[...] = jnp.zeros_like(dh_ref)

        @pl.when(pl.program_id(1) > 0)
        def _():
            x = h_ref[0]
            r = lax.rsqrt(jnp.mean(x * x, axis=-1, keepdims=True) + NORM_EPS)
            xr = x * r
            g = g_ref[...]
            diff = xr * g - t_ref[0]
            loss_ref[...] += 0.5 * jnp.sum(jnp.mean(diff * diff, axis=-1, keepdims=True))
            dy = diff * (1.0 / d)
            u = dy * g
            dh_ref[0] = r * (u - xr * jnp.mean(u * xr, axis=-1, keepdims=True))
            dg_ref[...] += jnp.sum(dy * xr, axis=0, keepdims=True)

    blk = pl.BlockSpec((1, BLOCK, d), lambda bi, n: (bi, n, 0))
    dh, dg, loss = pl.pallas_call(
        body, grid=(b, nb),
        in_specs=[blk, pl.BlockSpec((1, d), lambda bi, n: (0, 0)),
                  pl.BlockSpec((1, BLOCK, d), lambda bi, n: (bi, jnp.maximum(n - 1, 0), 0))],
        out_specs=[blk, pl.BlockSpec((1, d), lambda bi, n: (0, 0)), pl.BlockSpec((8, 128), lambda bi, n: (0, 0))],
        out_shape=[jax.ShapeDtypeStruct((b, lp, d), F32), jax.ShapeDtypeStruct((1, d), F32), jax.ShapeDtypeStruct((8, 128), F32)],
        compiler_params=_params(("arbitrary", "arbitrary")), name="loss_head",
    )(h3, gf, target)
    return dh.reshape(b * lp, d), dg, loss


def _iota2(shape, dim):
    return lax.broadcasted_iota(jnp.int32, shape, dim)


KEYS = 512
SB_FWD_KEYS = 512


def _lo_lanes():
    return _iota2((1, BLOCK), 1) < HEAD


def _halves(x, lo):
    zero = jnp.zeros_like(x)
    return jnp.where(lo, x, zero), jnp.where(lo, zero, x)


def _rows_of_pair(a, b):
    return jnp.where(_iota2((BLOCK, 1), 0) < HEAD, a, b)


def _split_rows_t(x):
    xt = x.T
    first = _iota2(xt.shape, 0) < HEAD
    zero = jnp.zeros_like(xt)
    return jnp.concatenate([jnp.where(first, xt, zero), jnp.where(first, zero, xt)], axis=1).astype(BF16)


def _key_chunk(c, lp, t_idx, strict, key_axis):
    keys = t_idx.shape[key_axis]
    first = c * keys
    s0 = pl.multiple_of(jnp.minimum(first, lp - keys), BLOCK)
    s_idx = s0 + _iota2(t_idx.shape, key_axis)
    seen = (s_idx < t_idx) if strict else (s_idx <= t_idx)
    return s0, seen & (s_idx >= jnp.maximum(first, N_PAD))


def _tri_dot(x, tri):
    return jnp.dot(x.astype(BF16), tri, preferred_element_type=F32)


def _stack_halves(x, lo):
    a, b = _halves(x, lo)
    return jnp.concatenate([a, b], axis=0)


def _pair(a, b, lo):
    return jnp.where(lo, a, b)


def _chunk_starts(lp):
    return [min(c * KEYS, lp - KEYS) for c in range(-(-lp // KEYS))]


def _put_rows(ref, r0, bq, a, b):
    for t in range(bq // BLOCK):
        part = slice(t * BLOCK, (t + 1) * BLOCK)
        ref[0, 0, r0 // BLOCK + t] = jnp.concatenate([a[:, part], b[:, part], jnp.zeros((6, BLOCK), F32)], axis=0)


def _get_rows(ref, r0, bq):
    return [jnp.concatenate([ref[0, 0, r0 // BLOCK + t, h:h + 1, :] for t in range(bq // BLOCK)], axis=1) for h in range(2)]


def _n_chunks(i, keys):
    return ((i + 1) * BLOCK + keys - 1) // keys


QROWS = 512


def _for_query_tiles(nb, tile, keys=KEYS):
    per = QROWS // BLOCK

    def step(j, _):
        tile(pl.multiple_of(j * QROWS, QROWS), QROWS, (j + 1) * (QROWS // keys))
        return 0

    lax.fori_loop(0, nb // per, step, 0)
    for i in range(nb - nb % per, nb):
        tile(i * BLOCK, BLOCK, _n_chunks(i, keys))


def _walk_chunks(r0, n, chunk, carry, leftwards=False, keys=KEYS):
    diag = jnp.maximum(r0 // keys, 1)

    def span(first, last, masked, carry):
        def step(t, cr):
            return chunk(last - 1 - t if leftwards else first + t, cr, masked)
        return lax.fori_loop(0, last - first, step, carry)

    spans = [(0, 1, True), (1, diag, False), (diag, n, True)]
    for first, last, masked in (reversed(spans) if leftwards else spans):
        carry = span(first, last, masked, carry)
    return carry


def _where(valid, x, other):
    return x if valid is None else jnp.where(valid, x, other)


HEAD_SCALE = HEAD ** -0.5
assert math.frexp(HEAD_SCALE)[0] == 0.5


def _sb_scores(q_h, k, valid, after):
    z = _dot_nt(q_h, k)
    lb = jnp.minimum(z, 0.0) - jnp.log(1.0 + jnp.exp(-jnp.abs(z)))
    l1m_all = lb - z
    l1m = _where(valid, l1m_all, 0.0)
    return lb, l1m_all, l1m, _tri_dot(l1m, after)


def _pair_stat_spec(nb):
    return pl.BlockSpec((1, 1, nb, 8, BLOCK), lambda bi, hp: (bi, hp, 0, 0, 0))


def _sb_fwd(proj3, name):
    b, lp, _ = proj3.shape
    nb = lp // BLOCK
    npair = SB_HEADS // 2

    keys = SB_FWD_KEYS

    def body(q_ref, k_ref, v_ref, o_ref, tot_ref):
        lo = _lo_lanes()
        after = (_iota2((keys, keys), 0) > _iota2((keys, keys), 1)).astype(BF16)

        def qtile(r0, bq, n):
            qs = _halves((q_ref[0, pl.ds(r0, bq), :] * HEAD_SCALE).astype(BF16), lo)
            t_idx = r0 + _iota2((bq, keys), 0)

            def kchunk(c, carry, masked):
                cs, acc = carry[:2], carry[2]
                s0, valid = _key_chunk(c, lp, t_idx, True, 1)
                valid = valid if masked else None
                k = k_ref[0, pl.ds(s0, keys), :].astype(BF16)
                a_s, new = [], []
                for h in range(2):
                    lb, _, l1m, suf = _sb_scores(qs[h], k, valid, after)
                    a_s.append(_where(valid, jnp.exp(lb + suf + cs[h]), 0.0).astype(BF16))
                    new.append(cs[h] + jnp.sum(l1m, axis=1, keepdims=True))
                v_bd = _stack_halves(v_ref[0, pl.ds(s0, keys), :].astype(BF16), lo)
                return (*new, acc + jnp.dot(jnp.concatenate(a_s, axis=1), v_bd, preferred_element_type=F32))

            zero = jnp.zeros((bq, 1), F32)
            c_a, c_b, acc = _walk_chunks(r0, n, kchunk, (zero, zero, jnp.zeros((bq, BLOCK), F32)), leftwards=True, keys=keys)
            o_ref[0, pl.ds(r0, bq), :] = acc
            tot_ref[0, pl.ds(r0, bq), :] = jnp.broadcast_to(_pair(c_a, c_b, lo), (bq, BLOCK))

        _for_query_tiles(nb, qtile, keys)

    def col(first):
        return pl.BlockSpec((1, lp, 2 * HEAD), lambda bi, hp: (bi, 0, first // (2 * HEAD) + hp))

    shp = jax.ShapeDtypeStruct((b, lp, SB_HEADS * HEAD), F32)
    return pl.pallas_call(
        body, grid=(b, npair), in_specs=[col(EV_Q), col(EV_K), col(EV_V)], out_specs=[col(0), col(0)], out_shape=[shp, shp],
        compiler_params=_params(("parallel", "parallel")), name=name,
    )(proj3, proj3, proj3)


def _sb_bwd(proj3, tot, do, name):
    b, lp, _ = proj3.shape
    nb = lp // BLOCK
    npair = SB_HEADS // 2

    def body(q_ref, k_ref, v_ref, tot_ref, do_ref, dq_ref, dk_ref, dv_ref):
        lo = _lo_lanes()
        after = (_iota2((KEYS, KEYS), 0) > _iota2((KEYS, KEYS), 1)).astype(BF16)
        before = (_iota2((KEYS, KEYS), 0) < _iota2((KEYS, KEYS), 1)).astype(BF16)
        dk_ref[...] = jnp.zeros_like(dk_ref)
        dv_ref[...] = jnp.zeros_like(dv_ref)

        def qtile(r0, bq, n):
            rows = pl.ds(r0, bq)
            qs = _halves((q_ref[0, rows, :] * HEAD_SCALE).astype(BF16), lo)
            dos = _halves(do_ref[0, rows, :].astype(BF16), lo)
            tot_i = tot_ref[0, rows, :]
            tots = (tot_i[:, 0:1], tot_i[:, HEAD:HEAD + 1])
            q_st, do_st = jnp.concatenate(qs, axis=0), jnp.concatenate(dos, axis=0)
            t_idx = r0 + _iota2((bq, KEYS), 0)

            def kchunk(c, carry, masked):
                s0, valid = _key_chunk(c, lp, t_idx, True, 1)
                valid = valid if masked else None
                keys = pl.ds(s0, KEYS)
                k = k_ref[0, keys, :].astype(BF16)
                v = v_ref[0, keys, :].astype(BF16)
                a_s, dzs, new = [], [], []
                for h in range(2):
                    left, pre = carry[2 * h], carry[2 * h + 1]
                    lb, l1m_all, l1m, suf = _sb_scores(qs[h], k, valid, after)
                    here = jnp.sum(l1m, axis=1, keepdims=True)
                    a = _where(valid, jnp.exp(lb + suf + (tots[h] - left - here)), 0.0)
                    w = a * _dot_nt(dos[h], v)
                    dz = _where(valid, w * jnp.exp(l1m_all) - (pre + _tri_dot(w, before)) * jnp.exp(lb), 0.0)
                    new += [left + here, pre + jnp.sum(w, axis=1, keepdims=True)]
                    a_s.append(a.astype(BF16))
                    dzs.append(dz.astype(BF16))
                dk_ref[0, keys, :] += _dot_tn(jnp.concatenate(dzs, axis=0), q_st)
                dv_ref[0, keys, :] += _dot_tn(jnp.concatenate(a_s, axis=0), do_st)
                dq = carry[4] + jnp.dot(jnp.concatenate(dzs, axis=1), _stack_halves(k, lo), preferred_element_type=F32)
                return (*new, dq)

            zero = jnp.zeros((bq, 1), F32)
            out = _walk_chunks(r0, n, kchunk, (zero, zero, zero, zero, jnp.zeros((bq, BLOCK), F32)))
            dq_ref[0, rows, :] = out[4] * HEAD_SCALE

        _for_query_tiles(nb, qtile)

    def col(first):
        return pl.BlockSpec((1, lp, 2 * HEAD), lambda bi, hp: (bi, 0, first // (2 * HEAD) + hp))

    shp = jax.ShapeDtypeStruct((b, lp, SB_HEADS * HEAD), F32)
    return pl.pallas_call(
        body, grid=(b, npair), in_specs=[col(EV_Q), col(EV_K), col(EV_V), col(0), col(0)], out_specs=[col(0)] * 3, out_shape=[shp] * 3,
        compiler_params=_params(("parallel", "parallel")), name=name,
    )(proj3, proj3, proj3, tot, do)


def _rope_tables(lp):
    half = MLA_ROPE // 2
    pos = (np.arange(lp) - N_PAD).astype(np.float32)
    inv = jnp.asarray(ROPE_BASE, F32) ** (-jnp.arange(half, dtype=F32) / half)
    ang = jnp.asarray(pos)[:, None] * inv[None, :]
    cos, sin = jnp.cos(ang), jnp.sin(ang)
    zeros = lambda n: jnp.zeros((lp, n), F32)
    c = jnp.concatenate([jnp.ones((lp, MLA_NOPE), F32), cos, cos, zeros(32)], axis=1)
    s1 = jnp.concatenate([zeros(MLA_NOPE), -sin, zeros(half), zeros(32)], axis=1)
    s2 = jnp.concatenate([zeros(MLA_NOPE), zeros(half), sin, zeros(32)], axis=1)
    return c, s1, s2


def _rope(x, c, s1, s2):
    half = MLA_ROPE // 2
    return x * c + pltpu.roll(x, BLOCK - half, 1) * s1 + pltpu.roll(x, half, 1) * s2


def _rope_t(dy, c, s1, s2):
    half = MLA_ROPE // 2
    return dy * c + pltpu.roll(dy * s1, half, 1) + pltpu.roll(dy * s2, BLOCK - half, 1)


def _rms_rows(x, g):
    r = lax.rsqrt(jnp.mean(x * x, axis=-1, keepdims=True) + NORM_EPS)
    return x * r, r


def _prep_rows(lp):
    return lp // 4 if lp % 64 == 0 else BLOCK


def _mla_prep_fwd(proj3, gq, gkv, wq, wk, wv, tabs, name):
    b, lp, _ = proj3.shape
    rows = _prep_rows(lp)
    hw = MLA_HEADS * BLOCK

    def body(cq_ref, ckv_ref, kr_ref, gq_ref, gkv_ref, wq_ref, wk_ref, wv_ref, c_ref, s1_ref, s2_ref, qf_ref, kf_ref, v_ref):
        c, s1, s2 = c_ref[...], s1_ref[...], s2_ref[...]
        xq, _ = _rms_rows(cq_ref[0], None)
        qh = _dot(xq * gq_ref[...], wq_ref[...])
        xk, _ = _rms_rows(ckv_ref[0], None)
        ckv_n = xk * gkv_ref[...]
        kv = _dot(ckv_n, wk_ref[...])
        v_ref[0] = _dot(ckv_n, wv_ref[...]).astype(v_ref.dtype)
        kr = _rope(kr_ref[0], c, s1, s2)
        for h in range(MLA_HEADS):
            ls = slice(h * BLOCK, (h + 1) * BLOCK)
            qf_ref[0, :, ls] = _rope(qh[:, ls], c, s1, s2).astype(qf_ref.dtype)
            kf_ref[0, :, ls] = (kv[:, ls] + kr).astype(kf_ref.dtype)

    def col(first, width):
        return pl.BlockSpec((1, rows, width), lambda bi, n: (bi, n, first // width))

    def whole(a):
        return pl.BlockSpec(a.shape, lambda bi, n: (0,) * a.ndim)

    tab = pl.BlockSpec((rows, BLOCK), lambda bi, n: (n, 0))
    return pl.pallas_call(
        body, grid=(b, lp // rows),
        in_specs=[col(EV_CQ, MLA_Q_LORA), col(EV_CKV, MLA_KV_LORA), col(EV_KR, BLOCK), whole(gq), whole(gkv), whole(wq), whole(wk),
                  whole(wv), tab, tab, tab],
        out_specs=[col(0, hw), col(0, hw), col(0, MLA_HEADS * HEAD)],
        out_shape=[jax.ShapeDtypeStruct((b, lp, hw), BF16), jax.ShapeDtypeStruct((b, lp, hw), BF16),
                   jax.ShapeDtypeStruct((b, lp, MLA_HEADS * HEAD), BF16)],
        compiler_params=_params(("parallel", "parallel")), name=name,
    )(proj3, proj3, proj3, gq, gkv, wq, wk, wv, *tabs)


def _mla_prep_bwd(proj3, gq, gkv, wq, wk, wv, tabs, dqf, dkf, dv, name):
    b, lp, _ = proj3.shape
    rows = _prep_rows(lp)
    hw = MLA_HEADS * BLOCK

    def body(cq_ref, ckv_ref, gq_ref, gkv_ref, wq_ref, wk_ref, wv_ref, c_ref, s1_ref, s2_ref, dqf_ref, dkf_ref, dv_ref,
             dcq_ref, dckv_ref, dkr_ref, dwq_ref, dwk_ref, dwv_ref, dgq_ref, dgkv_ref, dqh):
        @pl.when((pl.program_id(0) == 0) & (pl.program_id(1) == 0))
        def _():
            for r in (dwq_ref, dwk_ref, dwv_ref, dgq_ref, dgkv_ref):
                r[...] = jnp.zeros_like(r)

        c, s1, s2 = c_ref[...], s1_ref[...], s2_ref[...]
        dkr = jnp.zeros((rows, BLOCK), F32)
        for h in range(MLA_HEADS):
            ls = slice(h * BLOCK, (h + 1) * BLOCK)
            dqh[:, ls] = _rope_t(dqf_ref[0, :, ls].astype(F32), c, s1, s2).astype(dqh.dtype)
            dkr = dkr + dkf_ref[0, :, ls].astype(F32)
        dkr_ref[0] = _rope_t(dkr, c, s1, s2).astype(dkr_ref.dtype)

        def norm_bwd(x, g, dy, dg_ref):
            xr, r = _rms_rows(x, None)
            u = dy * g
            dg_ref[...] += jnp.sum(dy * xr, axis=0, keepdims=True)
            return r * (u - xr * jnp.mean(u * xr, axis=-1, keepdims=True))

        xq, _ = _rms_rows(cq_ref[0], None)
        cq_n = xq * gq_ref[...]
        dwq_ref[...] += _dot_tn(cq_n, dqh[...])
        dcq_ref[0] = norm_bwd(cq_ref[0], gq_ref[...], _dot_nt(dqh[...], wq_ref[...]), dgq_ref).astype(dcq_ref.dtype)
        xk, _ = _rms_rows(ckv_ref[0], None)
        ckv_n = xk * gkv_ref[...]
        dkf_, dv_ = dkf_ref[0], dv_ref[0]
        dwk_ref[...] += _dot_tn(ckv_n, dkf_)
        dwv_ref[...] += _dot_tn(ckv_n, dv_)
        dckv_n = _dot_nt(dkf_, wk_ref[...]) + _dot_nt(dv_, wv_ref[...])
        dckv_ref[0] = norm_bwd(ckv_ref[0], gkv_ref[...], dckv_n, dgkv_ref).astype(dckv_ref.dtype)

    def col(first, width):
        return pl.BlockSpec((1, rows, width), lambda bi, n: (bi, n, first // width))

    def whole(a):
        return pl.BlockSpec(a.shape, lambda bi, n: (0,) * len(a.shape))

    tab = pl.BlockSpec((rows, BLOCK), lambda bi, n: (n, 0))
    acc_shapes = [jax.ShapeDtypeStruct(a.shape, F32) for a in (wq, wk, wv, gq, gkv)]
    return pl.pallas_call(
        body, grid=(b, lp // rows),
        in_specs=[col(EV_CQ, MLA_Q_LORA), col(EV_CKV, MLA_KV_LORA), whole(gq), whole(gkv), whole(wq), whole(wk), whole(wv), tab, tab, tab,
                  col(0, hw), col(0, hw), col(0, MLA_HEADS * HEAD)],
        out_specs=[col(0, MLA_Q_LORA), col(0, MLA_KV_LORA), col(0, BLOCK)] + [whole(a) for a in acc_shapes],
        out_shape=[jax.ShapeDtypeStruct((b, lp, MLA_Q_LORA), BF16), jax.ShapeDtypeStruct((b, lp, MLA_KV_LORA), BF16),
                   jax.ShapeDtypeStruct((b, lp, BLOCK), BF16)] + acc_shapes,
        scratch_shapes=[pltpu.VMEM((rows, hw), BF16)],
        compiler_params=_params(("arbitrary", "arbitrary")), name=name,
    )(proj3, proj3, gq, gkv, wq, wk, wv, *tabs, dqf, dkf, dv)


def _mla_fwd(qf, kf, v, name):
    b, lp, _ = qf.shape
    nb = lp // BLOCK
    npair = MLA_HEADS // 2
    scale = (MLA_NOPE + MLA_ROPE) ** -0.5
    starts = _chunk_starts(lp)

    def body(q_ref, k_ref, v_ref, o_ref, lse_ref, vt_ref):
        for c, s0 in enumerate(starts):
            vt_ref[c] = _split_rows_t(v_ref[0, s0:s0 + KEYS, :].astype(F32))

        def qtile(r0, bq, n):
            qs = [q_ref[0, pl.ds(r0, bq), h * BLOCK:(h + 1) * BLOCK] for h in range(2)]
            t_idx = r0 + _iota2((KEYS, bq), 1)

            def kchunk(c, carry, masked):
                stats, acc = carry[:4], carry[4]
                s0, valid = _key_chunk(c, lp, t_idx, False, 0)
                valid = valid if masked else None
                ps, new, alphas = [], [], []
                for h in range(2):
                    m, l = stats[2 * h], stats[2 * h + 1]
                    s = _where(valid, _dot_nt(k_ref[0, pl.ds(s0, KEYS), h * BLOCK:(h + 1) * BLOCK], qs[h]) * scale, NEG)
                    m_new = jnp.maximum(m, jnp.max(s, axis=0, keepdims=True))
                    p = _where(valid, jnp.exp(s - m_new), 0.0)
                    alpha = jnp.exp(m - m_new)
                    new += [m_new, alpha * l + jnp.sum(p, axis=0, keepdims=True)]
                    alphas.append(alpha)
                    ps.append(p.astype(BF16))
                pv = jnp.dot(vt_ref[c], jnp.concatenate(ps, axis=0), preferred_element_type=F32)
                return (*new, _rows_of_pair(alphas[0], alphas[1]) * acc + pv)

            neg, zero = jnp.full((1, bq), NEG, F32), jnp.zeros((1, bq), F32)
            m_a, l_a, m_b, l_b, acc = _walk_chunks(r0, n, kchunk, (neg, zero, neg, zero, jnp.zeros((BLOCK, bq), F32)))
            safe = [jnp.where(l > 0.0, l, 1.0) for l in (l_a, l_b)]
            o_ref[0, pl.ds(r0, bq), :] = (acc / _rows_of_pair(safe[0], safe[1])).T
            lse = [jnp.where(l > 0.0, m + jnp.log(sf), 0.0) for m, l, sf in ((m_a, l_a, safe[0]), (m_b, l_b, safe[1]))]
            _put_rows(lse_ref, r0, bq, lse[0], lse[1])

        _for_query_tiles(nb, qtile)

    wide = pl.BlockSpec((1, lp, 2 * BLOCK), lambda bi, hp: (bi, 0, hp))
    thin = pl.BlockSpec((1, lp, 2 * HEAD), lambda bi, hp: (bi, 0, hp))
    return pl.pallas_call(
        body, grid=(b, npair), in_specs=[wide, wide, thin], out_specs=[thin, _pair_stat_spec(nb)],
        out_shape=[jax.ShapeDtypeStruct((b, lp, MLA_HEADS * HEAD), F32), jax.ShapeDtypeStruct((b, npair, nb, 8, BLOCK), F32)],
        scratch_shapes=[pltpu.VMEM((len(starts), BLOCK, 2 * KEYS), BF16)],
        compiler_params=_params(("parallel", "parallel")), name=name,
    )(qf, kf, v)


def _mla_bwd(qf, kf, v, o, lse, do, name):
    b, lp, _ = qf.shape
    nb = lp // BLOCK
    npair = MLA_HEADS // 2
    scale = (MLA_NOPE + MLA_ROPE) ** -0.5

    starts = _chunk_starts(lp)

    def body(q_ref, k_ref, v_ref, o_ref, lse_ref, do_ref, dq_ref, dk_ref, dv_ref, kt_ref):
        lo = _lo_lanes()
        dk_ref[...] = jnp.zeros_like(dk_ref)
        dv_ref[...] = jnp.zeros_like(dv_ref)
        for c, s0 in enumerate(starts):
            for h in range(2):
                kt_ref[c, h] = k_ref[0, s0:s0 + KEYS, h * BLOCK:(h + 1) * BLOCK].astype(F32).T.astype(BF16)

        def qtile(r0, bq, n):
            rows = pl.ds(r0, bq)
            qs = [q_ref[0, rows, h * BLOCK:(h + 1) * BLOCK] for h in range(2)]
            do_i = do_ref[0, rows, :]
            dos = _halves(do_i.astype(BF16), lo)
            do_st = jnp.concatenate(dos, axis=0)
            both = (do_i * o_ref[0, rows, :]).T
            dsum = (jnp.sum(both[:HEAD], axis=0, keepdims=True), jnp.sum(both[HEAD:], axis=0, keepdims=True))
            lses = _get_rows(lse_ref, r0, bq)
            t_idx = r0 + _iota2((KEYS, bq), 1)

            def kchunk(c, dqts, masked):
                s0, valid = _key_chunk(c, lp, t_idx, False, 0)
                valid = valid if masked else None
                keys = pl.ds(s0, KEYS)
                v_c = v_ref[0, keys, :]
                ps, out = [], []
                for h in range(2):
                    lanes = slice(h * BLOCK, (h + 1) * BLOCK)
                    s = _dot_nt(k_ref[0, keys, lanes], qs[h]) * scale
                    p = _where(valid, jnp.exp(s - lses[h]), 0.0)
                    ds = (p * (_dot_nt(v_c, dos[h]) - dsum[h]) * scale).astype(BF16)
                    dk_ref[0, keys, lanes] += jnp.dot(ds, qs[h], preferred_element_type=F32)
                    out.append(dqts[h] + jnp.dot(kt_ref[c, h], ds, preferred_element_type=F32))
                    ps.append(p.astype(BF16))
                dv_ref[0, keys, :] += jnp.dot(jnp.concatenate(ps, axis=1), do_st, preferred_element_type=F32)
                return tuple(out)

            zero = jnp.zeros((BLOCK, bq), F32)
            dq_a, dq_b = _walk_chunks(r0, n, kchunk, (zero, zero))
            dq_ref[0, rows, 0:BLOCK] = dq_a.T
            dq_ref[0, rows, BLOCK:2 * BLOCK] = dq_b.T

        _for_query_tiles(nb, qtile)

    wide = pl.BlockSpec((1, lp, 2 * BLOCK), lambda bi, hp: (bi, 0, hp))
    thin = pl.BlockSpec((1, lp, 2 * HEAD), lambda bi, hp: (bi, 0, hp))
    return pl.pallas_call(
        body, grid=(b, npair), in_specs=[wide, wide, thin, thin, _pair_stat_spec(nb), thin], out_specs=[wide, wide, thin],
        out_shape=[jax.ShapeDtypeStruct(qf.shape, F32), jax.ShapeDtypeStruct(qf.shape, F32), jax.ShapeDtypeStruct(v.shape, F32)],
        scratch_shapes=[pltpu.VMEM((len(starts), 2, BLOCK, KEYS), BF16)],
        compiler_params=_params(("parallel", "parallel")), name=name,
    )(qf, kf, v, o, lse, do)


SWA_KEYS = 2 * BLOCK + N_META


def _swa_keys(k_ref, v_ref, n, kv):
    prev = jnp.maximum(n - 1, 0)
    rows = lambda blk: pl.ds(pl.multiple_of(blk * BLOCK, BLOCK), BLOCK)
    mine = (_iota2((1, BLOCK), 1) >= HEAD).astype(jnp.int32) == kv

    def both_halves(ref):
        x = jnp.concatenate([ref[0, rows(prev), :], ref[0, rows(n), :], ref[0, N_PAD:BLOCK, :]], axis=0)
        return jnp.where(mine, x, pltpu.roll(x, HEAD, 1))

    slot = _iota2((SWA_KEYS, BLOCK), 0)
    s_idx = jnp.where(slot < 2 * BLOCK, (n - 1) * BLOCK + slot, slot - 2 * BLOCK + N_PAD)
    dist = n * BLOCK + _iota2((SWA_KEYS, BLOCK), 1) - s_idx
    band = (slot < 2 * BLOCK) & (dist >= 0) & (dist < SWA_WINDOW) & (s_idx >= BLOCK)
    meta = (slot >= 2 * BLOCK) & (dist >= 0)
    return both_halves(k_ref), both_halves(v_ref), band | meta, dist.astype(F32), prev


def _pad_keys(x):
    return jnp.concatenate([x, jnp.zeros((3 * BLOCK - SWA_KEYS, x.shape[1]), x.dtype)], axis=0)


def _swa_probs(q_h, kdup, valid, dist, head, sink_ref):
    slope = jnp.exp(jnp.full((1, 1), -8.0 * math.log(2.0) / SWA_HEADS, F32) * (head + 1).astype(F32))
    s = jnp.where(valid, _dot_nt(kdup, q_h) - slope * dist, NEG)
    sink = sink_ref[pl.ds(head, 1), 0:1]
    m = jnp.maximum(jnp.max(s, axis=0, keepdims=True), sink)
    e = jnp.where(valid, jnp.exp(s - m), 0.0)
    es = jnp.exp(sink - m)
    inv = 1.0 / (jnp.sum(e, axis=0, keepdims=True) + es)
    return e * inv, es * inv


SWA_PAIRS = SWA_HEADS // SWA_KV_HEADS // 2
SWA_GROUP = SWA_PAIRS * 2 * HEAD


def _swa_specs(b, lp):
    nb = lp // BLOCK
    qcol = lambda first: pl.BlockSpec((1, BLOCK, SWA_GROUP), lambda bi, kv, n: (bi, n, first // SWA_GROUP + kv))
    kcol = lambda first: pl.BlockSpec((1, lp, BLOCK), lambda bi, kv, n: (bi, 0, first // BLOCK))
    sink = pl.BlockSpec((SWA_HEADS, BLOCK), lambda bi, kv, n: (0, 0))
    return (b, SWA_KV_HEADS, nb), qcol, kcol, sink


def _swa_fwd(proj3, sinks, name):
    b, lp, _ = proj3.shape
    grid, qcol, kcol, sink = _swa_specs(b, lp)

    def body(q_ref, k_ref, v_ref, sink_ref, o_ref):
        kv, n = pl.program_id(1), pl.program_id(2)
        lo = _lo_lanes()
        kdup, vdup, valid, dist, _ = _swa_keys(k_ref, v_ref, n, kv)
        kdup = kdup.astype(BF16)
        vt = _split_rows_t(_pad_keys(vdup))
        for p in range(SWA_PAIRS):
            lanes = slice(p * BLOCK, (p + 1) * BLOCK)
            qs = _halves((q_ref[0, :, lanes] * HEAD_SCALE).astype(BF16), lo)
            probs = [_swa_probs(qs[hh], kdup, valid, dist, (kv * SWA_PAIRS + p) * 2 + hh, sink_ref)[0].astype(BF16) for hh in range(2)]
            o_ref[0, :, lanes] = jnp.dot(vt, jnp.concatenate([_pad_keys(pr) for pr in probs], axis=0), preferred_element_type=F32).T

    return pl.pallas_call(
        body, grid=grid, in_specs=[qcol(OD_Q), kcol(OD_K), kcol(OD_V), sink], out_specs=qcol(0),
        out_shape=jax.ShapeDtypeStruct((b, lp, SWA_HEADS * HEAD), F32),
        compiler_params=_params(("parallel", "parallel", "parallel")), name=name,
    )(proj3, proj3, proj3, sinks)


def _swa_bwd(proj3, sinks, do, name):
    b, lp, _ = proj3.shape
    nb = lp // BLOCK
    grid, qcol, kcol, sink = _swa_specs(b, lp)

    def body(q_ref, k_ref, v_ref, sink_ref, do_ref, dq_ref, dk_ref, dv_ref, dsink_ref, dk_acc, dv_acc):
        kv, n = pl.program_id(1), pl.program_id(2)

        @pl.when((n == 0) & (pl.program_id(0) == 0) & (kv == 0))
        def _():
            dsink_ref[...] = jnp.zeros_like(dsink_ref)

        @pl.when(n == 0)
        def _():
            dk_acc[...] = jnp.zeros_like(dk_acc)
            dv_acc[...] = jnp.zeros_like(dv_acc)

        lo = _lo_lanes()
        kdup, vdup, valid, dist, prev = _swa_keys(k_ref, v_ref, n, kv)
        kt = _split_rows_t(_pad_keys(kdup))
        kdup, vdup = kdup.astype(BF16), vdup.astype(BF16)
        dkc = jnp.zeros((SWA_KEYS, BLOCK), F32)
        dvc = jnp.zeros((SWA_KEYS, BLOCK), F32)
        for p in range(SWA_PAIRS):
            lanes = slice(p * BLOCK, (p + 1) * BLOCK)
            qs = _halves((q_ref[0, :, lanes] * HEAD_SCALE).astype(BF16), lo)
            dos = _halves(do_ref[0, :, lanes].astype(BF16), lo)
            dss, prs = [], []
            for hh in range(2):
                head = (kv * SWA_PAIRS + p) * 2 + hh
                pr, ps = _swa_probs(qs[hh], kdup, valid, dist, head, sink_ref)
                dp = _dot_nt(vdup, dos[hh])
                dsum = jnp.sum(pr * dp, axis=0, keepdims=True)
                dsink_ref[pl.ds(head, 1), :] += jnp.broadcast_to(-jnp.sum(ps * dsum, axis=1, keepdims=True), (1, BLOCK))
                dss.append((pr * (dp - dsum)).astype(BF16))
                prs.append(pr.astype(BF16))
            dq_ref[0, :, lanes] = jnp.dot(kt, jnp.concatenate([_pad_keys(d) for d in dss], axis=0), preferred_element_type=F32).T * HEAD_SCALE
            dkc = dkc + jnp.dot(jnp.concatenate(dss, axis=1), jnp.concatenate(qs, axis=0), preferred_element_type=F32)
            dvc = dvc + jnp.dot(jnp.concatenate(prs, axis=1), jnp.concatenate(dos, axis=0), preferred_element_type=F32)
        rows = lambda blk: pl.ds(pl.multiple_of(blk * BLOCK, BLOCK), BLOCK)
        for r, part in ((rows(prev), slice(0, BLOCK)), (rows(n), slice(BLOCK, 2 * BLOCK)), (slice(N_PAD, BLOCK), slice(2 * BLOCK, SWA_KEYS))):
            dk_acc[r, :] += dkc[part]
            dv_acc[r, :] += dvc[part]

        for acc, ref in ((dk_acc, dk_ref), (dv_acc, dv_ref)):
            @pl.when((n == nb - 1) & (kv == 0))
            def _():
                x = acc[...]
                ref[0] = x + pltpu.roll(x, HEAD, 1)

            @pl.when((n == nb - 1) & (kv == 1))
            def _():
                x = acc[...]
                ref[0] = jnp.where(lo, ref[0], x + pltpu.roll(x, HEAD, 1))

    kvout = pl.BlockSpec((1, lp, BLOCK), lambda bi, kv, n: (bi, 0, 0))
    kvshape = jax.ShapeDtypeStruct((b, lp, BLOCK), F32)
    return pl.pallas_call(
        body, grid=grid, in_specs=[qcol(OD_Q), kcol(OD_K), kcol(OD_V), sink, qcol(0)], out_specs=[qcol(0), kvout, kvout, sink],
        out_shape=[jax.ShapeDtypeStruct((b, lp, SWA_HEADS * HEAD), F32), kvshape, kvshape, jax.ShapeDtypeStruct((SWA_HEADS, BLOCK), F32)],
        scratch_shapes=[pltpu.VMEM((lp, BLOCK), F32), pltpu.VMEM((lp, BLOCK), F32)],
        compiler_params=_params(("arbitrary", "arbitrary", "arbitrary")), name=name,
    )(proj3, proj3, proj3, sinks, do)


def _kernel_weights(ev_w_in, ev_w_uq, ev_w_ukv, od_w_in):
    zeros = lambda r, c: jnp.zeros((r, c), ev_w_in.dtype)
    q_sb, k_sb, v_sb, g_sb, c_q, c_kv, k_r, g_mla = jnp.split(ev_w_in, [512, 1024, 1536, 2048, 2304, 2432, 2464], axis=1)
    w0 = jnp.concatenate([g_sb, g_mla, q_sb, k_sb, v_sb, c_q, c_kv, zeros(D_MODEL, MLA_NOPE), k_r, zeros(D_MODEL, 32)], axis=1)
    uq = ev_w_uq.reshape(MLA_Q_LORA, MLA_HEADS, MLA_NOPE + MLA_ROPE)
    wq = jnp.pad(uq, ((0, 0), (0, 0), (0, BLOCK - MLA_NOPE - MLA_ROPE))).reshape(MLA_Q_LORA, MLA_HEADS * BLOCK)
    ukv = ev_w_ukv.reshape(MLA_KV_LORA, MLA_HEADS, BLOCK)
    wk = jnp.pad(ukv[:, :, :MLA_NOPE], ((0, 0), (0, 0), (0, BLOCK - MLA_NOPE))).reshape(MLA_KV_LORA, MLA_HEADS * BLOCK)
    wv = ukv[:, :, MLA_NOPE:].reshape(MLA_KV_LORA, MLA_HEADS * HEAD)
    q, k, v, g = jnp.split(od_w_in, [1024, 1152, 1280], axis=1)
    w1 = jnp.concatenate([g, q, k, v], axis=1)
    return w0, wq, wk, wv, w1


def _od_w_in_grad(dw1):
    sl = lambda a, first, n: a[:, first:first + n]
    return jnp.concatenate([sl(dw1, OD_Q, 1024), sl(dw1, OD_K, 128), sl(dw1, OD_V, 128), sl(dw1, OD_G, 1024)], axis=1)


def _original_grads(dw0, dwq, dwk, dwv):
    sl = lambda a, first, n: a[:, first:first + n]
    d_ev_w_in = jnp.concatenate([sl(dw0, EV_Q, 512), sl(dw0, EV_K, 512), sl(dw0, EV_V, 512), sl(dw0, EV_G, 512), sl(dw0, EV_CQ, 256),
                                 sl(dw0, EV_CKV, 128), sl(dw0, EV_KR + MLA_NOPE, MLA_ROPE), sl(dw0, EV_G + 512, 512)], axis=1)
    d_uq = dwq.reshape(MLA_Q_LORA, MLA_HEADS, BLOCK)[:, :, :MLA_NOPE + MLA_ROPE].reshape(MLA_Q_LORA, -1)
    d_ukv = jnp.concatenate([dwk.reshape(MLA_KV_LORA, MLA_HEADS, BLOCK)[:, :, :MLA_NOPE], dwv.reshape(MLA_KV_LORA, MLA_HEADS, HEAD)],
                            axis=2).reshape(MLA_KV_LORA, -1)
    return d_ev_w_in, d_uq, d_ukv


def _meta_rows_sum(dh0_3):
    b, _, d = dh0_3.shape

    def body(x_ref, o_ref):
        acc = x_ref[0, N_PAD:BLOCK, :]
        for i in range(1, b):
            acc = acc + x_ref[i, N_PAD:BLOCK, :]
        o_ref[...] = acc

    return pl.pallas_call(
        body, grid=(1,), in_specs=[pl.BlockSpec((b, BLOCK, d), lambda i: (0, 0, 0))], out_specs=pl.BlockSpec((N_META, d), lambda i: (0, 0)),
        out_shape=jax.ShapeDtypeStruct((N_META, d), F32), compiler_params=_params(("arbitrary",)), name="meta_rows_sum",
    )(dh0_3)


def _local_step(x, meta, norm_g, final_g, gq, gkv, sinks, target, ev_w_in, ev_w_uq, ev_w_ukv, wo0, od_w_in, wo1, reduce_early=None,
                reduce_in_proj=None):
    b, seq, d = x.shape
    lp = seq + BLOCK
    t = b * lp
    w0, wq, wk, wv, w1 = _kernel_weights(ev_w_in, ev_w_uq, ev_w_ukv, od_w_in)
    h0 = jnp.concatenate([jnp.zeros((b, N_PAD, d), F32), jnp.broadcast_to(meta[None], (b, N_META, d)), x], axis=1).reshape(t, d)
    tabs = _rope_tables(lp)
    g0, g1 = norm_g[0:1], norm_g[1:2]

    hn0 = _rms_fwd(h0, g0, "norm0")
    proj0 = _mm(hn0, w0, "inproj0")
    p0 = proj0.reshape(b, lp, EV_N)
    o_sb, sb_tot = _sb_fwd(p0, "sb_fwd")
    qf, kf, v = _mla_prep_fwd(p0, gq, gkv, wq, wk, wv, tabs, "mla_prep_fwd")
    o_mla, lse = _mla_fwd(qf, kf, v, "mla_fwd")
    o0 = [o_sb.reshape(t, -1), o_mla.reshape(t, -1)]
    ao0 = _gate_fwd(o0, proj0, "gate0")
    h1 = _mm(ao0, wo0, "outproj0", res=h0)

    hn1 = _rms_fwd(h1, g1, "norm1")
    proj1 = _mm(hn1, w1, "inproj1")
    p1 = proj1.reshape(b, lp, OD_N)
    sinks_b = jnp.broadcast_to(sinks.reshape(SWA_HEADS, 1), (SWA_HEADS, BLOCK))
    o1 = _swa_fwd(p1, sinks_b, "swa_fwd").reshape(t, -1)
    ao1 = _gate_fwd([o1], proj1, "gate1")
    h2 = _mm(ao1, wo1, "outproj1", res=h1)

    dh2, d_final_g, loss = _loss_head(h2, final_g.reshape(1, d), target, b, lp)

    d_wo1 = _mm_tn(ao1, dh2, "d_wo1")
    dao1 = _mm_nt(dh2, wo1, "d_ao1")
    (do1,), dg1 = _gate_bwd(dao1, [o1], proj1, "gate1_bwd")
    dq1, dk4, dv4, d_sinks = _swa_bwd(p1, sinks_b, do1.reshape(b, lp, -1), "swa_bwd")
    unheads = lambda a: a.reshape(t, SWA_KV_HEADS * HEAD).astype(BF16)
    dproj1 = jnp.concatenate([dg1, dq1.reshape(t, -1).astype(BF16), unheads(dk4), unheads(dv4)], axis=1)
    d_w1 = _mm_tn(hn1, dproj1, "d_w1")
    dhn1 = _mm_nt(dproj1, w1, "d_hn1")
    dh1, d_g1 = _rms_bwd(h1, g1, dhn1, dh2, "norm1_bwd")

    d_wo0 = _mm_tn(ao0, dh1, "d_wo0")
    early = dict(od_w_in=_od_w_in_grad(d_w1), ev_w_out=d_wo0, od_w_out=d_wo1)
    dao0 = _mm_nt(dh1, wo0, "d_ao0")
    if reduce_early is not None:
        pending, finish = reduce_early(early)
        dao0, pending = lax.optimization_barrier((dao0, pending))
        early_done = finish(pending)
    (do_sb, do_mla), dg0 = _gate_bwd(dao0, o0, proj0, "gate0_bwd")
    dq_sb, dk_sb, dv_sb = _sb_bwd(p0, sb_tot, do_sb.reshape(b, lp, -1), "sb_bwd")
    dqf, dkf, dv = _mla_bwd(qf, kf, v, o_mla, lse, do_mla.reshape(b, lp, -1), "mla_bwd")
    dcq, dckv, dkr, d_wq, d_wk, d_wv, d_gq, d_gkv = _mla_prep_bwd(p0, gq, gkv, wq, wk, wv, tabs, dqf, dkf, dv, "mla_prep_bwd")
    flat = lambda a: a.reshape(t, -1).astype(BF16)
    dproj0 = jnp.concatenate([dg0, flat(dq_sb), flat(dk_sb), flat(dv_sb), flat(dcq), flat(dckv), flat(dkr)], axis=1)
    if reduce_early is not None:
        dproj0, early_done = lax.optimization_barrier((dproj0, early_done))
    d_w0 = _mm_tn(hn0, dproj0, "d_w0")
    d_ev_w_in, d_uq, d_ukv = _original_grads(d_w0, d_wq, d_wk, d_wv)
    dhn0 = _mm_nt(dproj0, w0, "d_hn0")
    if reduce_early is not None:
        pending, finish = reduce_in_proj(d_ev_w_in)
        dhn0, pending = lax.optimization_barrier((dhn0, pending))
        in_proj_done = finish(pending)
    dh0, d_g0 = _rms_bwd(h0, g0, dhn0, dh1, "norm0_bwd")
    dh0 = dh0.reshape(b, lp, d)

    grads = dict(meta=_meta_rows_sum(dh0), norm_g=jnp.concatenate([d_g0, d_g1], axis=0), final_g=d_final_g.reshape(d),
                 ev_q_norm_g=d_gq, ev_kv_norm_g=d_gkv, ev_w_uq=d_uq, ev_w_ukv=d_ukv, od_sinks=d_sinks[:, 0].reshape(1, SWA_HEADS))
    if reduce_early is None:
        return loss, dh0[:, BLOCK:], {**grads, **early, "ev_w_in": d_ev_w_in}
    return loss, dh0[:, BLOCK:], grads, in_proj_done, early_done


MESH = pl.DeviceIdType.MESH
ANY = pl.BlockSpec(memory_space=pl.ANY)


def _place():
    return lax.axis_index("x"), lax.axis_index("y"), lax.axis_index("c")


def _other_chips(x, y):
    return [(1 - x, y), (x, 1 - y), (1 - x, 1 - y)]


def _with_own_slot(slots, own):
    me = 2 * lax.axis_index("x") + lax.axis_index("y")
    return lax.dynamic_update_slice(slots, own[None], (me,) + (0,) * own.ndim)


def _gather_weights(packs, meta, name):
    n = len(packs)

    def body(*refs):
        ins, m_ref, outs, mo_ref = refs[:n], refs[n], refs[n + 1:2 * n + 1], refs[2 * n + 1]
        send_sems, recv_sems = refs[2 * n + 2:]
        x, y, c = _place()
        me, sib = 2 * x + y, (x, y, 1 - c)
        chips = _other_chips(x, y)

        def copy(k, src, dst, to):
            return pltpu.make_async_remote_copy(src_ref=src, dst_ref=dst, send_sem=send_sems.at[k], recv_sem=recv_sems.at[k], device_id=to,
                                                device_id_type=MESH)

        def half(i, chip, h):
            rows = packs[i].shape[0] // 2
            return outs[i].at[chip, pl.ds(h * rows, rows), :]

        def mine(i):
            rows = packs[i].shape[0] // 2
            return ins[i].at[pl.ds(c * rows, rows), :]

        sent = [copy(6 * i + k, mine(i), half(i, me, c), (px, py, c)) for i in range(n) for k, (px, py) in enumerate(chips)]
        sent += [copy(6 * n + k, m_ref, mo_ref.at[me], (px, py, c)) for k, (px, py) in enumerate(chips)]
        for cp in sent:
            cp.start()
        for i in range(n):
            for k, (px, py) in enumerate(chips):
                landed = half(i, 2 * px + py, c)
                copy(6 * i + k, mine(i), landed, (px, py, c)).wait_recv()
                fwd = copy(6 * i + 3 + k, landed, landed, sib)
                fwd.start()
                sent.append(fwd)
        for k, (px, py) in enumerate(chips):
            for i in range(n):
                other = half(i, 2 * px + py, 1 - c)
                copy(6 * i + 3 + k, other, other, sib).wait_recv()
            copy(6 * n + k, m_ref, mo_ref.at[2 * px + py], (px, py, c)).wait_recv()
        for cp in sent:
            cp.wait_send()

    nsem = 6 * n + 3
    res = pl.pallas_call(
        body, in_specs=[ANY] * (n + 1), out_specs=[ANY] * (n + 1),
        out_shape=[jax.ShapeDtypeStruct((N_CHIPS,) + a.shape, a.dtype) for a in list(packs) + [meta]],
        scratch_shapes=[pltpu.SemaphoreType.DMA((nsem,)), pltpu.SemaphoreType.DMA((nsem,))],
        name=name,
    )(*packs, meta)
    return [_with_own_slot(r, a) for r, a in zip(res[:n], packs)], _with_own_slot(res[n], meta)


def _grads_to_sibling(gs, name):
    n = len(gs)

    def body(*refs):
        ins, outs, send_sems, recv_sems = refs[:n], refs[n:2 * n], refs[2 * n], refs[2 * n + 1]
        x, y, c = _place()
        cps = []
        for i in range(n):
            rows = gs[i].shape[1] // 2
            cps.append(pltpu.make_async_remote_copy(src_ref=ins[i].at[:, pl.ds((1 - c) * rows, rows), :], dst_ref=outs[i],
                                                    send_sem=send_sems.at[i], recv_sem=recv_sems.at[i], device_id=(x, y, 1 - c),
                                                    device_id_type=MESH))
        for cp in cps:
            cp.start()
        for cp in cps:
            cp.wait()

    return pl.pallas_call(
        body, in_specs=[ANY] * n, out_specs=[ANY] * n,
        out_shape=[jax.ShapeDtypeStruct((g.shape[0], g.shape[1] // 2, g.shape[2]), g.dtype) for g in gs],
        scratch_shapes=[pltpu.SemaphoreType.DMA((n,)), pltpu.SemaphoreType.DMA((n,))],
        name=name,
    )(*gs)


def _share_halves(rs, name):
    n = len(rs)

    def body(*refs):
        ins, outs, send_sems, recv_sems = refs[:n], refs[n:2 * n], refs[2 * n], refs[2 * n + 1]
        x, y, c = _place()
        cps = [pltpu.make_async_remote_copy(src_ref=ins[i], dst_ref=outs[i], send_sem=send_sems.at[i], recv_sem=recv_sems.at[i],
                                            device_id=(x, y, 1 - c), device_id_type=MESH) for i in range(n)]
        for cp in cps:
            cp.start()
        for cp in cps:
            cp.wait()

    theirs = pl.pallas_call(
        body, in_specs=[ANY] * n, out_specs=[ANY] * n, out_shape=[jax.ShapeDtypeStruct(r.shape, r.dtype) for r in rs],
        scratch_shapes=[pltpu.SemaphoreType.DMA((n,)), pltpu.SemaphoreType.DMA((n,))],
        name=name,
    )(*rs)
    first = lax.axis_index("c") == 0
    return [jnp.where(first, jnp.concatenate([r, t], axis=0), jnp.concatenate([t, r], axis=0)) for r, t in zip(rs, theirs)]


def _chip_scatter(ss, name):
    n = len(ss)

    def body(*refs):
        ins, outs, send_sems, recv_sems = refs[:n], refs[n:2 * n], refs[2 * n], refs[2 * n + 1]
        x, y, c = _place()
        me = 2 * x + y
        chips = _other_chips(x, y)
        for i in range(n):
            for k, (px, py) in enumerate(chips):
                pltpu.make_async_remote_copy(src_ref=ins[i].at[2 * px + py], dst_ref=outs[i].at[me], send_sem=send_sems.at[3 * i + k],
                                             recv_sem=recv_sems.at[3 * i + k], device_id=(px, py, c), device_id_type=MESH).start()
        for i in range(n):
            for k, (px, py) in enumerate(chips):
                cp = pltpu.make_async_remote_copy(src_ref=ins[i].at[2 * px + py], dst_ref=outs[i].at[2 * px + py],
                                                  send_sem=send_sems.at[3 * i + k], recv_sem=recv_sems.at[3 * i + k],
                                                  device_id=(px, py, c), device_id_type=MESH)
                cp.wait_recv()
                cp.wait_send()

    parts = pl.pallas_call(
        body, in_specs=[ANY] * n, out_specs=[ANY] * n, out_shape=[jax.ShapeDtypeStruct(s.shape, s.dtype) for s in ss],
        scratch_shapes=[pltpu.SemaphoreType.DMA((3 * n,)), pltpu.SemaphoreType.DMA((3 * n,))],
        name=name,
    )(*ss)
    me = 2 * lax.axis_index("x") + lax.axis_index("y")
    return [_with_own_slot(p, lax.dynamic_index_in_dim(s, me, axis=0, keepdims=False)) for p, s in zip(parts, ss)]


HBM_SPACE = pltpu.MemorySpace.HBM


def _on_sequencer(name, collective_id, n_sems, body):
    @pl.kernel(mesh=plsc.ScalarSubcoreMesh(axis_name="sequencer", num_cores=1), name=name,
               scratch_types=(pltpu.SemaphoreType.DMA((n_sems,)), pltpu.SemaphoreType.DMA((n_sems,))),
               compiler_params=pltpu.CompilerParams(collective_id=collective_id))
    def launch(send_sems, recv_sems):
        body(send_sems, recv_sems)

    launch()


def _handshake(peers):
    barrier = pltpu.get_barrier_semaphore()
    for peer in peers:
        pl.semaphore_signal(barrier, inc=1, device_id=peer, device_id_type=MESH)
    pl.semaphore_wait(barrier, len(peers))


def _gather_on_sequencer(packs, name):
    n = len(packs)
    ins = [jax.new_ref(p, memory_space=HBM_SPACE) for p in packs]
    outs = [jax.empty_ref(jax.ShapeDtypeStruct((N_CHIPS,) + p.shape, p.dtype), memory_space=HBM_SPACE) for p in packs]

    def body(send_sems, recv_sems):
        x, y, c = _place()
        me, sib = 2 * x + y, (x, y, 1 - c)
        chips = _other_chips(x, y)
        _handshake([(px, py, c) for px, py in chips] + [sib])

        def copy(k, src, dst, to):
            return pltpu.make_async_remote_copy(src_ref=src, dst_ref=dst, send_sem=send_sems.at[k], recv_sem=recv_sems.at[k], device_id=to,
                                                device_id_type=MESH)

        def half(i, chip, h):
            rows = packs[i].shape[0] // 2
            return outs[i].at[chip, pl.ds(h * rows, rows), :]

        def mine(i):
            rows = packs[i].shape[0] // 2
            return ins[i].at[pl.ds(c * rows, rows), :]

        sent = [copy(6 * i + k, mine(i), half(i, me, c), (px, py, c)) for i in range(n) for k, (px, py) in enumerate(chips)]
        for cp in sent:
            cp.start()
        for i in range(n):
            for k, (px, py) in enumerate(chips):
                landed = half(i, 2 * px + py, c)
                copy(6 * i + k, mine(i), landed, (px, py, c)).wait_recv()
                fwd = copy(6 * i + 3 + k, landed, landed, sib)
                fwd.start()
                sent.append(fwd)
        for i in range(n):
            for k, (px, py) in enumerate(chips):
                other = half(i, 2 * px + py, 1 - c)
                copy(6 * i + 3 + k, other, other, sib).wait_recv()
        for cp in sent:
            cp.wait_send()

    _on_sequencer(name, 1, 6 * n, body)
    return [_with_own_slot(o[...], p) for o, p in zip(outs, packs)]


def _grads_to_sibling_on_sequencer(gs, name, collective_id):
    n = len(gs)
    ins = [jax.new_ref(g, memory_space=HBM_SPACE) for g in gs]
    outs = [jax.empty_ref(jax.ShapeDtypeStruct((g.shape[0], g.shape[1] // 2, g.shape[2]), g.dtype), memory_space=HBM_SPACE) for g in gs]

    def body(send_sems, recv_sems):
        x, y, c = _place()
        _handshake([(x, y, 1 - c)])
        cps = []
        for i in range(n):
            rows = gs[i].shape[1] // 2
            cps.append(pltpu.make_async_remote_copy(src_ref=ins[i].at[:, pl.ds((1 - c) * rows, rows), :], dst_ref=outs[i],
                                                    send_sem=send_sems.at[i], recv_sem=recv_sems.at[i], device_id=(x, y, 1 - c),
                                                    device_id_type=MESH))
        for cp in cps:
            cp.start()
        for cp in cps:
            cp.wait()

    _on_sequencer(name, collective_id, n, body)
    return [o[...] for o in outs]


def _chip_scatter_on_sequencer(ss, name, collective_id):
    n = len(ss)
    ins = [jax.new_ref(s, memory_space=HBM_SPACE) for s in ss]
    outs = [jax.empty_ref(jax.ShapeDtypeStruct(s.shape, s.dtype), memory_space=HBM_SPACE) for s in ss]

    def body(send_sems, recv_sems):
        x, y, c = _place()
        me = 2 * x + y
        chips = _other_chips(x, y)
        _handshake([(px, py, c) for px, py in chips])
        for i in range(n):
            for k, (px, py) in enumerate(chips):
                pltpu.make_async_remote_copy(src_ref=ins[i].at[2 * px + py], dst_ref=outs[i].at[me], send_sem=send_sems.at[3 * i + k],
                                             recv_sem=recv_sems.at[3 * i + k], device_id=(px, py, c), device_id_type=MESH).start()
        for i in range(n):
            for k, (px, py) in enumerate(chips):
                cp = pltpu.make_async_remote_copy(src_ref=ins[i].at[2 * px + py], dst_ref=outs[i].at[2 * px + py],
                                                  send_sem=send_sems.at[3 * i + k], recv_sem=recv_sems.at[3 * i + k],
                                                  device_id=(px, py, c), device_id_type=MESH)
                cp.wait_recv()
                cp.wait_send()

    _on_sequencer(name, collective_id, 3 * n, body)
    me = 2 * lax.axis_index("x") + lax.axis_index("y")
    return [_with_own_slot(o[...], lax.dynamic_index_in_dim(s, me, axis=0, keepdims=False)) for o, s in zip(outs, ss)]


def _all_reduce_small(v, name):
    shape = v.shape

    def body(v_ref, o_ref, slots, send_sems, recv_sems):
        x, y, c = _place()
        me = 4 * x + 2 * y + c
        slots[me] = v_ref[...]
        for r in range(1, N_DEV):
            peer = (x ^ (r >> 2), y ^ ((r >> 1) & 1), c ^ (r & 1))
            pltpu.make_async_remote_copy(src_ref=v_ref, dst_ref=slots.at[me], send_sem=send_sems.at[r - 1], recv_sem=recv_sems.at[r - 1],
                                         device_id=peer, device_id_type=MESH).start()
        for r in range(1, N_DEV):
            peer = (x ^ (r >> 2), y ^ ((r >> 1) & 1), c ^ (r & 1))
            cp = pltpu.make_async_remote_copy(src_ref=v_ref, dst_ref=slots.at[4 * peer[0] + 2 * peer[1] + peer[2]], send_sem=send_sems.at[r - 1],
                                              recv_sem=recv_sems.at[r - 1], device_id=peer, device_id_type=MESH)
            cp.wait_recv()
            cp.wait_send()
        acc = slots[0]
        for d in range(1, N_DEV):
            acc = acc + slots[d]
        o_ref[...] = acc

    vm = pl.BlockSpec(memory_space=pltpu.VMEM)
    return pl.pallas_call(
        body, in_specs=[vm], out_specs=vm, out_shape=jax.ShapeDtypeStruct(shape, F32),
        scratch_shapes=[pltpu.VMEM((N_DEV,) + shape, F32), pltpu.SemaphoreType.DMA((N_DEV - 1,)), pltpu.SemaphoreType.DMA((N_DEV - 1,))],
        name=name,
    )(v)


def _add_sibling(g, gsib, core, name):
    n, _, cdim = g.shape
    half = gsib.shape[1]
    tr = half // 2

    def body(core_ref, a_ref, b_ref, o_ref):
        o_ref[...] = (a_ref[...] + b_ref[...]).astype(o_ref.dtype)

    blk = pl.BlockSpec((1, tr, cdim), lambda j, i, core_ref: (j, i, 0))
    return pl.pallas_call(
        body,
        grid_spec=pltpu.PrefetchScalarGridSpec(
            num_scalar_prefetch=1, grid=(n, half // tr),
            in_specs=[pl.BlockSpec((1, tr, cdim), lambda j, i, core_ref: (j, core_ref[0] * (half // tr) + i, 0)), blk], out_specs=blk),
        out_shape=jax.ShapeDtypeStruct(gsib.shape, BF16), compiler_params=_params(("parallel", "parallel")), name=name,
    )(core, g, gsib)


def _sum_parts(parts, name):
    n, r, cdim = parts.shape
    tr = r // 2

    def body(p_ref, o_ref):
        acc = p_ref[0].astype(F32)
        for j in range(1, n):
            acc = acc + p_ref[j].astype(F32)
        o_ref[...] = acc

    return pl.pallas_call(
        body, grid=(r // tr,), in_specs=[pl.BlockSpec((n, tr, cdim), lambda i: (0, i, 0))],
        out_specs=pl.BlockSpec((tr, cdim), lambda i: (i, 0)), out_shape=jax.ShapeDtypeStruct((r, cdim), F32),
        compiler_params=_params(("parallel",)), name=name,
    )(parts)


def _adamw(parts, w, m, v, name):
    npart, r, cdim = parts.shape
    tr = r // 4 if r % 32 == 0 else r

    def body(p_ref, w_ref, m_ref, v_ref, g_ref, d_ref, nm_ref, nv_ref):
        g = p_ref[0]
        for j in range(1, npart):
            g = g + p_ref[j]
        m_new = ADAM_B1 * m_ref[...] + (1.0 - ADAM_B1) * g
        v_new = ADAM_B2 * v_ref[...] + (1.0 - ADAM_B2) * (g * g)
        m_hat = m_new / (1.0 - ADAM_B1 ** ADAM_STEP)
        v_hat = v_new / (1.0 - ADAM_B2 ** ADAM_STEP)
        g_ref[...] = g
        d_ref[...] = -ADAM_LR * (m_hat / (jnp.sqrt(v_hat) + ADAM_EPS) + ADAM_WD * w_ref[...])
        nm_ref[...] = m_new
        nv_ref[...] = v_new

    blk = pl.BlockSpec((tr, cdim), lambda i: (i, 0))
    shp = jax.ShapeDtypeStruct((r, cdim), F32)
    return pl.pallas_call(
        body, grid=(r // tr,), in_specs=[pl.BlockSpec((npart, tr, cdim), lambda i: (0, i, 0)), blk, blk, blk], out_specs=[blk] * 4,
        out_shape=[shp] * 4, compiler_params=_params(("parallel",)), name=name,
    )(parts, w, m, v)


BIG = ("ev_w_in", "ev_w_uq", "ev_w_ukv", "ev_w_out", "od_w_in", "od_w_out", "meta")
SMALL = ("norm_g", "final_g", "ev_q_norm_g", "ev_kv_norm_g", "od_sinks")
SMALL_SHAPE = (8, 512)
BY_ROWS = ("ev_w_out", "od_w_out")

EV_IN_SHARD, OD_IN_SHARD, UQ_SHARD = 2976 // N_CHIPS, 2304 // N_CHIPS, 768 // N_CHIPS


def _pack_big(a, lead=()):
    return _pack_first(a, lead) + _pack_later(a, lead)


def _pad_lanes(x, width):
    return jnp.concatenate([x, jnp.zeros(x.shape[:-1] + (width - x.shape[-1],), x.dtype)], axis=-1)


def _pack_latent(a, lead=()):
    ax = len(lead)
    corner = jnp.concatenate([a["ev_w_ukv"], a["meta"], jnp.zeros(lead + (256 - MLA_KV_LORA - N_META, 256), a["meta"].dtype)], axis=ax)
    return jnp.concatenate([_pad_lanes(a["ev_w_uq"], 256), corner], axis=ax + 1)


def _pack_first(a, lead=()):
    return _pad_lanes(a["ev_w_in"], 768), _pack_latent(a, lead)


def _pack_later(a, lead=()):
    return _pad_lanes(a["od_w_in"], 640), jnp.concatenate([a["ev_w_out"], a["od_w_out"]], axis=len(lead) + 1)


N_FIRST = 2
LATER = ("od_w_in", "ev_w_out", "od_w_out")


def _unpack_big(p_in0, p_lat, p_in1, p_out):
    return dict(ev_w_in=p_in0[..., :EV_IN_SHARD], od_w_in=p_in1[..., :OD_IN_SHARD], ev_w_out=p_out[..., :D_MODEL],
                od_w_out=p_out[..., D_MODEL:], ev_w_uq=p_lat[..., :UQ_SHARD], ev_w_ukv=p_lat[..., :MLA_KV_LORA, 256:],
                meta=p_lat[..., MLA_KV_LORA:MLA_KV_LORA + N_META, 256:])


def _chip_shards(full, by_rows):
    if by_rows:
        return full.reshape(N_CHIPS, full.shape[0] // N_CHIPS, full.shape[1])
    return full.reshape(full.shape[0], N_CHIPS, -1).transpose(1, 0, 2)


def _from_chip_shards(slots, by_rows):
    if by_rows:
        return slots.reshape(-1, slots.shape[2])
    return slots.transpose(1, 0, 2).reshape(slots.shape[1], -1)


def _pack_small(arrs, extra=None):
    flat = [a.reshape(-1) for a in arrs] + ([] if extra is None else [extra.reshape(-1)])
    used = sum(f.shape[0] for f in flat)
    return jnp.pad(jnp.concatenate(flat), (0, SMALL_SHAPE[0] * SMALL_SHAPE[1] - used)).reshape(SMALL_SHAPE)


def _unpack_small(p, shapes):
    flat, out, at = p.reshape(-1), [], 0
    for s in shapes:
        n = int(np.prod(s))
        out.append(flat[at:at + n].reshape(s))
        at += n
    return out, flat[at]


def kernel(x, meta, norm_g, final_g, ev_w_in, ev_q_norm_g, ev_kv_norm_g, ev_w_uq, ev_w_ukv, ev_w_out, od_w_in, od_sinks, od_w_out, loss_target, m_meta, m_norm_g, m_final_g, m_ev_w_in, m_ev_q_norm_g, m_ev_kv_norm_g, m_ev_w_uq, m_ev_w_ukv, m_ev_w_out, m_od_w_in, m_od_sinks, m_od_w_out, v_meta, v_norm_g, v_final_g, v_ev_w_in, v_ev_q_norm_g, v_ev_kv_norm_g, v_ev_w_uq, v_ev_w_ukv, v_ev_w_out, v_od_w_in, v_od_sinks, v_od_w_out):
    given = dict(locals())
    two_d = lambda a: a[0] if a.ndim == 3 else a
    packs = {k: _pack_big({n: two_d(given[k + n]) for n in BIG}) for k in ("", "m_", "v_")}

    wbf = [p.astype(BF16) for p in packs[""]]
    later = _gather_on_sequencer(wbf[N_FIRST:], "gather_later_weights")
    first, meta_all = _gather_weights(wbf[:N_FIRST], meta, "gather_weights")
    full = {n: _from_chip_shards(a, n in BY_ROWS) for n, a in _unpack_big(*first, *later).items()}
    meta_full = _from_chip_shards(meta_all, False)

    core = lax.axis_index("c").astype(jnp.int32).reshape(1)
    shards = lambda g: {n: _chip_shards(a, n in BY_ROWS) for n, a in g.items()}

    def sums_of(group, tag, to_sibling):
        return [_add_sibling(g, s, core, f"add_sibling_{tag}{i}") for i, (g, s) in enumerate(zip(group, to_sibling(group)))]

    def update(sums, tag, first_pack, scatter):
        parts = scatter(sums, "grads_to_chips_" + tag)
        reduced = _share_halves([_sum_parts(p, f"add_chips_{tag}{i}") for i, p in enumerate(parts)], "reduced_to_sibling_" + tag)
        return [_adamw(r[None], packs[""][first_pack + i], packs["m_"][first_pack + i], packs["v_"][first_pack + i],
                       f"adamw_matrices_{first_pack + i}") for i, r in enumerate(reduced)]

    def behind(tag, first_pack, ids):
        def start(packed):
            sums = sums_of(packed, tag, lambda p: _grads_to_sibling_on_sequencer(p, "grads_to_sibling_" + tag, ids[0]))
            return sums, lambda s: update(s, tag, first_pack, lambda t, name: _chip_scatter_on_sequencer(t, name, ids[1]))
        return start

    loss, grad_x, grads, updated_in_proj, updated_later = _local_step(
        x, meta_full, norm_g, final_g, ev_q_norm_g, ev_kv_norm_g, od_sinks, loss_target, full["ev_w_in"], full["ev_w_uq"],
        full["ev_w_ukv"], full["ev_w_out"], full["od_w_in"], full["od_w_out"],
        reduce_early=lambda g: behind("later", N_FIRST, (3, 2))(_pack_later(shards(g), (N_CHIPS,))),
        reduce_in_proj=lambda g: behind("in_proj", 0, (5, 4))((_pad_lanes(_chip_shards(g, False), 768),)))

    latent = (_pack_latent(shards({n: grads[n] for n in ("ev_w_uq", "ev_w_ukv", "meta")}), (N_CHIPS,)),)
    updated_latent = update(sums_of(latent, "latent", lambda p: _grads_to_sibling(p, "grads_to_sibling_latent")), "latent", 1, _chip_scatter)
    updated = updated_in_proj + updated_latent + updated_later
    big_out = [{n: a.reshape(given[n].shape) for n, a in _unpack_big(*outs).items()} for outs in zip(*updated)]

    small_shapes = [given[n].shape for n in SMALL]
    ssum = _all_reduce_small(_pack_small([grads[n] for n in SMALL], loss[0, 0]), "reduce_vectors")
    small_out = _adamw(ssum[None], _pack_small([given[n] for n in SMALL]), _pack_small([given["m_" + n] for n in SMALL]),
                       _pack_small([given["v_" + n] for n in SMALL]), "adamw_vectors")
    total_loss = ssum.reshape(-1)[sum(int(np.prod(s)) for s in small_shapes)]
    small_out = [_unpack_small(o, small_shapes)[0] for o in small_out]

    names = ("meta", "norm_g", "final_g", "ev_w_in", "ev_q_norm_g", "ev_kv_norm_g", "ev_w_uq", "ev_w_ukv", "ev_w_out", "od_w_in", "od_sinks",
             "od_w_out")
    outs = [total_loss, grad_x]
    for kind in range(4):
        for n in names:
            outs.append(big_out[kind][n] if n in BIG else small_out[kind][SMALL.index(n)])
    return tuple(outs)
```

```python
import functools
import math

import numpy as np
import jax
import jax.numpy as jnp
from jax import lax
from jax.experimental import pallas as pl
from jax.experimental.pallas import tpu as pltpu
from jax.experimental.pallas import tpu_sc as plsc

F32 = jnp.float32
BF16 = jnp.bfloat16

D_MODEL = 1024
BLOCK = 128
N_META = 16
N_PAD = BLOCK - N_META
NORM_EPS = 1e-6
NEG = -1e30
HEAD = 64
SB_HEADS = 8
MLA_HEADS = 8
MLA_Q_LORA = 256
MLA_KV_LORA = 128
MLA_NOPE = 64
MLA_ROPE = 32
ROPE_BASE = 10000.0
SWA_HEADS = 16
SWA_KV_HEADS = 2
SWA_WINDOW = 128
N_CHIPS = 4
N_DEV = 8

ADAM_LR = 0.001
ADAM_B1 = 0.9
ADAM_B2 = 0.999
ADAM_EPS = 1e-08
ADAM_WD = 0.01
ADAM_STEP = 10

VMEM_LIMIT = 48 * 1024 * 1024

EV_G, EV_Q, EV_K, EV_V, EV_CQ, EV_CKV, EV_KR, EV_N = 0, 1024, 1536, 2048, 2560, 2816, 2944, 3072
OD_G, OD_Q, OD_K, OD_V, OD_N = 0, 1024, 2048, 2176, 2304


def _params(sem=None):
    return pltpu.CompilerParams(dimension_semantics=sem, vmem_limit_bytes=VMEM_LIMIT)


def _row_tile(m):
    return 256 if m % 256 == 0 else 128


def _matmul_rows(m):
    for c in (1088, 1024, 768, 640, 512, 384, 256):
        if m % c == 0:
            return c
    return 128


def _dot(a, b):
    return jnp.dot(a.astype(BF16), b.astype(BF16), preferred_element_type=F32)


def _dot_nt(a, b):
    return lax.dot_general(a.astype(BF16), b.astype(BF16), (((1,), (1,)), ((), ())), preferred_element_type=F32)


def _dot_tn(a, b):
    return lax.dot_general(a.astype(BF16), b.astype(BF16), (((0,), (0,)), ((), ())), preferred_element_type=F32)


def _rms_fwd(h, g, name):
    t, d = h.shape
    tm = _row_tile(t)

    def body(h_ref, g_ref, o_ref):
        x = h_ref[...]
        r = lax.rsqrt(jnp.mean(x * x, axis=-1, keepdims=True) + NORM_EPS)
        o_ref[...] = ((x * r) * g_ref[...]).astype(o_ref.dtype)

    return pl.pallas_call(
        body, grid=(t // tm,),
        in_specs=[pl.BlockSpec((tm, d), lambda i: (i, 0)), pl.BlockSpec((1, d), lambda i: (0, 0))],
        out_specs=pl.BlockSpec((tm, d), lambda i: (i, 0)),
        out_shape=jax.ShapeDtypeStruct((t, d), BF16), compiler_params=_params(("parallel",)), name=name,
    )(h, g)


def _rms_bwd(h, g, dy, dres, name):
    t, d = h.shape
    tm = _row_tile(t)

    def body(h_ref, g_ref, dy_ref, dres_ref, dh_ref, dg_ref):
        @pl.when(pl.program_id(0) == 0)
        def _():
            dg_ref[...] = jnp.zeros_like(dg_ref)

        x = h_ref[...]
        r = lax.rsqrt(jnp.mean(x * x, axis=-1, keepdims=True) + NORM_EPS)
        xr = x * r
        dy_ = dy_ref[...]
        u = dy_ * g_ref[...]
        dh_ref[...] = dres_ref[...] + r * (u - xr * jnp.mean(u * xr, axis=-1, keepdims=True))
        dg_ref[...] += jnp.sum(dy_ * xr, axis=0, keepdims=True)

    row = pl.BlockSpec((tm, d), lambda i: (i, 0))
    vec = pl.BlockSpec((1, d), lambda i: (0, 0))
    return pl.pallas_call(
        body, grid=(t // tm,), in_specs=[row, vec, row, row], out_specs=[row, vec],
        out_shape=[jax.ShapeDtypeStruct((t, d), F32), jax.ShapeDtypeStruct((1, d), F32)],
        compiler_params=_params(("arbitrary",)), name=name,
    )(h, g, dy, dres)


def _col_tile(n):
    for c in (1024, 768, 640, 512, 384, 256, 128):
        if n % c == 0:
            return c
    return n


def _mm(a, w, name, res=None, out_dtype=F32, a_cols=None):
    m = a.shape[0]
    k, n = w.shape
    a_blk = 0 if a_cols is None else a_cols[0] // k
    assert a_cols is None or (a_cols[1] == k and a_cols[0] % k == 0)
    tm, tn = _matmul_rows(m), _col_tile(n)

    def body(*refs):
        if res is None:
            a_ref, w_ref, o_ref = refs
            acc = _dot(a_ref[...], w_ref[...])
        else:
            a_ref, w_ref, r_ref, o_ref = refs
            acc = r_ref[...] + _dot(a_ref[...], w_ref[...])
        o_ref[...] = acc.astype(o_ref.dtype)

    in_specs = [pl.BlockSpec((tm, k), lambda j, i: (i, a_blk)), pl.BlockSpec((k, tn), lambda j, i: (0, j))]
    args = [a, w]
    if res is not None:
        in_specs.append(pl.BlockSpec((tm, tn), lambda j, i: (i, j)))
        args.append(res)
    return pl.pallas_call(
        body, grid=(n // tn, m // tm), in_specs=in_specs, out_specs=pl.BlockSpec((tm, tn), lambda j, i: (i, j)),
        out_shape=jax.ShapeDtypeStruct((m, n), out_dtype), compiler_params=_params(("parallel", "parallel")), name=name,
    )(*args)


def _mm_nt(a, w, name):
    m, n = a.shape
    k = w.shape[0]
    tm, tk = _matmul_rows(m), _col_tile(k)

    def body(a_ref, w_ref, o_ref):
        o_ref[...] = _dot_nt(a_ref[...], w_ref[...])

    return pl.pallas_call(
        body, grid=(k // tk, m // tm),
        in_specs=[pl.BlockSpec((tm, n), lambda j, i: (i, 0)), pl.BlockSpec((tk, n), lambda j, i: (j, 0))],
        out_specs=pl.BlockSpec((tm, tk), lambda j, i: (i, j)),
        out_shape=jax.ShapeDtypeStruct((m, k), F32), compiler_params=_params(("parallel", "parallel")), name=name,
    )(a, w)


def _mm_tn(x, dy, name):
    m, k = x.shape
    n = dy.shape[1]
    tm, tn = _matmul_rows(m), _col_tile(n)

    def body(x_ref, dy_ref, o_ref):
        @pl.when(pl.program_id(1) == 0)
        def _():
            o_ref[...] = jnp.zeros_like(o_ref)

        o_ref[...] += _dot_tn(x_ref[...], dy_ref[...])

    return pl.pallas_call(
        body, grid=(n // tn, m // tm),
        in_specs=[pl.BlockSpec((tm, k), lambda j, i: (i, 0)), pl.BlockSpec((tm, tn), lambda j, i: (i, j))],
        out_specs=pl.BlockSpec((k, tn), lambda j, i: (0, j)),
        out_shape=jax.ShapeDtypeStruct((k, n), F32), compiler_params=_params(("parallel", "arbitrary")), name=name,
    )(x, dy)


def _silu_parts(g):
    s = 1.0 / (1.0 + jnp.exp(-g))
    return g * s, s * (1.0 + g * (1.0 - s))


def _gate_fwd(o_parts, proj, name):
    t = proj.shape[0]
    tm = _row_tile(t)
    w = D_MODEL // len(o_parts)

    def body(*refs):
        g_ref, o_ref = refs[-2], refs[-1]
        for p, r in enumerate(refs[:-2]):
            sil, _ = _silu_parts(g_ref[:, p * w:(p + 1) * w])
            o_ref[:, p * w:(p + 1) * w] = (r[...].astype(F32) * sil).astype(o_ref.dtype)

    return pl.pallas_call(
        body, grid=(t // tm,),
        in_specs=[pl.BlockSpec((tm, w), lambda i: (i, 0)) for _ in o_parts] + [pl.BlockSpec((tm, D_MODEL), lambda i: (i, 0))],
        out_specs=pl.BlockSpec((tm, D_MODEL), lambda i: (i, 0)),
        out_shape=jax.ShapeDtypeStruct((t, D_MODEL), BF16), compiler_params=_params(("parallel",)), name=name,
    )(*o_parts, proj)


def _gate_bwd(dao, o_parts, proj, name):
    t = proj.shape[0]
    tm = _row_tile(t)
    np_ = len(o_parts)
    w = D_MODEL // np_

    def body(*refs):
        dao_ref, g_ref = refs[0], refs[1 + np_]
        do_refs, dg_ref = refs[2 + np_:2 + 2 * np_], refs[-1]
        for p in range(np_):
            sl = slice(p * w, (p + 1) * w)
            sil, dsil = _silu_parts(g_ref[:, sl])
            da = dao_ref[:, sl]
            do_refs[p][...] = da * sil
            dg_ref[:, sl] = (da * refs[1 + p][...].astype(F32) * dsil).astype(dg_ref.dtype)

    full = pl.BlockSpec((tm, D_MODEL), lambda i: (i, 0))
    part = pl.BlockSpec((tm, w), lambda i: (i, 0))
    outs = pl.pallas_call(
        body, grid=(t // tm,), in_specs=[full] + [part] * np_ + [full], out_specs=[part] * np_ + [full],
        out_shape=[jax.ShapeDtypeStruct((t, w), F32)] * np_ + [jax.ShapeDtypeStruct((t, D_MODEL), BF16)],
        compiler_params=_params(("parallel",)), name=name,
    )(dao, *o_parts, proj)
    return outs[:np_], outs[np_]


def _loss_head(h2, gf, target, b, lp):
    d = h2.shape[1]
    nb = lp // BLOCK
    h3 = h2.reshape(b, lp, d)

    def body(h_ref, g_ref, t_ref, dh_ref, dg_ref, loss_ref):
        first = (pl.program_id(0) == 0) & (pl.program_id(1) == 0)

        @pl.when(first)
        def _():
            dg_ref[...] = jnp.zeros_like(dg_ref)
            loss_ref[...] = jnp.zeros_like(loss_ref)

        @pl.when(pl.program_id(1) == 0)
        def _():
            dh_ref[...] = jnp.zeros_like(dh_ref)

        @pl.when(pl.program_id(1) > 0)
        def _():
            x = h_ref[0]
            r = lax.rsqrt(jnp.mean(x * x, axis=-1, keepdims=True) + NORM_EPS)
            xr = x * r
            g = g_ref[...]
            diff = xr * g - t_ref[0]
            loss_ref[...] += 0.5 * jnp.sum(jnp.mean(diff * diff, axis=-1, keepdims=True))
            dy = diff * (1.0 / d)
            u = dy * g
            dh_ref[0] = r * (u - xr * jnp.mean(u * xr, axis=-1, keepdims=True))
            dg_ref[...] += jnp.sum(dy * xr, axis=0, keepdims=True)

    blk = pl.BlockSpec((1, BLOCK, d), lambda bi, n: (bi, n, 0))
    dh, dg, loss = pl.pallas_call(
        body, grid=(b, nb),
        in_specs=[blk, pl.BlockSpec((1, d), lambda bi, n: (0, 0)),
                  pl.BlockSpec((1, BLOCK, d), lambda bi, n: (bi, jnp.maximum(n - 1, 0), 0))],
        out_specs=[blk, pl.BlockSpec((1, d), lambda bi, n: (0, 0)), pl.BlockSpec((8, 128), lambda bi, n: (0, 0))],
        out_shape=[jax.ShapeDtypeStruct((b, lp, d), F32), jax.ShapeDtypeStruct((1, d), F32), jax.ShapeDtypeStruct((8, 128), F32)],
        compiler_params=_params(("arbitrary", "arbitrary")), name="loss_head",
    )(h3, gf, target)
    return dh.reshape(b * lp, d), dg, loss


def _iota2(shape, dim):
    return lax.broadcasted_iota(jnp.int32, shape, dim)


KEYS = 512
SB_FWD_KEYS = 512


def _lo_lanes():
    return _iota2((1, BLOCK), 1) < HEAD


def _halves(x, lo):
    zero = jnp.zeros_like(x)
    return jnp.where(lo, x, zero), jnp.where(lo, zero, x)


def _rows_of_pair(a, b):
    return jnp.where(_iota2((BLOCK, 1), 0) < HEAD, a, b)


def _split_rows_t(x):
    xt = x.T
    first = _iota2(xt.shape, 0) < HEAD
    zero = jnp.zeros_like(xt)
    return jnp.concatenate([jnp.where(first, xt, zero), jnp.where(first, zero, xt)], axis=1).astype(BF16)


def _key_chunk(c, lp, t_idx, strict, key_axis):
    keys = t_idx.shape[key_axis]
    first = c * keys
    s0 = pl.multiple_of(jnp.minimum(first, lp - keys), BLOCK)
    s_idx = s0 + _iota2(t_idx.shape, key_axis)
    seen = (s_idx < t_idx) if strict else (s_idx <= t_idx)
    return s0, seen & (s_idx >= jnp.maximum(first, N_PAD))


def _tri_dot(x, tri):
    return jnp.dot(x.astype(BF16), tri, preferred_element_type=F32)


def _stack_halves(x, lo):
    a, b = _halves(x, lo)
    return jnp.concatenate([a, b], axis=0)


def _pair(a, b, lo):
    return jnp.where(lo, a, b)


def _chunk_starts(lp):
    return [min(c * KEYS, lp - KEYS) for c in range(-(-lp // KEYS))]


def _put_rows(ref, r0, bq, a, b):
    for t in range(bq // BLOCK):
        part = slice(t * BLOCK, (t + 1) * BLOCK)
        ref[0, 0, r0 // BLOCK + t] = jnp.concatenate([a[:, part], b[:, part], jnp.zeros((6, BLOCK), F32)], axis=0)


def _get_rows(ref, r0, bq):
    return [jnp.concatenate([ref[0, 0, r0 // BLOCK + t, h:h + 1, :] for t in range(bq // BLOCK)], axis=1) for h in range(2)]


def _n_chunks(i, keys):
    return ((i + 1) * BLOCK + keys - 1) // keys


QROWS = 512


def _for_query_tiles(nb, tile, keys=KEYS):
    per = QROWS // BLOCK

    def step(j, _):
        tile(pl.multiple_of(j * QROWS, QROWS), QROWS, (j + 1) * (QROWS // keys))
        return 0

    lax.fori_loop(0, nb // per, step, 0)
    for i in range(nb - nb % per, nb):
        tile(i * BLOCK, BLOCK, _n_chunks(i, keys))


def _walk_chunks(r0, n, chunk, carry, leftwards=False, keys=KEYS):
    diag = jnp.maximum(r0 // keys, 1)

    def span(first, last, masked, carry):
        def step(t, cr):
            return chunk(last - 1 - t if leftwards else first + t, cr, masked)
        return lax.fori_loop(0, last - first, step, carry)

    spans = [(0, 1, True), (1, diag, False), (diag, n, True)]
    for first, last, masked in (reversed(spans) if leftwards else spans):
        carry = span(first, last, masked, carry)
    return carry


def _where(valid, x, other):
    return x if valid is None else jnp.where(valid, x, other)


HEAD_SCALE = HEAD ** -0.5
assert math.frexp(HEAD_SCALE)[0] == 0.5


def _sb_scores(q_h, k, valid, after):
    z = _dot_nt(q_h, k)
    lb = jnp.minimum(z, 0.0) - jnp.log(1.0 + jnp.exp(-jnp.abs(z)))
    l1m_all = lb - z
    l1m = _where(valid, l1m_all, 0.0)
    return lb, l1m_all, l1m, _tri_dot(l1m, after)


def _pair_stat_spec(nb):
    return pl.BlockSpec((1, 1, nb, 8, BLOCK), lambda bi, hp: (bi, hp, 0, 0, 0))


def _sb_fwd(proj3, name):
    b, lp, _ = proj3.shape
    nb = lp // BLOCK
    npair = SB_HEADS // 2

    keys = SB_FWD_KEYS

    def body(q_ref, k_ref, v_ref, o_ref, tot_ref):
        lo = _lo_lanes()
        after = (_iota2((keys, keys), 0) > _iota2((keys, keys), 1)).astype(BF16)

        def qtile(r0, bq, n):
            qs = _halves((q_ref[0, pl.ds(r0, bq), :] * HEAD_SCALE).astype(BF16), lo)
            t_idx = r0 + _iota2((bq, keys), 0)

            def kchunk(c, carry, masked):
                cs, acc = carry[:2], carry[2]
                s0, valid = _key_chunk(c, lp, t_idx, True, 1)
                valid = valid if masked else None
                k = k_ref[0, pl.ds(s0, keys), :].astype(BF16)
                a_s, new = [], []
                for h in range(2):
                    lb, _, l1m, suf = _sb_scores(qs[h], k, valid, after)
                    a_s.append(_where(valid, jnp.exp(lb + suf + cs[h]), 0.0).astype(BF16))
                    new.append(cs[h] + jnp.sum(l1m, axis=1, keepdims=True))
                v_bd = _stack_halves(v_ref[0, pl.ds(s0, keys), :].astype(BF16), lo)
                return (*new, acc + jnp.dot(jnp.concatenate(a_s, axis=1), v_bd, preferred_element_type=F32))

            zero = jnp.zeros((bq, 1), F32)
            c_a, c_b, acc = _walk_chunks(r0, n, kchunk, (zero, zero, jnp.zeros((bq, BLOCK), F32)), leftwards=True, keys=keys)
            o_ref[0, pl.ds(r0, bq), :] = acc
            tot_ref[0, pl.ds(r0, bq), :] = jnp.broadcast_to(_pair(c_a, c_b, lo), (bq, BLOCK))

        _for_query_tiles(nb, qtile, keys)

    def col(first):
        return pl.BlockSpec((1, lp, 2 * HEAD), lambda bi, hp: (bi, 0, first // (2 * HEAD) + hp))

    shp = jax.ShapeDtypeStruct((b, lp, SB_HEADS * HEAD), F32)
    return pl.pallas_call(
        body, grid=(b, npair), in_specs=[col(EV_Q), col(EV_K), col(EV_V)], out_specs=[col(0), col(0)], out_shape=[shp, shp],
        compiler_params=_params(("parallel", "parallel")), name=name,
    )(proj3, proj3, proj3)


def _sb_bwd(proj3, tot, do, name):
    b, lp, _ = proj3.shape
    nb = lp // BLOCK
    npair = SB_HEADS // 2

    def body(q_ref, k_ref, v_ref, tot_ref, do_ref, dq_ref, dk_ref, dv_ref):
        lo = _lo_lanes()
        after = (_iota2((KEYS, KEYS), 0) > _iota2((KEYS, KEYS), 1)).astype(BF16)
        before = (_iota2((KEYS, KEYS), 0) < _iota2((KEYS, KEYS), 1)).astype(BF16)
        dk_ref[...] = jnp.zeros_like(dk_ref)
        dv_ref[...] = jnp.zeros_like(dv_ref)

        def qtile(r0, bq, n):
            rows = pl.ds(r0, bq)
            qs = _halves((q_ref[0, rows, :] * HEAD_SCALE).astype(BF16), lo)
            dos = _halves(do_ref[0, rows, :].astype(BF16), lo)
            tot_i = tot_ref[0, rows, :]
            tots = (tot_i[:, 0:1], tot_i[:, HEAD:HEAD + 1])
            q_st, do_st = jnp.concatenate(qs, axis=0), jnp.concatenate(dos, axis=0)
            t_idx = r0 + _iota2((bq, KEYS), 0)

            def kchunk(c, carry, masked):
                s0, valid = _key_chunk(c, lp, t_idx, True, 1)
                valid = valid if masked else None
                keys = pl.ds(s0, KEYS)
                k = k_ref[0, keys, :].astype(BF16)
                v = v_ref[0, keys, :].astype(BF16)
                a_s, dzs, new = [], [], []
                for h in range(2):
                    left, pre = carry[2 * h], carry[2 * h + 1]
                    lb, l1m_all, l1m, suf = _sb_scores(qs[h], k, valid, after)
                    here = jnp.sum(l1m, axis=1, keepdims=True)
                    a = _where(valid, jnp.exp(lb + suf + (tots[h] - left - here)), 0.0)
                    w = a * _dot_nt(dos[h], v)
                    dz = _where(valid, w * jnp.exp(l1m_all) - (pre + _tri_dot(w, before)) * jnp.exp(lb), 0.0)
                    new += [left + here, pre + jnp.sum(w, axis=1, keepdims=True)]
                    a_s.append(a.astype(BF16))
                    dzs.append(dz.astype(BF16))
                dk_ref[0, keys, :] += _dot_tn(jnp.concatenate(dzs, axis=0), q_st)
                dv_ref[0, keys, :] += _dot_tn(jnp.concatenate(a_s, axis=0), do_st)
                dq = carry[4] + jnp.dot(jnp.concatenate(dzs, axis=1), _stack_halves(k, lo), preferred_element_type=F32)
                return (*new, dq)

            zero = jnp.zeros((bq, 1), F32)
            out = _walk_chunks(r0, n, kchunk, (zero, zero, zero, zero, jnp.zeros((bq, BLOCK), F32)))
            dq_ref[0, rows, :] = out[4] * HEAD_SCALE

        _for_query_tiles(nb, qtile)

    def col(first):
        return pl.BlockSpec((1, lp, 2 * HEAD), lambda bi, hp: (bi, 0, first // (2 * HEAD) + hp))

    shp = jax.ShapeDtypeStruct((b, lp, SB_HEADS * HEAD), F32)
    return pl.pallas_call(
        body, grid=(b, npair), in_specs=[col(EV_Q), col(EV_K), col(EV_V), col(0), col(0)], out_specs=[col(0)] * 3, out_shape=[shp] * 3,
        compiler_params=_params(("parallel", "parallel")), name=name,
    )(proj3, proj3, proj3, tot, do)


def _rope_tables(lp):
    half = MLA_ROPE // 2
    pos = (np.arange(lp) - N_PAD).astype(np.float32)
    inv = jnp.asarray(ROPE_BASE, F32) ** (-jnp.arange(half, dtype=F32) / half)
    ang = jnp.asarray(pos)[:, None] * inv[None, :]
    cos, sin = jnp.cos(ang), jnp.sin(ang)
    zeros = lambda n: jnp.zeros((lp, n), F32)
    c = jnp.concatenate([jnp.ones((lp, MLA_NOPE), F32), cos, cos, zeros(32)], axis=1)
    s1 = jnp.concatenate([zeros(MLA_NOPE), -sin, zeros(half), zeros(32)], axis=1)
    s2 = jnp.concatenate([zeros(MLA_NOPE), zeros(half), sin, zeros(32)], axis=1)
    return c, s1, s2


def _rope(x, c, s1, s2):
    half = MLA_ROPE // 2
    return x * c + pltpu.roll(x, BLOCK - half, 1) * s1 + pltpu.roll(x, half, 1) * s2


def _rope_t(dy, c, s1, s2):
    half = MLA_ROPE // 2
    return dy * c + pltpu.roll(dy * s1, half, 1) + pltpu.roll(dy * s2, BLOCK - half, 1)


def _rms_rows(x, g):
    r = lax.rsqrt(jnp.mean(x * x, axis=-1, keepdims=True) + NORM_EPS)
    return x * r, r


def _prep_rows(lp):
    return lp // 4 if lp % 64 == 0 else BLOCK


def _mla_prep_fwd(proj3, gq, gkv, wq, wk, wv, tabs, name):
    b, lp, _ = proj3.shape
    rows = _prep_rows(lp)
    hw = MLA_HEADS * BLOCK

    def body(cq_ref, ckv_ref, kr_ref, gq_ref, gkv_ref, wq_ref, wk_ref, wv_ref, c_ref, s1_ref, s2_ref, qf_ref, kf_ref, v_ref):
        c, s1, s2 = c_ref[...], s1_ref[...], s2_ref[...]
        xq, _ = _rms_rows(cq_ref[0], None)
        qh = _dot(xq * gq_ref[...], wq_ref[...])
        xk, _ = _rms_rows(ckv_ref[0], None)
        ckv_n = xk * gkv_ref[...]
        kv = _dot(ckv_n, wk_ref[...])
        v_ref[0] = _dot(ckv_n, wv_ref[...]).astype(v_ref.dtype)
        kr = _rope(kr_ref[0], c, s1, s2)
        for h in range(MLA_HEADS):
            ls = slice(h * BLOCK, (h + 1) * BLOCK)
            qf_ref[0, :, ls] = _rope(qh[:, ls], c, s1, s2).astype(qf_ref.dtype)
            kf_ref[0, :, ls] = (kv[:, ls] + kr).astype(kf_ref.dtype)

    def col(first, width):
        return pl.BlockSpec((1, rows, width), lambda bi, n: (bi, n, first // width))

    def whole(a):
        return pl.BlockSpec(a.shape, lambda bi, n: (0,) * a.ndim)

    tab = pl.BlockSpec((rows, BLOCK), lambda bi, n: (n, 0))
    return pl.pallas_call(
        body, grid=(b, lp // rows),
        in_specs=[col(EV_CQ, MLA_Q_LORA), col(EV_CKV, MLA_KV_LORA), col(EV_KR, BLOCK), whole(gq), whole(gkv), whole(wq), whole(wk),
                  whole(wv), tab, tab, tab],
        out_specs=[col(0, hw), col(0, hw), col(0, MLA_HEADS * HEAD)],
        out_shape=[jax.ShapeDtypeStruct((b, lp, hw), BF16), jax.ShapeDtypeStruct((b, lp, hw), BF16),
                   jax.ShapeDtypeStruct((b, lp, MLA_HEADS * HEAD), BF16)],
        compiler_params=_params(("parallel", "parallel")), name=name,
    )(proj3, proj3, proj3, gq, gkv, wq, wk, wv, *tabs)


def _mla_prep_bwd(proj3, gq, gkv, wq, wk, wv, tabs, dqf, dkf, dv, name):
    b, lp, _ = proj3.shape
    rows = _prep_rows(lp)
    hw = MLA_HEADS * BLOCK

    def body(cq_ref, ckv_ref, gq_ref, gkv_ref, wq_ref, wk_ref, wv_ref, c_ref, s1_ref, s2_ref, dqf_ref, dkf_ref, dv_ref,
             dcq_ref, dckv_ref, dkr_ref, dwq_ref, dwk_ref, dwv_ref, dgq_ref, dgkv_ref, dqh):
        @pl.when((pl.program_id(0) == 0) & (pl.program_id(1) == 0))
        def _():
            for r in (dwq_ref, dwk_ref, dwv_ref, dgq_ref, dgkv_ref):
                r[...] = jnp.zeros_like(r)

        c, s1, s2 = c_ref[...], s1_ref[...], s2_ref[...]
        dkr = jnp.zeros((rows, BLOCK), F32)
        for h in range(MLA_HEADS):
            ls = slice(h * BLOCK, (h + 1) * BLOCK)
            dqh[:, ls] = _rope_t(dqf_ref[0, :, ls].astype(F32), c, s1, s2).astype(dqh.dtype)
            dkr = dkr + dkf_ref[0, :, ls].astype(F32)
        dkr_ref[0] = _rope_t(dkr, c, s1, s2).astype(dkr_ref.dtype)

        def norm_bwd(x, g, dy, dg_ref):
            xr, r = _rms_rows(x, None)
            u = dy * g
            dg_ref[...] += jnp.sum(dy * xr, axis=0, keepdims=True)
            return r * (u - xr * jnp.mean(u * xr, axis=-1, keepdims=True))

        xq, _ = _rms_rows(cq_ref[0], None)
        cq_n = xq * gq_ref[...]
        dwq_ref[...] += _dot_tn(cq_n, dqh[...])
        dcq_ref[0] = norm_bwd(cq_ref[0], gq_ref[...], _dot_nt(dqh[...], wq_ref[...]), dgq_ref).astype(dcq_ref.dtype)
        xk, _ = _rms_rows(ckv_ref[0], None)
        ckv_n = xk * gkv_ref[...]
        dkf_, dv_ = dkf_ref[0], dv_ref[0]
        dwk_ref[...] += _dot_tn(ckv_n, dkf_)
        dwv_ref[...] += _dot_tn(ckv_n, dv_)
        dckv_n = _dot_nt(dkf_, wk_ref[...]) + _dot_nt(dv_, wv_ref[...])
        dckv_ref[0] = norm_bwd(ckv_ref[0], gkv_ref[...], dckv_n, dgkv_ref).astype(dckv_ref.dtype)

    def col(first, width):
        return pl.BlockSpec((1, rows, width), lambda bi, n: (bi, n, first // width))

    def whole(a):
        return pl.BlockSpec(a.shape, lambda bi, n: (0,) * len(a.shape))

    tab = pl.BlockSpec((rows, BLOCK), lambda bi, n: (n, 0))
    acc_shapes = [jax.ShapeDtypeStruct(a.shape, F32) for a in (wq, wk, wv, gq, gkv)]
    return pl.pallas_call(
        body, grid=(b, lp // rows),
        in_specs=[col(EV_CQ, MLA_Q_LORA), col(EV_CKV, MLA_KV_LORA), whole(gq), whole(gkv), whole(wq), whole(wk), whole(wv), tab, tab, tab,
                  col(0, hw), col(0, hw), col(0, MLA_HEADS * HEAD)],
        out_specs=[col(0, MLA_Q_LORA), col(0, MLA_KV_LORA), col(0, BLOCK)] + [whole(a) for a in acc_shapes],
        out_shape=[jax.ShapeDtypeStruct((b, lp, MLA_Q_LORA), BF16), jax.ShapeDtypeStruct((b, lp, MLA_KV_LORA), BF16),
                   jax.ShapeDtypeStruct((b, lp, BLOCK), BF16)] + acc_shapes,
        scratch_shapes=[pltpu.VMEM((rows, hw), BF16)],
        compiler_params=_params(("arbitrary", "arbitrary")), name=name,
    )(proj3, proj3, gq, gkv, wq, wk, wv, *tabs, dqf, dkf, dv)


def _mla_fwd(qf, kf, v, name):
    b, lp, _ = qf.shape
    nb = lp // BLOCK
    npair = MLA_HEADS // 2
    scale = (MLA_NOPE + MLA_ROPE) ** -0.5
    starts = _chunk_starts(lp)

    def body(q_ref, k_ref, v_ref, o_ref, lse_ref, vt_ref):
        for c, s0 in enumerate(starts):
            vt_ref[c] = _split_rows_t(v_ref[0, s0:s0 + KEYS, :].astype(F32))

        def qtile(r0, bq, n):
            qs = [q_ref[0, pl.ds(r0, bq), h * BLOCK:(h + 1) * BLOCK] for h in range(2)]
            t_idx = r0 + _iota2((KEYS, bq), 1)

            def kchunk(c, carry, masked):
                stats, acc = carry[:4], carry[4]
                s0, valid = _key_chunk(c, lp, t_idx, False, 0)
                valid = valid if masked else None
                ps, new, alphas = [], [], []
                for h in range(2):
                    m, l = stats[2 * h], stats[2 * h + 1]
                    s = _where(valid, _dot_nt(k_ref[0, pl.ds(s0, KEYS), h * BLOCK:(h + 1) * BLOCK], qs[h]) * scale, NEG)
                    m_new = jnp.maximum(m, jnp.max(s, axis=0, keepdims=True))
                    p = _where(valid, jnp.exp(s - m_new), 0.0)
                    alpha = jnp.exp(m - m_new)
                    new += [m_new, alpha * l + jnp.sum(p, axis=0, keepdims=True)]
                    alphas.append(alpha)
                    ps.append(p.astype(BF16))
                pv = jnp.dot(vt_ref[c], jnp.concatenate(ps, axis=0), preferred_element_type=F32)
                return (*new, _rows_of_pair(alphas[0], alphas[1]) * acc + pv)

            neg, zero = jnp.full((1, bq), NEG, F32), jnp.zeros((1, bq), F32)
            m_a, l_a, m_b, l_b, acc = _walk_chunks(r0, n, kchunk, (neg, zero, neg, zero, jnp.zeros((BLOCK, bq), F32)))
            safe = [jnp.where(l > 0.0, l, 1.0) for l in (l_a, l_b)]
            o_ref[0, pl.ds(r0, bq), :] = (acc / _rows_of_pair(safe[0], safe[1])).T
            lse = [jnp.where(l > 0.0, m + jnp.log(sf), 0.0) for m, l, sf in ((m_a, l_a, safe[0]), (m_b, l_b, safe[1]))]
            _put_rows(lse_ref, r0, bq, lse[0], lse[1])

        _for_query_tiles(nb, qtile)

    wide = pl.BlockSpec((1, lp, 2 * BLOCK), lambda bi, hp: (bi, 0, hp))
    thin = pl.BlockSpec((1, lp, 2 * HEAD), lambda bi, hp: (bi, 0, hp))
    return pl.pallas_call(
        body, grid=(b, npair), in_specs=[wide, wide, thin], out_specs=[thin, _pair_stat_spec(nb)],
        out_shape=[jax.ShapeDtypeStruct((b, lp, MLA_HEADS * HEAD), F32), jax.ShapeDtypeStruct((b, npair, nb, 8, BLOCK), F32)],
        scratch_shapes=[pltpu.VMEM((len(starts), BLOCK, 2 * KEYS), BF16)],
        compiler_params=_params(("parallel", "parallel")), name=name,
    )(qf, kf, v)


def _mla_bwd(qf, kf, v, o, lse, do, name):
    b, lp, _ = qf.shape
    nb = lp // BLOCK
    npair = MLA_HEADS // 2
    scale = (MLA_NOPE + MLA_ROPE) ** -0.5

    starts = _chunk_starts(lp)

    def body(q_ref, k_ref, v_ref, o_ref, lse_ref, do_ref, dq_ref, dk_ref, dv_ref, kt_ref):
        lo = _lo_lanes()
        dk_ref[...] = jnp.zeros_like(dk_ref)
        dv_ref[...] = jnp.zeros_like(dv_ref)
        for c, s0 in enumerate(starts):
            for h in range(2):
                kt_ref[c, h] = k_ref[0, s0:s0 + KEYS, h * BLOCK:(h + 1) * BLOCK].astype(F32).T.astype(BF16)

        def qtile(r0, bq, n):
            rows = pl.ds(r0, bq)
            qs = [q_ref[0, rows, h * BLOCK:(h + 1) * BLOCK] for h in range(2)]
            do_i = do_ref[0, rows, :]
            dos = _halves(do_i.astype(BF16), lo)
            do_st = jnp.concatenate(dos, axis=0)
            both = (do_i * o_ref[0, rows, :]).T
            dsum = (jnp.sum(both[:HEAD], axis=0, keepdims=True), jnp.sum(both[HEAD:], axis=0, keepdims=True))
            lses = _get_rows(lse_ref, r0, bq)
            t_idx = r0 + _iota2((KEYS, bq), 1)

            def kchunk(c, dqts, masked):
                s0, valid = _key_chunk(c, lp, t_idx, False, 0)
                valid = valid if masked else None
                keys = pl.ds(s0, KEYS)
                v_c = v_ref[0, keys, :]
                ps, out = [], []
                for h in range(2):
                    lanes = slice(h * BLOCK, (h + 1) * BLOCK)
                    s = _dot_nt(k_ref[0, keys, lanes], qs[h]) * scale
                    p = _where(valid, jnp.exp(s - lses[h]), 0.0)
                    ds = (p * (_dot_nt(v_c, dos[h]) - dsum[h]) * scale).astype(BF16)
                    dk_ref[0, keys, lanes] += jnp.dot(ds, qs[h], preferred_element_type=F32)
                    out.append(dqts[h] + jnp.dot(kt_ref[c, h], ds, preferred_element_type=F32))
                    ps.append(p.astype(BF16))
                dv_ref[0, keys, :] += jnp.dot(jnp.concatenate(ps, axis=1), do_st, preferred_element_type=F32)
                return tuple(out)

            zero = jnp.zeros((BLOCK, bq), F32)
            dq_a, dq_b = _walk_chunks(r0, n, kchunk, (zero, zero))
            dq_ref[0, rows, 0:BLOCK] = dq_a.T
            dq_ref[0, rows, BLOCK:2 * BLOCK] = dq_b.T

        _for_query_tiles(nb, qtile)

    wide = pl.BlockSpec((1, lp, 2 * BLOCK), lambda bi, hp: (bi, 0, hp))
    thin = pl.BlockSpec((1, lp, 2 * HEAD), lambda bi, hp: (bi, 0, hp))
    return pl.pallas_call(
        body, grid=(b, npair), in_specs=[wide, wide, thin, thin, _pair_stat_spec(nb), thin], out_specs=[wide, wide, thin],
        out_shape=[jax.ShapeDtypeStruct(qf.shape, F32), jax.ShapeDtypeStruct(qf.shape, F32), jax.ShapeDtypeStruct(v.shape, F32)],
        scratch_shapes=[pltpu.VMEM((len(starts), 2, BLOCK, KEYS), BF16)],
        compiler_params=_params(("parallel", "parallel")), name=name,
    )(qf, kf, v, o, lse, do)


SWA_KEYS = 2 * BLOCK + N_META


def _swa_keys(k_ref, v_ref, n, kv):
    prev = jnp.maximum(n - 1, 0)
    rows = lambda blk: pl.ds(pl.multiple_of(blk * BLOCK, BLOCK), BLOCK)
    mine = (_iota2((1, BLOCK), 1) >= HEAD).astype(jnp.int32) == kv

    def both_halves(ref):
        x = jnp.concatenate([ref[0, rows(prev), :], ref[0, rows(n), :], ref[0, N_PAD:BLOCK, :]], axis=0)
        return jnp.where(mine, x, pltpu.roll(x, HEAD, 1))

    slot = _iota2((SWA_KEYS, BLOCK), 0)
    s_idx = jnp.where(slot < 2 * BLOCK, (n - 1) * BLOCK + slot, slot - 2 * BLOCK + N_PAD)
    dist = n * BLOCK + _iota2((SWA_KEYS, BLOCK), 1) - s_idx
    band = (slot < 2 * BLOCK) & (dist >= 0) & (dist < SWA_WINDOW) & (s_idx >= BLOCK)
    meta = (slot >= 2 * BLOCK) & (dist >= 0)
    return both_halves(k_ref), both_halves(v_ref), band | meta, dist.astype(F32), prev


def _pad_keys(x):
    return jnp.concatenate([x, jnp.zeros((3 * BLOCK - SWA_KEYS, x.shape[1]), x.dtype)], axis=0)


def _swa_probs(q_h, kdup, valid, dist, head, sink_ref):
    slope = jnp.exp(jnp.full((1, 1), -8.0 * math.log(2.0) / SWA_HEADS, F32) * (head + 1).astype(F32))
    s = jnp.where(valid, _dot_nt(kdup, q_h) - slope * dist, NEG)
    sink = sink_ref[pl.ds(head, 1), 0:1]
    m = jnp.maximum(jnp.max(s, axis=0, keepdims=True), sink)
    e = jnp.where(valid, jnp.exp(s - m), 0.0)
    es = jnp.exp(sink - m)
    inv = 1.0 / (jnp.sum(e, axis=0, keepdims=True) + es)
    return e * inv, es * inv


SWA_PAIRS = SWA_HEADS // SWA_KV_HEADS // 2
SWA_GROUP = SWA_PAIRS * 2 * HEAD


def _swa_specs(b, lp):
    nb = lp // BLOCK
    qcol = lambda first: pl.BlockSpec((1, BLOCK, SWA_GROUP), lambda bi, kv, n: (bi, n, first // SWA_GROUP + kv))
    kcol = lambda first: pl.BlockSpec((1, lp, BLOCK), lambda bi, kv, n: (bi, 0, first // BLOCK))
    sink = pl.BlockSpec((SWA_HEADS, BLOCK), lambda bi, kv, n: (0, 0))
    return (b, SWA_KV_HEADS, nb), qcol, kcol, sink


def _swa_fwd(proj3, sinks, name):
    b, lp, _ = proj3.shape
    grid, qcol, kcol, sink = _swa_specs(b, lp)

    def body(q_ref, k_ref, v_ref, sink_ref, o_ref):
        kv, n = pl.program_id(1), pl.program_id(2)
        lo = _lo_lanes()
        kdup, vdup, valid, dist, _ = _swa_keys(k_ref, v_ref, n, kv)
        kdup = kdup.astype(BF16)
        vt = _split_rows_t(_pad_keys(vdup))
        for p in range(SWA_PAIRS):
            lanes = slice(p * BLOCK, (p + 1) * BLOCK)
            qs = _halves((q_ref[0, :, lanes] * HEAD_SCALE).astype(BF16), lo)
            probs = [_swa_probs(qs[hh], kdup, valid, dist, (kv * SWA_PAIRS + p) * 2 + hh, sink_ref)[0].astype(BF16) for hh in range(2)]
            o_ref[0, :, lanes] = jnp.dot(vt, jnp.concatenate([_pad_keys(pr) for pr in probs], axis=0), preferred_element_type=F32).T

    return pl.pallas_call(
        body, grid=grid, in_specs=[qcol(OD_Q), kcol(OD_K), kcol(OD_V), sink], out_specs=qcol(0),
        out_shape=jax.ShapeDtypeStruct((b, lp, SWA_HEADS * HEAD), F32),
        compiler_params=_params(("parallel", "parallel", "parallel")), name=name,
    )(proj3, proj3, proj3, sinks)


def _swa_bwd(proj3, sinks, do, name):
    b, lp, _ = proj3.shape
    nb = lp // BLOCK
    grid, qcol, kcol, sink = _swa_specs(b, lp)

    def body(q_ref, k_ref, v_ref, sink_ref, do_ref, dq_ref, dk_ref, dv_ref, dsink_ref, dk_acc, dv_acc):
        kv, n = pl.program_id(1), pl.program_id(2)

        @pl.when((n == 0) & (pl.program_id(0) == 0) & (kv == 0))
        def _():
            dsink_ref[...] = jnp.zeros_like(dsink_ref)

        @pl.when(n == 0)
        def _():
            dk_acc[...] = jnp.zeros_like(dk_acc)
            dv_acc[...] = jnp.zeros_like(dv_acc)

        lo = _lo_lanes()
        kdup, vdup, valid, dist, prev = _swa_keys(k_ref, v_ref, n, kv)
        kt = _split_rows_t(_pad_keys(kdup))
        kdup, vdup = kdup.astype(BF16), vdup.astype(BF16)
        dkc = jnp.zeros((SWA_KEYS, BLOCK), F32)
        dvc = jnp.zeros((SWA_KEYS, BLOCK), F32)
        for p in range(SWA_PAIRS):
            lanes = slice(p * BLOCK, (p + 1) * BLOCK)
            qs = _halves((q_ref[0, :, lanes] * HEAD_SCALE).astype(BF16), lo)
            dos = _halves(do_ref[0, :, lanes].astype(BF16), lo)
            dss, prs = [], []
            for hh in range(2):
                head = (kv * SWA_PAIRS + p) * 2 + hh
                pr, ps = _swa_probs(qs[hh], kdup, valid, dist, head, sink_ref)
                dp = _dot_nt(vdup, dos[hh])
                dsum = jnp.sum(pr * dp, axis=0, keepdims=True)
                dsink_ref[pl.ds(head, 1), :] += jnp.broadcast_to(-jnp.sum(ps * dsum, axis=1, keepdims=True), (1, BLOCK))
                dss.append((pr * (dp - dsum)).astype(BF16))
                prs.append(pr.astype(BF16))
            dq_ref[0, :, lanes] = jnp.dot(kt, jnp.concatenate([_pad_keys(d) for d in dss], axis=0), preferred_element_type=F32).T * HEAD_SCALE
            dkc = dkc + jnp.dot(jnp.concatenate(dss, axis=1), jnp.concatenate(qs, axis=0), preferred_element_type=F32)
            dvc = dvc + jnp.dot(jnp.concatenate(prs, axis=1), jnp.concatenate(dos, axis=0), preferred_element_type=F32)
        rows = lambda blk: pl.ds(pl.multiple_of(blk * BLOCK, BLOCK), BLOCK)
        for r, part in ((rows(prev), slice(0, BLOCK)), (rows(n), slice(BLOCK, 2 * BLOCK)), (slice(N_PAD, BLOCK), slice(2 * BLOCK, SWA_KEYS))):
            dk_acc[r, :] += dkc[part]
            dv_acc[r, :] += dvc[part]

        for acc, ref in ((dk_acc, dk_ref), (dv_acc, dv_ref)):
            @pl.when((n == nb - 1) & (kv == 0))
            def _():
                x = acc[...]
                ref[0] = x + pltpu.roll(x, HEAD, 1)

            @pl.when((n == nb - 1) & (kv == 1))
            def _():
                x = acc[...]
                ref[0] = jnp.where(lo, ref[0], x + pltpu.roll(x, HEAD, 1))

    kvout = pl.BlockSpec((1, lp, BLOCK), lambda bi, kv, n: (bi, 0, 0))
    kvshape = jax.ShapeDtypeStruct((b, lp, BLOCK), F32)
    return pl.pallas_call(
        body, grid=grid, in_specs=[qcol(OD_Q), kcol(OD_K), kcol(OD_V), sink, qcol(0)], out_specs=[qcol(0), kvout, kvout, sink],
        out_shape=[jax.ShapeDtypeStruct((b, lp, SWA_HEADS * HEAD), F32), kvshape, kvshape, jax.ShapeDtypeStruct((SWA_HEADS, BLOCK), F32)],
        scratch_shapes=[pltpu.VMEM((lp, BLOCK), F32), pltpu.VMEM((lp, BLOCK), F32)],
        compiler_params=_params(("arbitrary", "arbitrary", "arbitrary")), name=name,
    )(proj3, proj3, proj3, sinks, do)


def _kernel_weights(ev_w_in, ev_w_uq, ev_w_ukv, od_w_in):
    zeros = lambda r, c: jnp.zeros((r, c), ev_w_in.dtype)
    q_sb, k_sb, v_sb, g_sb, c_q, c_kv, k_r, g_mla = jnp.split(ev_w_in, [512, 1024, 1536, 2048, 2304, 2432, 2464], axis=1)
    w0 = jnp.concatenate([g_sb, g_mla, q_sb, k_sb, v_sb, c_q, c_kv, zeros(D_MODEL, MLA_NOPE), k_r, zeros(D_MODEL, 32)], axis=1)
    uq = ev_w_uq.reshape(MLA_Q_LORA, MLA_HEADS, MLA_NOPE + MLA_ROPE)
    wq = jnp.pad(uq, ((0, 0), (0, 0), (0, BLOCK - MLA_NOPE - MLA_ROPE))).reshape(MLA_Q_LORA, MLA_HEADS * BLOCK)
    ukv = ev_w_ukv.reshape(MLA_KV_LORA, MLA_HEADS, BLOCK)
    wk = jnp.pad(ukv[:, :, :MLA_NOPE], ((0, 0), (0, 0), (0, BLOCK - MLA_NOPE))).reshape(MLA_KV_LORA, MLA_HEADS * BLOCK)
    wv = ukv[:, :, MLA_NOPE:].reshape(MLA_KV_LORA, MLA_HEADS * HEAD)
    q, k, v, g = jnp.split(od_w_in, [1024, 1152, 1280], axis=1)
    w1 = jnp.concatenate([g, q, k, v], axis=1)
    return w0, wq, wk, wv, w1


def _od_w_in_grad(dw1):
    sl = lambda a, first, n: a[:, first:first + n]
    return jnp.concatenate([sl(dw1, OD_Q, 1024), sl(dw1, OD_K, 128), sl(dw1, OD_V, 128), sl(dw1, OD_G, 1024)], axis=1)


def _original_grads(dw0, dwq, dwk, dwv):
    sl = lambda a, first, n: a[:, first:first + n]
    d_ev_w_in = jnp.concatenate([sl(dw0, EV_Q, 512), sl(dw0, EV_K, 512), sl(dw0, EV_V, 512), sl(dw0, EV_G, 512), sl(dw0, EV_CQ, 256),
                                 sl(dw0, EV_CKV, 128), sl(dw0, EV_KR + MLA_NOPE, MLA_ROPE), sl(dw0, EV_G + 512, 512)], axis=1)
    d_uq = dwq.reshape(MLA_Q_LORA, MLA_HEADS, BLOCK)[:, :, :MLA_NOPE + MLA_ROPE].reshape(MLA_Q_LORA, -1)
    d_ukv = jnp.concatenate([dwk.reshape(MLA_KV_LORA, MLA_HEADS, BLOCK)[:, :, :MLA_NOPE], dwv.reshape(MLA_KV_LORA, MLA_HEADS, HEAD)],
                            axis=2).reshape(MLA_KV_LORA, -1)
    return d_ev_w_in, d_uq, d_ukv


def _meta_rows_sum(dh0_3):
    b, _, d = dh0_3.shape

    def body(x_ref, o_ref):
        acc = x_ref[0, N_PAD:BLOCK, :]
        for i in range(1, b):
            acc = acc + x_ref[i, N_PAD:BLOCK, :]
        o_ref[...] = acc

    return pl.pallas_call(
        body, grid=(1,), in_specs=[pl.BlockSpec((b, BLOCK, d), lambda i: (0, 0, 0))], out_specs=pl.BlockSpec((N_META, d), lambda i: (0, 0)),
        out_shape=jax.ShapeDtypeStruct((N_META, d), F32), compiler_params=_params(("arbitrary",)), name="meta_rows_sum",
    )(dh0_3)


def _local_step(x, meta, norm_g, final_g, gq, gkv, sinks, target, ev_w_in, ev_w_uq, ev_w_ukv, wo0, od_w_in, wo1, reduce_early=None):
    b, seq, d = x.shape
    lp = seq + BLOCK
    t = b * lp
    w0, wq, wk, wv, w1 = _kernel_weights(ev_w_in, ev_w_uq, ev_w_ukv, od_w_in)
    h0 = jnp.concatenate([jnp.zeros((b, N_PAD, d), F32), jnp.broadcast_to(meta[None], (b, N_META, d)), x], axis=1).reshape(t, d)
    tabs = _rope_tables(lp)
    g0, g1 = norm_g[0:1], norm_g[1:2]

    hn0 = _rms_fwd(h0, g0, "norm0")
    proj0 = _mm(hn0, w0, "inproj0")
    p0 = proj0.reshape(b, lp, EV_N)
    o_sb, sb_tot = _sb_fwd(p0, "sb_fwd")
    qf, kf, v = _mla_prep_fwd(p0, gq, gkv, wq, wk, wv, tabs, "mla_prep_fwd")
    o_mla, lse = _mla_fwd(qf, kf, v, "mla_fwd")
    o0 = [o_sb.reshape(t, -1), o_mla.reshape(t, -1)]
    ao0 = _gate_fwd(o0, proj0, "gate0")
    h1 = _mm(ao0, wo0, "outproj0", res=h0)

    hn1 = _rms_fwd(h1, g1, "norm1")
    proj1 = _mm(hn1, w1, "inproj1")
    p1 = proj1.reshape(b, lp, OD_N)
    sinks_b = jnp.broadcast_to(sinks.reshape(SWA_HEADS, 1), (SWA_HEADS, BLOCK))
    o1 = _swa_fwd(p1, sinks_b, "swa_fwd").reshape(t, -1)
    ao1 = _gate_fwd([o1], proj1, "gate1")
    h2 = _mm(ao1, wo1, "outproj1", res=h1)

    dh2, d_final_g, loss = _loss_head(h2, final_g.reshape(1, d), target, b, lp)

    d_wo1 = _mm_tn(ao1, dh2, "d_wo1")
    dao1 = _mm_nt(dh2, wo1, "d_ao1")
    (do1,), dg1 = _gate_bwd(dao1, [o1], proj1, "gate1_bwd")
    dq1, dk4, dv4, d_sinks = _swa_bwd(p1, sinks_b, do1.reshape(b, lp, -1), "swa_bwd")
    unheads = lambda a: a.reshape(t, SWA_KV_HEADS * HEAD).astype(BF16)
    dproj1 = jnp.concatenate([dg1, dq1.reshape(t, -1).astype(BF16), unheads(dk4), unheads(dv4)], axis=1)
    d_w1 = _mm_tn(hn1, dproj1, "d_w1")
    dhn1 = _mm_nt(dproj1, w1, "d_hn1")
    dh1, d_g1 = _rms_bwd(h1, g1, dhn1, dh2, "norm1_bwd")

    d_wo0 = _mm_tn(ao0, dh1, "d_wo0")
    early = dict(od_w_in=_od_w_in_grad(d_w1), ev_w_out=d_wo0, od_w_out=d_wo1)
    dao0 = _mm_nt(dh1, wo0, "d_ao0")
    if reduce_early is not None:
        pending, finish = reduce_early(early)
        dao0, pending = lax.optimization_barrier((dao0, pending))
        early_done = finish(pending)
    (do_sb, do_mla), dg0 = _gate_bwd(dao0, o0, proj0, "gate0_bwd")
    dq_sb, dk_sb, dv_sb = _sb_bwd(p0, sb_tot, do_sb.reshape(b, lp, -1), "sb_bwd")
    dqf, dkf, dv = _mla_bwd(qf, kf, v, o_mla, lse, do_mla.reshape(b, lp, -1), "mla_bwd")
    dcq, dckv, dkr, d_wq, d_wk, d_wv, d_gq, d_gkv = _mla_prep_bwd(p0, gq, gkv, wq, wk, wv, tabs, dqf, dkf, dv, "mla_prep_bwd")
    flat = lambda a: a.reshape(t, -1).astype(BF16)
    dproj0 = jnp.concatenate([dg0, flat(dq_sb), flat(dk_sb), flat(dv_sb), flat(dcq), flat(dckv), flat(dkr)], axis=1)
    if reduce_early is not None:
        dproj0, early_done = lax.optimization_barrier((dproj0, early_done))
    d_w0 = _mm_tn(hn0, dproj0, "d_w0")
    d_ev_w_in, d_uq, d_ukv = _original_grads(d_w0, d_wq, d_wk, d_wv)
    dhn0 = _mm_nt(dproj0, w0, "d_hn0")
    dh0, d_g0 = _rms_bwd(h0, g0, dhn0, dh1, "norm0_bwd")
    dh0 = dh0.reshape(b, lp, d)

    grads = dict(meta=_meta_rows_sum(dh0), norm_g=jnp.concatenate([d_g0, d_g1], axis=0), final_g=d_final_g.reshape(d),
                 ev_w_in=d_ev_w_in, ev_q_norm_g=d_gq, ev_kv_norm_g=d_gkv, ev_w_uq=d_uq, ev_w_ukv=d_ukv,
                 od_sinks=d_sinks[:, 0].reshape(1, SWA_HEADS))
    if reduce_early is None:
        return loss, dh0[:, BLOCK:], {**grads, **early}
    return loss, dh0[:, BLOCK:], grads, early_done


MESH = pl.DeviceIdType.MESH
ANY = pl.BlockSpec(memory_space=pl.ANY)


def _place():
    return lax.axis_index("x"), lax.axis_index("y"), lax.axis_index("c")


def _other_chips(x, y):
    return [(1 - x, y), (x, 1 - y), (1 - x, 1 - y)]


def _with_own_slot(slots, own):
    me = 2 * lax.axis_index("x") + lax.axis_index("y")
    return lax.dynamic_update_slice(slots, own[None], (me,) + (0,) * own.ndim)


def _gather_weights(packs, meta, name):
    n = len(packs)

    def body(*refs):
        ins, m_ref, outs, mo_ref = refs[:n], refs[n], refs[n + 1:2 * n + 1], refs[2 * n + 1]
        send_sems, recv_sems = refs[2 * n + 2:]
        x, y, c = _place()
        me, sib = 2 * x + y, (x, y, 1 - c)
        chips = _other_chips(x, y)

        def copy(k, src, dst, to):
            return pltpu.make_async_remote_copy(src_ref=src, dst_ref=dst, send_sem=send_sems.at[k], recv_sem=recv_sems.at[k], device_id=to,
                                                device_id_type=MESH)

        def half(i, chip, h):
            rows = packs[i].shape[0] // 2
            return outs[i].at[chip, pl.ds(h * rows, rows), :]

        def mine(i):
            rows = packs[i].shape[0] // 2
            return ins[i].at[pl.ds(c * rows, rows), :]

        sent = [copy(6 * i + k, mine(i), half(i, me, c), (px, py, c)) for i in range(n) for k, (px, py) in enumerate(chips)]
        sent += [copy(6 * n + k, m_ref, mo_ref.at[me], (px, py, c)) for k, (px, py) in enumerate(chips)]
        for cp in sent:
            cp.start()
        for i in range(n):
            for k, (px, py) in enumerate(chips):
                landed = half(i, 2 * px + py, c)
                copy(6 * i + k, mine(i), landed, (px, py, c)).wait_recv()
                fwd = copy(6 * i + 3 + k, landed, landed, sib)
                fwd.start()
                sent.append(fwd)
        for k, (px, py) in enumerate(chips):
            for i in range(n):
                other = half(i, 2 * px + py, 1 - c)
                copy(6 * i + 3 + k, other, other, sib).wait_recv()
            copy(6 * n + k, m_ref, mo_ref.at[2 * px + py], (px, py, c)).wait_recv()
        for cp in sent:
            cp.wait_send()

    nsem = 6 * n + 3
    res = pl.pallas_call(
        body, in_specs=[ANY] * (n + 1), out_specs=[ANY] * (n + 1),
        out_shape=[jax.ShapeDtypeStruct((N_CHIPS,) + a.shape, a.dtype) for a in list(packs) + [meta]],
        scratch_shapes=[pltpu.SemaphoreType.DMA((nsem,)), pltpu.SemaphoreType.DMA((nsem,))],
        name=name,
    )(*packs, meta)
    return [_with_own_slot(r, a) for r, a in zip(res[:n], packs)], _with_own_slot(res[n], meta)


def _grads_to_sibling(gs, name):
    n = len(gs)

    def body(*refs):
        ins, outs, send_sems, recv_sems = refs[:n], refs[n:2 * n], refs[2 * n], refs[2 * n + 1]
        x, y, c = _place()
        cps = []
        for i in range(n):
            rows = gs[i].shape[1] // 2
            cps.append(pltpu.make_async_remote_copy(src_ref=ins[i].at[:, pl.ds((1 - c) * rows, rows), :], dst_ref=outs[i],
                                                    send_sem=send_sems.at[i], recv_sem=recv_sems.at[i], device_id=(x, y, 1 - c),
                                                    device_id_type=MESH))
        for cp in cps:
            cp.start()
        for cp in cps:
            cp.wait()

    return pl.pallas_call(
        body, in_specs=[ANY] * n, out_specs=[ANY] * n,
        out_shape=[jax.ShapeDtypeStruct((g.shape[0], g.shape[1] // 2, g.shape[2]), g.dtype) for g in gs],
        scratch_shapes=[pltpu.SemaphoreType.DMA((n,)), pltpu.SemaphoreType.DMA((n,))],
        name=name,
    )(*gs)


def _share_halves(rs, name):
    n = len(rs)

    def body(*refs):
        ins, outs, send_sems, recv_sems = refs[:n], refs[n:2 * n], refs[2 * n], refs[2 * n + 1]
        x, y, c = _place()
        cps = [pltpu.make_async_remote_copy(src_ref=ins[i], dst_ref=outs[i], send_sem=send_sems.at[i], recv_sem=recv_sems.at[i],
                                            device_id=(x, y, 1 - c), device_id_type=MESH) for i in range(n)]
        for cp in cps:
            cp.start()
        for cp in cps:
            cp.wait()

    theirs = pl.pallas_call(
        body, in_specs=[ANY] * n, out_specs=[ANY] * n, out_shape=[jax.ShapeDtypeStruct(r.shape, r.dtype) for r in rs],
        scratch_shapes=[pltpu.SemaphoreType.DMA((n,)), pltpu.SemaphoreType.DMA((n,))],
        name=name,
    )(*rs)
    first = lax.axis_index("c") == 0
    return [jnp.where(first, jnp.concatenate([r, t], axis=0), jnp.concatenate([t, r], axis=0)) for r, t in zip(rs, theirs)]


def _chip_scatter(ss, name):
    n = len(ss)

    def body(*refs):
        ins, outs, send_sems, recv_sems = refs[:n], refs[n:2 * n], refs[2 * n], refs[2 * n + 1]
        x, y, c = _place()
        me = 2 * x + y
        chips = _other_chips(x, y)
        for i in range(n):
            for k, (px, py) in enumerate(chips):
                pltpu.make_async_remote_copy(src_ref=ins[i].at[2 * px + py], dst_ref=outs[i].at[me], send_sem=send_sems.at[3 * i + k],
                                             recv_sem=recv_sems.at[3 * i + k], device_id=(px, py, c), device_id_type=MESH).start()
        for i in range(n):
            for k, (px, py) in enumerate(chips):
                cp = pltpu.make_async_remote_copy(src_ref=ins[i].at[2 * px + py], dst_ref=outs[i].at[2 * px + py],
                                                  send_sem=send_sems.at[3 * i + k], recv_sem=recv_sems.at[3 * i + k],
                                                  device_id=(px, py, c), device_id_type=MESH)
                cp.wait_recv()
                cp.wait_send()

    parts = pl.pallas_call(
        body, in_specs=[ANY] * n, out_specs=[ANY] * n, out_shape=[jax.ShapeDtypeStruct(s.shape, s.dtype) for s in ss],
        scratch_shapes=[pltpu.SemaphoreType.DMA((3 * n,)), pltpu.SemaphoreType.DMA((3 * n,))],
        name=name,
    )(*ss)
    me = 2 * lax.axis_index("x") + lax.axis_index("y")
    return [_with_own_slot(p, lax.dynamic_index_in_dim(s, me, axis=0, keepdims=False)) for p, s in zip(parts, ss)]


HBM_SPACE = pltpu.MemorySpace.HBM


def _on_sequencer(name, collective_id, n_sems, body):
    @pl.kernel(mesh=plsc.ScalarSubcoreMesh(axis_name="sequencer", num_cores=1), name=name,
               scratch_types=(pltpu.SemaphoreType.DMA((n_sems,)), pltpu.SemaphoreType.DMA((n_sems,))),
               compiler_params=pltpu.CompilerParams(collective_id=collective_id))
    def launch(send_sems, recv_sems):
        body(send_sems, recv_sems)

    launch()


def _handshake(peers):
    barrier = pltpu.get_barrier_semaphore()
    for peer in peers:
        pl.semaphore_signal(barrier, inc=1, device_id=peer, device_id_type=MESH)
    pl.semaphore_wait(barrier, len(peers))


def _gather_on_sequencer(packs, name):
    n = len(packs)
    ins = [jax.new_ref(p, memory_space=HBM_SPACE) for p in packs]
    outs = [jax.empty_ref(jax.ShapeDtypeStruct((N_CHIPS,) + p.shape, p.dtype), memory_space=HBM_SPACE) for p in packs]

    def body(send_sems, recv_sems):
        x, y, c = _place()
        me, sib = 2 * x + y, (x, y, 1 - c)
        chips = _other_chips(x, y)
        _handshake([(px, py, c) for px, py in chips] + [sib])

        def copy(k, src, dst, to):
            return pltpu.make_async_remote_copy(src_ref=src, dst_ref=dst, send_sem=send_sems.at[k], recv_sem=recv_sems.at[k], device_id=to,
                                                device_id_type=MESH)

        def half(i, chip, h):
            rows = packs[i].shape[0] // 2
            return outs[i].at[chip, pl.ds(h * rows, rows), :]

        def mine(i):
            rows = packs[i].shape[0] // 2
            return ins[i].at[pl.ds(c * rows, rows), :]

        sent = [copy(6 * i + k, mine(i), half(i, me, c), (px, py, c)) for i in range(n) for k, (px, py) in enumerate(chips)]
        for cp in sent:
            cp.start()
        for i in range(n):
            for k, (px, py) in enumerate(chips):
                landed = half(i, 2 * px + py, c)
                copy(6 * i + k, mine(i), landed, (px, py, c)).wait_recv()
                fwd = copy(6 * i + 3 + k, landed, landed, sib)
                fwd.start()
                sent.append(fwd)
        for i in range(n):
            for k, (px, py) in enumerate(chips):
                other = half(i, 2 * px + py, 1 - c)
                copy(6 * i + 3 + k, other, other, sib).wait_recv()
        for cp in sent:
            cp.wait_send()

    _on_sequencer(name, 1, 6 * n, body)
    return [_with_own_slot(o[...], p) for o, p in zip(outs, packs)]


def _grads_to_sibling_on_sequencer(gs, name, collective_id):
    n = len(gs)
    ins = [jax.new_ref(g, memory_space=HBM_SPACE) for g in gs]
    outs = [jax.empty_ref(jax.ShapeDtypeStruct((g.shape[0], g.shape[1] // 2, g.shape[2]), g.dtype), memory_space=HBM_SPACE) for g in gs]

    def body(send_sems, recv_sems):
        x, y, c = _place()
        _handshake([(x, y, 1 - c)])
        cps = []
        for i in range(n):
            rows = gs[i].shape[1] // 2
            cps.append(pltpu.make_async_remote_copy(src_ref=ins[i].at[:, pl.ds((1 - c) * rows, rows), :], dst_ref=outs[i],
                                                    send_sem=send_sems.at[i], recv_sem=recv_sems.at[i], device_id=(x, y, 1 - c),
                                                    device_id_type=MESH))
        for cp in cps:
            cp.start()
        for cp in cps:
            cp.wait()

    _on_sequencer(name, collective_id, n, body)
    return [o[...] for o in outs]


def _chip_scatter_on_sequencer(ss, name, collective_id):
    n = len(ss)
    ins = [jax.new_ref(s, memory_space=HBM_SPACE) for s in ss]
    outs = [jax.empty_ref(jax.ShapeDtypeStruct(s.shape, s.dtype), memory_space=HBM_SPACE) for s in ss]

    def body(send_sems, recv_sems):
        x, y, c = _place()
        me = 2 * x + y
        chips = _other_chips(x, y)
        _handshake([(px, py, c) for px, py in chips])
        for i in range(n):
            for k, (px, py) in enumerate(chips):
                pltpu.make_async_remote_copy(src_ref=ins[i].at[2 * px + py], dst_ref=outs[i].at[me], send_sem=send_sems.at[3 * i + k],
                                             recv_sem=recv_sems.at[3 * i + k], device_id=(px, py, c), device_id_type=MESH).start()
        for i in range(n):
            for k, (px, py) in enumerate(chips):
                cp = pltpu.make_async_remote_copy(src_ref=ins[i].at[2 * px + py], dst_ref=outs[i].at[2 * px + py],
                                                  send_sem=send_sems.at[3 * i + k], recv_sem=recv_sems.at[3 * i + k],
                                                  device_id=(px, py, c), device_id_type=MESH)
                cp.wait_recv()
                cp.wait_send()

    _on_sequencer(name, collective_id, 3 * n, body)
    me = 2 * lax.axis_index("x") + lax.axis_index("y")
    return [_with_own_slot(o[...], lax.dynamic_index_in_dim(s, me, axis=0, keepdims=False)) for o, s in zip(outs, ss)]


def _all_reduce_small(v, name):
    shape = v.shape

    def body(v_ref, o_ref, slots, send_sems, recv_sems):
        x, y, c = _place()
        me = 4 * x + 2 * y + c
        slots[me] = v_ref[...]
        for r in range(1, N_DEV):
            peer = (x ^ (r >> 2), y ^ ((r >> 1) & 1), c ^ (r & 1))
            pltpu.make_async_remote_copy(src_ref=v_ref, dst_ref=slots.at[me], send_sem=send_sems.at[r - 1], recv_sem=recv_sems.at[r - 1],
                                         device_id=peer, device_id_type=MESH).start()
        for r in range(1, N_DEV):
            peer = (x ^ (r >> 2), y ^ ((r >> 1) & 1), c ^ (r & 1))
            cp = pltpu.make_async_remote_copy(src_ref=v_ref, dst_ref=slots.at[4 * peer[0] + 2 * peer[1] + peer[2]], send_sem=send_sems.at[r - 1],
                                              recv_sem=recv_sems.at[r - 1], device_id=peer, device_id_type=MESH)
            cp.wait_recv()
            cp.wait_send()
        acc = slots[0]
        for d in range(1, N_DEV):
            acc = acc + slots[d]
        o_ref[...] = acc

    vm = pl.BlockSpec(memory_space=pltpu.VMEM)
    return pl.pallas_call(
        body, in_specs=[vm], out_specs=vm, out_shape=jax.ShapeDtypeStruct(shape, F32),
        scratch_shapes=[pltpu.VMEM((N_DEV,) + shape, F32), pltpu.SemaphoreType.DMA((N_DEV - 1,)), pltpu.SemaphoreType.DMA((N_DEV - 1,))],
        name=name,
    )(v)


def _add_sibling(g, gsib, core, name):
    n, _, cdim = g.shape
    half = gsib.shape[1]
    tr = half // 2

    def body(core_ref, a_ref, b_ref, o_ref):
        o_ref[...] = (a_ref[...] + b_ref[...]).astype(o_ref.dtype)

    blk = pl.BlockSpec((1, tr, cdim), lambda j, i, core_ref: (j, i, 0))
    return pl.pallas_call(
        body,
        grid_spec=pltpu.PrefetchScalarGridSpec(
            num_scalar_prefetch=1, grid=(n, half // tr),
            in_specs=[pl.BlockSpec((1, tr, cdim), lambda j, i, core_ref: (j, core_ref[0] * (half // tr) + i, 0)), blk], out_specs=blk),
        out_shape=jax.ShapeDtypeStruct(gsib.shape, BF16), compiler_params=_params(("parallel", "parallel")), name=name,
    )(core, g, gsib)


def _sum_parts(parts, name):
    n, r, cdim = parts.shape
    tr = r // 2

    def body(p_ref, o_ref):
        acc = p_ref[0].astype(F32)
        for j in range(1, n):
            acc = acc + p_ref[j].astype(F32)
        o_ref[...] = acc

    return pl.pallas_call(
        body, grid=(r // tr,), in_specs=[pl.BlockSpec((n, tr, cdim), lambda i: (0, i, 0))],
        out_specs=pl.BlockSpec((tr, cdim), lambda i: (i, 0)), out_shape=jax.ShapeDtypeStruct((r, cdim), F32),
        compiler_params=_params(("parallel",)), name=name,
    )(parts)


def _adamw(parts, w, m, v, name):
    npart, r, cdim = parts.shape
    tr = r // 4 if r % 32 == 0 else r

    def body(p_ref, w_ref, m_ref, v_ref, g_ref, d_ref, nm_ref, nv_ref):
        g = p_ref[0]
        for j in range(1, npart):
            g = g + p_ref[j]
        m_new = ADAM_B1 * m_ref[...] + (1.0 - ADAM_B1) * g
        v_new = ADAM_B2 * v_ref[...] + (1.0 - ADAM_B2) * (g * g)
        m_hat = m_new / (1.0 - ADAM_B1 ** ADAM_STEP)
        v_hat = v_new / (1.0 - ADAM_B2 ** ADAM_STEP)
        g_ref[...] = g
        d_ref[...] = -ADAM_LR * (m_hat / (jnp.sqrt(v_hat) + ADAM_EPS) + ADAM_WD * w_ref[...])
        nm_ref[...] = m_new
        nv_ref[...] = v_new

    blk = pl.BlockSpec((tr, cdim), lambda i: (i, 0))
    shp = jax.ShapeDtypeStruct((r, cdim), F32)
    return pl.pallas_call(
        body, grid=(r // tr,), in_specs=[pl.BlockSpec((npart, tr, cdim), lambda i: (0, i, 0)), blk, blk, blk], out_specs=[blk] * 4,
        out_shape=[shp] * 4, compiler_params=_params(("parallel",)), name=name,
    )(parts, w, m, v)


BIG = ("ev_w_in", "ev_w_uq", "ev_w_ukv", "ev_w_out", "od_w_in", "od_w_out", "meta")
SMALL = ("norm_g", "final_g", "ev_q_norm_g", "ev_kv_norm_g", "od_sinks")
SMALL_SHAPE = (8, 512)
BY_ROWS = ("ev_w_out", "od_w_out")

EV_IN_SHARD, OD_IN_SHARD, UQ_SHARD = 2976 // N_CHIPS, 2304 // N_CHIPS, 768 // N_CHIPS


def _pack_big(a, lead=()):
    return _pack_first(a, lead) + _pack_later(a, lead)


def _pad_lanes(x, width):
    return jnp.concatenate([x, jnp.zeros(x.shape[:-1] + (width - x.shape[-1],), x.dtype)], axis=-1)


def _pack_latent(a, lead=()):
    ax = len(lead)
    corner = jnp.concatenate([a["ev_w_ukv"], a["meta"], jnp.zeros(lead + (256 - MLA_KV_LORA - N_META, 256), a["meta"].dtype)], axis=ax)
    return jnp.concatenate([_pad_lanes(a["ev_w_uq"], 256), corner], axis=ax + 1)


def _pack_first(a, lead=()):
    return _pad_lanes(a["ev_w_in"], 768), _pack_latent(a, lead)


def _pack_later(a, lead=()):
    return _pad_lanes(a["od_w_in"], 640), jnp.concatenate([a["ev_w_out"], a["od_w_out"]], axis=len(lead) + 1)


N_FIRST = 2
LATER = ("od_w_in", "ev_w_out", "od_w_out")


def _unpack_big(p_in0, p_lat, p_in1, p_out):
    return dict(ev_w_in=p_in0[..., :EV_IN_SHARD], od_w_in=p_in1[..., :OD_IN_SHARD], ev_w_out=p_out[..., :D_MODEL],
                od_w_out=p_out[..., D_MODEL:], ev_w_uq=p_lat[..., :UQ_SHARD], ev_w_ukv=p_lat[..., :MLA_KV_LORA, 256:],
                meta=p_lat[..., MLA_KV_LORA:MLA_KV_LORA + N_META, 256:])


def _chip_shards(full, by_rows):
    if by_rows:
        return full.reshape(N_CHIPS, full.shape[0] // N_CHIPS, full.shape[1])
    return full.reshape(full.shape[0], N_CHIPS, -1).transpose(1, 0, 2)


def _from_chip_shards(slots, by_rows):
    if by_rows:
        return slots.reshape(-1, slots.shape[2])
    return slots.transpose(1, 0, 2).reshape(slots.shape[1], -1)


def _pack_small(arrs, extra=None):
    flat = [a.reshape(-1) for a in arrs] + ([] if extra is None else [extra.reshape(-1)])
    used = sum(f.shape[0] for f in flat)
    return jnp.pad(jnp.concatenate(flat), (0, SMALL_SHAPE[0] * SMALL_SHAPE[1] - used)).reshape(SMALL_SHAPE)


def _unpack_small(p, shapes):
    flat, out, at = p.reshape(-1), [], 0
    for s in shapes:
        n = int(np.prod(s))
        out.append(flat[at:at + n].reshape(s))
        at += n
    return out, flat[at]


def kernel(x, meta, norm_g, final_g, ev_w_in, ev_q_norm_g, ev_kv_norm_g, ev_w_uq, ev_w_ukv, ev_w_out, od_w_in, od_sinks, od_w_out, loss_target, m_meta, m_norm_g, m_final_g, m_ev_w_in, m_ev_q_norm_g, m_ev_kv_norm_g, m_ev_w_uq, m_ev_w_ukv, m_ev_w_out, m_od_w_in, m_od_sinks, m_od_w_out, v_meta, v_norm_g, v_final_g, v_ev_w_in, v_ev_q_norm_g, v_ev_kv_norm_g, v_ev_w_uq, v_ev_w_ukv, v_ev_w_out, v_od_w_in, v_od_sinks, v_od_w_out):
    given = dict(locals())
    two_d = lambda a: a[0] if a.ndim == 3 else a
    packs = {k: _pack_big({n: two_d(given[k + n]) for n in BIG}) for k in ("", "m_", "v_")}

    wbf = [p.astype(BF16) for p in packs[""]]
    later = _gather_on_sequencer(wbf[N_FIRST:], "gather_later_weights")
    first, meta_all = _gather_weights(wbf[:N_FIRST], meta, "gather_weights")
    full = {n: _from_chip_shards(a, n in BY_ROWS) for n, a in _unpack_big(*first, *later).items()}
    meta_full = _from_chip_shards(meta_all, False)

    core = lax.axis_index("c").astype(jnp.int32).reshape(1)
    shards = lambda g: {n: _chip_shards(a, n in BY_ROWS) for n, a in g.items()}

    def sums_of(group, tag, to_sibling):
        return [_add_sibling(g, s, core, f"add_sibling_{tag}{i}") for i, (g, s) in enumerate(zip(group, to_sibling(group)))]

    def update(sums, tag, first_pack, scatter):
        parts = scatter(sums, "grads_to_chips_" + tag)
        reduced = _share_halves([_sum_parts(p, f"add_chips_{tag}{i}") for i, p in enumerate(parts)], "reduced_to_sibling_" + tag)
        return [_adamw(r[None], packs[""][first_pack + i], packs["m_"][first_pack + i], packs["v_"][first_pack + i],
                       f"adamw_matrices_{first_pack + i}") for i, r in enumerate(reduced)]

    def behind(tag, first_pack, ids):
        def start(packed):
            sums = sums_of(packed, tag, lambda p: _grads_to_sibling_on_sequencer(p, "grads_to_sibling_" + tag, ids[0]))
            return sums, lambda s: update(s, tag, first_pack, lambda t, name: _chip_scatter_on_sequencer(t, name, ids[1]))
        return start

    loss, grad_x, grads, updated_later = _local_step(
        x, meta_full, norm_g, final_g, ev_q_norm_g, ev_kv_norm_g, od_sinks, loss_target, full["ev_w_in"], full["ev_w_uq"],
        full["ev_w_ukv"], full["ev_w_out"], full["od_w_in"], full["od_w_out"],
        reduce_early=lambda g: behind("later", N_FIRST, (3, 2))(_pack_later(shards(g), (N_CHIPS,))))

    first = _pack_first(shards({n: grads[n] for n in BIG if n not in LATER}), (N_CHIPS,))
    updated_first = update(sums_of(first, "first", lambda p: _grads_to_sibling(p, "grads_to_sibling_first")), "first", 0, _chip_scatter)
    updated = updated_first + updated_later
    big_out = [{n: a.reshape(given[n].shape) for n, a in _unpack_big(*outs).items()} for outs in zip(*updated)]

    small_shapes = [given[n].shape for n in SMALL]
    ssum = _all_reduce_small(_pack_small([grads[n] for n in SMALL], loss[0, 0]), "reduce_vectors")
    small_out = _adamw(ssum[None], _pack_small([given[n] for n in SMALL]), _pack_small([given["m_" + n] for n in SMALL]),
                       _pack_small([given["v_" + n] for n in SMALL]), "adamw_vectors")
    total_loss = ssum.reshape(-1)[sum(int(np.prod(s)) for s in small_shapes)]
    small_out = [_unpack_small(o, small_shapes)[0] for o in small_out]

    names = ("meta", "norm_g", "final_g", "ev_w_in", "ev_q_norm_g", "ev_kv_norm_g", "ev_w_uq", "ev_w_ukv", "ev_w_out", "od_w_in", "od_sinks",
             "od_w_out")
    outs = [total_loss, grad_x]
    for kind in range(4):
        for n in names:
            outs.append(big_out[kind][n] if n in BIG else small_out[kind][SMALL.index(n)])
    return tuple(outs)
```

```python
import functools
import math

import numpy as np
import jax
import jax.numpy as jnp
from jax import lax
from jax.experimental import pallas as pl
from jax.experimental.pallas import tpu as pltpu
from jax.experimental.pallas import tpu_sc as plsc

F32 = jnp.float32
BF16 = jnp.bfloat16

D_MODEL = 1024
BLOCK = 128
N_META = 16
N_PAD = BLOCK - N_META
NORM_EPS = 1e-6
NEG = -1e30
HEAD = 64
SB_HEADS = 8
MLA_HEADS = 8
MLA_Q_LORA = 256
MLA_KV_LORA = 128
MLA_NOPE = 64
MLA_ROPE = 32
ROPE_BASE = 10000.0
SWA_HEADS = 16
SWA_KV_HEADS = 2
SWA_WINDOW = 128
N_CHIPS = 4
N_DEV = 8

ADAM_LR = 0.001
ADAM_B1 = 0.9
ADAM_B2 = 0.999
ADAM_EPS = 1e-08
ADAM_WD = 0.01
ADAM_STEP = 10

VMEM_LIMIT = 48 * 1024 * 1024

EV_G, EV_Q, EV_K, EV_V, EV_CQ, EV_CKV, EV_KR, EV_N = 0, 1024, 1536, 2048, 2560, 2816, 2944, 3072
OD_G, OD_Q, OD_K, OD_V, OD_N = 0, 1024, 2048, 2176, 2304


def _params(sem=None):
    return pltpu.CompilerParams(dimension_semantics=sem, vmem_limit_bytes=VMEM_LIMIT)


def _row_tile(m):
    return 256 if m % 256 == 0 else 128


def _matmul_rows(m):
    for c in (1088, 1024, 768, 640, 512, 384, 256):
        if m % c == 0:
            return c
    return 128


def _dot(a, b):
    return jnp.dot(a.astype(BF16), b.astype(BF16), preferred_element_type=F32)


def _dot_nt(a, b):
    return lax.dot_general(a.astype(BF16), b.astype(BF16), (((1,), (1,)), ((), ())), preferred_element_type=F32)


def _dot_tn(a, b):
    return lax.dot_general(a.astype(BF16), b.astype(BF16), (((0,), (0,)), ((), ())), preferred_element_type=F32)


def _rms_fwd(h, g, name):
    t, d = h.shape
    tm = _row_tile(t)

    def body(h_ref, g_ref, o_ref):
        x = h_ref[...]
        r = lax.rsqrt(jnp.mean(x * x, axis=-1, keepdims=True) + NORM_EPS)
        o_ref[...] = ((x * r) * g_ref[...]).astype(o_ref.dtype)

    return pl.pallas_call(
        body, grid=(t // tm,),
        in_specs=[pl.BlockSpec((tm, d), lambda i: (i, 0)), pl.BlockSpec((1, d), lambda i: (0, 0))],
        out_specs=pl.BlockSpec((tm, d), lambda i: (i, 0)),
        out_shape=jax.ShapeDtypeStruct((t, d), BF16), compiler_params=_params(("parallel",)), name=name,
    )(h, g)


def _rms_bwd(h, g, dy, dres, name):
    t, d = h.shape
    tm = _row_tile(t)

    def body(h_ref, g_ref, dy_ref, dres_ref, dh_ref, dg_ref):
        @pl.when(pl.program_id(0) == 0)
        def _():
            dg_ref[...] = jnp.zeros_like(dg_ref)

        x = h_ref[...]
        r = lax.rsqrt(jnp.mean(x * x, axis=-1, keepdims=True) + NORM_EPS)
        xr = x * r
        dy_ = dy_ref[...]
        u = dy_ * g_ref[...]
        dh_ref[...] = dres_ref[...] + r * (u - xr * jnp.mean(u * xr, axis=-1, keepdims=True))
        dg_ref[...] += jnp.sum(dy_ * xr, axis=0, keepdims=True)

    row = pl.BlockSpec((tm, d), lambda i: (i, 0))
    vec = pl.BlockSpec((1, d), lambda i: (0, 0))
    return pl.pallas_call(
        body, grid=(t // tm,), in_specs=[row, vec, row, row], out_specs=[row, vec],
        out_shape=[jax.ShapeDtypeStruct((t, d), F32), jax.ShapeDtypeStruct((1, d), F32)],
        compiler_params=_params(("arbitrary",)), name=name,
    )(h, g, dy, dres)


def _col_tile(n):
    for c in (1024, 768, 640, 512, 384, 256, 128):
        if n % c == 0:
            return c
    return n


def _mm(a, w, name, res=None, out_dtype=F32, a_cols=None):
    m = a.shape[0]
    k, n = w.shape
    a_blk = 0 if a_cols is None else a_cols[0] // k
    assert a_cols is None or (a_cols[1] == k and a_cols[0] % k == 0)
    tm, tn = _matmul_rows(m), _col_tile(n)

    def body(*refs):
        if res is None:
            a_ref, w_ref, o_ref = refs
            acc = _dot(a_ref[...], w_ref[...])
        else:
            a_ref, w_ref, r_ref, o_ref = refs
            acc = r_ref[...] + _dot(a_ref[...], w_ref[...])
        o_ref[...] = acc.astype(o_ref.dtype)

    in_specs = [pl.BlockSpec((tm, k), lambda j, i: (i, a_blk)), pl.BlockSpec((k, tn), lambda j, i: (0, j))]
    args = [a, w]
    if res is not None:
        in_specs.append(pl.BlockSpec((tm, tn), lambda j, i: (i, j)))
        args.append(res)
    return pl.pallas_call(
        body, grid=(n // tn, m // tm), in_specs=in_specs, out_specs=pl.BlockSpec((tm, tn), lambda j, i: (i, j)),
        out_shape=jax.ShapeDtypeStruct((m, n), out_dtype), compiler_params=_params(("parallel", "parallel")), name=name,
    )(*args)


def _mm_nt(a, w, name):
    m, n = a.shape
    k = w.shape[0]
    tm, tk = _matmul_rows(m), _col_tile(k)

    def body(a_ref, w_ref, o_ref):
        o_ref[...] = _dot_nt(a_ref[...], w_ref[...])

    return pl.pallas_call(
        body, grid=(k // tk, m // tm),
        in_specs=[pl.BlockSpec((tm, n), lambda j, i: (i, 0)), pl.BlockSpec((tk, n), lambda j, i: (j, 0))],
        out_specs=pl.BlockSpec((tm, tk), lambda j, i: (i, j)),
        out_shape=jax.ShapeDtypeStruct((m, k), F32), compiler_params=_params(("parallel", "parallel")), name=name,
    )(a, w)


def _mm_tn(x, dy, name):
    m, k = x.shape
    n = dy.shape[1]
    tm, tn = _matmul_rows(m), _col_tile(n)

    def body(x_ref, dy_ref, o_ref):
        @pl.when(pl.program_id(1) == 0)
        def _():
            o_ref[...] = jnp.zeros_like(o_ref)

        o_ref[...] += _dot_tn(x_ref[...], dy_ref[...])

    return pl.pallas_call(
        body, grid=(n // tn, m // tm),
        in_specs=[pl.BlockSpec((tm, k), lambda j, i: (i, 0)), pl.BlockSpec((tm, tn), lambda j, i: (i, j))],
        out_specs=pl.BlockSpec((k, tn), lambda j, i: (0, j)),
        out_shape=jax.ShapeDtypeStruct((k, n), F32), compiler_params=_params(("parallel", "arbitrary")), name=name,
    )(x, dy)


def _silu_parts(g):
    s = 1.0 / (1.0 + jnp.exp(-g))
    return g * s, s * (1.0 + g * (1.0 - s))


def _gate_fwd(o_parts, proj, name):
    t = proj.shape[0]
    tm = _row_tile(t)
    w = D_MODEL // len(o_parts)

    def body(*refs):
        g_ref, o_ref = refs[-2], refs[-1]
        for p, r in enumerate(refs[:-2]):
            sil, _ = _silu_parts(g_ref[:, p * w:(p + 1) * w])
            o_ref[:, p * w:(p + 1) * w] = (r[...].astype(F32) * sil).astype(o_ref.dtype)

    return pl.pallas_call(
        body, grid=(t // tm,),
        in_specs=[pl.BlockSpec((tm, w), lambda i: (i, 0)) for _ in o_parts] + [pl.BlockSpec((tm, D_MODEL), lambda i: (i, 0))],
        out_specs=pl.BlockSpec((tm, D_MODEL), lambda i: (i, 0)),
        out_shape=jax.ShapeDtypeStruct((t, D_MODEL), BF16), compiler_params=_params(("parallel",)), name=name,
    )(*o_parts, proj)


def _gate_bwd(dao, o_parts, proj, name):
    t = proj.shape[0]
    tm = _row_tile(t)
    np_ = len(o_parts)
    w = D_MODEL // np_

    def body(*refs):
        dao_ref, g_ref = refs[0], refs[1 + np_]
        do_refs, dg_ref = refs[2 + np_:2 + 2 * np_], refs[-1]
        for p in range(np_):
            sl = slice(p * w, (p + 1) * w)
            sil, dsil = _silu_parts(g_ref[:, sl])
            da = dao_ref[:, sl]
            do_refs[p][...] = da * sil
            dg_ref[:, sl] = (da * refs[1 + p][...].astype(F32) * dsil).astype(dg_ref.dtype)

    full = pl.BlockSpec((tm, D_MODEL), lambda i: (i, 0))
    part = pl.BlockSpec((tm, w), lambda i: (i, 0))
    outs = pl.pallas_call(
        body, grid=(t // tm,), in_specs=[full] + [part] * np_ + [full], out_specs=[part] * np_ + [full],
        out_shape=[jax.ShapeDtypeStruct((t, w), F32)] * np_ + [jax.ShapeDtypeStruct((t, D_MODEL), BF16)],
        compiler_params=_params(("parallel",)), name=name,
    )(dao, *o_parts, proj)
    return outs[:np_], outs[np_]


def _loss_head(h2, gf, target, b, lp):
    d = h2.shape[1]
    nb = lp // BLOCK
    h3 = h2.reshape(b, lp, d)

    def body(h_ref, g_ref, t_ref, dh_ref, dg_ref, loss_ref):
        first = (pl.program_id(0) == 0) & (pl.program_id(1) == 0)

        @pl.when(first)
        def _():
            dg_ref[...] = jnp.zeros_like(dg_ref)
            loss_ref[...] = jnp.zeros_like(loss_ref)

        @pl.when(pl.program_id(1) == 0)
        def _():
            dh_ref[...] = jnp.zeros_like(dh_ref)

        @pl.when(pl.program_id(1) > 0)
        def _():
            x = h_ref[0]
            r = lax.rsqrt(jnp.mean(x * x, axis=-1, keepdims=True) + NORM_EPS)
            xr = x * r
            g = g_ref[...]
            diff = xr * g - t_ref[0]
            loss_ref[...] += 0.5 * jnp.sum(jnp.mean(diff * diff, axis=-1, keepdims=True))
            dy = diff * (1.0 / d)
            u = dy * g
            dh_ref[0] = r * (u - xr * jnp.mean(u * xr, axis=-1, keepdims=True))
            dg_ref[...] += jnp.sum(dy * xr, axis=0, keepdims=True)

    blk = pl.BlockSpec((1, BLOCK, d), lambda bi, n: (bi, n, 0))
    dh, dg, loss = pl.pallas_call(
        body, grid=(b, nb),
        in_specs=[blk, pl.BlockSpec((1, d), lambda bi, n: (0, 0)),
                  pl.BlockSpec((1, BLOCK, d), lambda bi, n: (bi, jnp.maximum(n - 1, 0), 0))],
        out_specs=[blk, pl.BlockSpec((1, d), lambda bi, n: (0, 0)), pl.BlockSpec((8, 128), lambda bi, n: (0, 0))],
        out_shape=[jax.ShapeDtypeStruct((b, lp, d), F32), jax.ShapeDtypeStruct((1, d), F32), jax.ShapeDtypeStruct((8, 128), F32)],
        compiler_params=_params(("arbitrary", "arbitrary")), name="loss_head",
    )(h3, gf, target)
    return dh.reshape(b * lp, d), dg, loss


def _iota2(shape, dim):
    return lax.broadcasted_iota(jnp.int32, shape, dim)


KEYS = 512
SB_FWD_KEYS = 512


def _lo_lanes():
    return _iota2((1, BLOCK), 1) < HEAD


def _halves(x, lo):
    zero = jnp.zeros_like(x)
    return jnp.where(lo, x, zero), jnp.where(lo, zero, x)


def _rows_of_pair(a, b):
    return jnp.where(_iota2((BLOCK, 1), 0) < HEAD, a, b)


def _split_rows_t(x):
    xt = x.T
    first = _iota2(xt.shape, 0) < HEAD
    zero = jnp.zeros_like(xt)
    return jnp.concatenate([jnp.where(first, xt, zero), jnp.where(first, zero, xt)], axis=1).astype(BF16)


def _key_chunk(c, lp, t_idx, strict, key_axis):
    keys = t_idx.shape[key_axis]
    first = c * keys
    s0 = pl.multiple_of(jnp.minimum(first, lp - keys), BLOCK)
    s_idx = s0 + _iota2(t_idx.shape, key_axis)
    seen = (s_idx < t_idx) if strict else (s_idx <= t_idx)
    return s0, seen & (s_idx >= jnp.maximum(first, N_PAD))


def _tri_dot(x, tri):
    return jnp.dot(x.astype(BF16), tri, preferred_element_type=F32)


def _stack_halves(x, lo):
    a, b = _halves(x, lo)
    return jnp.concatenate([a, b], axis=0)


def _pair(a, b, lo):
    return jnp.where(lo, a, b)


def _chunk_starts(lp):
    return [min(c * KEYS, lp - KEYS) for c in range(-(-lp // KEYS))]


def _put_rows(ref, r0, bq, a, b):
    for t in range(bq // BLOCK):
        part = slice(t * BLOCK, (t + 1) * BLOCK)
        ref[0, 0, r0 // BLOCK + t] = jnp.concatenate([a[:, part], b[:, part], jnp.zeros((6, BLOCK), F32)], axis=0)


def _get_rows(ref, r0, bq):
    return [jnp.concatenate([ref[0, 0, r0 // BLOCK + t, h:h + 1, :] for t in range(bq // BLOCK)], axis=1) for h in range(2)]


def _n_chunks(i, keys):
    return ((i + 1) * BLOCK + keys - 1) // keys


QROWS = 512


def _for_query_tiles(nb, tile, keys=KEYS):
    per = QROWS // BLOCK

    def step(j, _):
        tile(pl.multiple_of(j * QROWS, QROWS), QROWS, (j + 1) * (QROWS // keys))
        return 0

    lax.fori_loop(0, nb // per, step, 0)
    for i in range(nb - nb % per, nb):
        tile(i * BLOCK, BLOCK, _n_chunks(i, keys))


def _walk_chunks(r0, n, chunk, carry, leftwards=False, keys=KEYS):
    diag = jnp.maximum(r0 // keys, 1)

    def span(first, last, masked, carry):
        def step(t, cr):
            return chunk(last - 1 - t if leftwards else first + t, cr, masked)
        return lax.fori_loop(0, last - first, step, carry)

    spans = [(0, 1, True), (1, diag, False), (diag, n, True)]
    for first, last, masked in (reversed(spans) if leftwards else spans):
        carry = span(first, last, masked, carry)
    return carry


def _where(valid, x, other):
    return x if valid is None else jnp.where(valid, x, other)


HEAD_SCALE = HEAD ** -0.5
assert math.frexp(HEAD_SCALE)[0] == 0.5


def _sb_scores(q_h, k, valid, after):
    z = _dot_nt(q_h, k)
    lb = jnp.minimum(z, 0.0) - jnp.log(1.0 + jnp.exp(-jnp.abs(z)))
    l1m_all = lb - z
    l1m = _where(valid, l1m_all, 0.0)
    return lb, l1m_all, l1m, _tri_dot(l1m, after)


def _pair_stat_spec(nb):
    return pl.BlockSpec((1, 1, nb, 8, BLOCK), lambda bi, hp: (bi, hp, 0, 0, 0))


def _sb_fwd(proj3, name):
    b, lp, _ = proj3.shape
    nb = lp // BLOCK
    npair = SB_HEADS // 2

    keys = SB_FWD_KEYS

    def body(q_ref, k_ref, v_ref, o_ref, tot_ref):
        lo = _lo_lanes()
        after = (_iota2((keys, keys), 0) > _iota2((keys, keys), 1)).astype(BF16)

        def qtile(r0, bq, n):
            qs = _halves((q_ref[0, pl.ds(r0, bq), :] * HEAD_SCALE).astype(BF16), lo)
            t_idx = r0 + _iota2((bq, keys), 0)

            def kchunk(c, carry, masked):
                cs, acc = carry[:2], carry[2]
                s0, valid = _key_chunk(c, lp, t_idx, True, 1)
                valid = valid if masked else None
                k = k_ref[0, pl.ds(s0, keys), :].astype(BF16)
                a_s, new = [], []
                for h in range(2):
                    lb, _, l1m, suf = _sb_scores(qs[h], k, valid, after)
                    a_s.append(_where(valid, jnp.exp(lb + suf + cs[h]), 0.0).astype(BF16))
                    new.append(cs[h] + jnp.sum(l1m, axis=1, keepdims=True))
                v_bd = _stack_halves(v_ref[0, pl.ds(s0, keys), :].astype(BF16), lo)
                return (*new, acc + jnp.dot(jnp.concatenate(a_s, axis=1), v_bd, preferred_element_type=F32))

            zero = jnp.zeros((bq, 1), F32)
            c_a, c_b, acc = _walk_chunks(r0, n, kchunk, (zero, zero, jnp.zeros((bq, BLOCK), F32)), leftwards=True, keys=keys)
            o_ref[0, pl.ds(r0, bq), :] = acc
            tot_ref[0, pl.ds(r0, bq), :] = jnp.broadcast_to(_pair(c_a, c_b, lo), (bq, BLOCK))

        _for_query_tiles(nb, qtile, keys)

    def col(first):
        return pl.BlockSpec((1, lp, 2 * HEAD), lambda bi, hp: (bi, 0, first // (2 * HEAD) + hp))

    shp = jax.ShapeDtypeStruct((b, lp, SB_HEADS * HEAD), F32)
    return pl.pallas_call(
        body, grid=(b, npair), in_specs=[col(EV_Q), col(EV_K), col(EV_V)], out_specs=[col(0), col(0)], out_shape=[shp, shp],
        compiler_params=_params(("parallel", "parallel")), name=name,
    )(proj3, proj3, proj3)


def _sb_bwd(proj3, tot, do, name):
    b, lp, _ = proj3.shape
    nb = lp // BLOCK
    npair = SB_HEADS // 2

    def body(q_ref, k_ref, v_ref, tot_ref, do_ref, dq_ref, dk_ref, dv_ref):
        lo = _lo_lanes()
        after = (_iota2((KEYS, KEYS), 0) > _iota2((KEYS, KEYS), 1)).astype(BF16)
        before = (_iota2((KEYS, KEYS), 0) < _iota2((KEYS, KEYS), 1)).astype(BF16)
        dk_ref[...] = jnp.zeros_like(dk_ref)
        dv_ref[...] = jnp.zeros_like(dv_ref)

        def qtile(r0, bq, n):
            rows = pl.ds(r0, bq)
            qs = _halves((q_ref[0, rows, :] * HEAD_SCALE).astype(BF16), lo)
            dos = _halves(do_ref[0, rows, :].astype(BF16), lo)
            tot_i = tot_ref[0, rows, :]
            tots = (tot_i[:, 0:1], tot_i[:, HEAD:HEAD + 1])
            q_st, do_st = jnp.concatenate(qs, axis=0), jnp.concatenate(dos, axis=0)
            t_idx = r0 + _iota2((bq, KEYS), 0)

            def kchunk(c, carry, masked):
                s0, valid = _key_chunk(c, lp, t_idx, True, 1)
                valid = valid if masked else None
                keys = pl.ds(s0, KEYS)
                k = k_ref[0, keys, :].astype(BF16)
                v = v_ref[0, keys, :].astype(BF16)
                a_s, dzs, new = [], [], []
                for h in range(2):
                    left, pre = carry[2 * h], carry[2 * h + 1]
                    lb, l1m_all, l1m, suf = _sb_scores(qs[h], k, valid, after)
                    here = jnp.sum(l1m, axis=1, keepdims=True)
                    a = _where(valid, jnp.exp(lb + suf + (tots[h] - left - here)), 0.0)
                    w = a * _dot_nt(dos[h], v)
                    dz = _where(valid, w * jnp.exp(l1m_all) - (pre + _tri_dot(w, before)) * jnp.exp(lb), 0.0)
                    new += [left + here, pre + jnp.sum(w, axis=1, keepdims=True)]
                    a_s.append(a.astype(BF16))
                    dzs.append(dz.astype(BF16))
                dk_ref[0, keys, :] += _dot_tn(jnp.concatenate(dzs, axis=0), q_st)
                dv_ref[0, keys, :] += _dot_tn(jnp.concatenate(a_s, axis=0), do_st)
                dq = carry[4] + jnp.dot(jnp.concatenate(dzs, axis=1), _stack_halves(k, lo), preferred_element_type=F32)
                return (*new, dq)

            zero = jnp.zeros((bq, 1), F32)
            out = _walk_chunks(r0, n, kchunk, (zero, zero, zero, zero, jnp.zeros((bq, BLOCK), F32)))
            dq_ref[0, rows, :] = out[4] * HEAD_SCALE

        _for_query_tiles(nb, qtile)

    def col(first):
        return pl.BlockSpec((1, lp, 2 * HEAD), lambda bi, hp: (bi, 0, first // (2 * HEAD) + hp))

    shp = jax.ShapeDtypeStruct((b, lp, SB_HEADS * HEAD), F32)
    return pl.pallas_call(
        body, grid=(b, npair), in_specs=[col(EV_Q), col(EV_K), col(EV_V), col(0), col(0)], out_specs=[col(0)] * 3, out_shape=[shp] * 3,
        compiler_params=_params(("parallel", "parallel")), name=name,
    )(proj3, proj3, proj3, tot, do)


def _rope_tables(lp):
    half = MLA_ROPE // 2
    pos = (np.arange(lp) - N_PAD).astype(np.float32)
    inv = jnp.asarray(ROPE_BASE, F32) ** (-jnp.arange(half, dtype=F32) / half)
    ang = jnp.asarray(pos)[:, None] * inv[None, :]
    cos, sin = jnp.cos(ang), jnp.sin(ang)
    zeros = lambda n: jnp.zeros((lp, n), F32)
    c = jnp.concatenate([jnp.ones((lp, MLA_NOPE), F32), cos, cos, zeros(32)], axis=1)
    s1 = jnp.concatenate([zeros(MLA_NOPE), -sin, zeros(half), zeros(32)], axis=1)
    s2 = jnp.concatenate([zeros(MLA_NOPE), zeros(half), sin, zeros(32)], axis=1)
    return c, s1, s2


def _rope(x, c, s1, s2):
    half = MLA_ROPE // 2
    return x * c + pltpu.roll(x, BLOCK - half, 1) * s1 + pltpu.roll(x, half, 1) * s2


def _rope_t(dy, c, s1, s2):
    half = MLA_ROPE // 2
    return dy * c + pltpu.roll(dy * s1, half, 1) + pltpu.roll(dy * s2, BLOCK - half, 1)


def _rms_rows(x, g):
    r = lax.rsqrt(jnp.mean(x * x, axis=-1, keepdims=True) + NORM_EPS)
    return x * r, r


def _prep_rows(lp):
    return lp // 4 if lp % 64 == 0 else BLOCK


def _mla_prep_fwd(proj3, gq, gkv, wq, wk, wv, tabs, name):
    b, lp, _ = proj3.shape
    rows = _prep_rows(lp)
    hw = MLA_HEADS * BLOCK

    def body(cq_ref, ckv_ref, kr_ref, gq_ref, gkv_ref, wq_ref, wk_ref, wv_ref, c_ref, s1_ref, s2_ref, qf_ref, kf_ref, v_ref):
        c, s1, s2 = c_ref[...], s1_ref[...], s2_ref[...]
        xq, _ = _rms_rows(cq_ref[0], None)
        qh = _dot(xq * gq_ref[...], wq_ref[...])
        xk, _ = _rms_rows(ckv_ref[0], None)
        ckv_n = xk * gkv_ref[...]
        kv = _dot(ckv_n, wk_ref[...])
        v_ref[0] = _dot(ckv_n, wv_ref[...]).astype(v_ref.dtype)
        kr = _rope(kr_ref[0], c, s1, s2)
        for h in range(MLA_HEADS):
            ls = slice(h * BLOCK, (h + 1) * BLOCK)
            qf_ref[0, :, ls] = _rope(qh[:, ls], c, s1, s2).astype(qf_ref.dtype)
            kf_ref[0, :, ls] = (kv[:, ls] + kr).astype(kf_ref.dtype)

    def col(first, width):
        return pl.BlockSpec((1, rows, width), lambda bi, n: (bi, n, first // width))

    def whole(a):
        return pl.BlockSpec(a.shape, lambda bi, n: (0,) * a.ndim)

    tab = pl.BlockSpec((rows, BLOCK), lambda bi, n: (n, 0))
    return pl.pallas_call(
        body, grid=(b, lp // rows),
        in_specs=[col(EV_CQ, MLA_Q_LORA), col(EV_CKV, MLA_KV_LORA), col(EV_KR, BLOCK), whole(gq), whole(gkv), whole(wq), whole(wk),
                  whole(wv), tab, tab, tab],
        out_specs=[col(0, hw), col(0, hw), col(0, MLA_HEADS * HEAD)],
        out_shape=[jax.ShapeDtypeStruct((b, lp, hw), BF16), jax.ShapeDtypeStruct((b, lp, hw), BF16),
                   jax.ShapeDtypeStruct((b, lp, MLA_HEADS * HEAD), BF16)],
        compiler_params=_params(("parallel", "parallel")), name=name,
    )(proj3, proj3, proj3, gq, gkv, wq, wk, wv, *tabs)


def _mla_prep_bwd(proj3, gq, gkv, wq, wk, wv, tabs, dqf, dkf, dv, name):
    b, lp, _ = proj3.shape
    rows = _prep_rows(lp)
    hw = MLA_HEADS * BLOCK

    def body(cq_ref, ckv_ref, gq_ref, gkv_ref, wq_ref, wk_ref, wv_ref, c_ref, s1_ref, s2_ref, dqf_ref, dkf_ref, dv_ref,
             dcq_ref, dckv_ref, dkr_ref, dwq_ref, dwk_ref, dwv_ref, dgq_ref, dgkv_ref, dqh):
        @pl.when((pl.program_id(0) == 0) & (pl.program_id(1) == 0))
        def _():
            for r in (dwq_ref, dwk_ref, dwv_ref, dgq_ref, dgkv_ref):
                r[...] = jnp.zeros_like(r)

        c, s1, s2 = c_ref[...], s1_ref[...], s2_ref[...]
        dkr = jnp.zeros((rows, BLOCK), F32)
        for h in range(MLA_HEADS):
            ls = slice(h * BLOCK, (h + 1) * BLOCK)
            dqh[:, ls] = _rope_t(dqf_ref[0, :, ls].astype(F32), c, s1, s2).astype(dqh.dtype)
            dkr = dkr + dkf_ref[0, :, ls].astype(F32)
        dkr_ref[0] = _rope_t(dkr, c, s1, s2).astype(dkr_ref.dtype)

        def norm_bwd(x, g, dy, dg_ref):
            xr, r = _rms_rows(x, None)
            u = dy * g
            dg_ref[...] += jnp.sum(dy * xr, axis=0, keepdims=True)
            return r * (u - xr * jnp.mean(u * xr, axis=-1, keepdims=True))

        xq, _ = _rms_rows(cq_ref[0], None)
        cq_n = xq * gq_ref[...]
        dwq_ref[...] += _dot_tn(cq_n, dqh[...])
        dcq_ref[0] = norm_bwd(cq_ref[0], gq_ref[...], _dot_nt(dqh[...], wq_ref[...]), dgq_ref).astype(dcq_ref.dtype)
        xk, _ = _rms_rows(ckv_ref[0], None)
        ckv_n = xk * gkv_ref[...]
        dkf_, dv_ = dkf_ref[0], dv_ref[0]
        dwk_ref[...] += _dot_tn(ckv_n, dkf_)
        dwv_ref[...] += _dot_tn(ckv_n, dv_)
        dckv_n = _dot_nt(dkf_, wk_ref[...]) + _dot_nt(dv_, wv_ref[...])
        dckv_ref[0] = norm_bwd(ckv_ref[0], gkv_ref[...], dckv_n, dgkv_ref).astype(dckv_ref.dtype)

    def col(first, width):
        return pl.BlockSpec((1, rows, width), lambda bi, n: (bi, n, first // width))

    def whole(a):
        return pl.BlockSpec(a.shape, lambda bi, n: (0,) * len(a.shape))

    tab = pl.BlockSpec((rows, BLOCK), lambda bi, n: (n, 0))
    acc_shapes = [jax.ShapeDtypeStruct(a.shape, F32) for a in (wq, wk, wv, gq, gkv)]
    return pl.pallas_call(
        body, grid=(b, lp // rows),
        in_specs=[col(EV_CQ, MLA_Q_LORA), col(EV_CKV, MLA_KV_LORA), whole(gq), whole(gkv), whole(wq), whole(wk), whole(wv), tab, tab, tab,
                  col(0, hw), col(0, hw), col(0, MLA_HEADS * HEAD)],
        out_specs=[col(0, MLA_Q_LORA), col(0, MLA_KV_LORA), col(0, BLOCK)] + [whole(a) for a in acc_shapes],
        out_shape=[jax.ShapeDtypeStruct((b, lp, MLA_Q_LORA), BF16), jax.ShapeDtypeStruct((b, lp, MLA_KV_LORA), BF16),
                   jax.ShapeDtypeStruct((b, lp, BLOCK), BF16)] + acc_shapes,
        scratch_shapes=[pltpu.VMEM((rows, hw), BF16)],
        compiler_params=_params(("arbitrary", "arbitrary")), name=name,
    )(proj3, proj3, gq, gkv, wq, wk, wv, *tabs, dqf, dkf, dv)


def _mla_fwd(qf, kf, v, name):
    b, lp, _ = qf.shape
    nb = lp // BLOCK
    npair = MLA_HEADS // 2
    scale = (MLA_NOPE + MLA_ROPE) ** -0.5
    starts = _chunk_starts(lp)

    def body(q_ref, k_ref, v_ref, o_ref, lse_ref, vt_ref):
        for c, s0 in enumerate(starts):
            vt_ref[c] = _split_rows_t(v_ref[0, s0:s0 + KEYS, :].astype(F32))

        def qtile(r0, bq, n):
            qs = [q_ref[0, pl.ds(r0, bq), h * BLOCK:(h + 1) * BLOCK] for h in range(2)]
            t_idx = r0 + _iota2((KEYS, bq), 1)

            def kchunk(c, carry, masked):
                stats, acc = carry[:4], carry[4]
                s0, valid = _key_chunk(c, lp, t_idx, False, 0)
                valid = valid if masked else None
                ps, new, alphas = [], [], []
                for h in range(2):
                    m, l = stats[2 * h], stats[2 * h + 1]
                    s = _where(valid, _dot_nt(k_ref[0, pl.ds(s0, KEYS), h * BLOCK:(h + 1) * BLOCK], qs[h]) * scale, NEG)
                    m_new = jnp.maximum(m, jnp.max(s, axis=0, keepdims=True))
                    p = _where(valid, jnp.exp(s - m_new), 0.0)
                    alpha = jnp.exp(m - m_new)
                    new += [m_new, alpha * l + jnp.sum(p, axis=0, keepdims=True)]
                    alphas.append(alpha)
                    ps.append(p.astype(BF16))
                pv = jnp.dot(vt_ref[c], jnp.concatenate(ps, axis=0), preferred_element_type=F32)
                return (*new, _rows_of_pair(alphas[0], alphas[1]) * acc + pv)

            neg, zero = jnp.full((1, bq), NEG, F32), jnp.zeros((1, bq), F32)
            m_a, l_a, m_b, l_b, acc = _walk_chunks(r0, n, kchunk, (neg, zero, neg, zero, jnp.zeros((BLOCK, bq), F32)))
            safe = [jnp.where(l > 0.0, l, 1.0) for l in (l_a, l_b)]
            o_ref[0, pl.ds(r0, bq), :] = (acc / _rows_of_pair(safe[0], safe[1])).T
            lse = [jnp.where(l > 0.0, m + jnp.log(sf), 0.0) for m, l, sf in ((m_a, l_a, safe[0]), (m_b, l_b, safe[1]))]
            _put_rows(lse_ref, r0, bq, lse[0], lse[1])

        _for_query_tiles(nb, qtile)

    wide = pl.BlockSpec((1, lp, 2 * BLOCK), lambda bi, hp: (bi, 0, hp))
    thin = pl.BlockSpec((1, lp, 2 * HEAD), lambda bi, hp: (bi, 0, hp))
    return pl.pallas_call(
        body, grid=(b, npair), in_specs=[wide, wide, thin], out_specs=[thin, _pair_stat_spec(nb)],
        out_shape=[jax.ShapeDtypeStruct((b, lp, MLA_HEADS * HEAD), F32), jax.ShapeDtypeStruct((b, npair, nb, 8, BLOCK), F32)],
        scratch_shapes=[pltpu.VMEM((len(starts), BLOCK, 2 * KEYS), BF16)],
        compiler_params=_params(("parallel", "parallel")), name=name,
    )(qf, kf, v)


def _mla_bwd(qf, kf, v, o, lse, do, name):
    b, lp, _ = qf.shape
    nb = lp // BLOCK
    npair = MLA_HEADS // 2
    scale = (MLA_NOPE + MLA_ROPE) ** -0.5

    starts = _chunk_starts(lp)

    def body(q_ref, k_ref, v_ref, o_ref, lse_ref, do_ref, dq_ref, dk_ref, dv_ref, kt_ref):
        lo = _lo_lanes()
        dk_ref[...] = jnp.zeros_like(dk_ref)
        dv_ref[...] = jnp.zeros_like(dv_ref)
        for c, s0 in enumerate(starts):
            for h in range(2):
                kt_ref[c, h] = k_ref[0, s0:s0 + KEYS, h * BLOCK:(h + 1) * BLOCK].astype(F32).T.astype(BF16)

        def qtile(r0, bq, n):
            rows = pl.ds(r0, bq)
            qs = [q_ref[0, rows, h * BLOCK:(h + 1) * BLOCK] for h in range(2)]
            do_i = do_ref[0, rows, :]
            dos = _halves(do_i.astype(BF16), lo)
            do_st = jnp.concatenate(dos, axis=0)
            both = (do_i * o_ref[0, rows, :]).T
            dsum = (jnp.sum(both[:HEAD], axis=0, keepdims=True), jnp.sum(both[HEAD:], axis=0, keepdims=True))
            lses = _get_rows(lse_ref, r0, bq)
            t_idx = r0 + _iota2((KEYS, bq), 1)

            def kchunk(c, dqts, masked):
                s0, valid = _key_chunk(c, lp, t_idx, False, 0)
                valid = valid if masked else None
                keys = pl.ds(s0, KEYS)
                v_c = v_ref[0, keys, :]
                ps, out = [], []
                for h in range(2):
                    lanes = slice(h * BLOCK, (h + 1) * BLOCK)
                    s = _dot_nt(k_ref[0, keys, lanes], qs[h]) * scale
                    p = _where(valid, jnp.exp(s - lses[h]), 0.0)
                    ds = (p * (_dot_nt(v_c, dos[h]) - dsum[h]) * scale).astype(BF16)
                    dk_ref[0, keys, lanes] += jnp.dot(ds, qs[h], preferred_element_type=F32)
                    out.append(dqts[h] + jnp.dot(kt_ref[c, h], ds, preferred_element_type=F32))
                    ps.append(p.astype(BF16))
                dv_ref[0, keys, :] += jnp.dot(jnp.concatenate(ps, axis=1), do_st, preferred_element_type=F32)
                return tuple(out)

            zero = jnp.zeros((BLOCK, bq), F32)
            dq_a, dq_b = _walk_chunks(r0, n, kchunk, (zero, zero))
            dq_ref[0, rows, 0:BLOCK] = dq_a.T
            dq_ref[0, rows, BLOCK:2 * BLOCK] = dq_b.T

        _for_query_tiles(nb, qtile)

    wide = pl.BlockSpec((1, lp, 2 * BLOCK), lambda bi, hp: (bi, 0, hp))
    thin = pl.BlockSpec((1, lp, 2 * HEAD), lambda bi, hp: (bi, 0, hp))
    return pl.pallas_call(
        body, grid=(b, npair), in_specs=[wide, wide, thin, thin, _pair_stat_spec(nb), thin], out_specs=[wide, wide, thin],
        out_shape=[jax.ShapeDtypeStruct(qf.shape, F32), jax.ShapeDtypeStruct(qf.shape, F32), jax.ShapeDtypeStruct(v.shape, F32)],
        scratch_shapes=[pltpu.VMEM((len(starts), 2, BLOCK, KEYS), BF16)],
        compiler_params=_params(("parallel", "parallel")), name=name,
    )(qf, kf, v, o, lse, do)


SWA_KEYS = 2 * BLOCK + N_META


def _swa_keys(k_ref, v_ref, n, kv):
    prev = jnp.maximum(n - 1, 0)
    rows = lambda blk: pl.ds(pl.multiple_of(blk * BLOCK, BLOCK), BLOCK)
    mine = (_iota2((1, BLOCK), 1) >= HEAD).astype(jnp.int32) == kv

    def both_halves(ref):
        x = jnp.concatenate([ref[0, rows(prev), :], ref[0, rows(n), :], ref[0, N_PAD:BLOCK, :]], axis=0)
        return jnp.where(mine, x, pltpu.roll(x, HEAD, 1))

    slot = _iota2((SWA_KEYS, BLOCK), 0)
    s_idx = jnp.where(slot < 2 * BLOCK, (n - 1) * BLOCK + slot, slot - 2 * BLOCK + N_PAD)
    dist = n * BLOCK + _iota2((SWA_KEYS, BLOCK), 1) - s_idx
    band = (slot < 2 * BLOCK) & (dist >= 0) & (dist < SWA_WINDOW) & (s_idx >= BLOCK)
    meta = (slot >= 2 * BLOCK) & (dist >= 0)
    return both_halves(k_ref), both_halves(v_ref), band | meta, dist.astype(F32), prev


def _pad_keys(x):
    return jnp.concatenate([x, jnp.zeros((3 * BLOCK - SWA_KEYS, x.shape[1]), x.dtype)], axis=0)


def _swa_probs(q_h, kdup, valid, dist, head, sink_ref):
    slope = jnp.exp(jnp.full((1, 1), -8.0 * math.log(2.0) / SWA_HEADS, F32) * (head + 1).astype(F32))
    s = jnp.where(valid, _dot_nt(kdup, q_h) - slope * dist, NEG)
    sink = sink_ref[pl.ds(head, 1), 0:1]
    m = jnp.maximum(jnp.max(s, axis=0, keepdims=True), sink)
    e = jnp.where(valid, jnp.exp(s - m), 0.0)
    es = jnp.exp(sink - m)
    inv = 1.0 / (jnp.sum(e, axis=0, keepdims=True) + es)
    return e * inv, es * inv


SWA_PAIRS = SWA_HEADS // SWA_KV_HEADS // 2
SWA_GROUP = SWA_PAIRS * 2 * HEAD


def _swa_specs(b, lp):
    nb = lp // BLOCK
    qcol = lambda first: pl.BlockSpec((1, BLOCK, SWA_GROUP), lambda bi, kv, n: (bi, n, first // SWA_GROUP + kv))
    kcol = lambda first: pl.BlockSpec((1, lp, BLOCK), lambda bi, kv, n: (bi, 0, first // BLOCK))
    sink = pl.BlockSpec((SWA_HEADS, BLOCK), lambda bi, kv, n: (0, 0))
    return (b, SWA_KV_HEADS, nb), qcol, kcol, sink


def _swa_fwd(proj3, sinks, name):
    b, lp, _ = proj3.shape
    grid, qcol, kcol, sink = _swa_specs(b, lp)

    def body(q_ref, k_ref, v_ref, sink_ref, o_ref):
        kv, n = pl.program_id(1), pl.program_id(2)
        lo = _lo_lanes()
        kdup, vdup, valid, dist, _ = _swa_keys(k_ref, v_ref, n, kv)
        kdup = kdup.astype(BF16)
        vt = _split_rows_t(_pad_keys(vdup))
        for p in range(SWA_PAIRS):
            lanes = slice(p * BLOCK, (p + 1) * BLOCK)
            qs = _halves((q_ref[0, :, lanes] * HEAD_SCALE).astype(BF16), lo)
            probs = [_swa_probs(qs[hh], kdup, valid, dist, (kv * SWA_PAIRS + p) * 2 + hh, sink_ref)[0].astype(BF16) for hh in range(2)]
            o_ref[0, :, lanes] = jnp.dot(vt, jnp.concatenate([_pad_keys(pr) for pr in probs], axis=0), preferred_element_type=F32).T

    return pl.pallas_call(
        body, grid=grid, in_specs=[qcol(OD_Q), kcol(OD_K), kcol(OD_V), sink], out_specs=qcol(0),
        out_shape=jax.ShapeDtypeStruct((b, lp, SWA_HEADS * HEAD), F32),
        compiler_params=_params(("parallel", "parallel", "parallel")), name=name,
    )(proj3, proj3, proj3, sinks)


def _swa_bwd(proj3, sinks, do, name):
    b, lp, _ = proj3.shape
    nb = lp // BLOCK
    grid, qcol, kcol, sink = _swa_specs(b, lp)

    def body(q_ref, k_ref, v_ref, sink_ref, do_ref, dq_ref, dk_ref, dv_ref, dsink_ref, dk_acc, dv_acc):
        kv, n = pl.program_id(1), pl.program_id(2)

        @pl.when((n == 0) & (pl.program_id(0) == 0) & (kv == 0))
        def _():
            dsink_ref[...] = jnp.zeros_like(dsink_ref)

        @pl.when(n == 0)
        def _():
            dk_acc[...] = jnp.zeros_like(dk_acc)
            dv_acc[...] = jnp.zeros_like(dv_acc)

        lo = _lo_lanes()
        kdup, vdup, valid, dist, prev = _swa_keys(k_ref, v_ref, n, kv)
        kt = _split_rows_t(_pad_keys(kdup))
        kdup, vdup = kdup.astype(BF16), vdup.astype(BF16)
        dkc = jnp.zeros((SWA_KEYS, BLOCK), F32)
        dvc = jnp.zeros((SWA_KEYS, BLOCK), F32)
        for p in range(SWA_PAIRS):
            lanes = slice(p * BLOCK, (p + 1) * BLOCK)
            qs = _halves((q_ref[0, :, lanes] * HEAD_SCALE).astype(BF16), lo)
            dos = _halves(do_ref[0, :, lanes].astype(BF16), lo)
            dss, prs = [], []
            for hh in range(2):
                head = (kv * SWA_PAIRS + p) * 2 + hh
                pr, ps = _swa_probs(qs[hh], kdup, valid, dist, head, sink_ref)
                dp = _dot_nt(vdup, dos[hh])
                dsum = jnp.sum(pr * dp, axis=0, keepdims=True)
                dsink_ref[pl.ds(head, 1), :] += jnp.broadcast_to(-jnp.sum(ps * dsum, axis=1, keepdims=True), (1, BLOCK))
                dss.append((pr * (dp - dsum)).astype(BF16))
                prs.append(pr.astype(BF16))
            dq_ref[0, :, lanes] = jnp.dot(kt, jnp.concatenate([_pad_keys(d) for d in dss], axis=0), preferred_element_type=F32).T * HEAD_SCALE
            dkc = dkc + jnp.dot(jnp.concatenate(dss, axis=1), jnp.concatenate(qs, axis=0), preferred_element_type=F32)
            dvc = dvc + jnp.dot(jnp.concatenate(prs, axis=1), jnp.concatenate(dos, axis=0), preferred_element_type=F32)
        rows = lambda blk: pl.ds(pl.multiple_of(blk * BLOCK, BLOCK), BLOCK)
        for r, part in ((rows(prev), slice(0, BLOCK)), (rows(n), slice(BLOCK, 2 * BLOCK)), (slice(N_PAD, BLOCK), slice(2 * BLOCK, SWA_KEYS))):
            dk_acc[r, :] += dkc[part]
            dv_acc[r, :] += dvc[part]

        for acc, ref in ((dk_acc, dk_ref), (dv_acc, dv_ref)):
            @pl.when((n == nb - 1) & (kv == 0))
            def _():
                x = acc[...]
                ref[0] = x + pltpu.roll(x, HEAD, 1)

            @pl.when((n == nb - 1) & (kv == 1))
            def _():
                x = acc[...]
                ref[0] = jnp.where(lo, ref[0], x + pltpu.roll(x, HEAD, 1))

    kvout = pl.BlockSpec((1, lp, BLOCK), lambda bi, kv, n: (bi, 0, 0))
    kvshape = jax.ShapeDtypeStruct((b, lp, BLOCK), F32)
    return pl.pallas_call(
        body, grid=grid, in_specs=[qcol(OD_Q), kcol(OD_K), kcol(OD_V), sink, qcol(0)], out_specs=[qcol(0), kvout, kvout, sink],
        out_shape=[jax.ShapeDtypeStruct((b, lp, SWA_HEADS * HEAD), F32), kvshape, kvshape, jax.ShapeDtypeStruct((SWA_HEADS, BLOCK), F32)],
        scratch_shapes=[pltpu.VMEM((lp, BLOCK), F32), pltpu.VMEM((lp, BLOCK), F32)],
        compiler_params=_params(("arbitrary", "arbitrary", "arbitrary")), name=name,
    )(proj3, proj3, proj3, sinks, do)


def _kernel_weights(ev_w_in, ev_w_uq, ev_w_ukv, od_w_in):
    zeros = lambda r, c: jnp.zeros((r, c), ev_w_in.dtype)
    q_sb, k_sb, v_sb, g_sb, c_q, c_kv, k_r, g_mla = jnp.split(ev_w_in, [512, 1024, 1536, 2048, 2304, 2432, 2464], axis=1)
    w0 = jnp.concatenate([g_sb, g_mla, q_sb, k_sb, v_sb, c_q, c_kv, zeros(D_MODEL, MLA_NOPE), k_r, zeros(D_MODEL, 32)], axis=1)
    uq = ev_w_uq.reshape(MLA_Q_LORA, MLA_HEADS, MLA_NOPE + MLA_ROPE)
    wq = jnp.pad(uq, ((0, 0), (0, 0), (0, BLOCK - MLA_NOPE - MLA_ROPE))).reshape(MLA_Q_LORA, MLA_HEADS * BLOCK)
    ukv = ev_w_ukv.reshape(MLA_KV_LORA, MLA_HEADS, BLOCK)
    wk = jnp.pad(ukv[:, :, :MLA_NOPE], ((0, 0), (0, 0), (0, BLOCK - MLA_NOPE))).reshape(MLA_KV_LORA, MLA_HEADS * BLOCK)
    wv = ukv[:, :, MLA_NOPE:].reshape(MLA_KV_LORA, MLA_HEADS * HEAD)
    q, k, v, g = jnp.split(od_w_in, [1024, 1152, 1280], axis=1)
    w1 = jnp.concatenate([g, q, k, v], axis=1)
    return w0, wq, wk, wv, w1


def _od_w_in_grad(dw1):
    sl = lambda a, first, n: a[:, first:first + n]
    return jnp.concatenate([sl(dw1, OD_Q, 1024), sl(dw1, OD_K, 128), sl(dw1, OD_V, 128), sl(dw1, OD_G, 1024)], axis=1)


def _original_grads(dw0, dwq, dwk, dwv):
    sl = lambda a, first, n: a[:, first:first + n]
    d_ev_w_in = jnp.concatenate([sl(dw0, EV_Q, 512), sl(dw0, EV_K, 512), sl(dw0, EV_V, 512), sl(dw0, EV_G, 512), sl(dw0, EV_CQ, 256),
                                 sl(dw0, EV_CKV, 128), sl(dw0, EV_KR + MLA_NOPE, MLA_ROPE), sl(dw0, EV_G + 512, 512)], axis=1)
    d_uq = dwq.reshape(MLA_Q_LORA, MLA_HEADS, BLOCK)[:, :, :MLA_NOPE + MLA_ROPE].reshape(MLA_Q_LORA, -1)
    d_ukv = jnp.concatenate([dwk.reshape(MLA_KV_LORA, MLA_HEADS, BLOCK)[:, :, :MLA_NOPE], dwv.reshape(MLA_KV_LORA, MLA_HEADS, HEAD)],
                            axis=2).reshape(MLA_KV_LORA, -1)
    return d_ev_w_in, d_uq, d_ukv


def _meta_rows_sum(dh0_3):
    b, _, d = dh0_3.shape

    def body(x_ref, o_ref):
        acc = x_ref[0, N_PAD:BLOCK, :]
        for i in range(1, b):
            acc = acc + x_ref[i, N_PAD:BLOCK, :]
        o_ref[...] = acc

    return pl.pallas_call(
        body, grid=(1,), in_specs=[pl.BlockSpec((b, BLOCK, d), lambda i: (0, 0, 0))], out_specs=pl.BlockSpec((N_META, d), lambda i: (0, 0)),
        out_shape=jax.ShapeDtypeStruct((N_META, d), F32), compiler_params=_params(("arbitrary",)), name="meta_rows_sum",
    )(dh0_3)


def _local_step(x, meta, norm_g, final_g, gq, gkv, sinks, target, ev_w_in, ev_w_uq, ev_w_ukv, wo0, od_w_in, wo1, reduce_early=None):
    b, seq, d = x.shape
    lp = seq + BLOCK
    t = b * lp
    w0, wq, wk, wv, w1 = _kernel_weights(ev_w_in, ev_w_uq, ev_w_ukv, od_w_in)
    h0 = jnp.concatenate([jnp.zeros((b, N_PAD, d), F32), jnp.broadcast_to(meta[None], (b, N_META, d)), x], axis=1).reshape(t, d)
    tabs = _rope_tables(lp)
    g0, g1 = norm_g[0:1], norm_g[1:2]

    hn0 = _rms_fwd(h0, g0, "norm0")
    proj0 = _mm(hn0, w0, "inproj0")
    p0 = proj0.reshape(b, lp, EV_N)
    o_sb, sb_tot = _sb_fwd(p0, "sb_fwd")
    qf, kf, v = _mla_prep_fwd(p0, gq, gkv, wq, wk, wv, tabs, "mla_prep_fwd")
    o_mla, lse = _mla_fwd(qf, kf, v, "mla_fwd")
    o0 = [o_sb.reshape(t, -1), o_mla.reshape(t, -1)]
    ao0 = _gate_fwd(o0, proj0, "gate0")
    h1 = _mm(ao0, wo0, "outproj0", res=h0)

    hn1 = _rms_fwd(h1, g1, "norm1")
    proj1 = _mm(hn1, w1, "inproj1")
    p1 = proj1.reshape(b, lp, OD_N)
    sinks_b = jnp.broadcast_to(sinks.reshape(SWA_HEADS, 1), (SWA_HEADS, BLOCK))
    o1 = _swa_fwd(p1, sinks_b, "swa_fwd").reshape(t, -1)
    ao1 = _gate_fwd([o1], proj1, "gate1")
    h2 = _mm(ao1, wo1, "outproj1", res=h1)

    dh2, d_final_g, loss = _loss_head(h2, final_g.reshape(1, d), target, b, lp)

    d_wo1 = _mm_tn(ao1, dh2, "d_wo1")
    dao1 = _mm_nt(dh2, wo1, "d_ao1")
    (do1,), dg1 = _gate_bwd(dao1, [o1], proj1, "gate1_bwd")
    dq1, dk4, dv4, d_sinks = _swa_bwd(p1, sinks_b, do1.reshape(b, lp, -1), "swa_bwd")
    unheads = lambda a: a.reshape(t, SWA_KV_HEADS * HEAD).astype(BF16)
    dproj1 = jnp.concatenate([dg1, dq1.reshape(t, -1).astype(BF16), unheads(dk4), unheads(dv4)], axis=1)
    d_w1 = _mm_tn(hn1, dproj1, "d_w1")
    dhn1 = _mm_nt(dproj1, w1, "d_hn1")
    dh1, d_g1 = _rms_bwd(h1, g1, dhn1, dh2, "norm1_bwd")

    d_wo0 = _mm_tn(ao0, dh1, "d_wo0")
    early = dict(od_w_in=_od_w_in_grad(d_w1), ev_w_out=d_wo0, od_w_out=d_wo1)
    dao0 = _mm_nt(dh1, wo0, "d_ao0")
    if reduce_early is not None:
        pending, finish = reduce_early(early)
        dao0, pending = lax.optimization_barrier((dao0, pending))
        early_done = finish(pending)
    (do_sb, do_mla), dg0 = _gate_bwd(dao0, o0, proj0, "gate0_bwd")
    dq_sb, dk_sb, dv_sb = _sb_bwd(p0, sb_tot, do_sb.reshape(b, lp, -1), "sb_bwd")
    dqf, dkf, dv = _mla_bwd(qf, kf, v, o_mla, lse, do_mla.reshape(b, lp, -1), "mla_bwd")
    dcq, dckv, dkr, d_wq, d_wk, d_wv, d_gq, d_gkv = _mla_prep_bwd(p0, gq, gkv, wq, wk, wv, tabs, dqf, dkf, dv, "mla_prep_bwd")
    flat = lambda a: a.reshape(t, -1).astype(BF16)
    dproj0 = jnp.concatenate([dg0, flat(dq_sb), flat(dk_sb), flat(dv_sb), flat(dcq), flat(dckv), flat(dkr)], axis=1)
    if reduce_early is not None:
        dproj0, early_done = lax.optimization_barrier((dproj0, early_done))
    d_w0 = _mm_tn(hn0, dproj0, "d_w0")
    d_ev_w_in, d_uq, d_ukv = _original_grads(d_w0, d_wq, d_wk, d_wv)
    dhn0 = _mm_nt(dproj0, w0, "d_hn0")
    dh0, d_g0 = _rms_bwd(h0, g0, dhn0, dh1, "norm0_bwd")
    dh0 = dh0.reshape(b, lp, d)

    grads = dict(meta=_meta_rows_sum(dh0), norm_g=jnp.concatenate([d_g0, d_g1], axis=0), final_g=d_final_g.reshape(d),
                 ev_w_in=d_ev_w_in, ev_q_norm_g=d_gq, ev_kv_norm_g=d_gkv, ev_w_uq=d_uq, ev_w_ukv=d_ukv,
                 od_sinks=d_sinks[:, 0].reshape(1, SWA_HEADS))
    if reduce_early is None:
        return loss, dh0[:, BLOCK:], {**grads, **early}
    return loss, dh0[:, BLOCK:], grads, early_done


MESH = pl.DeviceIdType.MESH
ANY = pl.BlockSpec(memory_space=pl.ANY)


def _place():
    return lax.axis_index("x"), lax.axis_index("y"), lax.axis_index("c")


def _other_chips(x, y):
    return [(1 - x, y), (x, 1 - y), (1 - x, 1 - y)]


def _with_own_slot(slots, own):
    me = 2 * lax.axis_index("x") + lax.axis_index("y")
    return lax.dynamic_update_slice(slots, own[None], (me,) + (0,) * own.ndim)


def _gather_weights(packs, meta, name):
    n = len(packs)

    def body(*refs):
        ins, m_ref, outs, mo_ref = refs[:n], refs[n], refs[n + 1:2 * n + 1], refs[2 * n + 1]
        send_sems, recv_sems = refs[2 * n + 2:]
        x, y, c = _place()
        me, sib = 2 * x + y, (x, y, 1 - c)
        chips = _other_chips(x, y)

        def copy(k, src, dst, to):
            return pltpu.make_async_remote_copy(src_ref=src, dst_ref=dst, send_sem=send_sems.at[k], recv_sem=recv_sems.at[k], device_id=to,
                                                device_id_type=MESH)

        def half(i, chip, h):
            rows = packs[i].shape[0] // 2
            return outs[i].at[chip, pl.ds(h * rows, rows), :]

        def mine(i):
            rows = packs[i].shape[0] // 2
            return ins[i].at[pl.ds(c * rows, rows), :]

        sent = [copy(6 * i + k, mine(i), half(i, me, c), (px, py, c)) for i in range(n) for k, (px, py) in enumerate(chips)]
        sent += [copy(6 * n + k, m_ref, mo_ref.at[me], (px, py, c)) for k, (px, py) in enumerate(chips)]
        for cp in sent:
            cp.start()
        for i in range(n):
            for k, (px, py) in enumerate(chips):
                landed = half(i, 2 * px + py, c)
                copy(6 * i + k, mine(i), landed, (px, py, c)).wait_recv()
                fwd = copy(6 * i + 3 + k, landed, landed, sib)
                fwd.start()
                sent.append(fwd)
        for k, (px, py) in enumerate(chips):
            for i in range(n):
                other = half(i, 2 * px + py, 1 - c)
                copy(6 * i + 3 + k, other, other, sib).wait_recv()
            copy(6 * n + k, m_ref, mo_ref.at[2 * px + py], (px, py, c)).wait_recv()
        for cp in sent:
            cp.wait_send()

    nsem = 6 * n + 3
    res = pl.pallas_call(
        body, in_specs=[ANY] * (n + 1), out_specs=[ANY] * (n + 1),
        out_shape=[jax.ShapeDtypeStruct((N_CHIPS,) + a.shape, a.dtype) for a in list(packs) + [meta]],
        scratch_shapes=[pltpu.SemaphoreType.DMA((nsem,)), pltpu.SemaphoreType.DMA((nsem,))],
        name=name,
    )(*packs, meta)
    return [_with_own_slot(r, a) for r, a in zip(res[:n], packs)], _with_own_slot(res[n], meta)


def _grads_to_sibling(gs, name):
    n = len(gs)

    def body(*refs):
        ins, outs, send_sems, recv_sems = refs[:n], refs[n:2 * n], refs[2 * n], refs[2 * n + 1]
        x, y, c = _place()
        cps = []
        for i in range(n):
            rows = gs[i].shape[1] // 2
            cps.append(pltpu.make_async_remote_copy(src_ref=ins[i].at[:, pl.ds((1 - c) * rows, rows), :], dst_ref=outs[i],
                                                    send_sem=send_sems.at[i], recv_sem=recv_sems.at[i], device_id=(x, y, 1 - c),
                                                    device_id_type=MESH))
        for cp in cps:
            cp.start()
        for cp in cps:
            cp.wait()

    return pl.pallas_call(
        body, in_specs=[ANY] * n, out_specs=[ANY] * n,
        out_shape=[jax.ShapeDtypeStruct((g.shape[0], g.shape[1] // 2, g.shape[2]), g.dtype) for g in gs],
        scratch_shapes=[pltpu.SemaphoreType.DMA((n,)), pltpu.SemaphoreType.DMA((n,))],
        name=name,
    )(*gs)


def _share_halves(rs, name):
    n = len(rs)

    def body(*refs):
        ins, outs, send_sems, recv_sems = refs[:n], refs[n:2 * n], refs[2 * n], refs[2 * n + 1]
        x, y, c = _place()
        cps = [pltpu.make_async_remote_copy(src_ref=ins[i], dst_ref=outs[i], send_sem=send_sems.at[i], recv_sem=recv_sems.at[i],
                                            device_id=(x, y, 1 - c), device_id_type=MESH) for i in range(n)]
        for cp in cps:
            cp.start()
        for cp in cps:
            cp.wait()

    theirs = pl.pallas_call(
        body, in_specs=[ANY] * n, out_specs=[ANY] * n, out_shape=[jax.ShapeDtypeStruct(r.shape, r.dtype) for r in rs],
        scratch_shapes=[pltpu.SemaphoreType.DMA((n,)), pltpu.SemaphoreType.DMA((n,))],
        name=name,
    )(*rs)
    first = lax.axis_index("c") == 0
    return [jnp.where(first, jnp.concatenate([r, t], axis=0), jnp.concatenate([t, r], axis=0)) for r, t in zip(rs, theirs)]


def _chip_scatter(ss, name):
    n = len(ss)

    def body(*refs):
        ins, outs, send_sems, recv_sems = refs[:n], refs[n:2 * n], refs[2 * n], refs[2 * n + 1]
        x, y, c = _place()
        me = 2 * x + y
        chips = _other_chips(x, y)
        for i in range(n):
            for k, (px, py) in enumerate(chips):
                pltpu.make_async_remote_copy(src_ref=ins[i].at[2 * px + py], dst_ref=outs[i].at[me], send_sem=send_sems.at[3 * i + k],
                                             recv_sem=recv_sems.at[3 * i + k], device_id=(px, py, c), device_id_type=MESH).start()
        for i in range(n):
            for k, (px, py) in enumerate(chips):
                cp = pltpu.make_async_remote_copy(src_ref=ins[i].at[2 * px + py], dst_ref=outs[i].at[2 * px + py],
                                                  send_sem=send_sems.at[3 * i + k], recv_sem=recv_sems.at[3 * i + k],
                                                  device_id=(px, py, c), device_id_type=MESH)
                cp.wait_recv()
                cp.wait_send()

    parts = pl.pallas_call(
        body, in_specs=[ANY] * n, out_specs=[ANY] * n, out_shape=[jax.ShapeDtypeStruct(s.shape, s.dtype) for s in ss],
        scratch_shapes=[pltpu.SemaphoreType.DMA((3 * n,)), pltpu.SemaphoreType.DMA((3 * n,))],
        name=name,
    )(*ss)
    me = 2 * lax.axis_index("x") + lax.axis_index("y")
    return [_with_own_slot(p, lax.dynamic_index_in_dim(s, me, axis=0, keepdims=False)) for p, s in zip(parts, ss)]


HBM_SPACE = pltpu.MemorySpace.HBM


def _on_sequencer(name, collective_id, n_sems, body):
    @pl.kernel(mesh=plsc.ScalarSubcoreMesh(axis_name="sequencer", num_cores=1), name=name,
               scratch_types=(pltpu.SemaphoreType.DMA((n_sems,)), pltpu.SemaphoreType.DMA((n_sems,))),
               compiler_params=pltpu.CompilerParams(collective_id=collective_id))
    def launch(send_sems, recv_sems):
        body(send_sems, recv_sems)

    launch()


def _handshake(peers):
    barrier = pltpu.get_barrier_semaphore()
    for peer in peers:
        pl.semaphore_signal(barrier, inc=1, device_id=peer, device_id_type=MESH)
    pl.semaphore_wait(barrier, len(peers))


def _gather_on_sequencer(packs, name):
    n = len(packs)
    ins = [jax.new_ref(p, memory_space=HBM_SPACE) for p in packs]
    outs = [jax.empty_ref(jax.ShapeDtypeStruct((N_CHIPS,) + p.shape, p.dtype), memory_space=HBM_SPACE) for p in packs]

    def body(send_sems, recv_sems):
        x, y, c = _place()
        me, sib = 2 * x + y, (x, y, 1 - c)
        chips = _other_chips(x, y)
        _handshake([(px, py, c) for px, py in chips] + [sib])

        def copy(k, src, dst, to):
            return pltpu.make_async_remote_copy(src_ref=src, dst_ref=dst, send_sem=send_sems.at[k], recv_sem=recv_sems.at[k], device_id=to,
                                                device_id_type=MESH)

        def half(i, chip, h):
            rows = packs[i].shape[0] // 2
            return outs[i].at[chip, pl.ds(h * rows, rows), :]

        def mine(i):
            rows = packs[i].shape[0] // 2
            return ins[i].at[pl.ds(c * rows, rows), :]

        sent = [copy(6 * i + k, mine(i), half(i, me, c), (px, py, c)) for i in range(n) for k, (px, py) in enumerate(chips)]
        for cp in sent:
            cp.start()
        for i in range(n):
            for k, (px, py) in enumerate(chips):
                landed = half(i, 2 * px + py, c)
                copy(6 * i + k, mine(i), landed, (px, py, c)).wait_recv()
                fwd = copy(6 * i + 3 + k, landed, landed, sib)
                fwd.start()
                sent.append(fwd)
        for i in range(n):
            for k, (px, py) in enumerate(chips):
                other = half(i, 2 * px + py, 1 - c)
                copy(6 * i + 3 + k, other, other, sib).wait_recv()
        for cp in sent:
            cp.wait_send()

    _on_sequencer(name, 1, 6 * n, body)
    return [_with_own_slot(o[...], p) for o, p in zip(outs, packs)]


def _grads_to_sibling_on_sequencer(gs, name, collective_id):
    n = len(gs)
    ins = [jax.new_ref(g, memory_space=HBM_SPACE) for g in gs]
    outs = [jax.empty_ref(jax.ShapeDtypeStruct((g.shape[0], g.shape[1] // 2, g.shape[2]), g.dtype), memory_space=HBM_SPACE) for g in gs]

    def body(send_sems, recv_sems):
        x, y, c = _place()
        _handshake([(x, y, 1 - c)])
        cps = []
        for i in range(n):
            rows = gs[i].shape[1] // 2
            cps.append(pltpu.make_async_remote_copy(src_ref=ins[i].at[:, pl.ds((1 - c) * rows, rows), :], dst_ref=outs[i],
                                                    send_sem=send_sems.at[i], recv_sem=recv_sems.at[i], device_id=(x, y, 1 - c),
                                                    device_id_type=MESH))
        for cp in cps:
            cp.start()
        for cp in cps:
            cp.wait()

    _on_sequencer(name, collective_id, n, body)
    return [o[...] for o in outs]


def _chip_scatter_on_sequencer(ss, name, collective_id):
    n = len(ss)
    ins = [jax.new_ref(s, memory_space=HBM_SPACE) for s in ss]
    outs = [jax.empty_ref(jax.ShapeDtypeStruct(s.shape, s.dtype), memory_space=HBM_SPACE) for s in ss]

    def body(send_sems, recv_sems):
        x, y, c = _place()
        me = 2 * x + y
        chips = _other_chips(x, y)
        _handshake([(px, py, c) for px, py in chips])
        for i in range(n):
            for k, (px, py) in enumerate(chips):
                pltpu.make_async_remote_copy(src_ref=ins[i].at[2 * px + py], dst_ref=outs[i].at[me], send_sem=send_sems.at[3 * i + k],
                                             recv_sem=recv_sems.at[3 * i + k], device_id=(px, py, c), device_id_type=MESH).start()
        for i in range(n):
            for k, (px, py) in enumerate(chips):
                cp = pltpu.make_async_remote_copy(src_ref=ins[i].at[2 * px + py], dst_ref=outs[i].at[2 * px + py],
                                                  send_sem=send_sems.at[3 * i + k], recv_sem=recv_sems.at[3 * i + k],
                                                  device_id=(px, py, c), device_id_type=MESH)
                cp.wait_recv()
                cp.wait_send()

    _on_sequencer(name, collective_id, 3 * n, body)
    me = 2 * lax.axis_index("x") + lax.axis_index("y")
    return [_with_own_slot(o[...], lax.dynamic_index_in_dim(s, me, axis=0, keepdims=False)) for o, s in zip(outs, ss)]


def _all_reduce_small(v, name):
    shape = v.shape

    def body(v_ref, o_ref, slots, send_sems, recv_sems):
        x, y, c = _place()
        me = 4 * x + 2 * y + c
        slots[me] = v_ref[...]
        for r in range(1, N_DEV):
            peer = (x ^ (r >> 2), y ^ ((r >> 1) & 1), c ^ (r & 1))
            pltpu.make_async_remote_copy(src_ref=v_ref, dst_ref=slots.at[me], send_sem=send_sems.at[r - 1], recv_sem=recv_sems.at[r - 1],
                                         device_id=peer, device_id_type=MESH).start()
        for r in range(1, N_DEV):
            peer = (x ^ (r >> 2), y ^ ((r >> 1) & 1), c ^ (r & 1))
            cp = pltpu.make_async_remote_copy(src_ref=v_ref, dst_ref=slots.at[4 * peer[0] + 2 * peer[1] + peer[2]], send_sem=send_sems.at[r - 1],
                                              recv_sem=recv_sems.at[r - 1], device_id=peer, device_id_type=MESH)
            cp.wait_recv()
            cp.wait_send()
        acc = slots[0]
        for d in range(1, N_DEV):
            acc = acc + slots[d]
        o_ref[...] = acc

    vm = pl.BlockSpec(memory_space=pltpu.VMEM)
    return pl.pallas_call(
        body, in_specs=[vm], out_specs=vm, out_shape=jax.ShapeDtypeStruct(shape, F32),
        scratch_shapes=[pltpu.VMEM((N_DEV,) + shape, F32), pltpu.SemaphoreType.DMA((N_DEV - 1,)), pltpu.SemaphoreType.DMA((N_DEV - 1,))],
        name=name,
    )(v)


def _add_sibling(g, gsib, core, name):
    n, _, cdim = g.shape
    half = gsib.shape[1]
    tr = half // 2

    def body(core_ref, a_ref, b_ref, o_ref):
        o_ref[...] = (a_ref[...] + b_ref[...]).astype(o_ref.dtype)

    blk = pl.BlockSpec((1, tr, cdim), lambda j, i, core_ref: (j, i, 0))
    return pl.pallas_call(
        body,
        grid_spec=pltpu.PrefetchScalarGridSpec(
            num_scalar_prefetch=1, grid=(n, half // tr),
            in_specs=[pl.BlockSpec((1, tr, cdim), lambda j, i, core_ref: (j, core_ref[0] * (half // tr) + i, 0)), blk], out_specs=blk),
        out_shape=jax.ShapeDtypeStruct(gsib.shape, BF16), compiler_params=_params(("parallel", "parallel")), name=name,
    )(core, g, gsib)


def _sum_parts(parts, name):
    n, r, cdim = parts.shape
    tr = r // 2

    def body(p_ref, o_ref):
        acc = p_ref[0].astype(F32)
        for j in range(1, n):
            acc = acc + p_ref[j].astype(F32)
        o_ref[...] = acc

    return pl.pallas_call(
        body, grid=(r // tr,), in_specs=[pl.BlockSpec((n, tr, cdim), lambda i: (0, i, 0))],
        out_specs=pl.BlockSpec((tr, cdim), lambda i: (i, 0)), out_shape=jax.ShapeDtypeStruct((r, cdim), F32),
        compiler_params=_params(("parallel",)), name=name,
    )(parts)


def _adamw(parts, w, m, v, name):
    npart, r, cdim = parts.shape
    tr = r // 4 if r % 32 == 0 else r

    def body(p_ref, w_ref, m_ref, v_ref, g_ref, d_ref, nm_ref, nv_ref):
        g = p_ref[0]
        for j in range(1, npart):
            g = g + p_ref[j]
        m_new = ADAM_B1 * m_ref[...] + (1.0 - ADAM_B1) * g
        v_new = ADAM_B2 * v_ref[...] + (1.0 - ADAM_B2) * (g * g)
        m_hat = m_new / (1.0 - ADAM_B1 ** ADAM_STEP)
        v_hat = v_new / (1.0 - ADAM_B2 ** ADAM_STEP)
        g_ref[...] = g
        d_ref[...] = -ADAM_LR * (m_hat / (jnp.sqrt(v_hat) + ADAM_EPS) + ADAM_WD * w_ref[...])
        nm_ref[...] = m_new
        nv_ref[...] = v_new

    blk = pl.BlockSpec((tr, cdim), lambda i: (i, 0))
    shp = jax.ShapeDtypeStruct((r, cdim), F32)
    return pl.pallas_call(
        body, grid=(r // tr,), in_specs=[pl.BlockSpec((npart, tr, cdim), lambda i: (0, i, 0)), blk, blk, blk], out_specs=[blk] * 4,
        out_shape=[shp] * 4, compiler_params=_params(("parallel",)), name=name,
    )(parts, w, m, v)


BIG = ("ev_w_in", "ev_w_uq", "ev_w_ukv", "ev_w_out", "od_w_in", "od_w_out", "meta")
SMALL = ("norm_g", "final_g", "ev_q_norm_g", "ev_kv_norm_g", "od_sinks")
SMALL_SHAPE = (8, 512)
BY_ROWS = ("ev_w_out", "od_w_out")

EV_IN_SHARD, OD_IN_SHARD, UQ_SHARD = 2976 // N_CHIPS, 2304 // N_CHIPS, 768 // N_CHIPS


def _pack_big(a, lead=()):
    return _pack_first(a, lead) + _pack_later(a, lead)


def _pad_lanes(x, width):
    return jnp.concatenate([x, jnp.zeros(x.shape[:-1] + (width - x.shape[-1],), x.dtype)], axis=-1)


def _pack_latent(a, lead=()):
    ax = len(lead)
    corner = jnp.concatenate([a["ev_w_ukv"], a["meta"], jnp.zeros(lead + (256 - MLA_KV_LORA - N_META, 256), a["meta"].dtype)], axis=ax)
    return jnp.concatenate([_pad_lanes(a["ev_w_uq"], 256), corner], axis=ax + 1)


def _pack_first(a, lead=()):
    return _pad_lanes(a["ev_w_in"], 768), _pack_latent(a, lead)


def _pack_later(a, lead=()):
    return _pad_lanes(a["od_w_in"], 640), jnp.concatenate([a["ev_w_out"], a["od_w_out"]], axis=len(lead) + 1)


N_FIRST = 2
LATER = ("od_w_in", "ev_w_out", "od_w_out")


def _unpack_big(p_in0, p_lat, p_in1, p_out):
    return dict(ev_w_in=p_in0[..., :EV_IN_SHARD], od_w_in=p_in1[..., :OD_IN_SHARD], ev_w_out=p_out[..., :D_MODEL],
                od_w_out=p_out[..., D_MODEL:], ev_w_uq=p_lat[..., :UQ_SHARD], ev_w_ukv=p_lat[..., :MLA_KV_LORA, 256:],
                meta=p_lat[..., MLA_KV_LORA:MLA_KV_LORA + N_META, 256:])


def _chip_shards(full, by_rows):
    if by_rows:
        return full.reshape(N_CHIPS, full.shape[0] // N_CHIPS, full.shape[1])
    cols = full.shape[1] // N_CHIPS
    return jnp.stack([full[:, j * cols:(j + 1) * cols] for j in range(N_CHIPS)])


def _from_chip_shards(slots, by_rows):
    if by_rows:
        return slots.reshape(-1, slots.shape[2])
    return jnp.concatenate([slots[j] for j in range(N_CHIPS)], axis=1)


def _pack_small(arrs, extra=None):
    flat = [a.reshape(-1) for a in arrs] + ([] if extra is None else [extra.reshape(-1)])
    used = sum(f.shape[0] for f in flat)
    return jnp.pad(jnp.concatenate(flat), (0, SMALL_SHAPE[0] * SMALL_SHAPE[1] - used)).reshape(SMALL_SHAPE)


def _unpack_small(p, shapes):
    flat, out, at = p.reshape(-1), [], 0
    for s in shapes:
        n = int(np.prod(s))
        out.append(flat[at:at + n].reshape(s))
        at += n
    return out, flat[at]


def kernel(x, meta, norm_g, final_g, ev_w_in, ev_q_norm_g, ev_kv_norm_g, ev_w_uq, ev_w_ukv, ev_w_out, od_w_in, od_sinks, od_w_out, loss_target, m_meta, m_norm_g, m_final_g, m_ev_w_in, m_ev_q_norm_g, m_ev_kv_norm_g, m_ev_w_uq, m_ev_w_ukv, m_ev_w_out, m_od_w_in, m_od_sinks, m_od_w_out, v_meta, v_norm_g, v_final_g, v_ev_w_in, v_ev_q_norm_g, v_ev_kv_norm_g, v_ev_w_uq, v_ev_w_ukv, v_ev_w_out, v_od_w_in, v_od_sinks, v_od_w_out):
    given = dict(locals())
    two_d = lambda a: a[0] if a.ndim == 3 else a
    packs = {k: _pack_big({n: two_d(given[k + n]) for n in BIG}) for k in ("", "m_", "v_")}

    wbf = [p.astype(BF16) for p in packs[""]]
    later = _gather_on_sequencer(wbf[N_FIRST:], "gather_later_weights")
    first, meta_all = _gather_weights(wbf[:N_FIRST], meta, "gather_weights")
    full = {n: _from_chip_shards(a, n in BY_ROWS) for n, a in _unpack_big(*first, *later).items()}
    meta_full = _from_chip_shards(meta_all, False)

    core = lax.axis_index("c").astype(jnp.int32).reshape(1)
    shards = lambda g: {n: _chip_shards(a, n in BY_ROWS) for n, a in g.items()}

    def sums_of(group, tag, to_sibling):
        return [_add_sibling(g, s, core, f"add_sibling_{tag}{i}") for i, (g, s) in enumerate(zip(group, to_sibling(group)))]

    def update(sums, tag, first_pack, scatter):
        parts = scatter(sums, "grads_to_chips_" + tag)
        reduced = _share_halves([_sum_parts(p, f"add_chips_{tag}{i}") for i, p in enumerate(parts)], "reduced_to_sibling_" + tag)
        return [_adamw(r[None], packs[""][first_pack + i], packs["m_"][first_pack + i], packs["v_"][first_pack + i],
                       f"adamw_matrices_{first_pack + i}") for i, r in enumerate(reduced)]

    def behind(tag, first_pack, ids):
        def start(packed):
            sums = sums_of(packed, tag, lambda p: _grads_to_sibling_on_sequencer(p, "grads_to_sibling_" + tag, ids[0]))
            return sums, lambda s: update(s, tag, first_pack, lambda t, name: _chip_scatter_on_sequencer(t, name, ids[1]))
        return start

    loss, grad_x, grads, updated_later = _local_step(
        x, meta_full, norm_g, final_g, ev_q_norm_g, ev_kv_norm_g, od_sinks, loss_target, full["ev_w_in"], full["ev_w_uq"],
        full["ev_w_ukv"], full["ev_w_out"], full["od_w_in"], full["od_w_out"],
        reduce_early=lambda g: behind("later", N_FIRST, (3, 2))(_pack_later(shards(g), (N_CHIPS,))))

    first = _pack_first(shards({n: grads[n] for n in BIG if n not in LATER}), (N_CHIPS,))
    updated_first = update(sums_of(first, "first", lambda p: _grads_to_sibling(p, "grads_to_sibling_first")), "first", 0,
                           lambda t, name: _chip_scatter_on_sequencer(t, name, 4))
    updated = updated_first + updated_later
    big_out = [{n: a.reshape(given[n].shape) for n, a in _unpack_big(*outs).items()} for outs in zip(*updated)]

    small_shapes = [given[n].shape for n in SMALL]
    ssum = _all_reduce_small(_pack_small([grads[n] for n in SMALL], loss[0, 0]), "reduce_vectors")
    small_out = _adamw(ssum[None], _pack_small([given[n] for n in SMALL]), _pack_small([given["m_" + n] for n in SMALL]),
                       _pack_small([given["v_" + n] for n in SMALL]), "adamw_vectors")
    total_loss = ssum.reshape(-1)[sum(int(np.prod(s)) for s in small_shapes)]
    small_out = [_unpack_small(o, small_shapes)[0] for o in small_out]

    names = ("meta", "norm_g", "final_g", "ev_w_in", "ev_q_norm_g", "ev_kv_norm_g", "ev_w_uq", "ev_w_ukv", "ev_w_out", "od_w_in", "od_sinks",
             "od_w_out")
    outs = [total_loss, grad_x]
    for kind in range(4):
        for n in names:
            outs.append(big_out[kind][n] if n in BIG else small_out[kind][SMALL.index(n)])
    return tuple(outs)
```

```python
import functools
import math

import numpy as np
import jax
import jax.numpy as jnp
from jax import lax
from jax.experimental import pallas as pl
from jax.experimental.pallas import tpu as pltpu
from jax.experimental.pallas import tpu_sc as plsc

F32 = jnp.float32
BF16 = jnp.bfloat16

D_MODEL = 1024
BLOCK = 128
N_META = 16
N_PAD = BLOCK - N_META
NORM_EPS = 1e-6
NEG = -1e30
HEAD = 64
SB_HEADS = 8
MLA_HEADS = 8
MLA_Q_LORA = 256
MLA_KV_LORA = 128
MLA_NOPE = 64
MLA_ROPE = 32
ROPE_BASE = 10000.0
SWA_HEADS = 16
SWA_KV_HEADS = 2
SWA_WINDOW = 128
N_CHIPS = 4
N_DEV = 8

ADAM_LR = 0.001
ADAM_B1 = 0.9
ADAM_B2 = 0.999
ADAM_EPS = 1e-08
ADAM_WD = 0.01
ADAM_STEP = 10

VMEM_LIMIT = 48 * 1024 * 1024

EV_G, EV_Q, EV_K, EV_V, EV_CQ, EV_CKV, EV_KR, EV_N = 0, 1024, 1536, 2048, 2560, 2816, 2944, 3072
OD_G, OD_Q, OD_K, OD_V, OD_N = 0, 1024, 2048, 2176, 2304


def _params(sem=None):
    return pltpu.CompilerParams(dimension_semantics=sem, vmem_limit_bytes=VMEM_LIMIT)


def _row_tile(m):
    return 256 if m % 256 == 0 else 128


def _matmul_rows(m):
    for c in (1088, 1024, 768, 640, 512, 384, 256):
        if m % c == 0:
            return c
    return 128


def _dot(a, b):
    return jnp.dot(a.astype(BF16), b.astype(BF16), preferred_element_type=F32)


def _dot_nt(a, b):
    return lax.dot_general(a.astype(BF16), b.astype(BF16), (((1,), (1,)), ((), ())), preferred_element_type=F32)


def _dot_tn(a, b):
    return lax.dot_general(a.astype(BF16), b.astype(BF16), (((0,), (0,)), ((), ())), preferred_element_type=F32)


def _rms_fwd(h, g, name):
    t, d = h.shape
    tm = _row_tile(t)

    def body(h_ref, g_ref, o_ref):
        x = h_ref[...]
        r = lax.rsqrt(jnp.mean(x * x, axis=-1, keepdims=True) + NORM_EPS)
        o_ref[...] = ((x * r) * g_ref[...]).astype(o_ref.dtype)

    return pl.pallas_call(
        body, grid=(t // tm,),
        in_specs=[pl.BlockSpec((tm, d), lambda i: (i, 0)), pl.BlockSpec((1, d), lambda i: (0, 0))],
        out_specs=pl.BlockSpec((tm, d), lambda i: (i, 0)),
        out_shape=jax.ShapeDtypeStruct((t, d), BF16), compiler_params=_params(("parallel",)), name=name,
    )(h, g)


def _rms_bwd(h, g, dy, dres, name):
    t, d = h.shape
    tm = _row_tile(t)

    def body(h_ref, g_ref, dy_ref, dres_ref, dh_ref, dg_ref):
        @pl.when(pl.program_id(0) == 0)
        def _():
            dg_ref[...] = jnp.zeros_like(dg_ref)

        x = h_ref[...]
        r = lax.rsqrt(jnp.mean(x * x, axis=-1, keepdims=True) + NORM_EPS)
        xr = x * r
        dy_ = dy_ref[...]
        u = dy_ * g_ref[...]
        dh_ref[...] = dres_ref[...] + r * (u - xr * jnp.mean(u * xr, axis=-1, keepdims=True))
        dg_ref[...] += jnp.sum(dy_ * xr, axis=0, keepdims=True)

    row = pl.BlockSpec((tm, d), lambda i: (i, 0))
    vec = pl.BlockSpec((1, d), lambda i: (0, 0))
    return pl.pallas_call(
        body, grid=(t // tm,), in_specs=[row, vec, row, row], out_specs=[row, vec],
        out_shape=[jax.ShapeDtypeStruct((t, d), F32), jax.ShapeDtypeStruct((1, d), F32)],
        compiler_params=_params(("arbitrary",)), name=name,
    )(h, g, dy, dres)


def _col_tile(n):
    for c in (1024, 768, 640, 512, 384, 256, 128):
        if n % c == 0:
            return c
    return n


def _mm(a, w, name, res=None, out_dtype=F32, a_cols=None):
    m = a.shape[0]
    k, n = w.shape
    a_blk = 0 if a_cols is None else a_cols[0] // k
    assert a_cols is None or (a_cols[1] == k and a_cols[0] % k == 0)
    tm, tn = _matmul_rows(m), _col_tile(n)

    def body(*refs):
        if res is None:
            a_ref, w_ref, o_ref = refs
            acc = _dot(a_ref[...], w_ref[...])
        else:
            a_ref, w_ref, r_ref, o_ref = refs
            acc = r_ref[...] + _dot(a_ref[...], w_ref[...])
        o_ref[...] = acc.astype(o_ref.dtype)

    in_specs = [pl.BlockSpec((tm, k), lambda j, i: (i, a_blk)), pl.BlockSpec((k, tn), lambda j, i: (0, j))]
    args = [a, w]
    if res is not None:
        in_specs.append(pl.BlockSpec((tm, tn), lambda j, i: (i, j)))
        args.append(res)
    return pl.pallas_call(
        body, grid=(n // tn, m // tm), in_specs=in_specs, out_specs=pl.BlockSpec((tm, tn), lambda j, i: (i, j)),
        out_shape=jax.ShapeDtypeStruct((m, n), out_dtype), compiler_params=_params(("parallel", "parallel")), name=name,
    )(*args)


def _mm_nt(a, w, name):
    m, n = a.shape
    k = w.shape[0]
    tm, tk = _matmul_rows(m), _col_tile(k)

    def body(a_ref, w_ref, o_ref):
        o_ref[...] = _dot_nt(a_ref[...], w_ref[...])

    return pl.pallas_call(
        body, grid=(k // tk, m // tm),
        in_specs=[pl.BlockSpec((tm, n), lambda j, i: (i, 0)), pl.BlockSpec((tk, n), lambda j, i: (j, 0))],
        out_specs=pl.BlockSpec((tm, tk), lambda j, i: (i, j)),
        out_shape=jax.ShapeDtypeStruct((m, k), F32), compiler_params=_params(("parallel", "parallel")), name=name,
    )(a, w)


def _mm_tn(x, dy, name):
    m, k = x.shape
    n = dy.shape[1]
    tm, tn = _matmul_rows(m), _col_tile(n)

    def body(x_ref, dy_ref, o_ref):
        @pl.when(pl.program_id(1) == 0)
        def _():
            o_ref[...] = jnp.zeros_like(o_ref)

        o_ref[...] += _dot_tn(x_ref[...], dy_ref[...])

    return pl.pallas_call(
        body, grid=(n // tn, m // tm),
        in_specs=[pl.BlockSpec((tm, k), lambda j, i: (i, 0)), pl.BlockSpec((tm, tn), lambda j, i: (i, j))],
        out_specs=pl.BlockSpec((k, tn), lambda j, i: (0, j)),
        out_shape=jax.ShapeDtypeStruct((k, n), F32), compiler_params=_params(("parallel", "arbitrary")), name=name,
    )(x, dy)


def _silu_parts(g):
    s = 1.0 / (1.0 + jnp.exp(-g))
    return g * s, s * (1.0 + g * (1.0 - s))


def _gate_fwd(o_parts, proj, name):
    t = proj.shape[0]
    tm = _row_tile(t)
    w = D_MODEL // len(o_parts)

    def body(*refs):
        g_ref, o_ref = refs[-2], refs[-1]
        for p, r in enumerate(refs[:-2]):
            sil, _ = _silu_parts(g_ref[:, p * w:(p + 1) * w])
            o_ref[:, p * w:(p + 1) * w] = (r[...].astype(F32) * sil).astype(o_ref.dtype)

    return pl.pallas_call(
        body, grid=(t // tm,),
        in_specs=[pl.BlockSpec((tm, w), lambda i: (i, 0)) for _ in o_parts] + [pl.BlockSpec((tm, D_MODEL), lambda i: (i, 0))],
        out_specs=pl.BlockSpec((tm, D_MODEL), lambda i: (i, 0)),
        out_shape=jax.ShapeDtypeStruct((t, D_MODEL), BF16), compiler_params=_params(("parallel",)), name=name,
    )(*o_parts, proj)


def _gate_bwd(dao, o_parts, proj, name):
    t = proj.shape[0]
    tm = _row_tile(t)
    np_ = len(o_parts)
    w = D_MODEL // np_

    def body(*refs):
        dao_ref, g_ref = refs[0], refs[1 + np_]
        do_refs, dg_ref = refs[2 + np_:2 + 2 * np_], refs[-1]
        for p in range(np_):
            sl = slice(p * w, (p + 1) * w)
            sil, dsil = _silu_parts(g_ref[:, sl])
            da = dao_ref[:, sl]
            do_refs[p][...] = da * sil
            dg_ref[:, sl] = (da * refs[1 + p][...].astype(F32) * dsil).astype(dg_ref.dtype)

    full = pl.BlockSpec((tm, D_MODEL), lambda i: (i, 0))
    part = pl.BlockSpec((tm, w), lambda i: (i, 0))
    outs = pl.pallas_call(
        body, grid=(t // tm,), in_specs=[full] + [part] * np_ + [full], out_specs=[part] * np_ + [full],
        out_shape=[jax.ShapeDtypeStruct((t, w), F32)] * np_ + [jax.ShapeDtypeStruct((t, D_MODEL), BF16)],
        compiler_params=_params(("parallel",)), name=name,
    )(dao, *o_parts, proj)
    return outs[:np_], outs[np_]


def _loss_head(h2, gf, target, b, lp):
    d = h2.shape[1]
    nb = lp // BLOCK
    h3 = h2.reshape(b, lp, d)

    def body(h_ref, g_ref, t_ref, dh_ref, dg_ref, loss_ref):
        first = (pl.program_id(0) == 0) & (pl.program_id(1) == 0)

        @pl.when(first)
        def _():
            dg_ref[...] = jnp.zeros_like(dg_ref)
            loss_ref[...] = jnp.zeros_like(loss_ref)

        @pl.when(pl.program_id(1) == 0)
        def _():
            dh_ref[...] = jnp.zeros_like(dh_ref)

        @pl.when(pl.program_id(1) > 0)
        def _():
            x = h_ref[0]
            r = lax.rsqrt(jnp.mean(x * x, axis=-1, keepdims=True) + NORM_EPS)
            xr = x * r
            g = g_ref[...]
            diff = xr * g - t_ref[0]
            loss_ref[...] += 0.5 * jnp.sum(jnp.mean(diff * diff, axis=-1, keepdims=True))
            dy = diff * (1.0 / d)
            u = dy * g
            dh_ref[0] = r * (u - xr * jnp.mean(u * xr, axis=-1, keepdims=True))
            dg_ref[...] += jnp.sum(dy * xr, axis=0, keepdims=True)

    blk = pl.BlockSpec((1, BLOCK, d), lambda bi, n: (bi, n, 0))
    dh, dg, loss = pl.pallas_call(
        body, grid=(b, nb),
        in_specs=[blk, pl.BlockSpec((1, d), lambda bi, n: (0, 0)),
                  pl.BlockSpec((1, BLOCK, d), lambda bi, n: (bi, jnp.maximum(n - 1, 0), 0))],
        out_specs=[blk, pl.BlockSpec((1, d), lambda bi, n: (0, 0)), pl.BlockSpec((8, 128), lambda bi, n: (0, 0))],
        out_shape=[jax.ShapeDtypeStruct((b, lp, d), F32), jax.ShapeDtypeStruct((1, d), F32), jax.ShapeDtypeStruct((8, 128), F32)],
        compiler_params=_params(("arbitrary", "arbitrary")), name="loss_head",
    )(h3, gf, target)
    return dh.reshape(b * lp, d), dg, loss


def _iota2(shape, dim):
    return lax.broadcasted_iota(jnp.int32, shape, dim)


KEYS = 512
SB_FWD_KEYS = 512


def _lo_lanes():
    return _iota2((1, BLOCK), 1) < HEAD


def _halves(x, lo):
    zero = jnp.zeros_like(x)
    return jnp.where(lo, x, zero), jnp.where(lo, zero, x)


def _rows_of_pair(a, b):
    return jnp.where(_iota2((BLOCK, 1), 0) < HEAD, a, b)


def _split_rows_t(x):
    xt = x.T
    first = _iota2(xt.shape, 0) < HEAD
    zero = jnp.zeros_like(xt)
    return jnp.concatenate([jnp.where(first, xt, zero), jnp.where(first, zero, xt)], axis=1).astype(BF16)


def _key_chunk(c, lp, t_idx, strict, key_axis):
    keys = t_idx.shape[key_axis]
    first = c * keys
    s0 = pl.multiple_of(jnp.minimum(first, lp - keys), BLOCK)
    s_idx = s0 + _iota2(t_idx.shape, key_axis)
    seen = (s_idx < t_idx) if strict else (s_idx <= t_idx)
    return s0, seen & (s_idx >= jnp.maximum(first, N_PAD))


def _tri_dot(x, tri):
    return jnp.dot(x.astype(BF16), tri, preferred_element_type=F32)


def _stack_halves(x, lo):
    a, b = _halves(x, lo)
    return jnp.concatenate([a, b], axis=0)


def _pair(a, b, lo):
    return jnp.where(lo, a, b)


def _chunk_starts(lp):
    return [min(c * KEYS, lp - KEYS) for c in range(-(-lp // KEYS))]


def _put_rows(ref, r0, bq, a, b):
    for t in range(bq // BLOCK):
        part = slice(t * BLOCK, (t + 1) * BLOCK)
        ref[0, 0, r0 // BLOCK + t] = jnp.concatenate([a[:, part], b[:, part], jnp.zeros((6, BLOCK), F32)], axis=0)


def _get_rows(ref, r0, bq):
    return [jnp.concatenate([ref[0, 0, r0 // BLOCK + t, h:h + 1, :] for t in range(bq // BLOCK)], axis=1) for h in range(2)]


def _n_chunks(i, keys):
    return ((i + 1) * BLOCK + keys - 1) // keys


QROWS = 512


def _for_query_tiles(nb, tile, keys=KEYS):
    per = QROWS // BLOCK

    def step(j, _):
        tile(pl.multiple_of(j * QROWS, QROWS), QROWS, (j + 1) * (QROWS // keys))
        return 0

    lax.fori_loop(0, nb // per, step, 0)
    for i in range(nb - nb % per, nb):
        tile(i * BLOCK, BLOCK, _n_chunks(i, keys))


def _walk_chunks(r0, n, chunk, carry, leftwards=False, keys=KEYS):
    diag = jnp.maximum(r0 // keys, 1)

    def span(first, last, masked, carry):
        def step(t, cr):
            return chunk(last - 1 - t if leftwards else first + t, cr, masked)
        return lax.fori_loop(0, last - first, step, carry)

    spans = [(0, 1, True), (1, diag, False), (diag, n, True)]
    for first, last, masked in (reversed(spans) if leftwards else spans):
        carry = span(first, last, masked, carry)
    return carry


def _where(valid, x, other):
    return x if valid is None else jnp.where(valid, x, other)


HEAD_SCALE = HEAD ** -0.5
assert math.frexp(HEAD_SCALE)[0] == 0.5


def _sb_scores(q_h, k, valid, after):
    z = _dot_nt(q_h, k)
    lb = jnp.minimum(z, 0.0) - jnp.log(1.0 + jnp.exp(-jnp.abs(z)))
    l1m_all = lb - z
    l1m = _where(valid, l1m_all, 0.0)
    return lb, l1m_all, l1m, _tri_dot(l1m, after)


def _pair_stat_spec(nb):
    return pl.BlockSpec((1, 1, nb, 8, BLOCK), lambda bi, hp: (bi, hp, 0, 0, 0))


def _sb_fwd(proj3, name):
    b, lp, _ = proj3.shape
    nb = lp // BLOCK
    npair = SB_HEADS // 2

    keys = SB_FWD_KEYS

    def body(q_ref, k_ref, v_ref, o_ref, tot_ref):
        lo = _lo_lanes()
        after = (_iota2((keys, keys), 0) > _iota2((keys, keys), 1)).astype(BF16)

        def qtile(r0, bq, n):
            qs = _halves((q_ref[0, pl.ds(r0, bq), :] * HEAD_SCALE).astype(BF16), lo)
            t_idx = r0 + _iota2((bq, keys), 0)

            def kchunk(c, carry, masked):
                cs, acc = carry[:2], carry[2]
                s0, valid = _key_chunk(c, lp, t_idx, True, 1)
                valid = valid if masked else None
                k = k_ref[0, pl.ds(s0, keys), :].astype(BF16)
                a_s, new = [], []
                for h in range(2):
                    lb, _, l1m, suf = _sb_scores(qs[h], k, valid, after)
                    a_s.append(_where(valid, jnp.exp(lb + suf + cs[h]), 0.0).astype(BF16))
                    new.append(cs[h] + jnp.sum(l1m, axis=1, keepdims=True))
                v_bd = _stack_halves(v_ref[0, pl.ds(s0, keys), :].astype(BF16), lo)
                return (*new, acc + jnp.dot(jnp.concatenate(a_s, axis=1), v_bd, preferred_element_type=F32))

            zero = jnp.zeros((bq, 1), F32)
            c_a, c_b, acc = _walk_chunks(r0, n, kchunk, (zero, zero, jnp.zeros((bq, BLOCK), F32)), leftwards=True, keys=keys)
            o_ref[0, pl.ds(r0, bq), :] = acc
            tot_ref[0, pl.ds(r0, bq), :] = jnp.broadcast_to(_pair(c_a, c_b, lo), (bq, BLOCK))

        _for_query_tiles(nb, qtile, keys)

    def col(first):
        return pl.BlockSpec((1, lp, 2 * HEAD), lambda bi, hp: (bi, 0, first // (2 * HEAD) + hp))

    shp = jax.ShapeDtypeStruct((b, lp, SB_HEADS * HEAD), F32)
    return pl.pallas_call(
        body, grid=(b, npair), in_specs=[col(EV_Q), col(EV_K), col(EV_V)], out_specs=[col(0), col(0)], out_shape=[shp, shp],
        compiler_params=_params(("parallel", "parallel")), name=name,
    )(proj3, proj3, proj3)


def _sb_bwd(proj3, tot, do, name):
    b, lp, _ = proj3.shape
    nb = lp // BLOCK
    npair = SB_HEADS // 2

    def body(q_ref, k_ref, v_ref, tot_ref, do_ref, dq_ref, dk_ref, dv_ref):
        lo = _lo_lanes()
        after = (_iota2((KEYS, KEYS), 0) > _iota2((KEYS, KEYS), 1)).astype(BF16)
        before = (_iota2((KEYS, KEYS), 0) < _iota2((KEYS, KEYS), 1)).astype(BF16)
        dk_ref[...] = jnp.zeros_like(dk_ref)
        dv_ref[...] = jnp.zeros_like(dv_ref)

        def qtile(r0, bq, n):
            rows = pl.ds(r0, bq)
            qs = _halves((q_ref[0, rows, :] * HEAD_SCALE).astype(BF16), lo)
            dos = _halves(do_ref[0, rows, :].astype(BF16), lo)
            tot_i = tot_ref[0, rows, :]
            tots = (tot_i[:, 0:1], tot_i[:, HEAD:HEAD + 1])
            q_st, do_st = jnp.concatenate(qs, axis=0), jnp.concatenate(dos, axis=0)
            t_idx = r0 + _iota2((bq, KEYS), 0)

            def kchunk(c, carry, masked):
                s0, valid = _key_chunk(c, lp, t_idx, True, 1)
                valid = valid if masked else None
                keys = pl.ds(s0, KEYS)
                k = k_ref[0, keys, :].astype(BF16)
                v = v_ref[0, keys, :].astype(BF16)
                a_s, dzs, new = [], [], []
                for h in range(2):
                    left, pre = carry[2 * h], carry[2 * h + 1]
                    lb, l1m_all, l1m, suf = _sb_scores(qs[h], k, valid, after)
                    here = jnp.sum(l1m, axis=1, keepdims=True)
                    a = _where(valid, jnp.exp(lb + suf + (tots[h] - left - here)), 0.0)
                    w = a * _dot_nt(dos[h], v)
                    dz = _where(valid, w * jnp.exp(l1m_all) - (pre + _tri_dot(w, before)) * jnp.exp(lb), 0.0)
                    new += [left + here, pre + jnp.sum(w, axis=1, keepdims=True)]
                    a_s.append(a.astype(BF16))
                    dzs.append(dz.astype(BF16))
                dk_ref[0, keys, :] += _dot_tn(jnp.concatenate(dzs, axis=0), q_st)
                dv_ref[0, keys, :] += _dot_tn(jnp.concatenate(a_s, axis=0), do_st)
                dq = carry[4] + jnp.dot(jnp.concatenate(dzs, axis=1), _stack_halves(k, lo), preferred_element_type=F32)
                return (*new, dq)

            zero = jnp.zeros((bq, 1), F32)
            out = _walk_chunks(r0, n, kchunk, (zero, zero, zero, zero, jnp.zeros((bq, BLOCK), F32)))
            dq_ref[0, rows, :] = out[4] * HEAD_SCALE

        _for_query_tiles(nb, qtile)

    def col(first):
        return pl.BlockSpec((1, lp, 2 * HEAD), lambda bi, hp: (bi, 0, first // (2 * HEAD) + hp))

    shp = jax.ShapeDtypeStruct((b, lp, SB_HEADS * HEAD), F32)
    return pl.pallas_call(
        body, grid=(b, npair), in_specs=[col(EV_Q), col(EV_K), col(EV_V), col(0), col(0)], out_specs=[col(0)] * 3, out_shape=[shp] * 3,
        compiler_params=_params(("parallel", "parallel")), name=name,
    )(proj3, proj3, proj3, tot, do)


def _rope_tables(lp):
    half = MLA_ROPE // 2
    pos = (np.arange(lp) - N_PAD).astype(np.float32)
    inv = jnp.asarray(ROPE_BASE, F32) ** (-jnp.arange(half, dtype=F32) / half)
    ang = jnp.asarray(pos)[:, None] * inv[None, :]
    cos, sin = jnp.cos(ang), jnp.sin(ang)
    zeros = lambda n: jnp.zeros((lp, n), F32)
    c = jnp.concatenate([jnp.ones((lp, MLA_NOPE), F32), cos, cos, zeros(32)], axis=1)
    s1 = jnp.concatenate([zeros(MLA_NOPE), -sin, zeros(half), zeros(32)], axis=1)
    s2 = jnp.concatenate([zeros(MLA_NOPE), zeros(half), sin, zeros(32)], axis=1)
    return c, s1, s2


def _rope(x, c, s1, s2):
    half = MLA_ROPE // 2
    return x * c + pltpu.roll(x, BLOCK - half, 1) * s1 + pltpu.roll(x, half, 1) * s2


def _rope_t(dy, c, s1, s2):
    half = MLA_ROPE // 2
    return dy * c + pltpu.roll(dy * s1, half, 1) + pltpu.roll(dy * s2, BLOCK - half, 1)


def _rms_rows(x, g):
    r = lax.rsqrt(jnp.mean(x * x, axis=-1, keepdims=True) + NORM_EPS)
    return x * r, r


def _prep_rows(lp):
    return lp // 4 if lp % 64 == 0 else BLOCK


def _mla_prep_fwd(proj3, gq, gkv, wq, wk, wv, tabs, name):
    b, lp, _ = proj3.shape
    rows = _prep_rows(lp)
    hw = MLA_HEADS * BLOCK

    def body(cq_ref, ckv_ref, kr_ref, gq_ref, gkv_ref, wq_ref, wk_ref, wv_ref, c_ref, s1_ref, s2_ref, qf_ref, kf_ref, v_ref):
        c, s1, s2 = c_ref[...], s1_ref[...], s2_ref[...]
        xq, _ = _rms_rows(cq_ref[0], None)
        qh = _dot(xq * gq_ref[...], wq_ref[...])
        xk, _ = _rms_rows(ckv_ref[0], None)
        ckv_n = xk * gkv_ref[...]
        kv = _dot(ckv_n, wk_ref[...])
        v_ref[0] = _dot(ckv_n, wv_ref[...]).astype(v_ref.dtype)
        kr = _rope(kr_ref[0], c, s1, s2)
        for h in range(MLA_HEADS):
            ls = slice(h * BLOCK, (h + 1) * BLOCK)
            qf_ref[0, :, ls] = _rope(qh[:, ls], c, s1, s2).astype(qf_ref.dtype)
            kf_ref[0, :, ls] = (kv[:, ls] + kr).astype(kf_ref.dtype)

    def col(first, width):
        return pl.BlockSpec((1, rows, width), lambda bi, n: (bi, n, first // width))

    def whole(a):
        return pl.BlockSpec(a.shape, lambda bi, n: (0,) * a.ndim)

    tab = pl.BlockSpec((rows, BLOCK), lambda bi, n: (n, 0))
    return pl.pallas_call(
        body, grid=(b, lp // rows),
        in_specs=[col(EV_CQ, MLA_Q_LORA), col(EV_CKV, MLA_KV_LORA), col(EV_KR, BLOCK), whole(gq), whole(gkv), whole(wq), whole(wk),
                  whole(wv), tab, tab, tab],
        out_specs=[col(0, hw), col(0, hw), col(0, MLA_HEADS * HEAD)],
        out_shape=[jax.ShapeDtypeStruct((b, lp, hw), BF16), jax.ShapeDtypeStruct((b, lp, hw), BF16),
                   jax.ShapeDtypeStruct((b, lp, MLA_HEADS * HEAD), BF16)],
        compiler_params=_params(("parallel", "parallel")), name=name,
    )(proj3, proj3, proj3, gq, gkv, wq, wk, wv, *tabs)


def _mla_prep_bwd(proj3, gq, gkv, wq, wk, wv, tabs, dqf, dkf, dv, name):
    b, lp, _ = proj3.shape
    rows = _prep_rows(lp)
    hw = MLA_HEADS * BLOCK

    def body(cq_ref, ckv_ref, gq_ref, gkv_ref, wq_ref, wk_ref, wv_ref, c_ref, s1_ref, s2_ref, dqf_ref, dkf_ref, dv_ref,
             dcq_ref, dckv_ref, dkr_ref, dwq_ref, dwk_ref, dwv_ref, dgq_ref, dgkv_ref, dqh):
        @pl.when((pl.program_id(0) == 0) & (pl.program_id(1) == 0))
        def _():
            for r in (dwq_ref, dwk_ref, dwv_ref, dgq_ref, dgkv_ref):
                r[...] = jnp.zeros_like(r)

        c, s1, s2 = c_ref[...], s1_ref[...], s2_ref[...]
        dkr = jnp.zeros((rows, BLOCK), F32)
        for h in range(MLA_HEADS):
            ls = slice(h * BLOCK, (h + 1) * BLOCK)
            dqh[:, ls] = _rope_t(dqf_ref[0, :, ls].astype(F32), c, s1, s2).astype(dqh.dtype)
            dkr = dkr + dkf_ref[0, :, ls].astype(F32)
        dkr_ref[0] = _rope_t(dkr, c, s1, s2).astype(dkr_ref.dtype)

        def norm_bwd(x, g, dy, dg_ref):
            xr, r = _rms_rows(x, None)
            u = dy * g
            dg_ref[...] += jnp.sum(dy * xr, axis=0, keepdims=True)
            return r * (u - xr * jnp.mean(u * xr, axis=-1, keepdims=True))

        xq, _ = _rms_rows(cq_ref[0], None)
        cq_n = xq * gq_ref[...]
        dwq_ref[...] += _dot_tn(cq_n, dqh[...])
        dcq_ref[0] = norm_bwd(cq_ref[0], gq_ref[...], _dot_nt(dqh[...], wq_ref[...]), dgq_ref).astype(dcq_ref.dtype)
        xk, _ = _rms_rows(ckv_ref[0], None)
        ckv_n = xk * gkv_ref[...]
        dkf_, dv_ = dkf_ref[0], dv_ref[0]
        dwk_ref[...] += _dot_tn(ckv_n, dkf_)
        dwv_ref[...] += _dot_tn(ckv_n, dv_)
        dckv_n = _dot_nt(dkf_, wk_ref[...]) + _dot_nt(dv_, wv_ref[...])
        dckv_ref[0] = norm_bwd(ckv_ref[0], gkv_ref[...], dckv_n, dgkv_ref).astype(dckv_ref.dtype)

    def col(first, width):
        return pl.BlockSpec((1, rows, width), lambda bi, n: (bi, n, first // width))

    def whole(a):
        return pl.BlockSpec(a.shape, lambda bi, n: (0,) * len(a.shape))

    tab = pl.BlockSpec((rows, BLOCK), lambda bi, n: (n, 0))
    acc_shapes = [jax.ShapeDtypeStruct(a.shape, F32) for a in (wq, wk, wv, gq, gkv)]
    return pl.pallas_call(
        body, grid=(b, lp // rows),
        in_specs=[col(EV_CQ, MLA_Q_LORA), col(EV_CKV, MLA_KV_LORA), whole(gq), whole(gkv), whole(wq), whole(wk), whole(wv), tab, tab, tab,
                  col(0, hw), col(0, hw), col(0, MLA_HEADS * HEAD)],
        out_specs=[col(0, MLA_Q_LORA), col(0, MLA_KV_LORA), col(0, BLOCK)] + [whole(a) for a in acc_shapes],
        out_shape=[jax.ShapeDtypeStruct((b, lp, MLA_Q_LORA), BF16), jax.ShapeDtypeStruct((b, lp, MLA_KV_LORA), BF16),
                   jax.ShapeDtypeStruct((b, lp, BLOCK), BF16)] + acc_shapes,
        scratch_shapes=[pltpu.VMEM((rows, hw), BF16)],
        compiler_params=_params(("arbitrary", "arbitrary")), name=name,
    )(proj3, proj3, gq, gkv, wq, wk, wv, *tabs, dqf, dkf, dv)


def _mla_fwd(qf, kf, v, name):
    b, lp, _ = qf.shape
    nb = lp // BLOCK
    npair = MLA_HEADS // 2
    scale = (MLA_NOPE + MLA_ROPE) ** -0.5
    starts = _chunk_starts(lp)

    def body(q_ref, k_ref, v_ref, o_ref, lse_ref, vt_ref):
        for c, s0 in enumerate(starts):
            vt_ref[c] = _split_rows_t(v_ref[0, s0:s0 + KEYS, :].astype(F32))

        def qtile(r0, bq, n):
            qs = [q_ref[0, pl.ds(r0, bq), h * BLOCK:(h + 1) * BLOCK] for h in range(2)]
            t_idx = r0 + _iota2((KEYS, bq), 1)

            def kchunk(c, carry, masked):
                stats, acc = carry[:4], carry[4]
                s0, valid = _key_chunk(c, lp, t_idx, False, 0)
                valid = valid if masked else None
                ps, new, alphas = [], [], []
                for h in range(2):
                    m, l = stats[2 * h], stats[2 * h + 1]
                    s = _where(valid, _dot_nt(k_ref[0, pl.ds(s0, KEYS), h * BLOCK:(h + 1) * BLOCK], qs[h]) * scale, NEG)
                    m_new = jnp.maximum(m, jnp.max(s, axis=0, keepdims=True))
                    p = _where(valid, jnp.exp(s - m_new), 0.0)
                    alpha = jnp.exp(m - m_new)
                    new += [m_new, alpha * l + jnp.sum(p, axis=0, keepdims=True)]
                    alphas.append(alpha)
                    ps.append(p.astype(BF16))
                pv = jnp.dot(vt_ref[c], jnp.concatenate(ps, axis=0), preferred_element_type=F32)
                return (*new, _rows_of_pair(alphas[0], alphas[1]) * acc + pv)

            neg, zero = jnp.full((1, bq), NEG, F32), jnp.zeros((1, bq), F32)
            m_a, l_a, m_b, l_b, acc = _walk_chunks(r0, n, kchunk, (neg, zero, neg, zero, jnp.zeros((BLOCK, bq), F32)))
            safe = [jnp.where(l > 0.0, l, 1.0) for l in (l_a, l_b)]
            o_ref[0, pl.ds(r0, bq), :] = (acc / _rows_of_pair(safe[0], safe[1])).T
            lse = [jnp.where(l > 0.0, m + jnp.log(sf), 0.0) for m, l, sf in ((m_a, l_a, safe[0]), (m_b, l_b, safe[1]))]
            _put_rows(lse_ref, r0, bq, lse[0], lse[1])

        _for_query_tiles(nb, qtile)

    wide = pl.BlockSpec((1, lp, 2 * BLOCK), lambda bi, hp: (bi, 0, hp))
    thin = pl.BlockSpec((1, lp, 2 * HEAD), lambda bi, hp: (bi, 0, hp))
    return pl.pallas_call(
        body, grid=(b, npair), in_specs=[wide, wide, thin], out_specs=[thin, _pair_stat_spec(nb)],
        out_shape=[jax.ShapeDtypeStruct((b, lp, MLA_HEADS * HEAD), F32), jax.ShapeDtypeStruct((b, npair, nb, 8, BLOCK), F32)],
        scratch_shapes=[pltpu.VMEM((len(starts), BLOCK, 2 * KEYS), BF16)],
        compiler_params=_params(("parallel", "parallel")), name=name,
    )(qf, kf, v)


def _mla_bwd(qf, kf, v, o, lse, do, name):
    b, lp, _ = qf.shape
    nb = lp // BLOCK
    npair = MLA_HEADS // 2
    scale = (MLA_NOPE + MLA_ROPE) ** -0.5

    starts = _chunk_starts(lp)

    def body(q_ref, k_ref, v_ref, o_ref, lse_ref, do_ref, dq_ref, dk_ref, dv_ref, kt_ref):
        lo = _lo_lanes()
        dk_ref[...] = jnp.zeros_like(dk_ref)
        dv_ref[...] = jnp.zeros_like(dv_ref)
        for c, s0 in enumerate(starts):
            for h in range(2):
                kt_ref[c, h] = k_ref[0, s0:s0 + KEYS, h * BLOCK:(h + 1) * BLOCK].astype(F32).T.astype(BF16)

        def qtile(r0, bq, n):
            rows = pl.ds(r0, bq)
            qs = [q_ref[0, rows, h * BLOCK:(h + 1) * BLOCK] for h in range(2)]
            do_i = do_ref[0, rows, :]
            dos = _halves(do_i.astype(BF16), lo)
            do_st = jnp.concatenate(dos, axis=0)
            both = (do_i * o_ref[0, rows, :]).T
            dsum = (jnp.sum(both[:HEAD], axis=0, keepdims=True), jnp.sum(both[HEAD:], axis=0, keepdims=True))
            lses = _get_rows(lse_ref, r0, bq)
            t_idx = r0 + _iota2((KEYS, bq), 1)

            def kchunk(c, dqts, masked):
                s0, valid = _key_chunk(c, lp, t_idx, False, 0)
                valid = valid if masked else None
                keys = pl.ds(s0, KEYS)
                v_c = v_ref[0, keys, :]
                ps, out = [], []
                for h in range(2):
                    lanes = slice(h * BLOCK, (h + 1) * BLOCK)
                    s = _dot_nt(k_ref[0, keys, lanes], qs[h]) * scale
                    p = _where(valid, jnp.exp(s - lses[h]), 0.0)
                    ds = (p * (_dot_nt(v_c, dos[h]) - dsum[h]) * scale).astype(BF16)
                    dk_ref[0, keys, lanes] += jnp.dot(ds, qs[h], preferred_element_type=F32)
                    out.append(dqts[h] + jnp.dot(kt_ref[c, h], ds, preferred_element_type=F32))
                    ps.append(p.astype(BF16))
                dv_ref[0, keys, :] += jnp.dot(jnp.concatenate(ps, axis=1), do_st, preferred_element_type=F32)
                return tuple(out)

            zero = jnp.zeros((BLOCK, bq), F32)
            dq_a, dq_b = _walk_chunks(r0, n, kchunk, (zero, zero))
            dq_ref[0, rows, 0:BLOCK] = dq_a.T
            dq_ref[0, rows, BLOCK:2 * BLOCK] = dq_b.T

        _for_query_tiles(nb, qtile)

    wide = pl.BlockSpec((1, lp, 2 * BLOCK), lambda bi, hp: (bi, 0, hp))
    thin = pl.BlockSpec((1, lp, 2 * HEAD), lambda bi, hp: (bi, 0, hp))
    return pl.pallas_call(
        body, grid=(b, npair), in_specs=[wide, wide, thin, thin, _pair_stat_spec(nb), thin], out_specs=[wide, wide, thin],
        out_shape=[jax.ShapeDtypeStruct(qf.shape, F32), jax.ShapeDtypeStruct(qf.shape, F32), jax.ShapeDtypeStruct(v.shape, F32)],
        scratch_shapes=[pltpu.VMEM((len(starts), 2, BLOCK, KEYS), BF16)],
        compiler_params=_params(("parallel", "parallel")), name=name,
    )(qf, kf, v, o, lse, do)


SWA_KEYS = 2 * BLOCK + N_META


def _swa_keys(k_ref, v_ref, n, kv):
    prev = jnp.maximum(n - 1, 0)
    rows = lambda blk: pl.ds(pl.multiple_of(blk * BLOCK, BLOCK), BLOCK)
    mine = (_iota2((1, BLOCK), 1) >= HEAD).astype(jnp.int32) == kv

    def both_halves(ref):
        x = jnp.concatenate([ref[0, rows(prev), :], ref[0, rows(n), :], ref[0, N_PAD:BLOCK, :]], axis=0)
        return jnp.where(mine, x, pltpu.roll(x, HEAD, 1))

    slot = _iota2((SWA_KEYS, BLOCK), 0)
    s_idx = jnp.where(slot < 2 * BLOCK, (n - 1) * BLOCK + slot, slot - 2 * BLOCK + N_PAD)
    dist = n * BLOCK + _iota2((SWA_KEYS, BLOCK), 1) - s_idx
    band = (slot < 2 * BLOCK) & (dist >= 0) & (dist < SWA_WINDOW) & (s_idx >= BLOCK)
    meta = (slot >= 2 * BLOCK) & (dist >= 0)
    return both_halves(k_ref), both_halves(v_ref), band | meta, dist.astype(F32), prev


def _pad_keys(x):
    return jnp.concatenate([x, jnp.zeros((3 * BLOCK - SWA_KEYS, x.shape[1]), x.dtype)], axis=0)


def _swa_probs(q_h, kdup, valid, dist, head, sink_ref):
    slope = jnp.exp(jnp.full((1, 1), -8.0 * math.log(2.0) / SWA_HEADS, F32) * (head + 1).astype(F32))
    s = jnp.where(valid, _dot_nt(kdup, q_h) - slope * dist, NEG)
    sink = sink_ref[pl.ds(head, 1), 0:1]
    m = jnp.maximum(jnp.max(s, axis=0, keepdims=True), sink)
    e = jnp.where(valid, jnp.exp(s - m), 0.0)
    es = jnp.exp(sink - m)
    inv = 1.0 / (jnp.sum(e, axis=0, keepdims=True) + es)
    return e * inv, es * inv


SWA_PAIRS = SWA_HEADS // SWA_KV_HEADS // 2
SWA_GROUP = SWA_PAIRS * 2 * HEAD


def _swa_specs(b, lp):
    nb = lp // BLOCK
    qcol = lambda first: pl.BlockSpec((1, BLOCK, SWA_GROUP), lambda bi, kv, n: (bi, n, first // SWA_GROUP + kv))
    kcol = lambda first: pl.BlockSpec((1, lp, BLOCK), lambda bi, kv, n: (bi, 0, first // BLOCK))
    sink = pl.BlockSpec((SWA_HEADS, BLOCK), lambda bi, kv, n: (0, 0))
    return (b, SWA_KV_HEADS, nb), qcol, kcol, sink


def _swa_fwd(proj3, sinks, name):
    b, lp, _ = proj3.shape
    grid, qcol, kcol, sink = _swa_specs(b, lp)

    def body(q_ref, k_ref, v_ref, sink_ref, o_ref):
        kv, n = pl.program_id(1), pl.program_id(2)
        lo = _lo_lanes()
        kdup, vdup, valid, dist, _ = _swa_keys(k_ref, v_ref, n, kv)
        kdup = kdup.astype(BF16)
        vt = _split_rows_t(_pad_keys(vdup))
        for p in range(SWA_PAIRS):
            lanes = slice(p * BLOCK, (p + 1) * BLOCK)
            qs = _halves((q_ref[0, :, lanes] * HEAD_SCALE).astype(BF16), lo)
            probs = [_swa_probs(qs[hh], kdup, valid, dist, (kv * SWA_PAIRS + p) * 2 + hh, sink_ref)[0].astype(BF16) for hh in range(2)]
            o_ref[0, :, lanes] = jnp.dot(vt, jnp.concatenate([_pad_keys(pr) for pr in probs], axis=0), preferred_element_type=F32).T

    return pl.pallas_call(
        body, grid=grid, in_specs=[qcol(OD_Q), kcol(OD_K), kcol(OD_V), sink], out_specs=qcol(0),
        out_shape=jax.ShapeDtypeStruct((b, lp, SWA_HEADS * HEAD), F32),
        compiler_params=_params(("parallel", "parallel", "parallel")), name=name,
    )(proj3, proj3, proj3, sinks)


def _swa_bwd(proj3, sinks, do, name):
    b, lp, _ = proj3.shape
    nb = lp // BLOCK
    grid, qcol, kcol, sink = _swa_specs(b, lp)

    def body(q_ref, k_ref, v_ref, sink_ref, do_ref, dq_ref, dk_ref, dv_ref, dsink_ref, dk_acc, dv_acc):
        kv, n = pl.program_id(1), pl.program_id(2)

        @pl.when((n == 0) & (pl.program_id(0) == 0) & (kv == 0))
        def _():
            dsink_ref[...] = jnp.zeros_like(dsink_ref)

        @pl.when(n == 0)
        def _():
            dk_acc[...] = jnp.zeros_like(dk_acc)
            dv_acc[...] = jnp.zeros_like(dv_acc)

        lo = _lo_lanes()
        kdup, vdup, valid, dist, prev = _swa_keys(k_ref, v_ref, n, kv)
        kt = _split_rows_t(_pad_keys(kdup))
        kdup, vdup = kdup.astype(BF16), vdup.astype(BF16)
        dkc = jnp.zeros((SWA_KEYS, BLOCK), F32)
        dvc = jnp.zeros((SWA_KEYS, BLOCK), F32)
        for p in range(SWA_PAIRS):
            lanes = slice(p * BLOCK, (p + 1) * BLOCK)
            qs = _halves((q_ref[0, :, lanes] * HEAD_SCALE).astype(BF16), lo)
            dos = _halves(do_ref[0, :, lanes].astype(BF16), lo)
            dss, prs = [], []
            for hh in range(2):
                head = (kv * SWA_PAIRS + p) * 2 + hh
                pr, ps = _swa_probs(qs[hh], kdup, valid, dist, head, sink_ref)
                dp = _dot_nt(vdup, dos[hh])
                dsum = jnp.sum(pr * dp, axis=0, keepdims=True)
                dsink_ref[pl.ds(head, 1), :] += jnp.broadcast_to(-jnp.sum(ps * dsum, axis=1, keepdims=True), (1, BLOCK))
                dss.append((pr * (dp - dsum)).astype(BF16))
                prs.append(pr.astype(BF16))
            dq_ref[0, :, lanes] = jnp.dot(kt, jnp.concatenate([_pad_keys(d) for d in dss], axis=0), preferred_element_type=F32).T * HEAD_SCALE
            dkc = dkc + jnp.dot(jnp.concatenate(dss, axis=1), jnp.concatenate(qs, axis=0), preferred_element_type=F32)
            dvc = dvc + jnp.dot(jnp.concatenate(prs, axis=1), jnp.concatenate(dos, axis=0), preferred_element_type=F32)
        rows = lambda blk: pl.ds(pl.multiple_of(blk * BLOCK, BLOCK), BLOCK)
        for r, part in ((rows(prev), slice(0, BLOCK)), (rows(n), slice(BLOCK, 2 * BLOCK)), (slice(N_PAD, BLOCK), slice(2 * BLOCK, SWA_KEYS))):
            dk_acc[r, :] += dkc[part]
            dv_acc[r, :] += dvc[part]

        for acc, ref in ((dk_acc, dk_ref), (dv_acc, dv_ref)):
            @pl.when((n == nb - 1) & (kv == 0))
            def _():
                x = acc[...]
                ref[0] = x + pltpu.roll(x, HEAD, 1)

            @pl.when((n == nb - 1) & (kv == 1))
            def _():
                x = acc[...]
                ref[0] = jnp.where(lo, ref[0], x + pltpu.roll(x, HEAD, 1))

    kvout = pl.BlockSpec((1, lp, BLOCK), lambda bi, kv, n: (bi, 0, 0))
    kvshape = jax.ShapeDtypeStruct((b, lp, BLOCK), F32)
    return pl.pallas_call(
        body, grid=grid, in_specs=[qcol(OD_Q), kcol(OD_K), kcol(OD_V), sink, qcol(0)], out_specs=[qcol(0), kvout, kvout, sink],
        out_shape=[jax.ShapeDtypeStruct((b, lp, SWA_HEADS * HEAD), F32), kvshape, kvshape, jax.ShapeDtypeStruct((SWA_HEADS, BLOCK), F32)],
        scratch_shapes=[pltpu.VMEM((lp, BLOCK), F32), pltpu.VMEM((lp, BLOCK), F32)],
        compiler_params=_params(("arbitrary", "arbitrary", "arbitrary")), name=name,
    )(proj3, proj3, proj3, sinks, do)


def _kernel_weights(ev_w_in, ev_w_uq, ev_w_ukv, od_w_in):
    zeros = lambda r, c: jnp.zeros((r, c), ev_w_in.dtype)
    q_sb, k_sb, v_sb, g_sb, c_q, c_kv, k_r, g_mla = jnp.split(ev_w_in, [512, 1024, 1536, 2048, 2304, 2432, 2464], axis=1)
    w0 = jnp.concatenate([g_sb, g_mla, q_sb, k_sb, v_sb, c_q, c_kv, zeros(D_MODEL, MLA_NOPE), k_r, zeros(D_MODEL, 32)], axis=1)
    uq = ev_w_uq.reshape(MLA_Q_LORA, MLA_HEADS, MLA_NOPE + MLA_ROPE)
    wq = jnp.pad(uq, ((0, 0), (0, 0), (0, BLOCK - MLA_NOPE - MLA_ROPE))).reshape(MLA_Q_LORA, MLA_HEADS * BLOCK)
    ukv = ev_w_ukv.reshape(MLA_KV_LORA, MLA_HEADS, BLOCK)
    wk = jnp.pad(ukv[:, :, :MLA_NOPE], ((0, 0), (0, 0), (0, BLOCK - MLA_NOPE))).reshape(MLA_KV_LORA, MLA_HEADS * BLOCK)
    wv = ukv[:, :, MLA_NOPE:].reshape(MLA_KV_LORA, MLA_HEADS * HEAD)
    q, k, v, g = jnp.split(od_w_in, [1024, 1152, 1280], axis=1)
    w1 = jnp.concatenate([g, q, k, v], axis=1)
    return w0, wq, wk, wv, w1


def _od_w_in_grad(dw1):
    sl = lambda a, first, n: a[:, first:first + n]
    return jnp.concatenate([sl(dw1, OD_Q, 1024), sl(dw1, OD_K, 128), sl(dw1, OD_V, 128), sl(dw1, OD_G, 1024)], axis=1)


def _original_grads(dw0, dwq, dwk, dwv):
    sl = lambda a, first, n: a[:, first:first + n]
    d_ev_w_in = jnp.concatenate([sl(dw0, EV_Q, 512), sl(dw0, EV_K, 512), sl(dw0, EV_V, 512), sl(dw0, EV_G, 512), sl(dw0, EV_CQ, 256),
                                 sl(dw0, EV_CKV, 128), sl(dw0, EV_KR + MLA_NOPE, MLA_ROPE), sl(dw0, EV_G + 512, 512)], axis=1)
    d_uq = dwq.reshape(MLA_Q_LORA, MLA_HEADS, BLOCK)[:, :, :MLA_NOPE + MLA_ROPE].reshape(MLA_Q_LORA, -1)
    d_ukv = jnp.concatenate([dwk.reshape(MLA_KV_LORA, MLA_HEADS, BLOCK)[:, :, :MLA_NOPE], dwv.reshape(MLA_KV_LORA, MLA_HEADS, HEAD)],
                            axis=2).reshape(MLA_KV_LORA, -1)
    return d_ev_w_in, d_uq, d_ukv


def _meta_rows_sum(dh0_3):
    b, _, d = dh0_3.shape

    def body(x_ref, o_ref):
        acc = x_ref[0, N_PAD:BLOCK, :]
        for i in range(1, b):
            acc = acc + x_ref[i, N_PAD:BLOCK, :]
        o_ref[...] = acc

    return pl.pallas_call(
        body, grid=(1,), in_specs=[pl.BlockSpec((b, BLOCK, d), lambda i: (0, 0, 0))], out_specs=pl.BlockSpec((N_META, d), lambda i: (0, 0)),
        out_shape=jax.ShapeDtypeStruct((N_META, d), F32), compiler_params=_params(("arbitrary",)), name="meta_rows_sum",
    )(dh0_3)


def _local_step(x, meta, norm_g, final_g, gq, gkv, sinks, target, ev_w_in, ev_w_uq, ev_w_ukv, wo0, od_w_in, wo1, reduce_early=None,
                reduce_in_proj=None):
    b, seq, d = x.shape
    lp = seq + BLOCK
    t = b * lp
    w0, wq, wk, wv, w1 = _kernel_weights(ev_w_in, ev_w_uq, ev_w_ukv, od_w_in)
    h0 = jnp.concatenate([jnp.zeros((b, N_PAD, d), F32), jnp.broadcast_to(meta[None], (b, N_META, d)), x], axis=1).reshape(t, d)
    tabs = _rope_tables(lp)
    g0, g1 = norm_g[0:1], norm_g[1:2]

    hn0 = _rms_fwd(h0, g0, "norm0")
    proj0 = _mm(hn0, w0, "inproj0")
    p0 = proj0.reshape(b, lp, EV_N)
    o_sb, sb_tot = _sb_fwd(p0, "sb_fwd")
    qf, kf, v = _mla_prep_fwd(p0, gq, gkv, wq, wk, wv, tabs, "mla_prep_fwd")
    o_mla, lse = _mla_fwd(qf, kf, v, "mla_fwd")
    o0 = [o_sb.reshape(t, -1), o_mla.reshape(t, -1)]
    ao0 = _gate_fwd(o0, proj0, "gate0")
    h1 = _mm(ao0, wo0, "outproj0", res=h0)

    hn1 = _rms_fwd(h1, g1, "norm1")
    proj1 = _mm(hn1, w1, "inproj1")
    p1 = proj1.reshape(b, lp, OD_N)
    sinks_b = jnp.broadcast_to(sinks.reshape(SWA_HEADS, 1), (SWA_HEADS, BLOCK))
    o1 = _swa_fwd(p1, sinks_b, "swa_fwd").reshape(t, -1)
    ao1 = _gate_fwd([o1], proj1, "gate1")
    h2 = _mm(ao1, wo1, "outproj1", res=h1)

    dh2, d_final_g, loss = _loss_head(h2, final_g.reshape(1, d), target, b, lp)

    d_wo1 = _mm_tn(ao1, dh2, "d_wo1")
    dao1 = _mm_nt(dh2, wo1, "d_ao1")
    (do1,), dg1 = _gate_bwd(dao1, [o1], proj1, "gate1_bwd")
    dq1, dk4, dv4, d_sinks = _swa_bwd(p1, sinks_b, do1.reshape(b, lp, -1), "swa_bwd")
    unheads = lambda a: a.reshape(t, SWA_KV_HEADS * HEAD).astype(BF16)
    dproj1 = jnp.concatenate([dg1, dq1.reshape(t, -1).astype(BF16), unheads(dk4), unheads(dv4)], axis=1)
    d_w1 = _mm_tn(hn1, dproj1, "d_w1")
    dhn1 = _mm_nt(dproj1, w1, "d_hn1")
    dh1, d_g1 = _rms_bwd(h1, g1, dhn1, dh2, "norm1_bwd")

    d_wo0 = _mm_tn(ao0, dh1, "d_wo0")
    early = dict(od_w_in=_od_w_in_grad(d_w1), ev_w_out=d_wo0, od_w_out=d_wo1)
    dao0 = _mm_nt(dh1, wo0, "d_ao0")
    if reduce_early is not None:
        pending, finish = reduce_early(early)
        dao0, pending = lax.optimization_barrier((dao0, pending))
        early_done = finish(pending)
    (do_sb, do_mla), dg0 = _gate_bwd(dao0, o0, proj0, "gate0_bwd")
    dq_sb, dk_sb, dv_sb = _sb_bwd(p0, sb_tot, do_sb.reshape(b, lp, -1), "sb_bwd")
    dqf, dkf, dv = _mla_bwd(qf, kf, v, o_mla, lse, do_mla.reshape(b, lp, -1), "mla_bwd")
    dcq, dckv, dkr, d_wq, d_wk, d_wv, d_gq, d_gkv = _mla_prep_bwd(p0, gq, gkv, wq, wk, wv, tabs, dqf, dkf, dv, "mla_prep_bwd")
    flat = lambda a: a.reshape(t, -1).astype(BF16)
    dproj0 = jnp.concatenate([dg0, flat(dq_sb), flat(dk_sb), flat(dv_sb), flat(dcq), flat(dckv), flat(dkr)], axis=1)
    if reduce_early is not None:
        dproj0, early_done = lax.optimization_barrier((dproj0, early_done))
    d_w0 = _mm_tn(hn0, dproj0, "d_w0")
    d_ev_w_in, d_uq, d_ukv = _original_grads(d_w0, d_wq, d_wk, d_wv)
    dhn0 = _mm_nt(dproj0, w0, "d_hn0")
    if reduce_in_proj is not None:
        pending, finish = reduce_in_proj(d_ev_w_in)
        dhn0, pending = lax.optimization_barrier((dhn0, pending))
        in_proj_done = finish(pending)
    dh0, d_g0 = _rms_bwd(h0, g0, dhn0, dh1, "norm0_bwd")
    dh0 = dh0.reshape(b, lp, d)

    grads = dict(meta=_meta_rows_sum(dh0), norm_g=jnp.concatenate([d_g0, d_g1], axis=0), final_g=d_final_g.reshape(d),
                 ev_q_norm_g=d_gq, ev_kv_norm_g=d_gkv, ev_w_uq=d_uq, ev_w_ukv=d_ukv, od_sinks=d_sinks[:, 0].reshape(1, SWA_HEADS))
    if reduce_early is None:
        return loss, dh0[:, BLOCK:], {**grads, **early, "ev_w_in": d_ev_w_in}
    return loss, dh0[:, BLOCK:], grads, in_proj_done, early_done


MESH = pl.DeviceIdType.MESH
ANY = pl.BlockSpec(memory_space=pl.ANY)


def _place():
    return lax.axis_index("x"), lax.axis_index("y"), lax.axis_index("c")


def _other_chips(x, y):
    return [(1 - x, y), (x, 1 - y), (1 - x, 1 - y)]


def _with_own_slot(slots, own):
    me = 2 * lax.axis_index("x") + lax.axis_index("y")
    return lax.dynamic_update_slice(slots, own[None], (me,) + (0,) * own.ndim)


def _gather_weights(packs, meta, name):
    n = len(packs)

    def body(*refs):
        ins, m_ref, outs, mo_ref = refs[:n], refs[n], refs[n + 1:2 * n + 1], refs[2 * n + 1]
        send_sems, recv_sems = refs[2 * n + 2:]
        x, y, c = _place()
        me, sib = 2 * x + y, (x, y, 1 - c)
        chips = _other_chips(x, y)

        def copy(k, src, dst, to):
            return pltpu.make_async_remote_copy(src_ref=src, dst_ref=dst, send_sem=send_sems.at[k], recv_sem=recv_sems.at[k], device_id=to,
                                                device_id_type=MESH)

        def half(i, chip, h):
            rows = packs[i].shape[0] // 2
            return outs[i].at[chip, pl.ds(h * rows, rows), :]

        def mine(i):
            rows = packs[i].shape[0] // 2
            return ins[i].at[pl.ds(c * rows, rows), :]

        sent = [copy(6 * i + k, mine(i), half(i, me, c), (px, py, c)) for i in range(n) for k, (px, py) in enumerate(chips)]
        sent += [copy(6 * n + k, m_ref, mo_ref.at[me], (px, py, c)) for k, (px, py) in enumerate(chips)]
        for cp in sent:
            cp.start()
        for i in range(n):
            for k, (px, py) in enumerate(chips):
                landed = half(i, 2 * px + py, c)
                copy(6 * i + k, mine(i), landed, (px, py, c)).wait_recv()
                fwd = copy(6 * i + 3 + k, landed, landed, sib)
                fwd.start()
                sent.append(fwd)
        for k, (px, py) in enumerate(chips):
            for i in range(n):
                other = half(i, 2 * px + py, 1 - c)
                copy(6 * i + 3 + k, other, other, sib).wait_recv()
            copy(6 * n + k, m_ref, mo_ref.at[2 * px + py], (px, py, c)).wait_recv()
        for cp in sent:
            cp.wait_send()

    nsem = 6 * n + 3
    res = pl.pallas_call(
        body, in_specs=[ANY] * (n + 1), out_specs=[ANY] * (n + 1),
        out_shape=[jax.ShapeDtypeStruct((N_CHIPS,) + a.shape, a.dtype) for a in list(packs) + [meta]],
        scratch_shapes=[pltpu.SemaphoreType.DMA((nsem,)), pltpu.SemaphoreType.DMA((nsem,))],
        name=name,
    )(*packs, meta)
    return [_with_own_slot(r, a) for r, a in zip(res[:n], packs)], _with_own_slot(res[n], meta)


def _grads_to_sibling(gs, name):
    n = len(gs)

    def body(*refs):
        ins, outs, send_sems, recv_sems = refs[:n], refs[n:2 * n], refs[2 * n], refs[2 * n + 1]
        x, y, c = _place()
        cps = []
        for i in range(n):
            rows = gs[i].shape[1] // 2
            cps.append(pltpu.make_async_remote_copy(src_ref=ins[i].at[:, pl.ds((1 - c) * rows, rows), :], dst_ref=outs[i],
                                                    send_sem=send_sems.at[i], recv_sem=recv_sems.at[i], device_id=(x, y, 1 - c),
                                                    device_id_type=MESH))
        for cp in cps:
            cp.start()
        for cp in cps:
            cp.wait()

    return pl.pallas_call(
        body, in_specs=[ANY] * n, out_specs=[ANY] * n,
        out_shape=[jax.ShapeDtypeStruct((g.shape[0], g.shape[1] // 2, g.shape[2]), g.dtype) for g in gs],
        scratch_shapes=[pltpu.SemaphoreType.DMA((n,)), pltpu.SemaphoreType.DMA((n,))],
        name=name,
    )(*gs)


def _share_halves(rs, name):
    n = len(rs)

    def body(*refs):
        ins, outs, send_sems, recv_sems = refs[:n], refs[n:2 * n], refs[2 * n], refs[2 * n + 1]
        x, y, c = _place()
        cps = [pltpu.make_async_remote_copy(src_ref=ins[i], dst_ref=outs[i], send_sem=send_sems.at[i], recv_sem=recv_sems.at[i],
                                            device_id=(x, y, 1 - c), device_id_type=MESH) for i in range(n)]
        for cp in cps:
            cp.start()
        for cp in cps:
            cp.wait()

    theirs = pl.pallas_call(
        body, in_specs=[ANY] * n, out_specs=[ANY] * n, out_shape=[jax.ShapeDtypeStruct(r.shape, r.dtype) for r in rs],
        scratch_shapes=[pltpu.SemaphoreType.DMA((n,)), pltpu.SemaphoreType.DMA((n,))],
        name=name,
    )(*rs)
    first = lax.axis_index("c") == 0
    return [jnp.where(first, jnp.concatenate([r, t], axis=0), jnp.concatenate([t, r], axis=0)) for r, t in zip(rs, theirs)]


def _chip_scatter(ss, name):
    n = len(ss)

    def body(*refs):
        ins, outs, send_sems, recv_sems = refs[:n], refs[n:2 * n], refs[2 * n], refs[2 * n + 1]
        x, y, c = _place()
        me = 2 * x + y
        chips = _other_chips(x, y)
        for i in range(n):
            for k, (px, py) in enumerate(chips):
                pltpu.make_async_remote_copy(src_ref=ins[i].at[2 * px + py], dst_ref=outs[i].at[me], send_sem=send_sems.at[3 * i + k],
                                             recv_sem=recv_sems.at[3 * i + k], device_id=(px, py, c), device_id_type=MESH).start()
        for i in range(n):
            for k, (px, py) in enumerate(chips):
                cp = pltpu.make_async_remote_copy(src_ref=ins[i].at[2 * px + py], dst_ref=outs[i].at[2 * px + py],
                                                  send_sem=send_sems.at[3 * i + k], recv_sem=recv_sems.at[3 * i + k],
                                                  device_id=(px, py, c), device_id_type=MESH)
                cp.wait_recv()
                cp.wait_send()

    parts = pl.pallas_call(
        body, in_specs=[ANY] * n, out_specs=[ANY] * n, out_shape=[jax.ShapeDtypeStruct(s.shape, s.dtype) for s in ss],
        scratch_shapes=[pltpu.SemaphoreType.DMA((3 * n,)), pltpu.SemaphoreType.DMA((3 * n,))],
        name=name,
    )(*ss)
    me = 2 * lax.axis_index("x") + lax.axis_index("y")
    return [_with_own_slot(p, lax.dynamic_index_in_dim(s, me, axis=0, keepdims=False)) for p, s in zip(parts, ss)]


HBM_SPACE = pltpu.MemorySpace.HBM


def _on_sequencer(name, collective_id, n_sems, body):
    @pl.kernel(mesh=plsc.ScalarSubcoreMesh(axis_name="sequencer", num_cores=1), name=name,
               scratch_types=(pltpu.SemaphoreType.DMA((n_sems,)), pltpu.SemaphoreType.DMA((n_sems,))),
               compiler_params=pltpu.CompilerParams(collective_id=collective_id))
    def launch(send_sems, recv_sems):
        body(send_sems, recv_sems)

    launch()


def _handshake(peers):
    barrier = pltpu.get_barrier_semaphore()
    for peer in peers:
        pl.semaphore_signal(barrier, inc=1, device_id=peer, device_id_type=MESH)
    pl.semaphore_wait(barrier, len(peers))


def _gather_on_sequencer(packs, name):
    n = len(packs)
    ins = [jax.new_ref(p, memory_space=HBM_SPACE) for p in packs]
    outs = [jax.empty_ref(jax.ShapeDtypeStruct((N_CHIPS,) + p.shape, p.dtype), memory_space=HBM_SPACE) for p in packs]

    def body(send_sems, recv_sems):
        x, y, c = _place()
        me, sib = 2 * x + y, (x, y, 1 - c)
        chips = _other_chips(x, y)
        _handshake([(px, py, c) for px, py in chips] + [sib])

        def copy(k, src, dst, to):
            return pltpu.make_async_remote_copy(src_ref=src, dst_ref=dst, send_sem=send_sems.at[k], recv_sem=recv_sems.at[k], device_id=to,
                                                device_id_type=MESH)

        def half(i, chip, h):
            rows = packs[i].shape[0] // 2
            return outs[i].at[chip, pl.ds(h * rows, rows), :]

        def mine(i):
            rows = packs[i].shape[0] // 2
            return ins[i].at[pl.ds(c * rows, rows), :]

        sent = [copy(6 * i + k, mine(i), half(i, me, c), (px, py, c)) for i in range(n) for k, (px, py) in enumerate(chips)]
        for cp in sent:
            cp.start()
        for i in range(n):
            for k, (px, py) in enumerate(chips):
                landed = half(i, 2 * px + py, c)
                copy(6 * i + k, mine(i), landed, (px, py, c)).wait_recv()
                fwd = copy(6 * i + 3 + k, landed, landed, sib)
                fwd.start()
                sent.append(fwd)
        for i in range(n):
            for k, (px, py) in enumerate(chips):
                other = half(i, 2 * px + py, 1 - c)
                copy(6 * i + 3 + k, other, other, sib).wait_recv()
        for cp in sent:
            cp.wait_send()

    _on_sequencer(name, 1, 6 * n, body)
    return [_with_own_slot(o[...], p) for o, p in zip(outs, packs)]


def _grads_to_sibling_on_sequencer(gs, name, collective_id):
    n = len(gs)
    ins = [jax.new_ref(g, memory_space=HBM_SPACE) for g in gs]
    outs = [jax.empty_ref(jax.ShapeDtypeStruct((g.shape[0], g.shape[1] // 2, g.shape[2]), g.dtype), memory_space=HBM_SPACE) for g in gs]

    def body(send_sems, recv_sems):
        x, y, c = _place()
        _handshake([(x, y, 1 - c)])
        cps = []
        for i in range(n):
            rows = gs[i].shape[1] // 2
            cps.append(pltpu.make_async_remote_copy(src_ref=ins[i].at[:, pl.ds((1 - c) * rows, rows), :], dst_ref=outs[i],
                                                    send_sem=send_sems.at[i], recv_sem=recv_sems.at[i], device_id=(x, y, 1 - c),
                                                    device_id_type=MESH))
        for cp in cps:
            cp.start()
        for cp in cps:
            cp.wait()

    _on_sequencer(name, collective_id, n, body)
    return [o[...] for o in outs]


def _chip_scatter_on_sequencer(ss, name, collective_id):
    n = len(ss)
    ins = [jax.new_ref(s, memory_space=HBM_SPACE) for s in ss]
    outs = [jax.empty_ref(jax.ShapeDtypeStruct(s.shape, s.dtype), memory_space=HBM_SPACE) for s in ss]

    def body(send_sems, recv_sems):
        x, y, c = _place()
        me = 2 * x + y
        chips = _other_chips(x, y)
        _handshake([(px, py, c) for px, py in chips])
        for i in range(n):
            for k, (px, py) in enumerate(chips):
                pltpu.make_async_remote_copy(src_ref=ins[i].at[2 * px + py], dst_ref=outs[i].at[me], send_sem=send_sems.at[3 * i + k],
                                             recv_sem=recv_sems.at[3 * i + k], device_id=(px, py, c), device_id_type=MESH).start()
        for i in range(n):
            for k, (px, py) in enumerate(chips):
                cp = pltpu.make_async_remote_copy(src_ref=ins[i].at[2 * px + py], dst_ref=outs[i].at[2 * px + py],
                                                  send_sem=send_sems.at[3 * i + k], recv_sem=recv_sems.at[3 * i + k],
                                                  device_id=(px, py, c), device_id_type=MESH)
                cp.wait_recv()
                cp.wait_send()

    _on_sequencer(name, collective_id, 3 * n, body)
    me = 2 * lax.axis_index("x") + lax.axis_index("y")
    return [_with_own_slot(o[...], lax.dynamic_index_in_dim(s, me, axis=0, keepdims=False)) for o, s in zip(outs, ss)]


def _all_reduce_small(v, name):
    shape = v.shape

    def body(v_ref, o_ref, slots, send_sems, recv_sems):
        x, y, c = _place()
        me = 4 * x + 2 * y + c
        slots[me] = v_ref[...]
        for r in range(1, N_DEV):
            peer = (x ^ (r >> 2), y ^ ((r >> 1) & 1), c ^ (r & 1))
            pltpu.make_async_remote_copy(src_ref=v_ref, dst_ref=slots.at[me], send_sem=send_sems.at[r - 1], recv_sem=recv_sems.at[r - 1],
                                         device_id=peer, device_id_type=MESH).start()
        for r in range(1, N_DEV):
            peer = (x ^ (r >> 2), y ^ ((r >> 1) & 1), c ^ (r & 1))
            cp = pltpu.make_async_remote_copy(src_ref=v_ref, dst_ref=slots.at[4 * peer[0] + 2 * peer[1] + peer[2]], send_sem=send_sems.at[r - 1],
                                              recv_sem=recv_sems.at[r - 1], device_id=peer, device_id_type=MESH)
            cp.wait_recv()
            cp.wait_send()
        acc = slots[0]
        for d in range(1, N_DEV):
            acc = acc + slots[d]
        o_ref[...] = acc

    vm = pl.BlockSpec(memory_space=pltpu.VMEM)
    return pl.pallas_call(
        body, in_specs=[vm], out_specs=vm, out_shape=jax.ShapeDtypeStruct(shape, F32),
        scratch_shapes=[pltpu.VMEM((N_DEV,) + shape, F32), pltpu.SemaphoreType.DMA((N_DEV - 1,)), pltpu.SemaphoreType.DMA((N_DEV - 1,))],
        name=name,
    )(v)


def _add_sibling(g, gsib, core, name):
    n, _, cdim = g.shape
    half = gsib.shape[1]
    tr = half // 2

    def body(core_ref, a_ref, b_ref, o_ref):
        o_ref[...] = (a_ref[...] + b_ref[...]).astype(o_ref.dtype)

    blk = pl.BlockSpec((1, tr, cdim), lambda j, i, core_ref: (j, i, 0))
    return pl.pallas_call(
        body,
        grid_spec=pltpu.PrefetchScalarGridSpec(
            num_scalar_prefetch=1, grid=(n, half // tr),
            in_specs=[pl.BlockSpec((1, tr, cdim), lambda j, i, core_ref: (j, core_ref[0] * (half // tr) + i, 0)), blk], out_specs=blk),
        out_shape=jax.ShapeDtypeStruct(gsib.shape, BF16), compiler_params=_params(("parallel", "parallel")), name=name,
    )(core, g, gsib)


def _sum_parts(parts, name):
    n, r, cdim = parts.shape
    tr = r // 2

    def body(p_ref, o_ref):
        acc = p_ref[0].astype(F32)
        for j in range(1, n):
            acc = acc + p_ref[j].astype(F32)
        o_ref[...] = acc

    return pl.pallas_call(
        body, grid=(r // tr,), in_specs=[pl.BlockSpec((n, tr, cdim), lambda i: (0, i, 0))],
        out_specs=pl.BlockSpec((tr, cdim), lambda i: (i, 0)), out_shape=jax.ShapeDtypeStruct((r, cdim), F32),
        compiler_params=_params(("parallel",)), name=name,
    )(parts)


def _adamw(parts, w, m, v, name):
    npart, r, cdim = parts.shape
    tr = r // 4 if r % 32 == 0 else r

    def body(p_ref, w_ref, m_ref, v_ref, g_ref, d_ref, nm_ref, nv_ref):
        g = p_ref[0]
        for j in range(1, npart):
            g = g + p_ref[j]
        m_new = ADAM_B1 * m_ref[...] + (1.0 - ADAM_B1) * g
        v_new = ADAM_B2 * v_ref[...] + (1.0 - ADAM_B2) * (g * g)
        m_hat = m_new / (1.0 - ADAM_B1 ** ADAM_STEP)
        v_hat = v_new / (1.0 - ADAM_B2 ** ADAM_STEP)
        g_ref[...] = g
        d_ref[...] = -ADAM_LR * (m_hat / (jnp.sqrt(v_hat) + ADAM_EPS) + ADAM_WD * w_ref[...])
        nm_ref[...] = m_new
        nv_ref[...] = v_new

    blk = pl.BlockSpec((tr, cdim), lambda i: (i, 0))
    shp = jax.ShapeDtypeStruct((r, cdim), F32)
    return pl.pallas_call(
        body, grid=(r // tr,), in_specs=[pl.BlockSpec((npart, tr, cdim), lambda i: (0, i, 0)), blk, blk, blk], out_specs=[blk] * 4,
        out_shape=[shp] * 4, compiler_params=_params(("parallel",)), name=name,
    )(parts, w, m, v)


BIG = ("ev_w_in", "ev_w_uq", "ev_w_ukv", "ev_w_out", "od_w_in", "od_w_out", "meta")
SMALL = ("norm_g", "final_g", "ev_q_norm_g", "ev_kv_norm_g", "od_sinks")
SMALL_SHAPE = (8, 512)
BY_ROWS = ("ev_w_out", "od_w_out")

EV_IN_SHARD, OD_IN_SHARD, UQ_SHARD = 2976 // N_CHIPS, 2304 // N_CHIPS, 768 // N_CHIPS


def _pack_big(a, lead=()):
    return _pack_first(a, lead) + _pack_later(a, lead)


def _pad_lanes(x, width):
    return jnp.concatenate([x, jnp.zeros(x.shape[:-1] + (width - x.shape[-1],), x.dtype)], axis=-1)


def _pack_latent(a, lead=()):
    ax = len(lead)
    corner = jnp.concatenate([a["ev_w_ukv"], a["meta"], jnp.zeros(lead + (256 - MLA_KV_LORA - N_META, 256), a["meta"].dtype)], axis=ax)
    return jnp.concatenate([_pad_lanes(a["ev_w_uq"], 256), corner], axis=ax + 1)


def _pack_first(a, lead=()):
    return _pad_lanes(a["ev_w_in"], 768), _pack_latent(a, lead)


def _pack_later(a, lead=()):
    return _pad_lanes(a["od_w_in"], 640), jnp.concatenate([a["ev_w_out"], a["od_w_out"]], axis=len(lead) + 1)


N_FIRST = 2
LATER = ("od_w_in", "ev_w_out", "od_w_out")


def _unpack_big(p_in0, p_lat, p_in1, p_out):
    return dict(ev_w_in=p_in0[..., :EV_IN_SHARD], od_w_in=p_in1[..., :OD_IN_SHARD], ev_w_out=p_out[..., :D_MODEL],
                od_w_out=p_out[..., D_MODEL:], ev_w_uq=p_lat[..., :UQ_SHARD], ev_w_ukv=p_lat[..., :MLA_KV_LORA, 256:],
                meta=p_lat[..., MLA_KV_LORA:MLA_KV_LORA + N_META, 256:])


def _chip_shards(full, by_rows):
    if by_rows:
        return full.reshape(N_CHIPS, full.shape[0] // N_CHIPS, full.shape[1])
    cols = full.shape[1] // N_CHIPS
    return jnp.stack([full[:, j * cols:(j + 1) * cols] for j in range(N_CHIPS)])


def _from_chip_shards(slots, by_rows):
    if by_rows:
        return slots.reshape(-1, slots.shape[2])
    return jnp.concatenate([slots[j] for j in range(N_CHIPS)], axis=1)


def _pack_small(arrs, extra=None):
    flat = [a.reshape(-1) for a in arrs] + ([] if extra is None else [extra.reshape(-1)])
    used = sum(f.shape[0] for f in flat)
    return jnp.pad(jnp.concatenate(flat), (0, SMALL_SHAPE[0] * SMALL_SHAPE[1] - used)).reshape(SMALL_SHAPE)


def _unpack_small(p, shapes):
    flat, out, at = p.reshape(-1), [], 0
    for s in shapes:
        n = int(np.prod(s))
        out.append(flat[at:at + n].reshape(s))
        at += n
    return out, flat[at]


def kernel(x, meta, norm_g, final_g, ev_w_in, ev_q_norm_g, ev_kv_norm_g, ev_w_uq, ev_w_ukv, ev_w_out, od_w_in, od_sinks, od_w_out, loss_target, m_meta, m_norm_g, m_final_g, m_ev_w_in, m_ev_q_norm_g, m_ev_kv_norm_g, m_ev_w_uq, m_ev_w_ukv, m_ev_w_out, m_od_w_in, m_od_sinks, m_od_w_out, v_meta, v_norm_g, v_final_g, v_ev_w_in, v_ev_q_norm_g, v_ev_kv_norm_g, v_ev_w_uq, v_ev_w_ukv, v_ev_w_out, v_od_w_in, v_od_sinks, v_od_w_out):
    given = dict(locals())
    two_d = lambda a: a[0] if a.ndim == 3 else a
    packs = {k: _pack_big({n: two_d(given[k + n]) for n in BIG}) for k in ("", "m_", "v_")}

    wbf = [p.astype(BF16) for p in packs[""]]
    later = _gather_on_sequencer(wbf[N_FIRST:], "gather_later_weights")
    first, meta_all = _gather_weights(wbf[:N_FIRST], meta, "gather_weights")
    full = {n: _from_chip_shards(a, n in BY_ROWS) for n, a in _unpack_big(*first, *later).items()}
    meta_full = _from_chip_shards(meta_all, False)

    core = lax.axis_index("c").astype(jnp.int32).reshape(1)
    shards = lambda g: {n: _chip_shards(a, n in BY_ROWS) for n, a in g.items()}

    def sums_of(group, tag, to_sibling):
        return [_add_sibling(g, s, core, f"add_sibling_{tag}{i}") for i, (g, s) in enumerate(zip(group, to_sibling(group)))]

    def update(sums, tag, first_pack, scatter):
        parts = scatter(sums, "grads_to_chips_" + tag)
        reduced = _share_halves([_sum_parts(p, f"add_chips_{tag}{i}") for i, p in enumerate(parts)], "reduced_to_sibling_" + tag)
        return [_adamw(r[None], packs[""][first_pack + i], packs["m_"][first_pack + i], packs["v_"][first_pack + i],
                       f"adamw_matrices_{first_pack + i}") for i, r in enumerate(reduced)]

    def behind(tag, first_pack, ids):
        def start(packed):
            sums = sums_of(packed, tag, lambda p: _grads_to_sibling_on_sequencer(p, "grads_to_sibling_" + tag, ids[0]))
            return sums, lambda s: update(s, tag, first_pack, lambda t, name: _chip_scatter_on_sequencer(t, name, ids[1]))
        return start

    loss, grad_x, grads, updated_in_proj, updated_later = _local_step(
        x, meta_full, norm_g, final_g, ev_q_norm_g, ev_kv_norm_g, od_sinks, loss_target, full["ev_w_in"], full["ev_w_uq"],
        full["ev_w_ukv"], full["ev_w_out"], full["od_w_in"], full["od_w_out"],
        reduce_early=lambda g: behind("later", N_FIRST, (3, 2))(_pack_later(shards(g), (N_CHIPS,))),
        reduce_in_proj=lambda g: behind("in_proj", 0, (5, 4))((_pad_lanes(_chip_shards(g, False), 768),)))

    latent = (_pack_latent(shards({n: grads[n] for n in ("ev_w_uq", "ev_w_ukv", "meta")}), (N_CHIPS,)),)
    updated_latent = update(sums_of(latent, "latent", lambda p: _grads_to_sibling(p, "grads_to_sibling_latent")), "latent", 1, _chip_scatter)
    updated = updated_in_proj + updated_latent + updated_later
    big_out = [{n: a.reshape(given[n].shape) for n, a in _unpack_big(*outs).items()} for outs in zip(*updated)]

    small_shapes = [given[n].shape for n in SMALL]
    ssum = _all_reduce_small(_pack_small([grads[n] for n in SMALL], loss[0, 0]), "reduce_vectors")
    small_out = _adamw(ssum[None], _pack_small([given[n] for n in SMALL]), _pack_small([given["m_" + n] for n in SMALL]),
                       _pack_small([given["v_" + n] for n in SMALL]), "adamw_vectors")
    total_loss = ssum.reshape(-1)[sum(int(np.prod(s)) for s in small_shapes)]
    small_out = [_unpack_small(o, small_shapes)[0] for o in small_out]

    names = ("meta", "norm_g", "final_g", "ev_w_in", "ev_q_norm_g", "ev_kv_norm_g", "ev_w_uq", "ev_w_ukv", "ev_w_out", "od_w_in", "od_sinks",
             "od_w_out")
    outs = [total_loss, grad_x]
    for kind in range(4):
        for n in names:
            outs.append(big_out[kind][n] if n in BIG else small_out[kind][SMALL.index(n)])
    return tuple(outs)
```

```python
import functools
import math

import numpy as np
import jax
import jax.numpy as jnp
from jax import lax
from jax.experimental import pallas as pl
from jax.experimental.pallas import tpu as pltpu
from jax.experimental.pallas import tpu_sc as plsc

F32 = jnp.float32
BF16 = jnp.bfloat16

D_MODEL = 1024
BLOCK = 128
N_META = 16
N_PAD = BLOCK - N_META
NORM_EPS = 1e-6
NEG = -1e30
HEAD = 64
SB_HEADS = 8
MLA_HEADS = 8
MLA_Q_LORA = 256
MLA_KV_LORA = 128
MLA_NOPE = 64
MLA_ROPE = 32
ROPE_BASE = 10000.0
SWA_HEADS = 16
SWA_KV_HEADS = 2
SWA_WINDOW = 128
N_CHIPS = 4
N_DEV = 8

ADAM_LR = 0.001
ADAM_B1 = 0.9
ADAM_B2 = 0.999
ADAM_EPS = 1e-08
ADAM_WD = 0.01
ADAM_STEP = 10

VMEM_LIMIT = 48 * 1024 * 1024

EV_G, EV_Q, EV_K, EV_V, EV_CQ, EV_CKV, EV_KR, EV_N = 0, 1024, 1536, 2048, 2560, 2816, 2944, 3072
OD_G, OD_Q, OD_K, OD_V, OD_N = 0, 1024, 2048, 2176, 2304


def _params(sem=None):
    return pltpu.CompilerParams(dimension_semantics=sem, vmem_limit_bytes=VMEM_LIMIT)


def _row_tile(m):
    for c in (544, 256):
        if m % c == 0:
            return c
    return 128


def _matmul_rows(m):
    for c in (1088, 1024, 768, 640, 512, 384, 256):
        if m % c == 0:
            return c
    return 128


def _dot(a, b):
    return jnp.dot(a.astype(BF16), b.astype(BF16), preferred_element_type=F32)


def _dot_nt(a, b):
    return lax.dot_general(a.astype(BF16), b.astype(BF16), (((1,), (1,)), ((), ())), preferred_element_type=F32)


def _dot_tn(a, b):
    return lax.dot_general(a.astype(BF16), b.astype(BF16), (((0,), (0,)), ((), ())), preferred_element_type=F32)


def _rms_fwd(h, g, name):
    t, d = h.shape
    tm = _row_tile(t)

    def body(h_ref, g_ref, o_ref):
        x = h_ref[...]
        r = lax.rsqrt(jnp.mean(x * x, axis=-1, keepdims=True) + NORM_EPS)
        o_ref[...] = ((x * r) * g_ref[...]).astype(o_ref.dtype)

    return pl.pallas_call(
        body, grid=(t // tm,),
        in_specs=[pl.BlockSpec((tm, d), lambda i: (i, 0)), pl.BlockSpec((1, d), lambda i: (0, 0))],
        out_specs=pl.BlockSpec((tm, d), lambda i: (i, 0)),
        out_shape=jax.ShapeDtypeStruct((t, d), BF16), compiler_params=_params(("parallel",)), name=name,
    )(h, g)


def _rms_bwd(h, g, dy, dres, name):
    t, d = h.shape
    tm = _row_tile(t)

    def body(h_ref, g_ref, dy_ref, dres_ref, dh_ref, dg_ref):
        @pl.when(pl.program_id(0) == 0)
        def _():
            dg_ref[...] = jnp.zeros_like(dg_ref)

        x = h_ref[...]
        r = lax.rsqrt(jnp.mean(x * x, axis=-1, keepdims=True) + NORM_EPS)
        xr = x * r
        dy_ = dy_ref[...]
        u = dy_ * g_ref[...]
        dh_ref[...] = dres_ref[...] + r * (u - xr * jnp.mean(u * xr, axis=-1, keepdims=True))
        dg_ref[...] += jnp.sum(dy_ * xr, axis=0, keepdims=True)

    row = pl.BlockSpec((tm, d), lambda i: (i, 0))
    vec = pl.BlockSpec((1, d), lambda i: (0, 0))
    return pl.pallas_call(
        body, grid=(t // tm,), in_specs=[row, vec, row, row], out_specs=[row, vec],
        out_shape=[jax.ShapeDtypeStruct((t, d), F32), jax.ShapeDtypeStruct((1, d), F32)],
        compiler_params=_params(("arbitrary",)), name=name,
    )(h, g, dy, dres)


def _col_tile(n):
    for c in (1024, 768, 640, 512, 384, 256, 128):
        if n % c == 0:
            return c
    return n


def _mm(a, w, name, res=None, out_dtype=F32, a_cols=None):
    m = a.shape[0]
    k, n = w.shape
    a_blk = 0 if a_cols is None else a_cols[0] // k
    assert a_cols is None or (a_cols[1] == k and a_cols[0] % k == 0)
    tm, tn = _matmul_rows(m), _col_tile(n)

    def body(*refs):
        if res is None:
            a_ref, w_ref, o_ref = refs
            acc = _dot(a_ref[...], w_ref[...])
        else:
            a_ref, w_ref, r_ref, o_ref = refs
            acc = r_ref[...] + _dot(a_ref[...], w_ref[...])
        o_ref[...] = acc.astype(o_ref.dtype)

    in_specs = [pl.BlockSpec((tm, k), lambda j, i: (i, a_blk)), pl.BlockSpec((k, tn), lambda j, i: (0, j))]
    args = [a, w]
    if res is not None:
        in_specs.append(pl.BlockSpec((tm, tn), lambda j, i: (i, j)))
        args.append(res)
    return pl.pallas_call(
        body, grid=(n // tn, m // tm), in_specs=in_specs, out_specs=pl.BlockSpec((tm, tn), lambda j, i: (i, j)),
        out_shape=jax.ShapeDtypeStruct((m, n), out_dtype), compiler_params=_params(("parallel", "parallel")), name=name,
    )(*args)


def _mm_nt(a, w, name):
    m, n = a.shape
    k = w.shape[0]
    tm, tk = _matmul_rows(m), _col_tile(k)

    def body(a_ref, w_ref, o_ref):
        o_ref[...] = _dot_nt(a_ref[...], w_ref[...])

    return pl.pallas_call(
        body, grid=(k // tk, m // tm),
        in_specs=[pl.BlockSpec((tm, n), lambda j, i: (i, 0)), pl.BlockSpec((tk, n), lambda j, i: (j, 0))],
        out_specs=pl.BlockSpec((tm, tk), lambda j, i: (i, j)),
        out_shape=jax.ShapeDtypeStruct((m, k), F32), compiler_params=_params(("parallel", "parallel")), name=name,
    )(a, w)


def _mm_tn(x, dy, name):
    m, k = x.shape
    n = dy.shape[1]
    tm, tn = _matmul_rows(m), _col_tile(n)

    def body(x_ref, dy_ref, o_ref):
        @pl.when(pl.program_id(1) == 0)
        def _():
            o_ref[...] = jnp.zeros_like(o_ref)

        o_ref[...] += _dot_tn(x_ref[...], dy_ref[...])

    return pl.pallas_call(
        body, grid=(n // tn, m // tm),
        in_specs=[pl.BlockSpec((tm, k), lambda j, i: (i, 0)), pl.BlockSpec((tm, tn), lambda j, i: (i, j))],
        out_specs=pl.BlockSpec((k, tn), lambda j, i: (0, j)),
        out_shape=jax.ShapeDtypeStruct((k, n), F32), compiler_params=_params(("parallel", "arbitrary")), name=name,
    )(x, dy)


def _silu_parts(g):
    s = 1.0 / (1.0 + jnp.exp(-g))
    return g * s, s * (1.0 + g * (1.0 - s))


def _gate_fwd(o_parts, proj, name):
    t = proj.shape[0]
    tm = _row_tile(t)
    w = D_MODEL // len(o_parts)

    def body(*refs):
        g_ref, o_ref = refs[-2], refs[-1]
        for p, r in enumerate(refs[:-2]):
            sil, _ = _silu_parts(g_ref[:, p * w:(p + 1) * w])
            o_ref[:, p * w:(p + 1) * w] = (r[...].astype(F32) * sil).astype(o_ref.dtype)

    return pl.pallas_call(
        body, grid=(t // tm,),
        in_specs=[pl.BlockSpec((tm, w), lambda i: (i, 0)) for _ in o_parts] + [pl.BlockSpec((tm, D_MODEL), lambda i: (i, 0))],
        out_specs=pl.BlockSpec((tm, D_MODEL), lambda i: (i, 0)),
        out_shape=jax.ShapeDtypeStruct((t, D_MODEL), BF16), compiler_params=_params(("parallel",)), name=name,
    )(*o_parts, proj)


def _gate_bwd(dao, o_parts, proj, name):
    t = proj.shape[0]
    tm = _row_tile(t)
    np_ = len(o_parts)
    w = D_MODEL // np_

    def body(*refs):
        dao_ref, g_ref = refs[0], refs[1 + np_]
        do_refs, dg_ref = refs[2 + np_:2 + 2 * np_], refs[-1]
        for p in range(np_):
            sl = slice(p * w, (p + 1) * w)
            sil, dsil = _silu_parts(g_ref[:, sl])
            da = dao_ref[:, sl]
            do_refs[p][...] = da * sil
            dg_ref[:, sl] = (da * refs[1 + p][...].astype(F32) * dsil).astype(dg_ref.dtype)

    full = pl.BlockSpec((tm, D_MODEL), lambda i: (i, 0))
    part = pl.BlockSpec((tm, w), lambda i: (i, 0))
    outs = pl.pallas_call(
        body, grid=(t // tm,), in_specs=[full] + [part] * np_ + [full], out_specs=[part] * np_ + [full],
        out_shape=[jax.ShapeDtypeStruct((t, w), F32)] * np_ + [jax.ShapeDtypeStruct((t, D_MODEL), BF16)],
        compiler_params=_params(("parallel",)), name=name,
    )(dao, *o_parts, proj)
    return outs[:np_], outs[np_]


def _loss_head(h2, gf, target, b, lp):
    d = h2.shape[1]
    nb = lp // BLOCK
    h3 = h2.reshape(b, lp, d)

    def body(h_ref, g_ref, t_ref, dh_ref, dg_ref, loss_ref):
        first = (pl.program_id(0) == 0) & (pl.program_id(1) == 0)

        @pl.when(first)
        def _():
            dg_ref[...] = jnp.zeros_like(dg_ref)
            loss_ref[...] = jnp.zeros_like(loss_ref)

        @pl.when(pl.program_id(1) == 0)
        def _():
            dh_ref[...] = jnp.zeros_like(dh_ref)

        @pl.when(pl.program_id(1) > 0)
        def _():
            x = h_ref[0]
            r = lax.rsqrt(jnp.mean(x * x, axis=-1, keepdims=True) + NORM_EPS)
            xr = x * r
            g = g_ref[...]
            diff = xr * g - t_ref[0]
            loss_ref[...] += 0.5 * jnp.sum(jnp.mean(diff * diff, axis=-1, keepdims=True))
            dy = diff * (1.0 / d)
            u = dy * g
            dh_ref[0] = r * (u - xr * jnp.mean(u * xr, axis=-1, keepdims=True))
            dg_ref[...] += jnp.sum(dy * xr, axis=0, keepdims=True)

    blk = pl.BlockSpec((1, BLOCK, d), lambda bi, n: (bi, n, 0))
    dh, dg, loss = pl.pallas_call(
        body, grid=(b, nb),
        in_specs=[blk, pl.BlockSpec((1, d), lambda bi, n: (0, 0)),
                  pl.BlockSpec((1, BLOCK, d), lambda bi, n: (bi, jnp.maximum(n - 1, 0), 0))],
        out_specs=[blk, pl.BlockSpec((1, d), lambda bi, n: (0, 0)), pl.BlockSpec((8, 128), lambda bi, n: (0, 0))],
        out_shape=[jax.ShapeDtypeStruct((b, lp, d), F32), jax.ShapeDtypeStruct((1, d), F32), jax.ShapeDtypeStruct((8, 128), F32)],
        compiler_params=_params(("arbitrary", "arbitrary")), name="loss_head",
    )(h3, gf, target)
    return dh.reshape(b * lp, d), dg, loss


def _iota2(shape, dim):
    return lax.broadcasted_iota(jnp.int32, shape, dim)


KEYS = 512
SB_FWD_KEYS = 512


def _lo_lanes():
    return _iota2((1, BLOCK), 1) < HEAD


def _halves(x, lo):
    zero = jnp.zeros_like(x)
    return jnp.where(lo, x, zero), jnp.where(lo, zero, x)


def _rows_of_pair(a, b):
    return jnp.where(_iota2((BLOCK, 1), 0) < HEAD, a, b)


def _split_rows_t(x):
    xt = x.T
    first = _iota2(xt.shape, 0) < HEAD
    zero = jnp.zeros_like(xt)
    return jnp.concatenate([jnp.where(first, xt, zero), jnp.where(first, zero, xt)], axis=1).astype(BF16)


def _key_chunk(c, lp, t_idx, strict, key_axis):
    keys = t_idx.shape[key_axis]
    first = c * keys
    s0 = pl.multiple_of(jnp.minimum(first, lp - keys), BLOCK)
    s_idx = s0 + _iota2(t_idx.shape, key_axis)
    seen = (s_idx < t_idx) if strict else (s_idx <= t_idx)
    return s0, seen & (s_idx >= jnp.maximum(first, N_PAD))


def _tri_dot(x, tri):
    return jnp.dot(x.astype(BF16), tri, preferred_element_type=F32)


def _stack_halves(x, lo):
    a, b = _halves(x, lo)
    return jnp.concatenate([a, b], axis=0)


def _pair(a, b, lo):
    return jnp.where(lo, a, b)


def _chunk_starts(lp):
    return [min(c * KEYS, lp - KEYS) for c in range(-(-lp // KEYS))]


def _put_rows(ref, r0, bq, a, b):
    for t in range(bq // BLOCK):
        part = slice(t * BLOCK, (t + 1) * BLOCK)
        ref[0, 0, r0 // BLOCK + t] = jnp.concatenate([a[:, part], b[:, part], jnp.zeros((6, BLOCK), F32)], axis=0)


def _get_rows(ref, r0, bq):
    return [jnp.concatenate([ref[0, 0, r0 // BLOCK + t, h:h + 1, :] for t in range(bq // BLOCK)], axis=1) for h in range(2)]


def _n_chunks(i, keys):
    return ((i + 1) * BLOCK + keys - 1) // keys


QROWS = 512


def _for_query_tiles(nb, tile, keys=KEYS):
    per = QROWS // BLOCK

    def step(j, _):
        tile(pl.multiple_of(j * QROWS, QROWS), QROWS, (j + 1) * (QROWS // keys))
        return 0

    lax.fori_loop(0, nb // per, step, 0)
    for i in range(nb - nb % per, nb):
        tile(i * BLOCK, BLOCK, _n_chunks(i, keys))


def _walk_chunks(r0, n, chunk, carry, leftwards=False, keys=KEYS):
    diag = jnp.maximum(r0 // keys, 1)

    def span(first, last, masked, carry):
        def step(t, cr):
            return chunk(last - 1 - t if leftwards else first + t, cr, masked)
        return lax.fori_loop(0, last - first, step, carry)

    spans = [(0, 1, True), (1, diag, False), (diag, n, True)]
    for first, last, masked in (reversed(spans) if leftwards else spans):
        carry = span(first, last, masked, carry)
    return carry


def _where(valid, x, other):
    return x if valid is None else jnp.where(valid, x, other)


HEAD_SCALE = HEAD ** -0.5
assert math.frexp(HEAD_SCALE)[0] == 0.5


def _sb_scores(q_h, k, valid, after):
    z = _dot_nt(q_h, k)
    lb = jnp.minimum(z, 0.0) - jnp.log(1.0 + jnp.exp(-jnp.abs(z)))
    l1m_all = lb - z
    l1m = _where(valid, l1m_all, 0.0)
    return lb, l1m_all, l1m, _tri_dot(l1m, after)


def _pair_stat_spec(nb):
    return pl.BlockSpec((1, 1, nb, 8, BLOCK), lambda bi, hp: (bi, hp, 0, 0, 0))


def _sb_fwd(proj3, name):
    b, lp, _ = proj3.shape
    nb = lp // BLOCK
    npair = SB_HEADS // 2

    keys = SB_FWD_KEYS

    def body(q_ref, k_ref, v_ref, o_ref, tot_ref):
        lo = _lo_lanes()
        after = (_iota2((keys, keys), 0) > _iota2((keys, keys), 1)).astype(BF16)

        def qtile(r0, bq, n):
            qs = _halves((q_ref[0, pl.ds(r0, bq), :] * HEAD_SCALE).astype(BF16), lo)
            t_idx = r0 + _iota2((bq, keys), 0)

            def kchunk(c, carry, masked):
                cs, acc = carry[:2], carry[2]
                s0, valid = _key_chunk(c, lp, t_idx, True, 1)
                valid = valid if masked else None
                k = k_ref[0, pl.ds(s0, keys), :].astype(BF16)
                a_s, new = [], []
                for h in range(2):
                    lb, _, l1m, suf = _sb_scores(qs[h], k, valid, after)
                    a_s.append(_where(valid, jnp.exp(lb + suf + cs[h]), 0.0).astype(BF16))
                    new.append(cs[h] + jnp.sum(l1m, axis=1, keepdims=True))
                v_bd = _stack_halves(v_ref[0, pl.ds(s0, keys), :].astype(BF16), lo)
                return (*new, acc + jnp.dot(jnp.concatenate(a_s, axis=1), v_bd, preferred_element_type=F32))

            zero = jnp.zeros((bq, 1), F32)
            c_a, c_b, acc = _walk_chunks(r0, n, kchunk, (zero, zero, jnp.zeros((bq, BLOCK), F32)), leftwards=True, keys=keys)
            o_ref[0, pl.ds(r0, bq), :] = acc
            tot_ref[0, pl.ds(r0, bq), :] = jnp.broadcast_to(_pair(c_a, c_b, lo), (bq, BLOCK))

        _for_query_tiles(nb, qtile, keys)

    def col(first):
        return pl.BlockSpec((1, lp, 2 * HEAD), lambda bi, hp: (bi, 0, first // (2 * HEAD) + hp))

    shp = jax.ShapeDtypeStruct((b, lp, SB_HEADS * HEAD), F32)
    return pl.pallas_call(
        body, grid=(b, npair), in_specs=[col(EV_Q), col(EV_K), col(EV_V)], out_specs=[col(0), col(0)], out_shape=[shp, shp],
        compiler_params=_params(("parallel", "parallel")), name=name,
    )(proj3, proj3, proj3)


def _sb_bwd(proj3, tot, do, name):
    b, lp, _ = proj3.shape
    nb = lp // BLOCK
    npair = SB_HEADS // 2

    def body(q_ref, k_ref, v_ref, tot_ref, do_ref, dq_ref, dk_ref, dv_ref):
        lo = _lo_lanes()
        after = (_iota2((KEYS, KEYS), 0) > _iota2((KEYS, KEYS), 1)).astype(BF16)
        before = (_iota2((KEYS, KEYS), 0) < _iota2((KEYS, KEYS), 1)).astype(BF16)
        dk_ref[...] = jnp.zeros_like(dk_ref)
        dv_ref[...] = jnp.zeros_like(dv_ref)

        def qtile(r0, bq, n):
            rows = pl.ds(r0, bq)
            qs = _halves((q_ref[0, rows, :] * HEAD_SCALE).astype(BF16), lo)
            dos = _halves(do_ref[0, rows, :].astype(BF16), lo)
            tot_i = tot_ref[0, rows, :]
            tots = (tot_i[:, 0:1], tot_i[:, HEAD:HEAD + 1])
            q_st, do_st = jnp.concatenate(qs, axis=0), jnp.concatenate(dos, axis=0)
            t_idx = r0 + _iota2((bq, KEYS), 0)

            def kchunk(c, carry, masked):
                s0, valid = _key_chunk(c, lp, t_idx, True, 1)
                valid = valid if masked else None
                keys = pl.ds(s0, KEYS)
                k = k_ref[0, keys, :].astype(BF16)
                v = v_ref[0, keys, :].astype(BF16)
                a_s, dzs, new = [], [], []
                for h in range(2):
                    left, pre = carry[2 * h], carry[2 * h + 1]
                    lb, l1m_all, l1m, suf = _sb_scores(qs[h], k, valid, after)
                    here = jnp.sum(l1m, axis=1, keepdims=True)
                    a = _where(valid, jnp.exp(lb + suf + (tots[h] - left - here)), 0.0)
                    w = a * _dot_nt(dos[h], v)
                    dz = _where(valid, w * jnp.exp(l1m_all) - (pre + _tri_dot(w, before)) * jnp.exp(lb), 0.0)
                    new += [left + here, pre + jnp.sum(w, axis=1, keepdims=True)]
                    a_s.append(a.astype(BF16))
                    dzs.append(dz.astype(BF16))
                dk_ref[0, keys, :] += _dot_tn(jnp.concatenate(dzs, axis=0), q_st)
                dv_ref[0, keys, :] += _dot_tn(jnp.concatenate(a_s, axis=0), do_st)
                dq = carry[4] + jnp.dot(jnp.concatenate(dzs, axis=1), _stack_halves(k, lo), preferred_element_type=F32)
                return (*new, dq)

            zero = jnp.zeros((bq, 1), F32)
            out = _walk_chunks(r0, n, kchunk, (zero, zero, zero, zero, jnp.zeros((bq, BLOCK), F32)))
            dq_ref[0, rows, :] = out[4] * HEAD_SCALE

        _for_query_tiles(nb, qtile)

    def col(first):
        return pl.BlockSpec((1, lp, 2 * HEAD), lambda bi, hp: (bi, 0, first // (2 * HEAD) + hp))

    shp = jax.ShapeDtypeStruct((b, lp, SB_HEADS * HEAD), F32)
    return pl.pallas_call(
        body, grid=(b, npair), in_specs=[col(EV_Q), col(EV_K), col(EV_V), col(0), col(0)], out_specs=[col(0)] * 3, out_shape=[shp] * 3,
        compiler_params=_params(("parallel", "parallel")), name=name,
    )(proj3, proj3, proj3, tot, do)


def _rope_tables(lp):
    half = MLA_ROPE // 2
    pos = (np.arange(lp) - N_PAD).astype(np.float32)
    inv = jnp.asarray(ROPE_BASE, F32) ** (-jnp.arange(half, dtype=F32) / half)
    ang = jnp.asarray(pos)[:, None] * inv[None, :]
    cos, sin = jnp.cos(ang), jnp.sin(ang)
    zeros = lambda n: jnp.zeros((lp, n), F32)
    c = jnp.concatenate([jnp.ones((lp, MLA_NOPE), F32), cos, cos, zeros(32)], axis=1)
    s1 = jnp.concatenate([zeros(MLA_NOPE), -sin, zeros(half), zeros(32)], axis=1)
    s2 = jnp.concatenate([zeros(MLA_NOPE), zeros(half), sin, zeros(32)], axis=1)
    return c, s1, s2


def _rope(x, c, s1, s2):
    half = MLA_ROPE // 2
    return x * c + pltpu.roll(x, BLOCK - half, 1) * s1 + pltpu.roll(x, half, 1) * s2


def _rope_t(dy, c, s1, s2):
    half = MLA_ROPE // 2
    return dy * c + pltpu.roll(dy * s1, half, 1) + pltpu.roll(dy * s2, BLOCK - half, 1)


def _rms_rows(x, g):
    r = lax.rsqrt(jnp.mean(x * x, axis=-1, keepdims=True) + NORM_EPS)
    return x * r, r


def _prep_rows(lp):
    return lp // 4 if lp % 64 == 0 else BLOCK


def _mla_prep_fwd(proj3, gq, gkv, wq, wk, wv, tabs, name):
    b, lp, _ = proj3.shape
    rows = _prep_rows(lp)
    hw = MLA_HEADS * BLOCK

    def body(cq_ref, ckv_ref, kr_ref, gq_ref, gkv_ref, wq_ref, wk_ref, wv_ref, c_ref, s1_ref, s2_ref, qf_ref, kf_ref, v_ref):
        c, s1, s2 = c_ref[...], s1_ref[...], s2_ref[...]
        xq, _ = _rms_rows(cq_ref[0], None)
        qh = _dot(xq * gq_ref[...], wq_ref[...])
        xk, _ = _rms_rows(ckv_ref[0], None)
        ckv_n = xk * gkv_ref[...]
        kv = _dot(ckv_n, wk_ref[...])
        v_ref[0] = _dot(ckv_n, wv_ref[...]).astype(v_ref.dtype)
        kr = _rope(kr_ref[0], c, s1, s2)
        for h in range(MLA_HEADS):
            ls = slice(h * BLOCK, (h + 1) * BLOCK)
            qf_ref[0, :, ls] = _rope(qh[:, ls], c, s1, s2).astype(qf_ref.dtype)
            kf_ref[0, :, ls] = (kv[:, ls] + kr).astype(kf_ref.dtype)

    def col(first, width):
        return pl.BlockSpec((1, rows, width), lambda bi, n: (bi, n, first // width))

    def whole(a):
        return pl.BlockSpec(a.shape, lambda bi, n: (0,) * a.ndim)

    tab = pl.BlockSpec((rows, BLOCK), lambda bi, n: (n, 0))
    return pl.pallas_call(
        body, grid=(b, lp // rows),
        in_specs=[col(EV_CQ, MLA_Q_LORA), col(EV_CKV, MLA_KV_LORA), col(EV_KR, BLOCK), whole(gq), whole(gkv), whole(wq), whole(wk),
                  whole(wv), tab, tab, tab],
        out_specs=[col(0, hw), col(0, hw), col(0, MLA_HEADS * HEAD)],
        out_shape=[jax.ShapeDtypeStruct((b, lp, hw), BF16), jax.ShapeDtypeStruct((b, lp, hw), BF16),
                   jax.ShapeDtypeStruct((b, lp, MLA_HEADS * HEAD), BF16)],
        compiler_params=_params(("parallel", "parallel")), name=name,
    )(proj3, proj3, proj3, gq, gkv, wq, wk, wv, *tabs)


def _mla_prep_bwd(proj3, gq, gkv, wq, wk, wv, tabs, dqf, dkf, dv, name):
    b, lp, _ = proj3.shape
    rows = _prep_rows(lp)
    hw = MLA_HEADS * BLOCK

    def body(cq_ref, ckv_ref, gq_ref, gkv_ref, wq_ref, wk_ref, wv_ref, c_ref, s1_ref, s2_ref, dqf_ref, dkf_ref, dv_ref,
             dcq_ref, dckv_ref, dkr_ref, dwq_ref, dwk_ref, dwv_ref, dgq_ref, dgkv_ref, dqh):
        @pl.when((pl.program_id(0) == 0) & (pl.program_id(1) == 0))
        def _():
            for r in (dwq_ref, dwk_ref, dwv_ref, dgq_ref, dgkv_ref):
                r[...] = jnp.zeros_like(r)

        c, s1, s2 = c_ref[...], s1_ref[...], s2_ref[...]
        dkr = jnp.zeros((rows, BLOCK), F32)
        for h in range(MLA_HEADS):
            ls = slice(h * BLOCK, (h + 1) * BLOCK)
            dqh[:, ls] = _rope_t(dqf_ref[0, :, ls].astype(F32), c, s1, s2).astype(dqh.dtype)
            dkr = dkr + dkf_ref[0, :, ls].astype(F32)
        dkr_ref[0] = _rope_t(dkr, c, s1, s2).astype(dkr_ref.dtype)

        def norm_bwd(x, g, dy, dg_ref):
            xr, r = _rms_rows(x, None)
            u = dy * g
            dg_ref[...] += jnp.sum(dy * xr, axis=0, keepdims=True)
            return r * (u - xr * jnp.mean(u * xr, axis=-1, keepdims=True))

        xq, _ = _rms_rows(cq_ref[0], None)
        cq_n = xq * gq_ref[...]
        dwq_ref[...] += _dot_tn(cq_n, dqh[...])
        dcq_ref[0] = norm_bwd(cq_ref[0], gq_ref[...], _dot_nt(dqh[...], wq_ref[...]), dgq_ref).astype(dcq_ref.dtype)
        xk, _ = _rms_rows(ckv_ref[0], None)
        ckv_n = xk * gkv_ref[...]
        dkf_, dv_ = dkf_ref[0], dv_ref[0]
        dwk_ref[...] += _dot_tn(ckv_n, dkf_)
        dwv_ref[...] += _dot_tn(ckv_n, dv_)
        dckv_n = _dot_nt(dkf_, wk_ref[...]) + _dot_nt(dv_, wv_ref[...])
        dckv_ref[0] = norm_bwd(ckv_ref[0], gkv_ref[...], dckv_n, dgkv_ref).astype(dckv_ref.dtype)

    def col(first, width):
        return pl.BlockSpec((1, rows, width), lambda bi, n: (bi, n, first // width))

    def whole(a):
        return pl.BlockSpec(a.shape, lambda bi, n: (0,) * len(a.shape))

    tab = pl.BlockSpec((rows, BLOCK), lambda bi, n: (n, 0))
    acc_shapes = [jax.ShapeDtypeStruct(a.shape, F32) for a in (wq, wk, wv, gq, gkv)]
    return pl.pallas_call(
        body, grid=(b, lp // rows),
        in_specs=[col(EV_CQ, MLA_Q_LORA), col(EV_CKV, MLA_KV_LORA), whole(gq), whole(gkv), whole(wq), whole(wk), whole(wv), tab, tab, tab,
                  col(0, hw), col(0, hw), col(0, MLA_HEADS * HEAD)],
        out_specs=[col(0, MLA_Q_LORA), col(0, MLA_KV_LORA), col(0, BLOCK)] + [whole(a) for a in acc_shapes],
        out_shape=[jax.ShapeDtypeStruct((b, lp, MLA_Q_LORA), BF16), jax.ShapeDtypeStruct((b, lp, MLA_KV_LORA), BF16),
                   jax.ShapeDtypeStruct((b, lp, BLOCK), BF16)] + acc_shapes,
        scratch_shapes=[pltpu.VMEM((rows, hw), BF16)],
        compiler_params=_params(("arbitrary", "arbitrary")), name=name,
    )(proj3, proj3, gq, gkv, wq, wk, wv, *tabs, dqf, dkf, dv)


def _mla_fwd(qf, kf, v, name):
    b, lp, _ = qf.shape
    nb = lp // BLOCK
    npair = MLA_HEADS // 2
    scale = (MLA_NOPE + MLA_ROPE) ** -0.5
    starts = _chunk_starts(lp)

    def body(q_ref, k_ref, v_ref, o_ref, lse_ref, vt_ref):
        for c, s0 in enumerate(starts):
            vt_ref[c] = _split_rows_t(v_ref[0, s0:s0 + KEYS, :].astype(F32))

        def qtile(r0, bq, n):
            qs = [q_ref[0, pl.ds(r0, bq), h * BLOCK:(h + 1) * BLOCK] for h in range(2)]
            t_idx = r0 + _iota2((KEYS, bq), 1)

            def kchunk(c, carry, masked):
                stats, acc = carry[:4], carry[4]
                s0, valid = _key_chunk(c, lp, t_idx, False, 0)
                valid = valid if masked else None
                ps, new, alphas = [], [], []
                for h in range(2):
                    m, l = stats[2 * h], stats[2 * h + 1]
                    s = _where(valid, _dot_nt(k_ref[0, pl.ds(s0, KEYS), h * BLOCK:(h + 1) * BLOCK], qs[h]) * scale, NEG)
                    m_new = jnp.maximum(m, jnp.max(s, axis=0, keepdims=True))
                    p = _where(valid, jnp.exp(s - m_new), 0.0)
                    alpha = jnp.exp(m - m_new)
                    new += [m_new, alpha * l + jnp.sum(p, axis=0, keepdims=True)]
                    alphas.append(alpha)
                    ps.append(p.astype(BF16))
                pv = jnp.dot(vt_ref[c], jnp.concatenate(ps, axis=0), preferred_element_type=F32)
                return (*new, _rows_of_pair(alphas[0], alphas[1]) * acc + pv)

            neg, zero = jnp.full((1, bq), NEG, F32), jnp.zeros((1, bq), F32)
            m_a, l_a, m_b, l_b, acc = _walk_chunks(r0, n, kchunk, (neg, zero, neg, zero, jnp.zeros((BLOCK, bq), F32)))
            safe = [jnp.where(l > 0.0, l, 1.0) for l in (l_a, l_b)]
            o_ref[0, pl.ds(r0, bq), :] = (acc / _rows_of_pair(safe[0], safe[1])).T
            lse = [jnp.where(l > 0.0, m + jnp.log(sf), 0.0) for m, l, sf in ((m_a, l_a, safe[0]), (m_b, l_b, safe[1]))]
            _put_rows(lse_ref, r0, bq, lse[0], lse[1])

        _for_query_tiles(nb, qtile)

    wide = pl.BlockSpec((1, lp, 2 * BLOCK), lambda bi, hp: (bi, 0, hp))
    thin = pl.BlockSpec((1, lp, 2 * HEAD), lambda bi, hp: (bi, 0, hp))
    return pl.pallas_call(
        body, grid=(b, npair), in_specs=[wide, wide, thin], out_specs=[thin, _pair_stat_spec(nb)],
        out_shape=[jax.ShapeDtypeStruct((b, lp, MLA_HEADS * HEAD), F32), jax.ShapeDtypeStruct((b, npair, nb, 8, BLOCK), F32)],
        scratch_shapes=[pltpu.VMEM((len(starts), BLOCK, 2 * KEYS), BF16)],
        compiler_params=_params(("parallel", "parallel")), name=name,
    )(qf, kf, v)


def _mla_bwd(qf, kf, v, o, lse, do, name):
    b, lp, _ = qf.shape
    nb = lp // BLOCK
    npair = MLA_HEADS // 2
    scale = (MLA_NOPE + MLA_ROPE) ** -0.5

    starts = _chunk_starts(lp)

    def body(q_ref, k_ref, v_ref, o_ref, lse_ref, do_ref, dq_ref, dk_ref, dv_ref, kt_ref):
        lo = _lo_lanes()
        dk_ref[...] = jnp.zeros_like(dk_ref)
        dv_ref[...] = jnp.zeros_like(dv_ref)
        for c, s0 in enumerate(starts):
            for h in range(2):
                kt_ref[c, h] = k_ref[0, s0:s0 + KEYS, h * BLOCK:(h + 1) * BLOCK].astype(F32).T.astype(BF16)

        def qtile(r0, bq, n):
            rows = pl.ds(r0, bq)
            qs = [q_ref[0, rows, h * BLOCK:(h + 1) * BLOCK] for h in range(2)]
            do_i = do_ref[0, rows, :]
            dos = _halves(do_i.astype(BF16), lo)
            do_st = jnp.concatenate(dos, axis=0)
            both = (do_i * o_ref[0, rows, :]).T
            dsum = (jnp.sum(both[:HEAD], axis=0, keepdims=True), jnp.sum(both[HEAD:], axis=0, keepdims=True))
            lses = _get_rows(lse_ref, r0, bq)
            t_idx = r0 + _iota2((KEYS, bq), 1)

            def kchunk(c, dqts, masked):
                s0, valid = _key_chunk(c, lp, t_idx, False, 0)
                valid = valid if masked else None
                keys = pl.ds(s0, KEYS)
                v_c = v_ref[0, keys, :]
                ps, out = [], []
                for h in range(2):
                    lanes = slice(h * BLOCK, (h + 1) * BLOCK)
                    s = _dot_nt(k_ref[0, keys, lanes], qs[h]) * scale
                    p = _where(valid, jnp.exp(s - lses[h]), 0.0)
                    ds = (p * (_dot_nt(v_c, dos[h]) - dsum[h]) * scale).astype(BF16)
                    dk_ref[0, keys, lanes] += jnp.dot(ds, qs[h], preferred_element_type=F32)
                    out.append(dqts[h] + jnp.dot(kt_ref[c, h], ds, preferred_element_type=F32))
                    ps.append(p.astype(BF16))
                dv_ref[0, keys, :] += jnp.dot(jnp.concatenate(ps, axis=1), do_st, preferred_element_type=F32)
                return tuple(out)

            zero = jnp.zeros((BLOCK, bq), F32)
            dq_a, dq_b = _walk_chunks(r0, n, kchunk, (zero, zero))
            dq_ref[0, rows, 0:BLOCK] = dq_a.T
            dq_ref[0, rows, BLOCK:2 * BLOCK] = dq_b.T

        _for_query_tiles(nb, qtile)

    wide = pl.BlockSpec((1, lp, 2 * BLOCK), lambda bi, hp: (bi, 0, hp))
    thin = pl.BlockSpec((1, lp, 2 * HEAD), lambda bi, hp: (bi, 0, hp))
    return pl.pallas_call(
        body, grid=(b, npair), in_specs=[wide, wide, thin, thin, _pair_stat_spec(nb), thin], out_specs=[wide, wide, thin],
        out_shape=[jax.ShapeDtypeStruct(qf.shape, F32), jax.ShapeDtypeStruct(qf.shape, F32), jax.ShapeDtypeStruct(v.shape, F32)],
        scratch_shapes=[pltpu.VMEM((len(starts), 2, BLOCK, KEYS), BF16)],
        compiler_params=_params(("parallel", "parallel")), name=name,
    )(qf, kf, v, o, lse, do)


SWA_KEYS = 2 * BLOCK + N_META


def _swa_keys(k_ref, v_ref, n, kv):
    prev = jnp.maximum(n - 1, 0)
    rows = lambda blk: pl.ds(pl.multiple_of(blk * BLOCK, BLOCK), BLOCK)
    mine = (_iota2((1, BLOCK), 1) >= HEAD).astype(jnp.int32) == kv

    def both_halves(ref):
        x = jnp.concatenate([ref[0, rows(prev), :], ref[0, rows(n), :], ref[0, N_PAD:BLOCK, :]], axis=0)
        return jnp.where(mine, x, pltpu.roll(x, HEAD, 1))

    slot = _iota2((SWA_KEYS, BLOCK), 0)
    s_idx = jnp.where(slot < 2 * BLOCK, (n - 1) * BLOCK + slot, slot - 2 * BLOCK + N_PAD)
    dist = n * BLOCK + _iota2((SWA_KEYS, BLOCK), 1) - s_idx
    band = (slot < 2 * BLOCK) & (dist >= 0) & (dist < SWA_WINDOW) & (s_idx >= BLOCK)
    meta = (slot >= 2 * BLOCK) & (dist >= 0)
    return both_halves(k_ref), both_halves(v_ref), band | meta, dist.astype(F32), prev


def _pad_keys(x):
    return jnp.concatenate([x, jnp.zeros((3 * BLOCK - SWA_KEYS, x.shape[1]), x.dtype)], axis=0)


def _swa_probs(q_h, kdup, valid, dist, head, sink_ref):
    slope = jnp.exp(jnp.full((1, 1), -8.0 * math.log(2.0) / SWA_HEADS, F32) * (head + 1).astype(F32))
    s = jnp.where(valid, _dot_nt(kdup, q_h) - slope * dist, NEG)
    sink = sink_ref[pl.ds(head, 1), 0:1]
    m = jnp.maximum(jnp.max(s, axis=0, keepdims=True), sink)
    e = jnp.where(valid, jnp.exp(s - m), 0.0)
    es = jnp.exp(sink - m)
    inv = 1.0 / (jnp.sum(e, axis=0, keepdims=True) + es)
    return e * inv, es * inv


SWA_PAIRS = SWA_HEADS // SWA_KV_HEADS // 2
SWA_GROUP = SWA_PAIRS * 2 * HEAD


def _swa_specs(b, lp):
    nb = lp // BLOCK
    qcol = lambda first: pl.BlockSpec((1, BLOCK, SWA_GROUP), lambda bi, kv, n: (bi, n, first // SWA_GROUP + kv))
    kcol = lambda first: pl.BlockSpec((1, lp, BLOCK), lambda bi, kv, n: (bi, 0, first // BLOCK))
    sink = pl.BlockSpec((SWA_HEADS, BLOCK), lambda bi, kv, n: (0, 0))
    return (b, SWA_KV_HEADS, nb), qcol, kcol, sink


def _swa_fwd(proj3, sinks, name):
    b, lp, _ = proj3.shape
    grid, qcol, kcol, sink = _swa_specs(b, lp)

    def body(q_ref, k_ref, v_ref, sink_ref, o_ref):
        kv, n = pl.program_id(1), pl.program_id(2)
        lo = _lo_lanes()
        kdup, vdup, valid, dist, _ = _swa_keys(k_ref, v_ref, n, kv)
        kdup = kdup.astype(BF16)
        vt = _split_rows_t(_pad_keys(vdup))
        for p in range(SWA_PAIRS):
            lanes = slice(p * BLOCK, (p + 1) * BLOCK)
            qs = _halves((q_ref[0, :, lanes] * HEAD_SCALE).astype(BF16), lo)
            probs = [_swa_probs(qs[hh], kdup, valid, dist, (kv * SWA_PAIRS + p) * 2 + hh, sink_ref)[0].astype(BF16) for hh in range(2)]
            o_ref[0, :, lanes] = jnp.dot(vt, jnp.concatenate([_pad_keys(pr) for pr in probs], axis=0), preferred_element_type=F32).T

    return pl.pallas_call(
        body, grid=grid, in_specs=[qcol(OD_Q), kcol(OD_K), kcol(OD_V), sink], out_specs=qcol(0),
        out_shape=jax.ShapeDtypeStruct((b, lp, SWA_HEADS * HEAD), F32),
        compiler_params=_params(("parallel", "parallel", "parallel")), name=name,
    )(proj3, proj3, proj3, sinks)


def _swa_bwd(proj3, sinks, do, name):
    b, lp, _ = proj3.shape
    nb = lp // BLOCK
    grid, qcol, kcol, sink = _swa_specs(b, lp)

    def body(q_ref, k_ref, v_ref, sink_ref, do_ref, dq_ref, dk_ref, dv_ref, dsink_ref, dk_acc, dv_acc):
        kv, n = pl.program_id(1), pl.program_id(2)

        @pl.when((n == 0) & (pl.program_id(0) == 0) & (kv == 0))
        def _():
            dsink_ref[...] = jnp.zeros_like(dsink_ref)

        @pl.when(n == 0)
        def _():
            dk_acc[...] = jnp.zeros_like(dk_acc)
            dv_acc[...] = jnp.zeros_like(dv_acc)

        lo = _lo_lanes()
        kdup, vdup, valid, dist, prev = _swa_keys(k_ref, v_ref, n, kv)
        kt = _split_rows_t(_pad_keys(kdup))
        kdup, vdup = kdup.astype(BF16), vdup.astype(BF16)
        dkc = jnp.zeros((SWA_KEYS, BLOCK), F32)
        dvc = jnp.zeros((SWA_KEYS, BLOCK), F32)
        for p in range(SWA_PAIRS):
            lanes = slice(p * BLOCK, (p + 1) * BLOCK)
            qs = _halves((q_ref[0, :, lanes] * HEAD_SCALE).astype(BF16), lo)
            dos = _halves(do_ref[0, :, lanes].astype(BF16), lo)
            dss, prs = [], []
            for hh in range(2):
                head = (kv * SWA_PAIRS + p) * 2 + hh
                pr, ps = _swa_probs(qs[hh], kdup, valid, dist, head, sink_ref)
                dp = _dot_nt(vdup, dos[hh])
                dsum = jnp.sum(pr * dp, axis=0, keepdims=True)
                dsink_ref[pl.ds(head, 1), :] += jnp.broadcast_to(-jnp.sum(ps * dsum, axis=1, keepdims=True), (1, BLOCK))
                dss.append((pr * (dp - dsum)).astype(BF16))
                prs.append(pr.astype(BF16))
            dq_ref[0, :, lanes] = jnp.dot(kt, jnp.concatenate([_pad_keys(d) for d in dss], axis=0), preferred_element_type=F32).T * HEAD_SCALE
            dkc = dkc + jnp.dot(jnp.concatenate(dss, axis=1), jnp.concatenate(qs, axis=0), preferred_element_type=F32)
            dvc = dvc + jnp.dot(jnp.concatenate(prs, axis=1), jnp.concatenate(dos, axis=0), preferred_element_type=F32)
        rows = lambda blk: pl.ds(pl.multiple_of(blk * BLOCK, BLOCK), BLOCK)
        for r, part in ((rows(prev), slice(0, BLOCK)), (rows(n), slice(BLOCK, 2 * BLOCK)), (slice(N_PAD, BLOCK), slice(2 * BLOCK, SWA_KEYS))):
            dk_acc[r, :] += dkc[part]
            dv_acc[r, :] += dvc[part]

        for acc, ref in ((dk_acc, dk_ref), (dv_acc, dv_ref)):
            @pl.when((n == nb - 1) & (kv == 0))
            def _():
                x = acc[...]
                ref[0] = x + pltpu.roll(x, HEAD, 1)

            @pl.when((n == nb - 1) & (kv == 1))
            def _():
                x = acc[...]
                ref[0] = jnp.where(lo, ref[0], x + pltpu.roll(x, HEAD, 1))

    kvout = pl.BlockSpec((1, lp, BLOCK), lambda bi, kv, n: (bi, 0, 0))
    kvshape = jax.ShapeDtypeStruct((b, lp, BLOCK), F32)
    return pl.pallas_call(
        body, grid=grid, in_specs=[qcol(OD_Q), kcol(OD_K), kcol(OD_V), sink, qcol(0)], out_specs=[qcol(0), kvout, kvout, sink],
        out_shape=[jax.ShapeDtypeStruct((b, lp, SWA_HEADS * HEAD), F32), kvshape, kvshape, jax.ShapeDtypeStruct((SWA_HEADS, BLOCK), F32)],
        scratch_shapes=[pltpu.VMEM((lp, BLOCK), F32), pltpu.VMEM((lp, BLOCK), F32)],
        compiler_params=_params(("arbitrary", "arbitrary", "arbitrary")), name=name,
    )(proj3, proj3, proj3, sinks, do)


def _kernel_weights(ev_w_in, ev_w_uq, ev_w_ukv, od_w_in):
    zeros = lambda r, c: jnp.zeros((r, c), ev_w_in.dtype)
    q_sb, k_sb, v_sb, g_sb, c_q, c_kv, k_r, g_mla = jnp.split(ev_w_in, [512, 1024, 1536, 2048, 2304, 2432, 2464], axis=1)
    w0 = jnp.concatenate([g_sb, g_mla, q_sb, k_sb, v_sb, c_q, c_kv, zeros(D_MODEL, MLA_NOPE), k_r, zeros(D_MODEL, 32)], axis=1)
    uq = ev_w_uq.reshape(MLA_Q_LORA, MLA_HEADS, MLA_NOPE + MLA_ROPE)
    wq = jnp.pad(uq, ((0, 0), (0, 0), (0, BLOCK - MLA_NOPE - MLA_ROPE))).reshape(MLA_Q_LORA, MLA_HEADS * BLOCK)
    ukv = ev_w_ukv.reshape(MLA_KV_LORA, MLA_HEADS, BLOCK)
    wk = jnp.pad(ukv[:, :, :MLA_NOPE], ((0, 0), (0, 0), (0, BLOCK - MLA_NOPE))).reshape(MLA_KV_LORA, MLA_HEADS * BLOCK)
    wv = ukv[:, :, MLA_NOPE:].reshape(MLA_KV_LORA, MLA_HEADS * HEAD)
    q, k, v, g = jnp.split(od_w_in, [1024, 1152, 1280], axis=1)
    w1 = jnp.concatenate([g, q, k, v], axis=1)
    return w0, wq, wk, wv, w1


def _od_w_in_grad(dw1):
    sl = lambda a, first, n: a[:, first:first + n]
    return jnp.concatenate([sl(dw1, OD_Q, 1024), sl(dw1, OD_K, 128), sl(dw1, OD_V, 128), sl(dw1, OD_G, 1024)], axis=1)


def _original_grads(dw0, dwq, dwk, dwv):
    sl = lambda a, first, n: a[:, first:first + n]
    d_ev_w_in = jnp.concatenate([sl(dw0, EV_Q, 512), sl(dw0, EV_K, 512), sl(dw0, EV_V, 512), sl(dw0, EV_G, 512), sl(dw0, EV_CQ, 256),
                                 sl(dw0, EV_CKV, 128), sl(dw0, EV_KR + MLA_NOPE, MLA_ROPE), sl(dw0, EV_G + 512, 512)], axis=1)
    d_uq = dwq.reshape(MLA_Q_LORA, MLA_HEADS, BLOCK)[:, :, :MLA_NOPE + MLA_ROPE].reshape(MLA_Q_LORA, -1)
    d_ukv = jnp.concatenate([dwk.reshape(MLA_KV_LORA, MLA_HEADS, BLOCK)[:, :, :MLA_NOPE], dwv.reshape(MLA_KV_LORA, MLA_HEADS, HEAD)],
                            axis=2).reshape(MLA_KV_LORA, -1)
    return d_ev_w_in, d_uq, d_ukv


def _meta_rows_sum(dh0_3):
    b, _, d = dh0_3.shape

    def body(x_ref, o_ref):
        acc = x_ref[0, N_PAD:BLOCK, :]
        for i in range(1, b):
            acc = acc + x_ref[i, N_PAD:BLOCK, :]
        o_ref[...] = acc

    return pl.pallas_call(
        body, grid=(1,), in_specs=[pl.BlockSpec((b, BLOCK, d), lambda i: (0, 0, 0))], out_specs=pl.BlockSpec((N_META, d), lambda i: (0, 0)),
        out_shape=jax.ShapeDtypeStruct((N_META, d), F32), compiler_params=_params(("arbitrary",)), name="meta_rows_sum",
    )(dh0_3)


def _local_step(x, meta, norm_g, final_g, gq, gkv, sinks, target, ev_w_in, ev_w_uq, ev_w_ukv, wo0, od_w_in, wo1, reduce_early=None):
    b, seq, d = x.shape
    lp = seq + BLOCK
    t = b * lp
    w0, wq, wk, wv, w1 = _kernel_weights(ev_w_in, ev_w_uq, ev_w_ukv, od_w_in)
    h0 = jnp.concatenate([jnp.zeros((b, N_PAD, d), F32), jnp.broadcast_to(meta[None], (b, N_META, d)), x], axis=1).reshape(t, d)
    tabs = _rope_tables(lp)
    g0, g1 = norm_g[0:1], norm_g[1:2]

    hn0 = _rms_fwd(h0, g0, "norm0")
    proj0 = _mm(hn0, w0, "inproj0")
    p0 = proj0.reshape(b, lp, EV_N)
    o_sb, sb_tot = _sb_fwd(p0, "sb_fwd")
    qf, kf, v = _mla_prep_fwd(p0, gq, gkv, wq, wk, wv, tabs, "mla_prep_fwd")
    o_mla, lse = _mla_fwd(qf, kf, v, "mla_fwd")
    o0 = [o_sb.reshape(t, -1), o_mla.reshape(t, -1)]
    ao0 = _gate_fwd(o0, proj0, "gate0")
    h1 = _mm(ao0, wo0, "outproj0", res=h0)

    hn1 = _rms_fwd(h1, g1, "norm1")
    proj1 = _mm(hn1, w1, "inproj1")
    p1 = proj1.reshape(b, lp, OD_N)
    sinks_b = jnp.broadcast_to(sinks.reshape(SWA_HEADS, 1), (SWA_HEADS, BLOCK))
    o1 = _swa_fwd(p1, sinks_b, "swa_fwd").reshape(t, -1)
    ao1 = _gate_fwd([o1], proj1, "gate1")
    h2 = _mm(ao1, wo1, "outproj1", res=h1)

    dh2, d_final_g, loss = _loss_head(h2, final_g.reshape(1, d), target, b, lp)

    d_wo1 = _mm_tn(ao1, dh2, "d_wo1")
    dao1 = _mm_nt(dh2, wo1, "d_ao1")
    (do1,), dg1 = _gate_bwd(dao1, [o1], proj1, "gate1_bwd")
    dq1, dk4, dv4, d_sinks = _swa_bwd(p1, sinks_b, do1.reshape(b, lp, -1), "swa_bwd")
    unheads = lambda a: a.reshape(t, SWA_KV_HEADS * HEAD).astype(BF16)
    dproj1 = jnp.concatenate([dg1, dq1.reshape(t, -1).astype(BF16), unheads(dk4), unheads(dv4)], axis=1)
    d_w1 = _mm_tn(hn1, dproj1, "d_w1")
    dhn1 = _mm_nt(dproj1, w1, "d_hn1")
    dh1, d_g1 = _rms_bwd(h1, g1, dhn1, dh2, "norm1_bwd")

    d_wo0 = _mm_tn(ao0, dh1, "d_wo0")
    early = dict(od_w_in=_od_w_in_grad(d_w1), ev_w_out=d_wo0, od_w_out=d_wo1)
    dao0 = _mm_nt(dh1, wo0, "d_ao0")
    if reduce_early is not None:
        pending, finish = reduce_early(early)
        dao0, pending = lax.optimization_barrier((dao0, pending))
        early_done = finish(pending)
    (do_sb, do_mla), dg0 = _gate_bwd(dao0, o0, proj0, "gate0_bwd")
    dq_sb, dk_sb, dv_sb = _sb_bwd(p0, sb_tot, do_sb.reshape(b, lp, -1), "sb_bwd")
    dqf, dkf, dv = _mla_bwd(qf, kf, v, o_mla, lse, do_mla.reshape(b, lp, -1), "mla_bwd")
    dcq, dckv, dkr, d_wq, d_wk, d_wv, d_gq, d_gkv = _mla_prep_bwd(p0, gq, gkv, wq, wk, wv, tabs, dqf, dkf, dv, "mla_prep_bwd")
    flat = lambda a: a.reshape(t, -1).astype(BF16)
    dproj0 = jnp.concatenate([dg0, flat(dq_sb), flat(dk_sb), flat(dv_sb), flat(dcq), flat(dckv), flat(dkr)], axis=1)
    if reduce_early is not None:
        dproj0, early_done = lax.optimization_barrier((dproj0, early_done))
    d_w0 = _mm_tn(hn0, dproj0, "d_w0")
    d_ev_w_in, d_uq, d_ukv = _original_grads(d_w0, d_wq, d_wk, d_wv)
    dhn0 = _mm_nt(dproj0, w0, "d_hn0")
    dh0, d_g0 = _rms_bwd(h0, g0, dhn0, dh1, "norm0_bwd")
    dh0 = dh0.reshape(b, lp, d)

    grads = dict(meta=_meta_rows_sum(dh0), norm_g=jnp.concatenate([d_g0, d_g1], axis=0), final_g=d_final_g.reshape(d),
                 ev_w_in=d_ev_w_in, ev_q_norm_g=d_gq, ev_kv_norm_g=d_gkv, ev_w_uq=d_uq, ev_w_ukv=d_ukv,
                 od_sinks=d_sinks[:, 0].reshape(1, SWA_HEADS))
    if reduce_early is None:
        return loss, dh0[:, BLOCK:], {**grads, **early}
    return loss, dh0[:, BLOCK:], grads, early_done


MESH = pl.DeviceIdType.MESH
ANY = pl.BlockSpec(memory_space=pl.ANY)


def _place():
    return lax.axis_index("x"), lax.axis_index("y"), lax.axis_index("c")


def _other_chips(x, y):
    return [(1 - x, y), (x, 1 - y), (1 - x, 1 - y)]


def _with_own_slot(slots, own):
    me = 2 * lax.axis_index("x") + lax.axis_index("y")
    return lax.dynamic_update_slice(slots, own[None], (me,) + (0,) * own.ndim)


def _gather_weights(packs, meta, name):
    n = len(packs)

    def body(*refs):
        ins, m_ref, outs, mo_ref = refs[:n], refs[n], refs[n + 1:2 * n + 1], refs[2 * n + 1]
        send_sems, recv_sems = refs[2 * n + 2:]
        x, y, c = _place()
        me, sib = 2 * x + y, (x, y, 1 - c)
        chips = _other_chips(x, y)

        def copy(k, src, dst, to):
            return pltpu.make_async_remote_copy(src_ref=src, dst_ref=dst, send_sem=send_sems.at[k], recv_sem=recv_sems.at[k], device_id=to,
                                                device_id_type=MESH)

        def half(i, chip, h):
            rows = packs[i].shape[0] // 2
            return outs[i].at[chip, pl.ds(h * rows, rows), :]

        def mine(i):
            rows = packs[i].shape[0] // 2
            return ins[i].at[pl.ds(c * rows, rows), :]

        sent = [copy(6 * i + k, mine(i), half(i, me, c), (px, py, c)) for i in range(n) for k, (px, py) in enumerate(chips)]
        sent += [copy(6 * n + k, m_ref, mo_ref.at[me], (px, py, c)) for k, (px, py) in enumerate(chips)]
        for cp in sent:
            cp.start()
        for i in range(n):
            for k, (px, py) in enumerate(chips):
                landed = half(i, 2 * px + py, c)
                copy(6 * i + k, mine(i), landed, (px, py, c)).wait_recv()
                fwd = copy(6 * i + 3 + k, landed, landed, sib)
                fwd.start()
                sent.append(fwd)
        for k, (px, py) in enumerate(chips):
            for i in range(n):
                other = half(i, 2 * px + py, 1 - c)
                copy(6 * i + 3 + k, other, other, sib).wait_recv()
            copy(6 * n + k, m_ref, mo_ref.at[2 * px + py], (px, py, c)).wait_recv()
        for cp in sent:
            cp.wait_send()

    nsem = 6 * n + 3
    res = pl.pallas_call(
        body, in_specs=[ANY] * (n + 1), out_specs=[ANY] * (n + 1),
        out_shape=[jax.ShapeDtypeStruct((N_CHIPS,) + a.shape, a.dtype) for a in list(packs) + [meta]],
        scratch_shapes=[pltpu.SemaphoreType.DMA((nsem,)), pltpu.SemaphoreType.DMA((nsem,))],
        name=name,
    )(*packs, meta)
    return [_with_own_slot(r, a) for r, a in zip(res[:n], packs)], _with_own_slot(res[n], meta)


def _grads_to_sibling(gs, name):
    n = len(gs)

    def body(*refs):
        ins, outs, send_sems, recv_sems = refs[:n], refs[n:2 * n], refs[2 * n], refs[2 * n + 1]
        x, y, c = _place()
        cps = []
        for i in range(n):
            rows = gs[i].shape[1] // 2
            cps.append(pltpu.make_async_remote_copy(src_ref=ins[i].at[:, pl.ds((1 - c) * rows, rows), :], dst_ref=outs[i],
                                                    send_sem=send_sems.at[i], recv_sem=recv_sems.at[i], device_id=(x, y, 1 - c),
                                                    device_id_type=MESH))
        for cp in cps:
            cp.start()
        for cp in cps:
            cp.wait()

    return pl.pallas_call(
        body, in_specs=[ANY] * n, out_specs=[ANY] * n,
        out_shape=[jax.ShapeDtypeStruct((g.shape[0], g.shape[1] // 2, g.shape[2]), g.dtype) for g in gs],
        scratch_shapes=[pltpu.SemaphoreType.DMA((n,)), pltpu.SemaphoreType.DMA((n,))],
        name=name,
    )(*gs)


def _share_halves(rs, name):
    n = len(rs)

    def body(*refs):
        ins, outs, send_sems, recv_sems = refs[:n], refs[n:2 * n], refs[2 * n], refs[2 * n + 1]
        x, y, c = _place()
        cps = [pltpu.make_async_remote_copy(src_ref=ins[i], dst_ref=outs[i], send_sem=send_sems.at[i], recv_sem=recv_sems.at[i],
                                            device_id=(x, y, 1 - c), device_id_type=MESH) for i in range(n)]
        for cp in cps:
            cp.start()
        for cp in cps:
            cp.wait()

    theirs = pl.pallas_call(
        body, in_specs=[ANY] * n, out_specs=[ANY] * n, out_shape=[jax.ShapeDtypeStruct(r.shape, r.dtype) for r in rs],
        scratch_shapes=[pltpu.SemaphoreType.DMA((n,)), pltpu.SemaphoreType.DMA((n,))],
        name=name,
    )(*rs)
    first = lax.axis_index("c") == 0
    return [jnp.where(first, jnp.concatenate([r, t], axis=0), jnp.concatenate([t, r], axis=0)) for r, t in zip(rs, theirs)]


def _chip_scatter(ss, name):
    n = len(ss)

    def body(*refs):
        ins, outs, send_sems, recv_sems = refs[:n], refs[n:2 * n], refs[2 * n], refs[2 * n + 1]
        x, y, c = _place()
        me = 2 * x + y
        chips = _other_chips(x, y)
        for i in range(n):
            for k, (px, py) in enumerate(chips):
                pltpu.make_async_remote_copy(src_ref=ins[i].at[2 * px + py], dst_ref=outs[i].at[me], send_sem=send_sems.at[3 * i + k],
                                             recv_sem=recv_sems.at[3 * i + k], device_id=(px, py, c), device_id_type=MESH).start()
        for i in range(n):
            for k, (px, py) in enumerate(chips):
                cp = pltpu.make_async_remote_copy(src_ref=ins[i].at[2 * px + py], dst_ref=outs[i].at[2 * px + py],
                                                  send_sem=send_sems.at[3 * i + k], recv_sem=recv_sems.at[3 * i + k],
                                                  device_id=(px, py, c), device_id_type=MESH)
                cp.wait_recv()
                cp.wait_send()

    parts = pl.pallas_call(
        body, in_specs=[ANY] * n, out_specs=[ANY] * n, out_shape=[jax.ShapeDtypeStruct(s.shape, s.dtype) for s in ss],
        scratch_shapes=[pltpu.SemaphoreType.DMA((3 * n,)), pltpu.SemaphoreType.DMA((3 * n,))],
        name=name,
    )(*ss)
    me = 2 * lax.axis_index("x") + lax.axis_index("y")
    return [_with_own_slot(p, lax.dynamic_index_in_dim(s, me, axis=0, keepdims=False)) for p, s in zip(parts, ss)]


HBM_SPACE = pltpu.MemorySpace.HBM


def _on_sequencer(name, collective_id, n_sems, body):
    @pl.kernel(mesh=plsc.ScalarSubcoreMesh(axis_name="sequencer", num_cores=1), name=name,
               scratch_types=(pltpu.SemaphoreType.DMA((n_sems,)), pltpu.SemaphoreType.DMA((n_sems,))),
               compiler_params=pltpu.CompilerParams(collective_id=collective_id))
    def launch(send_sems, recv_sems):
        body(send_sems, recv_sems)

    launch()


def _handshake(peers):
    barrier = pltpu.get_barrier_semaphore()
    for peer in peers:
        pl.semaphore_signal(barrier, inc=1, device_id=peer, device_id_type=MESH)
    pl.semaphore_wait(barrier, len(peers))


def _gather_on_sequencer(packs, name):
    n = len(packs)
    ins = [jax.new_ref(p, memory_space=HBM_SPACE) for p in packs]
    outs = [jax.empty_ref(jax.ShapeDtypeStruct((N_CHIPS,) + p.shape, p.dtype), memory_space=HBM_SPACE) for p in packs]

    def body(send_sems, recv_sems):
        x, y, c = _place()
        me, sib = 2 * x + y, (x, y, 1 - c)
        chips = _other_chips(x, y)
        _handshake([(px, py, c) for px, py in chips] + [sib])

        def copy(k, src, dst, to):
            return pltpu.make_async_remote_copy(src_ref=src, dst_ref=dst, send_sem=send_sems.at[k], recv_sem=recv_sems.at[k], device_id=to,
                                                device_id_type=MESH)

        def half(i, chip, h):
            rows = packs[i].shape[0] // 2
            return outs[i].at[chip, pl.ds(h * rows, rows), :]

        def mine(i):
            rows = packs[i].shape[0] // 2
            return ins[i].at[pl.ds(c * rows, rows), :]

        sent = [copy(6 * i + k, mine(i), half(i, me, c), (px, py, c)) for i in range(n) for k, (px, py) in enumerate(chips)]
        for cp in sent:
            cp.start()
        for i in range(n):
            for k, (px, py) in enumerate(chips):
                landed = half(i, 2 * px + py, c)
                copy(6 * i + k, mine(i), landed, (px, py, c)).wait_recv()
                fwd = copy(6 * i + 3 + k, landed, landed, sib)
                fwd.start()
                sent.append(fwd)
        for i in range(n):
            for k, (px, py) in enumerate(chips):
                other = half(i, 2 * px + py, 1 - c)
                copy(6 * i + 3 + k, other, other, sib).wait_recv()
        for cp in sent:
            cp.wait_send()

    _on_sequencer(name, 1, 6 * n, body)
    return [_with_own_slot(o[...], p) for o, p in zip(outs, packs)]


def _grads_to_sibling_on_sequencer(gs, name, collective_id):
    n = len(gs)
    ins = [jax.new_ref(g, memory_space=HBM_SPACE) for g in gs]
    outs = [jax.empty_ref(jax.ShapeDtypeStruct((g.shape[0], g.shape[1] // 2, g.shape[2]), g.dtype), memory_space=HBM_SPACE) for g in gs]

    def body(send_sems, recv_sems):
        x, y, c = _place()
        _handshake([(x, y, 1 - c)])
        cps = []
        for i in range(n):
            rows = gs[i].shape[1] // 2
            cps.append(pltpu.make_async_remote_copy(src_ref=ins[i].at[:, pl.ds((1 - c) * rows, rows), :], dst_ref=outs[i],
                                                    send_sem=send_sems.at[i], recv_sem=recv_sems.at[i], device_id=(x, y, 1 - c),
                                                    device_id_type=MESH))
        for cp in cps:
            cp.start()
        for cp in cps:
            cp.wait()

    _on_sequencer(name, collective_id, n, body)
    return [o[...] for o in outs]


def _chip_scatter_on_sequencer(ss, name, collective_id):
    n = len(ss)
    ins = [jax.new_ref(s, memory_space=HBM_SPACE) for s in ss]
    outs = [jax.empty_ref(jax.ShapeDtypeStruct(s.shape, s.dtype), memory_space=HBM_SPACE) for s in ss]

    def body(send_sems, recv_sems):
        x, y, c = _place()
        me = 2 * x + y
        chips = _other_chips(x, y)
        _handshake([(px, py, c) for px, py in chips])
        for i in range(n):
            for k, (px, py) in enumerate(chips):
                pltpu.make_async_remote_copy(src_ref=ins[i].at[2 * px + py], dst_ref=outs[i].at[me], send_sem=send_sems.at[3 * i + k],
                                             recv_sem=recv_sems.at[3 * i + k], device_id=(px, py, c), device_id_type=MESH).start()
        for i in range(n):
            for k, (px, py) in enumerate(chips):
                cp = pltpu.make_async_remote_copy(src_ref=ins[i].at[2 * px + py], dst_ref=outs[i].at[2 * px + py],
                                                  send_sem=send_sems.at[3 * i + k], recv_sem=recv_sems.at[3 * i + k],
                                                  device_id=(px, py, c), device_id_type=MESH)
                cp.wait_recv()
                cp.wait_send()

    _on_sequencer(name, collective_id, 3 * n, body)
    me = 2 * lax.axis_index("x") + lax.axis_index("y")
    return [_with_own_slot(o[...], lax.dynamic_index_in_dim(s, me, axis=0, keepdims=False)) for o, s in zip(outs, ss)]


def _all_reduce_small(v, name):
    shape = v.shape

    def body(v_ref, o_ref, slots, send_sems, recv_sems):
        x, y, c = _place()
        me = 4 * x + 2 * y + c
        slots[me] = v_ref[...]
        for r in range(1, N_DEV):
            peer = (x ^ (r >> 2), y ^ ((r >> 1) & 1), c ^ (r & 1))
            pltpu.make_async_remote_copy(src_ref=v_ref, dst_ref=slots.at[me], send_sem=send_sems.at[r - 1], recv_sem=recv_sems.at[r - 1],
                                         device_id=peer, device_id_type=MESH).start()
        for r in range(1, N_DEV):
            peer = (x ^ (r >> 2), y ^ ((r >> 1) & 1), c ^ (r & 1))
            cp = pltpu.make_async_remote_copy(src_ref=v_ref, dst_ref=slots.at[4 * peer[0] + 2 * peer[1] + peer[2]], send_sem=send_sems.at[r - 1],
                                              recv_sem=recv_sems.at[r - 1], device_id=peer, device_id_type=MESH)
            cp.wait_recv()
            cp.wait_send()
        acc = slots[0]
        for d in range(1, N_DEV):
            acc = acc + slots[d]
        o_ref[...] = acc

    vm = pl.BlockSpec(memory_space=pltpu.VMEM)
    return pl.pallas_call(
        body, in_specs=[vm], out_specs=vm, out_shape=jax.ShapeDtypeStruct(shape, F32),
        scratch_shapes=[pltpu.VMEM((N_DEV,) + shape, F32), pltpu.SemaphoreType.DMA((N_DEV - 1,)), pltpu.SemaphoreType.DMA((N_DEV - 1,))],
        name=name,
    )(v)


def _add_sibling(g, gsib, core, name):
    n, _, cdim = g.shape
    half = gsib.shape[1]
    tr = half // 2

    def body(core_ref, a_ref, b_ref, o_ref):
        o_ref[...] = (a_ref[...] + b_ref[...]).astype(o_ref.dtype)

    blk = pl.BlockSpec((1, tr, cdim), lambda j, i, core_ref: (j, i, 0))
    return pl.pallas_call(
        body,
        grid_spec=pltpu.PrefetchScalarGridSpec(
            num_scalar_prefetch=1, grid=(n, half // tr),
            in_specs=[pl.BlockSpec((1, tr, cdim), lambda j, i, core_ref: (j, core_ref[0] * (half // tr) + i, 0)), blk], out_specs=blk),
        out_shape=jax.ShapeDtypeStruct(gsib.shape, BF16), compiler_params=_params(("parallel", "parallel")), name=name,
    )(core, g, gsib)


def _sum_parts(parts, name):
    n, r, cdim = parts.shape
    tr = r // 2

    def body(p_ref, o_ref):
        acc = p_ref[0].astype(F32)
        for j in range(1, n):
            acc = acc + p_ref[j].astype(F32)
        o_ref[...] = acc

    return pl.pallas_call(
        body, grid=(r // tr,), in_specs=[pl.BlockSpec((n, tr, cdim), lambda i: (0, i, 0))],
        out_specs=pl.BlockSpec((tr, cdim), lambda i: (i, 0)), out_shape=jax.ShapeDtypeStruct((r, cdim), F32),
        compiler_params=_params(("parallel",)), name=name,
    )(parts)


def _adamw(parts, w, m, v, name):
    npart, r, cdim = parts.shape
    tr = r // 4 if r % 32 == 0 else r

    def body(p_ref, w_ref, m_ref, v_ref, g_ref, d_ref, nm_ref, nv_ref):
        g = p_ref[0]
        for j in range(1, npart):
            g = g + p_ref[j]
        m_new = ADAM_B1 * m_ref[...] + (1.0 - ADAM_B1) * g
        v_new = ADAM_B2 * v_ref[...] + (1.0 - ADAM_B2) * (g * g)
        m_hat = m_new / (1.0 - ADAM_B1 ** ADAM_STEP)
        v_hat = v_new / (1.0 - ADAM_B2 ** ADAM_STEP)
        g_ref[...] = g
        d_ref[...] = -ADAM_LR * (m_hat / (jnp.sqrt(v_hat) + ADAM_EPS) + ADAM_WD * w_ref[...])
        nm_ref[...] = m_new
        nv_ref[...] = v_new

    blk = pl.BlockSpec((tr, cdim), lambda i: (i, 0))
    shp = jax.ShapeDtypeStruct((r, cdim), F32)
    return pl.pallas_call(
        body, grid=(r // tr,), in_specs=[pl.BlockSpec((npart, tr, cdim), lambda i: (0, i, 0)), blk, blk, blk], out_specs=[blk] * 4,
        out_shape=[shp] * 4, compiler_params=_params(("parallel",)), name=name,
    )(parts, w, m, v)


BIG = ("ev_w_in", "ev_w_uq", "ev_w_ukv", "ev_w_out", "od_w_in", "od_w_out", "meta")
SMALL = ("norm_g", "final_g", "ev_q_norm_g", "ev_kv_norm_g", "od_sinks")
SMALL_SHAPE = (8, 512)
BY_ROWS = ("ev_w_out", "od_w_out")

EV_IN_SHARD, OD_IN_SHARD, UQ_SHARD = 2976 // N_CHIPS, 2304 // N_CHIPS, 768 // N_CHIPS


def _pack_big(a, lead=()):
    return _pack_first(a, lead) + _pack_later(a, lead)


def _pad_lanes(x, width):
    return jnp.concatenate([x, jnp.zeros(x.shape[:-1] + (width - x.shape[-1],), x.dtype)], axis=-1)


def _pack_latent(a, lead=()):
    ax = len(lead)
    corner = jnp.concatenate([a["ev_w_ukv"], a["meta"], jnp.zeros(lead + (256 - MLA_KV_LORA - N_META, 256), a["meta"].dtype)], axis=ax)
    return jnp.concatenate([_pad_lanes(a["ev_w_uq"], 256), corner], axis=ax + 1)


def _pack_first(a, lead=()):
    return _pad_lanes(a["ev_w_in"], 768), _pack_latent(a, lead)


def _pack_later(a, lead=()):
    return _pad_lanes(a["od_w_in"], 640), jnp.concatenate([a["ev_w_out"], a["od_w_out"]], axis=len(lead) + 1)


N_FIRST = 2
LATER = ("od_w_in", "ev_w_out", "od_w_out")


def _unpack_big(p_in0, p_lat, p_in1, p_out):
    return dict(ev_w_in=p_in0[..., :EV_IN_SHARD], od_w_in=p_in1[..., :OD_IN_SHARD], ev_w_out=p_out[..., :D_MODEL],
                od_w_out=p_out[..., D_MODEL:], ev_w_uq=p_lat[..., :UQ_SHARD], ev_w_ukv=p_lat[..., :MLA_KV_LORA, 256:],
                meta=p_lat[..., MLA_KV_LORA:MLA_KV_LORA + N_META, 256:])


def _chip_shards(full, by_rows):
    if by_rows:
        return full.reshape(N_CHIPS, full.shape[0] // N_CHIPS, full.shape[1])
    cols = full.shape[1] // N_CHIPS
    return jnp.stack([full[:, j * cols:(j + 1) * cols] for j in range(N_CHIPS)])


def _from_chip_shards(slots, by_rows):
    if by_rows:
        return slots.reshape(-1, slots.shape[2])
    return jnp.concatenate([slots[j] for j in range(N_CHIPS)], axis=1)


def _pack_small(arrs, extra=None):
    flat = [a.reshape(-1) for a in arrs] + ([] if extra is None else [extra.reshape(-1)])
    used = sum(f.shape[0] for f in flat)
    return jnp.pad(jnp.concatenate(flat), (0, SMALL_SHAPE[0] * SMALL_SHAPE[1] - used)).reshape(SMALL_SHAPE)


def _unpack_small(p, shapes):
    flat, out, at = p.reshape(-1), [], 0
    for s in shapes:
        n = int(np.prod(s))
        out.append(flat[at:at + n].reshape(s))
        at += n
    return out, flat[at]


def kernel(x, meta, norm_g, final_g, ev_w_in, ev_q_norm_g, ev_kv_norm_g, ev_w_uq, ev_w_ukv, ev_w_out, od_w_in, od_sinks, od_w_out, loss_target, m_meta, m_norm_g, m_final_g, m_ev_w_in, m_ev_q_norm_g, m_ev_kv_norm_g, m_ev_w_uq, m_ev_w_ukv, m_ev_w_out, m_od_w_in, m_od_sinks, m_od_w_out, v_meta, v_norm_g, v_final_g, v_ev_w_in, v_ev_q_norm_g, v_ev_kv_norm_g, v_ev_w_uq, v_ev_w_ukv, v_ev_w_out, v_od_w_in, v_od_sinks, v_od_w_out):
    given = dict(locals())
    two_d = lambda a: a[0] if a.ndim == 3 else a
    packs = {k: _pack_big({n: two_d(given[k + n]) for n in BIG}) for k in ("", "m_", "v_")}

    wbf = [p.astype(BF16) for p in packs[""]]
    later = _gather_on_sequencer(wbf[N_FIRST:], "gather_later_weights")
    first, meta_all = _gather_weights(wbf[:N_FIRST], meta, "gather_weights")
    full = {n: _from_chip_shards(a, n in BY_ROWS) for n, a in _unpack_big(*first, *later).items()}
    meta_full = _from_chip_shards(meta_all, False)

    core = lax.axis_index("c").astype(jnp.int32).reshape(1)
    shards = lambda g: {n: _chip_shards(a, n in BY_ROWS) for n, a in g.items()}

    def sums_of(group, tag, to_sibling):
        return [_add_sibling(g, s, core, f"add_sibling_{tag}{i}") for i, (g, s) in enumerate(zip(group, to_sibling(group)))]

    def update(sums, tag, first_pack, scatter):
        parts = scatter(sums, "grads_to_chips_" + tag)
        reduced = _share_halves([_sum_parts(p, f"add_chips_{tag}{i}") for i, p in enumerate(parts)], "reduced_to_sibling_" + tag)
        return [_adamw(r[None], packs[""][first_pack + i], packs["m_"][first_pack + i], packs["v_"][first_pack + i],
                       f"adamw_matrices_{first_pack + i}") for i, r in enumerate(reduced)]

    def behind(tag, first_pack, ids):
        def start(packed):
            sums = sums_of(packed, tag, lambda p: _grads_to_sibling_on_sequencer(p, "grads_to_sibling_" + tag, ids[0]))
            return sums, lambda s: update(s, tag, first_pack, lambda t, name: _chip_scatter_on_sequencer(t, name, ids[1]))
        return start

    loss, grad_x, grads, updated_later = _local_step(
        x, meta_full, norm_g, final_g, ev_q_norm_g, ev_kv_norm_g, od_sinks, loss_target, full["ev_w_in"], full["ev_w_uq"],
        full["ev_w_ukv"], full["ev_w_out"], full["od_w_in"], full["od_w_out"],
        reduce_early=lambda g: behind("later", N_FIRST, (3, 2))(_pack_later(shards(g), (N_CHIPS,))))

    first = _pack_first(shards({n: grads[n] for n in BIG if n not in LATER}), (N_CHIPS,))
    updated_first = update(sums_of(first, "first", lambda p: _grads_to_sibling(p, "grads_to_sibling_first")), "first", 0,
                           lambda t, name: _chip_scatter_on_sequencer(t, name, 4))
    updated = updated_first + updated_later
    big_out = [{n: a.reshape(given[n].shape) for n, a in _unpack_big(*outs).items()} for outs in zip(*updated)]

    small_shapes = [given[n].shape for n in SMALL]
    ssum = _all_reduce_small(_pack_small([grads[n] for n in SMALL], loss[0, 0]), "reduce_vectors")
    small_out = _adamw(ssum[None], _pack_small([given[n] for n in SMALL]), _pack_small([given["m_" + n] for n in SMALL]),
                       _pack_small([given["v_" + n] for n in SMALL]), "adamw_vectors")
    total_loss = ssum.reshape(-1)[sum(int(np.prod(s)) for s in small_shapes)]
    small_out = [_unpack_small(o, small_shapes)[0] for o in small_out]

    names = ("meta", "norm_g", "final_g", "ev_w_in", "ev_q_norm_g", "ev_kv_norm_g", "ev_w_uq", "ev_w_ukv", "ev_w_out", "od_w_in", "od_sinks",
             "od_w_out")
    outs = [total_loss, grad_x]
    for kind in range(4):
        for n in names:
            outs.append(big_out[kind][n] if n in BIG else small_out[kind][SMALL.index(n)])
    return tuple(outs)
```

```python
import functools
import math

import numpy as np
import jax
import jax.numpy as jnp
from jax import lax
from jax.experimental import pallas as pl
from jax.experimental.pallas import tpu as pltpu
from jax.experimental.pallas import tpu_sc as plsc

F32 = jnp.float32
BF16 = jnp.bfloat16

D_MODEL = 1024
BLOCK = 128
N_META = 16
N_PAD = BLOCK - N_META
NORM_EPS = 1e-6
NEG = -1e30
HEAD = 64
SB_HEADS = 8
MLA_HEADS = 8
MLA_Q_LORA = 256
MLA_KV_LORA = 128
MLA_NOPE = 64
MLA_ROPE = 32
ROPE_BASE = 10000.0
SWA_HEADS = 16
SWA_KV_HEADS = 2
SWA_WINDOW = 128
N_CHIPS = 4
N_DEV = 8

ADAM_LR = 0.001
ADAM_B1 = 0.9
ADAM_B2 = 0.999
ADAM_EPS = 1e-08
ADAM_WD = 0.01
ADAM_STEP = 10

VMEM_LIMIT = 48 * 1024 * 1024

EV_G, EV_Q, EV_K, EV_V, EV_CQ, EV_CKV, EV_KR, EV_N = 0, 1024, 1536, 2048, 2560, 2816, 2944, 3072
OD_G, OD_Q, OD_K, OD_V, OD_N = 0, 1024, 2048, 2176, 2304


def _params(sem=None):
    return pltpu.CompilerParams(dimension_semantics=sem, vmem_limit_bytes=VMEM_LIMIT)


def _row_tile(m):
    for c in (544, 256):
        if m % c == 0:
            return c
    return 128


def _matmul_rows(m):
    for c in (1088, 1024, 768, 640, 512, 384, 256):
        if m % c == 0:
            return c
    return 128


def _dot(a, b):
    return jnp.dot(a.astype(BF16), b.astype(BF16), preferred_element_type=F32)


def _dot_nt(a, b):
    return lax.dot_general(a.astype(BF16), b.astype(BF16), (((1,), (1,)), ((), ())), preferred_element_type=F32)


def _dot_tn(a, b):
    return lax.dot_general(a.astype(BF16), b.astype(BF16), (((0,), (0,)), ((), ())), preferred_element_type=F32)


def _rms_fwd(h, g, name):
    t, d = h.shape
    tm = _row_tile(t)

    def body(h_ref, g_ref, o_ref):
        x = h_ref[...]
        r = lax.rsqrt(jnp.mean(x * x, axis=-1, keepdims=True) + NORM_EPS)
        o_ref[...] = ((x * r) * g_ref[...]).astype(o_ref.dtype)

    return pl.pallas_call(
        body, grid=(t // tm,),
        in_specs=[pl.BlockSpec((tm, d), lambda i: (i, 0)), pl.BlockSpec((1, d), lambda i: (0, 0))],
        out_specs=pl.BlockSpec((tm, d), lambda i: (i, 0)),
        out_shape=jax.ShapeDtypeStruct((t, d), BF16), compiler_params=_params(("parallel",)), name=name,
    )(h, g)


def _rms_bwd(h, g, dy, dres, name):
    t, d = h.shape
    tm = _row_tile(t)

    def body(h_ref, g_ref, dy_ref, dres_ref, dh_ref, dg_ref):
        @pl.when(pl.program_id(0) == 0)
        def _():
            dg_ref[...] = jnp.zeros_like(dg_ref)

        x = h_ref[...]
        r = lax.rsqrt(jnp.mean(x * x, axis=-1, keepdims=True) + NORM_EPS)
        xr = x * r
        dy_ = dy_ref[...]
        u = dy_ * g_ref[...]
        dh_ref[...] = dres_ref[...] + r * (u - xr * jnp.mean(u * xr, axis=-1, keepdims=True))
        dg_ref[...] += jnp.sum(dy_ * xr, axis=0, keepdims=True)

    row = pl.BlockSpec((tm, d), lambda i: (i, 0))
    vec = pl.BlockSpec((1, d), lambda i: (0, 0))
    return pl.pallas_call(
        body, grid=(t // tm,), in_specs=[row, vec, row, row], out_specs=[row, vec],
        out_shape=[jax.ShapeDtypeStruct((t, d), F32), jax.ShapeDtypeStruct((1, d), F32)],
        compiler_params=_params(("arbitrary",)), name=name,
    )(h, g, dy, dres)


def _col_tile(n):
    for c in (1024, 768, 640, 512, 384, 256, 128):
        if n % c == 0:
            return c
    return n


def _mm(a, w, name, res=None, out_dtype=F32, a_cols=None):
    m = a.shape[0]
    k, n = w.shape
    a_blk = 0 if a_cols is None else a_cols[0] // k
    assert a_cols is None or (a_cols[1] == k and a_cols[0] % k == 0)
    tm, tn = _matmul_rows(m), _col_tile(n)

    def body(*refs):
        if res is None:
            a_ref, w_ref, o_ref = refs
            acc = _dot(a_ref[...], w_ref[...])
        else:
            a_ref, w_ref, r_ref, o_ref = refs
            acc = r_ref[...] + _dot(a_ref[...], w_ref[...])
        o_ref[...] = acc.astype(o_ref.dtype)

    in_specs = [pl.BlockSpec((tm, k), lambda j, i: (i, a_blk)), pl.BlockSpec((k, tn), lambda j, i: (0, j))]
    args = [a, w]
    if res is not None:
        in_specs.append(pl.BlockSpec((tm, tn), lambda j, i: (i, j)))
        args.append(res)
    return pl.pallas_call(
        body, grid=(n // tn, m // tm), in_specs=in_specs, out_specs=pl.BlockSpec((tm, tn), lambda j, i: (i, j)),
        out_shape=jax.ShapeDtypeStruct((m, n), out_dtype), compiler_params=_params(("parallel", "parallel")), name=name,
    )(*args)


def _mm_nt(a, w, name):
    m, n = a.shape
    k = w.shape[0]
    tm, tk = _matmul_rows(m), _col_tile(k)

    def body(a_ref, w_ref, o_ref):
        o_ref[...] = _dot_nt(a_ref[...], w_ref[...])

    return pl.pallas_call(
        body, grid=(k // tk, m // tm),
        in_specs=[pl.BlockSpec((tm, n), lambda j, i: (i, 0)), pl.BlockSpec((tk, n), lambda j, i: (j, 0))],
        out_specs=pl.BlockSpec((tm, tk), lambda j, i: (i, j)),
        out_shape=jax.ShapeDtypeStruct((m, k), F32), compiler_params=_params(("parallel", "parallel")), name=name,
    )(a, w)


def _mm_tn(x, dy, name):
    m, k = x.shape
    n = dy.shape[1]
    tm, tn = _matmul_rows(m), _col_tile(n)

    def body(x_ref, dy_ref, o_ref):
        @pl.when(pl.program_id(1) == 0)
        def _():
            o_ref[...] = jnp.zeros_like(o_ref)

        o_ref[...] += _dot_tn(x_ref[...], dy_ref[...])

    return pl.pallas_call(
        body, grid=(n // tn, m // tm),
        in_specs=[pl.BlockSpec((tm, k), lambda j, i: (i, 0)), pl.BlockSpec((tm, tn), lambda j, i: (i, j))],
        out_specs=pl.BlockSpec((k, tn), lambda j, i: (0, j)),
        out_shape=jax.ShapeDtypeStruct((k, n), F32), compiler_params=_params(("parallel", "arbitrary")), name=name,
    )(x, dy)


def _silu_parts(g):
    s = 1.0 / (1.0 + jnp.exp(-g))
    return g * s, s * (1.0 + g * (1.0 - s))


def _gate_fwd(o_parts, proj, name):
    t = proj.shape[0]
    tm = _row_tile(t)
    w = D_MODEL // len(o_parts)

    def body(*refs):
        g_ref, o_ref = refs[-2], refs[-1]
        for p, r in enumerate(refs[:-2]):
            sil, _ = _silu_parts(g_ref[:, p * w:(p + 1) * w])
            o_ref[:, p * w:(p + 1) * w] = (r[...].astype(F32) * sil).astype(o_ref.dtype)

    return pl.pallas_call(
        body, grid=(t // tm,),
        in_specs=[pl.BlockSpec((tm, w), lambda i: (i, 0)) for _ in o_parts] + [pl.BlockSpec((tm, D_MODEL), lambda i: (i, 0))],
        out_specs=pl.BlockSpec((tm, D_MODEL), lambda i: (i, 0)),
        out_shape=jax.ShapeDtypeStruct((t, D_MODEL), BF16), compiler_params=_params(("parallel",)), name=name,
    )(*o_parts, proj)


def _gate_bwd(dao, o_parts, proj, name):
    t = proj.shape[0]
    tm = _row_tile(t)
    np_ = len(o_parts)
    w = D_MODEL // np_

    def body(*refs):
        dao_ref, g_ref = refs[0], refs[1 + np_]
        do_refs, dg_ref = refs[2 + np_:2 + 2 * np_], refs[-1]
        for p in range(np_):
            sl = slice(p * w, (p + 1) * w)
            sil, dsil = _silu_parts(g_ref[:, sl])
            da = dao_ref[:, sl]
            do_refs[p][...] = da * sil
            dg_ref[:, sl] = (da * refs[1 + p][...].astype(F32) * dsil).astype(dg_ref.dtype)

    full = pl.BlockSpec((tm, D_MODEL), lambda i: (i, 0))
    part = pl.BlockSpec((tm, w), lambda i: (i, 0))
    outs = pl.pallas_call(
        body, grid=(t // tm,), in_specs=[full] + [part] * np_ + [full], out_specs=[part] * np_ + [full],
        out_shape=[jax.ShapeDtypeStruct((t, w), F32)] * np_ + [jax.ShapeDtypeStruct((t, D_MODEL), BF16)],
        compiler_params=_params(("parallel",)), name=name,
    )(dao, *o_parts, proj)
    return outs[:np_], outs[np_]


def _loss_head(h2, gf, target, b, lp):
    d = h2.shape[1]
    nb = lp // BLOCK
    h3 = h2.reshape(b, lp, d)

    def body(h_ref, g_ref, t_ref, dh_ref, dg_ref, loss_ref):
        first = (pl.program_id(0) == 0) & (pl.program_id(1) == 0)

        @pl.when(first)
        def _():
            dg_ref[...] = jnp.zeros_like(dg_ref)
            loss_ref[...] = jnp.zeros_like(loss_ref)

        @pl.when(pl.program_id(1) == 0)
        def _():
            dh_ref[...] = jnp.zeros_like(dh_ref)

        @pl.when(pl.program_id(1) > 0)
        def _():
            x = h_ref[0]
            r = lax.rsqrt(jnp.mean(x * x, axis=-1, keepdims=True) + NORM_EPS)
            xr = x * r
            g = g_ref[...]
            diff = xr * g - t_ref[0]
            loss_ref[...] += 0.5 * jnp.sum(jnp.mean(diff * diff, axis=-1, keepdims=True))
            dy = diff * (1.0 / d)
            u = dy * g
            dh_ref[0] = r * (u - xr * jnp.mean(u * xr, axis=-1, keepdims=True))
            dg_ref[...] += jnp.sum(dy * xr, axis=0, keepdims=True)

    blk = pl.BlockSpec((1, BLOCK, d), lambda bi, n: (bi, n, 0))
    dh, dg, loss = pl.pallas_call(
        body, grid=(b, nb),
        in_specs=[blk, pl.BlockSpec((1, d), lambda bi, n: (0, 0)),
                  pl.BlockSpec((1, BLOCK, d), lambda bi, n: (bi, jnp.maximum(n - 1, 0), 0))],
        out_specs=[blk, pl.BlockSpec((1, d), lambda bi, n: (0, 0)), pl.BlockSpec((8, 128), lambda bi, n: (0, 0))],
        out_shape=[jax.ShapeDtypeStruct((b, lp, d), F32), jax.ShapeDtypeStruct((1, d), F32), jax.ShapeDtypeStruct((8, 128), F32)],
        compiler_params=_params(("arbitrary", "arbitrary")), name="loss_head",
    )(h3, gf, target)
    return dh.reshape(b * lp, d), dg, loss


def _iota2(shape, dim):
    return lax.broadcasted_iota(jnp.int32, shape, dim)


KEYS = 512
SB_FWD_KEYS = 512


def _lo_lanes():
    return _iota2((1, BLOCK), 1) < HEAD


def _halves(x, lo):
    zero = jnp.zeros_like(x)
    return jnp.where(lo, x, zero), jnp.where(lo, zero, x)


def _rows_of_pair(a, b):
    return jnp.where(_iota2((BLOCK, 1), 0) < HEAD, a, b)


def _split_rows_t(x):
    xt = x.T
    first = _iota2(xt.shape, 0) < HEAD
    zero = jnp.zeros_like(xt)
    return jnp.concatenate([jnp.where(first, xt, zero), jnp.where(first, zero, xt)], axis=1).astype(BF16)


def _key_chunk(c, lp, t_idx, strict, key_axis):
    keys = t_idx.shape[key_axis]
    first = c * keys
    s0 = pl.multiple_of(jnp.minimum(first, lp - keys), BLOCK)
    s_idx = s0 + _iota2(t_idx.shape, key_axis)
    seen = (s_idx < t_idx) if strict else (s_idx <= t_idx)
    return s0, seen & (s_idx >= jnp.maximum(first, N_PAD))


def _tri_dot(x, tri):
    return jnp.dot(x.astype(BF16), tri, preferred_element_type=F32)


def _stack_halves(x, lo):
    a, b = _halves(x, lo)
    return jnp.concatenate([a, b], axis=0)


def _pair(a, b, lo):
    return jnp.where(lo, a, b)


def _chunk_starts(lp):
    return [min(c * KEYS, lp - KEYS) for c in range(-(-lp // KEYS))]


def _put_rows(ref, r0, bq, a, b):
    for t in range(bq // BLOCK):
        part = slice(t * BLOCK, (t + 1) * BLOCK)
        ref[0, 0, r0 // BLOCK + t] = jnp.concatenate([a[:, part], b[:, part], jnp.zeros((6, BLOCK), F32)], axis=0)


def _get_rows(ref, r0, bq):
    return [jnp.concatenate([ref[0, 0, r0 // BLOCK + t, h:h + 1, :] for t in range(bq // BLOCK)], axis=1) for h in range(2)]


def _n_chunks(i, keys):
    return ((i + 1) * BLOCK + keys - 1) // keys


QROWS = 512


def _for_query_tiles(nb, tile, keys=KEYS):
    per = QROWS // BLOCK

    def step(j, _):
        tile(pl.multiple_of(j * QROWS, QROWS), QROWS, (j + 1) * (QROWS // keys))
        return 0

    lax.fori_loop(0, nb // per, step, 0)
    for i in range(nb - nb % per, nb):
        tile(i * BLOCK, BLOCK, _n_chunks(i, keys))


def _walk_chunks(r0, n, chunk, carry, leftwards=False, keys=KEYS):
    diag = jnp.maximum(r0 // keys, 1)

    def span(first, last, masked, carry):
        def step(t, cr):
            return chunk(last - 1 - t if leftwards else first + t, cr, masked)
        return lax.fori_loop(0, last - first, step, carry)

    spans = [(0, 1, True), (1, diag, False), (diag, n, True)]
    for first, last, masked in (reversed(spans) if leftwards else spans):
        carry = span(first, last, masked, carry)
    return carry


def _where(valid, x, other):
    return x if valid is None else jnp.where(valid, x, other)


HEAD_SCALE = HEAD ** -0.5
assert math.frexp(HEAD_SCALE)[0] == 0.5


def _sb_scores(q_h, k, valid, after):
    z = _dot_nt(q_h, k)
    lb = jnp.minimum(z, 0.0) - jnp.log(1.0 + jnp.exp(-jnp.abs(z)))
    l1m_all = lb - z
    l1m = _where(valid, l1m_all, 0.0)
    return lb, l1m_all, l1m, _tri_dot(l1m, after)


def _pair_stat_spec(nb):
    return pl.BlockSpec((1, 1, nb, 8, BLOCK), lambda bi, hp: (bi, hp, 0, 0, 0))


def _sb_fwd(proj3, name):
    b, lp, _ = proj3.shape
    nb = lp // BLOCK
    npair = SB_HEADS // 2

    keys = SB_FWD_KEYS

    def body(q_ref, k_ref, v_ref, o_ref, tot_ref):
        lo = _lo_lanes()
        after = (_iota2((keys, keys), 0) > _iota2((keys, keys), 1)).astype(BF16)

        def qtile(r0, bq, n):
            qs = _halves((q_ref[0, pl.ds(r0, bq), :] * HEAD_SCALE).astype(BF16), lo)
            t_idx = r0 + _iota2((bq, keys), 0)

            def kchunk(c, carry, masked):
                cs, acc = carry[:2], carry[2]
                s0, valid = _key_chunk(c, lp, t_idx, True, 1)
                valid = valid if masked else None
                k = k_ref[0, pl.ds(s0, keys), :].astype(BF16)
                a_s, new = [], []
                for h in range(2):
                    lb, _, l1m, suf = _sb_scores(qs[h], k, valid, after)
                    a_s.append(_where(valid, jnp.exp(lb + suf + cs[h]), 0.0).astype(BF16))
                    new.append(cs[h] + jnp.sum(l1m, axis=1, keepdims=True))
                v_bd = _stack_halves(v_ref[0, pl.ds(s0, keys), :].astype(BF16), lo)
                return (*new, acc + jnp.dot(jnp.concatenate(a_s, axis=1), v_bd, preferred_element_type=F32))

            zero = jnp.zeros((bq, 1), F32)
            c_a, c_b, acc = _walk_chunks(r0, n, kchunk, (zero, zero, jnp.zeros((bq, BLOCK), F32)), leftwards=True, keys=keys)
            o_ref[0, pl.ds(r0, bq), :] = acc
            tot_ref[0, pl.ds(r0, bq), :] = jnp.broadcast_to(_pair(c_a, c_b, lo), (bq, BLOCK))

        _for_query_tiles(nb, qtile, keys)

    def col(first):
        return pl.BlockSpec((1, lp, 2 * HEAD), lambda bi, hp: (bi, 0, first // (2 * HEAD) + hp))

    shp = jax.ShapeDtypeStruct((b, lp, SB_HEADS * HEAD), F32)
    return pl.pallas_call(
        body, grid=(b, npair), in_specs=[col(EV_Q), col(EV_K), col(EV_V)], out_specs=[col(0), col(0)], out_shape=[shp, shp],
        compiler_params=_params(("parallel", "parallel")), name=name,
    )(proj3, proj3, proj3)


def _sb_bwd(proj3, tot, do, name):
    b, lp, _ = proj3.shape
    nb = lp // BLOCK
    npair = SB_HEADS // 2

    def body(q_ref, k_ref, v_ref, tot_ref, do_ref, dq_ref, dk_ref, dv_ref):
        lo = _lo_lanes()
        after = (_iota2((KEYS, KEYS), 0) > _iota2((KEYS, KEYS), 1)).astype(BF16)
        before = (_iota2((KEYS, KEYS), 0) < _iota2((KEYS, KEYS), 1)).astype(BF16)
        dk_ref[...] = jnp.zeros_like(dk_ref)
        dv_ref[...] = jnp.zeros_like(dv_ref)

        def qtile(r0, bq, n):
            rows = pl.ds(r0, bq)
            qs = _halves((q_ref[0, rows, :] * HEAD_SCALE).astype(BF16), lo)
            dos = _halves(do_ref[0, rows, :].astype(BF16), lo)
            tot_i = tot_ref[0, rows, :]
            tots = (tot_i[:, 0:1], tot_i[:, HEAD:HEAD + 1])
            q_st, do_st = jnp.concatenate(qs, axis=0), jnp.concatenate(dos, axis=0)
            t_idx = r0 + _iota2((bq, KEYS), 0)

            def kchunk(c, carry, masked):
                s0, valid = _key_chunk(c, lp, t_idx, True, 1)
                valid = valid if masked else None
                keys = pl.ds(s0, KEYS)
                k = k_ref[0, keys, :].astype(BF16)
                v = v_ref[0, keys, :].astype(BF16)
                a_s, dzs, new = [], [], []
                for h in range(2):
                    left, pre = carry[2 * h], carry[2 * h + 1]
                    lb, l1m_all, l1m, suf = _sb_scores(qs[h], k, valid, after)
                    here = jnp.sum(l1m, axis=1, keepdims=True)
                    a = _where(valid, jnp.exp(lb + suf + (tots[h] - left - here)), 0.0)
                    w = a * _dot_nt(dos[h], v)
                    dz = _where(valid, w * jnp.exp(l1m_all) - (pre + _tri_dot(w, before)) * jnp.exp(lb), 0.0)
                    new += [left + here, pre + jnp.sum(w, axis=1, keepdims=True)]
                    a_s.append(a.astype(BF16))
                    dzs.append(dz.astype(BF16))
                dk_ref[0, keys, :] += _dot_tn(jnp.concatenate(dzs, axis=0), q_st)
                dv_ref[0, keys, :] += _dot_tn(jnp.concatenate(a_s, axis=0), do_st)
                dq = carry[4] + jnp.dot(jnp.concatenate(dzs, axis=1), _stack_halves(k, lo), preferred_element_type=F32)
                return (*new, dq)

            zero = jnp.zeros((bq, 1), F32)
            out = _walk_chunks(r0, n, kchunk, (zero, zero, zero, zero, jnp.zeros((bq, BLOCK), F32)))
            dq_ref[0, rows, :] = out[4] * HEAD_SCALE

        _for_query_tiles(nb, qtile)

    def col(first):
        return pl.BlockSpec((1, lp, 2 * HEAD), lambda bi, hp: (bi, 0, first // (2 * HEAD) + hp))

    shp = jax.ShapeDtypeStruct((b, lp, SB_HEADS * HEAD), F32)
    return pl.pallas_call(
        body, grid=(b, npair), in_specs=[col(EV_Q), col(EV_K), col(EV_V), col(0), col(0)], out_specs=[col(0)] * 3, out_shape=[shp] * 3,
        compiler_params=_params(("parallel", "parallel")), name=name,
    )(proj3, proj3, proj3, tot, do)


def _rope_tables(lp):
    half = MLA_ROPE // 2
    pos = (np.arange(lp) - N_PAD).astype(np.float32)
    inv = jnp.asarray(ROPE_BASE, F32) ** (-jnp.arange(half, dtype=F32) / half)
    ang = jnp.asarray(pos)[:, None] * inv[None, :]
    cos, sin = jnp.cos(ang), jnp.sin(ang)
    zeros = lambda n: jnp.zeros((lp, n), F32)
    c = jnp.concatenate([jnp.ones((lp, MLA_NOPE), F32), cos, cos, zeros(32)], axis=1)
    s1 = jnp.concatenate([zeros(MLA_NOPE), -sin, zeros(half), zeros(32)], axis=1)
    s2 = jnp.concatenate([zeros(MLA_NOPE), zeros(half), sin, zeros(32)], axis=1)
    return c, s1, s2


def _rope(x, c, s1, s2):
    half = MLA_ROPE // 2
    return x * c + pltpu.roll(x, BLOCK - half, 1) * s1 + pltpu.roll(x, half, 1) * s2


def _rope_t(dy, c, s1, s2):
    half = MLA_ROPE // 2
    return dy * c + pltpu.roll(dy * s1, half, 1) + pltpu.roll(dy * s2, BLOCK - half, 1)


def _rms_rows(x, g):
    r = lax.rsqrt(jnp.mean(x * x, axis=-1, keepdims=True) + NORM_EPS)
    return x * r, r


def _prep_rows(lp):
    return lp // 2 if lp % 32 == 0 else BLOCK


def _mla_prep_fwd(proj3, gq, gkv, wq, wk, wv, tabs, name):
    b, lp, _ = proj3.shape
    rows = _prep_rows(lp)
    hw = MLA_HEADS * BLOCK

    def body(cq_ref, ckv_ref, kr_ref, gq_ref, gkv_ref, wq_ref, wk_ref, wv_ref, c_ref, s1_ref, s2_ref, qf_ref, kf_ref, v_ref):
        c, s1, s2 = c_ref[...], s1_ref[...], s2_ref[...]
        xq, _ = _rms_rows(cq_ref[0], None)
        qh = _dot(xq * gq_ref[...], wq_ref[...])
        xk, _ = _rms_rows(ckv_ref[0], None)
        ckv_n = xk * gkv_ref[...]
        kv = _dot(ckv_n, wk_ref[...])
        v_ref[0] = _dot(ckv_n, wv_ref[...]).astype(v_ref.dtype)
        kr = _rope(kr_ref[0], c, s1, s2)
        for h in range(MLA_HEADS):
            ls = slice(h * BLOCK, (h + 1) * BLOCK)
            qf_ref[0, :, ls] = _rope(qh[:, ls], c, s1, s2).astype(qf_ref.dtype)
            kf_ref[0, :, ls] = (kv[:, ls] + kr).astype(kf_ref.dtype)

    def col(first, width):
        return pl.BlockSpec((1, rows, width), lambda bi, n: (bi, n, first // width))

    def whole(a):
        return pl.BlockSpec(a.shape, lambda bi, n: (0,) * a.ndim)

    tab = pl.BlockSpec((rows, BLOCK), lambda bi, n: (n, 0))
    return pl.pallas_call(
        body, grid=(b, lp // rows),
        in_specs=[col(EV_CQ, MLA_Q_LORA), col(EV_CKV, MLA_KV_LORA), col(EV_KR, BLOCK), whole(gq), whole(gkv), whole(wq), whole(wk),
                  whole(wv), tab, tab, tab],
        out_specs=[col(0, hw), col(0, hw), col(0, MLA_HEADS * HEAD)],
        out_shape=[jax.ShapeDtypeStruct((b, lp, hw), BF16), jax.ShapeDtypeStruct((b, lp, hw), BF16),
                   jax.ShapeDtypeStruct((b, lp, MLA_HEADS * HEAD), BF16)],
        compiler_params=_params(("parallel", "parallel")), name=name,
    )(proj3, proj3, proj3, gq, gkv, wq, wk, wv, *tabs)


def _mla_prep_bwd(proj3, gq, gkv, wq, wk, wv, tabs, dqf, dkf, dv, name):
    b, lp, _ = proj3.shape
    rows = _prep_rows(lp)
    hw = MLA_HEADS * BLOCK

    def body(cq_ref, ckv_ref, gq_ref, gkv_ref, wq_ref, wk_ref, wv_ref, c_ref, s1_ref, s2_ref, dqf_ref, dkf_ref, dv_ref,
             dcq_ref, dckv_ref, dkr_ref, dwq_ref, dwk_ref, dwv_ref, dgq_ref, dgkv_ref, dqh):
        @pl.when((pl.program_id(0) == 0) & (pl.program_id(1) == 0))
        def _():
            for r in (dwq_ref, dwk_ref, dwv_ref, dgq_ref, dgkv_ref):
                r[...] = jnp.zeros_like(r)

        c, s1, s2 = c_ref[...], s1_ref[...], s2_ref[...]
        dkr = jnp.zeros((rows, BLOCK), F32)
        for h in range(MLA_HEADS):
            ls = slice(h * BLOCK, (h + 1) * BLOCK)
            dqh[:, ls] = _rope_t(dqf_ref[0, :, ls].astype(F32), c, s1, s2).astype(dqh.dtype)
            dkr = dkr + dkf_ref[0, :, ls].astype(F32)
        dkr_ref[0] = _rope_t(dkr, c, s1, s2).astype(dkr_ref.dtype)

        def norm_bwd(x, g, dy, dg_ref):
            xr, r = _rms_rows(x, None)
            u = dy * g
            dg_ref[...] += jnp.sum(dy * xr, axis=0, keepdims=True)
            return r * (u - xr * jnp.mean(u * xr, axis=-1, keepdims=True))

        xq, _ = _rms_rows(cq_ref[0], None)
        cq_n = xq * gq_ref[...]
        dwq_ref[...] += _dot_tn(cq_n, dqh[...])
        dcq_ref[0] = norm_bwd(cq_ref[0], gq_ref[...], _dot_nt(dqh[...], wq_ref[...]), dgq_ref).astype(dcq_ref.dtype)
        xk, _ = _rms_rows(ckv_ref[0], None)
        ckv_n = xk * gkv_ref[...]
        dkf_, dv_ = dkf_ref[0], dv_ref[0]
        dwk_ref[...] += _dot_tn(ckv_n, dkf_)
        dwv_ref[...] += _dot_tn(ckv_n, dv_)
        dckv_n = _dot_nt(dkf_, wk_ref[...]) + _dot_nt(dv_, wv_ref[...])
        dckv_ref[0] = norm_bwd(ckv_ref[0], gkv_ref[...], dckv_n, dgkv_ref).astype(dckv_ref.dtype)

    def col(first, width):
        return pl.BlockSpec((1, rows, width), lambda bi, n: (bi, n, first // width))

    def whole(a):
        return pl.BlockSpec(a.shape, lambda bi, n: (0,) * len(a.shape))

    tab = pl.BlockSpec((rows, BLOCK), lambda bi, n: (n, 0))
    acc_shapes = [jax.ShapeDtypeStruct(a.shape, F32) for a in (wq, wk, wv, gq, gkv)]
    return pl.pallas_call(
        body, grid=(b, lp // rows),
        in_specs=[col(EV_CQ, MLA_Q_LORA), col(EV_CKV, MLA_KV_LORA), whole(gq), whole(gkv), whole(wq), whole(wk), whole(wv), tab, tab, tab,
                  col(0, hw), col(0, hw), col(0, MLA_HEADS * HEAD)],
        out_specs=[col(0, MLA_Q_LORA), col(0, MLA_KV_LORA), col(0, BLOCK)] + [whole(a) for a in acc_shapes],
        out_shape=[jax.ShapeDtypeStruct((b, lp, MLA_Q_LORA), BF16), jax.ShapeDtypeStruct((b, lp, MLA_KV_LORA), BF16),
                   jax.ShapeDtypeStruct((b, lp, BLOCK), BF16)] + acc_shapes,
        scratch_shapes=[pltpu.VMEM((rows, hw), BF16)],
        compiler_params=_params(("arbitrary", "arbitrary")), name=name,
    )(proj3, proj3, gq, gkv, wq, wk, wv, *tabs, dqf, dkf, dv)


def _mla_fwd(qf, kf, v, name):
    b, lp, _ = qf.shape
    nb = lp // BLOCK
    npair = MLA_HEADS // 2
    scale = (MLA_NOPE + MLA_ROPE) ** -0.5
    starts = _chunk_starts(lp)

    def body(q_ref, k_ref, v_ref, o_ref, lse_ref, vt_ref):
        for c, s0 in enumerate(starts):
            vt_ref[c] = _split_rows_t(v_ref[0, s0:s0 + KEYS, :].astype(F32))

        def qtile(r0, bq, n):
            qs = [q_ref[0, pl.ds(r0, bq), h * BLOCK:(h + 1) * BLOCK] for h in range(2)]
            t_idx = r0 + _iota2((KEYS, bq), 1)

            def kchunk(c, carry, masked):
                stats, acc = carry[:4], carry[4]
                s0, valid = _key_chunk(c, lp, t_idx, False, 0)
                valid = valid if masked else None
                ps, new, alphas = [], [], []
                for h in range(2):
                    m, l = stats[2 * h], stats[2 * h + 1]
                    s = _where(valid, _dot_nt(k_ref[0, pl.ds(s0, KEYS), h * BLOCK:(h + 1) * BLOCK], qs[h]) * scale, NEG)
                    m_new = jnp.maximum(m, jnp.max(s, axis=0, keepdims=True))
                    p = _where(valid, jnp.exp(s - m_new), 0.0)
                    alpha = jnp.exp(m - m_new)
                    new += [m_new, alpha * l + jnp.sum(p, axis=0, keepdims=True)]
                    alphas.append(alpha)
                    ps.append(p.astype(BF16))
                pv = jnp.dot(vt_ref[c], jnp.concatenate(ps, axis=0), preferred_element_type=F32)
                return (*new, _rows_of_pair(alphas[0], alphas[1]) * acc + pv)

            neg, zero = jnp.full((1, bq), NEG, F32), jnp.zeros((1, bq), F32)
            m_a, l_a, m_b, l_b, acc = _walk_chunks(r0, n, kchunk, (neg, zero, neg, zero, jnp.zeros((BLOCK, bq), F32)))
            safe = [jnp.where(l > 0.0, l, 1.0) for l in (l_a, l_b)]
            o_ref[0, pl.ds(r0, bq), :] = (acc / _rows_of_pair(safe[0], safe[1])).T
            lse = [jnp.where(l > 0.0, m + jnp.log(sf), 0.0) for m, l, sf in ((m_a, l_a, safe[0]), (m_b, l_b, safe[1]))]
            _put_rows(lse_ref, r0, bq, lse[0], lse[1])

        _for_query_tiles(nb, qtile)

    wide = pl.BlockSpec((1, lp, 2 * BLOCK), lambda bi, hp: (bi, 0, hp))
    thin = pl.BlockSpec((1, lp, 2 * HEAD), lambda bi, hp: (bi, 0, hp))
    return pl.pallas_call(
        body, grid=(b, npair), in_specs=[wide, wide, thin], out_specs=[thin, _pair_stat_spec(nb)],
        out_shape=[jax.ShapeDtypeStruct((b, lp, MLA_HEADS * HEAD), F32), jax.ShapeDtypeStruct((b, npair, nb, 8, BLOCK), F32)],
        scratch_shapes=[pltpu.VMEM((len(starts), BLOCK, 2 * KEYS), BF16)],
        compiler_params=_params(("parallel", "parallel")), name=name,
    )(qf, kf, v)


def _mla_bwd(qf, kf, v, o, lse, do, name):
    b, lp, _ = qf.shape
    nb = lp // BLOCK
    npair = MLA_HEADS // 2
    scale = (MLA_NOPE + MLA_ROPE) ** -0.5

    starts = _chunk_starts(lp)

    def body(q_ref, k_ref, v_ref, o_ref, lse_ref, do_ref, dq_ref, dk_ref, dv_ref, kt_ref):
        lo = _lo_lanes()
        dk_ref[...] = jnp.zeros_like(dk_ref)
        dv_ref[...] = jnp.zeros_like(dv_ref)
        for c, s0 in enumerate(starts):
            for h in range(2):
                kt_ref[c, h] = k_ref[0, s0:s0 + KEYS, h * BLOCK:(h + 1) * BLOCK].astype(F32).T.astype(BF16)

        def qtile(r0, bq, n):
            rows = pl.ds(r0, bq)
            qs = [q_ref[0, rows, h * BLOCK:(h + 1) * BLOCK] for h in range(2)]
            do_i = do_ref[0, rows, :]
            dos = _halves(do_i.astype(BF16), lo)
            do_st = jnp.concatenate(dos, axis=0)
            both = (do_i * o_ref[0, rows, :]).T
            dsum = (jnp.sum(both[:HEAD], axis=0, keepdims=True), jnp.sum(both[HEAD:], axis=0, keepdims=True))
            lses = _get_rows(lse_ref, r0, bq)
            t_idx = r0 + _iota2((KEYS, bq), 1)

            def kchunk(c, dqts, masked):
                s0, valid = _key_chunk(c, lp, t_idx, False, 0)
                valid = valid if masked else None
                keys = pl.ds(s0, KEYS)
                v_c = v_ref[0, keys, :]
                ps, out = [], []
                for h in range(2):
                    lanes = slice(h * BLOCK, (h + 1) * BLOCK)
                    s = _dot_nt(k_ref[0, keys, lanes], qs[h]) * scale
                    p = _where(valid, jnp.exp(s - lses[h]), 0.0)
                    ds = (p * (_dot_nt(v_c, dos[h]) - dsum[h]) * scale).astype(BF16)
                    dk_ref[0, keys, lanes] += jnp.dot(ds, qs[h], preferred_element_type=F32)
                    out.append(dqts[h] + jnp.dot(kt_ref[c, h], ds, preferred_element_type=F32))
                    ps.append(p.astype(BF16))
                dv_ref[0, keys, :] += jnp.dot(jnp.concatenate(ps, axis=1), do_st, preferred_element_type=F32)
                return tuple(out)

            zero = jnp.zeros((BLOCK, bq), F32)
            dq_a, dq_b = _walk_chunks(r0, n, kchunk, (zero, zero))
            dq_ref[0, rows, 0:BLOCK] = dq_a.T
            dq_ref[0, rows, BLOCK:2 * BLOCK] = dq_b.T

        _for_query_tiles(nb, qtile)

    wide = pl.BlockSpec((1, lp, 2 * BLOCK), lambda bi, hp: (bi, 0, hp))
    thin = pl.BlockSpec((1, lp, 2 * HEAD), lambda bi, hp: (bi, 0, hp))
    return pl.pallas_call(
        body, grid=(b, npair), in_specs=[wide, wide, thin, thin, _pair_stat_spec(nb), thin], out_specs=[wide, wide, thin],
        out_shape=[jax.ShapeDtypeStruct(qf.shape, F32), jax.ShapeDtypeStruct(qf.shape, F32), jax.ShapeDtypeStruct(v.shape, F32)],
        scratch_shapes=[pltpu.VMEM((len(starts), 2, BLOCK, KEYS), BF16)],
        compiler_params=_params(("parallel", "parallel")), name=name,
    )(qf, kf, v, o, lse, do)


SWA_KEYS = 2 * BLOCK + N_META


def _swa_keys(k_ref, v_ref, n, kv):
    prev = jnp.maximum(n - 1, 0)
    rows = lambda blk: pl.ds(pl.multiple_of(blk * BLOCK, BLOCK), BLOCK)
    mine = (_iota2((1, BLOCK), 1) >= HEAD).astype(jnp.int32) == kv

    def both_halves(ref):
        x = jnp.concatenate([ref[0, rows(prev), :], ref[0, rows(n), :], ref[0, N_PAD:BLOCK, :]], axis=0)
        return jnp.where(mine, x, pltpu.roll(x, HEAD, 1))

    slot = _iota2((SWA_KEYS, BLOCK), 0)
    s_idx = jnp.where(slot < 2 * BLOCK, (n - 1) * BLOCK + slot, slot - 2 * BLOCK + N_PAD)
    dist = n * BLOCK + _iota2((SWA_KEYS, BLOCK), 1) - s_idx
    band = (slot < 2 * BLOCK) & (dist >= 0) & (dist < SWA_WINDOW) & (s_idx >= BLOCK)
    meta = (slot >= 2 * BLOCK) & (dist >= 0)
    return both_halves(k_ref), both_halves(v_ref), band | meta, dist.astype(F32), prev


def _pad_keys(x):
    return jnp.concatenate([x, jnp.zeros((3 * BLOCK - SWA_KEYS, x.shape[1]), x.dtype)], axis=0)


def _swa_probs(q_h, kdup, valid, dist, head, sink_ref):
    slope = jnp.exp(jnp.full((1, 1), -8.0 * math.log(2.0) / SWA_HEADS, F32) * (head + 1).astype(F32))
    s = jnp.where(valid, _dot_nt(kdup, q_h) - slope * dist, NEG)
    sink = sink_ref[pl.ds(head, 1), 0:1]
    m = jnp.maximum(jnp.max(s, axis=0, keepdims=True), sink)
    e = jnp.where(valid, jnp.exp(s - m), 0.0)
    es = jnp.exp(sink - m)
    inv = 1.0 / (jnp.sum(e, axis=0, keepdims=True) + es)
    return e * inv, es * inv


SWA_PAIRS = SWA_HEADS // SWA_KV_HEADS // 2
SWA_GROUP = SWA_PAIRS * 2 * HEAD


def _swa_specs(b, lp):
    nb = lp // BLOCK
    qcol = lambda first: pl.BlockSpec((1, BLOCK, SWA_GROUP), lambda bi, kv, n: (bi, n, first // SWA_GROUP + kv))
    kcol = lambda first: pl.BlockSpec((1, lp, BLOCK), lambda bi, kv, n: (bi, 0, first // BLOCK))
    sink = pl.BlockSpec((SWA_HEADS, BLOCK), lambda bi, kv, n: (0, 0))
    return (b, SWA_KV_HEADS, nb), qcol, kcol, sink


def _swa_fwd(proj3, sinks, name):
    b, lp, _ = proj3.shape
    grid, qcol, kcol, sink = _swa_specs(b, lp)

    def body(q_ref, k_ref, v_ref, sink_ref, o_ref):
        kv, n = pl.program_id(1), pl.program_id(2)
        lo = _lo_lanes()
        kdup, vdup, valid, dist, _ = _swa_keys(k_ref, v_ref, n, kv)
        kdup = kdup.astype(BF16)
        vt = _split_rows_t(_pad_keys(vdup))
        for p in range(SWA_PAIRS):
            lanes = slice(p * BLOCK, (p + 1) * BLOCK)
            qs = _halves((q_ref[0, :, lanes] * HEAD_SCALE).astype(BF16), lo)
            probs = [_swa_probs(qs[hh], kdup, valid, dist, (kv * SWA_PAIRS + p) * 2 + hh, sink_ref)[0].astype(BF16) for hh in range(2)]
            o_ref[0, :, lanes] = jnp.dot(vt, jnp.concatenate([_pad_keys(pr) for pr in probs], axis=0), preferred_element_type=F32).T

    return pl.pallas_call(
        body, grid=grid, in_specs=[qcol(OD_Q), kcol(OD_K), kcol(OD_V), sink], out_specs=qcol(0),
        out_shape=jax.ShapeDtypeStruct((b, lp, SWA_HEADS * HEAD), F32),
        compiler_params=_params(("parallel", "parallel", "parallel")), name=name,
    )(proj3, proj3, proj3, sinks)


def _swa_bwd(proj3, sinks, do, name):
    b, lp, _ = proj3.shape
    nb = lp // BLOCK
    grid, qcol, kcol, sink = _swa_specs(b, lp)

    def body(q_ref, k_ref, v_ref, sink_ref, do_ref, dq_ref, dk_ref, dv_ref, dsink_ref, dk_acc, dv_acc):
        kv, n = pl.program_id(1), pl.program_id(2)

        @pl.when((n == 0) & (pl.program_id(0) == 0) & (kv == 0))
        def _():
            dsink_ref[...] = jnp.zeros_like(dsink_ref)

        @pl.when(n == 0)
        def _():
            dk_acc[...] = jnp.zeros_like(dk_acc)
            dv_acc[...] = jnp.zeros_like(dv_acc)

        lo = _lo_lanes()
        kdup, vdup, valid, dist, prev = _swa_keys(k_ref, v_ref, n, kv)
        kt = _split_rows_t(_pad_keys(kdup))
        kdup, vdup = kdup.astype(BF16), vdup.astype(BF16)
        dkc = jnp.zeros((SWA_KEYS, BLOCK), F32)
        dvc = jnp.zeros((SWA_KEYS, BLOCK), F32)
        for p in range(SWA_PAIRS):
            lanes = slice(p * BLOCK, (p + 1) * BLOCK)
            qs = _halves((q_ref[0, :, lanes] * HEAD_SCALE).astype(BF16), lo)
            dos = _halves(do_ref[0, :, lanes].astype(BF16), lo)
            dss, prs = [], []
            for hh in range(2):
                head = (kv * SWA_PAIRS + p) * 2 + hh
                pr, ps = _swa_probs(qs[hh], kdup, valid, dist, head, sink_ref)
                dp = _dot_nt(vdup, dos[hh])
                dsum = jnp.sum(pr * dp, axis=0, keepdims=True)
                dsink_ref[pl.ds(head, 1), :] += jnp.broadcast_to(-jnp.sum(ps * dsum, axis=1, keepdims=True), (1, BLOCK))
                dss.append((pr * (dp - dsum)).astype(BF16))
                prs.append(pr.astype(BF16))
            dq_ref[0, :, lanes] = jnp.dot(kt, jnp.concatenate([_pad_keys(d) for d in dss], axis=0), preferred_element_type=F32).T * HEAD_SCALE
            dkc = dkc + jnp.dot(jnp.concatenate(dss, axis=1), jnp.concatenate(qs, axis=0), preferred_element_type=F32)
            dvc = dvc + jnp.dot(jnp.concatenate(prs, axis=1), jnp.concatenate(dos, axis=0), preferred_element_type=F32)
        rows = lambda blk: pl.ds(pl.multiple_of(blk * BLOCK, BLOCK), BLOCK)
        for r, part in ((rows(prev), slice(0, BLOCK)), (rows(n), slice(BLOCK, 2 * BLOCK)), (slice(N_PAD, BLOCK), slice(2 * BLOCK, SWA_KEYS))):
            dk_acc[r, :] += dkc[part]
            dv_acc[r, :] += dvc[part]

        for acc, ref in ((dk_acc, dk_ref), (dv_acc, dv_ref)):
            @pl.when((n == nb - 1) & (kv == 0))
            def _():
                x = acc[...]
                ref[0] = x + pltpu.roll(x, HEAD, 1)

            @pl.when((n == nb - 1) & (kv == 1))
            def _():
                x = acc[...]
                ref[0] = jnp.where(lo, ref[0], x + pltpu.roll(x, HEAD, 1))

    kvout = pl.BlockSpec((1, lp, BLOCK), lambda bi, kv, n: (bi, 0, 0))
    kvshape = jax.ShapeDtypeStruct((b, lp, BLOCK), F32)
    return pl.pallas_call(
        body, grid=grid, in_specs=[qcol(OD_Q), kcol(OD_K), kcol(OD_V), sink, qcol(0)], out_specs=[qcol(0), kvout, kvout, sink],
        out_shape=[jax.ShapeDtypeStruct((b, lp, SWA_HEADS * HEAD), F32), kvshape, kvshape, jax.ShapeDtypeStruct((SWA_HEADS, BLOCK), F32)],
        scratch_shapes=[pltpu.VMEM((lp, BLOCK), F32), pltpu.VMEM((lp, BLOCK), F32)],
        compiler_params=_params(("arbitrary", "arbitrary", "arbitrary")), name=name,
    )(proj3, proj3, proj3, sinks, do)


def _kernel_weights(ev_w_in, ev_w_uq, ev_w_ukv, od_w_in):
    zeros = lambda r, c: jnp.zeros((r, c), ev_w_in.dtype)
    q_sb, k_sb, v_sb, g_sb, c_q, c_kv, k_r, g_mla = jnp.split(ev_w_in, [512, 1024, 1536, 2048, 2304, 2432, 2464], axis=1)
    w0 = jnp.concatenate([g_sb, g_mla, q_sb, k_sb, v_sb, c_q, c_kv, zeros(D_MODEL, MLA_NOPE), k_r, zeros(D_MODEL, 32)], axis=1)
    uq = ev_w_uq.reshape(MLA_Q_LORA, MLA_HEADS, MLA_NOPE + MLA_ROPE)
    wq = jnp.pad(uq, ((0, 0), (0, 0), (0, BLOCK - MLA_NOPE - MLA_ROPE))).reshape(MLA_Q_LORA, MLA_HEADS * BLOCK)
    ukv = ev_w_ukv.reshape(MLA_KV_LORA, MLA_HEADS, BLOCK)
    wk = jnp.pad(ukv[:, :, :MLA_NOPE], ((0, 0), (0, 0), (0, BLOCK - MLA_NOPE))).reshape(MLA_KV_LORA, MLA_HEADS * BLOCK)
    wv = ukv[:, :, MLA_NOPE:].reshape(MLA_KV_LORA, MLA_HEADS * HEAD)
    q, k, v, g = jnp.split(od_w_in, [1024, 1152, 1280], axis=1)
    w1 = jnp.concatenate([g, q, k, v], axis=1)
    return w0, wq, wk, wv, w1


def _od_w_in_grad(dw1):
    sl = lambda a, first, n: a[:, first:first + n]
    return jnp.concatenate([sl(dw1, OD_Q, 1024), sl(dw1, OD_K, 128), sl(dw1, OD_V, 128), sl(dw1, OD_G, 1024)], axis=1)


def _original_grads(dw0, dwq, dwk, dwv):
    sl = lambda a, first, n: a[:, first:first + n]
    d_ev_w_in = jnp.concatenate([sl(dw0, EV_Q, 512), sl(dw0, EV_K, 512), sl(dw0, EV_V, 512), sl(dw0, EV_G, 512), sl(dw0, EV_CQ, 256),
                                 sl(dw0, EV_CKV, 128), sl(dw0, EV_KR + MLA_NOPE, MLA_ROPE), sl(dw0, EV_G + 512, 512)], axis=1)
    d_uq = dwq.reshape(MLA_Q_LORA, MLA_HEADS, BLOCK)[:, :, :MLA_NOPE + MLA_ROPE].reshape(MLA_Q_LORA, -1)
    d_ukv = jnp.concatenate([dwk.reshape(MLA_KV_LORA, MLA_HEADS, BLOCK)[:, :, :MLA_NOPE], dwv.reshape(MLA_KV_LORA, MLA_HEADS, HEAD)],
                            axis=2).reshape(MLA_KV_LORA, -1)
    return d_ev_w_in, d_uq, d_ukv


def _meta_rows_sum(dh0_3):
    b, _, d = dh0_3.shape

    def body(x_ref, o_ref):
        acc = x_ref[0, N_PAD:BLOCK, :]
        for i in range(1, b):
            acc = acc + x_ref[i, N_PAD:BLOCK, :]
        o_ref[...] = acc

    return pl.pallas_call(
        body, grid=(1,), in_specs=[pl.BlockSpec((b, BLOCK, d), lambda i: (0, 0, 0))], out_specs=pl.BlockSpec((N_META, d), lambda i: (0, 0)),
        out_shape=jax.ShapeDtypeStruct((N_META, d), F32), compiler_params=_params(("arbitrary",)), name="meta_rows_sum",
    )(dh0_3)


def _local_step(x, meta, norm_g, final_g, gq, gkv, sinks, target, ev_w_in, ev_w_uq, ev_w_ukv, wo0, od_w_in, wo1, reduce_early=None):
    b, seq, d = x.shape
    lp = seq + BLOCK
    t = b * lp
    w0, wq, wk, wv, w1 = _kernel_weights(ev_w_in, ev_w_uq, ev_w_ukv, od_w_in)
    h0 = jnp.concatenate([jnp.zeros((b, N_PAD, d), F32), jnp.broadcast_to(meta[None], (b, N_META, d)), x], axis=1).reshape(t, d)
    tabs = _rope_tables(lp)
    g0, g1 = norm_g[0:1], norm_g[1:2]

    hn0 = _rms_fwd(h0, g0, "norm0")
    proj0 = _mm(hn0, w0, "inproj0")
    p0 = proj0.reshape(b, lp, EV_N)
    o_sb, sb_tot = _sb_fwd(p0, "sb_fwd")
    qf, kf, v = _mla_prep_fwd(p0, gq, gkv, wq, wk, wv, tabs, "mla_prep_fwd")
    o_mla, lse = _mla_fwd(qf, kf, v, "mla_fwd")
    o0 = [o_sb.reshape(t, -1), o_mla.reshape(t, -1)]
    ao0 = _gate_fwd(o0, proj0, "gate0")
    h1 = _mm(ao0, wo0, "outproj0", res=h0)

    hn1 = _rms_fwd(h1, g1, "norm1")
    proj1 = _mm(hn1, w1, "inproj1")
    p1 = proj1.reshape(b, lp, OD_N)
    sinks_b = jnp.broadcast_to(sinks.reshape(SWA_HEADS, 1), (SWA_HEADS, BLOCK))
    o1 = _swa_fwd(p1, sinks_b, "swa_fwd").reshape(t, -1)
    ao1 = _gate_fwd([o1], proj1, "gate1")
    h2 = _mm(ao1, wo1, "outproj1", res=h1)

    dh2, d_final_g, loss = _loss_head(h2, final_g.reshape(1, d), target, b, lp)

    d_wo1 = _mm_tn(ao1, dh2, "d_wo1")
    dao1 = _mm_nt(dh2, wo1, "d_ao1")
    (do1,), dg1 = _gate_bwd(dao1, [o1], proj1, "gate1_bwd")
    dq1, dk4, dv4, d_sinks = _swa_bwd(p1, sinks_b, do1.reshape(b, lp, -1), "swa_bwd")
    unheads = lambda a: a.reshape(t, SWA_KV_HEADS * HEAD).astype(BF16)
    dproj1 = jnp.concatenate([dg1, dq1.reshape(t, -1).astype(BF16), unheads(dk4), unheads(dv4)], axis=1)
    d_w1 = _mm_tn(hn1, dproj1, "d_w1")
    dhn1 = _mm_nt(dproj1, w1, "d_hn1")
    dh1, d_g1 = _rms_bwd(h1, g1, dhn1, dh2, "norm1_bwd")

    d_wo0 = _mm_tn(ao0, dh1, "d_wo0")
    early = dict(od_w_in=_od_w_in_grad(d_w1), ev_w_out=d_wo0, od_w_out=d_wo1)
    dao0 = _mm_nt(dh1, wo0, "d_ao0")
    if reduce_early is not None:
        pending, finish = reduce_early(early)
        dao0, pending = lax.optimization_barrier((dao0, pending))
        early_done = finish(pending)
    (do_sb, do_mla), dg0 = _gate_bwd(dao0, o0, proj0, "gate0_bwd")
    dq_sb, dk_sb, dv_sb = _sb_bwd(p0, sb_tot, do_sb.reshape(b, lp, -1), "sb_bwd")
    dqf, dkf, dv = _mla_bwd(qf, kf, v, o_mla, lse, do_mla.reshape(b, lp, -1), "mla_bwd")
    dcq, dckv, dkr, d_wq, d_wk, d_wv, d_gq, d_gkv = _mla_prep_bwd(p0, gq, gkv, wq, wk, wv, tabs, dqf, dkf, dv, "mla_prep_bwd")
    flat = lambda a: a.reshape(t, -1).astype(BF16)
    dproj0 = jnp.concatenate([dg0, flat(dq_sb), flat(dk_sb), flat(dv_sb), flat(dcq), flat(dckv), flat(dkr)], axis=1)
    if reduce_early is not None:
        dproj0, early_done = lax.optimization_barrier((dproj0, early_done))
    d_w0 = _mm_tn(hn0, dproj0, "d_w0")
    d_ev_w_in, d_uq, d_ukv = _original_grads(d_w0, d_wq, d_wk, d_wv)
    dhn0 = _mm_nt(dproj0, w0, "d_hn0")
    dh0, d_g0 = _rms_bwd(h0, g0, dhn0, dh1, "norm0_bwd")
    dh0 = dh0.reshape(b, lp, d)

    grads = dict(meta=_meta_rows_sum(dh0), norm_g=jnp.concatenate([d_g0, d_g1], axis=0), final_g=d_final_g.reshape(d),
                 ev_w_in=d_ev_w_in, ev_q_norm_g=d_gq, ev_kv_norm_g=d_gkv, ev_w_uq=d_uq, ev_w_ukv=d_ukv,
                 od_sinks=d_sinks[:, 0].reshape(1, SWA_HEADS))
    if reduce_early is None:
        return loss, dh0[:, BLOCK:], {**grads, **early}
    return loss, dh0[:, BLOCK:], grads, early_done


MESH = pl.DeviceIdType.MESH
ANY = pl.BlockSpec(memory_space=pl.ANY)


def _place():
    return lax.axis_index("x"), lax.axis_index("y"), lax.axis_index("c")


def _other_chips(x, y):
    return [(1 - x, y), (x, 1 - y), (1 - x, 1 - y)]


def _with_own_slot(slots, own):
    me = 2 * lax.axis_index("x") + lax.axis_index("y")
    return lax.dynamic_update_slice(slots, own[None], (me,) + (0,) * own.ndim)


def _gather_weights(packs, meta, name):
    n = len(packs)

    def body(*refs):
        ins, m_ref, outs, mo_ref = refs[:n], refs[n], refs[n + 1:2 * n + 1], refs[2 * n + 1]
        send_sems, recv_sems = refs[2 * n + 2:]
        x, y, c = _place()
        me, sib = 2 * x + y, (x, y, 1 - c)
        chips = _other_chips(x, y)

        def copy(k, src, dst, to):
            return pltpu.make_async_remote_copy(src_ref=src, dst_ref=dst, send_sem=send_sems.at[k], recv_sem=recv_sems.at[k], device_id=to,
                                                device_id_type=MESH)

        def half(i, chip, h):
            rows = packs[i].shape[0] // 2
            return outs[i].at[chip, pl.ds(h * rows, rows), :]

        def mine(i):
            rows = packs[i].shape[0] // 2
            return ins[i].at[pl.ds(c * rows, rows), :]

        sent = [copy(6 * i + k, mine(i), half(i, me, c), (px, py, c)) for i in range(n) for k, (px, py) in enumerate(chips)]
        sent += [copy(6 * n + k, m_ref, mo_ref.at[me], (px, py, c)) for k, (px, py) in enumerate(chips)]
        for cp in sent:
            cp.start()
        for i in range(n):
            for k, (px, py) in enumerate(chips):
                landed = half(i, 2 * px + py, c)
                copy(6 * i + k, mine(i), landed, (px, py, c)).wait_recv()
                fwd = copy(6 * i + 3 + k, landed, landed, sib)
                fwd.start()
                sent.append(fwd)
        for k, (px, py) in enumerate(chips):
            for i in range(n):
                other = half(i, 2 * px + py, 1 - c)
                copy(6 * i + 3 + k, other, other, sib).wait_recv()
            copy(6 * n + k, m_ref, mo_ref.at[2 * px + py], (px, py, c)).wait_recv()
        for cp in sent:
            cp.wait_send()

    nsem = 6 * n + 3
    res = pl.pallas_call(
        body, in_specs=[ANY] * (n + 1), out_specs=[ANY] * (n + 1),
        out_shape=[jax.ShapeDtypeStruct((N_CHIPS,) + a.shape, a.dtype) for a in list(packs) + [meta]],
        scratch_shapes=[pltpu.SemaphoreType.DMA((nsem,)), pltpu.SemaphoreType.DMA((nsem,))],
        name=name,
    )(*packs, meta)
    return [_with_own_slot(r, a) for r, a in zip(res[:n], packs)], _with_own_slot(res[n], meta)


def _grads_to_sibling(gs, name):
    n = len(gs)

    def body(*refs):
        ins, outs, send_sems, recv_sems = refs[:n], refs[n:2 * n], refs[2 * n], refs[2 * n + 1]
        x, y, c = _place()
        cps = []
        for i in range(n):
            rows = gs[i].shape[1] // 2
            cps.append(pltpu.make_async_remote_copy(src_ref=ins[i].at[:, pl.ds((1 - c) * rows, rows), :], dst_ref=outs[i],
                                                    send_sem=send_sems.at[i], recv_sem=recv_sems.at[i], device_id=(x, y, 1 - c),
                                                    device_id_type=MESH))
        for cp in cps:
            cp.start()
        for cp in cps:
            cp.wait()

    return pl.pallas_call(
        body, in_specs=[ANY] * n, out_specs=[ANY] * n,
        out_shape=[jax.ShapeDtypeStruct((g.shape[0], g.shape[1] // 2, g.shape[2]), g.dtype) for g in gs],
        scratch_shapes=[pltpu.SemaphoreType.DMA((n,)), pltpu.SemaphoreType.DMA((n,))],
        name=name,
    )(*gs)


def _share_halves(rs, name):
    n = len(rs)

    def body(*refs):
        ins, outs, send_sems, recv_sems = refs[:n], refs[n:2 * n], refs[2 * n], refs[2 * n + 1]
        x, y, c = _place()
        cps = [pltpu.make_async_remote_copy(src_ref=ins[i], dst_ref=outs[i], send_sem=send_sems.at[i], recv_sem=recv_sems.at[i],
                                            device_id=(x, y, 1 - c), device_id_type=MESH) for i in range(n)]
        for cp in cps:
            cp.start()
        for cp in cps:
            cp.wait()

    theirs = pl.pallas_call(
        body, in_specs=[ANY] * n, out_specs=[ANY] * n, out_shape=[jax.ShapeDtypeStruct(r.shape, r.dtype) for r in rs],
        scratch_shapes=[pltpu.SemaphoreType.DMA((n,)), pltpu.SemaphoreType.DMA((n,))],
        name=name,
    )(*rs)
    first = lax.axis_index("c") == 0
    return [jnp.where(first, jnp.concatenate([r, t], axis=0), jnp.concatenate([t, r], axis=0)) for r, t in zip(rs, theirs)]


def _chip_scatter(ss, name):
    n = len(ss)

    def body(*refs):
        ins, outs, send_sems, recv_sems = refs[:n], refs[n:2 * n], refs[2 * n], refs[2 * n + 1]
        x, y, c = _place()
        me = 2 * x + y
        chips = _other_chips(x, y)
        for i in range(n):
            for k, (px, py) in enumerate(chips):
                pltpu.make_async_remote_copy(src_ref=ins[i].at[2 * px + py], dst_ref=outs[i].at[me], send_sem=send_sems.at[3 * i + k],
                                             recv_sem=recv_sems.at[3 * i + k], device_id=(px, py, c), device_id_type=MESH).start()
        for i in range(n):
            for k, (px, py) in enumerate(chips):
                cp = pltpu.make_async_remote_copy(src_ref=ins[i].at[2 * px + py], dst_ref=outs[i].at[2 * px + py],
                                                  send_sem=send_sems.at[3 * i + k], recv_sem=recv_sems.at[3 * i + k],
                                                  device_id=(px, py, c), device_id_type=MESH)
                cp.wait_recv()
                cp.wait_send()

    parts = pl.pallas_call(
        body, in_specs=[ANY] * n, out_specs=[ANY] * n, out_shape=[jax.ShapeDtypeStruct(s.shape, s.dtype) for s in ss],
        scratch_shapes=[pltpu.SemaphoreType.DMA((3 * n,)), pltpu.SemaphoreType.DMA((3 * n,))],
        name=name,
    )(*ss)
    me = 2 * lax.axis_index("x") + lax.axis_index("y")
    return [_with_own_slot(p, lax.dynamic_index_in_dim(s, me, axis=0, keepdims=False)) for p, s in zip(parts, ss)]


HBM_SPACE = pltpu.MemorySpace.HBM


def _on_sequencer(name, collective_id, n_sems, body):
    @pl.kernel(mesh=plsc.ScalarSubcoreMesh(axis_name="sequencer", num_cores=1), name=name,
               scratch_types=(pltpu.SemaphoreType.DMA((n_sems,)), pltpu.SemaphoreType.DMA((n_sems,))),
               compiler_params=pltpu.CompilerParams(collective_id=collective_id))
    def launch(send_sems, recv_sems):
        body(send_sems, recv_sems)

    launch()


def _handshake(peers):
    barrier = pltpu.get_barrier_semaphore()
    for peer in peers:
        pl.semaphore_signal(barrier, inc=1, device_id=peer, device_id_type=MESH)
    pl.semaphore_wait(barrier, len(peers))


def _gather_on_sequencer(packs, name):
    n = len(packs)
    ins = [jax.new_ref(p, memory_space=HBM_SPACE) for p in packs]
    outs = [jax.empty_ref(jax.ShapeDtypeStruct((N_CHIPS,) + p.shape, p.dtype), memory_space=HBM_SPACE) for p in packs]

    def body(send_sems, recv_sems):
        x, y, c = _place()
        me, sib = 2 * x + y, (x, y, 1 - c)
        chips = _other_chips(x, y)
        _handshake([(px, py, c) for px, py in chips] + [sib])

        def copy(k, src, dst, to):
            return pltpu.make_async_remote_copy(src_ref=src, dst_ref=dst, send_sem=send_sems.at[k], recv_sem=recv_sems.at[k], device_id=to,
                                                device_id_type=MESH)

        def half(i, chip, h):
            rows = packs[i].shape[0] // 2
            return outs[i].at[chip, pl.ds(h * rows, rows), :]

        def mine(i):
            rows = packs[i].shape[0] // 2
            return ins[i].at[pl.ds(c * rows, rows), :]

        sent = [copy(6 * i + k, mine(i), half(i, me, c), (px, py, c)) for i in range(n) for k, (px, py) in enumerate(chips)]
        for cp in sent:
            cp.start()
        for i in range(n):
            for k, (px, py) in enumerate(chips):
                landed = half(i, 2 * px + py, c)
                copy(6 * i + k, mine(i), landed, (px, py, c)).wait_recv()
                fwd = copy(6 * i + 3 + k, landed, landed, sib)
                fwd.start()
                sent.append(fwd)
        for i in range(n):
            for k, (px, py) in enumerate(chips):
                other = half(i, 2 * px + py, 1 - c)
                copy(6 * i + 3 + k, other, other, sib).wait_recv()
        for cp in sent:
            cp.wait_send()

    _on_sequencer(name, 1, 6 * n, body)
    return [_with_own_slot(o[...], p) for o, p in zip(outs, packs)]


def _grads_to_sibling_on_sequencer(gs, name, collective_id):
    n = len(gs)
    ins = [jax.new_ref(g, memory_space=HBM_SPACE) for g in gs]
    outs = [jax.empty_ref(jax.ShapeDtypeStruct((g.shape[0], g.shape[1] // 2, g.shape[2]), g.dtype), memory_space=HBM_SPACE) for g in gs]

    def body(send_sems, recv_sems):
        x, y, c = _place()
        _handshake([(x, y, 1 - c)])
        cps = []
        for i in range(n):
            rows = gs[i].shape[1] // 2
            cps.append(pltpu.make_async_remote_copy(src_ref=ins[i].at[:, pl.ds((1 - c) * rows, rows), :], dst_ref=outs[i],
                                                    send_sem=send_sems.at[i], recv_sem=recv_sems.at[i], device_id=(x, y, 1 - c),
                                                    device_id_type=MESH))
        for cp in cps:
            cp.start()
        for cp in cps:
            cp.wait()

    _on_sequencer(name, collective_id, n, body)
    return [o[...] for o in outs]


def _chip_scatter_on_sequencer(ss, name, collective_id):
    n = len(ss)
    ins = [jax.new_ref(s, memory_space=HBM_SPACE) for s in ss]
    outs = [jax.empty_ref(jax.ShapeDtypeStruct(s.shape, s.dtype), memory_space=HBM_SPACE) for s in ss]

    def body(send_sems, recv_sems):
        x, y, c = _place()
        me = 2 * x + y
        chips = _other_chips(x, y)
        _handshake([(px, py, c) for px, py in chips])
        for i in range(n):
            for k, (px, py) in enumerate(chips):
                pltpu.make_async_remote_copy(src_ref=ins[i].at[2 * px + py], dst_ref=outs[i].at[me], send_sem=send_sems.at[3 * i + k],
                                             recv_sem=recv_sems.at[3 * i + k], device_id=(px, py, c), device_id_type=MESH).start()
        for i in range(n):
            for k, (px, py) in enumerate(chips):
                cp = pltpu.make_async_remote_copy(src_ref=ins[i].at[2 * px + py], dst_ref=outs[i].at[2 * px + py],
                                                  send_sem=send_sems.at[3 * i + k], recv_sem=recv_sems.at[3 * i + k],
                                                  device_id=(px, py, c), device_id_type=MESH)
                cp.wait_recv()
                cp.wait_send()

    _on_sequencer(name, collective_id, 3 * n, body)
    me = 2 * lax.axis_index("x") + lax.axis_index("y")
    return [_with_own_slot(o[...], lax.dynamic_index_in_dim(s, me, axis=0, keepdims=False)) for o, s in zip(outs, ss)]


def _all_reduce_small(v, name):
    shape = v.shape

    def body(v_ref, o_ref, slots, send_sems, recv_sems):
        x, y, c = _place()
        me = 4 * x + 2 * y + c
        slots[me] = v_ref[...]
        for r in range(1, N_DEV):
            peer = (x ^ (r >> 2), y ^ ((r >> 1) & 1), c ^ (r & 1))
            pltpu.make_async_remote_copy(src_ref=v_ref, dst_ref=slots.at[me], send_sem=send_sems.at[r - 1], recv_sem=recv_sems.at[r - 1],
                                         device_id=peer, device_id_type=MESH).start()
        for r in range(1, N_DEV):
            peer = (x ^ (r >> 2), y ^ ((r >> 1) & 1), c ^ (r & 1))
            cp = pltpu.make_async_remote_copy(src_ref=v_ref, dst_ref=slots.at[4 * peer[0] + 2 * peer[1] + peer[2]], send_sem=send_sems.at[r - 1],
                                              recv_sem=recv_sems.at[r - 1], device_id=peer, device_id_type=MESH)
            cp.wait_recv()
            cp.wait_send()
        acc = slots[0]
        for d in range(1, N_DEV):
            acc = acc + slots[d]
        o_ref[...] = acc

    vm = pl.BlockSpec(memory_space=pltpu.VMEM)
    return pl.pallas_call(
        body, in_specs=[vm], out_specs=vm, out_shape=jax.ShapeDtypeStruct(shape, F32),
        scratch_shapes=[pltpu.VMEM((N_DEV,) + shape, F32), pltpu.SemaphoreType.DMA((N_DEV - 1,)), pltpu.SemaphoreType.DMA((N_DEV - 1,))],
        name=name,
    )(v)


def _add_sibling(g, gsib, core, name):
    n, _, cdim = g.shape
    half = gsib.shape[1]
    tr = half // 2

    def body(core_ref, a_ref, b_ref, o_ref):
        o_ref[...] = (a_ref[...] + b_ref[...]).astype(o_ref.dtype)

    blk = pl.BlockSpec((1, tr, cdim), lambda j, i, core_ref: (j, i, 0))
    return pl.pallas_call(
        body,
        grid_spec=pltpu.PrefetchScalarGridSpec(
            num_scalar_prefetch=1, grid=(n, half // tr),
            in_specs=[pl.BlockSpec((1, tr, cdim), lambda j, i, core_ref: (j, core_ref[0] * (half // tr) + i, 0)), blk], out_specs=blk),
        out_shape=jax.ShapeDtypeStruct(gsib.shape, BF16), compiler_params=_params(("parallel", "parallel")), name=name,
    )(core, g, gsib)


def _sum_parts(parts, name):
    n, r, cdim = parts.shape
    tr = r // 2

    def body(p_ref, o_ref):
        acc = p_ref[0].astype(F32)
        for j in range(1, n):
            acc = acc + p_ref[j].astype(F32)
        o_ref[...] = acc

    return pl.pallas_call(
        body, grid=(r // tr,), in_specs=[pl.BlockSpec((n, tr, cdim), lambda i: (0, i, 0))],
        out_specs=pl.BlockSpec((tr, cdim), lambda i: (i, 0)), out_shape=jax.ShapeDtypeStruct((r, cdim), F32),
        compiler_params=_params(("parallel",)), name=name,
    )(parts)


def _adamw(parts, w, m, v, name):
    npart, r, cdim = parts.shape
    tr = r // 4 if r % 32 == 0 else r

    def body(p_ref, w_ref, m_ref, v_ref, g_ref, d_ref, nm_ref, nv_ref):
        g = p_ref[0]
        for j in range(1, npart):
            g = g + p_ref[j]
        m_new = ADAM_B1 * m_ref[...] + (1.0 - ADAM_B1) * g
        v_new = ADAM_B2 * v_ref[...] + (1.0 - ADAM_B2) * (g * g)
        m_hat = m_new / (1.0 - ADAM_B1 ** ADAM_STEP)
        v_hat = v_new / (1.0 - ADAM_B2 ** ADAM_STEP)
        g_ref[...] = g
        d_ref[...] = -ADAM_LR * (m_hat / (jnp.sqrt(v_hat) + ADAM_EPS) + ADAM_WD * w_ref[...])
        nm_ref[...] = m_new
        nv_ref[...] = v_new

    blk = pl.BlockSpec((tr, cdim), lambda i: (i, 0))
    shp = jax.ShapeDtypeStruct((r, cdim), F32)
    return pl.pallas_call(
        body, grid=(r // tr,), in_specs=[pl.BlockSpec((npart, tr, cdim), lambda i: (0, i, 0)), blk, blk, blk], out_specs=[blk] * 4,
        out_shape=[shp] * 4, compiler_params=_params(("parallel",)), name=name,
    )(parts, w, m, v)


BIG = ("ev_w_in", "ev_w_uq", "ev_w_ukv", "ev_w_out", "od_w_in", "od_w_out", "meta")
SMALL = ("norm_g", "final_g", "ev_q_norm_g", "ev_kv_norm_g", "od_sinks")
SMALL_SHAPE = (8, 512)
BY_ROWS = ("ev_w_out", "od_w_out")

EV_IN_SHARD, OD_IN_SHARD, UQ_SHARD = 2976 // N_CHIPS, 2304 // N_CHIPS, 768 // N_CHIPS


def _pack_big(a, lead=()):
    return _pack_first(a, lead) + _pack_later(a, lead)


def _pad_lanes(x, width):
    return jnp.concatenate([x, jnp.zeros(x.shape[:-1] + (width - x.shape[-1],), x.dtype)], axis=-1)


def _pack_latent(a, lead=()):
    ax = len(lead)
    corner = jnp.concatenate([a["ev_w_ukv"], a["meta"], jnp.zeros(lead + (256 - MLA_KV_LORA - N_META, 256), a["meta"].dtype)], axis=ax)
    return jnp.concatenate([_pad_lanes(a["ev_w_uq"], 256), corner], axis=ax + 1)


def _pack_first(a, lead=()):
    return _pad_lanes(a["ev_w_in"], 768), _pack_latent(a, lead)


def _pack_later(a, lead=()):
    return _pad_lanes(a["od_w_in"], 640), jnp.concatenate([a["ev_w_out"], a["od_w_out"]], axis=len(lead) + 1)


N_FIRST = 2
LATER = ("od_w_in", "ev_w_out", "od_w_out")


def _unpack_big(p_in0, p_lat, p_in1, p_out):
    return dict(ev_w_in=p_in0[..., :EV_IN_SHARD], od_w_in=p_in1[..., :OD_IN_SHARD], ev_w_out=p_out[..., :D_MODEL],
                od_w_out=p_out[..., D_MODEL:], ev_w_uq=p_lat[..., :UQ_SHARD], ev_w_ukv=p_lat[..., :MLA_KV_LORA, 256:],
                meta=p_lat[..., MLA_KV_LORA:MLA_KV_LORA + N_META, 256:])


def _chip_shards(full, by_rows):
    if by_rows:
        return full.reshape(N_CHIPS, full.shape[0] // N_CHIPS, full.shape[1])
    cols = full.shape[1] // N_CHIPS
    return jnp.stack([full[:, j * cols:(j + 1) * cols] for j in range(N_CHIPS)])


def _from_chip_shards(slots, by_rows):
    if by_rows:
        return slots.reshape(-1, slots.shape[2])
    return jnp.concatenate([slots[j] for j in range(N_CHIPS)], axis=1)


def _pack_small(arrs, extra=None):
    flat = [a.reshape(-1) for a in arrs] + ([] if extra is None else [extra.reshape(-1)])
    used = sum(f.shape[0] for f in flat)
    return jnp.pad(jnp.concatenate(flat), (0, SMALL_SHAPE[0] * SMALL_SHAPE[1] - used)).reshape(SMALL_SHAPE)


def _unpack_small(p, shapes):
    flat, out, at = p.reshape(-1), [], 0
    for s in shapes:
        n = int(np.prod(s))
        out.append(flat[at:at + n].reshape(s))
        at += n
    return out, flat[at]


def kernel(x, meta, norm_g, final_g, ev_w_in, ev_q_norm_g, ev_kv_norm_g, ev_w_uq, ev_w_ukv, ev_w_out, od_w_in, od_sinks, od_w_out, loss_target, m_meta, m_norm_g, m_final_g, m_ev_w_in, m_ev_q_norm_g, m_ev_kv_norm_g, m_ev_w_uq, m_ev_w_ukv, m_ev_w_out, m_od_w_in, m_od_sinks, m_od_w_out, v_meta, v_norm_g, v_final_g, v_ev_w_in, v_ev_q_norm_g, v_ev_kv_norm_g, v_ev_w_uq, v_ev_w_ukv, v_ev_w_out, v_od_w_in, v_od_sinks, v_od_w_out):
    given = dict(locals())
    two_d = lambda a: a[0] if a.ndim == 3 else a
    packs = {k: _pack_big({n: two_d(given[k + n]) for n in BIG}) for k in ("", "m_", "v_")}

    wbf = [p.astype(BF16) for p in packs[""]]
    later = _gather_on_sequencer(wbf[N_FIRST:], "gather_later_weights")
    first, meta_all = _gather_weights(wbf[:N_FIRST], meta, "gather_weights")
    full = {n: _from_chip_shards(a, n in BY_ROWS) for n, a in _unpack_big(*first, *later).items()}
    meta_full = _from_chip_shards(meta_all, False)

    core = lax.axis_index("c").astype(jnp.int32).reshape(1)
    shards = lambda g: {n: _chip_shards(a, n in BY_ROWS) for n, a in g.items()}

    def sums_of(group, tag, to_sibling):
        return [_add_sibling(g, s, core, f"add_sibling_{tag}{i}") for i, (g, s) in enumerate(zip(group, to_sibling(group)))]

    def update(sums, tag, first_pack, scatter):
        parts = scatter(sums, "grads_to_chips_" + tag)
        reduced = _share_halves([_sum_parts(p, f"add_chips_{tag}{i}") for i, p in enumerate(parts)], "reduced_to_sibling_" + tag)
        return [_adamw(r[None], packs[""][first_pack + i], packs["m_"][first_pack + i], packs["v_"][first_pack + i],
                       f"adamw_matrices_{first_pack + i}") for i, r in enumerate(reduced)]

    def behind(tag, first_pack, ids):
        def start(packed):
            sums = sums_of(packed, tag, lambda p: _grads_to_sibling_on_sequencer(p, "grads_to_sibling_" + tag, ids[0]))
            return sums, lambda s: update(s, tag, first_pack, lambda t, name: _chip_scatter_on_sequencer(t, name, ids[1]))
        return start

    loss, grad_x, grads, updated_later = _local_step(
        x, meta_full, norm_g, final_g, ev_q_norm_g, ev_kv_norm_g, od_sinks, loss_target, full["ev_w_in"], full["ev_w_uq"],
        full["ev_w_ukv"], full["ev_w_out"], full["od_w_in"], full["od_w_out"],
        reduce_early=lambda g: behind("later", N_FIRST, (3, 2))(_pack_later(shards(g), (N_CHIPS,))))

    first = _pack_first(shards({n: grads[n] for n in BIG if n not in LATER}), (N_CHIPS,))
    updated_first = update(sums_of(first, "first", lambda p: _grads_to_sibling(p, "grads_to_sibling_first")), "first", 0,
                           lambda t, name: _chip_scatter_on_sequencer(t, name, 4))
    updated = updated_first + updated_later
    big_out = [{n: a.reshape(given[n].shape) for n, a in _unpack_big(*outs).items()} for outs in zip(*updated)]

    small_shapes = [given[n].shape for n in SMALL]
    ssum = _all_reduce_small(_pack_small([grads[n] for n in SMALL], loss[0, 0]), "reduce_vectors")
    small_out = _adamw(ssum[None], _pack_small([given[n] for n in SMALL]), _pack_small([given["m_" + n] for n in SMALL]),
                       _pack_small([given["v_" + n] for n in SMALL]), "adamw_vectors")
    total_loss = ssum.reshape(-1)[sum(int(np.prod(s)) for s in small_shapes)]
    small_out = [_unpack_small(o, small_shapes)[0] for o in small_out]

    names = ("meta", "norm_g", "final_g", "ev_w_in", "ev_q_norm_g", "ev_kv_norm_g", "ev_w_uq", "ev_w_ukv", "ev_w_out", "od_w_in", "od_sinks",
             "od_w_out")
    outs = [total_loss, grad_x]
    for kind in range(4):
        for n in names:
            outs.append(big_out[kind][n] if n in BIG else small_out[kind][SMALL.index(n)])
    return tuple(outs)
```

```python
import functools
import math

import numpy as np
import jax
import jax.numpy as jnp
from jax import lax
from jax.experimental import pallas as pl
from jax.experimental.pallas import tpu as pltpu
from jax.experimental.pallas import tpu_sc as plsc

F32 = jnp.float32
BF16 = jnp.bfloat16

D_MODEL = 1024
BLOCK = 128
N_META = 16
N_PAD = BLOCK - N_META
NORM_EPS = 1e-6
NEG = -1e30
HEAD = 64
SB_HEADS = 8
MLA_HEADS = 8
MLA_Q_LORA = 256
MLA_KV_LORA = 128
MLA_NOPE = 64
MLA_ROPE = 32
ROPE_BASE = 10000.0
SWA_HEADS = 16
SWA_KV_HEADS = 2
SWA_WINDOW = 128
N_CHIPS = 4
N_DEV = 8

ADAM_LR = 0.001
ADAM_B1 = 0.9
ADAM_B2 = 0.999
ADAM_EPS = 1e-08
ADAM_WD = 0.01
ADAM_STEP = 10

VMEM_LIMIT = 48 * 1024 * 1024

EV_G, EV_Q, EV_K, EV_V, EV_CQ, EV_CKV, EV_KR, EV_N = 0, 1024, 1536, 2048, 2560, 2816, 2944, 3072
OD_G, OD_Q, OD_K, OD_V, OD_N = 0, 1024, 2048, 2176, 2304


def _params(sem=None):
    return pltpu.CompilerParams(dimension_semantics=sem, vmem_limit_bytes=VMEM_LIMIT)


def _row_tile(m):
    for c in (544, 256):
        if m % c == 0:
            return c
    return 128


def _matmul_rows(m):
    for c in (1088, 1024, 768, 640, 512, 384, 256):
        if m % c == 0:
            return c
    return 128


def _dot(a, b):
    return jnp.dot(a.astype(BF16), b.astype(BF16), preferred_element_type=F32)


def _dot_nt(a, b):
    return lax.dot_general(a.astype(BF16), b.astype(BF16), (((1,), (1,)), ((), ())), preferred_element_type=F32)


def _dot_tn(a, b):
    return lax.dot_general(a.astype(BF16), b.astype(BF16), (((0,), (0,)), ((), ())), preferred_element_type=F32)


def _rms_fwd(h, g, name):
    t, d = h.shape
    tm = _row_tile(t)

    def body(h_ref, g_ref, o_ref):
        x = h_ref[...]
        r = lax.rsqrt(jnp.mean(x * x, axis=-1, keepdims=True) + NORM_EPS)
        o_ref[...] = ((x * r) * g_ref[...]).astype(o_ref.dtype)

    return pl.pallas_call(
        body, grid=(t // tm,),
        in_specs=[pl.BlockSpec((tm, d), lambda i: (i, 0)), pl.BlockSpec((1, d), lambda i: (0, 0))],
        out_specs=pl.BlockSpec((tm, d), lambda i: (i, 0)),
        out_shape=jax.ShapeDtypeStruct((t, d), BF16), compiler_params=_params(("parallel",)), name=name,
    )(h, g)


def _rms_bwd(h, g, dy, dres, name):
    t, d = h.shape
    tm = _row_tile(t)

    def body(h_ref, g_ref, dy_ref, dres_ref, dh_ref, dg_ref):
        @pl.when(pl.program_id(0) == 0)
        def _():
            dg_ref[...] = jnp.zeros_like(dg_ref)

        x = h_ref[...]
        r = lax.rsqrt(jnp.mean(x * x, axis=-1, keepdims=True) + NORM_EPS)
        xr = x * r
        dy_ = dy_ref[...]
        u = dy_ * g_ref[...]
        dh_ref[...] = dres_ref[...] + r * (u - xr * jnp.mean(u * xr, axis=-1, keepdims=True))
        dg_ref[...] += jnp.sum(dy_ * xr, axis=0, keepdims=True)

    row = pl.BlockSpec((tm, d), lambda i: (i, 0))
    vec = pl.BlockSpec((1, d), lambda i: (0, 0))
    return pl.pallas_call(
        body, grid=(t // tm,), in_specs=[row, vec, row, row], out_specs=[row, vec],
        out_shape=[jax.ShapeDtypeStruct((t, d), F32), jax.ShapeDtypeStruct((1, d), F32)],
        compiler_params=_params(("arbitrary",)), name=name,
    )(h, g, dy, dres)


def _col_tile(n):
    for c in (1024, 768, 640, 512, 384, 256, 128):
        if n % c == 0:
            return c
    return n


def _mm(a, w, name, res=None, out_dtype=F32, a_cols=None):
    m = a.shape[0]
    k, n = w.shape
    a_blk = 0 if a_cols is None else a_cols[0] // k
    assert a_cols is None or (a_cols[1] == k and a_cols[0] % k == 0)
    tm, tn = _matmul_rows(m), _col_tile(n)

    def body(*refs):
        if res is None:
            a_ref, w_ref, o_ref = refs
            acc = _dot(a_ref[...], w_ref[...])
        else:
            a_ref, w_ref, r_ref, o_ref = refs
            acc = r_ref[...] + _dot(a_ref[...], w_ref[...])
        o_ref[...] = acc.astype(o_ref.dtype)

    in_specs = [pl.BlockSpec((tm, k), lambda j, i: (i, a_blk)), pl.BlockSpec((k, tn), lambda j, i: (0, j))]
    args = [a, w]
    if res is not None:
        in_specs.append(pl.BlockSpec((tm, tn), lambda j, i: (i, j)))
        args.append(res)
    return pl.pallas_call(
        body, grid=(n // tn, m // tm), in_specs=in_specs, out_specs=pl.BlockSpec((tm, tn), lambda j, i: (i, j)),
        out_shape=jax.ShapeDtypeStruct((m, n), out_dtype), compiler_params=_params(("parallel", "parallel")), name=name,
    )(*args)


def _mm_nt(a, w, name):
    m, n = a.shape
    k = w.shape[0]
    tm, tk = _matmul_rows(m), _col_tile(k)

    def body(a_ref, w_ref, o_ref):
        o_ref[...] = _dot_nt(a_ref[...], w_ref[...])

    return pl.pallas_call(
        body, grid=(k // tk, m // tm),
        in_specs=[pl.BlockSpec((tm, n), lambda j, i: (i, 0)), pl.BlockSpec((tk, n), lambda j, i: (j, 0))],
        out_specs=pl.BlockSpec((tm, tk), lambda j, i: (i, j)),
        out_shape=jax.ShapeDtypeStruct((m, k), F32), compiler_params=_params(("parallel", "parallel")), name=name,
    )(a, w)


def _mm_tn(x, dy, name):
    m, k = x.shape
    n = dy.shape[1]
    tm, tn = _matmul_rows(m), _col_tile(n)

    def body(x_ref, dy_ref, o_ref):
        @pl.when(pl.program_id(1) == 0)
        def _():
            o_ref[...] = jnp.zeros_like(o_ref)

        o_ref[...] += _dot_tn(x_ref[...], dy_ref[...])

    return pl.pallas_call(
        body, grid=(n // tn, m // tm),
        in_specs=[pl.BlockSpec((tm, k), lambda j, i: (i, 0)), pl.BlockSpec((tm, tn), lambda j, i: (i, j))],
        out_specs=pl.BlockSpec((k, tn), lambda j, i: (0, j)),
        out_shape=jax.ShapeDtypeStruct((k, n), F32), compiler_params=_params(("parallel", "arbitrary")), name=name,
    )(x, dy)


def _silu_parts(g):
    s = 1.0 / (1.0 + jnp.exp(-g))
    return g * s, s * (1.0 + g * (1.0 - s))


def _gate_fwd(o_parts, proj, name):
    t = proj.shape[0]
    tm = _row_tile(t)
    w = D_MODEL // len(o_parts)

    def body(*refs):
        g_ref, o_ref = refs[-2], refs[-1]
        for p, r in enumerate(refs[:-2]):
            sil, _ = _silu_parts(g_ref[:, p * w:(p + 1) * w])
            o_ref[:, p * w:(p + 1) * w] = (r[...].astype(F32) * sil).astype(o_ref.dtype)

    return pl.pallas_call(
        body, grid=(t // tm,),
        in_specs=[pl.BlockSpec((tm, w), lambda i: (i, 0)) for _ in o_parts] + [pl.BlockSpec((tm, D_MODEL), lambda i: (i, 0))],
        out_specs=pl.BlockSpec((tm, D_MODEL), lambda i: (i, 0)),
        out_shape=jax.ShapeDtypeStruct((t, D_MODEL), BF16), compiler_params=_params(("parallel",)), name=name,
    )(*o_parts, proj)


def _gate_bwd(dao, o_parts, proj, name):
    t = proj.shape[0]
    tm = _row_tile(t)
    np_ = len(o_parts)
    w = D_MODEL // np_

    def body(*refs):
        dao_ref, g_ref = refs[0], refs[1 + np_]
        do_refs, dg_ref = refs[2 + np_:2 + 2 * np_], refs[-1]
        for p in range(np_):
            sl = slice(p * w, (p + 1) * w)
            sil, dsil = _silu_parts(g_ref[:, sl])
            da = dao_ref[:, sl]
            do_refs[p][...] = (da * sil).astype(do_refs[p].dtype)
            dg_ref[:, sl] = (da * refs[1 + p][...].astype(F32) * dsil).astype(dg_ref.dtype)

    full = pl.BlockSpec((tm, D_MODEL), lambda i: (i, 0))
    part = pl.BlockSpec((tm, w), lambda i: (i, 0))
    outs = pl.pallas_call(
        body, grid=(t // tm,), in_specs=[full] + [part] * np_ + [full], out_specs=[part] * np_ + [full],
        out_shape=[jax.ShapeDtypeStruct((t, w), BF16)] * np_ + [jax.ShapeDtypeStruct((t, D_MODEL), BF16)],
        compiler_params=_params(("parallel",)), name=name,
    )(dao, *o_parts, proj)
    return outs[:np_], outs[np_]


def _loss_head(h2, gf, target, b, lp):
    d = h2.shape[1]
    nb = lp // BLOCK
    h3 = h2.reshape(b, lp, d)

    def body(h_ref, g_ref, t_ref, dh_ref, dg_ref, loss_ref):
        first = (pl.program_id(0) == 0) & (pl.program_id(1) == 0)

        @pl.when(first)
        def _():
            dg_ref[...] = jnp.zeros_like(dg_ref)
            loss_ref[...] = jnp.zeros_like(loss_ref)

        @pl.when(pl.program_id(1) == 0)
        def _():
            dh_ref[...] = jnp.zeros_like(dh_ref)

        @pl.when(pl.program_id(1) > 0)
        def _():
            x = h_ref[0]
            r = lax.rsqrt(jnp.mean(x * x, axis=-1, keepdims=True) + NORM_EPS)
            xr = x * r
            g = g_ref[...]
            diff = xr * g - t_ref[0]
            loss_ref[...] += 0.5 * jnp.sum(jnp.mean(diff * diff, axis=-1, keepdims=True))
            dy = diff * (1.0 / d)
            u = dy * g
            dh_ref[0] = r * (u - xr * jnp.mean(u * xr, axis=-1, keepdims=True))
            dg_ref[...] += jnp.sum(dy * xr, axis=0, keepdims=True)

    blk = pl.BlockSpec((1, BLOCK, d), lambda bi, n: (bi, n, 0))
    dh, dg, loss = pl.pallas_call(
        body, grid=(b, nb),
        in_specs=[blk, pl.BlockSpec((1, d), lambda bi, n: (0, 0)),
                  pl.BlockSpec((1, BLOCK, d), lambda bi, n: (bi, jnp.maximum(n - 1, 0), 0))],
        out_specs=[blk, pl.BlockSpec((1, d), lambda bi, n: (0, 0)), pl.BlockSpec((8, 128), lambda bi, n: (0, 0))],
        out_shape=[jax.ShapeDtypeStruct((b, lp, d), F32), jax.ShapeDtypeStruct((1, d), F32), jax.ShapeDtypeStruct((8, 128), F32)],
        compiler_params=_params(("arbitrary", "arbitrary")), name="loss_head",
    )(h3, gf, target)
    return dh.reshape(b * lp, d), dg, loss


def _iota2(shape, dim):
    return lax.broadcasted_iota(jnp.int32, shape, dim)


KEYS = 512
SB_FWD_KEYS = 512


def _lo_lanes():
    return _iota2((1, BLOCK), 1) < HEAD


def _halves(x, lo):
    zero = jnp.zeros_like(x)
    return jnp.where(lo, x, zero), jnp.where(lo, zero, x)


def _rows_of_pair(a, b):
    return jnp.where(_iota2((BLOCK, 1), 0) < HEAD, a, b)


def _split_rows_t(x):
    xt = x.T
    first = _iota2(xt.shape, 0) < HEAD
    zero = jnp.zeros_like(xt)
    return jnp.concatenate([jnp.where(first, xt, zero), jnp.where(first, zero, xt)], axis=1).astype(BF16)


def _key_chunk(c, lp, t_idx, strict, key_axis):
    keys = t_idx.shape[key_axis]
    first = c * keys
    s0 = pl.multiple_of(jnp.minimum(first, lp - keys), BLOCK)
    s_idx = s0 + _iota2(t_idx.shape, key_axis)
    seen = (s_idx < t_idx) if strict else (s_idx <= t_idx)
    return s0, seen & (s_idx >= jnp.maximum(first, N_PAD))


def _tri_dot(x, tri):
    return jnp.dot(x.astype(BF16), tri, preferred_element_type=F32)


def _stack_halves(x, lo):
    a, b = _halves(x, lo)
    return jnp.concatenate([a, b], axis=0)


def _pair(a, b, lo):
    return jnp.where(lo, a, b)


def _chunk_starts(lp):
    return [min(c * KEYS, lp - KEYS) for c in range(-(-lp // KEYS))]


def _put_rows(ref, r0, bq, a, b):
    for t in range(bq // BLOCK):
        part = slice(t * BLOCK, (t + 1) * BLOCK)
        ref[0, 0, r0 // BLOCK + t] = jnp.concatenate([a[:, part], b[:, part], jnp.zeros((6, BLOCK), F32)], axis=0)


def _get_rows(ref, r0, bq):
    return [jnp.concatenate([ref[0, 0, r0 // BLOCK + t, h:h + 1, :] for t in range(bq // BLOCK)], axis=1) for h in range(2)]


def _n_chunks(i, keys):
    return ((i + 1) * BLOCK + keys - 1) // keys


QROWS = 512


def _for_query_tiles(nb, tile, keys=KEYS):
    per = QROWS // BLOCK

    def step(j, _):
        tile(pl.multiple_of(j * QROWS, QROWS), QROWS, (j + 1) * (QROWS // keys))
        return 0

    lax.fori_loop(0, nb // per, step, 0)
    for i in range(nb - nb % per, nb):
        tile(i * BLOCK, BLOCK, _n_chunks(i, keys))


def _walk_chunks(r0, n, chunk, carry, leftwards=False, keys=KEYS):
    diag = jnp.maximum(r0 // keys, 1)

    def span(first, last, masked, carry):
        def step(t, cr):
            return chunk(last - 1 - t if leftwards else first + t, cr, masked)
        return lax.fori_loop(0, last - first, step, carry)

    spans = [(0, 1, True), (1, diag, False), (diag, n, True)]
    for first, last, masked in (reversed(spans) if leftwards else spans):
        carry = span(first, last, masked, carry)
    return carry


def _where(valid, x, other):
    return x if valid is None else jnp.where(valid, x, other)


HEAD_SCALE = HEAD ** -0.5
assert math.frexp(HEAD_SCALE)[0] == 0.5


def _sb_scores(q_h, k, valid, after):
    z = _dot_nt(q_h, k)
    lb = jnp.minimum(z, 0.0) - jnp.log(1.0 + jnp.exp(-jnp.abs(z)))
    l1m_all = lb - z
    l1m = _where(valid, l1m_all, 0.0)
    return lb, l1m_all, l1m, _tri_dot(l1m, after)


def _pair_stat_spec(nb):
    return pl.BlockSpec((1, 1, nb, 8, BLOCK), lambda bi, hp: (bi, hp, 0, 0, 0))


def _sb_fwd(proj3, name):
    b, lp, _ = proj3.shape
    nb = lp // BLOCK
    npair = SB_HEADS // 2

    keys = SB_FWD_KEYS

    def body(q_ref, k_ref, v_ref, o_ref, tot_ref):
        lo = _lo_lanes()
        after = (_iota2((keys, keys), 0) > _iota2((keys, keys), 1)).astype(BF16)

        def qtile(r0, bq, n):
            qs = _halves((q_ref[0, pl.ds(r0, bq), :] * HEAD_SCALE).astype(BF16), lo)
            t_idx = r0 + _iota2((bq, keys), 0)

            def kchunk(c, carry, masked):
                cs, acc = carry[:2], carry[2]
                s0, valid = _key_chunk(c, lp, t_idx, True, 1)
                valid = valid if masked else None
                k = k_ref[0, pl.ds(s0, keys), :].astype(BF16)
                a_s, new = [], []
                for h in range(2):
                    lb, _, l1m, suf = _sb_scores(qs[h], k, valid, after)
                    a_s.append(_where(valid, jnp.exp(lb + suf + cs[h]), 0.0).astype(BF16))
                    new.append(cs[h] + jnp.sum(l1m, axis=1, keepdims=True))
                v_bd = _stack_halves(v_ref[0, pl.ds(s0, keys), :].astype(BF16), lo)
                return (*new, acc + jnp.dot(jnp.concatenate(a_s, axis=1), v_bd, preferred_element_type=F32))

            zero = jnp.zeros((bq, 1), F32)
            c_a, c_b, acc = _walk_chunks(r0, n, kchunk, (zero, zero, jnp.zeros((bq, BLOCK), F32)), leftwards=True, keys=keys)
            o_ref[0, pl.ds(r0, bq), :] = acc
            tot_ref[0, pl.ds(r0, bq), :] = jnp.broadcast_to(_pair(c_a, c_b, lo), (bq, BLOCK))

        _for_query_tiles(nb, qtile, keys)

    def col(first):
        return pl.BlockSpec((1, lp, 2 * HEAD), lambda bi, hp: (bi, 0, first // (2 * HEAD) + hp))

    shp = jax.ShapeDtypeStruct((b, lp, SB_HEADS * HEAD), F32)
    return pl.pallas_call(
        body, grid=(b, npair), in_specs=[col(EV_Q), col(EV_K), col(EV_V)], out_specs=[col(0), col(0)], out_shape=[shp, shp],
        compiler_params=_params(("parallel", "parallel")), name=name,
    )(proj3, proj3, proj3)


def _sb_bwd(proj3, tot, do, name):
    b, lp, _ = proj3.shape
    nb = lp // BLOCK
    npair = SB_HEADS // 2

    def body(q_ref, k_ref, v_ref, tot_ref, do_ref, dq_ref, dk_ref, dv_ref):
        lo = _lo_lanes()
        after = (_iota2((KEYS, KEYS), 0) > _iota2((KEYS, KEYS), 1)).astype(BF16)
        before = (_iota2((KEYS, KEYS), 0) < _iota2((KEYS, KEYS), 1)).astype(BF16)
        dk_ref[...] = jnp.zeros_like(dk_ref)
        dv_ref[...] = jnp.zeros_like(dv_ref)

        def qtile(r0, bq, n):
            rows = pl.ds(r0, bq)
            qs = _halves((q_ref[0, rows, :] * HEAD_SCALE).astype(BF16), lo)
            dos = _halves(do_ref[0, rows, :].astype(BF16), lo)
            tot_i = tot_ref[0, rows, :]
            tots = (tot_i[:, 0:1], tot_i[:, HEAD:HEAD + 1])
            q_st, do_st = jnp.concatenate(qs, axis=0), jnp.concatenate(dos, axis=0)
            t_idx = r0 + _iota2((bq, KEYS), 0)

            def kchunk(c, carry, masked):
                s0, valid = _key_chunk(c, lp, t_idx, True, 1)
                valid = valid if masked else None
                keys = pl.ds(s0, KEYS)
                k = k_ref[0, keys, :].astype(BF16)
                v = v_ref[0, keys, :].astype(BF16)
                a_s, dzs, new = [], [], []
                for h in range(2):
                    left, pre = carry[2 * h], carry[2 * h + 1]
                    lb, l1m_all, l1m, suf = _sb_scores(qs[h], k, valid, after)
                    here = jnp.sum(l1m, axis=1, keepdims=True)
                    a = _where(valid, jnp.exp(lb + suf + (tots[h] - left - here)), 0.0)
                    w = a * _dot_nt(dos[h], v)
                    dz = _where(valid, w * jnp.exp(l1m_all) - (pre + _tri_dot(w, before)) * jnp.exp(lb), 0.0)
                    new += [left + here, pre + jnp.sum(w, axis=1, keepdims=True)]
                    a_s.append(a.astype(BF16))
                    dzs.append(dz.astype(BF16))
                dk_ref[0, keys, :] += _dot_tn(jnp.concatenate(dzs, axis=0), q_st)
                dv_ref[0, keys, :] += _dot_tn(jnp.concatenate(a_s, axis=0), do_st)
                dq = carry[4] + jnp.dot(jnp.concatenate(dzs, axis=1), _stack_halves(k, lo), preferred_element_type=F32)
                return (*new, dq)

            zero = jnp.zeros((bq, 1), F32)
            out = _walk_chunks(r0, n, kchunk, (zero, zero, zero, zero, jnp.zeros((bq, BLOCK), F32)))
            dq_ref[0, rows, :] = out[4] * HEAD_SCALE

        _for_query_tiles(nb, qtile)

    def col(first):
        return pl.BlockSpec((1, lp, 2 * HEAD), lambda bi, hp: (bi, 0, first // (2 * HEAD) + hp))

    shp = jax.ShapeDtypeStruct((b, lp, SB_HEADS * HEAD), F32)
    return pl.pallas_call(
        body, grid=(b, npair), in_specs=[col(EV_Q), col(EV_K), col(EV_V), col(0), col(0)], out_specs=[col(0)] * 3, out_shape=[shp] * 3,
        compiler_params=_params(("parallel", "parallel")), name=name,
    )(proj3, proj3, proj3, tot, do)


def _rope_tables(lp):
    half = MLA_ROPE // 2
    pos = (np.arange(lp) - N_PAD).astype(np.float32)
    inv = jnp.asarray(ROPE_BASE, F32) ** (-jnp.arange(half, dtype=F32) / half)
    ang = jnp.asarray(pos)[:, None] * inv[None, :]
    cos, sin = jnp.cos(ang), jnp.sin(ang)
    zeros = lambda n: jnp.zeros((lp, n), F32)
    c = jnp.concatenate([jnp.ones((lp, MLA_NOPE), F32), cos, cos, zeros(32)], axis=1)
    s1 = jnp.concatenate([zeros(MLA_NOPE), -sin, zeros(half), zeros(32)], axis=1)
    s2 = jnp.concatenate([zeros(MLA_NOPE), zeros(half), sin, zeros(32)], axis=1)
    return c, s1, s2


def _rope(x, c, s1, s2):
    half = MLA_ROPE // 2
    return x * c + pltpu.roll(x, BLOCK - half, 1) * s1 + pltpu.roll(x, half, 1) * s2


def _rope_t(dy, c, s1, s2):
    half = MLA_ROPE // 2
    return dy * c + pltpu.roll(dy * s1, half, 1) + pltpu.roll(dy * s2, BLOCK - half, 1)


def _rms_rows(x, g):
    r = lax.rsqrt(jnp.mean(x * x, axis=-1, keepdims=True) + NORM_EPS)
    return x * r, r


def _prep_rows(lp):
    return lp // 2 if lp % 32 == 0 else BLOCK


def _mla_prep_fwd(proj3, gq, gkv, wq, wk, wv, tabs, name):
    b, lp, _ = proj3.shape
    rows = _prep_rows(lp)
    hw = MLA_HEADS * BLOCK

    def body(cq_ref, ckv_ref, kr_ref, gq_ref, gkv_ref, wq_ref, wk_ref, wv_ref, c_ref, s1_ref, s2_ref, qf_ref, kf_ref, v_ref):
        c, s1, s2 = c_ref[...], s1_ref[...], s2_ref[...]
        xq, _ = _rms_rows(cq_ref[0], None)
        qh = _dot(xq * gq_ref[...], wq_ref[...])
        xk, _ = _rms_rows(ckv_ref[0], None)
        ckv_n = xk * gkv_ref[...]
        kv = _dot(ckv_n, wk_ref[...])
        v_ref[0] = _dot(ckv_n, wv_ref[...]).astype(v_ref.dtype)
        kr = _rope(kr_ref[0], c, s1, s2)
        for h in range(MLA_HEADS):
            ls = slice(h * BLOCK, (h + 1) * BLOCK)
            qf_ref[0, :, ls] = _rope(qh[:, ls], c, s1, s2).astype(qf_ref.dtype)
            kf_ref[0, :, ls] = (kv[:, ls] + kr).astype(kf_ref.dtype)

    def col(first, width):
        return pl.BlockSpec((1, rows, width), lambda bi, n: (bi, n, first // width))

    def whole(a):
        return pl.BlockSpec(a.shape, lambda bi, n: (0,) * a.ndim)

    tab = pl.BlockSpec((rows, BLOCK), lambda bi, n: (n, 0))
    return pl.pallas_call(
        body, grid=(b, lp // rows),
        in_specs=[col(EV_CQ, MLA_Q_LORA), col(EV_CKV, MLA_KV_LORA), col(EV_KR, BLOCK), whole(gq), whole(gkv), whole(wq), whole(wk),
                  whole(wv), tab, tab, tab],
        out_specs=[col(0, hw), col(0, hw), col(0, MLA_HEADS * HEAD)],
        out_shape=[jax.ShapeDtypeStruct((b, lp, hw), BF16), jax.ShapeDtypeStruct((b, lp, hw), BF16),
                   jax.ShapeDtypeStruct((b, lp, MLA_HEADS * HEAD), BF16)],
        compiler_params=_params(("parallel", "parallel")), name=name,
    )(proj3, proj3, proj3, gq, gkv, wq, wk, wv, *tabs)


def _mla_prep_bwd(proj3, gq, gkv, wq, wk, wv, tabs, dqf, dkf, dv, name):
    b, lp, _ = proj3.shape
    rows = _prep_rows(lp)
    hw = MLA_HEADS * BLOCK

    def body(cq_ref, ckv_ref, gq_ref, gkv_ref, wq_ref, wk_ref, wv_ref, c_ref, s1_ref, s2_ref, dqf_ref, dkf_ref, dv_ref,
             dcq_ref, dckv_ref, dkr_ref, dwq_ref, dwk_ref, dwv_ref, dgq_ref, dgkv_ref, dqh):
        @pl.when((pl.program_id(0) == 0) & (pl.program_id(1) == 0))
        def _():
            for r in (dwq_ref, dwk_ref, dwv_ref, dgq_ref, dgkv_ref):
                r[...] = jnp.zeros_like(r)

        c, s1, s2 = c_ref[...], s1_ref[...], s2_ref[...]
        dkr = jnp.zeros((rows, BLOCK), F32)
        for h in range(MLA_HEADS):
            ls = slice(h * BLOCK, (h + 1) * BLOCK)
            dqh[:, ls] = _rope_t(dqf_ref[0, :, ls].astype(F32), c, s1, s2).astype(dqh.dtype)
            dkr = dkr + dkf_ref[0, :, ls].astype(F32)
        dkr_ref[0] = _rope_t(dkr, c, s1, s2).astype(dkr_ref.dtype)

        def norm_bwd(x, g, dy, dg_ref):
            xr, r = _rms_rows(x, None)
            u = dy * g
            dg_ref[...] += jnp.sum(dy * xr, axis=0, keepdims=True)
            return r * (u - xr * jnp.mean(u * xr, axis=-1, keepdims=True))

        xq, _ = _rms_rows(cq_ref[0], None)
        cq_n = xq * gq_ref[...]
        dwq_ref[...] += _dot_tn(cq_n, dqh[...])
        dcq_ref[0] = norm_bwd(cq_ref[0], gq_ref[...], _dot_nt(dqh[...], wq_ref[...]), dgq_ref).astype(dcq_ref.dtype)
        xk, _ = _rms_rows(ckv_ref[0], None)
        ckv_n = xk * gkv_ref[...]
        dkf_, dv_ = dkf_ref[0], dv_ref[0]
        dwk_ref[...] += _dot_tn(ckv_n, dkf_)
        dwv_ref[...] += _dot_tn(ckv_n, dv_)
        dckv_n = _dot_nt(dkf_, wk_ref[...]) + _dot_nt(dv_, wv_ref[...])
        dckv_ref[0] = norm_bwd(ckv_ref[0], gkv_ref[...], dckv_n, dgkv_ref).astype(dckv_ref.dtype)

    def col(first, width):
        return pl.BlockSpec((1, rows, width), lambda bi, n: (bi, n, first // width))

    def whole(a):
        return pl.BlockSpec(a.shape, lambda bi, n: (0,) * len(a.shape))

    tab = pl.BlockSpec((rows, BLOCK), lambda bi, n: (n, 0))
    acc_shapes = [jax.ShapeDtypeStruct(a.shape, F32) for a in (wq, wk, wv, gq, gkv)]
    return pl.pallas_call(
        body, grid=(b, lp // rows),
        in_specs=[col(EV_CQ, MLA_Q_LORA), col(EV_CKV, MLA_KV_LORA), whole(gq), whole(gkv), whole(wq), whole(wk), whole(wv), tab, tab, tab,
                  col(0, hw), col(0, hw), col(0, MLA_HEADS * HEAD)],
        out_specs=[col(0, MLA_Q_LORA), col(0, MLA_KV_LORA), col(0, BLOCK)] + [whole(a) for a in acc_shapes],
        out_shape=[jax.ShapeDtypeStruct((b, lp, MLA_Q_LORA), BF16), jax.ShapeDtypeStruct((b, lp, MLA_KV_LORA), BF16),
                   jax.ShapeDtypeStruct((b, lp, BLOCK), BF16)] + acc_shapes,
        scratch_shapes=[pltpu.VMEM((rows, hw), BF16)],
        compiler_params=_params(("arbitrary", "arbitrary")), name=name,
    )(proj3, proj3, gq, gkv, wq, wk, wv, *tabs, dqf, dkf, dv)


def _mla_fwd(qf, kf, v, name):
    b, lp, _ = qf.shape
    nb = lp // BLOCK
    npair = MLA_HEADS // 2
    scale = (MLA_NOPE + MLA_ROPE) ** -0.5
    starts = _chunk_starts(lp)

    def body(q_ref, k_ref, v_ref, o_ref, lse_ref, vt_ref):
        for c, s0 in enumerate(starts):
            vt_ref[c] = _split_rows_t(v_ref[0, s0:s0 + KEYS, :].astype(F32))

        def qtile(r0, bq, n):
            qs = [q_ref[0, pl.ds(r0, bq), h * BLOCK:(h + 1) * BLOCK] for h in range(2)]
            t_idx = r0 + _iota2((KEYS, bq), 1)

            def kchunk(c, carry, masked):
                stats, acc = carry[:4], carry[4]
                s0, valid = _key_chunk(c, lp, t_idx, False, 0)
                valid = valid if masked else None
                ps, new, alphas = [], [], []
                for h in range(2):
                    m, l = stats[2 * h], stats[2 * h + 1]
                    s = _where(valid, _dot_nt(k_ref[0, pl.ds(s0, KEYS), h * BLOCK:(h + 1) * BLOCK], qs[h]) * scale, NEG)
                    m_new = jnp.maximum(m, jnp.max(s, axis=0, keepdims=True))
                    p = _where(valid, jnp.exp(s - m_new), 0.0)
                    alpha = jnp.exp(m - m_new)
                    new += [m_new, alpha * l + jnp.sum(p, axis=0, keepdims=True)]
                    alphas.append(alpha)
                    ps.append(p.astype(BF16))
                pv = jnp.dot(vt_ref[c], jnp.concatenate(ps, axis=0), preferred_element_type=F32)
                return (*new, _rows_of_pair(alphas[0], alphas[1]) * acc + pv)

            neg, zero = jnp.full((1, bq), NEG, F32), jnp.zeros((1, bq), F32)
            m_a, l_a, m_b, l_b, acc = _walk_chunks(r0, n, kchunk, (neg, zero, neg, zero, jnp.zeros((BLOCK, bq), F32)))
            safe = [jnp.where(l > 0.0, l, 1.0) for l in (l_a, l_b)]
            o_ref[0, pl.ds(r0, bq), :] = (acc / _rows_of_pair(safe[0], safe[1])).T
            lse = [jnp.where(l > 0.0, m + jnp.log(sf), 0.0) for m, l, sf in ((m_a, l_a, safe[0]), (m_b, l_b, safe[1]))]
            _put_rows(lse_ref, r0, bq, lse[0], lse[1])

        _for_query_tiles(nb, qtile)

    wide = pl.BlockSpec((1, lp, 2 * BLOCK), lambda bi, hp: (bi, 0, hp))
    thin = pl.BlockSpec((1, lp, 2 * HEAD), lambda bi, hp: (bi, 0, hp))
    return pl.pallas_call(
        body, grid=(b, npair), in_specs=[wide, wide, thin], out_specs=[thin, _pair_stat_spec(nb)],
        out_shape=[jax.ShapeDtypeStruct((b, lp, MLA_HEADS * HEAD), F32), jax.ShapeDtypeStruct((b, npair, nb, 8, BLOCK), F32)],
        scratch_shapes=[pltpu.VMEM((len(starts), BLOCK, 2 * KEYS), BF16)],
        compiler_params=_params(("parallel", "parallel")), name=name,
    )(qf, kf, v)


def _mla_bwd(qf, kf, v, o, lse, do, name):
    b, lp, _ = qf.shape
    nb = lp // BLOCK
    npair = MLA_HEADS // 2
    scale = (MLA_NOPE + MLA_ROPE) ** -0.5

    starts = _chunk_starts(lp)

    def body(q_ref, k_ref, v_ref, o_ref, lse_ref, do_ref, dq_ref, dk_ref, dv_ref, kt_ref):
        lo = _lo_lanes()
        dk_ref[...] = jnp.zeros_like(dk_ref)
        dv_ref[...] = jnp.zeros_like(dv_ref)
        for c, s0 in enumerate(starts):
            for h in range(2):
                kt_ref[c, h] = k_ref[0, s0:s0 + KEYS, h * BLOCK:(h + 1) * BLOCK].astype(F32).T.astype(BF16)

        def qtile(r0, bq, n):
            rows = pl.ds(r0, bq)
            qs = [q_ref[0, rows, h * BLOCK:(h + 1) * BLOCK] for h in range(2)]
            do_i = do_ref[0, rows, :]
            dos = _halves(do_i.astype(BF16), lo)
            do_st = jnp.concatenate(dos, axis=0)
            both = (do_i * o_ref[0, rows, :]).T
            dsum = (jnp.sum(both[:HEAD], axis=0, keepdims=True), jnp.sum(both[HEAD:], axis=0, keepdims=True))
            lses = _get_rows(lse_ref, r0, bq)
            t_idx = r0 + _iota2((KEYS, bq), 1)

            def kchunk(c, dqts, masked):
                s0, valid = _key_chunk(c, lp, t_idx, False, 0)
                valid = valid if masked else None
                keys = pl.ds(s0, KEYS)
                v_c = v_ref[0, keys, :]
                ps, out = [], []
                for h in range(2):
                    lanes = slice(h * BLOCK, (h + 1) * BLOCK)
                    s = _dot_nt(k_ref[0, keys, lanes], qs[h]) * scale
                    p = _where(valid, jnp.exp(s - lses[h]), 0.0)
                    ds = (p * (_dot_nt(v_c, dos[h]) - dsum[h]) * scale).astype(BF16)
                    dk_ref[0, keys, lanes] += jnp.dot(ds, qs[h], preferred_element_type=F32)
                    out.append(dqts[h] + jnp.dot(kt_ref[c, h], ds, preferred_element_type=F32))
                    ps.append(p.astype(BF16))
                dv_ref[0, keys, :] += jnp.dot(jnp.concatenate(ps, axis=1), do_st, preferred_element_type=F32)
                return tuple(out)

            zero = jnp.zeros((BLOCK, bq), F32)
            dq_a, dq_b = _walk_chunks(r0, n, kchunk, (zero, zero))
            dq_ref[0, rows, 0:BLOCK] = dq_a.T
            dq_ref[0, rows, BLOCK:2 * BLOCK] = dq_b.T

        _for_query_tiles(nb, qtile)

    wide = pl.BlockSpec((1, lp, 2 * BLOCK), lambda bi, hp: (bi, 0, hp))
    thin = pl.BlockSpec((1, lp, 2 * HEAD), lambda bi, hp: (bi, 0, hp))
    return pl.pallas_call(
        body, grid=(b, npair), in_specs=[wide, wide, thin, thin, _pair_stat_spec(nb), thin], out_specs=[wide, wide, thin],
        out_shape=[jax.ShapeDtypeStruct(qf.shape, F32), jax.ShapeDtypeStruct(qf.shape, F32), jax.ShapeDtypeStruct(v.shape, F32)],
        scratch_shapes=[pltpu.VMEM((len(starts), 2, BLOCK, KEYS), BF16)],
        compiler_params=_params(("parallel", "parallel")), name=name,
    )(qf, kf, v, o, lse, do)


SWA_KEYS = 2 * BLOCK + N_META


def _swa_keys(k_ref, v_ref, n, kv):
    prev = jnp.maximum(n - 1, 0)
    rows = lambda blk: pl.ds(pl.multiple_of(blk * BLOCK, BLOCK), BLOCK)
    mine = (_iota2((1, BLOCK), 1) >= HEAD).astype(jnp.int32) == kv

    def both_halves(ref):
        x = jnp.concatenate([ref[0, rows(prev), :], ref[0, rows(n), :], ref[0, N_PAD:BLOCK, :]], axis=0)
        return jnp.where(mine, x, pltpu.roll(x, HEAD, 1))

    slot = _iota2((SWA_KEYS, BLOCK), 0)
    s_idx = jnp.where(slot < 2 * BLOCK, (n - 1) * BLOCK + slot, slot - 2 * BLOCK + N_PAD)
    dist = n * BLOCK + _iota2((SWA_KEYS, BLOCK), 1) - s_idx
    band = (slot < 2 * BLOCK) & (dist >= 0) & (dist < SWA_WINDOW) & (s_idx >= BLOCK)
    meta = (slot >= 2 * BLOCK) & (dist >= 0)
    return both_halves(k_ref), both_halves(v_ref), band | meta, dist.astype(F32), prev


def _pad_keys(x):
    return jnp.concatenate([x, jnp.zeros((3 * BLOCK - SWA_KEYS, x.shape[1]), x.dtype)], axis=0)


def _swa_probs(q_h, kdup, valid, dist, head, sink_ref):
    slope = jnp.exp(jnp.full((1, 1), -8.0 * math.log(2.0) / SWA_HEADS, F32) * (head + 1).astype(F32))
    s = jnp.where(valid, _dot_nt(kdup, q_h) - slope * dist, NEG)
    sink = sink_ref[pl.ds(head, 1), 0:1]
    m = jnp.maximum(jnp.max(s, axis=0, keepdims=True), sink)
    e = jnp.where(valid, jnp.exp(s - m), 0.0)
    es = jnp.exp(sink - m)
    inv = 1.0 / (jnp.sum(e, axis=0, keepdims=True) + es)
    return e * inv, es * inv


SWA_PAIRS = SWA_HEADS // SWA_KV_HEADS // 2
SWA_GROUP = SWA_PAIRS * 2 * HEAD


def _swa_specs(b, lp):
    nb = lp // BLOCK
    qcol = lambda first: pl.BlockSpec((1, BLOCK, SWA_GROUP), lambda bi, kv, n: (bi, n, first // SWA_GROUP + kv))
    kcol = lambda first: pl.BlockSpec((1, lp, BLOCK), lambda bi, kv, n: (bi, 0, first // BLOCK))
    sink = pl.BlockSpec((SWA_HEADS, BLOCK), lambda bi, kv, n: (0, 0))
    return (b, SWA_KV_HEADS, nb), qcol, kcol, sink


def _swa_fwd(proj3, sinks, name):
    b, lp, _ = proj3.shape
    grid, qcol, kcol, sink = _swa_specs(b, lp)

    def body(q_ref, k_ref, v_ref, sink_ref, o_ref):
        kv, n = pl.program_id(1), pl.program_id(2)
        lo = _lo_lanes()
        kdup, vdup, valid, dist, _ = _swa_keys(k_ref, v_ref, n, kv)
        kdup = kdup.astype(BF16)
        vt = _split_rows_t(_pad_keys(vdup))
        for p in range(SWA_PAIRS):
            lanes = slice(p * BLOCK, (p + 1) * BLOCK)
            qs = _halves((q_ref[0, :, lanes] * HEAD_SCALE).astype(BF16), lo)
            probs = [_swa_probs(qs[hh], kdup, valid, dist, (kv * SWA_PAIRS + p) * 2 + hh, sink_ref)[0].astype(BF16) for hh in range(2)]
            o_ref[0, :, lanes] = jnp.dot(vt, jnp.concatenate([_pad_keys(pr) for pr in probs], axis=0), preferred_element_type=F32).T

    return pl.pallas_call(
        body, grid=grid, in_specs=[qcol(OD_Q), kcol(OD_K), kcol(OD_V), sink], out_specs=qcol(0),
        out_shape=jax.ShapeDtypeStruct((b, lp, SWA_HEADS * HEAD), F32),
        compiler_params=_params(("parallel", "parallel", "parallel")), name=name,
    )(proj3, proj3, proj3, sinks)


def _swa_bwd(proj3, sinks, do, name):
    b, lp, _ = proj3.shape
    nb = lp // BLOCK
    grid, qcol, kcol, sink = _swa_specs(b, lp)

    def body(q_ref, k_ref, v_ref, sink_ref, do_ref, dq_ref, dk_ref, dv_ref, dsink_ref, dk_acc, dv_acc):
        kv, n = pl.program_id(1), pl.program_id(2)

        @pl.when((n == 0) & (pl.program_id(0) == 0) & (kv == 0))
        def _():
            dsink_ref[...] = jnp.zeros_like(dsink_ref)

        @pl.when(n == 0)
        def _():
            dk_acc[...] = jnp.zeros_like(dk_acc)
            dv_acc[...] = jnp.zeros_like(dv_acc)

        lo = _lo_lanes()
        kdup, vdup, valid, dist, prev = _swa_keys(k_ref, v_ref, n, kv)
        kt = _split_rows_t(_pad_keys(kdup))
        kdup, vdup = kdup.astype(BF16), vdup.astype(BF16)
        dkc = jnp.zeros((SWA_KEYS, BLOCK), F32)
        dvc = jnp.zeros((SWA_KEYS, BLOCK), F32)
        for p in range(SWA_PAIRS):
            lanes = slice(p * BLOCK, (p + 1) * BLOCK)
            qs = _halves((q_ref[0, :, lanes] * HEAD_SCALE).astype(BF16), lo)
            dos = _halves(do_ref[0, :, lanes].astype(BF16), lo)
            dss, prs = [], []
            for hh in range(2):
                head = (kv * SWA_PAIRS + p) * 2 + hh
                pr, ps = _swa_probs(qs[hh], kdup, valid, dist, head, sink_ref)
                dp = _dot_nt(vdup, dos[hh])
                dsum = jnp.sum(pr * dp, axis=0, keepdims=True)
                dsink_ref[pl.ds(head, 1), :] += jnp.broadcast_to(-jnp.sum(ps * dsum, axis=1, keepdims=True), (1, BLOCK))
                dss.append((pr * (dp - dsum)).astype(BF16))
                prs.append(pr.astype(BF16))
            dq_ref[0, :, lanes] = jnp.dot(kt, jnp.concatenate([_pad_keys(d) for d in dss], axis=0), preferred_element_type=F32).T * HEAD_SCALE
            dkc = dkc + jnp.dot(jnp.concatenate(dss, axis=1), jnp.concatenate(qs, axis=0), preferred_element_type=F32)
            dvc = dvc + jnp.dot(jnp.concatenate(prs, axis=1), jnp.concatenate(dos, axis=0), preferred_element_type=F32)
        rows = lambda blk: pl.ds(pl.multiple_of(blk * BLOCK, BLOCK), BLOCK)
        for r, part in ((rows(prev), slice(0, BLOCK)), (rows(n), slice(BLOCK, 2 * BLOCK)), (slice(N_PAD, BLOCK), slice(2 * BLOCK, SWA_KEYS))):
            dk_acc[r, :] += dkc[part]
            dv_acc[r, :] += dvc[part]

        for acc, ref in ((dk_acc, dk_ref), (dv_acc, dv_ref)):
            @pl.when((n == nb - 1) & (kv == 0))
            def _():
                x = acc[...]
                ref[0] = x + pltpu.roll(x, HEAD, 1)

            @pl.when((n == nb - 1) & (kv == 1))
            def _():
                x = acc[...]
                ref[0] = jnp.where(lo, ref[0], x + pltpu.roll(x, HEAD, 1))

    kvout = pl.BlockSpec((1, lp, BLOCK), lambda bi, kv, n: (bi, 0, 0))
    kvshape = jax.ShapeDtypeStruct((b, lp, BLOCK), F32)
    return pl.pallas_call(
        body, grid=grid, in_specs=[qcol(OD_Q), kcol(OD_K), kcol(OD_V), sink, qcol(0)], out_specs=[qcol(0), kvout, kvout, sink],
        out_shape=[jax.ShapeDtypeStruct((b, lp, SWA_HEADS * HEAD), F32), kvshape, kvshape, jax.ShapeDtypeStruct((SWA_HEADS, BLOCK), F32)],
        scratch_shapes=[pltpu.VMEM((lp, BLOCK), F32), pltpu.VMEM((lp, BLOCK), F32)],
        compiler_params=_params(("arbitrary", "arbitrary", "arbitrary")), name=name,
    )(proj3, proj3, proj3, sinks, do)


def _kernel_weights(ev_w_in, ev_w_uq, ev_w_ukv, od_w_in):
    zeros = lambda r, c: jnp.zeros((r, c), ev_w_in.dtype)
    q_sb, k_sb, v_sb, g_sb, c_q, c_kv, k_r, g_mla = jnp.split(ev_w_in, [512, 1024, 1536, 2048, 2304, 2432, 2464], axis=1)
    w0 = jnp.concatenate([g_sb, g_mla, q_sb, k_sb, v_sb, c_q, c_kv, zeros(D_MODEL, MLA_NOPE), k_r, zeros(D_MODEL, 32)], axis=1)
    uq = ev_w_uq.reshape(MLA_Q_LORA, MLA_HEADS, MLA_NOPE + MLA_ROPE)
    wq = jnp.pad(uq, ((0, 0), (0, 0), (0, BLOCK - MLA_NOPE - MLA_ROPE))).reshape(MLA_Q_LORA, MLA_HEADS * BLOCK)
    ukv = ev_w_ukv.reshape(MLA_KV_LORA, MLA_HEADS, BLOCK)
    wk = jnp.pad(ukv[:, :, :MLA_NOPE], ((0, 0), (0, 0), (0, BLOCK - MLA_NOPE))).reshape(MLA_KV_LORA, MLA_HEADS * BLOCK)
    wv = ukv[:, :, MLA_NOPE:].reshape(MLA_KV_LORA, MLA_HEADS * HEAD)
    q, k, v, g = jnp.split(od_w_in, [1024, 1152, 1280], axis=1)
    w1 = jnp.concatenate([g, q, k, v], axis=1)
    return w0, wq, wk, wv, w1


def _od_w_in_grad(dw1):
    sl = lambda a, first, n: a[:, first:first + n]
    return jnp.concatenate([sl(dw1, OD_Q, 1024), sl(dw1, OD_K, 128), sl(dw1, OD_V, 128), sl(dw1, OD_G, 1024)], axis=1)


def _original_grads(dw0, dwq, dwk, dwv):
    sl = lambda a, first, n: a[:, first:first + n]
    d_ev_w_in = jnp.concatenate([sl(dw0, EV_Q, 512), sl(dw0, EV_K, 512), sl(dw0, EV_V, 512), sl(dw0, EV_G, 512), sl(dw0, EV_CQ, 256),
                                 sl(dw0, EV_CKV, 128), sl(dw0, EV_KR + MLA_NOPE, MLA_ROPE), sl(dw0, EV_G + 512, 512)], axis=1)
    d_uq = dwq.reshape(MLA_Q_LORA, MLA_HEADS, BLOCK)[:, :, :MLA_NOPE + MLA_ROPE].reshape(MLA_Q_LORA, -1)
    d_ukv = jnp.concatenate([dwk.reshape(MLA_KV_LORA, MLA_HEADS, BLOCK)[:, :, :MLA_NOPE], dwv.reshape(MLA_KV_LORA, MLA_HEADS, HEAD)],
                            axis=2).reshape(MLA_KV_LORA, -1)
    return d_ev_w_in, d_uq, d_ukv


def _meta_rows_sum(dh0_3):
    b, _, d = dh0_3.shape

    def body(x_ref, o_ref):
        acc = x_ref[0, N_PAD:BLOCK, :]
        for i in range(1, b):
            acc = acc + x_ref[i, N_PAD:BLOCK, :]
        o_ref[...] = acc

    return pl.pallas_call(
        body, grid=(1,), in_specs=[pl.BlockSpec((b, BLOCK, d), lambda i: (0, 0, 0))], out_specs=pl.BlockSpec((N_META, d), lambda i: (0, 0)),
        out_shape=jax.ShapeDtypeStruct((N_META, d), F32), compiler_params=_params(("arbitrary",)), name="meta_rows_sum",
    )(dh0_3)


def _local_step(x, meta, norm_g, final_g, gq, gkv, sinks, target, ev_w_in, ev_w_uq, ev_w_ukv, wo0, od_w_in, wo1, reduce_early=None):
    b, seq, d = x.shape
    lp = seq + BLOCK
    t = b * lp
    w0, wq, wk, wv, w1 = _kernel_weights(ev_w_in, ev_w_uq, ev_w_ukv, od_w_in)
    h0 = jnp.concatenate([jnp.zeros((b, N_PAD, d), F32), jnp.broadcast_to(meta[None], (b, N_META, d)), x], axis=1).reshape(t, d)
    tabs = _rope_tables(lp)
    g0, g1 = norm_g[0:1], norm_g[1:2]

    hn0 = _rms_fwd(h0, g0, "norm0")
    proj0 = _mm(hn0, w0, "inproj0")
    p0 = proj0.reshape(b, lp, EV_N)
    o_sb, sb_tot = _sb_fwd(p0, "sb_fwd")
    qf, kf, v = _mla_prep_fwd(p0, gq, gkv, wq, wk, wv, tabs, "mla_prep_fwd")
    o_mla, lse = _mla_fwd(qf, kf, v, "mla_fwd")
    o0 = [o_sb.reshape(t, -1), o_mla.reshape(t, -1)]
    ao0 = _gate_fwd(o0, proj0, "gate0")
    h1 = _mm(ao0, wo0, "outproj0", res=h0)

    hn1 = _rms_fwd(h1, g1, "norm1")
    proj1 = _mm(hn1, w1, "inproj1")
    p1 = proj1.reshape(b, lp, OD_N)
    sinks_b = jnp.broadcast_to(sinks.reshape(SWA_HEADS, 1), (SWA_HEADS, BLOCK))
    o1 = _swa_fwd(p1, sinks_b, "swa_fwd").reshape(t, -1)
    ao1 = _gate_fwd([o1], proj1, "gate1")
    h2 = _mm(ao1, wo1, "outproj1", res=h1)

    dh2, d_final_g, loss = _loss_head(h2, final_g.reshape(1, d), target, b, lp)

    d_wo1 = _mm_tn(ao1, dh2, "d_wo1")
    dao1 = _mm_nt(dh2, wo1, "d_ao1")
    (do1,), dg1 = _gate_bwd(dao1, [o1], proj1, "gate1_bwd")
    dq1, dk4, dv4, d_sinks = _swa_bwd(p1, sinks_b, do1.reshape(b, lp, -1), "swa_bwd")
    unheads = lambda a: a.reshape(t, SWA_KV_HEADS * HEAD).astype(BF16)
    dproj1 = jnp.concatenate([dg1, dq1.reshape(t, -1).astype(BF16), unheads(dk4), unheads(dv4)], axis=1)
    d_w1 = _mm_tn(hn1, dproj1, "d_w1")
    dhn1 = _mm_nt(dproj1, w1, "d_hn1")
    dh1, d_g1 = _rms_bwd(h1, g1, dhn1, dh2, "norm1_bwd")

    d_wo0 = _mm_tn(ao0, dh1, "d_wo0")
    early = dict(od_w_in=_od_w_in_grad(d_w1), ev_w_out=d_wo0, od_w_out=d_wo1)
    dao0 = _mm_nt(dh1, wo0, "d_ao0")
    if reduce_early is not None:
        pending, finish = reduce_early(early)
        dao0, pending = lax.optimization_barrier((dao0, pending))
        early_done = finish(pending)
    (do_sb, do_mla), dg0 = _gate_bwd(dao0, o0, proj0, "gate0_bwd")
    dq_sb, dk_sb, dv_sb = _sb_bwd(p0, sb_tot, do_sb.reshape(b, lp, -1), "sb_bwd")
    dqf, dkf, dv = _mla_bwd(qf, kf, v, o_mla, lse, do_mla.reshape(b, lp, -1), "mla_bwd")
    dcq, dckv, dkr, d_wq, d_wk, d_wv, d_gq, d_gkv = _mla_prep_bwd(p0, gq, gkv, wq, wk, wv, tabs, dqf, dkf, dv, "mla_prep_bwd")
    flat = lambda a: a.reshape(t, -1).astype(BF16)
    dproj0 = jnp.concatenate([dg0, flat(dq_sb), flat(dk_sb), flat(dv_sb), flat(dcq), flat(dckv), flat(dkr)], axis=1)
    if reduce_early is not None:
        dproj0, early_done = lax.optimization_barrier((dproj0, early_done))
    d_w0 = _mm_tn(hn0, dproj0, "d_w0")
    d_ev_w_in, d_uq, d_ukv = _original_grads(d_w0, d_wq, d_wk, d_wv)
    dhn0 = _mm_nt(dproj0, w0, "d_hn0")
    dh0, d_g0 = _rms_bwd(h0, g0, dhn0, dh1, "norm0_bwd")
    dh0 = dh0.reshape(b, lp, d)

    grads = dict(meta=_meta_rows_sum(dh0), norm_g=jnp.concatenate([d_g0, d_g1], axis=0), final_g=d_final_g.reshape(d),
                 ev_w_in=d_ev_w_in, ev_q_norm_g=d_gq, ev_kv_norm_g=d_gkv, ev_w_uq=d_uq, ev_w_ukv=d_ukv,
                 od_sinks=d_sinks[:, 0].reshape(1, SWA_HEADS))
    if reduce_early is None:
        return loss, dh0[:, BLOCK:], {**grads, **early}
    return loss, dh0[:, BLOCK:], grads, early_done


MESH = pl.DeviceIdType.MESH
ANY = pl.BlockSpec(memory_space=pl.ANY)


def _place():
    return lax.axis_index("x"), lax.axis_index("y"), lax.axis_index("c")


def _other_chips(x, y):
    return [(1 - x, y), (x, 1 - y), (1 - x, 1 - y)]


def _with_own_slot(slots, own):
    me = 2 * lax.axis_index("x") + lax.axis_index("y")
    return lax.dynamic_update_slice(slots, own[None], (me,) + (0,) * own.ndim)


def _gather_weights(packs, meta, name):
    n = len(packs)

    def body(*refs):
        ins, m_ref, outs, mo_ref = refs[:n], refs[n], refs[n + 1:2 * n + 1], refs[2 * n + 1]
        send_sems, recv_sems = refs[2 * n + 2:]
        x, y, c = _place()
        me, sib = 2 * x + y, (x, y, 1 - c)
        chips = _other_chips(x, y)

        def copy(k, src, dst, to):
            return pltpu.make_async_remote_copy(src_ref=src, dst_ref=dst, send_sem=send_sems.at[k], recv_sem=recv_sems.at[k], device_id=to,
                                                device_id_type=MESH)

        def half(i, chip, h):
            rows = packs[i].shape[0] // 2
            return outs[i].at[chip, pl.ds(h * rows, rows), :]

        def mine(i):
            rows = packs[i].shape[0] // 2
            return ins[i].at[pl.ds(c * rows, rows), :]

        sent = [copy(6 * i + k, mine(i), half(i, me, c), (px, py, c)) for i in range(n) for k, (px, py) in enumerate(chips)]
        sent += [copy(6 * n + k, m_ref, mo_ref.at[me], (px, py, c)) for k, (px, py) in enumerate(chips)]
        for cp in sent:
            cp.start()
        for i in range(n):
            for k, (px, py) in enumerate(chips):
                landed = half(i, 2 * px + py, c)
                copy(6 * i + k, mine(i), landed, (px, py, c)).wait_recv()
                fwd = copy(6 * i + 3 + k, landed, landed, sib)
                fwd.start()
                sent.append(fwd)
        for k, (px, py) in enumerate(chips):
            for i in range(n):
                other = half(i, 2 * px + py, 1 - c)
                copy(6 * i + 3 + k, other, other, sib).wait_recv()
            copy(6 * n + k, m_ref, mo_ref.at[2 * px + py], (px, py, c)).wait_recv()
        for cp in sent:
            cp.wait_send()

    nsem = 6 * n + 3
    res = pl.pallas_call(
        body, in_specs=[ANY] * (n + 1), out_specs=[ANY] * (n + 1),
        out_shape=[jax.ShapeDtypeStruct((N_CHIPS,) + a.shape, a.dtype) for a in list(packs) + [meta]],
        scratch_shapes=[pltpu.SemaphoreType.DMA((nsem,)), pltpu.SemaphoreType.DMA((nsem,))],
        name=name,
    )(*packs, meta)
    return [_with_own_slot(r, a) for r, a in zip(res[:n], packs)], _with_own_slot(res[n], meta)


def _grads_to_sibling(gs, name):
    n = len(gs)

    def body(*refs):
        ins, outs, send_sems, recv_sems = refs[:n], refs[n:2 * n], refs[2 * n], refs[2 * n + 1]
        x, y, c = _place()
        cps = []
        for i in range(n):
            rows = gs[i].shape[1] // 2
            cps.append(pltpu.make_async_remote_copy(src_ref=ins[i].at[:, pl.ds((1 - c) * rows, rows), :], dst_ref=outs[i],
                                                    send_sem=send_sems.at[i], recv_sem=recv_sems.at[i], device_id=(x, y, 1 - c),
                                                    device_id_type=MESH))
        for cp in cps:
            cp.start()
        for cp in cps:
            cp.wait()

    return pl.pallas_call(
        body, in_specs=[ANY] * n, out_specs=[ANY] * n,
        out_shape=[jax.ShapeDtypeStruct((g.shape[0], g.shape[1] // 2, g.shape[2]), g.dtype) for g in gs],
        scratch_shapes=[pltpu.SemaphoreType.DMA((n,)), pltpu.SemaphoreType.DMA((n,))],
        name=name,
    )(*gs)


def _share_halves(rs, name):
    n = len(rs)

    def body(*refs):
        ins, outs, send_sems, recv_sems = refs[:n], refs[n:2 * n], refs[2 * n], refs[2 * n + 1]
        x, y, c = _place()
        cps = [pltpu.make_async_remote_copy(src_ref=ins[i], dst_ref=outs[i], send_sem=send_sems.at[i], recv_sem=recv_sems.at[i],
                                            device_id=(x, y, 1 - c), device_id_type=MESH) for i in range(n)]
        for cp in cps:
            cp.start()
        for cp in cps:
            cp.wait()

    theirs = pl.pallas_call(
        body, in_specs=[ANY] * n, out_specs=[ANY] * n, out_shape=[jax.ShapeDtypeStruct(r.shape, r.dtype) for r in rs],
        scratch_shapes=[pltpu.SemaphoreType.DMA((n,)), pltpu.SemaphoreType.DMA((n,))],
        name=name,
    )(*rs)
    first = lax.axis_index("c") == 0
    return [jnp.where(first, jnp.concatenate([r, t], axis=0), jnp.concatenate([t, r], axis=0)) for r, t in zip(rs, theirs)]


def _chip_scatter(ss, name):
    n = len(ss)

    def body(*refs):
        ins, outs, send_sems, recv_sems = refs[:n], refs[n:2 * n], refs[2 * n], refs[2 * n + 1]
        x, y, c = _place()
        me = 2 * x + y
        chips = _other_chips(x, y)
        for i in range(n):
            for k, (px, py) in enumerate(chips):
                pltpu.make_async_remote_copy(src_ref=ins[i].at[2 * px + py], dst_ref=outs[i].at[me], send_sem=send_sems.at[3 * i + k],
                                             recv_sem=recv_sems.at[3 * i + k], device_id=(px, py, c), device_id_type=MESH).start()
        for i in range(n):
            for k, (px, py) in enumerate(chips):
                cp = pltpu.make_async_remote_copy(src_ref=ins[i].at[2 * px + py], dst_ref=outs[i].at[2 * px + py],
                                                  send_sem=send_sems.at[3 * i + k], recv_sem=recv_sems.at[3 * i + k],
                                                  device_id=(px, py, c), device_id_type=MESH)
                cp.wait_recv()
                cp.wait_send()

    parts = pl.pallas_call(
        body, in_specs=[ANY] * n, out_specs=[ANY] * n, out_shape=[jax.ShapeDtypeStruct(s.shape, s.dtype) for s in ss],
        scratch_shapes=[pltpu.SemaphoreType.DMA((3 * n,)), pltpu.SemaphoreType.DMA((3 * n,))],
        name=name,
    )(*ss)
    me = 2 * lax.axis_index("x") + lax.axis_index("y")
    return [_with_own_slot(p, lax.dynamic_index_in_dim(s, me, axis=0, keepdims=False)) for p, s in zip(parts, ss)]


HBM_SPACE = pltpu.MemorySpace.HBM


def _on_sequencer(name, collective_id, n_sems, body):
    @pl.kernel(mesh=plsc.ScalarSubcoreMesh(axis_name="sequencer", num_cores=1), name=name,
               scratch_types=(pltpu.SemaphoreType.DMA((n_sems,)), pltpu.SemaphoreType.DMA((n_sems,))),
               compiler_params=pltpu.CompilerParams(collective_id=collective_id))
    def launch(send_sems, recv_sems):
        body(send_sems, recv_sems)

    launch()


def _handshake(peers):
    barrier = pltpu.get_barrier_semaphore()
    for peer in peers:
        pl.semaphore_signal(barrier, inc=1, device_id=peer, device_id_type=MESH)
    pl.semaphore_wait(barrier, len(peers))


def _gather_on_sequencer(packs, name):
    n = len(packs)
    ins = [jax.new_ref(p, memory_space=HBM_SPACE) for p in packs]
    outs = [jax.empty_ref(jax.ShapeDtypeStruct((N_CHIPS,) + p.shape, p.dtype), memory_space=HBM_SPACE) for p in packs]

    def body(send_sems, recv_sems):
        x, y, c = _place()
        me, sib = 2 * x + y, (x, y, 1 - c)
        chips = _other_chips(x, y)
        _handshake([(px, py, c) for px, py in chips] + [sib])

        def copy(k, src, dst, to):
            return pltpu.make_async_remote_copy(src_ref=src, dst_ref=dst, send_sem=send_sems.at[k], recv_sem=recv_sems.at[k], device_id=to,
                                                device_id_type=MESH)

        def half(i, chip, h):
            rows = packs[i].shape[0] // 2
            return outs[i].at[chip, pl.ds(h * rows, rows), :]

        def mine(i):
            rows = packs[i].shape[0] // 2
            return ins[i].at[pl.ds(c * rows, rows), :]

        sent = [copy(6 * i + k, mine(i), half(i, me, c), (px, py, c)) for i in range(n) for k, (px, py) in enumerate(chips)]
        for cp in sent:
            cp.start()
        for i in range(n):
            for k, (px, py) in enumerate(chips):
                landed = half(i, 2 * px + py, c)
                copy(6 * i + k, mine(i), landed, (px, py, c)).wait_recv()
                fwd = copy(6 * i + 3 + k, landed, landed, sib)
                fwd.start()
                sent.append(fwd)
        for i in range(n):
            for k, (px, py) in enumerate(chips):
                other = half(i, 2 * px + py, 1 - c)
                copy(6 * i + 3 + k, other, other, sib).wait_recv()
        for cp in sent:
            cp.wait_send()

    _on_sequencer(name, 1, 6 * n, body)
    return [_with_own_slot(o[...], p) for o, p in zip(outs, packs)]


def _grads_to_sibling_on_sequencer(gs, name, collective_id):
    n = len(gs)
    ins = [jax.new_ref(g, memory_space=HBM_SPACE) for g in gs]
    outs = [jax.empty_ref(jax.ShapeDtypeStruct((g.shape[0], g.shape[1] // 2, g.shape[2]), g.dtype), memory_space=HBM_SPACE) for g in gs]

    def body(send_sems, recv_sems):
        x, y, c = _place()
        _handshake([(x, y, 1 - c)])
        cps = []
        for i in range(n):
            rows = gs[i].shape[1] // 2
            cps.append(pltpu.make_async_remote_copy(src_ref=ins[i].at[:, pl.ds((1 - c) * rows, rows), :], dst_ref=outs[i],
                                                    send_sem=send_sems.at[i], recv_sem=recv_sems.at[i], device_id=(x, y, 1 - c),
                                                    device_id_type=MESH))
        for cp in cps:
            cp.start()
        for cp in cps:
            cp.wait()

    _on_sequencer(name, collective_id, n, body)
    return [o[...] for o in outs]


def _chip_scatter_on_sequencer(ss, name, collective_id):
    n = len(ss)
    ins = [jax.new_ref(s, memory_space=HBM_SPACE) for s in ss]
    outs = [jax.empty_ref(jax.ShapeDtypeStruct(s.shape, s.dtype), memory_space=HBM_SPACE) for s in ss]

    def body(send_sems, recv_sems):
        x, y, c = _place()
        me = 2 * x + y
        chips = _other_chips(x, y)
        _handshake([(px, py, c) for px, py in chips])
        for i in range(n):
            for k, (px, py) in enumerate(chips):
                pltpu.make_async_remote_copy(src_ref=ins[i].at[2 * px + py], dst_ref=outs[i].at[me], send_sem=send_sems.at[3 * i + k],
                                             recv_sem=recv_sems.at[3 * i + k], device_id=(px, py, c), device_id_type=MESH).start()
        for i in range(n):
            for k, (px, py) in enumerate(chips):
                cp = pltpu.make_async_remote_copy(src_ref=ins[i].at[2 * px + py], dst_ref=outs[i].at[2 * px + py],
                                                  send_sem=send_sems.at[3 * i + k], recv_sem=recv_sems.at[3 * i + k],
                                                  device_id=(px, py, c), device_id_type=MESH)
                cp.wait_recv()
                cp.wait_send()

    _on_sequencer(name, collective_id, 3 * n, body)
    me = 2 * lax.axis_index("x") + lax.axis_index("y")
    return [_with_own_slot(o[...], lax.dynamic_index_in_dim(s, me, axis=0, keepdims=False)) for o, s in zip(outs, ss)]


def _all_reduce_small(v, name):
    shape = v.shape

    def body(v_ref, o_ref, slots, send_sems, recv_sems):
        x, y, c = _place()
        me = 4 * x + 2 * y + c
        slots[me] = v_ref[...]
        for r in range(1, N_DEV):
            peer = (x ^ (r >> 2), y ^ ((r >> 1) & 1), c ^ (r & 1))
            pltpu.make_async_remote_copy(src_ref=v_ref, dst_ref=slots.at[me], send_sem=send_sems.at[r - 1], recv_sem=recv_sems.at[r - 1],
                                         device_id=peer, device_id_type=MESH).start()
        for r in range(1, N_DEV):
            peer = (x ^ (r >> 2), y ^ ((r >> 1) & 1), c ^ (r & 1))
            cp = pltpu.make_async_remote_copy(src_ref=v_ref, dst_ref=slots.at[4 * peer[0] + 2 * peer[1] + peer[2]], send_sem=send_sems.at[r - 1],
                                              recv_sem=recv_sems.at[r - 1], device_id=peer, device_id_type=MESH)
            cp.wait_recv()
            cp.wait_send()
        acc = slots[0]
        for d in range(1, N_DEV):
            acc = acc + slots[d]
        o_ref[...] = acc

    vm = pl.BlockSpec(memory_space=pltpu.VMEM)
    return pl.pallas_call(
        body, in_specs=[vm], out_specs=vm, out_shape=jax.ShapeDtypeStruct(shape, F32),
        scratch_shapes=[pltpu.VMEM((N_DEV,) + shape, F32), pltpu.SemaphoreType.DMA((N_DEV - 1,)), pltpu.SemaphoreType.DMA((N_DEV - 1,))],
        name=name,
    )(v)


def _add_sibling(g, gsib, core, name):
    n, _, cdim = g.shape
    half = gsib.shape[1]
    tr = half // 2

    def body(core_ref, a_ref, b_ref, o_ref):
        o_ref[...] = (a_ref[...] + b_ref[...]).astype(o_ref.dtype)

    blk = pl.BlockSpec((1, tr, cdim), lambda j, i, core_ref: (j, i, 0))
    return pl.pallas_call(
        body,
        grid_spec=pltpu.PrefetchScalarGridSpec(
            num_scalar_prefetch=1, grid=(n, half // tr),
            in_specs=[pl.BlockSpec((1, tr, cdim), lambda j, i, core_ref: (j, core_ref[0] * (half // tr) + i, 0)), blk], out_specs=blk),
        out_shape=jax.ShapeDtypeStruct(gsib.shape, BF16), compiler_params=_params(("parallel", "parallel")), name=name,
    )(core, g, gsib)


def _sum_parts(parts, name):
    n, r, cdim = parts.shape
    tr = r // 2

    def body(p_ref, o_ref):
        acc = p_ref[0].astype(F32)
        for j in range(1, n):
            acc = acc + p_ref[j].astype(F32)
        o_ref[...] = acc

    return pl.pallas_call(
        body, grid=(r // tr,), in_specs=[pl.BlockSpec((n, tr, cdim), lambda i: (0, i, 0))],
        out_specs=pl.BlockSpec((tr, cdim), lambda i: (i, 0)), out_shape=jax.ShapeDtypeStruct((r, cdim), F32),
        compiler_params=_params(("parallel",)), name=name,
    )(parts)


def _adamw(parts, w, m, v, name):
    npart, r, cdim = parts.shape
    tr = r // 4 if r % 32 == 0 else r

    def body(p_ref, w_ref, m_ref, v_ref, g_ref, d_ref, nm_ref, nv_ref):
        g = p_ref[0]
        for j in range(1, npart):
            g = g + p_ref[j]
        m_new = ADAM_B1 * m_ref[...] + (1.0 - ADAM_B1) * g
        v_new = ADAM_B2 * v_ref[...] + (1.0 - ADAM_B2) * (g * g)
        m_hat = m_new / (1.0 - ADAM_B1 ** ADAM_STEP)
        v_hat = v_new / (1.0 - ADAM_B2 ** ADAM_STEP)
        g_ref[...] = g
        d_ref[...] = -ADAM_LR * (m_hat / (jnp.sqrt(v_hat) + ADAM_EPS) + ADAM_WD * w_ref[...])
        nm_ref[...] = m_new
        nv_ref[...] = v_new

    blk = pl.BlockSpec((tr, cdim), lambda i: (i, 0))
    shp = jax.ShapeDtypeStruct((r, cdim), F32)
    return pl.pallas_call(
        body, grid=(r // tr,), in_specs=[pl.BlockSpec((npart, tr, cdim), lambda i: (0, i, 0)), blk, blk, blk], out_specs=[blk] * 4,
        out_shape=[shp] * 4, compiler_params=_params(("parallel",)), name=name,
    )(parts, w, m, v)


BIG = ("ev_w_in", "ev_w_uq", "ev_w_ukv", "ev_w_out", "od_w_in", "od_w_out", "meta")
SMALL = ("norm_g", "final_g", "ev_q_norm_g", "ev_kv_norm_g", "od_sinks")
SMALL_SHAPE = (8, 512)
BY_ROWS = ("ev_w_out", "od_w_out")

EV_IN_SHARD, OD_IN_SHARD, UQ_SHARD = 2976 // N_CHIPS, 2304 // N_CHIPS, 768 // N_CHIPS


def _pack_big(a, lead=()):
    return _pack_first(a, lead) + _pack_later(a, lead)


def _pad_lanes(x, width):
    return jnp.concatenate([x, jnp.zeros(x.shape[:-1] + (width - x.shape[-1],), x.dtype)], axis=-1)


def _pack_latent(a, lead=()):
    ax = len(lead)
    corner = jnp.concatenate([a["ev_w_ukv"], a["meta"], jnp.zeros(lead + (256 - MLA_KV_LORA - N_META, 256), a["meta"].dtype)], axis=ax)
    return jnp.concatenate([_pad_lanes(a["ev_w_uq"], 256), corner], axis=ax + 1)


def _pack_first(a, lead=()):
    return _pad_lanes(a["ev_w_in"], 768), _pack_latent(a, lead)


def _pack_later(a, lead=()):
    return _pad_lanes(a["od_w_in"], 640), jnp.concatenate([a["ev_w_out"], a["od_w_out"]], axis=len(lead) + 1)


N_FIRST = 2
LATER = ("od_w_in", "ev_w_out", "od_w_out")


def _unpack_big(p_in0, p_lat, p_in1, p_out):
    return dict(ev_w_in=p_in0[..., :EV_IN_SHARD], od_w_in=p_in1[..., :OD_IN_SHARD], ev_w_out=p_out[..., :D_MODEL],
                od_w_out=p_out[..., D_MODEL:], ev_w_uq=p_lat[..., :UQ_SHARD], ev_w_ukv=p_lat[..., :MLA_KV_LORA, 256:],
                meta=p_lat[..., MLA_KV_LORA:MLA_KV_LORA + N_META, 256:])


def _chip_shards(full, by_rows):
    if by_rows:
        return full.reshape(N_CHIPS, full.shape[0] // N_CHIPS, full.shape[1])
    cols = full.shape[1] // N_CHIPS
    return jnp.stack([full[:, j * cols:(j + 1) * cols] for j in range(N_CHIPS)])


def _from_chip_shards(slots, by_rows):
    if by_rows:
        return slots.reshape(-1, slots.shape[2])
    return jnp.concatenate([slots[j] for j in range(N_CHIPS)], axis=1)


def _pack_small(arrs, extra=None):
    flat = [a.reshape(-1) for a in arrs] + ([] if extra is None else [extra.reshape(-1)])
    used = sum(f.shape[0] for f in flat)
    return jnp.pad(jnp.concatenate(flat), (0, SMALL_SHAPE[0] * SMALL_SHAPE[1] - used)).reshape(SMALL_SHAPE)


def _unpack_small(p, shapes):
    flat, out, at = p.reshape(-1), [], 0
    for s in shapes:
        n = int(np.prod(s))
        out.append(flat[at:at + n].reshape(s))
        at += n
    return out, flat[at]


def kernel(x, meta, norm_g, final_g, ev_w_in, ev_q_norm_g, ev_kv_norm_g, ev_w_uq, ev_w_ukv, ev_w_out, od_w_in, od_sinks, od_w_out, loss_target, m_meta, m_norm_g, m_final_g, m_ev_w_in, m_ev_q_norm_g, m_ev_kv_norm_g, m_ev_w_uq, m_ev_w_ukv, m_ev_w_out, m_od_w_in, m_od_sinks, m_od_w_out, v_meta, v_norm_g, v_final_g, v_ev_w_in, v_ev_q_norm_g, v_ev_kv_norm_g, v_ev_w_uq, v_ev_w_ukv, v_ev_w_out, v_od_w_in, v_od_sinks, v_od_w_out):
    given = dict(locals())
    two_d = lambda a: a[0] if a.ndim == 3 else a
    packs = {k: _pack_big({n: two_d(given[k + n]) for n in BIG}) for k in ("", "m_", "v_")}

    wbf = [p.astype(BF16) for p in packs[""]]
    later = _gather_on_sequencer(wbf[N_FIRST:], "gather_later_weights")
    first, meta_all = _gather_weights(wbf[:N_FIRST], meta, "gather_weights")
    full = {n: _from_chip_shards(a, n in BY_ROWS) for n, a in _unpack_big(*first, *later).items()}
    meta_full = _from_chip_shards(meta_all, False)

    core = lax.axis_index("c").astype(jnp.int32).reshape(1)
    shards = lambda g: {n: _chip_shards(a, n in BY_ROWS) for n, a in g.items()}

    def sums_of(group, tag, to_sibling):
        return [_add_sibling(g, s, core, f"add_sibling_{tag}{i}") for i, (g, s) in enumerate(zip(group, to_sibling(group)))]

    def update(sums, tag, first_pack, scatter):
        parts = scatter(sums, "grads_to_chips_" + tag)
        reduced = _share_halves([_sum_parts(p, f"add_chips_{tag}{i}") for i, p in enumerate(parts)], "reduced_to_sibling_" + tag)
        return [_adamw(r[None], packs[""][first_pack + i], packs["m_"][first_pack + i], packs["v_"][first_pack + i],
                       f"adamw_matrices_{first_pack + i}") for i, r in enumerate(reduced)]

    def behind(tag, first_pack, ids):
        def start(packed):
            sums = sums_of(packed, tag, lambda p: _grads_to_sibling_on_sequencer(p, "grads_to_sibling_" + tag, ids[0]))
            return sums, lambda s: update(s, tag, first_pack, lambda t, name: _chip_scatter_on_sequencer(t, name, ids[1]))
        return start

    loss, grad_x, grads, updated_later = _local_step(
        x, meta_full, norm_g, final_g, ev_q_norm_g, ev_kv_norm_g, od_sinks, loss_target, full["ev_w_in"], full["ev_w_uq"],
        full["ev_w_ukv"], full["ev_w_out"], full["od_w_in"], full["od_w_out"],
        reduce_early=lambda g: behind("later", N_FIRST, (3, 2))(_pack_later(shards(g), (N_CHIPS,))))

    first = _pack_first(shards({n: grads[n] for n in BIG if n not in LATER}), (N_CHIPS,))
    updated_first = update(sums_of(first, "first", lambda p: _grads_to_sibling(p, "grads_to_sibling_first")), "first", 0,
                           lambda t, name: _chip_scatter_on_sequencer(t, name, 4))
    updated = updated_first + updated_later
    big_out = [{n: a.reshape(given[n].shape) for n, a in _unpack_big(*outs).items()} for outs in zip(*updated)]

    small_shapes = [given[n].shape for n in SMALL]
    ssum = _all_reduce_small(_pack_small([grads[n] for n in SMALL], loss[0, 0]), "reduce_vectors")
    small_out = _adamw(ssum[None], _pack_small([given[n] for n in SMALL]), _pack_small([given["m_" + n] for n in SMALL]),
                       _pack_small([given["v_" + n] for n in SMALL]), "adamw_vectors")
    total_loss = ssum.reshape(-1)[sum(int(np.prod(s)) for s in small_shapes)]
    small_out = [_unpack_small(o, small_shapes)[0] for o in small_out]

    names = ("meta", "norm_g", "final_g", "ev_w_in", "ev_q_norm_g", "ev_kv_norm_g", "ev_w_uq", "ev_w_ukv", "ev_w_out", "od_w_in", "od_sinks",
             "od_w_out")
    outs = [total_loss, grad_x]
    for kind in range(4):
        for n in names:
            outs.append(big_out[kind][n] if n in BIG else small_out[kind][SMALL.index(n)])
    return tuple(outs)
```
